```python
import jax, jax.numpy as jnp
from jax import lax
import numpy as np

D_MODEL = 1024
BATCH = 8
SEQ = 16384
DEPTH = 2

N_EVEN = (DEPTH + 1) // 2
N_ODD = DEPTH // 2

POOL_DIM = D_MODEL // 2
POOL_WINDOWS = (2, 4, 8, 16)
N_POOL_GROUPS = len(POOL_WINDOWS)
POOL_GROUP_DIM = POOL_DIM // N_POOL_GROUPS

MLA_HEADS = 8
QK_NOPE_DIM = 64
QK_ROPE_DIM = 32
QK_DIM = QK_NOPE_DIM + QK_ROPE_DIM
V_HEAD_DIM = 64
Q_LORA_RANK = 256
KV_LORA_RANK = 128
ROPE_BASE = 10000.0
Q_BLOCK = 128

EVEN_IN_DIM = POOL_DIM + Q_LORA_RANK + KV_LORA_RANK + QK_ROPE_DIM
EVEN_MIX_DIM = POOL_DIM + MLA_HEADS * V_HEAD_DIM

LRU_WIDTH = D_MODEL
LRU_HEADS = 4
LRU_HEAD_DIM = LRU_WIDTH // LRU_HEADS
CONV_WIDTH = 4
LRU_C = 8.0

MEM_TOKENS = 256
MEM_HEADS = 4
MEM_HEAD_DIM = D_MODEL // MEM_HEADS

D_FF = -(-8 * D_MODEL // (3 * 256)) * 256

RMS_EPS = 1e-6
NEG_INF = -1e30

kernel_name = "hybrid_pool_mla_rglru_memxattn"


def rms_norm(x, g):
    xf = x.astype(jnp.float32)
    y = xf * lax.rsqrt(jnp.mean(xf * xf, axis=-1, keepdims=True) + RMS_EPS)
    return (y * g).astype(x.dtype)


def rope_tables(positions):
    inv_freq = ROPE_BASE ** (-jnp.arange(0, QK_ROPE_DIM, 2, dtype=jnp.float32) / QK_ROPE_DIM)
    ang = positions.astype(jnp.float32)[..., None] * inv_freq
    return jnp.cos(ang), jnp.sin(ang)


def apply_rope(t, cos, sin):
    t1, t2 = jnp.split(t.astype(jnp.float32), 2, axis=-1)
    out = jnp.concatenate([t1 * cos - t2 * sin, t2 * cos + t1 * sin], axis=-1)
    return out.astype(t.dtype)


def pool_mixer(u, pool_w, pool_scale):
    B, S, _ = u.shape
    ug = u.reshape(B, S, N_POOL_GROUPS, POOL_GROUP_DIM)
    uf = ug.astype(jnp.float32)
    csum = jnp.concatenate([jnp.zeros((B, 1, N_POOL_GROUPS, POOL_GROUP_DIM), jnp.float32),
                            jnp.cumsum(uf, axis=1)], axis=1)
    t = jnp.arange(S)
    means = []
    for g, w in enumerate(POOL_WINDOWS):
        lo = jnp.maximum(t + 1 - w, 0)
        win_sum = csum[:, 1:, g] - csum[:, lo, g]
        cnt = jnp.minimum(t + 1, w).astype(jnp.float32)
        means.append(win_sum / cnt[None, :, None])
    pooled = (jnp.stack(means, axis=2) - uf).astype(u.dtype)
    y = jnp.einsum('bsgc,gcd->bsgd', pooled, pool_w).reshape(B, S, POOL_DIM)
    return y * pool_scale.astype(y.dtype)


def mla_causal_attention(q_nope, q_rope, k_nope, k_rope, v):
    B, S, H, _ = q_nope.shape
    nb = S // Q_BLOCK
    qn = q_nope.reshape(B, nb, Q_BLOCK, H, QK_NOPE_DIM).transpose(1, 0, 2, 3, 4)
    qr = q_rope.reshape(B, nb, Q_BLOCK, H, QK_ROPE_DIM).transpose(1, 0, 2, 3, 4)
    starts = jnp.arange(nb, dtype=jnp.int32) * Q_BLOCK
    kpos = jnp.arange(S, dtype=jnp.int32)
    scale = QK_DIM ** -0.5

    def one_block(args):
        qn_b, qr_b, start = args
        s = (jnp.einsum('bqhd,bkhd->bhqk', qn_b, k_nope).astype(jnp.float32)
             + jnp.einsum('bqhr,bkr->bhqk', qr_b, k_rope).astype(jnp.float32)) * scale
        qpos = start + jnp.arange(Q_BLOCK, dtype=jnp.int32)
        mask = kpos[None, :] <= qpos[:, None]
        s = jnp.where(mask[None, None], s, NEG_INF)
        p = jax.nn.softmax(s, axis=-1).astype(v.dtype)
        return jnp.einsum('bhqk,bkhd->bqhd', p, v)

    out = lax.map(one_block, (qn, qr, starts))
    return out.transpose(1, 0, 2, 3, 4).reshape(B, S, H * V_HEAD_DIM)


def even_mixer(h, cos, sin, w_in, pool_w, pool_scale, q_norm, w_q_up, kv_norm, w_kv_up, w_out):
    B, S, _ = h.shape
    z = h @ w_in
    u, cq, ckv, kr = jnp.split(z, [POOL_DIM, POOL_DIM + Q_LORA_RANK,
                                   POOL_DIM + Q_LORA_RANK + KV_LORA_RANK], axis=-1)
    y_pool = pool_mixer(u, pool_w, pool_scale)
    q = (rms_norm(cq, q_norm) @ w_q_up).reshape(B, S, MLA_HEADS, QK_DIM)
    q_nope, q_rope = jnp.split(q, [QK_NOPE_DIM], axis=-1)
    kv = (rms_norm(ckv, kv_norm) @ w_kv_up).reshape(B, S, MLA_HEADS, QK_NOPE_DIM + V_HEAD_DIM)
    k_nope, v = jnp.split(kv, [QK_NOPE_DIM], axis=-1)
    q_rope = apply_rope(q_rope, cos[:, :, None, :], sin[:, :, None, :])
    k_rope = apply_rope(kr, cos, sin)
    y_att = mla_causal_attention(q_nope, q_rope, k_nope, k_rope, v)
    return jnp.concatenate([y_pool, y_att], axis=-1) @ w_out


def causal_depthwise_conv(xb, conv_w, conv_b):
    y = lax.conv_general_dilated(xb, conv_w[:, None, :].astype(xb.dtype), window_strides=(1,),
                                 padding=((CONV_WIDTH - 1, 0),),
                                 dimension_numbers=('NWC', 'WIO', 'NWC'),
                                 feature_group_count=xb.shape[-1])
    return y + conv_b.astype(y.dtype)


def linear_scan_combine(c1, c2):
    a1, b1 = c1
    a2, b2 = c2
    return a1 * a2, a2 * b1 + b2


def odd_mixer(h, reset, w_in, conv_w, conv_b, w_rgate, b_rgate, w_igate, b_igate, lam, w_out):
    B, S, _ = h.shape
    z = h @ w_in
    gate_branch, xb = jnp.split(z, [LRU_WIDTH], axis=-1)
    xb = causal_depthwise_conv(xb, conv_w, conv_b)
    xg = xb.reshape(B, S, LRU_HEADS, LRU_HEAD_DIM)
    r = jax.nn.sigmoid(jnp.einsum('bshc,hcd->bshd', xg, w_rgate).reshape(B, S, LRU_WIDTH) + b_rgate)
    i = jax.nn.sigmoid(jnp.einsum('bshc,hcd->bshd', xg, w_igate).reshape(B, S, LRU_WIDTH) + b_igate)
    log_a = -LRU_C * r.astype(jnp.float32) * jax.nn.softplus(-lam.astype(jnp.float32))
    a = jnp.exp(log_a)
    mult = jnp.sqrt(jnp.maximum(-jnp.expm1(2.0 * log_a), 0.0))
    a = jnp.where(reset, 0.0, a)
    mult = jnp.where(reset, 1.0, mult)
    b = mult * (i * xb).astype(jnp.float32)
    _, hseq = lax.associative_scan(linear_scan_combine, (a, b), axis=1)
    y = jax.nn.gelu(gate_branch) * hseq.astype(h.dtype)
    return y @ w_out


def mem_cross_attention(h, mem, norm_mem, w_q, w_kv, w_o):
    B, S, _ = h.shape
    m = rms_norm(mem, norm_mem)
    q = (h @ w_q).reshape(B, S, MEM_HEADS, MEM_HEAD_DIM)
    k, v = jnp.split(m @ w_kv, 2, axis=-1)
    k = k.reshape(B, -1, MEM_HEADS, MEM_HEAD_DIM)
    v = v.reshape(B, -1, MEM_HEADS, MEM_HEAD_DIM)
    s = jnp.einsum('bqhd,bkhd->bhqk', q, k).astype(jnp.float32) * (MEM_HEAD_DIM ** -0.5)
    p = jax.nn.softmax(s, axis=-1).astype(v.dtype)
    o = jnp.einsum('bhqk,bkhd->bqhd', p, v).reshape(B, S, D_MODEL)
    return o @ w_o


def swiglu(h, w_gate_up, w_down):
    g, u = jnp.split(h @ w_gate_up, 2, axis=-1)
    return (jax.nn.silu(g) * u) @ w_down


def _fwd_setup_inputs(seed: int = 0) -> dict:
    key = jax.random.key(seed)
    ks = iter(jax.random.split(key, 48))
    f32 = jnp.float32

    def w(shape, fan_in):
        return jax.random.normal(next(ks), shape, f32) * fan_in ** -0.5

    def gain(shape):
        return 1.0 + 0.02 * jax.random.normal(next(ks), shape, f32)

    def bias(shape):
        return 0.02 * jax.random.normal(next(ks), shape, f32)

    E, O, L = N_EVEN, N_ODD, DEPTH
    x = jax.random.normal(next(ks), (BATCH, SEQ, D_MODEL), f32)
    mem = jax.random.normal(next(ks), (BATCH, MEM_TOKENS, D_MODEL), f32)
    positions = jnp.broadcast_to(jnp.arange(SEQ, dtype=jnp.int32), (BATCH, SEQ))
    a_c = jax.random.uniform(next(ks), (O, LRU_WIDTH), f32, 0.9, 0.999)
    s_l = a_c ** (1.0 / LRU_C)
    lam = jnp.log(s_l) - jnp.log1p(-s_l)
    return {
        "x": x,
        "mem": mem,
        "positions": positions,
        "ev_norm": gain((E, D_MODEL)),
        "ev_w_in": w((E, D_MODEL, EVEN_IN_DIM), D_MODEL),
        "ev_pool_w": w((E, N_POOL_GROUPS, POOL_GROUP_DIM, POOL_GROUP_DIM), POOL_GROUP_DIM),
        "ev_pool_scale": gain((E, POOL_DIM)),
        "ev_q_norm": gain((E, Q_LORA_RANK)),
        "ev_w_q_up": w((E, Q_LORA_RANK, MLA_HEADS * QK_DIM), Q_LORA_RANK),
        "ev_kv_norm": gain((E, KV_LORA_RANK)),
        "ev_w_kv_up": w((E, KV_LORA_RANK, MLA_HEADS * (QK_NOPE_DIM + V_HEAD_DIM)), KV_LORA_RANK),
        "ev_w_out": w((E, EVEN_MIX_DIM, D_MODEL), EVEN_MIX_DIM),
        "od_norm": gain((O, D_MODEL)),
        "od_w_in": w((O, D_MODEL, 2 * LRU_WIDTH), D_MODEL),
        "od_conv_w": w((O, CONV_WIDTH, LRU_WIDTH), CONV_WIDTH),
        "od_conv_b": bias((O, LRU_WIDTH)),
        "od_w_rgate": w((O, LRU_HEADS, LRU_HEAD_DIM, LRU_HEAD_DIM), LRU_HEAD_DIM),
        "od_b_rgate": bias((O, LRU_WIDTH)),
        "od_w_igate": w((O, LRU_HEADS, LRU_HEAD_DIM, LRU_HEAD_DIM), LRU_HEAD_DIM),
        "od_b_igate": bias((O, LRU_WIDTH)),
        "od_lambda": lam,
        "od_w_out": w((O, LRU_WIDTH, D_MODEL), LRU_WIDTH),
        "xa_norm_x": gain((L, D_MODEL)),
        "xa_norm_mem": gain((L, D_MODEL)),
        "xa_w_q": w((L, D_MODEL, D_MODEL), D_MODEL),
        "xa_w_kv": w((L, D_MODEL, 2 * D_MODEL), D_MODEL),
        "xa_w_o": w((L, D_MODEL, D_MODEL), D_MODEL),
        "ffn_norm": gain((L, D_MODEL)),
        "ffn_w_gate_up": w((L, D_MODEL, 2 * D_FF), D_MODEL),
        "ffn_w_down": w((L, D_FF, D_MODEL), D_FF),
        "final_norm": gain((D_MODEL,)),
    }


def _fwd_reference(x, mem, positions,
              ev_norm, ev_w_in, ev_pool_w, ev_pool_scale, ev_q_norm, ev_w_q_up,
              ev_kv_norm, ev_w_kv_up, ev_w_out,
              od_norm, od_w_in, od_conv_w, od_conv_b, od_w_rgate, od_b_rgate,
              od_w_igate, od_b_igate, od_lambda, od_w_out,
              xa_norm_x, xa_norm_mem, xa_w_q, xa_w_kv, xa_w_o,
              ffn_norm, ffn_w_gate_up, ffn_w_down, final_norm):
    cos, sin = rope_tables(positions)
    reset = (positions == 0)[..., None]
    for layer in range(DEPTH):
        j = layer // 2
        if layer % 2 == 0:
            h = rms_norm(x, ev_norm[j])
            x = x + even_mixer(h, cos, sin, ev_w_in[j], ev_pool_w[j], ev_pool_scale[j],
                               ev_q_norm[j], ev_w_q_up[j], ev_kv_norm[j], ev_w_kv_up[j],
                               ev_w_out[j])
        else:
            h = rms_norm(x, od_norm[j])
            x = x + odd_mixer(h, reset, od_w_in[j], od_conv_w[j], od_conv_b[j],
                              od_w_rgate[j], od_b_rgate[j], od_w_igate[j], od_b_igate[j],
                              od_lambda[j], od_w_out[j])
        x = x + mem_cross_attention(rms_norm(x, xa_norm_x[layer]), mem, xa_norm_mem[layer],
                                    xa_w_q[layer], xa_w_kv[layer], xa_w_o[layer])
        x = x + swiglu(rms_norm(x, ffn_norm[layer]), ffn_w_gate_up[layer], ffn_w_down[layer])
    return rms_norm(x, final_norm)


import jax as _jax
import jax.numpy as _jnp

TWIN_FORMAT = 'train_step'
FWD_PARAMS = ['x', 'mem', 'positions', 'ev_norm', 'ev_w_in', 'ev_pool_w', 'ev_pool_scale', 'ev_q_norm', 'ev_w_q_up', 'ev_kv_norm', 'ev_w_kv_up', 'ev_w_out', 'od_norm', 'od_w_in', 'od_conv_w', 'od_conv_b', 'od_w_rgate', 'od_b_rgate', 'od_w_igate', 'od_b_igate', 'od_lambda', 'od_w_out', 'xa_norm_x', 'xa_norm_mem', 'xa_w_q', 'xa_w_kv', 'xa_w_o', 'ffn_norm', 'ffn_w_gate_up', 'ffn_w_down', 'final_norm']
TWIN_WEIGHTS = ['ev_norm', 'ev_w_in', 'ev_pool_w', 'ev_pool_scale', 'ev_q_norm', 'ev_w_q_up', 'ev_kv_norm', 'ev_w_kv_up', 'ev_w_out', 'od_norm', 'od_w_in', 'od_conv_w', 'od_conv_b', 'od_w_rgate', 'od_b_rgate', 'od_w_igate', 'od_b_igate', 'od_lambda', 'od_w_out', 'xa_norm_x', 'xa_norm_mem', 'xa_w_q', 'xa_w_kv', 'xa_w_o', 'ffn_norm', 'ffn_w_gate_up', 'ffn_w_down', 'final_norm']
TWIN_DIFF_INPUT = 'x'
TWIN_INPUTS = ['x', 'mem', 'positions', 'ev_norm', 'ev_w_in', 'ev_pool_w', 'ev_pool_scale', 'ev_q_norm', 'ev_w_q_up', 'ev_kv_norm', 'ev_w_kv_up', 'ev_w_out', 'od_norm', 'od_w_in', 'od_conv_w', 'od_conv_b', 'od_w_rgate', 'od_b_rgate', 'od_w_igate', 'od_b_igate', 'od_lambda', 'od_w_out', 'xa_norm_x', 'xa_norm_mem', 'xa_w_q', 'xa_w_kv', 'xa_w_o', 'ffn_norm', 'ffn_w_gate_up', 'ffn_w_down', 'final_norm', 'loss_target', 'm_ev_norm', 'm_ev_w_in', 'm_ev_pool_w', 'm_ev_pool_scale', 'm_ev_q_norm', 'm_ev_w_q_up', 'm_ev_kv_norm', 'm_ev_w_kv_up', 'm_ev_w_out', 'm_od_norm', 'm_od_w_in', 'm_od_conv_w', 'm_od_conv_b', 'm_od_w_rgate', 'm_od_b_rgate', 'm_od_w_igate', 'm_od_b_igate', 'm_od_lambda', 'm_od_w_out', 'm_xa_norm_x', 'm_xa_norm_mem', 'm_xa_w_q', 'm_xa_w_kv', 'm_xa_w_o', 'm_ffn_norm', 'm_ffn_w_gate_up', 'm_ffn_w_down', 'm_final_norm', 'v_ev_norm', 'v_ev_w_in', 'v_ev_pool_w', 'v_ev_pool_scale', 'v_ev_q_norm', 'v_ev_w_q_up', 'v_ev_kv_norm', 'v_ev_w_kv_up', 'v_ev_w_out', 'v_od_norm', 'v_od_w_in', 'v_od_conv_w', 'v_od_conv_b', 'v_od_w_rgate', 'v_od_b_rgate', 'v_od_w_igate', 'v_od_b_igate', 'v_od_lambda', 'v_od_w_out', 'v_xa_norm_x', 'v_xa_norm_mem', 'v_xa_w_q', 'v_xa_w_kv', 'v_xa_w_o', 'v_ffn_norm', 'v_ffn_w_gate_up', 'v_ffn_w_down', 'v_final_norm']
TWIN_OUTPUTS = ['loss', 'grad_x', 'grad_ev_norm', 'grad_ev_w_in', 'grad_ev_pool_w', 'grad_ev_pool_scale', 'grad_ev_q_norm', 'grad_ev_w_q_up', 'grad_ev_kv_norm', 'grad_ev_w_kv_up', 'grad_ev_w_out', 'grad_od_norm', 'grad_od_w_in', 'grad_od_conv_w', 'grad_od_conv_b', 'grad_od_w_rgate', 'grad_od_b_rgate', 'grad_od_w_igate', 'grad_od_b_igate', 'grad_od_lambda', 'grad_od_w_out', 'grad_xa_norm_x', 'grad_xa_norm_mem', 'grad_xa_w_q', 'grad_xa_w_kv', 'grad_xa_w_o', 'grad_ffn_norm', 'grad_ffn_w_gate_up', 'grad_ffn_w_down', 'grad_final_norm', 'delta_ev_norm', 'delta_ev_w_in', 'delta_ev_pool_w', 'delta_ev_pool_scale', 'delta_ev_q_norm', 'delta_ev_w_q_up', 'delta_ev_kv_norm', 'delta_ev_w_kv_up', 'delta_ev_w_out', 'delta_od_norm', 'delta_od_w_in', 'delta_od_conv_w', 'delta_od_conv_b', 'delta_od_w_rgate', 'delta_od_b_rgate', 'delta_od_w_igate', 'delta_od_b_igate', 'delta_od_lambda', 'delta_od_w_out', 'delta_xa_norm_x', 'delta_xa_norm_mem', 'delta_xa_w_q', 'delta_xa_w_kv', 'delta_xa_w_o', 'delta_ffn_norm', 'delta_ffn_w_gate_up', 'delta_ffn_w_down', 'delta_final_norm', 'new_m_ev_norm', 'new_m_ev_w_in', 'new_m_ev_pool_w', 'new_m_ev_pool_scale', 'new_m_ev_q_norm', 'new_m_ev_w_q_up', 'new_m_ev_kv_norm', 'new_m_ev_w_kv_up', 'new_m_ev_w_out', 'new_m_od_norm', 'new_m_od_w_in', 'new_m_od_conv_w', 'new_m_od_conv_b', 'new_m_od_w_rgate', 'new_m_od_b_rgate', 'new_m_od_w_igate', 'new_m_od_b_igate', 'new_m_od_lambda', 'new_m_od_w_out', 'new_m_xa_norm_x', 'new_m_xa_norm_mem', 'new_m_xa_w_q', 'new_m_xa_w_kv', 'new_m_xa_w_o', 'new_m_ffn_norm', 'new_m_ffn_w_gate_up', 'new_m_ffn_w_down', 'new_m_final_norm', 'new_v_ev_norm', 'new_v_ev_w_in', 'new_v_ev_pool_w', 'new_v_ev_pool_scale', 'new_v_ev_q_norm', 'new_v_ev_w_q_up', 'new_v_ev_kv_norm', 'new_v_ev_w_kv_up', 'new_v_ev_w_out', 'new_v_od_norm', 'new_v_od_w_in', 'new_v_od_conv_w', 'new_v_od_conv_b', 'new_v_od_w_rgate', 'new_v_od_b_rgate', 'new_v_od_w_igate', 'new_v_od_b_igate', 'new_v_od_lambda', 'new_v_od_w_out', 'new_v_xa_norm_x', 'new_v_xa_norm_mem', 'new_v_xa_w_q', 'new_v_xa_w_kv', 'new_v_xa_w_o', 'new_v_ffn_norm', 'new_v_ffn_w_gate_up', 'new_v_ffn_w_down', 'new_v_final_norm']
TWIN_LEAF_KINDS = {'loss': 'loss', 'grad_x': 'grad_x', 'grad_ev_norm': 'grad_w', 'grad_ev_w_in': 'grad_w', 'grad_ev_pool_w': 'grad_w', 'grad_ev_pool_scale': 'grad_w', 'grad_ev_q_norm': 'grad_w', 'grad_ev_w_q_up': 'grad_w', 'grad_ev_kv_norm': 'grad_w', 'grad_ev_w_kv_up': 'grad_w', 'grad_ev_w_out': 'grad_w', 'grad_od_norm': 'grad_w', 'grad_od_w_in': 'grad_w', 'grad_od_conv_w': 'grad_w', 'grad_od_conv_b': 'grad_w', 'grad_od_w_rgate': 'grad_w', 'grad_od_b_rgate': 'grad_w', 'grad_od_w_igate': 'grad_w', 'grad_od_b_igate': 'grad_w', 'grad_od_lambda': 'grad_w', 'grad_od_w_out': 'grad_w', 'grad_xa_norm_x': 'grad_w', 'grad_xa_norm_mem': 'grad_w', 'grad_xa_w_q': 'grad_w', 'grad_xa_w_kv': 'grad_w', 'grad_xa_w_o': 'grad_w', 'grad_ffn_norm': 'grad_w', 'grad_ffn_w_gate_up': 'grad_w', 'grad_ffn_w_down': 'grad_w', 'grad_final_norm': 'grad_w', 'delta_ev_norm': 'delta_w', 'delta_ev_w_in': 'delta_w', 'delta_ev_pool_w': 'delta_w', 'delta_ev_pool_scale': 'delta_w', 'delta_ev_q_norm': 'delta_w', 'delta_ev_w_q_up': 'delta_w', 'delta_ev_kv_norm': 'delta_w', 'delta_ev_w_kv_up': 'delta_w', 'delta_ev_w_out': 'delta_w', 'delta_od_norm': 'delta_w', 'delta_od_w_in': 'delta_w', 'delta_od_conv_w': 'delta_w', 'delta_od_conv_b': 'delta_w', 'delta_od_w_rgate': 'delta_w', 'delta_od_b_rgate': 'delta_w', 'delta_od_w_igate': 'delta_w', 'delta_od_b_igate': 'delta_w', 'delta_od_lambda': 'delta_w', 'delta_od_w_out': 'delta_w', 'delta_xa_norm_x': 'delta_w', 'delta_xa_norm_mem': 'delta_w', 'delta_xa_w_q': 'delta_w', 'delta_xa_w_kv': 'delta_w', 'delta_xa_w_o': 'delta_w', 'delta_ffn_norm': 'delta_w', 'delta_ffn_w_gate_up': 'delta_w', 'delta_ffn_w_down': 'delta_w', 'delta_final_norm': 'delta_w', 'new_m_ev_norm': 'new_m', 'new_m_ev_w_in': 'new_m', 'new_m_ev_pool_w': 'new_m', 'new_m_ev_pool_scale': 'new_m', 'new_m_ev_q_norm': 'new_m', 'new_m_ev_w_q_up': 'new_m', 'new_m_ev_kv_norm': 'new_m', 'new_m_ev_w_kv_up': 'new_m', 'new_m_ev_w_out': 'new_m', 'new_m_od_norm': 'new_m', 'new_m_od_w_in': 'new_m', 'new_m_od_conv_w': 'new_m', 'new_m_od_conv_b': 'new_m', 'new_m_od_w_rgate': 'new_m', 'new_m_od_b_rgate': 'new_m', 'new_m_od_w_igate': 'new_m', 'new_m_od_b_igate': 'new_m', 'new_m_od_lambda': 'new_m', 'new_m_od_w_out': 'new_m', 'new_m_xa_norm_x': 'new_m', 'new_m_xa_norm_mem': 'new_m', 'new_m_xa_w_q': 'new_m', 'new_m_xa_w_kv': 'new_m', 'new_m_xa_w_o': 'new_m', 'new_m_ffn_norm': 'new_m', 'new_m_ffn_w_gate_up': 'new_m', 'new_m_ffn_w_down': 'new_m', 'new_m_final_norm': 'new_m', 'new_v_ev_norm': 'new_v', 'new_v_ev_w_in': 'new_v', 'new_v_ev_pool_w': 'new_v', 'new_v_ev_pool_scale': 'new_v', 'new_v_ev_q_norm': 'new_v', 'new_v_ev_w_q_up': 'new_v', 'new_v_ev_kv_norm': 'new_v', 'new_v_ev_w_kv_up': 'new_v', 'new_v_ev_w_out': 'new_v', 'new_v_od_norm': 'new_v', 'new_v_od_w_in': 'new_v', 'new_v_od_conv_w': 'new_v', 'new_v_od_conv_b': 'new_v', 'new_v_od_w_rgate': 'new_v', 'new_v_od_b_rgate': 'new_v', 'new_v_od_w_igate': 'new_v', 'new_v_od_b_igate': 'new_v', 'new_v_od_lambda': 'new_v', 'new_v_od_w_out': 'new_v', 'new_v_xa_norm_x': 'new_v', 'new_v_xa_norm_mem': 'new_v', 'new_v_xa_w_q': 'new_v', 'new_v_xa_w_kv': 'new_v', 'new_v_xa_w_o': 'new_v', 'new_v_ffn_norm': 'new_v', 'new_v_ffn_w_gate_up': 'new_v', 'new_v_ffn_w_down': 'new_v', 'new_v_final_norm': 'new_v'}


def _forward(args):
    return _fwd_reference(*[args[k] for k in FWD_PARAMS])


def _output_shape():
    def fwd():
        inp = _fwd_setup_inputs(0)
        return _fwd_reference(*[inp[k] for k in FWD_PARAMS])
    out = _jax.eval_shape(fwd)
    return out.shape, out.dtype

N_MICROBATCH = 1
ADAM_LR = 0.001
ADAM_B1 = 0.9
ADAM_B2 = 0.999
ADAM_EPS = 1e-08
ADAM_WD = 0.01
ADAM_STEP = 10
PER_EXAMPLE_BATCH_AXIS = {'x': 0, 'mem': 0, 'positions': 0, 'loss_target': 0}
SHARED_INPUTS = []
_WEIGHT_DTYPES = {'ev_norm': _jnp.float32, 'ev_w_in': _jnp.float32, 'ev_pool_w': _jnp.float32, 'ev_pool_scale': _jnp.float32, 'ev_q_norm': _jnp.float32, 'ev_w_q_up': _jnp.float32, 'ev_kv_norm': _jnp.float32, 'ev_w_kv_up': _jnp.float32, 'ev_w_out': _jnp.float32, 'od_norm': _jnp.float32, 'od_w_in': _jnp.float32, 'od_conv_w': _jnp.float32, 'od_conv_b': _jnp.float32, 'od_w_rgate': _jnp.float32, 'od_b_rgate': _jnp.float32, 'od_w_igate': _jnp.float32, 'od_b_igate': _jnp.float32, 'od_lambda': _jnp.float32, 'od_w_out': _jnp.float32, 'xa_norm_x': _jnp.float32, 'xa_norm_mem': _jnp.float32, 'xa_w_q': _jnp.float32, 'xa_w_kv': _jnp.float32, 'xa_w_o': _jnp.float32, 'ffn_norm': _jnp.float32, 'ffn_w_gate_up': _jnp.float32, 'ffn_w_down': _jnp.float32, 'final_norm': _jnp.float32}
MOMENT_SCALE = {'ev_norm': 2.474111e-01, 'ev_w_in': 2.574917e-01, 'ev_pool_w': 3.370228e-01, 'ev_pool_scale': 3.274308e-01, 'ev_q_norm': 9.646495e-02, 'ev_w_q_up': 5.646459e-02, 'ev_kv_norm': 2.276296e-01, 'ev_w_kv_up': 8.230692e-02, 'ev_w_out': 2.354986e-01, 'od_norm': 1.630753e-01, 'od_w_in': 1.265220e-01, 'od_conv_w': 1.376395e-01, 'od_conv_b': 1.606848e+00, 'od_w_rgate': 3.649868e-02, 'od_b_rgate': 3.160090e-02, 'od_w_igate': 6.518776e-02, 'od_b_igate': 4.995977e-02, 'od_lambda': 6.404220e-02, 'od_w_out': 1.243706e-01, 'xa_norm_x': 3.457255e-02, 'xa_norm_mem': 6.003965e-02, 'xa_w_q': 3.438712e-02, 'xa_w_kv': 4.671083e-02, 'xa_w_o': 5.852006e-02, 'ffn_norm': 2.480826e-01, 'ffn_w_gate_up': 9.967218e-02, 'ffn_w_down': 1.625412e-01, 'final_norm': 1.280719e+02}


def _to_microbatches(a, axis):
    t = _jnp.moveaxis(a, axis, 0)
    t = t.reshape((N_MICROBATCH, t.shape[0] // N_MICROBATCH) + t.shape[1:])
    return _jnp.moveaxis(t, 1, axis + 1)


def setup_inputs(seed: int = 0) -> dict:
    inp = _fwd_setup_inputs(seed)
    key = _jax.random.fold_in(_jax.random.key(seed), 7919)
    shape, _ = _output_shape()
    out = dict(inp)
    out["loss_target"] = _jax.random.normal(_jax.random.fold_in(key, 0), shape, _jnp.float32)
    for i, name in enumerate(TWIN_WEIGHTS):
        w = inp[name].astype(_jnp.float32)
        if MOMENT_SCALE is None:
            s = _jnp.sqrt(_jnp.mean(_jnp.square(w)) + 1e-30)
        else:
            s = MOMENT_SCALE[name]
        km, kv = _jax.random.split(_jax.random.fold_in(key, i + 1))
        out[name] = w
        out["m_" + name] = s * _jax.random.normal(km, w.shape, _jnp.float32)
        out["v_" + name] = (s * s) * _jax.random.uniform(kv, w.shape, _jnp.float32, 0.5, 1.5)
    if N_MICROBATCH > 1:
        for name, axis in PER_EXAMPLE_BATCH_AXIS.items():
            out[name] = _to_microbatches(out[name], axis)
    return {'x': out['x'], 'mem': out['mem'], 'positions': out['positions'], 'ev_norm': out['ev_norm'], 'ev_w_in': out['ev_w_in'], 'ev_pool_w': out['ev_pool_w'], 'ev_pool_scale': out['ev_pool_scale'], 'ev_q_norm': out['ev_q_norm'], 'ev_w_q_up': out['ev_w_q_up'], 'ev_kv_norm': out['ev_kv_norm'], 'ev_w_kv_up': out['ev_w_kv_up'], 'ev_w_out': out['ev_w_out'], 'od_norm': out['od_norm'], 'od_w_in': out['od_w_in'], 'od_conv_w': out['od_conv_w'], 'od_conv_b': out['od_conv_b'], 'od_w_rgate': out['od_w_rgate'], 'od_b_rgate': out['od_b_rgate'], 'od_w_igate': out['od_w_igate'], 'od_b_igate': out['od_b_igate'], 'od_lambda': out['od_lambda'], 'od_w_out': out['od_w_out'], 'xa_norm_x': out['xa_norm_x'], 'xa_norm_mem': out['xa_norm_mem'], 'xa_w_q': out['xa_w_q'], 'xa_w_kv': out['xa_w_kv'], 'xa_w_o': out['xa_w_o'], 'ffn_norm': out['ffn_norm'], 'ffn_w_gate_up': out['ffn_w_gate_up'], 'ffn_w_down': out['ffn_w_down'], 'final_norm': out['final_norm'], 'loss_target': out['loss_target'], 'm_ev_norm': out['m_ev_norm'], 'm_ev_w_in': out['m_ev_w_in'], 'm_ev_pool_w': out['m_ev_pool_w'], 'm_ev_pool_scale': out['m_ev_pool_scale'], 'm_ev_q_norm': out['m_ev_q_norm'], 'm_ev_w_q_up': out['m_ev_w_q_up'], 'm_ev_kv_norm': out['m_ev_kv_norm'], 'm_ev_w_kv_up': out['m_ev_w_kv_up'], 'm_ev_w_out': out['m_ev_w_out'], 'm_od_norm': out['m_od_norm'], 'm_od_w_in': out['m_od_w_in'], 'm_od_conv_w': out['m_od_conv_w'], 'm_od_conv_b': out['m_od_conv_b'], 'm_od_w_rgate': out['m_od_w_rgate'], 'm_od_b_rgate': out['m_od_b_rgate'], 'm_od_w_igate': out['m_od_w_igate'], 'm_od_b_igate': out['m_od_b_igate'], 'm_od_lambda': out['m_od_lambda'], 'm_od_w_out': out['m_od_w_out'], 'm_xa_norm_x': out['m_xa_norm_x'], 'm_xa_norm_mem': out['m_xa_norm_mem'], 'm_xa_w_q': out['m_xa_w_q'], 'm_xa_w_kv': out['m_xa_w_kv'], 'm_xa_w_o': out['m_xa_w_o'], 'm_ffn_norm': out['m_ffn_norm'], 'm_ffn_w_gate_up': out['m_ffn_w_gate_up'], 'm_ffn_w_down': out['m_ffn_w_down'], 'm_final_norm': out['m_final_norm'], 'v_ev_norm': out['v_ev_norm'], 'v_ev_w_in': out['v_ev_w_in'], 'v_ev_pool_w': out['v_ev_pool_w'], 'v_ev_pool_scale': out['v_ev_pool_scale'], 'v_ev_q_norm': out['v_ev_q_norm'], 'v_ev_w_q_up': out['v_ev_w_q_up'], 'v_ev_kv_norm': out['v_ev_kv_norm'], 'v_ev_w_kv_up': out['v_ev_w_kv_up'], 'v_ev_w_out': out['v_ev_w_out'], 'v_od_norm': out['v_od_norm'], 'v_od_w_in': out['v_od_w_in'], 'v_od_conv_w': out['v_od_conv_w'], 'v_od_conv_b': out['v_od_conv_b'], 'v_od_w_rgate': out['v_od_w_rgate'], 'v_od_b_rgate': out['v_od_b_rgate'], 'v_od_w_igate': out['v_od_w_igate'], 'v_od_b_igate': out['v_od_b_igate'], 'v_od_lambda': out['v_od_lambda'], 'v_od_w_out': out['v_od_w_out'], 'v_xa_norm_x': out['v_xa_norm_x'], 'v_xa_norm_mem': out['v_xa_norm_mem'], 'v_xa_w_q': out['v_xa_w_q'], 'v_xa_w_kv': out['v_xa_w_kv'], 'v_xa_w_o': out['v_xa_w_o'], 'v_ffn_norm': out['v_ffn_norm'], 'v_ffn_w_gate_up': out['v_ffn_w_gate_up'], 'v_ffn_w_down': out['v_ffn_w_down'], 'v_final_norm': out['v_final_norm']}


def _loss(weights, diff, rest, loss_target):
    with _jax.named_scope("forward"):
        args = {**rest, TWIN_DIFF_INPUT: diff, **{k: w.astype(_WEIGHT_DTYPES[k]) for k, w in weights.items()}}
        y = _forward(args)
    with _jax.named_scope("loss_head"):
        err = _jnp.square(y.astype(_jnp.float32) - loss_target)
        return 0.5 * _jnp.sum(_jnp.mean(err, axis=-1)) if err.ndim else 0.5 * err


def _adamw(w, g, m, v):
    m = ADAM_B1 * m + (1.0 - ADAM_B1) * g
    v = ADAM_B2 * v + (1.0 - ADAM_B2) * _jnp.square(g)
    m_hat = m / (1.0 - ADAM_B1 ** ADAM_STEP)
    v_hat = v / (1.0 - ADAM_B2 ** ADAM_STEP)
    delta = -ADAM_LR * (m_hat / (_jnp.sqrt(v_hat) + ADAM_EPS) + ADAM_WD * w)
    return delta, m, v


def reference(x, mem, positions, ev_norm, ev_w_in, ev_pool_w, ev_pool_scale, ev_q_norm, ev_w_q_up, ev_kv_norm, ev_w_kv_up, ev_w_out, od_norm, od_w_in, od_conv_w, od_conv_b, od_w_rgate, od_b_rgate, od_w_igate, od_b_igate, od_lambda, od_w_out, xa_norm_x, xa_norm_mem, xa_w_q, xa_w_kv, xa_w_o, ffn_norm, ffn_w_gate_up, ffn_w_down, final_norm, loss_target, m_ev_norm, m_ev_w_in, m_ev_pool_w, m_ev_pool_scale, m_ev_q_norm, m_ev_w_q_up, m_ev_kv_norm, m_ev_w_kv_up, m_ev_w_out, m_od_norm, m_od_w_in, m_od_conv_w, m_od_conv_b, m_od_w_rgate, m_od_b_rgate, m_od_w_igate, m_od_b_igate, m_od_lambda, m_od_w_out, m_xa_norm_x, m_xa_norm_mem, m_xa_w_q, m_xa_w_kv, m_xa_w_o, m_ffn_norm, m_ffn_w_gate_up, m_ffn_w_down, m_final_norm, v_ev_norm, v_ev_w_in, v_ev_pool_w, v_ev_pool_scale, v_ev_q_norm, v_ev_w_q_up, v_ev_kv_norm, v_ev_w_kv_up, v_ev_w_out, v_od_norm, v_od_w_in, v_od_conv_w, v_od_conv_b, v_od_w_rgate, v_od_b_rgate, v_od_w_igate, v_od_b_igate, v_od_lambda, v_od_w_out, v_xa_norm_x, v_xa_norm_mem, v_xa_w_q, v_xa_w_kv, v_xa_w_o, v_ffn_norm, v_ffn_w_gate_up, v_ffn_w_down, v_final_norm):
    given = dict(x=x, mem=mem, positions=positions, ev_norm=ev_norm, ev_w_in=ev_w_in, ev_pool_w=ev_pool_w, ev_pool_scale=ev_pool_scale, ev_q_norm=ev_q_norm, ev_w_q_up=ev_w_q_up, ev_kv_norm=ev_kv_norm, ev_w_kv_up=ev_w_kv_up, ev_w_out=ev_w_out, od_norm=od_norm, od_w_in=od_w_in, od_conv_w=od_conv_w, od_conv_b=od_conv_b, od_w_rgate=od_w_rgate, od_b_rgate=od_b_rgate, od_w_igate=od_w_igate, od_b_igate=od_b_igate, od_lambda=od_lambda, od_w_out=od_w_out, xa_norm_x=xa_norm_x, xa_norm_mem=xa_norm_mem, xa_w_q=xa_w_q, xa_w_kv=xa_w_kv, xa_w_o=xa_w_o, ffn_norm=ffn_norm, ffn_w_gate_up=ffn_w_gate_up, ffn_w_down=ffn_w_down, final_norm=final_norm, loss_target=loss_target, m_ev_norm=m_ev_norm, m_ev_w_in=m_ev_w_in, m_ev_pool_w=m_ev_pool_w, m_ev_pool_scale=m_ev_pool_scale, m_ev_q_norm=m_ev_q_norm, m_ev_w_q_up=m_ev_w_q_up, m_ev_kv_norm=m_ev_kv_norm, m_ev_w_kv_up=m_ev_w_kv_up, m_ev_w_out=m_ev_w_out, m_od_norm=m_od_norm, m_od_w_in=m_od_w_in, m_od_conv_w=m_od_conv_w, m_od_conv_b=m_od_conv_b, m_od_w_rgate=m_od_w_rgate, m_od_b_rgate=m_od_b_rgate, m_od_w_igate=m_od_w_igate, m_od_b_igate=m_od_b_igate, m_od_lambda=m_od_lambda, m_od_w_out=m_od_w_out, m_xa_norm_x=m_xa_norm_x, m_xa_norm_mem=m_xa_norm_mem, m_xa_w_q=m_xa_w_q, m_xa_w_kv=m_xa_w_kv, m_xa_w_o=m_xa_w_o, m_ffn_norm=m_ffn_norm, m_ffn_w_gate_up=m_ffn_w_gate_up, m_ffn_w_down=m_ffn_w_down, m_final_norm=m_final_norm, v_ev_norm=v_ev_norm, v_ev_w_in=v_ev_w_in, v_ev_pool_w=v_ev_pool_w, v_ev_pool_scale=v_ev_pool_scale, v_ev_q_norm=v_ev_q_norm, v_ev_w_q_up=v_ev_w_q_up, v_ev_kv_norm=v_ev_kv_norm, v_ev_w_kv_up=v_ev_w_kv_up, v_ev_w_out=v_ev_w_out, v_od_norm=v_od_norm, v_od_w_in=v_od_w_in, v_od_conv_w=v_od_conv_w, v_od_conv_b=v_od_conv_b, v_od_w_rgate=v_od_w_rgate, v_od_b_rgate=v_od_b_rgate, v_od_w_igate=v_od_w_igate, v_od_b_igate=v_od_b_igate, v_od_lambda=v_od_lambda, v_od_w_out=v_od_w_out, v_xa_norm_x=v_xa_norm_x, v_xa_norm_mem=v_xa_norm_mem, v_xa_w_q=v_xa_w_q, v_xa_w_kv=v_xa_w_kv, v_xa_w_o=v_xa_w_o, v_ffn_norm=v_ffn_norm, v_ffn_w_gate_up=v_ffn_w_gate_up, v_ffn_w_down=v_ffn_w_down, v_final_norm=v_final_norm)
    weights = {n: given[n] for n in TWIN_WEIGHTS}
    shared = {n: given[n] for n in SHARED_INPUTS}
    per_example = {n: given[n] for n in ['x', 'mem', 'positions']}
    grad_fn = _jax.value_and_grad(_loss, argnums=(0, 1))

    def one_microbatch(ex, loss_target):
        ex = dict(ex)
        diff = ex.pop(TWIN_DIFF_INPUT)
        return grad_fn(weights, diff, {**shared, **ex}, loss_target)

    if N_MICROBATCH == 1:
        loss, (grad_w, grad_x) = one_microbatch(per_example, given["loss_target"])
    else:
        def body(carry, xs):
            loss_sum, grad_sum = carry
            l_k, (gw_k, gx_k) = one_microbatch(xs[0], xs[1])
            with _jax.named_scope("update"):
                return (loss_sum + l_k, _jax.tree.map(_jnp.add, grad_sum, gw_k)), gx_k

        init = (_jnp.zeros((), _jnp.float32), _jax.tree.map(_jnp.zeros_like, weights))
        (loss, grad_w), grad_x = _jax.lax.scan(body, init, (per_example, given["loss_target"]))
    with _jax.named_scope("update"):
        delta_w, new_m, new_v = {}, {}, {}
        for n in TWIN_WEIGHTS:
            delta_w[n], new_m[n], new_v[n] = _adamw(weights[n], grad_w[n], given["m_" + n], given["v_" + n])
    return (loss, grad_x, *[grad_w[n] for n in TWIN_WEIGHTS], *[delta_w[n] for n in TWIN_WEIGHTS],
            *[new_m[n] for n in TWIN_WEIGHTS], *[new_v[n] for n in TWIN_WEIGHTS])
```

```python
import functools
import math

import jax
import jax.numpy as jnp
from jax import lax
from jax.experimental import pallas as pl
from jax.experimental.pallas import tpu as pltpu

F32 = jnp.float32
BF16 = jnp.bfloat16

D_MODEL = 1024
POOL_DIM = 512
POOL_WINDOWS = (2, 4, 8, 16)
POOL_GROUP = 128
MLA_HEADS = 8
QK_NOPE = 64
QK_ROPE = 32
QK_DIM = QK_NOPE + QK_ROPE
V_HEAD = 64
HEAD_PAD = 128
Q_RANK = 256
KV_RANK = 128
ROPE_BASE = 10000.0
LRU_HEADS = 4
LRU_HEAD_DIM = 256
CONV_WIDTH = 4
LRU_C = 8.0
MEM_HEADS = 4
MEM_HEAD_DIM = 256
D_FF = 2816
RMS_EPS = 1e-6
NEG_INF = -1e30

ADAM_LR = 0.001
ADAM_B1 = 0.9
ADAM_B2 = 0.999
ADAM_EPS = 1e-08
ADAM_WD = 0.01
ADAM_STEP = 10

N_CHIPS = 4
LANES = 128
VMEM_LIMIT = 56 * 1024 * 1024
MESH = pl.DeviceIdType.MESH

NN = (((1,), (0,)), ((), ()))
NT = (((1,), (1,)), ((), ()))
TN = (((0,), (0,)), ((), ()))


def _cp(n):
    return pltpu.CompilerParams(dimension_semantics=("arbitrary",) * n, vmem_limit_bytes=VMEM_LIMIT)


def _dot(a, b, dims=NN):
    return lax.dot_general(a, b, dims, preferred_element_type=F32)


def _rows(ts, w, cb=0):
    return pl.BlockSpec((ts, w), lambda i: (i, cb))


def _const(shape):
    return pl.BlockSpec(shape, lambda i: (0,) * len(shape))


def _mm(a, bs, epi, outs, *, tn, nj, nt=False, extras=(), a_cb=0, k=None, tm=None, name):
    M = a.shape[0]
    k = k or a.shape[1]
    tm = tm or min(M, 512)
    nb, ne = len(bs), len(extras)
    dims = NT if nt else NN

    def body(*refs):
        av = refs[0][...].astype(BF16)
        accs = [_dot(av, r[...].astype(BF16), dims) for r in refs[1:1 + nb]]
        vals = epi(accs, [r[...] for r in refs[1 + nb:1 + nb + ne]])
        for o, v in zip(refs[1 + nb + ne:], vals):
            o[...] = v.astype(o.dtype)

    in_specs = [pl.BlockSpec((tm, k), lambda j, i: (i, a_cb))]
    for (_, rb, cb) in bs:
        if nt:
            in_specs.append(pl.BlockSpec((tn, k), lambda j, i, rb=rb, cb=cb: (rb + j, cb)))
        else:
            in_specs.append(pl.BlockSpec((k, tn), lambda j, i, rb=rb, cb=cb: (rb, cb + j)))
    for (_, cb) in extras:
        in_specs.append(pl.BlockSpec((tm, tn), lambda j, i, cb=cb: (i, cb + j)))
    out_specs = [pl.BlockSpec((tm, tn), lambda j, i, cb=cb: (i, cb + j)) for (_, _, cb) in outs]
    res = pl.pallas_call(
        body, grid=(nj, M // tm), in_specs=in_specs, out_specs=out_specs,
        out_shape=[jax.ShapeDtypeStruct((M, n), dt) for (n, dt, _) in outs],
        compiler_params=_cp(2), name=name,
    )(a, *[b for (b, _, _) in bs], *[e for (e, _) in extras])
    return res


def _first(accs, extras):
    return [accs[0]]


def _add_res(accs, extras):
    return [accs[0] + extras[0].astype(F32)]


def _mm_tn(a, b, *, ka=None, a_cb=0, nb=None, b_cb=0, tk=None, tn=None, ts=None, name):
    S = a.shape[0]
    ka = ka or a.shape[1]
    nb = nb or b.shape[1]
    tk = tk or ka
    tn = tn or nb
    ts = ts or min(S, 512)
    a0, b0 = a_cb * (ka // tk), b_cb * (nb // tn)

    def body(a_ref, b_ref, o_ref):
        @pl.when(pl.program_id(2) == 0)
        def _():
            o_ref[...] = jnp.zeros_like(o_ref)

        o_ref[...] += _dot(a_ref[...].astype(BF16), b_ref[...].astype(BF16), TN)

    return pl.pallas_call(
        body, grid=(ka // tk, nb // tn, S // ts),
        in_specs=[pl.BlockSpec((ts, tk), lambda p, q, s: (s, a0 + p)),
                  pl.BlockSpec((ts, tn), lambda p, q, s: (s, b0 + q))],
        out_specs=pl.BlockSpec((tk, tn), lambda p, q, s: (p, q)),
        out_shape=jax.ShapeDtypeStruct((ka, nb), F32), compiler_params=_cp(3), name=name,
    )(a, b)


def _mm_tn_grouped(a, b, groups, w, *, name):
    S = a.shape[0]
    ts = min(S, 512)

    def body(a_ref, b_ref, o_ref):
        @pl.when(pl.program_id(1) == 0)
        def _():
            o_ref[...] = jnp.zeros_like(o_ref)

        o_ref[0] += _dot(a_ref[...].astype(BF16), b_ref[...].astype(BF16), TN)

    return pl.pallas_call(
        body, grid=(groups, S // ts),
        in_specs=[pl.BlockSpec((ts, w), lambda g, s: (s, g)), pl.BlockSpec((ts, w), lambda g, s: (s, g))],
        out_specs=pl.BlockSpec((1, w, w), lambda g, s: (g, 0, 0)),
        out_shape=jax.ShapeDtypeStruct((groups, w, w), F32), compiler_params=_cp(2), name=name,
    )(a, b)


def _rms(x, g, *, cb=0, w=None, ts=None, name):
    S = x.shape[0]
    w = w or x.shape[1]
    ts = ts or min(S, 512)

    def body(x_ref, g_ref, o_ref):
        xv = x_ref[...].astype(F32)
        r = lax.rsqrt(jnp.mean(xv * xv, axis=-1, keepdims=True) + RMS_EPS)
        o_ref[...] = (xv * r * g_ref[...]).astype(o_ref.dtype)

    return pl.pallas_call(
        body, grid=(S // ts,), in_specs=[_rows(ts, w, cb), _const((1, w))], out_specs=_rows(ts, w),
        out_shape=jax.ShapeDtypeStruct((S, w), BF16), compiler_params=_cp(1), name=name,
    )(x, g.reshape(1, w))


def _rms_bwd(x, g, dy, *, cb=0, w=None, res=None, out_dtype=F32, ts=None, name):
    S = x.shape[0]
    w = w or x.shape[1]
    ts = ts or min(S, 512)
    has_res = res is not None

    def body(*refs):
        x_ref, g_ref, dy_ref = refs[:3]
        dx_ref, dg_ref = refs[-2:]
        xv = x_ref[...].astype(F32)
        r = lax.rsqrt(jnp.mean(xv * xv, axis=-1, keepdims=True) + RMS_EPS)
        n = xv * r
        dyv = dy_ref[...].astype(F32)
        dn = dyv * g_ref[...]
        dx = r * (dn - n * jnp.mean(dn * n, axis=-1, keepdims=True))
        if has_res:
            dx = dx + refs[3][...].astype(F32)
        dx_ref[...] = dx.astype(dx_ref.dtype)

        @pl.when(pl.program_id(0) == 0)
        def _():
            dg_ref[...] = jnp.zeros_like(dg_ref)

        dg_ref[...] += jnp.sum(dyv * n, axis=0, keepdims=True)

    ins = [x, g.reshape(1, w), dy] + ([res] if has_res else [])
    in_specs = [_rows(ts, w, cb), _const((1, w)), _rows(ts, w)] + ([_rows(ts, w)] if has_res else [])
    return pl.pallas_call(
        body, grid=(S // ts,), in_specs=in_specs, out_specs=[_rows(ts, w), _const((1, w))],
        out_shape=[jax.ShapeDtypeStruct((S, w), out_dtype), jax.ShapeDtypeStruct((1, w), F32)],
        compiler_params=_cp(1), name=name,
    )(*ins)


HALO = 16


def _pool_counts(i, ts, rows, first_row):
    t = i * ts + first_row + lax.broadcasted_iota(jnp.int32, (rows, 1), 0)
    return [jnp.minimum(t + 1, w).astype(F32) for w in POOL_WINDOWS]


def _pool_fwd(z, pool_w, pool_scale, *, name):
    S = z.shape[0]
    ts = min(S, 512)
    nh = ts // HALO

    def body(u_ref, halo_ref, w_ref, sc_ref, y_ref, p_ref):
        i = pl.program_id(0)
        u = u_ref[...]
        halo = jnp.where(i > 0, halo_ref[...], 0.0)
        xe = jnp.concatenate([halo, u], axis=0)
        sums = []
        s = xe
        for sh in (1, 2, 4, 8):
            s = s + pltpu.roll(s, sh, 0)
            sums.append(s)
        cnts = _pool_counts(i, ts, ts, 0)
        for g in range(4):
            lo, hi = g * POOL_GROUP, (g + 1) * POOL_GROUP
            pooled = (sums[g][HALO:, lo:hi] / cnts[g] - u[:, lo:hi]).astype(BF16)
            p_ref[:, lo:hi] = pooled
            y_ref[:, lo:hi] = (_dot(pooled, w_ref[g]) * sc_ref[:, lo:hi]).astype(y_ref.dtype)

    return pl.pallas_call(
        body, grid=(S // ts,),
        in_specs=[_rows(ts, POOL_DIM), pl.BlockSpec((HALO, POOL_DIM), lambda i: (jnp.maximum(i * nh - 1, 0), 0)),
                  _const((4, POOL_GROUP, POOL_GROUP)), _const((1, POOL_DIM))],
        out_specs=[_rows(ts, POOL_DIM), _rows(ts, POOL_DIM)],
        out_shape=[jax.ShapeDtypeStruct((S, POOL_DIM), BF16)] * 2, compiler_params=_cp(1), name=name,
    )(z, z, pool_w, pool_scale)


def _pool_bwd(dmix, pooled, pool_w, pool_scale, *, name):
    S = dmix.shape[0]
    ts = min(S, 512)
    nh = ts // HALO
    last = S // HALO - 1

    def body(dy_ref, dyh_ref, p_ref, w_ref, sc_ref, du_ref, dyp_ref, dsc_ref):
        i = pl.program_id(0)
        dyv = dy_ref[...].astype(F32)
        dyh = jnp.where(i < pl.num_programs(0) - 1, dyh_ref[...].astype(F32), 0.0)
        dye = jnp.concatenate([dyv, dyh], axis=0) * sc_ref[...]
        dypre = dye.astype(BF16)
        dyp_ref[...] = dypre[:ts]
        cnts = _pool_counts(i, ts, ts + HALO, 0)
        n = ts + HALO
        dsc = []
        for g in range(4):
            lo, hi = g * POOL_GROUP, (g + 1) * POOL_GROUP
            ypre = _dot(p_ref[:, lo:hi], w_ref[g])
            dsc.append(jnp.sum(dyv[:, lo:hi] * ypre, axis=0, keepdims=True))
            dpool = _dot(dypre[:, lo:hi], w_ref[g], NT)
            s = dpool / cnts[g]
            for sh in (1, 2, 4, 8)[:g + 1]:
                s = s + pltpu.roll(s, n - sh, 0)
            du_ref[:, lo:hi] = (s[:ts] - dpool[:ts]).astype(du_ref.dtype)

        @pl.when(i == 0)
        def _():
            dsc_ref[...] = jnp.zeros_like(dsc_ref)

        dsc_ref[...] += jnp.concatenate(dsc, axis=1)

    return pl.pallas_call(
        body, grid=(S // ts,),
        in_specs=[_rows(ts, POOL_DIM),
                  pl.BlockSpec((HALO, POOL_DIM), lambda i: (jnp.minimum((i + 1) * nh, last), 0)),
                  _rows(ts, POOL_DIM), _const((4, POOL_GROUP, POOL_GROUP)), _const((1, POOL_DIM))],
        out_specs=[_rows(ts, POOL_DIM), _rows(ts, POOL_DIM), _const((1, POOL_DIM))],
        out_shape=[jax.ShapeDtypeStruct((S, POOL_DIM), BF16)] * 2 + [jax.ShapeDtypeStruct((1, POOL_DIM), F32)],
        compiler_params=_cp(1), name=name,
    )(dmix, dmix, pooled, pool_w, pool_scale)


def _rope_partner(t):
    lane = lax.broadcasted_iota(jnp.int32, t.shape, 1)
    swapped = jnp.where(lane < QK_NOPE + QK_ROPE // 2, pltpu.roll(t, HEAD_PAD - QK_ROPE // 2, 1),
                        pltpu.roll(t, QK_ROPE // 2, 1))
    return jnp.where((lane >= QK_NOPE) & (lane < QK_DIM), swapped, 0.0)


def _rope_fwd(q_pad, k_pad, z, ctab, stab, *, name):
    S = q_pad.shape[0]
    ts = min(S, 512)

    def body(q_ref, k_ref, kr_ref, c_ref, s_ref, qo_ref, ko_ref):
        c, s = c_ref[...], s_ref[...]
        q = q_ref[...]
        qo_ref[...] = (q * c + _rope_partner(q) * s).astype(qo_ref.dtype)
        kr = kr_ref[...]
        ko_ref[...] = (k_ref[...] + kr * c + _rope_partner(kr) * s).astype(ko_ref.dtype)

    blk = pl.BlockSpec((ts, HEAD_PAD), lambda i, h: (i, h))
    tab = pl.BlockSpec((ts, HEAD_PAD), lambda i, h: (i, 0))
    return pl.pallas_call(
        body, grid=(S // ts, MLA_HEADS),
        in_specs=[blk, blk, pl.BlockSpec((ts, HEAD_PAD), lambda i, h: (i, 7)), tab, tab],
        out_specs=[blk, blk], out_shape=[jax.ShapeDtypeStruct((S, MLA_HEADS * HEAD_PAD), BF16)] * 2,
        compiler_params=_cp(2), name=name,
    )(q_pad, k_pad, z, ctab, stab)


def _rope_bwd(dq_rot, dk_cat, ctab, stab, *, name):
    S = dq_rot.shape[0]
    ts = min(S, 512)

    def body(dq_ref, dk_ref, c_ref, s_ref, dqo_ref, dkr_ref):
        c, s = c_ref[...], s_ref[...]
        for h in range(MLA_HEADS):
            g = dq_ref[:, h * HEAD_PAD:(h + 1) * HEAD_PAD]
            dqo_ref[:, h * HEAD_PAD:(h + 1) * HEAD_PAD] = (g * c + _rope_partner(g * s)).astype(dqo_ref.dtype)
        dk = dk_ref[...]
        g = dk[:, :HEAD_PAD]
        for h in range(1, MLA_HEADS):
            g = g + dk[:, h * HEAD_PAD:(h + 1) * HEAD_PAD]
        lane = lax.broadcasted_iota(jnp.int32, g.shape, 1)
        on_rope = (lane >= QK_NOPE) & (lane < QK_DIM)
        dkr_ref[...] = jnp.where(on_rope, g * c + _rope_partner(g * s), 0.0).astype(dkr_ref.dtype)

    wide = _rows(ts, MLA_HEADS * HEAD_PAD)
    return pl.pallas_call(
        body, grid=(S // ts,), in_specs=[wide, wide, _rows(ts, HEAD_PAD), _rows(ts, HEAD_PAD)],
        out_specs=[wide, _rows(ts, HEAD_PAD)],
        out_shape=[jax.ShapeDtypeStruct((S, MLA_HEADS * HEAD_PAD), BF16), jax.ShapeDtypeStruct((S, HEAD_PAD), BF16)],
        compiler_params=_cp(1), name=name,
    )(dq_rot, dk_cat, ctab, stab)


ATT_SCALE = QK_DIM ** -0.5


def _flash_fwd(q, k, v, *, name):
    S = q.shape[0]
    tq = min(S, 512)

    def body(q_ref, k_ref, v_ref, o_ref, lse_ref):
        qi = pl.program_id(1)
        qv = q_ref[...]

        def step(j, carry, masked):
            m, l, acc = carry
            off = pl.multiple_of(j * tq, tq)
            s = _dot(qv, k_ref[pl.ds(off, tq), :], NT) * ATT_SCALE
            if masked:
                row = lax.broadcasted_iota(jnp.int32, (tq, tq), 0)
                col = lax.broadcasted_iota(jnp.int32, (tq, tq), 1)
                s = jnp.where(col <= row, s, NEG_INF)
            m_new = jnp.maximum(m, jnp.max(s, axis=-1, keepdims=True))
            p = jnp.exp(s - m_new)
            alpha = jnp.exp(m - m_new)
            l = alpha * l + jnp.sum(p, axis=-1, keepdims=True)
            acc = alpha * acc + _dot(p.astype(BF16), v_ref[pl.ds(off, tq), :])
            return m_new, l, acc

        init = (jnp.full((tq, 1), NEG_INF, F32), jnp.zeros((tq, 1), F32), jnp.zeros((tq, HEAD_PAD), F32))
        carry = lax.fori_loop(0, qi, lambda j, c: step(j, c, False), init)
        m, l, acc = step(qi, carry, True)
        o_ref[...] = (acc / l).astype(o_ref.dtype)
        lse_ref[...] = jnp.broadcast_to(m + jnp.log(l), (tq, HEAD_PAD))

    blk = pl.BlockSpec((tq, HEAD_PAD), lambda h, i: (i, h))
    full = pl.BlockSpec((S, HEAD_PAD), lambda h, i: (0, h))
    return pl.pallas_call(
        body, grid=(MLA_HEADS, S // tq), in_specs=[blk, full, full], out_specs=[blk, blk],
        out_shape=[jax.ShapeDtypeStruct((S, MLA_HEADS * HEAD_PAD), BF16),
                   jax.ShapeDtypeStruct((S, MLA_HEADS * HEAD_PAD), F32)],
        compiler_params=_cp(2), name=name,
    )(q, k, v)


def _attn_delta(dmix, o, *, name):
    S = o.shape[0]
    ts = min(S, 512)

    def body(do_ref, o_ref, d_ref):
        d = jnp.sum(do_ref[...].astype(F32) * o_ref[...].astype(F32), axis=-1, keepdims=True)
        d_ref[...] = jnp.broadcast_to(d, (ts, HEAD_PAD))

    blk = pl.BlockSpec((ts, HEAD_PAD), lambda i, h: (i, h))
    return pl.pallas_call(
        body, grid=(S // ts, MLA_HEADS),
        in_specs=[pl.BlockSpec((ts, HEAD_PAD), lambda i, h: (i, POOL_DIM // HEAD_PAD + h)), blk], out_specs=blk,
        out_shape=jax.ShapeDtypeStruct((S, MLA_HEADS * HEAD_PAD), F32), compiler_params=_cp(2), name=name,
    )(dmix, o)


def _as_rows(col, tq):
    S = col.shape[0]
    r = col.reshape(S, MLA_HEADS, HEAD_PAD)[:, :, 0].T.reshape(MLA_HEADS, S // tq, 1, tq)
    return jnp.broadcast_to(r, (MLA_HEADS, S // tq, 8, tq)).reshape(MLA_HEADS, (S // tq) * 8, tq)


def _flash_bwd(q, k, v, dmix, lse_rows, delta_rows, *, name):
    S = q.shape[0]
    tq = min(S, 512)
    nq = S // tq

    def body(q_ref, do_ref, lse_ref, dl_ref, k_ref, v_ref, dq_ref, dk_ref, dv_ref):
        j = pl.program_id(1)

        @pl.when(j == 0)
        def _():
            dq_ref[...] = jnp.zeros_like(dq_ref)

        kv, vv = k_ref[...], v_ref[...]

        def step(i, carry, masked):
            dk, dv = carry
            off = pl.multiple_of(i * tq, tq)
            off8 = pl.multiple_of(i * 8, 8)
            qv = q_ref[pl.ds(off, tq), :]
            dov = do_ref[pl.ds(off, tq), :]
            lse = lse_ref[0, pl.ds(off8, 8), :][0:1]
            dl = dl_ref[0, pl.ds(off8, 8), :][0:1]
            st = _dot(kv, qv, NT) * ATT_SCALE
            if masked:
                krow = lax.broadcasted_iota(jnp.int32, (tq, tq), 0)
                qcol = lax.broadcasted_iota(jnp.int32, (tq, tq), 1)
                st = jnp.where(krow <= qcol, st, NEG_INF)
            pt = jnp.exp(st - lse)
            dv = dv + _dot(pt.astype(BF16), dov)
            dpt = _dot(vv, dov, NT)
            dst = (pt * (dpt - dl) * ATT_SCALE).astype(BF16)
            dk = dk + _dot(dst, qv)
            dq_ref[pl.ds(off, tq), :] += _dot(dst, kv, TN)
            return dk, dv

        zero = jnp.zeros((tq, HEAD_PAD), F32)
        carry = step(j, (zero, zero), True)
        dk, dv = lax.fori_loop(j + 1, nq, lambda i, c: step(i, c, False), carry)
        dk_ref[...] = dk
        dv_ref[...] = dv

    blk = pl.BlockSpec((tq, HEAD_PAD), lambda h, j: (j, h))
    full = pl.BlockSpec((S, HEAD_PAD), lambda h, j: (0, h))
    stat = pl.BlockSpec((1, nq * 8, tq), lambda h, j: (h, 0, 0))
    wide = jax.ShapeDtypeStruct((S, MLA_HEADS * HEAD_PAD), F32)
    return pl.pallas_call(
        body, grid=(MLA_HEADS, nq),
        in_specs=[full, pl.BlockSpec((S, HEAD_PAD), lambda h, j: (0, POOL_DIM // HEAD_PAD + h)), stat, stat, blk, blk],
        out_specs=[full, blk, blk], out_shape=[wide, wide, wide], compiler_params=_cp(2), name=name,
    )(q, dmix, lse_rows, delta_rows, k, v)


MEM_SCALE = MEM_HEAD_DIM ** -0.5


def _xattn_probs(qh, kh):
    s = _dot(qh, kh, NT) * MEM_SCALE
    e = jnp.exp(s - jnp.max(s, axis=-1, keepdims=True))
    return e / jnp.sum(e, axis=-1, keepdims=True)


def _xattn_fwd(q, kvm, *, name):
    S = q.shape[0]
    ts = min(S, 512)
    nm = kvm.shape[0]

    def body(q_ref, kv_ref, o_ref):
        for h in range(MEM_HEADS):
            lo, hi = h * MEM_HEAD_DIM, (h + 1) * MEM_HEAD_DIM
            p = _xattn_probs(q_ref[:, lo:hi], kv_ref[:, lo:hi])
            o_ref[:, lo:hi] = _dot(p.astype(BF16), kv_ref[:, D_MODEL + lo:D_MODEL + hi]).astype(o_ref.dtype)

    return pl.pallas_call(
        body, grid=(S // ts,), in_specs=[_rows(ts, D_MODEL), _const((nm, 2 * D_MODEL))],
        out_specs=_rows(ts, D_MODEL), out_shape=jax.ShapeDtypeStruct((S, D_MODEL), BF16),
        compiler_params=_cp(1), name=name,
    )(q, kvm)


def _xattn_bwd(q, kvm, do, *, name):
    S = q.shape[0]
    ts = min(S, 512)
    nm = kvm.shape[0]

    def body(q_ref, kv_ref, do_ref, dq_ref, dkv_ref):
        @pl.when(pl.program_id(0) == 0)
        def _():
            dkv_ref[...] = jnp.zeros_like(dkv_ref)

        for h in range(MEM_HEADS):
            lo, hi = h * MEM_HEAD_DIM, (h + 1) * MEM_HEAD_DIM
            qh, kh, vh = q_ref[:, lo:hi], kv_ref[:, lo:hi], kv_ref[:, D_MODEL + lo:D_MODEL + hi]
            doh = do_ref[:, lo:hi].astype(BF16)
            p = _xattn_probs(qh, kh)
            dp = _dot(doh, vh, NT)
            ds = (p * (dp - jnp.sum(dp * p, axis=-1, keepdims=True)) * MEM_SCALE).astype(BF16)
            dq_ref[:, lo:hi] = _dot(ds, kh).astype(dq_ref.dtype)
            dkv_ref[:, lo:hi] += _dot(ds, qh, TN)
            dkv_ref[:, D_MODEL + lo:D_MODEL + hi] += _dot(p.astype(BF16), doh, TN)

    return pl.pallas_call(
        body, grid=(S // ts,), in_specs=[_rows(ts, D_MODEL), _const((nm, 2 * D_MODEL)), _rows(ts, D_MODEL)],
        out_specs=[_rows(ts, D_MODEL), _const((nm, 2 * D_MODEL))],
        out_shape=[jax.ShapeDtypeStruct((S, D_MODEL), BF16), jax.ShapeDtypeStruct((nm, 2 * D_MODEL), F32)],
        compiler_params=_cp(1), name=name,
    )(q, kvm, do)


CONV_HALO = 8


def _sigmoid(x):
    return 1.0 / (1.0 + jnp.exp(-x))


def _softplus(x):
    return jnp.maximum(x, 0.0) + jnp.log(1.0 + jnp.exp(-jnp.abs(x)))


def _neg_expm1(x):
    series = -x * (1.0 + x * (1.0 / 2) * (1.0 + x * (1.0 / 3) * (1.0 + x * (1.0 / 4) * (1.0 + x * (1.0 / 5)))))
    return jnp.where(x > -0.05, series, 1.0 - jnp.exp(x))


GELU_C = math.sqrt(2.0 / math.pi)


def _gelu(x):
    return 0.5 * x * (1.0 + jnp.tanh(GELU_C * (x + 0.044715 * x * x * x)))


def _gelu_grad(x):
    t = jnp.tanh(GELU_C * (x + 0.044715 * x * x * x))
    return 0.5 * (1.0 + t) + 0.5 * x * (1.0 - t * t) * GELU_C * (1.0 + 3 * 0.044715 * x * x)


def _lru_gates(xc, wr_ref, br, wi_ref, bi, sp, reset):
    xcb = xc.astype(BF16)
    pr, pi = [], []
    for h in range(LRU_HEADS):
        lo, hi = h * LRU_HEAD_DIM, (h + 1) * LRU_HEAD_DIM
        pr.append(_dot(xcb[:, lo:hi], wr_ref[h]))
        pi.append(_dot(xcb[:, lo:hi], wi_ref[h]))
    r = _sigmoid(jnp.concatenate(pr, axis=1) + br)
    ig = _sigmoid(jnp.concatenate(pi, axis=1) + bi)
    log_a = -LRU_C * r * sp
    a = jnp.where(reset, 0.0, jnp.exp(log_a))
    mult = jnp.where(reset, 1.0, jnp.sqrt(jnp.maximum(_neg_expm1(2.0 * log_a), 0.0)))
    return r, ig, a, mult


def _lru_fwd(z, reset, conv_w, conv_b, w_r, b_r, w_i, b_i, lam, *, name):
    S = z.shape[0]
    ts = min(S, 512)
    nh = ts // CONV_HALO
    W = D_MODEL

    def body(gate_ref, xb_ref, halo_ref, rs_ref, cw_ref, cb_ref, wr_ref, br_ref, wi_ref, bi_ref, lam_ref,
             xc_ref, h_ref, y_ref, a_buf, carry):
        i = pl.program_id(0)

        @pl.when(i == 0)
        def _():
            carry[...] = jnp.zeros_like(carry)

        halo = jnp.where(i > 0, halo_ref[...], 0.0)
        xe = jnp.concatenate([halo, xb_ref[...]], axis=0)
        xc = cb_ref[...] + cw_ref[3:4, :] * xe[CONV_HALO:]
        for kk in range(CONV_WIDTH - 1):
            xc = xc + cw_ref[kk:kk + 1, :] * pltpu.roll(xe, CONV_WIDTH - 1 - kk, 0)[CONV_HALO:]
        xc_ref[...] = xc
        reset = rs_ref[...] > 0.5
        _, ig, a, mult = _lru_gates(xc, wr_ref, br_ref[...], wi_ref, bi_ref[...], _softplus(-lam_ref[...]), reset)
        a_buf[...] = a
        h_ref[...] = mult * (ig * xc)

        def scan(t, h):
            h = a_buf[pl.ds(t, 1), :] * h + h_ref[pl.ds(t, 1), :]
            h_ref[pl.ds(t, 1), :] = h
            return h

        carry[...] = lax.fori_loop(0, ts, scan, carry[...], unroll=8)
        y_ref[...] = (_gelu(gate_ref[...]) * h_ref[...]).astype(y_ref.dtype)

    vec = _const((1, W))
    gw = _const((LRU_HEADS, LRU_HEAD_DIM, LRU_HEAD_DIM))
    return pl.pallas_call(
        body, grid=(S // ts,),
        in_specs=[_rows(ts, W, 0), _rows(ts, W, 1),
                  pl.BlockSpec((CONV_HALO, W), lambda i: (jnp.maximum(i * nh - 1, 0), 1)),
                  _rows(ts, 1), _const((CONV_WIDTH, W)), vec, gw, vec, gw, vec, vec],
        out_specs=[_rows(ts, W)] * 3,
        out_shape=[jax.ShapeDtypeStruct((S, W), F32), jax.ShapeDtypeStruct((S, W), F32),
                   jax.ShapeDtypeStruct((S, W), BF16)],
        scratch_shapes=[pltpu.VMEM((ts, W), F32), pltpu.VMEM((1, W), F32)],
        compiler_params=_cp(1), name=name,
    )(z, z, z, reset, conv_w, conv_b, w_r, b_r, w_i, b_i, lam)


def _lru_bwd(dy, z, xc, hseq, reset, w_r, b_r, w_i, b_i, lam, *, name):
    S = z.shape[0]
    ts = min(S, 512)
    nt = S // ts
    nh = ts // CONV_HALO
    W = D_MODEL

    def body(dy_ref, gate_ref, xc_ref, h_ref, hh_ref, rs_ref, wr_ref, br_ref, wi_ref, bi_ref, lam_ref,
             dg_ref, dxc_ref, dpr_ref, dpi_ref, acc_ref, a_buf, dh_buf, carry):
        i = pl.program_id(0)
        tile = nt - 1 - i

        @pl.when(i == 0)
        def _():
            carry[...] = jnp.zeros_like(carry)
            acc_ref[...] = jnp.zeros_like(acc_ref)

        xc = xc_ref[...]
        lam_v = lam_ref[...]
        sp = _softplus(-lam_v)
        reset = rs_ref[...] > 0.5
        r, ig, a, mult = _lru_gates(xc, wr_ref, br_ref[...], wi_ref, bi_ref[...], sp, reset)
        gate = gate_ref[...]
        dyv = dy_ref[...].astype(F32)
        h = h_ref[...]
        dg_ref[...] = (dyv * h * _gelu_grad(gate)).astype(dg_ref.dtype)
        a_buf[...] = a
        dh_buf[...] = dyv * _gelu(gate)

        def scan(k, c):
            t = ts - 1 - k
            dh = dh_buf[pl.ds(t, 1), :] + c
            dh_buf[pl.ds(t, 1), :] = dh
            return a_buf[pl.ds(t, 1), :] * dh

        carry[...] = lax.fori_loop(0, ts, scan, carry[...], unroll=8)
        dh = dh_buf[...]
        hh = jnp.where(tile > 0, hh_ref[...], 0.0)
        h_prev = pltpu.roll(jnp.concatenate([hh, h], axis=0), 1, 0)[CONV_HALO:]
        da = dh * h_prev
        bx = ig * xc
        dmult = dh * bx
        dbx = dh * mult
        di = dbx * xc
        dlog_a = jnp.where(reset, 0.0, da * a - dmult * a * a / jnp.maximum(mult, 1e-30))
        dr = dlog_a * (-LRU_C) * sp
        dpre_r = dr * r * (1.0 - r)
        dpre_i = di * ig * (1.0 - ig)
        dprb, dpib = dpre_r.astype(BF16), dpre_i.astype(BF16)
        dpr_ref[...] = dprb
        dpi_ref[...] = dpib
        back = []
        for hd in range(LRU_HEADS):
            lo, hi = hd * LRU_HEAD_DIM, (hd + 1) * LRU_HEAD_DIM
            back.append(_dot(dprb[:, lo:hi], wr_ref[hd], NT) + _dot(dpib[:, lo:hi], wi_ref[hd], NT))
        dxc_ref[...] = dbx * ig + jnp.concatenate(back, axis=1)
        dlam = jnp.sum(dlog_a * (-LRU_C) * r, axis=0, keepdims=True) * (-_sigmoid(-lam_v))
        acc_ref[0:1, :] += jnp.sum(dpre_r, axis=0, keepdims=True)
        acc_ref[1:2, :] += jnp.sum(dpre_i, axis=0, keepdims=True)
        acc_ref[2:3, :] += dlam

    rev = lambda cb: pl.BlockSpec((ts, W), lambda i: (nt - 1 - i, cb))
    vec = _const((1, W))
    gw = _const((LRU_HEADS, LRU_HEAD_DIM, LRU_HEAD_DIM))
    return pl.pallas_call(
        body, grid=(nt,),
        in_specs=[rev(0), rev(0), rev(0), rev(0),
                  pl.BlockSpec((CONV_HALO, W), lambda i: (jnp.maximum((nt - 1 - i) * nh - 1, 0), 0)),
                  pl.BlockSpec((ts, 1), lambda i: (nt - 1 - i, 0)), gw, vec, gw, vec, vec],
        out_specs=[rev(0), rev(0), rev(0), rev(0), _const((8, W))],
        out_shape=[jax.ShapeDtypeStruct((S, W), BF16), jax.ShapeDtypeStruct((S, W), F32),
                   jax.ShapeDtypeStruct((S, W), BF16), jax.ShapeDtypeStruct((S, W), BF16),
                   jax.ShapeDtypeStruct((8, W), F32)],
        scratch_shapes=[pltpu.VMEM((ts, W), F32), pltpu.VMEM((ts, W), F32), pltpu.VMEM((1, W), F32)],
        compiler_params=_cp(1), name=name,
    )(dy, z, xc, hseq, hseq, reset, w_r, b_r, w_i, b_i, lam)


def _conv_bwd(dxc, z, conv_w, *, name):
    S = dxc.shape[0]
    ts = min(S, 512)
    nh = ts // CONV_HALO
    last = S // CONV_HALO - 1
    W = D_MODEL
    n = ts + CONV_HALO

    def body(d_ref, dn_ref, xb_ref, xp_ref, cw_ref, dxb_ref, acc_ref):
        i = pl.program_id(0)

        @pl.when(i == 0)
        def _():
            acc_ref[...] = jnp.zeros_like(acc_ref)

        d = d_ref[...]
        de = jnp.concatenate([d, jnp.where(i < pl.num_programs(0) - 1, dn_ref[...], 0.0)], axis=0)
        xe = jnp.concatenate([jnp.where(i > 0, xp_ref[...], 0.0), xb_ref[...]], axis=0)
        dxb = cw_ref[3:4, :] * d
        acc_ref[3:4, :] += jnp.sum(d * xe[CONV_HALO:], axis=0, keepdims=True)
        for kk in range(CONV_WIDTH - 1):
            sh = CONV_WIDTH - 1 - kk
            dxb = dxb + cw_ref[kk:kk + 1, :] * pltpu.roll(de, n - sh, 0)[:ts]
            acc_ref[kk:kk + 1, :] += jnp.sum(d * pltpu.roll(xe, sh, 0)[CONV_HALO:], axis=0, keepdims=True)
        dxb_ref[...] = dxb.astype(dxb_ref.dtype)
        acc_ref[4:5, :] += jnp.sum(d, axis=0, keepdims=True)

    return pl.pallas_call(
        body, grid=(S // ts,),
        in_specs=[_rows(ts, W), pl.BlockSpec((CONV_HALO, W), lambda i: (jnp.minimum((i + 1) * nh, last), 0)),
                  _rows(ts, W, 1), pl.BlockSpec((CONV_HALO, W), lambda i: (jnp.maximum(i * nh - 1, 0), 1)),
                  _const((CONV_WIDTH, W))],
        out_specs=[_rows(ts, W), _const((8, W))],
        out_shape=[jax.ShapeDtypeStruct((S, W), BF16), jax.ShapeDtypeStruct((8, W), F32)],
        compiler_params=_cp(1), name=name,
    )(dxc, dxc, z, z, conv_w)


def _loss_head(x, g, target, *, name):
    S, D = x.shape
    ts = min(S, 512)

    def body(x_ref, g_ref, t_ref, dx_ref, dg_ref, l_ref):
        @pl.when(pl.program_id(0) == 0)
        def _():
            dg_ref[...] = jnp.zeros_like(dg_ref)
            l_ref[...] = jnp.zeros_like(l_ref)

        xv = x_ref[...]
        r = lax.rsqrt(jnp.mean(xv * xv, axis=-1, keepdims=True) + RMS_EPS)
        n = xv * r
        err = n * g_ref[...] - t_ref[...]
        l_ref[...] += 0.5 * jnp.sum(jnp.sum(err * err, axis=-1, keepdims=True) * (1.0 / D), axis=0, keepdims=True)
        dy = err * (1.0 / D)
        dn = dy * g_ref[...]
        dx_ref[...] = r * (dn - n * jnp.mean(dn * n, axis=-1, keepdims=True))
        dg_ref[...] += jnp.sum(dy * n, axis=0, keepdims=True)

    return pl.pallas_call(
        body, grid=(S // ts,), in_specs=[_rows(ts, D), _const((1, D)), _rows(ts, D)],
        out_specs=[_rows(ts, D), _const((1, D)), _const((8, LANES))],
        out_shape=[jax.ShapeDtypeStruct((S, D), F32), jax.ShapeDtypeStruct((1, D), F32),
                   jax.ShapeDtypeStruct((8, LANES), F32)],
        compiler_params=_cp(1), name=name,
    )(x, g.reshape(1, D), target)


def _adamw(w, g, m, v, *, name):
    shape = w.shape
    cols = shape[-1]
    rows = w.size // cols
    br = rows
    if rows * cols * 4 > (1 << 20):
        br = max(d for d in range(8, rows + 1, 8) if rows % d == 0 and d * cols * 4 <= (3 << 19))

    def body(w_ref, g_ref, m_ref, v_ref, d_ref, mo_ref, vo_ref):
        gv = g_ref[...]
        mn = ADAM_B1 * m_ref[...] + (1.0 - ADAM_B1) * gv
        vn = ADAM_B2 * v_ref[...] + (1.0 - ADAM_B2) * (gv * gv)
        m_hat = mn / (1.0 - ADAM_B1 ** ADAM_STEP)
        v_hat = vn / (1.0 - ADAM_B2 ** ADAM_STEP)
        d_ref[...] = -ADAM_LR * (m_hat / (jnp.sqrt(v_hat) + ADAM_EPS) + ADAM_WD * w_ref[...])
        mo_ref[...] = mn
        vo_ref[...] = vn

    spec = _rows(br, cols)
    outs = pl.pallas_call(
        body, grid=(rows // br,), in_specs=[spec] * 4, out_specs=[spec] * 3,
        out_shape=[jax.ShapeDtypeStruct((rows, cols), F32)] * 3, compiler_params=_cp(1), name=name,
    )(*[t.reshape(rows, cols) for t in (w, g, m, v)])
    return [o.reshape(shape) for o in outs]


def _pad_heads(w, width):
    k = w.shape[0]
    return jnp.pad(w.reshape(k, MLA_HEADS, width), ((0, 0), (0, 0), (0, HEAD_PAD - width))).reshape(k, -1)


def _unpad_heads(w, width):
    k = w.shape[0]
    return w.reshape(k, MLA_HEADS, HEAD_PAD)[:, :, :width].reshape(k, MLA_HEADS * width)


def _rope_tables(positions):
    inv_freq = ROPE_BASE ** (-jnp.arange(0, QK_ROPE, 2, dtype=F32) / QK_ROPE)
    ang = positions.astype(F32)[:, None] * inv_freq
    cos, sin = jnp.cos(ang), jnp.sin(ang)
    S = positions.shape[0]
    ones, zeros = jnp.ones((S, QK_NOPE), F32), jnp.zeros((S, QK_NOPE), F32)
    ctab = jnp.concatenate([ones, cos, cos, ones[:, :HEAD_PAD - QK_DIM]], axis=1)
    stab = jnp.concatenate([zeros, -sin, sin, zeros[:, :HEAD_PAD - QK_DIM]], axis=1)
    return ctab, stab


def _memory_block(x, mem, W, layer, tag):
    hx = _rms(x, W["xa_norm_x"][layer], name=f"{tag}_xa_norm")
    qx = _mm(hx, [(W["xa_w_q"][layer], 0, 0)], _first, [(D_MODEL, BF16, 0)], tn=D_MODEL, nj=1, name=f"{tag}_xa_q")[0]
    mn = _rms(mem, W["xa_norm_mem"][layer], name=f"{tag}_xa_norm_mem")
    kvm = _mm(mn, [(W["xa_w_kv"][layer], 0, 0)], _first, [(2 * D_MODEL, BF16, 0)], tn=2 * D_MODEL, nj=1,
              name=f"{tag}_xa_kv")[0]
    o = _xattn_fwd(qx, kvm, name=f"{tag}_xa_attn")
    xo = _mm(o, [(W["xa_w_o"][layer], 0, 0)], _add_res, [(D_MODEL, F32, 0)], extras=[(x, 0)], tn=D_MODEL, nj=1,
             name=f"{tag}_xa_out")[0]
    return xo, (x, hx, qx, mn, kvm, o)


def _memory_block_bwd(dxo, mem, W, layer, saved, tag, grads):
    x, hx, qx, mn, kvm, o = saved
    wq, wkv, wo = W["xa_w_q"][layer], W["xa_w_kv"][layer], W["xa_w_o"][layer]
    do = _mm(dxo, [(wo, 0, 0)], _first, [(D_MODEL, BF16, 0)], nt=True, tn=D_MODEL, nj=1, name=f"{tag}_xa_do")[0]
    grads["xa_w_o"][layer] = _mm_tn(o, dxo, name=f"{tag}_xa_dwo")
    dqx, dkvm = _xattn_bwd(qx, kvm, do, name=f"{tag}_xa_attn_bwd")
    dhx = _mm(dqx, [(wq, 0, 0)], _first, [(D_MODEL, F32, 0)], nt=True, tn=D_MODEL, nj=1, name=f"{tag}_xa_dhx")[0]
    grads["xa_w_q"][layer] = _mm_tn(hx, dqx, name=f"{tag}_xa_dwq")
    dx, dg = _rms_bwd(x, W["xa_norm_x"][layer], dhx, res=dxo, name=f"{tag}_xa_norm_bwd")
    grads["xa_norm_x"][layer] = dg[0]
    dmn = _mm(dkvm, [(wkv, 0, 0)], _first, [(D_MODEL, F32, 0)], nt=True, tn=D_MODEL, nj=1, name=f"{tag}_xa_dmn")[0]
    grads["xa_w_kv"][layer] = _mm_tn(mn, dkvm, tn=D_MODEL, name=f"{tag}_xa_dwkv")
    _, dgm = _rms_bwd(mem, W["xa_norm_mem"][layer], dmn, name=f"{tag}_xa_norm_mem_bwd")
    grads["xa_norm_mem"][layer] = dgm[0]
    return dx


FF_TN = D_FF // 2


def _silu_mul(accs, extras):
    g, u = accs
    return [g * _sigmoid(g) * u, g, u]


def _silu_mul_bwd(accs, extras):
    da = accs[0]
    g, u = extras[0].astype(F32), extras[1].astype(F32)
    sg = _sigmoid(g)
    return [da * u * sg * (1.0 + g * (1.0 - sg)), da * g * sg]


def _ffn_block(x, W, layer, tag):
    hf = _rms(x, W["ffn_norm"][layer], name=f"{tag}_ffn_norm")
    wgu, wd = W["ffn_w_gate_up"][layer], W["ffn_w_down"][layer]
    act, g, u = _mm(hf, [(wgu, 0, 0), (wgu, 0, 2)], _silu_mul, [(D_FF, BF16, 0)] * 3, tn=FF_TN, nj=2,
                    name=f"{tag}_ffn_up")
    xo = _mm(act, [(wd, 0, 0)], _add_res, [(D_MODEL, F32, 0)], extras=[(x, 0)], tn=D_MODEL, nj=1,
             name=f"{tag}_ffn_down")[0]
    return xo, (x, hf, act, g, u)


def _ffn_block_bwd(dxo, W, layer, saved, tag, grads):
    x, hf, act, g, u = saved
    wgu, wd = W["ffn_w_gate_up"][layer], W["ffn_w_down"][layer]
    dg, du = _mm(dxo, [(wd, 0, 0)], _silu_mul_bwd, [(D_FF, BF16, 0)] * 2, nt=True, extras=[(g, 0), (u, 0)], tn=FF_TN,
                 nj=2, name=f"{tag}_ffn_dact")
    grads["ffn_w_down"][layer] = _mm_tn(act, dxo, tk=FF_TN, name=f"{tag}_ffn_dwd")
    dhf = _mm(dg, [(wgu, 0, 0)], _first, [(D_MODEL, F32, 0)], nt=True, tn=D_MODEL, nj=1, name=f"{tag}_ffn_dhf_g")[0]
    dhf = _mm(du, [(wgu, 0, 1)], _add_res, [(D_MODEL, F32, 0)], nt=True, extras=[(dhf, 0)], tn=D_MODEL, nj=1,
              name=f"{tag}_ffn_dhf_u")[0]
    grads["ffn_w_gate_up"][layer] = jnp.concatenate(
        [_mm_tn(hf, dg, tn=FF_TN, name=f"{tag}_ffn_dwg"), _mm_tn(hf, du, tn=FF_TN, name=f"{tag}_ffn_dwu")], axis=1)
    dx, dgn = _rms_bwd(x, W["ffn_norm"][layer], dhf, res=dxo, name=f"{tag}_ffn_norm_bwd")
    grads["ffn_norm"][layer] = dgn[0]
    return dx


def _even_block(x, tabs, W, tag):
    ctab, stab = tabs
    w_in = W["ev_w_in"][0]
    zero = jnp.zeros((D_MODEL, QK_NOPE), BF16)
    w_in_pad = jnp.concatenate([w_in[:, :896], zero, w_in[:, 896:], zero[:, :HEAD_PAD - QK_DIM]], axis=1)
    w_q_pad = _pad_heads(W["ev_w_q_up"][0], QK_DIM)
    wkv = W["ev_w_kv_up"][0].reshape(KV_RANK, MLA_HEADS, QK_NOPE + V_HEAD)
    w_kv_pad = jnp.concatenate([_pad_heads(wkv[:, :, :QK_NOPE].reshape(KV_RANK, -1), QK_NOPE),
                                _pad_heads(wkv[:, :, QK_NOPE:].reshape(KV_RANK, -1), V_HEAD)], axis=1)
    w_out = W["ev_w_out"][0]
    w_att = jnp.pad(w_out[POOL_DIM:].reshape(MLA_HEADS, V_HEAD, D_MODEL), ((0, 0), (0, HEAD_PAD - V_HEAD), (0, 0)))
    w_out_pad = jnp.concatenate([w_out[:POOL_DIM], w_att.reshape(MLA_HEADS * HEAD_PAD, D_MODEL)], axis=0)
    pool_w = W["ev_pool_w"][0].astype(BF16)
    pool_scale = W["ev_pool_scale"]

    h = _rms(x, W["ev_norm"][0], name=f"{tag}_norm")
    z = _mm(h, [(w_in_pad, 0, 0)], _first, [(D_MODEL, F32, 0)], tn=D_MODEL, nj=1, name=f"{tag}_in")[0]
    y_pool, pooled = _pool_fwd(z, pool_w, pool_scale, name=f"{tag}_pool")
    cqn = _rms(z, W["ev_q_norm"][0], cb=2, w=Q_RANK, name=f"{tag}_q_norm")
    ckvn = _rms(z, W["ev_kv_norm"][0], cb=6, w=KV_RANK, name=f"{tag}_kv_norm")
    q_pad = _mm(cqn, [(w_q_pad, 0, 0)], _first, [(D_MODEL, F32, 0)], tn=D_MODEL, nj=1, name=f"{tag}_q_up")[0]
    k_pad, v_pad = _mm(ckvn, [(w_kv_pad, 0, 0), (w_kv_pad, 0, 1)], lambda a, e: a,
                       [(D_MODEL, F32, 0), (D_MODEL, BF16, 0)], tn=D_MODEL, nj=1, name=f"{tag}_kv_up")
    q_rot, k_cat = _rope_fwd(q_pad, k_pad, z, ctab, stab, name=f"{tag}_rope")
    o, lse = _flash_fwd(q_rot, k_cat, v_pad, name=f"{tag}_attn")
    mix = jnp.concatenate([y_pool, o], axis=1)
    xo = _mm(mix, [(w_out_pad, 0, 0)], _add_res, [(D_MODEL, F32, 0)], extras=[(x, 0)], tn=D_MODEL, nj=1,
             name=f"{tag}_out")[0]
    saved = (x, h, z, pooled, cqn, ckvn, q_rot, k_cat, v_pad, o, lse, mix,
             (w_in_pad, w_q_pad, w_kv_pad, w_out_pad, pool_w, pool_scale))
    return xo, saved


def _even_block_bwd(dxo, tabs, W, saved, tag, grads):
    ctab, stab = tabs
    x, h, z, pooled, cqn, ckvn, q_rot, k_cat, v_pad, o, lse, mix, wts = saved
    w_in_pad, w_q_pad, w_kv_pad, w_out_pad, pool_w, pool_scale = wts
    S = x.shape[0]
    tq = min(S, 512)
    mixw = POOL_DIM + MLA_HEADS * HEAD_PAD
    dmix = _mm(dxo, [(w_out_pad, 0, 0)], _first, [(mixw, BF16, 0)], nt=True, tn=mixw, nj=1, name=f"{tag}_dmix")[0]
    dw_out_pad = _mm_tn(mix, dxo, tk=mixw // 3, name=f"{tag}_dw_out")
    datt = dw_out_pad[POOL_DIM:].reshape(MLA_HEADS, HEAD_PAD, D_MODEL)[:, :V_HEAD].reshape(-1, D_MODEL)
    grads["ev_w_out"] = jnp.concatenate([dw_out_pad[:POOL_DIM], datt], axis=0)[None]
    delta = _attn_delta(dmix, o, name=f"{tag}_delta")
    dq_rot, dk_cat, dv_pad = _flash_bwd(q_rot, k_cat, v_pad, dmix, _as_rows(lse, tq), _as_rows(delta, tq),
                                        name=f"{tag}_attn_bwd")
    dq_pad, dkr = _rope_bwd(dq_rot, dk_cat, ctab, stab, name=f"{tag}_rope_bwd")
    dw_q_pad = _mm_tn(cqn, dq_pad, name=f"{tag}_dw_q_up")
    grads["ev_w_q_up"] = _unpad_heads(dw_q_pad, QK_DIM)[None]
    dcqn = _mm(dq_pad, [(w_q_pad, 0, 0)], _first, [(Q_RANK, F32, 0)], nt=True, tn=Q_RANK, nj=1, name=f"{tag}_dcqn")[0]
    dwk = _unpad_heads(_mm_tn(ckvn, dk_cat, name=f"{tag}_dw_k_up"), QK_NOPE).reshape(KV_RANK, MLA_HEADS, QK_NOPE)
    dwv = _unpad_heads(_mm_tn(ckvn, dv_pad, name=f"{tag}_dw_v_up"), V_HEAD).reshape(KV_RANK, MLA_HEADS, V_HEAD)
    grads["ev_w_kv_up"] = jnp.concatenate([dwk, dwv], axis=2).reshape(1, KV_RANK, -1)
    dckvn = _mm(dk_cat, [(w_kv_pad, 0, 0)], _first, [(KV_RANK, F32, 0)], nt=True, tn=KV_RANK, nj=1,
                name=f"{tag}_dckvn_k")[0]
    dckvn = _mm(dv_pad, [(w_kv_pad, 0, 1)], _add_res, [(KV_RANK, F32, 0)], nt=True, extras=[(dckvn, 0)], tn=KV_RANK,
                nj=1, name=f"{tag}_dckvn_v")[0]
    dcq, dgq = _rms_bwd(z, W["ev_q_norm"][0], dcqn, cb=2, w=Q_RANK, out_dtype=BF16, name=f"{tag}_q_norm_bwd")
    dckv, dgkv = _rms_bwd(z, W["ev_kv_norm"][0], dckvn, cb=6, w=KV_RANK, out_dtype=BF16, name=f"{tag}_kv_norm_bwd")
    grads["ev_q_norm"], grads["ev_kv_norm"] = dgq, dgkv
    du, dypre, dscale = _pool_bwd(dmix, pooled, pool_w, pool_scale, name=f"{tag}_pool_bwd")
    grads["ev_pool_scale"] = dscale
    grads["ev_pool_w"] = _mm_tn_grouped(pooled, dypre, 4, POOL_GROUP, name=f"{tag}_dpool_w")[None]
    dz = jnp.concatenate([du, dcq, dckv, dkr], axis=1)
    dw_in_pad = _mm_tn(h, dz, name=f"{tag}_dw_in")
    grads["ev_w_in"] = jnp.concatenate([dw_in_pad[:, :896], dw_in_pad[:, 960:992]], axis=1)[None]
    dh = _mm(dz, [(w_in_pad, 0, 0)], _first, [(D_MODEL, F32, 0)], nt=True, tn=D_MODEL, nj=1, name=f"{tag}_dh")[0]
    dx, dgn = _rms_bwd(x, W["ev_norm"][0], dh, res=dxo, name=f"{tag}_norm_bwd")
    grads["ev_norm"] = dgn
    return dx


def _odd_block(x, reset, W, tag):
    h = _rms(x, W["od_norm"][0], name=f"{tag}_norm")
    z = _mm(h, [(W["od_w_in"][0], 0, 0)], _first, [(2 * D_MODEL, F32, 0)], tn=D_MODEL, nj=2, name=f"{tag}_in")[0]
    w_r, w_i = W["od_w_rgate"][0], W["od_w_igate"][0]
    vecs = [W[n].reshape(1, D_MODEL) for n in ("od_conv_b", "od_b_rgate", "od_b_igate", "od_lambda")]
    xc, hseq, y = _lru_fwd(z, reset, W["od_conv_w"][0], vecs[0], w_r, vecs[1], w_i, vecs[2], vecs[3],
                           name=f"{tag}_lru")
    xo = _mm(y, [(W["od_w_out"][0], 0, 0)], _add_res, [(D_MODEL, F32, 0)], extras=[(x, 0)], tn=D_MODEL, nj=1,
             name=f"{tag}_out")[0]
    return xo, (x, h, z, xc, hseq, y, vecs)


def _odd_block_bwd(dxo, reset, W, saved, tag, grads):
    x, h, z, xc, hseq, y, vecs = saved
    w_r, w_i = W["od_w_rgate"][0], W["od_w_igate"][0]
    dy = _mm(dxo, [(W["od_w_out"][0], 0, 0)], _first, [(D_MODEL, F32, 0)], nt=True, tn=D_MODEL, nj=1,
             name=f"{tag}_dy")[0]
    grads["od_w_out"] = _mm_tn(y, dxo, name=f"{tag}_dw_out")[None]
    dgate, dxc, dpr, dpi, acc = _lru_bwd(dy, z, xc, hseq, reset, w_r, vecs[1], w_i, vecs[2], vecs[3],
                                         name=f"{tag}_lru_bwd")
    grads["od_b_rgate"], grads["od_b_igate"], grads["od_lambda"] = acc[0:1], acc[1:2], acc[2:3]
    grads["od_w_rgate"] = _mm_tn_grouped(xc, dpr, LRU_HEADS, LRU_HEAD_DIM, name=f"{tag}_dw_rgate")[None]
    grads["od_w_igate"] = _mm_tn_grouped(xc, dpi, LRU_HEADS, LRU_HEAD_DIM, name=f"{tag}_dw_igate")[None]
    dxb, cacc = _conv_bwd(dxc, z, W["od_conv_w"][0], name=f"{tag}_conv_bwd")
    grads["od_conv_w"], grads["od_conv_b"] = cacc[None, 0:4], cacc[4:5]
    dz = jnp.concatenate([dgate, dxb], axis=1)
    grads["od_w_in"] = _mm_tn(h, dz, tn=D_MODEL, name=f"{tag}_dw_in")[None]
    dh = _mm(dz, [(W["od_w_in"][0], 0, 0)], _first, [(D_MODEL, F32, 0)], nt=True, tn=D_MODEL, nj=1,
             name=f"{tag}_dh")[0]
    dx, dgn = _rms_bwd(x, W["od_norm"][0], dh, res=dxo, name=f"{tag}_norm_bwd")
    grads["od_norm"] = dgn
    return dx


def _local_step(x, mem, positions, target, W):
    tabs = _rope_tables(positions)
    reset = (positions == 0).astype(F32)[:, None]
    grads = {n: [None, None] for n in ("xa_norm_x", "xa_norm_mem", "xa_w_q", "xa_w_kv", "xa_w_o", "ffn_norm",
                                       "ffn_w_gate_up", "ffn_w_down")}
    x1, s_even = _even_block(x, tabs, W, "l0_even")
    x2, s_xa0 = _memory_block(x1, mem, W, 0, "l0")
    x3, s_ff0 = _ffn_block(x2, W, 0, "l0")
    x4, s_odd = _odd_block(x3, reset, W, "l1_odd")
    x5, s_xa1 = _memory_block(x4, mem, W, 1, "l1")
    x6, s_ff1 = _ffn_block(x5, W, 1, "l1")
    d, dgf, loss = _loss_head(x6, W["final_norm"], target, name="loss_head")
    grads["final_norm"] = dgf[0]
    d = _ffn_block_bwd(d, W, 1, s_ff1, "l1", grads)
    d = _memory_block_bwd(d, mem, W, 1, s_xa1, "l1", grads)
    d = _odd_block_bwd(d, reset, W, s_odd, "l1_odd", grads)
    d = _ffn_block_bwd(d, W, 0, s_ff0, "l0", grads)
    d = _memory_block_bwd(d, mem, W, 0, s_xa0, "l0", grads)
    d = _even_block_bwd(d, tabs, W, s_even, "l0_even", grads)
    for n, v in grads.items():
        if isinstance(v, list):
            grads[n] = jnp.stack(v)
    return loss[0, 0], d, grads


WEIGHTS = ("ev_norm", "ev_w_in", "ev_pool_w", "ev_pool_scale", "ev_q_norm", "ev_w_q_up", "ev_kv_norm", "ev_w_kv_up",
           "ev_w_out", "od_norm", "od_w_in", "od_conv_w", "od_conv_b", "od_w_rgate", "od_b_rgate", "od_w_igate",
           "od_b_igate", "od_lambda", "od_w_out", "xa_norm_x", "xa_norm_mem", "xa_w_q", "xa_w_kv", "xa_w_o",
           "ffn_norm", "ffn_w_gate_up", "ffn_w_down", "final_norm")
SHARD_AXIS = {"ev_w_in": 1, "ev_w_q_up": 2, "ev_w_kv_up": 2, "ev_w_out": 1, "od_norm": 1, "od_w_in": 2,
              "od_conv_w": 2, "od_conv_b": 1, "od_w_rgate": 2, "od_b_rgate": 1, "od_w_igate": 2, "od_b_igate": 1,
              "od_lambda": 1, "od_w_out": 1, "xa_w_q": 1, "xa_w_kv": 2, "xa_w_o": 1, "ffn_w_gate_up": 2,
              "ffn_w_down": 1}
MATMUL_WEIGHTS = ("ev_w_in", "ev_w_q_up", "ev_w_kv_up", "ev_w_out", "od_w_in", "od_w_rgate", "od_w_igate",
                  "od_w_out", "xa_w_q", "xa_w_kv", "xa_w_o", "ffn_w_gate_up", "ffn_w_down")
SHARDED = tuple(n for n in WEIGHTS if n in SHARD_AXIS)
REPLICATED = tuple(n for n in WEIGHTS if n not in SHARD_AXIS)
ANY = pl.BlockSpec(memory_space=pl.ANY)


def _pack(parts, quantum):
    flat = jnp.concatenate([p.reshape(-1) for p in parts])
    pad = (-flat.shape[0]) % quantum
    return jnp.pad(flat, (0, pad)).reshape(-1, LANES)


def _other_chips(x, y):
    return [(1 - x, y), (x, 1 - y), (1 - x, 1 - y)]


def _all_gather_chips(p, *, name):
    R = p.shape[0]

    def body(p_ref, out_ref, send_sems, recv_sems, local_sem):
        x, y, c = lax.axis_index("x"), lax.axis_index("y"), lax.axis_index("c")
        mine = pltpu.make_async_copy(p_ref, out_ref.at[2 * x + y], local_sem)
        mine.start()
        sends = []
        for k, (px, py) in enumerate(_other_chips(x, y)):
            cp = pltpu.make_async_remote_copy(src_ref=p_ref, dst_ref=out_ref.at[2 * x + y], send_sem=send_sems.at[k],
                                              recv_sem=recv_sems.at[k], device_id=(px, py, c), device_id_type=MESH)
            cp.start()
            sends.append(cp)
        for k, (px, py) in enumerate(_other_chips(x, y)):
            pltpu.make_async_remote_copy(src_ref=p_ref, dst_ref=out_ref.at[2 * px + py], send_sem=send_sems.at[k],
                                         recv_sem=recv_sems.at[k], device_id=(px, py, c),
                                         device_id_type=MESH).wait_recv()
        for cp in sends:
            cp.wait_send()
        mine.wait()

    return pl.pallas_call(
        body, in_specs=[ANY], out_specs=ANY, out_shape=jax.ShapeDtypeStruct((N_CHIPS, R, LANES), p.dtype),
        scratch_shapes=[pltpu.SemaphoreType.DMA((3,)), pltpu.SemaphoreType.DMA((3,)), pltpu.SemaphoreType.DMA],
        name=name,
    )(p)


def _exchange_chips(gp, *, name):
    R = gp.shape[1]

    def body(g_ref, out_ref, send_sems, recv_sems, local_sem):
        x, y, c = lax.axis_index("x"), lax.axis_index("y"), lax.axis_index("c")
        me = 2 * x + y
        mine = pltpu.make_async_copy(g_ref.at[me], out_ref.at[me], local_sem)
        mine.start()
        sends = []
        for k, (px, py) in enumerate(_other_chips(x, y)):
            cp = pltpu.make_async_remote_copy(src_ref=g_ref.at[2 * px + py], dst_ref=out_ref.at[me],
                                              send_sem=send_sems.at[k], recv_sem=recv_sems.at[k],
                                              device_id=(px, py, c), device_id_type=MESH)
            cp.start()
            sends.append(cp)
        for k, (px, py) in enumerate(_other_chips(x, y)):
            pltpu.make_async_remote_copy(src_ref=g_ref.at[me], dst_ref=out_ref.at[2 * px + py],
                                         send_sem=send_sems.at[k], recv_sem=recv_sems.at[k], device_id=(px, py, c),
                                         device_id_type=MESH).wait_recv()
        for cp in sends:
            cp.wait_send()
        mine.wait()

    return pl.pallas_call(
        body, in_specs=[ANY], out_specs=ANY, out_shape=jax.ShapeDtypeStruct((N_CHIPS, R, LANES), gp.dtype),
        scratch_shapes=[pltpu.SemaphoreType.DMA((3,)), pltpu.SemaphoreType.DMA((3,)), pltpu.SemaphoreType.DMA],
        name=name,
    )(gp)


def _exchange_sibling(s, *, name):
    def body(s_ref, out_ref, send_sem, recv_sem):
        x, y, c = lax.axis_index("x"), lax.axis_index("y"), lax.axis_index("c")
        cp = pltpu.make_async_remote_copy(src_ref=s_ref, dst_ref=out_ref, send_sem=send_sem, recv_sem=recv_sem,
                                          device_id=(x, y, 1 - c), device_id_type=MESH)
        cp.start()
        cp.wait()

    return pl.pallas_call(
        body, in_specs=[ANY], out_specs=ANY, out_shape=jax.ShapeDtypeStruct(s.shape, s.dtype),
        scratch_shapes=[pltpu.SemaphoreType.DMA, pltpu.SemaphoreType.DMA],
        name=name,
    )(s)


def _row_tile(rows):
    return max(d for d in range(8, 2049, 8) if rows % d == 0)


def _sum_slots(r, *, name):
    R = r.shape[1]
    tr = _row_tile(R)

    def body(r_ref, o_ref):
        o_ref[...] = ((r_ref[0] + r_ref[1]) + r_ref[2]) + r_ref[3]

    return pl.pallas_call(
        body, grid=(R // tr,), in_specs=[pl.BlockSpec((N_CHIPS, tr, LANES), lambda i: (0, i, 0))],
        out_specs=_rows(tr, LANES), out_shape=jax.ShapeDtypeStruct((R, LANES), F32), compiler_params=_cp(1), name=name,
    )(r)


def _sum_pair(a, b, *, name):
    R = a.shape[0]
    tr = _row_tile(R)

    def body(a_ref, b_ref, o_ref):
        o_ref[...] = a_ref[...] + b_ref[...]

    return pl.pallas_call(
        body, grid=(R // tr,), in_specs=[_rows(tr, LANES)] * 2, out_specs=_rows(tr, LANES),
        out_shape=jax.ShapeDtypeStruct((R, LANES), F32), compiler_params=_cp(1), name=name,
    )(a, b)


def _gather_weights(w):
    parts = []
    for n in SHARDED:
        if n in MATMUL_WEIGHTS:
            parts.append(w[n].astype(BF16))
        else:
            parts.append(lax.bitcast_convert_type(w[n].reshape(-1), BF16))
    g = _all_gather_chips(_pack(parts, 16 * LANES), name="gather_weights").reshape(N_CHIPS, -1)
    full = {n: w[n] for n in REPLICATED}
    off = 0
    for n in SHARDED:
        shape = w[n].shape
        if n in MATMUL_WEIGHTS:
            seg = g[:, off:off + w[n].size].reshape(N_CHIPS, *shape)
            off += w[n].size
        else:
            seg = lax.bitcast_convert_type(g[:, off:off + 2 * w[n].size].reshape(N_CHIPS, -1, 2), F32)
            seg = seg.reshape(N_CHIPS, *shape)
            off += 2 * w[n].size
        full[n] = jnp.concatenate([seg[q] for q in range(N_CHIPS)], axis=SHARD_AXIS[n])
    return full


def _reduce_grads(grads, shard_shapes):
    slots = []
    for q in range(N_CHIPS):
        parts = [jnp.split(grads[n], N_CHIPS, axis=SHARD_AXIS[n])[q] for n in SHARDED]
        parts += [grads[n] for n in REPLICATED]
        slots.append(_pack(parts, 8 * LANES))
    mine = _sum_slots(_exchange_chips(jnp.stack(slots), name="exchange_grads"), name="sum_chips")
    total = _sum_pair(mine, _exchange_sibling(mine, name="exchange_sibling"), name="sum_cores").reshape(-1)
    out, off = {}, 0
    for n in SHARDED + REPLICATED:
        size = math.prod(shard_shapes[n])
        out[n] = total[off:off + size].reshape(shard_shapes[n])
        off += size
    return out


def kernel(
        x, mem, positions, ev_norm, ev_w_in, ev_pool_w, ev_pool_scale, ev_q_norm, ev_w_q_up, ev_kv_norm,
        ev_w_kv_up, ev_w_out, od_norm, od_w_in, od_conv_w, od_conv_b, od_w_rgate, od_b_rgate, od_w_igate,
        od_b_igate, od_lambda, od_w_out, xa_norm_x, xa_norm_mem, xa_w_q, xa_w_kv, xa_w_o, ffn_norm,
        ffn_w_gate_up, ffn_w_down, final_norm, loss_target, m_ev_norm, m_ev_w_in, m_ev_pool_w, m_ev_pool_scale,
        m_ev_q_norm, m_ev_w_q_up, m_ev_kv_norm, m_ev_w_kv_up, m_ev_w_out, m_od_norm, m_od_w_in, m_od_conv_w,
        m_od_conv_b, m_od_w_rgate, m_od_b_rgate, m_od_w_igate, m_od_b_igate, m_od_lambda, m_od_w_out,
        m_xa_norm_x, m_xa_norm_mem, m_xa_w_q, m_xa_w_kv, m_xa_w_o, m_ffn_norm, m_ffn_w_gate_up, m_ffn_w_down,
        m_final_norm, v_ev_norm, v_ev_w_in, v_ev_pool_w, v_ev_pool_scale, v_ev_q_norm, v_ev_w_q_up,
        v_ev_kv_norm, v_ev_w_kv_up, v_ev_w_out, v_od_norm, v_od_w_in, v_od_conv_w, v_od_conv_b, v_od_w_rgate,
        v_od_b_rgate, v_od_w_igate, v_od_b_igate, v_od_lambda, v_od_w_out, v_xa_norm_x, v_xa_norm_mem, v_xa_w_q,
        v_xa_w_kv, v_xa_w_o, v_ffn_norm, v_ffn_w_gate_up, v_ffn_w_down, v_final_norm):
    given = dict(locals())
    w = {n: given[n] for n in WEIGHTS}
    full = _gather_weights(w)
    loss, grad_x, grads = _local_step(x[0], mem[0], positions[0], loss_target[0], full)
    grads = {n: grads[n].reshape(full[n].shape) for n in WEIGHTS}
    g = _reduce_grads(grads, {n: w[n].shape for n in WEIGHTS})
    loss = lax.psum(loss, ("x", "y", "c"))
    deltas, new_m, new_v = [], [], []
    for n in WEIGHTS:
        d, m, v = _adamw(w[n], g[n], given["m_" + n], given["v_" + n], name=f"adamw_{n}")
        deltas.append(d)
        new_m.append(m)
        new_v.append(v)
    return (loss, grad_x[None], *[g[n] for n in WEIGHTS], *deltas, *new_m, *new_v)
```

```python
import functools
import math

import jax
import jax.numpy as jnp
from jax import lax
from jax.experimental import pallas as pl
from jax.experimental.pallas import tpu as pltpu

F32 = jnp.float32
BF16 = jnp.bfloat16

D_MODEL = 1024
POOL_DIM = 512
POOL_WINDOWS = (2, 4, 8, 16)
POOL_GROUP = 128
MLA_HEADS = 8
QK_NOPE = 64
QK_ROPE = 32
QK_DIM = QK_NOPE + QK_ROPE
V_HEAD = 64
HEAD_PAD = 128
Q_RANK = 256
KV_RANK = 128
ROPE_BASE = 10000.0
LRU_HEADS = 4
LRU_HEAD_DIM = 256
CONV_WIDTH = 4
LRU_C = 8.0
MEM_HEADS = 4
MEM_HEAD_DIM = 256
D_FF = 2816
RMS_EPS = 1e-6
NEG_INF = -1e30

ADAM_LR = 0.001
ADAM_B1 = 0.9
ADAM_B2 = 0.999
ADAM_EPS = 1e-08
ADAM_WD = 0.01
ADAM_STEP = 10

N_CHIPS = 4
LANES = 128
VMEM_LIMIT = 56 * 1024 * 1024
MESH = pl.DeviceIdType.MESH
ANY = pl.BlockSpec(memory_space=pl.ANY)
MIX_DIM = POOL_DIM + MLA_HEADS * HEAD_PAD

NN = (((1,), (0,)), ((), ()))
NT = (((1,), (1,)), ((), ()))
TN = (((0,), (0,)), ((), ()))


def _cp(n):
    return pltpu.CompilerParams(dimension_semantics=("arbitrary",) * n, vmem_limit_bytes=VMEM_LIMIT)


def _dot(a, b, dims=NN):
    return lax.dot_general(a, b, dims, preferred_element_type=F32)


def _rows(ts, w, cb=0):
    return pl.BlockSpec((ts, w), lambda i: (i, cb))


def _const(shape):
    return pl.BlockSpec(shape, lambda i: (0,) * len(shape))


def _mm(a, bs, epi, outs, *, tn, nj, nt=False, extras=(), a_cb=0, k=None, tm=None, name):
    M = a.shape[0]
    k = k or a.shape[1]
    tm = tm or min(M, 512)
    nb, ne = len(bs), len(extras)
    dims = NT if nt else NN

    def body(*refs):
        av = refs[0][...].astype(BF16)
        accs = [_dot(av, r[...].astype(BF16), dims) for r in refs[1:1 + nb]]
        vals = epi(accs, [r[...] for r in refs[1 + nb:1 + nb + ne]])
        for o, v in zip(refs[1 + nb + ne:], vals):
            o[...] = v.astype(o.dtype)

    in_specs = [pl.BlockSpec((tm, k), lambda j, i: (i, a_cb))]
    for (_, rb, cb) in bs:
        if nt:
            in_specs.append(pl.BlockSpec((tn, k), lambda j, i, rb=rb, cb=cb: (rb + j, cb)))
        else:
            in_specs.append(pl.BlockSpec((k, tn), lambda j, i, rb=rb, cb=cb: (rb, cb + j)))
    for (_, cb) in extras:
        in_specs.append(pl.BlockSpec((tm, tn), lambda j, i, cb=cb: (i, cb + j)))
    out_specs = [pl.BlockSpec((tm, tn), lambda j, i, cb=cb: (i, cb + j)) for (_, _, cb) in outs]
    res = pl.pallas_call(
        body, grid=(nj, M // tm), in_specs=in_specs, out_specs=out_specs,
        out_shape=[jax.ShapeDtypeStruct((M, n), dt) for (n, dt, _) in outs],
        compiler_params=_cp(2), name=name,
    )(a, *[b for (b, _, _) in bs], *[e for (e, _) in extras])
    return res


def _first(accs, extras):
    return [accs[0]]


def _add_res(accs, extras):
    return [accs[0] + extras[0].astype(F32)]


def _mm_tn(a, b, *, ka=None, a_cb=0, nb=None, b_cb=0, tk=None, tn=None, ts=None, name):
    S = a.shape[0]
    ka = ka or a.shape[1]
    nb = nb or b.shape[1]
    tk = tk or ka
    tn = tn or nb
    ts = ts or min(S, 512)
    a0, b0 = a_cb * (ka // tk), b_cb * (nb // tn)

    def body(a_ref, b_ref, o_ref):
        @pl.when(pl.program_id(2) == 0)
        def _():
            o_ref[...] = jnp.zeros_like(o_ref)

        o_ref[...] += _dot(a_ref[...].astype(BF16), b_ref[...].astype(BF16), TN)

    return pl.pallas_call(
        body, grid=(ka // tk, nb // tn, S // ts),
        in_specs=[pl.BlockSpec((ts, tk), lambda p, q, s: (s, a0 + p)),
                  pl.BlockSpec((ts, tn), lambda p, q, s: (s, b0 + q))],
        out_specs=pl.BlockSpec((tk, tn), lambda p, q, s: (p, q)),
        out_shape=jax.ShapeDtypeStruct((ka, nb), F32), compiler_params=_cp(3), name=name,
    )(a, b)


def _mm_tn_owners(a, bs, *, name):
    S, ka = a.shape
    nb = sum(b.shape[1] for b in bs)
    tn = nb // N_CHIPS
    ts = min(S, 512)
    per = N_CHIPS // len(bs)

    def body(a_ref, *refs):
        o_ref = refs[-1]
        q = pl.program_id(0)

        @pl.when(pl.program_id(1) == 0)
        def _():
            o_ref[...] = jnp.zeros_like(o_ref)

        av = a_ref[...].astype(BF16)
        for n, b_ref in enumerate(refs[:-1]):
            @pl.when(q // per == n)
            def _():
                o_ref[0] += _dot(av, b_ref[...].astype(BF16), TN)

    in_specs = [pl.BlockSpec((ts, ka), lambda q, s: (s, 0))]
    for n in range(len(bs)):
        in_specs.append(pl.BlockSpec((ts, tn), lambda q, s, n=n: (jnp.where(q // per == n, s, 0),
                                                                  jnp.clip(q - n * per, 0, per - 1))))
    return pl.pallas_call(
        body, grid=(N_CHIPS, S // ts), in_specs=in_specs,
        out_specs=pl.BlockSpec((1, ka, tn), lambda q, s: (q, 0, 0)),
        out_shape=jax.ShapeDtypeStruct((N_CHIPS, ka, tn), F32), compiler_params=_cp(2), name=name,
    )(a, *bs)


def _mm_tn_grouped(a, b, groups, w, *, name):
    S = a.shape[0]
    ts = min(S, 512)

    def body(a_ref, b_ref, o_ref):
        @pl.when(pl.program_id(1) == 0)
        def _():
            o_ref[...] = jnp.zeros_like(o_ref)

        o_ref[0] += _dot(a_ref[...].astype(BF16), b_ref[...].astype(BF16), TN)

    return pl.pallas_call(
        body, grid=(groups, S // ts),
        in_specs=[pl.BlockSpec((ts, w), lambda g, s: (s, g)), pl.BlockSpec((ts, w), lambda g, s: (s, g))],
        out_specs=pl.BlockSpec((1, w, w), lambda g, s: (g, 0, 0)),
        out_shape=jax.ShapeDtypeStruct((groups, w, w), F32), compiler_params=_cp(2), name=name,
    )(a, b)


def _rms(x, g, *, cb=0, w=None, ts=None, name):
    S = x.shape[0]
    w = w or x.shape[1]
    ts = ts or min(S, 512)

    def body(x_ref, g_ref, o_ref):
        xv = x_ref[...].astype(F32)
        r = lax.rsqrt(jnp.mean(xv * xv, axis=-1, keepdims=True) + RMS_EPS)
        o_ref[...] = (xv * r * g_ref[...]).astype(o_ref.dtype)

    return pl.pallas_call(
        body, grid=(S // ts,), in_specs=[_rows(ts, w, cb), _const((1, w))], out_specs=_rows(ts, w),
        out_shape=jax.ShapeDtypeStruct((S, w), BF16), compiler_params=_cp(1), name=name,
    )(x, g.reshape(1, w))


def _rms_bwd(x, g, dy, *, cb=0, w=None, res=None, out_dtype=F32, ts=None, name):
    S = x.shape[0]
    w = w or x.shape[1]
    ts = ts or min(S, 512)
    has_res = res is not None

    def body(*refs):
        x_ref, g_ref, dy_ref = refs[:3]
        dx_ref, dg_ref = refs[-2:]
        xv = x_ref[...].astype(F32)
        r = lax.rsqrt(jnp.mean(xv * xv, axis=-1, keepdims=True) + RMS_EPS)
        n = xv * r
        dyv = dy_ref[...].astype(F32)
        dn = dyv * g_ref[...]
        dx = r * (dn - n * jnp.mean(dn * n, axis=-1, keepdims=True))
        if has_res:
            dx = dx + refs[3][...].astype(F32)
        dx_ref[...] = dx.astype(dx_ref.dtype)

        @pl.when(pl.program_id(0) == 0)
        def _():
            dg_ref[...] = jnp.zeros_like(dg_ref)

        dg_ref[...] += jnp.sum(dyv * n, axis=0, keepdims=True)

    ins = [x, g.reshape(1, w), dy] + ([res] if has_res else [])
    in_specs = [_rows(ts, w, cb), _const((1, w)), _rows(ts, w)] + ([_rows(ts, w)] if has_res else [])
    return pl.pallas_call(
        body, grid=(S // ts,), in_specs=in_specs, out_specs=[_rows(ts, w), _const((1, w))],
        out_shape=[jax.ShapeDtypeStruct((S, w), out_dtype), jax.ShapeDtypeStruct((1, w), F32)],
        compiler_params=_cp(1), name=name,
    )(*ins)


HALO = 16


def _pool_counts(i, ts, rows, first_row):
    t = i * ts + first_row + lax.broadcasted_iota(jnp.int32, (rows, 1), 0)
    return [jnp.minimum(t + 1, w).astype(F32) for w in POOL_WINDOWS]


def _pool_fwd(z, pool_w, pool_scale, *, name):
    S = z.shape[0]
    ts = min(S, 512)
    nh = ts // HALO

    def body(u_ref, halo_ref, w_ref, sc_ref, y_ref, p_ref):
        i = pl.program_id(0)
        u = u_ref[...]
        halo = jnp.where(i > 0, halo_ref[...], 0.0)
        xe = jnp.concatenate([halo, u], axis=0)
        sums = []
        s = xe
        for sh in (1, 2, 4, 8):
            s = s + pltpu.roll(s, sh, 0)
            sums.append(s)
        cnts = _pool_counts(i, ts, ts, 0)
        for g in range(4):
            lo, hi = g * POOL_GROUP, (g + 1) * POOL_GROUP
            pooled = (sums[g][HALO:, lo:hi] / cnts[g] - u[:, lo:hi]).astype(BF16)
            p_ref[:, lo:hi] = pooled
            y_ref[:, lo:hi] = (_dot(pooled, w_ref[g]) * sc_ref[:, lo:hi]).astype(y_ref.dtype)

    return pl.pallas_call(
        body, grid=(S // ts,),
        in_specs=[_rows(ts, POOL_DIM), pl.BlockSpec((HALO, POOL_DIM), lambda i: (jnp.maximum(i * nh - 1, 0), 0)),
                  _const((4, POOL_GROUP, POOL_GROUP)), _const((1, POOL_DIM))],
        out_specs=[_rows(ts, POOL_DIM), _rows(ts, POOL_DIM)],
        out_shape=[jax.ShapeDtypeStruct((S, MIX_DIM), BF16), jax.ShapeDtypeStruct((S, POOL_DIM), BF16)],
        compiler_params=_cp(1), name=name,
    )(z, z, pool_w, pool_scale)


def _pool_bwd(dmix, pooled, pool_w, pool_scale, *, name):
    S = dmix.shape[0]
    ts = min(S, 512)
    nh = ts // HALO
    last = S // HALO - 1

    def body(dy_ref, dyh_ref, p_ref, w_ref, sc_ref, du_ref, dyp_ref, dsc_ref):
        i = pl.program_id(0)
        dyv = dy_ref[...].astype(F32)
        dyh = jnp.where(i < pl.num_programs(0) - 1, dyh_ref[...].astype(F32), 0.0)
        dye = jnp.concatenate([dyv, dyh], axis=0) * sc_ref[...]
        dypre = dye.astype(BF16)
        dyp_ref[...] = dypre[:ts]
        cnts = _pool_counts(i, ts, ts + HALO, 0)
        n = ts + HALO
        dsc = []
        for g in range(4):
            lo, hi = g * POOL_GROUP, (g + 1) * POOL_GROUP
            ypre = _dot(p_ref[:, lo:hi], w_ref[g])
            dsc.append(jnp.sum(dyv[:, lo:hi] * ypre, axis=0, keepdims=True))
            dpool = _dot(dypre[:, lo:hi], w_ref[g], NT)
            s = dpool / cnts[g]
            for sh in (1, 2, 4, 8)[:g + 1]:
                s = s + pltpu.roll(s, n - sh, 0)
            du_ref[:, lo:hi] = (s[:ts] - dpool[:ts]).astype(du_ref.dtype)

        @pl.when(i == 0)
        def _():
            dsc_ref[...] = jnp.zeros_like(dsc_ref)

        dsc_ref[...] += jnp.concatenate(dsc, axis=1)

    return pl.pallas_call(
        body, grid=(S // ts,),
        in_specs=[_rows(ts, POOL_DIM),
                  pl.BlockSpec((HALO, POOL_DIM), lambda i: (jnp.minimum((i + 1) * nh, last), 0)),
                  _rows(ts, POOL_DIM), _const((4, POOL_GROUP, POOL_GROUP)), _const((1, POOL_DIM))],
        out_specs=[_rows(ts, POOL_DIM), _rows(ts, POOL_DIM), _const((1, POOL_DIM))],
        out_shape=[jax.ShapeDtypeStruct((S, POOL_DIM), BF16)] * 2 + [jax.ShapeDtypeStruct((1, POOL_DIM), F32)],
        compiler_params=_cp(1), name=name,
    )(dmix, dmix, pooled, pool_w, pool_scale)


def _rope_partner(t):
    lane = lax.broadcasted_iota(jnp.int32, t.shape, 1)
    swapped = jnp.where(lane < QK_NOPE + QK_ROPE // 2, pltpu.roll(t, HEAD_PAD - QK_ROPE // 2, 1),
                        pltpu.roll(t, QK_ROPE // 2, 1))
    return jnp.where((lane >= QK_NOPE) & (lane < QK_DIM), swapped, 0.0)


def _rope_fwd(q_pad, k_pad, z, ctab, stab, *, name):
    S = q_pad.shape[0]
    ts = min(S, 512)

    def body(q_ref, k_ref, kr_ref, c_ref, s_ref, qo_ref, ko_ref):
        c, s = c_ref[...], s_ref[...]
        q = q_ref[...]
        qo_ref[...] = (q * c + _rope_partner(q) * s).astype(qo_ref.dtype)
        kr = kr_ref[...]
        ko_ref[...] = (k_ref[...] + kr * c + _rope_partner(kr) * s).astype(ko_ref.dtype)

    blk = pl.BlockSpec((ts, HEAD_PAD), lambda i, h: (i, h))
    tab = pl.BlockSpec((ts, HEAD_PAD), lambda i, h: (i, 0))
    return pl.pallas_call(
        body, grid=(S // ts, MLA_HEADS),
        in_specs=[blk, blk, pl.BlockSpec((ts, HEAD_PAD), lambda i, h: (i, 7)), tab, tab],
        out_specs=[blk, blk], out_shape=[jax.ShapeDtypeStruct((S, MLA_HEADS * HEAD_PAD), BF16)] * 2,
        compiler_params=_cp(2), name=name,
    )(q_pad, k_pad, z, ctab, stab)


def _rope_bwd(dq_rot, dk_cat, ctab, stab, *, name):
    S = dq_rot.shape[0]
    ts = min(S, 512)

    def body(dq_ref, dk_ref, c_ref, s_ref, dqo_ref, dkr_ref):
        c, s = c_ref[...], s_ref[...]
        for h in range(MLA_HEADS):
            g = dq_ref[:, h * HEAD_PAD:(h + 1) * HEAD_PAD]
            dqo_ref[:, h * HEAD_PAD:(h + 1) * HEAD_PAD] = (g * c + _rope_partner(g * s)).astype(dqo_ref.dtype)
        dk = dk_ref[...]
        g = dk[:, :HEAD_PAD]
        for h in range(1, MLA_HEADS):
            g = g + dk[:, h * HEAD_PAD:(h + 1) * HEAD_PAD]
        lane = lax.broadcasted_iota(jnp.int32, g.shape, 1)
        on_rope = (lane >= QK_NOPE) & (lane < QK_DIM)
        dkr_ref[...] = jnp.where(on_rope, g * c + _rope_partner(g * s), 0.0).astype(dkr_ref.dtype)

    wide = _rows(ts, MLA_HEADS * HEAD_PAD)
    return pl.pallas_call(
        body, grid=(S // ts,), in_specs=[wide, wide, _rows(ts, HEAD_PAD), _rows(ts, HEAD_PAD)],
        out_specs=[wide, _rows(ts, HEAD_PAD)],
        out_shape=[jax.ShapeDtypeStruct((S, MLA_HEADS * HEAD_PAD), BF16), jax.ShapeDtypeStruct((S, HEAD_PAD), BF16)],
        compiler_params=_cp(1), name=name,
    )(dq_rot, dk_cat, ctab, stab)


ATT_SCALE = QK_DIM ** -0.5


HEADS_PER_STEP = 2
ATT_COL0 = POOL_DIM // HEAD_PAD


def _stat_rows(col):
    return jnp.broadcast_to(col, (col.shape[0], LANES)).T[0:8]


def _flash_fwd(q, k, v, mix, *, name):
    S = q.shape[0]
    tq = min(S, 512)
    nq = S // tq
    hs = HEADS_PER_STEP
    wide = hs * HEAD_PAD

    def body(q_ref, k_ref, v_ref, mix_ref, o_ref, lse_ref):
        qi = pl.program_id(1)
        qv = [q_ref[:, a * HEAD_PAD:(a + 1) * HEAD_PAD] for a in range(hs)]

        def step(j, carry, masked):
            off = pl.multiple_of(j * tq, tq)
            out = []
            for a in range(hs):
                m, l, acc = carry[a]
                s = _dot(qv[a], k_ref[pl.ds(off, tq), a * HEAD_PAD:(a + 1) * HEAD_PAD], NT) * ATT_SCALE
                if masked:
                    row = lax.broadcasted_iota(jnp.int32, (tq, tq), 0)
                    col = lax.broadcasted_iota(jnp.int32, (tq, tq), 1)
                    s = jnp.where(col <= row, s, NEG_INF)
                m_new = jnp.maximum(m, jnp.max(s, axis=-1, keepdims=True))
                p = jnp.exp(s - m_new)
                alpha = jnp.exp(m - m_new)
                l = alpha * l + jnp.sum(p, axis=-1, keepdims=True)
                acc = alpha * acc + _dot(p.astype(BF16), v_ref[pl.ds(off, tq), a * HEAD_PAD:(a + 1) * HEAD_PAD])
                out.append((m_new, l, acc))
            return tuple(out)

        one = (jnp.full((tq, 1), NEG_INF, F32), jnp.zeros((tq, 1), F32), jnp.zeros((tq, HEAD_PAD), F32))
        carry = lax.fori_loop(0, qi, lambda j, c: step(j, c, False), (one,) * hs)
        carry = step(qi, carry, True)
        for a in range(hs):
            m, l, acc = carry[a]
            o_ref[:, a * HEAD_PAD:(a + 1) * HEAD_PAD] = (acc / l).astype(o_ref.dtype)
            lse_ref[a] = _stat_rows(m + jnp.log(l))

    blk = pl.BlockSpec((tq, wide), lambda h, i: (i, h))
    full = pl.BlockSpec((S, wide), lambda h, i: (0, h))
    return pl.pallas_call(
        body, grid=(MLA_HEADS // hs, nq), in_specs=[blk, full, full, ANY],
        out_specs=[pl.BlockSpec((tq, wide), lambda h, i: (i, ATT_COL0 // hs + h)),
                   pl.BlockSpec((hs, 8, tq), lambda h, i: (h, i, 0))],
        out_shape=[jax.ShapeDtypeStruct(mix.shape, mix.dtype), jax.ShapeDtypeStruct((MLA_HEADS, nq * 8, tq), F32)],
        input_output_aliases={3: 0}, compiler_params=_cp(2), name=name,
    )(q, k, v, mix)


def _attn_delta(dmix, mix, *, name):
    S = mix.shape[0]
    ts = min(S, 512)
    blk = pl.BlockSpec((ts, HEAD_PAD), lambda i, h: (i, ATT_COL0 + h))

    def body(do_ref, o_ref, d_ref):
        d_ref[0] = _stat_rows(jnp.sum(do_ref[...].astype(F32) * o_ref[...].astype(F32), axis=-1, keepdims=True))

    return pl.pallas_call(
        body, grid=(S // ts, MLA_HEADS), in_specs=[blk, blk],
        out_specs=pl.BlockSpec((1, 8, ts), lambda i, h: (h, i, 0)),
        out_shape=jax.ShapeDtypeStruct((MLA_HEADS, (S // ts) * 8, ts), F32), compiler_params=_cp(2), name=name,
    )(dmix, mix)


def _flash_bwd(q, k, v, dmix, lse_rows, delta_rows, *, name):
    S = q.shape[0]
    tq = min(S, 512)
    nq = S // tq

    def body(q_ref, do_ref, lse_ref, dl_ref, k_ref, v_ref, dq_ref, dk_ref, dv_ref):
        j = pl.program_id(1)

        @pl.when(j == 0)
        def _():
            dq_ref[...] = jnp.zeros_like(dq_ref)

        kv, vv = k_ref[...], v_ref[...]

        def step(i, carry, masked):
            dk, dv = carry
            off = pl.multiple_of(i * tq, tq)
            off8 = pl.multiple_of(i * 8, 8)
            qv = q_ref[pl.ds(off, tq), :]
            dov = do_ref[pl.ds(off, tq), :]
            lse = lse_ref[0, pl.ds(off8, 8), :][0:1]
            dl = dl_ref[0, pl.ds(off8, 8), :][0:1]
            st = _dot(kv, qv, NT) * ATT_SCALE
            if masked:
                krow = lax.broadcasted_iota(jnp.int32, (tq, tq), 0)
                qcol = lax.broadcasted_iota(jnp.int32, (tq, tq), 1)
                st = jnp.where(krow <= qcol, st, NEG_INF)
            pt = jnp.exp(st - lse)
            dv = dv + _dot(pt.astype(BF16), dov)
            dpt = _dot(vv, dov, NT)
            dst = (pt * (dpt - dl) * ATT_SCALE).astype(BF16)
            dk = dk + _dot(dst, qv)
            dq_ref[pl.ds(off, tq), :] += _dot(dst, kv, TN)
            return dk, dv

        zero = jnp.zeros((tq, HEAD_PAD), F32)
        carry = step(j, (zero, zero), True)
        dk, dv = lax.fori_loop(j + 1, nq, lambda i, c: step(i, c, False), carry)
        dk_ref[...] = dk
        dv_ref[...] = dv

    blk = pl.BlockSpec((tq, HEAD_PAD), lambda h, j: (j, h))
    full = pl.BlockSpec((S, HEAD_PAD), lambda h, j: (0, h))
    stat = pl.BlockSpec((1, nq * 8, tq), lambda h, j: (h, 0, 0))
    wide = jax.ShapeDtypeStruct((S, MLA_HEADS * HEAD_PAD), F32)
    return pl.pallas_call(
        body, grid=(MLA_HEADS, nq),
        in_specs=[full, pl.BlockSpec((S, HEAD_PAD), lambda h, j: (0, POOL_DIM // HEAD_PAD + h)), stat, stat, blk, blk],
        out_specs=[full, blk, blk], out_shape=[wide, wide, wide], compiler_params=_cp(2), name=name,
    )(q, dmix, lse_rows, delta_rows, k, v)


MEM_SCALE = MEM_HEAD_DIM ** -0.5


def _xattn_probs(qh, kh):
    s = _dot(qh, kh, NT) * MEM_SCALE
    e = jnp.exp(s - jnp.max(s, axis=-1, keepdims=True))
    return e / jnp.sum(e, axis=-1, keepdims=True)


def _xattn_fwd(q, kvm, *, name):
    S = q.shape[0]
    ts = min(S, 512)
    nm = kvm.shape[0]

    def body(q_ref, kv_ref, o_ref):
        for h in range(MEM_HEADS):
            lo, hi = h * MEM_HEAD_DIM, (h + 1) * MEM_HEAD_DIM
            p = _xattn_probs(q_ref[:, lo:hi], kv_ref[:, lo:hi])
            o_ref[:, lo:hi] = _dot(p.astype(BF16), kv_ref[:, D_MODEL + lo:D_MODEL + hi]).astype(o_ref.dtype)

    return pl.pallas_call(
        body, grid=(S // ts,), in_specs=[_rows(ts, D_MODEL), _const((nm, 2 * D_MODEL))],
        out_specs=_rows(ts, D_MODEL), out_shape=jax.ShapeDtypeStruct((S, D_MODEL), BF16),
        compiler_params=_cp(1), name=name,
    )(q, kvm)


def _xattn_bwd(q, kvm, do, *, name):
    S = q.shape[0]
    ts = min(S, 512)
    nm = kvm.shape[0]

    def body(q_ref, kv_ref, do_ref, dq_ref, dkv_ref):
        @pl.when(pl.program_id(0) == 0)
        def _():
            dkv_ref[...] = jnp.zeros_like(dkv_ref)

        for h in range(MEM_HEADS):
            lo, hi = h * MEM_HEAD_DIM, (h + 1) * MEM_HEAD_DIM
            qh, kh, vh = q_ref[:, lo:hi], kv_ref[:, lo:hi], kv_ref[:, D_MODEL + lo:D_MODEL + hi]
            doh = do_ref[:, lo:hi].astype(BF16)
            p = _xattn_probs(qh, kh)
            dp = _dot(doh, vh, NT)
            ds = (p * (dp - jnp.sum(dp * p, axis=-1, keepdims=True)) * MEM_SCALE).astype(BF16)
            dq_ref[:, lo:hi] = _dot(ds, kh).astype(dq_ref.dtype)
            dkv_ref[:, lo:hi] += _dot(ds, qh, TN)
            dkv_ref[:, D_MODEL + lo:D_MODEL + hi] += _dot(p.astype(BF16), doh, TN)

    return pl.pallas_call(
        body, grid=(S // ts,), in_specs=[_rows(ts, D_MODEL), _const((nm, 2 * D_MODEL)), _rows(ts, D_MODEL)],
        out_specs=[_rows(ts, D_MODEL), _const((nm, 2 * D_MODEL))],
        out_shape=[jax.ShapeDtypeStruct((S, D_MODEL), BF16), jax.ShapeDtypeStruct((nm, 2 * D_MODEL), F32)],
        compiler_params=_cp(1), name=name,
    )(q, kvm, do)


CONV_HALO = 8


def _sigmoid(x):
    return 1.0 / (1.0 + jnp.exp(-x))


def _softplus(x):
    return jnp.maximum(x, 0.0) + jnp.log(1.0 + jnp.exp(-jnp.abs(x)))


def _neg_expm1(x):
    series = -x * (1.0 + x * (1.0 / 2) * (1.0 + x * (1.0 / 3) * (1.0 + x * (1.0 / 4) * (1.0 + x * (1.0 / 5)))))
    return jnp.where(x > -0.05, series, 1.0 - jnp.exp(x))


GELU_C = math.sqrt(2.0 / math.pi)


def _gelu(x):
    return 0.5 * x * (1.0 + jnp.tanh(GELU_C * (x + 0.044715 * x * x * x)))


def _gelu_grad(x):
    t = jnp.tanh(GELU_C * (x + 0.044715 * x * x * x))
    return 0.5 * (1.0 + t) + 0.5 * x * (1.0 - t * t) * GELU_C * (1.0 + 3 * 0.044715 * x * x)


def _lru_gates(xc, wr_ref, br, wi_ref, bi, sp, reset):
    xcb = xc.astype(BF16)
    pr, pi = [], []
    for h in range(LRU_HEADS):
        lo, hi = h * LRU_HEAD_DIM, (h + 1) * LRU_HEAD_DIM
        pr.append(_dot(xcb[:, lo:hi], wr_ref[h]))
        pi.append(_dot(xcb[:, lo:hi], wi_ref[h]))
    r = _sigmoid(jnp.concatenate(pr, axis=1) + br)
    ig = _sigmoid(jnp.concatenate(pi, axis=1) + bi)
    log_a = -LRU_C * r * sp
    a = jnp.where(reset, 0.0, jnp.exp(log_a))
    mult = jnp.where(reset, 1.0, jnp.sqrt(jnp.maximum(_neg_expm1(2.0 * log_a), 0.0)))
    return r, ig, a, mult


def _lru_fwd(z, reset, conv_w, conv_b, w_r, b_r, w_i, b_i, lam, *, name):
    S = z.shape[0]
    ts = min(S, 512)
    nh = ts // CONV_HALO
    W = D_MODEL

    def body(gate_ref, xb_ref, halo_ref, rs_ref, cw_ref, cb_ref, wr_ref, br_ref, wi_ref, bi_ref, lam_ref,
             xc_ref, h_ref, y_ref, a_buf, carry):
        i = pl.program_id(0)

        @pl.when(i == 0)
        def _():
            carry[...] = jnp.zeros_like(carry)

        halo = jnp.where(i > 0, halo_ref[...], 0.0)
        xe = jnp.concatenate([halo, xb_ref[...]], axis=0)
        xc = cb_ref[...] + cw_ref[3:4, :] * xe[CONV_HALO:]
        for kk in range(CONV_WIDTH - 1):
            xc = xc + cw_ref[kk:kk + 1, :] * pltpu.roll(xe, CONV_WIDTH - 1 - kk, 0)[CONV_HALO:]
        xc_ref[...] = xc
        reset = rs_ref[...] > 0.5
        _, ig, a, mult = _lru_gates(xc, wr_ref, br_ref[...], wi_ref, bi_ref[...], _softplus(-lam_ref[...]), reset)
        a_buf[...] = a
        h_ref[...] = mult * (ig * xc)

        def scan(t, h):
            h = a_buf[pl.ds(t, 1), :] * h + h_ref[pl.ds(t, 1), :]
            h_ref[pl.ds(t, 1), :] = h
            return h

        carry[...] = lax.fori_loop(0, ts, scan, carry[...], unroll=8)
        y_ref[...] = (_gelu(gate_ref[...]) * h_ref[...]).astype(y_ref.dtype)

    vec = _const((1, W))
    gw = _const((LRU_HEADS, LRU_HEAD_DIM, LRU_HEAD_DIM))
    return pl.pallas_call(
        body, grid=(S // ts,),
        in_specs=[_rows(ts, W, 0), _rows(ts, W, 1),
                  pl.BlockSpec((CONV_HALO, W), lambda i: (jnp.maximum(i * nh - 1, 0), 1)),
                  _rows(ts, 1), _const((CONV_WIDTH, W)), vec, gw, vec, gw, vec, vec],
        out_specs=[_rows(ts, W)] * 3,
        out_shape=[jax.ShapeDtypeStruct((S, W), F32), jax.ShapeDtypeStruct((S, W), F32),
                   jax.ShapeDtypeStruct((S, W), BF16)],
        scratch_shapes=[pltpu.VMEM((ts, W), F32), pltpu.VMEM((1, W), F32)],
        compiler_params=_cp(1), name=name,
    )(z, z, z, reset, conv_w, conv_b, w_r, b_r, w_i, b_i, lam)


def _lru_bwd(dy, z, xc, hseq, reset, w_r, b_r, w_i, b_i, lam, *, name):
    S = z.shape[0]
    ts = min(S, 512)
    nt = S // ts
    nh = ts // CONV_HALO
    W = D_MODEL

    def body(dy_ref, gate_ref, xc_ref, h_ref, hh_ref, rs_ref, wr_ref, br_ref, wi_ref, bi_ref, lam_ref,
             dg_ref, dxc_ref, dpr_ref, dpi_ref, acc_ref, a_buf, dh_buf, carry):
        i = pl.program_id(0)
        tile = nt - 1 - i

        @pl.when(i == 0)
        def _():
            carry[...] = jnp.zeros_like(carry)
            acc_ref[...] = jnp.zeros_like(acc_ref)

        xc = xc_ref[...]
        lam_v = lam_ref[...]
        sp = _softplus(-lam_v)
        reset = rs_ref[...] > 0.5
        r, ig, a, mult = _lru_gates(xc, wr_ref, br_ref[...], wi_ref, bi_ref[...], sp, reset)
        gate = gate_ref[...]
        dyv = dy_ref[...].astype(F32)
        h = h_ref[...]
        dg_ref[...] = (dyv * h * _gelu_grad(gate)).astype(dg_ref.dtype)
        a_buf[...] = a
        dh_buf[...] = dyv * _gelu(gate)

        def scan(k, c):
            t = ts - 1 - k
            dh = dh_buf[pl.ds(t, 1), :] + c
            dh_buf[pl.ds(t, 1), :] = dh
            return a_buf[pl.ds(t, 1), :] * dh

        carry[...] = lax.fori_loop(0, ts, scan, carry[...], unroll=8)
        dh = dh_buf[...]
        hh = jnp.where(tile > 0, hh_ref[...], 0.0)
        h_prev = pltpu.roll(jnp.concatenate([hh, h], axis=0), 1, 0)[CONV_HALO:]
        da = dh * h_prev
        bx = ig * xc
        dmult = dh * bx
        dbx = dh * mult
        di = dbx * xc
        dlog_a = jnp.where(reset, 0.0, da * a - dmult * a * a / jnp.maximum(mult, 1e-30))
        dr = dlog_a * (-LRU_C) * sp
        dpre_r = dr * r * (1.0 - r)
        dpre_i = di * ig * (1.0 - ig)
        dprb, dpib = dpre_r.astype(BF16), dpre_i.astype(BF16)
        dpr_ref[...] = dprb
        dpi_ref[...] = dpib
        back = []
        for hd in range(LRU_HEADS):
            lo, hi = hd * LRU_HEAD_DIM, (hd + 1) * LRU_HEAD_DIM
            back.append(_dot(dprb[:, lo:hi], wr_ref[hd], NT) + _dot(dpib[:, lo:hi], wi_ref[hd], NT))
        dxc_ref[...] = dbx * ig + jnp.concatenate(back, axis=1)
        dlam = jnp.sum(dlog_a * (-LRU_C) * r, axis=0, keepdims=True) * (-_sigmoid(-lam_v))
        acc_ref[0:1, :] += jnp.sum(dpre_r, axis=0, keepdims=True)
        acc_ref[1:2, :] += jnp.sum(dpre_i, axis=0, keepdims=True)
        acc_ref[2:3, :] += dlam

    rev = lambda cb: pl.BlockSpec((ts, W), lambda i: (nt - 1 - i, cb))
    vec = _const((1, W))
    gw = _const((LRU_HEADS, LRU_HEAD_DIM, LRU_HEAD_DIM))
    return pl.pallas_call(
        body, grid=(nt,),
        in_specs=[rev(0), rev(0), rev(0), rev(0),
                  pl.BlockSpec((CONV_HALO, W), lambda i: (jnp.maximum((nt - 1 - i) * nh - 1, 0), 0)),
                  pl.BlockSpec((ts, 1), lambda i: (nt - 1 - i, 0)), gw, vec, gw, vec, vec],
        out_specs=[rev(0), rev(0), rev(0), rev(0), _const((8, W))],
        out_shape=[jax.ShapeDtypeStruct((S, W), BF16), jax.ShapeDtypeStruct((S, W), F32),
                   jax.ShapeDtypeStruct((S, W), BF16), jax.ShapeDtypeStruct((S, W), BF16),
                   jax.ShapeDtypeStruct((8, W), F32)],
        scratch_shapes=[pltpu.VMEM((ts, W), F32), pltpu.VMEM((ts, W), F32), pltpu.VMEM((1, W), F32)],
        compiler_params=_cp(1), name=name,
    )(dy, z, xc, hseq, hseq, reset, w_r, b_r, w_i, b_i, lam)


def _conv_bwd(dxc, z, conv_w, *, name):
    S = dxc.shape[0]
    ts = min(S, 512)
    nh = ts // CONV_HALO
    last = S // CONV_HALO - 1
    W = D_MODEL
    n = ts + CONV_HALO

    def body(d_ref, dn_ref, xb_ref, xp_ref, cw_ref, dxb_ref, acc_ref):
        i = pl.program_id(0)

        @pl.when(i == 0)
        def _():
            acc_ref[...] = jnp.zeros_like(acc_ref)

        d = d_ref[...]
        de = jnp.concatenate([d, jnp.where(i < pl.num_programs(0) - 1, dn_ref[...], 0.0)], axis=0)
        xe = jnp.concatenate([jnp.where(i > 0, xp_ref[...], 0.0), xb_ref[...]], axis=0)
        dxb = cw_ref[3:4, :] * d
        acc_ref[3:4, :] += jnp.sum(d * xe[CONV_HALO:], axis=0, keepdims=True)
        for kk in range(CONV_WIDTH - 1):
            sh = CONV_WIDTH - 1 - kk
            dxb = dxb + cw_ref[kk:kk + 1, :] * pltpu.roll(de, n - sh, 0)[:ts]
            acc_ref[kk:kk + 1, :] += jnp.sum(d * pltpu.roll(xe, sh, 0)[CONV_HALO:], axis=0, keepdims=True)
        dxb_ref[...] = dxb.astype(dxb_ref.dtype)
        acc_ref[4:5, :] += jnp.sum(d, axis=0, keepdims=True)

    return pl.pallas_call(
        body, grid=(S // ts,),
        in_specs=[_rows(ts, W), pl.BlockSpec((CONV_HALO, W), lambda i: (jnp.minimum((i + 1) * nh, last), 0)),
                  _rows(ts, W, 1), pl.BlockSpec((CONV_HALO, W), lambda i: (jnp.maximum(i * nh - 1, 0), 1)),
                  _const((CONV_WIDTH, W))],
        out_specs=[_rows(ts, W), _const((8, W))],
        out_shape=[jax.ShapeDtypeStruct((S, W), BF16), jax.ShapeDtypeStruct((8, W), F32)],
        compiler_params=_cp(1), name=name,
    )(dxc, dxc, z, z, conv_w)


def _loss_head(x, g, target, *, name):
    S, D = x.shape
    ts = min(S, 512)

    def body(x_ref, g_ref, t_ref, dx_ref, dg_ref, l_ref):
        @pl.when(pl.program_id(0) == 0)
        def _():
            dg_ref[...] = jnp.zeros_like(dg_ref)
            l_ref[...] = jnp.zeros_like(l_ref)

        xv = x_ref[...]
        r = lax.rsqrt(jnp.mean(xv * xv, axis=-1, keepdims=True) + RMS_EPS)
        n = xv * r
        err = n * g_ref[...] - t_ref[...]
        l_ref[...] += 0.5 * jnp.sum(jnp.sum(err * err, axis=-1, keepdims=True) * (1.0 / D), axis=0, keepdims=True)
        dy = err * (1.0 / D)
        dn = dy * g_ref[...]
        dx_ref[...] = r * (dn - n * jnp.mean(dn * n, axis=-1, keepdims=True))
        dg_ref[...] += jnp.sum(dy * n, axis=0, keepdims=True)

    return pl.pallas_call(
        body, grid=(S // ts,), in_specs=[_rows(ts, D), _const((1, D)), _rows(ts, D)],
        out_specs=[_rows(ts, D), _const((1, D)), _const((8, LANES))],
        out_shape=[jax.ShapeDtypeStruct((S, D), F32), jax.ShapeDtypeStruct((1, D), F32),
                   jax.ShapeDtypeStruct((8, LANES), F32)],
        compiler_params=_cp(1), name=name,
    )(x, g.reshape(1, D), target)


def _adamw(w, ga, gb, m, v, *, name):
    shape = w.shape
    cols = shape[-1]
    rows = w.size // cols
    br = rows
    if rows * cols * 4 > (1 << 20):
        br = max(d for d in range(8, rows + 1, 8) if rows % d == 0 and d * cols * 4 <= (1 << 20))

    def body(w_ref, ga_ref, gb_ref, m_ref, v_ref, g_ref, d_ref, mo_ref, vo_ref):
        gv = ga_ref[...] + gb_ref[...]
        g_ref[...] = gv
        mn = ADAM_B1 * m_ref[...] + (1.0 - ADAM_B1) * gv
        vn = ADAM_B2 * v_ref[...] + (1.0 - ADAM_B2) * (gv * gv)
        m_hat = mn / (1.0 - ADAM_B1 ** ADAM_STEP)
        v_hat = vn / (1.0 - ADAM_B2 ** ADAM_STEP)
        d_ref[...] = -ADAM_LR * (m_hat / (jnp.sqrt(v_hat) + ADAM_EPS) + ADAM_WD * w_ref[...])
        mo_ref[...] = mn
        vo_ref[...] = vn

    spec = _rows(br, cols)
    outs = pl.pallas_call(
        body, grid=(rows // br,), in_specs=[spec] * 5, out_specs=[spec] * 4,
        out_shape=[jax.ShapeDtypeStruct((rows, cols), F32)] * 4, compiler_params=_cp(1), name=name,
    )(*[t.reshape(rows, cols) for t in (w, ga, gb, m, v)])
    return [o.reshape(shape) for o in outs]


def _pad_heads(w, width):
    k = w.shape[0]
    return jnp.pad(w.reshape(k, MLA_HEADS, width), ((0, 0), (0, 0), (0, HEAD_PAD - width))).reshape(k, -1)


def _unpad_heads(w, width):
    k = w.shape[0]
    return w.reshape(k, MLA_HEADS, HEAD_PAD)[:, :, :width].reshape(k, MLA_HEADS * width)


def _rope_tables(positions):
    inv_freq = ROPE_BASE ** (-jnp.arange(0, QK_ROPE, 2, dtype=F32) / QK_ROPE)
    ang = positions.astype(F32)[:, None] * inv_freq
    cos, sin = jnp.cos(ang), jnp.sin(ang)
    S = positions.shape[0]
    ones, zeros = jnp.ones((S, QK_NOPE), F32), jnp.zeros((S, QK_NOPE), F32)
    ctab = jnp.concatenate([ones, cos, cos, ones[:, :HEAD_PAD - QK_DIM]], axis=1)
    stab = jnp.concatenate([zeros, -sin, sin, zeros[:, :HEAD_PAD - QK_DIM]], axis=1)
    return ctab, stab


def _memory_block(x, mem, W, layer, tag):
    hx = _rms(x, W["xa_norm_x"][layer], name=f"{tag}_xa_norm")
    qx = _mm(hx, [(W["xa_w_q"][layer], 0, 0)], _first, [(D_MODEL, BF16, 0)], tn=D_MODEL, nj=1, name=f"{tag}_xa_q")[0]
    mn = _rms(mem, W["xa_norm_mem"][layer], name=f"{tag}_xa_norm_mem")
    kvm = _mm(mn, [(W["xa_w_kv"][layer], 0, 0)], _first, [(2 * D_MODEL, BF16, 0)], tn=2 * D_MODEL, nj=1,
              name=f"{tag}_xa_kv")[0]
    o = _xattn_fwd(qx, kvm, name=f"{tag}_xa_attn")
    xo = _mm(o, [(W["xa_w_o"][layer], 0, 0)], _add_res, [(D_MODEL, F32, 0)], extras=[(x, 0)], tn=D_MODEL, nj=1,
             name=f"{tag}_xa_out")[0]
    return xo, (x, hx, qx, mn, kvm, o)


def _memory_block_bwd(dxo, mem, W, layer, saved, tag, grads):
    x, hx, qx, mn, kvm, o = saved
    wq, wkv, wo = W["xa_w_q"][layer], W["xa_w_kv"][layer], W["xa_w_o"][layer]
    do = _mm(dxo, [(wo, 0, 0)], _first, [(D_MODEL, BF16, 0)], nt=True, tn=D_MODEL, nj=1, name=f"{tag}_xa_do")[0]
    grads["xa_w_o"][layer] = _owner_major(_mm_tn(o, dxo, name=f"{tag}_xa_dwo"), 0)
    dqx, dkvm = _xattn_bwd(qx, kvm, do, name=f"{tag}_xa_attn_bwd")
    dhx = _mm(dqx, [(wq, 0, 0)], _first, [(D_MODEL, F32, 0)], nt=True, tn=D_MODEL, nj=1, name=f"{tag}_xa_dhx")[0]
    grads["xa_w_q"][layer] = _owner_major(_mm_tn(hx, dqx, name=f"{tag}_xa_dwq"), 0)
    dx, dg = _rms_bwd(x, W["xa_norm_x"][layer], dhx, res=dxo, name=f"{tag}_xa_norm_bwd")
    grads["xa_norm_x"][layer] = dg[0]
    dmn = _mm(dkvm, [(wkv, 0, 0)], _first, [(D_MODEL, F32, 0)], nt=True, tn=D_MODEL, nj=1, name=f"{tag}_xa_dmn")[0]
    grads["xa_w_kv"][layer] = _mm_tn_owners(mn, [dkvm], name=f"{tag}_xa_dwkv")
    _, dgm = _rms_bwd(mem, W["xa_norm_mem"][layer], dmn, name=f"{tag}_xa_norm_mem_bwd")
    grads["xa_norm_mem"][layer] = dgm[0]
    return dx


FF_TN = D_FF // 2


def _silu_mul(accs, extras):
    g, u = accs
    return [g * _sigmoid(g) * u, g, u]


def _silu_mul_bwd(accs, extras):
    da = accs[0]
    g, u = extras[0].astype(F32), extras[1].astype(F32)
    sg = _sigmoid(g)
    return [da * u * sg * (1.0 + g * (1.0 - sg)), da * g * sg]


def _ffn_block(x, W, layer, tag):
    hf = _rms(x, W["ffn_norm"][layer], name=f"{tag}_ffn_norm")
    wgu, wd = W["ffn_w_gate_up"][layer], W["ffn_w_down"][layer]
    act, g, u = _mm(hf, [(wgu, 0, 0), (wgu, 0, 2)], _silu_mul, [(D_FF, BF16, 0)] * 3, tn=FF_TN, nj=2,
                    name=f"{tag}_ffn_up")
    xo = _mm(act, [(wd, 0, 0)], _add_res, [(D_MODEL, F32, 0)], extras=[(x, 0)], tn=D_MODEL, nj=1,
             name=f"{tag}_ffn_down")[0]
    return xo, (x, hf, act, g, u)


def _ffn_block_bwd(dxo, W, layer, saved, tag, grads):
    x, hf, act, g, u = saved
    wgu, wd = W["ffn_w_gate_up"][layer], W["ffn_w_down"][layer]
    dg, du = _mm(dxo, [(wd, 0, 0)], _silu_mul_bwd, [(D_FF, BF16, 0)] * 2, nt=True, extras=[(g, 0), (u, 0)], tn=FF_TN,
                 nj=2, name=f"{tag}_ffn_dact")
    grads["ffn_w_down"][layer] = _owner_major(_mm_tn(act, dxo, tk=FF_TN, name=f"{tag}_ffn_dwd"), 0)
    dhf = _mm(dg, [(wgu, 0, 0)], _first, [(D_MODEL, F32, 0)], nt=True, tn=D_MODEL, nj=1, name=f"{tag}_ffn_dhf_g")[0]
    dhf = _mm(du, [(wgu, 0, 1)], _add_res, [(D_MODEL, F32, 0)], nt=True, extras=[(dhf, 0)], tn=D_MODEL, nj=1,
              name=f"{tag}_ffn_dhf_u")[0]
    grads["ffn_w_gate_up"][layer] = _mm_tn_owners(hf, [dg, du], name=f"{tag}_ffn_dwgu")
    dx, dgn = _rms_bwd(x, W["ffn_norm"][layer], dhf, res=dxo, name=f"{tag}_ffn_norm_bwd")
    grads["ffn_norm"][layer] = dgn[0]
    return dx


def _even_block(x, tabs, W, tag):
    ctab, stab = tabs
    w_in = W["ev_w_in"][0]
    zero = jnp.zeros((D_MODEL, QK_NOPE), BF16)
    w_in_pad = jnp.concatenate([w_in[:, :896], zero, w_in[:, 896:], zero[:, :HEAD_PAD - QK_DIM]], axis=1)
    w_q_pad = _pad_heads(W["ev_w_q_up"][0], QK_DIM)
    wkv = W["ev_w_kv_up"][0].reshape(KV_RANK, MLA_HEADS, QK_NOPE + V_HEAD)
    w_kv_pad = jnp.concatenate([_pad_heads(wkv[:, :, :QK_NOPE].reshape(KV_RANK, -1), QK_NOPE),
                                _pad_heads(wkv[:, :, QK_NOPE:].reshape(KV_RANK, -1), V_HEAD)], axis=1)
    w_out = W["ev_w_out"][0]
    w_att = jnp.pad(w_out[POOL_DIM:].reshape(MLA_HEADS, V_HEAD, D_MODEL), ((0, 0), (0, HEAD_PAD - V_HEAD), (0, 0)))
    w_out_pad = jnp.concatenate([w_out[:POOL_DIM], w_att.reshape(MLA_HEADS * HEAD_PAD, D_MODEL)], axis=0)
    pool_w = W["ev_pool_w"][0].astype(BF16)
    pool_scale = W["ev_pool_scale"]

    h = _rms(x, W["ev_norm"][0], name=f"{tag}_norm")
    z = _mm(h, [(w_in_pad, 0, 0)], _first, [(D_MODEL, F32, 0)], tn=D_MODEL, nj=1, name=f"{tag}_in")[0]
    mix, pooled = _pool_fwd(z, pool_w, pool_scale, name=f"{tag}_pool")
    cqn = _rms(z, W["ev_q_norm"][0], cb=2, w=Q_RANK, name=f"{tag}_q_norm")
    ckvn = _rms(z, W["ev_kv_norm"][0], cb=6, w=KV_RANK, name=f"{tag}_kv_norm")
    q_pad = _mm(cqn, [(w_q_pad, 0, 0)], _first, [(D_MODEL, F32, 0)], tn=D_MODEL, nj=1, name=f"{tag}_q_up")[0]
    k_pad, v_pad = _mm(ckvn, [(w_kv_pad, 0, 0), (w_kv_pad, 0, 1)], lambda a, e: a,
                       [(D_MODEL, F32, 0), (D_MODEL, BF16, 0)], tn=D_MODEL, nj=1, name=f"{tag}_kv_up")
    q_rot, k_cat = _rope_fwd(q_pad, k_pad, z, ctab, stab, name=f"{tag}_rope")
    mix, lse = _flash_fwd(q_rot, k_cat, v_pad, mix, name=f"{tag}_attn")
    xo = _mm(mix, [(w_out_pad, 0, 0)], _add_res, [(D_MODEL, F32, 0)], extras=[(x, 0)], tn=D_MODEL, nj=1,
             name=f"{tag}_out")[0]
    saved = (x, h, z, pooled, cqn, ckvn, q_rot, k_cat, v_pad, lse, mix,
             (w_in_pad, w_q_pad, w_kv_pad, w_out_pad, pool_w, pool_scale))
    return xo, saved


def _even_block_bwd(dxo, tabs, W, saved, tag, grads):
    ctab, stab = tabs
    x, h, z, pooled, cqn, ckvn, q_rot, k_cat, v_pad, lse, mix, wts = saved
    w_in_pad, w_q_pad, w_kv_pad, w_out_pad, pool_w, pool_scale = wts
    dmix = _mm(dxo, [(w_out_pad, 0, 0)], _first, [(MIX_DIM, BF16, 0)], nt=True, tn=MIX_DIM, nj=1,
               name=f"{tag}_dmix")[0]
    dw_out_pad = _mm_tn(mix, dxo, tk=MIX_DIM // 3, name=f"{tag}_dw_out")
    datt = dw_out_pad[POOL_DIM:].reshape(MLA_HEADS, HEAD_PAD, D_MODEL)[:, :V_HEAD].reshape(-1, D_MODEL)
    grads["ev_w_out"] = [_owner_major(jnp.concatenate([dw_out_pad[:POOL_DIM], datt], axis=0), 0)]
    delta = _attn_delta(dmix, mix, name=f"{tag}_delta")
    dq_rot, dk_cat, dv_pad = _flash_bwd(q_rot, k_cat, v_pad, dmix, lse, delta, name=f"{tag}_attn_bwd")
    dq_pad, dkr = _rope_bwd(dq_rot, dk_cat, ctab, stab, name=f"{tag}_rope_bwd")
    dw_q_pad = _mm_tn(cqn, dq_pad, name=f"{tag}_dw_q_up")
    grads["ev_w_q_up"] = [_owner_major(_unpad_heads(dw_q_pad, QK_DIM), 1)]
    dcqn = _mm(dq_pad, [(w_q_pad, 0, 0)], _first, [(Q_RANK, F32, 0)], nt=True, tn=Q_RANK, nj=1, name=f"{tag}_dcqn")[0]
    dwk = _unpad_heads(_mm_tn(ckvn, dk_cat, name=f"{tag}_dw_k_up"), QK_NOPE).reshape(KV_RANK, MLA_HEADS, QK_NOPE)
    dwv = _unpad_heads(_mm_tn(ckvn, dv_pad, name=f"{tag}_dw_v_up"), V_HEAD).reshape(KV_RANK, MLA_HEADS, V_HEAD)
    grads["ev_w_kv_up"] = [_owner_major(jnp.concatenate([dwk, dwv], axis=2).reshape(KV_RANK, -1), 1)]
    dckvn = _mm(dk_cat, [(w_kv_pad, 0, 0)], _first, [(KV_RANK, F32, 0)], nt=True, tn=KV_RANK, nj=1,
                name=f"{tag}_dckvn_k")[0]
    dckvn = _mm(dv_pad, [(w_kv_pad, 0, 1)], _add_res, [(KV_RANK, F32, 0)], nt=True, extras=[(dckvn, 0)], tn=KV_RANK,
                nj=1, name=f"{tag}_dckvn_v")[0]
    dcq, dgq = _rms_bwd(z, W["ev_q_norm"][0], dcqn, cb=2, w=Q_RANK, out_dtype=BF16, name=f"{tag}_q_norm_bwd")
    dckv, dgkv = _rms_bwd(z, W["ev_kv_norm"][0], dckvn, cb=6, w=KV_RANK, out_dtype=BF16, name=f"{tag}_kv_norm_bwd")
    grads["ev_q_norm"], grads["ev_kv_norm"] = dgq, dgkv
    du, dypre, dscale = _pool_bwd(dmix, pooled, pool_w, pool_scale, name=f"{tag}_pool_bwd")
    grads["ev_pool_scale"] = dscale
    grads["ev_pool_w"] = _mm_tn_grouped(pooled, dypre, 4, POOL_GROUP, name=f"{tag}_dpool_w")[None]
    dz = jnp.concatenate([du, dcq, dckv, dkr], axis=1)
    dw_in_pad = _mm_tn(h, dz, name=f"{tag}_dw_in")
    grads["ev_w_in"] = [_owner_major(jnp.concatenate([dw_in_pad[:, :896], dw_in_pad[:, 960:992]], axis=1), 0)]
    dh = _mm(dz, [(w_in_pad, 0, 0)], _first, [(D_MODEL, F32, 0)], nt=True, tn=D_MODEL, nj=1, name=f"{tag}_dh")[0]
    dx, dgn = _rms_bwd(x, W["ev_norm"][0], dh, res=dxo, name=f"{tag}_norm_bwd")
    grads["ev_norm"] = dgn
    return dx


def _odd_block(x, reset, W, tag):
    h = _rms(x, W["od_norm"][0], name=f"{tag}_norm")
    z = _mm(h, [(W["od_w_in"][0], 0, 0)], _first, [(2 * D_MODEL, F32, 0)], tn=D_MODEL, nj=2, name=f"{tag}_in")[0]
    w_r, w_i = W["od_w_rgate"][0], W["od_w_igate"][0]
    vecs = [W[n].reshape(1, D_MODEL) for n in ("od_conv_b", "od_b_rgate", "od_b_igate", "od_lambda")]
    xc, hseq, y = _lru_fwd(z, reset, W["od_conv_w"][0], vecs[0], w_r, vecs[1], w_i, vecs[2], vecs[3],
                           name=f"{tag}_lru")
    xo = _mm(y, [(W["od_w_out"][0], 0, 0)], _add_res, [(D_MODEL, F32, 0)], extras=[(x, 0)], tn=D_MODEL, nj=1,
             name=f"{tag}_out")[0]
    return xo, (x, h, z, xc, hseq, y, vecs)


def _odd_block_bwd(dxo, reset, W, saved, tag, grads):
    x, h, z, xc, hseq, y, vecs = saved
    w_r, w_i = W["od_w_rgate"][0], W["od_w_igate"][0]
    dy = _mm(dxo, [(W["od_w_out"][0], 0, 0)], _first, [(D_MODEL, F32, 0)], nt=True, tn=D_MODEL, nj=1,
             name=f"{tag}_dy")[0]
    grads["od_w_out"] = [_owner_major(_mm_tn(y, dxo, name=f"{tag}_dw_out"), 0)]
    dgate, dxc, dpr, dpi, acc = _lru_bwd(dy, z, xc, hseq, reset, w_r, vecs[1], w_i, vecs[2], vecs[3],
                                         name=f"{tag}_lru_bwd")
    grads["od_b_rgate"], grads["od_b_igate"], grads["od_lambda"] = acc[0:1], acc[1:2], acc[2:3]
    grads["od_w_rgate"] = [_owner_major(_mm_tn_grouped(xc, dpr, LRU_HEADS, LRU_HEAD_DIM, name=f"{tag}_dw_rgate"), 1)]
    grads["od_w_igate"] = [_owner_major(_mm_tn_grouped(xc, dpi, LRU_HEADS, LRU_HEAD_DIM, name=f"{tag}_dw_igate"), 1)]
    dxb, cacc = _conv_bwd(dxc, z, W["od_conv_w"][0], name=f"{tag}_conv_bwd")
    grads["od_conv_w"], grads["od_conv_b"] = cacc[None, 0:4], cacc[4:5]
    dz = jnp.concatenate([dgate, dxb], axis=1)
    grads["od_w_in"] = [_mm_tn_owners(h, [dz], name=f"{tag}_dw_in")]
    dh = _mm(dz, [(W["od_w_in"][0], 0, 0)], _first, [(D_MODEL, F32, 0)], nt=True, tn=D_MODEL, nj=1,
             name=f"{tag}_dh")[0]
    dx, dgn = _rms_bwd(x, W["od_norm"][0], dh, res=dxo, name=f"{tag}_norm_bwd")
    grads["od_norm"] = dgn
    return dx


def _local_step(x, mem, positions, target, W):
    tabs = _rope_tables(positions)
    reset = (positions == 0).astype(F32)[:, None]
    grads = {n: [None, None] for n in ("xa_norm_x", "xa_norm_mem", "xa_w_q", "xa_w_kv", "xa_w_o", "ffn_norm",
                                       "ffn_w_gate_up", "ffn_w_down")}
    x1, s_even = _even_block(x, tabs, W, "l0_even")
    x2, s_xa0 = _memory_block(x1, mem, W, 0, "l0")
    x3, s_ff0 = _ffn_block(x2, W, 0, "l0")
    x4, s_odd = _odd_block(x3, reset, W, "l1_odd")
    x5, s_xa1 = _memory_block(x4, mem, W, 1, "l1")
    x6, s_ff1 = _ffn_block(x5, W, 1, "l1")
    d, dgf, loss = _loss_head(x6, W["final_norm"], target, name="loss_head")
    grads["final_norm"] = dgf[0]
    d = _ffn_block_bwd(d, W, 1, s_ff1, "l1", grads)
    d = _memory_block_bwd(d, mem, W, 1, s_xa1, "l1", grads)
    d = _odd_block_bwd(d, reset, W, s_odd, "l1_odd", grads)
    d = _ffn_block_bwd(d, W, 0, s_ff0, "l0", grads)
    d = _memory_block_bwd(d, mem, W, 0, s_xa0, "l0", grads)
    d = _even_block_bwd(d, tabs, W, s_even, "l0_even", grads)
    big = {n: grads.pop(n) for n in MATMUL_WEIGHTS}
    for n, v in grads.items():
        if isinstance(v, list):
            grads[n] = jnp.stack(v)
    return loss[0, 0], d, big, grads


WEIGHTS = ("ev_norm", "ev_w_in", "ev_pool_w", "ev_pool_scale", "ev_q_norm", "ev_w_q_up", "ev_kv_norm", "ev_w_kv_up",
           "ev_w_out", "od_norm", "od_w_in", "od_conv_w", "od_conv_b", "od_w_rgate", "od_b_rgate", "od_w_igate",
           "od_b_igate", "od_lambda", "od_w_out", "xa_norm_x", "xa_norm_mem", "xa_w_q", "xa_w_kv", "xa_w_o",
           "ffn_norm", "ffn_w_gate_up", "ffn_w_down", "final_norm")
SHARD_AXIS = {"ev_w_in": 1, "ev_w_q_up": 2, "ev_w_kv_up": 2, "ev_w_out": 1, "od_norm": 1, "od_w_in": 2,
              "od_conv_w": 2, "od_conv_b": 1, "od_w_rgate": 2, "od_b_rgate": 1, "od_w_igate": 2, "od_b_igate": 1,
              "od_lambda": 1, "od_w_out": 1, "xa_w_q": 1, "xa_w_kv": 2, "xa_w_o": 1, "ffn_w_gate_up": 2,
              "ffn_w_down": 1}
MATMUL_WEIGHTS = ("ev_w_in", "ev_w_q_up", "ev_w_kv_up", "ev_w_out", "od_w_in", "od_w_rgate", "od_w_igate",
                  "od_w_out", "xa_w_q", "xa_w_kv", "xa_w_o", "ffn_w_gate_up", "ffn_w_down")
SMALL_SHARDED = tuple(n for n in WEIGHTS if n in SHARD_AXIS and n not in MATMUL_WEIGHTS)
REPLICATED = tuple(n for n in WEIGHTS if n not in SHARD_AXIS)


def _pack(parts, quantum):
    flat = jnp.concatenate([p.reshape(-1) for p in parts])
    pad = (-flat.shape[0]) % quantum
    return jnp.pad(flat, (0, pad)).reshape(-1, LANES)


def _unpack(flat, shapes):
    out, off = [], 0
    for shape in shapes:
        size = math.prod(shape)
        out.append(flat[off:off + size].reshape(shape))
        off += size
    return out


def _run_copies(local, remote, send_sems, recv_sems, local_sems):
    locals_ = [pltpu.make_async_copy(src, dst, local_sems.at[n]) for n, (src, dst) in enumerate(local)]
    for cp in locals_:
        cp.start()
    sends = [pltpu.make_async_remote_copy(src_ref=src, dst_ref=dst, send_sem=send_sems.at[k, n],
                                          recv_sem=recv_sems.at[k, n], device_id=dev, device_id_type=MESH)
             for (k, n, src, dst, _, dev) in remote]
    for cp in sends:
        cp.start()
    for (k, n, src, _, arrival, dev) in remote:
        pltpu.make_async_remote_copy(src_ref=src, dst_ref=arrival, send_sem=send_sems.at[k, n],
                                     recv_sem=recv_sems.at[k, n], device_id=dev, device_id_type=MESH).wait_recv()
    for cp in sends:
        cp.wait_send()
    for cp in locals_:
        cp.wait()


def _chip_peers(x, y):
    return [(1 - x, y), (x, 1 - y), (1 - x, 1 - y)]


def _owner_block(ref, axis, q):
    size = ref.shape[axis] // N_CHIPS
    idx = [slice(None)] * len(ref.shape)
    idx[axis] = pl.ds(q * size, size)
    return ref.at[tuple(idx)]


def _comm_call(body, ins, out_shapes, n_items, n_peers, *, name):
    return pl.pallas_call(
        body, in_specs=[ANY] * len(ins), out_specs=[ANY] * len(out_shapes), out_shape=out_shapes,
        scratch_shapes=[pltpu.SemaphoreType.DMA((n_peers, n_items)), pltpu.SemaphoreType.DMA((n_peers, n_items)),
                        pltpu.SemaphoreType.DMA((n_items,))],
        name=name,
    )(*ins)


def _gather_chips(shards, axes, *, name):
    n = len(shards)
    full = [jax.ShapeDtypeStruct(tuple(d * (N_CHIPS if a == ax else 1) for a, d in enumerate(s.shape)), s.dtype)
            for s, ax in zip(shards, axes)]

    def body(*refs):
        srcs, dsts = refs[:n], refs[n:2 * n]
        x, y, c = lax.axis_index("x"), lax.axis_index("y"), lax.axis_index("c")
        me = 2 * x + y
        local = [(srcs[i], _owner_block(dsts[i], axes[i], me)) for i in range(n)]
        remote = [(k, i, srcs[i], _owner_block(dsts[i], axes[i], me), _owner_block(dsts[i], axes[i], 2 * px + py),
                   (px, py, c))
                  for k, (px, py) in enumerate(_chip_peers(x, y)) for i in range(n)]
        _run_copies(local, remote, *refs[2 * n:])

    return _comm_call(body, shards, full, n, 3, name=name)


def _exchange_chips(items, *, name):
    flat = [(n, l, a) for n, layers in enumerate(items) for l, a in enumerate(layers)]
    outs = [jax.ShapeDtypeStruct((N_CHIPS, len(layers)) + layers[0].shape[1:], layers[0].dtype) for layers in items]
    ni = len(flat)

    def body(*refs):
        srcs, dsts = refs[:ni], refs[ni:ni + len(items)]
        x, y, c = lax.axis_index("x"), lax.axis_index("y"), lax.axis_index("c")
        me = 2 * x + y
        local = [(srcs[i].at[me], dsts[n].at[me, l]) for i, (n, l, _) in enumerate(flat)]
        remote = [(k, i, srcs[i].at[2 * px + py], dsts[n].at[me, l], dsts[n].at[2 * px + py, l], (px, py, c))
                  for k, (px, py) in enumerate(_chip_peers(x, y)) for i, (n, l, _) in enumerate(flat)]
        _run_copies(local, remote, *refs[ni + len(items):])

    return _comm_call(body, [a for (_, _, a) in flat], outs, ni, 3, name=name)


def _exchange_sibling(arrays, *, name):
    n = len(arrays)

    def body(*refs):
        x, y, c = lax.axis_index("x"), lax.axis_index("y"), lax.axis_index("c")
        remote = [(0, i, refs[i], refs[n + i], refs[n + i], (x, y, 1 - c)) for i in range(n)]
        _run_copies([], remote, *refs[2 * n:])

    return _comm_call(body, arrays, [jax.ShapeDtypeStruct(a.shape, a.dtype) for a in arrays], n, 1, name=name)


def _sum_slots(r, *, name):
    shape = r.shape[1:]
    cols = shape[-1]
    rows = math.prod(shape) // cols
    tr = max(d for d in range(8, rows + 1, 8) if rows % d == 0 and d * cols * 16 <= (4 << 20))

    def body(r_ref, o_ref):
        o_ref[...] = ((r_ref[0] + r_ref[1]) + r_ref[2]) + r_ref[3]

    return pl.pallas_call(
        body, grid=(rows // tr,), in_specs=[pl.BlockSpec((N_CHIPS, tr, cols), lambda i: (0, i, 0))],
        out_specs=_rows(tr, cols), out_shape=jax.ShapeDtypeStruct((rows, cols), F32), compiler_params=_cp(1),
        name=name,
    )(r.reshape(N_CHIPS, rows, cols)).reshape(shape)


def _gather_weights(w):
    small = _pack([w[n] for n in SMALL_SHARDED], 8 * LANES)
    stacked = [n for n in MATMUL_WEIGHTS if SHARD_AXIS[n] == w[n].ndim - 1 and w[n].shape[-1] % LANES]
    shards = [w[n].astype(BF16)[None] if n in stacked else w[n].astype(BF16) for n in MATMUL_WEIGHTS]
    got = _gather_chips(shards + [small], [0 if n in stacked else SHARD_AXIS[n] for n in MATMUL_WEIGHTS] + [0],
                        name="gather_weights")
    full = {n: w[n] for n in REPLICATED}
    for n, g in zip(MATMUL_WEIGHTS, got[:-1]):
        full[n] = jnp.concatenate([g[q] for q in range(N_CHIPS)], axis=SHARD_AXIS[n]) if n in stacked else g
    per_chip = [_unpack(got[-1][q * small.shape[0]:(q + 1) * small.shape[0]].reshape(-1),
                        [w[n].shape for n in SMALL_SHARDED]) for q in range(N_CHIPS)]
    for i, n in enumerate(SMALL_SHARDED):
        full[n] = jnp.concatenate([per_chip[q][i] for q in range(N_CHIPS)], axis=SHARD_AXIS[n])
    return full


def _owner_major(g, axis):
    shape = g.shape
    size = shape[axis] // N_CHIPS
    g = jnp.moveaxis(g.reshape(shape[:axis] + (N_CHIPS, size) + shape[axis + 1:]), axis, 0)
    return g.reshape(N_CHIPS, -1, shape[-1] if axis < len(shape) - 1 else size)


def _reduce_grads(big, grads, w):
    small = [_pack([jnp.split(grads[n], N_CHIPS, axis=SHARD_AXIS[n])[q] for n in SMALL_SHARDED], 8 * LANES)
             for q in range(N_CHIPS)]
    repl = _pack([grads[n] for n in REPLICATED], 8 * LANES)
    items = [big[n] for n in MATMUL_WEIGHTS] + [[jnp.stack(small)], [jnp.stack([repl] * N_CHIPS)]]
    got = _exchange_chips(items, name="exchange_grads")
    mine = [_sum_slots(r, name=f"sum_chips_{i}") for i, r in enumerate(got)]
    other = _exchange_sibling(mine, name="exchange_sibling")
    return mine, other


def kernel(
        x, mem, positions, ev_norm, ev_w_in, ev_pool_w, ev_pool_scale, ev_q_norm, ev_w_q_up, ev_kv_norm,
        ev_w_kv_up, ev_w_out, od_norm, od_w_in, od_conv_w, od_conv_b, od_w_rgate, od_b_rgate, od_w_igate,
        od_b_igate, od_lambda, od_w_out, xa_norm_x, xa_norm_mem, xa_w_q, xa_w_kv, xa_w_o, ffn_norm,
        ffn_w_gate_up, ffn_w_down, final_norm, loss_target, m_ev_norm, m_ev_w_in, m_ev_pool_w, m_ev_pool_scale,
        m_ev_q_norm, m_ev_w_q_up, m_ev_kv_norm, m_ev_w_kv_up, m_ev_w_out, m_od_norm, m_od_w_in, m_od_conv_w,
        m_od_conv_b, m_od_w_rgate, m_od_b_rgate, m_od_w_igate, m_od_b_igate, m_od_lambda, m_od_w_out,
        m_xa_norm_x, m_xa_norm_mem, m_xa_w_q, m_xa_w_kv, m_xa_w_o, m_ffn_norm, m_ffn_w_gate_up, m_ffn_w_down,
        m_final_norm, v_ev_norm, v_ev_w_in, v_ev_pool_w, v_ev_pool_scale, v_ev_q_norm, v_ev_w_q_up,
        v_ev_kv_norm, v_ev_w_kv_up, v_ev_w_out, v_od_norm, v_od_w_in, v_od_conv_w, v_od_conv_b, v_od_w_rgate,
        v_od_b_rgate, v_od_w_igate, v_od_b_igate, v_od_lambda, v_od_w_out, v_xa_norm_x, v_xa_norm_mem, v_xa_w_q,
        v_xa_w_kv, v_xa_w_o, v_ffn_norm, v_ffn_w_gate_up, v_ffn_w_down, v_final_norm):
    given = dict(locals())
    w = {n: given[n] for n in WEIGHTS}
    full = _gather_weights(w)
    loss, grad_x, big, grads = _local_step(x[0], mem[0], positions[0], loss_target[0], full)
    grads = {n: grads[n].reshape(full[n].shape) for n in SMALL_SHARDED + REPLICATED}
    mine, other = _reduce_grads(big, grads, w)
    loss = lax.psum(loss, ("x", "y", "c"))
    out = {}
    for i, n in enumerate(MATMUL_WEIGHTS):
        out[n] = _adamw(w[n], mine[i].reshape(w[n].shape), other[i].reshape(w[n].shape), given["m_" + n],
                        given["v_" + n], name=f"adamw_{n}")
    for i, group in ((len(MATMUL_WEIGHTS), SMALL_SHARDED), (len(MATMUL_WEIGHTS) + 1, REPLICATED)):
        packed = [_pack([given[pre + n] for n in group], 8 * LANES) for pre in ("", "m_", "v_")]
        res = _adamw(packed[0], mine[i].reshape(packed[0].shape), other[i].reshape(packed[0].shape), packed[1],
                     packed[2], name=f"adamw_group{i}")
        for j, arrs in enumerate(zip(*[_unpack(r.reshape(-1), [w[n].shape for n in group]) for r in res])):
            out[group[j]] = list(arrs)
    return (loss, grad_x[None], *[out[n][k] for k in range(4) for n in WEIGHTS])
```

```python
import functools
import math

import jax
import jax.numpy as jnp
from jax import lax
from jax.experimental import pallas as pl
from jax.experimental.pallas import tpu as pltpu

F32 = jnp.float32
BF16 = jnp.bfloat16

D_MODEL = 1024
POOL_DIM = 512
POOL_WINDOWS = (2, 4, 8, 16)
POOL_GROUP = 128
MLA_HEADS = 8
QK_NOPE = 64
QK_ROPE = 32
QK_DIM = QK_NOPE + QK_ROPE
V_HEAD = 64
HEAD_PAD = 128
Q_RANK = 256
KV_RANK = 128
ROPE_BASE = 10000.0
LRU_HEADS = 4
LRU_HEAD_DIM = 256
CONV_WIDTH = 4
LRU_C = 8.0
MEM_HEADS = 4
MEM_HEAD_DIM = 256
D_FF = 2816
RMS_EPS = 1e-6
NEG_INF = -1e30

ADAM_LR = 0.001
ADAM_B1 = 0.9
ADAM_B2 = 0.999
ADAM_EPS = 1e-08
ADAM_WD = 0.01
ADAM_STEP = 10

N_CHIPS = 4
LANES = 128
VMEM_LIMIT = 56 * 1024 * 1024
MESH = pl.DeviceIdType.MESH
ANY = pl.BlockSpec(memory_space=pl.ANY)
MIX_DIM = POOL_DIM + MLA_HEADS * HEAD_PAD

NN = (((1,), (0,)), ((), ()))
NT = (((1,), (1,)), ((), ()))
TN = (((0,), (0,)), ((), ()))


def _cp(n):
    return pltpu.CompilerParams(dimension_semantics=("arbitrary",) * n, vmem_limit_bytes=VMEM_LIMIT)


def _dot(a, b, dims=NN):
    return lax.dot_general(a, b, dims, preferred_element_type=F32)


def _rows(ts, w, cb=0):
    return pl.BlockSpec((ts, w), lambda i: (i, cb))


def _const(shape):
    return pl.BlockSpec(shape, lambda i: (0,) * len(shape))


def _mm(a, bs, epi, outs, *, tn, nj, nt=False, extras=(), a_cb=0, k=None, tm=None, name):
    M = a.shape[0]
    k = k or a.shape[1]
    tm = tm or min(M, 512)
    nb, ne = len(bs), len(extras)
    dims = NT if nt else NN

    def body(*refs):
        av = refs[0][...].astype(BF16)
        accs = [_dot(av, r[...].astype(BF16), dims) for r in refs[1:1 + nb]]
        vals = epi(accs, [r[...] for r in refs[1 + nb:1 + nb + ne]])
        for o, v in zip(refs[1 + nb + ne:], vals):
            o[...] = v.astype(o.dtype)

    in_specs = [pl.BlockSpec((tm, k), lambda j, i: (i, a_cb))]
    for (_, rb, cb) in bs:
        if nt:
            in_specs.append(pl.BlockSpec((tn, k), lambda j, i, rb=rb, cb=cb: (rb + j, cb)))
        else:
            in_specs.append(pl.BlockSpec((k, tn), lambda j, i, rb=rb, cb=cb: (rb, cb + j)))
    for (_, cb) in extras:
        in_specs.append(pl.BlockSpec((tm, tn), lambda j, i, cb=cb: (i, cb + j)))
    out_specs = [pl.BlockSpec((tm, tn), lambda j, i, cb=cb: (i, cb + j)) for (_, _, cb) in outs]
    res = pl.pallas_call(
        body, grid=(nj, M // tm), in_specs=in_specs, out_specs=out_specs,
        out_shape=[jax.ShapeDtypeStruct((M, n), dt) for (n, dt, _) in outs],
        compiler_params=_cp(2), name=name,
    )(a, *[b for (b, _, _) in bs], *[e for (e, _) in extras])
    return res


def _first(accs, extras):
    return [accs[0]]


def _add_res(accs, extras):
    return [accs[0] + extras[0].astype(F32)]


def _mm_tn(a, b, *, ka=None, a_cb=0, nb=None, b_cb=0, tk=None, tn=None, ts=None, name):
    S = a.shape[0]
    ka = ka or a.shape[1]
    nb = nb or b.shape[1]
    tk = tk or ka
    tn = tn or nb
    ts = ts or min(S, 512)
    a0, b0 = a_cb * (ka // tk), b_cb * (nb // tn)

    def body(a_ref, b_ref, o_ref):
        @pl.when(pl.program_id(2) == 0)
        def _():
            o_ref[...] = jnp.zeros_like(o_ref)

        o_ref[...] += _dot(a_ref[...].astype(BF16), b_ref[...].astype(BF16), TN)

    return pl.pallas_call(
        body, grid=(ka // tk, nb // tn, S // ts),
        in_specs=[pl.BlockSpec((ts, tk), lambda p, q, s: (s, a0 + p)),
                  pl.BlockSpec((ts, tn), lambda p, q, s: (s, b0 + q))],
        out_specs=pl.BlockSpec((tk, tn), lambda p, q, s: (p, q)),
        out_shape=jax.ShapeDtypeStruct((ka, nb), F32), compiler_params=_cp(3), name=name,
    )(a, b)


def _mm_tn_owners(a, bs, *, name):
    S, ka = a.shape
    nb = sum(b.shape[1] for b in bs)
    tn = nb // N_CHIPS
    ts = min(S, 512)
    per = N_CHIPS // len(bs)

    def body(a_ref, *refs):
        o_ref = refs[-1]
        q = pl.program_id(0)

        @pl.when(pl.program_id(1) == 0)
        def _():
            o_ref[...] = jnp.zeros_like(o_ref)

        av = a_ref[...].astype(BF16)
        for n, b_ref in enumerate(refs[:-1]):
            @pl.when(q // per == n)
            def _():
                o_ref[0] += _dot(av, b_ref[...].astype(BF16), TN)

    in_specs = [pl.BlockSpec((ts, ka), lambda q, s: (s, 0))]
    for n in range(len(bs)):
        in_specs.append(pl.BlockSpec((ts, tn), lambda q, s, n=n: (jnp.where(q // per == n, s, 0),
                                                                  jnp.clip(q - n * per, 0, per - 1))))
    return pl.pallas_call(
        body, grid=(N_CHIPS, S // ts), in_specs=in_specs,
        out_specs=pl.BlockSpec((1, ka, tn), lambda q, s: (q, 0, 0)),
        out_shape=jax.ShapeDtypeStruct((N_CHIPS, ka, tn), F32), compiler_params=_cp(2), name=name,
    )(a, *bs)


def _mm_tn_grouped(a, b, groups, w, *, name):
    S = a.shape[0]
    ts = min(S, 512)

    def body(a_ref, b_ref, o_ref):
        @pl.when(pl.program_id(1) == 0)
        def _():
            o_ref[...] = jnp.zeros_like(o_ref)

        o_ref[0] += _dot(a_ref[...].astype(BF16), b_ref[...].astype(BF16), TN)

    return pl.pallas_call(
        body, grid=(groups, S // ts),
        in_specs=[pl.BlockSpec((ts, w), lambda g, s: (s, g)), pl.BlockSpec((ts, w), lambda g, s: (s, g))],
        out_specs=pl.BlockSpec((1, w, w), lambda g, s: (g, 0, 0)),
        out_shape=jax.ShapeDtypeStruct((groups, w, w), F32), compiler_params=_cp(2), name=name,
    )(a, b)


def _rms(x, g, *, cb=0, w=None, ts=None, name):
    S = x.shape[0]
    w = w or x.shape[1]
    ts = ts or min(S, 512)

    def body(x_ref, g_ref, o_ref):
        xv = x_ref[...].astype(F32)
        r = lax.rsqrt(jnp.mean(xv * xv, axis=-1, keepdims=True) + RMS_EPS)
        o_ref[...] = (xv * r * g_ref[...]).astype(o_ref.dtype)

    return pl.pallas_call(
        body, grid=(S // ts,), in_specs=[_rows(ts, w, cb), _const((1, w))], out_specs=_rows(ts, w),
        out_shape=jax.ShapeDtypeStruct((S, w), BF16), compiler_params=_cp(1), name=name,
    )(x, g.reshape(1, w))


def _rms_bwd(x, g, dy, *, cb=0, w=None, res=None, out_dtype=F32, ts=None, name):
    S = x.shape[0]
    w = w or x.shape[1]
    ts = ts or min(S, 512)
    has_res = res is not None

    def body(*refs):
        x_ref, g_ref, dy_ref = refs[:3]
        dx_ref, dg_ref = refs[-2:]
        xv = x_ref[...].astype(F32)
        r = lax.rsqrt(jnp.mean(xv * xv, axis=-1, keepdims=True) + RMS_EPS)
        n = xv * r
        dyv = dy_ref[...].astype(F32)
        dn = dyv * g_ref[...]
        dx = r * (dn - n * jnp.mean(dn * n, axis=-1, keepdims=True))
        if has_res:
            dx = dx + refs[3][...].astype(F32)
        dx_ref[...] = dx.astype(dx_ref.dtype)

        @pl.when(pl.program_id(0) == 0)
        def _():
            dg_ref[...] = jnp.zeros_like(dg_ref)

        dg_ref[...] += jnp.sum(dyv * n, axis=0, keepdims=True)

    ins = [x, g.reshape(1, w), dy] + ([res] if has_res else [])
    in_specs = [_rows(ts, w, cb), _const((1, w)), _rows(ts, w)] + ([_rows(ts, w)] if has_res else [])
    return pl.pallas_call(
        body, grid=(S // ts,), in_specs=in_specs, out_specs=[_rows(ts, w), _const((1, w))],
        out_shape=[jax.ShapeDtypeStruct((S, w), out_dtype), jax.ShapeDtypeStruct((1, w), F32)],
        compiler_params=_cp(1), name=name,
    )(*ins)


HALO = 16


def _pool_counts(i, ts, rows, first_row):
    t = i * ts + first_row + lax.broadcasted_iota(jnp.int32, (rows, 1), 0)
    return [jnp.minimum(t + 1, w).astype(F32) for w in POOL_WINDOWS]


def _pool_fwd(z, pool_w, pool_scale, *, name):
    S = z.shape[0]
    ts = min(S, 512)
    nh = ts // HALO

    def body(u_ref, halo_ref, w_ref, sc_ref, y_ref, p_ref):
        i = pl.program_id(0)
        u = u_ref[...]
        halo = jnp.where(i > 0, halo_ref[...], 0.0)
        xe = jnp.concatenate([halo, u], axis=0)
        sums = []
        s = xe
        for sh in (1, 2, 4, 8):
            s = s + pltpu.roll(s, sh, 0)
            sums.append(s)
        cnts = _pool_counts(i, ts, ts, 0)
        for g in range(4):
            lo, hi = g * POOL_GROUP, (g + 1) * POOL_GROUP
            pooled = (sums[g][HALO:, lo:hi] / cnts[g] - u[:, lo:hi]).astype(BF16)
            p_ref[:, lo:hi] = pooled
            y_ref[:, lo:hi] = (_dot(pooled, w_ref[g]) * sc_ref[:, lo:hi]).astype(y_ref.dtype)

    return pl.pallas_call(
        body, grid=(S // ts,),
        in_specs=[_rows(ts, POOL_DIM), pl.BlockSpec((HALO, POOL_DIM), lambda i: (jnp.maximum(i * nh - 1, 0), 0)),
                  _const((4, POOL_GROUP, POOL_GROUP)), _const((1, POOL_DIM))],
        out_specs=[_rows(ts, POOL_DIM), _rows(ts, POOL_DIM)],
        out_shape=[jax.ShapeDtypeStruct((S, MIX_DIM), BF16), jax.ShapeDtypeStruct((S, POOL_DIM), BF16)],
        compiler_params=_cp(1), name=name,
    )(z, z, pool_w, pool_scale)


def _pool_bwd(dmix, pooled, pool_w, pool_scale, *, name):
    S = dmix.shape[0]
    ts = min(S, 512)
    nh = ts // HALO
    last = S // HALO - 1

    def body(dy_ref, dyh_ref, p_ref, w_ref, sc_ref, du_ref, dyp_ref, dsc_ref):
        i = pl.program_id(0)
        dyv = dy_ref[...].astype(F32)
        dyh = jnp.where(i < pl.num_programs(0) - 1, dyh_ref[...].astype(F32), 0.0)
        dye = jnp.concatenate([dyv, dyh], axis=0) * sc_ref[...]
        dypre = dye.astype(BF16)
        dyp_ref[...] = dypre[:ts]
        cnts = _pool_counts(i, ts, ts + HALO, 0)
        n = ts + HALO
        dsc = []
        for g in range(4):
            lo, hi = g * POOL_GROUP, (g + 1) * POOL_GROUP
            ypre = _dot(p_ref[:, lo:hi], w_ref[g])
            dsc.append(jnp.sum(dyv[:, lo:hi] * ypre, axis=0, keepdims=True))
            dpool = _dot(dypre[:, lo:hi], w_ref[g], NT)
            s = dpool / cnts[g]
            for sh in (1, 2, 4, 8)[:g + 1]:
                s = s + pltpu.roll(s, n - sh, 0)
            du_ref[:, lo:hi] = (s[:ts] - dpool[:ts]).astype(du_ref.dtype)

        @pl.when(i == 0)
        def _():
            dsc_ref[...] = jnp.zeros_like(dsc_ref)

        dsc_ref[...] += jnp.concatenate(dsc, axis=1)

    return pl.pallas_call(
        body, grid=(S // ts,),
        in_specs=[_rows(ts, POOL_DIM),
                  pl.BlockSpec((HALO, POOL_DIM), lambda i: (jnp.minimum((i + 1) * nh, last), 0)),
                  _rows(ts, POOL_DIM), _const((4, POOL_GROUP, POOL_GROUP)), _const((1, POOL_DIM))],
        out_specs=[_rows(ts, POOL_DIM), _rows(ts, POOL_DIM), _const((1, POOL_DIM))],
        out_shape=[jax.ShapeDtypeStruct((S, POOL_DIM), BF16)] * 2 + [jax.ShapeDtypeStruct((1, POOL_DIM), F32)],
        compiler_params=_cp(1), name=name,
    )(dmix, dmix, pooled, pool_w, pool_scale)


def _rope_partner(t):
    lane = lax.broadcasted_iota(jnp.int32, t.shape, 1)
    swapped = jnp.where(lane < QK_NOPE + QK_ROPE // 2, pltpu.roll(t, HEAD_PAD - QK_ROPE // 2, 1),
                        pltpu.roll(t, QK_ROPE // 2, 1))
    return jnp.where((lane >= QK_NOPE) & (lane < QK_DIM), swapped, 0.0)


def _rope_fwd(q_pad, k_pad, z, ctab, stab, *, name):
    S = q_pad.shape[0]
    ts = min(S, 512)

    def body(q_ref, k_ref, kr_ref, c_ref, s_ref, qo_ref, ko_ref):
        c, s = c_ref[...], s_ref[...]
        q = q_ref[...]
        qo_ref[...] = (q * c + _rope_partner(q) * s).astype(qo_ref.dtype)
        kr = kr_ref[...]
        ko_ref[...] = (k_ref[...] + kr * c + _rope_partner(kr) * s).astype(ko_ref.dtype)

    blk = pl.BlockSpec((ts, HEAD_PAD), lambda i, h: (i, h))
    tab = pl.BlockSpec((ts, HEAD_PAD), lambda i, h: (i, 0))
    return pl.pallas_call(
        body, grid=(S // ts, MLA_HEADS),
        in_specs=[blk, blk, pl.BlockSpec((ts, HEAD_PAD), lambda i, h: (i, 7)), tab, tab],
        out_specs=[blk, blk], out_shape=[jax.ShapeDtypeStruct((S, MLA_HEADS * HEAD_PAD), BF16)] * 2,
        compiler_params=_cp(2), name=name,
    )(q_pad, k_pad, z, ctab, stab)


def _rope_bwd(dq_rot, dk_cat, ctab, stab, *, name):
    S = dq_rot.shape[0]
    ts = min(S, 512)

    def body(dq_ref, dk_ref, c_ref, s_ref, dqo_ref, dkr_ref):
        c, s = c_ref[...], s_ref[...]
        for h in range(MLA_HEADS):
            g = dq_ref[:, h * HEAD_PAD:(h + 1) * HEAD_PAD]
            dqo_ref[:, h * HEAD_PAD:(h + 1) * HEAD_PAD] = (g * c + _rope_partner(g * s)).astype(dqo_ref.dtype)
        dk = dk_ref[...]
        g = dk[:, :HEAD_PAD]
        for h in range(1, MLA_HEADS):
            g = g + dk[:, h * HEAD_PAD:(h + 1) * HEAD_PAD]
        lane = lax.broadcasted_iota(jnp.int32, g.shape, 1)
        on_rope = (lane >= QK_NOPE) & (lane < QK_DIM)
        dkr_ref[...] = jnp.where(on_rope, g * c + _rope_partner(g * s), 0.0).astype(dkr_ref.dtype)

    wide = _rows(ts, MLA_HEADS * HEAD_PAD)
    return pl.pallas_call(
        body, grid=(S // ts,), in_specs=[wide, wide, _rows(ts, HEAD_PAD), _rows(ts, HEAD_PAD)],
        out_specs=[wide, _rows(ts, HEAD_PAD)],
        out_shape=[jax.ShapeDtypeStruct((S, MLA_HEADS * HEAD_PAD), BF16), jax.ShapeDtypeStruct((S, HEAD_PAD), BF16)],
        compiler_params=_cp(1), name=name,
    )(dq_rot, dk_cat, ctab, stab)


ATT_SCALE = QK_DIM ** -0.5
LOG2E = math.log2(math.e)


HEADS_PER_STEP = 2
ATT_COL0 = POOL_DIM // HEAD_PAD


def _stat_rows(col):
    return jnp.broadcast_to(col, (col.shape[0], LANES)).T[0:8]


def _flash_fwd(q, k, v, mix, *, name):
    S = q.shape[0]
    tq = min(S, 512)
    nq = S // tq
    hs = HEADS_PER_STEP
    wide = hs * HEAD_PAD

    def body(q_ref, k_ref, v_ref, mix_ref, o_ref, lse_ref):
        qi = pl.program_id(1)
        qv = [q_ref[:, a * HEAD_PAD:(a + 1) * HEAD_PAD] for a in range(hs)]

        def step(j, carry, masked):
            off = pl.multiple_of(j * tq, tq)
            out = []
            for a in range(hs):
                m, acc = carry[a]
                s = _dot(qv[a], k_ref[pl.ds(off, tq), a * HEAD_PAD:(a + 1) * HEAD_PAD], NT)
                if masked:
                    row = lax.broadcasted_iota(jnp.int32, (tq, tq), 0)
                    col = lax.broadcasted_iota(jnp.int32, (tq, tq), 1)
                    s = jnp.where(col <= row, s, NEG_INF)
                m_new = jnp.maximum(m, jnp.max(s, axis=-1, keepdims=True))
                p = jnp.exp2((s - m_new) * (ATT_SCALE * LOG2E))
                alpha = jnp.exp2((m - m_new) * (ATT_SCALE * LOG2E))
                acc = alpha * acc + _dot(p.astype(BF16), v_ref[pl.ds(off, tq), a * HEAD_PAD:(a + 1) * HEAD_PAD])
                out.append((m_new, acc))
            return tuple(out)

        one = (jnp.full((tq, 1), NEG_INF, F32), jnp.zeros((tq, HEAD_PAD), F32))
        carry = lax.fori_loop(0, qi, lambda j, c: step(j, c, False), (one,) * hs)
        carry = step(qi, carry, True)
        for a in range(hs):
            m, acc = carry[a]
            l = acc[:, V_HEAD:V_HEAD + 1]
            o_ref[:, a * HEAD_PAD:(a + 1) * HEAD_PAD] = (acc / l).astype(o_ref.dtype)
            lse_ref[a] = _stat_rows(m * ATT_SCALE + jnp.log(l))

    blk = pl.BlockSpec((tq, wide), lambda h, i: (i, h))
    full = pl.BlockSpec((S, wide), lambda h, i: (0, h))
    return pl.pallas_call(
        body, grid=(MLA_HEADS // hs, nq), in_specs=[blk, full, full, ANY],
        out_specs=[pl.BlockSpec((tq, wide), lambda h, i: (i, ATT_COL0 // hs + h)),
                   pl.BlockSpec((hs, 8, tq), lambda h, i: (h, i, 0))],
        out_shape=[jax.ShapeDtypeStruct(mix.shape, mix.dtype), jax.ShapeDtypeStruct((MLA_HEADS, nq * 8, tq), F32)],
        input_output_aliases={3: 0}, compiler_params=_cp(2), name=name,
    )(q, k, v, mix)


def _attn_delta(dmix, mix, *, name):
    S = mix.shape[0]
    ts = min(S, 512)
    blk = pl.BlockSpec((ts, HEAD_PAD), lambda i, h: (i, ATT_COL0 + h))

    def body(do_ref, o_ref, d_ref):
        d_ref[0] = _stat_rows(jnp.sum(do_ref[...].astype(F32) * o_ref[...].astype(F32), axis=-1, keepdims=True))

    return pl.pallas_call(
        body, grid=(S // ts, MLA_HEADS), in_specs=[blk, blk],
        out_specs=pl.BlockSpec((1, 8, ts), lambda i, h: (h, i, 0)),
        out_shape=jax.ShapeDtypeStruct((MLA_HEADS, (S // ts) * 8, ts), F32), compiler_params=_cp(2), name=name,
    )(dmix, mix)


def _flash_bwd(q, k, v, dmix, lse_rows, delta_rows, *, name):
    S = q.shape[0]
    tq = min(S, 512)
    nq = S // tq

    def body(q_ref, do_ref, lse_ref, dl_ref, k_ref, v_ref, dq_ref, dk_ref, dv_ref):
        j = pl.program_id(1)

        @pl.when(j == 0)
        def _():
            dq_ref[...] = jnp.zeros_like(dq_ref)

        kv, vv = k_ref[...], v_ref[...]

        def step(i, carry, masked):
            dk, dv = carry
            off = pl.multiple_of(i * tq, tq)
            off8 = pl.multiple_of(i * 8, 8)
            qv = q_ref[pl.ds(off, tq), :]
            dov = do_ref[pl.ds(off, tq), :]
            lse = lse_ref[0, pl.ds(off8, 8), :][0:1]
            dl = dl_ref[0, pl.ds(off8, 8), :][0:1]
            st = _dot(kv, qv, NT) * ATT_SCALE
            if masked:
                krow = lax.broadcasted_iota(jnp.int32, (tq, tq), 0)
                qcol = lax.broadcasted_iota(jnp.int32, (tq, tq), 1)
                st = jnp.where(krow <= qcol, st, NEG_INF)
            pt = jnp.exp(st - lse)
            dv = dv + _dot(pt.astype(BF16), dov)
            dpt = _dot(vv, dov, NT)
            dst = (pt * (dpt - dl) * ATT_SCALE).astype(BF16)
            dk = dk + _dot(dst, qv)
            dq_ref[pl.ds(off, tq), :] += _dot(dst, kv, TN)
            return dk, dv

        zero = jnp.zeros((tq, HEAD_PAD), F32)
        carry = step(j, (zero, zero), True)
        dk, dv = lax.fori_loop(j + 1, nq, lambda i, c: step(i, c, False), carry)
        dk_ref[...] = dk
        dv_ref[...] = dv

    blk = pl.BlockSpec((tq, HEAD_PAD), lambda h, j: (j, h))
    full = pl.BlockSpec((S, HEAD_PAD), lambda h, j: (0, h))
    stat = pl.BlockSpec((1, nq * 8, tq), lambda h, j: (h, 0, 0))
    wide = jax.ShapeDtypeStruct((S, MLA_HEADS * HEAD_PAD), F32)
    return pl.pallas_call(
        body, grid=(MLA_HEADS, nq),
        in_specs=[full, pl.BlockSpec((S, HEAD_PAD), lambda h, j: (0, POOL_DIM // HEAD_PAD + h)), stat, stat, blk, blk],
        out_specs=[full, blk, blk], out_shape=[wide, wide, wide], compiler_params=_cp(2), name=name,
    )(q, dmix, lse_rows, delta_rows, k, v)


MEM_SCALE = MEM_HEAD_DIM ** -0.5


def _xattn_probs(qh, kh):
    s = _dot(qh, kh, NT) * MEM_SCALE
    e = jnp.exp(s - jnp.max(s, axis=-1, keepdims=True))
    return e / jnp.sum(e, axis=-1, keepdims=True)


def _xattn_fwd(q, kvm, *, name):
    S = q.shape[0]
    ts = min(S, 512)
    nm = kvm.shape[0]

    def body(q_ref, kv_ref, o_ref):
        for h in range(MEM_HEADS):
            lo, hi = h * MEM_HEAD_DIM, (h + 1) * MEM_HEAD_DIM
            p = _xattn_probs(q_ref[:, lo:hi], kv_ref[:, lo:hi])
            o_ref[:, lo:hi] = _dot(p.astype(BF16), kv_ref[:, D_MODEL + lo:D_MODEL + hi]).astype(o_ref.dtype)

    return pl.pallas_call(
        body, grid=(S // ts,), in_specs=[_rows(ts, D_MODEL), _const((nm, 2 * D_MODEL))],
        out_specs=_rows(ts, D_MODEL), out_shape=jax.ShapeDtypeStruct((S, D_MODEL), BF16),
        compiler_params=_cp(1), name=name,
    )(q, kvm)


def _xattn_bwd(q, kvm, do, *, name):
    S = q.shape[0]
    ts = min(S, 512)
    nm = kvm.shape[0]

    def body(q_ref, kv_ref, do_ref, dq_ref, dkv_ref):
        @pl.when(pl.program_id(0) == 0)
        def _():
            dkv_ref[...] = jnp.zeros_like(dkv_ref)

        for h in range(MEM_HEADS):
            lo, hi = h * MEM_HEAD_DIM, (h + 1) * MEM_HEAD_DIM
            qh, kh, vh = q_ref[:, lo:hi], kv_ref[:, lo:hi], kv_ref[:, D_MODEL + lo:D_MODEL + hi]
            doh = do_ref[:, lo:hi].astype(BF16)
            p = _xattn_probs(qh, kh)
            dp = _dot(doh, vh, NT)
            ds = (p * (dp - jnp.sum(dp * p, axis=-1, keepdims=True)) * MEM_SCALE).astype(BF16)
            dq_ref[:, lo:hi] = _dot(ds, kh).astype(dq_ref.dtype)
            dkv_ref[:, lo:hi] += _dot(ds, qh, TN)
            dkv_ref[:, D_MODEL + lo:D_MODEL + hi] += _dot(p.astype(BF16), doh, TN)

    return pl.pallas_call(
        body, grid=(S // ts,), in_specs=[_rows(ts, D_MODEL), _const((nm, 2 * D_MODEL)), _rows(ts, D_MODEL)],
        out_specs=[_rows(ts, D_MODEL), _const((nm, 2 * D_MODEL))],
        out_shape=[jax.ShapeDtypeStruct((S, D_MODEL), BF16), jax.ShapeDtypeStruct((nm, 2 * D_MODEL), F32)],
        compiler_params=_cp(1), name=name,
    )(q, kvm, do)


CONV_HALO = 8


def _sigmoid(x):
    return 1.0 / (1.0 + jnp.exp(-x))


def _softplus(x):
    return jnp.maximum(x, 0.0) + jnp.log(1.0 + jnp.exp(-jnp.abs(x)))


def _neg_expm1(x):
    series = -x * (1.0 + x * (1.0 / 2) * (1.0 + x * (1.0 / 3) * (1.0 + x * (1.0 / 4) * (1.0 + x * (1.0 / 5)))))
    return jnp.where(x > -0.05, series, 1.0 - jnp.exp(x))


GELU_C = math.sqrt(2.0 / math.pi)


def _gelu(x):
    return 0.5 * x * (1.0 + jnp.tanh(GELU_C * (x + 0.044715 * x * x * x)))


def _gelu_grad(x):
    t = jnp.tanh(GELU_C * (x + 0.044715 * x * x * x))
    return 0.5 * (1.0 + t) + 0.5 * x * (1.0 - t * t) * GELU_C * (1.0 + 3 * 0.044715 * x * x)


def _lru_gates(xc, wr_ref, br, wi_ref, bi, sp, reset):
    xcb = xc.astype(BF16)
    pr, pi = [], []
    for h in range(LRU_HEADS):
        lo, hi = h * LRU_HEAD_DIM, (h + 1) * LRU_HEAD_DIM
        pr.append(_dot(xcb[:, lo:hi], wr_ref[h]))
        pi.append(_dot(xcb[:, lo:hi], wi_ref[h]))
    r = _sigmoid(jnp.concatenate(pr, axis=1) + br)
    ig = _sigmoid(jnp.concatenate(pi, axis=1) + bi)
    log_a = -LRU_C * r * sp
    a = jnp.where(reset, 0.0, jnp.exp(log_a))
    mult = jnp.where(reset, 1.0, jnp.sqrt(jnp.maximum(_neg_expm1(2.0 * log_a), 0.0)))
    return r, ig, a, mult


def _lru_fwd(z, reset, conv_w, conv_b, w_r, b_r, w_i, b_i, lam, *, name):
    S = z.shape[0]
    ts = min(S, 512)
    nh = ts // CONV_HALO
    W = D_MODEL

    def body(gate_ref, xb_ref, halo_ref, rs_ref, cw_ref, cb_ref, wr_ref, br_ref, wi_ref, bi_ref, lam_ref,
             xc_ref, h_ref, y_ref, a_buf, carry):
        i = pl.program_id(0)

        @pl.when(i == 0)
        def _():
            carry[...] = jnp.zeros_like(carry)

        halo = jnp.where(i > 0, halo_ref[...], 0.0)
        xe = jnp.concatenate([halo, xb_ref[...]], axis=0)
        xc = cb_ref[...] + cw_ref[3:4, :] * xe[CONV_HALO:]
        for kk in range(CONV_WIDTH - 1):
            xc = xc + cw_ref[kk:kk + 1, :] * pltpu.roll(xe, CONV_WIDTH - 1 - kk, 0)[CONV_HALO:]
        xc_ref[...] = xc
        reset = rs_ref[...] > 0.5
        _, ig, a, mult = _lru_gates(xc, wr_ref, br_ref[...], wi_ref, bi_ref[...], _softplus(-lam_ref[...]), reset)
        a_buf[...] = a
        h_ref[...] = mult * (ig * xc)

        def scan(t, h):
            h = a_buf[pl.ds(t, 1), :] * h + h_ref[pl.ds(t, 1), :]
            h_ref[pl.ds(t, 1), :] = h
            return h

        carry[...] = lax.fori_loop(0, ts, scan, carry[...], unroll=8)
        y_ref[...] = (_gelu(gate_ref[...]) * h_ref[...]).astype(y_ref.dtype)

    vec = _const((1, W))
    gw = _const((LRU_HEADS, LRU_HEAD_DIM, LRU_HEAD_DIM))
    return pl.pallas_call(
        body, grid=(S // ts,),
        in_specs=[_rows(ts, W, 0), _rows(ts, W, 1),
                  pl.BlockSpec((CONV_HALO, W), lambda i: (jnp.maximum(i * nh - 1, 0), 1)),
                  _rows(ts, 1), _const((CONV_WIDTH, W)), vec, gw, vec, gw, vec, vec],
        out_specs=[_rows(ts, W)] * 3,
        out_shape=[jax.ShapeDtypeStruct((S, W), F32), jax.ShapeDtypeStruct((S, W), F32),
                   jax.ShapeDtypeStruct((S, W), BF16)],
        scratch_shapes=[pltpu.VMEM((ts, W), F32), pltpu.VMEM((1, W), F32)],
        compiler_params=_cp(1), name=name,
    )(z, z, z, reset, conv_w, conv_b, w_r, b_r, w_i, b_i, lam)


def _lru_bwd(dy, z, xc, hseq, reset, w_r, b_r, w_i, b_i, lam, *, name):
    S = z.shape[0]
    ts = min(S, 512)
    nt = S // ts
    nh = ts // CONV_HALO
    W = D_MODEL

    def body(dy_ref, gate_ref, xc_ref, h_ref, hh_ref, rs_ref, wr_ref, br_ref, wi_ref, bi_ref, lam_ref,
             dg_ref, dxc_ref, dpr_ref, dpi_ref, acc_ref, a_buf, dh_buf, carry):
        i = pl.program_id(0)
        tile = nt - 1 - i

        @pl.when(i == 0)
        def _():
            carry[...] = jnp.zeros_like(carry)
            acc_ref[...] = jnp.zeros_like(acc_ref)

        xc = xc_ref[...]
        lam_v = lam_ref[...]
        sp = _softplus(-lam_v)
        reset = rs_ref[...] > 0.5
        r, ig, a, mult = _lru_gates(xc, wr_ref, br_ref[...], wi_ref, bi_ref[...], sp, reset)
        gate = gate_ref[...]
        dyv = dy_ref[...].astype(F32)
        h = h_ref[...]
        dg_ref[...] = (dyv * h * _gelu_grad(gate)).astype(dg_ref.dtype)
        a_buf[...] = a
        dh_buf[...] = dyv * _gelu(gate)

        def scan(k, c):
            t = ts - 1 - k
            dh = dh_buf[pl.ds(t, 1), :] + c
            dh_buf[pl.ds(t, 1), :] = dh
            return a_buf[pl.ds(t, 1), :] * dh

        carry[...] = lax.fori_loop(0, ts, scan, carry[...], unroll=8)
        dh = dh_buf[...]
        hh = jnp.where(tile > 0, hh_ref[...], 0.0)
        h_prev = pltpu.roll(jnp.concatenate([hh, h], axis=0), 1, 0)[CONV_HALO:]
        da = dh * h_prev
        bx = ig * xc
        dmult = dh * bx
        dbx = dh * mult
        di = dbx * xc
        dlog_a = jnp.where(reset, 0.0, da * a - dmult * a * a / jnp.maximum(mult, 1e-30))
        dr = dlog_a * (-LRU_C) * sp
        dpre_r = dr * r * (1.0 - r)
        dpre_i = di * ig * (1.0 - ig)
        dprb, dpib = dpre_r.astype(BF16), dpre_i.astype(BF16)
        dpr_ref[...] = dprb
        dpi_ref[...] = dpib
        back = []
        for hd in range(LRU_HEADS):
            lo, hi = hd * LRU_HEAD_DIM, (hd + 1) * LRU_HEAD_DIM
            back.append(_dot(dprb[:, lo:hi], wr_ref[hd], NT) + _dot(dpib[:, lo:hi], wi_ref[hd], NT))
        dxc_ref[...] = dbx * ig + jnp.concatenate(back, axis=1)
        dlam = jnp.sum(dlog_a * (-LRU_C) * r, axis=0, keepdims=True) * (-_sigmoid(-lam_v))
        acc_ref[0:1, :] += jnp.sum(dpre_r, axis=0, keepdims=True)
        acc_ref[1:2, :] += jnp.sum(dpre_i, axis=0, keepdims=True)
        acc_ref[2:3, :] += dlam

    rev = lambda cb: pl.BlockSpec((ts, W), lambda i: (nt - 1 - i, cb))
    vec = _const((1, W))
    gw = _const((LRU_HEADS, LRU_HEAD_DIM, LRU_HEAD_DIM))
    return pl.pallas_call(
        body, grid=(nt,),
        in_specs=[rev(0), rev(0), rev(0), rev(0),
                  pl.BlockSpec((CONV_HALO, W), lambda i: (jnp.maximum((nt - 1 - i) * nh - 1, 0), 0)),
                  pl.BlockSpec((ts, 1), lambda i: (nt - 1 - i, 0)), gw, vec, gw, vec, vec],
        out_specs=[rev(0), rev(0), rev(0), rev(0), _const((8, W))],
        out_shape=[jax.ShapeDtypeStruct((S, W), BF16), jax.ShapeDtypeStruct((S, W), F32),
                   jax.ShapeDtypeStruct((S, W), BF16), jax.ShapeDtypeStruct((S, W), BF16),
                   jax.ShapeDtypeStruct((8, W), F32)],
        scratch_shapes=[pltpu.VMEM((ts, W), F32), pltpu.VMEM((ts, W), F32), pltpu.VMEM((1, W), F32)],
        compiler_params=_cp(1), name=name,
    )(dy, z, xc, hseq, hseq, reset, w_r, b_r, w_i, b_i, lam)


def _conv_bwd(dxc, z, conv_w, *, name):
    S = dxc.shape[0]
    ts = min(S, 512)
    nh = ts // CONV_HALO
    last = S // CONV_HALO - 1
    W = D_MODEL
    n = ts + CONV_HALO

    def body(d_ref, dn_ref, xb_ref, xp_ref, cw_ref, dxb_ref, acc_ref):
        i = pl.program_id(0)

        @pl.when(i == 0)
        def _():
            acc_ref[...] = jnp.zeros_like(acc_ref)

        d = d_ref[...]
        de = jnp.concatenate([d, jnp.where(i < pl.num_programs(0) - 1, dn_ref[...], 0.0)], axis=0)
        xe = jnp.concatenate([jnp.where(i > 0, xp_ref[...], 0.0), xb_ref[...]], axis=0)
        dxb = cw_ref[3:4, :] * d
        acc_ref[3:4, :] += jnp.sum(d * xe[CONV_HALO:], axis=0, keepdims=True)
        for kk in range(CONV_WIDTH - 1):
            sh = CONV_WIDTH - 1 - kk
            dxb = dxb + cw_ref[kk:kk + 1, :] * pltpu.roll(de, n - sh, 0)[:ts]
            acc_ref[kk:kk + 1, :] += jnp.sum(d * pltpu.roll(xe, sh, 0)[CONV_HALO:], axis=0, keepdims=True)
        dxb_ref[...] = dxb.astype(dxb_ref.dtype)
        acc_ref[4:5, :] += jnp.sum(d, axis=0, keepdims=True)

    return pl.pallas_call(
        body, grid=(S // ts,),
        in_specs=[_rows(ts, W), pl.BlockSpec((CONV_HALO, W), lambda i: (jnp.minimum((i + 1) * nh, last), 0)),
                  _rows(ts, W, 1), pl.BlockSpec((CONV_HALO, W), lambda i: (jnp.maximum(i * nh - 1, 0), 1)),
                  _const((CONV_WIDTH, W))],
        out_specs=[_rows(ts, W), _const((8, W))],
        out_shape=[jax.ShapeDtypeStruct((S, W), BF16), jax.ShapeDtypeStruct((8, W), F32)],
        compiler_params=_cp(1), name=name,
    )(dxc, dxc, z, z, conv_w)


def _loss_head(x, g, target, *, name):
    S, D = x.shape
    ts = min(S, 512)

    def body(x_ref, g_ref, t_ref, dx_ref, dg_ref, l_ref):
        @pl.when(pl.program_id(0) == 0)
        def _():
            dg_ref[...] = jnp.zeros_like(dg_ref)
            l_ref[...] = jnp.zeros_like(l_ref)

        xv = x_ref[...]
        r = lax.rsqrt(jnp.mean(xv * xv, axis=-1, keepdims=True) + RMS_EPS)
        n = xv * r
        err = n * g_ref[...] - t_ref[...]
        l_ref[...] += 0.5 * jnp.sum(jnp.sum(err * err, axis=-1, keepdims=True) * (1.0 / D), axis=0, keepdims=True)
        dy = err * (1.0 / D)
        dn = dy * g_ref[...]
        dx_ref[...] = r * (dn - n * jnp.mean(dn * n, axis=-1, keepdims=True))
        dg_ref[...] += jnp.sum(dy * n, axis=0, keepdims=True)

    return pl.pallas_call(
        body, grid=(S // ts,), in_specs=[_rows(ts, D), _const((1, D)), _rows(ts, D)],
        out_specs=[_rows(ts, D), _const((1, D)), _const((8, LANES))],
        out_shape=[jax.ShapeDtypeStruct((S, D), F32), jax.ShapeDtypeStruct((1, D), F32),
                   jax.ShapeDtypeStruct((8, LANES), F32)],
        compiler_params=_cp(1), name=name,
    )(x, g.reshape(1, D), target)


def _adamw(w, ga, gb, m, v, *, name):
    shape = w.shape
    cols = shape[-1]
    rows = w.size // cols
    br = rows
    if rows * cols * 4 > (1 << 20):
        br = max(d for d in range(8, rows + 1, 8) if rows % d == 0 and d * cols * 4 <= (1 << 20))

    def body(w_ref, ga_ref, gb_ref, m_ref, v_ref, g_ref, d_ref, mo_ref, vo_ref):
        gv = ga_ref[...] + gb_ref[...]
        g_ref[...] = gv
        mn = ADAM_B1 * m_ref[...] + (1.0 - ADAM_B1) * gv
        vn = ADAM_B2 * v_ref[...] + (1.0 - ADAM_B2) * (gv * gv)
        m_hat = mn / (1.0 - ADAM_B1 ** ADAM_STEP)
        v_hat = vn / (1.0 - ADAM_B2 ** ADAM_STEP)
        d_ref[...] = -ADAM_LR * (m_hat / (jnp.sqrt(v_hat) + ADAM_EPS) + ADAM_WD * w_ref[...])
        mo_ref[...] = mn
        vo_ref[...] = vn

    spec = _rows(br, cols)
    outs = pl.pallas_call(
        body, grid=(rows // br,), in_specs=[spec] * 5, out_specs=[spec] * 4,
        out_shape=[jax.ShapeDtypeStruct((rows, cols), F32)] * 4, compiler_params=_cp(1), name=name,
    )(*[t.reshape(rows, cols) for t in (w, ga, gb, m, v)])
    return [o.reshape(shape) for o in outs]


def _pad_heads(w, width):
    k = w.shape[0]
    return jnp.pad(w.reshape(k, MLA_HEADS, width), ((0, 0), (0, 0), (0, HEAD_PAD - width))).reshape(k, -1)


def _unpad_heads(w, width):
    k = w.shape[0]
    return w.reshape(k, MLA_HEADS, HEAD_PAD)[:, :, :width].reshape(k, MLA_HEADS * width)


def _rope_tables(positions):
    inv_freq = ROPE_BASE ** (-jnp.arange(0, QK_ROPE, 2, dtype=F32) / QK_ROPE)
    ang = positions.astype(F32)[:, None] * inv_freq
    cos, sin = jnp.cos(ang), jnp.sin(ang)
    S = positions.shape[0]
    ones, zeros = jnp.ones((S, QK_NOPE), F32), jnp.zeros((S, QK_NOPE), F32)
    ctab = jnp.concatenate([ones, cos, cos, ones[:, :HEAD_PAD - QK_DIM]], axis=1)
    stab = jnp.concatenate([zeros, -sin, sin, zeros[:, :HEAD_PAD - QK_DIM]], axis=1)
    return ctab, stab


def _memory_block(x, mem, W, layer, tag):
    hx = _rms(x, W["xa_norm_x"][layer], name=f"{tag}_xa_norm")
    qx = _mm(hx, [(W["xa_w_q"][layer], 0, 0)], _first, [(D_MODEL, BF16, 0)], tn=D_MODEL, nj=1, name=f"{tag}_xa_q")[0]
    mn = _rms(mem, W["xa_norm_mem"][layer], name=f"{tag}_xa_norm_mem")
    kvm = _mm(mn, [(W["xa_w_kv"][layer], 0, 0)], _first, [(2 * D_MODEL, BF16, 0)], tn=2 * D_MODEL, nj=1,
              name=f"{tag}_xa_kv")[0]
    o = _xattn_fwd(qx, kvm, name=f"{tag}_xa_attn")
    xo = _mm(o, [(W["xa_w_o"][layer], 0, 0)], _add_res, [(D_MODEL, F32, 0)], extras=[(x, 0)], tn=D_MODEL, nj=1,
             name=f"{tag}_xa_out")[0]
    return xo, (x, hx, qx, mn, kvm, o)


def _memory_block_bwd(dxo, mem, W, layer, saved, tag, grads):
    x, hx, qx, mn, kvm, o = saved
    wq, wkv, wo = W["xa_w_q"][layer], W["xa_w_kv"][layer], W["xa_w_o"][layer]
    do = _mm(dxo, [(wo, 0, 0)], _first, [(D_MODEL, BF16, 0)], nt=True, tn=D_MODEL, nj=1, name=f"{tag}_xa_do")[0]
    grads["xa_w_o"][layer] = _owner_major(_mm_tn(o, dxo, name=f"{tag}_xa_dwo"), 0)
    dqx, dkvm = _xattn_bwd(qx, kvm, do, name=f"{tag}_xa_attn_bwd")
    dhx = _mm(dqx, [(wq, 0, 0)], _first, [(D_MODEL, F32, 0)], nt=True, tn=D_MODEL, nj=1, name=f"{tag}_xa_dhx")[0]
    grads["xa_w_q"][layer] = _owner_major(_mm_tn(hx, dqx, name=f"{tag}_xa_dwq"), 0)
    dx, dg = _rms_bwd(x, W["xa_norm_x"][layer], dhx, res=dxo, name=f"{tag}_xa_norm_bwd")
    grads["xa_norm_x"][layer] = dg[0]
    dmn = _mm(dkvm, [(wkv, 0, 0)], _first, [(D_MODEL, F32, 0)], nt=True, tn=D_MODEL, nj=1, name=f"{tag}_xa_dmn")[0]
    grads["xa_w_kv"][layer] = _mm_tn_owners(mn, [dkvm], name=f"{tag}_xa_dwkv")
    _, dgm = _rms_bwd(mem, W["xa_norm_mem"][layer], dmn, name=f"{tag}_xa_norm_mem_bwd")
    grads["xa_norm_mem"][layer] = dgm[0]
    return dx


FF_TN = D_FF // 2


def _silu_mul(accs, extras):
    g, u = accs
    return [g * _sigmoid(g) * u, g, u]


def _silu_mul_bwd(accs, extras):
    da = accs[0]
    g, u = extras[0].astype(F32), extras[1].astype(F32)
    sg = _sigmoid(g)
    return [da * u * sg * (1.0 + g * (1.0 - sg)), da * g * sg]


def _ffn_block(x, W, layer, tag):
    hf = _rms(x, W["ffn_norm"][layer], name=f"{tag}_ffn_norm")
    wgu, wd = W["ffn_w_gate_up"][layer], W["ffn_w_down"][layer]
    act, g, u = _mm(hf, [(wgu, 0, 0), (wgu, 0, 2)], _silu_mul, [(D_FF, BF16, 0)] * 3, tn=FF_TN, nj=2,
                    name=f"{tag}_ffn_up")
    xo = _mm(act, [(wd, 0, 0)], _add_res, [(D_MODEL, F32, 0)], extras=[(x, 0)], tn=D_MODEL, nj=1,
             name=f"{tag}_ffn_down")[0]
    return xo, (x, hf, act, g, u)


def _ffn_block_bwd(dxo, W, layer, saved, tag, grads):
    x, hf, act, g, u = saved
    wgu, wd = W["ffn_w_gate_up"][layer], W["ffn_w_down"][layer]
    dg, du = _mm(dxo, [(wd, 0, 0)], _silu_mul_bwd, [(D_FF, BF16, 0)] * 2, nt=True, extras=[(g, 0), (u, 0)], tn=FF_TN,
                 nj=2, name=f"{tag}_ffn_dact")
    grads["ffn_w_down"][layer] = _owner_major(_mm_tn(act, dxo, tk=FF_TN, name=f"{tag}_ffn_dwd"), 0)
    dhf = _mm(dg, [(wgu, 0, 0)], _first, [(D_MODEL, F32, 0)], nt=True, tn=D_MODEL, nj=1, name=f"{tag}_ffn_dhf_g")[0]
    dhf = _mm(du, [(wgu, 0, 1)], _add_res, [(D_MODEL, F32, 0)], nt=True, extras=[(dhf, 0)], tn=D_MODEL, nj=1,
              name=f"{tag}_ffn_dhf_u")[0]
    grads["ffn_w_gate_up"][layer] = _mm_tn_owners(hf, [dg, du], name=f"{tag}_ffn_dwgu")
    dx, dgn = _rms_bwd(x, W["ffn_norm"][layer], dhf, res=dxo, name=f"{tag}_ffn_norm_bwd")
    grads["ffn_norm"][layer] = dgn[0]
    return dx


def _keys_and_values(accs, extras):
    k, v = accs
    lane = lax.broadcasted_iota(jnp.int32, v.shape, 1)
    return [k, jnp.where(lane % HEAD_PAD == V_HEAD, 1.0, v)]


def _even_block(x, tabs, W, tag):
    ctab, stab = tabs
    w_in = W["ev_w_in"][0]
    zero = jnp.zeros((D_MODEL, QK_NOPE), BF16)
    w_in_pad = jnp.concatenate([w_in[:, :896], zero, w_in[:, 896:], zero[:, :HEAD_PAD - QK_DIM]], axis=1)
    w_q_pad = _pad_heads(W["ev_w_q_up"][0], QK_DIM)
    wkv = W["ev_w_kv_up"][0].reshape(KV_RANK, MLA_HEADS, QK_NOPE + V_HEAD)
    w_kv_pad = jnp.concatenate([_pad_heads(wkv[:, :, :QK_NOPE].reshape(KV_RANK, -1), QK_NOPE),
                                _pad_heads(wkv[:, :, QK_NOPE:].reshape(KV_RANK, -1), V_HEAD)], axis=1)
    w_out = W["ev_w_out"][0]
    w_att = jnp.pad(w_out[POOL_DIM:].reshape(MLA_HEADS, V_HEAD, D_MODEL), ((0, 0), (0, HEAD_PAD - V_HEAD), (0, 0)))
    w_out_pad = jnp.concatenate([w_out[:POOL_DIM], w_att.reshape(MLA_HEADS * HEAD_PAD, D_MODEL)], axis=0)
    pool_w = W["ev_pool_w"][0].astype(BF16)
    pool_scale = W["ev_pool_scale"]

    h = _rms(x, W["ev_norm"][0], name=f"{tag}_norm")
    z = _mm(h, [(w_in_pad, 0, 0)], _first, [(D_MODEL, F32, 0)], tn=D_MODEL, nj=1, name=f"{tag}_in")[0]
    mix, pooled = _pool_fwd(z, pool_w, pool_scale, name=f"{tag}_pool")
    cqn = _rms(z, W["ev_q_norm"][0], cb=2, w=Q_RANK, name=f"{tag}_q_norm")
    ckvn = _rms(z, W["ev_kv_norm"][0], cb=6, w=KV_RANK, name=f"{tag}_kv_norm")
    q_pad = _mm(cqn, [(w_q_pad, 0, 0)], _first, [(D_MODEL, F32, 0)], tn=D_MODEL, nj=1, name=f"{tag}_q_up")[0]
    k_pad, v_pad = _mm(ckvn, [(w_kv_pad, 0, 0), (w_kv_pad, 0, 1)], _keys_and_values,
                       [(D_MODEL, F32, 0), (D_MODEL, BF16, 0)], tn=D_MODEL, nj=1, name=f"{tag}_kv_up")
    q_rot, k_cat = _rope_fwd(q_pad, k_pad, z, ctab, stab, name=f"{tag}_rope")
    mix, lse = _flash_fwd(q_rot, k_cat, v_pad, mix, name=f"{tag}_attn")
    xo = _mm(mix, [(w_out_pad, 0, 0)], _add_res, [(D_MODEL, F32, 0)], extras=[(x, 0)], tn=D_MODEL, nj=1,
             name=f"{tag}_out")[0]
    saved = (x, h, z, pooled, cqn, ckvn, q_rot, k_cat, v_pad, lse, mix,
             (w_in_pad, w_q_pad, w_kv_pad, w_out_pad, pool_w, pool_scale))
    return xo, saved


def _even_block_bwd(dxo, tabs, W, saved, tag, grads, token=None):
    ctab, stab = tabs
    x, h, z, pooled, cqn, ckvn, q_rot, k_cat, v_pad, lse, mix, wts = saved
    w_in_pad, w_q_pad, w_kv_pad, w_out_pad, pool_w, pool_scale = wts
    if token is not None:
        w_out_pad = w_out_pad + token[0:1, 0:1].astype(BF16)
    dmix = _mm(dxo, [(w_out_pad, 0, 0)], _first, [(MIX_DIM, BF16, 0)], nt=True, tn=MIX_DIM, nj=1,
               name=f"{tag}_dmix")[0]
    dw_out_pad = _mm_tn(mix, dxo, tk=MIX_DIM // 3, name=f"{tag}_dw_out")
    datt = dw_out_pad[POOL_DIM:].reshape(MLA_HEADS, HEAD_PAD, D_MODEL)[:, :V_HEAD].reshape(-1, D_MODEL)
    grads["ev_w_out"] = [_owner_major(jnp.concatenate([dw_out_pad[:POOL_DIM], datt], axis=0), 0)]
    delta = _attn_delta(dmix, mix, name=f"{tag}_delta")
    dq_rot, dk_cat, dv_pad = _flash_bwd(q_rot, k_cat, v_pad, dmix, lse, delta, name=f"{tag}_attn_bwd")
    dq_pad, dkr = _rope_bwd(dq_rot, dk_cat, ctab, stab, name=f"{tag}_rope_bwd")
    dw_q_pad = _mm_tn(cqn, dq_pad, name=f"{tag}_dw_q_up")
    grads["ev_w_q_up"] = [_owner_major(_unpad_heads(dw_q_pad, QK_DIM), 1)]
    dcqn = _mm(dq_pad, [(w_q_pad, 0, 0)], _first, [(Q_RANK, F32, 0)], nt=True, tn=Q_RANK, nj=1, name=f"{tag}_dcqn")[0]
    dwk = _unpad_heads(_mm_tn(ckvn, dk_cat, name=f"{tag}_dw_k_up"), QK_NOPE).reshape(KV_RANK, MLA_HEADS, QK_NOPE)
    dwv = _unpad_heads(_mm_tn(ckvn, dv_pad, name=f"{tag}_dw_v_up"), V_HEAD).reshape(KV_RANK, MLA_HEADS, V_HEAD)
    grads["ev_w_kv_up"] = [_owner_major(jnp.concatenate([dwk, dwv], axis=2).reshape(KV_RANK, -1), 1)]
    dckvn = _mm(dk_cat, [(w_kv_pad, 0, 0)], _first, [(KV_RANK, F32, 0)], nt=True, tn=KV_RANK, nj=1,
                name=f"{tag}_dckvn_k")[0]
    dckvn = _mm(dv_pad, [(w_kv_pad, 0, 1)], _add_res, [(KV_RANK, F32, 0)], nt=True, extras=[(dckvn, 0)], tn=KV_RANK,
                nj=1, name=f"{tag}_dckvn_v")[0]
    dcq, dgq = _rms_bwd(z, W["ev_q_norm"][0], dcqn, cb=2, w=Q_RANK, out_dtype=BF16, name=f"{tag}_q_norm_bwd")
    dckv, dgkv = _rms_bwd(z, W["ev_kv_norm"][0], dckvn, cb=6, w=KV_RANK, out_dtype=BF16, name=f"{tag}_kv_norm_bwd")
    grads["ev_q_norm"], grads["ev_kv_norm"] = dgq, dgkv
    du, dypre, dscale = _pool_bwd(dmix, pooled, pool_w, pool_scale, name=f"{tag}_pool_bwd")
    grads["ev_pool_scale"] = dscale
    grads["ev_pool_w"] = _mm_tn_grouped(pooled, dypre, 4, POOL_GROUP, name=f"{tag}_dpool_w")[None]
    dz = jnp.concatenate([du, dcq, dckv, dkr], axis=1)
    dw_in_pad = _mm_tn(h, dz, name=f"{tag}_dw_in")
    grads["ev_w_in"] = [_owner_major(jnp.concatenate([dw_in_pad[:, :896], dw_in_pad[:, 960:992]], axis=1), 0)]
    dh = _mm(dz, [(w_in_pad, 0, 0)], _first, [(D_MODEL, F32, 0)], nt=True, tn=D_MODEL, nj=1, name=f"{tag}_dh")[0]
    dx, dgn = _rms_bwd(x, W["ev_norm"][0], dh, res=dxo, name=f"{tag}_norm_bwd")
    grads["ev_norm"] = dgn
    return dx


def _odd_block(x, reset, W, tag):
    h = _rms(x, W["od_norm"][0], name=f"{tag}_norm")
    z = _mm(h, [(W["od_w_in"][0], 0, 0)], _first, [(2 * D_MODEL, F32, 0)], tn=D_MODEL, nj=2, name=f"{tag}_in")[0]
    w_r, w_i = W["od_w_rgate"][0], W["od_w_igate"][0]
    vecs = [W[n].reshape(1, D_MODEL) for n in ("od_conv_b", "od_b_rgate", "od_b_igate", "od_lambda")]
    xc, hseq, y = _lru_fwd(z, reset, W["od_conv_w"][0], vecs[0], w_r, vecs[1], w_i, vecs[2], vecs[3],
                           name=f"{tag}_lru")
    xo = _mm(y, [(W["od_w_out"][0], 0, 0)], _add_res, [(D_MODEL, F32, 0)], extras=[(x, 0)], tn=D_MODEL, nj=1,
             name=f"{tag}_out")[0]
    return xo, (x, h, z, xc, hseq, y, vecs)


def _odd_block_bwd(dxo, reset, W, saved, tag, grads):
    x, h, z, xc, hseq, y, vecs = saved
    w_r, w_i = W["od_w_rgate"][0], W["od_w_igate"][0]
    dy = _mm(dxo, [(W["od_w_out"][0], 0, 0)], _first, [(D_MODEL, F32, 0)], nt=True, tn=D_MODEL, nj=1,
             name=f"{tag}_dy")[0]
    grads["od_w_out"] = [_owner_major(_mm_tn(y, dxo, name=f"{tag}_dw_out"), 0)]
    dgate, dxc, dpr, dpi, acc = _lru_bwd(dy, z, xc, hseq, reset, w_r, vecs[1], w_i, vecs[2], vecs[3],
                                         name=f"{tag}_lru_bwd")
    grads["od_b_rgate"], grads["od_b_igate"], grads["od_lambda"] = acc[0:1], acc[1:2], acc[2:3]
    grads["od_w_rgate"] = [_owner_major(_mm_tn_grouped(xc, dpr, LRU_HEADS, LRU_HEAD_DIM, name=f"{tag}_dw_rgate"), 1)]
    grads["od_w_igate"] = [_owner_major(_mm_tn_grouped(xc, dpi, LRU_HEADS, LRU_HEAD_DIM, name=f"{tag}_dw_igate"), 1)]
    dxb, cacc = _conv_bwd(dxc, z, W["od_conv_w"][0], name=f"{tag}_conv_bwd")
    grads["od_conv_w"], grads["od_conv_b"] = cacc[None, 0:4], cacc[4:5]
    dz = jnp.concatenate([dgate, dxb], axis=1)
    grads["od_w_in"] = [_mm_tn_owners(h, [dz], name=f"{tag}_dw_in")]
    dh = _mm(dz, [(W["od_w_in"][0], 0, 0)], _first, [(D_MODEL, F32, 0)], nt=True, tn=D_MODEL, nj=1,
             name=f"{tag}_dh")[0]
    dx, dgn = _rms_bwd(x, W["od_norm"][0], dh, res=dxo, name=f"{tag}_norm_bwd")
    grads["od_norm"] = dgn
    return dx


def _local_step(x, mem, positions, target, W, later_weights=None, exchange_earlier=None):
    tabs = _rope_tables(positions)
    reset = (positions == 0).astype(F32)[:, None]
    grads = {n: [None, None] for n in ("xa_norm_x", "xa_norm_mem", "xa_w_q", "xa_w_kv", "xa_w_o", "ffn_norm",
                                       "ffn_w_gate_up", "ffn_w_down")}
    x1, s_even = _even_block(x, tabs, W, "l0_even")
    if later_weights is not None:
        W = {**W, **later_weights(x1)}
    x2, s_xa0 = _memory_block(x1, mem, W, 0, "l0")
    x3, s_ff0 = _ffn_block(x2, W, 0, "l0")
    x4, s_odd = _odd_block(x3, reset, W, "l1_odd")
    x5, s_xa1 = _memory_block(x4, mem, W, 1, "l1")
    x6, s_ff1 = _ffn_block(x5, W, 1, "l1")
    d, dgf, loss = _loss_head(x6, W["final_norm"], target, name="loss_head")
    grads["final_norm"] = dgf[0]
    d = _ffn_block_bwd(d, W, 1, s_ff1, "l1", grads)
    d = _memory_block_bwd(d, mem, W, 1, s_xa1, "l1", grads)
    d = _odd_block_bwd(d, reset, W, s_odd, "l1_odd", grads)
    d = _ffn_block_bwd(d, W, 0, s_ff0, "l0", grads)
    d = _memory_block_bwd(d, mem, W, 0, s_xa0, "l0", grads)
    token = exchange_earlier(grads) if exchange_earlier is not None else None
    d = _even_block_bwd(d, tabs, W, s_even, "l0_even", grads, token)
    big = {n: grads.pop(n) for n in MATMUL_WEIGHTS}
    for n, v in grads.items():
        if isinstance(v, list):
            grads[n] = jnp.stack(v)
    return loss[0, 0], d, big, grads


WEIGHTS = ("ev_norm", "ev_w_in", "ev_pool_w", "ev_pool_scale", "ev_q_norm", "ev_w_q_up", "ev_kv_norm", "ev_w_kv_up",
           "ev_w_out", "od_norm", "od_w_in", "od_conv_w", "od_conv_b", "od_w_rgate", "od_b_rgate", "od_w_igate",
           "od_b_igate", "od_lambda", "od_w_out", "xa_norm_x", "xa_norm_mem", "xa_w_q", "xa_w_kv", "xa_w_o",
           "ffn_norm", "ffn_w_gate_up", "ffn_w_down", "final_norm")
SHARD_AXIS = {"ev_w_in": 1, "ev_w_q_up": 2, "ev_w_kv_up": 2, "ev_w_out": 1, "od_norm": 1, "od_w_in": 2,
              "od_conv_w": 2, "od_conv_b": 1, "od_w_rgate": 2, "od_b_rgate": 1, "od_w_igate": 2, "od_b_igate": 1,
              "od_lambda": 1, "od_w_out": 1, "xa_w_q": 1, "xa_w_kv": 2, "xa_w_o": 1, "ffn_w_gate_up": 2,
              "ffn_w_down": 1}
MATMUL_WEIGHTS = ("ev_w_in", "ev_w_q_up", "ev_w_kv_up", "ev_w_out", "od_w_in", "od_w_rgate", "od_w_igate",
                  "od_w_out", "xa_w_q", "xa_w_kv", "xa_w_o", "ffn_w_gate_up", "ffn_w_down")
SMALL_SHARDED = tuple(n for n in WEIGHTS if n in SHARD_AXIS and n not in MATMUL_WEIGHTS)
REPLICATED = tuple(n for n in WEIGHTS if n not in SHARD_AXIS)


def _pack(parts, quantum):
    flat = jnp.concatenate([p.reshape(-1) for p in parts])
    pad = (-flat.shape[0]) % quantum
    return jnp.pad(flat, (0, pad)).reshape(-1, LANES)


def _unpack(flat, shapes):
    out, off = [], 0
    for shape in shapes:
        size = math.prod(shape)
        out.append(flat[off:off + size].reshape(shape))
        off += size
    return out


def _run_copies(local, remote, send_sems, recv_sems, local_sems):
    locals_ = [pltpu.make_async_copy(src, dst, local_sems.at[n]) for n, (src, dst) in enumerate(local)]
    for cp in locals_:
        cp.start()
    sends = [pltpu.make_async_remote_copy(src_ref=src, dst_ref=dst, send_sem=send_sems.at[k, n],
                                          recv_sem=recv_sems.at[k, n], device_id=dev, device_id_type=MESH)
             for (k, n, src, dst, _, dev) in remote]
    for cp in sends:
        cp.start()
    for (k, n, src, _, arrival, dev) in remote:
        pltpu.make_async_remote_copy(src_ref=src, dst_ref=arrival, send_sem=send_sems.at[k, n],
                                     recv_sem=recv_sems.at[k, n], device_id=dev, device_id_type=MESH).wait_recv()
    for cp in sends:
        cp.wait_send()
    for cp in locals_:
        cp.wait()


def _chip_peers(x, y):
    return [(1 - x, y), (x, 1 - y), (1 - x, 1 - y)]


def _owner_block(ref, axis, q):
    size = ref.shape[axis] // N_CHIPS
    idx = [slice(None)] * len(ref.shape)
    idx[axis] = pl.ds(q * size, size)
    return ref.at[tuple(idx)]


def _comm_call(body, ins, out_shapes, n_items, n_peers, *, name):
    return pl.pallas_call(
        body, in_specs=[ANY] * len(ins), out_specs=[ANY] * len(out_shapes), out_shape=out_shapes,
        scratch_shapes=[pltpu.SemaphoreType.DMA((n_peers, n_items)), pltpu.SemaphoreType.DMA((n_peers, n_items)),
                        pltpu.SemaphoreType.DMA((n_items,))],
        name=name,
    )(*ins)


def _gather_chips(shards, axes, *, name):
    n = len(shards)
    full = [jax.ShapeDtypeStruct(tuple(d * (N_CHIPS if a == ax else 1) for a, d in enumerate(s.shape)), s.dtype)
            for s, ax in zip(shards, axes)]

    def body(*refs):
        srcs, dsts = refs[:n], refs[n:2 * n]
        x, y, c = lax.axis_index("x"), lax.axis_index("y"), lax.axis_index("c")
        me = 2 * x + y
        local = [(srcs[i], _owner_block(dsts[i], axes[i], me)) for i in range(n)]
        remote = [(k, i, srcs[i], _owner_block(dsts[i], axes[i], me), _owner_block(dsts[i], axes[i], 2 * px + py),
                   (px, py, c))
                  for k, (px, py) in enumerate(_chip_peers(x, y)) for i in range(n)]
        _run_copies(local, remote, *refs[2 * n:])

    return _comm_call(body, shards, full, n, 3, name=name)


def _exchange_chips(items, *, name):
    flat = [(n, l, a) for n, layers in enumerate(items) for l, a in enumerate(layers)]
    outs = [jax.ShapeDtypeStruct((N_CHIPS, len(layers)) + layers[0].shape[1:], layers[0].dtype) for layers in items]
    ni = len(flat)

    def body(*refs):
        srcs, dsts = refs[:ni], refs[ni:ni + len(items)]
        x, y, c = lax.axis_index("x"), lax.axis_index("y"), lax.axis_index("c")
        me = 2 * x + y
        local = [(srcs[i].at[me], dsts[n].at[me, l]) for i, (n, l, _) in enumerate(flat)]
        remote = [(k, i, srcs[i].at[2 * px + py], dsts[n].at[me, l], dsts[n].at[2 * px + py, l], (px, py, c))
                  for k, (px, py) in enumerate(_chip_peers(x, y)) for i, (n, l, _) in enumerate(flat)]
        _run_copies(local, remote, *refs[ni + len(items):])

    return _comm_call(body, [a for (_, _, a) in flat], outs, ni, 3, name=name)


HBM = pl.BlockSpec(memory_space=pltpu.HBM)
SEM = pl.BlockSpec(memory_space=pltpu.SEMAPHORE)
DATAFLOW = pltpu.SideEffectType.DATAFLOW_SIDE_EFFECTING


def _gather_plan(axes):
    return lambda srcs, lands, me, peer: [
        (srcs[i], _owner_block(lands[i], ax, me), _owner_block(lands[i], ax, peer)) for i, ax in enumerate(axes)]


def _exchange_plan(where):
    return lambda srcs, lands, me, peer: [
        (srcs[i].at[peer], lands[n].at[me, l], lands[n].at[peer, l]) for i, (n, l) in enumerate(where)]


def _split_start(srcs, lands, plan, *, name):
    ns, nl = len(srcs), len(lands)
    nsem = 3 * len(plan(list(srcs), list(lands), 0, 0))

    def body(*refs):
        src_refs, land_refs = refs[:ns], refs[ns:ns + nl]
        send_sems, recv_sems = refs[ns + nl:ns + nl + nsem], refs[ns + nl + nsem:ns + nl + 2 * nsem]
        x, y, c = lax.axis_index("x"), lax.axis_index("y"), lax.axis_index("c")
        n = 0
        for px, py in _chip_peers(x, y):
            for src, dst, _ in plan(src_refs, land_refs, 2 * x + y, 2 * px + py):
                pltpu.make_async_remote_copy(src_ref=src, dst_ref=dst, send_sem=send_sems[n], recv_sem=recv_sems[n],
                                             device_id=(px, py, c), device_id_type=MESH).start()
                n += 1
        refs[-1][...] = jnp.zeros_like(refs[-1])

    arrays = list(srcs) + list(lands)
    out = pl.pallas_call(
        body, name=name, in_specs=[HBM] * (ns + nl),
        out_specs=[SEM] * (2 * nsem) + [HBM] * (ns + nl) + [pl.BlockSpec(memory_space=pltpu.VMEM)],
        out_shape=[pltpu.SemaphoreType.DMA(())] * (2 * nsem) + [pltpu.HBM(a.shape, a.dtype) for a in arrays]
        + [jax.ShapeDtypeStruct((8, LANES), F32)],
        input_output_aliases={i: 2 * nsem + i for i in range(ns + nl)},
        compiler_params=pltpu.CompilerParams(has_side_effects=DATAFLOW),
    )(*[pltpu.with_memory_space_constraint(a, pltpu.HBM) for a in arrays])
    sems, rest = out[:2 * nsem], out[2 * nsem:]
    return sems[:nsem], sems[nsem:], rest[:ns], rest[ns:ns + nl], rest[-1]


def _split_wait(handle, after, plan, *, name):
    send_sems, recv_sems, srcs, lands, _ = handle
    ns, nl, nsem = len(srcs), len(lands), len(send_sems)

    def body(*refs):
        src_refs, land_refs = refs[:ns], refs[ns:ns + nl]
        send_refs, recv_refs = refs[ns + nl:ns + nl + nsem], refs[ns + nl + nsem:ns + nl + 2 * nsem]
        x, y, c = lax.axis_index("x"), lax.axis_index("y"), lax.axis_index("c")
        n = 0
        for px, py in _chip_peers(x, y):
            for src, _, arrival in plan(src_refs, land_refs, 2 * x + y, 2 * px + py):
                cp = pltpu.make_async_remote_copy(src_ref=src, dst_ref=arrival, send_sem=send_refs[n],
                                                  recv_sem=recv_refs[n], device_id=(px, py, c), device_id_type=MESH)
                cp.wait_send()
                cp.wait_recv()
                n += 1

    out = pl.pallas_call(
        body, name=name, in_specs=[HBM] * (ns + nl) + [SEM] * (2 * nsem) + [ANY], out_specs=[HBM] * (ns + nl),
        out_shape=[pltpu.HBM(a.shape, a.dtype) for a in list(srcs) + list(lands)],
        input_output_aliases={i: i for i in range(ns + nl)},
        compiler_params=pltpu.CompilerParams(has_side_effects=DATAFLOW),
    )(*srcs, *lands, *send_sems, *recv_sems, after)
    return out[ns:]


def _exchange_sibling(arrays, *, name):
    n = len(arrays)

    def body(*refs):
        x, y, c = lax.axis_index("x"), lax.axis_index("y"), lax.axis_index("c")
        remote = [(0, i, refs[i], refs[n + i], refs[n + i], (x, y, 1 - c)) for i in range(n)]
        _run_copies([], remote, *refs[2 * n:])

    return _comm_call(body, arrays, [jax.ShapeDtypeStruct(a.shape, a.dtype) for a in arrays], n, 1, name=name)


def _sum_slots(r, *, name):
    shape = r.shape[1:]
    cols = shape[-1]
    rows = math.prod(shape) // cols
    tr = max(d for d in range(8, rows + 1, 8) if rows % d == 0 and d * cols * 16 <= (4 << 20))

    def body(r_ref, o_ref):
        o_ref[...] = ((r_ref[0] + r_ref[1]) + r_ref[2]) + r_ref[3]

    return pl.pallas_call(
        body, grid=(rows // tr,), in_specs=[pl.BlockSpec((N_CHIPS, tr, cols), lambda i: (0, i, 0))],
        out_specs=_rows(tr, cols), out_shape=jax.ShapeDtypeStruct((rows, cols), F32), compiler_params=_cp(1),
        name=name,
    )(r.reshape(N_CHIPS, rows, cols)).reshape(shape)


FIRST_WEIGHTS = ("ev_w_in", "ev_w_q_up", "ev_w_kv_up", "ev_w_out")
LATER_WEIGHTS = tuple(n for n in MATMUL_WEIGHTS if n not in FIRST_WEIGHTS)
LAST_GRADS = FIRST_WEIGHTS
EARLIER_GRADS = tuple(n for n in MATMUL_WEIGHTS if n not in LAST_GRADS)


def _my_chip():
    return 2 * lax.axis_index("x") + lax.axis_index("y")


def _gather_first(w):
    small = _pack([w[n] for n in SMALL_SHARDED], 8 * LANES)
    stacked = [n for n in FIRST_WEIGHTS if SHARD_AXIS[n] == w[n].ndim - 1 and w[n].shape[-1] % LANES]
    shards = [w[n].astype(BF16)[None] if n in stacked else w[n].astype(BF16) for n in FIRST_WEIGHTS]
    got = _gather_chips(shards + [small], [0 if n in stacked else SHARD_AXIS[n] for n in FIRST_WEIGHTS] + [0],
                        name="gather_first")
    full = {n: w[n] for n in REPLICATED}
    for n, g in zip(FIRST_WEIGHTS, got[:-1]):
        full[n] = jnp.concatenate([g[q] for q in range(N_CHIPS)], axis=SHARD_AXIS[n]) if n in stacked else g
    per_chip = [_unpack(got[-1][q * small.shape[0]:(q + 1) * small.shape[0]].reshape(-1),
                        [w[n].shape for n in SMALL_SHARDED]) for q in range(N_CHIPS)]
    for i, n in enumerate(SMALL_SHARDED):
        full[n] = jnp.concatenate([per_chip[q][i] for q in range(N_CHIPS)], axis=SHARD_AXIS[n])
    return full


def _gather_later_start(w):
    shards = [w[n].astype(BF16) for n in LATER_WEIGHTS]
    axes = [SHARD_AXIS[n] for n in LATER_WEIGHTS]
    lands = []
    for s, ax in zip(shards, axes):
        shape = tuple(d * (N_CHIPS if a == ax else 1) for a, d in enumerate(s.shape))
        lands.append(lax.dynamic_update_slice_in_dim(lax.empty(shape, s.dtype), s, _my_chip() * s.shape[ax], ax))
    return _split_start(shards, lands, _gather_plan(axes), name="gather_later_start"), _gather_plan(axes)


def _owner_major(g, axis):
    shape = g.shape
    size = shape[axis] // N_CHIPS
    g = jnp.moveaxis(g.reshape(shape[:axis] + (N_CHIPS, size) + shape[axis + 1:]), axis, 0)
    return g.reshape(N_CHIPS, -1, shape[-1] if axis < len(shape) - 1 else size)


def _exchange_earlier_start(grads, full_shapes):
    small = [_pack([jnp.split(grads[n].reshape(full_shapes[n]), N_CHIPS, axis=SHARD_AXIS[n])[q]
                    for n in SMALL_SHARDED], 8 * LANES) for q in range(N_CHIPS)]
    items = [grads[n] for n in EARLIER_GRADS] + [[jnp.stack(small)]]
    me = _my_chip()
    srcs, lands, where = [], [], []
    for n, layers in enumerate(items):
        land = lax.empty((N_CHIPS, len(layers)) + layers[0].shape[1:], layers[0].dtype)
        for l, a in enumerate(layers):
            own = lax.dynamic_index_in_dim(a, me, 0, keepdims=True)[:, None]
            land = lax.dynamic_update_slice(land, own, (me, l) + (0,) * (a.ndim - 1))
            srcs.append(a)
            where.append((n, l))
        lands.append(land)
    plan = _exchange_plan(where)
    return _split_start(srcs, lands, plan, name="exchange_earlier_start"), plan


def _exchange_last(big, grads, full_shapes):
    repl = _pack([grads[n].reshape(full_shapes[n]) for n in REPLICATED], 8 * LANES)
    return _exchange_chips([big[n] for n in LAST_GRADS] + [[jnp.stack([repl] * N_CHIPS)]], name="exchange_last")


def kernel(
        x, mem, positions, ev_norm, ev_w_in, ev_pool_w, ev_pool_scale, ev_q_norm, ev_w_q_up, ev_kv_norm,
        ev_w_kv_up, ev_w_out, od_norm, od_w_in, od_conv_w, od_conv_b, od_w_rgate, od_b_rgate, od_w_igate,
        od_b_igate, od_lambda, od_w_out, xa_norm_x, xa_norm_mem, xa_w_q, xa_w_kv, xa_w_o, ffn_norm,
        ffn_w_gate_up, ffn_w_down, final_norm, loss_target, m_ev_norm, m_ev_w_in, m_ev_pool_w, m_ev_pool_scale,
        m_ev_q_norm, m_ev_w_q_up, m_ev_kv_norm, m_ev_w_kv_up, m_ev_w_out, m_od_norm, m_od_w_in, m_od_conv_w,
        m_od_conv_b, m_od_w_rgate, m_od_b_rgate, m_od_w_igate, m_od_b_igate, m_od_lambda, m_od_w_out,
        m_xa_norm_x, m_xa_norm_mem, m_xa_w_q, m_xa_w_kv, m_xa_w_o, m_ffn_norm, m_ffn_w_gate_up, m_ffn_w_down,
        m_final_norm, v_ev_norm, v_ev_w_in, v_ev_pool_w, v_ev_pool_scale, v_ev_q_norm, v_ev_w_q_up,
        v_ev_kv_norm, v_ev_w_kv_up, v_ev_w_out, v_od_norm, v_od_w_in, v_od_conv_w, v_od_conv_b, v_od_w_rgate,
        v_od_b_rgate, v_od_w_igate, v_od_b_igate, v_od_lambda, v_od_w_out, v_xa_norm_x, v_xa_norm_mem, v_xa_w_q,
        v_xa_w_kv, v_xa_w_o, v_ffn_norm, v_ffn_w_gate_up, v_ffn_w_down, v_final_norm):
    given = dict(locals())
    w = {n: given[n] for n in WEIGHTS}
    full_shapes = {n: tuple(d * (N_CHIPS if a == SHARD_AXIS.get(n) else 1) for a, d in enumerate(w[n].shape))
                   for n in WEIGHTS}
    full = _gather_first(w)
    later, later_plan = _gather_later_start(w)
    full["ev_norm"] = full["ev_norm"] + later[4][0:1, 0:1]
    exchange = {}

    def later_weights(after):
        return dict(zip(LATER_WEIGHTS, _split_wait(later, after, later_plan, name="gather_later_wait")))

    def exchange_earlier(grads):
        exchange["handle"], exchange["plan"] = _exchange_earlier_start(grads, full_shapes)
        return exchange["handle"][4]

    loss, grad_x, big, grads = _local_step(x[0], mem[0], positions[0], loss_target[0], full, later_weights,
                                           exchange_earlier)
    got = dict(zip(EARLIER_GRADS + ("small",),
                   _split_wait(exchange["handle"], grad_x, exchange["plan"], name="exchange_earlier_wait")))
    got.update(zip(LAST_GRADS + ("replicated",), _exchange_last(big, grads, full_shapes)))
    order = MATMUL_WEIGHTS + ("small", "replicated")
    mine = [_sum_slots(got[n], name=f"sum_chips_{n}") for n in order]
    other = _exchange_sibling(mine, name="exchange_sibling")
    loss = lax.psum(loss, ("x", "y", "c"))
    out = {}
    for i, n in enumerate(MATMUL_WEIGHTS):
        out[n] = _adamw(w[n], mine[i].reshape(w[n].shape), other[i].reshape(w[n].shape), given["m_" + n],
                        given["v_" + n], name=f"adamw_{n}")
    for i, group in ((len(MATMUL_WEIGHTS), SMALL_SHARDED), (len(MATMUL_WEIGHTS) + 1, REPLICATED)):
        packed = [_pack([given[pre + n] for n in group], 8 * LANES) for pre in ("", "m_", "v_")]
        res = _adamw(packed[0], mine[i].reshape(packed[0].shape), other[i].reshape(packed[0].shape), packed[1],
                     packed[2], name=f"adamw_group{i}")
        for j, arrs in enumerate(zip(*[_unpack(r.reshape(-1), [w[n].shape for n in group]) for r in res])):
            out[group[j]] = list(arrs)
    return (loss, grad_x[None], *[out[n][k] for k in range(4) for n in WEIGHTS])
```

```python
import functools
import math

import jax
import jax.numpy as jnp
from jax import lax
from jax.experimental import pallas as pl
from jax.experimental.pallas import tpu as pltpu

F32 = jnp.float32
BF16 = jnp.bfloat16

D_MODEL = 1024
POOL_DIM = 512
POOL_WINDOWS = (2, 4, 8, 16)
POOL_GROUP = 128
MLA_HEADS = 8
QK_NOPE = 64
QK_ROPE = 32
QK_DIM = QK_NOPE + QK_ROPE
V_HEAD = 64
HEAD_PAD = 128
Q_RANK = 256
KV_RANK = 128
ROPE_BASE = 10000.0
LRU_HEADS = 4
LRU_HEAD_DIM = 256
CONV_WIDTH = 4
LRU_C = 8.0
MEM_HEADS = 4
MEM_HEAD_DIM = 256
D_FF = 2816
RMS_EPS = 1e-6
NEG_INF = -1e30

ADAM_LR = 0.001
ADAM_B1 = 0.9
ADAM_B2 = 0.999
ADAM_EPS = 1e-08
ADAM_WD = 0.01
ADAM_STEP = 10

N_CHIPS = 4
LANES = 128
VMEM_LIMIT = 56 * 1024 * 1024
MESH = pl.DeviceIdType.MESH
ANY = pl.BlockSpec(memory_space=pl.ANY)
MIX_DIM = POOL_DIM + MLA_HEADS * HEAD_PAD

NN = (((1,), (0,)), ((), ()))
NT = (((1,), (1,)), ((), ()))
TN = (((0,), (0,)), ((), ()))


def _cp(n):
    return pltpu.CompilerParams(dimension_semantics=("arbitrary",) * n, vmem_limit_bytes=VMEM_LIMIT)


def _dot(a, b, dims=NN):
    return lax.dot_general(a, b, dims, preferred_element_type=F32)


def _rows(ts, w, cb=0):
    return pl.BlockSpec((ts, w), lambda i: (i, cb))


def _const(shape):
    return pl.BlockSpec(shape, lambda i: (0,) * len(shape))


def _mm(a, bs, epi, outs, *, tn, nj, nt=False, extras=(), rows=(), sums=(), a_cb=0, k=None, tm=None, name):
    M = a.shape[0]
    k = k or a.shape[1]
    tm = tm or min(M, 512)
    nb, ne, nr, no = len(bs), len(extras), len(rows), len(outs)
    dims = NT if nt else NN
    assert not sums or nj == 1

    def body(*refs):
        av = refs[0][...].astype(BF16)
        accs = [_dot(av, r[...].astype(BF16), dims) for r in refs[1:1 + nb]]
        vals = epi(accs, [r[...] for r in refs[1 + nb:1 + nb + ne + nr]])
        outs_refs = refs[1 + nb + ne + nr:]
        for o, v in zip(outs_refs[:no], vals[:no]):
            o[...] = v.astype(o.dtype)
        if sums:
            @pl.when(pl.program_id(1) == 0)
            def _():
                for o in outs_refs[no:]:
                    o[...] = jnp.zeros_like(o)

            for o, v in zip(outs_refs[no:], vals[no:]):
                o[...] += v

    in_specs = [pl.BlockSpec((tm, k), lambda j, i: (i, a_cb))]
    for (_, rb, cb) in bs:
        if nt:
            in_specs.append(pl.BlockSpec((tn, k), lambda j, i, rb=rb, cb=cb: (rb + j, cb)))
        else:
            in_specs.append(pl.BlockSpec((k, tn), lambda j, i, rb=rb, cb=cb: (rb, cb + j)))
    for (_, cb) in extras:
        in_specs.append(pl.BlockSpec((tm, tn), lambda j, i, cb=cb: (i, cb + j)))
    in_specs += [pl.BlockSpec((1, tn), lambda j, i: (0, 0))] * nr
    out_specs = [pl.BlockSpec((tm, tn), lambda j, i, cb=cb: (i, cb + j)) for (_, _, cb) in outs]
    out_specs += [pl.BlockSpec((1, w), lambda j, i: (0, 0)) for w in sums]
    res = pl.pallas_call(
        body, grid=(nj, M // tm), in_specs=in_specs, out_specs=out_specs,
        out_shape=[jax.ShapeDtypeStruct((M, n), dt) for (n, dt, _) in outs]
        + [jax.ShapeDtypeStruct((1, w), F32) for w in sums],
        compiler_params=_cp(2), name=name,
    )(a, *[b for (b, _, _) in bs], *[e for (e, _) in extras], *rows)
    return res


def _first(accs, extras):
    return [accs[0]]


def _add_res(accs, extras):
    return [accs[0] + extras[0].astype(F32)]


def _norm_bwd_epilogue(partials):
    def epi(accs, vals):
        dh = accs[0]
        for part in vals[:partials]:
            dh = dh + part.astype(F32)
        x, res, g = vals[partials:partials + 3]
        r = lax.rsqrt(jnp.mean(x * x, axis=-1, keepdims=True) + RMS_EPS)
        n = x * r
        dn = dh * g
        return [r * (dn - n * jnp.mean(dn * n, axis=-1, keepdims=True)) + res, jnp.sum(dh * n, axis=0, keepdims=True)]

    return epi


def _mm_tn(a, b, *, ka=None, a_cb=0, nb=None, b_cb=0, tk=None, tn=None, ts=None, name):
    S = a.shape[0]
    ka = ka or a.shape[1]
    nb = nb or b.shape[1]
    tk = tk or ka
    tn = tn or nb
    ts = ts or min(S, 512)
    a0, b0 = a_cb * (ka // tk), b_cb * (nb // tn)

    def body(a_ref, b_ref, o_ref):
        @pl.when(pl.program_id(2) == 0)
        def _():
            o_ref[...] = jnp.zeros_like(o_ref)

        o_ref[...] += _dot(a_ref[...].astype(BF16), b_ref[...].astype(BF16), TN)

    return pl.pallas_call(
        body, grid=(ka // tk, nb // tn, S // ts),
        in_specs=[pl.BlockSpec((ts, tk), lambda p, q, s: (s, a0 + p)),
                  pl.BlockSpec((ts, tn), lambda p, q, s: (s, b0 + q))],
        out_specs=pl.BlockSpec((tk, tn), lambda p, q, s: (p, q)),
        out_shape=jax.ShapeDtypeStruct((ka, nb), F32), compiler_params=_cp(3), name=name,
    )(a, b)


def _mm_tn_owners(a, bs, *, name):
    S, ka = a.shape
    nb = sum(b.shape[1] for b in bs)
    tn = nb // N_CHIPS
    ts = min(S, 512)
    per = N_CHIPS // len(bs)

    def body(a_ref, *refs):
        o_ref = refs[-1]
        q = pl.program_id(0)

        @pl.when(pl.program_id(1) == 0)
        def _():
            o_ref[...] = jnp.zeros_like(o_ref)

        av = a_ref[...].astype(BF16)
        for n, b_ref in enumerate(refs[:-1]):
            @pl.when(q // per == n)
            def _():
                o_ref[0] += _dot(av, b_ref[...].astype(BF16), TN)

    in_specs = [pl.BlockSpec((ts, ka), lambda q, s: (s, 0))]
    for n in range(len(bs)):
        in_specs.append(pl.BlockSpec((ts, tn), lambda q, s, n=n: (jnp.where(q // per == n, s, 0),
                                                                  jnp.clip(q - n * per, 0, per - 1))))
    return pl.pallas_call(
        body, grid=(N_CHIPS, S // ts), in_specs=in_specs,
        out_specs=pl.BlockSpec((1, ka, tn), lambda q, s: (q, 0, 0)),
        out_shape=jax.ShapeDtypeStruct((N_CHIPS, ka, tn), F32), compiler_params=_cp(2), name=name,
    )(a, *bs)


def _mm_tn_grouped(a, b, groups, w, *, name):
    S = a.shape[0]
    ts = min(S, 512)

    def body(a_ref, b_ref, o_ref):
        @pl.when(pl.program_id(1) == 0)
        def _():
            o_ref[...] = jnp.zeros_like(o_ref)

        o_ref[0] += _dot(a_ref[...].astype(BF16), b_ref[...].astype(BF16), TN)

    return pl.pallas_call(
        body, grid=(groups, S // ts),
        in_specs=[pl.BlockSpec((ts, w), lambda g, s: (s, g)), pl.BlockSpec((ts, w), lambda g, s: (s, g))],
        out_specs=pl.BlockSpec((1, w, w), lambda g, s: (g, 0, 0)),
        out_shape=jax.ShapeDtypeStruct((groups, w, w), F32), compiler_params=_cp(2), name=name,
    )(a, b)


def _rms(x, g, *, cb=0, w=None, ts=None, name):
    S = x.shape[0]
    w = w or x.shape[1]
    ts = ts or min(S, 512)

    def body(x_ref, g_ref, o_ref):
        xv = x_ref[...].astype(F32)
        r = lax.rsqrt(jnp.mean(xv * xv, axis=-1, keepdims=True) + RMS_EPS)
        o_ref[...] = (xv * r * g_ref[...]).astype(o_ref.dtype)

    return pl.pallas_call(
        body, grid=(S // ts,), in_specs=[_rows(ts, w, cb), _const((1, w))], out_specs=_rows(ts, w),
        out_shape=jax.ShapeDtypeStruct((S, w), BF16), compiler_params=_cp(1), name=name,
    )(x, g.reshape(1, w))


def _rms_bwd(x, g, dy, *, cb=0, w=None, res=None, out_dtype=F32, ts=None, name):
    S = x.shape[0]
    w = w or x.shape[1]
    ts = ts or min(S, 512)
    has_res = res is not None

    def body(*refs):
        x_ref, g_ref, dy_ref = refs[:3]
        dx_ref, dg_ref = refs[-2:]
        xv = x_ref[...].astype(F32)
        r = lax.rsqrt(jnp.mean(xv * xv, axis=-1, keepdims=True) + RMS_EPS)
        n = xv * r
        dyv = dy_ref[...].astype(F32)
        dn = dyv * g_ref[...]
        dx = r * (dn - n * jnp.mean(dn * n, axis=-1, keepdims=True))
        if has_res:
            dx = dx + refs[3][...].astype(F32)
        dx_ref[...] = dx.astype(dx_ref.dtype)

        @pl.when(pl.program_id(0) == 0)
        def _():
            dg_ref[...] = jnp.zeros_like(dg_ref)

        dg_ref[...] += jnp.sum(dyv * n, axis=0, keepdims=True)

    ins = [x, g.reshape(1, w), dy] + ([res] if has_res else [])
    in_specs = [_rows(ts, w, cb), _const((1, w)), _rows(ts, w)] + ([_rows(ts, w)] if has_res else [])
    return pl.pallas_call(
        body, grid=(S // ts,), in_specs=in_specs, out_specs=[_rows(ts, w), _const((1, w))],
        out_shape=[jax.ShapeDtypeStruct((S, w), out_dtype), jax.ShapeDtypeStruct((1, w), F32)],
        compiler_params=_cp(1), name=name,
    )(*ins)


HALO = 16


def _pool_counts(i, ts, rows, first_row):
    t = i * ts + first_row + lax.broadcasted_iota(jnp.int32, (rows, 1), 0)
    return [jnp.minimum(t + 1, w).astype(F32) for w in POOL_WINDOWS]


def _pool_fwd(z, pool_w, pool_scale, *, name):
    S = z.shape[0]
    ts = min(S, 512)
    nh = ts // HALO

    def body(u_ref, halo_ref, w_ref, sc_ref, y_ref, p_ref):
        i = pl.program_id(0)
        u = u_ref[...]
        halo = jnp.where(i > 0, halo_ref[...], 0.0)
        xe = jnp.concatenate([halo, u], axis=0)
        sums = []
        s = xe
        for sh in (1, 2, 4, 8):
            s = s + pltpu.roll(s, sh, 0)
            sums.append(s)
        cnts = _pool_counts(i, ts, ts, 0)
        for g in range(4):
            lo, hi = g * POOL_GROUP, (g + 1) * POOL_GROUP
            pooled = (sums[g][HALO:, lo:hi] / cnts[g] - u[:, lo:hi]).astype(BF16)
            p_ref[:, lo:hi] = pooled
            y_ref[:, lo:hi] = (_dot(pooled, w_ref[g]) * sc_ref[:, lo:hi]).astype(y_ref.dtype)

    return pl.pallas_call(
        body, grid=(S // ts,),
        in_specs=[_rows(ts, POOL_DIM), pl.BlockSpec((HALO, POOL_DIM), lambda i: (jnp.maximum(i * nh - 1, 0), 0)),
                  _const((4, POOL_GROUP, POOL_GROUP)), _const((1, POOL_DIM))],
        out_specs=[_rows(ts, POOL_DIM), _rows(ts, POOL_DIM)],
        out_shape=[jax.ShapeDtypeStruct((S, MIX_DIM), BF16), jax.ShapeDtypeStruct((S, POOL_DIM), BF16)],
        compiler_params=_cp(1), name=name,
    )(z, z, pool_w, pool_scale)


def _pool_bwd(dmix, pooled, pool_w, pool_scale, *, name):
    S = dmix.shape[0]
    ts = min(S, 512)
    nh = ts // HALO
    last = S // HALO - 1

    def body(dy_ref, dyh_ref, p_ref, w_ref, sc_ref, du_ref, dyp_ref, dsc_ref):
        i = pl.program_id(0)
        dyv = dy_ref[...].astype(F32)
        dyh = jnp.where(i < pl.num_programs(0) - 1, dyh_ref[...].astype(F32), 0.0)
        dye = jnp.concatenate([dyv, dyh], axis=0) * sc_ref[...]
        dypre = dye.astype(BF16)
        dyp_ref[...] = dypre[:ts]
        cnts = _pool_counts(i, ts, ts + HALO, 0)
        n = ts + HALO
        dsc = []
        for g in range(4):
            lo, hi = g * POOL_GROUP, (g + 1) * POOL_GROUP
            ypre = _dot(p_ref[:, lo:hi], w_ref[g])
            dsc.append(jnp.sum(dyv[:, lo:hi] * ypre, axis=0, keepdims=True))
            dpool = _dot(dypre[:, lo:hi], w_ref[g], NT)
            s = dpool / cnts[g]
            for sh in (1, 2, 4, 8)[:g + 1]:
                s = s + pltpu.roll(s, n - sh, 0)
            du_ref[:, lo:hi] = (s[:ts] - dpool[:ts]).astype(du_ref.dtype)

        @pl.when(i == 0)
        def _():
            dsc_ref[...] = jnp.zeros_like(dsc_ref)

        dsc_ref[...] += jnp.concatenate(dsc, axis=1)

    return pl.pallas_call(
        body, grid=(S // ts,),
        in_specs=[_rows(ts, POOL_DIM),
                  pl.BlockSpec((HALO, POOL_DIM), lambda i: (jnp.minimum((i + 1) * nh, last), 0)),
                  _rows(ts, POOL_DIM), _const((4, POOL_GROUP, POOL_GROUP)), _const((1, POOL_DIM))],
        out_specs=[_rows(ts, POOL_DIM), _rows(ts, POOL_DIM), _const((1, POOL_DIM))],
        out_shape=[jax.ShapeDtypeStruct((S, POOL_DIM), BF16)] * 2 + [jax.ShapeDtypeStruct((1, POOL_DIM), F32)],
        compiler_params=_cp(1), name=name,
    )(dmix, dmix, pooled, pool_w, pool_scale)


def _rope_partner(t):
    lane = lax.broadcasted_iota(jnp.int32, t.shape, 1)
    swapped = jnp.where(lane < QK_NOPE + QK_ROPE // 2, pltpu.roll(t, HEAD_PAD - QK_ROPE // 2, 1),
                        pltpu.roll(t, QK_ROPE // 2, 1))
    return jnp.where((lane >= QK_NOPE) & (lane < QK_DIM), swapped, 0.0)


def _rope_fwd(q_pad, k_pad, z, ctab, stab, *, name):
    S = q_pad.shape[0]
    ts = min(S, 512)

    def body(q_ref, k_ref, kr_ref, c_ref, s_ref, qo_ref, ko_ref):
        c, s = c_ref[...], s_ref[...]
        kr = kr_ref[...]
        kr_rot = kr * c + _rope_partner(kr) * s
        for h in range(MLA_HEADS):
            lo, hi = h * HEAD_PAD, (h + 1) * HEAD_PAD
            q = q_ref[:, lo:hi]
            qo_ref[:, lo:hi] = (q * c + _rope_partner(q) * s).astype(qo_ref.dtype)
            ko_ref[:, lo:hi] = (k_ref[:, lo:hi] + kr_rot).astype(ko_ref.dtype)

    wide = _rows(ts, MLA_HEADS * HEAD_PAD)
    return pl.pallas_call(
        body, grid=(S // ts,),
        in_specs=[wide, wide, _rows(ts, HEAD_PAD, 7), _rows(ts, HEAD_PAD), _rows(ts, HEAD_PAD)],
        out_specs=[wide, wide], out_shape=[jax.ShapeDtypeStruct((S, MLA_HEADS * HEAD_PAD), BF16)] * 2,
        compiler_params=_cp(1), name=name,
    )(q_pad, k_pad, z, ctab, stab)


def _rope_bwd(dq_rot, dk_cat, ctab, stab, *, name):
    S = dq_rot.shape[0]
    ts = min(S, 512)

    def body(dq_ref, dk_ref, c_ref, s_ref, dqo_ref, dkr_ref):
        c, s = c_ref[...], s_ref[...]
        for h in range(MLA_HEADS):
            g = dq_ref[:, h * HEAD_PAD:(h + 1) * HEAD_PAD]
            dqo_ref[:, h * HEAD_PAD:(h + 1) * HEAD_PAD] = (g * c + _rope_partner(g * s)).astype(dqo_ref.dtype)
        dk = dk_ref[...]
        g = dk[:, :HEAD_PAD]
        for h in range(1, MLA_HEADS):
            g = g + dk[:, h * HEAD_PAD:(h + 1) * HEAD_PAD]
        lane = lax.broadcasted_iota(jnp.int32, g.shape, 1)
        on_rope = (lane >= QK_NOPE) & (lane < QK_DIM)
        dkr_ref[...] = jnp.where(on_rope, g * c + _rope_partner(g * s), 0.0).astype(dkr_ref.dtype)

    wide = _rows(ts, MLA_HEADS * HEAD_PAD)
    return pl.pallas_call(
        body, grid=(S // ts,), in_specs=[wide, wide, _rows(ts, HEAD_PAD), _rows(ts, HEAD_PAD)],
        out_specs=[wide, _rows(ts, HEAD_PAD)],
        out_shape=[jax.ShapeDtypeStruct((S, MLA_HEADS * HEAD_PAD), BF16), jax.ShapeDtypeStruct((S, HEAD_PAD), BF16)],
        compiler_params=_cp(1), name=name,
    )(dq_rot, dk_cat, ctab, stab)


ATT_SCALE = QK_DIM ** -0.5
LOG2E = math.log2(math.e)


HEADS_PER_STEP = 2
ATT_COL0 = POOL_DIM // HEAD_PAD


def _stat_rows(col):
    return jnp.broadcast_to(col, (col.shape[0], LANES)).T[0:8]


def _flash_fwd(q, k, v, mix, *, name):
    S = q.shape[0]
    tq = min(S, 512)
    nq = S // tq
    hs = HEADS_PER_STEP
    wide = hs * HEAD_PAD

    def body(q_ref, k_ref, v_ref, mix_ref, o_ref, lse_ref):
        qi = pl.program_id(1)
        qv = [q_ref[:, a * HEAD_PAD:(a + 1) * HEAD_PAD] for a in range(hs)]

        def step(j, carry, masked):
            off = pl.multiple_of(j * tq, tq)
            out = []
            for a in range(hs):
                m, acc = carry[a]
                s = _dot(qv[a], k_ref[pl.ds(off, tq), a * HEAD_PAD:(a + 1) * HEAD_PAD], NT)
                if masked:
                    row = lax.broadcasted_iota(jnp.int32, (tq, tq), 0)
                    col = lax.broadcasted_iota(jnp.int32, (tq, tq), 1)
                    s = jnp.where(col <= row, s, NEG_INF)
                m_new = jnp.maximum(m, jnp.max(s, axis=-1, keepdims=True))
                p = jnp.exp2((s - m_new) * (ATT_SCALE * LOG2E))
                alpha = jnp.exp2((m - m_new) * (ATT_SCALE * LOG2E))
                acc = alpha * acc + _dot(p.astype(BF16), v_ref[pl.ds(off, tq), a * HEAD_PAD:(a + 1) * HEAD_PAD])
                out.append((m_new, acc))
            return tuple(out)

        one = (jnp.full((tq, 1), NEG_INF, F32), jnp.zeros((tq, HEAD_PAD), F32))
        carry = lax.fori_loop(0, qi, lambda j, c: step(j, c, False), (one,) * hs)
        carry = step(qi, carry, True)
        for a in range(hs):
            m, acc = carry[a]
            l = acc[:, V_HEAD:V_HEAD + 1]
            o_ref[:, a * HEAD_PAD:(a + 1) * HEAD_PAD] = (acc / l).astype(o_ref.dtype)
            lse_ref[a] = _stat_rows(m * ATT_SCALE + jnp.log(l))

    blk = pl.BlockSpec((tq, wide), lambda h, i: (i, h))
    full = pl.BlockSpec((S, wide), lambda h, i: (0, h))
    return pl.pallas_call(
        body, grid=(MLA_HEADS // hs, nq), in_specs=[blk, full, full, ANY],
        out_specs=[pl.BlockSpec((tq, wide), lambda h, i: (i, ATT_COL0 // hs + h)),
                   pl.BlockSpec((hs, 8, tq), lambda h, i: (h, i, 0))],
        out_shape=[jax.ShapeDtypeStruct(mix.shape, mix.dtype), jax.ShapeDtypeStruct((MLA_HEADS, nq * 8, tq), F32)],
        input_output_aliases={3: 0}, compiler_params=_cp(2), name=name,
    )(q, k, v, mix)


def _attn_delta(dmix, mix, *, name):
    S = mix.shape[0]
    ts = min(S, 512)
    half = MLA_HEADS // 2
    halves = [_rows(ts, half * HEAD_PAD, 1), _rows(ts, half * HEAD_PAD, 2)]

    def body(do0_ref, do1_ref, o0_ref, o1_ref, d_ref):
        for n, (do_ref, o_ref) in enumerate(((do0_ref, o0_ref), (do1_ref, o1_ref))):
            prod = do_ref[...].astype(F32) * o_ref[...].astype(F32)
            for a in range(half):
                d_ref[n * half + a] = _stat_rows(
                    jnp.sum(prod[:, a * HEAD_PAD:(a + 1) * HEAD_PAD], axis=-1, keepdims=True))

    return pl.pallas_call(
        body, grid=(S // ts,), in_specs=halves + halves,
        out_specs=pl.BlockSpec((MLA_HEADS, 8, ts), lambda i: (0, i, 0)),
        out_shape=jax.ShapeDtypeStruct((MLA_HEADS, (S // ts) * 8, ts), F32), compiler_params=_cp(1), name=name,
    )(dmix, dmix, mix, mix)


def _flash_bwd(q, k, v, dmix, lse_rows, delta_rows, *, name):
    S = q.shape[0]
    tq = min(S, 512)
    nq = S // tq

    def body(q_ref, do_ref, lse_ref, dl_ref, k_ref, v_ref, dq_ref, dk_ref, dv_ref):
        j = pl.program_id(1)

        @pl.when(j == 0)
        def _():
            dq_ref[...] = jnp.zeros_like(dq_ref)

        kv, vv = k_ref[...], v_ref[...]

        def step(i, carry, masked):
            dk, dv = carry
            off = pl.multiple_of(i * tq, tq)
            off8 = pl.multiple_of(i * 8, 8)
            qv = q_ref[pl.ds(off, tq), :]
            dov = do_ref[pl.ds(off, tq), :]
            lse2 = lse_ref[0, pl.ds(off8, 8), :][0:1] * LOG2E
            dl = dl_ref[0, pl.ds(off8, 8), :][0:1]
            st = _dot(kv, qv, NT)
            if masked:
                krow = lax.broadcasted_iota(jnp.int32, (tq, tq), 0)
                qcol = lax.broadcasted_iota(jnp.int32, (tq, tq), 1)
                st = jnp.where(krow <= qcol, st, NEG_INF)
            pt = jnp.exp2(st * (ATT_SCALE * LOG2E) - lse2)
            dv = dv + _dot(pt.astype(BF16), dov)
            dst = (pt * (_dot(vv, dov, NT) - dl)).astype(BF16)
            dk = dk + _dot(dst, qv)
            dq_ref[pl.ds(off, tq), :] += _dot(dst, kv, TN)
            return dk, dv

        zero = jnp.zeros((tq, HEAD_PAD), F32)
        carry = step(j, (zero, zero), True)
        dk, dv = lax.fori_loop(j + 1, nq, lambda i, c: step(i, c, False), carry)
        dk_ref[...] = dk * ATT_SCALE
        dv_ref[...] = dv

        @pl.when(j == nq - 1)
        def _():
            dq_ref[...] = dq_ref[...] * ATT_SCALE

    blk = pl.BlockSpec((tq, HEAD_PAD), lambda h, j: (j, h))
    full = pl.BlockSpec((S, HEAD_PAD), lambda h, j: (0, h))
    stat = pl.BlockSpec((1, nq * 8, tq), lambda h, j: (h, 0, 0))
    wide = jax.ShapeDtypeStruct((S, MLA_HEADS * HEAD_PAD), F32)
    return pl.pallas_call(
        body, grid=(MLA_HEADS, nq),
        in_specs=[full, pl.BlockSpec((S, HEAD_PAD), lambda h, j: (0, POOL_DIM // HEAD_PAD + h)), stat, stat, blk, blk],
        out_specs=[full, blk, blk], out_shape=[wide, wide, wide], compiler_params=_cp(2), name=name,
    )(q, dmix, lse_rows, delta_rows, k, v)


MEM_SCALE = MEM_HEAD_DIM ** -0.5


def _xattn_probs(qh, kh):
    s = _dot(qh, kh, NT) * MEM_SCALE
    e = jnp.exp(s - jnp.max(s, axis=-1, keepdims=True))
    return e / jnp.sum(e, axis=-1, keepdims=True)


def _xattn_fwd(q, kvm, *, name):
    S = q.shape[0]
    ts = min(S, 512)
    nm = kvm.shape[0]

    def body(q_ref, kv_ref, o_ref):
        for h in range(MEM_HEADS):
            lo, hi = h * MEM_HEAD_DIM, (h + 1) * MEM_HEAD_DIM
            p = _xattn_probs(q_ref[:, lo:hi], kv_ref[:, lo:hi])
            o_ref[:, lo:hi] = _dot(p.astype(BF16), kv_ref[:, D_MODEL + lo:D_MODEL + hi]).astype(o_ref.dtype)

    return pl.pallas_call(
        body, grid=(S // ts,), in_specs=[_rows(ts, D_MODEL), _const((nm, 2 * D_MODEL))],
        out_specs=_rows(ts, D_MODEL), out_shape=jax.ShapeDtypeStruct((S, D_MODEL), BF16),
        compiler_params=_cp(1), name=name,
    )(q, kvm)


def _xattn_bwd(q, kvm, do, *, name):
    S = q.shape[0]
    ts = min(S, 512)
    nm = kvm.shape[0]

    def body(q_ref, kv_ref, do_ref, dq_ref, dkv_ref):
        @pl.when(pl.program_id(0) == 0)
        def _():
            dkv_ref[...] = jnp.zeros_like(dkv_ref)

        for h in range(MEM_HEADS):
            lo, hi = h * MEM_HEAD_DIM, (h + 1) * MEM_HEAD_DIM
            qh, kh, vh = q_ref[:, lo:hi], kv_ref[:, lo:hi], kv_ref[:, D_MODEL + lo:D_MODEL + hi]
            doh = do_ref[:, lo:hi].astype(BF16)
            p = _xattn_probs(qh, kh)
            dp = _dot(doh, vh, NT)
            ds = (p * (dp - jnp.sum(dp * p, axis=-1, keepdims=True)) * MEM_SCALE).astype(BF16)
            dq_ref[:, lo:hi] = _dot(ds, kh).astype(dq_ref.dtype)
            dkv_ref[:, lo:hi] += _dot(ds, qh, TN)
            dkv_ref[:, D_MODEL + lo:D_MODEL + hi] += _dot(p.astype(BF16), doh, TN)

    return pl.pallas_call(
        body, grid=(S // ts,), in_specs=[_rows(ts, D_MODEL), _const((nm, 2 * D_MODEL)), _rows(ts, D_MODEL)],
        out_specs=[_rows(ts, D_MODEL), _const((nm, 2 * D_MODEL))],
        out_shape=[jax.ShapeDtypeStruct((S, D_MODEL), BF16), jax.ShapeDtypeStruct((nm, 2 * D_MODEL), F32)],
        compiler_params=_cp(1), name=name,
    )(q, kvm, do)


CONV_HALO = 8


def _sigmoid(x):
    return 1.0 / (1.0 + jnp.exp(-x))


def _softplus(x):
    return jnp.maximum(x, 0.0) + jnp.log(1.0 + jnp.exp(-jnp.abs(x)))


def _neg_expm1(x):
    series = -x * (1.0 + x * (1.0 / 2) * (1.0 + x * (1.0 / 3) * (1.0 + x * (1.0 / 4) * (1.0 + x * (1.0 / 5)))))
    return jnp.where(x > -0.05, series, 1.0 - jnp.exp(x))


GELU_C = math.sqrt(2.0 / math.pi)


def _gelu(x):
    return 0.5 * x * (1.0 + jnp.tanh(GELU_C * (x + 0.044715 * x * x * x)))


def _gelu_grad(x):
    t = jnp.tanh(GELU_C * (x + 0.044715 * x * x * x))
    return 0.5 * (1.0 + t) + 0.5 * x * (1.0 - t * t) * GELU_C * (1.0 + 3 * 0.044715 * x * x)


def _lru_gates(xc, wr_ref, br, wi_ref, bi, sp, reset):
    xcb = xc.astype(BF16)
    pr, pi = [], []
    for h in range(LRU_HEADS):
        lo, hi = h * LRU_HEAD_DIM, (h + 1) * LRU_HEAD_DIM
        pr.append(_dot(xcb[:, lo:hi], wr_ref[h]))
        pi.append(_dot(xcb[:, lo:hi], wi_ref[h]))
    r = _sigmoid(jnp.concatenate(pr, axis=1) + br)
    ig = _sigmoid(jnp.concatenate(pi, axis=1) + bi)
    log_a = -LRU_C * r * sp
    a = jnp.where(reset, 0.0, jnp.exp(log_a))
    mult = jnp.where(reset, 1.0, jnp.sqrt(jnp.maximum(_neg_expm1(2.0 * log_a), 0.0)))
    return r, ig, a, mult


SUBLANES = 8


def _compose_groups(a, b, reverse):
    n = a.shape[0]
    row = lax.broadcasted_iota(jnp.int32, a.shape, 0) % SUBLANES
    for s in (1, 2, 4):
        inside = (row < SUBLANES - s) if reverse else (row >= s)
        shift = n - s if reverse else s
        a_s = jnp.where(inside, pltpu.roll(a, shift, 0), 1.0)
        b_s = jnp.where(inside, pltpu.roll(b, shift, 0), 0.0)
        b = a * b_s + b
        a = a * a_s
    return a, b


def _chain_groups(a_buf, h_ref, state, reverse):
    groups = a_buf.shape[0] // SUBLANES

    def group(g, h_in):
        off = pl.multiple_of((groups - 1 - g if reverse else g) * SUBLANES, SUBLANES)
        h = a_buf[pl.ds(off, SUBLANES), :] * h_in + h_ref[pl.ds(off, SUBLANES), :]
        h_ref[pl.ds(off, SUBLANES), :] = h
        return jnp.broadcast_to(h[0:1] if reverse else h[SUBLANES - 1:SUBLANES], h.shape)

    return lax.fori_loop(0, groups, group, state, unroll=4)[0:1]


def _lru_fwd(z, reset, conv_w, conv_b, w_r, b_r, w_i, b_i, lam, *, name):
    S = z.shape[0]
    ts = min(S, 512)
    nh = ts // CONV_HALO
    W = D_MODEL

    def body(gate_ref, xb_ref, halo_ref, rs_ref, cw_ref, cb_ref, wr_ref, br_ref, wi_ref, bi_ref, lam_ref,
             xc_ref, h_ref, y_ref, a_buf, carry):
        i = pl.program_id(0)

        @pl.when(i == 0)
        def _():
            carry[...] = jnp.zeros_like(carry)

        halo = jnp.where(i > 0, halo_ref[...], 0.0)
        xe = jnp.concatenate([halo, xb_ref[...]], axis=0)
        xc = cb_ref[...] + cw_ref[3:4, :] * xe[CONV_HALO:]
        for kk in range(CONV_WIDTH - 1):
            xc = xc + cw_ref[kk:kk + 1, :] * pltpu.roll(xe, CONV_WIDTH - 1 - kk, 0)[CONV_HALO:]
        xc_ref[...] = xc
        reset = rs_ref[...] > 0.5
        _, ig, a, mult = _lru_gates(xc, wr_ref, br_ref[...], wi_ref, bi_ref[...], _softplus(-lam_ref[...]), reset)
        a_buf[...], h_ref[...] = _compose_groups(a, mult * (ig * xc), False)
        carry[...] = _chain_groups(a_buf, h_ref, jnp.broadcast_to(carry[...], (SUBLANES, W)), False)
        y_ref[...] = (_gelu(gate_ref[...]) * h_ref[...]).astype(y_ref.dtype)

    vec = _const((1, W))
    gw = _const((LRU_HEADS, LRU_HEAD_DIM, LRU_HEAD_DIM))
    return pl.pallas_call(
        body, grid=(S // ts,),
        in_specs=[_rows(ts, W, 0), _rows(ts, W, 1),
                  pl.BlockSpec((CONV_HALO, W), lambda i: (jnp.maximum(i * nh - 1, 0), 1)),
                  _rows(ts, 1), _const((CONV_WIDTH, W)), vec, gw, vec, gw, vec, vec],
        out_specs=[_rows(ts, W)] * 3,
        out_shape=[jax.ShapeDtypeStruct((S, W), F32), jax.ShapeDtypeStruct((S, W), F32),
                   jax.ShapeDtypeStruct((S, W), BF16)],
        scratch_shapes=[pltpu.VMEM((ts, W), F32), pltpu.VMEM((1, W), F32)],
        compiler_params=_cp(1), name=name,
    )(z, z, z, reset, conv_w, conv_b, w_r, b_r, w_i, b_i, lam)


def _lru_bwd(dy, z, xc, hseq, reset, w_r, b_r, w_i, b_i, lam, *, name):
    S = z.shape[0]
    ts = min(S, 512)
    nt = S // ts
    nh = ts // CONV_HALO
    W = D_MODEL

    def body(dy_ref, gate_ref, xc_ref, h_ref, hh_ref, rs_ref, wr_ref, br_ref, wi_ref, bi_ref, lam_ref,
             dg_ref, dxc_ref, dpr_ref, dpi_ref, acc_ref, a_buf, dh_buf, carry):
        i = pl.program_id(0)
        tile = nt - 1 - i

        @pl.when(i == 0)
        def _():
            carry[...] = jnp.zeros_like(carry)
            acc_ref[...] = jnp.zeros_like(acc_ref)

        xc = xc_ref[...]
        lam_v = lam_ref[...]
        sp = _softplus(-lam_v)
        reset = rs_ref[...] > 0.5
        r, ig, a, mult = _lru_gates(xc, wr_ref, br_ref[...], wi_ref, bi_ref[...], sp, reset)
        gate = gate_ref[...]
        dyv = dy_ref[...].astype(F32)
        h = h_ref[...]
        dg_ref[...] = (dyv * h * _gelu_grad(gate)).astype(dg_ref.dtype)
        last_row = lax.broadcasted_iota(jnp.int32, a.shape, 0) == ts - 1
        a_buf[...], dh_buf[...] = _compose_groups(jnp.where(last_row, 1.0, pltpu.roll(a, ts - 1, 0)),
                                                  dyv * _gelu(gate), True)
        _chain_groups(a_buf, dh_buf, jnp.broadcast_to(carry[...], (SUBLANES, W)), True)
        dh = dh_buf[...]
        carry[...] = a[0:1] * dh[0:1]
        hh = jnp.where(tile > 0, hh_ref[...], 0.0)
        h_prev = pltpu.roll(jnp.concatenate([hh, h], axis=0), 1, 0)[CONV_HALO:]
        da = dh * h_prev
        bx = ig * xc
        dmult = dh * bx
        dbx = dh * mult
        di = dbx * xc
        dlog_a = jnp.where(reset, 0.0, da * a - dmult * a * a / jnp.maximum(mult, 1e-30))
        dr = dlog_a * (-LRU_C) * sp
        dpre_r = dr * r * (1.0 - r)
        dpre_i = di * ig * (1.0 - ig)
        dprb, dpib = dpre_r.astype(BF16), dpre_i.astype(BF16)
        dpr_ref[...] = dprb
        dpi_ref[...] = dpib
        back = []
        for hd in range(LRU_HEADS):
            lo, hi = hd * LRU_HEAD_DIM, (hd + 1) * LRU_HEAD_DIM
            back.append(_dot(dprb[:, lo:hi], wr_ref[hd], NT) + _dot(dpib[:, lo:hi], wi_ref[hd], NT))
        dxc_ref[...] = dbx * ig + jnp.concatenate(back, axis=1)
        dlam = jnp.sum(dlog_a * (-LRU_C) * r, axis=0, keepdims=True) * (-_sigmoid(-lam_v))
        acc_ref[0:1, :] += jnp.sum(dpre_r, axis=0, keepdims=True)
        acc_ref[1:2, :] += jnp.sum(dpre_i, axis=0, keepdims=True)
        acc_ref[2:3, :] += dlam

    rev = lambda cb: pl.BlockSpec((ts, W), lambda i: (nt - 1 - i, cb))
    vec = _const((1, W))
    gw = _const((LRU_HEADS, LRU_HEAD_DIM, LRU_HEAD_DIM))
    return pl.pallas_call(
        body, grid=(nt,),
        in_specs=[rev(0), rev(0), rev(0), rev(0),
                  pl.BlockSpec((CONV_HALO, W), lambda i: (jnp.maximum((nt - 1 - i) * nh - 1, 0), 0)),
                  pl.BlockSpec((ts, 1), lambda i: (nt - 1 - i, 0)), gw, vec, gw, vec, vec],
        out_specs=[rev(0), rev(0), rev(0), rev(0), _const((8, W))],
        out_shape=[jax.ShapeDtypeStruct((S, W), BF16), jax.ShapeDtypeStruct((S, W), F32),
                   jax.ShapeDtypeStruct((S, W), BF16), jax.ShapeDtypeStruct((S, W), BF16),
                   jax.ShapeDtypeStruct((8, W), F32)],
        scratch_shapes=[pltpu.VMEM((ts, W), F32), pltpu.VMEM((ts, W), F32), pltpu.VMEM((1, W), F32)],
        compiler_params=_cp(1), name=name,
    )(dy, z, xc, hseq, hseq, reset, w_r, b_r, w_i, b_i, lam)


def _conv_bwd(dxc, z, conv_w, *, name):
    S = dxc.shape[0]
    ts = min(S, 512)
    nh = ts // CONV_HALO
    last = S // CONV_HALO - 1
    W = D_MODEL
    n = ts + CONV_HALO

    def body(d_ref, dn_ref, xb_ref, xp_ref, cw_ref, dxb_ref, acc_ref):
        i = pl.program_id(0)

        @pl.when(i == 0)
        def _():
            acc_ref[...] = jnp.zeros_like(acc_ref)

        d = d_ref[...]
        de = jnp.concatenate([d, jnp.where(i < pl.num_programs(0) - 1, dn_ref[...], 0.0)], axis=0)
        xe = jnp.concatenate([jnp.where(i > 0, xp_ref[...], 0.0), xb_ref[...]], axis=0)
        dxb = cw_ref[3:4, :] * d
        acc_ref[3:4, :] += jnp.sum(d * xe[CONV_HALO:], axis=0, keepdims=True)
        for kk in range(CONV_WIDTH - 1):
            sh = CONV_WIDTH - 1 - kk
            dxb = dxb + cw_ref[kk:kk + 1, :] * pltpu.roll(de, n - sh, 0)[:ts]
            acc_ref[kk:kk + 1, :] += jnp.sum(d * pltpu.roll(xe, sh, 0)[CONV_HALO:], axis=0, keepdims=True)
        dxb_ref[...] = dxb.astype(dxb_ref.dtype)
        acc_ref[4:5, :] += jnp.sum(d, axis=0, keepdims=True)

    return pl.pallas_call(
        body, grid=(S // ts,),
        in_specs=[_rows(ts, W), pl.BlockSpec((CONV_HALO, W), lambda i: (jnp.minimum((i + 1) * nh, last), 0)),
                  _rows(ts, W, 1), pl.BlockSpec((CONV_HALO, W), lambda i: (jnp.maximum(i * nh - 1, 0), 1)),
                  _const((CONV_WIDTH, W))],
        out_specs=[_rows(ts, W), _const((8, W))],
        out_shape=[jax.ShapeDtypeStruct((S, W), BF16), jax.ShapeDtypeStruct((8, W), F32)],
        compiler_params=_cp(1), name=name,
    )(dxc, dxc, z, z, conv_w)


def _loss_head(x, g, target, *, name):
    S, D = x.shape
    ts = min(S, 512)

    def body(x_ref, g_ref, t_ref, dx_ref, dg_ref, l_ref):
        @pl.when(pl.program_id(0) == 0)
        def _():
            dg_ref[...] = jnp.zeros_like(dg_ref)
            l_ref[...] = jnp.zeros_like(l_ref)

        xv = x_ref[...]
        r = lax.rsqrt(jnp.mean(xv * xv, axis=-1, keepdims=True) + RMS_EPS)
        n = xv * r
        err = n * g_ref[...] - t_ref[...]
        l_ref[...] += 0.5 * jnp.sum(jnp.sum(err * err, axis=-1, keepdims=True) * (1.0 / D), axis=0, keepdims=True)
        dy = err * (1.0 / D)
        dn = dy * g_ref[...]
        dx_ref[...] = r * (dn - n * jnp.mean(dn * n, axis=-1, keepdims=True))
        dg_ref[...] += jnp.sum(dy * n, axis=0, keepdims=True)

    return pl.pallas_call(
        body, grid=(S // ts,), in_specs=[_rows(ts, D), _const((1, D)), _rows(ts, D)],
        out_specs=[_rows(ts, D), _const((1, D)), _const((8, LANES))],
        out_shape=[jax.ShapeDtypeStruct((S, D), F32), jax.ShapeDtypeStruct((1, D), F32),
                   jax.ShapeDtypeStruct((8, LANES), F32)],
        compiler_params=_cp(1), name=name,
    )(x, g.reshape(1, D), target)


def _adamw(w, ga, gb, m, v, *, name):
    shape = w.shape
    cols = shape[-1]
    rows = w.size // cols
    br = rows
    if rows * cols * 4 > (1 << 20):
        br = max(d for d in range(8, rows + 1, 8) if rows % d == 0 and d * cols * 4 <= (1 << 20))

    def body(w_ref, ga_ref, gb_ref, m_ref, v_ref, g_ref, d_ref, mo_ref, vo_ref):
        gv = ga_ref[...] + gb_ref[...]
        g_ref[...] = gv
        mn = ADAM_B1 * m_ref[...] + (1.0 - ADAM_B1) * gv
        vn = ADAM_B2 * v_ref[...] + (1.0 - ADAM_B2) * (gv * gv)
        m_hat = mn / (1.0 - ADAM_B1 ** ADAM_STEP)
        v_hat = vn / (1.0 - ADAM_B2 ** ADAM_STEP)
        d_ref[...] = -ADAM_LR * (m_hat / (jnp.sqrt(v_hat) + ADAM_EPS) + ADAM_WD * w_ref[...])
        mo_ref[...] = mn
        vo_ref[...] = vn

    spec = _rows(br, cols)
    outs = pl.pallas_call(
        body, grid=(rows // br,), in_specs=[spec] * 5, out_specs=[spec] * 4,
        out_shape=[jax.ShapeDtypeStruct((rows, cols), F32)] * 4, compiler_params=_cp(1), name=name,
    )(*[t.reshape(rows, cols) for t in (w, ga, gb, m, v)])
    return [o.reshape(shape) for o in outs]


def _pad_heads(w, width):
    k = w.shape[0]
    return jnp.pad(w.reshape(k, MLA_HEADS, width), ((0, 0), (0, 0), (0, HEAD_PAD - width))).reshape(k, -1)


def _unpad_heads(w, width):
    k = w.shape[0]
    return w.reshape(k, MLA_HEADS, HEAD_PAD)[:, :, :width].reshape(k, MLA_HEADS * width)


def _rope_tables(positions):
    inv_freq = ROPE_BASE ** (-jnp.arange(0, QK_ROPE, 2, dtype=F32) / QK_ROPE)
    ang = positions.astype(F32)[:, None] * inv_freq
    cos, sin = jnp.cos(ang), jnp.sin(ang)
    S = positions.shape[0]
    ones, zeros = jnp.ones((S, QK_NOPE), F32), jnp.zeros((S, QK_NOPE), F32)
    ctab = jnp.concatenate([ones, cos, cos, ones[:, :HEAD_PAD - QK_DIM]], axis=1)
    stab = jnp.concatenate([zeros, -sin, sin, zeros[:, :HEAD_PAD - QK_DIM]], axis=1)
    return ctab, stab


def _memory_block(x, mem, W, layer, tag):
    hx = _rms(x, W["xa_norm_x"][layer], name=f"{tag}_xa_norm")
    qx = _mm(hx, [(W["xa_w_q"][layer], 0, 0)], _first, [(D_MODEL, BF16, 0)], tn=D_MODEL, nj=1, name=f"{tag}_xa_q")[0]
    mn = _rms(mem, W["xa_norm_mem"][layer], name=f"{tag}_xa_norm_mem")
    kvm = _mm(mn, [(W["xa_w_kv"][layer], 0, 0)], _first, [(2 * D_MODEL, BF16, 0)], tn=2 * D_MODEL, nj=1,
              name=f"{tag}_xa_kv")[0]
    o = _xattn_fwd(qx, kvm, name=f"{tag}_xa_attn")
    xo = _mm(o, [(W["xa_w_o"][layer], 0, 0)], _add_res, [(D_MODEL, F32, 0)], extras=[(x, 0)], tn=D_MODEL, nj=1,
             name=f"{tag}_xa_out")[0]
    return xo, (x, hx, qx, mn, kvm, o)


def _memory_block_bwd(dxo, mem, W, layer, saved, tag, grads):
    x, hx, qx, mn, kvm, o = saved
    wq, wkv, wo = W["xa_w_q"][layer], W["xa_w_kv"][layer], W["xa_w_o"][layer]
    do = _mm(dxo, [(wo, 0, 0)], _first, [(D_MODEL, BF16, 0)], nt=True, tn=D_MODEL, nj=1, name=f"{tag}_xa_do")[0]
    grads["xa_w_o"][layer] = _owner_major(_mm_tn(o, dxo, name=f"{tag}_xa_dwo"), 0)
    dqx, dkvm = _xattn_bwd(qx, kvm, do, name=f"{tag}_xa_attn_bwd")
    grads["xa_w_q"][layer] = _owner_major(_mm_tn(hx, dqx, name=f"{tag}_xa_dwq"), 0)
    dx, dg = _mm(dqx, [(wq, 0, 0)], _norm_bwd_epilogue(0), [(D_MODEL, F32, 0)], nt=True, extras=[(x, 0), (dxo, 0)],
                 rows=[W["xa_norm_x"][layer].reshape(1, D_MODEL)], sums=[D_MODEL], tn=D_MODEL, nj=1,
                 name=f"{tag}_xa_dhx")
    grads["xa_norm_x"][layer] = dg[0]
    dmn = _mm(dkvm, [(wkv, 0, 0)], _first, [(D_MODEL, F32, 0)], nt=True, tn=D_MODEL, nj=1, name=f"{tag}_xa_dmn")[0]
    grads["xa_w_kv"][layer] = _mm_tn_owners(mn, [dkvm], name=f"{tag}_xa_dwkv")
    _, dgm = _rms_bwd(mem, W["xa_norm_mem"][layer], dmn, name=f"{tag}_xa_norm_mem_bwd")
    grads["xa_norm_mem"][layer] = dgm[0]
    return dx


FF_TN = D_FF // 2


def _silu_mul(accs, extras):
    g, u = accs
    return [g * _sigmoid(g) * u, g, u]


def _silu_mul_bwd(accs, extras):
    da = accs[0]
    g, u = extras[0].astype(F32), extras[1].astype(F32)
    sg = _sigmoid(g)
    return [da * u * sg * (1.0 + g * (1.0 - sg)), da * g * sg]


def _ffn_block(x, W, layer, tag):
    hf = _rms(x, W["ffn_norm"][layer], name=f"{tag}_ffn_norm")
    wgu, wd = W["ffn_w_gate_up"][layer], W["ffn_w_down"][layer]
    act, g, u = _mm(hf, [(wgu, 0, 0), (wgu, 0, 2)], _silu_mul, [(D_FF, BF16, 0)] * 3, tn=FF_TN, nj=2,
                    name=f"{tag}_ffn_up")
    xo = _mm(act, [(wd, 0, 0)], _add_res, [(D_MODEL, F32, 0)], extras=[(x, 0)], tn=D_MODEL, nj=1,
             name=f"{tag}_ffn_down")[0]
    return xo, (x, hf, act, g, u)


def _ffn_block_bwd(dxo, W, layer, saved, tag, grads):
    x, hf, act, g, u = saved
    wgu, wd = W["ffn_w_gate_up"][layer], W["ffn_w_down"][layer]
    dg, du = _mm(dxo, [(wd, 0, 0)], _silu_mul_bwd, [(D_FF, BF16, 0)] * 2, nt=True, extras=[(g, 0), (u, 0)], tn=FF_TN,
                 nj=2, name=f"{tag}_ffn_dact")
    grads["ffn_w_down"][layer] = _owner_major(_mm_tn(act, dxo, tk=FF_TN, name=f"{tag}_ffn_dwd"), 0)
    dhf = _mm(dg, [(wgu, 0, 0)], _first, [(D_MODEL, F32, 0)], nt=True, tn=D_MODEL, nj=1, name=f"{tag}_ffn_dhf_g")[0]
    dx, dgn = _mm(du, [(wgu, 0, 1)], _norm_bwd_epilogue(1), [(D_MODEL, F32, 0)], nt=True,
                  extras=[(dhf, 0), (x, 0), (dxo, 0)], rows=[W["ffn_norm"][layer].reshape(1, D_MODEL)],
                  sums=[D_MODEL], tn=D_MODEL, nj=1, name=f"{tag}_ffn_dhf_u")
    grads["ffn_w_gate_up"][layer] = _mm_tn_owners(hf, [dg, du], name=f"{tag}_ffn_dwgu")
    grads["ffn_norm"][layer] = dgn[0]
    return dx


def _keys_and_values(accs, extras):
    k, v = accs
    lane = lax.broadcasted_iota(jnp.int32, v.shape, 1)
    return [k, jnp.where(lane % HEAD_PAD == V_HEAD, 1.0, v)]


def _even_block(x, tabs, W, tag):
    ctab, stab = tabs
    w_in = W["ev_w_in"][0]
    zero = jnp.zeros((D_MODEL, QK_NOPE), BF16)
    w_in_pad = jnp.concatenate([w_in[:, :896], zero, w_in[:, 896:], zero[:, :HEAD_PAD - QK_DIM]], axis=1)
    w_q_pad = _pad_heads(W["ev_w_q_up"][0], QK_DIM)
    wkv = W["ev_w_kv_up"][0].reshape(KV_RANK, MLA_HEADS, QK_NOPE + V_HEAD)
    w_kv_pad = jnp.concatenate([_pad_heads(wkv[:, :, :QK_NOPE].reshape(KV_RANK, -1), QK_NOPE),
                                _pad_heads(wkv[:, :, QK_NOPE:].reshape(KV_RANK, -1), V_HEAD)], axis=1)
    w_out = W["ev_w_out"][0]
    w_att = jnp.pad(w_out[POOL_DIM:].reshape(MLA_HEADS, V_HEAD, D_MODEL), ((0, 0), (0, HEAD_PAD - V_HEAD), (0, 0)))
    w_out_pad = jnp.concatenate([w_out[:POOL_DIM], w_att.reshape(MLA_HEADS * HEAD_PAD, D_MODEL)], axis=0)
    pool_w = W["ev_pool_w"][0].astype(BF16)
    pool_scale = W["ev_pool_scale"]

    h = _rms(x, W["ev_norm"][0], name=f"{tag}_norm")
    z = _mm(h, [(w_in_pad, 0, 0)], _first, [(D_MODEL, F32, 0)], tn=D_MODEL, nj=1, name=f"{tag}_in")[0]
    mix, pooled = _pool_fwd(z, pool_w, pool_scale, name=f"{tag}_pool")
    cqn = _rms(z, W["ev_q_norm"][0], cb=2, w=Q_RANK, name=f"{tag}_q_norm")
    ckvn = _rms(z, W["ev_kv_norm"][0], cb=6, w=KV_RANK, name=f"{tag}_kv_norm")
    q_pad = _mm(cqn, [(w_q_pad, 0, 0)], _first, [(D_MODEL, F32, 0)], tn=D_MODEL, nj=1, name=f"{tag}_q_up")[0]
    k_pad, v_pad = _mm(ckvn, [(w_kv_pad, 0, 0), (w_kv_pad, 0, 1)], _keys_and_values,
                       [(D_MODEL, F32, 0), (D_MODEL, BF16, 0)], tn=D_MODEL, nj=1, name=f"{tag}_kv_up")
    q_rot, k_cat = _rope_fwd(q_pad, k_pad, z, ctab, stab, name=f"{tag}_rope")
    mix, lse = _flash_fwd(q_rot, k_cat, v_pad, mix, name=f"{tag}_attn")
    xo = _mm(mix, [(w_out_pad, 0, 0)], _add_res, [(D_MODEL, F32, 0)], extras=[(x, 0)], tn=D_MODEL, nj=1,
             name=f"{tag}_out")[0]
    saved = (x, h, z, pooled, cqn, ckvn, q_rot, k_cat, v_pad, lse, mix,
             (w_in_pad, w_q_pad, w_kv_pad, w_out_pad, pool_w, pool_scale))
    return xo, saved


def _even_block_bwd(dxo, tabs, W, saved, tag, grads, token=None):
    ctab, stab = tabs
    x, h, z, pooled, cqn, ckvn, q_rot, k_cat, v_pad, lse, mix, wts = saved
    w_in_pad, w_q_pad, w_kv_pad, w_out_pad, pool_w, pool_scale = wts
    if token is not None:
        w_out_pad = w_out_pad + token[0:1, 0:1].astype(BF16)
    dmix = _mm(dxo, [(w_out_pad, 0, 0)], _first, [(MIX_DIM, BF16, 0)], nt=True, tn=MIX_DIM, nj=1,
               name=f"{tag}_dmix")[0]
    dw_out_pad = _mm_tn(mix, dxo, tk=MIX_DIM // 3, name=f"{tag}_dw_out")
    datt = dw_out_pad[POOL_DIM:].reshape(MLA_HEADS, HEAD_PAD, D_MODEL)[:, :V_HEAD].reshape(-1, D_MODEL)
    grads["ev_w_out"] = [_owner_major(jnp.concatenate([dw_out_pad[:POOL_DIM], datt], axis=0), 0)]
    delta = _attn_delta(dmix, mix, name=f"{tag}_delta")
    dq_rot, dk_cat, dv_pad = _flash_bwd(q_rot, k_cat, v_pad, dmix, lse, delta, name=f"{tag}_attn_bwd")
    dq_pad, dkr = _rope_bwd(dq_rot, dk_cat, ctab, stab, name=f"{tag}_rope_bwd")
    dw_q_pad = _mm_tn(cqn, dq_pad, name=f"{tag}_dw_q_up")
    grads["ev_w_q_up"] = [_owner_major(_unpad_heads(dw_q_pad, QK_DIM), 1)]
    dcqn = _mm(dq_pad, [(w_q_pad, 0, 0)], _first, [(Q_RANK, F32, 0)], nt=True, tn=Q_RANK, nj=1, name=f"{tag}_dcqn")[0]
    dwk = _unpad_heads(_mm_tn(ckvn, dk_cat, name=f"{tag}_dw_k_up"), QK_NOPE).reshape(KV_RANK, MLA_HEADS, QK_NOPE)
    dwv = _unpad_heads(_mm_tn(ckvn, dv_pad, name=f"{tag}_dw_v_up"), V_HEAD).reshape(KV_RANK, MLA_HEADS, V_HEAD)
    grads["ev_w_kv_up"] = [_owner_major(jnp.concatenate([dwk, dwv], axis=2).reshape(KV_RANK, -1), 1)]
    dckvn = _mm(dk_cat, [(w_kv_pad, 0, 0)], _first, [(KV_RANK, F32, 0)], nt=True, tn=KV_RANK, nj=1,
                name=f"{tag}_dckvn_k")[0]
    dckvn = _mm(dv_pad, [(w_kv_pad, 0, 1)], _add_res, [(KV_RANK, F32, 0)], nt=True, extras=[(dckvn, 0)], tn=KV_RANK,
                nj=1, name=f"{tag}_dckvn_v")[0]
    dcq, dgq = _rms_bwd(z, W["ev_q_norm"][0], dcqn, cb=2, w=Q_RANK, out_dtype=BF16, name=f"{tag}_q_norm_bwd")
    dckv, dgkv = _rms_bwd(z, W["ev_kv_norm"][0], dckvn, cb=6, w=KV_RANK, out_dtype=BF16, name=f"{tag}_kv_norm_bwd")
    grads["ev_q_norm"], grads["ev_kv_norm"] = dgq, dgkv
    du, dypre, dscale = _pool_bwd(dmix, pooled, pool_w, pool_scale, name=f"{tag}_pool_bwd")
    grads["ev_pool_scale"] = dscale
    grads["ev_pool_w"] = _mm_tn_grouped(pooled, dypre, 4, POOL_GROUP, name=f"{tag}_dpool_w")[None]
    dz = jnp.concatenate([du, dcq, dckv, dkr], axis=1)
    dw_in_pad = _mm_tn(h, dz, name=f"{tag}_dw_in")
    grads["ev_w_in"] = [_owner_major(jnp.concatenate([dw_in_pad[:, :896], dw_in_pad[:, 960:992]], axis=1), 0)]
    dx, dgn = _mm(dz, [(w_in_pad, 0, 0)], _norm_bwd_epilogue(0), [(D_MODEL, F32, 0)], nt=True,
                  extras=[(x, 0), (dxo, 0)], rows=[W["ev_norm"][0].reshape(1, D_MODEL)], sums=[D_MODEL], tn=D_MODEL,
                  nj=1, name=f"{tag}_dh")
    grads["ev_norm"] = dgn
    return dx


def _odd_block(x, reset, W, tag):
    h = _rms(x, W["od_norm"][0], name=f"{tag}_norm")
    z = _mm(h, [(W["od_w_in"][0], 0, 0)], _first, [(2 * D_MODEL, F32, 0)], tn=D_MODEL, nj=2, name=f"{tag}_in")[0]
    w_r, w_i = W["od_w_rgate"][0], W["od_w_igate"][0]
    vecs = [W[n].reshape(1, D_MODEL) for n in ("od_conv_b", "od_b_rgate", "od_b_igate", "od_lambda")]
    xc, hseq, y = _lru_fwd(z, reset, W["od_conv_w"][0], vecs[0], w_r, vecs[1], w_i, vecs[2], vecs[3],
                           name=f"{tag}_lru")
    xo = _mm(y, [(W["od_w_out"][0], 0, 0)], _add_res, [(D_MODEL, F32, 0)], extras=[(x, 0)], tn=D_MODEL, nj=1,
             name=f"{tag}_out")[0]
    return xo, (x, h, z, xc, hseq, y, vecs)


def _odd_block_bwd(dxo, reset, W, saved, tag, grads):
    x, h, z, xc, hseq, y, vecs = saved
    w_r, w_i = W["od_w_rgate"][0], W["od_w_igate"][0]
    dy = _mm(dxo, [(W["od_w_out"][0], 0, 0)], _first, [(D_MODEL, F32, 0)], nt=True, tn=D_MODEL, nj=1,
             name=f"{tag}_dy")[0]
    grads["od_w_out"] = [_owner_major(_mm_tn(y, dxo, name=f"{tag}_dw_out"), 0)]
    dgate, dxc, dpr, dpi, acc = _lru_bwd(dy, z, xc, hseq, reset, w_r, vecs[1], w_i, vecs[2], vecs[3],
                                         name=f"{tag}_lru_bwd")
    grads["od_b_rgate"], grads["od_b_igate"], grads["od_lambda"] = acc[0:1], acc[1:2], acc[2:3]
    grads["od_w_rgate"] = [_owner_major(_mm_tn_grouped(xc, dpr, LRU_HEADS, LRU_HEAD_DIM, name=f"{tag}_dw_rgate"), 1)]
    grads["od_w_igate"] = [_owner_major(_mm_tn_grouped(xc, dpi, LRU_HEADS, LRU_HEAD_DIM, name=f"{tag}_dw_igate"), 1)]
    dxb, cacc = _conv_bwd(dxc, z, W["od_conv_w"][0], name=f"{tag}_conv_bwd")
    grads["od_conv_w"], grads["od_conv_b"] = cacc[None, 0:4], cacc[4:5]
    dz = jnp.concatenate([dgate, dxb], axis=1)
    grads["od_w_in"] = [_mm_tn_owners(h, [dz], name=f"{tag}_dw_in")]
    dx, dgn = _mm(dz, [(W["od_w_in"][0], 0, 0)], _norm_bwd_epilogue(0), [(D_MODEL, F32, 0)], nt=True,
                  extras=[(x, 0), (dxo, 0)], rows=[W["od_norm"][0].reshape(1, D_MODEL)], sums=[D_MODEL], tn=D_MODEL,
                  nj=1, name=f"{tag}_dh")
    grads["od_norm"] = dgn
    return dx


def _local_step(x, mem, positions, target, W, later_weights=None, exchange_earlier=None):
    tabs = _rope_tables(positions)
    reset = (positions == 0).astype(F32)[:, None]
    grads = {n: [None, None] for n in ("xa_norm_x", "xa_norm_mem", "xa_w_q", "xa_w_kv", "xa_w_o", "ffn_norm",
                                       "ffn_w_gate_up", "ffn_w_down")}
    x1, s_even = _even_block(x, tabs, W, "l0_even")
    if later_weights is not None:
        W = {**W, **later_weights(x1)}
    x2, s_xa0 = _memory_block(x1, mem, W, 0, "l0")
    x3, s_ff0 = _ffn_block(x2, W, 0, "l0")
    x4, s_odd = _odd_block(x3, reset, W, "l1_odd")
    x5, s_xa1 = _memory_block(x4, mem, W, 1, "l1")
    x6, s_ff1 = _ffn_block(x5, W, 1, "l1")
    d, dgf, loss = _loss_head(x6, W["final_norm"], target, name="loss_head")
    grads["final_norm"] = dgf[0]
    d = _ffn_block_bwd(d, W, 1, s_ff1, "l1", grads)
    d = _memory_block_bwd(d, mem, W, 1, s_xa1, "l1", grads)
    d = _odd_block_bwd(d, reset, W, s_odd, "l1_odd", grads)
    d = _ffn_block_bwd(d, W, 0, s_ff0, "l0", grads)
    d = _memory_block_bwd(d, mem, W, 0, s_xa0, "l0", grads)
    token = exchange_earlier(grads) if exchange_earlier is not None else None
    d = _even_block_bwd(d, tabs, W, s_even, "l0_even", grads, token)
    big = {n: grads.pop(n) for n in MATMUL_WEIGHTS}
    for n, v in grads.items():
        if isinstance(v, list):
            grads[n] = jnp.stack(v)
    return loss[0, 0], d, big, grads


WEIGHTS = ("ev_norm", "ev_w_in", "ev_pool_w", "ev_pool_scale", "ev_q_norm", "ev_w_q_up", "ev_kv_norm", "ev_w_kv_up",
           "ev_w_out", "od_norm", "od_w_in", "od_conv_w", "od_conv_b", "od_w_rgate", "od_b_rgate", "od_w_igate",
           "od_b_igate", "od_lambda", "od_w_out", "xa_norm_x", "xa_norm_mem", "xa_w_q", "xa_w_kv", "xa_w_o",
           "ffn_norm", "ffn_w_gate_up", "ffn_w_down", "final_norm")
SHARD_AXIS = {"ev_w_in": 1, "ev_w_q_up": 2, "ev_w_kv_up": 2, "ev_w_out": 1, "od_norm": 1, "od_w_in": 2,
              "od_conv_w": 2, "od_conv_b": 1, "od_w_rgate": 2, "od_b_rgate": 1, "od_w_igate": 2, "od_b_igate": 1,
              "od_lambda": 1, "od_w_out": 1, "xa_w_q": 1, "xa_w_kv": 2, "xa_w_o": 1, "ffn_w_gate_up": 2,
              "ffn_w_down": 1}
MATMUL_WEIGHTS = ("ev_w_in", "ev_w_q_up", "ev_w_kv_up", "ev_w_out", "od_w_in", "od_w_rgate", "od_w_igate",
                  "od_w_out", "xa_w_q", "xa_w_kv", "xa_w_o", "ffn_w_gate_up", "ffn_w_down")
SMALL_SHARDED = tuple(n for n in WEIGHTS if n in SHARD_AXIS and n not in MATMUL_WEIGHTS)
REPLICATED = tuple(n for n in WEIGHTS if n not in SHARD_AXIS)


def _pack(parts, quantum):
    flat = jnp.concatenate([p.reshape(-1) for p in parts])
    pad = (-flat.shape[0]) % quantum
    return jnp.pad(flat, (0, pad)).reshape(-1, LANES)


def _unpack(flat, shapes):
    out, off = [], 0
    for shape in shapes:
        size = math.prod(shape)
        out.append(flat[off:off + size].reshape(shape))
        off += size
    return out


def _run_copies(local, remote, send_sems, recv_sems, local_sems):
    locals_ = [pltpu.make_async_copy(src, dst, local_sems.at[n]) for n, (src, dst) in enumerate(local)]
    for cp in locals_:
        cp.start()
    sends = [pltpu.make_async_remote_copy(src_ref=src, dst_ref=dst, send_sem=send_sems.at[k, n],
                                          recv_sem=recv_sems.at[k, n], device_id=dev, device_id_type=MESH)
             for (k, n, src, dst, _, dev) in remote]
    for cp in sends:
        cp.start()
    for (k, n, src, _, arrival, dev) in remote:
        pltpu.make_async_remote_copy(src_ref=src, dst_ref=arrival, send_sem=send_sems.at[k, n],
                                     recv_sem=recv_sems.at[k, n], device_id=dev, device_id_type=MESH).wait_recv()
    for cp in sends:
        cp.wait_send()
    for cp in locals_:
        cp.wait()


def _chip_peers(x, y):
    return [(1 - x, y), (x, 1 - y), (1 - x, 1 - y)]


def _owner_block(ref, axis, q):
    size = ref.shape[axis] // N_CHIPS
    idx = [slice(None)] * len(ref.shape)
    idx[axis] = pl.ds(q * size, size)
    return ref.at[tuple(idx)]


def _comm_call(body, ins, out_shapes, n_items, n_peers, *, name):
    return pl.pallas_call(
        body, in_specs=[ANY] * len(ins), out_specs=[ANY] * len(out_shapes), out_shape=out_shapes,
        scratch_shapes=[pltpu.SemaphoreType.DMA((n_peers, n_items)), pltpu.SemaphoreType.DMA((n_peers, n_items)),
                        pltpu.SemaphoreType.DMA((n_items,))],
        name=name,
    )(*ins)


def _gather_chips(shards, axes, *, name):
    n = len(shards)
    full = [jax.ShapeDtypeStruct(tuple(d * (N_CHIPS if a == ax else 1) for a, d in enumerate(s.shape)), s.dtype)
            for s, ax in zip(shards, axes)]

    def body(*refs):
        srcs, dsts = refs[:n], refs[n:2 * n]
        x, y, c = lax.axis_index("x"), lax.axis_index("y"), lax.axis_index("c")
        me = 2 * x + y
        local = [(srcs[i], _owner_block(dsts[i], axes[i], me)) for i in range(n)]
        remote = [(k, i, srcs[i], _owner_block(dsts[i], axes[i], me), _owner_block(dsts[i], axes[i], 2 * px + py),
                   (px, py, c))
                  for k, (px, py) in enumerate(_chip_peers(x, y)) for i in range(n)]
        _run_copies(local, remote, *refs[2 * n:])

    return _comm_call(body, shards, full, n, 3, name=name)


def _exchange_chips(items, *, name):
    flat = [(n, l, a) for n, layers in enumerate(items) for l, a in enumerate(layers)]
    outs = [jax.ShapeDtypeStruct((N_CHIPS, len(layers)) + layers[0].shape[1:], layers[0].dtype) for layers in items]
    ni = len(flat)

    def body(*refs):
        srcs, dsts = refs[:ni], refs[ni:ni + len(items)]
        x, y, c = lax.axis_index("x"), lax.axis_index("y"), lax.axis_index("c")
        me = 2 * x + y
        local = [(srcs[i].at[me], dsts[n].at[me, l]) for i, (n, l, _) in enumerate(flat)]
        remote = [(k, i, srcs[i].at[2 * px + py], dsts[n].at[me, l], dsts[n].at[2 * px + py, l], (px, py, c))
                  for k, (px, py) in enumerate(_chip_peers(x, y)) for i, (n, l, _) in enumerate(flat)]
        _run_copies(local, remote, *refs[ni + len(items):])

    return _comm_call(body, [a for (_, _, a) in flat], outs, ni, 3, name=name)


HBM = pl.BlockSpec(memory_space=pltpu.HBM)
SEM = pl.BlockSpec(memory_space=pltpu.SEMAPHORE)
DATAFLOW = pltpu.SideEffectType.DATAFLOW_SIDE_EFFECTING


def _gather_plan(axes):
    return lambda srcs, lands, me, peer: [
        (srcs[i], _owner_block(lands[i], ax, me), _owner_block(lands[i], ax, peer)) for i, ax in enumerate(axes)]


def _exchange_plan(where):
    return lambda srcs, lands, me, peer: [
        (srcs[i].at[peer], lands[n].at[me, l], lands[n].at[peer, l]) for i, (n, l) in enumerate(where)]


def _split_start(srcs, lands, plan, *, name):
    ns, nl = len(srcs), len(lands)
    nsem = 3 * len(plan(list(srcs), list(lands), 0, 0))

    def body(*refs):
        src_refs, land_refs = refs[:ns], refs[ns:ns + nl]
        send_sems, recv_sems = refs[ns + nl:ns + nl + nsem], refs[ns + nl + nsem:ns + nl + 2 * nsem]
        x, y, c = lax.axis_index("x"), lax.axis_index("y"), lax.axis_index("c")
        n = 0
        for px, py in _chip_peers(x, y):
            for src, dst, _ in plan(src_refs, land_refs, 2 * x + y, 2 * px + py):
                pltpu.make_async_remote_copy(src_ref=src, dst_ref=dst, send_sem=send_sems[n], recv_sem=recv_sems[n],
                                             device_id=(px, py, c), device_id_type=MESH).start()
                n += 1
        refs[-1][...] = jnp.zeros_like(refs[-1])

    arrays = list(srcs) + list(lands)
    out = pl.pallas_call(
        body, name=name, in_specs=[HBM] * (ns + nl),
        out_specs=[SEM] * (2 * nsem) + [HBM] * (ns + nl) + [pl.BlockSpec(memory_space=pltpu.VMEM)],
        out_shape=[pltpu.SemaphoreType.DMA(())] * (2 * nsem) + [pltpu.HBM(a.shape, a.dtype) for a in arrays]
        + [jax.ShapeDtypeStruct((8, LANES), F32)],
        input_output_aliases={i: 2 * nsem + i for i in range(ns + nl)},
        compiler_params=pltpu.CompilerParams(has_side_effects=DATAFLOW),
    )(*[pltpu.with_memory_space_constraint(a, pltpu.HBM) for a in arrays])
    sems, rest = out[:2 * nsem], out[2 * nsem:]
    return sems[:nsem], sems[nsem:], rest[:ns], rest[ns:ns + nl], rest[-1]


def _split_wait(handle, after, plan, *, name):
    send_sems, recv_sems, srcs, lands, _ = handle
    ns, nl, nsem = len(srcs), len(lands), len(send_sems)

    def body(*refs):
        src_refs, land_refs = refs[:ns], refs[ns:ns + nl]
        send_refs, recv_refs = refs[ns + nl:ns + nl + nsem], refs[ns + nl + nsem:ns + nl + 2 * nsem]
        x, y, c = lax.axis_index("x"), lax.axis_index("y"), lax.axis_index("c")
        n = 0
        for px, py in _chip_peers(x, y):
            for src, _, arrival in plan(src_refs, land_refs, 2 * x + y, 2 * px + py):
                cp = pltpu.make_async_remote_copy(src_ref=src, dst_ref=arrival, send_sem=send_refs[n],
                                                  recv_sem=recv_refs[n], device_id=(px, py, c), device_id_type=MESH)
                cp.wait_send()
                cp.wait_recv()
                n += 1

    out = pl.pallas_call(
        body, name=name, in_specs=[HBM] * (ns + nl) + [SEM] * (2 * nsem) + [ANY], out_specs=[HBM] * (ns + nl),
        out_shape=[pltpu.HBM(a.shape, a.dtype) for a in list(srcs) + list(lands)],
        input_output_aliases={i: i for i in range(ns + nl)},
        compiler_params=pltpu.CompilerParams(has_side_effects=DATAFLOW),
    )(*srcs, *lands, *send_sems, *recv_sems, after)
    return out[ns:]


def _exchange_sibling(arrays, *, name):
    n = len(arrays)

    def body(*refs):
        x, y, c = lax.axis_index("x"), lax.axis_index("y"), lax.axis_index("c")
        remote = [(0, i, refs[i], refs[n + i], refs[n + i], (x, y, 1 - c)) for i in range(n)]
        _run_copies([], remote, *refs[2 * n:])

    return _comm_call(body, arrays, [jax.ShapeDtypeStruct(a.shape, a.dtype) for a in arrays], n, 1, name=name)


def _sum_slots(r, *, name):
    shape = r.shape[1:]
    cols = shape[-1]
    rows = math.prod(shape) // cols
    tr = max(d for d in range(8, rows + 1, 8) if rows % d == 0 and d * cols * 16 <= (4 << 20))

    def body(r_ref, o_ref):
        o_ref[...] = ((r_ref[0] + r_ref[1]) + r_ref[2]) + r_ref[3]

    return pl.pallas_call(
        body, grid=(rows // tr,), in_specs=[pl.BlockSpec((N_CHIPS, tr, cols), lambda i: (0, i, 0))],
        out_specs=_rows(tr, cols), out_shape=jax.ShapeDtypeStruct((rows, cols), F32), compiler_params=_cp(1),
        name=name,
    )(r.reshape(N_CHIPS, rows, cols)).reshape(shape)


FIRST_WEIGHTS = ("ev_w_in", "ev_w_q_up", "ev_w_kv_up", "ev_w_out")
LATER_WEIGHTS = tuple(n for n in MATMUL_WEIGHTS if n not in FIRST_WEIGHTS)
LAST_GRADS = FIRST_WEIGHTS
EARLIER_GRADS = tuple(n for n in MATMUL_WEIGHTS if n not in LAST_GRADS)


def _my_chip():
    return 2 * lax.axis_index("x") + lax.axis_index("y")


def _gather_first(w):
    small = _pack([w[n] for n in SMALL_SHARDED], 8 * LANES)
    stacked = [n for n in FIRST_WEIGHTS if SHARD_AXIS[n] == w[n].ndim - 1 and w[n].shape[-1] % LANES]
    shards = [w[n].astype(BF16)[None] if n in stacked else w[n].astype(BF16) for n in FIRST_WEIGHTS]
    got = _gather_chips(shards + [small], [0 if n in stacked else SHARD_AXIS[n] for n in FIRST_WEIGHTS] + [0],
                        name="gather_first")
    full = {n: w[n] for n in REPLICATED}
    for n, g in zip(FIRST_WEIGHTS, got[:-1]):
        full[n] = jnp.concatenate([g[q] for q in range(N_CHIPS)], axis=SHARD_AXIS[n]) if n in stacked else g
    per_chip = [_unpack(got[-1][q * small.shape[0]:(q + 1) * small.shape[0]].reshape(-1),
                        [w[n].shape for n in SMALL_SHARDED]) for q in range(N_CHIPS)]
    for i, n in enumerate(SMALL_SHARDED):
        full[n] = jnp.concatenate([per_chip[q][i] for q in range(N_CHIPS)], axis=SHARD_AXIS[n])
    return full


def _gather_later_start(w):
    shards = [w[n].astype(BF16) for n in LATER_WEIGHTS]
    axes = [SHARD_AXIS[n] for n in LATER_WEIGHTS]
    lands = []
    for s, ax in zip(shards, axes):
        shape = tuple(d * (N_CHIPS if a == ax else 1) for a, d in enumerate(s.shape))
        lands.append(lax.dynamic_update_slice_in_dim(lax.empty(shape, s.dtype), s, _my_chip() * s.shape[ax], ax))
    return _split_start(shards, lands, _gather_plan(axes), name="gather_later_start"), _gather_plan(axes)


def _owner_major(g, axis):
    shape = g.shape
    size = shape[axis] // N_CHIPS
    g = jnp.moveaxis(g.reshape(shape[:axis] + (N_CHIPS, size) + shape[axis + 1:]), axis, 0)
    return g.reshape(N_CHIPS, -1, shape[-1] if axis < len(shape) - 1 else size)


def _exchange_earlier_start(grads, full_shapes):
    small = [_pack([jnp.split(grads[n].reshape(full_shapes[n]), N_CHIPS, axis=SHARD_AXIS[n])[q]
                    for n in SMALL_SHARDED], 8 * LANES) for q in range(N_CHIPS)]
    items = [grads[n] for n in EARLIER_GRADS] + [[jnp.stack(small)]]
    me = _my_chip()
    srcs, lands, where = [], [], []
    for n, layers in enumerate(items):
        land = lax.empty((N_CHIPS, len(layers)) + layers[0].shape[1:], layers[0].dtype)
        for l, a in enumerate(layers):
            own = lax.dynamic_index_in_dim(a, me, 0, keepdims=True)[:, None]
            land = lax.dynamic_update_slice(land, own, (me, l) + (0,) * (a.ndim - 1))
            srcs.append(a)
            where.append((n, l))
        lands.append(land)
    plan = _exchange_plan(where)
    return _split_start(srcs, lands, plan, name="exchange_earlier_start"), plan


def _exchange_last(big, grads, full_shapes, loss):
    repl = _pack([grads[n].reshape(full_shapes[n]) for n in REPLICATED] + [loss.reshape(1)], 8 * LANES)
    return _exchange_chips([big[n] for n in LAST_GRADS] + [[jnp.stack([repl] * N_CHIPS)]], name="exchange_last")


def kernel(
        x, mem, positions, ev_norm, ev_w_in, ev_pool_w, ev_pool_scale, ev_q_norm, ev_w_q_up, ev_kv_norm,
        ev_w_kv_up, ev_w_out, od_norm, od_w_in, od_conv_w, od_conv_b, od_w_rgate, od_b_rgate, od_w_igate,
        od_b_igate, od_lambda, od_w_out, xa_norm_x, xa_norm_mem, xa_w_q, xa_w_kv, xa_w_o, ffn_norm,
        ffn_w_gate_up, ffn_w_down, final_norm, loss_target, m_ev_norm, m_ev_w_in, m_ev_pool_w, m_ev_pool_scale,
        m_ev_q_norm, m_ev_w_q_up, m_ev_kv_norm, m_ev_w_kv_up, m_ev_w_out, m_od_norm, m_od_w_in, m_od_conv_w,
        m_od_conv_b, m_od_w_rgate, m_od_b_rgate, m_od_w_igate, m_od_b_igate, m_od_lambda, m_od_w_out,
        m_xa_norm_x, m_xa_norm_mem, m_xa_w_q, m_xa_w_kv, m_xa_w_o, m_ffn_norm, m_ffn_w_gate_up, m_ffn_w_down,
        m_final_norm, v_ev_norm, v_ev_w_in, v_ev_pool_w, v_ev_pool_scale, v_ev_q_norm, v_ev_w_q_up,
        v_ev_kv_norm, v_ev_w_kv_up, v_ev_w_out, v_od_norm, v_od_w_in, v_od_conv_w, v_od_conv_b, v_od_w_rgate,
        v_od_b_rgate, v_od_w_igate, v_od_b_igate, v_od_lambda, v_od_w_out, v_xa_norm_x, v_xa_norm_mem, v_xa_w_q,
        v_xa_w_kv, v_xa_w_o, v_ffn_norm, v_ffn_w_gate_up, v_ffn_w_down, v_final_norm):
    given = dict(locals())
    w = {n: given[n] for n in WEIGHTS}
    full_shapes = {n: tuple(d * (N_CHIPS if a == SHARD_AXIS.get(n) else 1) for a, d in enumerate(w[n].shape))
                   for n in WEIGHTS}
    full = _gather_first(w)
    later, later_plan = _gather_later_start(w)
    full["ev_norm"] = full["ev_norm"] + later[4][0:1, 0:1]
    exchange = {}

    def later_weights(after):
        return dict(zip(LATER_WEIGHTS, _split_wait(later, after, later_plan, name="gather_later_wait")))

    def exchange_earlier(grads):
        exchange["handle"], exchange["plan"] = _exchange_earlier_start(grads, full_shapes)
        return exchange["handle"][4]

    loss, grad_x, big, grads = _local_step(x[0], mem[0], positions[0], loss_target[0], full, later_weights,
                                           exchange_earlier)
    got = dict(zip(EARLIER_GRADS + ("small",),
                   _split_wait(exchange["handle"], grad_x, exchange["plan"], name="exchange_earlier_wait")))
    got.update(zip(LAST_GRADS + ("replicated",), _exchange_last(big, grads, full_shapes, loss)))
    order = MATMUL_WEIGHTS + ("small", "replicated")
    mine = [_sum_slots(got[n], name=f"sum_chips_{n}") for n in order]
    other = _exchange_sibling(mine, name="exchange_sibling")
    out = {}
    for i, n in enumerate(MATMUL_WEIGHTS):
        out[n] = _adamw(w[n], mine[i].reshape(w[n].shape), other[i].reshape(w[n].shape), given["m_" + n],
                        given["v_" + n], name=f"adamw_{n}")
    for i, group in ((len(MATMUL_WEIGHTS), SMALL_SHARDED), (len(MATMUL_WEIGHTS) + 1, REPLICATED)):
        spare = [jnp.zeros((1,), F32)] if group is REPLICATED else []
        packed = [_pack([given[pre + n] for n in group] + spare, 8 * LANES) for pre in ("", "m_", "v_")]
        res = _adamw(packed[0], mine[i].reshape(packed[0].shape), other[i].reshape(packed[0].shape), packed[1],
                     packed[2], name=f"adamw_group{i}")
        shapes = [w[n].shape for n in group] + [(1,)] * len(spare)
        for j, arrs in enumerate(zip(*[_unpack(r.reshape(-1), shapes) for r in res])):
            if j < len(group):
                out[group[j]] = list(arrs)
            else:
                loss = arrs[0][0]
    return (loss, grad_x[None], *[out[n][k] for k in range(4) for n in WEIGHTS])
```

```python
import functools
import math

import jax
import jax.numpy as jnp
from jax import lax
from jax.experimental import pallas as pl
from jax.experimental.pallas import tpu as pltpu

F32 = jnp.float32
BF16 = jnp.bfloat16

D_MODEL = 1024
POOL_DIM = 512
POOL_WINDOWS = (2, 4, 8, 16)
POOL_GROUP = 128
MLA_HEADS = 8
QK_NOPE = 64
QK_ROPE = 32
QK_DIM = QK_NOPE + QK_ROPE
V_HEAD = 64
HEAD_PAD = 128
Q_RANK = 256
KV_RANK = 128
ROPE_BASE = 10000.0
LRU_HEADS = 4
LRU_HEAD_DIM = 256
CONV_WIDTH = 4
LRU_C = 8.0
MEM_HEADS = 4
MEM_HEAD_DIM = 256
D_FF = 2816
RMS_EPS = 1e-6
NEG_INF = -1e30

ADAM_LR = 0.001
ADAM_B1 = 0.9
ADAM_B2 = 0.999
ADAM_EPS = 1e-08
ADAM_WD = 0.01
ADAM_STEP = 10

N_CHIPS = 4
LANES = 128
VMEM_LIMIT = 56 * 1024 * 1024
MESH = pl.DeviceIdType.MESH
ANY = pl.BlockSpec(memory_space=pl.ANY)
MIX_DIM = POOL_DIM + MLA_HEADS * HEAD_PAD

NN = (((1,), (0,)), ((), ()))
NT = (((1,), (1,)), ((), ()))
TN = (((0,), (0,)), ((), ()))


def _cp(n):
    return pltpu.CompilerParams(dimension_semantics=("arbitrary",) * n, vmem_limit_bytes=VMEM_LIMIT)


def _dot(a, b, dims=NN):
    return lax.dot_general(a, b, dims, preferred_element_type=F32)


def _rows(ts, w, cb=0):
    return pl.BlockSpec((ts, w), lambda i: (i, cb))


def _const(shape):
    return pl.BlockSpec(shape, lambda i: (0,) * len(shape))


def _mm(a, bs, epi, outs, *, tn, nj, nt=False, extras=(), rows=(), sums=(), a_cb=0, k=None, tm=None, name):
    M = a.shape[0]
    k = k or a.shape[1]
    tm = tm or min(M, 512)
    nb, ne, nr, no = len(bs), len(extras), len(rows), len(outs)
    dims = NT if nt else NN
    assert not sums or nj == 1

    def body(*refs):
        av = refs[0][...].astype(BF16)
        accs = [_dot(av, r[...].astype(BF16), dims) for r in refs[1:1 + nb]]
        vals = epi(accs, [r[...] for r in refs[1 + nb:1 + nb + ne + nr]])
        outs_refs = refs[1 + nb + ne + nr:]
        for o, v in zip(outs_refs[:no], vals[:no]):
            o[...] = v.astype(o.dtype)
        if sums:
            @pl.when(pl.program_id(1) == 0)
            def _():
                for o in outs_refs[no:]:
                    o[...] = jnp.zeros_like(o)

            for o, v in zip(outs_refs[no:], vals[no:]):
                o[...] += v

    in_specs = [pl.BlockSpec((tm, k), lambda j, i: (i, a_cb))]
    for (_, rb, cb) in bs:
        if nt:
            in_specs.append(pl.BlockSpec((tn, k), lambda j, i, rb=rb, cb=cb: (rb + j, cb)))
        else:
            in_specs.append(pl.BlockSpec((k, tn), lambda j, i, rb=rb, cb=cb: (rb, cb + j)))
    for (_, cb) in extras:
        in_specs.append(pl.BlockSpec((tm, tn), lambda j, i, cb=cb: (i, cb + j)))
    in_specs += [pl.BlockSpec((1, tn), lambda j, i: (0, 0))] * nr
    out_specs = [pl.BlockSpec((tm, tn), lambda j, i, cb=cb: (i, cb + j)) for (_, _, cb) in outs]
    out_specs += [pl.BlockSpec((1, w), lambda j, i: (0, 0)) for w in sums]
    res = pl.pallas_call(
        body, grid=(nj, M // tm), in_specs=in_specs, out_specs=out_specs,
        out_shape=[jax.ShapeDtypeStruct((M, n), dt) for (n, dt, _) in outs]
        + [jax.ShapeDtypeStruct((1, w), F32) for w in sums],
        compiler_params=_cp(2), name=name,
    )(a, *[b for (b, _, _) in bs], *[e for (e, _) in extras], *rows)
    return res


def _first(accs, extras):
    return [accs[0]]


def _add_res(accs, extras):
    return [accs[0] + extras[0].astype(F32)]


def _norm_bwd_epilogue(partials):
    def epi(accs, vals):
        dh = accs[0]
        for part in vals[:partials]:
            dh = dh + part.astype(F32)
        x, res, g = vals[partials:partials + 3]
        r = lax.rsqrt(jnp.mean(x * x, axis=-1, keepdims=True) + RMS_EPS)
        n = x * r
        dn = dh * g
        return [r * (dn - n * jnp.mean(dn * n, axis=-1, keepdims=True)) + res, jnp.sum(dh * n, axis=0, keepdims=True)]

    return epi


TN_VMEM_BUDGET = 36 * 1024 * 1024


def _contraction_rows(S, row_bytes, out_elems):
    ts = min(S, 2048)
    while ts > 512 and 2 * (ts * row_bytes + out_elems * 4) > TN_VMEM_BUDGET:
        ts //= 2
    return ts


def _mm_tn(a, b, *, ka=None, a_cb=0, nb=None, b_cb=0, tk=None, tn=None, ts=None, name):
    S = a.shape[0]
    ka = ka or a.shape[1]
    nb = nb or b.shape[1]
    tk = tk or ka
    tn = tn or nb
    ts = ts or _contraction_rows(S, tk * a.dtype.itemsize + tn * b.dtype.itemsize, tk * tn)
    a0, b0 = a_cb * (ka // tk), b_cb * (nb // tn)

    def body(a_ref, b_ref, o_ref):
        @pl.when(pl.program_id(2) == 0)
        def _():
            o_ref[...] = jnp.zeros_like(o_ref)

        o_ref[...] += _dot(a_ref[...].astype(BF16), b_ref[...].astype(BF16), TN)

    return pl.pallas_call(
        body, grid=(ka // tk, nb // tn, S // ts),
        in_specs=[pl.BlockSpec((ts, tk), lambda p, q, s: (s, a0 + p)),
                  pl.BlockSpec((ts, tn), lambda p, q, s: (s, b0 + q))],
        out_specs=pl.BlockSpec((tk, tn), lambda p, q, s: (p, q)),
        out_shape=jax.ShapeDtypeStruct((ka, nb), F32), compiler_params=_cp(3), name=name,
    )(a, b)


def _mm_tn_owners(a, bs, *, name):
    S, ka = a.shape
    nb = sum(b.shape[1] for b in bs)
    tn = nb // N_CHIPS
    ts = _contraction_rows(S, ka * a.dtype.itemsize + len(bs) * tn * bs[0].dtype.itemsize, ka * tn)
    per = N_CHIPS // len(bs)

    def body(a_ref, *refs):
        o_ref = refs[-1]
        q = pl.program_id(0)

        @pl.when(pl.program_id(1) == 0)
        def _():
            o_ref[...] = jnp.zeros_like(o_ref)

        av = a_ref[...].astype(BF16)
        for n, b_ref in enumerate(refs[:-1]):
            @pl.when(q // per == n)
            def _():
                o_ref[0] += _dot(av, b_ref[...].astype(BF16), TN)

    in_specs = [pl.BlockSpec((ts, ka), lambda q, s: (s, 0))]
    for n in range(len(bs)):
        in_specs.append(pl.BlockSpec((ts, tn), lambda q, s, n=n: (jnp.where(q // per == n, s, 0),
                                                                  jnp.clip(q - n * per, 0, per - 1))))
    return pl.pallas_call(
        body, grid=(N_CHIPS, S // ts), in_specs=in_specs,
        out_specs=pl.BlockSpec((1, ka, tn), lambda q, s: (q, 0, 0)),
        out_shape=jax.ShapeDtypeStruct((N_CHIPS, ka, tn), F32), compiler_params=_cp(2), name=name,
    )(a, *bs)


def _mm_tn_grouped(a, b, groups, w, *, name):
    S = a.shape[0]
    ts = _contraction_rows(S, w * (a.dtype.itemsize + b.dtype.itemsize), w * w)

    def body(a_ref, b_ref, o_ref):
        @pl.when(pl.program_id(1) == 0)
        def _():
            o_ref[...] = jnp.zeros_like(o_ref)

        o_ref[0] += _dot(a_ref[...].astype(BF16), b_ref[...].astype(BF16), TN)

    return pl.pallas_call(
        body, grid=(groups, S // ts),
        in_specs=[pl.BlockSpec((ts, w), lambda g, s: (s, g)), pl.BlockSpec((ts, w), lambda g, s: (s, g))],
        out_specs=pl.BlockSpec((1, w, w), lambda g, s: (g, 0, 0)),
        out_shape=jax.ShapeDtypeStruct((groups, w, w), F32), compiler_params=_cp(2), name=name,
    )(a, b)


def _rms(x, g, *, cb=0, w=None, ts=None, name):
    S = x.shape[0]
    w = w or x.shape[1]
    ts = ts or min(S, 512)

    def body(x_ref, g_ref, o_ref):
        xv = x_ref[...].astype(F32)
        r = lax.rsqrt(jnp.mean(xv * xv, axis=-1, keepdims=True) + RMS_EPS)
        o_ref[...] = (xv * r * g_ref[...]).astype(o_ref.dtype)

    return pl.pallas_call(
        body, grid=(S // ts,), in_specs=[_rows(ts, w, cb), _const((1, w))], out_specs=_rows(ts, w),
        out_shape=jax.ShapeDtypeStruct((S, w), BF16), compiler_params=_cp(1), name=name,
    )(x, g.reshape(1, w))


def _rms_bwd(x, g, dy, *, cb=0, w=None, res=None, out_dtype=F32, ts=None, name):
    S = x.shape[0]
    w = w or x.shape[1]
    ts = ts or min(S, 512)
    has_res = res is not None

    def body(*refs):
        x_ref, g_ref, dy_ref = refs[:3]
        dx_ref, dg_ref = refs[-2:]
        xv = x_ref[...].astype(F32)
        r = lax.rsqrt(jnp.mean(xv * xv, axis=-1, keepdims=True) + RMS_EPS)
        n = xv * r
        dyv = dy_ref[...].astype(F32)
        dn = dyv * g_ref[...]
        dx = r * (dn - n * jnp.mean(dn * n, axis=-1, keepdims=True))
        if has_res:
            dx = dx + refs[3][...].astype(F32)
        dx_ref[...] = dx.astype(dx_ref.dtype)

        @pl.when(pl.program_id(0) == 0)
        def _():
            dg_ref[...] = jnp.zeros_like(dg_ref)

        dg_ref[...] += jnp.sum(dyv * n, axis=0, keepdims=True)

    ins = [x, g.reshape(1, w), dy] + ([res] if has_res else [])
    in_specs = [_rows(ts, w, cb), _const((1, w)), _rows(ts, w)] + ([_rows(ts, w)] if has_res else [])
    return pl.pallas_call(
        body, grid=(S // ts,), in_specs=in_specs, out_specs=[_rows(ts, w), _const((1, w))],
        out_shape=[jax.ShapeDtypeStruct((S, w), out_dtype), jax.ShapeDtypeStruct((1, w), F32)],
        compiler_params=_cp(1), name=name,
    )(*ins)


HALO = 16


def _pool_counts(i, ts, rows, first_row):
    t = i * ts + first_row + lax.broadcasted_iota(jnp.int32, (rows, 1), 0)
    return [jnp.minimum(t + 1, w).astype(F32) for w in POOL_WINDOWS]


def _pool_fwd(z, pool_w, pool_scale, *, name):
    S = z.shape[0]
    ts = min(S, 512)
    nh = ts // HALO

    def body(u_ref, halo_ref, w_ref, sc_ref, y_ref, p_ref):
        i = pl.program_id(0)
        u = u_ref[...]
        halo = jnp.where(i > 0, halo_ref[...], 0.0)
        xe = jnp.concatenate([halo, u], axis=0)
        sums = []
        s = xe
        for sh in (1, 2, 4, 8):
            s = s + pltpu.roll(s, sh, 0)
            sums.append(s)
        cnts = _pool_counts(i, ts, ts, 0)
        for g in range(4):
            lo, hi = g * POOL_GROUP, (g + 1) * POOL_GROUP
            pooled = (sums[g][HALO:, lo:hi] / cnts[g] - u[:, lo:hi]).astype(BF16)
            p_ref[:, lo:hi] = pooled
            y_ref[:, lo:hi] = (_dot(pooled, w_ref[g]) * sc_ref[:, lo:hi]).astype(y_ref.dtype)

    return pl.pallas_call(
        body, grid=(S // ts,),
        in_specs=[_rows(ts, POOL_DIM), pl.BlockSpec((HALO, POOL_DIM), lambda i: (jnp.maximum(i * nh - 1, 0), 0)),
                  _const((4, POOL_GROUP, POOL_GROUP)), _const((1, POOL_DIM))],
        out_specs=[_rows(ts, POOL_DIM), _rows(ts, POOL_DIM)],
        out_shape=[jax.ShapeDtypeStruct((S, MIX_DIM), BF16), jax.ShapeDtypeStruct((S, POOL_DIM), BF16)],
        compiler_params=_cp(1), name=name,
    )(z, z, pool_w, pool_scale)


def _pool_bwd(dmix, pooled, pool_w, pool_scale, *, name):
    S = dmix.shape[0]
    ts = min(S, 512)
    nh = ts // HALO
    last = S // HALO - 1

    def body(dy_ref, dyh_ref, p_ref, w_ref, sc_ref, du_ref, dyp_ref, dsc_ref):
        i = pl.program_id(0)
        dyv = dy_ref[...].astype(F32)
        dyh = jnp.where(i < pl.num_programs(0) - 1, dyh_ref[...].astype(F32), 0.0)
        dye = jnp.concatenate([dyv, dyh], axis=0) * sc_ref[...]
        dypre = dye.astype(BF16)
        dyp_ref[...] = dypre[:ts]
        cnts = _pool_counts(i, ts, ts + HALO, 0)
        n = ts + HALO
        dsc = []
        for g in range(4):
            lo, hi = g * POOL_GROUP, (g + 1) * POOL_GROUP
            ypre = _dot(p_ref[:, lo:hi], w_ref[g])
            dsc.append(jnp.sum(dyv[:, lo:hi] * ypre, axis=0, keepdims=True))
            dpool = _dot(dypre[:, lo:hi], w_ref[g], NT)
            s = dpool / cnts[g]
            for sh in (1, 2, 4, 8)[:g + 1]:
                s = s + pltpu.roll(s, n - sh, 0)
            du_ref[:, lo:hi] = (s[:ts] - dpool[:ts]).astype(du_ref.dtype)

        @pl.when(i == 0)
        def _():
            dsc_ref[...] = jnp.zeros_like(dsc_ref)

        dsc_ref[...] += jnp.concatenate(dsc, axis=1)

    return pl.pallas_call(
        body, grid=(S // ts,),
        in_specs=[_rows(ts, POOL_DIM),
                  pl.BlockSpec((HALO, POOL_DIM), lambda i: (jnp.minimum((i + 1) * nh, last), 0)),
                  _rows(ts, POOL_DIM), _const((4, POOL_GROUP, POOL_GROUP)), _const((1, POOL_DIM))],
        out_specs=[_rows(ts, POOL_DIM), _rows(ts, POOL_DIM), _const((1, POOL_DIM))],
        out_shape=[jax.ShapeDtypeStruct((S, POOL_DIM), BF16)] * 2 + [jax.ShapeDtypeStruct((1, POOL_DIM), F32)],
        compiler_params=_cp(1), name=name,
    )(dmix, dmix, pooled, pool_w, pool_scale)


def _rope_partner(t):
    lane = lax.broadcasted_iota(jnp.int32, t.shape, 1)
    swapped = jnp.where(lane < QK_NOPE + QK_ROPE // 2, pltpu.roll(t, HEAD_PAD - QK_ROPE // 2, 1),
                        pltpu.roll(t, QK_ROPE // 2, 1))
    return jnp.where((lane >= QK_NOPE) & (lane < QK_DIM), swapped, 0.0)


def _rope_fwd(q_pad, k_pad, z, ctab, stab, *, name):
    S = q_pad.shape[0]
    ts = min(S, 512)

    def body(q_ref, k_ref, kr_ref, c_ref, s_ref, qo_ref, ko_ref):
        c, s = c_ref[...], s_ref[...]
        kr = kr_ref[...]
        kr_rot = kr * c + _rope_partner(kr) * s
        for h in range(MLA_HEADS):
            lo, hi = h * HEAD_PAD, (h + 1) * HEAD_PAD
            q = q_ref[:, lo:hi]
            qo_ref[:, lo:hi] = (q * c + _rope_partner(q) * s).astype(qo_ref.dtype)
            ko_ref[:, lo:hi] = (k_ref[:, lo:hi] + kr_rot).astype(ko_ref.dtype)

    wide = _rows(ts, MLA_HEADS * HEAD_PAD)
    return pl.pallas_call(
        body, grid=(S // ts,),
        in_specs=[wide, wide, _rows(ts, HEAD_PAD, 7), _rows(ts, HEAD_PAD), _rows(ts, HEAD_PAD)],
        out_specs=[wide, wide], out_shape=[jax.ShapeDtypeStruct((S, MLA_HEADS * HEAD_PAD), BF16)] * 2,
        compiler_params=_cp(1), name=name,
    )(q_pad, k_pad, z, ctab, stab)


def _rope_bwd(dq_rot, dk_cat, ctab, stab, *, name):
    S = dq_rot.shape[0]
    ts = min(S, 512)

    def body(dq_ref, dk_ref, c_ref, s_ref, dqo_ref, dkr_ref):
        c, s = c_ref[...], s_ref[...]
        for h in range(MLA_HEADS):
            g = dq_ref[:, h * HEAD_PAD:(h + 1) * HEAD_PAD]
            dqo_ref[:, h * HEAD_PAD:(h + 1) * HEAD_PAD] = (g * c + _rope_partner(g * s)).astype(dqo_ref.dtype)
        dk = dk_ref[...]
        g = dk[:, :HEAD_PAD]
        for h in range(1, MLA_HEADS):
            g = g + dk[:, h * HEAD_PAD:(h + 1) * HEAD_PAD]
        lane = lax.broadcasted_iota(jnp.int32, g.shape, 1)
        on_rope = (lane >= QK_NOPE) & (lane < QK_DIM)
        dkr_ref[...] = jnp.where(on_rope, g * c + _rope_partner(g * s), 0.0).astype(dkr_ref.dtype)

    wide = _rows(ts, MLA_HEADS * HEAD_PAD)
    return pl.pallas_call(
        body, grid=(S // ts,), in_specs=[wide, wide, _rows(ts, HEAD_PAD), _rows(ts, HEAD_PAD)],
        out_specs=[wide, _rows(ts, HEAD_PAD)],
        out_shape=[jax.ShapeDtypeStruct((S, MLA_HEADS * HEAD_PAD), BF16), jax.ShapeDtypeStruct((S, HEAD_PAD), BF16)],
        compiler_params=_cp(1), name=name,
    )(dq_rot, dk_cat, ctab, stab)


ATT_SCALE = QK_DIM ** -0.5
LOG2E = math.log2(math.e)


HEADS_PER_STEP = 2
ATT_COL0 = POOL_DIM // HEAD_PAD


def _stat_rows(col):
    return jnp.broadcast_to(col, (col.shape[0], LANES)).T[0:8]


def _flash_fwd(q, k, v, mix, *, name):
    S = q.shape[0]
    tq = min(S, 512)
    nq = S // tq
    hs = HEADS_PER_STEP
    wide = hs * HEAD_PAD

    def body(q_ref, k_ref, v_ref, mix_ref, o_ref, lse_ref):
        qi = pl.program_id(1)
        qv = [q_ref[:, a * HEAD_PAD:(a + 1) * HEAD_PAD] for a in range(hs)]

        def step(j, carry, masked):
            off = pl.multiple_of(j * tq, tq)
            out = []
            for a in range(hs):
                m, acc = carry[a]
                s = _dot(qv[a], k_ref[pl.ds(off, tq), a * HEAD_PAD:(a + 1) * HEAD_PAD], NT)
                if masked:
                    row = lax.broadcasted_iota(jnp.int32, (tq, tq), 0)
                    col = lax.broadcasted_iota(jnp.int32, (tq, tq), 1)
                    s = jnp.where(col <= row, s, NEG_INF)
                m_new = jnp.maximum(m, jnp.max(s, axis=-1, keepdims=True))
                p = jnp.exp2((s - m_new) * (ATT_SCALE * LOG2E))
                alpha = jnp.exp2((m - m_new) * (ATT_SCALE * LOG2E))
                acc = alpha * acc + _dot(p.astype(BF16), v_ref[pl.ds(off, tq), a * HEAD_PAD:(a + 1) * HEAD_PAD])
                out.append((m_new, acc))
            return tuple(out)

        one = (jnp.full((tq, 1), NEG_INF, F32), jnp.zeros((tq, HEAD_PAD), F32))
        carry = lax.fori_loop(0, qi, lambda j, c: step(j, c, False), (one,) * hs)
        carry = step(qi, carry, True)
        for a in range(hs):
            m, acc = carry[a]
            l = acc[:, V_HEAD:V_HEAD + 1]
            o_ref[:, a * HEAD_PAD:(a + 1) * HEAD_PAD] = (acc / l).astype(o_ref.dtype)
            lse_ref[a] = _stat_rows(m * ATT_SCALE + jnp.log(l))

    blk = pl.BlockSpec((tq, wide), lambda h, i: (i, h))
    full = pl.BlockSpec((S, wide), lambda h, i: (0, h))
    return pl.pallas_call(
        body, grid=(MLA_HEADS // hs, nq), in_specs=[blk, full, full, ANY],
        out_specs=[pl.BlockSpec((tq, wide), lambda h, i: (i, ATT_COL0 // hs + h)),
                   pl.BlockSpec((hs, 8, tq), lambda h, i: (h, i, 0))],
        out_shape=[jax.ShapeDtypeStruct(mix.shape, mix.dtype), jax.ShapeDtypeStruct((MLA_HEADS, nq * 8, tq), F32)],
        input_output_aliases={3: 0}, compiler_params=_cp(2), name=name,
    )(q, k, v, mix)


def _attn_delta(dmix, mix, *, name):
    S = mix.shape[0]
    ts = min(S, 512)
    half = MLA_HEADS // 2
    halves = [_rows(ts, half * HEAD_PAD, 1), _rows(ts, half * HEAD_PAD, 2)]

    def body(do0_ref, do1_ref, o0_ref, o1_ref, d_ref):
        for n, (do_ref, o_ref) in enumerate(((do0_ref, o0_ref), (do1_ref, o1_ref))):
            prod = do_ref[...].astype(F32) * o_ref[...].astype(F32)
            for a in range(half):
                d_ref[n * half + a] = _stat_rows(
                    jnp.sum(prod[:, a * HEAD_PAD:(a + 1) * HEAD_PAD], axis=-1, keepdims=True))

    return pl.pallas_call(
        body, grid=(S // ts,), in_specs=halves + halves,
        out_specs=pl.BlockSpec((MLA_HEADS, 8, ts), lambda i: (0, i, 0)),
        out_shape=jax.ShapeDtypeStruct((MLA_HEADS, (S // ts) * 8, ts), F32), compiler_params=_cp(1), name=name,
    )(dmix, dmix, mix, mix)


def _flash_bwd(q, k, v, dmix, lse_rows, delta_rows, *, name):
    S = q.shape[0]
    tq = min(S, 512)
    nq = S // tq
    hs = HEADS_PER_STEP
    wide = hs * HEAD_PAD

    def body(q_hbm, do_hbm, lse_ref, dl_ref, k_ref, v_ref, dq_hbm, dk_ref, dv_ref, q_all, do_all, dq_all):
        g, j = pl.program_id(0), pl.program_id(1)
        cols = pl.multiple_of(g * wide, wide)

        @pl.when(j == 0)
        def _():
            pltpu.sync_copy(q_hbm.at[:, pl.ds(cols, wide)], q_all)
            pltpu.sync_copy(do_hbm.at[:, pl.ds(POOL_DIM + cols, wide)], do_all)
            dq_all[...] = jnp.zeros_like(dq_all)

        heads = [slice(a * HEAD_PAD, (a + 1) * HEAD_PAD) for a in range(hs)]
        kv = [k_ref[:, a] for a in heads]
        vv = [v_ref[:, a] for a in heads]

        def step(i, carry, masked):
            off = pl.multiple_of(i * tq, tq)
            off8 = pl.multiple_of(i * 8, 8)
            out = []
            for a in range(hs):
                dk, dv = carry[a]
                qv = q_all[pl.ds(off, tq), heads[a]]
                dov = do_all[pl.ds(off, tq), heads[a]]
                lse2 = lse_ref[a, pl.ds(off8, 8), :][0:1] * LOG2E
                dl = dl_ref[a, pl.ds(off8, 8), :][0:1]
                st = _dot(kv[a], qv, NT)
                if masked:
                    krow = lax.broadcasted_iota(jnp.int32, (tq, tq), 0)
                    qcol = lax.broadcasted_iota(jnp.int32, (tq, tq), 1)
                    st = jnp.where(krow <= qcol, st, NEG_INF)
                pt = jnp.exp2(st * (ATT_SCALE * LOG2E) - lse2)
                dv = dv + _dot(pt.astype(BF16), dov)
                dst = (pt * (_dot(vv[a], dov, NT) - dl)).astype(BF16)
                dk = dk + _dot(dst, qv)
                dq_all[pl.ds(off, tq), heads[a]] += _dot(dst, kv[a], TN)
                out.append((dk, dv))
            return tuple(out)

        zero = jnp.zeros((tq, HEAD_PAD), F32)
        carry = step(j, ((zero, zero),) * hs, True)
        carry = lax.fori_loop(j + 1, nq, lambda i, c: step(i, c, False), carry)
        for a in range(hs):
            dk_ref[:, heads[a]] = carry[a][0] * ATT_SCALE
            dv_ref[:, heads[a]] = carry[a][1]

        @pl.when(j == nq - 1)
        def _():
            dq_all[...] = dq_all[...] * ATT_SCALE
            pltpu.sync_copy(dq_all, dq_hbm.at[:, pl.ds(cols, wide)])

    blk = pl.BlockSpec((tq, wide), lambda g, j: (j, g))
    stat = pl.BlockSpec((hs, nq * 8, tq), lambda g, j: (g, 0, 0))
    full = jax.ShapeDtypeStruct((S, MLA_HEADS * HEAD_PAD), F32)
    return pl.pallas_call(
        body, grid=(MLA_HEADS // hs, nq), in_specs=[ANY, ANY, stat, stat, blk, blk], out_specs=[ANY, blk, blk],
        out_shape=[full, full, full],
        scratch_shapes=[pltpu.VMEM((S, wide), BF16), pltpu.VMEM((S, wide), BF16), pltpu.VMEM((S, wide), F32)],
        compiler_params=_cp(2), name=name,
    )(q, dmix, lse_rows, delta_rows, k, v)


MEM_SCALE = MEM_HEAD_DIM ** -0.5


def _xattn_probs(qh, kh):
    s = _dot(qh, kh, NT) * MEM_SCALE
    e = jnp.exp(s - jnp.max(s, axis=-1, keepdims=True))
    return e / jnp.sum(e, axis=-1, keepdims=True)


def _xattn_fwd(q, kvm, *, name):
    S = q.shape[0]
    ts = min(S, 512)
    nm = kvm.shape[0]

    def body(q_ref, kv_ref, o_ref):
        for h in range(MEM_HEADS):
            lo, hi = h * MEM_HEAD_DIM, (h + 1) * MEM_HEAD_DIM
            p = _xattn_probs(q_ref[:, lo:hi], kv_ref[:, lo:hi])
            o_ref[:, lo:hi] = _dot(p.astype(BF16), kv_ref[:, D_MODEL + lo:D_MODEL + hi]).astype(o_ref.dtype)

    return pl.pallas_call(
        body, grid=(S // ts,), in_specs=[_rows(ts, D_MODEL), _const((nm, 2 * D_MODEL))],
        out_specs=_rows(ts, D_MODEL), out_shape=jax.ShapeDtypeStruct((S, D_MODEL), BF16),
        compiler_params=_cp(1), name=name,
    )(q, kvm)


def _xattn_bwd(q, kvm, do, *, name):
    S = q.shape[0]
    ts = min(S, 512)
    nm = kvm.shape[0]

    def body(q_ref, kv_ref, do_ref, dq_ref, dkv_ref):
        @pl.when(pl.program_id(0) == 0)
        def _():
            dkv_ref[...] = jnp.zeros_like(dkv_ref)

        for h in range(MEM_HEADS):
            lo, hi = h * MEM_HEAD_DIM, (h + 1) * MEM_HEAD_DIM
            qh, kh, vh = q_ref[:, lo:hi], kv_ref[:, lo:hi], kv_ref[:, D_MODEL + lo:D_MODEL + hi]
            doh = do_ref[:, lo:hi].astype(BF16)
            p = _xattn_probs(qh, kh)
            dp = _dot(doh, vh, NT)
            ds = (p * (dp - jnp.sum(dp * p, axis=-1, keepdims=True)) * MEM_SCALE).astype(BF16)
            dq_ref[:, lo:hi] = _dot(ds, kh).astype(dq_ref.dtype)
            dkv_ref[:, lo:hi] += _dot(ds, qh, TN)
            dkv_ref[:, D_MODEL + lo:D_MODEL + hi] += _dot(p.astype(BF16), doh, TN)

    return pl.pallas_call(
        body, grid=(S // ts,), in_specs=[_rows(ts, D_MODEL), _const((nm, 2 * D_MODEL)), _rows(ts, D_MODEL)],
        out_specs=[_rows(ts, D_MODEL), _const((nm, 2 * D_MODEL))],
        out_shape=[jax.ShapeDtypeStruct((S, D_MODEL), BF16), jax.ShapeDtypeStruct((nm, 2 * D_MODEL), F32)],
        compiler_params=_cp(1), name=name,
    )(q, kvm, do)


CONV_HALO = 8


def _sigmoid(x):
    return 1.0 / (1.0 + jnp.exp(-x))


def _softplus(x):
    return jnp.maximum(x, 0.0) + jnp.log(1.0 + jnp.exp(-jnp.abs(x)))


def _neg_expm1(x):
    series = -x * (1.0 + x * (1.0 / 2) * (1.0 + x * (1.0 / 3) * (1.0 + x * (1.0 / 4) * (1.0 + x * (1.0 / 5)))))
    return jnp.where(x > -0.05, series, 1.0 - jnp.exp(x))


GELU_C = math.sqrt(2.0 / math.pi)


def _gelu(x):
    return 0.5 * x * (1.0 + jnp.tanh(GELU_C * (x + 0.044715 * x * x * x)))


def _gelu_grad(x):
    t = jnp.tanh(GELU_C * (x + 0.044715 * x * x * x))
    return 0.5 * (1.0 + t) + 0.5 * x * (1.0 - t * t) * GELU_C * (1.0 + 3 * 0.044715 * x * x)


def _lru_gates(xc, wr_ref, br, wi_ref, bi, sp, reset):
    xcb = xc.astype(BF16)
    pr, pi = [], []
    for h in range(LRU_HEADS):
        lo, hi = h * LRU_HEAD_DIM, (h + 1) * LRU_HEAD_DIM
        pr.append(_dot(xcb[:, lo:hi], wr_ref[h]))
        pi.append(_dot(xcb[:, lo:hi], wi_ref[h]))
    r = _sigmoid(jnp.concatenate(pr, axis=1) + br)
    ig = _sigmoid(jnp.concatenate(pi, axis=1) + bi)
    log_a = -LRU_C * r * sp
    a = jnp.where(reset, 0.0, jnp.exp(log_a))
    mult = jnp.where(reset, 1.0, jnp.sqrt(jnp.maximum(_neg_expm1(2.0 * log_a), 0.0)))
    return r, ig, a, mult


SUBLANES = 8


def _compose_groups(a, b, reverse):
    n = a.shape[0]
    row = lax.broadcasted_iota(jnp.int32, a.shape, 0) % SUBLANES
    for s in (1, 2, 4):
        inside = (row < SUBLANES - s) if reverse else (row >= s)
        shift = n - s if reverse else s
        a_s = jnp.where(inside, pltpu.roll(a, shift, 0), 1.0)
        b_s = jnp.where(inside, pltpu.roll(b, shift, 0), 0.0)
        b = a * b_s + b
        a = a * a_s
    return a, b


def _chain_groups(a_buf, h_ref, state, reverse):
    groups = a_buf.shape[0] // SUBLANES

    def group(g, h_in):
        off = pl.multiple_of((groups - 1 - g if reverse else g) * SUBLANES, SUBLANES)
        h = a_buf[pl.ds(off, SUBLANES), :] * h_in + h_ref[pl.ds(off, SUBLANES), :]
        h_ref[pl.ds(off, SUBLANES), :] = h
        return jnp.broadcast_to(h[0:1] if reverse else h[SUBLANES - 1:SUBLANES], h.shape)

    return lax.fori_loop(0, groups, group, state, unroll=4)[0:1]


def _lru_fwd(z, reset, conv_w, conv_b, w_r, b_r, w_i, b_i, lam, *, name):
    S = z.shape[0]
    ts = min(S, 512)
    nh = ts // CONV_HALO
    W = D_MODEL

    def body(gate_ref, xb_ref, halo_ref, rs_ref, cw_ref, cb_ref, wr_ref, br_ref, wi_ref, bi_ref, lam_ref,
             xc_ref, h_ref, y_ref, a_buf, carry):
        i = pl.program_id(0)

        @pl.when(i == 0)
        def _():
            carry[...] = jnp.zeros_like(carry)

        halo = jnp.where(i > 0, halo_ref[...], 0.0)
        xe = jnp.concatenate([halo, xb_ref[...]], axis=0)
        xc = cb_ref[...] + cw_ref[3:4, :] * xe[CONV_HALO:]
        for kk in range(CONV_WIDTH - 1):
            xc = xc + cw_ref[kk:kk + 1, :] * pltpu.roll(xe, CONV_WIDTH - 1 - kk, 0)[CONV_HALO:]
        xc_ref[...] = xc
        reset = rs_ref[...] > 0.5
        _, ig, a, mult = _lru_gates(xc, wr_ref, br_ref[...], wi_ref, bi_ref[...], _softplus(-lam_ref[...]), reset)
        a_buf[...], h_ref[...] = _compose_groups(a, mult * (ig * xc), False)
        carry[...] = _chain_groups(a_buf, h_ref, jnp.broadcast_to(carry[...], (SUBLANES, W)), False)
        y_ref[...] = (_gelu(gate_ref[...]) * h_ref[...]).astype(y_ref.dtype)

    vec = _const((1, W))
    gw = _const((LRU_HEADS, LRU_HEAD_DIM, LRU_HEAD_DIM))
    return pl.pallas_call(
        body, grid=(S // ts,),
        in_specs=[_rows(ts, W, 0), _rows(ts, W, 1),
                  pl.BlockSpec((CONV_HALO, W), lambda i: (jnp.maximum(i * nh - 1, 0), 1)),
                  _rows(ts, 1), _const((CONV_WIDTH, W)), vec, gw, vec, gw, vec, vec],
        out_specs=[_rows(ts, W)] * 3,
        out_shape=[jax.ShapeDtypeStruct((S, W), F32), jax.ShapeDtypeStruct((S, W), F32),
                   jax.ShapeDtypeStruct((S, W), BF16)],
        scratch_shapes=[pltpu.VMEM((ts, W), F32), pltpu.VMEM((1, W), F32)],
        compiler_params=_cp(1), name=name,
    )(z, z, z, reset, conv_w, conv_b, w_r, b_r, w_i, b_i, lam)


def _lru_bwd(dy, z, xc, hseq, reset, w_r, b_r, w_i, b_i, lam, *, name):
    S = z.shape[0]
    ts = min(S, 512)
    nt = S // ts
    nh = ts // CONV_HALO
    W = D_MODEL

    def body(dy_ref, gate_ref, xc_ref, h_ref, hh_ref, rs_ref, wr_ref, br_ref, wi_ref, bi_ref, lam_ref,
             dg_ref, dxc_ref, dpr_ref, dpi_ref, acc_ref, a_buf, dh_buf, carry):
        i = pl.program_id(0)
        tile = nt - 1 - i

        @pl.when(i == 0)
        def _():
            carry[...] = jnp.zeros_like(carry)
            acc_ref[...] = jnp.zeros_like(acc_ref)

        xc = xc_ref[...]
        lam_v = lam_ref[...]
        sp = _softplus(-lam_v)
        reset = rs_ref[...] > 0.5
        r, ig, a, mult = _lru_gates(xc, wr_ref, br_ref[...], wi_ref, bi_ref[...], sp, reset)
        gate = gate_ref[...]
        dyv = dy_ref[...].astype(F32)
        h = h_ref[...]
        dg_ref[...] = (dyv * h * _gelu_grad(gate)).astype(dg_ref.dtype)
        last_row = lax.broadcasted_iota(jnp.int32, a.shape, 0) == ts - 1
        a_buf[...], dh_buf[...] = _compose_groups(jnp.where(last_row, 1.0, pltpu.roll(a, ts - 1, 0)),
                                                  dyv * _gelu(gate), True)
        _chain_groups(a_buf, dh_buf, jnp.broadcast_to(carry[...], (SUBLANES, W)), True)
        dh = dh_buf[...]
        carry[...] = a[0:1] * dh[0:1]
        hh = jnp.where(tile > 0, hh_ref[...], 0.0)
        h_prev = pltpu.roll(jnp.concatenate([hh, h], axis=0), 1, 0)[CONV_HALO:]
        da = dh * h_prev
        bx = ig * xc
        dmult = dh * bx
        dbx = dh * mult
        di = dbx * xc
        dlog_a = jnp.where(reset, 0.0, da * a - dmult * a * a / jnp.maximum(mult, 1e-30))
        dr = dlog_a * (-LRU_C) * sp
        dpre_r = dr * r * (1.0 - r)
        dpre_i = di * ig * (1.0 - ig)
        dprb, dpib = dpre_r.astype(BF16), dpre_i.astype(BF16)
        dpr_ref[...] = dprb
        dpi_ref[...] = dpib
        back = []
        for hd in range(LRU_HEADS):
            lo, hi = hd * LRU_HEAD_DIM, (hd + 1) * LRU_HEAD_DIM
            back.append(_dot(dprb[:, lo:hi], wr_ref[hd], NT) + _dot(dpib[:, lo:hi], wi_ref[hd], NT))
        dxc_ref[...] = dbx * ig + jnp.concatenate(back, axis=1)
        dlam = jnp.sum(dlog_a * (-LRU_C) * r, axis=0, keepdims=True) * (-_sigmoid(-lam_v))
        acc_ref[0:1, :] += jnp.sum(dpre_r, axis=0, keepdims=True)
        acc_ref[1:2, :] += jnp.sum(dpre_i, axis=0, keepdims=True)
        acc_ref[2:3, :] += dlam

    rev = lambda cb: pl.BlockSpec((ts, W), lambda i: (nt - 1 - i, cb))
    vec = _const((1, W))
    gw = _const((LRU_HEADS, LRU_HEAD_DIM, LRU_HEAD_DIM))
    return pl.pallas_call(
        body, grid=(nt,),
        in_specs=[rev(0), rev(0), rev(0), rev(0),
                  pl.BlockSpec((CONV_HALO, W), lambda i: (jnp.maximum((nt - 1 - i) * nh - 1, 0), 0)),
                  pl.BlockSpec((ts, 1), lambda i: (nt - 1 - i, 0)), gw, vec, gw, vec, vec],
        out_specs=[rev(0), rev(0), rev(0), rev(0), _const((8, W))],
        out_shape=[jax.ShapeDtypeStruct((S, W), BF16), jax.ShapeDtypeStruct((S, W), F32),
                   jax.ShapeDtypeStruct((S, W), BF16), jax.ShapeDtypeStruct((S, W), BF16),
                   jax.ShapeDtypeStruct((8, W), F32)],
        scratch_shapes=[pltpu.VMEM((ts, W), F32), pltpu.VMEM((ts, W), F32), pltpu.VMEM((1, W), F32)],
        compiler_params=_cp(1), name=name,
    )(dy, z, xc, hseq, hseq, reset, w_r, b_r, w_i, b_i, lam)


def _conv_bwd(dxc, z, conv_w, *, name):
    S = dxc.shape[0]
    ts = min(S, 512)
    nh = ts // CONV_HALO
    last = S // CONV_HALO - 1
    W = D_MODEL
    n = ts + CONV_HALO

    def body(d_ref, dn_ref, xb_ref, xp_ref, cw_ref, dxb_ref, acc_ref):
        i = pl.program_id(0)

        @pl.when(i == 0)
        def _():
            acc_ref[...] = jnp.zeros_like(acc_ref)

        d = d_ref[...]
        de = jnp.concatenate([d, jnp.where(i < pl.num_programs(0) - 1, dn_ref[...], 0.0)], axis=0)
        xe = jnp.concatenate([jnp.where(i > 0, xp_ref[...], 0.0), xb_ref[...]], axis=0)
        dxb = cw_ref[3:4, :] * d
        acc_ref[3:4, :] += jnp.sum(d * xe[CONV_HALO:], axis=0, keepdims=True)
        for kk in range(CONV_WIDTH - 1):
            sh = CONV_WIDTH - 1 - kk
            dxb = dxb + cw_ref[kk:kk + 1, :] * pltpu.roll(de, n - sh, 0)[:ts]
            acc_ref[kk:kk + 1, :] += jnp.sum(d * pltpu.roll(xe, sh, 0)[CONV_HALO:], axis=0, keepdims=True)
        dxb_ref[...] = dxb.astype(dxb_ref.dtype)
        acc_ref[4:5, :] += jnp.sum(d, axis=0, keepdims=True)

    return pl.pallas_call(
        body, grid=(S // ts,),
        in_specs=[_rows(ts, W), pl.BlockSpec((CONV_HALO, W), lambda i: (jnp.minimum((i + 1) * nh, last), 0)),
                  _rows(ts, W, 1), pl.BlockSpec((CONV_HALO, W), lambda i: (jnp.maximum(i * nh - 1, 0), 1)),
                  _const((CONV_WIDTH, W))],
        out_specs=[_rows(ts, W), _const((8, W))],
        out_shape=[jax.ShapeDtypeStruct((S, W), BF16), jax.ShapeDtypeStruct((8, W), F32)],
        compiler_params=_cp(1), name=name,
    )(dxc, dxc, z, z, conv_w)


def _loss_head(x, g, target, *, name):
    S, D = x.shape
    ts = min(S, 512)

    def body(x_ref, g_ref, t_ref, dx_ref, dg_ref, l_ref):
        @pl.when(pl.program_id(0) == 0)
        def _():
            dg_ref[...] = jnp.zeros_like(dg_ref)
            l_ref[...] = jnp.zeros_like(l_ref)

        xv = x_ref[...]
        r = lax.rsqrt(jnp.mean(xv * xv, axis=-1, keepdims=True) + RMS_EPS)
        n = xv * r
        err = n * g_ref[...] - t_ref[...]
        l_ref[...] += 0.5 * jnp.sum(jnp.sum(err * err, axis=-1, keepdims=True) * (1.0 / D), axis=0, keepdims=True)
        dy = err * (1.0 / D)
        dn = dy * g_ref[...]
        dx_ref[...] = r * (dn - n * jnp.mean(dn * n, axis=-1, keepdims=True))
        dg_ref[...] += jnp.sum(dy * n, axis=0, keepdims=True)

    return pl.pallas_call(
        body, grid=(S // ts,), in_specs=[_rows(ts, D), _const((1, D)), _rows(ts, D)],
        out_specs=[_rows(ts, D), _const((1, D)), _const((8, LANES))],
        out_shape=[jax.ShapeDtypeStruct((S, D), F32), jax.ShapeDtypeStruct((1, D), F32),
                   jax.ShapeDtypeStruct((8, LANES), F32)],
        compiler_params=_cp(1), name=name,
    )(x, g.reshape(1, D), target)


def _adamw(w, ga, gb, m, v, *, name):
    shape = w.shape
    cols = shape[-1]
    rows = w.size // cols
    br = rows
    if rows * cols * 4 > (1 << 20):
        br = max(d for d in range(8, rows + 1, 8) if rows % d == 0 and d * cols * 4 <= (1 << 20))

    def body(w_ref, ga_ref, gb_ref, m_ref, v_ref, g_ref, d_ref, mo_ref, vo_ref):
        gv = ga_ref[...] + gb_ref[...]
        g_ref[...] = gv
        mn = ADAM_B1 * m_ref[...] + (1.0 - ADAM_B1) * gv
        vn = ADAM_B2 * v_ref[...] + (1.0 - ADAM_B2) * (gv * gv)
        m_hat = mn / (1.0 - ADAM_B1 ** ADAM_STEP)
        v_hat = vn / (1.0 - ADAM_B2 ** ADAM_STEP)
        d_ref[...] = -ADAM_LR * (m_hat / (jnp.sqrt(v_hat) + ADAM_EPS) + ADAM_WD * w_ref[...])
        mo_ref[...] = mn
        vo_ref[...] = vn

    spec = _rows(br, cols)
    outs = pl.pallas_call(
        body, grid=(rows // br,), in_specs=[spec] * 5, out_specs=[spec] * 4,
        out_shape=[jax.ShapeDtypeStruct((rows, cols), F32)] * 4, compiler_params=_cp(1), name=name,
    )(*[t.reshape(rows, cols) for t in (w, ga, gb, m, v)])
    return [o.reshape(shape) for o in outs]


def _pad_heads(w, width):
    k = w.shape[0]
    return jnp.pad(w.reshape(k, MLA_HEADS, width), ((0, 0), (0, 0), (0, HEAD_PAD - width))).reshape(k, -1)


def _unpad_heads(w, width):
    k = w.shape[0]
    return w.reshape(k, MLA_HEADS, HEAD_PAD)[:, :, :width].reshape(k, MLA_HEADS * width)


def _rope_tables(positions):
    inv_freq = ROPE_BASE ** (-jnp.arange(0, QK_ROPE, 2, dtype=F32) / QK_ROPE)
    ang = positions.astype(F32)[:, None] * inv_freq
    cos, sin = jnp.cos(ang), jnp.sin(ang)
    S = positions.shape[0]
    ones, zeros = jnp.ones((S, QK_NOPE), F32), jnp.zeros((S, QK_NOPE), F32)
    ctab = jnp.concatenate([ones, cos, cos, ones[:, :HEAD_PAD - QK_DIM]], axis=1)
    stab = jnp.concatenate([zeros, -sin, sin, zeros[:, :HEAD_PAD - QK_DIM]], axis=1)
    return ctab, stab


def _memory_block(x, mem, W, layer, tag):
    hx = _rms(x, W["xa_norm_x"][layer], name=f"{tag}_xa_norm")
    qx = _mm(hx, [(W["xa_w_q"][layer], 0, 0)], _first, [(D_MODEL, BF16, 0)], tn=D_MODEL, nj=1, name=f"{tag}_xa_q")[0]
    mn = _rms(mem, W["xa_norm_mem"][layer], name=f"{tag}_xa_norm_mem")
    kvm = _mm(mn, [(W["xa_w_kv"][layer], 0, 0)], _first, [(2 * D_MODEL, BF16, 0)], tn=2 * D_MODEL, nj=1,
              name=f"{tag}_xa_kv")[0]
    o = _xattn_fwd(qx, kvm, name=f"{tag}_xa_attn")
    xo = _mm(o, [(W["xa_w_o"][layer], 0, 0)], _add_res, [(D_MODEL, F32, 0)], extras=[(x, 0)], tn=D_MODEL, nj=1,
             name=f"{tag}_xa_out")[0]
    return xo, (x, hx, qx, mn, kvm, o)


def _memory_block_bwd(dxo, mem, W, layer, saved, tag, grads):
    x, hx, qx, mn, kvm, o = saved
    wq, wkv, wo = W["xa_w_q"][layer], W["xa_w_kv"][layer], W["xa_w_o"][layer]
    do = _mm(dxo, [(wo, 0, 0)], _first, [(D_MODEL, BF16, 0)], nt=True, tn=D_MODEL, nj=1, name=f"{tag}_xa_do")[0]
    grads["xa_w_o"][layer] = _owner_major(_mm_tn(o, dxo, name=f"{tag}_xa_dwo"), 0)
    dqx, dkvm = _xattn_bwd(qx, kvm, do, name=f"{tag}_xa_attn_bwd")
    grads["xa_w_q"][layer] = _owner_major(_mm_tn(hx, dqx, name=f"{tag}_xa_dwq"), 0)
    dx, dg = _mm(dqx, [(wq, 0, 0)], _norm_bwd_epilogue(0), [(D_MODEL, F32, 0)], nt=True, extras=[(x, 0), (dxo, 0)],
                 rows=[W["xa_norm_x"][layer].reshape(1, D_MODEL)], sums=[D_MODEL], tn=D_MODEL, nj=1,
                 name=f"{tag}_xa_dhx")
    grads["xa_norm_x"][layer] = dg[0]
    dmn = _mm(dkvm, [(wkv, 0, 0)], _first, [(D_MODEL, F32, 0)], nt=True, tn=D_MODEL, nj=1, name=f"{tag}_xa_dmn")[0]
    grads["xa_w_kv"][layer] = _mm_tn_owners(mn, [dkvm], name=f"{tag}_xa_dwkv")
    _, dgm = _rms_bwd(mem, W["xa_norm_mem"][layer], dmn, name=f"{tag}_xa_norm_mem_bwd")
    grads["xa_norm_mem"][layer] = dgm[0]
    return dx


FF_TN = D_FF // 2

def _silu_mul(accs, extras):
    g, u = accs
    return [g * _sigmoid(g) * u, g, u]


def _silu_mul_bwd(accs, extras):
    da = accs[0]
    g, u = extras[0].astype(F32), extras[1].astype(F32)
    sg = _sigmoid(g)
    return [da * u * sg * (1.0 + g * (1.0 - sg)), da * g * sg]


def _ffn_block(x, W, layer, tag):
    hf = _rms(x, W["ffn_norm"][layer], name=f"{tag}_ffn_norm")
    wgu, wd = W["ffn_w_gate_up"][layer], W["ffn_w_down"][layer]
    act, g, u = _mm(hf, [(wgu, 0, 0), (wgu, 0, 2)], _silu_mul, [(D_FF, BF16, 0)] * 3, tn=FF_TN, nj=2,
                    name=f"{tag}_ffn_up")
    xo = _mm(act, [(wd, 0, 0)], _add_res, [(D_MODEL, F32, 0)], extras=[(x, 0)], tn=D_MODEL, nj=1,
             name=f"{tag}_ffn_down")[0]
    return xo, (x, hf, act, g, u)


def _ffn_block_bwd(dxo, W, layer, saved, tag, grads):
    x, hf, act, g, u = saved
    wgu, wd = W["ffn_w_gate_up"][layer], W["ffn_w_down"][layer]
    dg, du = _mm(dxo, [(wd, 0, 0)], _silu_mul_bwd, [(D_FF, BF16, 0)] * 2, nt=True, extras=[(g, 0), (u, 0)], tn=FF_TN,
                 nj=2, name=f"{tag}_ffn_dact")
    grads["ffn_w_down"][layer] = _owner_major(_mm_tn(act, dxo, tk=FF_TN, name=f"{tag}_ffn_dwd"), 0)
    dhf = _mm(dg, [(wgu, 0, 0)], _first, [(D_MODEL, F32, 0)], nt=True, tn=D_MODEL, nj=1, name=f"{tag}_ffn_dhf_g")[0]
    dx, dgn = _mm(du, [(wgu, 0, 1)], _norm_bwd_epilogue(1), [(D_MODEL, F32, 0)], nt=True,
                  extras=[(dhf, 0), (x, 0), (dxo, 0)], rows=[W["ffn_norm"][layer].reshape(1, D_MODEL)],
                  sums=[D_MODEL], tn=D_MODEL, nj=1, name=f"{tag}_ffn_dhf_u")
    grads["ffn_w_gate_up"][layer] = _mm_tn_owners(hf, [dg, du], name=f"{tag}_ffn_dwgu")
    grads["ffn_norm"][layer] = dgn[0]
    return dx


def _keys_and_values(accs, extras):
    k, v = accs
    lane = lax.broadcasted_iota(jnp.int32, v.shape, 1)
    return [k, jnp.where(lane % HEAD_PAD == V_HEAD, 1.0, v)]


def _even_block(x, tabs, W, tag):
    ctab, stab = tabs
    w_in = W["ev_w_in"][0]
    zero = jnp.zeros((D_MODEL, QK_NOPE), BF16)
    w_in_pad = jnp.concatenate([w_in[:, :896], zero, w_in[:, 896:], zero[:, :HEAD_PAD - QK_DIM]], axis=1)
    w_q_pad = _pad_heads(W["ev_w_q_up"][0], QK_DIM)
    wkv = W["ev_w_kv_up"][0].reshape(KV_RANK, MLA_HEADS, QK_NOPE + V_HEAD)
    w_kv_pad = jnp.concatenate([_pad_heads(wkv[:, :, :QK_NOPE].reshape(KV_RANK, -1), QK_NOPE),
                                _pad_heads(wkv[:, :, QK_NOPE:].reshape(KV_RANK, -1), V_HEAD)], axis=1)
    w_out = W["ev_w_out"][0]
    w_att = jnp.pad(w_out[POOL_DIM:].reshape(MLA_HEADS, V_HEAD, D_MODEL), ((0, 0), (0, HEAD_PAD - V_HEAD), (0, 0)))
    w_out_pad = jnp.concatenate([w_out[:POOL_DIM], w_att.reshape(MLA_HEADS * HEAD_PAD, D_MODEL)], axis=0)
    pool_w = W["ev_pool_w"][0].astype(BF16)
    pool_scale = W["ev_pool_scale"]

    h = _rms(x, W["ev_norm"][0], name=f"{tag}_norm")
    z = _mm(h, [(w_in_pad, 0, 0)], _first, [(D_MODEL, F32, 0)], tn=D_MODEL, nj=1, name=f"{tag}_in")[0]
    mix, pooled = _pool_fwd(z, pool_w, pool_scale, name=f"{tag}_pool")
    cqn = _rms(z, W["ev_q_norm"][0], cb=2, w=Q_RANK, name=f"{tag}_q_norm")
    ckvn = _rms(z, W["ev_kv_norm"][0], cb=6, w=KV_RANK, name=f"{tag}_kv_norm")
    q_pad = _mm(cqn, [(w_q_pad, 0, 0)], _first, [(D_MODEL, F32, 0)], tn=D_MODEL, nj=1, name=f"{tag}_q_up")[0]
    k_pad, v_pad = _mm(ckvn, [(w_kv_pad, 0, 0), (w_kv_pad, 0, 1)], _keys_and_values,
                       [(D_MODEL, F32, 0), (D_MODEL, BF16, 0)], tn=D_MODEL, nj=1, name=f"{tag}_kv_up")
    q_rot, k_cat = _rope_fwd(q_pad, k_pad, z, ctab, stab, name=f"{tag}_rope")
    mix, lse = _flash_fwd(q_rot, k_cat, v_pad, mix, name=f"{tag}_attn")
    xo = _mm(mix, [(w_out_pad, 0, 0)], _add_res, [(D_MODEL, F32, 0)], extras=[(x, 0)], tn=D_MODEL, nj=1,
             name=f"{tag}_out")[0]
    saved = (x, h, z, pooled, cqn, ckvn, q_rot, k_cat, v_pad, lse, mix,
             (w_in_pad, w_q_pad, w_kv_pad, w_out_pad, pool_w, pool_scale))
    return xo, saved


def _even_block_bwd(dxo, tabs, W, saved, tag, grads, token=None):
    ctab, stab = tabs
    x, h, z, pooled, cqn, ckvn, q_rot, k_cat, v_pad, lse, mix, wts = saved
    w_in_pad, w_q_pad, w_kv_pad, w_out_pad, pool_w, pool_scale = wts
    if token is not None:
        w_out_pad = w_out_pad + token[0:1, 0:1].astype(BF16)
    dmix = _mm(dxo, [(w_out_pad, 0, 0)], _first, [(MIX_DIM, BF16, 0)], nt=True, tn=MIX_DIM, nj=1,
               name=f"{tag}_dmix")[0]
    dw_out_pad = _mm_tn(mix, dxo, tk=MIX_DIM // 3, name=f"{tag}_dw_out")
    datt = dw_out_pad[POOL_DIM:].reshape(MLA_HEADS, HEAD_PAD, D_MODEL)[:, :V_HEAD].reshape(-1, D_MODEL)
    grads["ev_w_out"] = [_owner_major(jnp.concatenate([dw_out_pad[:POOL_DIM], datt], axis=0), 0)]
    delta = _attn_delta(dmix, mix, name=f"{tag}_delta")
    dq_rot, dk_cat, dv_pad = _flash_bwd(q_rot, k_cat, v_pad, dmix, lse, delta, name=f"{tag}_attn_bwd")
    dq_pad, dkr = _rope_bwd(dq_rot, dk_cat, ctab, stab, name=f"{tag}_rope_bwd")
    dw_q_pad = _mm_tn(cqn, dq_pad, name=f"{tag}_dw_q_up")
    grads["ev_w_q_up"] = [_owner_major(_unpad_heads(dw_q_pad, QK_DIM), 1)]
    dcqn = _mm(dq_pad, [(w_q_pad, 0, 0)], _first, [(Q_RANK, F32, 0)], nt=True, tn=Q_RANK, nj=1, name=f"{tag}_dcqn")[0]
    dwk = _unpad_heads(_mm_tn(ckvn, dk_cat, name=f"{tag}_dw_k_up"), QK_NOPE).reshape(KV_RANK, MLA_HEADS, QK_NOPE)
    dwv = _unpad_heads(_mm_tn(ckvn, dv_pad, name=f"{tag}_dw_v_up"), V_HEAD).reshape(KV_RANK, MLA_HEADS, V_HEAD)
    grads["ev_w_kv_up"] = [_owner_major(jnp.concatenate([dwk, dwv], axis=2).reshape(KV_RANK, -1), 1)]
    dckvn = _mm(dk_cat, [(w_kv_pad, 0, 0)], _first, [(KV_RANK, F32, 0)], nt=True, tn=KV_RANK, nj=1,
                name=f"{tag}_dckvn_k")[0]
    dckvn = _mm(dv_pad, [(w_kv_pad, 0, 1)], _add_res, [(KV_RANK, F32, 0)], nt=True, extras=[(dckvn, 0)], tn=KV_RANK,
                nj=1, name=f"{tag}_dckvn_v")[0]
    dcq, dgq = _rms_bwd(z, W["ev_q_norm"][0], dcqn, cb=2, w=Q_RANK, out_dtype=BF16, name=f"{tag}_q_norm_bwd")
    dckv, dgkv = _rms_bwd(z, W["ev_kv_norm"][0], dckvn, cb=6, w=KV_RANK, out_dtype=BF16, name=f"{tag}_kv_norm_bwd")
    grads["ev_q_norm"], grads["ev_kv_norm"] = dgq, dgkv
    du, dypre, dscale = _pool_bwd(dmix, pooled, pool_w, pool_scale, name=f"{tag}_pool_bwd")
    grads["ev_pool_scale"] = dscale
    grads["ev_pool_w"] = _mm_tn_grouped(pooled, dypre, 4, POOL_GROUP, name=f"{tag}_dpool_w")[None]
    dz = jnp.concatenate([du, dcq, dckv, dkr], axis=1)
    dw_in_pad = _mm_tn(h, dz, name=f"{tag}_dw_in")
    grads["ev_w_in"] = [_owner_major(jnp.concatenate([dw_in_pad[:, :896], dw_in_pad[:, 960:992]], axis=1), 0)]
    dx, dgn = _mm(dz, [(w_in_pad, 0, 0)], _norm_bwd_epilogue(0), [(D_MODEL, F32, 0)], nt=True,
                  extras=[(x, 0), (dxo, 0)], rows=[W["ev_norm"][0].reshape(1, D_MODEL)], sums=[D_MODEL], tn=D_MODEL,
                  nj=1, name=f"{tag}_dh")
    grads["ev_norm"] = dgn
    return dx


def _odd_block(x, reset, W, tag):
    h = _rms(x, W["od_norm"][0], name=f"{tag}_norm")
    z = _mm(h, [(W["od_w_in"][0], 0, 0)], _first, [(2 * D_MODEL, F32, 0)], tn=D_MODEL, nj=2, name=f"{tag}_in")[0]
    w_r, w_i = W["od_w_rgate"][0], W["od_w_igate"][0]
    vecs = [W[n].reshape(1, D_MODEL) for n in ("od_conv_b", "od_b_rgate", "od_b_igate", "od_lambda")]
    xc, hseq, y = _lru_fwd(z, reset, W["od_conv_w"][0], vecs[0], w_r, vecs[1], w_i, vecs[2], vecs[3],
                           name=f"{tag}_lru")
    xo = _mm(y, [(W["od_w_out"][0], 0, 0)], _add_res, [(D_MODEL, F32, 0)], extras=[(x, 0)], tn=D_MODEL, nj=1,
             name=f"{tag}_out")[0]
    return xo, (x, h, z, xc, hseq, y, vecs)


def _odd_block_bwd(dxo, reset, W, saved, tag, grads):
    x, h, z, xc, hseq, y, vecs = saved
    w_r, w_i = W["od_w_rgate"][0], W["od_w_igate"][0]
    dy = _mm(dxo, [(W["od_w_out"][0], 0, 0)], _first, [(D_MODEL, F32, 0)], nt=True, tn=D_MODEL, nj=1,
             name=f"{tag}_dy")[0]
    grads["od_w_out"] = [_owner_major(_mm_tn(y, dxo, name=f"{tag}_dw_out"), 0)]
    dgate, dxc, dpr, dpi, acc = _lru_bwd(dy, z, xc, hseq, reset, w_r, vecs[1], w_i, vecs[2], vecs[3],
                                         name=f"{tag}_lru_bwd")
    grads["od_b_rgate"], grads["od_b_igate"], grads["od_lambda"] = acc[0:1], acc[1:2], acc[2:3]
    grads["od_w_rgate"] = [_owner_major(_mm_tn_grouped(xc, dpr, LRU_HEADS, LRU_HEAD_DIM, name=f"{tag}_dw_rgate"), 1)]
    grads["od_w_igate"] = [_owner_major(_mm_tn_grouped(xc, dpi, LRU_HEADS, LRU_HEAD_DIM, name=f"{tag}_dw_igate"), 1)]
    dxb, cacc = _conv_bwd(dxc, z, W["od_conv_w"][0], name=f"{tag}_conv_bwd")
    grads["od_conv_w"], grads["od_conv_b"] = cacc[None, 0:4], cacc[4:5]
    dz = jnp.concatenate([dgate, dxb], axis=1)
    grads["od_w_in"] = [_mm_tn_owners(h, [dz], name=f"{tag}_dw_in")]
    dx, dgn = _mm(dz, [(W["od_w_in"][0], 0, 0)], _norm_bwd_epilogue(0), [(D_MODEL, F32, 0)], nt=True,
                  extras=[(x, 0), (dxo, 0)], rows=[W["od_norm"][0].reshape(1, D_MODEL)], sums=[D_MODEL], tn=D_MODEL,
                  nj=1, name=f"{tag}_dh")
    grads["od_norm"] = dgn
    return dx


def _local_step(x, mem, positions, target, W, later_weights=None, exchange_earlier=None):
    tabs = _rope_tables(positions)
    reset = (positions == 0).astype(F32)[:, None]
    grads = {n: [None, None] for n in ("xa_norm_x", "xa_norm_mem", "xa_w_q", "xa_w_kv", "xa_w_o", "ffn_norm",
                                       "ffn_w_gate_up", "ffn_w_down")}
    x1, s_even = _even_block(x, tabs, W, "l0_even")
    if later_weights is not None:
        W = {**W, **later_weights(x1)}
    x2, s_xa0 = _memory_block(x1, mem, W, 0, "l0")
    x3, s_ff0 = _ffn_block(x2, W, 0, "l0")
    x4, s_odd = _odd_block(x3, reset, W, "l1_odd")
    x5, s_xa1 = _memory_block(x4, mem, W, 1, "l1")
    x6, s_ff1 = _ffn_block(x5, W, 1, "l1")
    d, dgf, loss = _loss_head(x6, W["final_norm"], target, name="loss_head")
    grads["final_norm"] = dgf[0]
    d = _ffn_block_bwd(d, W, 1, s_ff1, "l1", grads)
    d = _memory_block_bwd(d, mem, W, 1, s_xa1, "l1", grads)
    d = _odd_block_bwd(d, reset, W, s_odd, "l1_odd", grads)
    d = _ffn_block_bwd(d, W, 0, s_ff0, "l0", grads)
    d = _memory_block_bwd(d, mem, W, 0, s_xa0, "l0", grads)
    token = exchange_earlier(grads) if exchange_earlier is not None else None
    d = _even_block_bwd(d, tabs, W, s_even, "l0_even", grads, token)
    big = {n: grads.pop(n) for n in MATMUL_WEIGHTS}
    for n, v in grads.items():
        if isinstance(v, list):
            grads[n] = jnp.stack(v)
    return loss[0, 0], d, big, grads


WEIGHTS = ("ev_norm", "ev_w_in", "ev_pool_w", "ev_pool_scale", "ev_q_norm", "ev_w_q_up", "ev_kv_norm", "ev_w_kv_up",
           "ev_w_out", "od_norm", "od_w_in", "od_conv_w", "od_conv_b", "od_w_rgate", "od_b_rgate", "od_w_igate",
           "od_b_igate", "od_lambda", "od_w_out", "xa_norm_x", "xa_norm_mem", "xa_w_q", "xa_w_kv", "xa_w_o",
           "ffn_norm", "ffn_w_gate_up", "ffn_w_down", "final_norm")
SHARD_AXIS = {"ev_w_in": 1, "ev_w_q_up": 2, "ev_w_kv_up": 2, "ev_w_out": 1, "od_norm": 1, "od_w_in": 2,
              "od_conv_w": 2, "od_conv_b": 1, "od_w_rgate": 2, "od_b_rgate": 1, "od_w_igate": 2, "od_b_igate": 1,
              "od_lambda": 1, "od_w_out": 1, "xa_w_q": 1, "xa_w_kv": 2, "xa_w_o": 1, "ffn_w_gate_up": 2,
              "ffn_w_down": 1}
MATMUL_WEIGHTS = ("ev_w_in", "ev_w_q_up", "ev_w_kv_up", "ev_w_out", "od_w_in", "od_w_rgate", "od_w_igate",
                  "od_w_out", "xa_w_q", "xa_w_kv", "xa_w_o", "ffn_w_gate_up", "ffn_w_down")
SMALL_SHARDED = tuple(n for n in WEIGHTS if n in SHARD_AXIS and n not in MATMUL_WEIGHTS)
REPLICATED = tuple(n for n in WEIGHTS if n not in SHARD_AXIS)


def _pack(parts, quantum):
    flat = jnp.concatenate([p.reshape(-1) for p in parts])
    pad = (-flat.shape[0]) % quantum
    return jnp.pad(flat, (0, pad)).reshape(-1, LANES)


def _unpack(flat, shapes):
    out, off = [], 0
    for shape in shapes:
        size = math.prod(shape)
        out.append(flat[off:off + size].reshape(shape))
        off += size
    return out


def _run_copies(local, remote, send_sems, recv_sems, local_sems):
    locals_ = [pltpu.make_async_copy(src, dst, local_sems.at[n]) for n, (src, dst) in enumerate(local)]
    for cp in locals_:
        cp.start()
    sends = [pltpu.make_async_remote_copy(src_ref=src, dst_ref=dst, send_sem=send_sems.at[k, n],
                                          recv_sem=recv_sems.at[k, n], device_id=dev, device_id_type=MESH)
             for (k, n, src, dst, _, dev) in remote]
    for cp in sends:
        cp.start()
    for (k, n, src, _, arrival, dev) in remote:
        pltpu.make_async_remote_copy(src_ref=src, dst_ref=arrival, send_sem=send_sems.at[k, n],
                                     recv_sem=recv_sems.at[k, n], device_id=dev, device_id_type=MESH).wait_recv()
    for cp in sends:
        cp.wait_send()
    for cp in locals_:
        cp.wait()


def _chip_peers(x, y):
    return [(1 - x, y), (x, 1 - y), (1 - x, 1 - y)]


def _owner_block(ref, axis, q):
    size = ref.shape[axis] // N_CHIPS
    idx = [slice(None)] * len(ref.shape)
    idx[axis] = pl.ds(q * size, size)
    return ref.at[tuple(idx)]


def _comm_call(body, ins, out_shapes, n_items, n_peers, *, name):
    return pl.pallas_call(
        body, in_specs=[ANY] * len(ins), out_specs=[ANY] * len(out_shapes), out_shape=out_shapes,
        scratch_shapes=[pltpu.SemaphoreType.DMA((n_peers, n_items)), pltpu.SemaphoreType.DMA((n_peers, n_items)),
                        pltpu.SemaphoreType.DMA((n_items,))],
        name=name,
    )(*ins)


def _gather_chips(shards, axes, *, name):
    n = len(shards)
    full = [jax.ShapeDtypeStruct(tuple(d * (N_CHIPS if a == ax else 1) for a, d in enumerate(s.shape)), s.dtype)
            for s, ax in zip(shards, axes)]

    def body(*refs):
        srcs, dsts = refs[:n], refs[n:2 * n]
        x, y, c = lax.axis_index("x"), lax.axis_index("y"), lax.axis_index("c")
        me = 2 * x + y
        local = [(srcs[i], _owner_block(dsts[i], axes[i], me)) for i in range(n)]
        remote = [(k, i, srcs[i], _owner_block(dsts[i], axes[i], me), _owner_block(dsts[i], axes[i], 2 * px + py),
                   (px, py, c))
                  for k, (px, py) in enumerate(_chip_peers(x, y)) for i in range(n)]
        _run_copies(local, remote, *refs[2 * n:])

    return _comm_call(body, shards, full, n, 3, name=name)


HBM = pl.BlockSpec(memory_space=pltpu.HBM)
SEM = pl.BlockSpec(memory_space=pltpu.SEMAPHORE)
DATAFLOW = pltpu.SideEffectType.DATAFLOW_SIDE_EFFECTING


def _gather_plan(axes):
    return lambda srcs, lands, me, peer: [
        (srcs[i], _owner_block(lands[i], ax, me), _owner_block(lands[i], ax, peer)) for i, ax in enumerate(axes)]


def _exchange_plan(where):
    return lambda srcs, lands, me, peer: [
        (srcs[i].at[peer], lands[n].at[me, l], lands[n].at[peer, l]) for i, (n, l) in enumerate(where)]


def _split_start(srcs, lands, plan, *, name):
    ns, nl = len(srcs), len(lands)
    nsem = 3 * len(plan(list(srcs), list(lands), 0, 0))

    def body(*refs):
        src_refs, land_refs = refs[:ns], refs[ns:ns + nl]
        send_sems, recv_sems = refs[ns + nl:ns + nl + nsem], refs[ns + nl + nsem:ns + nl + 2 * nsem]
        x, y, c = lax.axis_index("x"), lax.axis_index("y"), lax.axis_index("c")
        n = 0
        for px, py in _chip_peers(x, y):
            for src, dst, _ in plan(src_refs, land_refs, 2 * x + y, 2 * px + py):
                pltpu.make_async_remote_copy(src_ref=src, dst_ref=dst, send_sem=send_sems[n], recv_sem=recv_sems[n],
                                             device_id=(px, py, c), device_id_type=MESH).start()
                n += 1
        refs[-1][...] = jnp.zeros_like(refs[-1])

    arrays = list(srcs) + list(lands)
    out = pl.pallas_call(
        body, name=name, in_specs=[HBM] * (ns + nl),
        out_specs=[SEM] * (2 * nsem) + [HBM] * (ns + nl) + [pl.BlockSpec(memory_space=pltpu.VMEM)],
        out_shape=[pltpu.SemaphoreType.DMA(())] * (2 * nsem) + [pltpu.HBM(a.shape, a.dtype) for a in arrays]
        + [jax.ShapeDtypeStruct((8, LANES), F32)],
        input_output_aliases={i: 2 * nsem + i for i in range(ns + nl)},
        compiler_params=pltpu.CompilerParams(has_side_effects=DATAFLOW),
    )(*[pltpu.with_memory_space_constraint(a, pltpu.HBM) for a in arrays])
    sems, rest = out[:2 * nsem], out[2 * nsem:]
    return sems[:nsem], sems[nsem:], rest[:ns], rest[ns:ns + nl], rest[-1]


def _split_wait(handle, after, plan, *, name):
    send_sems, recv_sems, srcs, lands, _ = handle
    ns, nl, nsem = len(srcs), len(lands), len(send_sems)

    def body(*refs):
        src_refs, land_refs = refs[:ns], refs[ns:ns + nl]
        send_refs, recv_refs = refs[ns + nl:ns + nl + nsem], refs[ns + nl + nsem:ns + nl + 2 * nsem]
        x, y, c = lax.axis_index("x"), lax.axis_index("y"), lax.axis_index("c")
        n = 0
        for px, py in _chip_peers(x, y):
            for src, _, arrival in plan(src_refs, land_refs, 2 * x + y, 2 * px + py):
                cp = pltpu.make_async_remote_copy(src_ref=src, dst_ref=arrival, send_sem=send_refs[n],
                                                  recv_sem=recv_refs[n], device_id=(px, py, c), device_id_type=MESH)
                cp.wait_send()
                cp.wait_recv()
                n += 1

    out = pl.pallas_call(
        body, name=name, in_specs=[HBM] * (ns + nl) + [SEM] * (2 * nsem) + [ANY], out_specs=[HBM] * (ns + nl),
        out_shape=[pltpu.HBM(a.shape, a.dtype) for a in list(srcs) + list(lands)],
        input_output_aliases={i: i for i in range(ns + nl)},
        compiler_params=pltpu.CompilerParams(has_side_effects=DATAFLOW),
    )(*srcs, *lands, *send_sems, *recv_sems, after)
    return out[ns:]


def _exchange_sibling(arrays, *, name):
    n = len(arrays)

    def body(*refs):
        x, y, c = lax.axis_index("x"), lax.axis_index("y"), lax.axis_index("c")
        remote = [(0, i, refs[i], refs[n + i], refs[n + i], (x, y, 1 - c)) for i in range(n)]
        _run_copies([], remote, *refs[2 * n:])

    return _comm_call(body, arrays, [jax.ShapeDtypeStruct(a.shape, a.dtype) for a in arrays], n, 1, name=name)


def _sum_slots(r, *, token=None, name):
    shape = r.shape[1:]
    cols = shape[-1]
    rows = math.prod(shape) // cols
    tr = max(d for d in range(8, rows + 1, 8) if rows % d == 0 and d * cols * 16 <= (4 << 20))

    def body(r_ref, *refs):
        total = ((r_ref[0] + r_ref[1]) + r_ref[2]) + r_ref[3]
        refs[-1][...] = total if token is None else total + refs[0][0:1, 0:1]

    in_specs = [pl.BlockSpec((N_CHIPS, tr, cols), lambda i: (0, i, 0))]
    in_specs += [] if token is None else [_const((8, LANES))]
    return pl.pallas_call(
        body, grid=(rows // tr,), in_specs=in_specs,
        out_specs=_rows(tr, cols), out_shape=jax.ShapeDtypeStruct((rows, cols), F32), compiler_params=_cp(1),
        name=name,
    )(r.reshape(N_CHIPS, rows, cols), *([] if token is None else [token])).reshape(shape)


FIRST_WEIGHTS = ("ev_w_in", "ev_w_q_up", "ev_w_kv_up", "ev_w_out")
LATER_WEIGHTS = tuple(n for n in MATMUL_WEIGHTS if n not in FIRST_WEIGHTS)
LAST_GRADS = FIRST_WEIGHTS
EARLIER_GRADS = tuple(n for n in MATMUL_WEIGHTS if n not in LAST_GRADS)


def _my_chip():
    return 2 * lax.axis_index("x") + lax.axis_index("y")


def _gather_first(w):
    small = _pack([w[n] for n in SMALL_SHARDED], 8 * LANES)
    stacked = [n for n in FIRST_WEIGHTS if SHARD_AXIS[n] == w[n].ndim - 1 and w[n].shape[-1] % LANES]
    shards = [w[n].astype(BF16)[None] if n in stacked else w[n].astype(BF16) for n in FIRST_WEIGHTS]
    got = _gather_chips(shards + [small], [0 if n in stacked else SHARD_AXIS[n] for n in FIRST_WEIGHTS] + [0],
                        name="gather_first")
    full = {n: w[n] for n in REPLICATED}
    for n, g in zip(FIRST_WEIGHTS, got[:-1]):
        full[n] = jnp.concatenate([g[q] for q in range(N_CHIPS)], axis=SHARD_AXIS[n]) if n in stacked else g
    per_chip = [_unpack(got[-1][q * small.shape[0]:(q + 1) * small.shape[0]].reshape(-1),
                        [w[n].shape for n in SMALL_SHARDED]) for q in range(N_CHIPS)]
    for i, n in enumerate(SMALL_SHARDED):
        full[n] = jnp.concatenate([per_chip[q][i] for q in range(N_CHIPS)], axis=SHARD_AXIS[n])
    return full


def _gather_later_start(w):
    shards = [w[n].astype(BF16) for n in LATER_WEIGHTS]
    axes = [SHARD_AXIS[n] for n in LATER_WEIGHTS]
    lands = []
    for s, ax in zip(shards, axes):
        shape = tuple(d * (N_CHIPS if a == ax else 1) for a, d in enumerate(s.shape))
        lands.append(lax.dynamic_update_slice_in_dim(lax.empty(shape, s.dtype), s, _my_chip() * s.shape[ax], ax))
    return _split_start(shards, lands, _gather_plan(axes), name="gather_later_start"), _gather_plan(axes)


def _owner_major(g, axis):
    shape = g.shape
    size = shape[axis] // N_CHIPS
    g = jnp.moveaxis(g.reshape(shape[:axis] + (N_CHIPS, size) + shape[axis + 1:]), axis, 0)
    return g.reshape(N_CHIPS, -1, shape[-1] if axis < len(shape) - 1 else size)


def _exchange_start(items, *, name):
    me = _my_chip()
    srcs, lands, where = [], [], []
    for n, layers in enumerate(items):
        land = lax.empty((N_CHIPS, len(layers)) + layers[0].shape[1:], layers[0].dtype)
        for l, a in enumerate(layers):
            own = lax.dynamic_index_in_dim(a, me, 0, keepdims=True)[:, None]
            land = lax.dynamic_update_slice(land, own, (me, l) + (0,) * (a.ndim - 1))
            srcs.append(a)
            where.append((n, l))
        lands.append(land)
    plan = _exchange_plan(where)
    return _split_start(srcs, lands, plan, name=name), plan


def _earlier_items(grads, full_shapes):
    small = [_pack([jnp.split(grads[n].reshape(full_shapes[n]), N_CHIPS, axis=SHARD_AXIS[n])[q]
                    for n in SMALL_SHARDED], 8 * LANES) for q in range(N_CHIPS)]
    return [grads[n] for n in EARLIER_GRADS] + [[jnp.stack(small)]]


def _last_items(big, grads, full_shapes, loss):
    repl = _pack([grads[n].reshape(full_shapes[n]) for n in REPLICATED] + [loss.reshape(1)], 8 * LANES)
    return [big[n] for n in LAST_GRADS] + [[jnp.stack([repl] * N_CHIPS)]]


def kernel(
        x, mem, positions, ev_norm, ev_w_in, ev_pool_w, ev_pool_scale, ev_q_norm, ev_w_q_up, ev_kv_norm,
        ev_w_kv_up, ev_w_out, od_norm, od_w_in, od_conv_w, od_conv_b, od_w_rgate, od_b_rgate, od_w_igate,
        od_b_igate, od_lambda, od_w_out, xa_norm_x, xa_norm_mem, xa_w_q, xa_w_kv, xa_w_o, ffn_norm,
        ffn_w_gate_up, ffn_w_down, final_norm, loss_target, m_ev_norm, m_ev_w_in, m_ev_pool_w, m_ev_pool_scale,
        m_ev_q_norm, m_ev_w_q_up, m_ev_kv_norm, m_ev_w_kv_up, m_ev_w_out, m_od_norm, m_od_w_in, m_od_conv_w,
        m_od_conv_b, m_od_w_rgate, m_od_b_rgate, m_od_w_igate, m_od_b_igate, m_od_lambda, m_od_w_out,
        m_xa_norm_x, m_xa_norm_mem, m_xa_w_q, m_xa_w_kv, m_xa_w_o, m_ffn_norm, m_ffn_w_gate_up, m_ffn_w_down,
        m_final_norm, v_ev_norm, v_ev_w_in, v_ev_pool_w, v_ev_pool_scale, v_ev_q_norm, v_ev_w_q_up,
        v_ev_kv_norm, v_ev_w_kv_up, v_ev_w_out, v_od_norm, v_od_w_in, v_od_conv_w, v_od_conv_b, v_od_w_rgate,
        v_od_b_rgate, v_od_w_igate, v_od_b_igate, v_od_lambda, v_od_w_out, v_xa_norm_x, v_xa_norm_mem, v_xa_w_q,
        v_xa_w_kv, v_xa_w_o, v_ffn_norm, v_ffn_w_gate_up, v_ffn_w_down, v_final_norm):
    given = dict(locals())
    w = {n: given[n] for n in WEIGHTS}
    full_shapes = {n: tuple(d * (N_CHIPS if a == SHARD_AXIS.get(n) else 1) for a, d in enumerate(w[n].shape))
                   for n in WEIGHTS}
    full = _gather_first(w)
    later, later_plan = _gather_later_start(w)
    full["ev_norm"] = full["ev_norm"] + later[4][0:1, 0:1]
    exchange = {}

    def later_weights(after):
        return dict(zip(LATER_WEIGHTS, _split_wait(later, after, later_plan, name="gather_later_wait")))

    def exchange_earlier(grads):
        exchange["handle"], exchange["plan"] = _exchange_start(_earlier_items(grads, full_shapes),
                                                               name="exchange_earlier_start")
        return exchange["handle"][4]

    loss, grad_x, big, grads = _local_step(x[0], mem[0], positions[0], loss_target[0], full, later_weights,
                                           exchange_earlier)
    earlier = EARLIER_GRADS + ("small",)
    got = dict(zip(earlier, _split_wait(exchange["handle"], grad_x, exchange["plan"], name="exchange_earlier_wait")))
    last, last_plan = _exchange_start(_last_items(big, grads, full_shapes, loss), name="exchange_last_start")
    sums = {n: _sum_slots(got[n], token=last[4] if i == 0 else None, name=f"sum_chips_{n}")
            for i, n in enumerate(earlier)}
    got = dict(zip(LAST_GRADS + ("replicated",),
                   _split_wait(last, sums[earlier[-1]], last_plan, name="exchange_last_wait")))
    sums.update({n: _sum_slots(got[n], name=f"sum_chips_{n}") for n in got})
    mine = [sums[n] for n in MATMUL_WEIGHTS + ("small", "replicated")]
    other = _exchange_sibling(mine, name="exchange_sibling")
    out = {}
    for i, n in enumerate(MATMUL_WEIGHTS):
        out[n] = _adamw(w[n], mine[i].reshape(w[n].shape), other[i].reshape(w[n].shape), given["m_" + n],
                        given["v_" + n], name=f"adamw_{n}")
    for i, group in ((len(MATMUL_WEIGHTS), SMALL_SHARDED), (len(MATMUL_WEIGHTS) + 1, REPLICATED)):
        spare = [jnp.zeros((1,), F32)] if group is REPLICATED else []
        packed = [_pack([given[pre + n] for n in group] + spare, 8 * LANES) for pre in ("", "m_", "v_")]
        res = _adamw(packed[0], mine[i].reshape(packed[0].shape), other[i].reshape(packed[0].shape), packed[1],
                     packed[2], name=f"adamw_group{i}")
        shapes = [w[n].shape for n in group] + [(1,)] * len(spare)
        for j, arrs in enumerate(zip(*[_unpack(r.reshape(-1), shapes) for r in res])):
            if j < len(group):
                out[group[j]] = list(arrs)
            else:
                loss = arrs[0][0]
    return (loss, grad_x[None], *[out[n][k] for k in range(4) for n in WEIGHTS])
```

```python
import functools
import math

import jax
import jax.numpy as jnp
from jax import lax
from jax.experimental import pallas as pl
from jax.experimental.pallas import tpu as pltpu

F32 = jnp.float32
BF16 = jnp.bfloat16

D_MODEL = 1024
POOL_DIM = 512
POOL_WINDOWS = (2, 4, 8, 16)
POOL_GROUP = 128
MLA_HEADS = 8
QK_NOPE = 64
QK_ROPE = 32
QK_DIM = QK_NOPE + QK_ROPE
V_HEAD = 64
HEAD_PAD = 128
Q_RANK = 256
KV_RANK = 128
ROPE_BASE = 10000.0
LRU_HEADS = 4
LRU_HEAD_DIM = 256
CONV_WIDTH = 4
LRU_C = 8.0
MEM_HEADS = 4
MEM_HEAD_DIM = 256
D_FF = 2816
RMS_EPS = 1e-6
NEG_INF = -1e30

ADAM_LR = 0.001
ADAM_B1 = 0.9
ADAM_B2 = 0.999
ADAM_EPS = 1e-08
ADAM_WD = 0.01
ADAM_STEP = 10

N_CHIPS = 4
LANES = 128
VMEM_LIMIT = 56 * 1024 * 1024
MESH = pl.DeviceIdType.MESH
ANY = pl.BlockSpec(memory_space=pl.ANY)
MIX_DIM = POOL_DIM + MLA_HEADS * HEAD_PAD

NN = (((1,), (0,)), ((), ()))
NT = (((1,), (1,)), ((), ()))
TN = (((0,), (0,)), ((), ()))


def _cp(n):
    return pltpu.CompilerParams(dimension_semantics=("arbitrary",) * n, vmem_limit_bytes=VMEM_LIMIT)


def _dot(a, b, dims=NN):
    return lax.dot_general(a, b, dims, preferred_element_type=F32)


def _rows(ts, w, cb=0):
    return pl.BlockSpec((ts, w), lambda i: (i, cb))


def _const(shape):
    return pl.BlockSpec(shape, lambda i: (0,) * len(shape))


def _mm(a, bs, epi, outs, *, tn, nj, nt=False, extras=(), rows=(), sums=(), a_cb=0, k=None, tm=None, name):
    M = a.shape[0]
    k = k or a.shape[1]
    tm = tm or min(M, 512)
    nb, ne, nr, no = len(bs), len(extras), len(rows), len(outs)
    dims = NT if nt else NN
    assert not sums or nj == 1

    def body(*refs):
        av = refs[0][...].astype(BF16)
        accs = [_dot(av, r[...].astype(BF16), dims) for r in refs[1:1 + nb]]
        vals = epi(accs, [r[...] for r in refs[1 + nb:1 + nb + ne + nr]])
        outs_refs = refs[1 + nb + ne + nr:]
        for o, v in zip(outs_refs[:no], vals[:no]):
            o[...] = v.astype(o.dtype)
        if sums:
            @pl.when(pl.program_id(1) == 0)
            def _():
                for o in outs_refs[no:]:
                    o[...] = jnp.zeros_like(o)

            for o, v in zip(outs_refs[no:], vals[no:]):
                o[...] += v

    in_specs = [pl.BlockSpec((tm, k), lambda j, i: (i, a_cb))]
    for (_, rb, cb) in bs:
        if nt:
            in_specs.append(pl.BlockSpec((tn, k), lambda j, i, rb=rb, cb=cb: (rb + j, cb)))
        else:
            in_specs.append(pl.BlockSpec((k, tn), lambda j, i, rb=rb, cb=cb: (rb, cb + j)))
    for (_, cb) in extras:
        in_specs.append(pl.BlockSpec((tm, tn), lambda j, i, cb=cb: (i, cb + j)))
    in_specs += [pl.BlockSpec((1, tn), lambda j, i: (0, 0))] * nr
    out_specs = [pl.BlockSpec((tm, tn), lambda j, i, cb=cb: (i, cb + j)) for (_, _, cb) in outs]
    out_specs += [pl.BlockSpec((1, w), lambda j, i: (0, 0)) for w in sums]
    res = pl.pallas_call(
        body, grid=(nj, M // tm), in_specs=in_specs, out_specs=out_specs,
        out_shape=[jax.ShapeDtypeStruct((M, n), dt) for (n, dt, _) in outs]
        + [jax.ShapeDtypeStruct((1, w), F32) for w in sums],
        compiler_params=_cp(2), name=name,
    )(a, *[b for (b, _, _) in bs], *[e for (e, _) in extras], *rows)
    return res


def _first(accs, extras):
    return [accs[0]]


def _add_res(accs, extras):
    return [accs[0] + extras[0].astype(F32)]


def _norm_bwd_epilogue(partials):
    def epi(accs, vals):
        dh = accs[0]
        for part in vals[:partials]:
            dh = dh + part.astype(F32)
        x, res, g = vals[partials:partials + 3]
        r = lax.rsqrt(jnp.mean(x * x, axis=-1, keepdims=True) + RMS_EPS)
        n = x * r
        dn = dh * g
        return [r * (dn - n * jnp.mean(dn * n, axis=-1, keepdims=True)) + res, jnp.sum(dh * n, axis=0, keepdims=True)]

    return epi


TN_VMEM_BUDGET = 36 * 1024 * 1024


def _contraction_rows(S, row_bytes, out_elems):
    ts = min(S, 2048)
    while ts > 512 and 2 * (ts * row_bytes + out_elems * 4) > TN_VMEM_BUDGET:
        ts //= 2
    return ts


def _mm_tn(a, b, *, ka=None, a_cb=0, nb=None, b_cb=0, tk=None, tn=None, ts=None, name):
    S = a.shape[0]
    ka = ka or a.shape[1]
    nb = nb or b.shape[1]
    tk = tk or ka
    tn = tn or nb
    ts = ts or _contraction_rows(S, tk * a.dtype.itemsize + tn * b.dtype.itemsize, tk * tn)
    a0, b0 = a_cb * (ka // tk), b_cb * (nb // tn)

    def body(a_ref, b_ref, o_ref):
        @pl.when(pl.program_id(2) == 0)
        def _():
            o_ref[...] = jnp.zeros_like(o_ref)

        o_ref[...] += _dot(a_ref[...].astype(BF16), b_ref[...].astype(BF16), TN)

    return pl.pallas_call(
        body, grid=(ka // tk, nb // tn, S // ts),
        in_specs=[pl.BlockSpec((ts, tk), lambda p, q, s: (s, a0 + p)),
                  pl.BlockSpec((ts, tn), lambda p, q, s: (s, b0 + q))],
        out_specs=pl.BlockSpec((tk, tn), lambda p, q, s: (p, q)),
        out_shape=jax.ShapeDtypeStruct((ka, nb), F32), compiler_params=_cp(3), name=name,
    )(a, b)


def _mm_tn_owners(a, bs, *, name):
    S, ka = a.shape
    nb = sum(b.shape[1] for b in bs)
    tn = nb // N_CHIPS
    ts = _contraction_rows(S, ka * a.dtype.itemsize + len(bs) * tn * bs[0].dtype.itemsize, ka * tn)
    per = N_CHIPS // len(bs)

    def body(a_ref, *refs):
        o_ref = refs[-1]
        q = pl.program_id(0)

        @pl.when(pl.program_id(1) == 0)
        def _():
            o_ref[...] = jnp.zeros_like(o_ref)

        av = a_ref[...].astype(BF16)
        for n, b_ref in enumerate(refs[:-1]):
            @pl.when(q // per == n)
            def _():
                o_ref[0] += _dot(av, b_ref[...].astype(BF16), TN)

    in_specs = [pl.BlockSpec((ts, ka), lambda q, s: (s, 0))]
    for n in range(len(bs)):
        in_specs.append(pl.BlockSpec((ts, tn), lambda q, s, n=n: (jnp.where(q // per == n, s, 0),
                                                                  jnp.clip(q - n * per, 0, per - 1))))
    return pl.pallas_call(
        body, grid=(N_CHIPS, S // ts), in_specs=in_specs,
        out_specs=pl.BlockSpec((1, ka, tn), lambda q, s: (q, 0, 0)),
        out_shape=jax.ShapeDtypeStruct((N_CHIPS, ka, tn), F32), compiler_params=_cp(2), name=name,
    )(a, *bs)


def _mm_tn_grouped(a, b, groups, w, *, name):
    S = a.shape[0]
    ts = _contraction_rows(S, w * (a.dtype.itemsize + b.dtype.itemsize), w * w)

    def body(a_ref, b_ref, o_ref):
        @pl.when(pl.program_id(1) == 0)
        def _():
            o_ref[...] = jnp.zeros_like(o_ref)

        o_ref[0] += _dot(a_ref[...].astype(BF16), b_ref[...].astype(BF16), TN)

    return pl.pallas_call(
        body, grid=(groups, S // ts),
        in_specs=[pl.BlockSpec((ts, w), lambda g, s: (s, g)), pl.BlockSpec((ts, w), lambda g, s: (s, g))],
        out_specs=pl.BlockSpec((1, w, w), lambda g, s: (g, 0, 0)),
        out_shape=jax.ShapeDtypeStruct((groups, w, w), F32), compiler_params=_cp(2), name=name,
    )(a, b)


def _rms(x, g, *, cb=0, w=None, ts=None, name):
    S = x.shape[0]
    w = w or x.shape[1]
    ts = ts or min(S, 512)

    def body(x_ref, g_ref, o_ref):
        xv = x_ref[...].astype(F32)
        r = lax.rsqrt(jnp.mean(xv * xv, axis=-1, keepdims=True) + RMS_EPS)
        o_ref[...] = (xv * r * g_ref[...]).astype(o_ref.dtype)

    return pl.pallas_call(
        body, grid=(S // ts,), in_specs=[_rows(ts, w, cb), _const((1, w))], out_specs=_rows(ts, w),
        out_shape=jax.ShapeDtypeStruct((S, w), BF16), compiler_params=_cp(1), name=name,
    )(x, g.reshape(1, w))


def _rms_bwd(x, g, dy, *, cb=0, w=None, res=None, out_dtype=F32, ts=None, name):
    S = x.shape[0]
    w = w or x.shape[1]
    ts = ts or min(S, 512)
    has_res = res is not None

    def body(*refs):
        x_ref, g_ref, dy_ref = refs[:3]
        dx_ref, dg_ref = refs[-2:]
        xv = x_ref[...].astype(F32)
        r = lax.rsqrt(jnp.mean(xv * xv, axis=-1, keepdims=True) + RMS_EPS)
        n = xv * r
        dyv = dy_ref[...].astype(F32)
        dn = dyv * g_ref[...]
        dx = r * (dn - n * jnp.mean(dn * n, axis=-1, keepdims=True))
        if has_res:
            dx = dx + refs[3][...].astype(F32)
        dx_ref[...] = dx.astype(dx_ref.dtype)

        @pl.when(pl.program_id(0) == 0)
        def _():
            dg_ref[...] = jnp.zeros_like(dg_ref)

        dg_ref[...] += jnp.sum(dyv * n, axis=0, keepdims=True)

    ins = [x, g.reshape(1, w), dy] + ([res] if has_res else [])
    in_specs = [_rows(ts, w, cb), _const((1, w)), _rows(ts, w)] + ([_rows(ts, w)] if has_res else [])
    return pl.pallas_call(
        body, grid=(S // ts,), in_specs=in_specs, out_specs=[_rows(ts, w), _const((1, w))],
        out_shape=[jax.ShapeDtypeStruct((S, w), out_dtype), jax.ShapeDtypeStruct((1, w), F32)],
        compiler_params=_cp(1), name=name,
    )(*ins)


HALO = 16


def _pool_counts(i, ts, rows, first_row):
    t = i * ts + first_row + lax.broadcasted_iota(jnp.int32, (rows, 1), 0)
    return [jnp.minimum(t + 1, w).astype(F32) for w in POOL_WINDOWS]


def _pool_fwd(z, pool_w, pool_scale, *, name):
    S = z.shape[0]
    ts = min(S, 512)
    nh = ts // HALO

    def body(u_ref, halo_ref, w_ref, sc_ref, y_ref, p_ref):
        i = pl.program_id(0)
        u = u_ref[...]
        halo = jnp.where(i > 0, halo_ref[...], 0.0)
        xe = jnp.concatenate([halo, u], axis=0)
        sums = []
        s = xe
        for sh in (1, 2, 4, 8):
            s = s + pltpu.roll(s, sh, 0)
            sums.append(s)
        cnts = _pool_counts(i, ts, ts, 0)
        for g in range(4):
            lo, hi = g * POOL_GROUP, (g + 1) * POOL_GROUP
            pooled = (sums[g][HALO:, lo:hi] / cnts[g] - u[:, lo:hi]).astype(BF16)
            p_ref[:, lo:hi] = pooled
            y_ref[:, lo:hi] = (_dot(pooled, w_ref[g]) * sc_ref[:, lo:hi]).astype(y_ref.dtype)

    return pl.pallas_call(
        body, grid=(S // ts,),
        in_specs=[_rows(ts, POOL_DIM), pl.BlockSpec((HALO, POOL_DIM), lambda i: (jnp.maximum(i * nh - 1, 0), 0)),
                  _const((4, POOL_GROUP, POOL_GROUP)), _const((1, POOL_DIM))],
        out_specs=[_rows(ts, POOL_DIM), _rows(ts, POOL_DIM)],
        out_shape=[jax.ShapeDtypeStruct((S, MIX_DIM), BF16), jax.ShapeDtypeStruct((S, POOL_DIM), BF16)],
        compiler_params=_cp(1), name=name,
    )(z, z, pool_w, pool_scale)


def _pool_bwd(dmix, pooled, pool_w, pool_scale, *, name):
    S = dmix.shape[0]
    ts = min(S, 512)
    nh = ts // HALO
    last = S // HALO - 1

    def body(dy_ref, dyh_ref, p_ref, w_ref, sc_ref, du_ref, dyp_ref, dsc_ref):
        i = pl.program_id(0)
        dyv = dy_ref[...].astype(F32)
        dyh = jnp.where(i < pl.num_programs(0) - 1, dyh_ref[...].astype(F32), 0.0)
        dye = jnp.concatenate([dyv, dyh], axis=0) * sc_ref[...]
        dypre = dye.astype(BF16)
        dyp_ref[...] = dypre[:ts]
        cnts = _pool_counts(i, ts, ts + HALO, 0)
        n = ts + HALO
        dsc = []
        for g in range(4):
            lo, hi = g * POOL_GROUP, (g + 1) * POOL_GROUP
            ypre = _dot(p_ref[:, lo:hi], w_ref[g])
            dsc.append(jnp.sum(dyv[:, lo:hi] * ypre, axis=0, keepdims=True))
            dpool = _dot(dypre[:, lo:hi], w_ref[g], NT)
            s = dpool / cnts[g]
            for sh in (1, 2, 4, 8)[:g + 1]:
                s = s + pltpu.roll(s, n - sh, 0)
            du_ref[:, lo:hi] = (s[:ts] - dpool[:ts]).astype(du_ref.dtype)

        @pl.when(i == 0)
        def _():
            dsc_ref[...] = jnp.zeros_like(dsc_ref)

        dsc_ref[...] += jnp.concatenate(dsc, axis=1)

    return pl.pallas_call(
        body, grid=(S // ts,),
        in_specs=[_rows(ts, POOL_DIM),
                  pl.BlockSpec((HALO, POOL_DIM), lambda i: (jnp.minimum((i + 1) * nh, last), 0)),
                  _rows(ts, POOL_DIM), _const((4, POOL_GROUP, POOL_GROUP)), _const((1, POOL_DIM))],
        out_specs=[_rows(ts, POOL_DIM), _rows(ts, POOL_DIM), _const((1, POOL_DIM))],
        out_shape=[jax.ShapeDtypeStruct((S, POOL_DIM), BF16)] * 2 + [jax.ShapeDtypeStruct((1, POOL_DIM), F32)],
        compiler_params=_cp(1), name=name,
    )(dmix, dmix, pooled, pool_w, pool_scale)


def _rope_partner(t):
    lane = lax.broadcasted_iota(jnp.int32, t.shape, 1)
    swapped = jnp.where(lane < QK_NOPE + QK_ROPE // 2, pltpu.roll(t, HEAD_PAD - QK_ROPE // 2, 1),
                        pltpu.roll(t, QK_ROPE // 2, 1))
    return jnp.where((lane >= QK_NOPE) & (lane < QK_DIM), swapped, 0.0)


def _rope_fwd(q_pad, k_pad, z, ctab, stab, *, name):
    S = q_pad.shape[0]
    ts = min(S, 512)

    def body(q_ref, k_ref, kr_ref, c_ref, s_ref, qo_ref, ko_ref):
        c, s = c_ref[...], s_ref[...]
        kr = kr_ref[...]
        kr_rot = kr * c + _rope_partner(kr) * s
        for h in range(MLA_HEADS):
            lo, hi = h * HEAD_PAD, (h + 1) * HEAD_PAD
            q = q_ref[:, lo:hi]
            qo_ref[:, lo:hi] = (q * c + _rope_partner(q) * s).astype(qo_ref.dtype)
            ko_ref[:, lo:hi] = (k_ref[:, lo:hi] + kr_rot).astype(ko_ref.dtype)

    wide = _rows(ts, MLA_HEADS * HEAD_PAD)
    return pl.pallas_call(
        body, grid=(S // ts,),
        in_specs=[wide, wide, _rows(ts, HEAD_PAD, 7), _rows(ts, HEAD_PAD), _rows(ts, HEAD_PAD)],
        out_specs=[wide, wide], out_shape=[jax.ShapeDtypeStruct((S, MLA_HEADS * HEAD_PAD), BF16)] * 2,
        compiler_params=_cp(1), name=name,
    )(q_pad, k_pad, z, ctab, stab)


def _rope_bwd(dq_rot, dk_cat, ctab, stab, *, name):
    S = dq_rot.shape[0]
    ts = min(S, 512)

    def body(dq_ref, dk_ref, c_ref, s_ref, dqo_ref, dkr_ref):
        c, s = c_ref[...], s_ref[...]
        for h in range(MLA_HEADS):
            g = dq_ref[:, h * HEAD_PAD:(h + 1) * HEAD_PAD]
            dqo_ref[:, h * HEAD_PAD:(h + 1) * HEAD_PAD] = (g * c + _rope_partner(g * s)).astype(dqo_ref.dtype)
        dk = dk_ref[...]
        g = dk[:, :HEAD_PAD]
        for h in range(1, MLA_HEADS):
            g = g + dk[:, h * HEAD_PAD:(h + 1) * HEAD_PAD]
        lane = lax.broadcasted_iota(jnp.int32, g.shape, 1)
        on_rope = (lane >= QK_NOPE) & (lane < QK_DIM)
        dkr_ref[...] = jnp.where(on_rope, g * c + _rope_partner(g * s), 0.0).astype(dkr_ref.dtype)

    wide = _rows(ts, MLA_HEADS * HEAD_PAD)
    return pl.pallas_call(
        body, grid=(S // ts,), in_specs=[wide, wide, _rows(ts, HEAD_PAD), _rows(ts, HEAD_PAD)],
        out_specs=[wide, _rows(ts, HEAD_PAD)],
        out_shape=[jax.ShapeDtypeStruct((S, MLA_HEADS * HEAD_PAD), BF16), jax.ShapeDtypeStruct((S, HEAD_PAD), BF16)],
        compiler_params=_cp(1), name=name,
    )(dq_rot, dk_cat, ctab, stab)


ATT_SCALE = QK_DIM ** -0.5
LOG2E = math.log2(math.e)


HEADS_PER_STEP = 2
ATT_COL0 = POOL_DIM // HEAD_PAD


FWD_TILE = 1024


def _stat_rows(col):
    return jnp.broadcast_to(col, (col.shape[0], LANES)).T[0:8]


def _retile_rows(rows, tq):
    heads, n8, t = rows.shape
    if t == tq:
        return rows
    flat = rows.reshape(heads, n8 // 8, 8, t)[:, :, 0].reshape(heads, -1, 1, tq)
    return jnp.broadcast_to(flat, (heads, flat.shape[1], 8, tq)).reshape(heads, -1, tq)


def _flash_fwd(q, k, v, mix, *, name):
    S = q.shape[0]
    tq = FWD_TILE if S % FWD_TILE == 0 else min(S, 512)
    nq = S // tq
    hs = HEADS_PER_STEP
    wide = hs * HEAD_PAD

    def body(q_ref, k_ref, v_ref, mix_ref, o_ref, lse_ref):
        qi = pl.program_id(1)
        qv = [q_ref[:, a * HEAD_PAD:(a + 1) * HEAD_PAD] for a in range(hs)]

        def step(j, carry, masked):
            off = pl.multiple_of(j * tq, tq)
            out = []
            for a in range(hs):
                m, acc = carry[a]
                s = _dot(qv[a], k_ref[pl.ds(off, tq), a * HEAD_PAD:(a + 1) * HEAD_PAD], NT)
                if masked:
                    row = lax.broadcasted_iota(jnp.int32, (tq, tq), 0)
                    col = lax.broadcasted_iota(jnp.int32, (tq, tq), 1)
                    s = jnp.where(col <= row, s, NEG_INF)
                m_new = jnp.maximum(m, jnp.max(s, axis=-1, keepdims=True))
                p = jnp.exp2((s - m_new) * (ATT_SCALE * LOG2E))
                alpha = jnp.exp2((m - m_new) * (ATT_SCALE * LOG2E))
                acc = alpha * acc + _dot(p.astype(BF16), v_ref[pl.ds(off, tq), a * HEAD_PAD:(a + 1) * HEAD_PAD])
                out.append((m_new, acc))
            return tuple(out)

        one = (jnp.full((tq, 1), NEG_INF, F32), jnp.zeros((tq, HEAD_PAD), F32))
        carry = lax.fori_loop(0, qi, lambda j, c: step(j, c, False), (one,) * hs)
        carry = step(qi, carry, True)
        for a in range(hs):
            m, acc = carry[a]
            l = acc[:, V_HEAD:V_HEAD + 1]
            o_ref[:, a * HEAD_PAD:(a + 1) * HEAD_PAD] = (acc / l).astype(o_ref.dtype)
            lse_ref[a] = _stat_rows(m * ATT_SCALE + jnp.log(l))

    blk = pl.BlockSpec((tq, wide), lambda h, i: (i, h))
    full = pl.BlockSpec((S, wide), lambda h, i: (0, h))
    return pl.pallas_call(
        body, grid=(MLA_HEADS // hs, nq), in_specs=[blk, full, full, ANY],
        out_specs=[pl.BlockSpec((tq, wide), lambda h, i: (i, ATT_COL0 // hs + h)),
                   pl.BlockSpec((hs, 8, tq), lambda h, i: (h, i, 0))],
        out_shape=[jax.ShapeDtypeStruct(mix.shape, mix.dtype), jax.ShapeDtypeStruct((MLA_HEADS, nq * 8, tq), F32)],
        input_output_aliases={3: 0}, compiler_params=_cp(2), name=name,
    )(q, k, v, mix)


BWD_TILE = 1024
BWD_HEADS_PER_STEP = 1


def _bwd_tile(S):
    return BWD_TILE if S % BWD_TILE == 0 else min(S, 512)


def _attn_delta(dmix, mix, *, name):
    S = mix.shape[0]
    ts = _bwd_tile(S)
    half = MLA_HEADS // 2
    halves = [_rows(ts, half * HEAD_PAD, 1), _rows(ts, half * HEAD_PAD, 2)]

    def body(do0_ref, do1_ref, o0_ref, o1_ref, d_ref):
        for n, (do_ref, o_ref) in enumerate(((do0_ref, o0_ref), (do1_ref, o1_ref))):
            prod = do_ref[...].astype(F32) * o_ref[...].astype(F32)
            for a in range(half):
                d_ref[n * half + a] = _stat_rows(
                    jnp.sum(prod[:, a * HEAD_PAD:(a + 1) * HEAD_PAD], axis=-1, keepdims=True))

    return pl.pallas_call(
        body, grid=(S // ts,), in_specs=halves + halves,
        out_specs=pl.BlockSpec((MLA_HEADS, 8, ts), lambda i: (0, i, 0)),
        out_shape=jax.ShapeDtypeStruct((MLA_HEADS, (S // ts) * 8, ts), F32), compiler_params=_cp(1), name=name,
    )(dmix, dmix, mix, mix)


def _flash_bwd(q, k, v, dmix, lse_rows, delta_rows, *, name):
    S = q.shape[0]
    tq = _bwd_tile(S)
    nq = S // tq
    hs = BWD_HEADS_PER_STEP
    wide = hs * HEAD_PAD

    def body(q_hbm, do_hbm, lse_ref, dl_ref, k_ref, v_ref, dq_hbm, dk_ref, dv_ref, q_all, do_all, dq_all):
        g, j = pl.program_id(0), pl.program_id(1)
        cols = pl.multiple_of(g * wide, wide)

        @pl.when(j == 0)
        def _():
            pltpu.sync_copy(q_hbm.at[:, pl.ds(cols, wide)], q_all)
            pltpu.sync_copy(do_hbm.at[:, pl.ds(POOL_DIM + cols, wide)], do_all)
            dq_all[...] = jnp.zeros_like(dq_all)

        heads = [slice(a * HEAD_PAD, (a + 1) * HEAD_PAD) for a in range(hs)]
        kv = [k_ref[:, a] for a in heads]
        vv = [v_ref[:, a] for a in heads]

        def step(i, carry, masked):
            off = pl.multiple_of(i * tq, tq)
            off8 = pl.multiple_of(i * 8, 8)
            out = []
            for a in range(hs):
                dk, dv = carry[a]
                qv = q_all[pl.ds(off, tq), heads[a]]
                dov = do_all[pl.ds(off, tq), heads[a]]
                lse2 = lse_ref[a, pl.ds(off8, 8), :][0:1] * LOG2E
                dl = dl_ref[a, pl.ds(off8, 8), :][0:1]
                st = _dot(kv[a], qv, NT)
                if masked:
                    krow = lax.broadcasted_iota(jnp.int32, (tq, tq), 0)
                    qcol = lax.broadcasted_iota(jnp.int32, (tq, tq), 1)
                    st = jnp.where(krow <= qcol, st, NEG_INF)
                pt = jnp.exp2(st * (ATT_SCALE * LOG2E) - lse2)
                dv = dv + _dot(pt.astype(BF16), dov)
                dst = (pt * (_dot(vv[a], dov, NT) - dl)).astype(BF16)
                dk = dk + _dot(dst, qv)
                dq_all[pl.ds(off, tq), heads[a]] += _dot(dst, kv[a], TN)
                out.append((dk, dv))
            return tuple(out)

        zero = jnp.zeros((tq, HEAD_PAD), F32)
        carry = step(j, ((zero, zero),) * hs, True)
        carry = lax.fori_loop(j + 1, nq, lambda i, c: step(i, c, False), carry)
        for a in range(hs):
            dk_ref[:, heads[a]] = carry[a][0] * ATT_SCALE
            dv_ref[:, heads[a]] = carry[a][1]

        @pl.when(j == nq - 1)
        def _():
            dq_all[...] = dq_all[...] * ATT_SCALE
            pltpu.sync_copy(dq_all, dq_hbm.at[:, pl.ds(cols, wide)])

    blk = pl.BlockSpec((tq, wide), lambda g, j: (j, g))
    stat = pl.BlockSpec((hs, nq * 8, tq), lambda g, j: (g, 0, 0))
    full = jax.ShapeDtypeStruct((S, MLA_HEADS * HEAD_PAD), F32)
    return pl.pallas_call(
        body, grid=(MLA_HEADS // hs, nq), in_specs=[ANY, ANY, stat, stat, blk, blk], out_specs=[ANY, blk, blk],
        out_shape=[full, full, full],
        scratch_shapes=[pltpu.VMEM((S, wide), BF16), pltpu.VMEM((S, wide), BF16), pltpu.VMEM((S, wide), F32)],
        compiler_params=_cp(2), name=name,
    )(q, dmix, lse_rows, delta_rows, k, v)


MEM_SCALE = MEM_HEAD_DIM ** -0.5


def _xattn_probs(qh, kh):
    s = _dot(qh, kh, NT) * MEM_SCALE
    e = jnp.exp(s - jnp.max(s, axis=-1, keepdims=True))
    return e / jnp.sum(e, axis=-1, keepdims=True)


def _xattn_fwd(q, kvm, *, name):
    S = q.shape[0]
    ts = min(S, 512)
    nm = kvm.shape[0]

    def body(q_ref, kv_ref, o_ref):
        for h in range(MEM_HEADS):
            lo, hi = h * MEM_HEAD_DIM, (h + 1) * MEM_HEAD_DIM
            p = _xattn_probs(q_ref[:, lo:hi], kv_ref[:, lo:hi])
            o_ref[:, lo:hi] = _dot(p.astype(BF16), kv_ref[:, D_MODEL + lo:D_MODEL + hi]).astype(o_ref.dtype)

    return pl.pallas_call(
        body, grid=(S // ts,), in_specs=[_rows(ts, D_MODEL), _const((nm, 2 * D_MODEL))],
        out_specs=_rows(ts, D_MODEL), out_shape=jax.ShapeDtypeStruct((S, D_MODEL), BF16),
        compiler_params=_cp(1), name=name,
    )(q, kvm)


def _xattn_bwd(q, kvm, do, *, name):
    S = q.shape[0]
    ts = min(S, 512)
    nm = kvm.shape[0]

    def body(q_ref, kv_ref, do_ref, dq_ref, dkv_ref):
        @pl.when(pl.program_id(0) == 0)
        def _():
            dkv_ref[...] = jnp.zeros_like(dkv_ref)

        for h in range(MEM_HEADS):
            lo, hi = h * MEM_HEAD_DIM, (h + 1) * MEM_HEAD_DIM
            qh, kh, vh = q_ref[:, lo:hi], kv_ref[:, lo:hi], kv_ref[:, D_MODEL + lo:D_MODEL + hi]
            doh = do_ref[:, lo:hi].astype(BF16)
            p = _xattn_probs(qh, kh)
            dp = _dot(doh, vh, NT)
            ds = (p * (dp - jnp.sum(dp * p, axis=-1, keepdims=True)) * MEM_SCALE).astype(BF16)
            dq_ref[:, lo:hi] = _dot(ds, kh).astype(dq_ref.dtype)
            dkv_ref[:, lo:hi] += _dot(ds, qh, TN)
            dkv_ref[:, D_MODEL + lo:D_MODEL + hi] += _dot(p.astype(BF16), doh, TN)

    return pl.pallas_call(
        body, grid=(S // ts,), in_specs=[_rows(ts, D_MODEL), _const((nm, 2 * D_MODEL)), _rows(ts, D_MODEL)],
        out_specs=[_rows(ts, D_MODEL), _const((nm, 2 * D_MODEL))],
        out_shape=[jax.ShapeDtypeStruct((S, D_MODEL), BF16), jax.ShapeDtypeStruct((nm, 2 * D_MODEL), F32)],
        compiler_params=_cp(1), name=name,
    )(q, kvm, do)


CONV_HALO = 8


def _sigmoid(x):
    return 1.0 / (1.0 + jnp.exp(-x))


def _softplus(x):
    return jnp.maximum(x, 0.0) + jnp.log(1.0 + jnp.exp(-jnp.abs(x)))


def _neg_expm1(x):
    series = -x * (1.0 + x * (1.0 / 2) * (1.0 + x * (1.0 / 3) * (1.0 + x * (1.0 / 4) * (1.0 + x * (1.0 / 5)))))
    return jnp.where(x > -0.05, series, 1.0 - jnp.exp(x))


GELU_C = math.sqrt(2.0 / math.pi)


def _gelu(x):
    return 0.5 * x * (1.0 + jnp.tanh(GELU_C * (x + 0.044715 * x * x * x)))


def _gelu_grad(x):
    t = jnp.tanh(GELU_C * (x + 0.044715 * x * x * x))
    return 0.5 * (1.0 + t) + 0.5 * x * (1.0 - t * t) * GELU_C * (1.0 + 3 * 0.044715 * x * x)


def _lru_gates(xc, wr_ref, br, wi_ref, bi, sp, reset):
    xcb = xc.astype(BF16)
    pr, pi = [], []
    for h in range(LRU_HEADS):
        lo, hi = h * LRU_HEAD_DIM, (h + 1) * LRU_HEAD_DIM
        pr.append(_dot(xcb[:, lo:hi], wr_ref[h]))
        pi.append(_dot(xcb[:, lo:hi], wi_ref[h]))
    r = _sigmoid(jnp.concatenate(pr, axis=1) + br)
    ig = _sigmoid(jnp.concatenate(pi, axis=1) + bi)
    log_a = -LRU_C * r * sp
    a = jnp.where(reset, 0.0, jnp.exp(log_a))
    mult = jnp.where(reset, 1.0, jnp.sqrt(jnp.maximum(_neg_expm1(2.0 * log_a), 0.0)))
    return r, ig, a, mult


SUBLANES = 8


def _compose_groups(a, b, reverse):
    n = a.shape[0]
    row = lax.broadcasted_iota(jnp.int32, a.shape, 0) % SUBLANES
    for s in (1, 2, 4):
        inside = (row < SUBLANES - s) if reverse else (row >= s)
        shift = n - s if reverse else s
        a_s = jnp.where(inside, pltpu.roll(a, shift, 0), 1.0)
        b_s = jnp.where(inside, pltpu.roll(b, shift, 0), 0.0)
        b = a * b_s + b
        a = a * a_s
    return a, b


def _chain_groups(a_buf, h_ref, state, reverse):
    groups = a_buf.shape[0] // SUBLANES

    def group(g, h_in):
        off = pl.multiple_of((groups - 1 - g if reverse else g) * SUBLANES, SUBLANES)
        h = a_buf[pl.ds(off, SUBLANES), :] * h_in + h_ref[pl.ds(off, SUBLANES), :]
        h_ref[pl.ds(off, SUBLANES), :] = h
        return jnp.broadcast_to(h[0:1] if reverse else h[SUBLANES - 1:SUBLANES], h.shape)

    return lax.fori_loop(0, groups, group, state, unroll=4)[0:1]


def _lru_fwd(z, reset, conv_w, conv_b, w_r, b_r, w_i, b_i, lam, *, name):
    S = z.shape[0]
    ts = min(S, 512)
    nh = ts // CONV_HALO
    W = D_MODEL

    def body(gate_ref, xb_ref, halo_ref, rs_ref, cw_ref, cb_ref, wr_ref, br_ref, wi_ref, bi_ref, lam_ref,
             xc_ref, h_ref, y_ref, a_buf, carry):
        i = pl.program_id(0)

        @pl.when(i == 0)
        def _():
            carry[...] = jnp.zeros_like(carry)

        halo = jnp.where(i > 0, halo_ref[...], 0.0)
        xe = jnp.concatenate([halo, xb_ref[...]], axis=0)
        xc = cb_ref[...] + cw_ref[3:4, :] * xe[CONV_HALO:]
        for kk in range(CONV_WIDTH - 1):
            xc = xc + cw_ref[kk:kk + 1, :] * pltpu.roll(xe, CONV_WIDTH - 1 - kk, 0)[CONV_HALO:]
        xc_ref[...] = xc
        reset = rs_ref[...] > 0.5
        _, ig, a, mult = _lru_gates(xc, wr_ref, br_ref[...], wi_ref, bi_ref[...], _softplus(-lam_ref[...]), reset)
        a_buf[...], h_ref[...] = _compose_groups(a, mult * (ig * xc), False)
        carry[...] = _chain_groups(a_buf, h_ref, jnp.broadcast_to(carry[...], (SUBLANES, W)), False)
        y_ref[...] = (_gelu(gate_ref[...]) * h_ref[...]).astype(y_ref.dtype)

    vec = _const((1, W))
    gw = _const((LRU_HEADS, LRU_HEAD_DIM, LRU_HEAD_DIM))
    return pl.pallas_call(
        body, grid=(S // ts,),
        in_specs=[_rows(ts, W, 0), _rows(ts, W, 1),
                  pl.BlockSpec((CONV_HALO, W), lambda i: (jnp.maximum(i * nh - 1, 0), 1)),
                  _rows(ts, 1), _const((CONV_WIDTH, W)), vec, gw, vec, gw, vec, vec],
        out_specs=[_rows(ts, W)] * 3,
        out_shape=[jax.ShapeDtypeStruct((S, W), F32), jax.ShapeDtypeStruct((S, W), F32),
                   jax.ShapeDtypeStruct((S, W), BF16)],
        scratch_shapes=[pltpu.VMEM((ts, W), F32), pltpu.VMEM((1, W), F32)],
        compiler_params=_cp(1), name=name,
    )(z, z, z, reset, conv_w, conv_b, w_r, b_r, w_i, b_i, lam)


def _lru_bwd(dy, z, xc, hseq, reset, w_r, b_r, w_i, b_i, lam, *, name):
    S = z.shape[0]
    ts = min(S, 512)
    nt = S // ts
    nh = ts // CONV_HALO
    W = D_MODEL

    def body(dy_ref, gate_ref, xc_ref, h_ref, hh_ref, rs_ref, wr_ref, br_ref, wi_ref, bi_ref, lam_ref,
             dg_ref, dxc_ref, dpr_ref, dpi_ref, acc_ref, a_buf, dh_buf, carry):
        i = pl.program_id(0)
        tile = nt - 1 - i

        @pl.when(i == 0)
        def _():
            carry[...] = jnp.zeros_like(carry)
            acc_ref[...] = jnp.zeros_like(acc_ref)

        xc = xc_ref[...]
        lam_v = lam_ref[...]
        sp = _softplus(-lam_v)
        reset = rs_ref[...] > 0.5
        r, ig, a, mult = _lru_gates(xc, wr_ref, br_ref[...], wi_ref, bi_ref[...], sp, reset)
        gate = gate_ref[...]
        dyv = dy_ref[...].astype(F32)
        h = h_ref[...]
        dg_ref[...] = (dyv * h * _gelu_grad(gate)).astype(dg_ref.dtype)
        last_row = lax.broadcasted_iota(jnp.int32, a.shape, 0) == ts - 1
        a_buf[...], dh_buf[...] = _compose_groups(jnp.where(last_row, 1.0, pltpu.roll(a, ts - 1, 0)),
                                                  dyv * _gelu(gate), True)
        _chain_groups(a_buf, dh_buf, jnp.broadcast_to(carry[...], (SUBLANES, W)), True)
        dh = dh_buf[...]
        carry[...] = a[0:1] * dh[0:1]
        hh = jnp.where(tile > 0, hh_ref[...], 0.0)
        h_prev = pltpu.roll(jnp.concatenate([hh, h], axis=0), 1, 0)[CONV_HALO:]
        da = dh * h_prev
        bx = ig * xc
        dmult = dh * bx
        dbx = dh * mult
        di = dbx * xc
        dlog_a = jnp.where(reset, 0.0, da * a - dmult * a * a / jnp.maximum(mult, 1e-30))
        dr = dlog_a * (-LRU_C) * sp
        dpre_r = dr * r * (1.0 - r)
        dpre_i = di * ig * (1.0 - ig)
        dprb, dpib = dpre_r.astype(BF16), dpre_i.astype(BF16)
        dpr_ref[...] = dprb
        dpi_ref[...] = dpib
        back = []
        for hd in range(LRU_HEADS):
            lo, hi = hd * LRU_HEAD_DIM, (hd + 1) * LRU_HEAD_DIM
            back.append(_dot(dprb[:, lo:hi], wr_ref[hd], NT) + _dot(dpib[:, lo:hi], wi_ref[hd], NT))
        dxc_ref[...] = dbx * ig + jnp.concatenate(back, axis=1)
        dlam = jnp.sum(dlog_a * (-LRU_C) * r, axis=0, keepdims=True) * (-_sigmoid(-lam_v))
        acc_ref[0:1, :] += jnp.sum(dpre_r, axis=0, keepdims=True)
        acc_ref[1:2, :] += jnp.sum(dpre_i, axis=0, keepdims=True)
        acc_ref[2:3, :] += dlam

    rev = lambda cb: pl.BlockSpec((ts, W), lambda i: (nt - 1 - i, cb))
    vec = _const((1, W))
    gw = _const((LRU_HEADS, LRU_HEAD_DIM, LRU_HEAD_DIM))
    return pl.pallas_call(
        body, grid=(nt,),
        in_specs=[rev(0), rev(0), rev(0), rev(0),
                  pl.BlockSpec((CONV_HALO, W), lambda i: (jnp.maximum((nt - 1 - i) * nh - 1, 0), 0)),
                  pl.BlockSpec((ts, 1), lambda i: (nt - 1 - i, 0)), gw, vec, gw, vec, vec],
        out_specs=[rev(0), rev(0), rev(0), rev(0), _const((8, W))],
        out_shape=[jax.ShapeDtypeStruct((S, W), BF16), jax.ShapeDtypeStruct((S, W), F32),
                   jax.ShapeDtypeStruct((S, W), BF16), jax.ShapeDtypeStruct((S, W), BF16),
                   jax.ShapeDtypeStruct((8, W), F32)],
        scratch_shapes=[pltpu.VMEM((ts, W), F32), pltpu.VMEM((ts, W), F32), pltpu.VMEM((1, W), F32)],
        compiler_params=_cp(1), name=name,
    )(dy, z, xc, hseq, hseq, reset, w_r, b_r, w_i, b_i, lam)


def _conv_bwd(dxc, z, conv_w, *, name):
    S = dxc.shape[0]
    ts = min(S, 512)
    nh = ts // CONV_HALO
    last = S // CONV_HALO - 1
    W = D_MODEL
    n = ts + CONV_HALO

    def body(d_ref, dn_ref, xb_ref, xp_ref, cw_ref, dxb_ref, acc_ref):
        i = pl.program_id(0)

        @pl.when(i == 0)
        def _():
            acc_ref[...] = jnp.zeros_like(acc_ref)

        d = d_ref[...]
        de = jnp.concatenate([d, jnp.where(i < pl.num_programs(0) - 1, dn_ref[...], 0.0)], axis=0)
        xe = jnp.concatenate([jnp.where(i > 0, xp_ref[...], 0.0), xb_ref[...]], axis=0)
        dxb = cw_ref[3:4, :] * d
        acc_ref[3:4, :] += jnp.sum(d * xe[CONV_HALO:], axis=0, keepdims=True)
        for kk in range(CONV_WIDTH - 1):
            sh = CONV_WIDTH - 1 - kk
            dxb = dxb + cw_ref[kk:kk + 1, :] * pltpu.roll(de, n - sh, 0)[:ts]
            acc_ref[kk:kk + 1, :] += jnp.sum(d * pltpu.roll(xe, sh, 0)[CONV_HALO:], axis=0, keepdims=True)
        dxb_ref[...] = dxb.astype(dxb_ref.dtype)
        acc_ref[4:5, :] += jnp.sum(d, axis=0, keepdims=True)

    return pl.pallas_call(
        body, grid=(S // ts,),
        in_specs=[_rows(ts, W), pl.BlockSpec((CONV_HALO, W), lambda i: (jnp.minimum((i + 1) * nh, last), 0)),
                  _rows(ts, W, 1), pl.BlockSpec((CONV_HALO, W), lambda i: (jnp.maximum(i * nh - 1, 0), 1)),
                  _const((CONV_WIDTH, W))],
        out_specs=[_rows(ts, W), _const((8, W))],
        out_shape=[jax.ShapeDtypeStruct((S, W), BF16), jax.ShapeDtypeStruct((8, W), F32)],
        compiler_params=_cp(1), name=name,
    )(dxc, dxc, z, z, conv_w)


def _loss_head(x, g, target, *, name):
    S, D = x.shape
    ts = min(S, 512)

    def body(x_ref, g_ref, t_ref, dx_ref, dg_ref, l_ref):
        @pl.when(pl.program_id(0) == 0)
        def _():
            dg_ref[...] = jnp.zeros_like(dg_ref)
            l_ref[...] = jnp.zeros_like(l_ref)

        xv = x_ref[...]
        r = lax.rsqrt(jnp.mean(xv * xv, axis=-1, keepdims=True) + RMS_EPS)
        n = xv * r
        err = n * g_ref[...] - t_ref[...]
        l_ref[...] += 0.5 * jnp.sum(jnp.sum(err * err, axis=-1, keepdims=True) * (1.0 / D), axis=0, keepdims=True)
        dy = err * (1.0 / D)
        dn = dy * g_ref[...]
        dx_ref[...] = r * (dn - n * jnp.mean(dn * n, axis=-1, keepdims=True))
        dg_ref[...] += jnp.sum(dy * n, axis=0, keepdims=True)

    return pl.pallas_call(
        body, grid=(S // ts,), in_specs=[_rows(ts, D), _const((1, D)), _rows(ts, D)],
        out_specs=[_rows(ts, D), _const((1, D)), _const((8, LANES))],
        out_shape=[jax.ShapeDtypeStruct((S, D), F32), jax.ShapeDtypeStruct((1, D), F32),
                   jax.ShapeDtypeStruct((8, LANES), F32)],
        compiler_params=_cp(1), name=name,
    )(x, g.reshape(1, D), target)


def _adamw(w, ga, gb, m, v, *, name):
    shape = w.shape
    cols = shape[-1]
    rows = w.size // cols
    br = rows
    if rows * cols * 4 > (1 << 20):
        br = max(d for d in range(8, rows + 1, 8) if rows % d == 0 and d * cols * 4 <= (1 << 20))

    def body(w_ref, ga_ref, gb_ref, m_ref, v_ref, g_ref, d_ref, mo_ref, vo_ref):
        gv = ga_ref[...] + gb_ref[...]
        g_ref[...] = gv
        mn = ADAM_B1 * m_ref[...] + (1.0 - ADAM_B1) * gv
        vn = ADAM_B2 * v_ref[...] + (1.0 - ADAM_B2) * (gv * gv)
        m_hat = mn / (1.0 - ADAM_B1 ** ADAM_STEP)
        v_hat = vn / (1.0 - ADAM_B2 ** ADAM_STEP)
        d_ref[...] = -ADAM_LR * (m_hat / (jnp.sqrt(v_hat) + ADAM_EPS) + ADAM_WD * w_ref[...])
        mo_ref[...] = mn
        vo_ref[...] = vn

    spec = _rows(br, cols)
    outs = pl.pallas_call(
        body, grid=(rows // br,), in_specs=[spec] * 5, out_specs=[spec] * 4,
        out_shape=[jax.ShapeDtypeStruct((rows, cols), F32)] * 4, compiler_params=_cp(1), name=name,
    )(*[t.reshape(rows, cols) for t in (w, ga, gb, m, v)])
    return [o.reshape(shape) for o in outs]


def _pad_heads(w, width):
    k = w.shape[0]
    return jnp.pad(w.reshape(k, MLA_HEADS, width), ((0, 0), (0, 0), (0, HEAD_PAD - width))).reshape(k, -1)


def _unpad_heads(w, width):
    k = w.shape[0]
    return w.reshape(k, MLA_HEADS, HEAD_PAD)[:, :, :width].reshape(k, MLA_HEADS * width)


def _rope_tables(positions):
    inv_freq = ROPE_BASE ** (-jnp.arange(0, QK_ROPE, 2, dtype=F32) / QK_ROPE)
    ang = positions.astype(F32)[:, None] * inv_freq
    cos, sin = jnp.cos(ang), jnp.sin(ang)
    S = positions.shape[0]
    ones, zeros = jnp.ones((S, QK_NOPE), F32), jnp.zeros((S, QK_NOPE), F32)
    ctab = jnp.concatenate([ones, cos, cos, ones[:, :HEAD_PAD - QK_DIM]], axis=1)
    stab = jnp.concatenate([zeros, -sin, sin, zeros[:, :HEAD_PAD - QK_DIM]], axis=1)
    return ctab, stab


def _memory_block(x, mem, W, layer, tag):
    hx = _rms(x, W["xa_norm_x"][layer], name=f"{tag}_xa_norm")
    qx = _mm(hx, [(W["xa_w_q"][layer], 0, 0)], _first, [(D_MODEL, BF16, 0)], tn=D_MODEL, nj=1, name=f"{tag}_xa_q")[0]
    mn = _rms(mem, W["xa_norm_mem"][layer], name=f"{tag}_xa_norm_mem")
    kvm = _mm(mn, [(W["xa_w_kv"][layer], 0, 0)], _first, [(2 * D_MODEL, BF16, 0)], tn=2 * D_MODEL, nj=1,
              name=f"{tag}_xa_kv")[0]
    o = _xattn_fwd(qx, kvm, name=f"{tag}_xa_attn")
    xo = _mm(o, [(W["xa_w_o"][layer], 0, 0)], _add_res, [(D_MODEL, F32, 0)], extras=[(x, 0)], tn=D_MODEL, nj=1,
             name=f"{tag}_xa_out")[0]
    return xo, (x, hx, qx, mn, kvm, o)


def _memory_block_bwd(dxo, mem, W, layer, saved, tag, grads):
    x, hx, qx, mn, kvm, o = saved
    wq, wkv, wo = W["xa_w_q"][layer], W["xa_w_kv"][layer], W["xa_w_o"][layer]
    do = _mm(dxo, [(wo, 0, 0)], _first, [(D_MODEL, BF16, 0)], nt=True, tn=D_MODEL, nj=1, name=f"{tag}_xa_do")[0]
    grads["xa_w_o"][layer] = _owner_major(_mm_tn(o, dxo, name=f"{tag}_xa_dwo"), 0)
    dqx, dkvm = _xattn_bwd(qx, kvm, do, name=f"{tag}_xa_attn_bwd")
    grads["xa_w_q"][layer] = _owner_major(_mm_tn(hx, dqx, name=f"{tag}_xa_dwq"), 0)
    dx, dg = _mm(dqx, [(wq, 0, 0)], _norm_bwd_epilogue(0), [(D_MODEL, F32, 0)], nt=True, extras=[(x, 0), (dxo, 0)],
                 rows=[W["xa_norm_x"][layer].reshape(1, D_MODEL)], sums=[D_MODEL], tn=D_MODEL, nj=1,
                 name=f"{tag}_xa_dhx")
    grads["xa_norm_x"][layer] = dg[0]
    dmn = _mm(dkvm, [(wkv, 0, 0)], _first, [(D_MODEL, F32, 0)], nt=True, tn=D_MODEL, nj=1, name=f"{tag}_xa_dmn")[0]
    grads["xa_w_kv"][layer] = _mm_tn_owners(mn, [dkvm], name=f"{tag}_xa_dwkv")
    _, dgm = _rms_bwd(mem, W["xa_norm_mem"][layer], dmn, name=f"{tag}_xa_norm_mem_bwd")
    grads["xa_norm_mem"][layer] = dgm[0]
    return dx


FF_TN = D_FF // 2

def _silu_mul(accs, extras):
    g, u = accs
    return [g * _sigmoid(g) * u, g, u]


def _silu_mul_bwd(accs, extras):
    da = accs[0]
    g, u = extras[0].astype(F32), extras[1].astype(F32)
    sg = _sigmoid(g)
    return [da * u * sg * (1.0 + g * (1.0 - sg)), da * g * sg]


def _ffn_block(x, W, layer, tag):
    hf = _rms(x, W["ffn_norm"][layer], name=f"{tag}_ffn_norm")
    wgu, wd = W["ffn_w_gate_up"][layer], W["ffn_w_down"][layer]
    act, g, u = _mm(hf, [(wgu, 0, 0), (wgu, 0, 2)], _silu_mul, [(D_FF, BF16, 0)] * 3, tn=FF_TN, nj=2,
                    name=f"{tag}_ffn_up")
    xo = _mm(act, [(wd, 0, 0)], _add_res, [(D_MODEL, F32, 0)], extras=[(x, 0)], tn=D_MODEL, nj=1,
             name=f"{tag}_ffn_down")[0]
    return xo, (x, hf, act, g, u)


def _ffn_block_bwd(dxo, W, layer, saved, tag, grads):
    x, hf, act, g, u = saved
    wgu, wd = W["ffn_w_gate_up"][layer], W["ffn_w_down"][layer]
    dg, du = _mm(dxo, [(wd, 0, 0)], _silu_mul_bwd, [(D_FF, BF16, 0)] * 2, nt=True, extras=[(g, 0), (u, 0)], tn=FF_TN,
                 nj=2, name=f"{tag}_ffn_dact")
    grads["ffn_w_down"][layer] = _owner_major(_mm_tn(act, dxo, tk=FF_TN, name=f"{tag}_ffn_dwd"), 0)
    dhf = _mm(dg, [(wgu, 0, 0)], _first, [(D_MODEL, F32, 0)], nt=True, tn=D_MODEL, nj=1, name=f"{tag}_ffn_dhf_g")[0]
    dx, dgn = _mm(du, [(wgu, 0, 1)], _norm_bwd_epilogue(1), [(D_MODEL, F32, 0)], nt=True,
                  extras=[(dhf, 0), (x, 0), (dxo, 0)], rows=[W["ffn_norm"][layer].reshape(1, D_MODEL)],
                  sums=[D_MODEL], tn=D_MODEL, nj=1, name=f"{tag}_ffn_dhf_u")
    grads["ffn_w_gate_up"][layer] = _mm_tn_owners(hf, [dg, du], name=f"{tag}_ffn_dwgu")
    grads["ffn_norm"][layer] = dgn[0]
    return dx


def _keys_and_values(accs, extras):
    k, v = accs
    lane = lax.broadcasted_iota(jnp.int32, v.shape, 1)
    return [k, jnp.where(lane % HEAD_PAD == V_HEAD, 1.0, v)]


def _even_block(x, tabs, W, tag):
    ctab, stab = tabs
    w_in = W["ev_w_in"][0]
    zero = jnp.zeros((D_MODEL, QK_NOPE), BF16)
    w_in_pad = jnp.concatenate([w_in[:, :896], zero, w_in[:, 896:], zero[:, :HEAD_PAD - QK_DIM]], axis=1)
    w_q_pad = _pad_heads(W["ev_w_q_up"][0], QK_DIM)
    wkv = W["ev_w_kv_up"][0].reshape(KV_RANK, MLA_HEADS, QK_NOPE + V_HEAD)
    w_kv_pad = jnp.concatenate([_pad_heads(wkv[:, :, :QK_NOPE].reshape(KV_RANK, -1), QK_NOPE),
                                _pad_heads(wkv[:, :, QK_NOPE:].reshape(KV_RANK, -1), V_HEAD)], axis=1)
    w_out = W["ev_w_out"][0]
    w_att = jnp.pad(w_out[POOL_DIM:].reshape(MLA_HEADS, V_HEAD, D_MODEL), ((0, 0), (0, HEAD_PAD - V_HEAD), (0, 0)))
    w_out_pad = jnp.concatenate([w_out[:POOL_DIM], w_att.reshape(MLA_HEADS * HEAD_PAD, D_MODEL)], axis=0)
    pool_w = W["ev_pool_w"][0].astype(BF16)
    pool_scale = W["ev_pool_scale"]

    h = _rms(x, W["ev_norm"][0], name=f"{tag}_norm")
    z = _mm(h, [(w_in_pad, 0, 0)], _first, [(D_MODEL, F32, 0)], tn=D_MODEL, nj=1, name=f"{tag}_in")[0]
    mix, pooled = _pool_fwd(z, pool_w, pool_scale, name=f"{tag}_pool")
    cqn = _rms(z, W["ev_q_norm"][0], cb=2, w=Q_RANK, name=f"{tag}_q_norm")
    ckvn = _rms(z, W["ev_kv_norm"][0], cb=6, w=KV_RANK, name=f"{tag}_kv_norm")
    q_pad = _mm(cqn, [(w_q_pad, 0, 0)], _first, [(D_MODEL, F32, 0)], tn=D_MODEL, nj=1, name=f"{tag}_q_up")[0]
    k_pad, v_pad = _mm(ckvn, [(w_kv_pad, 0, 0), (w_kv_pad, 0, 1)], _keys_and_values,
                       [(D_MODEL, F32, 0), (D_MODEL, BF16, 0)], tn=D_MODEL, nj=1, name=f"{tag}_kv_up")
    q_rot, k_cat = _rope_fwd(q_pad, k_pad, z, ctab, stab, name=f"{tag}_rope")
    mix, lse = _flash_fwd(q_rot, k_cat, v_pad, mix, name=f"{tag}_attn")
    xo = _mm(mix, [(w_out_pad, 0, 0)], _add_res, [(D_MODEL, F32, 0)], extras=[(x, 0)], tn=D_MODEL, nj=1,
             name=f"{tag}_out")[0]
    saved = (x, h, z, pooled, cqn, ckvn, q_rot, k_cat, v_pad, lse, mix,
             (w_in_pad, w_q_pad, w_kv_pad, w_out_pad, pool_w, pool_scale))
    return xo, saved


def _even_block_bwd(dxo, tabs, W, saved, tag, grads, token=None):
    ctab, stab = tabs
    x, h, z, pooled, cqn, ckvn, q_rot, k_cat, v_pad, lse, mix, wts = saved
    w_in_pad, w_q_pad, w_kv_pad, w_out_pad, pool_w, pool_scale = wts
    if token is not None:
        w_out_pad = w_out_pad + token[0:1, 0:1].astype(BF16)
    dmix = _mm(dxo, [(w_out_pad, 0, 0)], _first, [(MIX_DIM, BF16, 0)], nt=True, tn=MIX_DIM, nj=1,
               name=f"{tag}_dmix")[0]
    dw_out_pad = _mm_tn(mix, dxo, tk=MIX_DIM // 3, name=f"{tag}_dw_out")
    datt = dw_out_pad[POOL_DIM:].reshape(MLA_HEADS, HEAD_PAD, D_MODEL)[:, :V_HEAD].reshape(-1, D_MODEL)
    grads["ev_w_out"] = [_owner_major(jnp.concatenate([dw_out_pad[:POOL_DIM], datt], axis=0), 0)]
    delta = _attn_delta(dmix, mix, name=f"{tag}_delta")
    dq_rot, dk_cat, dv_pad = _flash_bwd(q_rot, k_cat, v_pad, dmix, _retile_rows(lse, delta.shape[2]), delta,
                                        name=f"{tag}_attn_bwd")
    dq_pad, dkr = _rope_bwd(dq_rot, dk_cat, ctab, stab, name=f"{tag}_rope_bwd")
    dw_q_pad = _mm_tn(cqn, dq_pad, name=f"{tag}_dw_q_up")
    grads["ev_w_q_up"] = [_owner_major(_unpad_heads(dw_q_pad, QK_DIM), 1)]
    dcqn = _mm(dq_pad, [(w_q_pad, 0, 0)], _first, [(Q_RANK, F32, 0)], nt=True, tn=Q_RANK, nj=1, name=f"{tag}_dcqn")[0]
    dwk = _unpad_heads(_mm_tn(ckvn, dk_cat, name=f"{tag}_dw_k_up"), QK_NOPE).reshape(KV_RANK, MLA_HEADS, QK_NOPE)
    dwv = _unpad_heads(_mm_tn(ckvn, dv_pad, name=f"{tag}_dw_v_up"), V_HEAD).reshape(KV_RANK, MLA_HEADS, V_HEAD)
    grads["ev_w_kv_up"] = [_owner_major(jnp.concatenate([dwk, dwv], axis=2).reshape(KV_RANK, -1), 1)]
    dckvn = _mm(dk_cat, [(w_kv_pad, 0, 0)], _first, [(KV_RANK, F32, 0)], nt=True, tn=KV_RANK, nj=1,
                name=f"{tag}_dckvn_k")[0]
    dckvn = _mm(dv_pad, [(w_kv_pad, 0, 1)], _add_res, [(KV_RANK, F32, 0)], nt=True, extras=[(dckvn, 0)], tn=KV_RANK,
                nj=1, name=f"{tag}_dckvn_v")[0]
    dcq, dgq = _rms_bwd(z, W["ev_q_norm"][0], dcqn, cb=2, w=Q_RANK, out_dtype=BF16, name=f"{tag}_q_norm_bwd")
    dckv, dgkv = _rms_bwd(z, W["ev_kv_norm"][0], dckvn, cb=6, w=KV_RANK, out_dtype=BF16, name=f"{tag}_kv_norm_bwd")
    grads["ev_q_norm"], grads["ev_kv_norm"] = dgq, dgkv
    du, dypre, dscale = _pool_bwd(dmix, pooled, pool_w, pool_scale, name=f"{tag}_pool_bwd")
    grads["ev_pool_scale"] = dscale
    grads["ev_pool_w"] = _mm_tn_grouped(pooled, dypre, 4, POOL_GROUP, name=f"{tag}_dpool_w")[None]
    dz = jnp.concatenate([du, dcq, dckv, dkr], axis=1)
    dw_in_pad = _mm_tn(h, dz, name=f"{tag}_dw_in")
    grads["ev_w_in"] = [_owner_major(jnp.concatenate([dw_in_pad[:, :896], dw_in_pad[:, 960:992]], axis=1), 0)]
    dx, dgn = _mm(dz, [(w_in_pad, 0, 0)], _norm_bwd_epilogue(0), [(D_MODEL, F32, 0)], nt=True,
                  extras=[(x, 0), (dxo, 0)], rows=[W["ev_norm"][0].reshape(1, D_MODEL)], sums=[D_MODEL], tn=D_MODEL,
                  nj=1, name=f"{tag}_dh")
    grads["ev_norm"] = dgn
    return dx


def _odd_block(x, reset, W, tag):
    h = _rms(x, W["od_norm"][0], name=f"{tag}_norm")
    z = _mm(h, [(W["od_w_in"][0], 0, 0)], _first, [(2 * D_MODEL, F32, 0)], tn=D_MODEL, nj=2, name=f"{tag}_in")[0]
    w_r, w_i = W["od_w_rgate"][0], W["od_w_igate"][0]
    vecs = [W[n].reshape(1, D_MODEL) for n in ("od_conv_b", "od_b_rgate", "od_b_igate", "od_lambda")]
    xc, hseq, y = _lru_fwd(z, reset, W["od_conv_w"][0], vecs[0], w_r, vecs[1], w_i, vecs[2], vecs[3],
                           name=f"{tag}_lru")
    xo = _mm(y, [(W["od_w_out"][0], 0, 0)], _add_res, [(D_MODEL, F32, 0)], extras=[(x, 0)], tn=D_MODEL, nj=1,
             name=f"{tag}_out")[0]
    return xo, (x, h, z, xc, hseq, y, vecs)


def _odd_block_bwd(dxo, reset, W, saved, tag, grads):
    x, h, z, xc, hseq, y, vecs = saved
    w_r, w_i = W["od_w_rgate"][0], W["od_w_igate"][0]
    dy = _mm(dxo, [(W["od_w_out"][0], 0, 0)], _first, [(D_MODEL, F32, 0)], nt=True, tn=D_MODEL, nj=1,
             name=f"{tag}_dy")[0]
    grads["od_w_out"] = [_owner_major(_mm_tn(y, dxo, name=f"{tag}_dw_out"), 0)]
    dgate, dxc, dpr, dpi, acc = _lru_bwd(dy, z, xc, hseq, reset, w_r, vecs[1], w_i, vecs[2], vecs[3],
                                         name=f"{tag}_lru_bwd")
    grads["od_b_rgate"], grads["od_b_igate"], grads["od_lambda"] = acc[0:1], acc[1:2], acc[2:3]
    grads["od_w_rgate"] = [_owner_major(_mm_tn_grouped(xc, dpr, LRU_HEADS, LRU_HEAD_DIM, name=f"{tag}_dw_rgate"), 1)]
    grads["od_w_igate"] = [_owner_major(_mm_tn_grouped(xc, dpi, LRU_HEADS, LRU_HEAD_DIM, name=f"{tag}_dw_igate"), 1)]
    dxb, cacc = _conv_bwd(dxc, z, W["od_conv_w"][0], name=f"{tag}_conv_bwd")
    grads["od_conv_w"], grads["od_conv_b"] = cacc[None, 0:4], cacc[4:5]
    dz = jnp.concatenate([dgate, dxb], axis=1)
    grads["od_w_in"] = [_mm_tn_owners(h, [dz], name=f"{tag}_dw_in")]
    dx, dgn = _mm(dz, [(W["od_w_in"][0], 0, 0)], _norm_bwd_epilogue(0), [(D_MODEL, F32, 0)], nt=True,
                  extras=[(x, 0), (dxo, 0)], rows=[W["od_norm"][0].reshape(1, D_MODEL)], sums=[D_MODEL], tn=D_MODEL,
                  nj=1, name=f"{tag}_dh")
    grads["od_norm"] = dgn
    return dx


def _local_step(x, mem, positions, target, W, later_weights=None, exchange_earlier=None):
    tabs = _rope_tables(positions)
    reset = (positions == 0).astype(F32)[:, None]
    grads = {n: [None, None] for n in ("xa_norm_x", "xa_norm_mem", "xa_w_q", "xa_w_kv", "xa_w_o", "ffn_norm",
                                       "ffn_w_gate_up", "ffn_w_down")}
    x1, s_even = _even_block(x, tabs, W, "l0_even")
    if later_weights is not None:
        W = {**W, **later_weights(x1)}
    x2, s_xa0 = _memory_block(x1, mem, W, 0, "l0")
    x3, s_ff0 = _ffn_block(x2, W, 0, "l0")
    x4, s_odd = _odd_block(x3, reset, W, "l1_odd")
    x5, s_xa1 = _memory_block(x4, mem, W, 1, "l1")
    x6, s_ff1 = _ffn_block(x5, W, 1, "l1")
    d, dgf, loss = _loss_head(x6, W["final_norm"], target, name="loss_head")
    grads["final_norm"] = dgf[0]
    d = _ffn_block_bwd(d, W, 1, s_ff1, "l1", grads)
    d = _memory_block_bwd(d, mem, W, 1, s_xa1, "l1", grads)
    d = _odd_block_bwd(d, reset, W, s_odd, "l1_odd", grads)
    d = _ffn_block_bwd(d, W, 0, s_ff0, "l0", grads)
    d = _memory_block_bwd(d, mem, W, 0, s_xa0, "l0", grads)
    token = exchange_earlier(grads) if exchange_earlier is not None else None
    d = _even_block_bwd(d, tabs, W, s_even, "l0_even", grads, token)
    big = {n: grads.pop(n) for n in MATMUL_WEIGHTS}
    for n, v in grads.items():
        if isinstance(v, list):
            grads[n] = jnp.stack(v)
    return loss[0, 0], d, big, grads


WEIGHTS = ("ev_norm", "ev_w_in", "ev_pool_w", "ev_pool_scale", "ev_q_norm", "ev_w_q_up", "ev_kv_norm", "ev_w_kv_up",
           "ev_w_out", "od_norm", "od_w_in", "od_conv_w", "od_conv_b", "od_w_rgate", "od_b_rgate", "od_w_igate",
           "od_b_igate", "od_lambda", "od_w_out", "xa_norm_x", "xa_norm_mem", "xa_w_q", "xa_w_kv", "xa_w_o",
           "ffn_norm", "ffn_w_gate_up", "ffn_w_down", "final_norm")
SHARD_AXIS = {"ev_w_in": 1, "ev_w_q_up": 2, "ev_w_kv_up": 2, "ev_w_out": 1, "od_norm": 1, "od_w_in": 2,
              "od_conv_w": 2, "od_conv_b": 1, "od_w_rgate": 2, "od_b_rgate": 1, "od_w_igate": 2, "od_b_igate": 1,
              "od_lambda": 1, "od_w_out": 1, "xa_w_q": 1, "xa_w_kv": 2, "xa_w_o": 1, "ffn_w_gate_up": 2,
              "ffn_w_down": 1}
MATMUL_WEIGHTS = ("ev_w_in", "ev_w_q_up", "ev_w_kv_up", "ev_w_out", "od_w_in", "od_w_rgate", "od_w_igate",
                  "od_w_out", "xa_w_q", "xa_w_kv", "xa_w_o", "ffn_w_gate_up", "ffn_w_down")
SMALL_SHARDED = tuple(n for n in WEIGHTS if n in SHARD_AXIS and n not in MATMUL_WEIGHTS)
REPLICATED = tuple(n for n in WEIGHTS if n not in SHARD_AXIS)


def _pack(parts, quantum):
    flat = jnp.concatenate([p.reshape(-1) for p in parts])
    pad = (-flat.shape[0]) % quantum
    return jnp.pad(flat, (0, pad)).reshape(-1, LANES)


def _unpack(flat, shapes):
    out, off = [], 0
    for shape in shapes:
        size = math.prod(shape)
        out.append(flat[off:off + size].reshape(shape))
        off += size
    return out


def _run_copies(local, remote, send_sems, recv_sems, local_sems):
    locals_ = [pltpu.make_async_copy(src, dst, local_sems.at[n]) for n, (src, dst) in enumerate(local)]
    for cp in locals_:
        cp.start()
    sends = [pltpu.make_async_remote_copy(src_ref=src, dst_ref=dst, send_sem=send_sems.at[k, n],
                                          recv_sem=recv_sems.at[k, n], device_id=dev, device_id_type=MESH)
             for (k, n, src, dst, _, dev) in remote]
    for cp in sends:
        cp.start()
    for (k, n, src, _, arrival, dev) in remote:
        pltpu.make_async_remote_copy(src_ref=src, dst_ref=arrival, send_sem=send_sems.at[k, n],
                                     recv_sem=recv_sems.at[k, n], device_id=dev, device_id_type=MESH).wait_recv()
    for cp in sends:
        cp.wait_send()
    for cp in locals_:
        cp.wait()


def _chip_peers(x, y):
    return [(1 - x, y), (x, 1 - y), (1 - x, 1 - y)]


def _owner_block(ref, axis, q):
    size = ref.shape[axis] // N_CHIPS
    idx = [slice(None)] * len(ref.shape)
    idx[axis] = pl.ds(q * size, size)
    return ref.at[tuple(idx)]


def _comm_call(body, ins, out_shapes, n_items, n_peers, *, name):
    return pl.pallas_call(
        body, in_specs=[ANY] * len(ins), out_specs=[ANY] * len(out_shapes), out_shape=out_shapes,
        scratch_shapes=[pltpu.SemaphoreType.DMA((n_peers, n_items)), pltpu.SemaphoreType.DMA((n_peers, n_items)),
                        pltpu.SemaphoreType.DMA((n_items,))],
        name=name,
    )(*ins)


def _gather_chips(shards, axes, *, name):
    n = len(shards)
    full = [jax.ShapeDtypeStruct(tuple(d * (N_CHIPS if a == ax else 1) for a, d in enumerate(s.shape)), s.dtype)
            for s, ax in zip(shards, axes)]

    def body(*refs):
        srcs, dsts = refs[:n], refs[n:2 * n]
        x, y, c = lax.axis_index("x"), lax.axis_index("y"), lax.axis_index("c")
        me = 2 * x + y
        local = [(srcs[i], _owner_block(dsts[i], axes[i], me)) for i in range(n)]
        remote = [(k, i, srcs[i], _owner_block(dsts[i], axes[i], me), _owner_block(dsts[i], axes[i], 2 * px + py),
                   (px, py, c))
                  for k, (px, py) in enumerate(_chip_peers(x, y)) for i in range(n)]
        _run_copies(local, remote, *refs[2 * n:])

    return _comm_call(body, shards, full, n, 3, name=name)


HBM = pl.BlockSpec(memory_space=pltpu.HBM)
SEM = pl.BlockSpec(memory_space=pltpu.SEMAPHORE)
DATAFLOW = pltpu.SideEffectType.DATAFLOW_SIDE_EFFECTING


def _gather_plan(axes):
    return lambda srcs, lands, me, peer: [
        (srcs[i], _owner_block(lands[i], ax, me), _owner_block(lands[i], ax, peer)) for i, ax in enumerate(axes)]


def _exchange_plan(where):
    return lambda srcs, lands, me, peer: [
        (srcs[i].at[peer], lands[n].at[me, l], lands[n].at[peer, l]) for i, (n, l) in enumerate(where)]


def _split_start(srcs, lands, plan, *, name):
    ns, nl = len(srcs), len(lands)
    nsem = 3 * len(plan(list(srcs), list(lands), 0, 0))

    def body(*refs):
        src_refs, land_refs = refs[:ns], refs[ns:ns + nl]
        send_sems, recv_sems = refs[ns + nl:ns + nl + nsem], refs[ns + nl + nsem:ns + nl + 2 * nsem]
        x, y, c = lax.axis_index("x"), lax.axis_index("y"), lax.axis_index("c")
        n = 0
        for px, py in _chip_peers(x, y):
            for src, dst, _ in plan(src_refs, land_refs, 2 * x + y, 2 * px + py):
                pltpu.make_async_remote_copy(src_ref=src, dst_ref=dst, send_sem=send_sems[n], recv_sem=recv_sems[n],
                                             device_id=(px, py, c), device_id_type=MESH).start()
                n += 1
        refs[-1][...] = jnp.zeros_like(refs[-1])

    arrays = list(srcs) + list(lands)
    out = pl.pallas_call(
        body, name=name, in_specs=[HBM] * (ns + nl),
        out_specs=[SEM] * (2 * nsem) + [HBM] * (ns + nl) + [pl.BlockSpec(memory_space=pltpu.VMEM)],
        out_shape=[pltpu.SemaphoreType.DMA(())] * (2 * nsem) + [pltpu.HBM(a.shape, a.dtype) for a in arrays]
        + [jax.ShapeDtypeStruct((8, LANES), F32)],
        input_output_aliases={i: 2 * nsem + i for i in range(ns + nl)},
        compiler_params=pltpu.CompilerParams(has_side_effects=DATAFLOW),
    )(*[pltpu.with_memory_space_constraint(a, pltpu.HBM) for a in arrays])
    sems, rest = out[:2 * nsem], out[2 * nsem:]
    return sems[:nsem], sems[nsem:], rest[:ns], rest[ns:ns + nl], rest[-1]


def _split_wait(handle, after, plan, *, name):
    send_sems, recv_sems, srcs, lands, _ = handle
    ns, nl, nsem = len(srcs), len(lands), len(send_sems)

    def body(*refs):
        src_refs, land_refs = refs[:ns], refs[ns:ns + nl]
        send_refs, recv_refs = refs[ns + nl:ns + nl + nsem], refs[ns + nl + nsem:ns + nl + 2 * nsem]
        x, y, c = lax.axis_index("x"), lax.axis_index("y"), lax.axis_index("c")
        n = 0
        for px, py in _chip_peers(x, y):
            for src, _, arrival in plan(src_refs, land_refs, 2 * x + y, 2 * px + py):
                cp = pltpu.make_async_remote_copy(src_ref=src, dst_ref=arrival, send_sem=send_refs[n],
                                                  recv_sem=recv_refs[n], device_id=(px, py, c), device_id_type=MESH)
                cp.wait_send()
                cp.wait_recv()
                n += 1

    out = pl.pallas_call(
        body, name=name, in_specs=[HBM] * (ns + nl) + [SEM] * (2 * nsem) + [ANY], out_specs=[HBM] * (ns + nl),
        out_shape=[pltpu.HBM(a.shape, a.dtype) for a in list(srcs) + list(lands)],
        input_output_aliases={i: i for i in range(ns + nl)},
        compiler_params=pltpu.CompilerParams(has_side_effects=DATAFLOW),
    )(*srcs, *lands, *send_sems, *recv_sems, after)
    return out[ns:]


def _exchange_sibling(arrays, *, name):
    n = len(arrays)

    def body(*refs):
        x, y, c = lax.axis_index("x"), lax.axis_index("y"), lax.axis_index("c")
        remote = [(0, i, refs[i], refs[n + i], refs[n + i], (x, y, 1 - c)) for i in range(n)]
        _run_copies([], remote, *refs[2 * n:])

    return _comm_call(body, arrays, [jax.ShapeDtypeStruct(a.shape, a.dtype) for a in arrays], n, 1, name=name)


def _sum_slots(r, *, token=None, name):
    shape = r.shape[1:]
    cols = shape[-1]
    rows = math.prod(shape) // cols
    tr = max(d for d in range(8, rows + 1, 8) if rows % d == 0 and d * cols * 16 <= (4 << 20))

    def body(r_ref, *refs):
        total = ((r_ref[0] + r_ref[1]) + r_ref[2]) + r_ref[3]
        refs[-1][...] = total if token is None else total + refs[0][0:1, 0:1]

    in_specs = [pl.BlockSpec((N_CHIPS, tr, cols), lambda i: (0, i, 0))]
    in_specs += [] if token is None else [_const((8, LANES))]
    return pl.pallas_call(
        body, grid=(rows // tr,), in_specs=in_specs,
        out_specs=_rows(tr, cols), out_shape=jax.ShapeDtypeStruct((rows, cols), F32), compiler_params=_cp(1),
        name=name,
    )(r.reshape(N_CHIPS, rows, cols), *([] if token is None else [token])).reshape(shape)


FIRST_WEIGHTS = ("ev_w_in", "ev_w_q_up", "ev_w_kv_up", "ev_w_out")
LATER_WEIGHTS = tuple(n for n in MATMUL_WEIGHTS if n not in FIRST_WEIGHTS)
LAST_GRADS = FIRST_WEIGHTS
EARLIER_GRADS = tuple(n for n in MATMUL_WEIGHTS if n not in LAST_GRADS)


def _my_chip():
    return 2 * lax.axis_index("x") + lax.axis_index("y")


def _gather_first(w):
    small = _pack([w[n] for n in SMALL_SHARDED], 8 * LANES)
    stacked = [n for n in FIRST_WEIGHTS if SHARD_AXIS[n] == w[n].ndim - 1 and w[n].shape[-1] % LANES]
    shards = [w[n].astype(BF16)[None] if n in stacked else w[n].astype(BF16) for n in FIRST_WEIGHTS]
    got = _gather_chips(shards + [small], [0 if n in stacked else SHARD_AXIS[n] for n in FIRST_WEIGHTS] + [0],
                        name="gather_first")
    full = {n: w[n] for n in REPLICATED}
    for n, g in zip(FIRST_WEIGHTS, got[:-1]):
        full[n] = jnp.concatenate([g[q] for q in range(N_CHIPS)], axis=SHARD_AXIS[n]) if n in stacked else g
    per_chip = [_unpack(got[-1][q * small.shape[0]:(q + 1) * small.shape[0]].reshape(-1),
                        [w[n].shape for n in SMALL_SHARDED]) for q in range(N_CHIPS)]
    for i, n in enumerate(SMALL_SHARDED):
        full[n] = jnp.concatenate([per_chip[q][i] for q in range(N_CHIPS)], axis=SHARD_AXIS[n])
    return full


def _gather_later_start(w):
    shards = [w[n].astype(BF16) for n in LATER_WEIGHTS]
    axes = [SHARD_AXIS[n] for n in LATER_WEIGHTS]
    lands = []
    for s, ax in zip(shards, axes):
        shape = tuple(d * (N_CHIPS if a == ax else 1) for a, d in enumerate(s.shape))
        lands.append(lax.dynamic_update_slice_in_dim(lax.empty(shape, s.dtype), s, _my_chip() * s.shape[ax], ax))
    return _split_start(shards, lands, _gather_plan(axes), name="gather_later_start"), _gather_plan(axes)


def _owner_major(g, axis):
    shape = g.shape
    size = shape[axis] // N_CHIPS
    g = jnp.moveaxis(g.reshape(shape[:axis] + (N_CHIPS, size) + shape[axis + 1:]), axis, 0)
    return g.reshape(N_CHIPS, -1, shape[-1] if axis < len(shape) - 1 else size)


def _exchange_start(items, *, name):
    me = _my_chip()
    srcs, lands, where = [], [], []
    for n, layers in enumerate(items):
        land = lax.empty((N_CHIPS, len(layers)) + layers[0].shape[1:], layers[0].dtype)
        for l, a in enumerate(layers):
            own = lax.dynamic_index_in_dim(a, me, 0, keepdims=True)[:, None]
            land = lax.dynamic_update_slice(land, own, (me, l) + (0,) * (a.ndim - 1))
            srcs.append(a)
            where.append((n, l))
        lands.append(land)
    plan = _exchange_plan(where)
    return _split_start(srcs, lands, plan, name=name), plan


def _earlier_items(grads, full_shapes):
    small = [_pack([jnp.split(grads[n].reshape(full_shapes[n]), N_CHIPS, axis=SHARD_AXIS[n])[q]
                    for n in SMALL_SHARDED], 8 * LANES) for q in range(N_CHIPS)]
    return [grads[n] for n in EARLIER_GRADS] + [[jnp.stack(small)]]


def _last_items(big, grads, full_shapes, loss):
    repl = _pack([grads[n].reshape(full_shapes[n]) for n in REPLICATED] + [loss.reshape(1)], 8 * LANES)
    return [big[n] for n in LAST_GRADS] + [[jnp.stack([repl] * N_CHIPS)]]


def kernel(
        x, mem, positions, ev_norm, ev_w_in, ev_pool_w, ev_pool_scale, ev_q_norm, ev_w_q_up, ev_kv_norm,
        ev_w_kv_up, ev_w_out, od_norm, od_w_in, od_conv_w, od_conv_b, od_w_rgate, od_b_rgate, od_w_igate,
        od_b_igate, od_lambda, od_w_out, xa_norm_x, xa_norm_mem, xa_w_q, xa_w_kv, xa_w_o, ffn_norm,
        ffn_w_gate_up, ffn_w_down, final_norm, loss_target, m_ev_norm, m_ev_w_in, m_ev_pool_w, m_ev_pool_scale,
        m_ev_q_norm, m_ev_w_q_up, m_ev_kv_norm, m_ev_w_kv_up, m_ev_w_out, m_od_norm, m_od_w_in, m_od_conv_w,
        m_od_conv_b, m_od_w_rgate, m_od_b_rgate, m_od_w_igate, m_od_b_igate, m_od_lambda, m_od_w_out,
        m_xa_norm_x, m_xa_norm_mem, m_xa_w_q, m_xa_w_kv, m_xa_w_o, m_ffn_norm, m_ffn_w_gate_up, m_ffn_w_down,
        m_final_norm, v_ev_norm, v_ev_w_in, v_ev_pool_w, v_ev_pool_scale, v_ev_q_norm, v_ev_w_q_up,
        v_ev_kv_norm, v_ev_w_kv_up, v_ev_w_out, v_od_norm, v_od_w_in, v_od_conv_w, v_od_conv_b, v_od_w_rgate,
        v_od_b_rgate, v_od_w_igate, v_od_b_igate, v_od_lambda, v_od_w_out, v_xa_norm_x, v_xa_norm_mem, v_xa_w_q,
        v_xa_w_kv, v_xa_w_o, v_ffn_norm, v_ffn_w_gate_up, v_ffn_w_down, v_final_norm):
    given = dict(locals())
    w = {n: given[n] for n in WEIGHTS}
    full_shapes = {n: tuple(d * (N_CHIPS if a == SHARD_AXIS.get(n) else 1) for a, d in enumerate(w[n].shape))
                   for n in WEIGHTS}
    full = _gather_first(w)
    later, later_plan = _gather_later_start(w)
    full["ev_norm"] = full["ev_norm"] + later[4][0:1, 0:1]
    exchange = {}

    def later_weights(after):
        return dict(zip(LATER_WEIGHTS, _split_wait(later, after, later_plan, name="gather_later_wait")))

    def exchange_earlier(grads):
        exchange["handle"], exchange["plan"] = _exchange_start(_earlier_items(grads, full_shapes),
                                                               name="exchange_earlier_start")
        return exchange["handle"][4]

    loss, grad_x, big, grads = _local_step(x[0], mem[0], positions[0], loss_target[0], full, later_weights,
                                           exchange_earlier)
    earlier = EARLIER_GRADS + ("small",)
    got = dict(zip(earlier, _split_wait(exchange["handle"], grad_x, exchange["plan"], name="exchange_earlier_wait")))
    last, last_plan = _exchange_start(_last_items(big, grads, full_shapes, loss), name="exchange_last_start")
    sums = {n: _sum_slots(got[n], token=last[4] if i == 0 else None, name=f"sum_chips_{n}")
            for i, n in enumerate(earlier)}
    got = dict(zip(LAST_GRADS + ("replicated",),
                   _split_wait(last, sums[earlier[-1]], last_plan, name="exchange_last_wait")))
    sums.update({n: _sum_slots(got[n], name=f"sum_chips_{n}") for n in got})
    mine = [sums[n] for n in MATMUL_WEIGHTS + ("small", "replicated")]
    other = _exchange_sibling(mine, name="exchange_sibling")
    out = {}
    for i, n in enumerate(MATMUL_WEIGHTS):
        out[n] = _adamw(w[n], mine[i].reshape(w[n].shape), other[i].reshape(w[n].shape), given["m_" + n],
                        given["v_" + n], name=f"adamw_{n}")
    for i, group in ((len(MATMUL_WEIGHTS), SMALL_SHARDED), (len(MATMUL_WEIGHTS) + 1, REPLICATED)):
        spare = [jnp.zeros((1,), F32)] if group is REPLICATED else []
        packed = [_pack([given[pre + n] for n in group] + spare, 8 * LANES) for pre in ("", "m_", "v_")]
        res = _adamw(packed[0], mine[i].reshape(packed[0].shape), other[i].reshape(packed[0].shape), packed[1],
                     packed[2], name=f"adamw_group{i}")
        shapes = [w[n].shape for n in group] + [(1,)] * len(spare)
        for j, arrs in enumerate(zip(*[_unpack(r.reshape(-1), shapes) for r in res])):
            if j < len(group):
                out[group[j]] = list(arrs)
            else:
                loss = arrs[0][0]
    return (loss, grad_x[None], *[out[n][k] for k in range(4) for n in WEIGHTS])
```

```python
import functools
import math

import jax
import jax.numpy as jnp
from jax import lax
from jax.experimental import pallas as pl
from jax.experimental.pallas import tpu as pltpu

F32 = jnp.float32
BF16 = jnp.bfloat16

D_MODEL = 1024
POOL_DIM = 512
POOL_WINDOWS = (2, 4, 8, 16)
POOL_GROUP = 128
MLA_HEADS = 8
QK_NOPE = 64
QK_ROPE = 32
QK_DIM = QK_NOPE + QK_ROPE
V_HEAD = 64
HEAD_PAD = 128
Q_RANK = 256
KV_RANK = 128
ROPE_BASE = 10000.0
LRU_HEADS = 4
LRU_HEAD_DIM = 256
CONV_WIDTH = 4
LRU_C = 8.0
MEM_HEADS = 4
MEM_HEAD_DIM = 256
D_FF = 2816
RMS_EPS = 1e-6
NEG_INF = -1e30

ADAM_LR = 0.001
ADAM_B1 = 0.9
ADAM_B2 = 0.999
ADAM_EPS = 1e-08
ADAM_WD = 0.01
ADAM_STEP = 10

N_CHIPS = 4
LANES = 128
VMEM_LIMIT = 56 * 1024 * 1024
MESH = pl.DeviceIdType.MESH
ANY = pl.BlockSpec(memory_space=pl.ANY)
MIX_DIM = POOL_DIM + MLA_HEADS * HEAD_PAD

NN = (((1,), (0,)), ((), ()))
NT = (((1,), (1,)), ((), ()))
TN = (((0,), (0,)), ((), ()))


def _cp(n):
    return pltpu.CompilerParams(dimension_semantics=("arbitrary",) * n, vmem_limit_bytes=VMEM_LIMIT)


def _dot(a, b, dims=NN):
    return lax.dot_general(a, b, dims, preferred_element_type=F32)


def _row_tile(S):
    return 1024 if S % 1024 == 0 else min(S, 512)


def _rows(ts, w, cb=0):
    return pl.BlockSpec((ts, w), lambda i: (i, cb))


def _const(shape):
    return pl.BlockSpec(shape, lambda i: (0,) * len(shape))


MM_VMEM_BUDGET = 40 * 1024 * 1024


def _mm(a, bs, epi, outs, *, tn, nj, nt=False, also=None, extras=(), rows=(), sums=(), a_cb=0, k=None, tm=None,
        name):
    M = a.shape[0]
    k = k or a.shape[1]
    nb, ne, nr, no = len(bs), len(extras), len(rows), len(outs)
    lhs = [(a, k, a_cb, b) for b in bs[:1]] + ([(also[0], also[0].shape[1], 0, also[1])] if also else [])
    if tm is None:
        per_row = 2 * (sum(kk * x.dtype.itemsize for x, kk, _, _ in lhs)
                       + sum(e.dtype.itemsize for e, _ in extras) * tn
                       + sum(jnp.dtype(dt).itemsize for _, dt, _ in outs) * tn) + nb * tn * 4
        weights = (1 if nj == 1 else 2) * (sum(b.dtype.itemsize for b, _, _ in bs) * k
                                           + (also[1][0].dtype.itemsize * lhs[-1][1] if also else 0)) * tn
        tm = 1024 if M % 1024 == 0 and 1024 * per_row + weights <= MM_VMEM_BUDGET else min(M, 512)
    dims = NT if nt else NN
    assert not sums or nj == 1
    na = 2 if also else 0

    def body(*refs):
        av = refs[0][...].astype(BF16)
        accs = [_dot(av, r[...].astype(BF16), dims) for r in refs[1:1 + nb]]
        if also:
            accs[0] = accs[0] + _dot(refs[1 + nb][...].astype(BF16), refs[2 + nb][...].astype(BF16), dims)
        refs = refs[:1 + nb] + refs[1 + nb + na:]
        vals = epi(accs, [r[...] for r in refs[1 + nb:1 + nb + ne + nr]])
        outs_refs = refs[1 + nb + ne + nr:]
        for o, v in zip(outs_refs[:no], vals[:no]):
            o[...] = v.astype(o.dtype)
        if sums:
            @pl.when(pl.program_id(1) == 0)
            def _():
                for o in outs_refs[no:]:
                    o[...] = jnp.zeros_like(o)

            for o, v in zip(outs_refs[no:], vals[no:]):
                o[...] += v

    in_specs = [pl.BlockSpec((tm, k), lambda j, i: (i, a_cb))]
    weights = [(k, rb, cb) for (_, rb, cb) in bs]
    if also:
        in_specs_also = pl.BlockSpec((tm, lhs[-1][1]), lambda j, i: (i, 0))
        weights.append((lhs[-1][1], also[1][1], also[1][2]))
    for n, (kk, rb, cb) in enumerate(weights):
        if also and n == nb:
            in_specs.append(in_specs_also)
        mode = dict(pipeline_mode=pl.Buffered(1)) if nj == 1 else {}
        if nt:
            in_specs.append(pl.BlockSpec((tn, kk), lambda j, i, rb=rb, cb=cb: (rb + j, cb), **mode))
        else:
            in_specs.append(pl.BlockSpec((kk, tn), lambda j, i, rb=rb, cb=cb: (rb, cb + j), **mode))
    for (_, cb) in extras:
        in_specs.append(pl.BlockSpec((tm, tn), lambda j, i, cb=cb: (i, cb + j)))
    in_specs += [pl.BlockSpec((1, tn), lambda j, i: (0, 0))] * nr
    out_specs = [pl.BlockSpec((tm, tn), lambda j, i, cb=cb: (i, cb + j)) for (_, _, cb) in outs]
    out_specs += [pl.BlockSpec((1, w), lambda j, i: (0, 0)) for w in sums]
    res = pl.pallas_call(
        body, grid=(nj, M // tm), in_specs=in_specs, out_specs=out_specs,
        out_shape=[jax.ShapeDtypeStruct((M, n), dt) for (n, dt, _) in outs]
        + [jax.ShapeDtypeStruct((1, w), F32) for w in sums],
        compiler_params=_cp(2), name=name,
    )(a, *[b for (b, _, _) in bs], *([also[0], also[1][0]] if also else []), *[e for (e, _) in extras], *rows)
    return res


def _first(accs, extras):
    return [accs[0]]


def _add_res(accs, extras):
    return [accs[0] + extras[0].astype(F32)]


def _norm_bwd_epilogue(partials):
    def epi(accs, vals):
        dh = accs[0]
        for part in vals[:partials]:
            dh = dh + part.astype(F32)
        x, res, g = vals[partials:partials + 3]
        r = lax.rsqrt(jnp.mean(x * x, axis=-1, keepdims=True) + RMS_EPS)
        n = x * r
        dn = dh * g
        return [r * (dn - n * jnp.mean(dn * n, axis=-1, keepdims=True)) + res, jnp.sum(dh * n, axis=0, keepdims=True)]

    return epi


TN_VMEM_BUDGET = 36 * 1024 * 1024


def _contraction_rows(S, row_bytes, out_elems):
    ts = min(S, 2048)
    while ts > 512 and 2 * (ts * row_bytes + out_elems * 4) > TN_VMEM_BUDGET:
        ts //= 2
    return ts


def _mm_tn(a, b, *, ka=None, a_cb=0, nb=None, b_cb=0, tk=None, tn=None, ts=None, name):
    S = a.shape[0]
    ka = ka or a.shape[1]
    nb = nb or b.shape[1]
    tk = tk or ka
    tn = tn or nb
    ts = ts or _contraction_rows(S, tk * a.dtype.itemsize + tn * b.dtype.itemsize, tk * tn)
    a0, b0 = a_cb * (ka // tk), b_cb * (nb // tn)

    def body(a_ref, b_ref, o_ref):
        @pl.when(pl.program_id(2) == 0)
        def _():
            o_ref[...] = jnp.zeros_like(o_ref)

        o_ref[...] += _dot(a_ref[...].astype(BF16), b_ref[...].astype(BF16), TN)

    return pl.pallas_call(
        body, grid=(ka // tk, nb // tn, S // ts),
        in_specs=[pl.BlockSpec((ts, tk), lambda p, q, s: (s, a0 + p)),
                  pl.BlockSpec((ts, tn), lambda p, q, s: (s, b0 + q))],
        out_specs=pl.BlockSpec((tk, tn), lambda p, q, s: (p, q)),
        out_shape=jax.ShapeDtypeStruct((ka, nb), F32), compiler_params=_cp(3), name=name,
    )(a, b)


def _mm_tn_owners(a, bs, *, name):
    S, ka = a.shape
    nb = sum(b.shape[1] for b in bs)
    tn = nb // N_CHIPS
    ts = _contraction_rows(S, ka * a.dtype.itemsize + len(bs) * tn * bs[0].dtype.itemsize, ka * tn)
    per = N_CHIPS // len(bs)

    def body(a_ref, *refs):
        o_ref = refs[-1]
        q = pl.program_id(0)

        @pl.when(pl.program_id(1) == 0)
        def _():
            o_ref[...] = jnp.zeros_like(o_ref)

        av = a_ref[...].astype(BF16)
        for n, b_ref in enumerate(refs[:-1]):
            @pl.when(q // per == n)
            def _():
                o_ref[0] += _dot(av, b_ref[...].astype(BF16), TN)

    in_specs = [pl.BlockSpec((ts, ka), lambda q, s: (s, 0))]
    for n in range(len(bs)):
        in_specs.append(pl.BlockSpec((ts, tn), lambda q, s, n=n: (jnp.where(q // per == n, s, 0),
                                                                  jnp.clip(q - n * per, 0, per - 1))))
    return pl.pallas_call(
        body, grid=(N_CHIPS, S // ts), in_specs=in_specs,
        out_specs=pl.BlockSpec((1, ka, tn), lambda q, s: (q, 0, 0)),
        out_shape=jax.ShapeDtypeStruct((N_CHIPS, ka, tn), F32), compiler_params=_cp(2), name=name,
    )(a, *bs)


def _mm_tn_grouped(a, b, groups, w, *, name):
    S = a.shape[0]
    ts = _contraction_rows(S, w * (a.dtype.itemsize + b.dtype.itemsize), w * w)

    def body(a_ref, b_ref, o_ref):
        @pl.when(pl.program_id(1) == 0)
        def _():
            o_ref[...] = jnp.zeros_like(o_ref)

        o_ref[0] += _dot(a_ref[...].astype(BF16), b_ref[...].astype(BF16), TN)

    return pl.pallas_call(
        body, grid=(groups, S // ts),
        in_specs=[pl.BlockSpec((ts, w), lambda g, s: (s, g)), pl.BlockSpec((ts, w), lambda g, s: (s, g))],
        out_specs=pl.BlockSpec((1, w, w), lambda g, s: (g, 0, 0)),
        out_shape=jax.ShapeDtypeStruct((groups, w, w), F32), compiler_params=_cp(2), name=name,
    )(a, b)


def _rms(x, g, *, cb=0, w=None, ts=None, name):
    S = x.shape[0]
    w = w or x.shape[1]
    ts = ts or _row_tile(S)

    def body(x_ref, g_ref, o_ref):
        xv = x_ref[...].astype(F32)
        r = lax.rsqrt(jnp.mean(xv * xv, axis=-1, keepdims=True) + RMS_EPS)
        o_ref[...] = (xv * r * g_ref[...]).astype(o_ref.dtype)

    return pl.pallas_call(
        body, grid=(S // ts,), in_specs=[_rows(ts, w, cb), _const((1, w))], out_specs=_rows(ts, w),
        out_shape=jax.ShapeDtypeStruct((S, w), BF16), compiler_params=_cp(1), name=name,
    )(x, g.reshape(1, w))


def _rms_bwd(x, g, dy, *, cb=0, w=None, res=None, out_dtype=F32, ts=None, name):
    S = x.shape[0]
    w = w or x.shape[1]
    ts = ts or min(S, 512)
    has_res = res is not None

    def body(*refs):
        x_ref, g_ref, dy_ref = refs[:3]
        dx_ref, dg_ref = refs[-2:]
        xv = x_ref[...].astype(F32)
        r = lax.rsqrt(jnp.mean(xv * xv, axis=-1, keepdims=True) + RMS_EPS)
        n = xv * r
        dyv = dy_ref[...].astype(F32)
        dn = dyv * g_ref[...]
        dx = r * (dn - n * jnp.mean(dn * n, axis=-1, keepdims=True))
        if has_res:
            dx = dx + refs[3][...].astype(F32)
        dx_ref[...] = dx.astype(dx_ref.dtype)

        @pl.when(pl.program_id(0) == 0)
        def _():
            dg_ref[...] = jnp.zeros_like(dg_ref)

        dg_ref[...] += jnp.sum(dyv * n, axis=0, keepdims=True)

    ins = [x, g.reshape(1, w), dy] + ([res] if has_res else [])
    in_specs = [_rows(ts, w, cb), _const((1, w)), _rows(ts, w)] + ([_rows(ts, w)] if has_res else [])
    return pl.pallas_call(
        body, grid=(S // ts,), in_specs=in_specs, out_specs=[_rows(ts, w), _const((1, w))],
        out_shape=[jax.ShapeDtypeStruct((S, w), out_dtype), jax.ShapeDtypeStruct((1, w), F32)],
        compiler_params=_cp(1), name=name,
    )(*ins)


HALO = 16


def _pool_counts(i, ts, rows, first_row):
    t = i * ts + first_row + lax.broadcasted_iota(jnp.int32, (rows, 1), 0)
    return [jnp.minimum(t + 1, w).astype(F32) for w in POOL_WINDOWS]


def _pool_fwd(z, pool_w, pool_scale, *, name):
    S = z.shape[0]
    ts = min(S, 512)
    nh = ts // HALO

    def body(u_ref, halo_ref, w_ref, sc_ref, y_ref, p_ref):
        i = pl.program_id(0)
        u = u_ref[...]
        halo = jnp.where(i > 0, halo_ref[...], 0.0)
        xe = jnp.concatenate([halo, u], axis=0)
        sums = []
        s = xe
        for sh in (1, 2, 4, 8):
            s = s + pltpu.roll(s, sh, 0)
            sums.append(s)
        cnts = _pool_counts(i, ts, ts, 0)
        for g in range(4):
            lo, hi = g * POOL_GROUP, (g + 1) * POOL_GROUP
            pooled = (sums[g][HALO:, lo:hi] / cnts[g] - u[:, lo:hi]).astype(BF16)
            p_ref[:, lo:hi] = pooled
            y_ref[:, lo:hi] = (_dot(pooled, w_ref[g]) * sc_ref[:, lo:hi]).astype(y_ref.dtype)

    return pl.pallas_call(
        body, grid=(S // ts,),
        in_specs=[_rows(ts, POOL_DIM), pl.BlockSpec((HALO, POOL_DIM), lambda i: (jnp.maximum(i * nh - 1, 0), 0)),
                  _const((4, POOL_GROUP, POOL_GROUP)), _const((1, POOL_DIM))],
        out_specs=[_rows(ts, POOL_DIM), _rows(ts, POOL_DIM)],
        out_shape=[jax.ShapeDtypeStruct((S, MIX_DIM), BF16), jax.ShapeDtypeStruct((S, POOL_DIM), BF16)],
        compiler_params=_cp(1), name=name,
    )(z, z, pool_w, pool_scale)


def _pool_bwd(dmix, pooled, pool_w, pool_scale, *, name):
    S = dmix.shape[0]
    ts = min(S, 512)
    nh = ts // HALO
    last = S // HALO - 1

    def body(dy_ref, dyh_ref, p_ref, w_ref, sc_ref, du_ref, dyp_ref, dsc_ref):
        i = pl.program_id(0)
        dyv = dy_ref[...].astype(F32)
        dyh = jnp.where(i < pl.num_programs(0) - 1, dyh_ref[...].astype(F32), 0.0)
        dye = jnp.concatenate([dyv, dyh], axis=0) * sc_ref[...]
        dypre = dye.astype(BF16)
        dyp_ref[...] = dypre[:ts]
        cnts = _pool_counts(i, ts, ts + HALO, 0)
        n = ts + HALO
        dsc = []
        for g in range(4):
            lo, hi = g * POOL_GROUP, (g + 1) * POOL_GROUP
            ypre = _dot(p_ref[:, lo:hi], w_ref[g])
            dsc.append(jnp.sum(dyv[:, lo:hi] * ypre, axis=0, keepdims=True))
            dpool = _dot(dypre[:, lo:hi], w_ref[g], NT)
            s = dpool / cnts[g]
            for sh in (1, 2, 4, 8)[:g + 1]:
                s = s + pltpu.roll(s, n - sh, 0)
            du_ref[:, lo:hi] = (s[:ts] - dpool[:ts]).astype(du_ref.dtype)

        @pl.when(i == 0)
        def _():
            dsc_ref[...] = jnp.zeros_like(dsc_ref)

        dsc_ref[...] += jnp.concatenate(dsc, axis=1)

    return pl.pallas_call(
        body, grid=(S // ts,),
        in_specs=[_rows(ts, POOL_DIM),
                  pl.BlockSpec((HALO, POOL_DIM), lambda i: (jnp.minimum((i + 1) * nh, last), 0)),
                  _rows(ts, POOL_DIM), _const((4, POOL_GROUP, POOL_GROUP)), _const((1, POOL_DIM))],
        out_specs=[_rows(ts, POOL_DIM), _rows(ts, POOL_DIM), _const((1, POOL_DIM))],
        out_shape=[jax.ShapeDtypeStruct((S, POOL_DIM), BF16)] * 2 + [jax.ShapeDtypeStruct((1, POOL_DIM), F32)],
        compiler_params=_cp(1), name=name,
    )(dmix, dmix, pooled, pool_w, pool_scale)


def _rope_partner(t):
    lane = lax.broadcasted_iota(jnp.int32, t.shape, 1)
    swapped = jnp.where(lane < QK_NOPE + QK_ROPE // 2, pltpu.roll(t, HEAD_PAD - QK_ROPE // 2, 1),
                        pltpu.roll(t, QK_ROPE // 2, 1))
    return jnp.where((lane >= QK_NOPE) & (lane < QK_DIM), swapped, 0.0)


def _rope_fwd(q_pad, k_pad, z, ctab, stab, *, name):
    S = q_pad.shape[0]
    ts = _row_tile(S)

    def body(q_ref, k_ref, kr_ref, c_ref, s_ref, qo_ref, ko_ref):
        c, s = c_ref[...], s_ref[...]
        kr = kr_ref[...]
        kr_rot = kr * c + _rope_partner(kr) * s
        for h in range(MLA_HEADS):
            lo, hi = h * HEAD_PAD, (h + 1) * HEAD_PAD
            q = q_ref[:, lo:hi]
            qo_ref[:, lo:hi] = (q * c + _rope_partner(q) * s).astype(qo_ref.dtype)
            ko_ref[:, lo:hi] = (k_ref[:, lo:hi] + kr_rot).astype(ko_ref.dtype)

    wide = _rows(ts, MLA_HEADS * HEAD_PAD)
    return pl.pallas_call(
        body, grid=(S // ts,),
        in_specs=[wide, wide, _rows(ts, HEAD_PAD, 7), _rows(ts, HEAD_PAD), _rows(ts, HEAD_PAD)],
        out_specs=[wide, wide], out_shape=[jax.ShapeDtypeStruct((S, MLA_HEADS * HEAD_PAD), BF16)] * 2,
        compiler_params=_cp(1), name=name,
    )(q_pad, k_pad, z, ctab, stab)


def _rope_bwd(dq_rot, dk_cat, ctab, stab, *, name):
    S = dq_rot.shape[0]
    ts = min(S, 512)

    def body(dq_ref, dk_ref, c_ref, s_ref, dqo_ref, dkr_ref):
        c, s = c_ref[...], s_ref[...]
        for h in range(MLA_HEADS):
            g = dq_ref[:, h * HEAD_PAD:(h + 1) * HEAD_PAD]
            dqo_ref[:, h * HEAD_PAD:(h + 1) * HEAD_PAD] = (g * c + _rope_partner(g * s)).astype(dqo_ref.dtype)
        dk = dk_ref[...]
        g = dk[:, :HEAD_PAD]
        for h in range(1, MLA_HEADS):
            g = g + dk[:, h * HEAD_PAD:(h + 1) * HEAD_PAD]
        lane = lax.broadcasted_iota(jnp.int32, g.shape, 1)
        on_rope = (lane >= QK_NOPE) & (lane < QK_DIM)
        dkr_ref[...] = jnp.where(on_rope, g * c + _rope_partner(g * s), 0.0).astype(dkr_ref.dtype)

    wide = _rows(ts, MLA_HEADS * HEAD_PAD)
    return pl.pallas_call(
        body, grid=(S // ts,), in_specs=[wide, wide, _rows(ts, HEAD_PAD), _rows(ts, HEAD_PAD)],
        out_specs=[wide, _rows(ts, HEAD_PAD)],
        out_shape=[jax.ShapeDtypeStruct((S, MLA_HEADS * HEAD_PAD), BF16), jax.ShapeDtypeStruct((S, HEAD_PAD), BF16)],
        compiler_params=_cp(1), name=name,
    )(dq_rot, dk_cat, ctab, stab)


ATT_SCALE = QK_DIM ** -0.5
LOG2E = math.log2(math.e)


HEADS_PER_STEP = 2
ATT_COL0 = POOL_DIM // HEAD_PAD


FWD_TILE = 1024


def _stat_rows(col):
    return jnp.broadcast_to(col, (col.shape[0], LANES)).T[0:8]


def _retile_rows(rows, tq):
    heads, n8, t = rows.shape
    if t == tq:
        return rows
    flat = rows.reshape(heads, n8 // 8, 8, t)[:, :, 0].reshape(heads, -1, 1, tq)
    return jnp.broadcast_to(flat, (heads, flat.shape[1], 8, tq)).reshape(heads, -1, tq)


def _flash_fwd(q, k, v, mix, *, name):
    S = q.shape[0]
    tq = FWD_TILE if S % FWD_TILE == 0 else min(S, 512)
    nq = S // tq
    hs = HEADS_PER_STEP
    wide = hs * HEAD_PAD

    def body(q_ref, k_ref, v_ref, mix_ref, o_ref, lse_ref):
        qi = pl.program_id(1)
        qv = [q_ref[:, a * HEAD_PAD:(a + 1) * HEAD_PAD] for a in range(hs)]

        def step(j, carry, masked):
            off = pl.multiple_of(j * tq, tq)
            out = []
            for a in range(hs):
                m, acc = carry[a]
                s = _dot(qv[a], k_ref[pl.ds(off, tq), a * HEAD_PAD:(a + 1) * HEAD_PAD], NT)
                if masked:
                    row = lax.broadcasted_iota(jnp.int32, (tq, tq), 0)
                    col = lax.broadcasted_iota(jnp.int32, (tq, tq), 1)
                    s = jnp.where(col <= row, s, NEG_INF)
                m_new = jnp.maximum(m, jnp.max(s, axis=-1, keepdims=True))
                p = jnp.exp2((s - m_new) * (ATT_SCALE * LOG2E))
                alpha = jnp.exp2((m - m_new) * (ATT_SCALE * LOG2E))
                acc = alpha * acc + _dot(p.astype(BF16), v_ref[pl.ds(off, tq), a * HEAD_PAD:(a + 1) * HEAD_PAD])
                out.append((m_new, acc))
            return tuple(out)

        one = (jnp.full((tq, 1), NEG_INF, F32), jnp.zeros((tq, HEAD_PAD), F32))
        carry = lax.fori_loop(0, qi, lambda j, c: step(j, c, False), (one,) * hs)
        carry = step(qi, carry, True)
        for a in range(hs):
            m, acc = carry[a]
            l = acc[:, V_HEAD:V_HEAD + 1]
            o_ref[:, a * HEAD_PAD:(a + 1) * HEAD_PAD] = (acc / l).astype(o_ref.dtype)
            lse_ref[a] = _stat_rows(m * ATT_SCALE + jnp.log(l))

    blk = pl.BlockSpec((tq, wide), lambda h, i: (i, h))
    full = pl.BlockSpec((S, wide), lambda h, i: (0, h))
    return pl.pallas_call(
        body, grid=(MLA_HEADS // hs, nq), in_specs=[blk, full, full, ANY],
        out_specs=[pl.BlockSpec((tq, wide), lambda h, i: (i, ATT_COL0 // hs + h)),
                   pl.BlockSpec((hs, 8, tq), lambda h, i: (h, i, 0))],
        out_shape=[jax.ShapeDtypeStruct(mix.shape, mix.dtype), jax.ShapeDtypeStruct((MLA_HEADS, nq * 8, tq), F32)],
        input_output_aliases={3: 0}, compiler_params=_cp(2), name=name,
    )(q, k, v, mix)


BWD_TILE = 1024
BWD_HEADS_PER_STEP = 1


def _bwd_tile(S):
    return BWD_TILE if S % BWD_TILE == 0 else min(S, 512)


def _attn_delta(dmix, mix, *, name):
    S = mix.shape[0]
    ts = _bwd_tile(S)
    half = MLA_HEADS // 2
    halves = [_rows(ts, half * HEAD_PAD, 1), _rows(ts, half * HEAD_PAD, 2)]

    def body(do0_ref, do1_ref, o0_ref, o1_ref, d_ref):
        for n, (do_ref, o_ref) in enumerate(((do0_ref, o0_ref), (do1_ref, o1_ref))):
            prod = do_ref[...].astype(F32) * o_ref[...].astype(F32)
            for a in range(half):
                d_ref[n * half + a] = _stat_rows(
                    jnp.sum(prod[:, a * HEAD_PAD:(a + 1) * HEAD_PAD], axis=-1, keepdims=True))

    return pl.pallas_call(
        body, grid=(S // ts,), in_specs=halves + halves,
        out_specs=pl.BlockSpec((MLA_HEADS, 8, ts), lambda i: (0, i, 0)),
        out_shape=jax.ShapeDtypeStruct((MLA_HEADS, (S // ts) * 8, ts), F32), compiler_params=_cp(1), name=name,
    )(dmix, dmix, mix, mix)


def _flash_bwd(q, k, v, dmix, lse_rows, delta_rows, *, name):
    S = q.shape[0]
    tq = _bwd_tile(S)
    nq = S // tq
    hs = BWD_HEADS_PER_STEP
    wide = hs * HEAD_PAD

    def body(q_hbm, do_hbm, lse_ref, dl_ref, k_ref, v_ref, dq_hbm, dk_ref, dv_ref, q_all, do_all, dq_all):
        g, j = pl.program_id(0), pl.program_id(1)
        cols = pl.multiple_of(g * wide, wide)

        @pl.when(j == 0)
        def _():
            pltpu.sync_copy(q_hbm.at[:, pl.ds(cols, wide)], q_all)
            pltpu.sync_copy(do_hbm.at[:, pl.ds(POOL_DIM + cols, wide)], do_all)
            dq_all[...] = jnp.zeros_like(dq_all)

        heads = [slice(a * HEAD_PAD, (a + 1) * HEAD_PAD) for a in range(hs)]
        kv = [k_ref[:, a] for a in heads]
        vv = [v_ref[:, a] for a in heads]

        def step(i, carry, masked):
            off = pl.multiple_of(i * tq, tq)
            off8 = pl.multiple_of(i * 8, 8)
            out = []
            for a in range(hs):
                dk, dv = carry[a]
                qv = q_all[pl.ds(off, tq), heads[a]]
                dov = do_all[pl.ds(off, tq), heads[a]]
                lse2 = lse_ref[a, pl.ds(off8, 8), :][0:1] * LOG2E
                dl = dl_ref[a, pl.ds(off8, 8), :][0:1]
                st = _dot(kv[a], qv, NT)
                if masked:
                    krow = lax.broadcasted_iota(jnp.int32, (tq, tq), 0)
                    qcol = lax.broadcasted_iota(jnp.int32, (tq, tq), 1)
                    st = jnp.where(krow <= qcol, st, NEG_INF)
                pt = jnp.exp2(st * (ATT_SCALE * LOG2E) - lse2)
                dv = dv + _dot(pt.astype(BF16), dov)
                dst = (pt * (_dot(vv[a], dov, NT) - dl)).astype(BF16)
                dk = dk + _dot(dst, qv)
                dq_all[pl.ds(off, tq), heads[a]] += _dot(dst, kv[a], TN)
                out.append((dk, dv))
            return tuple(out)

        zero = jnp.zeros((tq, HEAD_PAD), F32)
        carry = step(j, ((zero, zero),) * hs, True)
        carry = lax.fori_loop(j + 1, nq, lambda i, c: step(i, c, False), carry)
        for a in range(hs):
            dk_ref[:, heads[a]] = carry[a][0] * ATT_SCALE
            dv_ref[:, heads[a]] = carry[a][1]

        @pl.when(j == nq - 1)
        def _():
            dq_all[...] = dq_all[...] * ATT_SCALE
            pltpu.sync_copy(dq_all, dq_hbm.at[:, pl.ds(cols, wide)])

    blk = pl.BlockSpec((tq, wide), lambda g, j: (j, g))
    stat = pl.BlockSpec((hs, nq * 8, tq), lambda g, j: (g, 0, 0))
    full = jax.ShapeDtypeStruct((S, MLA_HEADS * HEAD_PAD), F32)
    return pl.pallas_call(
        body, grid=(MLA_HEADS // hs, nq), in_specs=[ANY, ANY, stat, stat, blk, blk], out_specs=[ANY, blk, blk],
        out_shape=[full, full, full],
        scratch_shapes=[pltpu.VMEM((S, wide), BF16), pltpu.VMEM((S, wide), BF16), pltpu.VMEM((S, wide), F32)],
        compiler_params=_cp(2), name=name,
    )(q, dmix, lse_rows, delta_rows, k, v)


MEM_SCALE = MEM_HEAD_DIM ** -0.5


def _xattn_probs(qh, kh):
    s = _dot(qh, kh, NT) * MEM_SCALE
    e = jnp.exp(s - jnp.max(s, axis=-1, keepdims=True))
    return e / jnp.sum(e, axis=-1, keepdims=True)


def _xattn_fwd(q, kvm, *, name):
    S = q.shape[0]
    ts = _row_tile(S)
    nm = kvm.shape[0]

    def body(q_ref, kv_ref, o_ref):
        for h in range(MEM_HEADS):
            lo, hi = h * MEM_HEAD_DIM, (h + 1) * MEM_HEAD_DIM
            p = _xattn_probs(q_ref[:, lo:hi], kv_ref[:, lo:hi])
            o_ref[:, lo:hi] = _dot(p.astype(BF16), kv_ref[:, D_MODEL + lo:D_MODEL + hi]).astype(o_ref.dtype)

    return pl.pallas_call(
        body, grid=(S // ts,), in_specs=[_rows(ts, D_MODEL), _const((nm, 2 * D_MODEL))],
        out_specs=_rows(ts, D_MODEL), out_shape=jax.ShapeDtypeStruct((S, D_MODEL), BF16),
        compiler_params=_cp(1), name=name,
    )(q, kvm)


def _xattn_bwd(q, kvm, do, *, name):
    S = q.shape[0]
    ts = min(S, 512)
    nm = kvm.shape[0]

    def body(q_ref, kv_ref, do_ref, dq_ref, dkv_ref):
        @pl.when(pl.program_id(0) == 0)
        def _():
            dkv_ref[...] = jnp.zeros_like(dkv_ref)

        for h in range(MEM_HEADS):
            lo, hi = h * MEM_HEAD_DIM, (h + 1) * MEM_HEAD_DIM
            qh, kh, vh = q_ref[:, lo:hi], kv_ref[:, lo:hi], kv_ref[:, D_MODEL + lo:D_MODEL + hi]
            doh = do_ref[:, lo:hi].astype(BF16)
            p = _xattn_probs(qh, kh)
            dp = _dot(doh, vh, NT)
            ds = (p * (dp - jnp.sum(dp * p, axis=-1, keepdims=True)) * MEM_SCALE).astype(BF16)
            dq_ref[:, lo:hi] = _dot(ds, kh).astype(dq_ref.dtype)
            dkv_ref[:, lo:hi] += _dot(ds, qh, TN)
            dkv_ref[:, D_MODEL + lo:D_MODEL + hi] += _dot(p.astype(BF16), doh, TN)

    return pl.pallas_call(
        body, grid=(S // ts,), in_specs=[_rows(ts, D_MODEL), _const((nm, 2 * D_MODEL)), _rows(ts, D_MODEL)],
        out_specs=[_rows(ts, D_MODEL), _const((nm, 2 * D_MODEL))],
        out_shape=[jax.ShapeDtypeStruct((S, D_MODEL), BF16), jax.ShapeDtypeStruct((nm, 2 * D_MODEL), F32)],
        compiler_params=_cp(1), name=name,
    )(q, kvm, do)


CONV_HALO = 8


def _sigmoid(x):
    return 0.5 * jnp.tanh(0.5 * x) + 0.5


def _softplus(x):
    return jnp.maximum(x, 0.0) + jnp.log(1.0 + jnp.exp(-jnp.abs(x)))


def _neg_expm1(x):
    series = -x * (1.0 + x * (1.0 / 2) * (1.0 + x * (1.0 / 3) * (1.0 + x * (1.0 / 4) * (1.0 + x * (1.0 / 5)))))
    return jnp.where(x > -0.05, series, 1.0 - jnp.exp(x))


GELU_C = math.sqrt(2.0 / math.pi)


def _gelu(x):
    return 0.5 * x * (1.0 + jnp.tanh(GELU_C * (x + 0.044715 * x * x * x)))


def _gelu_grad(x):
    t = jnp.tanh(GELU_C * (x + 0.044715 * x * x * x))
    return 0.5 * (1.0 + t) + 0.5 * x * (1.0 - t * t) * GELU_C * (1.0 + 3 * 0.044715 * x * x)


def _lru_gates(xc, wr_ref, br, wi_ref, bi, sp, reset):
    xcb = xc.astype(BF16)
    pr, pi = [], []
    for h in range(LRU_HEADS):
        lo, hi = h * LRU_HEAD_DIM, (h + 1) * LRU_HEAD_DIM
        pr.append(_dot(xcb[:, lo:hi], wr_ref[h]))
        pi.append(_dot(xcb[:, lo:hi], wi_ref[h]))
    r = _sigmoid(jnp.concatenate(pr, axis=1) + br)
    ig = _sigmoid(jnp.concatenate(pi, axis=1) + bi)
    log_a = -LRU_C * r * sp
    a = jnp.where(reset, 0.0, jnp.exp(log_a))
    mult = jnp.where(reset, 1.0, jnp.sqrt(jnp.maximum(_neg_expm1(2.0 * log_a), 0.0)))
    return r, ig, a, mult


SUBLANES = 8


def _compose_groups(a, b, reverse):
    n = a.shape[0]
    row = lax.broadcasted_iota(jnp.int32, a.shape, 0) % SUBLANES
    for s in (1, 2, 4):
        inside = (row < SUBLANES - s) if reverse else (row >= s)
        shift = n - s if reverse else s
        a_s = jnp.where(inside, pltpu.roll(a, shift, 0), 1.0)
        b_s = jnp.where(inside, pltpu.roll(b, shift, 0), 0.0)
        b = a * b_s + b
        a = a * a_s
    return a, b


def _chain_groups(a_buf, h_ref, state, reverse):
    groups = a_buf.shape[0] // SUBLANES

    def group(g, h_in):
        off = pl.multiple_of((groups - 1 - g if reverse else g) * SUBLANES, SUBLANES)
        h = a_buf[pl.ds(off, SUBLANES), :] * h_in + h_ref[pl.ds(off, SUBLANES), :]
        h_ref[pl.ds(off, SUBLANES), :] = h
        return jnp.broadcast_to(h[0:1] if reverse else h[SUBLANES - 1:SUBLANES], h.shape)

    return lax.fori_loop(0, groups, group, state, unroll=4)[0:1]


def _lru_fwd(z, reset, conv_w, conv_b, w_r, b_r, w_i, b_i, lam, *, name):
    S = z.shape[0]
    ts = min(S, 512)
    nh = ts // CONV_HALO
    W = D_MODEL

    def body(gate_ref, xb_ref, halo_ref, rs_ref, cw_ref, cb_ref, wr_ref, br_ref, wi_ref, bi_ref, lam_ref,
             xc_ref, h_ref, y_ref, a_buf, carry):
        i = pl.program_id(0)

        @pl.when(i == 0)
        def _():
            carry[...] = jnp.zeros_like(carry)

        halo = jnp.where(i > 0, halo_ref[...], 0.0)
        xe = jnp.concatenate([halo, xb_ref[...]], axis=0)
        xc = cb_ref[...] + cw_ref[3:4, :] * xe[CONV_HALO:]
        for kk in range(CONV_WIDTH - 1):
            xc = xc + cw_ref[kk:kk + 1, :] * pltpu.roll(xe, CONV_WIDTH - 1 - kk, 0)[CONV_HALO:]
        xc_ref[...] = xc
        reset = rs_ref[...] > 0.5
        _, ig, a, mult = _lru_gates(xc, wr_ref, br_ref[...], wi_ref, bi_ref[...], _softplus(-lam_ref[...]), reset)
        a_buf[...], h_ref[...] = _compose_groups(a, mult * (ig * xc), False)
        carry[...] = _chain_groups(a_buf, h_ref, jnp.broadcast_to(carry[...], (SUBLANES, W)), False)
        y_ref[...] = (_gelu(gate_ref[...]) * h_ref[...]).astype(y_ref.dtype)

    vec = _const((1, W))
    gw = _const((LRU_HEADS, LRU_HEAD_DIM, LRU_HEAD_DIM))
    return pl.pallas_call(
        body, grid=(S // ts,),
        in_specs=[_rows(ts, W, 0), _rows(ts, W, 1),
                  pl.BlockSpec((CONV_HALO, W), lambda i: (jnp.maximum(i * nh - 1, 0), 1)),
                  _rows(ts, 1), _const((CONV_WIDTH, W)), vec, gw, vec, gw, vec, vec],
        out_specs=[_rows(ts, W)] * 3,
        out_shape=[jax.ShapeDtypeStruct((S, W), F32), jax.ShapeDtypeStruct((S, W), F32),
                   jax.ShapeDtypeStruct((S, W), BF16)],
        scratch_shapes=[pltpu.VMEM((ts, W), F32), pltpu.VMEM((1, W), F32)],
        compiler_params=_cp(1), name=name,
    )(z, z, z, reset, conv_w, conv_b, w_r, b_r, w_i, b_i, lam)


def _lru_bwd(dy, z, xc, hseq, reset, w_r, b_r, w_i, b_i, lam, *, name):
    S = z.shape[0]
    ts = min(S, 512)
    nt = S // ts
    nh = ts // CONV_HALO
    W = D_MODEL

    def body(dy_ref, gate_ref, xc_ref, h_ref, hh_ref, rs_ref, wr_ref, br_ref, wi_ref, bi_ref, lam_ref,
             dg_ref, dxc_ref, dpr_ref, dpi_ref, acc_ref, a_buf, dh_buf, carry):
        i = pl.program_id(0)
        tile = nt - 1 - i

        @pl.when(i == 0)
        def _():
            carry[...] = jnp.zeros_like(carry)
            acc_ref[...] = jnp.zeros_like(acc_ref)

        xc = xc_ref[...]
        lam_v = lam_ref[...]
        sp = _softplus(-lam_v)
        reset = rs_ref[...] > 0.5
        r, ig, a, mult = _lru_gates(xc, wr_ref, br_ref[...], wi_ref, bi_ref[...], sp, reset)
        gate = gate_ref[...]
        dyv = dy_ref[...].astype(F32)
        h = h_ref[...]
        dg_ref[...] = (dyv * h * _gelu_grad(gate)).astype(dg_ref.dtype)
        last_row = lax.broadcasted_iota(jnp.int32, a.shape, 0) == ts - 1
        a_buf[...], dh_buf[...] = _compose_groups(jnp.where(last_row, 1.0, pltpu.roll(a, ts - 1, 0)),
                                                  dyv * _gelu(gate), True)
        _chain_groups(a_buf, dh_buf, jnp.broadcast_to(carry[...], (SUBLANES, W)), True)
        dh = dh_buf[...]
        carry[...] = a[0:1] * dh[0:1]
        hh = jnp.where(tile > 0, hh_ref[...], 0.0)
        h_prev = pltpu.roll(jnp.concatenate([hh, h], axis=0), 1, 0)[CONV_HALO:]
        da = dh * h_prev
        bx = ig * xc
        dmult = dh * bx
        dbx = dh * mult
        di = dbx * xc
        dlog_a = jnp.where(reset, 0.0, da * a - dmult * a * a / jnp.maximum(mult, 1e-30))
        dr = dlog_a * (-LRU_C) * sp
        dpre_r = dr * r * (1.0 - r)
        dpre_i = di * ig * (1.0 - ig)
        dprb, dpib = dpre_r.astype(BF16), dpre_i.astype(BF16)
        dpr_ref[...] = dprb
        dpi_ref[...] = dpib
        back = []
        for hd in range(LRU_HEADS):
            lo, hi = hd * LRU_HEAD_DIM, (hd + 1) * LRU_HEAD_DIM
            back.append(_dot(dprb[:, lo:hi], wr_ref[hd], NT) + _dot(dpib[:, lo:hi], wi_ref[hd], NT))
        dxc_ref[...] = dbx * ig + jnp.concatenate(back, axis=1)
        dlam = jnp.sum(dlog_a * (-LRU_C) * r, axis=0, keepdims=True) * (-_sigmoid(-lam_v))
        acc_ref[0:1, :] += jnp.sum(dpre_r, axis=0, keepdims=True)
        acc_ref[1:2, :] += jnp.sum(dpre_i, axis=0, keepdims=True)
        acc_ref[2:3, :] += dlam

    rev = lambda cb: pl.BlockSpec((ts, W), lambda i: (nt - 1 - i, cb))
    vec = _const((1, W))
    gw = _const((LRU_HEADS, LRU_HEAD_DIM, LRU_HEAD_DIM))
    return pl.pallas_call(
        body, grid=(nt,),
        in_specs=[rev(0), rev(0), rev(0), rev(0),
                  pl.BlockSpec((CONV_HALO, W), lambda i: (jnp.maximum((nt - 1 - i) * nh - 1, 0), 0)),
                  pl.BlockSpec((ts, 1), lambda i: (nt - 1 - i, 0)), gw, vec, gw, vec, vec],
        out_specs=[rev(0), rev(0), rev(0), rev(0), _const((8, W))],
        out_shape=[jax.ShapeDtypeStruct((S, W), BF16), jax.ShapeDtypeStruct((S, W), F32),
                   jax.ShapeDtypeStruct((S, W), BF16), jax.ShapeDtypeStruct((S, W), BF16),
                   jax.ShapeDtypeStruct((8, W), F32)],
        scratch_shapes=[pltpu.VMEM((ts, W), F32), pltpu.VMEM((ts, W), F32), pltpu.VMEM((1, W), F32)],
        compiler_params=_cp(1), name=name,
    )(dy, z, xc, hseq, hseq, reset, w_r, b_r, w_i, b_i, lam)


def _conv_bwd(dxc, z, conv_w, *, name):
    S = dxc.shape[0]
    ts = min(S, 512)
    nh = ts // CONV_HALO
    last = S // CONV_HALO - 1
    W = D_MODEL
    n = ts + CONV_HALO

    def body(d_ref, dn_ref, xb_ref, xp_ref, cw_ref, dxb_ref, acc_ref):
        i = pl.program_id(0)

        @pl.when(i == 0)
        def _():
            acc_ref[...] = jnp.zeros_like(acc_ref)

        d = d_ref[...]
        de = jnp.concatenate([d, jnp.where(i < pl.num_programs(0) - 1, dn_ref[...], 0.0)], axis=0)
        xe = jnp.concatenate([jnp.where(i > 0, xp_ref[...], 0.0), xb_ref[...]], axis=0)
        dxb = cw_ref[3:4, :] * d
        acc_ref[3:4, :] += jnp.sum(d * xe[CONV_HALO:], axis=0, keepdims=True)
        for kk in range(CONV_WIDTH - 1):
            sh = CONV_WIDTH - 1 - kk
            dxb = dxb + cw_ref[kk:kk + 1, :] * pltpu.roll(de, n - sh, 0)[:ts]
            acc_ref[kk:kk + 1, :] += jnp.sum(d * pltpu.roll(xe, sh, 0)[CONV_HALO:], axis=0, keepdims=True)
        dxb_ref[...] = dxb.astype(dxb_ref.dtype)
        acc_ref[4:5, :] += jnp.sum(d, axis=0, keepdims=True)

    return pl.pallas_call(
        body, grid=(S // ts,),
        in_specs=[_rows(ts, W), pl.BlockSpec((CONV_HALO, W), lambda i: (jnp.minimum((i + 1) * nh, last), 0)),
                  _rows(ts, W, 1), pl.BlockSpec((CONV_HALO, W), lambda i: (jnp.maximum(i * nh - 1, 0), 1)),
                  _const((CONV_WIDTH, W))],
        out_specs=[_rows(ts, W), _const((8, W))],
        out_shape=[jax.ShapeDtypeStruct((S, W), BF16), jax.ShapeDtypeStruct((8, W), F32)],
        compiler_params=_cp(1), name=name,
    )(dxc, dxc, z, z, conv_w)


def _loss_head(x, g, target, *, name):
    S, D = x.shape
    ts = _row_tile(S)

    def body(x_ref, g_ref, t_ref, dx_ref, dg_ref, l_ref):
        @pl.when(pl.program_id(0) == 0)
        def _():
            dg_ref[...] = jnp.zeros_like(dg_ref)
            l_ref[...] = jnp.zeros_like(l_ref)

        xv = x_ref[...]
        r = lax.rsqrt(jnp.mean(xv * xv, axis=-1, keepdims=True) + RMS_EPS)
        n = xv * r
        err = n * g_ref[...] - t_ref[...]
        l_ref[...] += 0.5 * jnp.sum(jnp.sum(err * err, axis=-1, keepdims=True) * (1.0 / D), axis=0, keepdims=True)
        dy = err * (1.0 / D)
        dn = dy * g_ref[...]
        dx_ref[...] = r * (dn - n * jnp.mean(dn * n, axis=-1, keepdims=True))
        dg_ref[...] += jnp.sum(dy * n, axis=0, keepdims=True)

    return pl.pallas_call(
        body, grid=(S // ts,), in_specs=[_rows(ts, D), _const((1, D)), _rows(ts, D)],
        out_specs=[_rows(ts, D), _const((1, D)), _const((8, LANES))],
        out_shape=[jax.ShapeDtypeStruct((S, D), F32), jax.ShapeDtypeStruct((1, D), F32),
                   jax.ShapeDtypeStruct((8, LANES), F32)],
        compiler_params=_cp(1), name=name,
    )(x, g.reshape(1, D), target)


def _adamw(w, ga, gb, m, v, *, name):
    shape = w.shape
    cols = shape[-1]
    rows = w.size // cols
    br = rows
    if rows * cols * 4 > (1 << 20):
        br = max(d for d in range(8, rows + 1, 8) if rows % d == 0 and d * cols * 4 <= (1 << 20))

    def body(w_ref, ga_ref, gb_ref, m_ref, v_ref, g_ref, d_ref, mo_ref, vo_ref):
        gv = ga_ref[...] + gb_ref[...]
        g_ref[...] = gv
        mn = ADAM_B1 * m_ref[...] + (1.0 - ADAM_B1) * gv
        vn = ADAM_B2 * v_ref[...] + (1.0 - ADAM_B2) * (gv * gv)
        m_hat = mn / (1.0 - ADAM_B1 ** ADAM_STEP)
        v_hat = vn / (1.0 - ADAM_B2 ** ADAM_STEP)
        d_ref[...] = -ADAM_LR * (m_hat / (jnp.sqrt(v_hat) + ADAM_EPS) + ADAM_WD * w_ref[...])
        mo_ref[...] = mn
        vo_ref[...] = vn

    spec = _rows(br, cols)
    outs = pl.pallas_call(
        body, grid=(rows // br,), in_specs=[spec] * 5, out_specs=[spec] * 4,
        out_shape=[jax.ShapeDtypeStruct((rows, cols), F32)] * 4, compiler_params=_cp(1), name=name,
    )(*[t.reshape(rows, cols) for t in (w, ga, gb, m, v)])
    return [o.reshape(shape) for o in outs]


def _pad_heads(w, width):
    k = w.shape[0]
    return jnp.pad(w.reshape(k, MLA_HEADS, width), ((0, 0), (0, 0), (0, HEAD_PAD - width))).reshape(k, -1)


def _unpad_heads(w, width):
    k = w.shape[0]
    return w.reshape(k, MLA_HEADS, HEAD_PAD)[:, :, :width].reshape(k, MLA_HEADS * width)


def _rope_tables(positions):
    inv_freq = ROPE_BASE ** (-jnp.arange(0, QK_ROPE, 2, dtype=F32) / QK_ROPE)
    ang = positions.astype(F32)[:, None] * inv_freq
    cos, sin = jnp.cos(ang), jnp.sin(ang)
    S = positions.shape[0]
    ones, zeros = jnp.ones((S, QK_NOPE), F32), jnp.zeros((S, QK_NOPE), F32)
    ctab = jnp.concatenate([ones, cos, cos, ones[:, :HEAD_PAD - QK_DIM]], axis=1)
    stab = jnp.concatenate([zeros, -sin, sin, zeros[:, :HEAD_PAD - QK_DIM]], axis=1)
    return ctab, stab


def _memory_block(x, mem, W, layer, tag):
    hx = _rms(x, W["xa_norm_x"][layer], name=f"{tag}_xa_norm")
    qx = _mm(hx, [(W["xa_w_q"][layer], 0, 0)], _first, [(D_MODEL, BF16, 0)], tn=D_MODEL, nj=1, name=f"{tag}_xa_q")[0]
    mn = _rms(mem, W["xa_norm_mem"][layer], name=f"{tag}_xa_norm_mem")
    kvm = _mm(mn, [(W["xa_w_kv"][layer], 0, 0)], _first, [(2 * D_MODEL, BF16, 0)], tn=2 * D_MODEL, nj=1,
              name=f"{tag}_xa_kv")[0]
    o = _xattn_fwd(qx, kvm, name=f"{tag}_xa_attn")
    xo = _mm(o, [(W["xa_w_o"][layer], 0, 0)], _add_res, [(D_MODEL, F32, 0)], extras=[(x, 0)], tn=D_MODEL, nj=1,
             name=f"{tag}_xa_out")[0]
    return xo, (x, hx, qx, mn, kvm, o)


def _memory_block_bwd(dxo, mem, W, layer, saved, tag, grads):
    x, hx, qx, mn, kvm, o = saved
    wq, wkv, wo = W["xa_w_q"][layer], W["xa_w_kv"][layer], W["xa_w_o"][layer]
    do = _mm(dxo, [(wo, 0, 0)], _first, [(D_MODEL, BF16, 0)], nt=True, tn=D_MODEL, nj=1, name=f"{tag}_xa_do")[0]
    grads["xa_w_o"][layer] = _owner_major(_mm_tn(o, dxo, name=f"{tag}_xa_dwo"), 0)
    dqx, dkvm = _xattn_bwd(qx, kvm, do, name=f"{tag}_xa_attn_bwd")
    grads["xa_w_q"][layer] = _owner_major(_mm_tn(hx, dqx, name=f"{tag}_xa_dwq"), 0)
    dx, dg = _mm(dqx, [(wq, 0, 0)], _norm_bwd_epilogue(0), [(D_MODEL, F32, 0)], nt=True, extras=[(x, 0), (dxo, 0)],
                 rows=[W["xa_norm_x"][layer].reshape(1, D_MODEL)], sums=[D_MODEL], tn=D_MODEL, nj=1,
                 name=f"{tag}_xa_dhx")
    grads["xa_norm_x"][layer] = dg[0]
    dmn = _mm(dkvm, [(wkv, 0, 0)], _first, [(D_MODEL, F32, 0)], nt=True, tn=D_MODEL, nj=1, name=f"{tag}_xa_dmn")[0]
    grads["xa_w_kv"][layer] = _mm_tn_owners(mn, [dkvm], name=f"{tag}_xa_dwkv")
    _, dgm = _rms_bwd(mem, W["xa_norm_mem"][layer], dmn, name=f"{tag}_xa_norm_mem_bwd")
    grads["xa_norm_mem"][layer] = dgm[0]
    return dx


FF_TN = D_FF // 2

def _silu_mul(accs, extras):
    g, u = accs
    return [g * _sigmoid(g) * u, g, u]


def _silu_mul_bwd(accs, extras):
    da = accs[0]
    g, u = extras[0].astype(F32), extras[1].astype(F32)
    sg = _sigmoid(g)
    return [da * u * sg * (1.0 + g * (1.0 - sg)), da * g * sg]


def _ffn_block(x, W, layer, tag):
    hf = _rms(x, W["ffn_norm"][layer], name=f"{tag}_ffn_norm")
    wgu, wd = W["ffn_w_gate_up"][layer], W["ffn_w_down"][layer]
    act, g, u = _mm(hf, [(wgu, 0, 0), (wgu, 0, 2)], _silu_mul, [(D_FF, BF16, 0)] * 3, tn=FF_TN, nj=2,
                    name=f"{tag}_ffn_up")
    xo = _mm(act, [(wd, 0, 0)], _add_res, [(D_MODEL, F32, 0)], extras=[(x, 0)], tn=D_MODEL, nj=1,
             name=f"{tag}_ffn_down")[0]
    return xo, (x, hf, act, g, u)


def _ffn_block_bwd(dxo, W, layer, saved, tag, grads):
    x, hf, act, g, u = saved
    wgu, wd = W["ffn_w_gate_up"][layer], W["ffn_w_down"][layer]
    dg, du = _mm(dxo, [(wd, 0, 0)], _silu_mul_bwd, [(D_FF, BF16, 0)] * 2, nt=True, extras=[(g, 0), (u, 0)], tn=FF_TN,
                 nj=2, name=f"{tag}_ffn_dact")
    grads["ffn_w_down"][layer] = _owner_major(_mm_tn(act, dxo, tk=FF_TN, name=f"{tag}_ffn_dwd"), 0)
    dx, dgn = _mm(dg, [(wgu, 0, 0)], _norm_bwd_epilogue(0), [(D_MODEL, F32, 0)], nt=True, also=(du, (wgu, 0, 1)),
                  extras=[(x, 0), (dxo, 0)], rows=[W["ffn_norm"][layer].reshape(1, D_MODEL)],
                  sums=[D_MODEL], tn=D_MODEL, nj=1, name=f"{tag}_ffn_dhf")
    grads["ffn_w_gate_up"][layer] = _mm_tn_owners(hf, [dg, du], name=f"{tag}_ffn_dwgu")
    grads["ffn_norm"][layer] = dgn[0]
    return dx


def _keys_and_values(accs, extras):
    k, v = accs
    lane = lax.broadcasted_iota(jnp.int32, v.shape, 1)
    return [k, jnp.where(lane % HEAD_PAD == V_HEAD, 1.0, v)]


def _even_block(x, tabs, W, tag):
    ctab, stab = tabs
    w_in = W["ev_w_in"][0]
    zero = jnp.zeros((D_MODEL, QK_NOPE), BF16)
    w_in_pad = jnp.concatenate([w_in[:, :896], zero, w_in[:, 896:], zero[:, :HEAD_PAD - QK_DIM]], axis=1)
    w_q_pad = _pad_heads(W["ev_w_q_up"][0], QK_DIM)
    wkv = W["ev_w_kv_up"][0].reshape(KV_RANK, MLA_HEADS, QK_NOPE + V_HEAD)
    w_kv_pad = jnp.concatenate([_pad_heads(wkv[:, :, :QK_NOPE].reshape(KV_RANK, -1), QK_NOPE),
                                _pad_heads(wkv[:, :, QK_NOPE:].reshape(KV_RANK, -1), V_HEAD)], axis=1)
    w_out = W["ev_w_out"][0]
    w_att = jnp.pad(w_out[POOL_DIM:].reshape(MLA_HEADS, V_HEAD, D_MODEL), ((0, 0), (0, HEAD_PAD - V_HEAD), (0, 0)))
    w_out_pad = jnp.concatenate([w_out[:POOL_DIM], w_att.reshape(MLA_HEADS * HEAD_PAD, D_MODEL)], axis=0)
    pool_w = W["ev_pool_w"][0].astype(BF16)
    pool_scale = W["ev_pool_scale"]

    h = _rms(x, W["ev_norm"][0], name=f"{tag}_norm")
    z = _mm(h, [(w_in_pad, 0, 0)], _first, [(D_MODEL, F32, 0)], tn=D_MODEL, nj=1, name=f"{tag}_in")[0]
    mix, pooled = _pool_fwd(z, pool_w, pool_scale, name=f"{tag}_pool")
    cqn = _rms(z, W["ev_q_norm"][0], cb=2, w=Q_RANK, name=f"{tag}_q_norm")
    ckvn = _rms(z, W["ev_kv_norm"][0], cb=6, w=KV_RANK, name=f"{tag}_kv_norm")
    q_pad = _mm(cqn, [(w_q_pad, 0, 0)], _first, [(D_MODEL, F32, 0)], tn=D_MODEL, nj=1, name=f"{tag}_q_up")[0]
    k_pad, v_pad = _mm(ckvn, [(w_kv_pad, 0, 0), (w_kv_pad, 0, 1)], _keys_and_values,
                       [(D_MODEL, F32, 0), (D_MODEL, BF16, 0)], tn=D_MODEL, nj=1, name=f"{tag}_kv_up")
    q_rot, k_cat = _rope_fwd(q_pad, k_pad, z, ctab, stab, name=f"{tag}_rope")
    mix, lse = _flash_fwd(q_rot, k_cat, v_pad, mix, name=f"{tag}_attn")
    xo = _mm(mix, [(w_out_pad, 0, 0)], _add_res, [(D_MODEL, F32, 0)], extras=[(x, 0)], tn=D_MODEL, nj=1,
             name=f"{tag}_out")[0]
    saved = (x, h, z, pooled, cqn, ckvn, q_rot, k_cat, v_pad, lse, mix,
             (w_in_pad, w_q_pad, w_kv_pad, w_out_pad, pool_w, pool_scale))
    return xo, saved


def _even_block_bwd(dxo, tabs, W, saved, tag, grads, token=None):
    ctab, stab = tabs
    x, h, z, pooled, cqn, ckvn, q_rot, k_cat, v_pad, lse, mix, wts = saved
    w_in_pad, w_q_pad, w_kv_pad, w_out_pad, pool_w, pool_scale = wts
    if token is not None:
        w_out_pad = w_out_pad + token[0:1, 0:1].astype(BF16)
    dmix = _mm(dxo, [(w_out_pad, 0, 0)], _first, [(MIX_DIM, BF16, 0)], nt=True, tn=MIX_DIM, nj=1,
               name=f"{tag}_dmix")[0]
    dw_out_pad = _mm_tn(mix, dxo, tk=MIX_DIM // 3, name=f"{tag}_dw_out")
    datt = dw_out_pad[POOL_DIM:].reshape(MLA_HEADS, HEAD_PAD, D_MODEL)[:, :V_HEAD].reshape(-1, D_MODEL)
    grads["ev_w_out"] = [_owner_major(jnp.concatenate([dw_out_pad[:POOL_DIM], datt], axis=0), 0)]
    delta = _attn_delta(dmix, mix, name=f"{tag}_delta")
    dq_rot, dk_cat, dv_pad = _flash_bwd(q_rot, k_cat, v_pad, dmix, _retile_rows(lse, delta.shape[2]), delta,
                                        name=f"{tag}_attn_bwd")
    dq_pad, dkr = _rope_bwd(dq_rot, dk_cat, ctab, stab, name=f"{tag}_rope_bwd")
    dw_q_pad = _mm_tn(cqn, dq_pad, name=f"{tag}_dw_q_up")
    grads["ev_w_q_up"] = [_owner_major(_unpad_heads(dw_q_pad, QK_DIM), 1)]
    dcqn = _mm(dq_pad, [(w_q_pad, 0, 0)], _first, [(Q_RANK, F32, 0)], nt=True, tn=Q_RANK, nj=1, name=f"{tag}_dcqn")[0]
    dwk = _unpad_heads(_mm_tn(ckvn, dk_cat, name=f"{tag}_dw_k_up"), QK_NOPE).reshape(KV_RANK, MLA_HEADS, QK_NOPE)
    dwv = _unpad_heads(_mm_tn(ckvn, dv_pad, name=f"{tag}_dw_v_up"), V_HEAD).reshape(KV_RANK, MLA_HEADS, V_HEAD)
    grads["ev_w_kv_up"] = [_owner_major(jnp.concatenate([dwk, dwv], axis=2).reshape(KV_RANK, -1), 1)]
    dckvn = _mm(dk_cat, [(w_kv_pad, 0, 0)], _first, [(KV_RANK, F32, 0)], nt=True, tn=KV_RANK, nj=1,
                name=f"{tag}_dckvn_k")[0]
    dckvn = _mm(dv_pad, [(w_kv_pad, 0, 1)], _add_res, [(KV_RANK, F32, 0)], nt=True, extras=[(dckvn, 0)], tn=KV_RANK,
                nj=1, name=f"{tag}_dckvn_v")[0]
    dcq, dgq = _rms_bwd(z, W["ev_q_norm"][0], dcqn, cb=2, w=Q_RANK, out_dtype=BF16, name=f"{tag}_q_norm_bwd")
    dckv, dgkv = _rms_bwd(z, W["ev_kv_norm"][0], dckvn, cb=6, w=KV_RANK, out_dtype=BF16, name=f"{tag}_kv_norm_bwd")
    grads["ev_q_norm"], grads["ev_kv_norm"] = dgq, dgkv
    du, dypre, dscale = _pool_bwd(dmix, pooled, pool_w, pool_scale, name=f"{tag}_pool_bwd")
    grads["ev_pool_scale"] = dscale
    grads["ev_pool_w"] = _mm_tn_grouped(pooled, dypre, 4, POOL_GROUP, name=f"{tag}_dpool_w")[None]
    dz = jnp.concatenate([du, dcq, dckv, dkr], axis=1)
    dw_in_pad = _mm_tn(h, dz, name=f"{tag}_dw_in")
    grads["ev_w_in"] = [_owner_major(jnp.concatenate([dw_in_pad[:, :896], dw_in_pad[:, 960:992]], axis=1), 0)]
    dx, dgn = _mm(dz, [(w_in_pad, 0, 0)], _norm_bwd_epilogue(0), [(D_MODEL, F32, 0)], nt=True,
                  extras=[(x, 0), (dxo, 0)], rows=[W["ev_norm"][0].reshape(1, D_MODEL)], sums=[D_MODEL], tn=D_MODEL,
                  nj=1, name=f"{tag}_dh")
    grads["ev_norm"] = dgn
    return dx


def _odd_block(x, reset, W, tag):
    h = _rms(x, W["od_norm"][0], name=f"{tag}_norm")
    z = _mm(h, [(W["od_w_in"][0], 0, 0)], _first, [(2 * D_MODEL, F32, 0)], tn=D_MODEL, nj=2, name=f"{tag}_in")[0]
    w_r, w_i = W["od_w_rgate"][0], W["od_w_igate"][0]
    vecs = [W[n].reshape(1, D_MODEL) for n in ("od_conv_b", "od_b_rgate", "od_b_igate", "od_lambda")]
    xc, hseq, y = _lru_fwd(z, reset, W["od_conv_w"][0], vecs[0], w_r, vecs[1], w_i, vecs[2], vecs[3],
                           name=f"{tag}_lru")
    xo = _mm(y, [(W["od_w_out"][0], 0, 0)], _add_res, [(D_MODEL, F32, 0)], extras=[(x, 0)], tn=D_MODEL, nj=1,
             name=f"{tag}_out")[0]
    return xo, (x, h, z, xc, hseq, y, vecs)


def _odd_block_bwd(dxo, reset, W, saved, tag, grads):
    x, h, z, xc, hseq, y, vecs = saved
    w_r, w_i = W["od_w_rgate"][0], W["od_w_igate"][0]
    dy = _mm(dxo, [(W["od_w_out"][0], 0, 0)], _first, [(D_MODEL, F32, 0)], nt=True, tn=D_MODEL, nj=1,
             name=f"{tag}_dy")[0]
    grads["od_w_out"] = [_owner_major(_mm_tn(y, dxo, name=f"{tag}_dw_out"), 0)]
    dgate, dxc, dpr, dpi, acc = _lru_bwd(dy, z, xc, hseq, reset, w_r, vecs[1], w_i, vecs[2], vecs[3],
                                         name=f"{tag}_lru_bwd")
    grads["od_b_rgate"], grads["od_b_igate"], grads["od_lambda"] = acc[0:1], acc[1:2], acc[2:3]
    grads["od_w_rgate"] = [_owner_major(_mm_tn_grouped(xc, dpr, LRU_HEADS, LRU_HEAD_DIM, name=f"{tag}_dw_rgate"), 1)]
    grads["od_w_igate"] = [_owner_major(_mm_tn_grouped(xc, dpi, LRU_HEADS, LRU_HEAD_DIM, name=f"{tag}_dw_igate"), 1)]
    dxb, cacc = _conv_bwd(dxc, z, W["od_conv_w"][0], name=f"{tag}_conv_bwd")
    grads["od_conv_w"], grads["od_conv_b"] = cacc[None, 0:4], cacc[4:5]
    dz = jnp.concatenate([dgate, dxb], axis=1)
    grads["od_w_in"] = [_mm_tn_owners(h, [dz], name=f"{tag}_dw_in")]
    dx, dgn = _mm(dz, [(W["od_w_in"][0], 0, 0)], _norm_bwd_epilogue(0), [(D_MODEL, F32, 0)], nt=True,
                  extras=[(x, 0), (dxo, 0)], rows=[W["od_norm"][0].reshape(1, D_MODEL)], sums=[D_MODEL], tn=D_MODEL,
                  nj=1, name=f"{tag}_dh")
    grads["od_norm"] = dgn
    return dx


def _local_step(x, mem, positions, target, W, later_weights=None, exchange_earlier=None):
    tabs = _rope_tables(positions)
    reset = (positions == 0).astype(F32)[:, None]
    grads = {n: [None, None] for n in ("xa_norm_x", "xa_norm_mem", "xa_w_q", "xa_w_kv", "xa_w_o", "ffn_norm",
                                       "ffn_w_gate_up", "ffn_w_down")}
    x1, s_even = _even_block(x, tabs, W, "l0_even")
    if later_weights is not None:
        W = {**W, **later_weights(x1)}
    x2, s_xa0 = _memory_block(x1, mem, W, 0, "l0")
    x3, s_ff0 = _ffn_block(x2, W, 0, "l0")
    x4, s_odd = _odd_block(x3, reset, W, "l1_odd")
    x5, s_xa1 = _memory_block(x4, mem, W, 1, "l1")
    x6, s_ff1 = _ffn_block(x5, W, 1, "l1")
    d, dgf, loss = _loss_head(x6, W["final_norm"], target, name="loss_head")
    grads["final_norm"] = dgf[0]
    d = _ffn_block_bwd(d, W, 1, s_ff1, "l1", grads)
    d = _memory_block_bwd(d, mem, W, 1, s_xa1, "l1", grads)
    d = _odd_block_bwd(d, reset, W, s_odd, "l1_odd", grads)
    d = _ffn_block_bwd(d, W, 0, s_ff0, "l0", grads)
    d = _memory_block_bwd(d, mem, W, 0, s_xa0, "l0", grads)
    token = exchange_earlier(grads) if exchange_earlier is not None else None
    d = _even_block_bwd(d, tabs, W, s_even, "l0_even", grads, token)
    big = {n: grads.pop(n) for n in MATMUL_WEIGHTS}
    for n, v in grads.items():
        if isinstance(v, list):
            grads[n] = jnp.stack(v)
    return loss[0, 0], d, big, grads


WEIGHTS = ("ev_norm", "ev_w_in", "ev_pool_w", "ev_pool_scale", "ev_q_norm", "ev_w_q_up", "ev_kv_norm", "ev_w_kv_up",
           "ev_w_out", "od_norm", "od_w_in", "od_conv_w", "od_conv_b", "od_w_rgate", "od_b_rgate", "od_w_igate",
           "od_b_igate", "od_lambda", "od_w_out", "xa_norm_x", "xa_norm_mem", "xa_w_q", "xa_w_kv", "xa_w_o",
           "ffn_norm", "ffn_w_gate_up", "ffn_w_down", "final_norm")
SHARD_AXIS = {"ev_w_in": 1, "ev_w_q_up": 2, "ev_w_kv_up": 2, "ev_w_out": 1, "od_norm": 1, "od_w_in": 2,
              "od_conv_w": 2, "od_conv_b": 1, "od_w_rgate": 2, "od_b_rgate": 1, "od_w_igate": 2, "od_b_igate": 1,
              "od_lambda": 1, "od_w_out": 1, "xa_w_q": 1, "xa_w_kv": 2, "xa_w_o": 1, "ffn_w_gate_up": 2,
              "ffn_w_down": 1}
MATMUL_WEIGHTS = ("ev_w_in", "ev_w_q_up", "ev_w_kv_up", "ev_w_out", "od_w_in", "od_w_rgate", "od_w_igate",
                  "od_w_out", "xa_w_q", "xa_w_kv", "xa_w_o", "ffn_w_gate_up", "ffn_w_down")
SMALL_SHARDED = tuple(n for n in WEIGHTS if n in SHARD_AXIS and n not in MATMUL_WEIGHTS)
REPLICATED = tuple(n for n in WEIGHTS if n not in SHARD_AXIS)


def _pack(parts, quantum):
    flat = jnp.concatenate([p.reshape(-1) for p in parts])
    pad = (-flat.shape[0]) % quantum
    return jnp.pad(flat, (0, pad)).reshape(-1, LANES)


def _unpack(flat, shapes):
    out, off = [], 0
    for shape in shapes:
        size = math.prod(shape)
        out.append(flat[off:off + size].reshape(shape))
        off += size
    return out


def _run_copies(local, remote, send_sems, recv_sems, local_sems):
    locals_ = [pltpu.make_async_copy(src, dst, local_sems.at[n]) for n, (src, dst) in enumerate(local)]
    for cp in locals_:
        cp.start()
    sends = [pltpu.make_async_remote_copy(src_ref=src, dst_ref=dst, send_sem=send_sems.at[k, n],
                                          recv_sem=recv_sems.at[k, n], device_id=dev, device_id_type=MESH)
             for (k, n, src, dst, _, dev) in remote]
    for cp in sends:
        cp.start()
    for (k, n, src, _, arrival, dev) in remote:
        pltpu.make_async_remote_copy(src_ref=src, dst_ref=arrival, send_sem=send_sems.at[k, n],
                                     recv_sem=recv_sems.at[k, n], device_id=dev, device_id_type=MESH).wait_recv()
    for cp in sends:
        cp.wait_send()
    for cp in locals_:
        cp.wait()


def _chip_peers(x, y):
    return [(1 - x, y), (x, 1 - y), (1 - x, 1 - y)]


def _owner_block(ref, axis, q):
    size = ref.shape[axis] // N_CHIPS
    idx = [slice(None)] * len(ref.shape)
    idx[axis] = pl.ds(q * size, size)
    return ref.at[tuple(idx)]


def _comm_call(body, ins, out_shapes, n_items, n_peers, *, name):
    return pl.pallas_call(
        body, in_specs=[ANY] * len(ins), out_specs=[ANY] * len(out_shapes), out_shape=out_shapes,
        scratch_shapes=[pltpu.SemaphoreType.DMA((n_peers, n_items)), pltpu.SemaphoreType.DMA((n_peers, n_items)),
                        pltpu.SemaphoreType.DMA((n_items,))],
        name=name,
    )(*ins)


def _gather_chips(shards, axes, *, name):
    n = len(shards)
    full = [jax.ShapeDtypeStruct(tuple(d * (N_CHIPS if a == ax else 1) for a, d in enumerate(s.shape)), s.dtype)
            for s, ax in zip(shards, axes)]

    def body(*refs):
        srcs, dsts = refs[:n], refs[n:2 * n]
        x, y, c = lax.axis_index("x"), lax.axis_index("y"), lax.axis_index("c")
        me = 2 * x + y
        local = [(srcs[i], _owner_block(dsts[i], axes[i], me)) for i in range(n)]
        remote = [(k, i, srcs[i], _owner_block(dsts[i], axes[i], me), _owner_block(dsts[i], axes[i], 2 * px + py),
                   (px, py, c))
                  for k, (px, py) in enumerate(_chip_peers(x, y)) for i in range(n)]
        _run_copies(local, remote, *refs[2 * n:])

    return _comm_call(body, shards, full, n, 3, name=name)


HBM = pl.BlockSpec(memory_space=pltpu.HBM)
SEM = pl.BlockSpec(memory_space=pltpu.SEMAPHORE)
DATAFLOW = pltpu.SideEffectType.DATAFLOW_SIDE_EFFECTING


def _gather_plan(axes):
    return lambda srcs, lands, me, peer: [
        (srcs[i], _owner_block(lands[i], ax, me), _owner_block(lands[i], ax, peer)) for i, ax in enumerate(axes)]


def _exchange_plan(where):
    return lambda srcs, lands, me, peer: [
        (srcs[i].at[peer], lands[n].at[me, l], lands[n].at[peer, l]) for i, (n, l) in enumerate(where)]


def _split_start(srcs, lands, plan, *, name):
    ns, nl = len(srcs), len(lands)
    nsem = 3 * len(plan(list(srcs), list(lands), 0, 0))

    def body(*refs):
        src_refs, land_refs = refs[:ns], refs[ns:ns + nl]
        send_sems, recv_sems = refs[ns + nl:ns + nl + nsem], refs[ns + nl + nsem:ns + nl + 2 * nsem]
        x, y, c = lax.axis_index("x"), lax.axis_index("y"), lax.axis_index("c")
        n = 0
        for px, py in _chip_peers(x, y):
            for src, dst, _ in plan(src_refs, land_refs, 2 * x + y, 2 * px + py):
                pltpu.make_async_remote_copy(src_ref=src, dst_ref=dst, send_sem=send_sems[n], recv_sem=recv_sems[n],
                                             device_id=(px, py, c), device_id_type=MESH).start()
                n += 1
        refs[-1][...] = jnp.zeros_like(refs[-1])

    arrays = list(srcs) + list(lands)
    out = pl.pallas_call(
        body, name=name, in_specs=[HBM] * (ns + nl),
        out_specs=[SEM] * (2 * nsem) + [HBM] * (ns + nl) + [pl.BlockSpec(memory_space=pltpu.VMEM)],
        out_shape=[pltpu.SemaphoreType.DMA(())] * (2 * nsem) + [pltpu.HBM(a.shape, a.dtype) for a in arrays]
        + [jax.ShapeDtypeStruct((8, LANES), F32)],
        input_output_aliases={i: 2 * nsem + i for i in range(ns + nl)},
        compiler_params=pltpu.CompilerParams(has_side_effects=DATAFLOW),
    )(*[pltpu.with_memory_space_constraint(a, pltpu.HBM) for a in arrays])
    sems, rest = out[:2 * nsem], out[2 * nsem:]
    return sems[:nsem], sems[nsem:], rest[:ns], rest[ns:ns + nl], rest[-1]


def _split_wait(handle, after, plan, *, name):
    send_sems, recv_sems, srcs, lands, _ = handle
    ns, nl, nsem = len(srcs), len(lands), len(send_sems)

    def body(*refs):
        src_refs, land_refs = refs[:ns], refs[ns:ns + nl]
        send_refs, recv_refs = refs[ns + nl:ns + nl + nsem], refs[ns + nl + nsem:ns + nl + 2 * nsem]
        x, y, c = lax.axis_index("x"), lax.axis_index("y"), lax.axis_index("c")
        n = 0
        for px, py in _chip_peers(x, y):
            for src, _, arrival in plan(src_refs, land_refs, 2 * x + y, 2 * px + py):
                cp = pltpu.make_async_remote_copy(src_ref=src, dst_ref=arrival, send_sem=send_refs[n],
                                                  recv_sem=recv_refs[n], device_id=(px, py, c), device_id_type=MESH)
                cp.wait_send()
                cp.wait_recv()
                n += 1

    out = pl.pallas_call(
        body, name=name, in_specs=[HBM] * (ns + nl) + [SEM] * (2 * nsem) + [ANY], out_specs=[HBM] * (ns + nl),
        out_shape=[pltpu.HBM(a.shape, a.dtype) for a in list(srcs) + list(lands)],
        input_output_aliases={i: i for i in range(ns + nl)},
        compiler_params=pltpu.CompilerParams(has_side_effects=DATAFLOW),
    )(*srcs, *lands, *send_sems, *recv_sems, after)
    return out[ns:]


def _exchange_sibling(arrays, *, name):
    n = len(arrays)

    def body(*refs):
        x, y, c = lax.axis_index("x"), lax.axis_index("y"), lax.axis_index("c")
        remote = [(0, i, refs[i], refs[n + i], refs[n + i], (x, y, 1 - c)) for i in range(n)]
        _run_copies([], remote, *refs[2 * n:])

    return _comm_call(body, arrays, [jax.ShapeDtypeStruct(a.shape, a.dtype) for a in arrays], n, 1, name=name)


def _sum_slots(r, *, token=None, name):
    shape = r.shape[1:]
    cols = shape[-1]
    rows = math.prod(shape) // cols
    tr = max(d for d in range(8, rows + 1, 8) if rows % d == 0 and d * cols * 16 <= (4 << 20))

    def body(r_ref, *refs):
        total = ((r_ref[0] + r_ref[1]) + r_ref[2]) + r_ref[3]
        refs[-1][...] = total if token is None else total + refs[0][0:1, 0:1]

    in_specs = [pl.BlockSpec((N_CHIPS, tr, cols), lambda i: (0, i, 0))]
    in_specs += [] if token is None else [_const((8, LANES))]
    return pl.pallas_call(
        body, grid=(rows // tr,), in_specs=in_specs,
        out_specs=_rows(tr, cols), out_shape=jax.ShapeDtypeStruct((rows, cols), F32), compiler_params=_cp(1),
        name=name,
    )(r.reshape(N_CHIPS, rows, cols), *([] if token is None else [token])).reshape(shape)


FIRST_WEIGHTS = ("ev_w_in", "ev_w_q_up", "ev_w_kv_up", "ev_w_out")
LATER_WEIGHTS = tuple(n for n in MATMUL_WEIGHTS if n not in FIRST_WEIGHTS)
LAST_GRADS = FIRST_WEIGHTS
EARLIER_GRADS = tuple(n for n in MATMUL_WEIGHTS if n not in LAST_GRADS)


def _my_chip():
    return 2 * lax.axis_index("x") + lax.axis_index("y")


def _gather_first(w):
    small = _pack([w[n] for n in SMALL_SHARDED], 8 * LANES)
    stacked = [n for n in FIRST_WEIGHTS if SHARD_AXIS[n] == w[n].ndim - 1 and w[n].shape[-1] % LANES]
    shards = [w[n].astype(BF16)[None] if n in stacked else w[n].astype(BF16) for n in FIRST_WEIGHTS]
    got = _gather_chips(shards + [small], [0 if n in stacked else SHARD_AXIS[n] for n in FIRST_WEIGHTS] + [0],
                        name="gather_first")
    full = {n: w[n] for n in REPLICATED}
    for n, g in zip(FIRST_WEIGHTS, got[:-1]):
        full[n] = jnp.concatenate([g[q] for q in range(N_CHIPS)], axis=SHARD_AXIS[n]) if n in stacked else g
    per_chip = [_unpack(got[-1][q * small.shape[0]:(q + 1) * small.shape[0]].reshape(-1),
                        [w[n].shape for n in SMALL_SHARDED]) for q in range(N_CHIPS)]
    for i, n in enumerate(SMALL_SHARDED):
        full[n] = jnp.concatenate([per_chip[q][i] for q in range(N_CHIPS)], axis=SHARD_AXIS[n])
    return full


def _gather_later_start(w):
    shards = [w[n].astype(BF16) for n in LATER_WEIGHTS]
    axes = [SHARD_AXIS[n] for n in LATER_WEIGHTS]
    lands = []
    for s, ax in zip(shards, axes):
        shape = tuple(d * (N_CHIPS if a == ax else 1) for a, d in enumerate(s.shape))
        lands.append(lax.dynamic_update_slice_in_dim(lax.empty(shape, s.dtype), s, _my_chip() * s.shape[ax], ax))
    return _split_start(shards, lands, _gather_plan(axes), name="gather_later_start"), _gather_plan(axes)


def _owner_major(g, axis):
    shape = g.shape
    size = shape[axis] // N_CHIPS
    g = jnp.moveaxis(g.reshape(shape[:axis] + (N_CHIPS, size) + shape[axis + 1:]), axis, 0)
    return g.reshape(N_CHIPS, -1, shape[-1] if axis < len(shape) - 1 else size)


def _exchange_start(items, *, name):
    me = _my_chip()
    srcs, lands, where = [], [], []
    for n, layers in enumerate(items):
        land = lax.empty((N_CHIPS, len(layers)) + layers[0].shape[1:], layers[0].dtype)
        for l, a in enumerate(layers):
            own = lax.dynamic_index_in_dim(a, me, 0, keepdims=True)[:, None]
            land = lax.dynamic_update_slice(land, own, (me, l) + (0,) * (a.ndim - 1))
            srcs.append(a)
            where.append((n, l))
        lands.append(land)
    plan = _exchange_plan(where)
    return _split_start(srcs, lands, plan, name=name), plan


def _earlier_items(grads, full_shapes):
    small = [_pack([jnp.split(grads[n].reshape(full_shapes[n]), N_CHIPS, axis=SHARD_AXIS[n])[q]
                    for n in SMALL_SHARDED], 8 * LANES) for q in range(N_CHIPS)]
    return [grads[n] for n in EARLIER_GRADS] + [[jnp.stack(small)]]


def _last_items(big, grads, full_shapes, loss):
    repl = _pack([grads[n].reshape(full_shapes[n]) for n in REPLICATED] + [loss.reshape(1)], 8 * LANES)
    return [big[n] for n in LAST_GRADS] + [[jnp.stack([repl] * N_CHIPS)]]


def kernel(
        x, mem, positions, ev_norm, ev_w_in, ev_pool_w, ev_pool_scale, ev_q_norm, ev_w_q_up, ev_kv_norm,
        ev_w_kv_up, ev_w_out, od_norm, od_w_in, od_conv_w, od_conv_b, od_w_rgate, od_b_rgate, od_w_igate,
        od_b_igate, od_lambda, od_w_out, xa_norm_x, xa_norm_mem, xa_w_q, xa_w_kv, xa_w_o, ffn_norm,
        ffn_w_gate_up, ffn_w_down, final_norm, loss_target, m_ev_norm, m_ev_w_in, m_ev_pool_w, m_ev_pool_scale,
        m_ev_q_norm, m_ev_w_q_up, m_ev_kv_norm, m_ev_w_kv_up, m_ev_w_out, m_od_norm, m_od_w_in, m_od_conv_w,
        m_od_conv_b, m_od_w_rgate, m_od_b_rgate, m_od_w_igate, m_od_b_igate, m_od_lambda, m_od_w_out,
        m_xa_norm_x, m_xa_norm_mem, m_xa_w_q, m_xa_w_kv, m_xa_w_o, m_ffn_norm, m_ffn_w_gate_up, m_ffn_w_down,
        m_final_norm, v_ev_norm, v_ev_w_in, v_ev_pool_w, v_ev_pool_scale, v_ev_q_norm, v_ev_w_q_up,
        v_ev_kv_norm, v_ev_w_kv_up, v_ev_w_out, v_od_norm, v_od_w_in, v_od_conv_w, v_od_conv_b, v_od_w_rgate,
        v_od_b_rgate, v_od_w_igate, v_od_b_igate, v_od_lambda, v_od_w_out, v_xa_norm_x, v_xa_norm_mem, v_xa_w_q,
        v_xa_w_kv, v_xa_w_o, v_ffn_norm, v_ffn_w_gate_up, v_ffn_w_down, v_final_norm):
    given = dict(locals())
    w = {n: given[n] for n in WEIGHTS}
    full_shapes = {n: tuple(d * (N_CHIPS if a == SHARD_AXIS.get(n) else 1) for a, d in enumerate(w[n].shape))
                   for n in WEIGHTS}
    full = _gather_first(w)
    later, later_plan = _gather_later_start(w)
    full["ev_norm"] = full["ev_norm"] + later[4][0:1, 0:1]
    exchange = {}

    def later_weights(after):
        return dict(zip(LATER_WEIGHTS, _split_wait(later, after, later_plan, name="gather_later_wait")))

    def exchange_earlier(grads):
        exchange["handle"], exchange["plan"] = _exchange_start(_earlier_items(grads, full_shapes),
                                                               name="exchange_earlier_start")
        return exchange["handle"][4]

    loss, grad_x, big, grads = _local_step(x[0], mem[0], positions[0], loss_target[0], full, later_weights,
                                           exchange_earlier)
    earlier = EARLIER_GRADS + ("small",)
    got = dict(zip(earlier, _split_wait(exchange["handle"], grad_x, exchange["plan"], name="exchange_earlier_wait")))
    last, last_plan = _exchange_start(_last_items(big, grads, full_shapes, loss), name="exchange_last_start")
    sums = {n: _sum_slots(got[n], token=last[4] if i == 0 else None, name=f"sum_chips_{n}")
            for i, n in enumerate(earlier)}
    got = dict(zip(LAST_GRADS + ("replicated",),
                   _split_wait(last, sums[earlier[-1]], last_plan, name="exchange_last_wait")))
    sums.update({n: _sum_slots(got[n], name=f"sum_chips_{n}") for n in got})
    mine = [sums[n] for n in MATMUL_WEIGHTS + ("small", "replicated")]
    other = _exchange_sibling(mine, name="exchange_sibling")
    out = {}
    for i, n in enumerate(MATMUL_WEIGHTS):
        out[n] = _adamw(w[n], mine[i].reshape(w[n].shape), other[i].reshape(w[n].shape), given["m_" + n],
                        given["v_" + n], name=f"adamw_{n}")
    for i, group in ((len(MATMUL_WEIGHTS), SMALL_SHARDED), (len(MATMUL_WEIGHTS) + 1, REPLICATED)):
        spare = [jnp.zeros((1,), F32)] if group is REPLICATED else []
        packed = [_pack([given[pre + n] for n in group] + spare, 8 * LANES) for pre in ("", "m_", "v_")]
        res = _adamw(packed[0], mine[i].reshape(packed[0].shape), other[i].reshape(packed[0].shape), packed[1],
                     packed[2], name=f"adamw_group{i}")
        shapes = [w[n].shape for n in group] + [(1,)] * len(spare)
        for j, arrs in enumerate(zip(*[_unpack(r.reshape(-1), shapes) for r in res])):
            if j < len(group):
                out[group[j]] = list(arrs)
            else:
                loss = arrs[0][0]
    return (loss, grad_x[None], *[out[n][k] for k in range(4) for n in WEIGHTS])
```

```python
import functools
import math

import jax
import jax.numpy as jnp
from jax import lax
from jax.experimental import pallas as pl
from jax.experimental.pallas import tpu as pltpu

F32 = jnp.float32
BF16 = jnp.bfloat16

D_MODEL = 1024
POOL_DIM = 512
POOL_WINDOWS = (2, 4, 8, 16)
POOL_GROUP = 128
MLA_HEADS = 8
QK_NOPE = 64
QK_ROPE = 32
QK_DIM = QK_NOPE + QK_ROPE
V_HEAD = 64
HEAD_PAD = 128
Q_RANK = 256
KV_RANK = 128
ROPE_BASE = 10000.0
LRU_HEADS = 4
LRU_HEAD_DIM = 256
CONV_WIDTH = 4
LRU_C = 8.0
MEM_HEADS = 4
MEM_HEAD_DIM = 256
D_FF = 2816
RMS_EPS = 1e-6
NEG_INF = -1e30

ADAM_LR = 0.001
ADAM_B1 = 0.9
ADAM_B2 = 0.999
ADAM_EPS = 1e-08
ADAM_WD = 0.01
ADAM_STEP = 10

N_CHIPS = 4
LANES = 128
VMEM_LIMIT = 56 * 1024 * 1024
MESH = pl.DeviceIdType.MESH
ANY = pl.BlockSpec(memory_space=pl.ANY)
MIX_DIM = POOL_DIM + MLA_HEADS * HEAD_PAD

NN = (((1,), (0,)), ((), ()))
NT = (((1,), (1,)), ((), ()))
TN = (((0,), (0,)), ((), ()))


def _cp(n):
    return pltpu.CompilerParams(dimension_semantics=("arbitrary",) * n, vmem_limit_bytes=VMEM_LIMIT)


def _dot(a, b, dims=NN):
    return lax.dot_general(a, b, dims, preferred_element_type=F32)


def _row_tile(S):
    return 1024 if S % 1024 == 0 else min(S, 512)


def _rows(ts, w, cb=0):
    return pl.BlockSpec((ts, w), lambda i: (i, cb))


def _const(shape):
    return pl.BlockSpec(shape, lambda i: (0,) * len(shape))


MM_VMEM_BUDGET = 40 * 1024 * 1024


def _mm(a, bs, epi, outs, *, tn, nj, nt=False, also=None, extras=(), rows=(), sums=(), a_cb=0, k=None, tm=None,
        name):
    M = a.shape[0]
    k = k or a.shape[1]
    nb, ne, nr, no = len(bs), len(extras), len(rows), len(outs)
    lhs = [(a, k, a_cb, b) for b in bs[:1]] + ([(also[0], also[0].shape[1], 0, also[1])] if also else [])
    if tm is None:
        per_row = 2 * (sum(kk * x.dtype.itemsize for x, kk, _, _ in lhs)
                       + sum(e.dtype.itemsize for e, _ in extras) * tn
                       + sum(jnp.dtype(dt).itemsize for _, dt, _ in outs) * tn) + nb * tn * 4
        weights = (1 if nj == 1 else 2) * (sum(b.dtype.itemsize for b, _, _ in bs) * k
                                           + (also[1][0].dtype.itemsize * lhs[-1][1] if also else 0)) * tn
        tm = 1024 if M % 1024 == 0 and 1024 * per_row + weights <= MM_VMEM_BUDGET else min(M, 512)
    dims = NT if nt else NN
    assert not sums or nj == 1
    na = 2 if also else 0

    def body(*refs):
        av = refs[0][...].astype(BF16)
        accs = [_dot(av, r[...].astype(BF16), dims) for r in refs[1:1 + nb]]
        if also:
            accs[0] = accs[0] + _dot(refs[1 + nb][...].astype(BF16), refs[2 + nb][...].astype(BF16), dims)
        refs = refs[:1 + nb] + refs[1 + nb + na:]
        vals = epi(accs, [r[...] for r in refs[1 + nb:1 + nb + ne + nr]])
        outs_refs = refs[1 + nb + ne + nr:]
        for o, v in zip(outs_refs[:no], vals[:no]):
            o[...] = v.astype(o.dtype)
        if sums:
            @pl.when(pl.program_id(1) == 0)
            def _():
                for o in outs_refs[no:]:
                    o[...] = jnp.zeros_like(o)

            for o, v in zip(outs_refs[no:], vals[no:]):
                o[...] += v

    in_specs = [pl.BlockSpec((tm, k), lambda j, i: (i, a_cb))]
    weights = [(k, rb, cb) for (_, rb, cb) in bs]
    if also:
        in_specs_also = pl.BlockSpec((tm, lhs[-1][1]), lambda j, i: (i, 0))
        weights.append((lhs[-1][1], also[1][1], also[1][2]))
    for n, (kk, rb, cb) in enumerate(weights):
        if also and n == nb:
            in_specs.append(in_specs_also)
        mode = dict(pipeline_mode=pl.Buffered(1)) if nj == 1 else {}
        if nt:
            in_specs.append(pl.BlockSpec((tn, kk), lambda j, i, rb=rb, cb=cb: (rb + j, cb), **mode))
        else:
            in_specs.append(pl.BlockSpec((kk, tn), lambda j, i, rb=rb, cb=cb: (rb, cb + j), **mode))
    for (_, cb) in extras:
        in_specs.append(pl.BlockSpec((tm, tn), lambda j, i, cb=cb: (i, cb + j)))
    in_specs += [pl.BlockSpec((1, tn), lambda j, i: (0, 0))] * nr
    out_specs = [pl.BlockSpec((tm, tn), lambda j, i, cb=cb: (i, cb + j)) for (_, _, cb) in outs]
    out_specs += [pl.BlockSpec((1, w), lambda j, i: (0, 0)) for w in sums]
    res = pl.pallas_call(
        body, grid=(nj, M // tm), in_specs=in_specs, out_specs=out_specs,
        out_shape=[jax.ShapeDtypeStruct((M, n), dt) for (n, dt, _) in outs]
        + [jax.ShapeDtypeStruct((1, w), F32) for w in sums],
        compiler_params=_cp(2), name=name,
    )(a, *[b for (b, _, _) in bs], *([also[0], also[1][0]] if also else []), *[e for (e, _) in extras], *rows)
    return res


def _first(accs, extras):
    return [accs[0]]


def _add_res(accs, extras):
    return [accs[0] + extras[0].astype(F32)]


def _norm_bwd_epilogue(partials):
    def epi(accs, vals):
        dh = accs[0]
        for part in vals[:partials]:
            dh = dh + part.astype(F32)
        x, res, g = vals[partials:partials + 3]
        r = lax.rsqrt(jnp.mean(x * x, axis=-1, keepdims=True) + RMS_EPS)
        n = x * r
        dn = dh * g
        return [r * (dn - n * jnp.mean(dn * n, axis=-1, keepdims=True)) + res, jnp.sum(dh * n, axis=0, keepdims=True)]

    return epi


TN_VMEM_BUDGET = 36 * 1024 * 1024


def _contraction_rows(S, row_bytes, out_elems):
    ts = min(S, 2048)
    while ts > 512 and 2 * (ts * row_bytes + out_elems * 4) > TN_VMEM_BUDGET:
        ts //= 2
    return ts


def _mm_tn(a, b, *, ka=None, a_cb=0, nb=None, b_cb=0, tk=None, tn=None, ts=None, name):
    S = a.shape[0]
    ka = ka or a.shape[1]
    nb = nb or b.shape[1]
    tk = tk or ka
    tn = tn or nb
    ts = ts or _contraction_rows(S, tk * a.dtype.itemsize + tn * b.dtype.itemsize, tk * tn)
    a0, b0 = a_cb * (ka // tk), b_cb * (nb // tn)

    def body(a_ref, b_ref, o_ref):
        @pl.when(pl.program_id(2) == 0)
        def _():
            o_ref[...] = jnp.zeros_like(o_ref)

        o_ref[...] += _dot(a_ref[...].astype(BF16), b_ref[...].astype(BF16), TN)

    return pl.pallas_call(
        body, grid=(ka // tk, nb // tn, S // ts),
        in_specs=[pl.BlockSpec((ts, tk), lambda p, q, s: (s, a0 + p)),
                  pl.BlockSpec((ts, tn), lambda p, q, s: (s, b0 + q))],
        out_specs=pl.BlockSpec((tk, tn), lambda p, q, s: (p, q)),
        out_shape=jax.ShapeDtypeStruct((ka, nb), F32), compiler_params=_cp(3), name=name,
    )(a, b)


def _mm_tn_owners(a, bs, *, name):
    S, ka = a.shape
    nb = sum(b.shape[1] for b in bs)
    tn = nb // N_CHIPS
    ts = _contraction_rows(S, ka * a.dtype.itemsize + len(bs) * tn * bs[0].dtype.itemsize, ka * tn)
    per = N_CHIPS // len(bs)

    def body(a_ref, *refs):
        o_ref = refs[-1]
        q = pl.program_id(0)

        @pl.when(pl.program_id(1) == 0)
        def _():
            o_ref[...] = jnp.zeros_like(o_ref)

        av = a_ref[...].astype(BF16)
        for n, b_ref in enumerate(refs[:-1]):
            @pl.when(q // per == n)
            def _():
                o_ref[0] += _dot(av, b_ref[...].astype(BF16), TN)

    in_specs = [pl.BlockSpec((ts, ka), lambda q, s: (s, 0))]
    for n in range(len(bs)):
        in_specs.append(pl.BlockSpec((ts, tn), lambda q, s, n=n: (jnp.where(q // per == n, s, 0),
                                                                  jnp.clip(q - n * per, 0, per - 1))))
    return pl.pallas_call(
        body, grid=(N_CHIPS, S // ts), in_specs=in_specs,
        out_specs=pl.BlockSpec((1, ka, tn), lambda q, s: (q, 0, 0)),
        out_shape=jax.ShapeDtypeStruct((N_CHIPS, ka, tn), F32), compiler_params=_cp(2), name=name,
    )(a, *bs)


def _mm_tn_grouped(a, b, groups, w, *, name):
    S = a.shape[0]
    ts = _contraction_rows(S, w * (a.dtype.itemsize + b.dtype.itemsize), w * w)

    def body(a_ref, b_ref, o_ref):
        @pl.when(pl.program_id(1) == 0)
        def _():
            o_ref[...] = jnp.zeros_like(o_ref)

        o_ref[0] += _dot(a_ref[...].astype(BF16), b_ref[...].astype(BF16), TN)

    return pl.pallas_call(
        body, grid=(groups, S // ts),
        in_specs=[pl.BlockSpec((ts, w), lambda g, s: (s, g)), pl.BlockSpec((ts, w), lambda g, s: (s, g))],
        out_specs=pl.BlockSpec((1, w, w), lambda g, s: (g, 0, 0)),
        out_shape=jax.ShapeDtypeStruct((groups, w, w), F32), compiler_params=_cp(2), name=name,
    )(a, b)


def _rms(x, g, *, cb=0, w=None, ts=None, name):
    S = x.shape[0]
    w = w or x.shape[1]
    ts = ts or _row_tile(S)

    def body(x_ref, g_ref, o_ref):
        xv = x_ref[...].astype(F32)
        r = lax.rsqrt(jnp.mean(xv * xv, axis=-1, keepdims=True) + RMS_EPS)
        o_ref[...] = (xv * r * g_ref[...]).astype(o_ref.dtype)

    return pl.pallas_call(
        body, grid=(S // ts,), in_specs=[_rows(ts, w, cb), _const((1, w))], out_specs=_rows(ts, w),
        out_shape=jax.ShapeDtypeStruct((S, w), BF16), compiler_params=_cp(1), name=name,
    )(x, g.reshape(1, w))


def _rms_bwd(x, g, dy, *, cb=0, w=None, res=None, out_dtype=F32, ts=None, name):
    S = x.shape[0]
    w = w or x.shape[1]
    ts = ts or min(S, 512)
    has_res = res is not None

    def body(*refs):
        x_ref, g_ref, dy_ref = refs[:3]
        dx_ref, dg_ref = refs[-2:]
        xv = x_ref[...].astype(F32)
        r = lax.rsqrt(jnp.mean(xv * xv, axis=-1, keepdims=True) + RMS_EPS)
        n = xv * r
        dyv = dy_ref[...].astype(F32)
        dn = dyv * g_ref[...]
        dx = r * (dn - n * jnp.mean(dn * n, axis=-1, keepdims=True))
        if has_res:
            dx = dx + refs[3][...].astype(F32)
        dx_ref[...] = dx.astype(dx_ref.dtype)

        @pl.when(pl.program_id(0) == 0)
        def _():
            dg_ref[...] = jnp.zeros_like(dg_ref)

        dg_ref[...] += jnp.sum(dyv * n, axis=0, keepdims=True)

    ins = [x, g.reshape(1, w), dy] + ([res] if has_res else [])
    in_specs = [_rows(ts, w, cb), _const((1, w)), _rows(ts, w)] + ([_rows(ts, w)] if has_res else [])
    return pl.pallas_call(
        body, grid=(S // ts,), in_specs=in_specs, out_specs=[_rows(ts, w), _const((1, w))],
        out_shape=[jax.ShapeDtypeStruct((S, w), out_dtype), jax.ShapeDtypeStruct((1, w), F32)],
        compiler_params=_cp(1), name=name,
    )(*ins)


HALO = 16


def _pool_counts(i, ts, rows, first_row):
    t = i * ts + first_row + lax.broadcasted_iota(jnp.int32, (rows, 1), 0)
    return [jnp.minimum(t + 1, w).astype(F32) for w in POOL_WINDOWS]


def _pool_fwd(z, pool_w, pool_scale, *, name):
    S = z.shape[0]
    ts = min(S, 512)
    nh = ts // HALO

    def body(u_ref, halo_ref, w_ref, sc_ref, y_ref, p_ref):
        i = pl.program_id(0)
        u = u_ref[...]
        halo = jnp.where(i > 0, halo_ref[...], 0.0)
        xe = jnp.concatenate([halo, u], axis=0)
        sums = []
        s = xe
        for sh in (1, 2, 4, 8):
            s = s + pltpu.roll(s, sh, 0)
            sums.append(s)
        cnts = _pool_counts(i, ts, ts, 0)
        for g in range(4):
            lo, hi = g * POOL_GROUP, (g + 1) * POOL_GROUP
            pooled = (sums[g][HALO:, lo:hi] / cnts[g] - u[:, lo:hi]).astype(BF16)
            p_ref[:, lo:hi] = pooled
            y_ref[:, lo:hi] = (_dot(pooled, w_ref[g]) * sc_ref[:, lo:hi]).astype(y_ref.dtype)

    return pl.pallas_call(
        body, grid=(S // ts,),
        in_specs=[_rows(ts, POOL_DIM), pl.BlockSpec((HALO, POOL_DIM), lambda i: (jnp.maximum(i * nh - 1, 0), 0)),
                  _const((4, POOL_GROUP, POOL_GROUP)), _const((1, POOL_DIM))],
        out_specs=[_rows(ts, POOL_DIM), _rows(ts, POOL_DIM)],
        out_shape=[jax.ShapeDtypeStruct((S, MIX_DIM), BF16), jax.ShapeDtypeStruct((S, POOL_DIM), BF16)],
        compiler_params=_cp(1), name=name,
    )(z, z, pool_w, pool_scale)


def _pool_bwd(dmix, pooled, pool_w, pool_scale, *, name):
    S = dmix.shape[0]
    ts = min(S, 512)
    nh = ts // HALO
    last = S // HALO - 1

    def body(dy_ref, dyh_ref, p_ref, w_ref, sc_ref, du_ref, dyp_ref, dsc_ref):
        i = pl.program_id(0)
        dyv = dy_ref[...].astype(F32)
        dyh = jnp.where(i < pl.num_programs(0) - 1, dyh_ref[...].astype(F32), 0.0)
        dye = jnp.concatenate([dyv, dyh], axis=0) * sc_ref[...]
        dypre = dye.astype(BF16)
        dyp_ref[...] = dypre[:ts]
        cnts = _pool_counts(i, ts, ts + HALO, 0)
        n = ts + HALO
        dsc = []
        for g in range(4):
            lo, hi = g * POOL_GROUP, (g + 1) * POOL_GROUP
            ypre = _dot(p_ref[:, lo:hi], w_ref[g])
            dsc.append(jnp.sum(dyv[:, lo:hi] * ypre, axis=0, keepdims=True))
            dpool = _dot(dypre[:, lo:hi], w_ref[g], NT)
            s = dpool / cnts[g]
            for sh in (1, 2, 4, 8)[:g + 1]:
                s = s + pltpu.roll(s, n - sh, 0)
            du_ref[:, lo:hi] = (s[:ts] - dpool[:ts]).astype(du_ref.dtype)

        @pl.when(i == 0)
        def _():
            dsc_ref[...] = jnp.zeros_like(dsc_ref)

        dsc_ref[...] += jnp.concatenate(dsc, axis=1)

    return pl.pallas_call(
        body, grid=(S // ts,),
        in_specs=[_rows(ts, POOL_DIM),
                  pl.BlockSpec((HALO, POOL_DIM), lambda i: (jnp.minimum((i + 1) * nh, last), 0)),
                  _rows(ts, POOL_DIM), _const((4, POOL_GROUP, POOL_GROUP)), _const((1, POOL_DIM))],
        out_specs=[_rows(ts, POOL_DIM), _rows(ts, POOL_DIM), _const((1, POOL_DIM))],
        out_shape=[jax.ShapeDtypeStruct((S, POOL_DIM), BF16)] * 2 + [jax.ShapeDtypeStruct((1, POOL_DIM), F32)],
        compiler_params=_cp(1), name=name,
    )(dmix, dmix, pooled, pool_w, pool_scale)


def _rope_partner(t):
    lane = lax.broadcasted_iota(jnp.int32, t.shape, 1)
    swapped = jnp.where(lane < QK_NOPE + QK_ROPE // 2, pltpu.roll(t, HEAD_PAD - QK_ROPE // 2, 1),
                        pltpu.roll(t, QK_ROPE // 2, 1))
    return jnp.where((lane >= QK_NOPE) & (lane < QK_DIM), swapped, 0.0)


def _rope_fwd(q_pad, k_pad, z, ctab, stab, *, name):
    S = q_pad.shape[0]
    ts = _row_tile(S)

    def body(q_ref, k_ref, kr_ref, c_ref, s_ref, qo_ref, ko_ref):
        c, s = c_ref[...], s_ref[...]
        kr = kr_ref[...]
        kr_rot = kr * c + _rope_partner(kr) * s
        for h in range(MLA_HEADS):
            lo, hi = h * HEAD_PAD, (h + 1) * HEAD_PAD
            q = q_ref[:, lo:hi]
            qo_ref[:, lo:hi] = (q * c + _rope_partner(q) * s).astype(qo_ref.dtype)
            ko_ref[:, lo:hi] = (k_ref[:, lo:hi] + kr_rot).astype(ko_ref.dtype)

    wide = _rows(ts, MLA_HEADS * HEAD_PAD)
    return pl.pallas_call(
        body, grid=(S // ts,),
        in_specs=[wide, wide, _rows(ts, HEAD_PAD, 7), _rows(ts, HEAD_PAD), _rows(ts, HEAD_PAD)],
        out_specs=[wide, wide], out_shape=[jax.ShapeDtypeStruct((S, MLA_HEADS * HEAD_PAD), BF16)] * 2,
        compiler_params=_cp(1), name=name,
    )(q_pad, k_pad, z, ctab, stab)


def _rope_bwd(dq_rot, dk_cat, ctab, stab, *, name):
    S = dq_rot.shape[0]
    ts = min(S, 512)

    def body(dq_ref, dk_ref, c_ref, s_ref, dqo_ref, dkr_ref):
        c, s = c_ref[...], s_ref[...]
        for h in range(MLA_HEADS):
            g = dq_ref[:, h * HEAD_PAD:(h + 1) * HEAD_PAD]
            dqo_ref[:, h * HEAD_PAD:(h + 1) * HEAD_PAD] = (g * c + _rope_partner(g * s)).astype(dqo_ref.dtype)
        dk = dk_ref[...]
        g = dk[:, :HEAD_PAD]
        for h in range(1, MLA_HEADS):
            g = g + dk[:, h * HEAD_PAD:(h + 1) * HEAD_PAD]
        lane = lax.broadcasted_iota(jnp.int32, g.shape, 1)
        on_rope = (lane >= QK_NOPE) & (lane < QK_DIM)
        dkr_ref[...] = jnp.where(on_rope, g * c + _rope_partner(g * s), 0.0).astype(dkr_ref.dtype)

    wide = _rows(ts, MLA_HEADS * HEAD_PAD)
    return pl.pallas_call(
        body, grid=(S // ts,), in_specs=[wide, wide, _rows(ts, HEAD_PAD), _rows(ts, HEAD_PAD)],
        out_specs=[wide, _rows(ts, HEAD_PAD)],
        out_shape=[jax.ShapeDtypeStruct((S, MLA_HEADS * HEAD_PAD), BF16), jax.ShapeDtypeStruct((S, HEAD_PAD), BF16)],
        compiler_params=_cp(1), name=name,
    )(dq_rot, dk_cat, ctab, stab)


ATT_SCALE = QK_DIM ** -0.5
LOG2E = math.log2(math.e)


HEADS_PER_STEP = 2
ATT_COL0 = POOL_DIM // HEAD_PAD


FWD_TILE = 1024


def _stat_rows(col):
    return jnp.broadcast_to(col, (col.shape[0], LANES)).T[0:8]


def _retile_rows(rows, tq):
    heads, n8, t = rows.shape
    if t == tq:
        return rows
    flat = rows.reshape(heads, n8 // 8, 8, t)[:, :, 0].reshape(heads, -1, 1, tq)
    return jnp.broadcast_to(flat, (heads, flat.shape[1], 8, tq)).reshape(heads, -1, tq)


def _flash_fwd(q, k, v, mix, *, name):
    S = q.shape[0]
    tq = FWD_TILE if S % FWD_TILE == 0 else min(S, 512)
    nq = S // tq
    hs = HEADS_PER_STEP
    wide = hs * HEAD_PAD

    def body(q_ref, k_ref, v_ref, mix_ref, o_ref, lse_ref):
        qi = pl.program_id(1)
        qv = [q_ref[:, a * HEAD_PAD:(a + 1) * HEAD_PAD] for a in range(hs)]

        def step(j, carry, masked):
            off = pl.multiple_of(j * tq, tq)
            out = []
            for a in range(hs):
                m, acc = carry[a]
                s = _dot(qv[a], k_ref[pl.ds(off, tq), a * HEAD_PAD:(a + 1) * HEAD_PAD], NT)
                if masked:
                    row = lax.broadcasted_iota(jnp.int32, (tq, tq), 0)
                    col = lax.broadcasted_iota(jnp.int32, (tq, tq), 1)
                    s = jnp.where(col <= row, s, NEG_INF)
                m_new = jnp.maximum(m, jnp.max(s, axis=-1, keepdims=True))
                p = jnp.exp2((s - m_new) * (ATT_SCALE * LOG2E))
                alpha = jnp.exp2((m - m_new) * (ATT_SCALE * LOG2E))
                acc = alpha * acc + _dot(p.astype(BF16), v_ref[pl.ds(off, tq), a * HEAD_PAD:(a + 1) * HEAD_PAD])
                out.append((m_new, acc))
            return tuple(out)

        one = (jnp.full((tq, 1), NEG_INF, F32), jnp.zeros((tq, HEAD_PAD), F32))
        carry = lax.fori_loop(0, qi, lambda j, c: step(j, c, False), (one,) * hs)
        carry = step(qi, carry, True)
        for a in range(hs):
            m, acc = carry[a]
            l = acc[:, V_HEAD:V_HEAD + 1]
            o_ref[:, a * HEAD_PAD:(a + 1) * HEAD_PAD] = (acc / l).astype(o_ref.dtype)
            lse_ref[a] = _stat_rows(m * ATT_SCALE + jnp.log(l))

    blk = pl.BlockSpec((tq, wide), lambda h, i: (i, h))
    full = pl.BlockSpec((S, wide), lambda h, i: (0, h))
    return pl.pallas_call(
        body, grid=(MLA_HEADS // hs, nq), in_specs=[blk, full, full, ANY],
        out_specs=[pl.BlockSpec((tq, wide), lambda h, i: (i, ATT_COL0 // hs + h)),
                   pl.BlockSpec((hs, 8, tq), lambda h, i: (h, i, 0))],
        out_shape=[jax.ShapeDtypeStruct(mix.shape, mix.dtype), jax.ShapeDtypeStruct((MLA_HEADS, nq * 8, tq), F32)],
        input_output_aliases={3: 0}, compiler_params=_cp(2), name=name,
    )(q, k, v, mix)


BWD_TILE = 1024
BWD_HEADS_PER_STEP = 1


def _bwd_tile(S):
    return BWD_TILE if S % BWD_TILE == 0 else min(S, 512)


def _attn_delta(dmix, mix, *, name):
    S = mix.shape[0]
    ts = _bwd_tile(S)
    half = MLA_HEADS // 2
    halves = [_rows(ts, half * HEAD_PAD, 1), _rows(ts, half * HEAD_PAD, 2)]

    def body(do0_ref, do1_ref, o0_ref, o1_ref, d_ref):
        for n, (do_ref, o_ref) in enumerate(((do0_ref, o0_ref), (do1_ref, o1_ref))):
            prod = do_ref[...].astype(F32) * o_ref[...].astype(F32)
            for a in range(half):
                d_ref[n * half + a] = _stat_rows(
                    jnp.sum(prod[:, a * HEAD_PAD:(a + 1) * HEAD_PAD], axis=-1, keepdims=True))

    return pl.pallas_call(
        body, grid=(S // ts,), in_specs=halves + halves,
        out_specs=pl.BlockSpec((MLA_HEADS, 8, ts), lambda i: (0, i, 0)),
        out_shape=jax.ShapeDtypeStruct((MLA_HEADS, (S // ts) * 8, ts), F32), compiler_params=_cp(1), name=name,
    )(dmix, dmix, mix, mix)


def _flash_bwd(q, k, v, dmix, lse_rows, delta_rows, *, name):
    S = q.shape[0]
    tq = _bwd_tile(S)
    nq = S // tq
    hs = BWD_HEADS_PER_STEP
    wide = hs * HEAD_PAD

    def body(q_hbm, do_hbm, lse_ref, dl_ref, k_ref, v_ref, dq_hbm, dk_ref, dv_ref, q_all, do_all, dq_all):
        g, j = pl.program_id(0), pl.program_id(1)
        cols = pl.multiple_of(g * wide, wide)

        @pl.when(j == 0)
        def _():
            pltpu.sync_copy(q_hbm.at[:, pl.ds(cols, wide)], q_all)
            pltpu.sync_copy(do_hbm.at[:, pl.ds(POOL_DIM + cols, wide)], do_all)
            dq_all[...] = jnp.zeros_like(dq_all)

        heads = [slice(a * HEAD_PAD, (a + 1) * HEAD_PAD) for a in range(hs)]
        kv = [k_ref[:, a] for a in heads]
        vv = [v_ref[:, a] for a in heads]

        def step(i, carry, masked):
            off = pl.multiple_of(i * tq, tq)
            off8 = pl.multiple_of(i * 8, 8)
            out = []
            for a in range(hs):
                dk, dv = carry[a]
                qv = q_all[pl.ds(off, tq), heads[a]]
                dov = do_all[pl.ds(off, tq), heads[a]]
                lse2 = lse_ref[a, pl.ds(off8, 8), :][0:1] * LOG2E
                dl = dl_ref[a, pl.ds(off8, 8), :][0:1]
                st = _dot(kv[a], qv, NT)
                if masked:
                    krow = lax.broadcasted_iota(jnp.int32, (tq, tq), 0)
                    qcol = lax.broadcasted_iota(jnp.int32, (tq, tq), 1)
                    st = jnp.where(krow <= qcol, st, NEG_INF)
                pt = jnp.exp2(st * (ATT_SCALE * LOG2E) - lse2)
                dv = dv + _dot(pt.astype(BF16), dov)
                dst = (pt * (_dot(vv[a], dov, NT) - dl)).astype(BF16)
                dk = dk + _dot(dst, qv)
                dq_all[pl.ds(off, tq), heads[a]] += _dot(dst, kv[a], TN)
                out.append((dk, dv))
            return tuple(out)

        zero = jnp.zeros((tq, HEAD_PAD), F32)
        carry = step(j, ((zero, zero),) * hs, True)
        carry = lax.fori_loop(j + 1, nq, lambda i, c: step(i, c, False), carry)
        for a in range(hs):
            dk_ref[:, heads[a]] = carry[a][0] * ATT_SCALE
            dv_ref[:, heads[a]] = carry[a][1]

        @pl.when(j == nq - 1)
        def _():
            dq_all[...] = dq_all[...] * ATT_SCALE
            pltpu.sync_copy(dq_all, dq_hbm.at[:, pl.ds(cols, wide)])

    blk = pl.BlockSpec((tq, wide), lambda g, j: (j, g))
    stat = pl.BlockSpec((hs, nq * 8, tq), lambda g, j: (g, 0, 0))
    full = jax.ShapeDtypeStruct((S, MLA_HEADS * HEAD_PAD), F32)
    return pl.pallas_call(
        body, grid=(MLA_HEADS // hs, nq), in_specs=[ANY, ANY, stat, stat, blk, blk], out_specs=[ANY, blk, blk],
        out_shape=[full, full, full],
        scratch_shapes=[pltpu.VMEM((S, wide), BF16), pltpu.VMEM((S, wide), BF16), pltpu.VMEM((S, wide), F32)],
        compiler_params=_cp(2), name=name,
    )(q, dmix, lse_rows, delta_rows, k, v)


MEM_SCALE = MEM_HEAD_DIM ** -0.5


def _xattn_probs(qh, kh):
    s = _dot(qh, kh, NT) * MEM_SCALE
    e = jnp.exp(s - jnp.max(s, axis=-1, keepdims=True))
    return e / jnp.sum(e, axis=-1, keepdims=True)


def _xa_block_fwd(x, kvm, w_q, w_o, g, *, name):
    S = x.shape[0]
    ts = min(S, 512)
    nm = kvm.shape[0]

    def body(x_ref, kv_ref, wq_ref, wo_ref, g_ref, xo_ref, hx_ref, q_ref, o_ref):
        xv = x_ref[...]
        r = lax.rsqrt(jnp.mean(xv * xv, axis=-1, keepdims=True) + RMS_EPS)
        hx = (xv * r * g_ref[...]).astype(BF16)
        hx_ref[...] = hx
        q = _dot(hx, wq_ref[...]).astype(BF16)
        q_ref[...] = q
        for h in range(MEM_HEADS):
            lo, hi = h * MEM_HEAD_DIM, (h + 1) * MEM_HEAD_DIM
            p = _xattn_probs(q[:, lo:hi], kv_ref[:, lo:hi])
            o_ref[:, lo:hi] = _dot(p.astype(BF16), kv_ref[:, D_MODEL + lo:D_MODEL + hi]).astype(o_ref.dtype)
        xo_ref[...] = xv + _dot(o_ref[...], wo_ref[...])

    square = _const((D_MODEL, D_MODEL))
    act = jax.ShapeDtypeStruct((S, D_MODEL), BF16)
    return pl.pallas_call(
        body, grid=(S // ts,),
        in_specs=[_rows(ts, D_MODEL), _const((nm, 2 * D_MODEL)), square, square, _const((1, D_MODEL))],
        out_specs=[_rows(ts, D_MODEL)] * 4, out_shape=[jax.ShapeDtypeStruct((S, D_MODEL), F32), act, act, act],
        compiler_params=_cp(1), name=name,
    )(x, kvm, w_q, w_o, g.reshape(1, D_MODEL))


def _xa_block_bwd(dxo, x, q, kvm, w_q, w_o, g, *, name):
    S = q.shape[0]
    ts = min(S, 512)
    nm = kvm.shape[0]

    def body(dxo_ref, x_ref, q_ref, kv_ref, wq_ref, wo_ref, g_ref, dx_ref, dq_ref, dkv_ref, dg_ref):
        @pl.when(pl.program_id(0) == 0)
        def _():
            dkv_ref[...] = jnp.zeros_like(dkv_ref)
            dg_ref[...] = jnp.zeros_like(dg_ref)

        dxo = dxo_ref[...]
        do = _dot(dxo.astype(BF16), wo_ref[...], NT).astype(BF16)
        for h in range(MEM_HEADS):
            lo, hi = h * MEM_HEAD_DIM, (h + 1) * MEM_HEAD_DIM
            qh, kh, vh = q_ref[:, lo:hi], kv_ref[:, lo:hi], kv_ref[:, D_MODEL + lo:D_MODEL + hi]
            doh = do[:, lo:hi]
            p = _xattn_probs(qh, kh)
            dp = _dot(doh, vh, NT)
            ds = (p * (dp - jnp.sum(dp * p, axis=-1, keepdims=True)) * MEM_SCALE).astype(BF16)
            dq_ref[:, lo:hi] = _dot(ds, kh).astype(dq_ref.dtype)
            dkv_ref[:, lo:hi] += _dot(ds, qh, TN)
            dkv_ref[:, D_MODEL + lo:D_MODEL + hi] += _dot(p.astype(BF16), doh, TN)
        dx, dg = _norm_bwd_epilogue(0)([_dot(dq_ref[...], wq_ref[...], NT)], [x_ref[...], dxo, g_ref[...]])
        dx_ref[...] = dx
        dg_ref[...] += dg

    square = _const((D_MODEL, D_MODEL))
    return pl.pallas_call(
        body, grid=(S // ts,),
        in_specs=[_rows(ts, D_MODEL), _rows(ts, D_MODEL), _rows(ts, D_MODEL), _const((nm, 2 * D_MODEL)), square,
                  square, _const((1, D_MODEL))],
        out_specs=[_rows(ts, D_MODEL), _rows(ts, D_MODEL), _const((nm, 2 * D_MODEL)), _const((1, D_MODEL))],
        out_shape=[jax.ShapeDtypeStruct((S, D_MODEL), F32), jax.ShapeDtypeStruct((S, D_MODEL), BF16),
                   jax.ShapeDtypeStruct((nm, 2 * D_MODEL), F32), jax.ShapeDtypeStruct((1, D_MODEL), F32)],
        compiler_params=_cp(1), name=name,
    )(dxo, x, q, kvm, w_q, w_o, g.reshape(1, D_MODEL))


CONV_HALO = 8


def _sigmoid(x):
    return 0.5 * jnp.tanh(0.5 * x) + 0.5


def _softplus(x):
    return jnp.maximum(x, 0.0) + jnp.log(1.0 + jnp.exp(-jnp.abs(x)))


def _neg_expm1(x):
    series = -x * (1.0 + x * (1.0 / 2) * (1.0 + x * (1.0 / 3) * (1.0 + x * (1.0 / 4) * (1.0 + x * (1.0 / 5)))))
    return jnp.where(x > -0.05, series, 1.0 - jnp.exp(x))


GELU_C = math.sqrt(2.0 / math.pi)


def _gelu(x):
    return 0.5 * x * (1.0 + jnp.tanh(GELU_C * (x + 0.044715 * x * x * x)))


def _gelu_grad(x):
    t = jnp.tanh(GELU_C * (x + 0.044715 * x * x * x))
    return 0.5 * (1.0 + t) + 0.5 * x * (1.0 - t * t) * GELU_C * (1.0 + 3 * 0.044715 * x * x)


def _lru_gates(xc, wr_ref, br, wi_ref, bi, sp, reset):
    xcb = xc.astype(BF16)
    pr, pi = [], []
    for h in range(LRU_HEADS):
        lo, hi = h * LRU_HEAD_DIM, (h + 1) * LRU_HEAD_DIM
        pr.append(_dot(xcb[:, lo:hi], wr_ref[h]))
        pi.append(_dot(xcb[:, lo:hi], wi_ref[h]))
    r = _sigmoid(jnp.concatenate(pr, axis=1) + br)
    ig = _sigmoid(jnp.concatenate(pi, axis=1) + bi)
    log_a = -LRU_C * r * sp
    a = jnp.where(reset, 0.0, jnp.exp(log_a))
    mult = jnp.where(reset, 1.0, jnp.sqrt(jnp.maximum(_neg_expm1(2.0 * log_a), 0.0)))
    return r, ig, a, mult


SUBLANES = 8


def _compose_groups(a, b, reverse):
    n = a.shape[0]
    row = lax.broadcasted_iota(jnp.int32, a.shape, 0) % SUBLANES
    for s in (1, 2, 4):
        inside = (row < SUBLANES - s) if reverse else (row >= s)
        shift = n - s if reverse else s
        a_s = jnp.where(inside, pltpu.roll(a, shift, 0), 1.0)
        b_s = jnp.where(inside, pltpu.roll(b, shift, 0), 0.0)
        b = a * b_s + b
        a = a * a_s
    return a, b


def _chain_groups(a_buf, h_ref, state, reverse):
    groups = a_buf.shape[0] // SUBLANES

    def group(g, h_in):
        off = pl.multiple_of((groups - 1 - g if reverse else g) * SUBLANES, SUBLANES)
        h = a_buf[pl.ds(off, SUBLANES), :] * h_in + h_ref[pl.ds(off, SUBLANES), :]
        h_ref[pl.ds(off, SUBLANES), :] = h
        return jnp.broadcast_to(h[0:1] if reverse else h[SUBLANES - 1:SUBLANES], h.shape)

    return lax.fori_loop(0, groups, group, state, unroll=4)[0:1]


def _lru_fwd(z, reset, conv_w, conv_b, w_r, b_r, w_i, b_i, lam, *, name):
    S = z.shape[0]
    ts = min(S, 512)
    nh = ts // CONV_HALO
    W = D_MODEL

    def body(gate_ref, xb_ref, halo_ref, rs_ref, cw_ref, cb_ref, wr_ref, br_ref, wi_ref, bi_ref, lam_ref,
             xc_ref, h_ref, y_ref, a_buf, carry):
        i = pl.program_id(0)

        @pl.when(i == 0)
        def _():
            carry[...] = jnp.zeros_like(carry)

        halo = jnp.where(i > 0, halo_ref[...], 0.0)
        xe = jnp.concatenate([halo, xb_ref[...]], axis=0)
        xc = cb_ref[...] + cw_ref[3:4, :] * xe[CONV_HALO:]
        for kk in range(CONV_WIDTH - 1):
            xc = xc + cw_ref[kk:kk + 1, :] * pltpu.roll(xe, CONV_WIDTH - 1 - kk, 0)[CONV_HALO:]
        xc_ref[...] = xc
        reset = rs_ref[...] > 0.5
        _, ig, a, mult = _lru_gates(xc, wr_ref, br_ref[...], wi_ref, bi_ref[...], _softplus(-lam_ref[...]), reset)
        a_buf[...], h_ref[...] = _compose_groups(a, mult * (ig * xc), False)
        carry[...] = _chain_groups(a_buf, h_ref, jnp.broadcast_to(carry[...], (SUBLANES, W)), False)
        y_ref[...] = (_gelu(gate_ref[...]) * h_ref[...]).astype(y_ref.dtype)

    vec = _const((1, W))
    gw = _const((LRU_HEADS, LRU_HEAD_DIM, LRU_HEAD_DIM))
    return pl.pallas_call(
        body, grid=(S // ts,),
        in_specs=[_rows(ts, W, 0), _rows(ts, W, 1),
                  pl.BlockSpec((CONV_HALO, W), lambda i: (jnp.maximum(i * nh - 1, 0), 1)),
                  _rows(ts, 1), _const((CONV_WIDTH, W)), vec, gw, vec, gw, vec, vec],
        out_specs=[_rows(ts, W)] * 3,
        out_shape=[jax.ShapeDtypeStruct((S, W), F32), jax.ShapeDtypeStruct((S, W), F32),
                   jax.ShapeDtypeStruct((S, W), BF16)],
        scratch_shapes=[pltpu.VMEM((ts, W), F32), pltpu.VMEM((1, W), F32)],
        compiler_params=_cp(1), name=name,
    )(z, z, z, reset, conv_w, conv_b, w_r, b_r, w_i, b_i, lam)


def _lru_bwd(dy, z, xc, hseq, reset, w_r, b_r, w_i, b_i, lam, *, name):
    S = z.shape[0]
    ts = min(S, 512)
    nt = S // ts
    nh = ts // CONV_HALO
    W = D_MODEL

    def body(dy_ref, gate_ref, xc_ref, h_ref, hh_ref, rs_ref, wr_ref, br_ref, wi_ref, bi_ref, lam_ref,
             dg_ref, dxc_ref, dpr_ref, dpi_ref, acc_ref, a_buf, dh_buf, carry):
        i = pl.program_id(0)
        tile = nt - 1 - i

        @pl.when(i == 0)
        def _():
            carry[...] = jnp.zeros_like(carry)
            acc_ref[...] = jnp.zeros_like(acc_ref)

        xc = xc_ref[...]
        lam_v = lam_ref[...]
        sp = _softplus(-lam_v)
        reset = rs_ref[...] > 0.5
        r, ig, a, mult = _lru_gates(xc, wr_ref, br_ref[...], wi_ref, bi_ref[...], sp, reset)
        gate = gate_ref[...]
        dyv = dy_ref[...].astype(F32)
        h = h_ref[...]
        dg_ref[...] = (dyv * h * _gelu_grad(gate)).astype(dg_ref.dtype)
        last_row = lax.broadcasted_iota(jnp.int32, a.shape, 0) == ts - 1
        a_buf[...], dh_buf[...] = _compose_groups(jnp.where(last_row, 1.0, pltpu.roll(a, ts - 1, 0)),
                                                  dyv * _gelu(gate), True)
        _chain_groups(a_buf, dh_buf, jnp.broadcast_to(carry[...], (SUBLANES, W)), True)
        dh = dh_buf[...]
        carry[...] = a[0:1] * dh[0:1]
        hh = jnp.where(tile > 0, hh_ref[...], 0.0)
        h_prev = pltpu.roll(jnp.concatenate([hh, h], axis=0), 1, 0)[CONV_HALO:]
        da = dh * h_prev
        bx = ig * xc
        dmult = dh * bx
        dbx = dh * mult
        di = dbx * xc
        dlog_a = jnp.where(reset, 0.0, da * a - dmult * a * a / jnp.maximum(mult, 1e-30))
        dr = dlog_a * (-LRU_C) * sp
        dpre_r = dr * r * (1.0 - r)
        dpre_i = di * ig * (1.0 - ig)
        dprb, dpib = dpre_r.astype(BF16), dpre_i.astype(BF16)
        dpr_ref[...] = dprb
        dpi_ref[...] = dpib
        back = []
        for hd in range(LRU_HEADS):
            lo, hi = hd * LRU_HEAD_DIM, (hd + 1) * LRU_HEAD_DIM
            back.append(_dot(dprb[:, lo:hi], wr_ref[hd], NT) + _dot(dpib[:, lo:hi], wi_ref[hd], NT))
        dxc_ref[...] = dbx * ig + jnp.concatenate(back, axis=1)
        dlam = jnp.sum(dlog_a * (-LRU_C) * r, axis=0, keepdims=True) * (-_sigmoid(-lam_v))
        acc_ref[0:1, :] += jnp.sum(dpre_r, axis=0, keepdims=True)
        acc_ref[1:2, :] += jnp.sum(dpre_i, axis=0, keepdims=True)
        acc_ref[2:3, :] += dlam

    rev = lambda cb: pl.BlockSpec((ts, W), lambda i: (nt - 1 - i, cb))
    vec = _const((1, W))
    gw = _const((LRU_HEADS, LRU_HEAD_DIM, LRU_HEAD_DIM))
    return pl.pallas_call(
        body, grid=(nt,),
        in_specs=[rev(0), rev(0), rev(0), rev(0),
                  pl.BlockSpec((CONV_HALO, W), lambda i: (jnp.maximum((nt - 1 - i) * nh - 1, 0), 0)),
                  pl.BlockSpec((ts, 1), lambda i: (nt - 1 - i, 0)), gw, vec, gw, vec, vec],
        out_specs=[rev(0), rev(0), rev(0), rev(0), _const((8, W))],
        out_shape=[jax.ShapeDtypeStruct((S, W), BF16), jax.ShapeDtypeStruct((S, W), F32),
                   jax.ShapeDtypeStruct((S, W), BF16), jax.ShapeDtypeStruct((S, W), BF16),
                   jax.ShapeDtypeStruct((8, W), F32)],
        scratch_shapes=[pltpu.VMEM((ts, W), F32), pltpu.VMEM((ts, W), F32), pltpu.VMEM((1, W), F32)],
        compiler_params=_cp(1), name=name,
    )(dy, z, xc, hseq, hseq, reset, w_r, b_r, w_i, b_i, lam)


def _conv_bwd(dxc, z, conv_w, *, name):
    S = dxc.shape[0]
    ts = min(S, 512)
    nh = ts // CONV_HALO
    last = S // CONV_HALO - 1
    W = D_MODEL
    n = ts + CONV_HALO

    def body(d_ref, dn_ref, xb_ref, xp_ref, cw_ref, dxb_ref, acc_ref):
        i = pl.program_id(0)

        @pl.when(i == 0)
        def _():
            acc_ref[...] = jnp.zeros_like(acc_ref)

        d = d_ref[...]
        de = jnp.concatenate([d, jnp.where(i < pl.num_programs(0) - 1, dn_ref[...], 0.0)], axis=0)
        xe = jnp.concatenate([jnp.where(i > 0, xp_ref[...], 0.0), xb_ref[...]], axis=0)
        dxb = cw_ref[3:4, :] * d
        acc_ref[3:4, :] += jnp.sum(d * xe[CONV_HALO:], axis=0, keepdims=True)
        for kk in range(CONV_WIDTH - 1):
            sh = CONV_WIDTH - 1 - kk
            dxb = dxb + cw_ref[kk:kk + 1, :] * pltpu.roll(de, n - sh, 0)[:ts]
            acc_ref[kk:kk + 1, :] += jnp.sum(d * pltpu.roll(xe, sh, 0)[CONV_HALO:], axis=0, keepdims=True)
        dxb_ref[...] = dxb.astype(dxb_ref.dtype)
        acc_ref[4:5, :] += jnp.sum(d, axis=0, keepdims=True)

    return pl.pallas_call(
        body, grid=(S // ts,),
        in_specs=[_rows(ts, W), pl.BlockSpec((CONV_HALO, W), lambda i: (jnp.minimum((i + 1) * nh, last), 0)),
                  _rows(ts, W, 1), pl.BlockSpec((CONV_HALO, W), lambda i: (jnp.maximum(i * nh - 1, 0), 1)),
                  _const((CONV_WIDTH, W))],
        out_specs=[_rows(ts, W), _const((8, W))],
        out_shape=[jax.ShapeDtypeStruct((S, W), BF16), jax.ShapeDtypeStruct((8, W), F32)],
        compiler_params=_cp(1), name=name,
    )(dxc, dxc, z, z, conv_w)


def _loss_head(x, g, target, *, name):
    S, D = x.shape
    ts = _row_tile(S)

    def body(x_ref, g_ref, t_ref, dx_ref, dg_ref, l_ref):
        @pl.when(pl.program_id(0) == 0)
        def _():
            dg_ref[...] = jnp.zeros_like(dg_ref)
            l_ref[...] = jnp.zeros_like(l_ref)

        xv = x_ref[...]
        r = lax.rsqrt(jnp.mean(xv * xv, axis=-1, keepdims=True) + RMS_EPS)
        n = xv * r
        err = n * g_ref[...] - t_ref[...]
        l_ref[...] += 0.5 * jnp.sum(jnp.sum(err * err, axis=-1, keepdims=True) * (1.0 / D), axis=0, keepdims=True)
        dy = err * (1.0 / D)
        dn = dy * g_ref[...]
        dx_ref[...] = r * (dn - n * jnp.mean(dn * n, axis=-1, keepdims=True))
        dg_ref[...] += jnp.sum(dy * n, axis=0, keepdims=True)

    return pl.pallas_call(
        body, grid=(S // ts,), in_specs=[_rows(ts, D), _const((1, D)), _rows(ts, D)],
        out_specs=[_rows(ts, D), _const((1, D)), _const((8, LANES))],
        out_shape=[jax.ShapeDtypeStruct((S, D), F32), jax.ShapeDtypeStruct((1, D), F32),
                   jax.ShapeDtypeStruct((8, LANES), F32)],
        compiler_params=_cp(1), name=name,
    )(x, g.reshape(1, D), target)


def _adamw(w, ga, gb, m, v, *, name):
    shape = w.shape
    cols = shape[-1]
    rows = w.size // cols
    br = rows
    if rows * cols * 4 > (1 << 20):
        br = max(d for d in range(8, rows + 1, 8) if rows % d == 0 and d * cols * 4 <= (1 << 20))

    def body(w_ref, ga_ref, gb_ref, m_ref, v_ref, g_ref, d_ref, mo_ref, vo_ref):
        gv = ga_ref[...] + gb_ref[...]
        g_ref[...] = gv
        mn = ADAM_B1 * m_ref[...] + (1.0 - ADAM_B1) * gv
        vn = ADAM_B2 * v_ref[...] + (1.0 - ADAM_B2) * (gv * gv)
        m_hat = mn / (1.0 - ADAM_B1 ** ADAM_STEP)
        v_hat = vn / (1.0 - ADAM_B2 ** ADAM_STEP)
        d_ref[...] = -ADAM_LR * (m_hat / (jnp.sqrt(v_hat) + ADAM_EPS) + ADAM_WD * w_ref[...])
        mo_ref[...] = mn
        vo_ref[...] = vn

    spec = _rows(br, cols)
    outs = pl.pallas_call(
        body, grid=(rows // br,), in_specs=[spec] * 5, out_specs=[spec] * 4,
        out_shape=[jax.ShapeDtypeStruct((rows, cols), F32)] * 4, compiler_params=_cp(1), name=name,
    )(*[t.reshape(rows, cols) for t in (w, ga, gb, m, v)])
    return [o.reshape(shape) for o in outs]


def _pad_heads(w, width):
    k = w.shape[0]
    return jnp.pad(w.reshape(k, MLA_HEADS, width), ((0, 0), (0, 0), (0, HEAD_PAD - width))).reshape(k, -1)


def _unpad_heads(w, width):
    k = w.shape[0]
    return w.reshape(k, MLA_HEADS, HEAD_PAD)[:, :, :width].reshape(k, MLA_HEADS * width)


def _rope_tables(positions):
    inv_freq = ROPE_BASE ** (-jnp.arange(0, QK_ROPE, 2, dtype=F32) / QK_ROPE)
    ang = positions.astype(F32)[:, None] * inv_freq
    cos, sin = jnp.cos(ang), jnp.sin(ang)
    S = positions.shape[0]
    ones, zeros = jnp.ones((S, QK_NOPE), F32), jnp.zeros((S, QK_NOPE), F32)
    ctab = jnp.concatenate([ones, cos, cos, ones[:, :HEAD_PAD - QK_DIM]], axis=1)
    stab = jnp.concatenate([zeros, -sin, sin, zeros[:, :HEAD_PAD - QK_DIM]], axis=1)
    return ctab, stab


def _memory_block(x, mem, W, layer, tag):
    mn = _rms(mem, W["xa_norm_mem"][layer], name=f"{tag}_xa_norm_mem")
    kvm = _mm(mn, [(W["xa_w_kv"][layer], 0, 0)], _first, [(2 * D_MODEL, BF16, 0)], tn=2 * D_MODEL, nj=1,
              name=f"{tag}_xa_kv")[0]
    xo, hx, qx, o = _xa_block_fwd(x, kvm, W["xa_w_q"][layer], W["xa_w_o"][layer], W["xa_norm_x"][layer],
                                  name=f"{tag}_xa_fwd")
    return xo, (x, hx, qx, mn, kvm, o)


def _memory_block_bwd(dxo, mem, W, layer, saved, tag, grads):
    x, hx, qx, mn, kvm, o = saved
    wq, wkv, wo = W["xa_w_q"][layer], W["xa_w_kv"][layer], W["xa_w_o"][layer]
    grads["xa_w_o"][layer] = _owner_major(_mm_tn(o, dxo, name=f"{tag}_xa_dwo"), 0)
    dx, dqx, dkvm, dg = _xa_block_bwd(dxo, x, qx, kvm, wq, wo, W["xa_norm_x"][layer], name=f"{tag}_xa_bwd")
    grads["xa_w_q"][layer] = _owner_major(_mm_tn(hx, dqx, name=f"{tag}_xa_dwq"), 0)
    grads["xa_norm_x"][layer] = dg[0]
    dmn = _mm(dkvm, [(wkv, 0, 0)], _first, [(D_MODEL, F32, 0)], nt=True, tn=D_MODEL, nj=1, name=f"{tag}_xa_dmn")[0]
    grads["xa_w_kv"][layer] = _mm_tn_owners(mn, [dkvm], name=f"{tag}_xa_dwkv")
    _, dgm = _rms_bwd(mem, W["xa_norm_mem"][layer], dmn, name=f"{tag}_xa_norm_mem_bwd")
    grads["xa_norm_mem"][layer] = dgm[0]
    return dx


FF_TN = D_FF // 2

def _silu_mul(accs, extras):
    g, u = accs
    return [g * _sigmoid(g) * u, g, u]


def _silu_mul_bwd(accs, extras):
    da = accs[0]
    g, u = extras[0].astype(F32), extras[1].astype(F32)
    sg = _sigmoid(g)
    return [da * u * sg * (1.0 + g * (1.0 - sg)), da * g * sg]


def _ffn_block(x, W, layer, tag):
    hf = _rms(x, W["ffn_norm"][layer], name=f"{tag}_ffn_norm")
    wgu, wd = W["ffn_w_gate_up"][layer], W["ffn_w_down"][layer]
    act, g, u = _mm(hf, [(wgu, 0, 0), (wgu, 0, 2)], _silu_mul, [(D_FF, BF16, 0)] * 3, tn=FF_TN, nj=2,
                    name=f"{tag}_ffn_up")
    xo = _mm(act, [(wd, 0, 0)], _add_res, [(D_MODEL, F32, 0)], extras=[(x, 0)], tn=D_MODEL, nj=1,
             name=f"{tag}_ffn_down")[0]
    return xo, (x, hf, act, g, u)


def _ffn_block_bwd(dxo, W, layer, saved, tag, grads):
    x, hf, act, g, u = saved
    wgu, wd = W["ffn_w_gate_up"][layer], W["ffn_w_down"][layer]
    dg, du = _mm(dxo, [(wd, 0, 0)], _silu_mul_bwd, [(D_FF, BF16, 0)] * 2, nt=True, extras=[(g, 0), (u, 0)], tn=FF_TN,
                 nj=2, name=f"{tag}_ffn_dact")
    grads["ffn_w_down"][layer] = _owner_major(_mm_tn(act, dxo, tk=FF_TN, name=f"{tag}_ffn_dwd"), 0)
    dx, dgn = _mm(dg, [(wgu, 0, 0)], _norm_bwd_epilogue(0), [(D_MODEL, F32, 0)], nt=True, also=(du, (wgu, 0, 1)),
                  extras=[(x, 0), (dxo, 0)], rows=[W["ffn_norm"][layer].reshape(1, D_MODEL)],
                  sums=[D_MODEL], tn=D_MODEL, nj=1, name=f"{tag}_ffn_dhf")
    grads["ffn_w_gate_up"][layer] = _mm_tn_owners(hf, [dg, du], name=f"{tag}_ffn_dwgu")
    grads["ffn_norm"][layer] = dgn[0]
    return dx


def _keys_and_values(accs, extras):
    k, v = accs
    lane = lax.broadcasted_iota(jnp.int32, v.shape, 1)
    return [k, jnp.where(lane % HEAD_PAD == V_HEAD, 1.0, v)]


def _even_block(x, tabs, W, tag):
    ctab, stab = tabs
    w_in = W["ev_w_in"][0]
    zero = jnp.zeros((D_MODEL, QK_NOPE), BF16)
    w_in_pad = jnp.concatenate([w_in[:, :896], zero, w_in[:, 896:], zero[:, :HEAD_PAD - QK_DIM]], axis=1)
    w_q_pad = _pad_heads(W["ev_w_q_up"][0], QK_DIM)
    wkv = W["ev_w_kv_up"][0].reshape(KV_RANK, MLA_HEADS, QK_NOPE + V_HEAD)
    w_kv_pad = jnp.concatenate([_pad_heads(wkv[:, :, :QK_NOPE].reshape(KV_RANK, -1), QK_NOPE),
                                _pad_heads(wkv[:, :, QK_NOPE:].reshape(KV_RANK, -1), V_HEAD)], axis=1)
    w_out = W["ev_w_out"][0]
    w_att = jnp.pad(w_out[POOL_DIM:].reshape(MLA_HEADS, V_HEAD, D_MODEL), ((0, 0), (0, HEAD_PAD - V_HEAD), (0, 0)))
    w_out_pad = jnp.concatenate([w_out[:POOL_DIM], w_att.reshape(MLA_HEADS * HEAD_PAD, D_MODEL)], axis=0)
    pool_w = W["ev_pool_w"][0].astype(BF16)
    pool_scale = W["ev_pool_scale"]

    h = _rms(x, W["ev_norm"][0], name=f"{tag}_norm")
    z = _mm(h, [(w_in_pad, 0, 0)], _first, [(D_MODEL, F32, 0)], tn=D_MODEL, nj=1, name=f"{tag}_in")[0]
    mix, pooled = _pool_fwd(z, pool_w, pool_scale, name=f"{tag}_pool")
    cqn = _rms(z, W["ev_q_norm"][0], cb=2, w=Q_RANK, name=f"{tag}_q_norm")
    ckvn = _rms(z, W["ev_kv_norm"][0], cb=6, w=KV_RANK, name=f"{tag}_kv_norm")
    q_pad = _mm(cqn, [(w_q_pad, 0, 0)], _first, [(D_MODEL, F32, 0)], tn=D_MODEL, nj=1, name=f"{tag}_q_up")[0]
    k_pad, v_pad = _mm(ckvn, [(w_kv_pad, 0, 0), (w_kv_pad, 0, 1)], _keys_and_values,
                       [(D_MODEL, F32, 0), (D_MODEL, BF16, 0)], tn=D_MODEL, nj=1, name=f"{tag}_kv_up")
    q_rot, k_cat = _rope_fwd(q_pad, k_pad, z, ctab, stab, name=f"{tag}_rope")
    mix, lse = _flash_fwd(q_rot, k_cat, v_pad, mix, name=f"{tag}_attn")
    xo = _mm(mix, [(w_out_pad, 0, 0)], _add_res, [(D_MODEL, F32, 0)], extras=[(x, 0)], tn=D_MODEL, nj=1,
             name=f"{tag}_out")[0]
    saved = (x, h, z, pooled, cqn, ckvn, q_rot, k_cat, v_pad, lse, mix,
             (w_in_pad, w_q_pad, w_kv_pad, w_out_pad, pool_w, pool_scale))
    return xo, saved


def _even_block_bwd(dxo, tabs, W, saved, tag, grads, token=None):
    ctab, stab = tabs
    x, h, z, pooled, cqn, ckvn, q_rot, k_cat, v_pad, lse, mix, wts = saved
    w_in_pad, w_q_pad, w_kv_pad, w_out_pad, pool_w, pool_scale = wts
    if token is not None:
        w_out_pad = w_out_pad + token[0:1, 0:1].astype(BF16)
    dmix = _mm(dxo, [(w_out_pad, 0, 0)], _first, [(MIX_DIM, BF16, 0)], nt=True, tn=MIX_DIM, nj=1,
               name=f"{tag}_dmix")[0]
    dw_out_pad = _mm_tn(mix, dxo, tk=MIX_DIM // 3, name=f"{tag}_dw_out")
    datt = dw_out_pad[POOL_DIM:].reshape(MLA_HEADS, HEAD_PAD, D_MODEL)[:, :V_HEAD].reshape(-1, D_MODEL)
    grads["ev_w_out"] = [_owner_major(jnp.concatenate([dw_out_pad[:POOL_DIM], datt], axis=0), 0)]
    delta = _attn_delta(dmix, mix, name=f"{tag}_delta")
    dq_rot, dk_cat, dv_pad = _flash_bwd(q_rot, k_cat, v_pad, dmix, _retile_rows(lse, delta.shape[2]), delta,
                                        name=f"{tag}_attn_bwd")
    dq_pad, dkr = _rope_bwd(dq_rot, dk_cat, ctab, stab, name=f"{tag}_rope_bwd")
    dw_q_pad = _mm_tn(cqn, dq_pad, name=f"{tag}_dw_q_up")
    grads["ev_w_q_up"] = [_owner_major(_unpad_heads(dw_q_pad, QK_DIM), 1)]
    dcqn = _mm(dq_pad, [(w_q_pad, 0, 0)], _first, [(Q_RANK, F32, 0)], nt=True, tn=Q_RANK, nj=1, name=f"{tag}_dcqn")[0]
    dwk = _unpad_heads(_mm_tn(ckvn, dk_cat, name=f"{tag}_dw_k_up"), QK_NOPE).reshape(KV_RANK, MLA_HEADS, QK_NOPE)
    dwv = _unpad_heads(_mm_tn(ckvn, dv_pad, name=f"{tag}_dw_v_up"), V_HEAD).reshape(KV_RANK, MLA_HEADS, V_HEAD)
    grads["ev_w_kv_up"] = [_owner_major(jnp.concatenate([dwk, dwv], axis=2).reshape(KV_RANK, -1), 1)]
    dckvn = _mm(dk_cat, [(w_kv_pad, 0, 0)], _first, [(KV_RANK, F32, 0)], nt=True, tn=KV_RANK, nj=1,
                name=f"{tag}_dckvn_k")[0]
    dckvn = _mm(dv_pad, [(w_kv_pad, 0, 1)], _add_res, [(KV_RANK, F32, 0)], nt=True, extras=[(dckvn, 0)], tn=KV_RANK,
                nj=1, name=f"{tag}_dckvn_v")[0]
    dcq, dgq = _rms_bwd(z, W["ev_q_norm"][0], dcqn, cb=2, w=Q_RANK, out_dtype=BF16, name=f"{tag}_q_norm_bwd")
    dckv, dgkv = _rms_bwd(z, W["ev_kv_norm"][0], dckvn, cb=6, w=KV_RANK, out_dtype=BF16, name=f"{tag}_kv_norm_bwd")
    grads["ev_q_norm"], grads["ev_kv_norm"] = dgq, dgkv
    du, dypre, dscale = _pool_bwd(dmix, pooled, pool_w, pool_scale, name=f"{tag}_pool_bwd")
    grads["ev_pool_scale"] = dscale
    grads["ev_pool_w"] = _mm_tn_grouped(pooled, dypre, 4, POOL_GROUP, name=f"{tag}_dpool_w")[None]
    dz = jnp.concatenate([du, dcq, dckv, dkr], axis=1)
    dw_in_pad = _mm_tn(h, dz, name=f"{tag}_dw_in")
    grads["ev_w_in"] = [_owner_major(jnp.concatenate([dw_in_pad[:, :896], dw_in_pad[:, 960:992]], axis=1), 0)]
    dx, dgn = _mm(dz, [(w_in_pad, 0, 0)], _norm_bwd_epilogue(0), [(D_MODEL, F32, 0)], nt=True,
                  extras=[(x, 0), (dxo, 0)], rows=[W["ev_norm"][0].reshape(1, D_MODEL)], sums=[D_MODEL], tn=D_MODEL,
                  nj=1, name=f"{tag}_dh")
    grads["ev_norm"] = dgn
    return dx


def _odd_block(x, reset, W, tag):
    h = _rms(x, W["od_norm"][0], name=f"{tag}_norm")
    z = _mm(h, [(W["od_w_in"][0], 0, 0)], _first, [(2 * D_MODEL, F32, 0)], tn=D_MODEL, nj=2, name=f"{tag}_in")[0]
    w_r, w_i = W["od_w_rgate"][0], W["od_w_igate"][0]
    vecs = [W[n].reshape(1, D_MODEL) for n in ("od_conv_b", "od_b_rgate", "od_b_igate", "od_lambda")]
    xc, hseq, y = _lru_fwd(z, reset, W["od_conv_w"][0], vecs[0], w_r, vecs[1], w_i, vecs[2], vecs[3],
                           name=f"{tag}_lru")
    xo = _mm(y, [(W["od_w_out"][0], 0, 0)], _add_res, [(D_MODEL, F32, 0)], extras=[(x, 0)], tn=D_MODEL, nj=1,
             name=f"{tag}_out")[0]
    return xo, (x, h, z, xc, hseq, y, vecs)


def _odd_block_bwd(dxo, reset, W, saved, tag, grads):
    x, h, z, xc, hseq, y, vecs = saved
    w_r, w_i = W["od_w_rgate"][0], W["od_w_igate"][0]
    dy = _mm(dxo, [(W["od_w_out"][0], 0, 0)], _first, [(D_MODEL, F32, 0)], nt=True, tn=D_MODEL, nj=1,
             name=f"{tag}_dy")[0]
    grads["od_w_out"] = [_owner_major(_mm_tn(y, dxo, name=f"{tag}_dw_out"), 0)]
    dgate, dxc, dpr, dpi, acc = _lru_bwd(dy, z, xc, hseq, reset, w_r, vecs[1], w_i, vecs[2], vecs[3],
                                         name=f"{tag}_lru_bwd")
    grads["od_b_rgate"], grads["od_b_igate"], grads["od_lambda"] = acc[0:1], acc[1:2], acc[2:3]
    grads["od_w_rgate"] = [_owner_major(_mm_tn_grouped(xc, dpr, LRU_HEADS, LRU_HEAD_DIM, name=f"{tag}_dw_rgate"), 1)]
    grads["od_w_igate"] = [_owner_major(_mm_tn_grouped(xc, dpi, LRU_HEADS, LRU_HEAD_DIM, name=f"{tag}_dw_igate"), 1)]
    dxb, cacc = _conv_bwd(dxc, z, W["od_conv_w"][0], name=f"{tag}_conv_bwd")
    grads["od_conv_w"], grads["od_conv_b"] = cacc[None, 0:4], cacc[4:5]
    dz = jnp.concatenate([dgate, dxb], axis=1)
    grads["od_w_in"] = [_mm_tn_owners(h, [dz], name=f"{tag}_dw_in")]
    dx, dgn = _mm(dz, [(W["od_w_in"][0], 0, 0)], _norm_bwd_epilogue(0), [(D_MODEL, F32, 0)], nt=True,
                  extras=[(x, 0), (dxo, 0)], rows=[W["od_norm"][0].reshape(1, D_MODEL)], sums=[D_MODEL], tn=D_MODEL,
                  nj=1, name=f"{tag}_dh")
    grads["od_norm"] = dgn
    return dx


def _local_step(x, mem, positions, target, W, later_weights=None, exchange_earlier=None):
    tabs = _rope_tables(positions)
    reset = (positions == 0).astype(F32)[:, None]
    grads = {n: [None, None] for n in ("xa_norm_x", "xa_norm_mem", "xa_w_q", "xa_w_kv", "xa_w_o", "ffn_norm",
                                       "ffn_w_gate_up", "ffn_w_down")}
    x1, s_even = _even_block(x, tabs, W, "l0_even")
    if later_weights is not None:
        W = {**W, **later_weights(x1)}
    x2, s_xa0 = _memory_block(x1, mem, W, 0, "l0")
    x3, s_ff0 = _ffn_block(x2, W, 0, "l0")
    x4, s_odd = _odd_block(x3, reset, W, "l1_odd")
    x5, s_xa1 = _memory_block(x4, mem, W, 1, "l1")
    x6, s_ff1 = _ffn_block(x5, W, 1, "l1")
    d, dgf, loss = _loss_head(x6, W["final_norm"], target, name="loss_head")
    grads["final_norm"] = dgf[0]
    d = _ffn_block_bwd(d, W, 1, s_ff1, "l1", grads)
    d = _memory_block_bwd(d, mem, W, 1, s_xa1, "l1", grads)
    d = _odd_block_bwd(d, reset, W, s_odd, "l1_odd", grads)
    d = _ffn_block_bwd(d, W, 0, s_ff0, "l0", grads)
    d = _memory_block_bwd(d, mem, W, 0, s_xa0, "l0", grads)
    token = exchange_earlier(grads) if exchange_earlier is not None else None
    d = _even_block_bwd(d, tabs, W, s_even, "l0_even", grads, token)
    big = {n: grads.pop(n) for n in MATMUL_WEIGHTS}
    for n, v in grads.items():
        if isinstance(v, list):
            grads[n] = jnp.stack(v)
    return loss[0, 0], d, big, grads


WEIGHTS = ("ev_norm", "ev_w_in", "ev_pool_w", "ev_pool_scale", "ev_q_norm", "ev_w_q_up", "ev_kv_norm", "ev_w_kv_up",
           "ev_w_out", "od_norm", "od_w_in", "od_conv_w", "od_conv_b", "od_w_rgate", "od_b_rgate", "od_w_igate",
           "od_b_igate", "od_lambda", "od_w_out", "xa_norm_x", "xa_norm_mem", "xa_w_q", "xa_w_kv", "xa_w_o",
           "ffn_norm", "ffn_w_gate_up", "ffn_w_down", "final_norm")
SHARD_AXIS = {"ev_w_in": 1, "ev_w_q_up": 2, "ev_w_kv_up": 2, "ev_w_out": 1, "od_norm": 1, "od_w_in": 2,
              "od_conv_w": 2, "od_conv_b": 1, "od_w_rgate": 2, "od_b_rgate": 1, "od_w_igate": 2, "od_b_igate": 1,
              "od_lambda": 1, "od_w_out": 1, "xa_w_q": 1, "xa_w_kv": 2, "xa_w_o": 1, "ffn_w_gate_up": 2,
              "ffn_w_down": 1}
MATMUL_WEIGHTS = ("ev_w_in", "ev_w_q_up", "ev_w_kv_up", "ev_w_out", "od_w_in", "od_w_rgate", "od_w_igate",
                  "od_w_out", "xa_w_q", "xa_w_kv", "xa_w_o", "ffn_w_gate_up", "ffn_w_down")
SMALL_SHARDED = tuple(n for n in WEIGHTS if n in SHARD_AXIS and n not in MATMUL_WEIGHTS)
REPLICATED = tuple(n for n in WEIGHTS if n not in SHARD_AXIS)


def _pack(parts, quantum):
    flat = jnp.concatenate([p.reshape(-1) for p in parts])
    pad = (-flat.shape[0]) % quantum
    return jnp.pad(flat, (0, pad)).reshape(-1, LANES)


def _unpack(flat, shapes):
    out, off = [], 0
    for shape in shapes:
        size = math.prod(shape)
        out.append(flat[off:off + size].reshape(shape))
        off += size
    return out


def _run_copies(local, remote, send_sems, recv_sems, local_sems):
    locals_ = [pltpu.make_async_copy(src, dst, local_sems.at[n]) for n, (src, dst) in enumerate(local)]
    for cp in locals_:
        cp.start()
    sends = [pltpu.make_async_remote_copy(src_ref=src, dst_ref=dst, send_sem=send_sems.at[k, n],
                                          recv_sem=recv_sems.at[k, n], device_id=dev, device_id_type=MESH)
             for (k, n, src, dst, _, dev) in remote]
    for cp in sends:
        cp.start()
    for (k, n, src, _, arrival, dev) in remote:
        pltpu.make_async_remote_copy(src_ref=src, dst_ref=arrival, send_sem=send_sems.at[k, n],
                                     recv_sem=recv_sems.at[k, n], device_id=dev, device_id_type=MESH).wait_recv()
    for cp in sends:
        cp.wait_send()
    for cp in locals_:
        cp.wait()


def _chip_peers(x, y):
    return [(1 - x, y), (x, 1 - y), (1 - x, 1 - y)]


def _owner_block(ref, axis, q):
    size = ref.shape[axis] // N_CHIPS
    idx = [slice(None)] * len(ref.shape)
    idx[axis] = pl.ds(q * size, size)
    return ref.at[tuple(idx)]


def _comm_call(body, ins, out_shapes, n_items, n_peers, *, name):
    return pl.pallas_call(
        body, in_specs=[ANY] * len(ins), out_specs=[ANY] * len(out_shapes), out_shape=out_shapes,
        scratch_shapes=[pltpu.SemaphoreType.DMA((n_peers, n_items)), pltpu.SemaphoreType.DMA((n_peers, n_items)),
                        pltpu.SemaphoreType.DMA((n_items,))],
        name=name,
    )(*ins)


def _gather_chips(shards, axes, *, name):
    n = len(shards)
    full = [jax.ShapeDtypeStruct(tuple(d * (N_CHIPS if a == ax else 1) for a, d in enumerate(s.shape)), s.dtype)
            for s, ax in zip(shards, axes)]

    def body(*refs):
        srcs, dsts = refs[:n], refs[n:2 * n]
        x, y, c = lax.axis_index("x"), lax.axis_index("y"), lax.axis_index("c")
        me = 2 * x + y
        local = [(srcs[i], _owner_block(dsts[i], axes[i], me)) for i in range(n)]
        remote = [(k, i, srcs[i], _owner_block(dsts[i], axes[i], me), _owner_block(dsts[i], axes[i], 2 * px + py),
                   (px, py, c))
                  for k, (px, py) in enumerate(_chip_peers(x, y)) for i in range(n)]
        _run_copies(local, remote, *refs[2 * n:])

    return _comm_call(body, shards, full, n, 3, name=name)


HBM = pl.BlockSpec(memory_space=pltpu.HBM)
SEM = pl.BlockSpec(memory_space=pltpu.SEMAPHORE)
DATAFLOW = pltpu.SideEffectType.DATAFLOW_SIDE_EFFECTING


def _gather_plan(axes):
    return lambda srcs, lands, me, peer: [
        (srcs[i], _owner_block(lands[i], ax, me), _owner_block(lands[i], ax, peer)) for i, ax in enumerate(axes)]


def _exchange_plan(where):
    return lambda srcs, lands, me, peer: [
        (srcs[i].at[peer], lands[n].at[me, l], lands[n].at[peer, l]) for i, (n, l) in enumerate(where)]


def _place_own(srcs, lands, plan, *, name):
    ns, nl = len(srcs), len(lands)
    ncopies = len(plan(list(srcs), list(lands), 0, 0))

    def body(*refs):
        me = 2 * lax.axis_index("x") + lax.axis_index("y")
        own = plan(refs[:ns], refs[ns + nl:ns + 2 * nl], me, me)
        copies = [pltpu.make_async_copy(src, dst, refs[-1].at[i]) for i, (src, dst, _) in enumerate(own)]
        for cp in copies:
            cp.start()
        for cp in copies:
            cp.wait()

    return pl.pallas_call(
        body, in_specs=[ANY] * (ns + nl), out_specs=[ANY] * nl,
        out_shape=[jax.ShapeDtypeStruct(a.shape, a.dtype) for a in lands],
        input_output_aliases={ns + i: i for i in range(nl)},
        scratch_shapes=[pltpu.SemaphoreType.DMA((ncopies,))], name=name,
    )(*srcs, *lands)


def _split_start(srcs, lands, plan, *, name):
    ns, nl = len(srcs), len(lands)
    nsem = 3 * len(plan(list(srcs), list(lands), 0, 0))

    def body(*refs):
        src_refs, land_refs = refs[:ns], refs[ns:ns + nl]
        send_sems, recv_sems = refs[ns + nl:ns + nl + nsem], refs[ns + nl + nsem:ns + nl + 2 * nsem]
        x, y, c = lax.axis_index("x"), lax.axis_index("y"), lax.axis_index("c")
        n = 0
        for px, py in _chip_peers(x, y):
            for src, dst, _ in plan(src_refs, land_refs, 2 * x + y, 2 * px + py):
                pltpu.make_async_remote_copy(src_ref=src, dst_ref=dst, send_sem=send_sems[n], recv_sem=recv_sems[n],
                                             device_id=(px, py, c), device_id_type=MESH).start()
                n += 1
        refs[-1][...] = jnp.zeros_like(refs[-1])

    arrays = list(srcs) + list(lands)
    out = pl.pallas_call(
        body, name=name, in_specs=[HBM] * (ns + nl),
        out_specs=[SEM] * (2 * nsem) + [HBM] * (ns + nl) + [pl.BlockSpec(memory_space=pltpu.VMEM)],
        out_shape=[pltpu.SemaphoreType.DMA(())] * (2 * nsem) + [pltpu.HBM(a.shape, a.dtype) for a in arrays]
        + [jax.ShapeDtypeStruct((8, LANES), F32)],
        input_output_aliases={i: 2 * nsem + i for i in range(ns + nl)},
        compiler_params=pltpu.CompilerParams(has_side_effects=DATAFLOW),
    )(*[pltpu.with_memory_space_constraint(a, pltpu.HBM) for a in arrays])
    sems, rest = out[:2 * nsem], out[2 * nsem:]
    return sems[:nsem], sems[nsem:], rest[:ns], rest[ns:ns + nl], rest[-1]


def _split_wait(handle, after, plan, *, name):
    send_sems, recv_sems, srcs, lands, _ = handle
    ns, nl, nsem = len(srcs), len(lands), len(send_sems)

    def body(*refs):
        src_refs, land_refs = refs[:ns], refs[ns:ns + nl]
        send_refs, recv_refs = refs[ns + nl:ns + nl + nsem], refs[ns + nl + nsem:ns + nl + 2 * nsem]
        x, y, c = lax.axis_index("x"), lax.axis_index("y"), lax.axis_index("c")
        n = 0
        for px, py in _chip_peers(x, y):
            for src, _, arrival in plan(src_refs, land_refs, 2 * x + y, 2 * px + py):
                cp = pltpu.make_async_remote_copy(src_ref=src, dst_ref=arrival, send_sem=send_refs[n],
                                                  recv_sem=recv_refs[n], device_id=(px, py, c), device_id_type=MESH)
                cp.wait_send()
                cp.wait_recv()
                n += 1

    out = pl.pallas_call(
        body, name=name, in_specs=[HBM] * (ns + nl) + [SEM] * (2 * nsem) + [ANY], out_specs=[HBM] * (ns + nl),
        out_shape=[pltpu.HBM(a.shape, a.dtype) for a in list(srcs) + list(lands)],
        input_output_aliases={i: i for i in range(ns + nl)},
        compiler_params=pltpu.CompilerParams(has_side_effects=DATAFLOW),
    )(*srcs, *lands, *send_sems, *recv_sems, after)
    return out[ns:]


def _exchange_sibling(arrays, *, name):
    n = len(arrays)

    def body(*refs):
        x, y, c = lax.axis_index("x"), lax.axis_index("y"), lax.axis_index("c")
        remote = [(0, i, refs[i], refs[n + i], refs[n + i], (x, y, 1 - c)) for i in range(n)]
        _run_copies([], remote, *refs[2 * n:])

    return _comm_call(body, arrays, [jax.ShapeDtypeStruct(a.shape, a.dtype) for a in arrays], n, 1, name=name)


def _sum_slots(r, *, token=None, name):
    shape = r.shape[1:]
    cols = shape[-1]
    rows = math.prod(shape) // cols
    tr = max(d for d in range(8, rows + 1, 8) if rows % d == 0 and d * cols * 16 <= (4 << 20))

    def body(r_ref, *refs):
        total = ((r_ref[0] + r_ref[1]) + r_ref[2]) + r_ref[3]
        refs[-1][...] = total if token is None else total + refs[0][0:1, 0:1]

    in_specs = [pl.BlockSpec((N_CHIPS, tr, cols), lambda i: (0, i, 0))]
    in_specs += [] if token is None else [_const((8, LANES))]
    return pl.pallas_call(
        body, grid=(rows // tr,), in_specs=in_specs,
        out_specs=_rows(tr, cols), out_shape=jax.ShapeDtypeStruct((rows, cols), F32), compiler_params=_cp(1),
        name=name,
    )(r.reshape(N_CHIPS, rows, cols), *([] if token is None else [token])).reshape(shape)


FIRST_WEIGHTS = ("ev_w_in", "ev_w_q_up", "ev_w_kv_up", "ev_w_out")
LATER_WEIGHTS = tuple(n for n in MATMUL_WEIGHTS if n not in FIRST_WEIGHTS)
LAST_GRADS = FIRST_WEIGHTS
EARLIER_GRADS = tuple(n for n in MATMUL_WEIGHTS if n not in LAST_GRADS)


def _my_chip():
    return 2 * lax.axis_index("x") + lax.axis_index("y")


def _gather_first(w):
    small = _pack([w[n] for n in SMALL_SHARDED], 8 * LANES)
    stacked = [n for n in FIRST_WEIGHTS if SHARD_AXIS[n] == w[n].ndim - 1 and w[n].shape[-1] % LANES]
    shards = [w[n].astype(BF16)[None] if n in stacked else w[n].astype(BF16) for n in FIRST_WEIGHTS]
    got = _gather_chips(shards + [small], [0 if n in stacked else SHARD_AXIS[n] for n in FIRST_WEIGHTS] + [0],
                        name="gather_first")
    full = {n: w[n] for n in REPLICATED}
    for n, g in zip(FIRST_WEIGHTS, got[:-1]):
        full[n] = jnp.concatenate([g[q] for q in range(N_CHIPS)], axis=SHARD_AXIS[n]) if n in stacked else g
    per_chip = [_unpack(got[-1][q * small.shape[0]:(q + 1) * small.shape[0]].reshape(-1),
                        [w[n].shape for n in SMALL_SHARDED]) for q in range(N_CHIPS)]
    for i, n in enumerate(SMALL_SHARDED):
        full[n] = jnp.concatenate([per_chip[q][i] for q in range(N_CHIPS)], axis=SHARD_AXIS[n])
    return full


def _gather_later_start(w):
    shards = [w[n].astype(BF16) for n in LATER_WEIGHTS]
    axes = [SHARD_AXIS[n] for n in LATER_WEIGHTS]
    plan = _gather_plan(axes)
    lands = [lax.empty(tuple(d * (N_CHIPS if a == ax else 1) for a, d in enumerate(s.shape)), s.dtype)
             for s, ax in zip(shards, axes)]
    lands = _place_own(shards, lands, plan, name="gather_later_own")
    return _split_start(shards, lands, plan, name="gather_later_start"), plan


def _owner_major(g, axis):
    shape = g.shape
    size = shape[axis] // N_CHIPS
    g = jnp.moveaxis(g.reshape(shape[:axis] + (N_CHIPS, size) + shape[axis + 1:]), axis, 0)
    return g.reshape(N_CHIPS, -1, shape[-1] if axis < len(shape) - 1 else size)


def _exchange_start(items, *, name):
    srcs = [a for layers in items for a in layers]
    plan = _exchange_plan([(n, l) for n, layers in enumerate(items) for l in range(len(layers))])
    lands = [lax.empty((N_CHIPS, len(layers)) + layers[0].shape[1:], layers[0].dtype) for layers in items]
    lands = _place_own(srcs, lands, plan, name=name.replace("start", "own"))
    return _split_start(srcs, lands, plan, name=name), plan


def _earlier_items(grads, full_shapes):
    small = [_pack([jnp.split(grads[n].reshape(full_shapes[n]), N_CHIPS, axis=SHARD_AXIS[n])[q]
                    for n in SMALL_SHARDED], 8 * LANES) for q in range(N_CHIPS)]
    return [grads[n] for n in EARLIER_GRADS] + [[jnp.stack(small)]]


def _last_items(big, grads, full_shapes, loss):
    repl = _pack([grads[n].reshape(full_shapes[n]) for n in REPLICATED] + [loss.reshape(1)], 8 * LANES)
    return [big[n] for n in LAST_GRADS] + [[jnp.stack([repl] * N_CHIPS)]]


def kernel(
        x, mem, positions, ev_norm, ev_w_in, ev_pool_w, ev_pool_scale, ev_q_norm, ev_w_q_up, ev_kv_norm,
        ev_w_kv_up, ev_w_out, od_norm, od_w_in, od_conv_w, od_conv_b, od_w_rgate, od_b_rgate, od_w_igate,
        od_b_igate, od_lambda, od_w_out, xa_norm_x, xa_norm_mem, xa_w_q, xa_w_kv, xa_w_o, ffn_norm,
        ffn_w_gate_up, ffn_w_down, final_norm, loss_target, m_ev_norm, m_ev_w_in, m_ev_pool_w, m_ev_pool_scale,
        m_ev_q_norm, m_ev_w_q_up, m_ev_kv_norm, m_ev_w_kv_up, m_ev_w_out, m_od_norm, m_od_w_in, m_od_conv_w,
        m_od_conv_b, m_od_w_rgate, m_od_b_rgate, m_od_w_igate, m_od_b_igate, m_od_lambda, m_od_w_out,
        m_xa_norm_x, m_xa_norm_mem, m_xa_w_q, m_xa_w_kv, m_xa_w_o, m_ffn_norm, m_ffn_w_gate_up, m_ffn_w_down,
        m_final_norm, v_ev_norm, v_ev_w_in, v_ev_pool_w, v_ev_pool_scale, v_ev_q_norm, v_ev_w_q_up,
        v_ev_kv_norm, v_ev_w_kv_up, v_ev_w_out, v_od_norm, v_od_w_in, v_od_conv_w, v_od_conv_b, v_od_w_rgate,
        v_od_b_rgate, v_od_w_igate, v_od_b_igate, v_od_lambda, v_od_w_out, v_xa_norm_x, v_xa_norm_mem, v_xa_w_q,
        v_xa_w_kv, v_xa_w_o, v_ffn_norm, v_ffn_w_gate_up, v_ffn_w_down, v_final_norm):
    given = dict(locals())
    w = {n: given[n] for n in WEIGHTS}
    full_shapes = {n: tuple(d * (N_CHIPS if a == SHARD_AXIS.get(n) else 1) for a, d in enumerate(w[n].shape))
                   for n in WEIGHTS}
    full = _gather_first(w)
    later, later_plan = _gather_later_start(w)
    full["ev_norm"] = full["ev_norm"] + later[4][0:1, 0:1]
    exchange = {}

    def later_weights(after):
        return dict(zip(LATER_WEIGHTS, _split_wait(later, after, later_plan, name="gather_later_wait")))

    def exchange_earlier(grads):
        exchange["handle"], exchange["plan"] = _exchange_start(_earlier_items(grads, full_shapes),
                                                               name="exchange_earlier_start")
        return exchange["handle"][4]

    loss, grad_x, big, grads = _local_step(x[0], mem[0], positions[0], loss_target[0], full, later_weights,
                                           exchange_earlier)
    earlier = EARLIER_GRADS + ("small",)
    got = dict(zip(earlier, _split_wait(exchange["handle"], grad_x, exchange["plan"], name="exchange_earlier_wait")))
    last, last_plan = _exchange_start(_last_items(big, grads, full_shapes, loss), name="exchange_last_start")
    sums = {n: _sum_slots(got[n], token=last[4] if i == 0 else None, name=f"sum_chips_{n}")
            for i, n in enumerate(earlier)}
    got = dict(zip(LAST_GRADS + ("replicated",),
                   _split_wait(last, sums[earlier[-1]], last_plan, name="exchange_last_wait")))
    sums.update({n: _sum_slots(got[n], name=f"sum_chips_{n}") for n in got})
    mine = [sums[n] for n in MATMUL_WEIGHTS + ("small", "replicated")]
    other = _exchange_sibling(mine, name="exchange_sibling")
    out = {}
    for i, n in enumerate(MATMUL_WEIGHTS):
        out[n] = _adamw(w[n], mine[i].reshape(w[n].shape), other[i].reshape(w[n].shape), given["m_" + n],
                        given["v_" + n], name=f"adamw_{n}")
    for i, group in ((len(MATMUL_WEIGHTS), SMALL_SHARDED), (len(MATMUL_WEIGHTS) + 1, REPLICATED)):
        spare = [jnp.zeros((1,), F32)] if group is REPLICATED else []
        packed = [_pack([given[pre + n] for n in group] + spare, 8 * LANES) for pre in ("", "m_", "v_")]
        res = _adamw(packed[0], mine[i].reshape(packed[0].shape), other[i].reshape(packed[0].shape), packed[1],
                     packed[2], name=f"adamw_group{i}")
        shapes = [w[n].shape for n in group] + [(1,)] * len(spare)
        for j, arrs in enumerate(zip(*[_unpack(r.reshape(-1), shapes) for r in res])):
            if j < len(group):
                out[group[j]] = list(arrs)
            else:
                loss = arrs[0][0]
    return (loss, grad_x[None], *[out[n][k] for k in range(4) for n in WEIGHTS])
```

```python
import functools
import math

import jax
import jax.numpy as jnp
from jax import lax
from jax.experimental import pallas as pl
from jax.experimental.pallas import tpu as pltpu

F32 = jnp.float32
BF16 = jnp.bfloat16

D_MODEL = 1024
POOL_DIM = 512
POOL_WINDOWS = (2, 4, 8, 16)
POOL_GROUP = 128
MLA_HEADS = 8
QK_NOPE = 64
QK_ROPE = 32
QK_DIM = QK_NOPE + QK_ROPE
V_HEAD = 64
HEAD_PAD = 128
Q_RANK = 256
KV_RANK = 128
ROPE_BASE = 10000.0
LRU_HEADS = 4
LRU_HEAD_DIM = 256
CONV_WIDTH = 4
LRU_C = 8.0
MEM_HEADS = 4
MEM_HEAD_DIM = 256
D_FF = 2816
RMS_EPS = 1e-6
NEG_INF = -1e30

ADAM_LR = 0.001
ADAM_B1 = 0.9
ADAM_B2 = 0.999
ADAM_EPS = 1e-08
ADAM_WD = 0.01
ADAM_STEP = 10

N_CHIPS = 4
LANES = 128
VMEM_LIMIT = 56 * 1024 * 1024
MESH = pl.DeviceIdType.MESH
ANY = pl.BlockSpec(memory_space=pl.ANY)
MIX_DIM = POOL_DIM + MLA_HEADS * HEAD_PAD

NN = (((1,), (0,)), ((), ()))
NT = (((1,), (1,)), ((), ()))
TN = (((0,), (0,)), ((), ()))


def _cp(n):
    return pltpu.CompilerParams(dimension_semantics=("arbitrary",) * n, vmem_limit_bytes=VMEM_LIMIT)


def _dot(a, b, dims=NN):
    return lax.dot_general(a, b, dims, preferred_element_type=F32)


def _row_tile(S):
    return 1024 if S % 1024 == 0 else min(S, 512)


def _rows(ts, w, cb=0):
    return pl.BlockSpec((ts, w), lambda i: (i, cb))


def _const(shape):
    return pl.BlockSpec(shape, lambda i: (0,) * len(shape))


MM_VMEM_BUDGET = 40 * 1024 * 1024


def _mm(a, bs, epi, outs, *, tn, nj, nt=False, also=None, extras=(), rows=(), sums=(), a_cb=0, k=None, tm=None,
        name):
    M = a.shape[0]
    k = k or a.shape[1]
    nb, ne, nr, no = len(bs), len(extras), len(rows), len(outs)
    lhs = [(a, k, a_cb, b) for b in bs[:1]] + ([(also[0], also[0].shape[1], 0, also[1])] if also else [])
    if tm is None:
        per_row = 2 * (sum(kk * x.dtype.itemsize for x, kk, _, _ in lhs)
                       + sum(e.dtype.itemsize for e, _ in extras) * tn
                       + sum(jnp.dtype(dt).itemsize for _, dt, _ in outs) * tn) + nb * tn * 4
        weights = (1 if nj == 1 else 2) * (sum(b.dtype.itemsize for b, _, _ in bs) * k
                                           + (also[1][0].dtype.itemsize * lhs[-1][1] if also else 0)) * tn
        tm = 1024 if M % 1024 == 0 and 1024 * per_row + weights <= MM_VMEM_BUDGET else min(M, 512)
    dims = NT if nt else NN
    assert not sums or nj == 1
    na = 2 if also else 0

    def body(*refs):
        av = refs[0][...].astype(BF16)
        accs = [_dot(av, r[...].astype(BF16), dims) for r in refs[1:1 + nb]]
        if also:
            accs[0] = accs[0] + _dot(refs[1 + nb][...].astype(BF16), refs[2 + nb][...].astype(BF16), dims)
        refs = refs[:1 + nb] + refs[1 + nb + na:]
        vals = epi(accs, [r[...] for r in refs[1 + nb:1 + nb + ne + nr]])
        outs_refs = refs[1 + nb + ne + nr:]
        for o, v in zip(outs_refs[:no], vals[:no]):
            o[...] = v.astype(o.dtype)
        if sums:
            @pl.when(pl.program_id(1) == 0)
            def _():
                for o in outs_refs[no:]:
                    o[...] = jnp.zeros_like(o)

            for o, v in zip(outs_refs[no:], vals[no:]):
                o[...] += v

    in_specs = [pl.BlockSpec((tm, k), lambda j, i: (i, a_cb))]
    weights = [(k, rb, cb) for (_, rb, cb) in bs]
    if also:
        in_specs_also = pl.BlockSpec((tm, lhs[-1][1]), lambda j, i: (i, 0))
        weights.append((lhs[-1][1], also[1][1], also[1][2]))
    for n, (kk, rb, cb) in enumerate(weights):
        if also and n == nb:
            in_specs.append(in_specs_also)
        mode = dict(pipeline_mode=pl.Buffered(1)) if nj == 1 else {}
        if nt:
            in_specs.append(pl.BlockSpec((tn, kk), lambda j, i, rb=rb, cb=cb: (rb + j, cb), **mode))
        else:
            in_specs.append(pl.BlockSpec((kk, tn), lambda j, i, rb=rb, cb=cb: (rb, cb + j), **mode))
    for (_, cb) in extras:
        in_specs.append(pl.BlockSpec((tm, tn), lambda j, i, cb=cb: (i, cb + j)))
    in_specs += [pl.BlockSpec((1, tn), lambda j, i: (0, 0))] * nr
    out_specs = [pl.BlockSpec((tm, tn), lambda j, i, cb=cb: (i, cb + j)) for (_, _, cb) in outs]
    out_specs += [pl.BlockSpec((1, w), lambda j, i: (0, 0)) for w in sums]
    res = pl.pallas_call(
        body, grid=(nj, M // tm), in_specs=in_specs, out_specs=out_specs,
        out_shape=[jax.ShapeDtypeStruct((M, n), dt) for (n, dt, _) in outs]
        + [jax.ShapeDtypeStruct((1, w), F32) for w in sums],
        compiler_params=_cp(2), name=name,
    )(a, *[b for (b, _, _) in bs], *([also[0], also[1][0]] if also else []), *[e for (e, _) in extras], *rows)
    return res


def _first(accs, extras):
    return [accs[0]]


def _add_res(accs, extras):
    return [accs[0] + extras[0].astype(F32)]


def _norm_bwd_epilogue(partials):
    def epi(accs, vals):
        dh = accs[0]
        for part in vals[:partials]:
            dh = dh + part.astype(F32)
        x, res, g = vals[partials:partials + 3]
        r = lax.rsqrt(jnp.mean(x * x, axis=-1, keepdims=True) + RMS_EPS)
        n = x * r
        dn = dh * g
        return [r * (dn - n * jnp.mean(dn * n, axis=-1, keepdims=True)) + res, jnp.sum(dh * n, axis=0, keepdims=True)]

    return epi


TN_VMEM_BUDGET = 36 * 1024 * 1024


def _contraction_rows(S, row_bytes, out_elems):
    ts = min(S, 2048)
    while ts > 512 and 2 * (ts * row_bytes + out_elems * 4) > TN_VMEM_BUDGET:
        ts //= 2
    return ts


def _mm_tn(a, b, *, ka=None, a_cb=0, nb=None, b_cb=0, tk=None, tn=None, ts=None, name):
    S = a.shape[0]
    ka = ka or a.shape[1]
    nb = nb or b.shape[1]
    tk = tk or ka
    tn = tn or nb
    ts = ts or _contraction_rows(S, tk * a.dtype.itemsize + tn * b.dtype.itemsize, tk * tn)
    a0, b0 = a_cb * (ka // tk), b_cb * (nb // tn)

    def body(a_ref, b_ref, o_ref):
        @pl.when(pl.program_id(2) == 0)
        def _():
            o_ref[...] = jnp.zeros_like(o_ref)

        o_ref[...] += _dot(a_ref[...].astype(BF16), b_ref[...].astype(BF16), TN)

    return pl.pallas_call(
        body, grid=(ka // tk, nb // tn, S // ts),
        in_specs=[pl.BlockSpec((ts, tk), lambda p, q, s: (s, a0 + p)),
                  pl.BlockSpec((ts, tn), lambda p, q, s: (s, b0 + q))],
        out_specs=pl.BlockSpec((tk, tn), lambda p, q, s: (p, q)),
        out_shape=jax.ShapeDtypeStruct((ka, nb), F32), compiler_params=_cp(3), name=name,
    )(a, b)


def _mm_tn_owners(a, bs, *, name):
    S, ka = a.shape
    nb = sum(b.shape[1] for b in bs)
    tn = nb // N_CHIPS
    ts = _contraction_rows(S, ka * a.dtype.itemsize + len(bs) * tn * bs[0].dtype.itemsize, ka * tn)
    per = N_CHIPS // len(bs)

    def body(a_ref, *refs):
        o_ref = refs[-1]
        q = pl.program_id(0)

        @pl.when(pl.program_id(1) == 0)
        def _():
            o_ref[...] = jnp.zeros_like(o_ref)

        av = a_ref[...].astype(BF16)
        for n, b_ref in enumerate(refs[:-1]):
            @pl.when(q // per == n)
            def _():
                o_ref[0] += _dot(av, b_ref[...].astype(BF16), TN)

    in_specs = [pl.BlockSpec((ts, ka), lambda q, s: (s, 0))]
    for n in range(len(bs)):
        in_specs.append(pl.BlockSpec((ts, tn), lambda q, s, n=n: (jnp.where(q // per == n, s, 0),
                                                                  jnp.clip(q - n * per, 0, per - 1))))
    return pl.pallas_call(
        body, grid=(N_CHIPS, S // ts), in_specs=in_specs,
        out_specs=pl.BlockSpec((1, ka, tn), lambda q, s: (q, 0, 0)),
        out_shape=jax.ShapeDtypeStruct((N_CHIPS, ka, tn), F32), compiler_params=_cp(2), name=name,
    )(a, *bs)


def _mm_tn_grouped(a, b, groups, w, *, name):
    S = a.shape[0]
    ts = _contraction_rows(S, w * (a.dtype.itemsize + b.dtype.itemsize), w * w)

    def body(a_ref, b_ref, o_ref):
        @pl.when(pl.program_id(1) == 0)
        def _():
            o_ref[...] = jnp.zeros_like(o_ref)

        o_ref[0] += _dot(a_ref[...].astype(BF16), b_ref[...].astype(BF16), TN)

    return pl.pallas_call(
        body, grid=(groups, S // ts),
        in_specs=[pl.BlockSpec((ts, w), lambda g, s: (s, g)), pl.BlockSpec((ts, w), lambda g, s: (s, g))],
        out_specs=pl.BlockSpec((1, w, w), lambda g, s: (g, 0, 0)),
        out_shape=jax.ShapeDtypeStruct((groups, w, w), F32), compiler_params=_cp(2), name=name,
    )(a, b)


def _rms(x, g, *, cb=0, w=None, ts=None, name):
    S = x.shape[0]
    w = w or x.shape[1]
    ts = ts or _row_tile(S)

    def body(x_ref, g_ref, o_ref):
        xv = x_ref[...].astype(F32)
        r = lax.rsqrt(jnp.mean(xv * xv, axis=-1, keepdims=True) + RMS_EPS)
        o_ref[...] = (xv * r * g_ref[...]).astype(o_ref.dtype)

    return pl.pallas_call(
        body, grid=(S // ts,), in_specs=[_rows(ts, w, cb), _const((1, w))], out_specs=_rows(ts, w),
        out_shape=jax.ShapeDtypeStruct((S, w), BF16), compiler_params=_cp(1), name=name,
    )(x, g.reshape(1, w))


def _rms_bwd(x, g, dy, *, cb=0, w=None, res=None, out_dtype=F32, ts=None, name):
    S = x.shape[0]
    w = w or x.shape[1]
    ts = ts or min(S, 512)
    has_res = res is not None

    def body(*refs):
        x_ref, g_ref, dy_ref = refs[:3]
        dx_ref, dg_ref = refs[-2:]
        xv = x_ref[...].astype(F32)
        r = lax.rsqrt(jnp.mean(xv * xv, axis=-1, keepdims=True) + RMS_EPS)
        n = xv * r
        dyv = dy_ref[...].astype(F32)
        dn = dyv * g_ref[...]
        dx = r * (dn - n * jnp.mean(dn * n, axis=-1, keepdims=True))
        if has_res:
            dx = dx + refs[3][...].astype(F32)
        dx_ref[...] = dx.astype(dx_ref.dtype)

        @pl.when(pl.program_id(0) == 0)
        def _():
            dg_ref[...] = jnp.zeros_like(dg_ref)

        dg_ref[...] += jnp.sum(dyv * n, axis=0, keepdims=True)

    ins = [x, g.reshape(1, w), dy] + ([res] if has_res else [])
    in_specs = [_rows(ts, w, cb), _const((1, w)), _rows(ts, w)] + ([_rows(ts, w)] if has_res else [])
    return pl.pallas_call(
        body, grid=(S // ts,), in_specs=in_specs, out_specs=[_rows(ts, w), _const((1, w))],
        out_shape=[jax.ShapeDtypeStruct((S, w), out_dtype), jax.ShapeDtypeStruct((1, w), F32)],
        compiler_params=_cp(1), name=name,
    )(*ins)


HALO = 16


def _pool_counts(i, ts, rows, first_row):
    t = i * ts + first_row + lax.broadcasted_iota(jnp.int32, (rows, 1), 0)
    return [jnp.minimum(t + 1, w).astype(F32) for w in POOL_WINDOWS]


def _pool_fwd(z, pool_w, pool_scale, *, name):
    S = z.shape[0]
    ts = min(S, 512)
    nh = ts // HALO

    def body(u_ref, halo_ref, w_ref, sc_ref, y_ref, p_ref):
        i = pl.program_id(0)
        u = u_ref[...]
        halo = jnp.where(i > 0, halo_ref[...], 0.0)
        xe = jnp.concatenate([halo, u], axis=0)
        sums = []
        s = xe
        for sh in (1, 2, 4, 8):
            s = s + pltpu.roll(s, sh, 0)
            sums.append(s)
        cnts = _pool_counts(i, ts, ts, 0)
        for g in range(4):
            lo, hi = g * POOL_GROUP, (g + 1) * POOL_GROUP
            pooled = (sums[g][HALO:, lo:hi] / cnts[g] - u[:, lo:hi]).astype(BF16)
            p_ref[:, lo:hi] = pooled
            y_ref[:, lo:hi] = (_dot(pooled, w_ref[g]) * sc_ref[:, lo:hi]).astype(y_ref.dtype)

    return pl.pallas_call(
        body, grid=(S // ts,),
        in_specs=[_rows(ts, POOL_DIM), pl.BlockSpec((HALO, POOL_DIM), lambda i: (jnp.maximum(i * nh - 1, 0), 0)),
                  _const((4, POOL_GROUP, POOL_GROUP)), _const((1, POOL_DIM))],
        out_specs=[_rows(ts, POOL_DIM), _rows(ts, POOL_DIM)],
        out_shape=[jax.ShapeDtypeStruct((S, MIX_DIM), BF16), jax.ShapeDtypeStruct((S, POOL_DIM), BF16)],
        compiler_params=_cp(1), name=name,
    )(z, z, pool_w, pool_scale)


def _pool_bwd(dmix, pooled, pool_w, pool_scale, *, name):
    S = dmix.shape[0]
    ts = min(S, 512)
    nh = ts // HALO
    last = S // HALO - 1

    def body(dy_ref, dyh_ref, p_ref, w_ref, sc_ref, du_ref, dyp_ref, dsc_ref):
        i = pl.program_id(0)
        dyv = dy_ref[...].astype(F32)
        dyh = jnp.where(i < pl.num_programs(0) - 1, dyh_ref[...].astype(F32), 0.0)
        dye = jnp.concatenate([dyv, dyh], axis=0) * sc_ref[...]
        dypre = dye.astype(BF16)
        dyp_ref[...] = dypre[:ts]
        cnts = _pool_counts(i, ts, ts + HALO, 0)
        n = ts + HALO
        dsc = []
        for g in range(4):
            lo, hi = g * POOL_GROUP, (g + 1) * POOL_GROUP
            ypre = _dot(p_ref[:, lo:hi], w_ref[g])
            dsc.append(jnp.sum(dyv[:, lo:hi] * ypre, axis=0, keepdims=True))
            dpool = _dot(dypre[:, lo:hi], w_ref[g], NT)
            s = dpool / cnts[g]
            for sh in (1, 2, 4, 8)[:g + 1]:
                s = s + pltpu.roll(s, n - sh, 0)
            du_ref[:, lo:hi] = (s[:ts] - dpool[:ts]).astype(du_ref.dtype)

        @pl.when(i == 0)
        def _():
            dsc_ref[...] = jnp.zeros_like(dsc_ref)

        dsc_ref[...] += jnp.concatenate(dsc, axis=1)

    return pl.pallas_call(
        body, grid=(S // ts,),
        in_specs=[_rows(ts, POOL_DIM),
                  pl.BlockSpec((HALO, POOL_DIM), lambda i: (jnp.minimum((i + 1) * nh, last), 0)),
                  _rows(ts, POOL_DIM), _const((4, POOL_GROUP, POOL_GROUP)), _const((1, POOL_DIM))],
        out_specs=[_rows(ts, POOL_DIM), _rows(ts, POOL_DIM), _const((1, POOL_DIM))],
        out_shape=[jax.ShapeDtypeStruct((S, POOL_DIM), BF16)] * 2 + [jax.ShapeDtypeStruct((1, POOL_DIM), F32)],
        compiler_params=_cp(1), name=name,
    )(dmix, dmix, pooled, pool_w, pool_scale)


def _rope_partner(t):
    lane = lax.broadcasted_iota(jnp.int32, t.shape, 1)
    swapped = jnp.where(lane < QK_NOPE + QK_ROPE // 2, pltpu.roll(t, HEAD_PAD - QK_ROPE // 2, 1),
                        pltpu.roll(t, QK_ROPE // 2, 1))
    return jnp.where((lane >= QK_NOPE) & (lane < QK_DIM), swapped, 0.0)


def _rope_fwd(q_pad, k_pad, z, ctab, stab, *, name):
    S = q_pad.shape[0]
    ts = _row_tile(S)

    def body(q_ref, k_ref, kr_ref, c_ref, s_ref, qo_ref, ko_ref):
        c, s = c_ref[...], s_ref[...]
        kr = kr_ref[...]
        kr_rot = kr * c + _rope_partner(kr) * s
        for h in range(MLA_HEADS):
            lo, hi = h * HEAD_PAD, (h + 1) * HEAD_PAD
            q = q_ref[:, lo:hi]
            qo_ref[:, lo:hi] = (q * c + _rope_partner(q) * s).astype(qo_ref.dtype)
            ko_ref[:, lo:hi] = (k_ref[:, lo:hi] + kr_rot).astype(ko_ref.dtype)

    wide = _rows(ts, MLA_HEADS * HEAD_PAD)
    return pl.pallas_call(
        body, grid=(S // ts,),
        in_specs=[wide, wide, _rows(ts, HEAD_PAD, 7), _rows(ts, HEAD_PAD), _rows(ts, HEAD_PAD)],
        out_specs=[wide, wide], out_shape=[jax.ShapeDtypeStruct((S, MLA_HEADS * HEAD_PAD), BF16)] * 2,
        compiler_params=_cp(1), name=name,
    )(q_pad, k_pad, z, ctab, stab)


def _rope_bwd(dq_rot, dk_cat, ctab, stab, *, name):
    S = dq_rot.shape[0]
    ts = min(S, 512)

    def body(dq_ref, dk_ref, c_ref, s_ref, dqo_ref, dkr_ref):
        c, s = c_ref[...], s_ref[...]
        for h in range(MLA_HEADS):
            g = dq_ref[:, h * HEAD_PAD:(h + 1) * HEAD_PAD]
            dqo_ref[:, h * HEAD_PAD:(h + 1) * HEAD_PAD] = (g * c + _rope_partner(g * s)).astype(dqo_ref.dtype)
        dk = dk_ref[...]
        g = dk[:, :HEAD_PAD]
        for h in range(1, MLA_HEADS):
            g = g + dk[:, h * HEAD_PAD:(h + 1) * HEAD_PAD]
        lane = lax.broadcasted_iota(jnp.int32, g.shape, 1)
        on_rope = (lane >= QK_NOPE) & (lane < QK_DIM)
        dkr_ref[...] = jnp.where(on_rope, g * c + _rope_partner(g * s), 0.0).astype(dkr_ref.dtype)

    wide = _rows(ts, MLA_HEADS * HEAD_PAD)
    return pl.pallas_call(
        body, grid=(S // ts,), in_specs=[wide, wide, _rows(ts, HEAD_PAD), _rows(ts, HEAD_PAD)],
        out_specs=[wide, _rows(ts, HEAD_PAD)],
        out_shape=[jax.ShapeDtypeStruct((S, MLA_HEADS * HEAD_PAD), BF16), jax.ShapeDtypeStruct((S, HEAD_PAD), BF16)],
        compiler_params=_cp(1), name=name,
    )(dq_rot, dk_cat, ctab, stab)


ATT_SCALE = QK_DIM ** -0.5
LOG2E = math.log2(math.e)


HEADS_PER_STEP = 2
ATT_COL0 = POOL_DIM // HEAD_PAD


FWD_TILE = 1024


def _stat_rows(col):
    return jnp.broadcast_to(col, (col.shape[0], LANES)).T[0:8]


def _retile_rows(rows, tq):
    heads, n8, t = rows.shape
    if t == tq:
        return rows
    flat = rows.reshape(heads, n8 // 8, 8, t)[:, :, 0].reshape(heads, -1, 1, tq)
    return jnp.broadcast_to(flat, (heads, flat.shape[1], 8, tq)).reshape(heads, -1, tq)


def _flash_fwd(q, k, v, mix, *, name):
    S = q.shape[0]
    tq = FWD_TILE if S % FWD_TILE == 0 else min(S, 512)
    nq = S // tq
    hs = HEADS_PER_STEP
    wide = hs * HEAD_PAD

    def body(q_ref, k_ref, v_ref, mix_ref, o_ref, lse_ref):
        qi = pl.program_id(1)
        qv = [q_ref[:, a * HEAD_PAD:(a + 1) * HEAD_PAD] for a in range(hs)]

        def update(m, acc, s, v):
            m_new = jnp.maximum(m, jnp.max(s, axis=-1, keepdims=True))
            p = jnp.exp2((s - m_new) * (ATT_SCALE * LOG2E))
            alpha = jnp.exp2((m - m_new) * (ATT_SCALE * LOG2E))
            return m_new, alpha * acc + _dot(p.astype(BF16), v)

        def step(j, carry):
            off = pl.multiple_of(j * tq, tq)
            out = []
            for a in range(hs):
                head = slice(a * HEAD_PAD, (a + 1) * HEAD_PAD)
                out.append(update(*carry[a], _dot(qv[a], k_ref[pl.ds(off, tq), head], NT), v_ref[pl.ds(off, tq), head]))
            return tuple(out)

        def diagonal(carry):
            off = pl.multiple_of(qi * tq, tq)
            half = tq // 2
            out = []
            for a in range(hs):
                head = slice(a * HEAD_PAD, (a + 1) * HEAD_PAD)
                m, acc = carry[a]
                parts = []
                for rows, keys in ((slice(0, half), half), (slice(half, tq), tq)):
                    s = _dot(qv[a][rows], k_ref[pl.ds(off, keys), head], NT)
                    row = lax.broadcasted_iota(jnp.int32, s.shape, 0) + rows.start
                    col = lax.broadcasted_iota(jnp.int32, s.shape, 1)
                    parts.append(update(m[rows], acc[rows], jnp.where(col <= row, s, NEG_INF),
                                        v_ref[pl.ds(off, keys), head]))
                out.append((jnp.concatenate([parts[0][0], parts[1][0]], axis=0),
                            jnp.concatenate([parts[0][1], parts[1][1]], axis=0)))
            return tuple(out)

        one = (jnp.full((tq, 1), NEG_INF, F32), jnp.zeros((tq, HEAD_PAD), F32))
        carry = diagonal(lax.fori_loop(0, qi, step, (one,) * hs))
        for a in range(hs):
            m, acc = carry[a]
            l = acc[:, V_HEAD:V_HEAD + 1]
            o_ref[:, a * HEAD_PAD:(a + 1) * HEAD_PAD] = (acc / l).astype(o_ref.dtype)
            lse_ref[a] = _stat_rows(m * ATT_SCALE + jnp.log(l))

    blk = pl.BlockSpec((tq, wide), lambda h, i: (i, h))
    full = pl.BlockSpec((S, wide), lambda h, i: (0, h))
    return pl.pallas_call(
        body, grid=(MLA_HEADS // hs, nq), in_specs=[blk, full, full, ANY],
        out_specs=[pl.BlockSpec((tq, wide), lambda h, i: (i, ATT_COL0 // hs + h)),
                   pl.BlockSpec((hs, 8, tq), lambda h, i: (h, i, 0))],
        out_shape=[jax.ShapeDtypeStruct(mix.shape, mix.dtype), jax.ShapeDtypeStruct((MLA_HEADS, nq * 8, tq), F32)],
        input_output_aliases={3: 0}, compiler_params=_cp(2), name=name,
    )(q, k, v, mix)


BWD_TILE = 1024
BWD_HEADS_PER_STEP = 1


def _bwd_tile(S):
    return BWD_TILE if S % BWD_TILE == 0 else min(S, 512)


def _attn_delta(dmix, mix, *, name):
    S = mix.shape[0]
    ts = _bwd_tile(S)
    half = MLA_HEADS // 2
    halves = [_rows(ts, half * HEAD_PAD, 1), _rows(ts, half * HEAD_PAD, 2)]

    def body(do0_ref, do1_ref, o0_ref, o1_ref, d_ref):
        for n, (do_ref, o_ref) in enumerate(((do0_ref, o0_ref), (do1_ref, o1_ref))):
            prod = do_ref[...].astype(F32) * o_ref[...].astype(F32)
            for a in range(half):
                d_ref[n * half + a] = _stat_rows(
                    jnp.sum(prod[:, a * HEAD_PAD:(a + 1) * HEAD_PAD], axis=-1, keepdims=True))

    return pl.pallas_call(
        body, grid=(S // ts,), in_specs=halves + halves,
        out_specs=pl.BlockSpec((MLA_HEADS, 8, ts), lambda i: (0, i, 0)),
        out_shape=jax.ShapeDtypeStruct((MLA_HEADS, (S // ts) * 8, ts), F32), compiler_params=_cp(1), name=name,
    )(dmix, dmix, mix, mix)


def _flash_bwd(q, k, v, dmix, lse_rows, delta_rows, *, name):
    S = q.shape[0]
    tq = _bwd_tile(S)
    nq = S // tq
    hs = BWD_HEADS_PER_STEP
    wide = hs * HEAD_PAD

    def body(q_hbm, do_hbm, lse_ref, dl_ref, k_ref, v_ref, dq_hbm, dk_ref, dv_ref, q_all, do_all, dq_all):
        g, j = pl.program_id(0), pl.program_id(1)
        cols = pl.multiple_of(g * wide, wide)

        @pl.when(j == 0)
        def _():
            pltpu.sync_copy(q_hbm.at[:, pl.ds(cols, wide)], q_all)
            pltpu.sync_copy(do_hbm.at[:, pl.ds(POOL_DIM + cols, wide)], do_all)
            dq_all[...] = jnp.zeros_like(dq_all)

        heads = [slice(a * HEAD_PAD, (a + 1) * HEAD_PAD) for a in range(hs)]
        kv = [k_ref[:, a] for a in heads]
        vv = [v_ref[:, a] for a in heads]

        def block(a, keys, rows, lse2, dl, first_query):
            qv, dov = q_all[rows, heads[a]], do_all[rows, heads[a]]
            st = _dot(kv[a][:keys], qv, NT)
            if first_query is not None:
                krow = lax.broadcasted_iota(jnp.int32, st.shape, 0)
                qcol = lax.broadcasted_iota(jnp.int32, st.shape, 1) + first_query
                st = jnp.where(krow <= qcol, st, NEG_INF)
            pt = jnp.exp2(st * (ATT_SCALE * LOG2E) - lse2)
            dst = (pt * (_dot(vv[a][:keys], dov, NT) - dl)).astype(BF16)
            dq_all[rows, heads[a]] += _dot(dst, kv[a][:keys], TN)
            return _dot(dst, qv), _dot(pt.astype(BF16), dov)

        def stats(a, i):
            off8 = pl.multiple_of(i * 8, 8)
            return lse_ref[a, pl.ds(off8, 8), :][0:1] * LOG2E, dl_ref[a, pl.ds(off8, 8), :][0:1]

        def step(i, carry):
            rows = pl.ds(pl.multiple_of(i * tq, tq), tq)
            out = []
            for a in range(hs):
                dk, dv = block(a, tq, rows, *stats(a, i), None)
                out.append((carry[a][0] + dk, carry[a][1] + dv))
            return tuple(out)

        def diagonal():
            half = tq // 2
            out = []
            for a in range(hs):
                lse2, dl = stats(a, j)
                off = pl.multiple_of(j * tq, tq)
                dk0, dv0 = block(a, half, pl.ds(off, half), lse2[:, :half], dl[:, :half], 0)
                dk1, dv1 = block(a, tq, pl.ds(pl.multiple_of(off + half, half), half), lse2[:, half:], dl[:, half:], half)
                zero = jnp.zeros((tq - half, HEAD_PAD), F32)
                out.append((dk1 + jnp.concatenate([dk0, zero], axis=0), dv1 + jnp.concatenate([dv0, zero], axis=0)))
            return tuple(out)

        carry = lax.fori_loop(j + 1, nq, step, diagonal())
        for a in range(hs):
            dk_ref[:, heads[a]] = carry[a][0] * ATT_SCALE
            dv_ref[:, heads[a]] = carry[a][1]

        @pl.when(j == nq - 1)
        def _():
            dq_all[...] = dq_all[...] * ATT_SCALE
            pltpu.sync_copy(dq_all, dq_hbm.at[:, pl.ds(cols, wide)])

    blk = pl.BlockSpec((tq, wide), lambda g, j: (j, g))
    stat = pl.BlockSpec((hs, nq * 8, tq), lambda g, j: (g, 0, 0))
    full = jax.ShapeDtypeStruct((S, MLA_HEADS * HEAD_PAD), F32)
    return pl.pallas_call(
        body, grid=(MLA_HEADS // hs, nq), in_specs=[ANY, ANY, stat, stat, blk, blk], out_specs=[ANY, blk, blk],
        out_shape=[full, full, full],
        scratch_shapes=[pltpu.VMEM((S, wide), BF16), pltpu.VMEM((S, wide), BF16), pltpu.VMEM((S, wide), F32)],
        compiler_params=_cp(2), name=name,
    )(q, dmix, lse_rows, delta_rows, k, v)


MEM_SCALE = MEM_HEAD_DIM ** -0.5


def _xattn_probs(qh, kh):
    s = _dot(qh, kh, NT) * MEM_SCALE
    e = jnp.exp(s - jnp.max(s, axis=-1, keepdims=True))
    return e / jnp.sum(e, axis=-1, keepdims=True)


def _xa_block_fwd(x, kvm, w_q, w_o, g, *, name):
    S = x.shape[0]
    ts = min(S, 512)
    nm = kvm.shape[0]

    def body(x_ref, kv_ref, wq_ref, wo_ref, g_ref, xo_ref, hx_ref, q_ref, o_ref):
        xv = x_ref[...]
        r = lax.rsqrt(jnp.mean(xv * xv, axis=-1, keepdims=True) + RMS_EPS)
        hx = (xv * r * g_ref[...]).astype(BF16)
        hx_ref[...] = hx
        q = _dot(hx, wq_ref[...]).astype(BF16)
        q_ref[...] = q
        for h in range(MEM_HEADS):
            lo, hi = h * MEM_HEAD_DIM, (h + 1) * MEM_HEAD_DIM
            p = _xattn_probs(q[:, lo:hi], kv_ref[:, lo:hi])
            o_ref[:, lo:hi] = _dot(p.astype(BF16), kv_ref[:, D_MODEL + lo:D_MODEL + hi]).astype(o_ref.dtype)
        xo_ref[...] = xv + _dot(o_ref[...], wo_ref[...])

    square = _const((D_MODEL, D_MODEL))
    act = jax.ShapeDtypeStruct((S, D_MODEL), BF16)
    return pl.pallas_call(
        body, grid=(S // ts,),
        in_specs=[_rows(ts, D_MODEL), _const((nm, 2 * D_MODEL)), square, square, _const((1, D_MODEL))],
        out_specs=[_rows(ts, D_MODEL)] * 4, out_shape=[jax.ShapeDtypeStruct((S, D_MODEL), F32), act, act, act],
        compiler_params=_cp(1), name=name,
    )(x, kvm, w_q, w_o, g.reshape(1, D_MODEL))


def _xa_block_bwd(dxo, x, q, kvm, w_q, w_o, g, *, name):
    S = q.shape[0]
    ts = min(S, 512)
    nm = kvm.shape[0]

    def body(dxo_ref, x_ref, q_ref, kv_ref, wq_ref, wo_ref, g_ref, dx_ref, dq_ref, dkv_ref, dg_ref):
        @pl.when(pl.program_id(0) == 0)
        def _():
            dkv_ref[...] = jnp.zeros_like(dkv_ref)
            dg_ref[...] = jnp.zeros_like(dg_ref)

        dxo = dxo_ref[...]
        do = _dot(dxo.astype(BF16), wo_ref[...], NT).astype(BF16)
        for h in range(MEM_HEADS):
            lo, hi = h * MEM_HEAD_DIM, (h + 1) * MEM_HEAD_DIM
            qh, kh, vh = q_ref[:, lo:hi], kv_ref[:, lo:hi], kv_ref[:, D_MODEL + lo:D_MODEL + hi]
            doh = do[:, lo:hi]
            p = _xattn_probs(qh, kh)
            dp = _dot(doh, vh, NT)
            ds = (p * (dp - jnp.sum(dp * p, axis=-1, keepdims=True)) * MEM_SCALE).astype(BF16)
            dq_ref[:, lo:hi] = _dot(ds, kh).astype(dq_ref.dtype)
            dkv_ref[:, lo:hi] += _dot(ds, qh, TN)
            dkv_ref[:, D_MODEL + lo:D_MODEL + hi] += _dot(p.astype(BF16), doh, TN)
        dx, dg = _norm_bwd_epilogue(0)([_dot(dq_ref[...], wq_ref[...], NT)], [x_ref[...], dxo, g_ref[...]])
        dx_ref[...] = dx
        dg_ref[...] += dg

    square = _const((D_MODEL, D_MODEL))
    return pl.pallas_call(
        body, grid=(S // ts,),
        in_specs=[_rows(ts, D_MODEL), _rows(ts, D_MODEL), _rows(ts, D_MODEL), _const((nm, 2 * D_MODEL)), square,
                  square, _const((1, D_MODEL))],
        out_specs=[_rows(ts, D_MODEL), _rows(ts, D_MODEL), _const((nm, 2 * D_MODEL)), _const((1, D_MODEL))],
        out_shape=[jax.ShapeDtypeStruct((S, D_MODEL), F32), jax.ShapeDtypeStruct((S, D_MODEL), BF16),
                   jax.ShapeDtypeStruct((nm, 2 * D_MODEL), F32), jax.ShapeDtypeStruct((1, D_MODEL), F32)],
        compiler_params=_cp(1), name=name,
    )(dxo, x, q, kvm, w_q, w_o, g.reshape(1, D_MODEL))


CONV_HALO = 8


def _sigmoid(x):
    return 0.5 * jnp.tanh(0.5 * x) + 0.5


def _softplus(x):
    return jnp.maximum(x, 0.0) + jnp.log(1.0 + jnp.exp(-jnp.abs(x)))


def _neg_expm1(x):
    series = -x * (1.0 + x * (1.0 / 2) * (1.0 + x * (1.0 / 3) * (1.0 + x * (1.0 / 4) * (1.0 + x * (1.0 / 5)))))
    return jnp.where(x > -0.05, series, 1.0 - jnp.exp(x))


GELU_C = math.sqrt(2.0 / math.pi)


def _gelu(x):
    return 0.5 * x * (1.0 + jnp.tanh(GELU_C * (x + 0.044715 * x * x * x)))


def _gelu_grad(x):
    t = jnp.tanh(GELU_C * (x + 0.044715 * x * x * x))
    return 0.5 * (1.0 + t) + 0.5 * x * (1.0 - t * t) * GELU_C * (1.0 + 3 * 0.044715 * x * x)


def _lru_gates(xc, wr_ref, br, wi_ref, bi, sp, reset):
    xcb = xc.astype(BF16)
    pr, pi = [], []
    for h in range(LRU_HEADS):
        lo, hi = h * LRU_HEAD_DIM, (h + 1) * LRU_HEAD_DIM
        pr.append(_dot(xcb[:, lo:hi], wr_ref[h]))
        pi.append(_dot(xcb[:, lo:hi], wi_ref[h]))
    r = _sigmoid(jnp.concatenate(pr, axis=1) + br)
    ig = _sigmoid(jnp.concatenate(pi, axis=1) + bi)
    log_a = -LRU_C * r * sp
    a = jnp.where(reset, 0.0, jnp.exp(log_a))
    mult = jnp.where(reset, 1.0, jnp.sqrt(jnp.maximum(_neg_expm1(2.0 * log_a), 0.0)))
    return r, ig, a, mult


SUBLANES = 8


def _compose_groups(a, b, reverse):
    n = a.shape[0]
    row = lax.broadcasted_iota(jnp.int32, a.shape, 0) % SUBLANES
    for s in (1, 2, 4):
        inside = (row < SUBLANES - s) if reverse else (row >= s)
        shift = n - s if reverse else s
        a_s = jnp.where(inside, pltpu.roll(a, shift, 0), 1.0)
        b_s = jnp.where(inside, pltpu.roll(b, shift, 0), 0.0)
        b = a * b_s + b
        a = a * a_s
    return a, b


def _chain_groups(a_buf, h_ref, state, reverse):
    groups = a_buf.shape[0] // SUBLANES

    def group(g, h_in):
        off = pl.multiple_of((groups - 1 - g if reverse else g) * SUBLANES, SUBLANES)
        h = a_buf[pl.ds(off, SUBLANES), :] * h_in + h_ref[pl.ds(off, SUBLANES), :]
        h_ref[pl.ds(off, SUBLANES), :] = h
        return jnp.broadcast_to(h[0:1] if reverse else h[SUBLANES - 1:SUBLANES], h.shape)

    return lax.fori_loop(0, groups, group, state, unroll=4)[0:1]


def _lru_fwd(z, reset, conv_w, conv_b, w_r, b_r, w_i, b_i, lam, *, name):
    S = z.shape[0]
    ts = min(S, 512)
    nh = ts // CONV_HALO
    W = D_MODEL

    def body(gate_ref, xb_ref, halo_ref, rs_ref, cw_ref, cb_ref, wr_ref, br_ref, wi_ref, bi_ref, lam_ref,
             xc_ref, h_ref, y_ref, a_buf, carry):
        i = pl.program_id(0)

        @pl.when(i == 0)
        def _():
            carry[...] = jnp.zeros_like(carry)

        halo = jnp.where(i > 0, halo_ref[...], 0.0)
        xe = jnp.concatenate([halo, xb_ref[...]], axis=0)
        xc = cb_ref[...] + cw_ref[3:4, :] * xe[CONV_HALO:]
        for kk in range(CONV_WIDTH - 1):
            xc = xc + cw_ref[kk:kk + 1, :] * pltpu.roll(xe, CONV_WIDTH - 1 - kk, 0)[CONV_HALO:]
        xc_ref[...] = xc
        reset = rs_ref[...] > 0.5
        _, ig, a, mult = _lru_gates(xc, wr_ref, br_ref[...], wi_ref, bi_ref[...], _softplus(-lam_ref[...]), reset)
        a_buf[...], h_ref[...] = _compose_groups(a, mult * (ig * xc), False)
        carry[...] = _chain_groups(a_buf, h_ref, jnp.broadcast_to(carry[...], (SUBLANES, W)), False)
        y_ref[...] = (_gelu(gate_ref[...]) * h_ref[...]).astype(y_ref.dtype)

    vec = _const((1, W))
    gw = _const((LRU_HEADS, LRU_HEAD_DIM, LRU_HEAD_DIM))
    return pl.pallas_call(
        body, grid=(S // ts,),
        in_specs=[_rows(ts, W, 0), _rows(ts, W, 1),
                  pl.BlockSpec((CONV_HALO, W), lambda i: (jnp.maximum(i * nh - 1, 0), 1)),
                  _rows(ts, 1), _const((CONV_WIDTH, W)), vec, gw, vec, gw, vec, vec],
        out_specs=[_rows(ts, W)] * 3,
        out_shape=[jax.ShapeDtypeStruct((S, W), F32), jax.ShapeDtypeStruct((S, W), F32),
                   jax.ShapeDtypeStruct((S, W), BF16)],
        scratch_shapes=[pltpu.VMEM((ts, W), F32), pltpu.VMEM((1, W), F32)],
        compiler_params=_cp(1), name=name,
    )(z, z, z, reset, conv_w, conv_b, w_r, b_r, w_i, b_i, lam)


def _lru_bwd(dy, z, xc, hseq, reset, w_r, b_r, w_i, b_i, lam, *, name):
    S = z.shape[0]
    ts = min(S, 512)
    nt = S // ts
    nh = ts // CONV_HALO
    W = D_MODEL

    def body(dy_ref, gate_ref, xc_ref, h_ref, hh_ref, rs_ref, wr_ref, br_ref, wi_ref, bi_ref, lam_ref,
             dg_ref, dxc_ref, dpr_ref, dpi_ref, acc_ref, a_buf, dh_buf, carry):
        i = pl.program_id(0)
        tile = nt - 1 - i

        @pl.when(i == 0)
        def _():
            carry[...] = jnp.zeros_like(carry)
            acc_ref[...] = jnp.zeros_like(acc_ref)

        xc = xc_ref[...]
        lam_v = lam_ref[...]
        sp = _softplus(-lam_v)
        reset = rs_ref[...] > 0.5
        r, ig, a, mult = _lru_gates(xc, wr_ref, br_ref[...], wi_ref, bi_ref[...], sp, reset)
        gate = gate_ref[...]
        dyv = dy_ref[...].astype(F32)
        h = h_ref[...]
        dg_ref[...] = (dyv * h * _gelu_grad(gate)).astype(dg_ref.dtype)
        last_row = lax.broadcasted_iota(jnp.int32, a.shape, 0) == ts - 1
        a_buf[...], dh_buf[...] = _compose_groups(jnp.where(last_row, 1.0, pltpu.roll(a, ts - 1, 0)),
                                                  dyv * _gelu(gate), True)
        _chain_groups(a_buf, dh_buf, jnp.broadcast_to(carry[...], (SUBLANES, W)), True)
        dh = dh_buf[...]
        carry[...] = a[0:1] * dh[0:1]
        hh = jnp.where(tile > 0, hh_ref[...], 0.0)
        h_prev = pltpu.roll(jnp.concatenate([hh, h], axis=0), 1, 0)[CONV_HALO:]
        da = dh * h_prev
        bx = ig * xc
        dmult = dh * bx
        dbx = dh * mult
        di = dbx * xc
        dlog_a = jnp.where(reset, 0.0, da * a - dmult * a * a / jnp.maximum(mult, 1e-30))
        dr = dlog_a * (-LRU_C) * sp
        dpre_r = dr * r * (1.0 - r)
        dpre_i = di * ig * (1.0 - ig)
        dprb, dpib = dpre_r.astype(BF16), dpre_i.astype(BF16)
        dpr_ref[...] = dprb
        dpi_ref[...] = dpib
        back = []
        for hd in range(LRU_HEADS):
            lo, hi = hd * LRU_HEAD_DIM, (hd + 1) * LRU_HEAD_DIM
            back.append(_dot(dprb[:, lo:hi], wr_ref[hd], NT) + _dot(dpib[:, lo:hi], wi_ref[hd], NT))
        dxc_ref[...] = dbx * ig + jnp.concatenate(back, axis=1)
        dlam = jnp.sum(dlog_a * (-LRU_C) * r, axis=0, keepdims=True) * (-_sigmoid(-lam_v))
        acc_ref[0:1, :] += jnp.sum(dpre_r, axis=0, keepdims=True)
        acc_ref[1:2, :] += jnp.sum(dpre_i, axis=0, keepdims=True)
        acc_ref[2:3, :] += dlam

    rev = lambda cb: pl.BlockSpec((ts, W), lambda i: (nt - 1 - i, cb))
    vec = _const((1, W))
    gw = _const((LRU_HEADS, LRU_HEAD_DIM, LRU_HEAD_DIM))
    return pl.pallas_call(
        body, grid=(nt,),
        in_specs=[rev(0), rev(0), rev(0), rev(0),
                  pl.BlockSpec((CONV_HALO, W), lambda i: (jnp.maximum((nt - 1 - i) * nh - 1, 0), 0)),
                  pl.BlockSpec((ts, 1), lambda i: (nt - 1 - i, 0)), gw, vec, gw, vec, vec],
        out_specs=[rev(0), rev(0), rev(0), rev(0), _const((8, W))],
        out_shape=[jax.ShapeDtypeStruct((S, W), BF16), jax.ShapeDtypeStruct((S, W), F32),
                   jax.ShapeDtypeStruct((S, W), BF16), jax.ShapeDtypeStruct((S, W), BF16),
                   jax.ShapeDtypeStruct((8, W), F32)],
        scratch_shapes=[pltpu.VMEM((ts, W), F32), pltpu.VMEM((ts, W), F32), pltpu.VMEM((1, W), F32)],
        compiler_params=_cp(1), name=name,
    )(dy, z, xc, hseq, hseq, reset, w_r, b_r, w_i, b_i, lam)


def _conv_bwd(dxc, z, conv_w, *, name):
    S = dxc.shape[0]
    ts = min(S, 512)
    nh = ts // CONV_HALO
    last = S // CONV_HALO - 1
    W = D_MODEL
    n = ts + CONV_HALO

    def body(d_ref, dn_ref, xb_ref, xp_ref, cw_ref, dxb_ref, acc_ref):
        i = pl.program_id(0)

        @pl.when(i == 0)
        def _():
            acc_ref[...] = jnp.zeros_like(acc_ref)

        d = d_ref[...]
        de = jnp.concatenate([d, jnp.where(i < pl.num_programs(0) - 1, dn_ref[...], 0.0)], axis=0)
        xe = jnp.concatenate([jnp.where(i > 0, xp_ref[...], 0.0), xb_ref[...]], axis=0)
        dxb = cw_ref[3:4, :] * d
        acc_ref[3:4, :] += jnp.sum(d * xe[CONV_HALO:], axis=0, keepdims=True)
        for kk in range(CONV_WIDTH - 1):
            sh = CONV_WIDTH - 1 - kk
            dxb = dxb + cw_ref[kk:kk + 1, :] * pltpu.roll(de, n - sh, 0)[:ts]
            acc_ref[kk:kk + 1, :] += jnp.sum(d * pltpu.roll(xe, sh, 0)[CONV_HALO:], axis=0, keepdims=True)
        dxb_ref[...] = dxb.astype(dxb_ref.dtype)
        acc_ref[4:5, :] += jnp.sum(d, axis=0, keepdims=True)

    return pl.pallas_call(
        body, grid=(S // ts,),
        in_specs=[_rows(ts, W), pl.BlockSpec((CONV_HALO, W), lambda i: (jnp.minimum((i + 1) * nh, last), 0)),
                  _rows(ts, W, 1), pl.BlockSpec((CONV_HALO, W), lambda i: (jnp.maximum(i * nh - 1, 0), 1)),
                  _const((CONV_WIDTH, W))],
        out_specs=[_rows(ts, W), _const((8, W))],
        out_shape=[jax.ShapeDtypeStruct((S, W), BF16), jax.ShapeDtypeStruct((8, W), F32)],
        compiler_params=_cp(1), name=name,
    )(dxc, dxc, z, z, conv_w)


def _loss_head(x, g, target, *, name):
    S, D = x.shape
    ts = _row_tile(S)

    def body(x_ref, g_ref, t_ref, dx_ref, dg_ref, l_ref):
        @pl.when(pl.program_id(0) == 0)
        def _():
            dg_ref[...] = jnp.zeros_like(dg_ref)
            l_ref[...] = jnp.zeros_like(l_ref)

        xv = x_ref[...]
        r = lax.rsqrt(jnp.mean(xv * xv, axis=-1, keepdims=True) + RMS_EPS)
        n = xv * r
        err = n * g_ref[...] - t_ref[...]
        l_ref[...] += 0.5 * jnp.sum(jnp.sum(err * err, axis=-1, keepdims=True) * (1.0 / D), axis=0, keepdims=True)
        dy = err * (1.0 / D)
        dn = dy * g_ref[...]
        dx_ref[...] = r * (dn - n * jnp.mean(dn * n, axis=-1, keepdims=True))
        dg_ref[...] += jnp.sum(dy * n, axis=0, keepdims=True)

    return pl.pallas_call(
        body, grid=(S // ts,), in_specs=[_rows(ts, D), _const((1, D)), _rows(ts, D)],
        out_specs=[_rows(ts, D), _const((1, D)), _const((8, LANES))],
        out_shape=[jax.ShapeDtypeStruct((S, D), F32), jax.ShapeDtypeStruct((1, D), F32),
                   jax.ShapeDtypeStruct((8, LANES), F32)],
        compiler_params=_cp(1), name=name,
    )(x, g.reshape(1, D), target)


def _adamw(w, ga, gb, m, v, *, name):
    shape = w.shape
    cols = shape[-1]
    rows = w.size // cols
    br = rows
    if rows * cols * 4 > (1 << 20):
        br = max(d for d in range(8, rows + 1, 8) if rows % d == 0 and d * cols * 4 <= (1 << 20))

    def body(w_ref, ga_ref, gb_ref, m_ref, v_ref, g_ref, d_ref, mo_ref, vo_ref):
        gv = ga_ref[...] + gb_ref[...]
        g_ref[...] = gv
        mn = ADAM_B1 * m_ref[...] + (1.0 - ADAM_B1) * gv
        vn = ADAM_B2 * v_ref[...] + (1.0 - ADAM_B2) * (gv * gv)
        m_hat = mn / (1.0 - ADAM_B1 ** ADAM_STEP)
        v_hat = vn / (1.0 - ADAM_B2 ** ADAM_STEP)
        d_ref[...] = -ADAM_LR * (m_hat / (jnp.sqrt(v_hat) + ADAM_EPS) + ADAM_WD * w_ref[...])
        mo_ref[...] = mn
        vo_ref[...] = vn

    spec = _rows(br, cols)
    outs = pl.pallas_call(
        body, grid=(rows // br,), in_specs=[spec] * 5, out_specs=[spec] * 4,
        out_shape=[jax.ShapeDtypeStruct((rows, cols), F32)] * 4, compiler_params=_cp(1), name=name,
    )(*[t.reshape(rows, cols) for t in (w, ga, gb, m, v)])
    return [o.reshape(shape) for o in outs]


def _pad_heads(w, width):
    k = w.shape[0]
    return jnp.pad(w.reshape(k, MLA_HEADS, width), ((0, 0), (0, 0), (0, HEAD_PAD - width))).reshape(k, -1)


def _unpad_heads(w, width):
    k = w.shape[0]
    return w.reshape(k, MLA_HEADS, HEAD_PAD)[:, :, :width].reshape(k, MLA_HEADS * width)


def _rope_tables(positions):
    inv_freq = ROPE_BASE ** (-jnp.arange(0, QK_ROPE, 2, dtype=F32) / QK_ROPE)
    ang = positions.astype(F32)[:, None] * inv_freq
    cos, sin = jnp.cos(ang), jnp.sin(ang)
    S = positions.shape[0]
    ones, zeros = jnp.ones((S, QK_NOPE), F32), jnp.zeros((S, QK_NOPE), F32)
    ctab = jnp.concatenate([ones, cos, cos, ones[:, :HEAD_PAD - QK_DIM]], axis=1)
    stab = jnp.concatenate([zeros, -sin, sin, zeros[:, :HEAD_PAD - QK_DIM]], axis=1)
    return ctab, stab


def _memory_block(x, mem, W, layer, tag):
    mn = _rms(mem, W["xa_norm_mem"][layer], name=f"{tag}_xa_norm_mem")
    kvm = _mm(mn, [(W["xa_w_kv"][layer], 0, 0)], _first, [(2 * D_MODEL, BF16, 0)], tn=2 * D_MODEL, nj=1,
              name=f"{tag}_xa_kv")[0]
    xo, hx, qx, o = _xa_block_fwd(x, kvm, W["xa_w_q"][layer], W["xa_w_o"][layer], W["xa_norm_x"][layer],
                                  name=f"{tag}_xa_fwd")
    return xo, (x, hx, qx, mn, kvm, o)


def _memory_block_bwd(dxo, mem, W, layer, saved, tag, grads):
    x, hx, qx, mn, kvm, o = saved
    wq, wkv, wo = W["xa_w_q"][layer], W["xa_w_kv"][layer], W["xa_w_o"][layer]
    grads["xa_w_o"][layer] = _owner_major(_mm_tn(o, dxo, name=f"{tag}_xa_dwo"), 0)
    dx, dqx, dkvm, dg = _xa_block_bwd(dxo, x, qx, kvm, wq, wo, W["xa_norm_x"][layer], name=f"{tag}_xa_bwd")
    grads["xa_w_q"][layer] = _owner_major(_mm_tn(hx, dqx, name=f"{tag}_xa_dwq"), 0)
    grads["xa_norm_x"][layer] = dg[0]
    dmn = _mm(dkvm, [(wkv, 0, 0)], _first, [(D_MODEL, F32, 0)], nt=True, tn=D_MODEL, nj=1, name=f"{tag}_xa_dmn")[0]
    grads["xa_w_kv"][layer] = _mm_tn_owners(mn, [dkvm], name=f"{tag}_xa_dwkv")
    _, dgm = _rms_bwd(mem, W["xa_norm_mem"][layer], dmn, name=f"{tag}_xa_norm_mem_bwd")
    grads["xa_norm_mem"][layer] = dgm[0]
    return dx


FF_TN = D_FF // 2

def _silu_mul(accs, extras):
    g, u = accs
    return [g * _sigmoid(g) * u, g, u]


def _silu_mul_bwd(accs, extras):
    da = accs[0]
    g, u = extras[0].astype(F32), extras[1].astype(F32)
    sg = _sigmoid(g)
    return [da * u * sg * (1.0 + g * (1.0 - sg)), da * g * sg]


def _ffn_block(x, W, layer, tag):
    hf = _rms(x, W["ffn_norm"][layer], name=f"{tag}_ffn_norm")
    wgu, wd = W["ffn_w_gate_up"][layer], W["ffn_w_down"][layer]
    act, g, u = _mm(hf, [(wgu, 0, 0), (wgu, 0, 2)], _silu_mul, [(D_FF, BF16, 0)] * 3, tn=FF_TN, nj=2,
                    name=f"{tag}_ffn_up")
    xo = _mm(act, [(wd, 0, 0)], _add_res, [(D_MODEL, F32, 0)], extras=[(x, 0)], tn=D_MODEL, nj=1,
             name=f"{tag}_ffn_down")[0]
    return xo, (x, hf, act, g, u)


def _ffn_block_bwd(dxo, W, layer, saved, tag, grads):
    x, hf, act, g, u = saved
    wgu, wd = W["ffn_w_gate_up"][layer], W["ffn_w_down"][layer]
    dg, du = _mm(dxo, [(wd, 0, 0)], _silu_mul_bwd, [(D_FF, BF16, 0)] * 2, nt=True, extras=[(g, 0), (u, 0)], tn=FF_TN,
                 nj=2, name=f"{tag}_ffn_dact")
    grads["ffn_w_down"][layer] = _owner_major(_mm_tn(act, dxo, tk=FF_TN, name=f"{tag}_ffn_dwd"), 0)
    dx, dgn = _mm(dg, [(wgu, 0, 0)], _norm_bwd_epilogue(0), [(D_MODEL, F32, 0)], nt=True, also=(du, (wgu, 0, 1)),
                  extras=[(x, 0), (dxo, 0)], rows=[W["ffn_norm"][layer].reshape(1, D_MODEL)],
                  sums=[D_MODEL], tn=D_MODEL, nj=1, name=f"{tag}_ffn_dhf")
    grads["ffn_w_gate_up"][layer] = _mm_tn_owners(hf, [dg, du], name=f"{tag}_ffn_dwgu")
    grads["ffn_norm"][layer] = dgn[0]
    return dx


def _keys_and_values(accs, extras):
    k, v = accs
    lane = lax.broadcasted_iota(jnp.int32, v.shape, 1)
    return [k, jnp.where(lane % HEAD_PAD == V_HEAD, 1.0, v)]


def _even_block(x, tabs, W, tag):
    ctab, stab = tabs
    w_in = W["ev_w_in"][0]
    zero = jnp.zeros((D_MODEL, QK_NOPE), BF16)
    w_in_pad = jnp.concatenate([w_in[:, :896], zero, w_in[:, 896:], zero[:, :HEAD_PAD - QK_DIM]], axis=1)
    w_q_pad = _pad_heads(W["ev_w_q_up"][0], QK_DIM)
    wkv = W["ev_w_kv_up"][0].reshape(KV_RANK, MLA_HEADS, QK_NOPE + V_HEAD)
    w_kv_pad = jnp.concatenate([_pad_heads(wkv[:, :, :QK_NOPE].reshape(KV_RANK, -1), QK_NOPE),
                                _pad_heads(wkv[:, :, QK_NOPE:].reshape(KV_RANK, -1), V_HEAD)], axis=1)
    w_out = W["ev_w_out"][0]
    w_att = jnp.pad(w_out[POOL_DIM:].reshape(MLA_HEADS, V_HEAD, D_MODEL), ((0, 0), (0, HEAD_PAD - V_HEAD), (0, 0)))
    w_out_pad = jnp.concatenate([w_out[:POOL_DIM], w_att.reshape(MLA_HEADS * HEAD_PAD, D_MODEL)], axis=0)
    pool_w = W["ev_pool_w"][0].astype(BF16)
    pool_scale = W["ev_pool_scale"]

    h = _rms(x, W["ev_norm"][0], name=f"{tag}_norm")
    z = _mm(h, [(w_in_pad, 0, 0)], _first, [(D_MODEL, F32, 0)], tn=D_MODEL, nj=1, name=f"{tag}_in")[0]
    mix, pooled = _pool_fwd(z, pool_w, pool_scale, name=f"{tag}_pool")
    cqn = _rms(z, W["ev_q_norm"][0], cb=2, w=Q_RANK, name=f"{tag}_q_norm")
    ckvn = _rms(z, W["ev_kv_norm"][0], cb=6, w=KV_RANK, name=f"{tag}_kv_norm")
    q_pad = _mm(cqn, [(w_q_pad, 0, 0)], _first, [(D_MODEL, F32, 0)], tn=D_MODEL, nj=1, name=f"{tag}_q_up")[0]
    k_pad, v_pad = _mm(ckvn, [(w_kv_pad, 0, 0), (w_kv_pad, 0, 1)], _keys_and_values,
                       [(D_MODEL, F32, 0), (D_MODEL, BF16, 0)], tn=D_MODEL, nj=1, name=f"{tag}_kv_up")
    q_rot, k_cat = _rope_fwd(q_pad, k_pad, z, ctab, stab, name=f"{tag}_rope")
    mix, lse = _flash_fwd(q_rot, k_cat, v_pad, mix, name=f"{tag}_attn")
    xo = _mm(mix, [(w_out_pad, 0, 0)], _add_res, [(D_MODEL, F32, 0)], extras=[(x, 0)], tn=D_MODEL, nj=1,
             name=f"{tag}_out")[0]
    saved = (x, h, z, pooled, cqn, ckvn, q_rot, k_cat, v_pad, lse, mix,
             (w_in_pad, w_q_pad, w_kv_pad, w_out_pad, pool_w, pool_scale))
    return xo, saved


def _even_block_bwd(dxo, tabs, W, saved, tag, grads, token=None):
    ctab, stab = tabs
    x, h, z, pooled, cqn, ckvn, q_rot, k_cat, v_pad, lse, mix, wts = saved
    w_in_pad, w_q_pad, w_kv_pad, w_out_pad, pool_w, pool_scale = wts
    if token is not None:
        w_out_pad = w_out_pad + token[0:1, 0:1].astype(BF16)
    dmix = _mm(dxo, [(w_out_pad, 0, 0)], _first, [(MIX_DIM, BF16, 0)], nt=True, tn=MIX_DIM, nj=1,
               name=f"{tag}_dmix")[0]
    dw_out_pad = _mm_tn(mix, dxo, tk=MIX_DIM // 3, name=f"{tag}_dw_out")
    datt = dw_out_pad[POOL_DIM:].reshape(MLA_HEADS, HEAD_PAD, D_MODEL)[:, :V_HEAD].reshape(-1, D_MODEL)
    grads["ev_w_out"] = [_owner_major(jnp.concatenate([dw_out_pad[:POOL_DIM], datt], axis=0), 0)]
    delta = _attn_delta(dmix, mix, name=f"{tag}_delta")
    dq_rot, dk_cat, dv_pad = _flash_bwd(q_rot, k_cat, v_pad, dmix, _retile_rows(lse, delta.shape[2]), delta,
                                        name=f"{tag}_attn_bwd")
    dq_pad, dkr = _rope_bwd(dq_rot, dk_cat, ctab, stab, name=f"{tag}_rope_bwd")
    dw_q_pad = _mm_tn(cqn, dq_pad, name=f"{tag}_dw_q_up")
    grads["ev_w_q_up"] = [_owner_major(_unpad_heads(dw_q_pad, QK_DIM), 1)]
    dcqn = _mm(dq_pad, [(w_q_pad, 0, 0)], _first, [(Q_RANK, F32, 0)], nt=True, tn=Q_RANK, nj=1, name=f"{tag}_dcqn")[0]
    dwk = _unpad_heads(_mm_tn(ckvn, dk_cat, name=f"{tag}_dw_k_up"), QK_NOPE).reshape(KV_RANK, MLA_HEADS, QK_NOPE)
    dwv = _unpad_heads(_mm_tn(ckvn, dv_pad, name=f"{tag}_dw_v_up"), V_HEAD).reshape(KV_RANK, MLA_HEADS, V_HEAD)
    grads["ev_w_kv_up"] = [_owner_major(jnp.concatenate([dwk, dwv], axis=2).reshape(KV_RANK, -1), 1)]
    dckvn = _mm(dk_cat, [(w_kv_pad, 0, 0)], _first, [(KV_RANK, F32, 0)], nt=True, tn=KV_RANK, nj=1,
                name=f"{tag}_dckvn_k")[0]
    dckvn = _mm(dv_pad, [(w_kv_pad, 0, 1)], _add_res, [(KV_RANK, F32, 0)], nt=True, extras=[(dckvn, 0)], tn=KV_RANK,
                nj=1, name=f"{tag}_dckvn_v")[0]
    dcq, dgq = _rms_bwd(z, W["ev_q_norm"][0], dcqn, cb=2, w=Q_RANK, out_dtype=BF16, name=f"{tag}_q_norm_bwd")
    dckv, dgkv = _rms_bwd(z, W["ev_kv_norm"][0], dckvn, cb=6, w=KV_RANK, out_dtype=BF16, name=f"{tag}_kv_norm_bwd")
    grads["ev_q_norm"], grads["ev_kv_norm"] = dgq, dgkv
    du, dypre, dscale = _pool_bwd(dmix, pooled, pool_w, pool_scale, name=f"{tag}_pool_bwd")
    grads["ev_pool_scale"] = dscale
    grads["ev_pool_w"] = _mm_tn_grouped(pooled, dypre, 4, POOL_GROUP, name=f"{tag}_dpool_w")[None]
    dz = jnp.concatenate([du, dcq, dckv, dkr], axis=1)
    dw_in_pad = _mm_tn(h, dz, name=f"{tag}_dw_in")
    grads["ev_w_in"] = [_owner_major(jnp.concatenate([dw_in_pad[:, :896], dw_in_pad[:, 960:992]], axis=1), 0)]
    dx, dgn = _mm(dz, [(w_in_pad, 0, 0)], _norm_bwd_epilogue(0), [(D_MODEL, F32, 0)], nt=True,
                  extras=[(x, 0), (dxo, 0)], rows=[W["ev_norm"][0].reshape(1, D_MODEL)], sums=[D_MODEL], tn=D_MODEL,
                  nj=1, name=f"{tag}_dh")
    grads["ev_norm"] = dgn
    return dx


def _odd_block(x, reset, W, tag):
    h = _rms(x, W["od_norm"][0], name=f"{tag}_norm")
    z = _mm(h, [(W["od_w_in"][0], 0, 0)], _first, [(2 * D_MODEL, F32, 0)], tn=D_MODEL, nj=2, name=f"{tag}_in")[0]
    w_r, w_i = W["od_w_rgate"][0], W["od_w_igate"][0]
    vecs = [W[n].reshape(1, D_MODEL) for n in ("od_conv_b", "od_b_rgate", "od_b_igate", "od_lambda")]
    xc, hseq, y = _lru_fwd(z, reset, W["od_conv_w"][0], vecs[0], w_r, vecs[1], w_i, vecs[2], vecs[3],
                           name=f"{tag}_lru")
    xo = _mm(y, [(W["od_w_out"][0], 0, 0)], _add_res, [(D_MODEL, F32, 0)], extras=[(x, 0)], tn=D_MODEL, nj=1,
             name=f"{tag}_out")[0]
    return xo, (x, h, z, xc, hseq, y, vecs)


def _odd_block_bwd(dxo, reset, W, saved, tag, grads):
    x, h, z, xc, hseq, y, vecs = saved
    w_r, w_i = W["od_w_rgate"][0], W["od_w_igate"][0]
    dy = _mm(dxo, [(W["od_w_out"][0], 0, 0)], _first, [(D_MODEL, F32, 0)], nt=True, tn=D_MODEL, nj=1,
             name=f"{tag}_dy")[0]
    grads["od_w_out"] = [_owner_major(_mm_tn(y, dxo, name=f"{tag}_dw_out"), 0)]
    dgate, dxc, dpr, dpi, acc = _lru_bwd(dy, z, xc, hseq, reset, w_r, vecs[1], w_i, vecs[2], vecs[3],
                                         name=f"{tag}_lru_bwd")
    grads["od_b_rgate"], grads["od_b_igate"], grads["od_lambda"] = acc[0:1], acc[1:2], acc[2:3]
    grads["od_w_rgate"] = [_owner_major(_mm_tn_grouped(xc, dpr, LRU_HEADS, LRU_HEAD_DIM, name=f"{tag}_dw_rgate"), 1)]
    grads["od_w_igate"] = [_owner_major(_mm_tn_grouped(xc, dpi, LRU_HEADS, LRU_HEAD_DIM, name=f"{tag}_dw_igate"), 1)]
    dxb, cacc = _conv_bwd(dxc, z, W["od_conv_w"][0], name=f"{tag}_conv_bwd")
    grads["od_conv_w"], grads["od_conv_b"] = cacc[None, 0:4], cacc[4:5]
    dz = jnp.concatenate([dgate, dxb], axis=1)
    grads["od_w_in"] = [_mm_tn_owners(h, [dz], name=f"{tag}_dw_in")]
    dx, dgn = _mm(dz, [(W["od_w_in"][0], 0, 0)], _norm_bwd_epilogue(0), [(D_MODEL, F32, 0)], nt=True,
                  extras=[(x, 0), (dxo, 0)], rows=[W["od_norm"][0].reshape(1, D_MODEL)], sums=[D_MODEL], tn=D_MODEL,
                  nj=1, name=f"{tag}_dh")
    grads["od_norm"] = dgn
    return dx


def _local_step(x, mem, positions, target, W, later_weights=None, exchange_earlier=None):
    tabs = _rope_tables(positions)
    reset = (positions == 0).astype(F32)[:, None]
    grads = {n: [None, None] for n in ("xa_norm_x", "xa_norm_mem", "xa_w_q", "xa_w_kv", "xa_w_o", "ffn_norm",
                                       "ffn_w_gate_up", "ffn_w_down")}
    x1, s_even = _even_block(x, tabs, W, "l0_even")
    if later_weights is not None:
        W = {**W, **later_weights(x1)}
    x2, s_xa0 = _memory_block(x1, mem, W, 0, "l0")
    x3, s_ff0 = _ffn_block(x2, W, 0, "l0")
    x4, s_odd = _odd_block(x3, reset, W, "l1_odd")
    x5, s_xa1 = _memory_block(x4, mem, W, 1, "l1")
    x6, s_ff1 = _ffn_block(x5, W, 1, "l1")
    d, dgf, loss = _loss_head(x6, W["final_norm"], target, name="loss_head")
    grads["final_norm"] = dgf[0]
    d = _ffn_block_bwd(d, W, 1, s_ff1, "l1", grads)
    d = _memory_block_bwd(d, mem, W, 1, s_xa1, "l1", grads)
    d = _odd_block_bwd(d, reset, W, s_odd, "l1_odd", grads)
    d = _ffn_block_bwd(d, W, 0, s_ff0, "l0", grads)
    d = _memory_block_bwd(d, mem, W, 0, s_xa0, "l0", grads)
    token = exchange_earlier(grads) if exchange_earlier is not None else None
    d = _even_block_bwd(d, tabs, W, s_even, "l0_even", grads, token)
    big = {n: grads.pop(n) for n in MATMUL_WEIGHTS}
    for n, v in grads.items():
        if isinstance(v, list):
            grads[n] = jnp.stack(v)
    return loss[0, 0], d, big, grads


WEIGHTS = ("ev_norm", "ev_w_in", "ev_pool_w", "ev_pool_scale", "ev_q_norm", "ev_w_q_up", "ev_kv_norm", "ev_w_kv_up",
           "ev_w_out", "od_norm", "od_w_in", "od_conv_w", "od_conv_b", "od_w_rgate", "od_b_rgate", "od_w_igate",
           "od_b_igate", "od_lambda", "od_w_out", "xa_norm_x", "xa_norm_mem", "xa_w_q", "xa_w_kv", "xa_w_o",
           "ffn_norm", "ffn_w_gate_up", "ffn_w_down", "final_norm")
SHARD_AXIS = {"ev_w_in": 1, "ev_w_q_up": 2, "ev_w_kv_up": 2, "ev_w_out": 1, "od_norm": 1, "od_w_in": 2,
              "od_conv_w": 2, "od_conv_b": 1, "od_w_rgate": 2, "od_b_rgate": 1, "od_w_igate": 2, "od_b_igate": 1,
              "od_lambda": 1, "od_w_out": 1, "xa_w_q": 1, "xa_w_kv": 2, "xa_w_o": 1, "ffn_w_gate_up": 2,
              "ffn_w_down": 1}
MATMUL_WEIGHTS = ("ev_w_in", "ev_w_q_up", "ev_w_kv_up", "ev_w_out", "od_w_in", "od_w_rgate", "od_w_igate",
                  "od_w_out", "xa_w_q", "xa_w_kv", "xa_w_o", "ffn_w_gate_up", "ffn_w_down")
SMALL_SHARDED = tuple(n for n in WEIGHTS if n in SHARD_AXIS and n not in MATMUL_WEIGHTS)
REPLICATED = tuple(n for n in WEIGHTS if n not in SHARD_AXIS)


def _pack(parts, quantum):
    flat = jnp.concatenate([p.reshape(-1) for p in parts])
    pad = (-flat.shape[0]) % quantum
    return jnp.pad(flat, (0, pad)).reshape(-1, LANES)


def _unpack(flat, shapes):
    out, off = [], 0
    for shape in shapes:
        size = math.prod(shape)
        out.append(flat[off:off + size].reshape(shape))
        off += size
    return out


def _run_copies(local, remote, send_sems, recv_sems, local_sems):
    locals_ = [pltpu.make_async_copy(src, dst, local_sems.at[n]) for n, (src, dst) in enumerate(local)]
    for cp in locals_:
        cp.start()
    sends = [pltpu.make_async_remote_copy(src_ref=src, dst_ref=dst, send_sem=send_sems.at[k, n],
                                          recv_sem=recv_sems.at[k, n], device_id=dev, device_id_type=MESH)
             for (k, n, src, dst, _, dev) in remote]
    for cp in sends:
        cp.start()
    for (k, n, src, _, arrival, dev) in remote:
        pltpu.make_async_remote_copy(src_ref=src, dst_ref=arrival, send_sem=send_sems.at[k, n],
                                     recv_sem=recv_sems.at[k, n], device_id=dev, device_id_type=MESH).wait_recv()
    for cp in sends:
        cp.wait_send()
    for cp in locals_:
        cp.wait()


def _chip_peers(x, y):
    return [(1 - x, y), (x, 1 - y), (1 - x, 1 - y)]


def _owner_block(ref, axis, q):
    size = ref.shape[axis] // N_CHIPS
    idx = [slice(None)] * len(ref.shape)
    idx[axis] = pl.ds(q * size, size)
    return ref.at[tuple(idx)]


def _comm_call(body, ins, out_shapes, n_items, n_peers, *, name):
    return pl.pallas_call(
        body, in_specs=[ANY] * len(ins), out_specs=[ANY] * len(out_shapes), out_shape=out_shapes,
        scratch_shapes=[pltpu.SemaphoreType.DMA((n_peers, n_items)), pltpu.SemaphoreType.DMA((n_peers, n_items)),
                        pltpu.SemaphoreType.DMA((n_items,))],
        name=name,
    )(*ins)


def _gather_chips(shards, axes, *, name):
    n = len(shards)
    full = [jax.ShapeDtypeStruct(tuple(d * (N_CHIPS if a == ax else 1) for a, d in enumerate(s.shape)), s.dtype)
            for s, ax in zip(shards, axes)]

    def body(*refs):
        srcs, dsts = refs[:n], refs[n:2 * n]
        x, y, c = lax.axis_index("x"), lax.axis_index("y"), lax.axis_index("c")
        me = 2 * x + y
        local = [(srcs[i], _owner_block(dsts[i], axes[i], me)) for i in range(n)]
        remote = [(k, i, srcs[i], _owner_block(dsts[i], axes[i], me), _owner_block(dsts[i], axes[i], 2 * px + py),
                   (px, py, c))
                  for k, (px, py) in enumerate(_chip_peers(x, y)) for i in range(n)]
        _run_copies(local, remote, *refs[2 * n:])

    return _comm_call(body, shards, full, n, 3, name=name)


HBM = pl.BlockSpec(memory_space=pltpu.HBM)
SEM = pl.BlockSpec(memory_space=pltpu.SEMAPHORE)
DATAFLOW = pltpu.SideEffectType.DATAFLOW_SIDE_EFFECTING


def _gather_plan(axes):
    return lambda srcs, lands, me, peer: [
        (srcs[i], _owner_block(lands[i], ax, me), _owner_block(lands[i], ax, peer)) for i, ax in enumerate(axes)]


def _exchange_plan(where):
    return lambda srcs, lands, me, peer: [
        (srcs[i].at[peer], lands[n].at[me, l], lands[n].at[peer, l]) for i, (n, l) in enumerate(where)]


def _split_start(srcs, lands, plan, *, name):
    ns, nl = len(srcs), len(lands)
    nsem = 3 * len(plan(list(srcs), list(lands), 0, 0))

    def body(*refs):
        src_refs, land_refs = refs[:ns], refs[ns:ns + nl]
        send_sems, recv_sems = refs[ns + nl:ns + nl + nsem], refs[ns + nl + nsem:ns + nl + 2 * nsem]
        x, y, c = lax.axis_index("x"), lax.axis_index("y"), lax.axis_index("c")
        n = 0
        for px, py in _chip_peers(x, y):
            for src, dst, _ in plan(src_refs, land_refs, 2 * x + y, 2 * px + py):
                pltpu.make_async_remote_copy(src_ref=src, dst_ref=dst, send_sem=send_sems[n], recv_sem=recv_sems[n],
                                             device_id=(px, py, c), device_id_type=MESH).start()
                n += 1
        refs[-1][...] = jnp.zeros_like(refs[-1])

    arrays = list(srcs) + list(lands)
    out = pl.pallas_call(
        body, name=name, in_specs=[HBM] * (ns + nl),
        out_specs=[SEM] * (2 * nsem) + [HBM] * (ns + nl) + [pl.BlockSpec(memory_space=pltpu.VMEM)],
        out_shape=[pltpu.SemaphoreType.DMA(())] * (2 * nsem) + [pltpu.HBM(a.shape, a.dtype) for a in arrays]
        + [jax.ShapeDtypeStruct((8, LANES), F32)],
        input_output_aliases={i: 2 * nsem + i for i in range(ns + nl)},
        compiler_params=pltpu.CompilerParams(has_side_effects=DATAFLOW),
    )(*[pltpu.with_memory_space_constraint(a, pltpu.HBM) for a in arrays])
    sems, rest = out[:2 * nsem], out[2 * nsem:]
    return sems[:nsem], sems[nsem:], rest[:ns], rest[ns:ns + nl], rest[-1]


def _split_wait(handle, after, plan, *, name):
    send_sems, recv_sems, srcs, lands, _ = handle
    ns, nl, nsem = len(srcs), len(lands), len(send_sems)

    def body(*refs):
        src_refs, land_refs = refs[:ns], refs[ns:ns + nl]
        send_refs, recv_refs = refs[ns + nl:ns + nl + nsem], refs[ns + nl + nsem:ns + nl + 2 * nsem]
        x, y, c = lax.axis_index("x"), lax.axis_index("y"), lax.axis_index("c")
        n = 0
        for px, py in _chip_peers(x, y):
            for src, _, arrival in plan(src_refs, land_refs, 2 * x + y, 2 * px + py):
                cp = pltpu.make_async_remote_copy(src_ref=src, dst_ref=arrival, send_sem=send_refs[n],
                                                  recv_sem=recv_refs[n], device_id=(px, py, c), device_id_type=MESH)
                cp.wait_send()
                cp.wait_recv()
                n += 1

    out = pl.pallas_call(
        body, name=name, in_specs=[HBM] * (ns + nl) + [SEM] * (2 * nsem) + [ANY], out_specs=[HBM] * (ns + nl),
        out_shape=[pltpu.HBM(a.shape, a.dtype) for a in list(srcs) + list(lands)],
        input_output_aliases={i: i for i in range(ns + nl)},
        compiler_params=pltpu.CompilerParams(has_side_effects=DATAFLOW),
    )(*srcs, *lands, *send_sems, *recv_sems, after)
    return out[ns:]


def _exchange_sibling(arrays, *, name):
    n = len(arrays)

    def body(*refs):
        x, y, c = lax.axis_index("x"), lax.axis_index("y"), lax.axis_index("c")
        remote = [(0, i, refs[i], refs[n + i], refs[n + i], (x, y, 1 - c)) for i in range(n)]
        _run_copies([], remote, *refs[2 * n:])

    return _comm_call(body, arrays, [jax.ShapeDtypeStruct(a.shape, a.dtype) for a in arrays], n, 1, name=name)


def _sum_slots(r, *, token=None, name):
    shape = r.shape[1:]
    cols = shape[-1]
    rows = math.prod(shape) // cols
    tr = max(d for d in range(8, rows + 1, 8) if rows % d == 0 and d * cols * 16 <= (4 << 20))

    def body(r_ref, *refs):
        total = ((r_ref[0] + r_ref[1]) + r_ref[2]) + r_ref[3]
        refs[-1][...] = total if token is None else total + refs[0][0:1, 0:1]

    in_specs = [pl.BlockSpec((N_CHIPS, tr, cols), lambda i: (0, i, 0))]
    in_specs += [] if token is None else [_const((8, LANES))]
    return pl.pallas_call(
        body, grid=(rows // tr,), in_specs=in_specs,
        out_specs=_rows(tr, cols), out_shape=jax.ShapeDtypeStruct((rows, cols), F32), compiler_params=_cp(1),
        name=name,
    )(r.reshape(N_CHIPS, rows, cols), *([] if token is None else [token])).reshape(shape)


FIRST_WEIGHTS = ("ev_w_in", "ev_w_q_up", "ev_w_kv_up", "ev_w_out")
LATER_WEIGHTS = tuple(n for n in MATMUL_WEIGHTS if n not in FIRST_WEIGHTS)
LAST_GRADS = FIRST_WEIGHTS
EARLIER_GRADS = tuple(n for n in MATMUL_WEIGHTS if n not in LAST_GRADS)


def _my_chip():
    return 2 * lax.axis_index("x") + lax.axis_index("y")


def _gather_first(w):
    small = _pack([w[n] for n in SMALL_SHARDED], 8 * LANES)
    stacked = [n for n in FIRST_WEIGHTS if SHARD_AXIS[n] == w[n].ndim - 1 and w[n].shape[-1] % LANES]
    shards = [w[n].astype(BF16)[None] if n in stacked else w[n].astype(BF16) for n in FIRST_WEIGHTS]
    got = _gather_chips(shards + [small], [0 if n in stacked else SHARD_AXIS[n] for n in FIRST_WEIGHTS] + [0],
                        name="gather_first")
    full = {n: w[n] for n in REPLICATED}
    for n, g in zip(FIRST_WEIGHTS, got[:-1]):
        full[n] = jnp.concatenate([g[q] for q in range(N_CHIPS)], axis=SHARD_AXIS[n]) if n in stacked else g
    per_chip = [_unpack(got[-1][q * small.shape[0]:(q + 1) * small.shape[0]].reshape(-1),
                        [w[n].shape for n in SMALL_SHARDED]) for q in range(N_CHIPS)]
    for i, n in enumerate(SMALL_SHARDED):
        full[n] = jnp.concatenate([per_chip[q][i] for q in range(N_CHIPS)], axis=SHARD_AXIS[n])
    return full


def _gather_later_start(w):
    shards = [w[n].astype(BF16) for n in LATER_WEIGHTS]
    axes = [SHARD_AXIS[n] for n in LATER_WEIGHTS]
    lands = []
    for s, ax in zip(shards, axes):
        shape = tuple(d * (N_CHIPS if a == ax else 1) for a, d in enumerate(s.shape))
        lands.append(lax.dynamic_update_slice_in_dim(lax.empty(shape, s.dtype), s, _my_chip() * s.shape[ax], ax))
    return _split_start(shards, lands, _gather_plan(axes), name="gather_later_start"), _gather_plan(axes)


def _owner_major(g, axis):
    shape = g.shape
    size = shape[axis] // N_CHIPS
    g = jnp.moveaxis(g.reshape(shape[:axis] + (N_CHIPS, size) + shape[axis + 1:]), axis, 0)
    return g.reshape(N_CHIPS, -1, shape[-1] if axis < len(shape) - 1 else size)


def _exchange_start(items, *, name):
    me = _my_chip()
    srcs, lands, where = [], [], []
    for n, layers in enumerate(items):
        land = lax.empty((N_CHIPS, len(layers)) + layers[0].shape[1:], layers[0].dtype)
        for l, a in enumerate(layers):
            own = lax.dynamic_index_in_dim(a, me, 0, keepdims=True)[:, None]
            land = lax.dynamic_update_slice(land, own, (me, l) + (0,) * (a.ndim - 1))
            srcs.append(a)
            where.append((n, l))
        lands.append(land)
    plan = _exchange_plan(where)
    return _split_start(srcs, lands, plan, name=name), plan


def _earlier_items(grads, full_shapes):
    small = [_pack([jnp.split(grads[n].reshape(full_shapes[n]), N_CHIPS, axis=SHARD_AXIS[n])[q]
                    for n in SMALL_SHARDED], 8 * LANES) for q in range(N_CHIPS)]
    return [grads[n] for n in EARLIER_GRADS] + [[jnp.stack(small)]]


def _last_items(big, grads, full_shapes, loss):
    repl = _pack([grads[n].reshape(full_shapes[n]) for n in REPLICATED] + [loss.reshape(1)], 8 * LANES)
    return [big[n] for n in LAST_GRADS] + [[jnp.stack([repl] * N_CHIPS)]]


def kernel(
        x, mem, positions, ev_norm, ev_w_in, ev_pool_w, ev_pool_scale, ev_q_norm, ev_w_q_up, ev_kv_norm,
        ev_w_kv_up, ev_w_out, od_norm, od_w_in, od_conv_w, od_conv_b, od_w_rgate, od_b_rgate, od_w_igate,
        od_b_igate, od_lambda, od_w_out, xa_norm_x, xa_norm_mem, xa_w_q, xa_w_kv, xa_w_o, ffn_norm,
        ffn_w_gate_up, ffn_w_down, final_norm, loss_target, m_ev_norm, m_ev_w_in, m_ev_pool_w, m_ev_pool_scale,
        m_ev_q_norm, m_ev_w_q_up, m_ev_kv_norm, m_ev_w_kv_up, m_ev_w_out, m_od_norm, m_od_w_in, m_od_conv_w,
        m_od_conv_b, m_od_w_rgate, m_od_b_rgate, m_od_w_igate, m_od_b_igate, m_od_lambda, m_od_w_out,
        m_xa_norm_x, m_xa_norm_mem, m_xa_w_q, m_xa_w_kv, m_xa_w_o, m_ffn_norm, m_ffn_w_gate_up, m_ffn_w_down,
        m_final_norm, v_ev_norm, v_ev_w_in, v_ev_pool_w, v_ev_pool_scale, v_ev_q_norm, v_ev_w_q_up,
        v_ev_kv_norm, v_ev_w_kv_up, v_ev_w_out, v_od_norm, v_od_w_in, v_od_conv_w, v_od_conv_b, v_od_w_rgate,
        v_od_b_rgate, v_od_w_igate, v_od_b_igate, v_od_lambda, v_od_w_out, v_xa_norm_x, v_xa_norm_mem, v_xa_w_q,
        v_xa_w_kv, v_xa_w_o, v_ffn_norm, v_ffn_w_gate_up, v_ffn_w_down, v_final_norm):
    given = dict(locals())
    w = {n: given[n] for n in WEIGHTS}
    full_shapes = {n: tuple(d * (N_CHIPS if a == SHARD_AXIS.get(n) else 1) for a, d in enumerate(w[n].shape))
                   for n in WEIGHTS}
    full = _gather_first(w)
    later, later_plan = _gather_later_start(w)
    full["ev_norm"] = full["ev_norm"] + later[4][0:1, 0:1]
    exchange = {}

    def later_weights(after):
        return dict(zip(LATER_WEIGHTS, _split_wait(later, after, later_plan, name="gather_later_wait")))

    def exchange_earlier(grads):
        exchange["handle"], exchange["plan"] = _exchange_start(_earlier_items(grads, full_shapes),
                                                               name="exchange_earlier_start")
        return exchange["handle"][4]

    loss, grad_x, big, grads = _local_step(x[0], mem[0], positions[0], loss_target[0], full, later_weights,
                                           exchange_earlier)
    earlier = EARLIER_GRADS + ("small",)
    got = dict(zip(earlier, _split_wait(exchange["handle"], grad_x, exchange["plan"], name="exchange_earlier_wait")))
    last, last_plan = _exchange_start(_last_items(big, grads, full_shapes, loss), name="exchange_last_start")
    sums = {n: _sum_slots(got[n], token=last[4] if i == 0 else None, name=f"sum_chips_{n}")
            for i, n in enumerate(earlier)}
    got = dict(zip(LAST_GRADS + ("replicated",),
                   _split_wait(last, sums[earlier[-1]], last_plan, name="exchange_last_wait")))
    sums.update({n: _sum_slots(got[n], name=f"sum_chips_{n}") for n in got})
    mine = [sums[n] for n in MATMUL_WEIGHTS + ("small", "replicated")]
    other = _exchange_sibling(mine, name="exchange_sibling")
    out = {}
    for i, n in enumerate(MATMUL_WEIGHTS):
        out[n] = _adamw(w[n], mine[i].reshape(w[n].shape), other[i].reshape(w[n].shape), given["m_" + n],
                        given["v_" + n], name=f"adamw_{n}")
    for i, group in ((len(MATMUL_WEIGHTS), SMALL_SHARDED), (len(MATMUL_WEIGHTS) + 1, REPLICATED)):
        spare = [jnp.zeros((1,), F32)] if group is REPLICATED else []
        packed = [_pack([given[pre + n] for n in group] + spare, 8 * LANES) for pre in ("", "m_", "v_")]
        res = _adamw(packed[0], mine[i].reshape(packed[0].shape), other[i].reshape(packed[0].shape), packed[1],
                     packed[2], name=f"adamw_group{i}")
        shapes = [w[n].shape for n in group] + [(1,)] * len(spare)
        for j, arrs in enumerate(zip(*[_unpack(r.reshape(-1), shapes) for r in res])):
            if j < len(group):
                out[group[j]] = list(arrs)
            else:
                loss = arrs[0][0]
    return (loss, grad_x[None], *[out[n][k] for k in range(4) for n in WEIGHTS])
```

```python
import functools
import math

import jax
import jax.numpy as jnp
from jax import lax
from jax.experimental import pallas as pl
from jax.experimental.pallas import tpu as pltpu

F32 = jnp.float32
BF16 = jnp.bfloat16

D_MODEL = 1024
POOL_DIM = 512
POOL_WINDOWS = (2, 4, 8, 16)
POOL_GROUP = 128
MLA_HEADS = 8
QK_NOPE = 64
QK_ROPE = 32
QK_DIM = QK_NOPE + QK_ROPE
V_HEAD = 64
HEAD_PAD = 128
Q_RANK = 256
KV_RANK = 128
ROPE_BASE = 10000.0
LRU_HEADS = 4
LRU_HEAD_DIM = 256
CONV_WIDTH = 4
LRU_C = 8.0
MEM_HEADS = 4
MEM_HEAD_DIM = 256
D_FF = 2816
RMS_EPS = 1e-6
NEG_INF = -1e30

ADAM_LR = 0.001
ADAM_B1 = 0.9
ADAM_B2 = 0.999
ADAM_EPS = 1e-08
ADAM_WD = 0.01
ADAM_STEP = 10

N_CHIPS = 4
LANES = 128
VMEM_LIMIT = 56 * 1024 * 1024
MESH = pl.DeviceIdType.MESH
ANY = pl.BlockSpec(memory_space=pl.ANY)
MIX_DIM = POOL_DIM + MLA_HEADS * HEAD_PAD

NN = (((1,), (0,)), ((), ()))
NT = (((1,), (1,)), ((), ()))
TN = (((0,), (0,)), ((), ()))


def _cp(n):
    return pltpu.CompilerParams(dimension_semantics=("arbitrary",) * n, vmem_limit_bytes=VMEM_LIMIT)


def _dot(a, b, dims=NN):
    return lax.dot_general(a, b, dims, preferred_element_type=F32)


def _row_tile(S):
    return 1024 if S % 1024 == 0 else min(S, 512)


def _rows(ts, w, cb=0):
    return pl.BlockSpec((ts, w), lambda i: (i, cb))


def _const(shape):
    return pl.BlockSpec(shape, lambda i: (0,) * len(shape))


MM_VMEM_BUDGET = 40 * 1024 * 1024


def _mm(a, bs, epi, outs, *, tn, nj, nt=False, also=None, extras=(), rows=(), sums=(), a_cb=0, k=None, tm=None,
        name):
    M = a.shape[0]
    k = k or a.shape[1]
    nb, ne, nr, no = len(bs), len(extras), len(rows), len(outs)
    lhs = [(a, k, a_cb, b) for b in bs[:1]] + ([(also[0], also[0].shape[1], 0, also[1])] if also else [])
    if tm is None:
        per_row = 2 * (sum(kk * x.dtype.itemsize for x, kk, _, _ in lhs)
                       + sum(e.dtype.itemsize for e, _ in extras) * tn
                       + sum(jnp.dtype(dt).itemsize for _, dt, _ in outs) * tn) + nb * tn * 4
        weights = (1 if nj == 1 else 2) * (sum(b.dtype.itemsize for b, _, _ in bs) * k
                                           + (also[1][0].dtype.itemsize * lhs[-1][1] if also else 0)) * tn
        tm = 1024 if M % 1024 == 0 and 1024 * per_row + weights <= MM_VMEM_BUDGET else min(M, 512)
    dims = NT if nt else NN
    assert not sums or nj == 1
    na = 2 if also else 0

    def body(*refs):
        av = refs[0][...].astype(BF16)
        accs = [_dot(av, r[...].astype(BF16), dims) for r in refs[1:1 + nb]]
        if also:
            accs[0] = accs[0] + _dot(refs[1 + nb][...].astype(BF16), refs[2 + nb][...].astype(BF16), dims)
        refs = refs[:1 + nb] + refs[1 + nb + na:]
        vals = epi(accs, [r[...] for r in refs[1 + nb:1 + nb + ne + nr]])
        outs_refs = refs[1 + nb + ne + nr:]
        for o, v in zip(outs_refs[:no], vals[:no]):
            o[...] = v.astype(o.dtype)
        if sums:
            @pl.when(pl.program_id(1) == 0)
            def _():
                for o in outs_refs[no:]:
                    o[...] = jnp.zeros_like(o)

            for o, v in zip(outs_refs[no:], vals[no:]):
                o[...] += v

    in_specs = [pl.BlockSpec((tm, k), lambda j, i: (i, a_cb))]
    weights = [(k, rb, cb) for (_, rb, cb) in bs]
    if also:
        in_specs_also = pl.BlockSpec((tm, lhs[-1][1]), lambda j, i: (i, 0))
        weights.append((lhs[-1][1], also[1][1], also[1][2]))
    for n, (kk, rb, cb) in enumerate(weights):
        if also and n == nb:
            in_specs.append(in_specs_also)
        mode = dict(pipeline_mode=pl.Buffered(1)) if nj == 1 else {}
        if nt:
            in_specs.append(pl.BlockSpec((tn, kk), lambda j, i, rb=rb, cb=cb: (rb + j, cb), **mode))
        else:
            in_specs.append(pl.BlockSpec((kk, tn), lambda j, i, rb=rb, cb=cb: (rb, cb + j), **mode))
    for (_, cb) in extras:
        in_specs.append(pl.BlockSpec((tm, tn), lambda j, i, cb=cb: (i, cb + j)))
    in_specs += [pl.BlockSpec((1, tn), lambda j, i: (0, 0))] * nr
    out_specs = [pl.BlockSpec((tm, tn), lambda j, i, cb=cb: (i, cb + j)) for (_, _, cb) in outs]
    out_specs += [pl.BlockSpec((1, w), lambda j, i: (0, 0)) for w in sums]
    res = pl.pallas_call(
        body, grid=(nj, M // tm), in_specs=in_specs, out_specs=out_specs,
        out_shape=[jax.ShapeDtypeStruct((M, n), dt) for (n, dt, _) in outs]
        + [jax.ShapeDtypeStruct((1, w), F32) for w in sums],
        compiler_params=_cp(2), name=name,
    )(a, *[b for (b, _, _) in bs], *([also[0], also[1][0]] if also else []), *[e for (e, _) in extras], *rows)
    return res


def _first(accs, extras):
    return [accs[0]]


def _add_res(accs, extras):
    return [accs[0] + extras[0].astype(F32)]


def _norm_bwd_epilogue(partials):
    def epi(accs, vals):
        dh = accs[0]
        for part in vals[:partials]:
            dh = dh + part.astype(F32)
        x, res, g = vals[partials:partials + 3]
        r = lax.rsqrt(jnp.mean(x * x, axis=-1, keepdims=True) + RMS_EPS)
        n = x * r
        dn = dh * g
        return [r * (dn - n * jnp.mean(dn * n, axis=-1, keepdims=True)) + res, jnp.sum(dh * n, axis=0, keepdims=True)]

    return epi


TN_VMEM_BUDGET = 36 * 1024 * 1024


def _contraction_rows(S, row_bytes, out_elems):
    ts = min(S, 2048)
    while ts > 512 and 2 * (ts * row_bytes + out_elems * 4) > TN_VMEM_BUDGET:
        ts //= 2
    return ts


def _mm_tn(a, b, *, ka=None, a_cb=0, nb=None, b_cb=0, tk=None, tn=None, ts=None, name):
    S = a.shape[0]
    ka = ka or a.shape[1]
    nb = nb or b.shape[1]
    tk = tk or ka
    tn = tn or nb
    ts = ts or _contraction_rows(S, tk * a.dtype.itemsize + tn * b.dtype.itemsize, tk * tn)
    a0, b0 = a_cb * (ka // tk), b_cb * (nb // tn)

    def body(a_ref, b_ref, o_ref):
        @pl.when(pl.program_id(2) == 0)
        def _():
            o_ref[...] = jnp.zeros_like(o_ref)

        o_ref[...] += _dot(a_ref[...].astype(BF16), b_ref[...].astype(BF16), TN)

    return pl.pallas_call(
        body, grid=(ka // tk, nb // tn, S // ts),
        in_specs=[pl.BlockSpec((ts, tk), lambda p, q, s: (s, a0 + p)),
                  pl.BlockSpec((ts, tn), lambda p, q, s: (s, b0 + q))],
        out_specs=pl.BlockSpec((tk, tn), lambda p, q, s: (p, q)),
        out_shape=jax.ShapeDtypeStruct((ka, nb), F32), compiler_params=_cp(3), name=name,
    )(a, b)


def _mm_tn_owners(a, bs, *, name):
    S, ka = a.shape
    nb = sum(b.shape[1] for b in bs)
    tn = nb // N_CHIPS
    ts = _contraction_rows(S, ka * a.dtype.itemsize + len(bs) * tn * bs[0].dtype.itemsize, ka * tn)
    per = N_CHIPS // len(bs)

    def body(a_ref, *refs):
        o_ref = refs[-1]
        q = pl.program_id(0)

        @pl.when(pl.program_id(1) == 0)
        def _():
            o_ref[...] = jnp.zeros_like(o_ref)

        av = a_ref[...].astype(BF16)
        for n, b_ref in enumerate(refs[:-1]):
            @pl.when(q // per == n)
            def _():
                o_ref[0] += _dot(av, b_ref[...].astype(BF16), TN)

    in_specs = [pl.BlockSpec((ts, ka), lambda q, s: (s, 0))]
    for n in range(len(bs)):
        in_specs.append(pl.BlockSpec((ts, tn), lambda q, s, n=n: (jnp.where(q // per == n, s, 0),
                                                                  jnp.clip(q - n * per, 0, per - 1))))
    return pl.pallas_call(
        body, grid=(N_CHIPS, S // ts), in_specs=in_specs,
        out_specs=pl.BlockSpec((1, ka, tn), lambda q, s: (q, 0, 0)),
        out_shape=jax.ShapeDtypeStruct((N_CHIPS, ka, tn), F32), compiler_params=_cp(2), name=name,
    )(a, *bs)


def _mm_tn_grouped(a, b, groups, w, *, name):
    S = a.shape[0]
    ts = _contraction_rows(S, w * (a.dtype.itemsize + b.dtype.itemsize), w * w)

    def body(a_ref, b_ref, o_ref):
        @pl.when(pl.program_id(1) == 0)
        def _():
            o_ref[...] = jnp.zeros_like(o_ref)

        o_ref[0] += _dot(a_ref[...].astype(BF16), b_ref[...].astype(BF16), TN)

    return pl.pallas_call(
        body, grid=(groups, S // ts),
        in_specs=[pl.BlockSpec((ts, w), lambda g, s: (s, g)), pl.BlockSpec((ts, w), lambda g, s: (s, g))],
        out_specs=pl.BlockSpec((1, w, w), lambda g, s: (g, 0, 0)),
        out_shape=jax.ShapeDtypeStruct((groups, w, w), F32), compiler_params=_cp(2), name=name,
    )(a, b)


def _rms(x, g, *, cb=0, w=None, ts=None, name):
    S = x.shape[0]
    w = w or x.shape[1]
    ts = ts or _row_tile(S)

    def body(x_ref, g_ref, o_ref):
        xv = x_ref[...].astype(F32)
        r = lax.rsqrt(jnp.mean(xv * xv, axis=-1, keepdims=True) + RMS_EPS)
        o_ref[...] = (xv * r * g_ref[...]).astype(o_ref.dtype)

    return pl.pallas_call(
        body, grid=(S // ts,), in_specs=[_rows(ts, w, cb), _const((1, w))], out_specs=_rows(ts, w),
        out_shape=jax.ShapeDtypeStruct((S, w), BF16), compiler_params=_cp(1), name=name,
    )(x, g.reshape(1, w))


def _rms_bwd(x, g, dy, *, cb=0, w=None, res=None, out_dtype=F32, ts=None, name):
    S = x.shape[0]
    w = w or x.shape[1]
    ts = ts or min(S, 512)
    has_res = res is not None

    def body(*refs):
        x_ref, g_ref, dy_ref = refs[:3]
        dx_ref, dg_ref = refs[-2:]
        xv = x_ref[...].astype(F32)
        r = lax.rsqrt(jnp.mean(xv * xv, axis=-1, keepdims=True) + RMS_EPS)
        n = xv * r
        dyv = dy_ref[...].astype(F32)
        dn = dyv * g_ref[...]
        dx = r * (dn - n * jnp.mean(dn * n, axis=-1, keepdims=True))
        if has_res:
            dx = dx + refs[3][...].astype(F32)
        dx_ref[...] = dx.astype(dx_ref.dtype)

        @pl.when(pl.program_id(0) == 0)
        def _():
            dg_ref[...] = jnp.zeros_like(dg_ref)

        dg_ref[...] += jnp.sum(dyv * n, axis=0, keepdims=True)

    ins = [x, g.reshape(1, w), dy] + ([res] if has_res else [])
    in_specs = [_rows(ts, w, cb), _const((1, w)), _rows(ts, w)] + ([_rows(ts, w)] if has_res else [])
    return pl.pallas_call(
        body, grid=(S // ts,), in_specs=in_specs, out_specs=[_rows(ts, w), _const((1, w))],
        out_shape=[jax.ShapeDtypeStruct((S, w), out_dtype), jax.ShapeDtypeStruct((1, w), F32)],
        compiler_params=_cp(1), name=name,
    )(*ins)


HALO = 16


def _pool_counts(i, ts, rows, first_row):
    t = i * ts + first_row + lax.broadcasted_iota(jnp.int32, (rows, 1), 0)
    return [jnp.minimum(t + 1, w).astype(F32) for w in POOL_WINDOWS]


def _even_front(x, g, w_in, pool_w, pool_scale, g_q, w_q, g_kv, w_kv, ctab, stab, *, name):
    S = x.shape[0]
    ts = min(S, 512)

    def body(x_ref, g_ref, win_ref, pw_ref, sc_ref, gq_ref, wq_ref, gkv_ref, wkv_ref, c_ref, s_ref,
             h_ref, z_ref, y_ref, p_ref, cqn_ref, ckvn_ref, q_ref, k_ref, v_ref, tail):
        i = pl.program_id(0)

        def normed(t, gain):
            r = lax.rsqrt(jnp.mean(t * t, axis=-1, keepdims=True) + RMS_EPS)
            return (t * r * gain).astype(BF16)

        h = normed(x_ref[...], g_ref[...])
        h_ref[...] = h
        z = _dot(h, win_ref[...])
        z_ref[...] = z
        u = z[:, :POOL_DIM]
        xe = jnp.concatenate([jnp.where(i > 0, tail[...], 0.0), u], axis=0)
        tail[...] = u[ts - HALO:]
        sums = []
        s = xe
        for sh in (1, 2, 4, 8):
            s = s + pltpu.roll(s, sh, 0)
            sums.append(s)
        cnts = _pool_counts(i, ts, ts, 0)
        for grp in range(4):
            lo, hi = grp * POOL_GROUP, (grp + 1) * POOL_GROUP
            pooled = (sums[grp][HALO:, lo:hi] / cnts[grp] - u[:, lo:hi]).astype(BF16)
            p_ref[:, lo:hi] = pooled
            y_ref[:, lo:hi] = (_dot(pooled, pw_ref[grp]) * sc_ref[:, lo:hi]).astype(y_ref.dtype)
        cqn = normed(z[:, POOL_DIM:POOL_DIM + Q_RANK], gq_ref[...])
        ckvn = normed(z[:, POOL_DIM + Q_RANK:POOL_DIM + Q_RANK + KV_RANK], gkv_ref[...])
        cqn_ref[...] = cqn
        ckvn_ref[...] = ckvn
        q = _dot(cqn, wq_ref[...])
        kv = _dot(ckvn, wkv_ref[...])
        c, sn = c_ref[...], s_ref[...]
        kr = z[:, D_MODEL - HEAD_PAD:]
        kr_rot = kr * c + _rope_partner(kr) * sn
        lane = lax.broadcasted_iota(jnp.int32, (ts, HEAD_PAD), 1)
        for hd in range(MLA_HEADS):
            lo, hi = hd * HEAD_PAD, (hd + 1) * HEAD_PAD
            qh = q[:, lo:hi]
            q_ref[:, lo:hi] = (qh * c + _rope_partner(qh) * sn).astype(q_ref.dtype)
            k_ref[:, lo:hi] = (kv[:, lo:hi] + kr_rot).astype(k_ref.dtype)
            v_ref[:, lo:hi] = jnp.where(lane == V_HEAD, 1.0, kv[:, D_MODEL + lo:D_MODEL + hi]).astype(v_ref.dtype)

    wide = jax.ShapeDtypeStruct((S, D_MODEL), BF16)
    return pl.pallas_call(
        body, grid=(S // ts,),
        in_specs=[_rows(ts, D_MODEL), _const((1, D_MODEL)), _const((D_MODEL, D_MODEL)),
                  _const((4, POOL_GROUP, POOL_GROUP)), _const((1, POOL_DIM)), _const((1, Q_RANK)),
                  _const((Q_RANK, D_MODEL)), _const((1, KV_RANK)), _const((KV_RANK, 2 * D_MODEL)),
                  _rows(ts, HEAD_PAD), _rows(ts, HEAD_PAD)],
        out_specs=[_rows(ts, D_MODEL), _rows(ts, D_MODEL), _rows(ts, POOL_DIM), _rows(ts, POOL_DIM),
                   _rows(ts, Q_RANK), _rows(ts, KV_RANK), _rows(ts, D_MODEL), _rows(ts, D_MODEL), _rows(ts, D_MODEL)],
        out_shape=[wide, jax.ShapeDtypeStruct((S, D_MODEL), F32), jax.ShapeDtypeStruct((S, MIX_DIM), BF16),
                   jax.ShapeDtypeStruct((S, POOL_DIM), BF16), jax.ShapeDtypeStruct((S, Q_RANK), BF16),
                   jax.ShapeDtypeStruct((S, KV_RANK), BF16), wide, wide, wide],
        scratch_shapes=[pltpu.VMEM((HALO, POOL_DIM), F32)], compiler_params=_cp(1), name=name,
    )(x, g.reshape(1, D_MODEL), w_in, pool_w, pool_scale, g_q.reshape(1, Q_RANK), w_q, g_kv.reshape(1, KV_RANK), w_kv,
      ctab, stab)


def _pool_bwd(dmix, pooled, pool_w, pool_scale, *, name):
    S = dmix.shape[0]
    ts = min(S, 512)
    nh = ts // HALO
    last = S // HALO - 1

    def body(dy_ref, dyh_ref, p_ref, w_ref, sc_ref, du_ref, dyp_ref, dsc_ref):
        i = pl.program_id(0)
        dyv = dy_ref[...].astype(F32)
        dyh = jnp.where(i < pl.num_programs(0) - 1, dyh_ref[...].astype(F32), 0.0)
        dye = jnp.concatenate([dyv, dyh], axis=0) * sc_ref[...]
        dypre = dye.astype(BF16)
        dyp_ref[...] = dypre[:ts]
        cnts = _pool_counts(i, ts, ts + HALO, 0)
        n = ts + HALO
        dsc = []
        for g in range(4):
            lo, hi = g * POOL_GROUP, (g + 1) * POOL_GROUP
            ypre = _dot(p_ref[:, lo:hi], w_ref[g])
            dsc.append(jnp.sum(dyv[:, lo:hi] * ypre, axis=0, keepdims=True))
            dpool = _dot(dypre[:, lo:hi], w_ref[g], NT)
            s = dpool / cnts[g]
            for sh in (1, 2, 4, 8)[:g + 1]:
                s = s + pltpu.roll(s, n - sh, 0)
            du_ref[:, lo:hi] = (s[:ts] - dpool[:ts]).astype(du_ref.dtype)

        @pl.when(i == 0)
        def _():
            dsc_ref[...] = jnp.zeros_like(dsc_ref)

        dsc_ref[...] += jnp.concatenate(dsc, axis=1)

    return pl.pallas_call(
        body, grid=(S // ts,),
        in_specs=[_rows(ts, POOL_DIM),
                  pl.BlockSpec((HALO, POOL_DIM), lambda i: (jnp.minimum((i + 1) * nh, last), 0)),
                  _rows(ts, POOL_DIM), _const((4, POOL_GROUP, POOL_GROUP)), _const((1, POOL_DIM))],
        out_specs=[_rows(ts, POOL_DIM), _rows(ts, POOL_DIM), _const((1, POOL_DIM))],
        out_shape=[jax.ShapeDtypeStruct((S, POOL_DIM), BF16)] * 2 + [jax.ShapeDtypeStruct((1, POOL_DIM), F32)],
        compiler_params=_cp(1), name=name,
    )(dmix, dmix, pooled, pool_w, pool_scale)


def _rope_partner(t):
    lane = lax.broadcasted_iota(jnp.int32, t.shape, 1)
    swapped = jnp.where(lane < QK_NOPE + QK_ROPE // 2, pltpu.roll(t, HEAD_PAD - QK_ROPE // 2, 1),
                        pltpu.roll(t, QK_ROPE // 2, 1))
    return jnp.where((lane >= QK_NOPE) & (lane < QK_DIM), swapped, 0.0)


def _rope_bwd(dq_rot, dk_cat, ctab, stab, *, name):
    S = dq_rot.shape[0]
    ts = min(S, 512)

    def body(dq_ref, dk_ref, c_ref, s_ref, dqo_ref, dkr_ref):
        c, s = c_ref[...], s_ref[...]
        for h in range(MLA_HEADS):
            g = dq_ref[:, h * HEAD_PAD:(h + 1) * HEAD_PAD]
            dqo_ref[:, h * HEAD_PAD:(h + 1) * HEAD_PAD] = (g * c + _rope_partner(g * s)).astype(dqo_ref.dtype)
        dk = dk_ref[...]
        g = dk[:, :HEAD_PAD]
        for h in range(1, MLA_HEADS):
            g = g + dk[:, h * HEAD_PAD:(h + 1) * HEAD_PAD]
        lane = lax.broadcasted_iota(jnp.int32, g.shape, 1)
        on_rope = (lane >= QK_NOPE) & (lane < QK_DIM)
        dkr_ref[...] = jnp.where(on_rope, g * c + _rope_partner(g * s), 0.0).astype(dkr_ref.dtype)

    wide = _rows(ts, MLA_HEADS * HEAD_PAD)
    return pl.pallas_call(
        body, grid=(S // ts,), in_specs=[wide, wide, _rows(ts, HEAD_PAD), _rows(ts, HEAD_PAD)],
        out_specs=[wide, _rows(ts, HEAD_PAD)],
        out_shape=[jax.ShapeDtypeStruct((S, MLA_HEADS * HEAD_PAD), BF16), jax.ShapeDtypeStruct((S, HEAD_PAD), BF16)],
        compiler_params=_cp(1), name=name,
    )(dq_rot, dk_cat, ctab, stab)


ATT_SCALE = QK_DIM ** -0.5
LOG2E = math.log2(math.e)


HEADS_PER_STEP = 2
ATT_COL0 = POOL_DIM // HEAD_PAD


FWD_TILE = 1024


def _stat_rows(col):
    return jnp.broadcast_to(col, (col.shape[0], LANES)).T[0:8]


def _retile_rows(rows, tq):
    heads, n8, t = rows.shape
    if t == tq:
        return rows
    flat = rows.reshape(heads, n8 // 8, 8, t)[:, :, 0].reshape(heads, -1, 1, tq)
    return jnp.broadcast_to(flat, (heads, flat.shape[1], 8, tq)).reshape(heads, -1, tq)


def _flash_fwd(q, k, v, mix, *, name):
    S = q.shape[0]
    tq = FWD_TILE if S % FWD_TILE == 0 else min(S, 512)
    nq = S // tq
    hs = HEADS_PER_STEP
    wide = hs * HEAD_PAD

    def body(q_ref, k_ref, v_ref, mix_ref, o_ref, lse_ref):
        qi = pl.program_id(1)
        qv = [q_ref[:, a * HEAD_PAD:(a + 1) * HEAD_PAD] for a in range(hs)]

        def update(m, acc, s, v):
            m_new = jnp.maximum(m, jnp.max(s, axis=-1, keepdims=True))
            p = jnp.exp2((s - m_new) * (ATT_SCALE * LOG2E))
            alpha = jnp.exp2((m - m_new) * (ATT_SCALE * LOG2E))
            return m_new, alpha * acc + _dot(p.astype(BF16), v)

        def step(j, carry, masked):
            off = pl.multiple_of(j * tq, tq)
            out = []
            for a in range(hs):
                head = slice(a * HEAD_PAD, (a + 1) * HEAD_PAD)
                s = _dot(qv[a], k_ref[pl.ds(off, tq), head], NT)
                if masked:
                    row = lax.broadcasted_iota(jnp.int32, (tq, tq), 0)
                    col = lax.broadcasted_iota(jnp.int32, (tq, tq), 1)
                    s = jnp.where(col <= row, s, NEG_INF)
                out.append(update(*carry[a], s, v_ref[pl.ds(off, tq), head]))
            return tuple(out)

        one = (jnp.full((tq, 1), NEG_INF, F32), jnp.zeros((tq, HEAD_PAD), F32))
        carry = step(qi, lax.fori_loop(0, qi, lambda j, c: step(j, c, False), (one,) * hs), True)
        for a in range(hs):
            m, acc = carry[a]
            l = acc[:, V_HEAD:V_HEAD + 1]
            o_ref[:, a * HEAD_PAD:(a + 1) * HEAD_PAD] = (acc / l).astype(o_ref.dtype)
            lse_ref[a] = _stat_rows(m * ATT_SCALE + jnp.log(l))

    blk = pl.BlockSpec((tq, wide), lambda h, i: (i, h))
    full = pl.BlockSpec((S, wide), lambda h, i: (0, h))
    return pl.pallas_call(
        body, grid=(MLA_HEADS // hs, nq), in_specs=[blk, full, full, ANY],
        out_specs=[pl.BlockSpec((tq, wide), lambda h, i: (i, ATT_COL0 // hs + h)),
                   pl.BlockSpec((hs, 8, tq), lambda h, i: (h, i, 0))],
        out_shape=[jax.ShapeDtypeStruct(mix.shape, mix.dtype), jax.ShapeDtypeStruct((MLA_HEADS, nq * 8, tq), F32)],
        input_output_aliases={3: 0}, compiler_params=_cp(2), name=name,
    )(q, k, v, mix)


BWD_TILE = 1024
BWD_HEADS_PER_STEP = 1


def _bwd_tile(S):
    return BWD_TILE if S % BWD_TILE == 0 else min(S, 512)


def _attn_delta(dmix, mix, *, name):
    S = mix.shape[0]
    ts = _bwd_tile(S)
    half = MLA_HEADS // 2
    halves = [_rows(ts, half * HEAD_PAD, 1), _rows(ts, half * HEAD_PAD, 2)]

    def body(do0_ref, do1_ref, o0_ref, o1_ref, d_ref):
        for n, (do_ref, o_ref) in enumerate(((do0_ref, o0_ref), (do1_ref, o1_ref))):
            prod = do_ref[...].astype(F32) * o_ref[...].astype(F32)
            for a in range(half):
                d_ref[n * half + a] = _stat_rows(
                    jnp.sum(prod[:, a * HEAD_PAD:(a + 1) * HEAD_PAD], axis=-1, keepdims=True))

    return pl.pallas_call(
        body, grid=(S // ts,), in_specs=halves + halves,
        out_specs=pl.BlockSpec((MLA_HEADS, 8, ts), lambda i: (0, i, 0)),
        out_shape=jax.ShapeDtypeStruct((MLA_HEADS, (S // ts) * 8, ts), F32), compiler_params=_cp(1), name=name,
    )(dmix, dmix, mix, mix)


def _flash_bwd(q, k, v, dmix, lse_rows, delta_rows, *, name):
    S = q.shape[0]
    tq = _bwd_tile(S)
    nq = S // tq
    hs = BWD_HEADS_PER_STEP
    wide = hs * HEAD_PAD

    def body(q_hbm, do_hbm, lse_ref, dl_ref, k_ref, v_ref, dq_hbm, dk_ref, dv_ref, q_all, do_all, dq_all):
        g, j = pl.program_id(0), pl.program_id(1)
        cols = pl.multiple_of(g * wide, wide)

        @pl.when(j == 0)
        def _():
            pltpu.sync_copy(q_hbm.at[:, pl.ds(cols, wide)], q_all)
            pltpu.sync_copy(do_hbm.at[:, pl.ds(POOL_DIM + cols, wide)], do_all)
            dq_all[...] = jnp.zeros_like(dq_all)

        heads = [slice(a * HEAD_PAD, (a + 1) * HEAD_PAD) for a in range(hs)]
        kv = [k_ref[:, a] for a in heads]
        vv = [v_ref[:, a] for a in heads]

        def block(a, keys, rows, lse2, dl, first_query):
            qv, dov = q_all[rows, heads[a]], do_all[rows, heads[a]]
            st = _dot(kv[a][:keys], qv, NT)
            if first_query is not None:
                krow = lax.broadcasted_iota(jnp.int32, st.shape, 0)
                qcol = lax.broadcasted_iota(jnp.int32, st.shape, 1) + first_query
                st = jnp.where(krow <= qcol, st, NEG_INF)
            pt = jnp.exp2(st * (ATT_SCALE * LOG2E) - lse2)
            dst = (pt * (_dot(vv[a][:keys], dov, NT) - dl)).astype(BF16)
            dq_all[rows, heads[a]] += _dot(dst, kv[a][:keys], TN)
            return _dot(dst, qv), _dot(pt.astype(BF16), dov)

        def stats(a, i):
            off8 = pl.multiple_of(i * 8, 8)
            return lse_ref[a, pl.ds(off8, 8), :][0:1] * LOG2E, dl_ref[a, pl.ds(off8, 8), :][0:1]

        def step(i, carry):
            rows = pl.ds(pl.multiple_of(i * tq, tq), tq)
            out = []
            for a in range(hs):
                dk, dv = block(a, tq, rows, *stats(a, i), None)
                out.append((carry[a][0] + dk, carry[a][1] + dv))
            return tuple(out)

        def diagonal():
            half = tq // 2
            out = []
            for a in range(hs):
                lse2, dl = stats(a, j)
                off = pl.multiple_of(j * tq, tq)
                dk0, dv0 = block(a, half, pl.ds(off, half), lse2[:, :half], dl[:, :half], 0)
                dk1, dv1 = block(a, tq, pl.ds(pl.multiple_of(off + half, half), half), lse2[:, half:], dl[:, half:], half)
                zero = jnp.zeros((tq - half, HEAD_PAD), F32)
                out.append((dk1 + jnp.concatenate([dk0, zero], axis=0), dv1 + jnp.concatenate([dv0, zero], axis=0)))
            return tuple(out)

        carry = lax.fori_loop(j + 1, nq, step, diagonal())
        for a in range(hs):
            dk_ref[:, heads[a]] = carry[a][0] * ATT_SCALE
            dv_ref[:, heads[a]] = carry[a][1]

        @pl.when(j == nq - 1)
        def _():
            dq_all[...] = dq_all[...] * ATT_SCALE
            pltpu.sync_copy(dq_all, dq_hbm.at[:, pl.ds(cols, wide)])

    blk = pl.BlockSpec((tq, wide), lambda g, j: (j, g))
    stat = pl.BlockSpec((hs, nq * 8, tq), lambda g, j: (g, 0, 0))
    full = jax.ShapeDtypeStruct((S, MLA_HEADS * HEAD_PAD), F32)
    return pl.pallas_call(
        body, grid=(MLA_HEADS // hs, nq), in_specs=[ANY, ANY, stat, stat, blk, blk], out_specs=[ANY, blk, blk],
        out_shape=[full, full, full],
        scratch_shapes=[pltpu.VMEM((S, wide), BF16), pltpu.VMEM((S, wide), BF16), pltpu.VMEM((S, wide), F32)],
        compiler_params=_cp(2), name=name,
    )(q, dmix, lse_rows, delta_rows, k, v)


MEM_SCALE = MEM_HEAD_DIM ** -0.5


def _xattn_probs(qh, kh):
    s = _dot(qh, kh, NT) * MEM_SCALE
    e = jnp.exp(s - jnp.max(s, axis=-1, keepdims=True))
    return e / jnp.sum(e, axis=-1, keepdims=True)


def _xa_block_fwd(x, kvm, w_q, w_o, g, *, name):
    S = x.shape[0]
    ts = min(S, 512)
    nm = kvm.shape[0]

    def body(x_ref, kv_ref, wq_ref, wo_ref, g_ref, xo_ref, hx_ref, q_ref, o_ref):
        xv = x_ref[...]
        r = lax.rsqrt(jnp.mean(xv * xv, axis=-1, keepdims=True) + RMS_EPS)
        hx = (xv * r * g_ref[...]).astype(BF16)
        hx_ref[...] = hx
        q = _dot(hx, wq_ref[...]).astype(BF16)
        q_ref[...] = q
        for h in range(MEM_HEADS):
            lo, hi = h * MEM_HEAD_DIM, (h + 1) * MEM_HEAD_DIM
            p = _xattn_probs(q[:, lo:hi], kv_ref[:, lo:hi])
            o_ref[:, lo:hi] = _dot(p.astype(BF16), kv_ref[:, D_MODEL + lo:D_MODEL + hi]).astype(o_ref.dtype)
        xo_ref[...] = xv + _dot(o_ref[...], wo_ref[...])

    square = _const((D_MODEL, D_MODEL))
    act = jax.ShapeDtypeStruct((S, D_MODEL), BF16)
    return pl.pallas_call(
        body, grid=(S // ts,),
        in_specs=[_rows(ts, D_MODEL), _const((nm, 2 * D_MODEL)), square, square, _const((1, D_MODEL))],
        out_specs=[_rows(ts, D_MODEL)] * 4, out_shape=[jax.ShapeDtypeStruct((S, D_MODEL), F32), act, act, act],
        compiler_params=_cp(1), name=name,
    )(x, kvm, w_q, w_o, g.reshape(1, D_MODEL))


def _xa_block_bwd(dxo, x, q, kvm, w_q, w_o, g, *, name):
    S = q.shape[0]
    ts = min(S, 512)
    nm = kvm.shape[0]

    def body(dxo_ref, x_ref, q_ref, kv_ref, wq_ref, wo_ref, g_ref, dx_ref, dq_ref, dkv_ref, dg_ref):
        @pl.when(pl.program_id(0) == 0)
        def _():
            dkv_ref[...] = jnp.zeros_like(dkv_ref)
            dg_ref[...] = jnp.zeros_like(dg_ref)

        dxo = dxo_ref[...]
        do = _dot(dxo.astype(BF16), wo_ref[...], NT).astype(BF16)
        for h in range(MEM_HEADS):
            lo, hi = h * MEM_HEAD_DIM, (h + 1) * MEM_HEAD_DIM
            qh, kh, vh = q_ref[:, lo:hi], kv_ref[:, lo:hi], kv_ref[:, D_MODEL + lo:D_MODEL + hi]
            doh = do[:, lo:hi]
            p = _xattn_probs(qh, kh)
            dp = _dot(doh, vh, NT)
            ds = (p * (dp - jnp.sum(dp * p, axis=-1, keepdims=True)) * MEM_SCALE).astype(BF16)
            dq_ref[:, lo:hi] = _dot(ds, kh).astype(dq_ref.dtype)
            dkv_ref[:, lo:hi] += _dot(ds, qh, TN)
            dkv_ref[:, D_MODEL + lo:D_MODEL + hi] += _dot(p.astype(BF16), doh, TN)
        dx, dg = _norm_bwd_epilogue(0)([_dot(dq_ref[...], wq_ref[...], NT)], [x_ref[...], dxo, g_ref[...]])
        dx_ref[...] = dx
        dg_ref[...] += dg

    square = _const((D_MODEL, D_MODEL))
    return pl.pallas_call(
        body, grid=(S // ts,),
        in_specs=[_rows(ts, D_MODEL), _rows(ts, D_MODEL), _rows(ts, D_MODEL), _const((nm, 2 * D_MODEL)), square,
                  square, _const((1, D_MODEL))],
        out_specs=[_rows(ts, D_MODEL), _rows(ts, D_MODEL), _const((nm, 2 * D_MODEL)), _const((1, D_MODEL))],
        out_shape=[jax.ShapeDtypeStruct((S, D_MODEL), F32), jax.ShapeDtypeStruct((S, D_MODEL), BF16),
                   jax.ShapeDtypeStruct((nm, 2 * D_MODEL), F32), jax.ShapeDtypeStruct((1, D_MODEL), F32)],
        compiler_params=_cp(1), name=name,
    )(dxo, x, q, kvm, w_q, w_o, g.reshape(1, D_MODEL))


CONV_HALO = 8


def _sigmoid(x):
    return 0.5 * jnp.tanh(0.5 * x) + 0.5


def _softplus(x):
    return jnp.maximum(x, 0.0) + jnp.log(1.0 + jnp.exp(-jnp.abs(x)))


def _neg_expm1(x):
    series = -x * (1.0 + x * (1.0 / 2) * (1.0 + x * (1.0 / 3) * (1.0 + x * (1.0 / 4) * (1.0 + x * (1.0 / 5)))))
    return jnp.where(x > -0.05, series, 1.0 - jnp.exp(x))


GELU_C = math.sqrt(2.0 / math.pi)


def _gelu(x):
    return 0.5 * x * (1.0 + jnp.tanh(GELU_C * (x + 0.044715 * x * x * x)))


def _gelu_grad(x):
    t = jnp.tanh(GELU_C * (x + 0.044715 * x * x * x))
    return 0.5 * (1.0 + t) + 0.5 * x * (1.0 - t * t) * GELU_C * (1.0 + 3 * 0.044715 * x * x)


def _lru_gates(xc, wr_ref, br, wi_ref, bi, sp, reset):
    xcb = xc.astype(BF16)
    pr, pi = [], []
    for h in range(LRU_HEADS):
        lo, hi = h * LRU_HEAD_DIM, (h + 1) * LRU_HEAD_DIM
        pr.append(_dot(xcb[:, lo:hi], wr_ref[h]))
        pi.append(_dot(xcb[:, lo:hi], wi_ref[h]))
    r = _sigmoid(jnp.concatenate(pr, axis=1) + br)
    ig = _sigmoid(jnp.concatenate(pi, axis=1) + bi)
    log_a = -LRU_C * r * sp
    a = jnp.where(reset, 0.0, jnp.exp(log_a))
    mult = jnp.where(reset, 1.0, jnp.sqrt(jnp.maximum(_neg_expm1(2.0 * log_a), 0.0)))
    return r, ig, a, mult


SUBLANES = 8


def _compose_groups(a, b, reverse):
    n = a.shape[0]
    row = lax.broadcasted_iota(jnp.int32, a.shape, 0) % SUBLANES
    for s in (1, 2, 4):
        inside = (row < SUBLANES - s) if reverse else (row >= s)
        shift = n - s if reverse else s
        a_s = jnp.where(inside, pltpu.roll(a, shift, 0), 1.0)
        b_s = jnp.where(inside, pltpu.roll(b, shift, 0), 0.0)
        b = a * b_s + b
        a = a * a_s
    return a, b


def _chain_groups(a_buf, h_ref, state, reverse):
    groups = a_buf.shape[0] // SUBLANES

    def group(g, h_in):
        off = pl.multiple_of((groups - 1 - g if reverse else g) * SUBLANES, SUBLANES)
        h = a_buf[pl.ds(off, SUBLANES), :] * h_in + h_ref[pl.ds(off, SUBLANES), :]
        h_ref[pl.ds(off, SUBLANES), :] = h
        return jnp.broadcast_to(h[0:1] if reverse else h[SUBLANES - 1:SUBLANES], h.shape)

    return lax.fori_loop(0, groups, group, state, unroll=4)[0:1]


def _lru_fwd(z, reset, conv_w, conv_b, w_r, b_r, w_i, b_i, lam, *, name):
    S = z.shape[0]
    ts = min(S, 512)
    nh = ts // CONV_HALO
    W = D_MODEL

    def body(gate_ref, xb_ref, halo_ref, rs_ref, cw_ref, cb_ref, wr_ref, br_ref, wi_ref, bi_ref, lam_ref,
             xc_ref, h_ref, y_ref, a_buf, carry):
        i = pl.program_id(0)

        @pl.when(i == 0)
        def _():
            carry[...] = jnp.zeros_like(carry)

        halo = jnp.where(i > 0, halo_ref[...], 0.0)
        xe = jnp.concatenate([halo, xb_ref[...]], axis=0)
        xc = cb_ref[...] + cw_ref[3:4, :] * xe[CONV_HALO:]
        for kk in range(CONV_WIDTH - 1):
            xc = xc + cw_ref[kk:kk + 1, :] * pltpu.roll(xe, CONV_WIDTH - 1 - kk, 0)[CONV_HALO:]
        xc_ref[...] = xc
        reset = rs_ref[...] > 0.5
        _, ig, a, mult = _lru_gates(xc, wr_ref, br_ref[...], wi_ref, bi_ref[...], _softplus(-lam_ref[...]), reset)
        a_buf[...], h_ref[...] = _compose_groups(a, mult * (ig * xc), False)
        carry[...] = _chain_groups(a_buf, h_ref, jnp.broadcast_to(carry[...], (SUBLANES, W)), False)
        y_ref[...] = (_gelu(gate_ref[...]) * h_ref[...]).astype(y_ref.dtype)

    vec = _const((1, W))
    gw = _const((LRU_HEADS, LRU_HEAD_DIM, LRU_HEAD_DIM))
    return pl.pallas_call(
        body, grid=(S // ts,),
        in_specs=[_rows(ts, W, 0), _rows(ts, W, 1),
                  pl.BlockSpec((CONV_HALO, W), lambda i: (jnp.maximum(i * nh - 1, 0), 1)),
                  _rows(ts, 1), _const((CONV_WIDTH, W)), vec, gw, vec, gw, vec, vec],
        out_specs=[_rows(ts, W)] * 3,
        out_shape=[jax.ShapeDtypeStruct((S, W), F32), jax.ShapeDtypeStruct((S, W), F32),
                   jax.ShapeDtypeStruct((S, W), BF16)],
        scratch_shapes=[pltpu.VMEM((ts, W), F32), pltpu.VMEM((1, W), F32)],
        compiler_params=_cp(1), name=name,
    )(z, z, z, reset, conv_w, conv_b, w_r, b_r, w_i, b_i, lam)


def _lru_bwd(dy, z, xc, hseq, reset, w_r, b_r, w_i, b_i, lam, *, name):
    S = z.shape[0]
    ts = min(S, 512)
    nt = S // ts
    nh = ts // CONV_HALO
    W = D_MODEL

    def body(dy_ref, gate_ref, xc_ref, h_ref, hh_ref, rs_ref, wr_ref, br_ref, wi_ref, bi_ref, lam_ref,
             dg_ref, dxc_ref, dpr_ref, dpi_ref, acc_ref, a_buf, dh_buf, carry):
        i = pl.program_id(0)
        tile = nt - 1 - i

        @pl.when(i == 0)
        def _():
            carry[...] = jnp.zeros_like(carry)
            acc_ref[...] = jnp.zeros_like(acc_ref)

        xc = xc_ref[...]
        lam_v = lam_ref[...]
        sp = _softplus(-lam_v)
        reset = rs_ref[...] > 0.5
        r, ig, a, mult = _lru_gates(xc, wr_ref, br_ref[...], wi_ref, bi_ref[...], sp, reset)
        gate = gate_ref[...]
        dyv = dy_ref[...].astype(F32)
        h = h_ref[...]
        dg_ref[...] = (dyv * h * _gelu_grad(gate)).astype(dg_ref.dtype)
        last_row = lax.broadcasted_iota(jnp.int32, a.shape, 0) == ts - 1
        a_buf[...], dh_buf[...] = _compose_groups(jnp.where(last_row, 1.0, pltpu.roll(a, ts - 1, 0)),
                                                  dyv * _gelu(gate), True)
        _chain_groups(a_buf, dh_buf, jnp.broadcast_to(carry[...], (SUBLANES, W)), True)
        dh = dh_buf[...]
        carry[...] = a[0:1] * dh[0:1]
        hh = jnp.where(tile > 0, hh_ref[...], 0.0)
        h_prev = pltpu.roll(jnp.concatenate([hh, h], axis=0), 1, 0)[CONV_HALO:]
        da = dh * h_prev
        bx = ig * xc
        dmult = dh * bx
        dbx = dh * mult
        di = dbx * xc
        dlog_a = jnp.where(reset, 0.0, da * a - dmult * a * a / jnp.maximum(mult, 1e-30))
        dr = dlog_a * (-LRU_C) * sp
        dpre_r = dr * r * (1.0 - r)
        dpre_i = di * ig * (1.0 - ig)
        dprb, dpib = dpre_r.astype(BF16), dpre_i.astype(BF16)
        dpr_ref[...] = dprb
        dpi_ref[...] = dpib
        back = []
        for hd in range(LRU_HEADS):
            lo, hi = hd * LRU_HEAD_DIM, (hd + 1) * LRU_HEAD_DIM
            back.append(_dot(dprb[:, lo:hi], wr_ref[hd], NT) + _dot(dpib[:, lo:hi], wi_ref[hd], NT))
        dxc_ref[...] = dbx * ig + jnp.concatenate(back, axis=1)
        dlam = jnp.sum(dlog_a * (-LRU_C) * r, axis=0, keepdims=True) * (-_sigmoid(-lam_v))
        acc_ref[0:1, :] += jnp.sum(dpre_r, axis=0, keepdims=True)
        acc_ref[1:2, :] += jnp.sum(dpre_i, axis=0, keepdims=True)
        acc_ref[2:3, :] += dlam

    rev = lambda cb: pl.BlockSpec((ts, W), lambda i: (nt - 1 - i, cb))
    vec = _const((1, W))
    gw = _const((LRU_HEADS, LRU_HEAD_DIM, LRU_HEAD_DIM))
    return pl.pallas_call(
        body, grid=(nt,),
        in_specs=[rev(0), rev(0), rev(0), rev(0),
                  pl.BlockSpec((CONV_HALO, W), lambda i: (jnp.maximum((nt - 1 - i) * nh - 1, 0), 0)),
                  pl.BlockSpec((ts, 1), lambda i: (nt - 1 - i, 0)), gw, vec, gw, vec, vec],
        out_specs=[rev(0), rev(0), rev(0), rev(0), _const((8, W))],
        out_shape=[jax.ShapeDtypeStruct((S, W), BF16), jax.ShapeDtypeStruct((S, W), F32),
                   jax.ShapeDtypeStruct((S, W), BF16), jax.ShapeDtypeStruct((S, W), BF16),
                   jax.ShapeDtypeStruct((8, W), F32)],
        scratch_shapes=[pltpu.VMEM((ts, W), F32), pltpu.VMEM((ts, W), F32), pltpu.VMEM((1, W), F32)],
        compiler_params=_cp(1), name=name,
    )(dy, z, xc, hseq, hseq, reset, w_r, b_r, w_i, b_i, lam)


def _conv_bwd(dxc, z, conv_w, *, name):
    S = dxc.shape[0]
    ts = min(S, 512)
    nh = ts // CONV_HALO
    last = S // CONV_HALO - 1
    W = D_MODEL
    n = ts + CONV_HALO

    def body(d_ref, dn_ref, xb_ref, xp_ref, cw_ref, dxb_ref, acc_ref):
        i = pl.program_id(0)

        @pl.when(i == 0)
        def _():
            acc_ref[...] = jnp.zeros_like(acc_ref)

        d = d_ref[...]
        de = jnp.concatenate([d, jnp.where(i < pl.num_programs(0) - 1, dn_ref[...], 0.0)], axis=0)
        xe = jnp.concatenate([jnp.where(i > 0, xp_ref[...], 0.0), xb_ref[...]], axis=0)
        dxb = cw_ref[3:4, :] * d
        acc_ref[3:4, :] += jnp.sum(d * xe[CONV_HALO:], axis=0, keepdims=True)
        for kk in range(CONV_WIDTH - 1):
            sh = CONV_WIDTH - 1 - kk
            dxb = dxb + cw_ref[kk:kk + 1, :] * pltpu.roll(de, n - sh, 0)[:ts]
            acc_ref[kk:kk + 1, :] += jnp.sum(d * pltpu.roll(xe, sh, 0)[CONV_HALO:], axis=0, keepdims=True)
        dxb_ref[...] = dxb.astype(dxb_ref.dtype)
        acc_ref[4:5, :] += jnp.sum(d, axis=0, keepdims=True)

    return pl.pallas_call(
        body, grid=(S // ts,),
        in_specs=[_rows(ts, W), pl.BlockSpec((CONV_HALO, W), lambda i: (jnp.minimum((i + 1) * nh, last), 0)),
                  _rows(ts, W, 1), pl.BlockSpec((CONV_HALO, W), lambda i: (jnp.maximum(i * nh - 1, 0), 1)),
                  _const((CONV_WIDTH, W))],
        out_specs=[_rows(ts, W), _const((8, W))],
        out_shape=[jax.ShapeDtypeStruct((S, W), BF16), jax.ShapeDtypeStruct((8, W), F32)],
        compiler_params=_cp(1), name=name,
    )(dxc, dxc, z, z, conv_w)


def _loss_head(x, g, target, *, name):
    S, D = x.shape
    ts = _row_tile(S)

    def body(x_ref, g_ref, t_ref, dx_ref, dg_ref, l_ref):
        @pl.when(pl.program_id(0) == 0)
        def _():
            dg_ref[...] = jnp.zeros_like(dg_ref)
            l_ref[...] = jnp.zeros_like(l_ref)

        xv = x_ref[...]
        r = lax.rsqrt(jnp.mean(xv * xv, axis=-1, keepdims=True) + RMS_EPS)
        n = xv * r
        err = n * g_ref[...] - t_ref[...]
        l_ref[...] += 0.5 * jnp.sum(jnp.sum(err * err, axis=-1, keepdims=True) * (1.0 / D), axis=0, keepdims=True)
        dy = err * (1.0 / D)
        dn = dy * g_ref[...]
        dx_ref[...] = r * (dn - n * jnp.mean(dn * n, axis=-1, keepdims=True))
        dg_ref[...] += jnp.sum(dy * n, axis=0, keepdims=True)

    return pl.pallas_call(
        body, grid=(S // ts,), in_specs=[_rows(ts, D), _const((1, D)), _rows(ts, D)],
        out_specs=[_rows(ts, D), _const((1, D)), _const((8, LANES))],
        out_shape=[jax.ShapeDtypeStruct((S, D), F32), jax.ShapeDtypeStruct((1, D), F32),
                   jax.ShapeDtypeStruct((8, LANES), F32)],
        compiler_params=_cp(1), name=name,
    )(x, g.reshape(1, D), target)


def _adamw(w, ga, gb, m, v, *, name):
    shape = w.shape
    cols = shape[-1]
    rows = w.size // cols
    br = rows
    if rows * cols * 4 > (1 << 20):
        br = max(d for d in range(8, rows + 1, 8) if rows % d == 0 and d * cols * 4 <= (1 << 20))

    def body(w_ref, ga_ref, gb_ref, m_ref, v_ref, g_ref, d_ref, mo_ref, vo_ref):
        gv = ga_ref[...] + gb_ref[...]
        g_ref[...] = gv
        mn = ADAM_B1 * m_ref[...] + (1.0 - ADAM_B1) * gv
        vn = ADAM_B2 * v_ref[...] + (1.0 - ADAM_B2) * (gv * gv)
        m_hat = mn / (1.0 - ADAM_B1 ** ADAM_STEP)
        v_hat = vn / (1.0 - ADAM_B2 ** ADAM_STEP)
        d_ref[...] = -ADAM_LR * (m_hat / (jnp.sqrt(v_hat) + ADAM_EPS) + ADAM_WD * w_ref[...])
        mo_ref[...] = mn
        vo_ref[...] = vn

    spec = _rows(br, cols)
    outs = pl.pallas_call(
        body, grid=(rows // br,), in_specs=[spec] * 5, out_specs=[spec] * 4,
        out_shape=[jax.ShapeDtypeStruct((rows, cols), F32)] * 4, compiler_params=_cp(1), name=name,
    )(*[t.reshape(rows, cols) for t in (w, ga, gb, m, v)])
    return [o.reshape(shape) for o in outs]


def _pad_heads(w, width):
    k = w.shape[0]
    return jnp.pad(w.reshape(k, MLA_HEADS, width), ((0, 0), (0, 0), (0, HEAD_PAD - width))).reshape(k, -1)


def _unpad_heads(w, width):
    k = w.shape[0]
    return w.reshape(k, MLA_HEADS, HEAD_PAD)[:, :, :width].reshape(k, MLA_HEADS * width)


def _rope_tables(positions):
    inv_freq = ROPE_BASE ** (-jnp.arange(0, QK_ROPE, 2, dtype=F32) / QK_ROPE)
    ang = positions.astype(F32)[:, None] * inv_freq
    cos, sin = jnp.cos(ang), jnp.sin(ang)
    S = positions.shape[0]
    ones, zeros = jnp.ones((S, QK_NOPE), F32), jnp.zeros((S, QK_NOPE), F32)
    ctab = jnp.concatenate([ones, cos, cos, ones[:, :HEAD_PAD - QK_DIM]], axis=1)
    stab = jnp.concatenate([zeros, -sin, sin, zeros[:, :HEAD_PAD - QK_DIM]], axis=1)
    return ctab, stab


def _memory_block(x, mem, W, layer, tag):
    mn = _rms(mem, W["xa_norm_mem"][layer], name=f"{tag}_xa_norm_mem")
    kvm = _mm(mn, [(W["xa_w_kv"][layer], 0, 0)], _first, [(2 * D_MODEL, BF16, 0)], tn=2 * D_MODEL, nj=1,
              name=f"{tag}_xa_kv")[0]
    xo, hx, qx, o = _xa_block_fwd(x, kvm, W["xa_w_q"][layer], W["xa_w_o"][layer], W["xa_norm_x"][layer],
                                  name=f"{tag}_xa_fwd")
    return xo, (x, hx, qx, mn, kvm, o)


def _memory_block_bwd(dxo, mem, W, layer, saved, tag, grads):
    x, hx, qx, mn, kvm, o = saved
    wq, wkv, wo = W["xa_w_q"][layer], W["xa_w_kv"][layer], W["xa_w_o"][layer]
    grads["xa_w_o"][layer] = _owner_major(_mm_tn(o, dxo, name=f"{tag}_xa_dwo"), 0)
    dx, dqx, dkvm, dg = _xa_block_bwd(dxo, x, qx, kvm, wq, wo, W["xa_norm_x"][layer], name=f"{tag}_xa_bwd")
    grads["xa_w_q"][layer] = _owner_major(_mm_tn(hx, dqx, name=f"{tag}_xa_dwq"), 0)
    grads["xa_norm_x"][layer] = dg[0]
    dmn = _mm(dkvm, [(wkv, 0, 0)], _first, [(D_MODEL, F32, 0)], nt=True, tn=D_MODEL, nj=1, name=f"{tag}_xa_dmn")[0]
    grads["xa_w_kv"][layer] = _mm_tn_owners(mn, [dkvm], name=f"{tag}_xa_dwkv")
    _, dgm = _rms_bwd(mem, W["xa_norm_mem"][layer], dmn, name=f"{tag}_xa_norm_mem_bwd")
    grads["xa_norm_mem"][layer] = dgm[0]
    return dx


FF_TN = D_FF // 2

def _silu_mul(accs, extras):
    g, u = accs
    return [g * _sigmoid(g) * u, g, u]


def _silu_mul_bwd(accs, extras):
    da = accs[0]
    g, u = extras[0].astype(F32), extras[1].astype(F32)
    sg = _sigmoid(g)
    return [da * u * sg * (1.0 + g * (1.0 - sg)), da * g * sg]


def _ffn_block(x, W, layer, tag):
    hf = _rms(x, W["ffn_norm"][layer], name=f"{tag}_ffn_norm")
    wgu, wd = W["ffn_w_gate_up"][layer], W["ffn_w_down"][layer]
    act, g, u = _mm(hf, [(wgu, 0, 0), (wgu, 0, 2)], _silu_mul, [(D_FF, BF16, 0)] * 3, tn=FF_TN, nj=2,
                    name=f"{tag}_ffn_up")
    xo = _mm(act, [(wd, 0, 0)], _add_res, [(D_MODEL, F32, 0)], extras=[(x, 0)], tn=D_MODEL, nj=1,
             name=f"{tag}_ffn_down")[0]
    return xo, (x, hf, act, g, u)


def _ffn_block_bwd(dxo, W, layer, saved, tag, grads):
    x, hf, act, g, u = saved
    wgu, wd = W["ffn_w_gate_up"][layer], W["ffn_w_down"][layer]
    dg, du = _mm(dxo, [(wd, 0, 0)], _silu_mul_bwd, [(D_FF, BF16, 0)] * 2, nt=True, extras=[(g, 0), (u, 0)], tn=FF_TN,
                 nj=2, name=f"{tag}_ffn_dact")
    grads["ffn_w_down"][layer] = _owner_major(_mm_tn(act, dxo, tk=FF_TN, name=f"{tag}_ffn_dwd"), 0)
    dx, dgn = _mm(dg, [(wgu, 0, 0)], _norm_bwd_epilogue(0), [(D_MODEL, F32, 0)], nt=True, also=(du, (wgu, 0, 1)),
                  extras=[(x, 0), (dxo, 0)], rows=[W["ffn_norm"][layer].reshape(1, D_MODEL)],
                  sums=[D_MODEL], tn=D_MODEL, nj=1, name=f"{tag}_ffn_dhf")
    grads["ffn_w_gate_up"][layer] = _mm_tn_owners(hf, [dg, du], name=f"{tag}_ffn_dwgu")
    grads["ffn_norm"][layer] = dgn[0]
    return dx


def _even_block(x, tabs, W, tag):
    ctab, stab = tabs
    w_in = W["ev_w_in"][0]
    zero = jnp.zeros((D_MODEL, QK_NOPE), BF16)
    w_in_pad = jnp.concatenate([w_in[:, :896], zero, w_in[:, 896:], zero[:, :HEAD_PAD - QK_DIM]], axis=1)
    w_q_pad = _pad_heads(W["ev_w_q_up"][0], QK_DIM)
    wkv = W["ev_w_kv_up"][0].reshape(KV_RANK, MLA_HEADS, QK_NOPE + V_HEAD)
    w_kv_pad = jnp.concatenate([_pad_heads(wkv[:, :, :QK_NOPE].reshape(KV_RANK, -1), QK_NOPE),
                                _pad_heads(wkv[:, :, QK_NOPE:].reshape(KV_RANK, -1), V_HEAD)], axis=1)
    w_out = W["ev_w_out"][0]
    w_att = jnp.pad(w_out[POOL_DIM:].reshape(MLA_HEADS, V_HEAD, D_MODEL), ((0, 0), (0, HEAD_PAD - V_HEAD), (0, 0)))
    w_out_pad = jnp.concatenate([w_out[:POOL_DIM], w_att.reshape(MLA_HEADS * HEAD_PAD, D_MODEL)], axis=0)
    pool_w = W["ev_pool_w"][0].astype(BF16)
    pool_scale = W["ev_pool_scale"]

    h, z, mix, pooled, cqn, ckvn, q_rot, k_cat, v_pad = _even_front(
        x, W["ev_norm"][0], w_in_pad, pool_w, pool_scale, W["ev_q_norm"][0], w_q_pad, W["ev_kv_norm"][0], w_kv_pad,
        ctab, stab, name=f"{tag}_front")
    mix, lse = _flash_fwd(q_rot, k_cat, v_pad, mix, name=f"{tag}_attn")
    xo = _mm(mix, [(w_out_pad, 0, 0)], _add_res, [(D_MODEL, F32, 0)], extras=[(x, 0)], tn=D_MODEL, nj=1,
             name=f"{tag}_out")[0]
    saved = (x, h, z, pooled, cqn, ckvn, q_rot, k_cat, v_pad, lse, mix,
             (w_in_pad, w_q_pad, w_kv_pad, w_out_pad, pool_w, pool_scale))
    return xo, saved


def _even_out_grad(dxo, saved, tag):
    mix = saved[10]
    dw_out_pad = _mm_tn(mix, dxo, tk=MIX_DIM // 3, name=f"{tag}_dw_out")
    datt = dw_out_pad[POOL_DIM:].reshape(MLA_HEADS, HEAD_PAD, D_MODEL)[:, :V_HEAD].reshape(-1, D_MODEL)
    return [_owner_major(jnp.concatenate([dw_out_pad[:POOL_DIM], datt], axis=0), 0)]


def _even_block_bwd(dxo, tabs, W, saved, tag, grads, token=None):
    ctab, stab = tabs
    x, h, z, pooled, cqn, ckvn, q_rot, k_cat, v_pad, lse, mix, wts = saved
    w_in_pad, w_q_pad, w_kv_pad, w_out_pad, pool_w, pool_scale = wts
    if token is not None:
        w_out_pad = w_out_pad + token[0:1, 0:1].astype(BF16)
    dmix = _mm(dxo, [(w_out_pad, 0, 0)], _first, [(MIX_DIM, BF16, 0)], nt=True, tn=MIX_DIM, nj=1,
               name=f"{tag}_dmix")[0]
    delta = _attn_delta(dmix, mix, name=f"{tag}_delta")
    dq_rot, dk_cat, dv_pad = _flash_bwd(q_rot, k_cat, v_pad, dmix, _retile_rows(lse, delta.shape[2]), delta,
                                        name=f"{tag}_attn_bwd")
    dq_pad, dkr = _rope_bwd(dq_rot, dk_cat, ctab, stab, name=f"{tag}_rope_bwd")
    dw_q_pad = _mm_tn(cqn, dq_pad, name=f"{tag}_dw_q_up")
    grads["ev_w_q_up"] = [_owner_major(_unpad_heads(dw_q_pad, QK_DIM), 1)]
    dcqn = _mm(dq_pad, [(w_q_pad, 0, 0)], _first, [(Q_RANK, F32, 0)], nt=True, tn=Q_RANK, nj=1, name=f"{tag}_dcqn")[0]
    dwk = _unpad_heads(_mm_tn(ckvn, dk_cat, name=f"{tag}_dw_k_up"), QK_NOPE).reshape(KV_RANK, MLA_HEADS, QK_NOPE)
    dwv = _unpad_heads(_mm_tn(ckvn, dv_pad, name=f"{tag}_dw_v_up"), V_HEAD).reshape(KV_RANK, MLA_HEADS, V_HEAD)
    grads["ev_w_kv_up"] = [_owner_major(jnp.concatenate([dwk, dwv], axis=2).reshape(KV_RANK, -1), 1)]
    dckvn = _mm(dk_cat, [(w_kv_pad, 0, 0)], _first, [(KV_RANK, F32, 0)], nt=True, tn=KV_RANK, nj=1,
                name=f"{tag}_dckvn_k")[0]
    dckvn = _mm(dv_pad, [(w_kv_pad, 0, 1)], _add_res, [(KV_RANK, F32, 0)], nt=True, extras=[(dckvn, 0)], tn=KV_RANK,
                nj=1, name=f"{tag}_dckvn_v")[0]
    dcq, dgq = _rms_bwd(z, W["ev_q_norm"][0], dcqn, cb=2, w=Q_RANK, out_dtype=BF16, name=f"{tag}_q_norm_bwd")
    dckv, dgkv = _rms_bwd(z, W["ev_kv_norm"][0], dckvn, cb=6, w=KV_RANK, out_dtype=BF16, name=f"{tag}_kv_norm_bwd")
    grads["ev_q_norm"], grads["ev_kv_norm"] = dgq, dgkv
    du, dypre, dscale = _pool_bwd(dmix, pooled, pool_w, pool_scale, name=f"{tag}_pool_bwd")
    grads["ev_pool_scale"] = dscale
    grads["ev_pool_w"] = _mm_tn_grouped(pooled, dypre, 4, POOL_GROUP, name=f"{tag}_dpool_w")[None]
    dz = jnp.concatenate([du, dcq, dckv, dkr], axis=1)
    dw_in_pad = _mm_tn(h, dz, name=f"{tag}_dw_in")
    grads["ev_w_in"] = [_owner_major(jnp.concatenate([dw_in_pad[:, :896], dw_in_pad[:, 960:992]], axis=1), 0)]
    dx, dgn = _mm(dz, [(w_in_pad, 0, 0)], _norm_bwd_epilogue(0), [(D_MODEL, F32, 0)], nt=True,
                  extras=[(x, 0), (dxo, 0)], rows=[W["ev_norm"][0].reshape(1, D_MODEL)], sums=[D_MODEL], tn=D_MODEL,
                  nj=1, name=f"{tag}_dh")
    grads["ev_norm"] = dgn
    return dx


def _odd_block(x, reset, W, tag):
    h = _rms(x, W["od_norm"][0], name=f"{tag}_norm")
    z = _mm(h, [(W["od_w_in"][0], 0, 0)], _first, [(2 * D_MODEL, F32, 0)], tn=D_MODEL, nj=2, name=f"{tag}_in")[0]
    w_r, w_i = W["od_w_rgate"][0], W["od_w_igate"][0]
    vecs = [W[n].reshape(1, D_MODEL) for n in ("od_conv_b", "od_b_rgate", "od_b_igate", "od_lambda")]
    xc, hseq, y = _lru_fwd(z, reset, W["od_conv_w"][0], vecs[0], w_r, vecs[1], w_i, vecs[2], vecs[3],
                           name=f"{tag}_lru")
    xo = _mm(y, [(W["od_w_out"][0], 0, 0)], _add_res, [(D_MODEL, F32, 0)], extras=[(x, 0)], tn=D_MODEL, nj=1,
             name=f"{tag}_out")[0]
    return xo, (x, h, z, xc, hseq, y, vecs)


def _odd_block_bwd(dxo, reset, W, saved, tag, grads):
    x, h, z, xc, hseq, y, vecs = saved
    w_r, w_i = W["od_w_rgate"][0], W["od_w_igate"][0]
    dy = _mm(dxo, [(W["od_w_out"][0], 0, 0)], _first, [(D_MODEL, F32, 0)], nt=True, tn=D_MODEL, nj=1,
             name=f"{tag}_dy")[0]
    grads["od_w_out"] = [_owner_major(_mm_tn(y, dxo, name=f"{tag}_dw_out"), 0)]
    dgate, dxc, dpr, dpi, acc = _lru_bwd(dy, z, xc, hseq, reset, w_r, vecs[1], w_i, vecs[2], vecs[3],
                                         name=f"{tag}_lru_bwd")
    grads["od_b_rgate"], grads["od_b_igate"], grads["od_lambda"] = acc[0:1], acc[1:2], acc[2:3]
    grads["od_w_rgate"] = [_owner_major(_mm_tn_grouped(xc, dpr, LRU_HEADS, LRU_HEAD_DIM, name=f"{tag}_dw_rgate"), 1)]
    grads["od_w_igate"] = [_owner_major(_mm_tn_grouped(xc, dpi, LRU_HEADS, LRU_HEAD_DIM, name=f"{tag}_dw_igate"), 1)]
    dxb, cacc = _conv_bwd(dxc, z, W["od_conv_w"][0], name=f"{tag}_conv_bwd")
    grads["od_conv_w"], grads["od_conv_b"] = cacc[None, 0:4], cacc[4:5]
    dz = jnp.concatenate([dgate, dxb], axis=1)
    grads["od_w_in"] = [_mm_tn_owners(h, [dz], name=f"{tag}_dw_in")]
    dx, dgn = _mm(dz, [(W["od_w_in"][0], 0, 0)], _norm_bwd_epilogue(0), [(D_MODEL, F32, 0)], nt=True,
                  extras=[(x, 0), (dxo, 0)], rows=[W["od_norm"][0].reshape(1, D_MODEL)], sums=[D_MODEL], tn=D_MODEL,
                  nj=1, name=f"{tag}_dh")
    grads["od_norm"] = dgn
    return dx


def _local_step(x, mem, positions, target, W, later_weights=None, exchange_earlier=None):
    tabs = _rope_tables(positions)
    reset = (positions == 0).astype(F32)[:, None]
    grads = {n: [None, None] for n in ("xa_norm_x", "xa_norm_mem", "xa_w_q", "xa_w_kv", "xa_w_o", "ffn_norm",
                                       "ffn_w_gate_up", "ffn_w_down")}
    x1, s_even = _even_block(x, tabs, W, "l0_even")
    if later_weights is not None:
        W = {**W, **later_weights(x1)}
    x2, s_xa0 = _memory_block(x1, mem, W, 0, "l0")
    x3, s_ff0 = _ffn_block(x2, W, 0, "l0")
    x4, s_odd = _odd_block(x3, reset, W, "l1_odd")
    x5, s_xa1 = _memory_block(x4, mem, W, 1, "l1")
    x6, s_ff1 = _ffn_block(x5, W, 1, "l1")
    d, dgf, loss = _loss_head(x6, W["final_norm"], target, name="loss_head")
    grads["final_norm"] = dgf[0]
    d = _ffn_block_bwd(d, W, 1, s_ff1, "l1", grads)
    d = _memory_block_bwd(d, mem, W, 1, s_xa1, "l1", grads)
    d = _odd_block_bwd(d, reset, W, s_odd, "l1_odd", grads)
    d = _ffn_block_bwd(d, W, 0, s_ff0, "l0", grads)
    d = _memory_block_bwd(d, mem, W, 0, s_xa0, "l0", grads)
    grads["ev_w_out"] = _even_out_grad(d, s_even, "l0_even")
    token = exchange_earlier(grads) if exchange_earlier is not None else None
    d = _even_block_bwd(d, tabs, W, s_even, "l0_even", grads, token)
    big = {n: grads.pop(n) for n in MATMUL_WEIGHTS}
    for n, v in grads.items():
        if isinstance(v, list):
            grads[n] = jnp.stack(v)
    return loss[0, 0], d, big, grads


WEIGHTS = ("ev_norm", "ev_w_in", "ev_pool_w", "ev_pool_scale", "ev_q_norm", "ev_w_q_up", "ev_kv_norm", "ev_w_kv_up",
           "ev_w_out", "od_norm", "od_w_in", "od_conv_w", "od_conv_b", "od_w_rgate", "od_b_rgate", "od_w_igate",
           "od_b_igate", "od_lambda", "od_w_out", "xa_norm_x", "xa_norm_mem", "xa_w_q", "xa_w_kv", "xa_w_o",
           "ffn_norm", "ffn_w_gate_up", "ffn_w_down", "final_norm")
SHARD_AXIS = {"ev_w_in": 1, "ev_w_q_up": 2, "ev_w_kv_up": 2, "ev_w_out": 1, "od_norm": 1, "od_w_in": 2,
              "od_conv_w": 2, "od_conv_b": 1, "od_w_rgate": 2, "od_b_rgate": 1, "od_w_igate": 2, "od_b_igate": 1,
              "od_lambda": 1, "od_w_out": 1, "xa_w_q": 1, "xa_w_kv": 2, "xa_w_o": 1, "ffn_w_gate_up": 2,
              "ffn_w_down": 1}
MATMUL_WEIGHTS = ("ev_w_in", "ev_w_q_up", "ev_w_kv_up", "ev_w_out", "od_w_in", "od_w_rgate", "od_w_igate",
                  "od_w_out", "xa_w_q", "xa_w_kv", "xa_w_o", "ffn_w_gate_up", "ffn_w_down")
SMALL_SHARDED = tuple(n for n in WEIGHTS if n in SHARD_AXIS and n not in MATMUL_WEIGHTS)
REPLICATED = tuple(n for n in WEIGHTS if n not in SHARD_AXIS)


def _pack(parts, quantum):
    flat = jnp.concatenate([p.reshape(-1) for p in parts])
    pad = (-flat.shape[0]) % quantum
    return jnp.pad(flat, (0, pad)).reshape(-1, LANES)


def _unpack(flat, shapes):
    out, off = [], 0
    for shape in shapes:
        size = math.prod(shape)
        out.append(flat[off:off + size].reshape(shape))
        off += size
    return out


def _run_copies(local, remote, send_sems, recv_sems, local_sems):
    locals_ = [pltpu.make_async_copy(src, dst, local_sems.at[n]) for n, (src, dst) in enumerate(local)]
    for cp in locals_:
        cp.start()
    sends = [pltpu.make_async_remote_copy(src_ref=src, dst_ref=dst, send_sem=send_sems.at[k, n],
                                          recv_sem=recv_sems.at[k, n], device_id=dev, device_id_type=MESH)
             for (k, n, src, dst, _, dev) in remote]
    for cp in sends:
        cp.start()
    for (k, n, src, _, arrival, dev) in remote:
        pltpu.make_async_remote_copy(src_ref=src, dst_ref=arrival, send_sem=send_sems.at[k, n],
                                     recv_sem=recv_sems.at[k, n], device_id=dev, device_id_type=MESH).wait_recv()
    for cp in sends:
        cp.wait_send()
    for cp in locals_:
        cp.wait()


def _chip_peers(x, y):
    return [(1 - x, y), (x, 1 - y), (1 - x, 1 - y)]


def _owner_block(ref, axis, q):
    size = ref.shape[axis] // N_CHIPS
    idx = [slice(None)] * len(ref.shape)
    idx[axis] = pl.ds(q * size, size)
    return ref.at[tuple(idx)]


def _comm_call(body, ins, out_shapes, n_items, n_peers, *, name):
    return pl.pallas_call(
        body, in_specs=[ANY] * len(ins), out_specs=[ANY] * len(out_shapes), out_shape=out_shapes,
        scratch_shapes=[pltpu.SemaphoreType.DMA((n_peers, n_items)), pltpu.SemaphoreType.DMA((n_peers, n_items)),
                        pltpu.SemaphoreType.DMA((n_items,))],
        name=name,
    )(*ins)


def _gather_chips(shards, axes, *, name):
    n = len(shards)
    full = [jax.ShapeDtypeStruct(tuple(d * (N_CHIPS if a == ax else 1) for a, d in enumerate(s.shape)), s.dtype)
            for s, ax in zip(shards, axes)]

    def body(*refs):
        srcs, dsts = refs[:n], refs[n:2 * n]
        x, y, c = lax.axis_index("x"), lax.axis_index("y"), lax.axis_index("c")
        me = 2 * x + y
        local = [(srcs[i], _owner_block(dsts[i], axes[i], me)) for i in range(n)]
        remote = [(k, i, srcs[i], _owner_block(dsts[i], axes[i], me), _owner_block(dsts[i], axes[i], 2 * px + py),
                   (px, py, c))
                  for k, (px, py) in enumerate(_chip_peers(x, y)) for i in range(n)]
        _run_copies(local, remote, *refs[2 * n:])

    return _comm_call(body, shards, full, n, 3, name=name)


HBM = pl.BlockSpec(memory_space=pltpu.HBM)
SEM = pl.BlockSpec(memory_space=pltpu.SEMAPHORE)
DATAFLOW = pltpu.SideEffectType.DATAFLOW_SIDE_EFFECTING


def _gather_plan(axes):
    return lambda srcs, lands, me, peer: [
        (srcs[i], _owner_block(lands[i], ax, me), _owner_block(lands[i], ax, peer)) for i, ax in enumerate(axes)]


def _exchange_plan(where):
    return lambda srcs, lands, me, peer: [
        (srcs[i].at[peer], lands[n].at[me, l], lands[n].at[peer, l]) for i, (n, l) in enumerate(where)]


def _split_start(srcs, lands, plan, *, name):
    ns, nl = len(srcs), len(lands)
    nsem = 3 * len(plan(list(srcs), list(lands), 0, 0))

    def body(*refs):
        src_refs, land_refs = refs[:ns], refs[ns:ns + nl]
        send_sems, recv_sems = refs[ns + nl:ns + nl + nsem], refs[ns + nl + nsem:ns + nl + 2 * nsem]
        x, y, c = lax.axis_index("x"), lax.axis_index("y"), lax.axis_index("c")
        n = 0
        for px, py in _chip_peers(x, y):
            for src, dst, _ in plan(src_refs, land_refs, 2 * x + y, 2 * px + py):
                pltpu.make_async_remote_copy(src_ref=src, dst_ref=dst, send_sem=send_sems[n], recv_sem=recv_sems[n],
                                             device_id=(px, py, c), device_id_type=MESH).start()
                n += 1
        refs[-1][...] = jnp.zeros_like(refs[-1])

    arrays = list(srcs) + list(lands)
    out = pl.pallas_call(
        body, name=name, in_specs=[HBM] * (ns + nl),
        out_specs=[SEM] * (2 * nsem) + [HBM] * (ns + nl) + [pl.BlockSpec(memory_space=pltpu.VMEM)],
        out_shape=[pltpu.SemaphoreType.DMA(())] * (2 * nsem) + [pltpu.HBM(a.shape, a.dtype) for a in arrays]
        + [jax.ShapeDtypeStruct((8, LANES), F32)],
        input_output_aliases={i: 2 * nsem + i for i in range(ns + nl)},
        compiler_params=pltpu.CompilerParams(has_side_effects=DATAFLOW),
    )(*[pltpu.with_memory_space_constraint(a, pltpu.HBM) for a in arrays])
    sems, rest = out[:2 * nsem], out[2 * nsem:]
    return sems[:nsem], sems[nsem:], rest[:ns], rest[ns:ns + nl], rest[-1]


def _split_wait(handle, after, plan, *, name):
    send_sems, recv_sems, srcs, lands, _ = handle
    ns, nl, nsem = len(srcs), len(lands), len(send_sems)

    def body(*refs):
        src_refs, land_refs = refs[:ns], refs[ns:ns + nl]
        send_refs, recv_refs = refs[ns + nl:ns + nl + nsem], refs[ns + nl + nsem:ns + nl + 2 * nsem]
        x, y, c = lax.axis_index("x"), lax.axis_index("y"), lax.axis_index("c")
        n = 0
        for px, py in _chip_peers(x, y):
            for src, _, arrival in plan(src_refs, land_refs, 2 * x + y, 2 * px + py):
                cp = pltpu.make_async_remote_copy(src_ref=src, dst_ref=arrival, send_sem=send_refs[n],
                                                  recv_sem=recv_refs[n], device_id=(px, py, c), device_id_type=MESH)
                cp.wait_send()
                cp.wait_recv()
                n += 1

    out = pl.pallas_call(
        body, name=name, in_specs=[HBM] * (ns + nl) + [SEM] * (2 * nsem) + [ANY], out_specs=[HBM] * (ns + nl),
        out_shape=[pltpu.HBM(a.shape, a.dtype) for a in list(srcs) + list(lands)],
        input_output_aliases={i: i for i in range(ns + nl)},
        compiler_params=pltpu.CompilerParams(has_side_effects=DATAFLOW),
    )(*srcs, *lands, *send_sems, *recv_sems, after)
    return out[ns:]


def _exchange_sibling(arrays, *, name):
    n = len(arrays)

    def body(*refs):
        x, y, c = lax.axis_index("x"), lax.axis_index("y"), lax.axis_index("c")
        remote = [(0, i, refs[i], refs[n + i], refs[n + i], (x, y, 1 - c)) for i in range(n)]
        _run_copies([], remote, *refs[2 * n:])

    return _comm_call(body, arrays, [jax.ShapeDtypeStruct(a.shape, a.dtype) for a in arrays], n, 1, name=name)


def _sum_slots(r, *, token=None, name):
    shape = r.shape[1:]
    cols = shape[-1]
    rows = math.prod(shape) // cols
    tr = max(d for d in range(8, rows + 1, 8) if rows % d == 0 and d * cols * 16 <= (4 << 20))

    def body(r_ref, *refs):
        total = ((r_ref[0] + r_ref[1]) + r_ref[2]) + r_ref[3]
        refs[-1][...] = total if token is None else total + refs[0][0:1, 0:1]

    in_specs = [pl.BlockSpec((N_CHIPS, tr, cols), lambda i: (0, i, 0))]
    in_specs += [] if token is None else [_const((8, LANES))]
    return pl.pallas_call(
        body, grid=(rows // tr,), in_specs=in_specs,
        out_specs=_rows(tr, cols), out_shape=jax.ShapeDtypeStruct((rows, cols), F32), compiler_params=_cp(1),
        name=name,
    )(r.reshape(N_CHIPS, rows, cols), *([] if token is None else [token])).reshape(shape)


FIRST_WEIGHTS = ("ev_w_in", "ev_w_q_up", "ev_w_kv_up", "ev_w_out")
LATER_WEIGHTS = tuple(n for n in MATMUL_WEIGHTS if n not in FIRST_WEIGHTS)
LAST_GRADS = ("ev_w_in", "ev_w_q_up", "ev_w_kv_up")
EARLIER_GRADS = tuple(n for n in MATMUL_WEIGHTS if n not in LAST_GRADS)


def _my_chip():
    return 2 * lax.axis_index("x") + lax.axis_index("y")


def _gather_first(w):
    small = _pack([w[n] for n in SMALL_SHARDED], 8 * LANES)
    stacked = [n for n in FIRST_WEIGHTS if SHARD_AXIS[n] == w[n].ndim - 1 and w[n].shape[-1] % LANES]
    shards = [w[n].astype(BF16)[None] if n in stacked else w[n].astype(BF16) for n in FIRST_WEIGHTS]
    got = _gather_chips(shards + [small], [0 if n in stacked else SHARD_AXIS[n] for n in FIRST_WEIGHTS] + [0],
                        name="gather_first")
    full = {n: w[n] for n in REPLICATED}
    for n, g in zip(FIRST_WEIGHTS, got[:-1]):
        full[n] = jnp.concatenate([g[q] for q in range(N_CHIPS)], axis=SHARD_AXIS[n]) if n in stacked else g
    per_chip = [_unpack(got[-1][q * small.shape[0]:(q + 1) * small.shape[0]].reshape(-1),
                        [w[n].shape for n in SMALL_SHARDED]) for q in range(N_CHIPS)]
    for i, n in enumerate(SMALL_SHARDED):
        full[n] = jnp.concatenate([per_chip[q][i] for q in range(N_CHIPS)], axis=SHARD_AXIS[n])
    return full


def _gather_later_start(w):
    shards = [w[n].astype(BF16) for n in LATER_WEIGHTS]
    axes = [SHARD_AXIS[n] for n in LATER_WEIGHTS]
    lands = []
    for s, ax in zip(shards, axes):
        shape = tuple(d * (N_CHIPS if a == ax else 1) for a, d in enumerate(s.shape))
        lands.append(lax.dynamic_update_slice_in_dim(lax.empty(shape, s.dtype), s, _my_chip() * s.shape[ax], ax))
    return _split_start(shards, lands, _gather_plan(axes), name="gather_later_start"), _gather_plan(axes)


def _owner_major(g, axis):
    shape = g.shape
    size = shape[axis] // N_CHIPS
    g = jnp.moveaxis(g.reshape(shape[:axis] + (N_CHIPS, size) + shape[axis + 1:]), axis, 0)
    return g.reshape(N_CHIPS, -1, shape[-1] if axis < len(shape) - 1 else size)


def _exchange_start(items, *, name):
    me = _my_chip()
    srcs, lands, where = [], [], []
    for n, layers in enumerate(items):
        land = lax.empty((N_CHIPS, len(layers)) + layers[0].shape[1:], layers[0].dtype)
        for l, a in enumerate(layers):
            own = lax.dynamic_index_in_dim(a, me, 0, keepdims=True)[:, None]
            land = lax.dynamic_update_slice(land, own, (me, l) + (0,) * (a.ndim - 1))
            srcs.append(a)
            where.append((n, l))
        lands.append(land)
    plan = _exchange_plan(where)
    return _split_start(srcs, lands, plan, name=name), plan


def _earlier_items(grads, full_shapes):
    small = [_pack([jnp.split(grads[n].reshape(full_shapes[n]), N_CHIPS, axis=SHARD_AXIS[n])[q]
                    for n in SMALL_SHARDED], 8 * LANES) for q in range(N_CHIPS)]
    return [grads[n] for n in EARLIER_GRADS] + [[jnp.stack(small)]]


def _last_items(big, grads, full_shapes, loss):
    repl = _pack([grads[n].reshape(full_shapes[n]) for n in REPLICATED] + [loss.reshape(1)], 8 * LANES)
    return [big[n] for n in LAST_GRADS] + [[jnp.stack([repl] * N_CHIPS)]]


def kernel(
        x, mem, positions, ev_norm, ev_w_in, ev_pool_w, ev_pool_scale, ev_q_norm, ev_w_q_up, ev_kv_norm,
        ev_w_kv_up, ev_w_out, od_norm, od_w_in, od_conv_w, od_conv_b, od_w_rgate, od_b_rgate, od_w_igate,
        od_b_igate, od_lambda, od_w_out, xa_norm_x, xa_norm_mem, xa_w_q, xa_w_kv, xa_w_o, ffn_norm,
        ffn_w_gate_up, ffn_w_down, final_norm, loss_target, m_ev_norm, m_ev_w_in, m_ev_pool_w, m_ev_pool_scale,
        m_ev_q_norm, m_ev_w_q_up, m_ev_kv_norm, m_ev_w_kv_up, m_ev_w_out, m_od_norm, m_od_w_in, m_od_conv_w,
        m_od_conv_b, m_od_w_rgate, m_od_b_rgate, m_od_w_igate, m_od_b_igate, m_od_lambda, m_od_w_out,
        m_xa_norm_x, m_xa_norm_mem, m_xa_w_q, m_xa_w_kv, m_xa_w_o, m_ffn_norm, m_ffn_w_gate_up, m_ffn_w_down,
        m_final_norm, v_ev_norm, v_ev_w_in, v_ev_pool_w, v_ev_pool_scale, v_ev_q_norm, v_ev_w_q_up,
        v_ev_kv_norm, v_ev_w_kv_up, v_ev_w_out, v_od_norm, v_od_w_in, v_od_conv_w, v_od_conv_b, v_od_w_rgate,
        v_od_b_rgate, v_od_w_igate, v_od_b_igate, v_od_lambda, v_od_w_out, v_xa_norm_x, v_xa_norm_mem, v_xa_w_q,
        v_xa_w_kv, v_xa_w_o, v_ffn_norm, v_ffn_w_gate_up, v_ffn_w_down, v_final_norm):
    given = dict(locals())
    w = {n: given[n] for n in WEIGHTS}
    full_shapes = {n: tuple(d * (N_CHIPS if a == SHARD_AXIS.get(n) else 1) for a, d in enumerate(w[n].shape))
                   for n in WEIGHTS}
    full = _gather_first(w)
    later, later_plan = _gather_later_start(w)
    full["ev_norm"] = full["ev_norm"] + later[4][0:1, 0:1]
    exchange = {}

    def later_weights(after):
        return dict(zip(LATER_WEIGHTS, _split_wait(later, after, later_plan, name="gather_later_wait")))

    def exchange_earlier(grads):
        exchange["handle"], exchange["plan"] = _exchange_start(_earlier_items(grads, full_shapes),
                                                               name="exchange_earlier_start")
        return exchange["handle"][4]

    loss, grad_x, big, grads = _local_step(x[0], mem[0], positions[0], loss_target[0], full, later_weights,
                                           exchange_earlier)
    earlier = EARLIER_GRADS + ("small",)
    got = dict(zip(earlier, _split_wait(exchange["handle"], grad_x, exchange["plan"], name="exchange_earlier_wait")))
    last, last_plan = _exchange_start(_last_items(big, grads, full_shapes, loss), name="exchange_last_start")
    sums = {n: _sum_slots(got[n], token=last[4] if i == 0 else None, name=f"sum_chips_{n}")
            for i, n in enumerate(earlier)}
    got = dict(zip(LAST_GRADS + ("replicated",),
                   _split_wait(last, sums[earlier[-1]], last_plan, name="exchange_last_wait")))
    sums.update({n: _sum_slots(got[n], name=f"sum_chips_{n}") for n in got})
    mine = [sums[n] for n in MATMUL_WEIGHTS + ("small", "replicated")]
    other = _exchange_sibling(mine, name="exchange_sibling")
    out = {}
    for i, n in enumerate(MATMUL_WEIGHTS):
        out[n] = _adamw(w[n], mine[i].reshape(w[n].shape), other[i].reshape(w[n].shape), given["m_" + n],
                        given["v_" + n], name=f"adamw_{n}")
    for i, group in ((len(MATMUL_WEIGHTS), SMALL_SHARDED), (len(MATMUL_WEIGHTS) + 1, REPLICATED)):
        spare = [jnp.zeros((1,), F32)] if group is REPLICATED else []
        packed = [_pack([given[pre + n] for n in group] + spare, 8 * LANES) for pre in ("", "m_", "v_")]
        res = _adamw(packed[0], mine[i].reshape(packed[0].shape), other[i].reshape(packed[0].shape), packed[1],
                     packed[2], name=f"adamw_group{i}")
        shapes = [w[n].shape for n in group] + [(1,)] * len(spare)
        for j, arrs in enumerate(zip(*[_unpack(r.reshape(-1), shapes) for r in res])):
            if j < len(group):
                out[group[j]] = list(arrs)
            else:
                loss = arrs[0][0]
    return (loss, grad_x[None], *[out[n][k] for k in range(4) for n in WEIGHTS])
```

```python
import functools
import math

import jax
import jax.numpy as jnp
from jax import lax
from jax.experimental import pallas as pl
from jax.experimental.pallas import tpu as pltpu

F32 = jnp.float32
BF16 = jnp.bfloat16

D_MODEL = 1024
POOL_DIM = 512
POOL_WINDOWS = (2, 4, 8, 16)
POOL_GROUP = 128
MLA_HEADS = 8
QK_NOPE = 64
QK_ROPE = 32
QK_DIM = QK_NOPE + QK_ROPE
V_HEAD = 64
HEAD_PAD = 128
Q_RANK = 256
KV_RANK = 128
ROPE_BASE = 10000.0
LRU_HEADS = 4
LRU_HEAD_DIM = 256
CONV_WIDTH = 4
LRU_C = 8.0
MEM_HEADS = 4
MEM_HEAD_DIM = 256
D_FF = 2816
RMS_EPS = 1e-6
NEG_INF = -1e30

ADAM_LR = 0.001
ADAM_B1 = 0.9
ADAM_B2 = 0.999
ADAM_EPS = 1e-08
ADAM_WD = 0.01
ADAM_STEP = 10

N_CHIPS = 4
LANES = 128
VMEM_LIMIT = 56 * 1024 * 1024
MESH = pl.DeviceIdType.MESH
ANY = pl.BlockSpec(memory_space=pl.ANY)
MIX_DIM = POOL_DIM + MLA_HEADS * HEAD_PAD

NN = (((1,), (0,)), ((), ()))
NT = (((1,), (1,)), ((), ()))
TN = (((0,), (0,)), ((), ()))


def _cp(n):
    return pltpu.CompilerParams(dimension_semantics=("arbitrary",) * n, vmem_limit_bytes=VMEM_LIMIT)


def _dot(a, b, dims=NN):
    return lax.dot_general(a, b, dims, preferred_element_type=F32)


def _row_tile(S):
    return 1024 if S % 1024 == 0 else min(S, 512)


def _rows(ts, w, cb=0):
    return pl.BlockSpec((ts, w), lambda i: (i, cb))


def _const(shape):
    return pl.BlockSpec(shape, lambda i: (0,) * len(shape))


MM_VMEM_BUDGET = 40 * 1024 * 1024


def _mm(a, bs, epi, outs, *, tn, nj, nt=False, also=None, extras=(), rows=(), sums=(), a_cb=0, k=None, tm=None,
        name):
    M = a.shape[0]
    k = k or a.shape[1]
    nb, ne, nr, no = len(bs), len(extras), len(rows), len(outs)
    lhs = [(a, k, a_cb, b) for b in bs[:1]] + ([(also[0], also[0].shape[1], 0, also[1])] if also else [])
    if tm is None:
        per_row = 2 * (sum(kk * x.dtype.itemsize for x, kk, _, _ in lhs)
                       + sum(e.dtype.itemsize for e, _ in extras) * tn
                       + sum(jnp.dtype(dt).itemsize for _, dt, _ in outs) * tn) + nb * tn * 4
        weights = (1 if nj == 1 else 2) * (sum(b.dtype.itemsize for b, _, _ in bs) * k
                                           + (also[1][0].dtype.itemsize * lhs[-1][1] if also else 0)) * tn
        tm = 1024 if M % 1024 == 0 and 1024 * per_row + weights <= MM_VMEM_BUDGET else min(M, 512)
    dims = NT if nt else NN
    assert not sums or nj == 1
    na = 2 if also else 0

    def body(*refs):
        av = refs[0][...].astype(BF16)
        accs = [_dot(av, r[...].astype(BF16), dims) for r in refs[1:1 + nb]]
        if also:
            accs[0] = accs[0] + _dot(refs[1 + nb][...].astype(BF16), refs[2 + nb][...].astype(BF16), dims)
        refs = refs[:1 + nb] + refs[1 + nb + na:]
        vals = epi(accs, [r[...] for r in refs[1 + nb:1 + nb + ne + nr]])
        outs_refs = refs[1 + nb + ne + nr:]
        for o, v in zip(outs_refs[:no], vals[:no]):
            o[...] = v.astype(o.dtype)
        if sums:
            @pl.when(pl.program_id(1) == 0)
            def _():
                for o in outs_refs[no:]:
                    o[...] = jnp.zeros_like(o)

            for o, v in zip(outs_refs[no:], vals[no:]):
                o[...] += v

    in_specs = [pl.BlockSpec((tm, k), lambda j, i: (i, a_cb))]
    weights = [(k, rb, cb) for (_, rb, cb) in bs]
    if also:
        in_specs_also = pl.BlockSpec((tm, lhs[-1][1]), lambda j, i: (i, 0))
        weights.append((lhs[-1][1], also[1][1], also[1][2]))
    for n, (kk, rb, cb) in enumerate(weights):
        if also and n == nb:
            in_specs.append(in_specs_also)
        mode = dict(pipeline_mode=pl.Buffered(1)) if nj == 1 else {}
        if nt:
            in_specs.append(pl.BlockSpec((tn, kk), lambda j, i, rb=rb, cb=cb: (rb + j, cb), **mode))
        else:
            in_specs.append(pl.BlockSpec((kk, tn), lambda j, i, rb=rb, cb=cb: (rb, cb + j), **mode))
    for (_, cb) in extras:
        in_specs.append(pl.BlockSpec((tm, tn), lambda j, i, cb=cb: (i, cb + j)))
    in_specs += [pl.BlockSpec((1, tn), lambda j, i: (0, 0))] * nr
    out_specs = [pl.BlockSpec((tm, tn), lambda j, i, cb=cb: (i, cb + j)) for (_, _, cb) in outs]
    out_specs += [pl.BlockSpec((1, w), lambda j, i: (0, 0)) for w in sums]
    res = pl.pallas_call(
        body, grid=(nj, M // tm), in_specs=in_specs, out_specs=out_specs,
        out_shape=[jax.ShapeDtypeStruct((M, n), dt) for (n, dt, _) in outs]
        + [jax.ShapeDtypeStruct((1, w), F32) for w in sums],
        compiler_params=_cp(2), name=name,
    )(a, *[b for (b, _, _) in bs], *([also[0], also[1][0]] if also else []), *[e for (e, _) in extras], *rows)
    return res


def _first(accs, extras):
    return [accs[0]]


def _add_res(accs, extras):
    return [accs[0] + extras[0].astype(F32)]


def _norm_bwd_epilogue(partials):
    def epi(accs, vals):
        dh = accs[0]
        for part in vals[:partials]:
            dh = dh + part.astype(F32)
        x, res, g = vals[partials:partials + 3]
        r = lax.rsqrt(jnp.mean(x * x, axis=-1, keepdims=True) + RMS_EPS)
        n = x * r
        dn = dh * g
        return [r * (dn - n * jnp.mean(dn * n, axis=-1, keepdims=True)) + res, jnp.sum(dh * n, axis=0, keepdims=True)]

    return epi


TN_VMEM_BUDGET = 36 * 1024 * 1024


def _contraction_rows(S, row_bytes, out_elems):
    ts = min(S, 2048)
    while ts > 512 and 2 * (ts * row_bytes + out_elems * 4) > TN_VMEM_BUDGET:
        ts //= 2
    return ts


def _mm_tn(a, b, *, ka=None, a_cb=0, nb=None, b_cb=0, tk=None, tn=None, ts=None, name):
    S = a.shape[0]
    ka = ka or a.shape[1]
    nb = nb or b.shape[1]
    tk = tk or ka
    tn = tn or nb
    ts = ts or _contraction_rows(S, tk * a.dtype.itemsize + tn * b.dtype.itemsize, tk * tn)
    a0, b0 = a_cb * (ka // tk), b_cb * (nb // tn)

    def body(a_ref, b_ref, o_ref):
        @pl.when(pl.program_id(2) == 0)
        def _():
            o_ref[...] = jnp.zeros_like(o_ref)

        o_ref[...] += _dot(a_ref[...].astype(BF16), b_ref[...].astype(BF16), TN)

    return pl.pallas_call(
        body, grid=(ka // tk, nb // tn, S // ts),
        in_specs=[pl.BlockSpec((ts, tk), lambda p, q, s: (s, a0 + p)),
                  pl.BlockSpec((ts, tn), lambda p, q, s: (s, b0 + q))],
        out_specs=pl.BlockSpec((tk, tn), lambda p, q, s: (p, q)),
        out_shape=jax.ShapeDtypeStruct((ka, nb), F32), compiler_params=_cp(3), name=name,
    )(a, b)


def _mm_tn_owners(a, bs, *, name):
    S, ka = a.shape
    nb = sum(b.shape[1] for b in bs)
    tn = nb // N_CHIPS
    ts = _contraction_rows(S, ka * a.dtype.itemsize + len(bs) * tn * bs[0].dtype.itemsize, ka * tn)
    per = N_CHIPS // len(bs)

    def body(a_ref, *refs):
        o_ref = refs[-1]
        q = pl.program_id(0)

        @pl.when(pl.program_id(1) == 0)
        def _():
            o_ref[...] = jnp.zeros_like(o_ref)

        av = a_ref[...].astype(BF16)
        for n, b_ref in enumerate(refs[:-1]):
            @pl.when(q // per == n)
            def _():
                o_ref[0] += _dot(av, b_ref[...].astype(BF16), TN)

    in_specs = [pl.BlockSpec((ts, ka), lambda q, s: (s, 0))]
    for n in range(len(bs)):
        in_specs.append(pl.BlockSpec((ts, tn), lambda q, s, n=n: (jnp.where(q // per == n, s, 0),
                                                                  jnp.clip(q - n * per, 0, per - 1))))
    return pl.pallas_call(
        body, grid=(N_CHIPS, S // ts), in_specs=in_specs,
        out_specs=pl.BlockSpec((1, ka, tn), lambda q, s: (q, 0, 0)),
        out_shape=jax.ShapeDtypeStruct((N_CHIPS, ka, tn), F32), compiler_params=_cp(2), name=name,
    )(a, *bs)


def _mm_tn_grouped(a, b, groups, w, *, name):
    S = a.shape[0]
    ts = _contraction_rows(S, w * (a.dtype.itemsize + b.dtype.itemsize), w * w)

    def body(a_ref, b_ref, o_ref):
        @pl.when(pl.program_id(1) == 0)
        def _():
            o_ref[...] = jnp.zeros_like(o_ref)

        o_ref[0] += _dot(a_ref[...].astype(BF16), b_ref[...].astype(BF16), TN)

    return pl.pallas_call(
        body, grid=(groups, S // ts),
        in_specs=[pl.BlockSpec((ts, w), lambda g, s: (s, g)), pl.BlockSpec((ts, w), lambda g, s: (s, g))],
        out_specs=pl.BlockSpec((1, w, w), lambda g, s: (g, 0, 0)),
        out_shape=jax.ShapeDtypeStruct((groups, w, w), F32), compiler_params=_cp(2), name=name,
    )(a, b)


def _rms(x, g, *, cb=0, w=None, ts=None, name):
    S = x.shape[0]
    w = w or x.shape[1]
    ts = ts or _row_tile(S)

    def body(x_ref, g_ref, o_ref):
        xv = x_ref[...].astype(F32)
        r = lax.rsqrt(jnp.mean(xv * xv, axis=-1, keepdims=True) + RMS_EPS)
        o_ref[...] = (xv * r * g_ref[...]).astype(o_ref.dtype)

    return pl.pallas_call(
        body, grid=(S // ts,), in_specs=[_rows(ts, w, cb), _const((1, w))], out_specs=_rows(ts, w),
        out_shape=jax.ShapeDtypeStruct((S, w), BF16), compiler_params=_cp(1), name=name,
    )(x, g.reshape(1, w))


def _rms_bwd(x, g, dy, *, cb=0, w=None, res=None, out_dtype=F32, ts=None, name):
    S = x.shape[0]
    w = w or x.shape[1]
    ts = ts or min(S, 512)
    has_res = res is not None

    def body(*refs):
        x_ref, g_ref, dy_ref = refs[:3]
        dx_ref, dg_ref = refs[-2:]
        xv = x_ref[...].astype(F32)
        r = lax.rsqrt(jnp.mean(xv * xv, axis=-1, keepdims=True) + RMS_EPS)
        n = xv * r
        dyv = dy_ref[...].astype(F32)
        dn = dyv * g_ref[...]
        dx = r * (dn - n * jnp.mean(dn * n, axis=-1, keepdims=True))
        if has_res:
            dx = dx + refs[3][...].astype(F32)
        dx_ref[...] = dx.astype(dx_ref.dtype)

        @pl.when(pl.program_id(0) == 0)
        def _():
            dg_ref[...] = jnp.zeros_like(dg_ref)

        dg_ref[...] += jnp.sum(dyv * n, axis=0, keepdims=True)

    ins = [x, g.reshape(1, w), dy] + ([res] if has_res else [])
    in_specs = [_rows(ts, w, cb), _const((1, w)), _rows(ts, w)] + ([_rows(ts, w)] if has_res else [])
    return pl.pallas_call(
        body, grid=(S // ts,), in_specs=in_specs, out_specs=[_rows(ts, w), _const((1, w))],
        out_shape=[jax.ShapeDtypeStruct((S, w), out_dtype), jax.ShapeDtypeStruct((1, w), F32)],
        compiler_params=_cp(1), name=name,
    )(*ins)


HALO = 16


def _pool_counts(i, ts, rows, first_row):
    t = i * ts + first_row + lax.broadcasted_iota(jnp.int32, (rows, 1), 0)
    return [jnp.minimum(t + 1, w).astype(F32) for w in POOL_WINDOWS]


def _even_front(x, g, w_in, pool_w, pool_scale, g_q, w_q, g_kv, w_kv, ctab, stab, *, name):
    S = x.shape[0]
    ts = min(S, 512)

    def body(x_ref, g_ref, win_ref, pw_ref, sc_ref, gq_ref, wq_ref, gkv_ref, wkv_ref, c_ref, s_ref,
             h_ref, z_ref, y_ref, p_ref, cqn_ref, ckvn_ref, q_ref, k_ref, v_ref, tail):
        i = pl.program_id(0)

        def normed(t, gain):
            r = lax.rsqrt(jnp.mean(t * t, axis=-1, keepdims=True) + RMS_EPS)
            return (t * r * gain).astype(BF16)

        h = normed(x_ref[...], g_ref[...])
        h_ref[...] = h
        z = _dot(h, win_ref[...])
        z_ref[...] = z
        u = z[:, :POOL_DIM]
        xe = jnp.concatenate([jnp.where(i > 0, tail[...], 0.0), u], axis=0)
        tail[...] = u[ts - HALO:]
        sums = []
        s = xe
        for sh in (1, 2, 4, 8):
            s = s + pltpu.roll(s, sh, 0)
            sums.append(s)
        cnts = _pool_counts(i, ts, ts, 0)
        for grp in range(4):
            lo, hi = grp * POOL_GROUP, (grp + 1) * POOL_GROUP
            pooled = (sums[grp][HALO:, lo:hi] / cnts[grp] - u[:, lo:hi]).astype(BF16)
            p_ref[:, lo:hi] = pooled
            y_ref[:, lo:hi] = (_dot(pooled, pw_ref[grp]) * sc_ref[:, lo:hi]).astype(y_ref.dtype)
        cqn = normed(z[:, POOL_DIM:POOL_DIM + Q_RANK], gq_ref[...])
        ckvn = normed(z[:, POOL_DIM + Q_RANK:POOL_DIM + Q_RANK + KV_RANK], gkv_ref[...])
        cqn_ref[...] = cqn
        ckvn_ref[...] = ckvn
        q = _dot(cqn, wq_ref[...])
        kv = _dot(ckvn, wkv_ref[...])
        c, sn = c_ref[...], s_ref[...]
        kr = z[:, D_MODEL - HEAD_PAD:]
        kr_rot = kr * c + _rope_partner(kr) * sn
        lane = lax.broadcasted_iota(jnp.int32, (ts, HEAD_PAD), 1)
        for hd in range(MLA_HEADS):
            lo, hi = hd * HEAD_PAD, (hd + 1) * HEAD_PAD
            qh = q[:, lo:hi]
            q_ref[:, lo:hi] = (qh * c + _rope_partner(qh) * sn).astype(q_ref.dtype)
            k_ref[:, lo:hi] = (kv[:, lo:hi] + kr_rot).astype(k_ref.dtype)
            v_ref[:, lo:hi] = jnp.where(lane == V_HEAD, 1.0, kv[:, D_MODEL + lo:D_MODEL + hi]).astype(v_ref.dtype)

    wide = jax.ShapeDtypeStruct((S, D_MODEL), BF16)
    return pl.pallas_call(
        body, grid=(S // ts,),
        in_specs=[_rows(ts, D_MODEL), _const((1, D_MODEL)), _const((D_MODEL, D_MODEL)),
                  _const((4, POOL_GROUP, POOL_GROUP)), _const((1, POOL_DIM)), _const((1, Q_RANK)),
                  _const((Q_RANK, D_MODEL)), _const((1, KV_RANK)), _const((KV_RANK, 2 * D_MODEL)),
                  _rows(ts, HEAD_PAD), _rows(ts, HEAD_PAD)],
        out_specs=[_rows(ts, D_MODEL), _rows(ts, D_MODEL), _rows(ts, POOL_DIM), _rows(ts, POOL_DIM),
                   _rows(ts, Q_RANK), _rows(ts, KV_RANK), _rows(ts, D_MODEL), _rows(ts, D_MODEL), _rows(ts, D_MODEL)],
        out_shape=[wide, jax.ShapeDtypeStruct((S, D_MODEL), F32), jax.ShapeDtypeStruct((S, MIX_DIM), BF16),
                   jax.ShapeDtypeStruct((S, POOL_DIM), BF16), jax.ShapeDtypeStruct((S, Q_RANK), BF16),
                   jax.ShapeDtypeStruct((S, KV_RANK), BF16), wide, wide, wide],
        scratch_shapes=[pltpu.VMEM((HALO, POOL_DIM), F32)], compiler_params=_cp(1), name=name,
    )(x, g.reshape(1, D_MODEL), w_in, pool_w, pool_scale, g_q.reshape(1, Q_RANK), w_q, g_kv.reshape(1, KV_RANK), w_kv,
      ctab, stab)


def _pool_bwd(dmix, pooled, pool_w, pool_scale, *, name):
    S = dmix.shape[0]
    ts = min(S, 512)
    nh = ts // HALO
    last = S // HALO - 1

    def body(dy_ref, dyh_ref, p_ref, w_ref, sc_ref, du_ref, dyp_ref, dsc_ref):
        i = pl.program_id(0)
        dyv = dy_ref[...].astype(F32)
        dyh = jnp.where(i < pl.num_programs(0) - 1, dyh_ref[...].astype(F32), 0.0)
        dye = jnp.concatenate([dyv, dyh], axis=0) * sc_ref[...]
        dypre = dye.astype(BF16)
        dyp_ref[...] = dypre[:ts]
        cnts = _pool_counts(i, ts, ts + HALO, 0)
        n = ts + HALO
        dsc = []
        for g in range(4):
            lo, hi = g * POOL_GROUP, (g + 1) * POOL_GROUP
            ypre = _dot(p_ref[:, lo:hi], w_ref[g])
            dsc.append(jnp.sum(dyv[:, lo:hi] * ypre, axis=0, keepdims=True))
            dpool = _dot(dypre[:, lo:hi], w_ref[g], NT)
            s = dpool / cnts[g]
            for sh in (1, 2, 4, 8)[:g + 1]:
                s = s + pltpu.roll(s, n - sh, 0)
            du_ref[:, lo:hi] = (s[:ts] - dpool[:ts]).astype(du_ref.dtype)

        @pl.when(i == 0)
        def _():
            dsc_ref[...] = jnp.zeros_like(dsc_ref)

        dsc_ref[...] += jnp.concatenate(dsc, axis=1)

    return pl.pallas_call(
        body, grid=(S // ts,),
        in_specs=[_rows(ts, POOL_DIM),
                  pl.BlockSpec((HALO, POOL_DIM), lambda i: (jnp.minimum((i + 1) * nh, last), 0)),
                  _rows(ts, POOL_DIM), _const((4, POOL_GROUP, POOL_GROUP)), _const((1, POOL_DIM))],
        out_specs=[_rows(ts, POOL_DIM), _rows(ts, POOL_DIM), _const((1, POOL_DIM))],
        out_shape=[jax.ShapeDtypeStruct((S, POOL_DIM), BF16)] * 2 + [jax.ShapeDtypeStruct((1, POOL_DIM), F32)],
        compiler_params=_cp(1), name=name,
    )(dmix, dmix, pooled, pool_w, pool_scale)


def _rope_partner(t):
    lane = lax.broadcasted_iota(jnp.int32, t.shape, 1)
    swapped = jnp.where(lane < QK_NOPE + QK_ROPE // 2, pltpu.roll(t, HEAD_PAD - QK_ROPE // 2, 1),
                        pltpu.roll(t, QK_ROPE // 2, 1))
    return jnp.where((lane >= QK_NOPE) & (lane < QK_DIM), swapped, 0.0)


def _rope_bwd(dq_rot, dk_cat, ctab, stab, *, name):
    S = dq_rot.shape[0]
    ts = min(S, 512)

    def body(dq_ref, dk_ref, c_ref, s_ref, dqo_ref, dkr_ref):
        c, s = c_ref[...], s_ref[...]
        for h in range(MLA_HEADS):
            g = dq_ref[:, h * HEAD_PAD:(h + 1) * HEAD_PAD]
            dqo_ref[:, h * HEAD_PAD:(h + 1) * HEAD_PAD] = (g * c + _rope_partner(g * s)).astype(dqo_ref.dtype)
        dk = dk_ref[...]
        g = dk[:, :HEAD_PAD]
        for h in range(1, MLA_HEADS):
            g = g + dk[:, h * HEAD_PAD:(h + 1) * HEAD_PAD]
        lane = lax.broadcasted_iota(jnp.int32, g.shape, 1)
        on_rope = (lane >= QK_NOPE) & (lane < QK_DIM)
        dkr_ref[...] = jnp.where(on_rope, g * c + _rope_partner(g * s), 0.0).astype(dkr_ref.dtype)

    wide = _rows(ts, MLA_HEADS * HEAD_PAD)
    return pl.pallas_call(
        body, grid=(S // ts,), in_specs=[wide, wide, _rows(ts, HEAD_PAD), _rows(ts, HEAD_PAD)],
        out_specs=[wide, _rows(ts, HEAD_PAD)],
        out_shape=[jax.ShapeDtypeStruct((S, MLA_HEADS * HEAD_PAD), BF16), jax.ShapeDtypeStruct((S, HEAD_PAD), BF16)],
        compiler_params=_cp(1), name=name,
    )(dq_rot, dk_cat, ctab, stab)


ATT_SCALE = QK_DIM ** -0.5
LOG2E = math.log2(math.e)


HEADS_PER_STEP = 2
ATT_COL0 = POOL_DIM // HEAD_PAD


FWD_TILE = 1024


def _stat_rows(col):
    return jnp.broadcast_to(col, (col.shape[0], LANES)).T[0:8]


def _retile_rows(rows, tq):
    heads, n8, t = rows.shape
    if t == tq:
        return rows
    flat = rows.reshape(heads, n8 // 8, 8, t)[:, :, 0].reshape(heads, -1, 1, tq)
    return jnp.broadcast_to(flat, (heads, flat.shape[1], 8, tq)).reshape(heads, -1, tq)


def _flash_fwd(q, k, v, mix, *, name):
    S = q.shape[0]
    tq = FWD_TILE if S % FWD_TILE == 0 else min(S, 512)
    nq = S // tq
    hs = HEADS_PER_STEP
    wide = hs * HEAD_PAD

    def body(q_ref, k_ref, v_ref, mix_ref, o_ref, lse_ref):
        qi = pl.program_id(1)
        qv = [q_ref[:, a * HEAD_PAD:(a + 1) * HEAD_PAD] for a in range(hs)]

        def update(m, acc, s, v):
            m_new = jnp.maximum(m, jnp.max(s, axis=-1, keepdims=True))
            p = jnp.exp2((s - m_new) * (ATT_SCALE * LOG2E))
            alpha = jnp.exp2((m - m_new) * (ATT_SCALE * LOG2E))
            return m_new, alpha * acc + _dot(p.astype(BF16), v)

        def step(j, carry, masked):
            off = pl.multiple_of(j * tq, tq)
            out = []
            for a in range(hs):
                head = slice(a * HEAD_PAD, (a + 1) * HEAD_PAD)
                s = _dot(qv[a], k_ref[pl.ds(off, tq), head], NT)
                if masked:
                    row = lax.broadcasted_iota(jnp.int32, (tq, tq), 0)
                    col = lax.broadcasted_iota(jnp.int32, (tq, tq), 1)
                    s = jnp.where(col <= row, s, NEG_INF)
                out.append(update(*carry[a], s, v_ref[pl.ds(off, tq), head]))
            return tuple(out)

        one = (jnp.full((tq, 1), NEG_INF, F32), jnp.zeros((tq, HEAD_PAD), F32))
        carry = step(qi, lax.fori_loop(0, qi, lambda j, c: step(j, c, False), (one,) * hs), True)
        for a in range(hs):
            m, acc = carry[a]
            l = acc[:, V_HEAD:V_HEAD + 1]
            o_ref[:, a * HEAD_PAD:(a + 1) * HEAD_PAD] = (acc / l).astype(o_ref.dtype)
            lse_ref[a] = _stat_rows(m * ATT_SCALE + jnp.log(l))

    blk = pl.BlockSpec((tq, wide), lambda h, i: (i, h))
    full = pl.BlockSpec((S, wide), lambda h, i: (0, h))
    return pl.pallas_call(
        body, grid=(MLA_HEADS // hs, nq), in_specs=[blk, full, full, ANY],
        out_specs=[pl.BlockSpec((tq, wide), lambda h, i: (i, ATT_COL0 // hs + h)),
                   pl.BlockSpec((hs, 8, tq), lambda h, i: (h, i, 0))],
        out_shape=[jax.ShapeDtypeStruct(mix.shape, mix.dtype), jax.ShapeDtypeStruct((MLA_HEADS, nq * 8, tq), F32)],
        input_output_aliases={3: 0}, compiler_params=_cp(2), name=name,
    )(q, k, v, mix)


BWD_TILE = 1024
BWD_HEADS_PER_STEP = 1


def _bwd_tile(S):
    return BWD_TILE if S % BWD_TILE == 0 else min(S, 512)


def _attn_delta(dmix, mix, *, name):
    S = mix.shape[0]
    ts = _bwd_tile(S)
    half = MLA_HEADS // 2
    halves = [_rows(ts, half * HEAD_PAD, 1), _rows(ts, half * HEAD_PAD, 2)]

    def body(do0_ref, do1_ref, o0_ref, o1_ref, d_ref):
        for n, (do_ref, o_ref) in enumerate(((do0_ref, o0_ref), (do1_ref, o1_ref))):
            prod = do_ref[...].astype(F32) * o_ref[...].astype(F32)
            for a in range(half):
                d_ref[n * half + a] = _stat_rows(
                    jnp.sum(prod[:, a * HEAD_PAD:(a + 1) * HEAD_PAD], axis=-1, keepdims=True))

    return pl.pallas_call(
        body, grid=(S // ts,), in_specs=halves + halves,
        out_specs=pl.BlockSpec((MLA_HEADS, 8, ts), lambda i: (0, i, 0)),
        out_shape=jax.ShapeDtypeStruct((MLA_HEADS, (S // ts) * 8, ts), F32), compiler_params=_cp(1), name=name,
    )(dmix, dmix, mix, mix)


def _flash_bwd(q, k, v, dmix, lse_rows, delta_rows, *, name):
    S = q.shape[0]
    tq = _bwd_tile(S)
    nq = S // tq
    hs = BWD_HEADS_PER_STEP
    wide = hs * HEAD_PAD

    def body(q_hbm, do_hbm, lse_ref, dl_ref, k_ref, v_ref, dq_hbm, dk_ref, dv_ref, q_all, do_all, dq_all):
        g, j = pl.program_id(0), pl.program_id(1)
        cols = pl.multiple_of(g * wide, wide)

        @pl.when(j == 0)
        def _():
            pltpu.sync_copy(q_hbm.at[:, pl.ds(cols, wide)], q_all)
            pltpu.sync_copy(do_hbm.at[:, pl.ds(POOL_DIM + cols, wide)], do_all)
            dq_all[...] = jnp.zeros_like(dq_all)

        heads = [slice(a * HEAD_PAD, (a + 1) * HEAD_PAD) for a in range(hs)]
        kv = [k_ref[:, a] for a in heads]
        vv = [v_ref[:, a] for a in heads]

        def block(a, keys, rows, lse2, dl, first_query):
            qv, dov = q_all[rows, heads[a]], do_all[rows, heads[a]]
            st = _dot(kv[a][:keys], qv, NT)
            if first_query is not None:
                krow = lax.broadcasted_iota(jnp.int32, st.shape, 0)
                qcol = lax.broadcasted_iota(jnp.int32, st.shape, 1) + first_query
                st = jnp.where(krow <= qcol, st, NEG_INF)
            pt = jnp.exp2(st * (ATT_SCALE * LOG2E) - lse2)
            dst = (pt * (_dot(vv[a][:keys], dov, NT) - dl)).astype(BF16)
            dq_all[rows, heads[a]] += _dot(dst, kv[a][:keys], TN)
            return _dot(dst, qv), _dot(pt.astype(BF16), dov)

        def stats(a, i):
            off8 = pl.multiple_of(i * 8, 8)
            return lse_ref[a, pl.ds(off8, 8), :][0:1] * LOG2E, dl_ref[a, pl.ds(off8, 8), :][0:1]

        def step(i, carry):
            rows = pl.ds(pl.multiple_of(i * tq, tq), tq)
            out = []
            for a in range(hs):
                dk, dv = block(a, tq, rows, *stats(a, i), None)
                out.append((carry[a][0] + dk, carry[a][1] + dv))
            return tuple(out)

        def diagonal():
            half = tq // 2
            out = []
            for a in range(hs):
                lse2, dl = stats(a, j)
                off = pl.multiple_of(j * tq, tq)
                dk0, dv0 = block(a, half, pl.ds(off, half), lse2[:, :half], dl[:, :half], 0)
                dk1, dv1 = block(a, tq, pl.ds(pl.multiple_of(off + half, half), half), lse2[:, half:], dl[:, half:], half)
                zero = jnp.zeros((tq - half, HEAD_PAD), F32)
                out.append((dk1 + jnp.concatenate([dk0, zero], axis=0), dv1 + jnp.concatenate([dv0, zero], axis=0)))
            return tuple(out)

        carry = lax.fori_loop(j + 1, nq, step, diagonal())
        for a in range(hs):
            dk_ref[:, heads[a]] = carry[a][0] * ATT_SCALE
            dv_ref[:, heads[a]] = carry[a][1]

        @pl.when(j == nq - 1)
        def _():
            dq_all[...] = dq_all[...] * ATT_SCALE
            pltpu.sync_copy(dq_all, dq_hbm.at[:, pl.ds(cols, wide)])

    blk = pl.BlockSpec((tq, wide), lambda g, j: (j, g))
    stat = pl.BlockSpec((hs, nq * 8, tq), lambda g, j: (g, 0, 0))
    full = jax.ShapeDtypeStruct((S, MLA_HEADS * HEAD_PAD), F32)
    return pl.pallas_call(
        body, grid=(MLA_HEADS // hs, nq), in_specs=[ANY, ANY, stat, stat, blk, blk], out_specs=[ANY, blk, blk],
        out_shape=[full, full, full],
        scratch_shapes=[pltpu.VMEM((S, wide), BF16), pltpu.VMEM((S, wide), BF16), pltpu.VMEM((S, wide), F32)],
        compiler_params=_cp(2), name=name,
    )(q, dmix, lse_rows, delta_rows, k, v)


MEM_SCALE = MEM_HEAD_DIM ** -0.5


def _xattn_probs(qh, kh):
    s = _dot(qh, kh, NT) * MEM_SCALE
    e = jnp.exp(s - jnp.max(s, axis=-1, keepdims=True))
    return e / jnp.sum(e, axis=-1, keepdims=True)


def _xa_block_fwd(x, kvm, w_q, w_o, g, *, name):
    S = x.shape[0]
    ts = min(S, 512)
    nm = kvm.shape[0]

    def body(x_ref, kv_ref, wq_ref, wo_ref, g_ref, xo_ref, hx_ref, q_ref, o_ref):
        xv = x_ref[...]
        r = lax.rsqrt(jnp.mean(xv * xv, axis=-1, keepdims=True) + RMS_EPS)
        hx = (xv * r * g_ref[...]).astype(BF16)
        hx_ref[...] = hx
        q = _dot(hx, wq_ref[...]).astype(BF16)
        q_ref[...] = q
        for h in range(MEM_HEADS):
            lo, hi = h * MEM_HEAD_DIM, (h + 1) * MEM_HEAD_DIM
            p = _xattn_probs(q[:, lo:hi], kv_ref[:, lo:hi])
            o_ref[:, lo:hi] = _dot(p.astype(BF16), kv_ref[:, D_MODEL + lo:D_MODEL + hi]).astype(o_ref.dtype)
        xo_ref[...] = xv + _dot(o_ref[...], wo_ref[...])

    square = _const((D_MODEL, D_MODEL))
    act = jax.ShapeDtypeStruct((S, D_MODEL), BF16)
    return pl.pallas_call(
        body, grid=(S // ts,),
        in_specs=[_rows(ts, D_MODEL), _const((nm, 2 * D_MODEL)), square, square, _const((1, D_MODEL))],
        out_specs=[_rows(ts, D_MODEL)] * 4, out_shape=[jax.ShapeDtypeStruct((S, D_MODEL), F32), act, act, act],
        compiler_params=_cp(1), name=name,
    )(x, kvm, w_q, w_o, g.reshape(1, D_MODEL))


def _xa_block_bwd(dxo, x, q, kvm, w_q, w_o, g, *, name):
    S = q.shape[0]
    ts = min(S, 512)
    nm = kvm.shape[0]

    def body(dxo_ref, x_ref, q_ref, kv_ref, wq_ref, wo_ref, g_ref, dx_ref, dq_ref, dkv_ref, dg_ref):
        @pl.when(pl.program_id(0) == 0)
        def _():
            dkv_ref[...] = jnp.zeros_like(dkv_ref)
            dg_ref[...] = jnp.zeros_like(dg_ref)

        dxo = dxo_ref[...]
        do = _dot(dxo.astype(BF16), wo_ref[...], NT).astype(BF16)
        for h in range(MEM_HEADS):
            lo, hi = h * MEM_HEAD_DIM, (h + 1) * MEM_HEAD_DIM
            qh, kh, vh = q_ref[:, lo:hi], kv_ref[:, lo:hi], kv_ref[:, D_MODEL + lo:D_MODEL + hi]
            doh = do[:, lo:hi]
            p = _xattn_probs(qh, kh)
            dp = _dot(doh, vh, NT)
            ds = (p * (dp - jnp.sum(dp * p, axis=-1, keepdims=True)) * MEM_SCALE).astype(BF16)
            dq_ref[:, lo:hi] = _dot(ds, kh).astype(dq_ref.dtype)
            dkv_ref[:, lo:hi] += _dot(ds, qh, TN)
            dkv_ref[:, D_MODEL + lo:D_MODEL + hi] += _dot(p.astype(BF16), doh, TN)
        dx, dg = _norm_bwd_epilogue(0)([_dot(dq_ref[...], wq_ref[...], NT)], [x_ref[...], dxo, g_ref[...]])
        dx_ref[...] = dx
        dg_ref[...] += dg

    square = _const((D_MODEL, D_MODEL))
    return pl.pallas_call(
        body, grid=(S // ts,),
        in_specs=[_rows(ts, D_MODEL), _rows(ts, D_MODEL), _rows(ts, D_MODEL), _const((nm, 2 * D_MODEL)), square,
                  square, _const((1, D_MODEL))],
        out_specs=[_rows(ts, D_MODEL), _rows(ts, D_MODEL), _const((nm, 2 * D_MODEL)), _const((1, D_MODEL))],
        out_shape=[jax.ShapeDtypeStruct((S, D_MODEL), F32), jax.ShapeDtypeStruct((S, D_MODEL), BF16),
                   jax.ShapeDtypeStruct((nm, 2 * D_MODEL), F32), jax.ShapeDtypeStruct((1, D_MODEL), F32)],
        compiler_params=_cp(1), name=name,
    )(dxo, x, q, kvm, w_q, w_o, g.reshape(1, D_MODEL))


CONV_HALO = 8


def _sigmoid(x):
    return 0.5 * jnp.tanh(0.5 * x) + 0.5


def _softplus(x):
    return jnp.maximum(x, 0.0) + jnp.log(1.0 + jnp.exp(-jnp.abs(x)))


def _neg_expm1(x):
    series = -x * (1.0 + x * (1.0 / 2) * (1.0 + x * (1.0 / 3) * (1.0 + x * (1.0 / 4) * (1.0 + x * (1.0 / 5)))))
    return jnp.where(x > -0.05, series, 1.0 - jnp.exp(x))


GELU_C = math.sqrt(2.0 / math.pi)


def _gelu(x):
    return 0.5 * x * (1.0 + jnp.tanh(GELU_C * (x + 0.044715 * x * x * x)))


def _gelu_grad(x):
    t = jnp.tanh(GELU_C * (x + 0.044715 * x * x * x))
    return 0.5 * (1.0 + t) + 0.5 * x * (1.0 - t * t) * GELU_C * (1.0 + 3 * 0.044715 * x * x)


def _lru_gates(xc, wr_ref, br, wi_ref, bi, sp, reset):
    xcb = xc.astype(BF16)
    pr, pi = [], []
    for h in range(LRU_HEADS):
        lo, hi = h * LRU_HEAD_DIM, (h + 1) * LRU_HEAD_DIM
        pr.append(_dot(xcb[:, lo:hi], wr_ref[h]))
        pi.append(_dot(xcb[:, lo:hi], wi_ref[h]))
    r = _sigmoid(jnp.concatenate(pr, axis=1) + br)
    ig = _sigmoid(jnp.concatenate(pi, axis=1) + bi)
    log_a = -LRU_C * r * sp
    a = jnp.where(reset, 0.0, jnp.exp(log_a))
    mult = jnp.where(reset, 1.0, jnp.sqrt(jnp.maximum(_neg_expm1(2.0 * log_a), 0.0)))
    return r, ig, a, mult


SUBLANES = 8


def _compose_groups(a, b, reverse):
    n = a.shape[0]
    row = lax.broadcasted_iota(jnp.int32, a.shape, 0) % SUBLANES
    for s in (1, 2, 4):
        inside = (row < SUBLANES - s) if reverse else (row >= s)
        shift = n - s if reverse else s
        a_s = jnp.where(inside, pltpu.roll(a, shift, 0), 1.0)
        b_s = jnp.where(inside, pltpu.roll(b, shift, 0), 0.0)
        b = a * b_s + b
        a = a * a_s
    return a, b


def _chain_groups(a_buf, h_ref, state, reverse):
    groups = a_buf.shape[0] // SUBLANES

    def group(g, h_in):
        off = pl.multiple_of((groups - 1 - g if reverse else g) * SUBLANES, SUBLANES)
        h = a_buf[pl.ds(off, SUBLANES), :] * h_in + h_ref[pl.ds(off, SUBLANES), :]
        h_ref[pl.ds(off, SUBLANES), :] = h
        return jnp.broadcast_to(h[0:1] if reverse else h[SUBLANES - 1:SUBLANES], h.shape)

    return lax.fori_loop(0, groups, group, state, unroll=4)[0:1]


def _lru_fwd(z, reset, conv_w, conv_b, w_r, b_r, w_i, b_i, lam, *, name):
    S = z.shape[0]
    ts = min(S, 512)
    nh = ts // CONV_HALO
    W = D_MODEL

    def body(gate_ref, xb_ref, halo_ref, rs_ref, cw_ref, cb_ref, wr_ref, br_ref, wi_ref, bi_ref, lam_ref,
             xc_ref, h_ref, y_ref, a_buf, carry):
        i = pl.program_id(0)

        @pl.when(i == 0)
        def _():
            carry[...] = jnp.zeros_like(carry)

        halo = jnp.where(i > 0, halo_ref[...], 0.0)
        xe = jnp.concatenate([halo, xb_ref[...]], axis=0)
        xc = cb_ref[...] + cw_ref[3:4, :] * xe[CONV_HALO:]
        for kk in range(CONV_WIDTH - 1):
            xc = xc + cw_ref[kk:kk + 1, :] * pltpu.roll(xe, CONV_WIDTH - 1 - kk, 0)[CONV_HALO:]
        xc_ref[...] = xc
        reset = rs_ref[...] > 0.5
        _, ig, a, mult = _lru_gates(xc, wr_ref, br_ref[...], wi_ref, bi_ref[...], _softplus(-lam_ref[...]), reset)
        a_buf[...], h_ref[...] = _compose_groups(a, mult * (ig * xc), False)
        carry[...] = _chain_groups(a_buf, h_ref, jnp.broadcast_to(carry[...], (SUBLANES, W)), False)
        y_ref[...] = (_gelu(gate_ref[...]) * h_ref[...]).astype(y_ref.dtype)

    vec = _const((1, W))
    gw = _const((LRU_HEADS, LRU_HEAD_DIM, LRU_HEAD_DIM))
    return pl.pallas_call(
        body, grid=(S // ts,),
        in_specs=[_rows(ts, W, 0), _rows(ts, W, 1),
                  pl.BlockSpec((CONV_HALO, W), lambda i: (jnp.maximum(i * nh - 1, 0), 1)),
                  _rows(ts, 1), _const((CONV_WIDTH, W)), vec, gw, vec, gw, vec, vec],
        out_specs=[_rows(ts, W)] * 3,
        out_shape=[jax.ShapeDtypeStruct((S, W), F32), jax.ShapeDtypeStruct((S, W), F32),
                   jax.ShapeDtypeStruct((S, W), BF16)],
        scratch_shapes=[pltpu.VMEM((ts, W), F32), pltpu.VMEM((1, W), F32)],
        compiler_params=_cp(1), name=name,
    )(z, z, z, reset, conv_w, conv_b, w_r, b_r, w_i, b_i, lam)


def _lru_bwd(dy, z, xc, hseq, reset, w_r, b_r, w_i, b_i, lam, *, name):
    S = z.shape[0]
    ts = min(S, 512)
    nt = S // ts
    nh = ts // CONV_HALO
    W = D_MODEL

    def body(dy_ref, gate_ref, xc_ref, h_ref, hh_ref, rs_ref, wr_ref, br_ref, wi_ref, bi_ref, lam_ref,
             dg_ref, dxc_ref, dpr_ref, dpi_ref, acc_ref, a_buf, dh_buf, carry):
        i = pl.program_id(0)
        tile = nt - 1 - i

        @pl.when(i == 0)
        def _():
            carry[...] = jnp.zeros_like(carry)
            acc_ref[...] = jnp.zeros_like(acc_ref)

        xc = xc_ref[...]
        lam_v = lam_ref[...]
        sp = _softplus(-lam_v)
        reset = rs_ref[...] > 0.5
        r, ig, a, mult = _lru_gates(xc, wr_ref, br_ref[...], wi_ref, bi_ref[...], sp, reset)
        gate = gate_ref[...]
        dyv = dy_ref[...].astype(F32)
        h = h_ref[...]
        dg_ref[...] = (dyv * h * _gelu_grad(gate)).astype(dg_ref.dtype)
        last_row = lax.broadcasted_iota(jnp.int32, a.shape, 0) == ts - 1
        a_buf[...], dh_buf[...] = _compose_groups(jnp.where(last_row, 1.0, pltpu.roll(a, ts - 1, 0)),
                                                  dyv * _gelu(gate), True)
        _chain_groups(a_buf, dh_buf, jnp.broadcast_to(carry[...], (SUBLANES, W)), True)
        dh = dh_buf[...]
        carry[...] = a[0:1] * dh[0:1]
        hh = jnp.where(tile > 0, hh_ref[...], 0.0)
        h_prev = pltpu.roll(jnp.concatenate([hh, h], axis=0), 1, 0)[CONV_HALO:]
        da = dh * h_prev
        bx = ig * xc
        dmult = dh * bx
        dbx = dh * mult
        di = dbx * xc
        dlog_a = jnp.where(reset, 0.0, da * a - dmult * a * a / jnp.maximum(mult, 1e-30))
        dr = dlog_a * (-LRU_C) * sp
        dpre_r = dr * r * (1.0 - r)
        dpre_i = di * ig * (1.0 - ig)
        dprb, dpib = dpre_r.astype(BF16), dpre_i.astype(BF16)
        dpr_ref[...] = dprb
        dpi_ref[...] = dpib
        back = []
        for hd in range(LRU_HEADS):
            lo, hi = hd * LRU_HEAD_DIM, (hd + 1) * LRU_HEAD_DIM
            back.append(_dot(dprb[:, lo:hi], wr_ref[hd], NT) + _dot(dpib[:, lo:hi], wi_ref[hd], NT))
        dxc_ref[...] = dbx * ig + jnp.concatenate(back, axis=1)
        dlam = jnp.sum(dlog_a * (-LRU_C) * r, axis=0, keepdims=True) * (-_sigmoid(-lam_v))
        acc_ref[0:1, :] += jnp.sum(dpre_r, axis=0, keepdims=True)
        acc_ref[1:2, :] += jnp.sum(dpre_i, axis=0, keepdims=True)
        acc_ref[2:3, :] += dlam

    rev = lambda cb: pl.BlockSpec((ts, W), lambda i: (nt - 1 - i, cb))
    vec = _const((1, W))
    gw = _const((LRU_HEADS, LRU_HEAD_DIM, LRU_HEAD_DIM))
    return pl.pallas_call(
        body, grid=(nt,),
        in_specs=[rev(0), rev(0), rev(0), rev(0),
                  pl.BlockSpec((CONV_HALO, W), lambda i: (jnp.maximum((nt - 1 - i) * nh - 1, 0), 0)),
                  pl.BlockSpec((ts, 1), lambda i: (nt - 1 - i, 0)), gw, vec, gw, vec, vec],
        out_specs=[rev(0), rev(0), rev(0), rev(0), _const((8, W))],
        out_shape=[jax.ShapeDtypeStruct((S, W), BF16), jax.ShapeDtypeStruct((S, W), F32),
                   jax.ShapeDtypeStruct((S, W), BF16), jax.ShapeDtypeStruct((S, W), BF16),
                   jax.ShapeDtypeStruct((8, W), F32)],
        scratch_shapes=[pltpu.VMEM((ts, W), F32), pltpu.VMEM((ts, W), F32), pltpu.VMEM((1, W), F32)],
        compiler_params=_cp(1), name=name,
    )(dy, z, xc, hseq, hseq, reset, w_r, b_r, w_i, b_i, lam)


def _conv_bwd(dxc, z, conv_w, *, name):
    S = dxc.shape[0]
    ts = min(S, 512)
    nh = ts // CONV_HALO
    last = S // CONV_HALO - 1
    W = D_MODEL
    n = ts + CONV_HALO

    def body(d_ref, dn_ref, xb_ref, xp_ref, cw_ref, dxb_ref, acc_ref):
        i = pl.program_id(0)

        @pl.when(i == 0)
        def _():
            acc_ref[...] = jnp.zeros_like(acc_ref)

        d = d_ref[...]
        de = jnp.concatenate([d, jnp.where(i < pl.num_programs(0) - 1, dn_ref[...], 0.0)], axis=0)
        xe = jnp.concatenate([jnp.where(i > 0, xp_ref[...], 0.0), xb_ref[...]], axis=0)
        dxb = cw_ref[3:4, :] * d
        acc_ref[3:4, :] += jnp.sum(d * xe[CONV_HALO:], axis=0, keepdims=True)
        for kk in range(CONV_WIDTH - 1):
            sh = CONV_WIDTH - 1 - kk
            dxb = dxb + cw_ref[kk:kk + 1, :] * pltpu.roll(de, n - sh, 0)[:ts]
            acc_ref[kk:kk + 1, :] += jnp.sum(d * pltpu.roll(xe, sh, 0)[CONV_HALO:], axis=0, keepdims=True)
        dxb_ref[...] = dxb.astype(dxb_ref.dtype)
        acc_ref[4:5, :] += jnp.sum(d, axis=0, keepdims=True)

    return pl.pallas_call(
        body, grid=(S // ts,),
        in_specs=[_rows(ts, W), pl.BlockSpec((CONV_HALO, W), lambda i: (jnp.minimum((i + 1) * nh, last), 0)),
                  _rows(ts, W, 1), pl.BlockSpec((CONV_HALO, W), lambda i: (jnp.maximum(i * nh - 1, 0), 1)),
                  _const((CONV_WIDTH, W))],
        out_specs=[_rows(ts, W), _const((8, W))],
        out_shape=[jax.ShapeDtypeStruct((S, W), BF16), jax.ShapeDtypeStruct((8, W), F32)],
        compiler_params=_cp(1), name=name,
    )(dxc, dxc, z, z, conv_w)


def _loss_head(x, g, target, *, name):
    S, D = x.shape
    ts = _row_tile(S)

    def body(x_ref, g_ref, t_ref, dx_ref, dg_ref, l_ref):
        @pl.when(pl.program_id(0) == 0)
        def _():
            dg_ref[...] = jnp.zeros_like(dg_ref)
            l_ref[...] = jnp.zeros_like(l_ref)

        xv = x_ref[...]
        r = lax.rsqrt(jnp.mean(xv * xv, axis=-1, keepdims=True) + RMS_EPS)
        n = xv * r
        err = n * g_ref[...] - t_ref[...]
        l_ref[...] += 0.5 * jnp.sum(jnp.sum(err * err, axis=-1, keepdims=True) * (1.0 / D), axis=0, keepdims=True)
        dy = err * (1.0 / D)
        dn = dy * g_ref[...]
        dx_ref[...] = r * (dn - n * jnp.mean(dn * n, axis=-1, keepdims=True))
        dg_ref[...] += jnp.sum(dy * n, axis=0, keepdims=True)

    return pl.pallas_call(
        body, grid=(S // ts,), in_specs=[_rows(ts, D), _const((1, D)), _rows(ts, D)],
        out_specs=[_rows(ts, D), _const((1, D)), _const((8, LANES))],
        out_shape=[jax.ShapeDtypeStruct((S, D), F32), jax.ShapeDtypeStruct((1, D), F32),
                   jax.ShapeDtypeStruct((8, LANES), F32)],
        compiler_params=_cp(1), name=name,
    )(x, g.reshape(1, D), target)


def _adamw(w, ga, gb, m, v, *, name):
    shape = w.shape
    cols = shape[-1]
    rows = w.size // cols
    br = rows
    if rows * cols * 4 > (1 << 20):
        br = max(d for d in range(8, rows + 1, 8) if rows % d == 0 and d * cols * 4 <= (1 << 20))

    def body(w_ref, ga_ref, gb_ref, m_ref, v_ref, g_ref, d_ref, mo_ref, vo_ref):
        gv = ga_ref[...] + gb_ref[...]
        g_ref[...] = gv
        mn = ADAM_B1 * m_ref[...] + (1.0 - ADAM_B1) * gv
        vn = ADAM_B2 * v_ref[...] + (1.0 - ADAM_B2) * (gv * gv)
        m_hat = mn / (1.0 - ADAM_B1 ** ADAM_STEP)
        v_hat = vn / (1.0 - ADAM_B2 ** ADAM_STEP)
        d_ref[...] = -ADAM_LR * (m_hat / (jnp.sqrt(v_hat) + ADAM_EPS) + ADAM_WD * w_ref[...])
        mo_ref[...] = mn
        vo_ref[...] = vn

    spec = _rows(br, cols)
    outs = pl.pallas_call(
        body, grid=(rows // br,), in_specs=[spec] * 5, out_specs=[spec] * 4,
        out_shape=[jax.ShapeDtypeStruct((rows, cols), F32)] * 4, compiler_params=_cp(1), name=name,
    )(*[t.reshape(rows, cols) for t in (w, ga, gb, m, v)])
    return [o.reshape(shape) for o in outs]


def _pad_heads(w, width):
    k = w.shape[0]
    return jnp.pad(w.reshape(k, MLA_HEADS, width), ((0, 0), (0, 0), (0, HEAD_PAD - width))).reshape(k, -1)


def _unpad_heads(w, width):
    k = w.shape[0]
    return w.reshape(k, MLA_HEADS, HEAD_PAD)[:, :, :width].reshape(k, MLA_HEADS * width)


def _rope_tables(positions):
    inv_freq = ROPE_BASE ** (-jnp.arange(0, QK_ROPE, 2, dtype=F32) / QK_ROPE)
    ang = positions.astype(F32)[:, None] * inv_freq
    cos, sin = jnp.cos(ang), jnp.sin(ang)
    S = positions.shape[0]
    ones, zeros = jnp.ones((S, QK_NOPE), F32), jnp.zeros((S, QK_NOPE), F32)
    ctab = jnp.concatenate([ones, cos, cos, ones[:, :HEAD_PAD - QK_DIM]], axis=1)
    stab = jnp.concatenate([zeros, -sin, sin, zeros[:, :HEAD_PAD - QK_DIM]], axis=1)
    return ctab, stab


def _memory_block(x, mem, W, layer, tag):
    mn = _rms(mem, W["xa_norm_mem"][layer], name=f"{tag}_xa_norm_mem")
    kvm = _mm(mn, [(W["xa_w_kv"][layer], 0, 0)], _first, [(2 * D_MODEL, BF16, 0)], tn=2 * D_MODEL, nj=1,
              name=f"{tag}_xa_kv")[0]
    xo, hx, qx, o = _xa_block_fwd(x, kvm, W["xa_w_q"][layer], W["xa_w_o"][layer], W["xa_norm_x"][layer],
                                  name=f"{tag}_xa_fwd")
    return xo, (x, hx, qx, mn, kvm, o)


def _memory_block_bwd(dxo, mem, W, layer, saved, tag, grads):
    x, hx, qx, mn, kvm, o = saved
    wq, wkv, wo = W["xa_w_q"][layer], W["xa_w_kv"][layer], W["xa_w_o"][layer]
    grads["xa_w_o"][layer] = _owner_major(_mm_tn(o, dxo, name=f"{tag}_xa_dwo"), 0)
    dx, dqx, dkvm, dg = _xa_block_bwd(dxo, x, qx, kvm, wq, wo, W["xa_norm_x"][layer], name=f"{tag}_xa_bwd")
    grads["xa_w_q"][layer] = _owner_major(_mm_tn(hx, dqx, name=f"{tag}_xa_dwq"), 0)
    grads["xa_norm_x"][layer] = dg[0]
    dmn = _mm(dkvm, [(wkv, 0, 0)], _first, [(D_MODEL, F32, 0)], nt=True, tn=D_MODEL, nj=1, name=f"{tag}_xa_dmn")[0]
    grads["xa_w_kv"][layer] = _mm_tn_owners(mn, [dkvm], name=f"{tag}_xa_dwkv")
    _, dgm = _rms_bwd(mem, W["xa_norm_mem"][layer], dmn, name=f"{tag}_xa_norm_mem_bwd")
    grads["xa_norm_mem"][layer] = dgm[0]
    return dx


FF_TN = D_FF // 2

def _silu_mul(accs, extras):
    g, u = accs
    return [g * _sigmoid(g) * u, g, u]


def _silu_mul_bwd(accs, extras):
    da = accs[0]
    g, u = extras[0].astype(F32), extras[1].astype(F32)
    sg = _sigmoid(g)
    return [da * u * sg * (1.0 + g * (1.0 - sg)), da * g * sg]


def _ffn_block(x, W, layer, tag):
    hf = _rms(x, W["ffn_norm"][layer], name=f"{tag}_ffn_norm")
    wgu, wd = W["ffn_w_gate_up"][layer], W["ffn_w_down"][layer]
    act, g, u = _mm(hf, [(wgu, 0, 0), (wgu, 0, 2)], _silu_mul, [(D_FF, BF16, 0)] * 3, tn=FF_TN, nj=2,
                    name=f"{tag}_ffn_up")
    xo = _mm(act, [(wd, 0, 0)], _add_res, [(D_MODEL, F32, 0)], extras=[(x, 0)], tn=D_MODEL, nj=1,
             name=f"{tag}_ffn_down")[0]
    return xo, (x, hf, act, g, u)


def _ffn_block_bwd(dxo, W, layer, saved, tag, grads):
    x, hf, act, g, u = saved
    wgu, wd = W["ffn_w_gate_up"][layer], W["ffn_w_down"][layer]
    dg, du = _mm(dxo, [(wd, 0, 0)], _silu_mul_bwd, [(D_FF, BF16, 0)] * 2, nt=True, extras=[(g, 0), (u, 0)], tn=FF_TN,
                 nj=2, name=f"{tag}_ffn_dact")
    grads["ffn_w_down"][layer] = _owner_major(_mm_tn(act, dxo, tk=FF_TN, name=f"{tag}_ffn_dwd"), 0)
    dx, dgn = _mm(dg, [(wgu, 0, 0)], _norm_bwd_epilogue(0), [(D_MODEL, F32, 0)], nt=True, also=(du, (wgu, 0, 1)),
                  extras=[(x, 0), (dxo, 0)], rows=[W["ffn_norm"][layer].reshape(1, D_MODEL)],
                  sums=[D_MODEL], tn=D_MODEL, nj=1, name=f"{tag}_ffn_dhf")
    grads["ffn_w_gate_up"][layer] = _mm_tn_owners(hf, [dg, du], name=f"{tag}_ffn_dwgu")
    grads["ffn_norm"][layer] = dgn[0]
    return dx


def _even_block(x, tabs, W, tag):
    ctab, stab = tabs
    w_in = W["ev_w_in"][0]
    zero = jnp.zeros((D_MODEL, QK_NOPE), BF16)
    w_in_pad = jnp.concatenate([w_in[:, :896], zero, w_in[:, 896:], zero[:, :HEAD_PAD - QK_DIM]], axis=1)
    w_q_pad = _pad_heads(W["ev_w_q_up"][0], QK_DIM)
    wkv = W["ev_w_kv_up"][0].reshape(KV_RANK, MLA_HEADS, QK_NOPE + V_HEAD)
    w_kv_pad = jnp.concatenate([_pad_heads(wkv[:, :, :QK_NOPE].reshape(KV_RANK, -1), QK_NOPE),
                                _pad_heads(wkv[:, :, QK_NOPE:].reshape(KV_RANK, -1), V_HEAD)], axis=1)
    w_out = W["ev_w_out"][0]
    w_att = jnp.pad(w_out[POOL_DIM:].reshape(MLA_HEADS, V_HEAD, D_MODEL), ((0, 0), (0, HEAD_PAD - V_HEAD), (0, 0)))
    w_out_pad = jnp.concatenate([w_out[:POOL_DIM], w_att.reshape(MLA_HEADS * HEAD_PAD, D_MODEL)], axis=0)
    pool_w = W["ev_pool_w"][0].astype(BF16)
    pool_scale = W["ev_pool_scale"]

    h, z, mix, pooled, cqn, ckvn, q_rot, k_cat, v_pad = _even_front(
        x, W["ev_norm"][0], w_in_pad, pool_w, pool_scale, W["ev_q_norm"][0], w_q_pad, W["ev_kv_norm"][0], w_kv_pad,
        ctab, stab, name=f"{tag}_front")
    mix, lse = _flash_fwd(q_rot, k_cat, v_pad, mix, name=f"{tag}_attn")
    xo = _mm(mix, [(w_out_pad, 0, 0)], _add_res, [(D_MODEL, F32, 0)], extras=[(x, 0)], tn=D_MODEL, nj=1,
             name=f"{tag}_out")[0]
    saved = (x, h, z, pooled, cqn, ckvn, q_rot, k_cat, v_pad, lse, mix,
             (w_in_pad, w_q_pad, w_kv_pad, w_out_pad, pool_w, pool_scale))
    return xo, saved


def _even_out_grad(dxo, saved, tag):
    mix = saved[10]
    dw_out_pad = _mm_tn(mix, dxo, tk=MIX_DIM // 3, name=f"{tag}_dw_out")
    datt = dw_out_pad[POOL_DIM:].reshape(MLA_HEADS, HEAD_PAD, D_MODEL)[:, :V_HEAD].reshape(-1, D_MODEL)
    return [_owner_major(jnp.concatenate([dw_out_pad[:POOL_DIM], datt], axis=0), 0)]


def _even_block_bwd(dxo, tabs, W, saved, tag, grads, token=None):
    ctab, stab = tabs
    x, h, z, pooled, cqn, ckvn, q_rot, k_cat, v_pad, lse, mix, wts = saved
    w_in_pad, w_q_pad, w_kv_pad, w_out_pad, pool_w, pool_scale = wts
    if token is not None:
        w_out_pad = w_out_pad + token[0:1, 0:1].astype(BF16)
    dmix = _mm(dxo, [(w_out_pad, 0, 0)], _first, [(MIX_DIM, BF16, 0)], nt=True, tn=MIX_DIM, nj=1,
               name=f"{tag}_dmix")[0]
    delta = _attn_delta(dmix, mix, name=f"{tag}_delta")
    dq_rot, dk_cat, dv_pad = _flash_bwd(q_rot, k_cat, v_pad, dmix, _retile_rows(lse, delta.shape[2]), delta,
                                        name=f"{tag}_attn_bwd")
    dq_pad, dkr = _rope_bwd(dq_rot, dk_cat, ctab, stab, name=f"{tag}_rope_bwd")
    dw_q_pad = _mm_tn(cqn, dq_pad, name=f"{tag}_dw_q_up")
    grads["ev_w_q_up"] = [_owner_major(_unpad_heads(dw_q_pad, QK_DIM), 1)]
    dcqn = _mm(dq_pad, [(w_q_pad, 0, 0)], _first, [(Q_RANK, F32, 0)], nt=True, tn=Q_RANK, nj=1, name=f"{tag}_dcqn")[0]
    dwk = _unpad_heads(_mm_tn(ckvn, dk_cat, name=f"{tag}_dw_k_up"), QK_NOPE).reshape(KV_RANK, MLA_HEADS, QK_NOPE)
    dwv = _unpad_heads(_mm_tn(ckvn, dv_pad, name=f"{tag}_dw_v_up"), V_HEAD).reshape(KV_RANK, MLA_HEADS, V_HEAD)
    grads["ev_w_kv_up"] = [_owner_major(jnp.concatenate([dwk, dwv], axis=2).reshape(KV_RANK, -1), 1)]
    dckvn = _mm(dk_cat, [(w_kv_pad, 0, 0)], _first, [(KV_RANK, F32, 0)], nt=True, tn=KV_RANK, nj=1,
                name=f"{tag}_dckvn_k")[0]
    dckvn = _mm(dv_pad, [(w_kv_pad, 0, 1)], _add_res, [(KV_RANK, F32, 0)], nt=True, extras=[(dckvn, 0)], tn=KV_RANK,
                nj=1, name=f"{tag}_dckvn_v")[0]
    dcq, dgq = _rms_bwd(z, W["ev_q_norm"][0], dcqn, cb=2, w=Q_RANK, out_dtype=BF16, name=f"{tag}_q_norm_bwd")
    dckv, dgkv = _rms_bwd(z, W["ev_kv_norm"][0], dckvn, cb=6, w=KV_RANK, out_dtype=BF16, name=f"{tag}_kv_norm_bwd")
    grads["ev_q_norm"], grads["ev_kv_norm"] = dgq, dgkv
    du, dypre, dscale = _pool_bwd(dmix, pooled, pool_w, pool_scale, name=f"{tag}_pool_bwd")
    grads["ev_pool_scale"] = dscale
    grads["ev_pool_w"] = _mm_tn_grouped(pooled, dypre, 4, POOL_GROUP, name=f"{tag}_dpool_w")[None]
    dz = jnp.concatenate([du, dcq, dckv, dkr], axis=1)
    dw_in_pad = _mm_tn(h, dz, name=f"{tag}_dw_in")
    grads["ev_w_in"] = [_owner_major(jnp.concatenate([dw_in_pad[:, :896], dw_in_pad[:, 960:992]], axis=1), 0)]
    dx, dgn = _mm(dz, [(w_in_pad, 0, 0)], _norm_bwd_epilogue(0), [(D_MODEL, F32, 0)], nt=True,
                  extras=[(x, 0), (dxo, 0)], rows=[W["ev_norm"][0].reshape(1, D_MODEL)], sums=[D_MODEL], tn=D_MODEL,
                  nj=1, name=f"{tag}_dh")
    grads["ev_norm"] = dgn
    return dx


def _odd_block(x, reset, W, tag):
    h = _rms(x, W["od_norm"][0], name=f"{tag}_norm")
    z = _mm(h, [(W["od_w_in"][0], 0, 0)], _first, [(2 * D_MODEL, F32, 0)], tn=D_MODEL, nj=2, name=f"{tag}_in")[0]
    w_r, w_i = W["od_w_rgate"][0], W["od_w_igate"][0]
    vecs = [W[n].reshape(1, D_MODEL) for n in ("od_conv_b", "od_b_rgate", "od_b_igate", "od_lambda")]
    xc, hseq, y = _lru_fwd(z, reset, W["od_conv_w"][0], vecs[0], w_r, vecs[1], w_i, vecs[2], vecs[3],
                           name=f"{tag}_lru")
    xo = _mm(y, [(W["od_w_out"][0], 0, 0)], _add_res, [(D_MODEL, F32, 0)], extras=[(x, 0)], tn=D_MODEL, nj=1,
             name=f"{tag}_out")[0]
    return xo, (x, h, z, xc, hseq, y, vecs)


def _odd_block_bwd(dxo, reset, W, saved, tag, grads):
    x, h, z, xc, hseq, y, vecs = saved
    w_r, w_i = W["od_w_rgate"][0], W["od_w_igate"][0]
    dy = _mm(dxo, [(W["od_w_out"][0], 0, 0)], _first, [(D_MODEL, F32, 0)], nt=True, tn=D_MODEL, nj=1,
             name=f"{tag}_dy")[0]
    grads["od_w_out"] = [_owner_major(_mm_tn(y, dxo, name=f"{tag}_dw_out"), 0)]
    dgate, dxc, dpr, dpi, acc = _lru_bwd(dy, z, xc, hseq, reset, w_r, vecs[1], w_i, vecs[2], vecs[3],
                                         name=f"{tag}_lru_bwd")
    grads["od_b_rgate"], grads["od_b_igate"], grads["od_lambda"] = acc[0:1], acc[1:2], acc[2:3]
    grads["od_w_rgate"] = [_owner_major(_mm_tn_grouped(xc, dpr, LRU_HEADS, LRU_HEAD_DIM, name=f"{tag}_dw_rgate"), 1)]
    grads["od_w_igate"] = [_owner_major(_mm_tn_grouped(xc, dpi, LRU_HEADS, LRU_HEAD_DIM, name=f"{tag}_dw_igate"), 1)]
    dxb, cacc = _conv_bwd(dxc, z, W["od_conv_w"][0], name=f"{tag}_conv_bwd")
    grads["od_conv_w"], grads["od_conv_b"] = cacc[None, 0:4], cacc[4:5]
    dz = jnp.concatenate([dgate, dxb], axis=1)
    grads["od_w_in"] = [_mm_tn_owners(h, [dz], name=f"{tag}_dw_in")]
    dx, dgn = _mm(dz, [(W["od_w_in"][0], 0, 0)], _norm_bwd_epilogue(0), [(D_MODEL, F32, 0)], nt=True,
                  extras=[(x, 0), (dxo, 0)], rows=[W["od_norm"][0].reshape(1, D_MODEL)], sums=[D_MODEL], tn=D_MODEL,
                  nj=1, name=f"{tag}_dh")
    grads["od_norm"] = dgn
    return dx


def _local_step(x, mem, positions, target, W, later_weights=None, exchange_earlier=None):
    tabs = _rope_tables(positions)
    reset = (positions == 0).astype(F32)[:, None]
    grads = {n: [None, None] for n in ("xa_norm_x", "xa_norm_mem", "xa_w_q", "xa_w_kv", "xa_w_o", "ffn_norm",
                                       "ffn_w_gate_up", "ffn_w_down")}
    x1, s_even = _even_block(x, tabs, W, "l0_even")
    if later_weights is not None:
        W = {**W, **later_weights(x1)}
    x2, s_xa0 = _memory_block(x1, mem, W, 0, "l0")
    x3, s_ff0 = _ffn_block(x2, W, 0, "l0")
    x4, s_odd = _odd_block(x3, reset, W, "l1_odd")
    x5, s_xa1 = _memory_block(x4, mem, W, 1, "l1")
    x6, s_ff1 = _ffn_block(x5, W, 1, "l1")
    d, dgf, loss = _loss_head(x6, W["final_norm"], target, name="loss_head")
    grads["final_norm"] = dgf[0]
    d = _ffn_block_bwd(d, W, 1, s_ff1, "l1", grads)
    d = _memory_block_bwd(d, mem, W, 1, s_xa1, "l1", grads)
    d = _odd_block_bwd(d, reset, W, s_odd, "l1_odd", grads)
    d = _ffn_block_bwd(d, W, 0, s_ff0, "l0", grads)
    d = _memory_block_bwd(d, mem, W, 0, s_xa0, "l0", grads)
    grads["ev_w_out"] = _even_out_grad(d, s_even, "l0_even")
    token = exchange_earlier(grads) if exchange_earlier is not None else None
    d = _even_block_bwd(d, tabs, W, s_even, "l0_even", grads, token)
    big = {n: grads.pop(n) for n in MATMUL_WEIGHTS}
    for n, v in grads.items():
        if isinstance(v, list):
            grads[n] = jnp.stack(v)
    return loss[0, 0], d, big, grads


WEIGHTS = ("ev_norm", "ev_w_in", "ev_pool_w", "ev_pool_scale", "ev_q_norm", "ev_w_q_up", "ev_kv_norm", "ev_w_kv_up",
           "ev_w_out", "od_norm", "od_w_in", "od_conv_w", "od_conv_b", "od_w_rgate", "od_b_rgate", "od_w_igate",
           "od_b_igate", "od_lambda", "od_w_out", "xa_norm_x", "xa_norm_mem", "xa_w_q", "xa_w_kv", "xa_w_o",
           "ffn_norm", "ffn_w_gate_up", "ffn_w_down", "final_norm")
SHARD_AXIS = {"ev_w_in": 1, "ev_w_q_up": 2, "ev_w_kv_up": 2, "ev_w_out": 1, "od_norm": 1, "od_w_in": 2,
              "od_conv_w": 2, "od_conv_b": 1, "od_w_rgate": 2, "od_b_rgate": 1, "od_w_igate": 2, "od_b_igate": 1,
              "od_lambda": 1, "od_w_out": 1, "xa_w_q": 1, "xa_w_kv": 2, "xa_w_o": 1, "ffn_w_gate_up": 2,
              "ffn_w_down": 1}
MATMUL_WEIGHTS = ("ev_w_in", "ev_w_q_up", "ev_w_kv_up", "ev_w_out", "od_w_in", "od_w_rgate", "od_w_igate",
                  "od_w_out", "xa_w_q", "xa_w_kv", "xa_w_o", "ffn_w_gate_up", "ffn_w_down")
SMALL_SHARDED = tuple(n for n in WEIGHTS if n in SHARD_AXIS and n not in MATMUL_WEIGHTS)
REPLICATED = tuple(n for n in WEIGHTS if n not in SHARD_AXIS)


def _pack(parts, quantum):
    flat = jnp.concatenate([p.reshape(-1) for p in parts])
    pad = (-flat.shape[0]) % quantum
    return jnp.pad(flat, (0, pad)).reshape(-1, LANES)


def _unpack(flat, shapes):
    out, off = [], 0
    for shape in shapes:
        size = math.prod(shape)
        out.append(flat[off:off + size].reshape(shape))
        off += size
    return out


def _run_copies(local, remote, send_sems, recv_sems, local_sems):
    locals_ = [pltpu.make_async_copy(src, dst, local_sems.at[n]) for n, (src, dst) in enumerate(local)]
    for cp in locals_:
        cp.start()
    sends = [pltpu.make_async_remote_copy(src_ref=src, dst_ref=dst, send_sem=send_sems.at[k, n],
                                          recv_sem=recv_sems.at[k, n], device_id=dev, device_id_type=MESH)
             for (k, n, src, dst, _, dev) in remote]
    for cp in sends:
        cp.start()
    for (k, n, src, _, arrival, dev) in remote:
        pltpu.make_async_remote_copy(src_ref=src, dst_ref=arrival, send_sem=send_sems.at[k, n],
                                     recv_sem=recv_sems.at[k, n], device_id=dev, device_id_type=MESH).wait_recv()
    for cp in sends:
        cp.wait_send()
    for cp in locals_:
        cp.wait()


def _chip_peers(x, y):
    return [(1 - x, y), (x, 1 - y), (1 - x, 1 - y)]


def _owner_block(ref, axis, q):
    size = ref.shape[axis] // N_CHIPS
    idx = [slice(None)] * len(ref.shape)
    idx[axis] = pl.ds(q * size, size)
    return ref.at[tuple(idx)]


def _comm_call(body, ins, out_shapes, n_items, n_peers, *, name):
    return pl.pallas_call(
        body, in_specs=[ANY] * len(ins), out_specs=[ANY] * len(out_shapes), out_shape=out_shapes,
        scratch_shapes=[pltpu.SemaphoreType.DMA((n_peers, n_items)), pltpu.SemaphoreType.DMA((n_peers, n_items)),
                        pltpu.SemaphoreType.DMA((n_items,))],
        name=name,
    )(*ins)


def _gather_chips(shards, axes, *, name):
    n = len(shards)
    full = [jax.ShapeDtypeStruct(tuple(d * (N_CHIPS if a == ax else 1) for a, d in enumerate(s.shape)), s.dtype)
            for s, ax in zip(shards, axes)]

    def body(*refs):
        srcs, dsts = refs[:n], refs[n:2 * n]
        x, y, c = lax.axis_index("x"), lax.axis_index("y"), lax.axis_index("c")
        me = 2 * x + y
        local = [(srcs[i], _owner_block(dsts[i], axes[i], me)) for i in range(n)]
        remote = [(k, i, srcs[i], _owner_block(dsts[i], axes[i], me), _owner_block(dsts[i], axes[i], 2 * px + py),
                   (px, py, c))
                  for k, (px, py) in enumerate(_chip_peers(x, y)) for i in range(n)]
        _run_copies(local, remote, *refs[2 * n:])

    return _comm_call(body, shards, full, n, 3, name=name)


HBM = pl.BlockSpec(memory_space=pltpu.HBM)
SEM = pl.BlockSpec(memory_space=pltpu.SEMAPHORE)
DATAFLOW = pltpu.SideEffectType.DATAFLOW_SIDE_EFFECTING


def _gather_plan(axes):
    return lambda srcs, lands, me, peer: [
        (srcs[i], _owner_block(lands[i], ax, me), _owner_block(lands[i], ax, peer)) for i, ax in enumerate(axes)]


def _exchange_plan(where):
    return lambda srcs, lands, me, peer: [
        (srcs[i].at[peer], lands[n].at[me, l], lands[n].at[peer, l]) for i, (n, l) in enumerate(where)]


def _split_peers(sibling):
    x, y, c = lax.axis_index("x"), lax.axis_index("y"), lax.axis_index("c")
    peers = [((px, py, c), 2 * px + py) for px, py in _chip_peers(x, y)]
    return 2 * x + y, peers + ([((x, y, 1 - c), 2 * x + y)] if sibling else [])


def _split_start(srcs, lands, plan, *, sibling=False, name):
    ns, nl = len(srcs), len(lands)
    nsem = (3 + sibling) * len(plan(list(srcs), list(lands), 0, 0))

    def body(*refs):
        src_refs, land_refs = refs[:ns], refs[ns:ns + nl]
        send_sems, recv_sems = refs[ns + nl:ns + nl + nsem], refs[ns + nl + nsem:ns + nl + 2 * nsem]
        me, peers = _split_peers(sibling)
        n = 0
        for device, chip in peers:
            for src, dst, _ in plan(src_refs, land_refs, me, chip):
                pltpu.make_async_remote_copy(src_ref=src, dst_ref=dst, send_sem=send_sems[n], recv_sem=recv_sems[n],
                                             device_id=device, device_id_type=MESH).start()
                n += 1
        refs[-1][...] = jnp.zeros_like(refs[-1])

    arrays = list(srcs) + list(lands)
    out = pl.pallas_call(
        body, name=name, in_specs=[HBM] * (ns + nl),
        out_specs=[SEM] * (2 * nsem) + [HBM] * (ns + nl) + [pl.BlockSpec(memory_space=pltpu.VMEM)],
        out_shape=[pltpu.SemaphoreType.DMA(())] * (2 * nsem) + [pltpu.HBM(a.shape, a.dtype) for a in arrays]
        + [jax.ShapeDtypeStruct((8, LANES), F32)],
        input_output_aliases={i: 2 * nsem + i for i in range(ns + nl)},
        compiler_params=pltpu.CompilerParams(has_side_effects=DATAFLOW),
    )(*[pltpu.with_memory_space_constraint(a, pltpu.HBM) for a in arrays])
    sems, rest = out[:2 * nsem], out[2 * nsem:]
    return sems[:nsem], sems[nsem:], rest[:ns], rest[ns:ns + nl], rest[-1]


def _split_wait(handle, after, plan, *, sibling=False, name):
    send_sems, recv_sems, srcs, lands, _ = handle
    ns, nl, nsem = len(srcs), len(lands), len(send_sems)

    def body(*refs):
        src_refs, land_refs = refs[:ns], refs[ns:ns + nl]
        send_refs, recv_refs = refs[ns + nl:ns + nl + nsem], refs[ns + nl + nsem:ns + nl + 2 * nsem]
        me, peers = _split_peers(sibling)
        n = 0
        for device, chip in peers:
            for src, _, arrival in plan(src_refs, land_refs, me, chip):
                cp = pltpu.make_async_remote_copy(src_ref=src, dst_ref=arrival, send_sem=send_refs[n],
                                                  recv_sem=recv_refs[n], device_id=device, device_id_type=MESH)
                cp.wait_send()
                cp.wait_recv()
                n += 1

    out = pl.pallas_call(
        body, name=name, in_specs=[HBM] * (ns + nl) + [SEM] * (2 * nsem) + [ANY], out_specs=[HBM] * (ns + nl),
        out_shape=[pltpu.HBM(a.shape, a.dtype) for a in list(srcs) + list(lands)],
        input_output_aliases={i: i for i in range(ns + nl)},
        compiler_params=pltpu.CompilerParams(has_side_effects=DATAFLOW),
    )(*srcs, *lands, *send_sems, *recv_sems, after)
    return out[ns:]


def _exchange_sibling(arrays, *, name):
    n = len(arrays)

    def body(*refs):
        x, y, c = lax.axis_index("x"), lax.axis_index("y"), lax.axis_index("c")
        remote = [(0, i, refs[i], refs[n + i], refs[n + i], (x, y, 1 - c)) for i in range(n)]
        _run_copies([], remote, *refs[2 * n:])

    return _comm_call(body, arrays, [jax.ShapeDtypeStruct(a.shape, a.dtype) for a in arrays], n, 1, name=name)


def _sum_slots(r, *, token=None, name):
    shape = r.shape[1:]
    cols = shape[-1]
    rows = math.prod(shape) // cols
    tr = max(d for d in range(8, rows + 1, 8) if rows % d == 0 and d * cols * 16 <= (4 << 20))

    def body(r_ref, *refs):
        total = ((r_ref[0] + r_ref[1]) + r_ref[2]) + r_ref[3]
        refs[-1][...] = total if token is None else total + refs[0][0:1, 0:1]

    in_specs = [pl.BlockSpec((N_CHIPS, tr, cols), lambda i: (0, i, 0))]
    in_specs += [] if token is None else [_const((8, LANES))]
    return pl.pallas_call(
        body, grid=(rows // tr,), in_specs=in_specs,
        out_specs=_rows(tr, cols), out_shape=jax.ShapeDtypeStruct((rows, cols), F32), compiler_params=_cp(1),
        name=name,
    )(r.reshape(N_CHIPS, rows, cols), *([] if token is None else [token])).reshape(shape)


FIRST_WEIGHTS = ("ev_w_in", "ev_w_q_up", "ev_w_kv_up", "ev_w_out")
LATER_WEIGHTS = tuple(n for n in MATMUL_WEIGHTS if n not in FIRST_WEIGHTS)
LAST_GRADS = ("ev_w_in", "ev_w_q_up", "ev_w_kv_up")
EARLIER_GRADS = tuple(n for n in MATMUL_WEIGHTS if n not in LAST_GRADS)


def _my_chip():
    return 2 * lax.axis_index("x") + lax.axis_index("y")


def _gather_first(w):
    small = _pack([w[n] for n in SMALL_SHARDED], 8 * LANES)
    stacked = [n for n in FIRST_WEIGHTS if SHARD_AXIS[n] == w[n].ndim - 1 and w[n].shape[-1] % LANES]
    shards = [w[n].astype(BF16)[None] if n in stacked else w[n].astype(BF16) for n in FIRST_WEIGHTS]
    got = _gather_chips(shards + [small], [0 if n in stacked else SHARD_AXIS[n] for n in FIRST_WEIGHTS] + [0],
                        name="gather_first")
    full = {n: w[n] for n in REPLICATED}
    for n, g in zip(FIRST_WEIGHTS, got[:-1]):
        full[n] = jnp.concatenate([g[q] for q in range(N_CHIPS)], axis=SHARD_AXIS[n]) if n in stacked else g
    per_chip = [_unpack(got[-1][q * small.shape[0]:(q + 1) * small.shape[0]].reshape(-1),
                        [w[n].shape for n in SMALL_SHARDED]) for q in range(N_CHIPS)]
    for i, n in enumerate(SMALL_SHARDED):
        full[n] = jnp.concatenate([per_chip[q][i] for q in range(N_CHIPS)], axis=SHARD_AXIS[n])
    return full


def _gather_later_start(w):
    shards = [w[n].astype(BF16) for n in LATER_WEIGHTS]
    axes = [SHARD_AXIS[n] for n in LATER_WEIGHTS]
    lands = [lax.empty(tuple(d * (N_CHIPS if a == ax else 1) for a, d in enumerate(s.shape)), s.dtype)
             for s, ax in zip(shards, axes)]
    plan = _gather_plan(axes)
    return _split_start(shards, lands, plan, sibling=True, name="gather_later_start"), plan


def _owner_major(g, axis):
    shape = g.shape
    size = shape[axis] // N_CHIPS
    g = jnp.moveaxis(g.reshape(shape[:axis] + (N_CHIPS, size) + shape[axis + 1:]), axis, 0)
    return g.reshape(N_CHIPS, -1, shape[-1] if axis < len(shape) - 1 else size)


def _exchange_start(items, *, cross, name):
    me = _my_chip()
    srcs, lands, where = [], [], []
    for n, layers in enumerate(items):
        land = lax.empty((N_CHIPS, len(layers)) + layers[0].shape[1:], layers[0].dtype)
        for l, a in enumerate(layers):
            if not cross:
                own = lax.dynamic_index_in_dim(a, me, 0, keepdims=True)[:, None]
                land = lax.dynamic_update_slice(land, own, (me, l) + (0,) * (a.ndim - 1))
            srcs.append(a)
            where.append((n, l))
        lands.append(land)
    plan = _exchange_plan(where)
    return _split_start(srcs, lands, plan, sibling=cross, name=name), plan


def _earlier_items(grads, full_shapes):
    small = [_pack([jnp.split(grads[n].reshape(full_shapes[n]), N_CHIPS, axis=SHARD_AXIS[n])[q]
                    for n in SMALL_SHARDED], 8 * LANES) for q in range(N_CHIPS)]
    return [grads[n] for n in EARLIER_GRADS] + [[jnp.stack(small)]]


def _last_items(big, grads, full_shapes, loss):
    repl = _pack([grads[n].reshape(full_shapes[n]) for n in REPLICATED] + [loss.reshape(1)], 8 * LANES)
    return [big[n] for n in LAST_GRADS] + [[jnp.stack([repl] * N_CHIPS)]]


def kernel(
        x, mem, positions, ev_norm, ev_w_in, ev_pool_w, ev_pool_scale, ev_q_norm, ev_w_q_up, ev_kv_norm,
        ev_w_kv_up, ev_w_out, od_norm, od_w_in, od_conv_w, od_conv_b, od_w_rgate, od_b_rgate, od_w_igate,
        od_b_igate, od_lambda, od_w_out, xa_norm_x, xa_norm_mem, xa_w_q, xa_w_kv, xa_w_o, ffn_norm,
        ffn_w_gate_up, ffn_w_down, final_norm, loss_target, m_ev_norm, m_ev_w_in, m_ev_pool_w, m_ev_pool_scale,
        m_ev_q_norm, m_ev_w_q_up, m_ev_kv_norm, m_ev_w_kv_up, m_ev_w_out, m_od_norm, m_od_w_in, m_od_conv_w,
        m_od_conv_b, m_od_w_rgate, m_od_b_rgate, m_od_w_igate, m_od_b_igate, m_od_lambda, m_od_w_out,
        m_xa_norm_x, m_xa_norm_mem, m_xa_w_q, m_xa_w_kv, m_xa_w_o, m_ffn_norm, m_ffn_w_gate_up, m_ffn_w_down,
        m_final_norm, v_ev_norm, v_ev_w_in, v_ev_pool_w, v_ev_pool_scale, v_ev_q_norm, v_ev_w_q_up,
        v_ev_kv_norm, v_ev_w_kv_up, v_ev_w_out, v_od_norm, v_od_w_in, v_od_conv_w, v_od_conv_b, v_od_w_rgate,
        v_od_b_rgate, v_od_w_igate, v_od_b_igate, v_od_lambda, v_od_w_out, v_xa_norm_x, v_xa_norm_mem, v_xa_w_q,
        v_xa_w_kv, v_xa_w_o, v_ffn_norm, v_ffn_w_gate_up, v_ffn_w_down, v_final_norm):
    given = dict(locals())
    w = {n: given[n] for n in WEIGHTS}
    full_shapes = {n: tuple(d * (N_CHIPS if a == SHARD_AXIS.get(n) else 1) for a, d in enumerate(w[n].shape))
                   for n in WEIGHTS}
    full = _gather_first(w)
    later, later_plan = _gather_later_start(w)
    full["ev_norm"] = full["ev_norm"] + later[4][0:1, 0:1]
    exchange = {}

    def later_weights(after):
        return dict(zip(LATER_WEIGHTS, _split_wait(later, after, later_plan, sibling=True, name="gather_later_wait")))

    def exchange_earlier(grads):
        exchange["handle"], exchange["plan"] = _exchange_start(_earlier_items(grads, full_shapes), cross=True,
                                                               name="exchange_earlier_start")
        return exchange["handle"][4]

    loss, grad_x, big, grads = _local_step(x[0], mem[0], positions[0], loss_target[0], full, later_weights,
                                           exchange_earlier)
    earlier = EARLIER_GRADS + ("small",)
    got = dict(zip(earlier, _split_wait(exchange["handle"], grad_x, exchange["plan"], sibling=True,
                                        name="exchange_earlier_wait")))
    last, last_plan = _exchange_start(_last_items(big, grads, full_shapes, loss), cross=False,
                                      name="exchange_last_start")
    sums = {n: _sum_slots(got[n], token=last[4] if i == 0 else None, name=f"sum_chips_{n}")
            for i, n in enumerate(earlier)}
    got = dict(zip(LAST_GRADS + ("replicated",),
                   _split_wait(last, sums[earlier[-1]], last_plan, name="exchange_last_wait")))
    sums.update({n: _sum_slots(got[n], name=f"sum_chips_{n}") for n in got})
    mine = [sums[n] for n in MATMUL_WEIGHTS + ("small", "replicated")]
    other = _exchange_sibling(mine, name="exchange_sibling")
    out = {}
    for i, n in enumerate(MATMUL_WEIGHTS):
        out[n] = _adamw(w[n], mine[i].reshape(w[n].shape), other[i].reshape(w[n].shape), given["m_" + n],
                        given["v_" + n], name=f"adamw_{n}")
    for i, group in ((len(MATMUL_WEIGHTS), SMALL_SHARDED), (len(MATMUL_WEIGHTS) + 1, REPLICATED)):
        spare = [jnp.zeros((1,), F32)] if group is REPLICATED else []
        packed = [_pack([given[pre + n] for n in group] + spare, 8 * LANES) for pre in ("", "m_", "v_")]
        res = _adamw(packed[0], mine[i].reshape(packed[0].shape), other[i].reshape(packed[0].shape), packed[1],
                     packed[2], name=f"adamw_group{i}")
        shapes = [w[n].shape for n in group] + [(1,)] * len(spare)
        for j, arrs in enumerate(zip(*[_unpack(r.reshape(-1), shapes) for r in res])):
            if j < len(group):
                out[group[j]] = list(arrs)
            else:
                loss = arrs[0][0]
    return (loss, grad_x[None], *[out[n][k] for k in range(4) for n in WEIGHTS])
```

```python
import functools
import math

import jax
import jax.numpy as jnp
from jax import lax
from jax.experimental import pallas as pl
from jax.experimental.pallas import tpu as pltpu

F32 = jnp.float32
BF16 = jnp.bfloat16

D_MODEL = 1024
POOL_DIM = 512
POOL_WINDOWS = (2, 4, 8, 16)
POOL_GROUP = 128
MLA_HEADS = 8
QK_NOPE = 64
QK_ROPE = 32
QK_DIM = QK_NOPE + QK_ROPE
V_HEAD = 64
HEAD_PAD = 128
Q_RANK = 256
KV_RANK = 128
ROPE_BASE = 10000.0
LRU_HEADS = 4
LRU_HEAD_DIM = 256
CONV_WIDTH = 4
LRU_C = 8.0
MEM_HEADS = 4
MEM_HEAD_DIM = 256
D_FF = 2816
RMS_EPS = 1e-6
NEG_INF = -1e30

ADAM_LR = 0.001
ADAM_B1 = 0.9
ADAM_B2 = 0.999
ADAM_EPS = 1e-08
ADAM_WD = 0.01
ADAM_STEP = 10

N_CHIPS = 4
LANES = 128
VMEM_LIMIT = 56 * 1024 * 1024
MESH = pl.DeviceIdType.MESH
ANY = pl.BlockSpec(memory_space=pl.ANY)
MIX_DIM = POOL_DIM + MLA_HEADS * HEAD_PAD

NN = (((1,), (0,)), ((), ()))
NT = (((1,), (1,)), ((), ()))
TN = (((0,), (0,)), ((), ()))


def _cp(n):
    return pltpu.CompilerParams(dimension_semantics=("arbitrary",) * n, vmem_limit_bytes=VMEM_LIMIT)


def _dot(a, b, dims=NN):
    return lax.dot_general(a, b, dims, preferred_element_type=F32)


def _row_tile(S):
    return 1024 if S % 1024 == 0 else min(S, 512)


def _rows(ts, w, cb=0):
    return pl.BlockSpec((ts, w), lambda i: (i, cb))


def _const(shape):
    return pl.BlockSpec(shape, lambda i: (0,) * len(shape))


MM_VMEM_BUDGET = 40 * 1024 * 1024


def _mm(a, bs, epi, outs, *, tn, nj, nt=False, also=None, extras=(), rows=(), sums=(), a_cb=0, k=None, tm=None,
        name):
    M = a.shape[0]
    k = k or a.shape[1]
    nb, ne, nr, no = len(bs), len(extras), len(rows), len(outs)
    lhs = [(a, k, a_cb, b) for b in bs[:1]] + ([(also[0], also[0].shape[1], 0, also[1])] if also else [])
    if tm is None:
        per_row = 2 * (sum(kk * x.dtype.itemsize for x, kk, _, _ in lhs)
                       + sum(e.dtype.itemsize for e, _ in extras) * tn
                       + sum(jnp.dtype(dt).itemsize for _, dt, _ in outs) * tn) + nb * tn * 4
        weights = (1 if nj == 1 else 2) * (sum(b.dtype.itemsize for b, _, _ in bs) * k
                                           + (also[1][0].dtype.itemsize * lhs[-1][1] if also else 0)) * tn
        tm = 1024 if M % 1024 == 0 and 1024 * per_row + weights <= MM_VMEM_BUDGET else min(M, 512)
    dims = NT if nt else NN
    assert not sums or nj == 1
    na = 2 if also else 0

    def body(*refs):
        av = refs[0][...].astype(BF16)
        accs = [_dot(av, r[...].astype(BF16), dims) for r in refs[1:1 + nb]]
        if also:
            accs[0] = accs[0] + _dot(refs[1 + nb][...].astype(BF16), refs[2 + nb][...].astype(BF16), dims)
        refs = refs[:1 + nb] + refs[1 + nb + na:]
        vals = epi(accs, [r[...] for r in refs[1 + nb:1 + nb + ne + nr]])
        outs_refs = refs[1 + nb + ne + nr:]
        for o, v in zip(outs_refs[:no], vals[:no]):
            o[...] = v.astype(o.dtype)
        if sums:
            @pl.when(pl.program_id(1) == 0)
            def _():
                for o in outs_refs[no:]:
                    o[...] = jnp.zeros_like(o)

            for o, v in zip(outs_refs[no:], vals[no:]):
                o[...] += v

    in_specs = [pl.BlockSpec((tm, k), lambda j, i: (i, a_cb))]
    weights = [(k, rb, cb) for (_, rb, cb) in bs]
    if also:
        in_specs_also = pl.BlockSpec((tm, lhs[-1][1]), lambda j, i: (i, 0))
        weights.append((lhs[-1][1], also[1][1], also[1][2]))
    for n, (kk, rb, cb) in enumerate(weights):
        if also and n == nb:
            in_specs.append(in_specs_also)
        mode = dict(pipeline_mode=pl.Buffered(1)) if nj == 1 else {}
        if nt:
            in_specs.append(pl.BlockSpec((tn, kk), lambda j, i, rb=rb, cb=cb: (rb + j, cb), **mode))
        else:
            in_specs.append(pl.BlockSpec((kk, tn), lambda j, i, rb=rb, cb=cb: (rb, cb + j), **mode))
    for (_, cb) in extras:
        in_specs.append(pl.BlockSpec((tm, tn), lambda j, i, cb=cb: (i, cb + j)))
    in_specs += [pl.BlockSpec((1, tn), lambda j, i: (0, 0))] * nr
    out_specs = [pl.BlockSpec((tm, tn), lambda j, i, cb=cb: (i, cb + j)) for (_, _, cb) in outs]
    out_specs += [pl.BlockSpec((1, w), lambda j, i: (0, 0)) for w in sums]
    res = pl.pallas_call(
        body, grid=(nj, M // tm), in_specs=in_specs, out_specs=out_specs,
        out_shape=[jax.ShapeDtypeStruct((M, n), dt) for (n, dt, _) in outs]
        + [jax.ShapeDtypeStruct((1, w), F32) for w in sums],
        compiler_params=_cp(2), name=name,
    )(a, *[b for (b, _, _) in bs], *([also[0], also[1][0]] if also else []), *[e for (e, _) in extras], *rows)
    return res


def _first(accs, extras):
    return [accs[0]]


def _add_res(accs, extras):
    return [accs[0] + extras[0].astype(F32)]


def _norm_bwd_epilogue(partials):
    def epi(accs, vals):
        dh = accs[0]
        for part in vals[:partials]:
            dh = dh + part.astype(F32)
        x, res, g = vals[partials:partials + 3]
        r = lax.rsqrt(jnp.mean(x * x, axis=-1, keepdims=True) + RMS_EPS)
        n = x * r
        dn = dh * g
        return [r * (dn - n * jnp.mean(dn * n, axis=-1, keepdims=True)) + res, jnp.sum(dh * n, axis=0, keepdims=True)]

    return epi


TN_VMEM_BUDGET = 36 * 1024 * 1024


def _contraction_rows(S, row_bytes, out_elems):
    ts = min(S, 2048)
    while ts > 512 and 2 * (ts * row_bytes + out_elems * 4) > TN_VMEM_BUDGET:
        ts //= 2
    return ts


def _mm_tn(a, b, *, ka=None, a_cb=0, nb=None, b_cb=0, tk=None, tn=None, ts=None, name):
    S = a.shape[0]
    ka = ka or a.shape[1]
    nb = nb or b.shape[1]
    tk = tk or ka
    tn = tn or nb
    ts = ts or _contraction_rows(S, tk * a.dtype.itemsize + tn * b.dtype.itemsize, tk * tn)
    a0, b0 = a_cb * (ka // tk), b_cb * (nb // tn)

    def body(a_ref, b_ref, o_ref):
        @pl.when(pl.program_id(2) == 0)
        def _():
            o_ref[...] = jnp.zeros_like(o_ref)

        o_ref[...] += _dot(a_ref[...].astype(BF16), b_ref[...].astype(BF16), TN)

    return pl.pallas_call(
        body, grid=(ka // tk, nb // tn, S // ts),
        in_specs=[pl.BlockSpec((ts, tk), lambda p, q, s: (s, a0 + p)),
                  pl.BlockSpec((ts, tn), lambda p, q, s: (s, b0 + q))],
        out_specs=pl.BlockSpec((tk, tn), lambda p, q, s: (p, q)),
        out_shape=jax.ShapeDtypeStruct((ka, nb), F32), compiler_params=_cp(3), name=name,
    )(a, b)


def _mm_tn_owners(a, bs, *, name):
    S, ka = a.shape
    nb = sum(b.shape[1] for b in bs)
    tn = nb // N_CHIPS
    ts = _contraction_rows(S, ka * a.dtype.itemsize + len(bs) * tn * bs[0].dtype.itemsize, ka * tn)
    per = N_CHIPS // len(bs)

    def body(a_ref, *refs):
        o_ref = refs[-1]
        q = pl.program_id(0)

        @pl.when(pl.program_id(1) == 0)
        def _():
            o_ref[...] = jnp.zeros_like(o_ref)

        av = a_ref[...].astype(BF16)
        for n, b_ref in enumerate(refs[:-1]):
            @pl.when(q // per == n)
            def _():
                o_ref[0] += _dot(av, b_ref[...].astype(BF16), TN)

    in_specs = [pl.BlockSpec((ts, ka), lambda q, s: (s, 0))]
    for n in range(len(bs)):
        in_specs.append(pl.BlockSpec((ts, tn), lambda q, s, n=n: (jnp.where(q // per == n, s, 0),
                                                                  jnp.clip(q - n * per, 0, per - 1))))
    return pl.pallas_call(
        body, grid=(N_CHIPS, S // ts), in_specs=in_specs,
        out_specs=pl.BlockSpec((1, ka, tn), lambda q, s: (q, 0, 0)),
        out_shape=jax.ShapeDtypeStruct((N_CHIPS, ka, tn), F32), compiler_params=_cp(2), name=name,
    )(a, *bs)


def _mm_tn_grouped(a, b, groups, w, *, name):
    S = a.shape[0]
    ts = _contraction_rows(S, w * (a.dtype.itemsize + b.dtype.itemsize), w * w)

    def body(a_ref, b_ref, o_ref):
        @pl.when(pl.program_id(1) == 0)
        def _():
            o_ref[...] = jnp.zeros_like(o_ref)

        o_ref[0] += _dot(a_ref[...].astype(BF16), b_ref[...].astype(BF16), TN)

    return pl.pallas_call(
        body, grid=(groups, S // ts),
        in_specs=[pl.BlockSpec((ts, w), lambda g, s: (s, g)), pl.BlockSpec((ts, w), lambda g, s: (s, g))],
        out_specs=pl.BlockSpec((1, w, w), lambda g, s: (g, 0, 0)),
        out_shape=jax.ShapeDtypeStruct((groups, w, w), F32), compiler_params=_cp(2), name=name,
    )(a, b)


def _rms(x, g, *, cb=0, w=None, ts=None, name):
    S = x.shape[0]
    w = w or x.shape[1]
    ts = ts or _row_tile(S)

    def body(x_ref, g_ref, o_ref):
        xv = x_ref[...].astype(F32)
        r = lax.rsqrt(jnp.mean(xv * xv, axis=-1, keepdims=True) + RMS_EPS)
        o_ref[...] = (xv * r * g_ref[...]).astype(o_ref.dtype)

    return pl.pallas_call(
        body, grid=(S // ts,), in_specs=[_rows(ts, w, cb), _const((1, w))], out_specs=_rows(ts, w),
        out_shape=jax.ShapeDtypeStruct((S, w), BF16), compiler_params=_cp(1), name=name,
    )(x, g.reshape(1, w))


def _rms_bwd(x, g, dy, *, cb=0, w=None, res=None, out_dtype=F32, ts=None, name):
    S = x.shape[0]
    w = w or x.shape[1]
    ts = ts or min(S, 512)
    has_res = res is not None

    def body(*refs):
        x_ref, g_ref, dy_ref = refs[:3]
        dx_ref, dg_ref = refs[-2:]
        xv = x_ref[...].astype(F32)
        r = lax.rsqrt(jnp.mean(xv * xv, axis=-1, keepdims=True) + RMS_EPS)
        n = xv * r
        dyv = dy_ref[...].astype(F32)
        dn = dyv * g_ref[...]
        dx = r * (dn - n * jnp.mean(dn * n, axis=-1, keepdims=True))
        if has_res:
            dx = dx + refs[3][...].astype(F32)
        dx_ref[...] = dx.astype(dx_ref.dtype)

        @pl.when(pl.program_id(0) == 0)
        def _():
            dg_ref[...] = jnp.zeros_like(dg_ref)

        dg_ref[...] += jnp.sum(dyv * n, axis=0, keepdims=True)

    ins = [x, g.reshape(1, w), dy] + ([res] if has_res else [])
    in_specs = [_rows(ts, w, cb), _const((1, w)), _rows(ts, w)] + ([_rows(ts, w)] if has_res else [])
    return pl.pallas_call(
        body, grid=(S // ts,), in_specs=in_specs, out_specs=[_rows(ts, w), _const((1, w))],
        out_shape=[jax.ShapeDtypeStruct((S, w), out_dtype), jax.ShapeDtypeStruct((1, w), F32)],
        compiler_params=_cp(1), name=name,
    )(*ins)


HALO = 16


def _pool_counts(i, ts, rows, first_row):
    t = i * ts + first_row + lax.broadcasted_iota(jnp.int32, (rows, 1), 0)
    return [jnp.minimum(t + 1, w).astype(F32) for w in POOL_WINDOWS]


def _even_front(x, g, w_in, pool_w, pool_scale, g_q, w_q, g_kv, w_kv, ctab, stab, *, name):
    S = x.shape[0]
    ts = min(S, 512)

    def body(x_ref, g_ref, win_ref, pw_ref, sc_ref, gq_ref, wq_ref, gkv_ref, wkv_ref, c_ref, s_ref,
             h_ref, z_ref, y_ref, p_ref, cqn_ref, ckvn_ref, q_ref, k_ref, v_ref, tail):
        i = pl.program_id(0)

        def normed(t, gain):
            r = lax.rsqrt(jnp.mean(t * t, axis=-1, keepdims=True) + RMS_EPS)
            return (t * r * gain).astype(BF16)

        h = normed(x_ref[...], g_ref[...])
        h_ref[...] = h
        z = _dot(h, win_ref[...])
        z_ref[...] = z
        u = z[:, :POOL_DIM]
        xe = jnp.concatenate([jnp.where(i > 0, tail[...], 0.0), u], axis=0)
        tail[...] = u[ts - HALO:]
        sums = []
        s = xe
        for sh in (1, 2, 4, 8):
            s = s + pltpu.roll(s, sh, 0)
            sums.append(s)
        cnts = _pool_counts(i, ts, ts, 0)
        for grp in range(4):
            lo, hi = grp * POOL_GROUP, (grp + 1) * POOL_GROUP
            pooled = (sums[grp][HALO:, lo:hi] / cnts[grp] - u[:, lo:hi]).astype(BF16)
            p_ref[:, lo:hi] = pooled
            y_ref[:, lo:hi] = (_dot(pooled, pw_ref[grp]) * sc_ref[:, lo:hi]).astype(y_ref.dtype)
        cqn = normed(z[:, POOL_DIM:POOL_DIM + Q_RANK], gq_ref[...])
        ckvn = normed(z[:, POOL_DIM + Q_RANK:POOL_DIM + Q_RANK + KV_RANK], gkv_ref[...])
        cqn_ref[...] = cqn
        ckvn_ref[...] = ckvn
        q = _dot(cqn, wq_ref[...])
        kv = _dot(ckvn, wkv_ref[...])
        c, sn = c_ref[...], s_ref[...]
        kr = z[:, D_MODEL - HEAD_PAD:]
        kr_rot = kr * c + _rope_partner(kr) * sn
        lane = lax.broadcasted_iota(jnp.int32, (ts, HEAD_PAD), 1)
        for hd in range(MLA_HEADS):
            lo, hi = hd * HEAD_PAD, (hd + 1) * HEAD_PAD
            qh = q[:, lo:hi]
            q_ref[:, lo:hi] = (qh * c + _rope_partner(qh) * sn).astype(q_ref.dtype)
            k_ref[:, lo:hi] = (kv[:, lo:hi] + kr_rot).astype(k_ref.dtype)
            v_ref[:, lo:hi] = jnp.where(lane == V_HEAD, 1.0, kv[:, D_MODEL + lo:D_MODEL + hi]).astype(v_ref.dtype)

    wide = jax.ShapeDtypeStruct((S, D_MODEL), BF16)
    return pl.pallas_call(
        body, grid=(S // ts,),
        in_specs=[_rows(ts, D_MODEL), _const((1, D_MODEL)), _const((D_MODEL, D_MODEL)),
                  _const((4, POOL_GROUP, POOL_GROUP)), _const((1, POOL_DIM)), _const((1, Q_RANK)),
                  _const((Q_RANK, D_MODEL)), _const((1, KV_RANK)), _const((KV_RANK, 2 * D_MODEL)),
                  _rows(ts, HEAD_PAD), _rows(ts, HEAD_PAD)],
        out_specs=[_rows(ts, D_MODEL), _rows(ts, D_MODEL), _rows(ts, POOL_DIM), _rows(ts, POOL_DIM),
                   _rows(ts, Q_RANK), _rows(ts, KV_RANK), _rows(ts, D_MODEL), _rows(ts, D_MODEL), _rows(ts, D_MODEL)],
        out_shape=[wide, jax.ShapeDtypeStruct((S, D_MODEL), F32), jax.ShapeDtypeStruct((S, MIX_DIM), BF16),
                   jax.ShapeDtypeStruct((S, POOL_DIM), BF16), jax.ShapeDtypeStruct((S, Q_RANK), BF16),
                   jax.ShapeDtypeStruct((S, KV_RANK), BF16), wide, wide, wide],
        scratch_shapes=[pltpu.VMEM((HALO, POOL_DIM), F32)], compiler_params=_cp(1), name=name,
    )(x, g.reshape(1, D_MODEL), w_in, pool_w, pool_scale, g_q.reshape(1, Q_RANK), w_q, g_kv.reshape(1, KV_RANK), w_kv,
      ctab, stab)


def _pool_bwd(dmix, pooled, pool_w, pool_scale, *, name):
    S = dmix.shape[0]
    ts = min(S, 512)
    nh = ts // HALO
    last = S // HALO - 1

    def body(dy_ref, dyh_ref, p_ref, w_ref, sc_ref, du_ref, dyp_ref, dsc_ref):
        i = pl.program_id(0)
        dyv = dy_ref[...].astype(F32)
        dyh = jnp.where(i < pl.num_programs(0) - 1, dyh_ref[...].astype(F32), 0.0)
        dye = jnp.concatenate([dyv, dyh], axis=0) * sc_ref[...]
        dypre = dye.astype(BF16)
        dyp_ref[...] = dypre[:ts]
        cnts = _pool_counts(i, ts, ts + HALO, 0)
        n = ts + HALO
        dsc = []
        for g in range(4):
            lo, hi = g * POOL_GROUP, (g + 1) * POOL_GROUP
            ypre = _dot(p_ref[:, lo:hi], w_ref[g])
            dsc.append(jnp.sum(dyv[:, lo:hi] * ypre, axis=0, keepdims=True))
            dpool = _dot(dypre[:, lo:hi], w_ref[g], NT)
            s = dpool / cnts[g]
            for sh in (1, 2, 4, 8)[:g + 1]:
                s = s + pltpu.roll(s, n - sh, 0)
            du_ref[:, lo:hi] = (s[:ts] - dpool[:ts]).astype(du_ref.dtype)

        @pl.when(i == 0)
        def _():
            dsc_ref[...] = jnp.zeros_like(dsc_ref)

        dsc_ref[...] += jnp.concatenate(dsc, axis=1)

    return pl.pallas_call(
        body, grid=(S // ts,),
        in_specs=[_rows(ts, POOL_DIM),
                  pl.BlockSpec((HALO, POOL_DIM), lambda i: (jnp.minimum((i + 1) * nh, last), 0)),
                  _rows(ts, POOL_DIM), _const((4, POOL_GROUP, POOL_GROUP)), _const((1, POOL_DIM))],
        out_specs=[_rows(ts, POOL_DIM), _rows(ts, POOL_DIM), _const((1, POOL_DIM))],
        out_shape=[jax.ShapeDtypeStruct((S, POOL_DIM), BF16)] * 2 + [jax.ShapeDtypeStruct((1, POOL_DIM), F32)],
        compiler_params=_cp(1), name=name,
    )(dmix, dmix, pooled, pool_w, pool_scale)


def _rope_partner(t):
    lane = lax.broadcasted_iota(jnp.int32, t.shape, 1)
    swapped = jnp.where(lane < QK_NOPE + QK_ROPE // 2, pltpu.roll(t, HEAD_PAD - QK_ROPE // 2, 1),
                        pltpu.roll(t, QK_ROPE // 2, 1))
    return jnp.where((lane >= QK_NOPE) & (lane < QK_DIM), swapped, 0.0)


def _rope_bwd(dq_rot, dk_cat, ctab, stab, *, name):
    S = dq_rot.shape[0]
    ts = min(S, 512)

    def body(dq_ref, dk_ref, c_ref, s_ref, dqo_ref, dkr_ref):
        c, s = c_ref[...], s_ref[...]
        for h in range(MLA_HEADS):
            g = dq_ref[:, h * HEAD_PAD:(h + 1) * HEAD_PAD]
            dqo_ref[:, h * HEAD_PAD:(h + 1) * HEAD_PAD] = (g * c + _rope_partner(g * s)).astype(dqo_ref.dtype)
        dk = dk_ref[...]
        g = dk[:, :HEAD_PAD]
        for h in range(1, MLA_HEADS):
            g = g + dk[:, h * HEAD_PAD:(h + 1) * HEAD_PAD]
        lane = lax.broadcasted_iota(jnp.int32, g.shape, 1)
        on_rope = (lane >= QK_NOPE) & (lane < QK_DIM)
        dkr_ref[...] = jnp.where(on_rope, g * c + _rope_partner(g * s), 0.0).astype(dkr_ref.dtype)

    wide = _rows(ts, MLA_HEADS * HEAD_PAD)
    return pl.pallas_call(
        body, grid=(S // ts,), in_specs=[wide, wide, _rows(ts, HEAD_PAD), _rows(ts, HEAD_PAD)],
        out_specs=[wide, _rows(ts, HEAD_PAD)],
        out_shape=[jax.ShapeDtypeStruct((S, MLA_HEADS * HEAD_PAD), BF16), jax.ShapeDtypeStruct((S, HEAD_PAD), BF16)],
        compiler_params=_cp(1), name=name,
    )(dq_rot, dk_cat, ctab, stab)


ATT_SCALE = QK_DIM ** -0.5
LOG2E = math.log2(math.e)


HEADS_PER_STEP = 2
ATT_COL0 = POOL_DIM // HEAD_PAD


FWD_TILE = 1024


def _stat_rows(col):
    return jnp.broadcast_to(col, (col.shape[0], LANES)).T[0:8]


def _retile_rows(rows, tq):
    heads, n8, t = rows.shape
    if t == tq:
        return rows
    flat = rows.reshape(heads, n8 // 8, 8, t)[:, :, 0].reshape(heads, -1, 1, tq)
    return jnp.broadcast_to(flat, (heads, flat.shape[1], 8, tq)).reshape(heads, -1, tq)


def _flash_fwd(q, k, v, mix, *, name):
    S = q.shape[0]
    tq = FWD_TILE if S % FWD_TILE == 0 else min(S, 512)
    nq = S // tq
    hs = HEADS_PER_STEP
    wide = hs * HEAD_PAD

    def body(q_ref, k_ref, v_ref, mix_ref, o_ref, lse_ref):
        qi = pl.program_id(1)
        qv = [q_ref[:, a * HEAD_PAD:(a + 1) * HEAD_PAD] for a in range(hs)]

        def update(m, acc, s, v):
            m_new = jnp.maximum(m, jnp.max(s, axis=-1, keepdims=True))
            p = jnp.exp2((s - m_new) * (ATT_SCALE * LOG2E))
            alpha = jnp.exp2((m - m_new) * (ATT_SCALE * LOG2E))
            return m_new, alpha * acc + _dot(p.astype(BF16), v)

        def step(j, carry, masked):
            off = pl.multiple_of(j * tq, tq)
            out = []
            for a in range(hs):
                head = slice(a * HEAD_PAD, (a + 1) * HEAD_PAD)
                s = _dot(qv[a], k_ref[pl.ds(off, tq), head], NT)
                if masked:
                    row = lax.broadcasted_iota(jnp.int32, (tq, tq), 0)
                    col = lax.broadcasted_iota(jnp.int32, (tq, tq), 1)
                    s = jnp.where(col <= row, s, NEG_INF)
                out.append(update(*carry[a], s, v_ref[pl.ds(off, tq), head]))
            return tuple(out)

        one = (jnp.full((tq, 1), NEG_INF, F32), jnp.zeros((tq, HEAD_PAD), F32))
        carry = step(qi, lax.fori_loop(0, qi, lambda j, c: step(j, c, False), (one,) * hs), True)
        for a in range(hs):
            m, acc = carry[a]
            l = acc[:, V_HEAD:V_HEAD + 1]
            o_ref[:, a * HEAD_PAD:(a + 1) * HEAD_PAD] = (acc / l).astype(o_ref.dtype)
            lse_ref[a] = _stat_rows(m * ATT_SCALE + jnp.log(l))

    blk = pl.BlockSpec((tq, wide), lambda h, i: (i, h))
    full = pl.BlockSpec((S, wide), lambda h, i: (0, h))
    return pl.pallas_call(
        body, grid=(MLA_HEADS // hs, nq), in_specs=[blk, full, full, ANY],
        out_specs=[pl.BlockSpec((tq, wide), lambda h, i: (i, ATT_COL0 // hs + h)),
                   pl.BlockSpec((hs, 8, tq), lambda h, i: (h, i, 0))],
        out_shape=[jax.ShapeDtypeStruct(mix.shape, mix.dtype), jax.ShapeDtypeStruct((MLA_HEADS, nq * 8, tq), F32)],
        input_output_aliases={3: 0}, compiler_params=_cp(2), name=name,
    )(q, k, v, mix)


BWD_TILE = 1024
BWD_HEADS_PER_STEP = 1


def _bwd_tile(S):
    return BWD_TILE if S % BWD_TILE == 0 else min(S, 512)


def _attn_delta(dmix, mix, *, name):
    S = mix.shape[0]
    ts = _bwd_tile(S)
    half = MLA_HEADS // 2
    halves = [_rows(ts, half * HEAD_PAD, 1), _rows(ts, half * HEAD_PAD, 2)]

    def body(do0_ref, do1_ref, o0_ref, o1_ref, d_ref):
        for n, (do_ref, o_ref) in enumerate(((do0_ref, o0_ref), (do1_ref, o1_ref))):
            prod = do_ref[...].astype(F32) * o_ref[...].astype(F32)
            for a in range(half):
                d_ref[n * half + a] = _stat_rows(
                    jnp.sum(prod[:, a * HEAD_PAD:(a + 1) * HEAD_PAD], axis=-1, keepdims=True))

    return pl.pallas_call(
        body, grid=(S // ts,), in_specs=halves + halves,
        out_specs=pl.BlockSpec((MLA_HEADS, 8, ts), lambda i: (0, i, 0)),
        out_shape=jax.ShapeDtypeStruct((MLA_HEADS, (S // ts) * 8, ts), F32), compiler_params=_cp(1), name=name,
    )(dmix, dmix, mix, mix)


def _flash_bwd(q, k, v, dmix, lse_rows, delta_rows, *, name):
    S = q.shape[0]
    tq = _bwd_tile(S)
    nq = S // tq
    hs = BWD_HEADS_PER_STEP
    wide = hs * HEAD_PAD

    def body(q_hbm, do_hbm, lse_ref, dl_ref, k_ref, v_ref, dq_hbm, dk_ref, dv_ref, q_all, do_all, dq_all):
        g, j = pl.program_id(0), pl.program_id(1)
        cols = pl.multiple_of(g * wide, wide)

        @pl.when(j == 0)
        def _():
            pltpu.sync_copy(q_hbm.at[:, pl.ds(cols, wide)], q_all)
            pltpu.sync_copy(do_hbm.at[:, pl.ds(POOL_DIM + cols, wide)], do_all)
            dq_all[...] = jnp.zeros_like(dq_all)

        heads = [slice(a * HEAD_PAD, (a + 1) * HEAD_PAD) for a in range(hs)]
        kv = [k_ref[:, a] for a in heads]
        vv = [v_ref[:, a] for a in heads]

        def block(a, keys, rows, lse2, dl, first_query):
            qv, dov = q_all[rows, heads[a]], do_all[rows, heads[a]]
            st = _dot(kv[a][:keys], qv, NT)
            if first_query is not None:
                krow = lax.broadcasted_iota(jnp.int32, st.shape, 0)
                qcol = lax.broadcasted_iota(jnp.int32, st.shape, 1) + first_query
                st = jnp.where(krow <= qcol, st, NEG_INF)
            pt = jnp.exp2(st * (ATT_SCALE * LOG2E) - lse2)
            dst = (pt * (_dot(vv[a][:keys], dov, NT) - dl)).astype(BF16)
            dq_all[rows, heads[a]] += _dot(dst, kv[a][:keys], TN)
            return _dot(dst, qv), _dot(pt.astype(BF16), dov)

        def stats(a, i):
            off8 = pl.multiple_of(i * 8, 8)
            return lse_ref[a, pl.ds(off8, 8), :][0:1] * LOG2E, dl_ref[a, pl.ds(off8, 8), :][0:1]

        def step(i, carry):
            rows = pl.ds(pl.multiple_of(i * tq, tq), tq)
            out = []
            for a in range(hs):
                dk, dv = block(a, tq, rows, *stats(a, i), None)
                out.append((carry[a][0] + dk, carry[a][1] + dv))
            return tuple(out)

        def diagonal():
            half = tq // 2
            out = []
            for a in range(hs):
                lse2, dl = stats(a, j)
                off = pl.multiple_of(j * tq, tq)
                dk0, dv0 = block(a, half, pl.ds(off, half), lse2[:, :half], dl[:, :half], 0)
                dk1, dv1 = block(a, tq, pl.ds(pl.multiple_of(off + half, half), half), lse2[:, half:], dl[:, half:], half)
                zero = jnp.zeros((tq - half, HEAD_PAD), F32)
                out.append((dk1 + jnp.concatenate([dk0, zero], axis=0), dv1 + jnp.concatenate([dv0, zero], axis=0)))
            return tuple(out)

        carry = lax.fori_loop(j + 1, nq, step, diagonal())
        for a in range(hs):
            dk_ref[:, heads[a]] = carry[a][0] * ATT_SCALE
            dv_ref[:, heads[a]] = carry[a][1]

        @pl.when(j == nq - 1)
        def _():
            dq_all[...] = dq_all[...] * ATT_SCALE
            pltpu.sync_copy(dq_all, dq_hbm.at[:, pl.ds(cols, wide)])

    blk = pl.BlockSpec((tq, wide), lambda g, j: (j, g))
    stat = pl.BlockSpec((hs, nq * 8, tq), lambda g, j: (g, 0, 0))
    full = jax.ShapeDtypeStruct((S, MLA_HEADS * HEAD_PAD), F32)
    return pl.pallas_call(
        body, grid=(MLA_HEADS // hs, nq), in_specs=[ANY, ANY, stat, stat, blk, blk], out_specs=[ANY, blk, blk],
        out_shape=[full, full, full],
        scratch_shapes=[pltpu.VMEM((S, wide), BF16), pltpu.VMEM((S, wide), BF16), pltpu.VMEM((S, wide), F32)],
        compiler_params=_cp(2), name=name,
    )(q, dmix, lse_rows, delta_rows, k, v)


MEM_SCALE = MEM_HEAD_DIM ** -0.5


def _xattn_probs(qh, kh):
    s = _dot(qh, kh, NT) * MEM_SCALE
    e = jnp.exp(s - jnp.max(s, axis=-1, keepdims=True))
    return e / jnp.sum(e, axis=-1, keepdims=True)


def _xa_block_fwd(x, kvm, w_q, w_o, g, *, name):
    S = x.shape[0]
    ts = min(S, 512)
    nm = kvm.shape[0]

    def body(x_ref, kv_ref, wq_ref, wo_ref, g_ref, xo_ref, hx_ref, q_ref, o_ref):
        xv = x_ref[...]
        r = lax.rsqrt(jnp.mean(xv * xv, axis=-1, keepdims=True) + RMS_EPS)
        hx = (xv * r * g_ref[...]).astype(BF16)
        hx_ref[...] = hx
        q = _dot(hx, wq_ref[...]).astype(BF16)
        q_ref[...] = q
        for h in range(MEM_HEADS):
            lo, hi = h * MEM_HEAD_DIM, (h + 1) * MEM_HEAD_DIM
            p = _xattn_probs(q[:, lo:hi], kv_ref[:, lo:hi])
            o_ref[:, lo:hi] = _dot(p.astype(BF16), kv_ref[:, D_MODEL + lo:D_MODEL + hi]).astype(o_ref.dtype)
        xo_ref[...] = xv + _dot(o_ref[...], wo_ref[...])

    square = _const((D_MODEL, D_MODEL))
    act = jax.ShapeDtypeStruct((S, D_MODEL), BF16)
    return pl.pallas_call(
        body, grid=(S // ts,),
        in_specs=[_rows(ts, D_MODEL), _const((nm, 2 * D_MODEL)), square, square, _const((1, D_MODEL))],
        out_specs=[_rows(ts, D_MODEL)] * 4, out_shape=[jax.ShapeDtypeStruct((S, D_MODEL), F32), act, act, act],
        compiler_params=_cp(1), name=name,
    )(x, kvm, w_q, w_o, g.reshape(1, D_MODEL))


def _xa_block_bwd(dxo, x, q, kvm, w_q, w_o, g, *, name):
    S = q.shape[0]
    ts = min(S, 512)
    nm = kvm.shape[0]

    def body(dxo_ref, x_ref, q_ref, kv_ref, wq_ref, wo_ref, g_ref, dx_ref, dq_ref, dkv_ref, dg_ref):
        @pl.when(pl.program_id(0) == 0)
        def _():
            dkv_ref[...] = jnp.zeros_like(dkv_ref)
            dg_ref[...] = jnp.zeros_like(dg_ref)

        dxo = dxo_ref[...]
        do = _dot(dxo.astype(BF16), wo_ref[...], NT).astype(BF16)
        for h in range(MEM_HEADS):
            lo, hi = h * MEM_HEAD_DIM, (h + 1) * MEM_HEAD_DIM
            qh, kh, vh = q_ref[:, lo:hi], kv_ref[:, lo:hi], kv_ref[:, D_MODEL + lo:D_MODEL + hi]
            doh = do[:, lo:hi]
            p = _xattn_probs(qh, kh)
            dp = _dot(doh, vh, NT)
            ds = (p * (dp - jnp.sum(dp * p, axis=-1, keepdims=True)) * MEM_SCALE).astype(BF16)
            dq_ref[:, lo:hi] = _dot(ds, kh).astype(dq_ref.dtype)
            dkv_ref[:, lo:hi] += _dot(ds, qh, TN)
            dkv_ref[:, D_MODEL + lo:D_MODEL + hi] += _dot(p.astype(BF16), doh, TN)
        dx, dg = _norm_bwd_epilogue(0)([_dot(dq_ref[...], wq_ref[...], NT)], [x_ref[...], dxo, g_ref[...]])
        dx_ref[...] = dx
        dg_ref[...] += dg

    square = _const((D_MODEL, D_MODEL))
    return pl.pallas_call(
        body, grid=(S // ts,),
        in_specs=[_rows(ts, D_MODEL), _rows(ts, D_MODEL), _rows(ts, D_MODEL), _const((nm, 2 * D_MODEL)), square,
                  square, _const((1, D_MODEL))],
        out_specs=[_rows(ts, D_MODEL), _rows(ts, D_MODEL), _const((nm, 2 * D_MODEL)), _const((1, D_MODEL))],
        out_shape=[jax.ShapeDtypeStruct((S, D_MODEL), F32), jax.ShapeDtypeStruct((S, D_MODEL), BF16),
                   jax.ShapeDtypeStruct((nm, 2 * D_MODEL), F32), jax.ShapeDtypeStruct((1, D_MODEL), F32)],
        compiler_params=_cp(1), name=name,
    )(dxo, x, q, kvm, w_q, w_o, g.reshape(1, D_MODEL))


CONV_HALO = 8


def _sigmoid(x):
    return 0.5 * jnp.tanh(0.5 * x) + 0.5


def _softplus(x):
    return jnp.maximum(x, 0.0) + jnp.log(1.0 + jnp.exp(-jnp.abs(x)))


def _neg_expm1(x):
    series = -x * (1.0 + x * (1.0 / 2) * (1.0 + x * (1.0 / 3) * (1.0 + x * (1.0 / 4) * (1.0 + x * (1.0 / 5)))))
    return jnp.where(x > -0.05, series, 1.0 - jnp.exp(x))


GELU_C = math.sqrt(2.0 / math.pi)


def _gelu(x):
    return 0.5 * x * (1.0 + jnp.tanh(GELU_C * (x + 0.044715 * x * x * x)))


def _gelu_grad(x):
    t = jnp.tanh(GELU_C * (x + 0.044715 * x * x * x))
    return 0.5 * (1.0 + t) + 0.5 * x * (1.0 - t * t) * GELU_C * (1.0 + 3 * 0.044715 * x * x)


def _lru_gates(xc, wr_ref, br, wi_ref, bi, sp, reset):
    xcb = xc.astype(BF16)
    pr, pi = [], []
    for h in range(LRU_HEADS):
        lo, hi = h * LRU_HEAD_DIM, (h + 1) * LRU_HEAD_DIM
        pr.append(_dot(xcb[:, lo:hi], wr_ref[h]))
        pi.append(_dot(xcb[:, lo:hi], wi_ref[h]))
    r = _sigmoid(jnp.concatenate(pr, axis=1) + br)
    ig = _sigmoid(jnp.concatenate(pi, axis=1) + bi)
    log_a = -LRU_C * r * sp
    a = jnp.where(reset, 0.0, jnp.exp(log_a))
    mult = jnp.where(reset, 1.0, jnp.sqrt(jnp.maximum(_neg_expm1(2.0 * log_a), 0.0)))
    return r, ig, a, mult


SUBLANES = 8


def _compose_groups(a, b, reverse):
    n = a.shape[0]
    row = lax.broadcasted_iota(jnp.int32, a.shape, 0) % SUBLANES
    for s in (1, 2, 4):
        inside = (row < SUBLANES - s) if reverse else (row >= s)
        shift = n - s if reverse else s
        a_s = jnp.where(inside, pltpu.roll(a, shift, 0), 1.0)
        b_s = jnp.where(inside, pltpu.roll(b, shift, 0), 0.0)
        b = a * b_s + b
        a = a * a_s
    return a, b


def _chain_groups(a_buf, h_ref, state, reverse):
    groups = a_buf.shape[0] // SUBLANES

    def group(g, h_in):
        off = pl.multiple_of((groups - 1 - g if reverse else g) * SUBLANES, SUBLANES)
        h = a_buf[pl.ds(off, SUBLANES), :] * h_in + h_ref[pl.ds(off, SUBLANES), :]
        h_ref[pl.ds(off, SUBLANES), :] = h
        return jnp.broadcast_to(h[0:1] if reverse else h[SUBLANES - 1:SUBLANES], h.shape)

    return lax.fori_loop(0, groups, group, state, unroll=4)[0:1]


def _lru_fwd(z, reset, conv_w, conv_b, w_r, b_r, w_i, b_i, lam, *, name):
    S = z.shape[0]
    ts = min(S, 512)
    nh = ts // CONV_HALO
    W = D_MODEL

    def body(gate_ref, xb_ref, halo_ref, rs_ref, cw_ref, cb_ref, wr_ref, br_ref, wi_ref, bi_ref, lam_ref,
             xc_ref, h_ref, y_ref, a_buf, carry):
        i = pl.program_id(0)

        @pl.when(i == 0)
        def _():
            carry[...] = jnp.zeros_like(carry)

        halo = jnp.where(i > 0, halo_ref[...], 0.0)
        xe = jnp.concatenate([halo, xb_ref[...]], axis=0)
        xc = cb_ref[...] + cw_ref[3:4, :] * xe[CONV_HALO:]
        for kk in range(CONV_WIDTH - 1):
            xc = xc + cw_ref[kk:kk + 1, :] * pltpu.roll(xe, CONV_WIDTH - 1 - kk, 0)[CONV_HALO:]
        xc_ref[...] = xc
        reset = rs_ref[...] > 0.5
        _, ig, a, mult = _lru_gates(xc, wr_ref, br_ref[...], wi_ref, bi_ref[...], _softplus(-lam_ref[...]), reset)
        a_buf[...], h_ref[...] = _compose_groups(a, mult * (ig * xc), False)
        carry[...] = _chain_groups(a_buf, h_ref, jnp.broadcast_to(carry[...], (SUBLANES, W)), False)
        y_ref[...] = (_gelu(gate_ref[...]) * h_ref[...]).astype(y_ref.dtype)

    vec = _const((1, W))
    gw = _const((LRU_HEADS, LRU_HEAD_DIM, LRU_HEAD_DIM))
    return pl.pallas_call(
        body, grid=(S // ts,),
        in_specs=[_rows(ts, W, 0), _rows(ts, W, 1),
                  pl.BlockSpec((CONV_HALO, W), lambda i: (jnp.maximum(i * nh - 1, 0), 1)),
                  _rows(ts, 1), _const((CONV_WIDTH, W)), vec, gw, vec, gw, vec, vec],
        out_specs=[_rows(ts, W)] * 3,
        out_shape=[jax.ShapeDtypeStruct((S, W), F32), jax.ShapeDtypeStruct((S, W), F32),
                   jax.ShapeDtypeStruct((S, W), BF16)],
        scratch_shapes=[pltpu.VMEM((ts, W), F32), pltpu.VMEM((1, W), F32)],
        compiler_params=_cp(1), name=name,
    )(z, z, z, reset, conv_w, conv_b, w_r, b_r, w_i, b_i, lam)


def _lru_bwd(dy, z, xc, hseq, reset, w_r, b_r, w_i, b_i, lam, *, name):
    S = z.shape[0]
    ts = min(S, 512)
    nt = S // ts
    nh = ts // CONV_HALO
    W = D_MODEL

    def body(dy_ref, gate_ref, xc_ref, h_ref, hh_ref, rs_ref, wr_ref, br_ref, wi_ref, bi_ref, lam_ref,
             dg_ref, dxc_ref, dpr_ref, dpi_ref, acc_ref, a_buf, dh_buf, carry):
        i = pl.program_id(0)
        tile = nt - 1 - i

        @pl.when(i == 0)
        def _():
            carry[...] = jnp.zeros_like(carry)
            acc_ref[...] = jnp.zeros_like(acc_ref)

        xc = xc_ref[...]
        lam_v = lam_ref[...]
        sp = _softplus(-lam_v)
        reset = rs_ref[...] > 0.5
        r, ig, a, mult = _lru_gates(xc, wr_ref, br_ref[...], wi_ref, bi_ref[...], sp, reset)
        gate = gate_ref[...]
        dyv = dy_ref[...].astype(F32)
        h = h_ref[...]
        dg_ref[...] = (dyv * h * _gelu_grad(gate)).astype(dg_ref.dtype)
        last_row = lax.broadcasted_iota(jnp.int32, a.shape, 0) == ts - 1
        a_buf[...], dh_buf[...] = _compose_groups(jnp.where(last_row, 1.0, pltpu.roll(a, ts - 1, 0)),
                                                  dyv * _gelu(gate), True)
        _chain_groups(a_buf, dh_buf, jnp.broadcast_to(carry[...], (SUBLANES, W)), True)
        dh = dh_buf[...]
        carry[...] = a[0:1] * dh[0:1]
        hh = jnp.where(tile > 0, hh_ref[...], 0.0)
        h_prev = pltpu.roll(jnp.concatenate([hh, h], axis=0), 1, 0)[CONV_HALO:]
        da = dh * h_prev
        bx = ig * xc
        dmult = dh * bx
        dbx = dh * mult
        di = dbx * xc
        dlog_a = jnp.where(reset, 0.0, da * a - dmult * a * a / jnp.maximum(mult, 1e-30))
        dr = dlog_a * (-LRU_C) * sp
        dpre_r = dr * r * (1.0 - r)
        dpre_i = di * ig * (1.0 - ig)
        dprb, dpib = dpre_r.astype(BF16), dpre_i.astype(BF16)
        dpr_ref[...] = dprb
        dpi_ref[...] = dpib
        back = []
        for hd in range(LRU_HEADS):
            lo, hi = hd * LRU_HEAD_DIM, (hd + 1) * LRU_HEAD_DIM
            back.append(_dot(dprb[:, lo:hi], wr_ref[hd], NT) + _dot(dpib[:, lo:hi], wi_ref[hd], NT))
        dxc_ref[...] = dbx * ig + jnp.concatenate(back, axis=1)
        dlam = jnp.sum(dlog_a * (-LRU_C) * r, axis=0, keepdims=True) * (-_sigmoid(-lam_v))
        acc_ref[0:1, :] += jnp.sum(dpre_r, axis=0, keepdims=True)
        acc_ref[1:2, :] += jnp.sum(dpre_i, axis=0, keepdims=True)
        acc_ref[2:3, :] += dlam

    rev = lambda cb: pl.BlockSpec((ts, W), lambda i: (nt - 1 - i, cb))
    vec = _const((1, W))
    gw = _const((LRU_HEADS, LRU_HEAD_DIM, LRU_HEAD_DIM))
    return pl.pallas_call(
        body, grid=(nt,),
        in_specs=[rev(0), rev(0), rev(0), rev(0),
                  pl.BlockSpec((CONV_HALO, W), lambda i: (jnp.maximum((nt - 1 - i) * nh - 1, 0), 0)),
                  pl.BlockSpec((ts, 1), lambda i: (nt - 1 - i, 0)), gw, vec, gw, vec, vec],
        out_specs=[rev(0), rev(0), rev(0), rev(0), _const((8, W))],
        out_shape=[jax.ShapeDtypeStruct((S, W), BF16), jax.ShapeDtypeStruct((S, W), F32),
                   jax.ShapeDtypeStruct((S, W), BF16), jax.ShapeDtypeStruct((S, W), BF16),
                   jax.ShapeDtypeStruct((8, W), F32)],
        scratch_shapes=[pltpu.VMEM((ts, W), F32), pltpu.VMEM((ts, W), F32), pltpu.VMEM((1, W), F32)],
        compiler_params=_cp(1), name=name,
    )(dy, z, xc, hseq, hseq, reset, w_r, b_r, w_i, b_i, lam)


def _conv_bwd(dxc, z, conv_w, *, name):
    S = dxc.shape[0]
    ts = min(S, 512)
    nh = ts // CONV_HALO
    last = S // CONV_HALO - 1
    W = D_MODEL
    n = ts + CONV_HALO

    def body(d_ref, dn_ref, xb_ref, xp_ref, cw_ref, dxb_ref, acc_ref):
        i = pl.program_id(0)

        @pl.when(i == 0)
        def _():
            acc_ref[...] = jnp.zeros_like(acc_ref)

        d = d_ref[...]
        de = jnp.concatenate([d, jnp.where(i < pl.num_programs(0) - 1, dn_ref[...], 0.0)], axis=0)
        xe = jnp.concatenate([jnp.where(i > 0, xp_ref[...], 0.0), xb_ref[...]], axis=0)
        dxb = cw_ref[3:4, :] * d
        acc_ref[3:4, :] += jnp.sum(d * xe[CONV_HALO:], axis=0, keepdims=True)
        for kk in range(CONV_WIDTH - 1):
            sh = CONV_WIDTH - 1 - kk
            dxb = dxb + cw_ref[kk:kk + 1, :] * pltpu.roll(de, n - sh, 0)[:ts]
            acc_ref[kk:kk + 1, :] += jnp.sum(d * pltpu.roll(xe, sh, 0)[CONV_HALO:], axis=0, keepdims=True)
        dxb_ref[...] = dxb.astype(dxb_ref.dtype)
        acc_ref[4:5, :] += jnp.sum(d, axis=0, keepdims=True)

    return pl.pallas_call(
        body, grid=(S // ts,),
        in_specs=[_rows(ts, W), pl.BlockSpec((CONV_HALO, W), lambda i: (jnp.minimum((i + 1) * nh, last), 0)),
                  _rows(ts, W, 1), pl.BlockSpec((CONV_HALO, W), lambda i: (jnp.maximum(i * nh - 1, 0), 1)),
                  _const((CONV_WIDTH, W))],
        out_specs=[_rows(ts, W), _const((8, W))],
        out_shape=[jax.ShapeDtypeStruct((S, W), BF16), jax.ShapeDtypeStruct((8, W), F32)],
        compiler_params=_cp(1), name=name,
    )(dxc, dxc, z, z, conv_w)


def _loss_head(x, g, target, *, name):
    S, D = x.shape
    ts = _row_tile(S)

    def body(x_ref, g_ref, t_ref, dx_ref, dg_ref, l_ref):
        @pl.when(pl.program_id(0) == 0)
        def _():
            dg_ref[...] = jnp.zeros_like(dg_ref)
            l_ref[...] = jnp.zeros_like(l_ref)

        xv = x_ref[...]
        r = lax.rsqrt(jnp.mean(xv * xv, axis=-1, keepdims=True) + RMS_EPS)
        n = xv * r
        err = n * g_ref[...] - t_ref[...]
        l_ref[...] += 0.5 * jnp.sum(jnp.sum(err * err, axis=-1, keepdims=True) * (1.0 / D), axis=0, keepdims=True)
        dy = err * (1.0 / D)
        dn = dy * g_ref[...]
        dx_ref[...] = r * (dn - n * jnp.mean(dn * n, axis=-1, keepdims=True))
        dg_ref[...] += jnp.sum(dy * n, axis=0, keepdims=True)

    return pl.pallas_call(
        body, grid=(S // ts,), in_specs=[_rows(ts, D), _const((1, D)), _rows(ts, D)],
        out_specs=[_rows(ts, D), _const((1, D)), _const((8, LANES))],
        out_shape=[jax.ShapeDtypeStruct((S, D), F32), jax.ShapeDtypeStruct((1, D), F32),
                   jax.ShapeDtypeStruct((8, LANES), F32)],
        compiler_params=_cp(1), name=name,
    )(x, g.reshape(1, D), target)


def _adamw(w, ga, gb, m, v, *, name):
    shape = w.shape
    cols = shape[-1]
    rows = w.size // cols
    br = rows
    if rows * cols * 4 > (1 << 20):
        br = max(d for d in range(8, rows + 1, 8) if rows % d == 0 and d * cols * 4 <= (1 << 20))

    def body(w_ref, ga_ref, gb_ref, m_ref, v_ref, g_ref, d_ref, mo_ref, vo_ref):
        gv = ga_ref[...] + gb_ref[...]
        g_ref[...] = gv
        mn = ADAM_B1 * m_ref[...] + (1.0 - ADAM_B1) * gv
        vn = ADAM_B2 * v_ref[...] + (1.0 - ADAM_B2) * (gv * gv)
        m_hat = mn / (1.0 - ADAM_B1 ** ADAM_STEP)
        v_hat = vn / (1.0 - ADAM_B2 ** ADAM_STEP)
        d_ref[...] = -ADAM_LR * (m_hat / (jnp.sqrt(v_hat) + ADAM_EPS) + ADAM_WD * w_ref[...])
        mo_ref[...] = mn
        vo_ref[...] = vn

    spec = _rows(br, cols)
    outs = pl.pallas_call(
        body, grid=(rows // br,), in_specs=[spec] * 5, out_specs=[spec] * 4,
        out_shape=[jax.ShapeDtypeStruct((rows, cols), F32)] * 4, compiler_params=_cp(1), name=name,
    )(*[t.reshape(rows, cols) for t in (w, ga, gb, m, v)])
    return [o.reshape(shape) for o in outs]


def _pad_heads(w, width):
    k = w.shape[0]
    return jnp.pad(w.reshape(k, MLA_HEADS, width), ((0, 0), (0, 0), (0, HEAD_PAD - width))).reshape(k, -1)


def _unpad_heads(w, width):
    k = w.shape[0]
    return w.reshape(k, MLA_HEADS, HEAD_PAD)[:, :, :width].reshape(k, MLA_HEADS * width)


def _rope_tables(positions):
    inv_freq = ROPE_BASE ** (-jnp.arange(0, QK_ROPE, 2, dtype=F32) / QK_ROPE)
    ang = positions.astype(F32)[:, None] * inv_freq
    cos, sin = jnp.cos(ang), jnp.sin(ang)
    S = positions.shape[0]
    ones, zeros = jnp.ones((S, QK_NOPE), F32), jnp.zeros((S, QK_NOPE), F32)
    ctab = jnp.concatenate([ones, cos, cos, ones[:, :HEAD_PAD - QK_DIM]], axis=1)
    stab = jnp.concatenate([zeros, -sin, sin, zeros[:, :HEAD_PAD - QK_DIM]], axis=1)
    return ctab, stab


def _memory_block(x, mem, W, layer, tag):
    mn = _rms(mem, W["xa_norm_mem"][layer], name=f"{tag}_xa_norm_mem")
    kvm = _mm(mn, [(W["xa_w_kv"][layer], 0, 0)], _first, [(2 * D_MODEL, BF16, 0)], tn=2 * D_MODEL, nj=1,
              name=f"{tag}_xa_kv")[0]
    xo, hx, qx, o = _xa_block_fwd(x, kvm, W["xa_w_q"][layer], W["xa_w_o"][layer], W["xa_norm_x"][layer],
                                  name=f"{tag}_xa_fwd")
    return xo, (x, hx, qx, mn, kvm, o)


def _memory_block_bwd(dxo, mem, W, layer, saved, tag, grads):
    x, hx, qx, mn, kvm, o = saved
    wq, wkv, wo = W["xa_w_q"][layer], W["xa_w_kv"][layer], W["xa_w_o"][layer]
    grads["xa_w_o"][layer] = _owner_major(_mm_tn(o, dxo, name=f"{tag}_xa_dwo"), 0)
    dx, dqx, dkvm, dg = _xa_block_bwd(dxo, x, qx, kvm, wq, wo, W["xa_norm_x"][layer], name=f"{tag}_xa_bwd")
    grads["xa_w_q"][layer] = _owner_major(_mm_tn(hx, dqx, name=f"{tag}_xa_dwq"), 0)
    grads["xa_norm_x"][layer] = dg[0]
    dmn = _mm(dkvm, [(wkv, 0, 0)], _first, [(D_MODEL, F32, 0)], nt=True, tn=D_MODEL, nj=1, name=f"{tag}_xa_dmn")[0]
    grads["xa_w_kv"][layer] = _mm_tn_owners(mn, [dkvm], name=f"{tag}_xa_dwkv")
    _, dgm = _rms_bwd(mem, W["xa_norm_mem"][layer], dmn, name=f"{tag}_xa_norm_mem_bwd")
    grads["xa_norm_mem"][layer] = dgm[0]
    return dx


FF_TN = D_FF // 2

def _silu_mul(accs, extras):
    g, u = accs
    return [g * _sigmoid(g) * u, g, u]


def _silu_mul_bwd(accs, extras):
    da = accs[0]
    g, u = extras[0].astype(F32), extras[1].astype(F32)
    sg = _sigmoid(g)
    return [da * u * sg * (1.0 + g * (1.0 - sg)), da * g * sg]


def _ffn_block(x, W, layer, tag):
    hf = _rms(x, W["ffn_norm"][layer], name=f"{tag}_ffn_norm")
    wgu, wd = W["ffn_w_gate_up"][layer], W["ffn_w_down"][layer]
    act, g, u = _mm(hf, [(wgu, 0, 0), (wgu, 0, 2)], _silu_mul, [(D_FF, BF16, 0)] * 3, tn=FF_TN, nj=2,
                    name=f"{tag}_ffn_up")
    xo = _mm(act, [(wd, 0, 0)], _add_res, [(D_MODEL, F32, 0)], extras=[(x, 0)], tn=D_MODEL, nj=1,
             name=f"{tag}_ffn_down")[0]
    return xo, (x, hf, act, g, u)


def _ffn_block_bwd(dxo, W, layer, saved, tag, grads):
    x, hf, act, g, u = saved
    wgu, wd = W["ffn_w_gate_up"][layer], W["ffn_w_down"][layer]
    dg, du = _mm(dxo, [(wd, 0, 0)], _silu_mul_bwd, [(D_FF, BF16, 0)] * 2, nt=True, extras=[(g, 0), (u, 0)], tn=FF_TN,
                 nj=2, name=f"{tag}_ffn_dact")
    grads["ffn_w_down"][layer] = _owner_major(_mm_tn(act, dxo, tk=FF_TN, name=f"{tag}_ffn_dwd"), 0)
    dx, dgn = _mm(dg, [(wgu, 0, 0)], _norm_bwd_epilogue(0), [(D_MODEL, F32, 0)], nt=True, also=(du, (wgu, 0, 1)),
                  extras=[(x, 0), (dxo, 0)], rows=[W["ffn_norm"][layer].reshape(1, D_MODEL)],
                  sums=[D_MODEL], tn=D_MODEL, nj=1, name=f"{tag}_ffn_dhf")
    grads["ffn_w_gate_up"][layer] = _mm_tn_owners(hf, [dg, du], name=f"{tag}_ffn_dwgu")
    grads["ffn_norm"][layer] = dgn[0]
    return dx


def _even_block(x, tabs, W, tag):
    ctab, stab = tabs
    w_in = W["ev_w_in"][0]
    zero = jnp.zeros((D_MODEL, QK_NOPE), BF16)
    w_in_pad = jnp.concatenate([w_in[:, :896], zero, w_in[:, 896:], zero[:, :HEAD_PAD - QK_DIM]], axis=1)
    w_q_pad = _pad_heads(W["ev_w_q_up"][0], QK_DIM)
    wkv = W["ev_w_kv_up"][0].reshape(KV_RANK, MLA_HEADS, QK_NOPE + V_HEAD)
    w_kv_pad = jnp.concatenate([_pad_heads(wkv[:, :, :QK_NOPE].reshape(KV_RANK, -1), QK_NOPE),
                                _pad_heads(wkv[:, :, QK_NOPE:].reshape(KV_RANK, -1), V_HEAD)], axis=1)
    w_out = W["ev_w_out"][0]
    w_att = jnp.pad(w_out[POOL_DIM:].reshape(MLA_HEADS, V_HEAD, D_MODEL), ((0, 0), (0, HEAD_PAD - V_HEAD), (0, 0)))
    w_out_pad = jnp.concatenate([w_out[:POOL_DIM], w_att.reshape(MLA_HEADS * HEAD_PAD, D_MODEL)], axis=0)
    pool_w = W["ev_pool_w"][0].astype(BF16)
    pool_scale = W["ev_pool_scale"]

    h, z, mix, pooled, cqn, ckvn, q_rot, k_cat, v_pad = _even_front(
        x, W["ev_norm"][0], w_in_pad, pool_w, pool_scale, W["ev_q_norm"][0], w_q_pad, W["ev_kv_norm"][0], w_kv_pad,
        ctab, stab, name=f"{tag}_front")
    mix, lse = _flash_fwd(q_rot, k_cat, v_pad, mix, name=f"{tag}_attn")
    xo = _mm(mix, [(w_out_pad, 0, 0)], _add_res, [(D_MODEL, F32, 0)], extras=[(x, 0)], tn=D_MODEL, nj=1,
             name=f"{tag}_out")[0]
    saved = (x, h, z, pooled, cqn, ckvn, q_rot, k_cat, v_pad, lse, mix,
             (w_in_pad, w_q_pad, w_kv_pad, w_out_pad, pool_w, pool_scale))
    return xo, saved


def _even_out_grad(dxo, saved, tag):
    mix = saved[10]
    dw_out_pad = _mm_tn(mix, dxo, tk=MIX_DIM // 3, name=f"{tag}_dw_out")
    datt = dw_out_pad[POOL_DIM:].reshape(MLA_HEADS, HEAD_PAD, D_MODEL)[:, :V_HEAD].reshape(-1, D_MODEL)
    return [_owner_major(jnp.concatenate([dw_out_pad[:POOL_DIM], datt], axis=0), 0)]


def _even_block_bwd(dxo, tabs, W, saved, tag, grads, token=None):
    ctab, stab = tabs
    x, h, z, pooled, cqn, ckvn, q_rot, k_cat, v_pad, lse, mix, wts = saved
    w_in_pad, w_q_pad, w_kv_pad, w_out_pad, pool_w, pool_scale = wts
    if token is not None:
        w_out_pad = w_out_pad + token[0:1, 0:1].astype(BF16)
    dmix = _mm(dxo, [(w_out_pad, 0, 0)], _first, [(MIX_DIM, BF16, 0)], nt=True, tn=MIX_DIM, nj=1,
               name=f"{tag}_dmix")[0]
    delta = _attn_delta(dmix, mix, name=f"{tag}_delta")
    dq_rot, dk_cat, dv_pad = _flash_bwd(q_rot, k_cat, v_pad, dmix, _retile_rows(lse, delta.shape[2]), delta,
                                        name=f"{tag}_attn_bwd")
    dq_pad, dkr = _rope_bwd(dq_rot, dk_cat, ctab, stab, name=f"{tag}_rope_bwd")
    dw_q_pad = _mm_tn(cqn, dq_pad, name=f"{tag}_dw_q_up")
    grads["ev_w_q_up"] = [_owner_major(_unpad_heads(dw_q_pad, QK_DIM), 1)]
    dcqn = _mm(dq_pad, [(w_q_pad, 0, 0)], _first, [(Q_RANK, F32, 0)], nt=True, tn=Q_RANK, nj=1, name=f"{tag}_dcqn")[0]
    dwk = _unpad_heads(_mm_tn(ckvn, dk_cat, name=f"{tag}_dw_k_up"), QK_NOPE).reshape(KV_RANK, MLA_HEADS, QK_NOPE)
    dwv = _unpad_heads(_mm_tn(ckvn, dv_pad, name=f"{tag}_dw_v_up"), V_HEAD).reshape(KV_RANK, MLA_HEADS, V_HEAD)
    grads["ev_w_kv_up"] = [_owner_major(jnp.concatenate([dwk, dwv], axis=2).reshape(KV_RANK, -1), 1)]
    dckvn = _mm(dk_cat, [(w_kv_pad, 0, 0)], _first, [(KV_RANK, F32, 0)], nt=True, tn=KV_RANK, nj=1,
                name=f"{tag}_dckvn_k")[0]
    dckvn = _mm(dv_pad, [(w_kv_pad, 0, 1)], _add_res, [(KV_RANK, F32, 0)], nt=True, extras=[(dckvn, 0)], tn=KV_RANK,
                nj=1, name=f"{tag}_dckvn_v")[0]
    dcq, dgq = _rms_bwd(z, W["ev_q_norm"][0], dcqn, cb=2, w=Q_RANK, out_dtype=BF16, name=f"{tag}_q_norm_bwd")
    dckv, dgkv = _rms_bwd(z, W["ev_kv_norm"][0], dckvn, cb=6, w=KV_RANK, out_dtype=BF16, name=f"{tag}_kv_norm_bwd")
    grads["ev_q_norm"], grads["ev_kv_norm"] = dgq, dgkv
    du, dypre, dscale = _pool_bwd(dmix, pooled, pool_w, pool_scale, name=f"{tag}_pool_bwd")
    grads["ev_pool_scale"] = dscale
    grads["ev_pool_w"] = _mm_tn_grouped(pooled, dypre, 4, POOL_GROUP, name=f"{tag}_dpool_w")[None]
    dz = jnp.concatenate([du, dcq, dckv, dkr], axis=1)
    dw_in_pad = _mm_tn(h, dz, name=f"{tag}_dw_in")
    grads["ev_w_in"] = [_owner_major(jnp.concatenate([dw_in_pad[:, :896], dw_in_pad[:, 960:992]], axis=1), 0)]
    dx, dgn = _mm(dz, [(w_in_pad, 0, 0)], _norm_bwd_epilogue(0), [(D_MODEL, F32, 0)], nt=True,
                  extras=[(x, 0), (dxo, 0)], rows=[W["ev_norm"][0].reshape(1, D_MODEL)], sums=[D_MODEL], tn=D_MODEL,
                  nj=1, name=f"{tag}_dh")
    grads["ev_norm"] = dgn
    return dx


def _odd_block(x, reset, W, tag):
    h = _rms(x, W["od_norm"][0], name=f"{tag}_norm")
    z = _mm(h, [(W["od_w_in"][0], 0, 0)], _first, [(2 * D_MODEL, F32, 0)], tn=D_MODEL, nj=2, name=f"{tag}_in")[0]
    w_r, w_i = W["od_w_rgate"][0], W["od_w_igate"][0]
    vecs = [W[n].reshape(1, D_MODEL) for n in ("od_conv_b", "od_b_rgate", "od_b_igate", "od_lambda")]
    xc, hseq, y = _lru_fwd(z, reset, W["od_conv_w"][0], vecs[0], w_r, vecs[1], w_i, vecs[2], vecs[3],
                           name=f"{tag}_lru")
    xo = _mm(y, [(W["od_w_out"][0], 0, 0)], _add_res, [(D_MODEL, F32, 0)], extras=[(x, 0)], tn=D_MODEL, nj=1,
             name=f"{tag}_out")[0]
    return xo, (x, h, z, xc, hseq, y, vecs)


def _odd_block_bwd(dxo, reset, W, saved, tag, grads):
    x, h, z, xc, hseq, y, vecs = saved
    w_r, w_i = W["od_w_rgate"][0], W["od_w_igate"][0]
    dy = _mm(dxo, [(W["od_w_out"][0], 0, 0)], _first, [(D_MODEL, F32, 0)], nt=True, tn=D_MODEL, nj=1,
             name=f"{tag}_dy")[0]
    grads["od_w_out"] = [_owner_major(_mm_tn(y, dxo, name=f"{tag}_dw_out"), 0)]
    dgate, dxc, dpr, dpi, acc = _lru_bwd(dy, z, xc, hseq, reset, w_r, vecs[1], w_i, vecs[2], vecs[3],
                                         name=f"{tag}_lru_bwd")
    grads["od_b_rgate"], grads["od_b_igate"], grads["od_lambda"] = acc[0:1], acc[1:2], acc[2:3]
    grads["od_w_rgate"] = [_owner_major(_mm_tn_grouped(xc, dpr, LRU_HEADS, LRU_HEAD_DIM, name=f"{tag}_dw_rgate"), 1)]
    grads["od_w_igate"] = [_owner_major(_mm_tn_grouped(xc, dpi, LRU_HEADS, LRU_HEAD_DIM, name=f"{tag}_dw_igate"), 1)]
    dxb, cacc = _conv_bwd(dxc, z, W["od_conv_w"][0], name=f"{tag}_conv_bwd")
    grads["od_conv_w"], grads["od_conv_b"] = cacc[None, 0:4], cacc[4:5]
    dz = jnp.concatenate([dgate, dxb], axis=1)
    grads["od_w_in"] = [_mm_tn_owners(h, [dz], name=f"{tag}_dw_in")]
    dx, dgn = _mm(dz, [(W["od_w_in"][0], 0, 0)], _norm_bwd_epilogue(0), [(D_MODEL, F32, 0)], nt=True,
                  extras=[(x, 0), (dxo, 0)], rows=[W["od_norm"][0].reshape(1, D_MODEL)], sums=[D_MODEL], tn=D_MODEL,
                  nj=1, name=f"{tag}_dh")
    grads["od_norm"] = dgn
    return dx


def _local_step(x, mem, positions, target, W, later_weights=None, exchange_earlier=None):
    tabs = _rope_tables(positions)
    reset = (positions == 0).astype(F32)[:, None]
    grads = {n: [None, None] for n in ("xa_norm_x", "xa_norm_mem", "xa_w_q", "xa_w_kv", "xa_w_o", "ffn_norm",
                                       "ffn_w_gate_up", "ffn_w_down")}
    x1, s_even = _even_block(x, tabs, W, "l0_even")
    if later_weights is not None:
        W = {**W, **later_weights(x1)}
    x2, s_xa0 = _memory_block(x1, mem, W, 0, "l0")
    x3, s_ff0 = _ffn_block(x2, W, 0, "l0")
    x4, s_odd = _odd_block(x3, reset, W, "l1_odd")
    x5, s_xa1 = _memory_block(x4, mem, W, 1, "l1")
    x6, s_ff1 = _ffn_block(x5, W, 1, "l1")
    d, dgf, loss = _loss_head(x6, W["final_norm"], target, name="loss_head")
    grads["final_norm"] = dgf[0]
    d = _ffn_block_bwd(d, W, 1, s_ff1, "l1", grads)
    d = _memory_block_bwd(d, mem, W, 1, s_xa1, "l1", grads)
    d = _odd_block_bwd(d, reset, W, s_odd, "l1_odd", grads)
    d = _ffn_block_bwd(d, W, 0, s_ff0, "l0", grads)
    d = _memory_block_bwd(d, mem, W, 0, s_xa0, "l0", grads)
    grads["ev_w_out"] = _even_out_grad(d, s_even, "l0_even")
    token = exchange_earlier(grads) if exchange_earlier is not None else None
    d = _even_block_bwd(d, tabs, W, s_even, "l0_even", grads, token)
    big = {n: grads.pop(n) for n in MATMUL_WEIGHTS}
    for n, v in grads.items():
        if isinstance(v, list):
            grads[n] = jnp.stack(v)
    return loss[0, 0], d, big, grads


WEIGHTS = ("ev_norm", "ev_w_in", "ev_pool_w", "ev_pool_scale", "ev_q_norm", "ev_w_q_up", "ev_kv_norm", "ev_w_kv_up",
           "ev_w_out", "od_norm", "od_w_in", "od_conv_w", "od_conv_b", "od_w_rgate", "od_b_rgate", "od_w_igate",
           "od_b_igate", "od_lambda", "od_w_out", "xa_norm_x", "xa_norm_mem", "xa_w_q", "xa_w_kv", "xa_w_o",
           "ffn_norm", "ffn_w_gate_up", "ffn_w_down", "final_norm")
SHARD_AXIS = {"ev_w_in": 1, "ev_w_q_up": 2, "ev_w_kv_up": 2, "ev_w_out": 1, "od_norm": 1, "od_w_in": 2,
              "od_conv_w": 2, "od_conv_b": 1, "od_w_rgate": 2, "od_b_rgate": 1, "od_w_igate": 2, "od_b_igate": 1,
              "od_lambda": 1, "od_w_out": 1, "xa_w_q": 1, "xa_w_kv": 2, "xa_w_o": 1, "ffn_w_gate_up": 2,
              "ffn_w_down": 1}
MATMUL_WEIGHTS = ("ev_w_in", "ev_w_q_up", "ev_w_kv_up", "ev_w_out", "od_w_in", "od_w_rgate", "od_w_igate",
                  "od_w_out", "xa_w_q", "xa_w_kv", "xa_w_o", "ffn_w_gate_up", "ffn_w_down")
SMALL_SHARDED = tuple(n for n in WEIGHTS if n in SHARD_AXIS and n not in MATMUL_WEIGHTS)
REPLICATED = tuple(n for n in WEIGHTS if n not in SHARD_AXIS)


def _pack(parts, quantum):
    flat = jnp.concatenate([p.reshape(-1) for p in parts])
    pad = (-flat.shape[0]) % quantum
    return jnp.pad(flat, (0, pad)).reshape(-1, LANES)


def _unpack(flat, shapes):
    out, off = [], 0
    for shape in shapes:
        size = math.prod(shape)
        out.append(flat[off:off + size].reshape(shape))
        off += size
    return out


def _run_copies(local, remote, send_sems, recv_sems, local_sems):
    locals_ = [pltpu.make_async_copy(src, dst, local_sems.at[n]) for n, (src, dst) in enumerate(local)]
    for cp in locals_:
        cp.start()
    sends = [pltpu.make_async_remote_copy(src_ref=src, dst_ref=dst, send_sem=send_sems.at[k, n],
                                          recv_sem=recv_sems.at[k, n], device_id=dev, device_id_type=MESH)
             for (k, n, src, dst, _, dev) in remote]
    for cp in sends:
        cp.start()
    for (k, n, src, _, arrival, dev) in remote:
        pltpu.make_async_remote_copy(src_ref=src, dst_ref=arrival, send_sem=send_sems.at[k, n],
                                     recv_sem=recv_sems.at[k, n], device_id=dev, device_id_type=MESH).wait_recv()
    for cp in sends:
        cp.wait_send()
    for cp in locals_:
        cp.wait()


def _chip_peers(x, y):
    return [(1 - x, y), (x, 1 - y), (1 - x, 1 - y)]


def _owner_block(ref, axis, q):
    size = ref.shape[axis] // N_CHIPS
    idx = [slice(None)] * len(ref.shape)
    idx[axis] = pl.ds(q * size, size)
    return ref.at[tuple(idx)]


def _comm_call(body, ins, out_shapes, n_items, n_peers, *, name):
    return pl.pallas_call(
        body, in_specs=[ANY] * len(ins), out_specs=[ANY] * len(out_shapes), out_shape=out_shapes,
        scratch_shapes=[pltpu.SemaphoreType.DMA((n_peers, n_items)), pltpu.SemaphoreType.DMA((n_peers, n_items)),
                        pltpu.SemaphoreType.DMA((n_items,))],
        name=name,
    )(*ins)


def _gather_chips(shards, axes, *, name):
    n = len(shards)
    full = [jax.ShapeDtypeStruct(tuple(d * (N_CHIPS if a == ax else 1) for a, d in enumerate(s.shape)), s.dtype)
            for s, ax in zip(shards, axes)]

    def body(*refs):
        srcs, dsts = refs[:n], refs[n:2 * n]
        x, y, c = lax.axis_index("x"), lax.axis_index("y"), lax.axis_index("c")
        me = 2 * x + y
        local = [(srcs[i], _owner_block(dsts[i], axes[i], me)) for i in range(n)]
        remote = [(k, i, srcs[i], _owner_block(dsts[i], axes[i], me), _owner_block(dsts[i], axes[i], 2 * px + py),
                   (px, py, c))
                  for k, (px, py) in enumerate(_chip_peers(x, y)) for i in range(n)]
        _run_copies(local, remote, *refs[2 * n:])

    return _comm_call(body, shards, full, n, 3, name=name)


HBM = pl.BlockSpec(memory_space=pltpu.HBM)
SEM = pl.BlockSpec(memory_space=pltpu.SEMAPHORE)
DATAFLOW = pltpu.SideEffectType.DATAFLOW_SIDE_EFFECTING


def _gather_plan(axes):
    return lambda srcs, lands, me, peer: [
        (srcs[i], _owner_block(lands[i], ax, me), _owner_block(lands[i], ax, peer)) for i, ax in enumerate(axes)]


def _exchange_plan(where):
    return lambda srcs, lands, me, peer: [
        (srcs[i].at[peer], lands[n].at[me, l], lands[n].at[peer, l]) for i, (n, l) in enumerate(where)]


def _split_peers(sibling):
    x, y, c = lax.axis_index("x"), lax.axis_index("y"), lax.axis_index("c")
    peers = [((px, py, c), 2 * px + py) for px, py in _chip_peers(x, y)]
    return 2 * x + y, peers + ([((x, y, 1 - c), 2 * x + y)] if sibling else [])


def _split_start(srcs, lands, plan, *, sibling=False, name):
    ns, nl = len(srcs), len(lands)
    nsem = (3 + sibling) * len(plan(list(srcs), list(lands), 0, 0))

    def body(*refs):
        src_refs, land_refs = refs[:ns], refs[ns:ns + nl]
        send_sems, recv_sems = refs[ns + nl:ns + nl + nsem], refs[ns + nl + nsem:ns + nl + 2 * nsem]
        me, peers = _split_peers(sibling)
        n = 0
        for device, chip in peers:
            for src, dst, _ in plan(src_refs, land_refs, me, chip):
                pltpu.make_async_remote_copy(src_ref=src, dst_ref=dst, send_sem=send_sems[n], recv_sem=recv_sems[n],
                                             device_id=device, device_id_type=MESH).start()
                n += 1
        refs[-1][...] = jnp.zeros_like(refs[-1])

    arrays = list(srcs) + list(lands)
    out = pl.pallas_call(
        body, name=name, in_specs=[HBM] * (ns + nl),
        out_specs=[SEM] * (2 * nsem) + [HBM] * (ns + nl) + [pl.BlockSpec(memory_space=pltpu.VMEM)],
        out_shape=[pltpu.SemaphoreType.DMA(())] * (2 * nsem) + [pltpu.HBM(a.shape, a.dtype) for a in arrays]
        + [jax.ShapeDtypeStruct((8, LANES), F32)],
        input_output_aliases={i: 2 * nsem + i for i in range(ns + nl)},
        compiler_params=pltpu.CompilerParams(has_side_effects=DATAFLOW),
    )(*[pltpu.with_memory_space_constraint(a, pltpu.HBM) for a in arrays])
    sems, rest = out[:2 * nsem], out[2 * nsem:]
    return sems[:nsem], sems[nsem:], rest[:ns], rest[ns:ns + nl], rest[-1]


def _split_wait(handle, after, plan, *, sibling=False, name):
    send_sems, recv_sems, srcs, lands, _ = handle
    ns, nl, nsem = len(srcs), len(lands), len(send_sems)

    def body(*refs):
        src_refs, land_refs = refs[:ns], refs[ns:ns + nl]
        send_refs, recv_refs = refs[ns + nl:ns + nl + nsem], refs[ns + nl + nsem:ns + nl + 2 * nsem]
        me, peers = _split_peers(sibling)
        n = 0
        for device, chip in peers:
            for src, _, arrival in plan(src_refs, land_refs, me, chip):
                cp = pltpu.make_async_remote_copy(src_ref=src, dst_ref=arrival, send_sem=send_refs[n],
                                                  recv_sem=recv_refs[n], device_id=device, device_id_type=MESH)
                cp.wait_send()
                cp.wait_recv()
                n += 1

    out = pl.pallas_call(
        body, name=name, in_specs=[HBM] * (ns + nl) + [SEM] * (2 * nsem) + [ANY], out_specs=[HBM] * (ns + nl),
        out_shape=[pltpu.HBM(a.shape, a.dtype) for a in list(srcs) + list(lands)],
        input_output_aliases={i: i for i in range(ns + nl)},
        compiler_params=pltpu.CompilerParams(has_side_effects=DATAFLOW),
    )(*srcs, *lands, *send_sems, *recv_sems, after)
    return out[ns:]


def _exchange_sibling(arrays, *, name):
    n = len(arrays)

    def body(*refs):
        x, y, c = lax.axis_index("x"), lax.axis_index("y"), lax.axis_index("c")
        remote = [(0, i, refs[i], refs[n + i], refs[n + i], (x, y, 1 - c)) for i in range(n)]
        _run_copies([], remote, *refs[2 * n:])

    return _comm_call(body, arrays, [jax.ShapeDtypeStruct(a.shape, a.dtype) for a in arrays], n, 1, name=name)


def _sum_slots(r, *, token=None, name):
    shape = r.shape[1:]
    cols = shape[-1]
    rows = math.prod(shape) // cols
    tr = max(d for d in range(8, rows + 1, 8) if rows % d == 0 and d * cols * 16 <= (4 << 20))

    def body(r_ref, *refs):
        total = ((r_ref[0] + r_ref[1]) + r_ref[2]) + r_ref[3]
        refs[-1][...] = total if token is None else total + refs[0][0:1, 0:1]

    in_specs = [pl.BlockSpec((N_CHIPS, tr, cols), lambda i: (0, i, 0))]
    in_specs += [] if token is None else [_const((8, LANES))]
    return pl.pallas_call(
        body, grid=(rows // tr,), in_specs=in_specs,
        out_specs=_rows(tr, cols), out_shape=jax.ShapeDtypeStruct((rows, cols), F32), compiler_params=_cp(1),
        name=name,
    )(r.reshape(N_CHIPS, rows, cols), *([] if token is None else [token])).reshape(shape)


FIRST_WEIGHTS = ("ev_w_in", "ev_w_q_up", "ev_w_kv_up", "ev_w_out")
LATER_WEIGHTS = tuple(n for n in MATMUL_WEIGHTS if n not in FIRST_WEIGHTS)
LAST_GRADS = ("ev_w_in", "ev_w_q_up", "ev_w_kv_up")
EARLIER_GRADS = tuple(n for n in MATMUL_WEIGHTS if n not in LAST_GRADS)


def _my_chip():
    return 2 * lax.axis_index("x") + lax.axis_index("y")


def _gather_first(w):
    small = _pack([w[n] for n in SMALL_SHARDED], 8 * LANES)
    stacked = [n for n in FIRST_WEIGHTS if SHARD_AXIS[n] == w[n].ndim - 1 and w[n].shape[-1] % LANES]
    shards = [w[n].astype(BF16)[None] if n in stacked else w[n].astype(BF16) for n in FIRST_WEIGHTS]
    got = _gather_chips(shards + [small], [0 if n in stacked else SHARD_AXIS[n] for n in FIRST_WEIGHTS] + [0],
                        name="gather_first")
    full = {n: w[n] for n in REPLICATED}
    for n, g in zip(FIRST_WEIGHTS, got[:-1]):
        full[n] = jnp.concatenate([g[q] for q in range(N_CHIPS)], axis=SHARD_AXIS[n]) if n in stacked else g
    per_chip = [_unpack(got[-1][q * small.shape[0]:(q + 1) * small.shape[0]].reshape(-1),
                        [w[n].shape for n in SMALL_SHARDED]) for q in range(N_CHIPS)]
    for i, n in enumerate(SMALL_SHARDED):
        full[n] = jnp.concatenate([per_chip[q][i] for q in range(N_CHIPS)], axis=SHARD_AXIS[n])
    return full


def _gather_later_start(w, after):
    behind = (after.reshape(-1)[0] * 0).astype(BF16)
    shards = [w[n].astype(BF16) + (behind if n == "od_w_rgate" else 0) for n in LATER_WEIGHTS]
    axes = [SHARD_AXIS[n] for n in LATER_WEIGHTS]
    lands = [lax.empty(tuple(d * (N_CHIPS if a == ax else 1) for a, d in enumerate(s.shape)), s.dtype)
             for s, ax in zip(shards, axes)]
    plan = _gather_plan(axes)
    return _split_start(shards, lands, plan, sibling=True, name="gather_later_start"), plan


def _owner_major(g, axis):
    shape = g.shape
    size = shape[axis] // N_CHIPS
    g = jnp.moveaxis(g.reshape(shape[:axis] + (N_CHIPS, size) + shape[axis + 1:]), axis, 0)
    return g.reshape(N_CHIPS, -1, shape[-1] if axis < len(shape) - 1 else size)


def _exchange_start(items, *, cross, name):
    me = _my_chip()
    srcs, lands, where = [], [], []
    for n, layers in enumerate(items):
        land = lax.empty((N_CHIPS, len(layers)) + layers[0].shape[1:], layers[0].dtype)
        for l, a in enumerate(layers):
            if not cross:
                own = lax.dynamic_index_in_dim(a, me, 0, keepdims=True)[:, None]
                land = lax.dynamic_update_slice(land, own, (me, l) + (0,) * (a.ndim - 1))
            srcs.append(a)
            where.append((n, l))
        lands.append(land)
    plan = _exchange_plan(where)
    return _split_start(srcs, lands, plan, sibling=cross, name=name), plan


def _earlier_items(grads, full_shapes):
    small = [_pack([jnp.split(grads[n].reshape(full_shapes[n]), N_CHIPS, axis=SHARD_AXIS[n])[q]
                    for n in SMALL_SHARDED], 8 * LANES) for q in range(N_CHIPS)]
    return [grads[n] for n in EARLIER_GRADS] + [[jnp.stack(small)]]


def _last_items(big, grads, full_shapes, loss):
    repl = _pack([grads[n].reshape(full_shapes[n]) for n in REPLICATED] + [loss.reshape(1)], 8 * LANES)
    return [big[n] for n in LAST_GRADS] + [[jnp.stack([repl] * N_CHIPS)]]


def kernel(
        x, mem, positions, ev_norm, ev_w_in, ev_pool_w, ev_pool_scale, ev_q_norm, ev_w_q_up, ev_kv_norm,
        ev_w_kv_up, ev_w_out, od_norm, od_w_in, od_conv_w, od_conv_b, od_w_rgate, od_b_rgate, od_w_igate,
        od_b_igate, od_lambda, od_w_out, xa_norm_x, xa_norm_mem, xa_w_q, xa_w_kv, xa_w_o, ffn_norm,
        ffn_w_gate_up, ffn_w_down, final_norm, loss_target, m_ev_norm, m_ev_w_in, m_ev_pool_w, m_ev_pool_scale,
        m_ev_q_norm, m_ev_w_q_up, m_ev_kv_norm, m_ev_w_kv_up, m_ev_w_out, m_od_norm, m_od_w_in, m_od_conv_w,
        m_od_conv_b, m_od_w_rgate, m_od_b_rgate, m_od_w_igate, m_od_b_igate, m_od_lambda, m_od_w_out,
        m_xa_norm_x, m_xa_norm_mem, m_xa_w_q, m_xa_w_kv, m_xa_w_o, m_ffn_norm, m_ffn_w_gate_up, m_ffn_w_down,
        m_final_norm, v_ev_norm, v_ev_w_in, v_ev_pool_w, v_ev_pool_scale, v_ev_q_norm, v_ev_w_q_up,
        v_ev_kv_norm, v_ev_w_kv_up, v_ev_w_out, v_od_norm, v_od_w_in, v_od_conv_w, v_od_conv_b, v_od_w_rgate,
        v_od_b_rgate, v_od_w_igate, v_od_b_igate, v_od_lambda, v_od_w_out, v_xa_norm_x, v_xa_norm_mem, v_xa_w_q,
        v_xa_w_kv, v_xa_w_o, v_ffn_norm, v_ffn_w_gate_up, v_ffn_w_down, v_final_norm):
    given = dict(locals())
    w = {n: given[n] for n in WEIGHTS}
    full_shapes = {n: tuple(d * (N_CHIPS if a == SHARD_AXIS.get(n) else 1) for a, d in enumerate(w[n].shape))
                   for n in WEIGHTS}
    full = _gather_first(w)
    later, later_plan = _gather_later_start(w, full["ev_w_out"])
    full["ev_norm"] = full["ev_norm"] + later[4][0:1, 0:1]
    exchange = {}

    def later_weights(after):
        return dict(zip(LATER_WEIGHTS, _split_wait(later, after, later_plan, sibling=True, name="gather_later_wait")))

    def exchange_earlier(grads):
        exchange["handle"], exchange["plan"] = _exchange_start(_earlier_items(grads, full_shapes), cross=True,
                                                               name="exchange_earlier_start")
        return exchange["handle"][4]

    loss, grad_x, big, grads = _local_step(x[0], mem[0], positions[0], loss_target[0], full, later_weights,
                                           exchange_earlier)
    earlier = EARLIER_GRADS + ("small",)
    got = dict(zip(earlier, _split_wait(exchange["handle"], grad_x, exchange["plan"], sibling=True,
                                        name="exchange_earlier_wait")))
    last, last_plan = _exchange_start(_last_items(big, grads, full_shapes, loss), cross=False,
                                      name="exchange_last_start")
    sums = {n: _sum_slots(got[n], token=last[4] if i == 0 else None, name=f"sum_chips_{n}")
            for i, n in enumerate(earlier)}
    got = dict(zip(LAST_GRADS + ("replicated",),
                   _split_wait(last, sums[earlier[-1]], last_plan, name="exchange_last_wait")))
    sums.update({n: _sum_slots(got[n], name=f"sum_chips_{n}") for n in got})
    mine = [sums[n] for n in MATMUL_WEIGHTS + ("small", "replicated")]
    other = _exchange_sibling(mine, name="exchange_sibling")
    out = {}
    for i, n in enumerate(MATMUL_WEIGHTS):
        out[n] = _adamw(w[n], mine[i].reshape(w[n].shape), other[i].reshape(w[n].shape), given["m_" + n],
                        given["v_" + n], name=f"adamw_{n}")
    for i, group in ((len(MATMUL_WEIGHTS), SMALL_SHARDED), (len(MATMUL_WEIGHTS) + 1, REPLICATED)):
        spare = [jnp.zeros((1,), F32)] if group is REPLICATED else []
        packed = [_pack([given[pre + n] for n in group] + spare, 8 * LANES) for pre in ("", "m_", "v_")]
        res = _adamw(packed[0], mine[i].reshape(packed[0].shape), other[i].reshape(packed[0].shape), packed[1],
                     packed[2], name=f"adamw_group{i}")
        shapes = [w[n].shape for n in group] + [(1,)] * len(spare)
        for j, arrs in enumerate(zip(*[_unpack(r.reshape(-1), shapes) for r in res])):
            if j < len(group):
                out[group[j]] = list(arrs)
            else:
                loss = arrs[0][0]
    return (loss, grad_x[None], *[out[n][k] for k in range(4) for n in WEIGHTS])
```

```python
import functools
import math

import jax
import jax.numpy as jnp
from jax import lax
from jax.experimental import pallas as pl
from jax.experimental.pallas import tpu as pltpu

F32 = jnp.float32
BF16 = jnp.bfloat16

D_MODEL = 1024
POOL_DIM = 512
POOL_WINDOWS = (2, 4, 8, 16)
POOL_GROUP = 128
MLA_HEADS = 8
QK_NOPE = 64
QK_ROPE = 32
QK_DIM = QK_NOPE + QK_ROPE
V_HEAD = 64
HEAD_PAD = 128
Q_RANK = 256
KV_RANK = 128
ROPE_BASE = 10000.0
LRU_HEADS = 4
LRU_HEAD_DIM = 256
CONV_WIDTH = 4
LRU_C = 8.0
MEM_HEADS = 4
MEM_HEAD_DIM = 256
D_FF = 2816
RMS_EPS = 1e-6
NEG_INF = -1e30

ADAM_LR = 0.001
ADAM_B1 = 0.9
ADAM_B2 = 0.999
ADAM_EPS = 1e-08
ADAM_WD = 0.01
ADAM_STEP = 10

N_CHIPS = 4
LANES = 128
VMEM_LIMIT = 56 * 1024 * 1024
MESH = pl.DeviceIdType.MESH
ANY = pl.BlockSpec(memory_space=pl.ANY)
MIX_DIM = POOL_DIM + MLA_HEADS * HEAD_PAD

NN = (((1,), (0,)), ((), ()))
NT = (((1,), (1,)), ((), ()))
TN = (((0,), (0,)), ((), ()))


def _cp(n):
    return pltpu.CompilerParams(dimension_semantics=("arbitrary",) * n, vmem_limit_bytes=VMEM_LIMIT)


def _dot(a, b, dims=NN):
    return lax.dot_general(a, b, dims, preferred_element_type=F32)


def _row_tile(S):
    return 1024 if S % 1024 == 0 else min(S, 512)


def _rows(ts, w, cb=0):
    return pl.BlockSpec((ts, w), lambda i: (i, cb))


def _const(shape):
    return pl.BlockSpec(shape, lambda i: (0,) * len(shape))


MM_VMEM_BUDGET = 40 * 1024 * 1024


def _mm(a, bs, epi, outs, *, tn, nj, nt=False, also=None, extras=(), rows=(), sums=(), a_cb=0, k=None, tm=None,
        name):
    M = a.shape[0]
    k = k or a.shape[1]
    nb, ne, nr, no = len(bs), len(extras), len(rows), len(outs)
    lhs = [(a, k, a_cb, b) for b in bs[:1]] + ([(also[0], also[0].shape[1], 0, also[1])] if also else [])
    if tm is None:
        per_row = 2 * (sum(kk * x.dtype.itemsize for x, kk, _, _ in lhs)
                       + sum(e.dtype.itemsize for e, _ in extras) * tn
                       + sum(jnp.dtype(dt).itemsize for _, dt, _ in outs) * tn) + nb * tn * 4
        weights = (1 if nj == 1 else 2) * (sum(b.dtype.itemsize for b, _, _ in bs) * k
                                           + (also[1][0].dtype.itemsize * lhs[-1][1] if also else 0)) * tn
        tm = 1024 if M % 1024 == 0 and 1024 * per_row + weights <= MM_VMEM_BUDGET else min(M, 512)
    dims = NT if nt else NN
    assert not sums or nj == 1
    na = 2 if also else 0

    def body(*refs):
        av = refs[0][...].astype(BF16)
        accs = [_dot(av, r[...].astype(BF16), dims) for r in refs[1:1 + nb]]
        if also:
            accs[0] = accs[0] + _dot(refs[1 + nb][...].astype(BF16), refs[2 + nb][...].astype(BF16), dims)
        refs = refs[:1 + nb] + refs[1 + nb + na:]
        vals = epi(accs, [r[...] for r in refs[1 + nb:1 + nb + ne + nr]])
        outs_refs = refs[1 + nb + ne + nr:]
        for o, v in zip(outs_refs[:no], vals[:no]):
            o[...] = v.astype(o.dtype)
        if sums:
            @pl.when(pl.program_id(1) == 0)
            def _():
                for o in outs_refs[no:]:
                    o[...] = jnp.zeros_like(o)

            for o, v in zip(outs_refs[no:], vals[no:]):
                o[...] += v

    in_specs = [pl.BlockSpec((tm, k), lambda j, i: (i, a_cb))]
    weights = [(k, rb, cb) for (_, rb, cb) in bs]
    if also:
        in_specs_also = pl.BlockSpec((tm, lhs[-1][1]), lambda j, i: (i, 0))
        weights.append((lhs[-1][1], also[1][1], also[1][2]))
    for n, (kk, rb, cb) in enumerate(weights):
        if also and n == nb:
            in_specs.append(in_specs_also)
        mode = dict(pipeline_mode=pl.Buffered(1)) if nj == 1 else {}
        if nt:
            in_specs.append(pl.BlockSpec((tn, kk), lambda j, i, rb=rb, cb=cb: (rb + j, cb), **mode))
        else:
            in_specs.append(pl.BlockSpec((kk, tn), lambda j, i, rb=rb, cb=cb: (rb, cb + j), **mode))
    for (_, cb) in extras:
        in_specs.append(pl.BlockSpec((tm, tn), lambda j, i, cb=cb: (i, cb + j)))
    in_specs += [pl.BlockSpec((1, tn), lambda j, i: (0, 0))] * nr
    out_specs = [pl.BlockSpec((tm, tn), lambda j, i, cb=cb: (i, cb + j)) for (_, _, cb) in outs]
    out_specs += [pl.BlockSpec((1, w), lambda j, i: (0, 0)) for w in sums]
    res = pl.pallas_call(
        body, grid=(nj, M // tm), in_specs=in_specs, out_specs=out_specs,
        out_shape=[jax.ShapeDtypeStruct((M, n), dt) for (n, dt, _) in outs]
        + [jax.ShapeDtypeStruct((1, w), F32) for w in sums],
        compiler_params=_cp(2), name=name,
    )(a, *[b for (b, _, _) in bs], *([also[0], also[1][0]] if also else []), *[e for (e, _) in extras], *rows)
    return res


def _first(accs, extras):
    return [accs[0]]


def _add_res(accs, extras):
    return [accs[0] + extras[0].astype(F32)]


def _norm_bwd_epilogue(partials):
    def epi(accs, vals):
        dh = accs[0]
        for part in vals[:partials]:
            dh = dh + part.astype(F32)
        x, res, g = vals[partials:partials + 3]
        r = lax.rsqrt(jnp.mean(x * x, axis=-1, keepdims=True) + RMS_EPS)
        n = x * r
        dn = dh * g
        return [r * (dn - n * jnp.mean(dn * n, axis=-1, keepdims=True)) + res, jnp.sum(dh * n, axis=0, keepdims=True)]

    return epi


TN_VMEM_BUDGET = 36 * 1024 * 1024


def _contraction_rows(S, row_bytes, out_elems):
    ts = min(S, 2048)
    while ts > 512 and 2 * (ts * row_bytes + out_elems * 4) > TN_VMEM_BUDGET:
        ts //= 2
    return ts


def _mm_tn(a, b, *, ka=None, a_cb=0, nb=None, b_cb=0, tk=None, tn=None, ts=None, name):
    S = a.shape[0]
    ka = ka or a.shape[1]
    nb = nb or b.shape[1]
    tk = tk or ka
    tn = tn or nb
    ts = ts or _contraction_rows(S, tk * a.dtype.itemsize + tn * b.dtype.itemsize, tk * tn)
    a0, b0 = a_cb * (ka // tk), b_cb * (nb // tn)

    def body(a_ref, b_ref, o_ref):
        @pl.when(pl.program_id(2) == 0)
        def _():
            o_ref[...] = jnp.zeros_like(o_ref)

        o_ref[...] += _dot(a_ref[...].astype(BF16), b_ref[...].astype(BF16), TN)

    return pl.pallas_call(
        body, grid=(ka // tk, nb // tn, S // ts),
        in_specs=[pl.BlockSpec((ts, tk), lambda p, q, s: (s, a0 + p)),
                  pl.BlockSpec((ts, tn), lambda p, q, s: (s, b0 + q))],
        out_specs=pl.BlockSpec((tk, tn), lambda p, q, s: (p, q)),
        out_shape=jax.ShapeDtypeStruct((ka, nb), F32), compiler_params=_cp(3), name=name,
    )(a, b)


def _mm_tn_owners(a, bs, *, name):
    S, ka = a.shape
    nb = sum(b.shape[1] for b in bs)
    tn = nb // N_CHIPS
    ts = _contraction_rows(S, ka * a.dtype.itemsize + len(bs) * tn * bs[0].dtype.itemsize, ka * tn)
    per = N_CHIPS // len(bs)

    def body(a_ref, *refs):
        o_ref = refs[-1]
        q = pl.program_id(0)

        @pl.when(pl.program_id(1) == 0)
        def _():
            o_ref[...] = jnp.zeros_like(o_ref)

        av = a_ref[...].astype(BF16)
        for n, b_ref in enumerate(refs[:-1]):
            @pl.when(q // per == n)
            def _():
                o_ref[0] += _dot(av, b_ref[...].astype(BF16), TN)

    in_specs = [pl.BlockSpec((ts, ka), lambda q, s: (s, 0))]
    for n in range(len(bs)):
        in_specs.append(pl.BlockSpec((ts, tn), lambda q, s, n=n: (jnp.where(q // per == n, s, 0),
                                                                  jnp.clip(q - n * per, 0, per - 1))))
    return pl.pallas_call(
        body, grid=(N_CHIPS, S // ts), in_specs=in_specs,
        out_specs=pl.BlockSpec((1, ka, tn), lambda q, s: (q, 0, 0)),
        out_shape=jax.ShapeDtypeStruct((N_CHIPS, ka, tn), F32), compiler_params=_cp(2), name=name,
    )(a, *bs)


def _mm_tn_grouped(a, b, groups, w, *, name):
    S = a.shape[0]
    ts = _contraction_rows(S, w * (a.dtype.itemsize + b.dtype.itemsize), w * w)

    def body(a_ref, b_ref, o_ref):
        @pl.when(pl.program_id(1) == 0)
        def _():
            o_ref[...] = jnp.zeros_like(o_ref)

        o_ref[0] += _dot(a_ref[...].astype(BF16), b_ref[...].astype(BF16), TN)

    return pl.pallas_call(
        body, grid=(groups, S // ts),
        in_specs=[pl.BlockSpec((ts, w), lambda g, s: (s, g)), pl.BlockSpec((ts, w), lambda g, s: (s, g))],
        out_specs=pl.BlockSpec((1, w, w), lambda g, s: (g, 0, 0)),
        out_shape=jax.ShapeDtypeStruct((groups, w, w), F32), compiler_params=_cp(2), name=name,
    )(a, b)


def _rms(x, g, *, cb=0, w=None, ts=None, name):
    S = x.shape[0]
    w = w or x.shape[1]
    ts = ts or _row_tile(S)

    def body(x_ref, g_ref, o_ref):
        xv = x_ref[...].astype(F32)
        r = lax.rsqrt(jnp.mean(xv * xv, axis=-1, keepdims=True) + RMS_EPS)
        o_ref[...] = (xv * r * g_ref[...]).astype(o_ref.dtype)

    return pl.pallas_call(
        body, grid=(S // ts,), in_specs=[_rows(ts, w, cb), _const((1, w))], out_specs=_rows(ts, w),
        out_shape=jax.ShapeDtypeStruct((S, w), BF16), compiler_params=_cp(1), name=name,
    )(x, g.reshape(1, w))


def _rms_bwd(x, g, dy, *, cb=0, w=None, res=None, out_dtype=F32, ts=None, name):
    S = x.shape[0]
    w = w or x.shape[1]
    ts = ts or min(S, 512)
    has_res = res is not None

    def body(*refs):
        x_ref, g_ref, dy_ref = refs[:3]
        dx_ref, dg_ref = refs[-2:]
        xv = x_ref[...].astype(F32)
        r = lax.rsqrt(jnp.mean(xv * xv, axis=-1, keepdims=True) + RMS_EPS)
        n = xv * r
        dyv = dy_ref[...].astype(F32)
        dn = dyv * g_ref[...]
        dx = r * (dn - n * jnp.mean(dn * n, axis=-1, keepdims=True))
        if has_res:
            dx = dx + refs[3][...].astype(F32)
        dx_ref[...] = dx.astype(dx_ref.dtype)

        @pl.when(pl.program_id(0) == 0)
        def _():
            dg_ref[...] = jnp.zeros_like(dg_ref)

        dg_ref[...] += jnp.sum(dyv * n, axis=0, keepdims=True)

    ins = [x, g.reshape(1, w), dy] + ([res] if has_res else [])
    in_specs = [_rows(ts, w, cb), _const((1, w)), _rows(ts, w)] + ([_rows(ts, w)] if has_res else [])
    return pl.pallas_call(
        body, grid=(S // ts,), in_specs=in_specs, out_specs=[_rows(ts, w), _const((1, w))],
        out_shape=[jax.ShapeDtypeStruct((S, w), out_dtype), jax.ShapeDtypeStruct((1, w), F32)],
        compiler_params=_cp(1), name=name,
    )(*ins)


HALO = 16


def _pool_counts(i, ts, rows, first_row):
    t = i * ts + first_row + lax.broadcasted_iota(jnp.int32, (rows, 1), 0)
    return [jnp.minimum(t + 1, w).astype(F32) for w in POOL_WINDOWS]


def _even_front(x, g, w_in, pool_w, pool_scale, g_q, w_q, g_kv, w_kv, ctab, stab, *, name):
    S = x.shape[0]
    ts = min(S, 512)

    def body(x_ref, g_ref, win_ref, pw_ref, sc_ref, gq_ref, wq_ref, gkv_ref, wkv_ref, c_ref, s_ref,
             h_ref, z_ref, y_ref, p_ref, cqn_ref, ckvn_ref, q_ref, k_ref, v_ref, tail):
        i = pl.program_id(0)

        def normed(t, gain):
            r = lax.rsqrt(jnp.mean(t * t, axis=-1, keepdims=True) + RMS_EPS)
            return (t * r * gain).astype(BF16)

        h = normed(x_ref[...], g_ref[...])
        h_ref[...] = h
        z = _dot(h, win_ref[...])
        z_ref[...] = z
        u = z[:, :POOL_DIM]
        xe = jnp.concatenate([jnp.where(i > 0, tail[...], 0.0), u], axis=0)
        tail[...] = u[ts - HALO:]
        sums = []
        s = xe
        for sh in (1, 2, 4, 8):
            s = s + pltpu.roll(s, sh, 0)
            sums.append(s)
        cnts = _pool_counts(i, ts, ts, 0)
        for grp in range(4):
            lo, hi = grp * POOL_GROUP, (grp + 1) * POOL_GROUP
            pooled = (sums[grp][HALO:, lo:hi] / cnts[grp] - u[:, lo:hi]).astype(BF16)
            p_ref[:, lo:hi] = pooled
            y_ref[:, lo:hi] = (_dot(pooled, pw_ref[grp]) * sc_ref[:, lo:hi]).astype(y_ref.dtype)
        cqn = normed(z[:, POOL_DIM:POOL_DIM + Q_RANK], gq_ref[...])
        ckvn = normed(z[:, POOL_DIM + Q_RANK:POOL_DIM + Q_RANK + KV_RANK], gkv_ref[...])
        cqn_ref[...] = cqn
        ckvn_ref[...] = ckvn
        q = _dot(cqn, wq_ref[...])
        kv = _dot(ckvn, wkv_ref[...])
        c, sn = c_ref[...], s_ref[...]
        kr = z[:, D_MODEL - HEAD_PAD:]
        kr_rot = kr * c + _rope_partner(kr) * sn
        lane = lax.broadcasted_iota(jnp.int32, (ts, HEAD_PAD), 1)
        for hd in range(MLA_HEADS):
            lo, hi = hd * HEAD_PAD, (hd + 1) * HEAD_PAD
            qh = q[:, lo:hi]
            q_ref[:, lo:hi] = (qh * c + _rope_partner(qh) * sn).astype(q_ref.dtype)
            k_ref[:, lo:hi] = (kv[:, lo:hi] + kr_rot).astype(k_ref.dtype)
            v_ref[:, lo:hi] = jnp.where(lane == V_HEAD, 1.0, kv[:, D_MODEL + lo:D_MODEL + hi]).astype(v_ref.dtype)

    wide = jax.ShapeDtypeStruct((S, D_MODEL), BF16)
    return pl.pallas_call(
        body, grid=(S // ts,),
        in_specs=[_rows(ts, D_MODEL), _const((1, D_MODEL)), _const((D_MODEL, D_MODEL)),
                  _const((4, POOL_GROUP, POOL_GROUP)), _const((1, POOL_DIM)), _const((1, Q_RANK)),
                  _const((Q_RANK, D_MODEL)), _const((1, KV_RANK)), _const((KV_RANK, 2 * D_MODEL)),
                  _rows(ts, HEAD_PAD), _rows(ts, HEAD_PAD)],
        out_specs=[_rows(ts, D_MODEL), _rows(ts, D_MODEL), _rows(ts, POOL_DIM), _rows(ts, POOL_DIM),
                   _rows(ts, Q_RANK), _rows(ts, KV_RANK), _rows(ts, D_MODEL), _rows(ts, D_MODEL), _rows(ts, D_MODEL)],
        out_shape=[wide, jax.ShapeDtypeStruct((S, D_MODEL), F32), jax.ShapeDtypeStruct((S, MIX_DIM), BF16),
                   jax.ShapeDtypeStruct((S, POOL_DIM), BF16), jax.ShapeDtypeStruct((S, Q_RANK), BF16),
                   jax.ShapeDtypeStruct((S, KV_RANK), BF16), wide, wide, wide],
        scratch_shapes=[pltpu.VMEM((HALO, POOL_DIM), F32)], compiler_params=_cp(1), name=name,
    )(x, g.reshape(1, D_MODEL), w_in, pool_w, pool_scale, g_q.reshape(1, Q_RANK), w_q, g_kv.reshape(1, KV_RANK), w_kv,
      ctab, stab)


def _norm_bwd_values(xv, gain, dy):
    r = lax.rsqrt(jnp.mean(xv * xv, axis=-1, keepdims=True) + RMS_EPS)
    n = xv * r
    dn = dy * gain
    return r * (dn - n * jnp.mean(dn * n, axis=-1, keepdims=True)), jnp.sum(dy * n, axis=0, keepdims=True)


def _even_back(dq_rot, dk_cat, dv, dmix, pooled, z, x, dxo, ctab, stab, w_q, w_kv, w_in, pool_w, pool_scale, g_q, g_kv,
               g_x, *, name):
    S = x.shape[0]
    ts = min(S, 512)
    nh = ts // HALO
    last = S // HALO - 1
    n = ts + HALO

    def body(dq_ref, dk_ref, dv_ref, dy_ref, dyh_ref, p_ref, z_ref, x_ref, dxo_ref, c_ref, s_ref, wq_ref, wkv_ref,
             win_ref, pw_ref, sc_ref, gq_ref, gkv_ref, gx_ref,
             dx_ref, dqp_ref, dz_ref, dyp_ref, dgq_ref, dgkv_ref, dsc_ref, dgx_ref):
        i = pl.program_id(0)

        @pl.when(i == 0)
        def _():
            for ref in (dgq_ref, dgkv_ref, dsc_ref, dgx_ref):
                ref[...] = jnp.zeros_like(ref)

        c, sn = c_ref[...], s_ref[...]
        z = z_ref[...]
        dk = dk_ref[...]
        for hd in range(MLA_HEADS):
            lo, hi = hd * HEAD_PAD, (hd + 1) * HEAD_PAD
            g = dq_ref[:, lo:hi]
            dqp_ref[:, lo:hi] = (g * c + _rope_partner(g * sn)).astype(dqp_ref.dtype)
            heads_sum = dk[:, lo:hi] if hd == 0 else heads_sum + dk[:, lo:hi]
        lane = lax.broadcasted_iota(jnp.int32, heads_sum.shape, 1)
        dkr = jnp.where((lane >= QK_NOPE) & (lane < QK_DIM), heads_sum * c + _rope_partner(heads_sum * sn), 0.0)
        dcqn = _dot(dqp_ref[...], wq_ref[...], NT)
        dckvn = _dot(dk.astype(BF16), wkv_ref[:, :D_MODEL], NT) + _dot(dv_ref[...].astype(BF16),
                                                                       wkv_ref[:, D_MODEL:], NT)
        dcq, dgq = _norm_bwd_values(z[:, POOL_DIM:POOL_DIM + Q_RANK], gq_ref[...], dcqn)
        dckv, dgkv = _norm_bwd_values(z[:, POOL_DIM + Q_RANK:POOL_DIM + Q_RANK + KV_RANK], gkv_ref[...], dckvn)
        dgq_ref[...] += dgq
        dgkv_ref[...] += dgkv
        dyv = dy_ref[...].astype(F32)
        dyh = jnp.where(i < pl.num_programs(0) - 1, dyh_ref[...].astype(F32), 0.0)
        dypre = (jnp.concatenate([dyv, dyh], axis=0) * sc_ref[...]).astype(BF16)
        dyp_ref[...] = dypre[:ts]
        cnts = _pool_counts(i, ts, n, 0)
        dsc = []
        for grp in range(4):
            lo, hi = grp * POOL_GROUP, (grp + 1) * POOL_GROUP
            dsc.append(jnp.sum(dyv[:, lo:hi] * _dot(p_ref[:, lo:hi], pw_ref[grp]), axis=0, keepdims=True))
            dpool = _dot(dypre[:, lo:hi], pw_ref[grp], NT)
            s = dpool / cnts[grp]
            for sh in (1, 2, 4, 8)[:grp + 1]:
                s = s + pltpu.roll(s, n - sh, 0)
            dz_ref[:, lo:hi] = (s[:ts] - dpool[:ts]).astype(dz_ref.dtype)
        dsc_ref[...] += jnp.concatenate(dsc, axis=1)
        dz_ref[:, POOL_DIM:POOL_DIM + Q_RANK] = dcq.astype(dz_ref.dtype)
        dz_ref[:, POOL_DIM + Q_RANK:POOL_DIM + Q_RANK + KV_RANK] = dckv.astype(dz_ref.dtype)
        dz_ref[:, D_MODEL - HEAD_PAD:] = dkr.astype(dz_ref.dtype)
        dx, dgx = _norm_bwd_values(x_ref[...], gx_ref[...], _dot(dz_ref[...], win_ref[...], NT))
        dx_ref[...] = dx + dxo_ref[...]
        dgx_ref[...] += dgx

    wide, pool = _rows(ts, D_MODEL), _rows(ts, POOL_DIM)
    f32 = lambda w: jax.ShapeDtypeStruct((1, w), F32)
    return pl.pallas_call(
        body, grid=(S // ts,),
        in_specs=[wide, wide, wide, pool,
                  pl.BlockSpec((HALO, POOL_DIM), lambda i: (jnp.minimum((i + 1) * nh, last), 0)), pool, wide, wide, wide,
                  _rows(ts, HEAD_PAD), _rows(ts, HEAD_PAD), _const((Q_RANK, D_MODEL)), _const((KV_RANK, 2 * D_MODEL)),
                  _const((D_MODEL, D_MODEL)), _const((4, POOL_GROUP, POOL_GROUP)), _const((1, POOL_DIM)),
                  _const((1, Q_RANK)), _const((1, KV_RANK)), _const((1, D_MODEL))],
        out_specs=[wide, wide, wide, pool, _const((1, Q_RANK)), _const((1, KV_RANK)), _const((1, POOL_DIM)),
                   _const((1, D_MODEL))],
        out_shape=[jax.ShapeDtypeStruct((S, D_MODEL), F32), jax.ShapeDtypeStruct((S, D_MODEL), BF16),
                   jax.ShapeDtypeStruct((S, D_MODEL), BF16), jax.ShapeDtypeStruct((S, POOL_DIM), BF16),
                   f32(Q_RANK), f32(KV_RANK), f32(POOL_DIM), f32(D_MODEL)],
        compiler_params=_cp(1), name=name,
    )(dq_rot, dk_cat, dv, dmix, dmix, pooled, z, x, dxo, ctab, stab, w_q, w_kv, w_in, pool_w, pool_scale,
      g_q.reshape(1, Q_RANK), g_kv.reshape(1, KV_RANK), g_x.reshape(1, D_MODEL))


def _rope_partner(t):
    lane = lax.broadcasted_iota(jnp.int32, t.shape, 1)
    swapped = jnp.where(lane < QK_NOPE + QK_ROPE // 2, pltpu.roll(t, HEAD_PAD - QK_ROPE // 2, 1),
                        pltpu.roll(t, QK_ROPE // 2, 1))
    return jnp.where((lane >= QK_NOPE) & (lane < QK_DIM), swapped, 0.0)


ATT_SCALE = QK_DIM ** -0.5
LOG2E = math.log2(math.e)


HEADS_PER_STEP = 2
ATT_COL0 = POOL_DIM // HEAD_PAD


FWD_TILE = 1024


def _stat_rows(col):
    return jnp.broadcast_to(col, (col.shape[0], LANES)).T[0:8]


def _retile_rows(rows, tq):
    heads, n8, t = rows.shape
    if t == tq:
        return rows
    flat = rows.reshape(heads, n8 // 8, 8, t)[:, :, 0].reshape(heads, -1, 1, tq)
    return jnp.broadcast_to(flat, (heads, flat.shape[1], 8, tq)).reshape(heads, -1, tq)


def _flash_fwd(q, k, v, mix, *, name):
    S = q.shape[0]
    tq = FWD_TILE if S % FWD_TILE == 0 else min(S, 512)
    nq = S // tq
    hs = HEADS_PER_STEP
    wide = hs * HEAD_PAD

    def body(q_ref, k_ref, v_ref, mix_ref, o_ref, lse_ref):
        qi = pl.program_id(1)
        qv = [q_ref[:, a * HEAD_PAD:(a + 1) * HEAD_PAD] for a in range(hs)]

        def update(m, acc, s, v):
            m_new = jnp.maximum(m, jnp.max(s, axis=-1, keepdims=True))
            p = jnp.exp2((s - m_new) * (ATT_SCALE * LOG2E))
            alpha = jnp.exp2((m - m_new) * (ATT_SCALE * LOG2E))
            return m_new, alpha * acc + _dot(p.astype(BF16), v)

        def step(j, carry, masked):
            off = pl.multiple_of(j * tq, tq)
            out = []
            for a in range(hs):
                head = slice(a * HEAD_PAD, (a + 1) * HEAD_PAD)
                s = _dot(qv[a], k_ref[pl.ds(off, tq), head], NT)
                if masked:
                    row = lax.broadcasted_iota(jnp.int32, (tq, tq), 0)
                    col = lax.broadcasted_iota(jnp.int32, (tq, tq), 1)
                    s = jnp.where(col <= row, s, NEG_INF)
                out.append(update(*carry[a], s, v_ref[pl.ds(off, tq), head]))
            return tuple(out)

        one = (jnp.full((tq, 1), NEG_INF, F32), jnp.zeros((tq, HEAD_PAD), F32))
        carry = step(qi, lax.fori_loop(0, qi, lambda j, c: step(j, c, False), (one,) * hs), True)
        for a in range(hs):
            m, acc = carry[a]
            l = acc[:, V_HEAD:V_HEAD + 1]
            o_ref[:, a * HEAD_PAD:(a + 1) * HEAD_PAD] = (acc / l).astype(o_ref.dtype)
            lse_ref[a] = _stat_rows(m * ATT_SCALE + jnp.log(l))

    blk = pl.BlockSpec((tq, wide), lambda h, i: (i, h))
    full = pl.BlockSpec((S, wide), lambda h, i: (0, h))
    return pl.pallas_call(
        body, grid=(MLA_HEADS // hs, nq), in_specs=[blk, full, full, ANY],
        out_specs=[pl.BlockSpec((tq, wide), lambda h, i: (i, ATT_COL0 // hs + h)),
                   pl.BlockSpec((hs, 8, tq), lambda h, i: (h, i, 0))],
        out_shape=[jax.ShapeDtypeStruct(mix.shape, mix.dtype), jax.ShapeDtypeStruct((MLA_HEADS, nq * 8, tq), F32)],
        input_output_aliases={3: 0}, compiler_params=_cp(2), name=name,
    )(q, k, v, mix)


BWD_TILE = 1024
BWD_HEADS_PER_STEP = 1


def _bwd_tile(S):
    return BWD_TILE if S % BWD_TILE == 0 else min(S, 512)


def _attn_delta(dmix, mix, *, name):
    S = mix.shape[0]
    ts = _bwd_tile(S)
    half = MLA_HEADS // 2
    halves = [_rows(ts, half * HEAD_PAD, 1), _rows(ts, half * HEAD_PAD, 2)]

    def body(do0_ref, do1_ref, o0_ref, o1_ref, d_ref):
        for n, (do_ref, o_ref) in enumerate(((do0_ref, o0_ref), (do1_ref, o1_ref))):
            prod = do_ref[...].astype(F32) * o_ref[...].astype(F32)
            for a in range(half):
                d_ref[n * half + a] = _stat_rows(
                    jnp.sum(prod[:, a * HEAD_PAD:(a + 1) * HEAD_PAD], axis=-1, keepdims=True))

    return pl.pallas_call(
        body, grid=(S // ts,), in_specs=halves + halves,
        out_specs=pl.BlockSpec((MLA_HEADS, 8, ts), lambda i: (0, i, 0)),
        out_shape=jax.ShapeDtypeStruct((MLA_HEADS, (S // ts) * 8, ts), F32), compiler_params=_cp(1), name=name,
    )(dmix, dmix, mix, mix)


def _flash_bwd(q, k, v, dmix, lse_rows, delta_rows, *, name):
    S = q.shape[0]
    tq = _bwd_tile(S)
    nq = S // tq
    hs = BWD_HEADS_PER_STEP
    wide = hs * HEAD_PAD

    def body(q_hbm, do_hbm, lse_ref, dl_ref, k_ref, v_ref, dq_hbm, dk_ref, dv_ref, q_all, do_all, dq_all):
        g, j = pl.program_id(0), pl.program_id(1)
        cols = pl.multiple_of(g * wide, wide)

        @pl.when(j == 0)
        def _():
            pltpu.sync_copy(q_hbm.at[:, pl.ds(cols, wide)], q_all)
            pltpu.sync_copy(do_hbm.at[:, pl.ds(POOL_DIM + cols, wide)], do_all)
            dq_all[...] = jnp.zeros_like(dq_all)

        heads = [slice(a * HEAD_PAD, (a + 1) * HEAD_PAD) for a in range(hs)]
        kv = [k_ref[:, a] for a in heads]
        vv = [v_ref[:, a] for a in heads]

        def block(a, keys, rows, lse2, dl, first_query):
            qv, dov = q_all[rows, heads[a]], do_all[rows, heads[a]]
            st = _dot(kv[a][:keys], qv, NT)
            if first_query is not None:
                krow = lax.broadcasted_iota(jnp.int32, st.shape, 0)
                qcol = lax.broadcasted_iota(jnp.int32, st.shape, 1) + first_query
                st = jnp.where(krow <= qcol, st, NEG_INF)
            pt = jnp.exp2(st * (ATT_SCALE * LOG2E) - lse2)
            dst = (pt * (_dot(vv[a][:keys], dov, NT) - dl)).astype(BF16)
            dq_all[rows, heads[a]] += _dot(dst, kv[a][:keys], TN)
            return _dot(dst, qv), _dot(pt.astype(BF16), dov)

        def stats(a, i):
            off8 = pl.multiple_of(i * 8, 8)
            return lse_ref[a, pl.ds(off8, 8), :][0:1] * LOG2E, dl_ref[a, pl.ds(off8, 8), :][0:1]

        def step(i, carry):
            rows = pl.ds(pl.multiple_of(i * tq, tq), tq)
            out = []
            for a in range(hs):
                dk, dv = block(a, tq, rows, *stats(a, i), None)
                out.append((carry[a][0] + dk, carry[a][1] + dv))
            return tuple(out)

        def diagonal():
            half = tq // 2
            out = []
            for a in range(hs):
                lse2, dl = stats(a, j)
                off = pl.multiple_of(j * tq, tq)
                dk0, dv0 = block(a, half, pl.ds(off, half), lse2[:, :half], dl[:, :half], 0)
                dk1, dv1 = block(a, tq, pl.ds(pl.multiple_of(off + half, half), half), lse2[:, half:], dl[:, half:], half)
                zero = jnp.zeros((tq - half, HEAD_PAD), F32)
                out.append((dk1 + jnp.concatenate([dk0, zero], axis=0), dv1 + jnp.concatenate([dv0, zero], axis=0)))
            return tuple(out)

        carry = lax.fori_loop(j + 1, nq, step, diagonal())
        for a in range(hs):
            dk_ref[:, heads[a]] = carry[a][0] * ATT_SCALE
            dv_ref[:, heads[a]] = carry[a][1]

        @pl.when(j == nq - 1)
        def _():
            dq_all[...] = dq_all[...] * ATT_SCALE
            pltpu.sync_copy(dq_all, dq_hbm.at[:, pl.ds(cols, wide)])

    blk = pl.BlockSpec((tq, wide), lambda g, j: (j, g))
    stat = pl.BlockSpec((hs, nq * 8, tq), lambda g, j: (g, 0, 0))
    full = jax.ShapeDtypeStruct((S, MLA_HEADS * HEAD_PAD), F32)
    return pl.pallas_call(
        body, grid=(MLA_HEADS // hs, nq), in_specs=[ANY, ANY, stat, stat, blk, blk], out_specs=[ANY, blk, blk],
        out_shape=[full, full, full],
        scratch_shapes=[pltpu.VMEM((S, wide), BF16), pltpu.VMEM((S, wide), BF16), pltpu.VMEM((S, wide), F32)],
        compiler_params=_cp(2), name=name,
    )(q, dmix, lse_rows, delta_rows, k, v)


MEM_SCALE = MEM_HEAD_DIM ** -0.5


def _xattn_probs(qh, kh):
    s = _dot(qh, kh, NT) * MEM_SCALE
    e = jnp.exp(s - jnp.max(s, axis=-1, keepdims=True))
    return e / jnp.sum(e, axis=-1, keepdims=True)


def _xa_block_fwd(x, kvm, w_q, w_o, g, *, name):
    S = x.shape[0]
    ts = min(S, 512)
    nm = kvm.shape[0]

    def body(x_ref, kv_ref, wq_ref, wo_ref, g_ref, xo_ref, hx_ref, q_ref, o_ref):
        xv = x_ref[...]
        r = lax.rsqrt(jnp.mean(xv * xv, axis=-1, keepdims=True) + RMS_EPS)
        hx = (xv * r * g_ref[...]).astype(BF16)
        hx_ref[...] = hx
        q = _dot(hx, wq_ref[...]).astype(BF16)
        q_ref[...] = q
        for h in range(MEM_HEADS):
            lo, hi = h * MEM_HEAD_DIM, (h + 1) * MEM_HEAD_DIM
            p = _xattn_probs(q[:, lo:hi], kv_ref[:, lo:hi])
            o_ref[:, lo:hi] = _dot(p.astype(BF16), kv_ref[:, D_MODEL + lo:D_MODEL + hi]).astype(o_ref.dtype)
        xo_ref[...] = xv + _dot(o_ref[...], wo_ref[...])

    square = _const((D_MODEL, D_MODEL))
    act = jax.ShapeDtypeStruct((S, D_MODEL), BF16)
    return pl.pallas_call(
        body, grid=(S // ts,),
        in_specs=[_rows(ts, D_MODEL), _const((nm, 2 * D_MODEL)), square, square, _const((1, D_MODEL))],
        out_specs=[_rows(ts, D_MODEL)] * 4, out_shape=[jax.ShapeDtypeStruct((S, D_MODEL), F32), act, act, act],
        compiler_params=_cp(1), name=name,
    )(x, kvm, w_q, w_o, g.reshape(1, D_MODEL))


def _xa_block_bwd(dxo, x, q, kvm, w_q, w_o, g, *, name):
    S = q.shape[0]
    ts = min(S, 512)
    nm = kvm.shape[0]

    def body(dxo_ref, x_ref, q_ref, kv_ref, wq_ref, wo_ref, g_ref, dx_ref, dq_ref, dkv_ref, dg_ref):
        @pl.when(pl.program_id(0) == 0)
        def _():
            dkv_ref[...] = jnp.zeros_like(dkv_ref)
            dg_ref[...] = jnp.zeros_like(dg_ref)

        dxo = dxo_ref[...]
        do = _dot(dxo.astype(BF16), wo_ref[...], NT).astype(BF16)
        for h in range(MEM_HEADS):
            lo, hi = h * MEM_HEAD_DIM, (h + 1) * MEM_HEAD_DIM
            qh, kh, vh = q_ref[:, lo:hi], kv_ref[:, lo:hi], kv_ref[:, D_MODEL + lo:D_MODEL + hi]
            doh = do[:, lo:hi]
            p = _xattn_probs(qh, kh)
            dp = _dot(doh, vh, NT)
            ds = (p * (dp - jnp.sum(dp * p, axis=-1, keepdims=True)) * MEM_SCALE).astype(BF16)
            dq_ref[:, lo:hi] = _dot(ds, kh).astype(dq_ref.dtype)
            dkv_ref[:, lo:hi] += _dot(ds, qh, TN)
            dkv_ref[:, D_MODEL + lo:D_MODEL + hi] += _dot(p.astype(BF16), doh, TN)
        dx, dg = _norm_bwd_epilogue(0)([_dot(dq_ref[...], wq_ref[...], NT)], [x_ref[...], dxo, g_ref[...]])
        dx_ref[...] = dx
        dg_ref[...] += dg

    square = _const((D_MODEL, D_MODEL))
    return pl.pallas_call(
        body, grid=(S // ts,),
        in_specs=[_rows(ts, D_MODEL), _rows(ts, D_MODEL), _rows(ts, D_MODEL), _const((nm, 2 * D_MODEL)), square,
                  square, _const((1, D_MODEL))],
        out_specs=[_rows(ts, D_MODEL), _rows(ts, D_MODEL), _const((nm, 2 * D_MODEL)), _const((1, D_MODEL))],
        out_shape=[jax.ShapeDtypeStruct((S, D_MODEL), F32), jax.ShapeDtypeStruct((S, D_MODEL), BF16),
                   jax.ShapeDtypeStruct((nm, 2 * D_MODEL), F32), jax.ShapeDtypeStruct((1, D_MODEL), F32)],
        compiler_params=_cp(1), name=name,
    )(dxo, x, q, kvm, w_q, w_o, g.reshape(1, D_MODEL))


CONV_HALO = 8


def _sigmoid(x):
    return 0.5 * jnp.tanh(0.5 * x) + 0.5


def _softplus(x):
    return jnp.maximum(x, 0.0) + jnp.log(1.0 + jnp.exp(-jnp.abs(x)))


def _neg_expm1(x):
    series = -x * (1.0 + x * (1.0 / 2) * (1.0 + x * (1.0 / 3) * (1.0 + x * (1.0 / 4) * (1.0 + x * (1.0 / 5)))))
    return jnp.where(x > -0.05, series, 1.0 - jnp.exp(x))


GELU_C = math.sqrt(2.0 / math.pi)


def _gelu(x):
    return 0.5 * x * (1.0 + jnp.tanh(GELU_C * (x + 0.044715 * x * x * x)))


def _gelu_grad(x):
    t = jnp.tanh(GELU_C * (x + 0.044715 * x * x * x))
    return 0.5 * (1.0 + t) + 0.5 * x * (1.0 - t * t) * GELU_C * (1.0 + 3 * 0.044715 * x * x)


def _lru_gates(xc, wr_ref, br, wi_ref, bi, sp, reset):
    xcb = xc.astype(BF16)
    pr, pi = [], []
    for h in range(LRU_HEADS):
        lo, hi = h * LRU_HEAD_DIM, (h + 1) * LRU_HEAD_DIM
        pr.append(_dot(xcb[:, lo:hi], wr_ref[h]))
        pi.append(_dot(xcb[:, lo:hi], wi_ref[h]))
    r = _sigmoid(jnp.concatenate(pr, axis=1) + br)
    ig = _sigmoid(jnp.concatenate(pi, axis=1) + bi)
    log_a = -LRU_C * r * sp
    a = jnp.where(reset, 0.0, jnp.exp(log_a))
    mult = jnp.where(reset, 1.0, jnp.sqrt(jnp.maximum(_neg_expm1(2.0 * log_a), 0.0)))
    return r, ig, a, mult


SUBLANES = 8


def _compose_groups(a, b, reverse):
    n = a.shape[0]
    row = lax.broadcasted_iota(jnp.int32, a.shape, 0) % SUBLANES
    for s in (1, 2, 4):
        inside = (row < SUBLANES - s) if reverse else (row >= s)
        shift = n - s if reverse else s
        a_s = jnp.where(inside, pltpu.roll(a, shift, 0), 1.0)
        b_s = jnp.where(inside, pltpu.roll(b, shift, 0), 0.0)
        b = a * b_s + b
        a = a * a_s
    return a, b


def _chain_groups(a_buf, h_ref, state, reverse):
    groups = a_buf.shape[0] // SUBLANES

    def group(g, h_in):
        off = pl.multiple_of((groups - 1 - g if reverse else g) * SUBLANES, SUBLANES)
        h = a_buf[pl.ds(off, SUBLANES), :] * h_in + h_ref[pl.ds(off, SUBLANES), :]
        h_ref[pl.ds(off, SUBLANES), :] = h
        return jnp.broadcast_to(h[0:1] if reverse else h[SUBLANES - 1:SUBLANES], h.shape)

    return lax.fori_loop(0, groups, group, state, unroll=4)[0:1]


def _lru_fwd(z, reset, conv_w, conv_b, w_r, b_r, w_i, b_i, lam, *, name):
    S = z.shape[0]
    ts = min(S, 512)
    nh = ts // CONV_HALO
    W = D_MODEL

    def body(gate_ref, xb_ref, halo_ref, rs_ref, cw_ref, cb_ref, wr_ref, br_ref, wi_ref, bi_ref, lam_ref,
             xc_ref, h_ref, y_ref, a_buf, carry):
        i = pl.program_id(0)

        @pl.when(i == 0)
        def _():
            carry[...] = jnp.zeros_like(carry)

        halo = jnp.where(i > 0, halo_ref[...], 0.0)
        xe = jnp.concatenate([halo, xb_ref[...]], axis=0)
        xc = cb_ref[...] + cw_ref[3:4, :] * xe[CONV_HALO:]
        for kk in range(CONV_WIDTH - 1):
            xc = xc + cw_ref[kk:kk + 1, :] * pltpu.roll(xe, CONV_WIDTH - 1 - kk, 0)[CONV_HALO:]
        xc_ref[...] = xc
        reset = rs_ref[...] > 0.5
        _, ig, a, mult = _lru_gates(xc, wr_ref, br_ref[...], wi_ref, bi_ref[...], _softplus(-lam_ref[...]), reset)
        a_buf[...], h_ref[...] = _compose_groups(a, mult * (ig * xc), False)
        carry[...] = _chain_groups(a_buf, h_ref, jnp.broadcast_to(carry[...], (SUBLANES, W)), False)
        y_ref[...] = (_gelu(gate_ref[...]) * h_ref[...]).astype(y_ref.dtype)

    vec = _const((1, W))
    gw = _const((LRU_HEADS, LRU_HEAD_DIM, LRU_HEAD_DIM))
    return pl.pallas_call(
        body, grid=(S // ts,),
        in_specs=[_rows(ts, W, 0), _rows(ts, W, 1),
                  pl.BlockSpec((CONV_HALO, W), lambda i: (jnp.maximum(i * nh - 1, 0), 1)),
                  _rows(ts, 1), _const((CONV_WIDTH, W)), vec, gw, vec, gw, vec, vec],
        out_specs=[_rows(ts, W)] * 3,
        out_shape=[jax.ShapeDtypeStruct((S, W), F32), jax.ShapeDtypeStruct((S, W), F32),
                   jax.ShapeDtypeStruct((S, W), BF16)],
        scratch_shapes=[pltpu.VMEM((ts, W), F32), pltpu.VMEM((1, W), F32)],
        compiler_params=_cp(1), name=name,
    )(z, z, z, reset, conv_w, conv_b, w_r, b_r, w_i, b_i, lam)


def _lru_bwd(dy, z, xc, hseq, reset, w_r, b_r, w_i, b_i, lam, *, name):
    S = z.shape[0]
    ts = min(S, 512)
    nt = S // ts
    nh = ts // CONV_HALO
    W = D_MODEL

    def body(dy_ref, gate_ref, xc_ref, h_ref, hh_ref, rs_ref, wr_ref, br_ref, wi_ref, bi_ref, lam_ref,
             dg_ref, dxc_ref, dpr_ref, dpi_ref, acc_ref, a_buf, dh_buf, carry):
        i = pl.program_id(0)
        tile = nt - 1 - i

        @pl.when(i == 0)
        def _():
            carry[...] = jnp.zeros_like(carry)
            acc_ref[...] = jnp.zeros_like(acc_ref)

        xc = xc_ref[...]
        lam_v = lam_ref[...]
        sp = _softplus(-lam_v)
        reset = rs_ref[...] > 0.5
        r, ig, a, mult = _lru_gates(xc, wr_ref, br_ref[...], wi_ref, bi_ref[...], sp, reset)
        gate = gate_ref[...]
        dyv = dy_ref[...].astype(F32)
        h = h_ref[...]
        dg_ref[...] = (dyv * h * _gelu_grad(gate)).astype(dg_ref.dtype)
        last_row = lax.broadcasted_iota(jnp.int32, a.shape, 0) == ts - 1
        a_buf[...], dh_buf[...] = _compose_groups(jnp.where(last_row, 1.0, pltpu.roll(a, ts - 1, 0)),
                                                  dyv * _gelu(gate), True)
        _chain_groups(a_buf, dh_buf, jnp.broadcast_to(carry[...], (SUBLANES, W)), True)
        dh = dh_buf[...]
        carry[...] = a[0:1] * dh[0:1]
        hh = jnp.where(tile > 0, hh_ref[...], 0.0)
        h_prev = pltpu.roll(jnp.concatenate([hh, h], axis=0), 1, 0)[CONV_HALO:]
        da = dh * h_prev
        bx = ig * xc
        dmult = dh * bx
        dbx = dh * mult
        di = dbx * xc
        dlog_a = jnp.where(reset, 0.0, da * a - dmult * a * a / jnp.maximum(mult, 1e-30))
        dr = dlog_a * (-LRU_C) * sp
        dpre_r = dr * r * (1.0 - r)
        dpre_i = di * ig * (1.0 - ig)
        dprb, dpib = dpre_r.astype(BF16), dpre_i.astype(BF16)
        dpr_ref[...] = dprb
        dpi_ref[...] = dpib
        back = []
        for hd in range(LRU_HEADS):
            lo, hi = hd * LRU_HEAD_DIM, (hd + 1) * LRU_HEAD_DIM
            back.append(_dot(dprb[:, lo:hi], wr_ref[hd], NT) + _dot(dpib[:, lo:hi], wi_ref[hd], NT))
        dxc_ref[...] = dbx * ig + jnp.concatenate(back, axis=1)
        dlam = jnp.sum(dlog_a * (-LRU_C) * r, axis=0, keepdims=True) * (-_sigmoid(-lam_v))
        acc_ref[0:1, :] += jnp.sum(dpre_r, axis=0, keepdims=True)
        acc_ref[1:2, :] += jnp.sum(dpre_i, axis=0, keepdims=True)
        acc_ref[2:3, :] += dlam

    rev = lambda cb: pl.BlockSpec((ts, W), lambda i: (nt - 1 - i, cb))
    vec = _const((1, W))
    gw = _const((LRU_HEADS, LRU_HEAD_DIM, LRU_HEAD_DIM))
    return pl.pallas_call(
        body, grid=(nt,),
        in_specs=[rev(0), rev(0), rev(0), rev(0),
                  pl.BlockSpec((CONV_HALO, W), lambda i: (jnp.maximum((nt - 1 - i) * nh - 1, 0), 0)),
                  pl.BlockSpec((ts, 1), lambda i: (nt - 1 - i, 0)), gw, vec, gw, vec, vec],
        out_specs=[rev(0), rev(0), rev(0), rev(0), _const((8, W))],
        out_shape=[jax.ShapeDtypeStruct((S, W), BF16), jax.ShapeDtypeStruct((S, W), F32),
                   jax.ShapeDtypeStruct((S, W), BF16), jax.ShapeDtypeStruct((S, W), BF16),
                   jax.ShapeDtypeStruct((8, W), F32)],
        scratch_shapes=[pltpu.VMEM((ts, W), F32), pltpu.VMEM((ts, W), F32), pltpu.VMEM((1, W), F32)],
        compiler_params=_cp(1), name=name,
    )(dy, z, xc, hseq, hseq, reset, w_r, b_r, w_i, b_i, lam)


def _conv_bwd(dxc, z, conv_w, *, name):
    S = dxc.shape[0]
    ts = min(S, 512)
    nh = ts // CONV_HALO
    last = S // CONV_HALO - 1
    W = D_MODEL
    n = ts + CONV_HALO

    def body(d_ref, dn_ref, xb_ref, xp_ref, cw_ref, dxb_ref, acc_ref):
        i = pl.program_id(0)

        @pl.when(i == 0)
        def _():
            acc_ref[...] = jnp.zeros_like(acc_ref)

        d = d_ref[...]
        de = jnp.concatenate([d, jnp.where(i < pl.num_programs(0) - 1, dn_ref[...], 0.0)], axis=0)
        xe = jnp.concatenate([jnp.where(i > 0, xp_ref[...], 0.0), xb_ref[...]], axis=0)
        dxb = cw_ref[3:4, :] * d
        acc_ref[3:4, :] += jnp.sum(d * xe[CONV_HALO:], axis=0, keepdims=True)
        for kk in range(CONV_WIDTH - 1):
            sh = CONV_WIDTH - 1 - kk
            dxb = dxb + cw_ref[kk:kk + 1, :] * pltpu.roll(de, n - sh, 0)[:ts]
            acc_ref[kk:kk + 1, :] += jnp.sum(d * pltpu.roll(xe, sh, 0)[CONV_HALO:], axis=0, keepdims=True)
        dxb_ref[...] = dxb.astype(dxb_ref.dtype)
        acc_ref[4:5, :] += jnp.sum(d, axis=0, keepdims=True)

    return pl.pallas_call(
        body, grid=(S // ts,),
        in_specs=[_rows(ts, W), pl.BlockSpec((CONV_HALO, W), lambda i: (jnp.minimum((i + 1) * nh, last), 0)),
                  _rows(ts, W, 1), pl.BlockSpec((CONV_HALO, W), lambda i: (jnp.maximum(i * nh - 1, 0), 1)),
                  _const((CONV_WIDTH, W))],
        out_specs=[_rows(ts, W), _const((8, W))],
        out_shape=[jax.ShapeDtypeStruct((S, W), BF16), jax.ShapeDtypeStruct((8, W), F32)],
        compiler_params=_cp(1), name=name,
    )(dxc, dxc, z, z, conv_w)


def _loss_head(x, g, target, *, name):
    S, D = x.shape
    ts = _row_tile(S)

    def body(x_ref, g_ref, t_ref, dx_ref, dg_ref, l_ref):
        @pl.when(pl.program_id(0) == 0)
        def _():
            dg_ref[...] = jnp.zeros_like(dg_ref)
            l_ref[...] = jnp.zeros_like(l_ref)

        xv = x_ref[...]
        r = lax.rsqrt(jnp.mean(xv * xv, axis=-1, keepdims=True) + RMS_EPS)
        n = xv * r
        err = n * g_ref[...] - t_ref[...]
        l_ref[...] += 0.5 * jnp.sum(jnp.sum(err * err, axis=-1, keepdims=True) * (1.0 / D), axis=0, keepdims=True)
        dy = err * (1.0 / D)
        dn = dy * g_ref[...]
        dx_ref[...] = r * (dn - n * jnp.mean(dn * n, axis=-1, keepdims=True))
        dg_ref[...] += jnp.sum(dy * n, axis=0, keepdims=True)

    return pl.pallas_call(
        body, grid=(S // ts,), in_specs=[_rows(ts, D), _const((1, D)), _rows(ts, D)],
        out_specs=[_rows(ts, D), _const((1, D)), _const((8, LANES))],
        out_shape=[jax.ShapeDtypeStruct((S, D), F32), jax.ShapeDtypeStruct((1, D), F32),
                   jax.ShapeDtypeStruct((8, LANES), F32)],
        compiler_params=_cp(1), name=name,
    )(x, g.reshape(1, D), target)


def _adamw(w, ga, gb, m, v, *, name):
    shape = w.shape
    cols = shape[-1]
    rows = w.size // cols
    br = rows
    if rows * cols * 4 > (1 << 20):
        br = max(d for d in range(8, rows + 1, 8) if rows % d == 0 and d * cols * 4 <= (1 << 20))

    def body(w_ref, ga_ref, gb_ref, m_ref, v_ref, g_ref, d_ref, mo_ref, vo_ref):
        gv = ga_ref[...] + gb_ref[...]
        g_ref[...] = gv
        mn = ADAM_B1 * m_ref[...] + (1.0 - ADAM_B1) * gv
        vn = ADAM_B2 * v_ref[...] + (1.0 - ADAM_B2) * (gv * gv)
        m_hat = mn / (1.0 - ADAM_B1 ** ADAM_STEP)
        v_hat = vn / (1.0 - ADAM_B2 ** ADAM_STEP)
        d_ref[...] = -ADAM_LR * (m_hat / (jnp.sqrt(v_hat) + ADAM_EPS) + ADAM_WD * w_ref[...])
        mo_ref[...] = mn
        vo_ref[...] = vn

    spec = _rows(br, cols)
    outs = pl.pallas_call(
        body, grid=(rows // br,), in_specs=[spec] * 5, out_specs=[spec] * 4,
        out_shape=[jax.ShapeDtypeStruct((rows, cols), F32)] * 4, compiler_params=_cp(1), name=name,
    )(*[t.reshape(rows, cols) for t in (w, ga, gb, m, v)])
    return [o.reshape(shape) for o in outs]


def _pad_heads(w, width):
    k = w.shape[0]
    return jnp.pad(w.reshape(k, MLA_HEADS, width), ((0, 0), (0, 0), (0, HEAD_PAD - width))).reshape(k, -1)


def _unpad_heads(w, width):
    k = w.shape[0]
    return w.reshape(k, MLA_HEADS, HEAD_PAD)[:, :, :width].reshape(k, MLA_HEADS * width)


def _rope_tables(positions):
    inv_freq = ROPE_BASE ** (-jnp.arange(0, QK_ROPE, 2, dtype=F32) / QK_ROPE)
    ang = positions.astype(F32)[:, None] * inv_freq
    cos, sin = jnp.cos(ang), jnp.sin(ang)
    S = positions.shape[0]
    ones, zeros = jnp.ones((S, QK_NOPE), F32), jnp.zeros((S, QK_NOPE), F32)
    ctab = jnp.concatenate([ones, cos, cos, ones[:, :HEAD_PAD - QK_DIM]], axis=1)
    stab = jnp.concatenate([zeros, -sin, sin, zeros[:, :HEAD_PAD - QK_DIM]], axis=1)
    return ctab, stab


def _memory_block(x, mem, W, layer, tag):
    mn = _rms(mem, W["xa_norm_mem"][layer], name=f"{tag}_xa_norm_mem")
    kvm = _mm(mn, [(W["xa_w_kv"][layer], 0, 0)], _first, [(2 * D_MODEL, BF16, 0)], tn=2 * D_MODEL, nj=1,
              name=f"{tag}_xa_kv")[0]
    xo, hx, qx, o = _xa_block_fwd(x, kvm, W["xa_w_q"][layer], W["xa_w_o"][layer], W["xa_norm_x"][layer],
                                  name=f"{tag}_xa_fwd")
    return xo, (x, hx, qx, mn, kvm, o)


def _memory_block_bwd(dxo, mem, W, layer, saved, tag, grads):
    x, hx, qx, mn, kvm, o = saved
    wq, wkv, wo = W["xa_w_q"][layer], W["xa_w_kv"][layer], W["xa_w_o"][layer]
    grads["xa_w_o"][layer] = _owner_major(_mm_tn(o, dxo, name=f"{tag}_xa_dwo"), 0)
    dx, dqx, dkvm, dg = _xa_block_bwd(dxo, x, qx, kvm, wq, wo, W["xa_norm_x"][layer], name=f"{tag}_xa_bwd")
    grads["xa_w_q"][layer] = _owner_major(_mm_tn(hx, dqx, name=f"{tag}_xa_dwq"), 0)
    grads["xa_norm_x"][layer] = dg[0]
    dmn = _mm(dkvm, [(wkv, 0, 0)], _first, [(D_MODEL, F32, 0)], nt=True, tn=D_MODEL, nj=1, name=f"{tag}_xa_dmn")[0]
    grads["xa_w_kv"][layer] = _mm_tn_owners(mn, [dkvm], name=f"{tag}_xa_dwkv")
    _, dgm = _rms_bwd(mem, W["xa_norm_mem"][layer], dmn, name=f"{tag}_xa_norm_mem_bwd")
    grads["xa_norm_mem"][layer] = dgm[0]
    return dx


FF_TN = D_FF // 2

def _silu_mul(accs, extras):
    g, u = accs
    return [g * _sigmoid(g) * u, g, u]


def _silu_mul_bwd(accs, extras):
    da = accs[0]
    g, u = extras[0].astype(F32), extras[1].astype(F32)
    sg = _sigmoid(g)
    return [da * u * sg * (1.0 + g * (1.0 - sg)), da * g * sg]


def _ffn_block(x, W, layer, tag):
    hf = _rms(x, W["ffn_norm"][layer], name=f"{tag}_ffn_norm")
    wgu, wd = W["ffn_w_gate_up"][layer], W["ffn_w_down"][layer]
    act, g, u = _mm(hf, [(wgu, 0, 0), (wgu, 0, 2)], _silu_mul, [(D_FF, BF16, 0)] * 3, tn=FF_TN, nj=2,
                    name=f"{tag}_ffn_up")
    xo = _mm(act, [(wd, 0, 0)], _add_res, [(D_MODEL, F32, 0)], extras=[(x, 0)], tn=D_MODEL, nj=1,
             name=f"{tag}_ffn_down")[0]
    return xo, (x, hf, act, g, u)


def _ffn_block_bwd(dxo, W, layer, saved, tag, grads):
    x, hf, act, g, u = saved
    wgu, wd = W["ffn_w_gate_up"][layer], W["ffn_w_down"][layer]
    dg, du = _mm(dxo, [(wd, 0, 0)], _silu_mul_bwd, [(D_FF, BF16, 0)] * 2, nt=True, extras=[(g, 0), (u, 0)], tn=FF_TN,
                 nj=2, name=f"{tag}_ffn_dact")
    grads["ffn_w_down"][layer] = _owner_major(_mm_tn(act, dxo, tk=FF_TN, name=f"{tag}_ffn_dwd"), 0)
    dx, dgn = _mm(dg, [(wgu, 0, 0)], _norm_bwd_epilogue(0), [(D_MODEL, F32, 0)], nt=True, also=(du, (wgu, 0, 1)),
                  extras=[(x, 0), (dxo, 0)], rows=[W["ffn_norm"][layer].reshape(1, D_MODEL)],
                  sums=[D_MODEL], tn=D_MODEL, nj=1, name=f"{tag}_ffn_dhf")
    grads["ffn_w_gate_up"][layer] = _mm_tn_owners(hf, [dg, du], name=f"{tag}_ffn_dwgu")
    grads["ffn_norm"][layer] = dgn[0]
    return dx


def _even_block(x, tabs, W, tag):
    ctab, stab = tabs
    w_in = W["ev_w_in"][0]
    zero = jnp.zeros((D_MODEL, QK_NOPE), BF16)
    w_in_pad = jnp.concatenate([w_in[:, :896], zero, w_in[:, 896:], zero[:, :HEAD_PAD - QK_DIM]], axis=1)
    w_q_pad = _pad_heads(W["ev_w_q_up"][0], QK_DIM)
    wkv = W["ev_w_kv_up"][0].reshape(KV_RANK, MLA_HEADS, QK_NOPE + V_HEAD)
    w_kv_pad = jnp.concatenate([_pad_heads(wkv[:, :, :QK_NOPE].reshape(KV_RANK, -1), QK_NOPE),
                                _pad_heads(wkv[:, :, QK_NOPE:].reshape(KV_RANK, -1), V_HEAD)], axis=1)
    w_out = W["ev_w_out"][0]
    w_att = jnp.pad(w_out[POOL_DIM:].reshape(MLA_HEADS, V_HEAD, D_MODEL), ((0, 0), (0, HEAD_PAD - V_HEAD), (0, 0)))
    w_out_pad = jnp.concatenate([w_out[:POOL_DIM], w_att.reshape(MLA_HEADS * HEAD_PAD, D_MODEL)], axis=0)
    pool_w = W["ev_pool_w"][0].astype(BF16)
    pool_scale = W["ev_pool_scale"]

    h, z, mix, pooled, cqn, ckvn, q_rot, k_cat, v_pad = _even_front(
        x, W["ev_norm"][0], w_in_pad, pool_w, pool_scale, W["ev_q_norm"][0], w_q_pad, W["ev_kv_norm"][0], w_kv_pad,
        ctab, stab, name=f"{tag}_front")
    mix, lse = _flash_fwd(q_rot, k_cat, v_pad, mix, name=f"{tag}_attn")
    xo = _mm(mix, [(w_out_pad, 0, 0)], _add_res, [(D_MODEL, F32, 0)], extras=[(x, 0)], tn=D_MODEL, nj=1,
             name=f"{tag}_out")[0]
    saved = (x, h, z, pooled, cqn, ckvn, q_rot, k_cat, v_pad, lse, mix,
             (w_in_pad, w_q_pad, w_kv_pad, w_out_pad, pool_w, pool_scale))
    return xo, saved


def _even_out_grad(dxo, saved, tag):
    mix = saved[10]
    dw_out_pad = _mm_tn(mix, dxo, tk=MIX_DIM // 3, name=f"{tag}_dw_out")
    datt = dw_out_pad[POOL_DIM:].reshape(MLA_HEADS, HEAD_PAD, D_MODEL)[:, :V_HEAD].reshape(-1, D_MODEL)
    return [_owner_major(jnp.concatenate([dw_out_pad[:POOL_DIM], datt], axis=0), 0)]


def _even_block_bwd(dxo, tabs, W, saved, tag, grads, token=None):
    ctab, stab = tabs
    x, h, z, pooled, cqn, ckvn, q_rot, k_cat, v_pad, lse, mix, wts = saved
    w_in_pad, w_q_pad, w_kv_pad, w_out_pad, pool_w, pool_scale = wts
    if token is not None:
        w_out_pad = w_out_pad + token[0:1, 0:1].astype(BF16)
    dmix = _mm(dxo, [(w_out_pad, 0, 0)], _first, [(MIX_DIM, BF16, 0)], nt=True, tn=MIX_DIM, nj=1,
               name=f"{tag}_dmix")[0]
    delta = _attn_delta(dmix, mix, name=f"{tag}_delta")
    dq_rot, dk_cat, dv_pad = _flash_bwd(q_rot, k_cat, v_pad, dmix, _retile_rows(lse, delta.shape[2]), delta,
                                        name=f"{tag}_attn_bwd")
    dx, dq_pad, dz, dypre, dgq, dgkv, dscale, dgn = _even_back(
        dq_rot, dk_cat, dv_pad, dmix, pooled, z, x, dxo, ctab, stab, w_q_pad, w_kv_pad, w_in_pad, pool_w, pool_scale,
        W["ev_q_norm"][0], W["ev_kv_norm"][0], W["ev_norm"][0], name=f"{tag}_back")
    grads["ev_q_norm"], grads["ev_kv_norm"], grads["ev_pool_scale"], grads["ev_norm"] = dgq, dgkv, dscale, dgn
    dw_q_pad = _mm_tn(cqn, dq_pad, name=f"{tag}_dw_q_up")
    grads["ev_w_q_up"] = [_owner_major(_unpad_heads(dw_q_pad, QK_DIM), 1)]
    dwk = _unpad_heads(_mm_tn(ckvn, dk_cat, name=f"{tag}_dw_k_up"), QK_NOPE).reshape(KV_RANK, MLA_HEADS, QK_NOPE)
    dwv = _unpad_heads(_mm_tn(ckvn, dv_pad, name=f"{tag}_dw_v_up"), V_HEAD).reshape(KV_RANK, MLA_HEADS, V_HEAD)
    grads["ev_w_kv_up"] = [_owner_major(jnp.concatenate([dwk, dwv], axis=2).reshape(KV_RANK, -1), 1)]
    grads["ev_pool_w"] = _mm_tn_grouped(pooled, dypre, 4, POOL_GROUP, name=f"{tag}_dpool_w")[None]
    dw_in_pad = _mm_tn(h, dz, name=f"{tag}_dw_in")
    grads["ev_w_in"] = [_owner_major(jnp.concatenate([dw_in_pad[:, :896], dw_in_pad[:, 960:992]], axis=1), 0)]
    return dx


def _odd_block(x, reset, W, tag):
    h = _rms(x, W["od_norm"][0], name=f"{tag}_norm")
    z = _mm(h, [(W["od_w_in"][0], 0, 0)], _first, [(2 * D_MODEL, F32, 0)], tn=D_MODEL, nj=2, name=f"{tag}_in")[0]
    w_r, w_i = W["od_w_rgate"][0], W["od_w_igate"][0]
    vecs = [W[n].reshape(1, D_MODEL) for n in ("od_conv_b", "od_b_rgate", "od_b_igate", "od_lambda")]
    xc, hseq, y = _lru_fwd(z, reset, W["od_conv_w"][0], vecs[0], w_r, vecs[1], w_i, vecs[2], vecs[3],
                           name=f"{tag}_lru")
    xo = _mm(y, [(W["od_w_out"][0], 0, 0)], _add_res, [(D_MODEL, F32, 0)], extras=[(x, 0)], tn=D_MODEL, nj=1,
             name=f"{tag}_out")[0]
    return xo, (x, h, z, xc, hseq, y, vecs)


def _odd_block_bwd(dxo, reset, W, saved, tag, grads):
    x, h, z, xc, hseq, y, vecs = saved
    w_r, w_i = W["od_w_rgate"][0], W["od_w_igate"][0]
    dy = _mm(dxo, [(W["od_w_out"][0], 0, 0)], _first, [(D_MODEL, F32, 0)], nt=True, tn=D_MODEL, nj=1,
             name=f"{tag}_dy")[0]
    grads["od_w_out"] = [_owner_major(_mm_tn(y, dxo, name=f"{tag}_dw_out"), 0)]
    dgate, dxc, dpr, dpi, acc = _lru_bwd(dy, z, xc, hseq, reset, w_r, vecs[1], w_i, vecs[2], vecs[3],
                                         name=f"{tag}_lru_bwd")
    grads["od_b_rgate"], grads["od_b_igate"], grads["od_lambda"] = acc[0:1], acc[1:2], acc[2:3]
    grads["od_w_rgate"] = [_owner_major(_mm_tn_grouped(xc, dpr, LRU_HEADS, LRU_HEAD_DIM, name=f"{tag}_dw_rgate"), 1)]
    grads["od_w_igate"] = [_owner_major(_mm_tn_grouped(xc, dpi, LRU_HEADS, LRU_HEAD_DIM, name=f"{tag}_dw_igate"), 1)]
    dxb, cacc = _conv_bwd(dxc, z, W["od_conv_w"][0], name=f"{tag}_conv_bwd")
    grads["od_conv_w"], grads["od_conv_b"] = cacc[None, 0:4], cacc[4:5]
    dz = jnp.concatenate([dgate, dxb], axis=1)
    grads["od_w_in"] = [_mm_tn_owners(h, [dz], name=f"{tag}_dw_in")]
    dx, dgn = _mm(dz, [(W["od_w_in"][0], 0, 0)], _norm_bwd_epilogue(0), [(D_MODEL, F32, 0)], nt=True,
                  extras=[(x, 0), (dxo, 0)], rows=[W["od_norm"][0].reshape(1, D_MODEL)], sums=[D_MODEL], tn=D_MODEL,
                  nj=1, name=f"{tag}_dh")
    grads["od_norm"] = dgn
    return dx


def _local_step(x, mem, positions, target, W, later_weights=None, exchange_earlier=None):
    tabs = _rope_tables(positions)
    reset = (positions == 0).astype(F32)[:, None]
    grads = {n: [None, None] for n in ("xa_norm_x", "xa_norm_mem", "xa_w_q", "xa_w_kv", "xa_w_o", "ffn_norm",
                                       "ffn_w_gate_up", "ffn_w_down")}
    x1, s_even = _even_block(x, tabs, W, "l0_even")
    if later_weights is not None:
        W = {**W, **later_weights(x1)}
    x2, s_xa0 = _memory_block(x1, mem, W, 0, "l0")
    x3, s_ff0 = _ffn_block(x2, W, 0, "l0")
    x4, s_odd = _odd_block(x3, reset, W, "l1_odd")
    x5, s_xa1 = _memory_block(x4, mem, W, 1, "l1")
    x6, s_ff1 = _ffn_block(x5, W, 1, "l1")
    d, dgf, loss = _loss_head(x6, W["final_norm"], target, name="loss_head")
    grads["final_norm"] = dgf[0]
    d = _ffn_block_bwd(d, W, 1, s_ff1, "l1", grads)
    d = _memory_block_bwd(d, mem, W, 1, s_xa1, "l1", grads)
    d = _odd_block_bwd(d, reset, W, s_odd, "l1_odd", grads)
    d = _ffn_block_bwd(d, W, 0, s_ff0, "l0", grads)
    d = _memory_block_bwd(d, mem, W, 0, s_xa0, "l0", grads)
    grads["ev_w_out"] = _even_out_grad(d, s_even, "l0_even")
    token = exchange_earlier(grads) if exchange_earlier is not None else None
    d = _even_block_bwd(d, tabs, W, s_even, "l0_even", grads, token)
    big = {n: grads.pop(n) for n in MATMUL_WEIGHTS}
    for n, v in grads.items():
        if isinstance(v, list):
            grads[n] = jnp.stack(v)
    return loss[0, 0], d, big, grads


WEIGHTS = ("ev_norm", "ev_w_in", "ev_pool_w", "ev_pool_scale", "ev_q_norm", "ev_w_q_up", "ev_kv_norm", "ev_w_kv_up",
           "ev_w_out", "od_norm", "od_w_in", "od_conv_w", "od_conv_b", "od_w_rgate", "od_b_rgate", "od_w_igate",
           "od_b_igate", "od_lambda", "od_w_out", "xa_norm_x", "xa_norm_mem", "xa_w_q", "xa_w_kv", "xa_w_o",
           "ffn_norm", "ffn_w_gate_up", "ffn_w_down", "final_norm")
SHARD_AXIS = {"ev_w_in": 1, "ev_w_q_up": 2, "ev_w_kv_up": 2, "ev_w_out": 1, "od_norm": 1, "od_w_in": 2,
              "od_conv_w": 2, "od_conv_b": 1, "od_w_rgate": 2, "od_b_rgate": 1, "od_w_igate": 2, "od_b_igate": 1,
              "od_lambda": 1, "od_w_out": 1, "xa_w_q": 1, "xa_w_kv": 2, "xa_w_o": 1, "ffn_w_gate_up": 2,
              "ffn_w_down": 1}
MATMUL_WEIGHTS = ("ev_w_in", "ev_w_q_up", "ev_w_kv_up", "ev_w_out", "od_w_in", "od_w_rgate", "od_w_igate",
                  "od_w_out", "xa_w_q", "xa_w_kv", "xa_w_o", "ffn_w_gate_up", "ffn_w_down")
SMALL_SHARDED = tuple(n for n in WEIGHTS if n in SHARD_AXIS and n not in MATMUL_WEIGHTS)
REPLICATED = tuple(n for n in WEIGHTS if n not in SHARD_AXIS)


def _pack(parts, quantum):
    flat = jnp.concatenate([p.reshape(-1) for p in parts])
    pad = (-flat.shape[0]) % quantum
    return jnp.pad(flat, (0, pad)).reshape(-1, LANES)


def _unpack(flat, shapes):
    out, off = [], 0
    for shape in shapes:
        size = math.prod(shape)
        out.append(flat[off:off + size].reshape(shape))
        off += size
    return out


def _run_copies(local, remote, send_sems, recv_sems, local_sems):
    locals_ = [pltpu.make_async_copy(src, dst, local_sems.at[n]) for n, (src, dst) in enumerate(local)]
    for cp in locals_:
        cp.start()
    sends = [pltpu.make_async_remote_copy(src_ref=src, dst_ref=dst, send_sem=send_sems.at[k, n],
                                          recv_sem=recv_sems.at[k, n], device_id=dev, device_id_type=MESH)
             for (k, n, src, dst, _, dev) in remote]
    for cp in sends:
        cp.start()
    for (k, n, src, _, arrival, dev) in remote:
        pltpu.make_async_remote_copy(src_ref=src, dst_ref=arrival, send_sem=send_sems.at[k, n],
                                     recv_sem=recv_sems.at[k, n], device_id=dev, device_id_type=MESH).wait_recv()
    for cp in sends:
        cp.wait_send()
    for cp in locals_:
        cp.wait()


def _chip_peers(x, y):
    return [(1 - x, y), (x, 1 - y), (1 - x, 1 - y)]


def _owner_block(ref, axis, q):
    size = ref.shape[axis] // N_CHIPS
    idx = [slice(None)] * len(ref.shape)
    idx[axis] = pl.ds(q * size, size)
    return ref.at[tuple(idx)]


def _comm_call(body, ins, out_shapes, n_items, n_peers, *, name):
    return pl.pallas_call(
        body, in_specs=[ANY] * len(ins), out_specs=[ANY] * len(out_shapes), out_shape=out_shapes,
        scratch_shapes=[pltpu.SemaphoreType.DMA((n_peers, n_items)), pltpu.SemaphoreType.DMA((n_peers, n_items)),
                        pltpu.SemaphoreType.DMA((n_items,))],
        name=name,
    )(*ins)


def _gather_chips(shards, axes, *, name):
    n = len(shards)
    full = [jax.ShapeDtypeStruct(tuple(d * (N_CHIPS if a == ax else 1) for a, d in enumerate(s.shape)), s.dtype)
            for s, ax in zip(shards, axes)]

    def body(*refs):
        srcs, dsts = refs[:n], refs[n:2 * n]
        x, y, c = lax.axis_index("x"), lax.axis_index("y"), lax.axis_index("c")
        me = 2 * x + y
        local = [(srcs[i], _owner_block(dsts[i], axes[i], me)) for i in range(n)]
        remote = [(k, i, srcs[i], _owner_block(dsts[i], axes[i], me), _owner_block(dsts[i], axes[i], 2 * px + py),
                   (px, py, c))
                  for k, (px, py) in enumerate(_chip_peers(x, y)) for i in range(n)]
        _run_copies(local, remote, *refs[2 * n:])

    return _comm_call(body, shards, full, n, 3, name=name)


HBM = pl.BlockSpec(memory_space=pltpu.HBM)
SEM = pl.BlockSpec(memory_space=pltpu.SEMAPHORE)
DATAFLOW = pltpu.SideEffectType.DATAFLOW_SIDE_EFFECTING


def _gather_plan(axes):
    return lambda srcs, lands, me, peer: [
        (srcs[i], _owner_block(lands[i], ax, me), _owner_block(lands[i], ax, peer)) for i, ax in enumerate(axes)]


def _exchange_plan(where):
    return lambda srcs, lands, me, peer: [
        (srcs[i].at[peer], lands[n].at[me, l], lands[n].at[peer, l]) for i, (n, l) in enumerate(where)]


def _split_peers(sibling):
    x, y, c = lax.axis_index("x"), lax.axis_index("y"), lax.axis_index("c")
    peers = [((px, py, c), 2 * px + py) for px, py in _chip_peers(x, y)]
    return 2 * x + y, peers + ([((x, y, 1 - c), 2 * x + y)] if sibling else [])


def _split_start(srcs, lands, plan, *, sibling=False, name):
    ns, nl = len(srcs), len(lands)
    nsem = (3 + sibling) * len(plan(list(srcs), list(lands), 0, 0))

    def body(*refs):
        src_refs, land_refs = refs[:ns], refs[ns:ns + nl]
        send_sems, recv_sems = refs[ns + nl:ns + nl + nsem], refs[ns + nl + nsem:ns + nl + 2 * nsem]
        me, peers = _split_peers(sibling)
        n = 0
        for device, chip in peers:
            for src, dst, _ in plan(src_refs, land_refs, me, chip):
                pltpu.make_async_remote_copy(src_ref=src, dst_ref=dst, send_sem=send_sems[n], recv_sem=recv_sems[n],
                                             device_id=device, device_id_type=MESH).start()
                n += 1
        refs[-1][...] = jnp.zeros_like(refs[-1])

    arrays = list(srcs) + list(lands)
    out = pl.pallas_call(
        body, name=name, in_specs=[HBM] * (ns + nl),
        out_specs=[SEM] * (2 * nsem) + [HBM] * (ns + nl) + [pl.BlockSpec(memory_space=pltpu.VMEM)],
        out_shape=[pltpu.SemaphoreType.DMA(())] * (2 * nsem) + [pltpu.HBM(a.shape, a.dtype) for a in arrays]
        + [jax.ShapeDtypeStruct((8, LANES), F32)],
        input_output_aliases={i: 2 * nsem + i for i in range(ns + nl)},
        compiler_params=pltpu.CompilerParams(has_side_effects=DATAFLOW),
    )(*[pltpu.with_memory_space_constraint(a, pltpu.HBM) for a in arrays])
    sems, rest = out[:2 * nsem], out[2 * nsem:]
    return sems[:nsem], sems[nsem:], rest[:ns], rest[ns:ns + nl], rest[-1]


def _split_wait(handle, after, plan, *, sibling=False, name):
    send_sems, recv_sems, srcs, lands, _ = handle
    ns, nl, nsem = len(srcs), len(lands), len(send_sems)

    def body(*refs):
        src_refs, land_refs = refs[:ns], refs[ns:ns + nl]
        send_refs, recv_refs = refs[ns + nl:ns + nl + nsem], refs[ns + nl + nsem:ns + nl + 2 * nsem]
        me, peers = _split_peers(sibling)
        n = 0
        for device, chip in peers:
            for src, _, arrival in plan(src_refs, land_refs, me, chip):
                cp = pltpu.make_async_remote_copy(src_ref=src, dst_ref=arrival, send_sem=send_refs[n],
                                                  recv_sem=recv_refs[n], device_id=device, device_id_type=MESH)
                cp.wait_send()
                cp.wait_recv()
                n += 1

    out = pl.pallas_call(
        body, name=name, in_specs=[HBM] * (ns + nl) + [SEM] * (2 * nsem) + [ANY], out_specs=[HBM] * (ns + nl),
        out_shape=[pltpu.HBM(a.shape, a.dtype) for a in list(srcs) + list(lands)],
        input_output_aliases={i: i for i in range(ns + nl)},
        compiler_params=pltpu.CompilerParams(has_side_effects=DATAFLOW),
    )(*srcs, *lands, *send_sems, *recv_sems, after)
    return out[ns:]


def _exchange_sibling(arrays, *, name):
    n = len(arrays)

    def body(*refs):
        x, y, c = lax.axis_index("x"), lax.axis_index("y"), lax.axis_index("c")
        remote = [(0, i, refs[i], refs[n + i], refs[n + i], (x, y, 1 - c)) for i in range(n)]
        _run_copies([], remote, *refs[2 * n:])

    return _comm_call(body, arrays, [jax.ShapeDtypeStruct(a.shape, a.dtype) for a in arrays], n, 1, name=name)


def _sum_slots(r, *, token=None, name):
    shape = r.shape[1:]
    cols = shape[-1]
    rows = math.prod(shape) // cols
    tr = max(d for d in range(8, rows + 1, 8) if rows % d == 0 and d * cols * 16 <= (4 << 20))

    def body(r_ref, *refs):
        total = ((r_ref[0] + r_ref[1]) + r_ref[2]) + r_ref[3]
        refs[-1][...] = total if token is None else total + refs[0][0:1, 0:1]

    in_specs = [pl.BlockSpec((N_CHIPS, tr, cols), lambda i: (0, i, 0))]
    in_specs += [] if token is None else [_const((8, LANES))]
    return pl.pallas_call(
        body, grid=(rows // tr,), in_specs=in_specs,
        out_specs=_rows(tr, cols), out_shape=jax.ShapeDtypeStruct((rows, cols), F32), compiler_params=_cp(1),
        name=name,
    )(r.reshape(N_CHIPS, rows, cols), *([] if token is None else [token])).reshape(shape)


FIRST_WEIGHTS = ("ev_w_in", "ev_w_q_up", "ev_w_kv_up", "ev_w_out")
LATER_WEIGHTS = tuple(n for n in MATMUL_WEIGHTS if n not in FIRST_WEIGHTS)
LAST_GRADS = ("ev_w_in", "ev_w_q_up", "ev_w_kv_up")
EARLIER_GRADS = tuple(n for n in MATMUL_WEIGHTS if n not in LAST_GRADS)


def _my_chip():
    return 2 * lax.axis_index("x") + lax.axis_index("y")


def _gather_first(w):
    small = _pack([w[n] for n in SMALL_SHARDED], 8 * LANES)
    stacked = [n for n in FIRST_WEIGHTS if SHARD_AXIS[n] == w[n].ndim - 1 and w[n].shape[-1] % LANES]
    shards = [w[n].astype(BF16)[None] if n in stacked else w[n].astype(BF16) for n in FIRST_WEIGHTS]
    got = _gather_chips(shards + [small], [0 if n in stacked else SHARD_AXIS[n] for n in FIRST_WEIGHTS] + [0],
                        name="gather_first")
    full = {n: w[n] for n in REPLICATED}
    for n, g in zip(FIRST_WEIGHTS, got[:-1]):
        full[n] = jnp.concatenate([g[q] for q in range(N_CHIPS)], axis=SHARD_AXIS[n]) if n in stacked else g
    per_chip = [_unpack(got[-1][q * small.shape[0]:(q + 1) * small.shape[0]].reshape(-1),
                        [w[n].shape for n in SMALL_SHARDED]) for q in range(N_CHIPS)]
    for i, n in enumerate(SMALL_SHARDED):
        full[n] = jnp.concatenate([per_chip[q][i] for q in range(N_CHIPS)], axis=SHARD_AXIS[n])
    return full


def _gather_later_start(w, after):
    behind = (after.reshape(-1)[0] * 0).astype(BF16)
    shards = [w[n].astype(BF16) + (behind if n == "od_w_rgate" else 0) for n in LATER_WEIGHTS]
    axes = [SHARD_AXIS[n] for n in LATER_WEIGHTS]
    lands = [lax.empty(tuple(d * (N_CHIPS if a == ax else 1) for a, d in enumerate(s.shape)), s.dtype)
             for s, ax in zip(shards, axes)]
    plan = _gather_plan(axes)
    return _split_start(shards, lands, plan, sibling=True, name="gather_later_start"), plan


def _owner_major(g, axis):
    shape = g.shape
    size = shape[axis] // N_CHIPS
    g = jnp.moveaxis(g.reshape(shape[:axis] + (N_CHIPS, size) + shape[axis + 1:]), axis, 0)
    return g.reshape(N_CHIPS, -1, shape[-1] if axis < len(shape) - 1 else size)


def _exchange_start(items, *, cross, name):
    me = _my_chip()
    srcs, lands, where = [], [], []
    for n, layers in enumerate(items):
        land = lax.empty((N_CHIPS, len(layers)) + layers[0].shape[1:], layers[0].dtype)
        for l, a in enumerate(layers):
            if not cross:
                own = lax.dynamic_index_in_dim(a, me, 0, keepdims=True)[:, None]
                land = lax.dynamic_update_slice(land, own, (me, l) + (0,) * (a.ndim - 1))
            srcs.append(a)
            where.append((n, l))
        lands.append(land)
    plan = _exchange_plan(where)
    return _split_start(srcs, lands, plan, sibling=cross, name=name), plan


def _earlier_items(grads, full_shapes):
    small = [_pack([jnp.split(grads[n].reshape(full_shapes[n]), N_CHIPS, axis=SHARD_AXIS[n])[q]
                    for n in SMALL_SHARDED], 8 * LANES) for q in range(N_CHIPS)]
    return [grads[n] for n in EARLIER_GRADS] + [[jnp.stack(small)]]


def _last_items(big, grads, full_shapes, loss):
    repl = _pack([grads[n].reshape(full_shapes[n]) for n in REPLICATED] + [loss.reshape(1)], 8 * LANES)
    return [big[n] for n in LAST_GRADS] + [[jnp.stack([repl] * N_CHIPS)]]


def kernel(
        x, mem, positions, ev_norm, ev_w_in, ev_pool_w, ev_pool_scale, ev_q_norm, ev_w_q_up, ev_kv_norm,
        ev_w_kv_up, ev_w_out, od_norm, od_w_in, od_conv_w, od_conv_b, od_w_rgate, od_b_rgate, od_w_igate,
        od_b_igate, od_lambda, od_w_out, xa_norm_x, xa_norm_mem, xa_w_q, xa_w_kv, xa_w_o, ffn_norm,
        ffn_w_gate_up, ffn_w_down, final_norm, loss_target, m_ev_norm, m_ev_w_in, m_ev_pool_w, m_ev_pool_scale,
        m_ev_q_norm, m_ev_w_q_up, m_ev_kv_norm, m_ev_w_kv_up, m_ev_w_out, m_od_norm, m_od_w_in, m_od_conv_w,
        m_od_conv_b, m_od_w_rgate, m_od_b_rgate, m_od_w_igate, m_od_b_igate, m_od_lambda, m_od_w_out,
        m_xa_norm_x, m_xa_norm_mem, m_xa_w_q, m_xa_w_kv, m_xa_w_o, m_ffn_norm, m_ffn_w_gate_up, m_ffn_w_down,
        m_final_norm, v_ev_norm, v_ev_w_in, v_ev_pool_w, v_ev_pool_scale, v_ev_q_norm, v_ev_w_q_up,
        v_ev_kv_norm, v_ev_w_kv_up, v_ev_w_out, v_od_norm, v_od_w_in, v_od_conv_w, v_od_conv_b, v_od_w_rgate,
        v_od_b_rgate, v_od_w_igate, v_od_b_igate, v_od_lambda, v_od_w_out, v_xa_norm_x, v_xa_norm_mem, v_xa_w_q,
        v_xa_w_kv, v_xa_w_o, v_ffn_norm, v_ffn_w_gate_up, v_ffn_w_down, v_final_norm):
    given = dict(locals())
    w = {n: given[n] for n in WEIGHTS}
    full_shapes = {n: tuple(d * (N_CHIPS if a == SHARD_AXIS.get(n) else 1) for a, d in enumerate(w[n].shape))
                   for n in WEIGHTS}
    full = _gather_first(w)
    later, later_plan = _gather_later_start(w, full["ev_w_out"])
    full["ev_norm"] = full["ev_norm"] + later[4][0:1, 0:1]
    exchange = {}

    def later_weights(after):
        return dict(zip(LATER_WEIGHTS, _split_wait(later, after, later_plan, sibling=True, name="gather_later_wait")))

    def exchange_earlier(grads):
        exchange["handle"], exchange["plan"] = _exchange_start(_earlier_items(grads, full_shapes), cross=True,
                                                               name="exchange_earlier_start")
        return exchange["handle"][4]

    loss, grad_x, big, grads = _local_step(x[0], mem[0], positions[0], loss_target[0], full, later_weights,
                                           exchange_earlier)
    earlier = EARLIER_GRADS + ("small",)
    got = dict(zip(earlier, _split_wait(exchange["handle"], grad_x, exchange["plan"], sibling=True,
                                        name="exchange_earlier_wait")))
    last, last_plan = _exchange_start(_last_items(big, grads, full_shapes, loss), cross=False,
                                      name="exchange_last_start")
    sums = {n: _sum_slots(got[n], token=last[4] if i == 0 else None, name=f"sum_chips_{n}")
            for i, n in enumerate(earlier)}
    got = dict(zip(LAST_GRADS + ("replicated",),
                   _split_wait(last, sums[earlier[-1]], last_plan, name="exchange_last_wait")))
    sums.update({n: _sum_slots(got[n], name=f"sum_chips_{n}") for n in got})
    mine = [sums[n] for n in MATMUL_WEIGHTS + ("small", "replicated")]
    other = _exchange_sibling(mine, name="exchange_sibling")
    out = {}
    for i, n in enumerate(MATMUL_WEIGHTS):
        out[n] = _adamw(w[n], mine[i].reshape(w[n].shape), other[i].reshape(w[n].shape), given["m_" + n],
                        given["v_" + n], name=f"adamw_{n}")
    for i, group in ((len(MATMUL_WEIGHTS), SMALL_SHARDED), (len(MATMUL_WEIGHTS) + 1, REPLICATED)):
        spare = [jnp.zeros((1,), F32)] if group is REPLICATED else []
        packed = [_pack([given[pre + n] for n in group] + spare, 8 * LANES) for pre in ("", "m_", "v_")]
        res = _adamw(packed[0], mine[i].reshape(packed[0].shape), other[i].reshape(packed[0].shape), packed[1],
                     packed[2], name=f"adamw_group{i}")
        shapes = [w[n].shape for n in group] + [(1,)] * len(spare)
        for j, arrs in enumerate(zip(*[_unpack(r.reshape(-1), shapes) for r in res])):
            if j < len(group):
                out[group[j]] = list(arrs)
            else:
                loss = arrs[0][0]
    return (loss, grad_x[None], *[out[n][k] for k in range(4) for n in WEIGHTS])
```

```python
import functools
import math

import jax
import jax.numpy as jnp
from jax import lax
from jax.experimental import pallas as pl
from jax.experimental.pallas import tpu as pltpu

F32 = jnp.float32
BF16 = jnp.bfloat16

D_MODEL = 1024
POOL_DIM = 512
POOL_WINDOWS = (2, 4, 8, 16)
POOL_GROUP = 128
MLA_HEADS = 8
QK_NOPE = 64
QK_ROPE = 32
QK_DIM = QK_NOPE + QK_ROPE
V_HEAD = 64
HEAD_PAD = 128
Q_RANK = 256
KV_RANK = 128
ROPE_BASE = 10000.0
LRU_HEADS = 4
LRU_HEAD_DIM = 256
CONV_WIDTH = 4
LRU_C = 8.0
MEM_HEADS = 4
MEM_HEAD_DIM = 256
D_FF = 2816
RMS_EPS = 1e-6
NEG_INF = -1e30

ADAM_LR = 0.001
ADAM_B1 = 0.9
ADAM_B2 = 0.999
ADAM_EPS = 1e-08
ADAM_WD = 0.01
ADAM_STEP = 10

N_CHIPS = 4
LANES = 128
VMEM_LIMIT = 56 * 1024 * 1024
MESH = pl.DeviceIdType.MESH
ANY = pl.BlockSpec(memory_space=pl.ANY)
MIX_DIM = POOL_DIM + MLA_HEADS * HEAD_PAD

NN = (((1,), (0,)), ((), ()))
NT = (((1,), (1,)), ((), ()))
TN = (((0,), (0,)), ((), ()))


def _cp(n):
    return pltpu.CompilerParams(dimension_semantics=("arbitrary",) * n, vmem_limit_bytes=VMEM_LIMIT)


def _dot(a, b, dims=NN):
    return lax.dot_general(a, b, dims, preferred_element_type=F32)


def _row_tile(S):
    return 1024 if S % 1024 == 0 else min(S, 512)


def _rows(ts, w, cb=0):
    return pl.BlockSpec((ts, w), lambda i: (i, cb))


def _const(shape):
    return pl.BlockSpec(shape, lambda i: (0,) * len(shape))


MM_VMEM_BUDGET = 40 * 1024 * 1024


def _mm(a, bs, epi, outs, *, tn, nj, nt=False, also=None, extras=(), rows=(), sums=(), a_cb=0, k=None, tm=None,
        name):
    M = a.shape[0]
    k = k or a.shape[1]
    nb, ne, nr, no = len(bs), len(extras), len(rows), len(outs)
    lhs = [(a, k, a_cb, b) for b in bs[:1]] + ([(also[0], also[0].shape[1], 0, also[1])] if also else [])
    if tm is None:
        per_row = 2 * (sum(kk * x.dtype.itemsize for x, kk, _, _ in lhs)
                       + sum(e.dtype.itemsize for e, _ in extras) * tn
                       + sum(jnp.dtype(dt).itemsize for _, dt, _ in outs) * tn) + nb * tn * 4
        weights = (1 if nj == 1 else 2) * (sum(b.dtype.itemsize for b, _, _ in bs) * k
                                           + (also[1][0].dtype.itemsize * lhs[-1][1] if also else 0)) * tn
        tm = 1024 if M % 1024 == 0 and 1024 * per_row + weights <= MM_VMEM_BUDGET else min(M, 512)
    dims = NT if nt else NN
    assert not sums or nj == 1
    na = 2 if also else 0

    def body(*refs):
        av = refs[0][...].astype(BF16)
        accs = [_dot(av, r[...].astype(BF16), dims) for r in refs[1:1 + nb]]
        if also:
            accs[0] = accs[0] + _dot(refs[1 + nb][...].astype(BF16), refs[2 + nb][...].astype(BF16), dims)
        refs = refs[:1 + nb] + refs[1 + nb + na:]
        vals = epi(accs, [r[...] for r in refs[1 + nb:1 + nb + ne + nr]])
        outs_refs = refs[1 + nb + ne + nr:]
        for o, v in zip(outs_refs[:no], vals[:no]):
            o[...] = v.astype(o.dtype)
        if sums:
            @pl.when(pl.program_id(1) == 0)
            def _():
                for o in outs_refs[no:]:
                    o[...] = jnp.zeros_like(o)

            for o, v in zip(outs_refs[no:], vals[no:]):
                o[...] += v

    in_specs = [pl.BlockSpec((tm, k), lambda j, i: (i, a_cb))]
    weights = [(k, rb, cb) for (_, rb, cb) in bs]
    if also:
        in_specs_also = pl.BlockSpec((tm, lhs[-1][1]), lambda j, i: (i, 0))
        weights.append((lhs[-1][1], also[1][1], also[1][2]))
    for n, (kk, rb, cb) in enumerate(weights):
        if also and n == nb:
            in_specs.append(in_specs_also)
        mode = dict(pipeline_mode=pl.Buffered(1)) if nj == 1 else {}
        if nt:
            in_specs.append(pl.BlockSpec((tn, kk), lambda j, i, rb=rb, cb=cb: (rb + j, cb), **mode))
        else:
            in_specs.append(pl.BlockSpec((kk, tn), lambda j, i, rb=rb, cb=cb: (rb, cb + j), **mode))
    for (_, cb) in extras:
        in_specs.append(pl.BlockSpec((tm, tn), lambda j, i, cb=cb: (i, cb + j)))
    in_specs += [pl.BlockSpec((1, tn), lambda j, i: (0, 0))] * nr
    out_specs = [pl.BlockSpec((tm, tn), lambda j, i, cb=cb: (i, cb + j)) for (_, _, cb) in outs]
    out_specs += [pl.BlockSpec((1, w), lambda j, i: (0, 0)) for w in sums]
    res = pl.pallas_call(
        body, grid=(nj, M // tm), in_specs=in_specs, out_specs=out_specs,
        out_shape=[jax.ShapeDtypeStruct((M, n), dt) for (n, dt, _) in outs]
        + [jax.ShapeDtypeStruct((1, w), F32) for w in sums],
        compiler_params=_cp(2), name=name,
    )(a, *[b for (b, _, _) in bs], *([also[0], also[1][0]] if also else []), *[e for (e, _) in extras], *rows)
    return res


def _first(accs, extras):
    return [accs[0]]


def _add_res(accs, extras):
    return [accs[0] + extras[0].astype(F32)]


def _norm_bwd_epilogue(partials):
    def epi(accs, vals):
        dh = accs[0]
        for part in vals[:partials]:
            dh = dh + part.astype(F32)
        x, res, g = vals[partials:partials + 3]
        r = lax.rsqrt(jnp.mean(x * x, axis=-1, keepdims=True) + RMS_EPS)
        n = x * r
        dn = dh * g
        return [r * (dn - n * jnp.mean(dn * n, axis=-1, keepdims=True)) + res, jnp.sum(dh * n, axis=0, keepdims=True)]

    return epi


TN_VMEM_BUDGET = 36 * 1024 * 1024


def _contraction_rows(S, row_bytes, out_elems):
    ts = min(S, 2048)
    while ts > 512 and 2 * (ts * row_bytes + out_elems * 4) > TN_VMEM_BUDGET:
        ts //= 2
    return ts


def _mm_tn(a, b, *, ka=None, a_cb=0, nb=None, b_cb=0, tk=None, tn=None, ts=None, name):
    S = a.shape[0]
    ka = ka or a.shape[1]
    nb = nb or b.shape[1]
    tk = tk or ka
    tn = tn or nb
    ts = ts or _contraction_rows(S, tk * a.dtype.itemsize + tn * b.dtype.itemsize, tk * tn)
    a0, b0 = a_cb * (ka // tk), b_cb * (nb // tn)

    def body(a_ref, b_ref, o_ref):
        @pl.when(pl.program_id(2) == 0)
        def _():
            o_ref[...] = jnp.zeros_like(o_ref)

        o_ref[...] += _dot(a_ref[...].astype(BF16), b_ref[...].astype(BF16), TN)

    return pl.pallas_call(
        body, grid=(ka // tk, nb // tn, S // ts),
        in_specs=[pl.BlockSpec((ts, tk), lambda p, q, s: (s, a0 + p)),
                  pl.BlockSpec((ts, tn), lambda p, q, s: (s, b0 + q))],
        out_specs=pl.BlockSpec((tk, tn), lambda p, q, s: (p, q)),
        out_shape=jax.ShapeDtypeStruct((ka, nb), F32), compiler_params=_cp(3), name=name,
    )(a, b)


def _mm_tn_owners(a, bs, *, name):
    S, ka = a.shape
    nb = sum(b.shape[1] for b in bs)
    tn = nb // N_CHIPS
    ts = _contraction_rows(S, ka * a.dtype.itemsize + len(bs) * tn * bs[0].dtype.itemsize, ka * tn)
    per = N_CHIPS // len(bs)

    def body(a_ref, *refs):
        o_ref = refs[-1]
        q = pl.program_id(0)

        @pl.when(pl.program_id(1) == 0)
        def _():
            o_ref[...] = jnp.zeros_like(o_ref)

        av = a_ref[...].astype(BF16)
        for n, b_ref in enumerate(refs[:-1]):
            @pl.when(q // per == n)
            def _():
                o_ref[0] += _dot(av, b_ref[...].astype(BF16), TN)

    in_specs = [pl.BlockSpec((ts, ka), lambda q, s: (s, 0))]
    for n in range(len(bs)):
        in_specs.append(pl.BlockSpec((ts, tn), lambda q, s, n=n: (jnp.where(q // per == n, s, 0),
                                                                  jnp.clip(q - n * per, 0, per - 1))))
    return pl.pallas_call(
        body, grid=(N_CHIPS, S // ts), in_specs=in_specs,
        out_specs=pl.BlockSpec((1, ka, tn), lambda q, s: (q, 0, 0)),
        out_shape=jax.ShapeDtypeStruct((N_CHIPS, ka, tn), F32), compiler_params=_cp(2), name=name,
    )(a, *bs)


def _mm_tn_grouped(a, b, groups, w, *, name):
    S = a.shape[0]
    ts = _contraction_rows(S, w * (a.dtype.itemsize + b.dtype.itemsize), w * w)

    def body(a_ref, b_ref, o_ref):
        @pl.when(pl.program_id(1) == 0)
        def _():
            o_ref[...] = jnp.zeros_like(o_ref)

        o_ref[0] += _dot(a_ref[...].astype(BF16), b_ref[...].astype(BF16), TN)

    return pl.pallas_call(
        body, grid=(groups, S // ts),
        in_specs=[pl.BlockSpec((ts, w), lambda g, s: (s, g)), pl.BlockSpec((ts, w), lambda g, s: (s, g))],
        out_specs=pl.BlockSpec((1, w, w), lambda g, s: (g, 0, 0)),
        out_shape=jax.ShapeDtypeStruct((groups, w, w), F32), compiler_params=_cp(2), name=name,
    )(a, b)


def _rms(x, g, *, cb=0, w=None, ts=None, name):
    S = x.shape[0]
    w = w or x.shape[1]
    ts = ts or _row_tile(S)

    def body(x_ref, g_ref, o_ref):
        xv = x_ref[...].astype(F32)
        r = lax.rsqrt(jnp.mean(xv * xv, axis=-1, keepdims=True) + RMS_EPS)
        o_ref[...] = (xv * r * g_ref[...]).astype(o_ref.dtype)

    return pl.pallas_call(
        body, grid=(S // ts,), in_specs=[_rows(ts, w, cb), _const((1, w))], out_specs=_rows(ts, w),
        out_shape=jax.ShapeDtypeStruct((S, w), BF16), compiler_params=_cp(1), name=name,
    )(x, g.reshape(1, w))


def _rms_bwd(x, g, dy, *, cb=0, w=None, res=None, out_dtype=F32, ts=None, name):
    S = x.shape[0]
    w = w or x.shape[1]
    ts = ts or min(S, 512)
    has_res = res is not None

    def body(*refs):
        x_ref, g_ref, dy_ref = refs[:3]
        dx_ref, dg_ref = refs[-2:]
        xv = x_ref[...].astype(F32)
        r = lax.rsqrt(jnp.mean(xv * xv, axis=-1, keepdims=True) + RMS_EPS)
        n = xv * r
        dyv = dy_ref[...].astype(F32)
        dn = dyv * g_ref[...]
        dx = r * (dn - n * jnp.mean(dn * n, axis=-1, keepdims=True))
        if has_res:
            dx = dx + refs[3][...].astype(F32)
        dx_ref[...] = dx.astype(dx_ref.dtype)

        @pl.when(pl.program_id(0) == 0)
        def _():
            dg_ref[...] = jnp.zeros_like(dg_ref)

        dg_ref[...] += jnp.sum(dyv * n, axis=0, keepdims=True)

    ins = [x, g.reshape(1, w), dy] + ([res] if has_res else [])
    in_specs = [_rows(ts, w, cb), _const((1, w)), _rows(ts, w)] + ([_rows(ts, w)] if has_res else [])
    return pl.pallas_call(
        body, grid=(S // ts,), in_specs=in_specs, out_specs=[_rows(ts, w), _const((1, w))],
        out_shape=[jax.ShapeDtypeStruct((S, w), out_dtype), jax.ShapeDtypeStruct((1, w), F32)],
        compiler_params=_cp(1), name=name,
    )(*ins)


HALO = 16


def _pool_counts(i, ts, rows, first_row):
    t = i * ts + first_row + lax.broadcasted_iota(jnp.int32, (rows, 1), 0)
    return [jnp.minimum(t + 1, w).astype(F32) for w in POOL_WINDOWS]


def _even_front(x, g, w_in, pool_w, pool_scale, g_q, w_q, g_kv, w_kv, ctab, stab, *, name):
    S = x.shape[0]
    ts = min(S, 512)

    def body(x_ref, g_ref, win_ref, pw_ref, sc_ref, gq_ref, wq_ref, gkv_ref, wkv_ref, c_ref, s_ref,
             h_ref, z_ref, y_ref, p_ref, cqn_ref, ckvn_ref, q_ref, k_ref, v_ref, tail):
        i = pl.program_id(0)

        def normed(t, gain):
            r = lax.rsqrt(jnp.mean(t * t, axis=-1, keepdims=True) + RMS_EPS)
            return (t * r * gain).astype(BF16)

        h = normed(x_ref[...], g_ref[...])
        h_ref[...] = h
        z = _dot(h, win_ref[...])
        z_ref[...] = z
        u = z[:, :POOL_DIM]
        xe = jnp.concatenate([jnp.where(i > 0, tail[...], 0.0), u], axis=0)
        tail[...] = u[ts - HALO:]
        sums = []
        s = xe
        for sh in (1, 2, 4, 8):
            s = s + pltpu.roll(s, sh, 0)
            sums.append(s)
        cnts = _pool_counts(i, ts, ts, 0)
        for grp in range(4):
            lo, hi = grp * POOL_GROUP, (grp + 1) * POOL_GROUP
            pooled = (sums[grp][HALO:, lo:hi] / cnts[grp] - u[:, lo:hi]).astype(BF16)
            p_ref[:, lo:hi] = pooled
            y_ref[:, lo:hi] = (_dot(pooled, pw_ref[grp]) * sc_ref[:, lo:hi]).astype(y_ref.dtype)
        cqn = normed(z[:, POOL_DIM:POOL_DIM + Q_RANK], gq_ref[...])
        ckvn = normed(z[:, POOL_DIM + Q_RANK:POOL_DIM + Q_RANK + KV_RANK], gkv_ref[...])
        cqn_ref[...] = cqn
        ckvn_ref[...] = ckvn
        q = _dot(cqn, wq_ref[...])
        kv = _dot(ckvn, wkv_ref[...])
        c, sn = c_ref[...], s_ref[...]
        kr = z[:, D_MODEL - HEAD_PAD:]
        kr_rot = kr * c + _rope_partner(kr) * sn
        lane = lax.broadcasted_iota(jnp.int32, (ts, HEAD_PAD), 1)
        for hd in range(MLA_HEADS):
            lo, hi = hd * HEAD_PAD, (hd + 1) * HEAD_PAD
            qh = q[:, lo:hi]
            q_ref[:, lo:hi] = (qh * c + _rope_partner(qh) * sn).astype(q_ref.dtype)
            k_ref[:, lo:hi] = (kv[:, lo:hi] + kr_rot).astype(k_ref.dtype)
            v_ref[:, lo:hi] = jnp.where(lane == V_HEAD, 1.0, kv[:, D_MODEL + lo:D_MODEL + hi]).astype(v_ref.dtype)

    wide = jax.ShapeDtypeStruct((S, D_MODEL), BF16)
    return pl.pallas_call(
        body, grid=(S // ts,),
        in_specs=[_rows(ts, D_MODEL), _const((1, D_MODEL)), _const((D_MODEL, D_MODEL)),
                  _const((4, POOL_GROUP, POOL_GROUP)), _const((1, POOL_DIM)), _const((1, Q_RANK)),
                  _const((Q_RANK, D_MODEL)), _const((1, KV_RANK)), _const((KV_RANK, 2 * D_MODEL)),
                  _rows(ts, HEAD_PAD), _rows(ts, HEAD_PAD)],
        out_specs=[_rows(ts, D_MODEL), _rows(ts, D_MODEL), _rows(ts, POOL_DIM), _rows(ts, POOL_DIM),
                   _rows(ts, Q_RANK), _rows(ts, KV_RANK), _rows(ts, D_MODEL), _rows(ts, D_MODEL), _rows(ts, D_MODEL)],
        out_shape=[wide, jax.ShapeDtypeStruct((S, D_MODEL), F32), jax.ShapeDtypeStruct((S, MIX_DIM), BF16),
                   jax.ShapeDtypeStruct((S, POOL_DIM), BF16), jax.ShapeDtypeStruct((S, Q_RANK), BF16),
                   jax.ShapeDtypeStruct((S, KV_RANK), BF16), wide, wide, wide],
        scratch_shapes=[pltpu.VMEM((HALO, POOL_DIM), F32)], compiler_params=_cp(1), name=name,
    )(x, g.reshape(1, D_MODEL), w_in, pool_w, pool_scale, g_q.reshape(1, Q_RANK), w_q, g_kv.reshape(1, KV_RANK), w_kv,
      ctab, stab)


def _norm_bwd_values(xv, gain, dy):
    r = lax.rsqrt(jnp.mean(xv * xv, axis=-1, keepdims=True) + RMS_EPS)
    n = xv * r
    dn = dy * gain
    return r * (dn - n * jnp.mean(dn * n, axis=-1, keepdims=True)), jnp.sum(dy * n, axis=0, keepdims=True)


def _even_back(dq_rot, dk_cat, dv, dmix, pooled, z, x, dxo, ctab, stab, w_q, w_kv, w_in, pool_w, pool_scale, g_q, g_kv,
               g_x, *, name):
    S = x.shape[0]
    ts = min(S, 512)
    nh = ts // HALO
    last = S // HALO - 1
    n = ts + HALO

    def body(dq_ref, dk_ref, dv_ref, dy_ref, dyh_ref, p_ref, z_ref, x_ref, dxo_ref, c_ref, s_ref, wq_ref, wkv_ref,
             win_ref, pw_ref, sc_ref, gq_ref, gkv_ref, gx_ref,
             dx_ref, dqp_ref, dz_ref, dyp_ref, dgq_ref, dgkv_ref, dsc_ref, dgx_ref):
        i = pl.program_id(0)

        @pl.when(i == 0)
        def _():
            for ref in (dgq_ref, dgkv_ref, dsc_ref, dgx_ref):
                ref[...] = jnp.zeros_like(ref)

        c, sn = c_ref[...], s_ref[...]
        z = z_ref[...]
        dk = dk_ref[...]
        for hd in range(MLA_HEADS):
            lo, hi = hd * HEAD_PAD, (hd + 1) * HEAD_PAD
            g = dq_ref[:, lo:hi]
            dqp_ref[:, lo:hi] = (g * c + _rope_partner(g * sn)).astype(dqp_ref.dtype)
            heads_sum = dk[:, lo:hi] if hd == 0 else heads_sum + dk[:, lo:hi]
        lane = lax.broadcasted_iota(jnp.int32, heads_sum.shape, 1)
        dkr = jnp.where((lane >= QK_NOPE) & (lane < QK_DIM), heads_sum * c + _rope_partner(heads_sum * sn), 0.0)
        dcqn = _dot(dqp_ref[...], wq_ref[...], NT)
        dckvn = _dot(dk.astype(BF16), wkv_ref[:, :D_MODEL], NT) + _dot(dv_ref[...].astype(BF16),
                                                                       wkv_ref[:, D_MODEL:], NT)
        dcq, dgq = _norm_bwd_values(z[:, POOL_DIM:POOL_DIM + Q_RANK], gq_ref[...], dcqn)
        dckv, dgkv = _norm_bwd_values(z[:, POOL_DIM + Q_RANK:POOL_DIM + Q_RANK + KV_RANK], gkv_ref[...], dckvn)
        dgq_ref[...] += dgq
        dgkv_ref[...] += dgkv
        dyv = dy_ref[...].astype(F32)
        dyh = jnp.where(i < pl.num_programs(0) - 1, dyh_ref[...].astype(F32), 0.0)
        dypre = (jnp.concatenate([dyv, dyh], axis=0) * sc_ref[...]).astype(BF16)
        dyp_ref[...] = dypre[:ts]
        cnts = _pool_counts(i, ts, n, 0)
        dsc = []
        for grp in range(4):
            lo, hi = grp * POOL_GROUP, (grp + 1) * POOL_GROUP
            dsc.append(jnp.sum(dyv[:, lo:hi] * _dot(p_ref[:, lo:hi], pw_ref[grp]), axis=0, keepdims=True))
            dpool = _dot(dypre[:, lo:hi], pw_ref[grp], NT)
            s = dpool / cnts[grp]
            for sh in (1, 2, 4, 8)[:grp + 1]:
                s = s + pltpu.roll(s, n - sh, 0)
            dz_ref[:, lo:hi] = (s[:ts] - dpool[:ts]).astype(dz_ref.dtype)
        dsc_ref[...] += jnp.concatenate(dsc, axis=1)
        dz_ref[:, POOL_DIM:POOL_DIM + Q_RANK] = dcq.astype(dz_ref.dtype)
        dz_ref[:, POOL_DIM + Q_RANK:POOL_DIM + Q_RANK + KV_RANK] = dckv.astype(dz_ref.dtype)
        dz_ref[:, D_MODEL - HEAD_PAD:] = dkr.astype(dz_ref.dtype)
        dx, dgx = _norm_bwd_values(x_ref[...], gx_ref[...], _dot(dz_ref[...], win_ref[...], NT))
        dx_ref[...] = dx + dxo_ref[...]
        dgx_ref[...] += dgx

    wide, pool = _rows(ts, D_MODEL), _rows(ts, POOL_DIM)
    f32 = lambda w: jax.ShapeDtypeStruct((1, w), F32)
    return pl.pallas_call(
        body, grid=(S // ts,),
        in_specs=[wide, wide, wide, pool,
                  pl.BlockSpec((HALO, POOL_DIM), lambda i: (jnp.minimum((i + 1) * nh, last), 0)), pool, wide, wide, wide,
                  _rows(ts, HEAD_PAD), _rows(ts, HEAD_PAD), _const((Q_RANK, D_MODEL)), _const((KV_RANK, 2 * D_MODEL)),
                  _const((D_MODEL, D_MODEL)), _const((4, POOL_GROUP, POOL_GROUP)), _const((1, POOL_DIM)),
                  _const((1, Q_RANK)), _const((1, KV_RANK)), _const((1, D_MODEL))],
        out_specs=[wide, wide, wide, pool, _const((1, Q_RANK)), _const((1, KV_RANK)), _const((1, POOL_DIM)),
                   _const((1, D_MODEL))],
        out_shape=[jax.ShapeDtypeStruct((S, D_MODEL), F32), jax.ShapeDtypeStruct((S, D_MODEL), BF16),
                   jax.ShapeDtypeStruct((S, D_MODEL), BF16), jax.ShapeDtypeStruct((S, POOL_DIM), BF16),
                   f32(Q_RANK), f32(KV_RANK), f32(POOL_DIM), f32(D_MODEL)],
        compiler_params=_cp(1), name=name,
    )(dq_rot, dk_cat, dv, dmix, dmix, pooled, z, x, dxo, ctab, stab, w_q, w_kv, w_in, pool_w, pool_scale,
      g_q.reshape(1, Q_RANK), g_kv.reshape(1, KV_RANK), g_x.reshape(1, D_MODEL))


def _rope_partner(t):
    lane = lax.broadcasted_iota(jnp.int32, t.shape, 1)
    swapped = jnp.where(lane < QK_NOPE + QK_ROPE // 2, pltpu.roll(t, HEAD_PAD - QK_ROPE // 2, 1),
                        pltpu.roll(t, QK_ROPE // 2, 1))
    return jnp.where((lane >= QK_NOPE) & (lane < QK_DIM), swapped, 0.0)


ATT_SCALE = QK_DIM ** -0.5
LOG2E = math.log2(math.e)


HEADS_PER_STEP = 2
ATT_COL0 = POOL_DIM // HEAD_PAD


FWD_TILE = 1024


def _stat_rows(col):
    return jnp.broadcast_to(col, (col.shape[0], LANES)).T[0:8]


def _retile_rows(rows, tq):
    heads, n8, t = rows.shape
    if t == tq:
        return rows
    flat = rows.reshape(heads, n8 // 8, 8, t)[:, :, 0].reshape(heads, -1, 1, tq)
    return jnp.broadcast_to(flat, (heads, flat.shape[1], 8, tq)).reshape(heads, -1, tq)


def _flash_fwd(q, k, v, mix, *, name):
    S = q.shape[0]
    tq = FWD_TILE if S % FWD_TILE == 0 else min(S, 512)
    nq = S // tq
    hs = HEADS_PER_STEP
    wide = hs * HEAD_PAD

    def body(q_ref, k_ref, v_ref, mix_ref, o_ref, lse_ref):
        qi = pl.program_id(1)
        qv = [q_ref[:, a * HEAD_PAD:(a + 1) * HEAD_PAD] for a in range(hs)]

        def update(m, acc, s, v):
            m_new = jnp.maximum(m, jnp.max(s, axis=-1, keepdims=True))
            p = jnp.exp2((s - m_new) * (ATT_SCALE * LOG2E))
            alpha = jnp.exp2((m - m_new) * (ATT_SCALE * LOG2E))
            return m_new, alpha * acc + _dot(p.astype(BF16), v)

        def step(j, carry, masked):
            off = pl.multiple_of(j * tq, tq)
            out = []
            for a in range(hs):
                head = slice(a * HEAD_PAD, (a + 1) * HEAD_PAD)
                s = _dot(qv[a], k_ref[pl.ds(off, tq), head], NT)
                if masked:
                    row = lax.broadcasted_iota(jnp.int32, (tq, tq), 0)
                    col = lax.broadcasted_iota(jnp.int32, (tq, tq), 1)
                    s = jnp.where(col <= row, s, NEG_INF)
                out.append(update(*carry[a], s, v_ref[pl.ds(off, tq), head]))
            return tuple(out)

        one = (jnp.full((tq, 1), NEG_INF, F32), jnp.zeros((tq, HEAD_PAD), F32))
        carry = step(qi, lax.fori_loop(0, qi, lambda j, c: step(j, c, False), (one,) * hs), True)
        for a in range(hs):
            m, acc = carry[a]
            l = acc[:, V_HEAD:V_HEAD + 1]
            o_ref[:, a * HEAD_PAD:(a + 1) * HEAD_PAD] = (acc / l).astype(o_ref.dtype)
            lse_ref[a] = _stat_rows(m * ATT_SCALE + jnp.log(l))

    blk = pl.BlockSpec((tq, wide), lambda h, i: (i, h))
    full = pl.BlockSpec((S, wide), lambda h, i: (0, h))
    return pl.pallas_call(
        body, grid=(MLA_HEADS // hs, nq), in_specs=[blk, full, full, ANY],
        out_specs=[pl.BlockSpec((tq, wide), lambda h, i: (i, ATT_COL0 // hs + h)),
                   pl.BlockSpec((hs, 8, tq), lambda h, i: (h, i, 0))],
        out_shape=[jax.ShapeDtypeStruct(mix.shape, mix.dtype), jax.ShapeDtypeStruct((MLA_HEADS, nq * 8, tq), F32)],
        input_output_aliases={3: 0}, compiler_params=_cp(2), name=name,
    )(q, k, v, mix)


BWD_TILE = 1024
BWD_HEADS_PER_STEP = 1


def _bwd_tile(S):
    return BWD_TILE if S % BWD_TILE == 0 else min(S, 512)


def _attn_delta(dmix, mix, *, name):
    S = mix.shape[0]
    ts = _bwd_tile(S)
    half = MLA_HEADS // 2
    halves = [_rows(ts, half * HEAD_PAD, 1), _rows(ts, half * HEAD_PAD, 2)]

    def body(do0_ref, do1_ref, o0_ref, o1_ref, d_ref):
        for n, (do_ref, o_ref) in enumerate(((do0_ref, o0_ref), (do1_ref, o1_ref))):
            prod = do_ref[...].astype(F32) * o_ref[...].astype(F32)
            for a in range(half):
                d_ref[n * half + a] = _stat_rows(
                    jnp.sum(prod[:, a * HEAD_PAD:(a + 1) * HEAD_PAD], axis=-1, keepdims=True))

    return pl.pallas_call(
        body, grid=(S // ts,), in_specs=halves + halves,
        out_specs=pl.BlockSpec((MLA_HEADS, 8, ts), lambda i: (0, i, 0)),
        out_shape=jax.ShapeDtypeStruct((MLA_HEADS, (S // ts) * 8, ts), F32), compiler_params=_cp(1), name=name,
    )(dmix, dmix, mix, mix)


def _flash_bwd(q, k, v, dmix, lse_rows, delta_rows, *, name):
    S = q.shape[0]
    tq = _bwd_tile(S)
    nq = S // tq
    hs = BWD_HEADS_PER_STEP
    wide = hs * HEAD_PAD

    def body(q_hbm, do_hbm, lse_ref, dl_ref, k_ref, v_ref, dq_hbm, dk_ref, dv_ref, q_all, do_all, dq_all):
        g, j = pl.program_id(0), pl.program_id(1)
        cols = pl.multiple_of(g * wide, wide)

        @pl.when(j == 0)
        def _():
            pltpu.sync_copy(q_hbm.at[:, pl.ds(cols, wide)], q_all)
            pltpu.sync_copy(do_hbm.at[:, pl.ds(POOL_DIM + cols, wide)], do_all)
            dq_all[...] = jnp.zeros_like(dq_all)

        heads = [slice(a * HEAD_PAD, (a + 1) * HEAD_PAD) for a in range(hs)]
        kv = [k_ref[:, a] for a in heads]
        vv = [v_ref[:, a] for a in heads]

        def block(a, keys, rows, lse2, dl, first_query):
            qv, dov = q_all[rows, heads[a]], do_all[rows, heads[a]]
            st = _dot(kv[a][:keys], qv, NT)
            if first_query is not None:
                krow = lax.broadcasted_iota(jnp.int32, st.shape, 0)
                qcol = lax.broadcasted_iota(jnp.int32, st.shape, 1) + first_query
                st = jnp.where(krow <= qcol, st, NEG_INF)
            pt = jnp.exp2(st * (ATT_SCALE * LOG2E) - lse2)
            dst = (pt * (_dot(vv[a][:keys], dov, NT) - dl)).astype(BF16)
            dq_all[rows, heads[a]] += _dot(dst, kv[a][:keys], TN)
            return _dot(dst, qv), _dot(pt.astype(BF16), dov)

        def stats(a, i):
            off8 = pl.multiple_of(i * 8, 8)
            return lse_ref[a, pl.ds(off8, 8), :][0:1] * LOG2E, dl_ref[a, pl.ds(off8, 8), :][0:1]

        def step(i, carry):
            rows = pl.ds(pl.multiple_of(i * tq, tq), tq)
            out = []
            for a in range(hs):
                dk, dv = block(a, tq, rows, *stats(a, i), None)
                out.append((carry[a][0] + dk, carry[a][1] + dv))
            return tuple(out)

        def diagonal():
            half = tq // 2
            out = []
            for a in range(hs):
                lse2, dl = stats(a, j)
                off = pl.multiple_of(j * tq, tq)
                dk0, dv0 = block(a, half, pl.ds(off, half), lse2[:, :half], dl[:, :half], 0)
                dk1, dv1 = block(a, tq, pl.ds(pl.multiple_of(off + half, half), half), lse2[:, half:], dl[:, half:], half)
                zero = jnp.zeros((tq - half, HEAD_PAD), F32)
                out.append((dk1 + jnp.concatenate([dk0, zero], axis=0), dv1 + jnp.concatenate([dv0, zero], axis=0)))
            return tuple(out)

        carry = lax.fori_loop(j + 1, nq, step, diagonal())
        for a in range(hs):
            dk_ref[:, heads[a]] = carry[a][0] * ATT_SCALE
            dv_ref[:, heads[a]] = carry[a][1]

        @pl.when(j == nq - 1)
        def _():
            dq_all[...] = dq_all[...] * ATT_SCALE
            pltpu.sync_copy(dq_all, dq_hbm.at[:, pl.ds(cols, wide)])

    blk = pl.BlockSpec((tq, wide), lambda g, j: (j, g))
    stat = pl.BlockSpec((hs, nq * 8, tq), lambda g, j: (g, 0, 0))
    full = jax.ShapeDtypeStruct((S, MLA_HEADS * HEAD_PAD), F32)
    return pl.pallas_call(
        body, grid=(MLA_HEADS // hs, nq), in_specs=[ANY, ANY, stat, stat, blk, blk], out_specs=[ANY, blk, blk],
        out_shape=[full, full, full],
        scratch_shapes=[pltpu.VMEM((S, wide), BF16), pltpu.VMEM((S, wide), BF16), pltpu.VMEM((S, wide), F32)],
        compiler_params=_cp(2), name=name,
    )(q, dmix, lse_rows, delta_rows, k, v)


MEM_SCALE = MEM_HEAD_DIM ** -0.5


def _xattn_probs(qh, kh):
    s = _dot(qh, kh, NT) * MEM_SCALE
    e = jnp.exp(s - jnp.max(s, axis=-1, keepdims=True))
    return e / jnp.sum(e, axis=-1, keepdims=True)


def _xa_block_fwd(x, kvm, w_q, w_o, g, *, name):
    S = x.shape[0]
    ts = min(S, 512)
    nm = kvm.shape[0]

    def body(x_ref, kv_ref, wq_ref, wo_ref, g_ref, xo_ref, hx_ref, q_ref, o_ref):
        xv = x_ref[...]
        r = lax.rsqrt(jnp.mean(xv * xv, axis=-1, keepdims=True) + RMS_EPS)
        hx = (xv * r * g_ref[...]).astype(BF16)
        hx_ref[...] = hx
        q = _dot(hx, wq_ref[...]).astype(BF16)
        q_ref[...] = q
        for h in range(MEM_HEADS):
            lo, hi = h * MEM_HEAD_DIM, (h + 1) * MEM_HEAD_DIM
            p = _xattn_probs(q[:, lo:hi], kv_ref[:, lo:hi])
            o_ref[:, lo:hi] = _dot(p.astype(BF16), kv_ref[:, D_MODEL + lo:D_MODEL + hi]).astype(o_ref.dtype)
        xo_ref[...] = xv + _dot(o_ref[...], wo_ref[...])

    square = _const((D_MODEL, D_MODEL))
    act = jax.ShapeDtypeStruct((S, D_MODEL), BF16)
    return pl.pallas_call(
        body, grid=(S // ts,),
        in_specs=[_rows(ts, D_MODEL), _const((nm, 2 * D_MODEL)), square, square, _const((1, D_MODEL))],
        out_specs=[_rows(ts, D_MODEL)] * 4, out_shape=[jax.ShapeDtypeStruct((S, D_MODEL), F32), act, act, act],
        compiler_params=_cp(1), name=name,
    )(x, kvm, w_q, w_o, g.reshape(1, D_MODEL))


def _xa_block_bwd(dxo, x, q, kvm, w_q, w_o, g, *, name):
    S = q.shape[0]
    ts = min(S, 512)
    nm = kvm.shape[0]

    def body(dxo_ref, x_ref, q_ref, kv_ref, wq_ref, wo_ref, g_ref, dx_ref, dq_ref, dkv_ref, dg_ref):
        @pl.when(pl.program_id(0) == 0)
        def _():
            dkv_ref[...] = jnp.zeros_like(dkv_ref)
            dg_ref[...] = jnp.zeros_like(dg_ref)

        dxo = dxo_ref[...]
        do = _dot(dxo.astype(BF16), wo_ref[...], NT).astype(BF16)
        for h in range(MEM_HEADS):
            lo, hi = h * MEM_HEAD_DIM, (h + 1) * MEM_HEAD_DIM
            qh, kh, vh = q_ref[:, lo:hi], kv_ref[:, lo:hi], kv_ref[:, D_MODEL + lo:D_MODEL + hi]
            doh = do[:, lo:hi]
            p = _xattn_probs(qh, kh)
            dp = _dot(doh, vh, NT)
            ds = (p * (dp - jnp.sum(dp * p, axis=-1, keepdims=True)) * MEM_SCALE).astype(BF16)
            dq_ref[:, lo:hi] = _dot(ds, kh).astype(dq_ref.dtype)
            dkv_ref[:, lo:hi] += _dot(ds, qh, TN)
            dkv_ref[:, D_MODEL + lo:D_MODEL + hi] += _dot(p.astype(BF16), doh, TN)
        dx, dg = _norm_bwd_epilogue(0)([_dot(dq_ref[...], wq_ref[...], NT)], [x_ref[...], dxo, g_ref[...]])
        dx_ref[...] = dx
        dg_ref[...] += dg

    square = _const((D_MODEL, D_MODEL))
    return pl.pallas_call(
        body, grid=(S // ts,),
        in_specs=[_rows(ts, D_MODEL), _rows(ts, D_MODEL), _rows(ts, D_MODEL), _const((nm, 2 * D_MODEL)), square,
                  square, _const((1, D_MODEL))],
        out_specs=[_rows(ts, D_MODEL), _rows(ts, D_MODEL), _const((nm, 2 * D_MODEL)), _const((1, D_MODEL))],
        out_shape=[jax.ShapeDtypeStruct((S, D_MODEL), F32), jax.ShapeDtypeStruct((S, D_MODEL), BF16),
                   jax.ShapeDtypeStruct((nm, 2 * D_MODEL), F32), jax.ShapeDtypeStruct((1, D_MODEL), F32)],
        compiler_params=_cp(1), name=name,
    )(dxo, x, q, kvm, w_q, w_o, g.reshape(1, D_MODEL))


CONV_HALO = 8


def _sigmoid(x):
    return 0.5 * jnp.tanh(0.5 * x) + 0.5


def _softplus(x):
    return jnp.maximum(x, 0.0) + jnp.log(1.0 + jnp.exp(-jnp.abs(x)))


def _neg_expm1(x):
    series = -x * (1.0 + x * (1.0 / 2) * (1.0 + x * (1.0 / 3) * (1.0 + x * (1.0 / 4) * (1.0 + x * (1.0 / 5)))))
    return jnp.where(x > -0.05, series, 1.0 - jnp.exp(x))


GELU_C = math.sqrt(2.0 / math.pi)


def _gelu(x):
    return 0.5 * x * (1.0 + jnp.tanh(GELU_C * (x + 0.044715 * x * x * x)))


def _gelu_grad(x):
    t = jnp.tanh(GELU_C * (x + 0.044715 * x * x * x))
    return 0.5 * (1.0 + t) + 0.5 * x * (1.0 - t * t) * GELU_C * (1.0 + 3 * 0.044715 * x * x)


def _lru_gates(xc, wr_ref, br, wi_ref, bi, sp, reset):
    xcb = xc.astype(BF16)
    pr, pi = [], []
    for h in range(LRU_HEADS):
        lo, hi = h * LRU_HEAD_DIM, (h + 1) * LRU_HEAD_DIM
        pr.append(_dot(xcb[:, lo:hi], wr_ref[h]))
        pi.append(_dot(xcb[:, lo:hi], wi_ref[h]))
    r = _sigmoid(jnp.concatenate(pr, axis=1) + br)
    ig = _sigmoid(jnp.concatenate(pi, axis=1) + bi)
    log_a = -LRU_C * r * sp
    a = jnp.where(reset, 0.0, jnp.exp(log_a))
    mult = jnp.where(reset, 1.0, jnp.sqrt(jnp.maximum(_neg_expm1(2.0 * log_a), 0.0)))
    return r, ig, a, mult


SUBLANES = 8


def _compose_groups(a, b, reverse):
    n = a.shape[0]
    row = lax.broadcasted_iota(jnp.int32, a.shape, 0) % SUBLANES
    for s in (1, 2, 4):
        inside = (row < SUBLANES - s) if reverse else (row >= s)
        shift = n - s if reverse else s
        a_s = jnp.where(inside, pltpu.roll(a, shift, 0), 1.0)
        b_s = jnp.where(inside, pltpu.roll(b, shift, 0), 0.0)
        b = a * b_s + b
        a = a * a_s
    return a, b


def _chain_groups(a_buf, h_ref, state, reverse):
    groups = a_buf.shape[0] // SUBLANES

    def group(g, h_in):
        off = pl.multiple_of((groups - 1 - g if reverse else g) * SUBLANES, SUBLANES)
        h = a_buf[pl.ds(off, SUBLANES), :] * h_in + h_ref[pl.ds(off, SUBLANES), :]
        h_ref[pl.ds(off, SUBLANES), :] = h
        return jnp.broadcast_to(h[0:1] if reverse else h[SUBLANES - 1:SUBLANES], h.shape)

    return lax.fori_loop(0, groups, group, state, unroll=4)[0:1]


def _lru_fwd(x, g, w_in, reset, conv_w, conv_b, w_r, b_r, w_i, b_i, lam, *, name):
    S = x.shape[0]
    ts = min(S, 512)
    W = D_MODEL

    def body(x_ref, g_ref, win_ref, rs_ref, cw_ref, cb_ref, wr_ref, br_ref, wi_ref, bi_ref, lam_ref,
             hn_ref, z_ref, xc_ref, h_ref, y_ref, a_buf, carry, tail):
        i = pl.program_id(0)

        @pl.when(i == 0)
        def _():
            carry[...] = jnp.zeros_like(carry)
            tail[...] = jnp.zeros_like(tail)

        xv = x_ref[...]
        hn = (xv * lax.rsqrt(jnp.mean(xv * xv, axis=-1, keepdims=True) + RMS_EPS) * g_ref[...]).astype(BF16)
        hn_ref[...] = hn
        z_ref[...] = _dot(hn, win_ref[...])
        xb = z_ref[:, W:]
        xe = jnp.concatenate([tail[...], xb], axis=0)
        tail[...] = xb[ts - CONV_HALO:]
        xc = cb_ref[...] + cw_ref[3:4, :] * xe[CONV_HALO:]
        for kk in range(CONV_WIDTH - 1):
            xc = xc + cw_ref[kk:kk + 1, :] * pltpu.roll(xe, CONV_WIDTH - 1 - kk, 0)[CONV_HALO:]
        xc_ref[...] = xc
        reset = rs_ref[...] > 0.5
        _, ig, a, mult = _lru_gates(xc, wr_ref, br_ref[...], wi_ref, bi_ref[...], _softplus(-lam_ref[...]), reset)
        a_buf[...], h_ref[...] = _compose_groups(a, mult * (ig * xc), False)
        carry[...] = _chain_groups(a_buf, h_ref, jnp.broadcast_to(carry[...], (SUBLANES, W)), False)
        y_ref[...] = (_gelu(z_ref[:, :W]) * h_ref[...]).astype(y_ref.dtype)

    vec = _const((1, W))
    gw = _const((LRU_HEADS, LRU_HEAD_DIM, LRU_HEAD_DIM))
    return pl.pallas_call(
        body, grid=(S // ts,),
        in_specs=[_rows(ts, W), vec, _const((W, 2 * W)), _rows(ts, 1), _const((CONV_WIDTH, W)), vec, gw, vec, gw, vec,
                  vec],
        out_specs=[_rows(ts, W), _rows(ts, 2 * W), _rows(ts, W), _rows(ts, W), _rows(ts, W)],
        out_shape=[jax.ShapeDtypeStruct((S, W), BF16), jax.ShapeDtypeStruct((S, 2 * W), F32),
                   jax.ShapeDtypeStruct((S, W), F32), jax.ShapeDtypeStruct((S, W), F32),
                   jax.ShapeDtypeStruct((S, W), BF16)],
        scratch_shapes=[pltpu.VMEM((ts, W), F32), pltpu.VMEM((1, W), F32), pltpu.VMEM((CONV_HALO, W), F32)],
        compiler_params=_cp(1), name=name,
    )(x, g.reshape(1, W), w_in, reset, conv_w, conv_b, w_r, b_r, w_i, b_i, lam)


def _lru_bwd(dxo, w_out, z, xc, hseq, reset, w_r, b_r, w_i, b_i, lam, *, name):
    S = z.shape[0]
    ts = min(S, 512)
    nt = S // ts
    nh = ts // CONV_HALO
    W = D_MODEL

    def body(dxo_ref, wout_ref, gate_ref, xc_ref, h_ref, hh_ref, rs_ref, wr_ref, br_ref, wi_ref, bi_ref, lam_ref,
             dg_ref, dxc_ref, dpr_ref, dpi_ref, acc_ref, a_buf, dh_buf, carry):
        i = pl.program_id(0)
        tile = nt - 1 - i

        @pl.when(i == 0)
        def _():
            carry[...] = jnp.zeros_like(carry)
            acc_ref[...] = jnp.zeros_like(acc_ref)

        xc = xc_ref[...]
        lam_v = lam_ref[...]
        sp = _softplus(-lam_v)
        reset = rs_ref[...] > 0.5
        r, ig, a, mult = _lru_gates(xc, wr_ref, br_ref[...], wi_ref, bi_ref[...], sp, reset)
        gate = gate_ref[...]
        dyv = _dot(dxo_ref[...].astype(BF16), wout_ref[...], NT)
        h = h_ref[...]
        dg_ref[...] = (dyv * h * _gelu_grad(gate)).astype(dg_ref.dtype)
        last_row = lax.broadcasted_iota(jnp.int32, a.shape, 0) == ts - 1
        a_buf[...], dh_buf[...] = _compose_groups(jnp.where(last_row, 1.0, pltpu.roll(a, ts - 1, 0)),
                                                  dyv * _gelu(gate), True)
        _chain_groups(a_buf, dh_buf, jnp.broadcast_to(carry[...], (SUBLANES, W)), True)
        dh = dh_buf[...]
        carry[...] = a[0:1] * dh[0:1]
        hh = jnp.where(tile > 0, hh_ref[...], 0.0)
        h_prev = pltpu.roll(jnp.concatenate([hh, h], axis=0), 1, 0)[CONV_HALO:]
        da = dh * h_prev
        bx = ig * xc
        dmult = dh * bx
        dbx = dh * mult
        di = dbx * xc
        dlog_a = jnp.where(reset, 0.0, da * a - dmult * a * a / jnp.maximum(mult, 1e-30))
        dr = dlog_a * (-LRU_C) * sp
        dpre_r = dr * r * (1.0 - r)
        dpre_i = di * ig * (1.0 - ig)
        dprb, dpib = dpre_r.astype(BF16), dpre_i.astype(BF16)
        dpr_ref[...] = dprb
        dpi_ref[...] = dpib
        back = []
        for hd in range(LRU_HEADS):
            lo, hi = hd * LRU_HEAD_DIM, (hd + 1) * LRU_HEAD_DIM
            back.append(_dot(dprb[:, lo:hi], wr_ref[hd], NT) + _dot(dpib[:, lo:hi], wi_ref[hd], NT))
        dxc_ref[...] = dbx * ig + jnp.concatenate(back, axis=1)
        dlam = jnp.sum(dlog_a * (-LRU_C) * r, axis=0, keepdims=True) * (-_sigmoid(-lam_v))
        acc_ref[0:1, :] += jnp.sum(dpre_r, axis=0, keepdims=True)
        acc_ref[1:2, :] += jnp.sum(dpre_i, axis=0, keepdims=True)
        acc_ref[2:3, :] += dlam

    rev = lambda cb: pl.BlockSpec((ts, W), lambda i: (nt - 1 - i, cb))
    vec = _const((1, W))
    gw = _const((LRU_HEADS, LRU_HEAD_DIM, LRU_HEAD_DIM))
    return pl.pallas_call(
        body, grid=(nt,),
        in_specs=[rev(0), _const((W, W)), rev(0), rev(0), rev(0),
                  pl.BlockSpec((CONV_HALO, W), lambda i: (jnp.maximum((nt - 1 - i) * nh - 1, 0), 0)),
                  pl.BlockSpec((ts, 1), lambda i: (nt - 1 - i, 0)), gw, vec, gw, vec, vec],
        out_specs=[rev(0), rev(0), rev(0), rev(0), _const((8, W))],
        out_shape=[jax.ShapeDtypeStruct((S, W), BF16), jax.ShapeDtypeStruct((S, W), F32),
                   jax.ShapeDtypeStruct((S, W), BF16), jax.ShapeDtypeStruct((S, W), BF16),
                   jax.ShapeDtypeStruct((8, W), F32)],
        scratch_shapes=[pltpu.VMEM((ts, W), F32), pltpu.VMEM((ts, W), F32), pltpu.VMEM((1, W), F32)],
        compiler_params=_cp(1), name=name,
    )(dxo, w_out, z, xc, hseq, hseq, reset, w_r, b_r, w_i, b_i, lam)


def _conv_bwd(dxc, z, conv_w, *, name):
    S = dxc.shape[0]
    ts = min(S, 512)
    nh = ts // CONV_HALO
    last = S // CONV_HALO - 1
    W = D_MODEL
    n = ts + CONV_HALO

    def body(d_ref, dn_ref, xb_ref, xp_ref, cw_ref, dxb_ref, acc_ref):
        i = pl.program_id(0)

        @pl.when(i == 0)
        def _():
            acc_ref[...] = jnp.zeros_like(acc_ref)

        d = d_ref[...]
        de = jnp.concatenate([d, jnp.where(i < pl.num_programs(0) - 1, dn_ref[...], 0.0)], axis=0)
        xe = jnp.concatenate([jnp.where(i > 0, xp_ref[...], 0.0), xb_ref[...]], axis=0)
        dxb = cw_ref[3:4, :] * d
        acc_ref[3:4, :] += jnp.sum(d * xe[CONV_HALO:], axis=0, keepdims=True)
        for kk in range(CONV_WIDTH - 1):
            sh = CONV_WIDTH - 1 - kk
            dxb = dxb + cw_ref[kk:kk + 1, :] * pltpu.roll(de, n - sh, 0)[:ts]
            acc_ref[kk:kk + 1, :] += jnp.sum(d * pltpu.roll(xe, sh, 0)[CONV_HALO:], axis=0, keepdims=True)
        dxb_ref[...] = dxb.astype(dxb_ref.dtype)
        acc_ref[4:5, :] += jnp.sum(d, axis=0, keepdims=True)

    return pl.pallas_call(
        body, grid=(S // ts,),
        in_specs=[_rows(ts, W), pl.BlockSpec((CONV_HALO, W), lambda i: (jnp.minimum((i + 1) * nh, last), 0)),
                  _rows(ts, W, 1), pl.BlockSpec((CONV_HALO, W), lambda i: (jnp.maximum(i * nh - 1, 0), 1)),
                  _const((CONV_WIDTH, W))],
        out_specs=[_rows(ts, W), _const((8, W))],
        out_shape=[jax.ShapeDtypeStruct((S, W), BF16), jax.ShapeDtypeStruct((8, W), F32)],
        compiler_params=_cp(1), name=name,
    )(dxc, dxc, z, z, conv_w)


def _loss_head(x, g, target, *, name):
    S, D = x.shape
    ts = _row_tile(S)

    def body(x_ref, g_ref, t_ref, dx_ref, dg_ref, l_ref):
        @pl.when(pl.program_id(0) == 0)
        def _():
            dg_ref[...] = jnp.zeros_like(dg_ref)
            l_ref[...] = jnp.zeros_like(l_ref)

        xv = x_ref[...]
        r = lax.rsqrt(jnp.mean(xv * xv, axis=-1, keepdims=True) + RMS_EPS)
        n = xv * r
        err = n * g_ref[...] - t_ref[...]
        l_ref[...] += 0.5 * jnp.sum(jnp.sum(err * err, axis=-1, keepdims=True) * (1.0 / D), axis=0, keepdims=True)
        dy = err * (1.0 / D)
        dn = dy * g_ref[...]
        dx_ref[...] = r * (dn - n * jnp.mean(dn * n, axis=-1, keepdims=True))
        dg_ref[...] += jnp.sum(dy * n, axis=0, keepdims=True)

    return pl.pallas_call(
        body, grid=(S // ts,), in_specs=[_rows(ts, D), _const((1, D)), _rows(ts, D)],
        out_specs=[_rows(ts, D), _const((1, D)), _const((8, LANES))],
        out_shape=[jax.ShapeDtypeStruct((S, D), F32), jax.ShapeDtypeStruct((1, D), F32),
                   jax.ShapeDtypeStruct((8, LANES), F32)],
        compiler_params=_cp(1), name=name,
    )(x, g.reshape(1, D), target)


def _adamw(w, ga, gb, m, v, *, name):
    shape = w.shape
    cols = shape[-1]
    rows = w.size // cols
    br = rows
    if rows * cols * 4 > (1 << 20):
        br = max(d for d in range(8, rows + 1, 8) if rows % d == 0 and d * cols * 4 <= (1 << 20))

    def body(w_ref, ga_ref, gb_ref, m_ref, v_ref, g_ref, d_ref, mo_ref, vo_ref):
        gv = ga_ref[...] + gb_ref[...]
        g_ref[...] = gv
        mn = ADAM_B1 * m_ref[...] + (1.0 - ADAM_B1) * gv
        vn = ADAM_B2 * v_ref[...] + (1.0 - ADAM_B2) * (gv * gv)
        m_hat = mn / (1.0 - ADAM_B1 ** ADAM_STEP)
        v_hat = vn / (1.0 - ADAM_B2 ** ADAM_STEP)
        d_ref[...] = -ADAM_LR * (m_hat / (jnp.sqrt(v_hat) + ADAM_EPS) + ADAM_WD * w_ref[...])
        mo_ref[...] = mn
        vo_ref[...] = vn

    spec = _rows(br, cols)
    outs = pl.pallas_call(
        body, grid=(rows // br,), in_specs=[spec] * 5, out_specs=[spec] * 4,
        out_shape=[jax.ShapeDtypeStruct((rows, cols), F32)] * 4, compiler_params=_cp(1), name=name,
    )(*[t.reshape(rows, cols) for t in (w, ga, gb, m, v)])
    return [o.reshape(shape) for o in outs]


def _pad_heads(w, width):
    k = w.shape[0]
    return jnp.pad(w.reshape(k, MLA_HEADS, width), ((0, 0), (0, 0), (0, HEAD_PAD - width))).reshape(k, -1)


def _unpad_heads(w, width):
    k = w.shape[0]
    return w.reshape(k, MLA_HEADS, HEAD_PAD)[:, :, :width].reshape(k, MLA_HEADS * width)


def _rope_tables(positions):
    inv_freq = ROPE_BASE ** (-jnp.arange(0, QK_ROPE, 2, dtype=F32) / QK_ROPE)
    ang = positions.astype(F32)[:, None] * inv_freq
    cos, sin = jnp.cos(ang), jnp.sin(ang)
    S = positions.shape[0]
    ones, zeros = jnp.ones((S, QK_NOPE), F32), jnp.zeros((S, QK_NOPE), F32)
    ctab = jnp.concatenate([ones, cos, cos, ones[:, :HEAD_PAD - QK_DIM]], axis=1)
    stab = jnp.concatenate([zeros, -sin, sin, zeros[:, :HEAD_PAD - QK_DIM]], axis=1)
    return ctab, stab


def _memory_block(x, mem, W, layer, tag):
    mn = _rms(mem, W["xa_norm_mem"][layer], name=f"{tag}_xa_norm_mem")
    kvm = _mm(mn, [(W["xa_w_kv"][layer], 0, 0)], _first, [(2 * D_MODEL, BF16, 0)], tn=2 * D_MODEL, nj=1,
              name=f"{tag}_xa_kv")[0]
    xo, hx, qx, o = _xa_block_fwd(x, kvm, W["xa_w_q"][layer], W["xa_w_o"][layer], W["xa_norm_x"][layer],
                                  name=f"{tag}_xa_fwd")
    return xo, (x, hx, qx, mn, kvm, o)


def _memory_block_bwd(dxo, mem, W, layer, saved, tag, grads):
    x, hx, qx, mn, kvm, o = saved
    wq, wkv, wo = W["xa_w_q"][layer], W["xa_w_kv"][layer], W["xa_w_o"][layer]
    grads["xa_w_o"][layer] = _owner_major(_mm_tn(o, dxo, name=f"{tag}_xa_dwo"), 0)
    dx, dqx, dkvm, dg = _xa_block_bwd(dxo, x, qx, kvm, wq, wo, W["xa_norm_x"][layer], name=f"{tag}_xa_bwd")
    grads["xa_w_q"][layer] = _owner_major(_mm_tn(hx, dqx, name=f"{tag}_xa_dwq"), 0)
    grads["xa_norm_x"][layer] = dg[0]
    dmn = _mm(dkvm, [(wkv, 0, 0)], _first, [(D_MODEL, F32, 0)], nt=True, tn=D_MODEL, nj=1, name=f"{tag}_xa_dmn")[0]
    grads["xa_w_kv"][layer] = _mm_tn_owners(mn, [dkvm], name=f"{tag}_xa_dwkv")
    _, dgm = _rms_bwd(mem, W["xa_norm_mem"][layer], dmn, name=f"{tag}_xa_norm_mem_bwd")
    grads["xa_norm_mem"][layer] = dgm[0]
    return dx


FF_TN = D_FF // 2

def _silu_mul(accs, extras):
    g, u = accs
    return [g * _sigmoid(g) * u, g, u]


def _silu_mul_bwd(accs, extras):
    da = accs[0]
    g, u = extras[0].astype(F32), extras[1].astype(F32)
    sg = _sigmoid(g)
    return [da * u * sg * (1.0 + g * (1.0 - sg)), da * g * sg]


def _ffn_block(x, W, layer, tag):
    hf = _rms(x, W["ffn_norm"][layer], name=f"{tag}_ffn_norm")
    wgu, wd = W["ffn_w_gate_up"][layer], W["ffn_w_down"][layer]
    act, g, u = _mm(hf, [(wgu, 0, 0), (wgu, 0, 2)], _silu_mul, [(D_FF, BF16, 0)] * 3, tn=FF_TN, nj=2,
                    name=f"{tag}_ffn_up")
    xo = _mm(act, [(wd, 0, 0)], _add_res, [(D_MODEL, F32, 0)], extras=[(x, 0)], tn=D_MODEL, nj=1,
             name=f"{tag}_ffn_down")[0]
    return xo, (x, hf, act, g, u)


def _ffn_block_bwd(dxo, W, layer, saved, tag, grads):
    x, hf, act, g, u = saved
    wgu, wd = W["ffn_w_gate_up"][layer], W["ffn_w_down"][layer]
    dg, du = _mm(dxo, [(wd, 0, 0)], _silu_mul_bwd, [(D_FF, BF16, 0)] * 2, nt=True, extras=[(g, 0), (u, 0)], tn=FF_TN,
                 nj=2, name=f"{tag}_ffn_dact")
    grads["ffn_w_down"][layer] = _owner_major(_mm_tn(act, dxo, tk=FF_TN, name=f"{tag}_ffn_dwd"), 0)
    dx, dgn = _mm(dg, [(wgu, 0, 0)], _norm_bwd_epilogue(0), [(D_MODEL, F32, 0)], nt=True, also=(du, (wgu, 0, 1)),
                  extras=[(x, 0), (dxo, 0)], rows=[W["ffn_norm"][layer].reshape(1, D_MODEL)],
                  sums=[D_MODEL], tn=D_MODEL, nj=1, name=f"{tag}_ffn_dhf")
    grads["ffn_w_gate_up"][layer] = _mm_tn_owners(hf, [dg, du], name=f"{tag}_ffn_dwgu")
    grads["ffn_norm"][layer] = dgn[0]
    return dx


def _even_block(x, tabs, W, tag):
    ctab, stab = tabs
    w_in = W["ev_w_in"][0]
    zero = jnp.zeros((D_MODEL, QK_NOPE), BF16)
    w_in_pad = jnp.concatenate([w_in[:, :896], zero, w_in[:, 896:], zero[:, :HEAD_PAD - QK_DIM]], axis=1)
    w_q_pad = _pad_heads(W["ev_w_q_up"][0], QK_DIM)
    wkv = W["ev_w_kv_up"][0].reshape(KV_RANK, MLA_HEADS, QK_NOPE + V_HEAD)
    w_kv_pad = jnp.concatenate([_pad_heads(wkv[:, :, :QK_NOPE].reshape(KV_RANK, -1), QK_NOPE),
                                _pad_heads(wkv[:, :, QK_NOPE:].reshape(KV_RANK, -1), V_HEAD)], axis=1)
    w_out = W["ev_w_out"][0]
    w_att = jnp.pad(w_out[POOL_DIM:].reshape(MLA_HEADS, V_HEAD, D_MODEL), ((0, 0), (0, HEAD_PAD - V_HEAD), (0, 0)))
    w_out_pad = jnp.concatenate([w_out[:POOL_DIM], w_att.reshape(MLA_HEADS * HEAD_PAD, D_MODEL)], axis=0)
    pool_w = W["ev_pool_w"][0].astype(BF16)
    pool_scale = W["ev_pool_scale"]

    h, z, mix, pooled, cqn, ckvn, q_rot, k_cat, v_pad = _even_front(
        x, W["ev_norm"][0], w_in_pad, pool_w, pool_scale, W["ev_q_norm"][0], w_q_pad, W["ev_kv_norm"][0], w_kv_pad,
        ctab, stab, name=f"{tag}_front")
    mix, lse = _flash_fwd(q_rot, k_cat, v_pad, mix, name=f"{tag}_attn")
    xo = _mm(mix, [(w_out_pad, 0, 0)], _add_res, [(D_MODEL, F32, 0)], extras=[(x, 0)], tn=D_MODEL, nj=1,
             name=f"{tag}_out")[0]
    saved = (x, h, z, pooled, cqn, ckvn, q_rot, k_cat, v_pad, lse, mix,
             (w_in_pad, w_q_pad, w_kv_pad, w_out_pad, pool_w, pool_scale))
    return xo, saved


def _even_out_grad(dxo, saved, tag):
    mix = saved[10]
    dw_out_pad = _mm_tn(mix, dxo, tk=MIX_DIM // 3, name=f"{tag}_dw_out")
    datt = dw_out_pad[POOL_DIM:].reshape(MLA_HEADS, HEAD_PAD, D_MODEL)[:, :V_HEAD].reshape(-1, D_MODEL)
    return [_owner_major(jnp.concatenate([dw_out_pad[:POOL_DIM], datt], axis=0), 0)]


def _even_block_bwd(dxo, tabs, W, saved, tag, grads, token=None):
    ctab, stab = tabs
    x, h, z, pooled, cqn, ckvn, q_rot, k_cat, v_pad, lse, mix, wts = saved
    w_in_pad, w_q_pad, w_kv_pad, w_out_pad, pool_w, pool_scale = wts
    if token is not None:
        w_out_pad = w_out_pad + token[0:1, 0:1].astype(BF16)
    dmix = _mm(dxo, [(w_out_pad, 0, 0)], _first, [(MIX_DIM, BF16, 0)], nt=True, tn=MIX_DIM, nj=1,
               name=f"{tag}_dmix")[0]
    delta = _attn_delta(dmix, mix, name=f"{tag}_delta")
    dq_rot, dk_cat, dv_pad = _flash_bwd(q_rot, k_cat, v_pad, dmix, _retile_rows(lse, delta.shape[2]), delta,
                                        name=f"{tag}_attn_bwd")
    dx, dq_pad, dz, dypre, dgq, dgkv, dscale, dgn = _even_back(
        dq_rot, dk_cat, dv_pad, dmix, pooled, z, x, dxo, ctab, stab, w_q_pad, w_kv_pad, w_in_pad, pool_w, pool_scale,
        W["ev_q_norm"][0], W["ev_kv_norm"][0], W["ev_norm"][0], name=f"{tag}_back")
    grads["ev_q_norm"], grads["ev_kv_norm"], grads["ev_pool_scale"], grads["ev_norm"] = dgq, dgkv, dscale, dgn
    dw_q_pad = _mm_tn(cqn, dq_pad, name=f"{tag}_dw_q_up")
    grads["ev_w_q_up"] = [_owner_major(_unpad_heads(dw_q_pad, QK_DIM), 1)]
    dwk = _unpad_heads(_mm_tn(ckvn, dk_cat, name=f"{tag}_dw_k_up"), QK_NOPE).reshape(KV_RANK, MLA_HEADS, QK_NOPE)
    dwv = _unpad_heads(_mm_tn(ckvn, dv_pad, name=f"{tag}_dw_v_up"), V_HEAD).reshape(KV_RANK, MLA_HEADS, V_HEAD)
    grads["ev_w_kv_up"] = [_owner_major(jnp.concatenate([dwk, dwv], axis=2).reshape(KV_RANK, -1), 1)]
    grads["ev_pool_w"] = _mm_tn_grouped(pooled, dypre, 4, POOL_GROUP, name=f"{tag}_dpool_w")[None]
    dw_in_pad = _mm_tn(h, dz, name=f"{tag}_dw_in")
    grads["ev_w_in"] = [_owner_major(jnp.concatenate([dw_in_pad[:, :896], dw_in_pad[:, 960:992]], axis=1), 0)]
    return dx


def _odd_block(x, reset, W, tag):
    w_r, w_i = W["od_w_rgate"][0], W["od_w_igate"][0]
    vecs = [W[n].reshape(1, D_MODEL) for n in ("od_conv_b", "od_b_rgate", "od_b_igate", "od_lambda")]
    h, z, xc, hseq, y = _lru_fwd(x, W["od_norm"][0], W["od_w_in"][0], reset, W["od_conv_w"][0], vecs[0], w_r,
                                 vecs[1], w_i, vecs[2], vecs[3], name=f"{tag}_lru")
    xo = _mm(y, [(W["od_w_out"][0], 0, 0)], _add_res, [(D_MODEL, F32, 0)], extras=[(x, 0)], tn=D_MODEL, nj=1,
             name=f"{tag}_out")[0]
    return xo, (x, h, z, xc, hseq, y, vecs)


def _odd_block_bwd(dxo, reset, W, saved, tag, grads):
    x, h, z, xc, hseq, y, vecs = saved
    w_r, w_i = W["od_w_rgate"][0], W["od_w_igate"][0]
    grads["od_w_out"] = [_owner_major(_mm_tn(y, dxo, name=f"{tag}_dw_out"), 0)]
    dgate, dxc, dpr, dpi, acc = _lru_bwd(dxo, W["od_w_out"][0], z, xc, hseq, reset, w_r, vecs[1], w_i, vecs[2],
                                         vecs[3], name=f"{tag}_lru_bwd")
    grads["od_b_rgate"], grads["od_b_igate"], grads["od_lambda"] = acc[0:1], acc[1:2], acc[2:3]
    grads["od_w_rgate"] = [_owner_major(_mm_tn_grouped(xc, dpr, LRU_HEADS, LRU_HEAD_DIM, name=f"{tag}_dw_rgate"), 1)]
    grads["od_w_igate"] = [_owner_major(_mm_tn_grouped(xc, dpi, LRU_HEADS, LRU_HEAD_DIM, name=f"{tag}_dw_igate"), 1)]
    dxb, cacc = _conv_bwd(dxc, z, W["od_conv_w"][0], name=f"{tag}_conv_bwd")
    grads["od_conv_w"], grads["od_conv_b"] = cacc[None, 0:4], cacc[4:5]
    dz = jnp.concatenate([dgate, dxb], axis=1)
    grads["od_w_in"] = [_mm_tn_owners(h, [dz], name=f"{tag}_dw_in")]
    dx, dgn = _mm(dz, [(W["od_w_in"][0], 0, 0)], _norm_bwd_epilogue(0), [(D_MODEL, F32, 0)], nt=True,
                  extras=[(x, 0), (dxo, 0)], rows=[W["od_norm"][0].reshape(1, D_MODEL)], sums=[D_MODEL], tn=D_MODEL,
                  nj=1, name=f"{tag}_dh")
    grads["od_norm"] = dgn
    return dx


def _local_step(x, mem, positions, target, W, later_weights=None, exchange_earlier=None):
    tabs = _rope_tables(positions)
    reset = (positions == 0).astype(F32)[:, None]
    grads = {n: [None, None] for n in ("xa_norm_x", "xa_norm_mem", "xa_w_q", "xa_w_kv", "xa_w_o", "ffn_norm",
                                       "ffn_w_gate_up", "ffn_w_down")}
    x1, s_even = _even_block(x, tabs, W, "l0_even")
    if later_weights is not None:
        W = {**W, **later_weights(x1)}
    x2, s_xa0 = _memory_block(x1, mem, W, 0, "l0")
    x3, s_ff0 = _ffn_block(x2, W, 0, "l0")
    x4, s_odd = _odd_block(x3, reset, W, "l1_odd")
    x5, s_xa1 = _memory_block(x4, mem, W, 1, "l1")
    x6, s_ff1 = _ffn_block(x5, W, 1, "l1")
    d, dgf, loss = _loss_head(x6, W["final_norm"], target, name="loss_head")
    grads["final_norm"] = dgf[0]
    d = _ffn_block_bwd(d, W, 1, s_ff1, "l1", grads)
    d = _memory_block_bwd(d, mem, W, 1, s_xa1, "l1", grads)
    d = _odd_block_bwd(d, reset, W, s_odd, "l1_odd", grads)
    d = _ffn_block_bwd(d, W, 0, s_ff0, "l0", grads)
    d = _memory_block_bwd(d, mem, W, 0, s_xa0, "l0", grads)
    grads["ev_w_out"] = _even_out_grad(d, s_even, "l0_even")
    token = exchange_earlier(grads) if exchange_earlier is not None else None
    d = _even_block_bwd(d, tabs, W, s_even, "l0_even", grads, token)
    big = {n: grads.pop(n) for n in MATMUL_WEIGHTS}
    for n, v in grads.items():
        if isinstance(v, list):
            grads[n] = jnp.stack(v)
    return loss[0, 0], d, big, grads


WEIGHTS = ("ev_norm", "ev_w_in", "ev_pool_w", "ev_pool_scale", "ev_q_norm", "ev_w_q_up", "ev_kv_norm", "ev_w_kv_up",
           "ev_w_out", "od_norm", "od_w_in", "od_conv_w", "od_conv_b", "od_w_rgate", "od_b_rgate", "od_w_igate",
           "od_b_igate", "od_lambda", "od_w_out", "xa_norm_x", "xa_norm_mem", "xa_w_q", "xa_w_kv", "xa_w_o",
           "ffn_norm", "ffn_w_gate_up", "ffn_w_down", "final_norm")
SHARD_AXIS = {"ev_w_in": 1, "ev_w_q_up": 2, "ev_w_kv_up": 2, "ev_w_out": 1, "od_norm": 1, "od_w_in": 2,
              "od_conv_w": 2, "od_conv_b": 1, "od_w_rgate": 2, "od_b_rgate": 1, "od_w_igate": 2, "od_b_igate": 1,
              "od_lambda": 1, "od_w_out": 1, "xa_w_q": 1, "xa_w_kv": 2, "xa_w_o": 1, "ffn_w_gate_up": 2,
              "ffn_w_down": 1}
MATMUL_WEIGHTS = ("ev_w_in", "ev_w_q_up", "ev_w_kv_up", "ev_w_out", "od_w_in", "od_w_rgate", "od_w_igate",
                  "od_w_out", "xa_w_q", "xa_w_kv", "xa_w_o", "ffn_w_gate_up", "ffn_w_down")
SMALL_SHARDED = tuple(n for n in WEIGHTS if n in SHARD_AXIS and n not in MATMUL_WEIGHTS)
REPLICATED = tuple(n for n in WEIGHTS if n not in SHARD_AXIS)


def _pack(parts, quantum):
    flat = jnp.concatenate([p.reshape(-1) for p in parts])
    pad = (-flat.shape[0]) % quantum
    return jnp.pad(flat, (0, pad)).reshape(-1, LANES)


def _unpack(flat, shapes):
    out, off = [], 0
    for shape in shapes:
        size = math.prod(shape)
        out.append(flat[off:off + size].reshape(shape))
        off += size
    return out


def _run_copies(local, remote, send_sems, recv_sems, local_sems):
    locals_ = [pltpu.make_async_copy(src, dst, local_sems.at[n]) for n, (src, dst) in enumerate(local)]
    for cp in locals_:
        cp.start()
    sends = [pltpu.make_async_remote_copy(src_ref=src, dst_ref=dst, send_sem=send_sems.at[k, n],
                                          recv_sem=recv_sems.at[k, n], device_id=dev, device_id_type=MESH)
             for (k, n, src, dst, _, dev) in remote]
    for cp in sends:
        cp.start()
    for (k, n, src, _, arrival, dev) in remote:
        pltpu.make_async_remote_copy(src_ref=src, dst_ref=arrival, send_sem=send_sems.at[k, n],
                                     recv_sem=recv_sems.at[k, n], device_id=dev, device_id_type=MESH).wait_recv()
    for cp in sends:
        cp.wait_send()
    for cp in locals_:
        cp.wait()


def _chip_peers(x, y):
    return [(1 - x, y), (x, 1 - y), (1 - x, 1 - y)]


def _owner_block(ref, axis, q):
    size = ref.shape[axis] // N_CHIPS
    idx = [slice(None)] * len(ref.shape)
    idx[axis] = pl.ds(q * size, size)
    return ref.at[tuple(idx)]


def _comm_call(body, ins, out_shapes, n_items, n_peers, *, name):
    return pl.pallas_call(
        body, in_specs=[ANY] * len(ins), out_specs=[ANY] * len(out_shapes), out_shape=out_shapes,
        scratch_shapes=[pltpu.SemaphoreType.DMA((n_peers, n_items)), pltpu.SemaphoreType.DMA((n_peers, n_items)),
                        pltpu.SemaphoreType.DMA((n_items,))],
        name=name,
    )(*ins)


def _gather_chips(shards, axes, *, name):
    n = len(shards)
    full = [jax.ShapeDtypeStruct(tuple(d * (N_CHIPS if a == ax else 1) for a, d in enumerate(s.shape)), s.dtype)
            for s, ax in zip(shards, axes)]

    def body(*refs):
        srcs, dsts = refs[:n], refs[n:2 * n]
        x, y, c = lax.axis_index("x"), lax.axis_index("y"), lax.axis_index("c")
        me = 2 * x + y
        local = [(srcs[i], _owner_block(dsts[i], axes[i], me)) for i in range(n)]
        remote = [(k, i, srcs[i], _owner_block(dsts[i], axes[i], me), _owner_block(dsts[i], axes[i], 2 * px + py),
                   (px, py, c))
                  for k, (px, py) in enumerate(_chip_peers(x, y)) for i in range(n)]
        _run_copies(local, remote, *refs[2 * n:])

    return _comm_call(body, shards, full, n, 3, name=name)


HBM = pl.BlockSpec(memory_space=pltpu.HBM)
SEM = pl.BlockSpec(memory_space=pltpu.SEMAPHORE)
DATAFLOW = pltpu.SideEffectType.DATAFLOW_SIDE_EFFECTING


def _gather_plan(axes):
    return lambda srcs, lands, me, peer: [
        (srcs[i], _owner_block(lands[i], ax, me), _owner_block(lands[i], ax, peer)) for i, ax in enumerate(axes)]


def _exchange_plan(where):
    return lambda srcs, lands, me, peer: [
        (srcs[i].at[peer], lands[n].at[me, l], lands[n].at[peer, l]) for i, (n, l) in enumerate(where)]


def _split_peers(sibling):
    x, y, c = lax.axis_index("x"), lax.axis_index("y"), lax.axis_index("c")
    peers = [((px, py, c), 2 * px + py) for px, py in _chip_peers(x, y)]
    return 2 * x + y, peers + ([((x, y, 1 - c), 2 * x + y)] if sibling else [])


def _split_start(srcs, lands, plan, *, sibling=False, name):
    ns, nl = len(srcs), len(lands)
    nsem = (3 + sibling) * len(plan(list(srcs), list(lands), 0, 0))

    def body(*refs):
        src_refs, land_refs = refs[:ns], refs[ns:ns + nl]
        send_sems, recv_sems = refs[ns + nl:ns + nl + nsem], refs[ns + nl + nsem:ns + nl + 2 * nsem]
        me, peers = _split_peers(sibling)
        n = 0
        for device, chip in peers:
            for src, dst, _ in plan(src_refs, land_refs, me, chip):
                pltpu.make_async_remote_copy(src_ref=src, dst_ref=dst, send_sem=send_sems[n], recv_sem=recv_sems[n],
                                             device_id=device, device_id_type=MESH).start()
                n += 1
        refs[-1][...] = jnp.zeros_like(refs[-1])

    arrays = list(srcs) + list(lands)
    out = pl.pallas_call(
        body, name=name, in_specs=[HBM] * (ns + nl),
        out_specs=[SEM] * (2 * nsem) + [HBM] * (ns + nl) + [pl.BlockSpec(memory_space=pltpu.VMEM)],
        out_shape=[pltpu.SemaphoreType.DMA(())] * (2 * nsem) + [pltpu.HBM(a.shape, a.dtype) for a in arrays]
        + [jax.ShapeDtypeStruct((8, LANES), F32)],
        input_output_aliases={i: 2 * nsem + i for i in range(ns + nl)},
        compiler_params=pltpu.CompilerParams(has_side_effects=DATAFLOW),
    )(*[pltpu.with_memory_space_constraint(a, pltpu.HBM) for a in arrays])
    sems, rest = out[:2 * nsem], out[2 * nsem:]
    return sems[:nsem], sems[nsem:], rest[:ns], rest[ns:ns + nl], rest[-1]


def _split_wait(handle, after, plan, *, sibling=False, name):
    send_sems, recv_sems, srcs, lands, _ = handle
    ns, nl, nsem = len(srcs), len(lands), len(send_sems)

    def body(*refs):
        src_refs, land_refs = refs[:ns], refs[ns:ns + nl]
        send_refs, recv_refs = refs[ns + nl:ns + nl + nsem], refs[ns + nl + nsem:ns + nl + 2 * nsem]
        me, peers = _split_peers(sibling)
        n = 0
        for device, chip in peers:
            for src, _, arrival in plan(src_refs, land_refs, me, chip):
                cp = pltpu.make_async_remote_copy(src_ref=src, dst_ref=arrival, send_sem=send_refs[n],
                                                  recv_sem=recv_refs[n], device_id=device, device_id_type=MESH)
                cp.wait_send()
                cp.wait_recv()
                n += 1

    out = pl.pallas_call(
        body, name=name, in_specs=[HBM] * (ns + nl) + [SEM] * (2 * nsem) + [ANY], out_specs=[HBM] * (ns + nl),
        out_shape=[pltpu.HBM(a.shape, a.dtype) for a in list(srcs) + list(lands)],
        input_output_aliases={i: i for i in range(ns + nl)},
        compiler_params=pltpu.CompilerParams(has_side_effects=DATAFLOW),
    )(*srcs, *lands, *send_sems, *recv_sems, after)
    return out[ns:]


def _exchange_sibling(arrays, *, name):
    n = len(arrays)

    def body(*refs):
        x, y, c = lax.axis_index("x"), lax.axis_index("y"), lax.axis_index("c")
        remote = [(0, i, refs[i], refs[n + i], refs[n + i], (x, y, 1 - c)) for i in range(n)]
        _run_copies([], remote, *refs[2 * n:])

    return _comm_call(body, arrays, [jax.ShapeDtypeStruct(a.shape, a.dtype) for a in arrays], n, 1, name=name)


def _sum_slots(r, *, token=None, name):
    shape = r.shape[1:]
    cols = shape[-1]
    rows = math.prod(shape) // cols
    tr = max(d for d in range(8, rows + 1, 8) if rows % d == 0 and d * cols * 16 <= (4 << 20))

    def body(r_ref, *refs):
        total = ((r_ref[0] + r_ref[1]) + r_ref[2]) + r_ref[3]
        refs[-1][...] = total if token is None else total + refs[0][0:1, 0:1]

    in_specs = [pl.BlockSpec((N_CHIPS, tr, cols), lambda i: (0, i, 0))]
    in_specs += [] if token is None else [_const((8, LANES))]
    return pl.pallas_call(
        body, grid=(rows // tr,), in_specs=in_specs,
        out_specs=_rows(tr, cols), out_shape=jax.ShapeDtypeStruct((rows, cols), F32), compiler_params=_cp(1),
        name=name,
    )(r.reshape(N_CHIPS, rows, cols), *([] if token is None else [token])).reshape(shape)


FIRST_WEIGHTS = ("ev_w_in", "ev_w_q_up", "ev_w_kv_up", "ev_w_out")
LATER_WEIGHTS = tuple(n for n in MATMUL_WEIGHTS if n not in FIRST_WEIGHTS)
LAST_GRADS = ("ev_w_in", "ev_w_q_up", "ev_w_kv_up")
EARLIER_GRADS = tuple(n for n in MATMUL_WEIGHTS if n not in LAST_GRADS)


def _my_chip():
    return 2 * lax.axis_index("x") + lax.axis_index("y")


def _gather_first(w):
    small = _pack([w[n] for n in SMALL_SHARDED], 8 * LANES)
    stacked = [n for n in FIRST_WEIGHTS if SHARD_AXIS[n] == w[n].ndim - 1 and w[n].shape[-1] % LANES]
    shards = [w[n].astype(BF16)[None] if n in stacked else w[n].astype(BF16) for n in FIRST_WEIGHTS]
    got = _gather_chips(shards + [small], [0 if n in stacked else SHARD_AXIS[n] for n in FIRST_WEIGHTS] + [0],
                        name="gather_first")
    full = {n: w[n] for n in REPLICATED}
    for n, g in zip(FIRST_WEIGHTS, got[:-1]):
        full[n] = jnp.concatenate([g[q] for q in range(N_CHIPS)], axis=SHARD_AXIS[n]) if n in stacked else g
    per_chip = [_unpack(got[-1][q * small.shape[0]:(q + 1) * small.shape[0]].reshape(-1),
                        [w[n].shape for n in SMALL_SHARDED]) for q in range(N_CHIPS)]
    for i, n in enumerate(SMALL_SHARDED):
        full[n] = jnp.concatenate([per_chip[q][i] for q in range(N_CHIPS)], axis=SHARD_AXIS[n])
    return full


def _gather_later_start(w, after):
    behind = (after.reshape(-1)[0] * 0).astype(BF16)
    shards = [w[n].astype(BF16) + (behind if n == "od_w_rgate" else 0) for n in LATER_WEIGHTS]
    axes = [SHARD_AXIS[n] for n in LATER_WEIGHTS]
    lands = [lax.empty(tuple(d * (N_CHIPS if a == ax else 1) for a, d in enumerate(s.shape)), s.dtype)
             for s, ax in zip(shards, axes)]
    plan = _gather_plan(axes)
    return _split_start(shards, lands, plan, sibling=True, name="gather_later_start"), plan


def _owner_major(g, axis):
    shape = g.shape
    size = shape[axis] // N_CHIPS
    g = jnp.moveaxis(g.reshape(shape[:axis] + (N_CHIPS, size) + shape[axis + 1:]), axis, 0)
    return g.reshape(N_CHIPS, -1, shape[-1] if axis < len(shape) - 1 else size)


def _exchange_start(items, *, cross, name):
    me = _my_chip()
    srcs, lands, where = [], [], []
    for n, layers in enumerate(items):
        land = lax.empty((N_CHIPS, len(layers)) + layers[0].shape[1:], layers[0].dtype)
        for l, a in enumerate(layers):
            if not cross:
                own = lax.dynamic_index_in_dim(a, me, 0, keepdims=True)[:, None]
                land = lax.dynamic_update_slice(land, own, (me, l) + (0,) * (a.ndim - 1))
            srcs.append(a)
            where.append((n, l))
        lands.append(land)
    plan = _exchange_plan(where)
    return _split_start(srcs, lands, plan, sibling=cross, name=name), plan


def _earlier_items(grads, full_shapes):
    small = [_pack([jnp.split(grads[n].reshape(full_shapes[n]), N_CHIPS, axis=SHARD_AXIS[n])[q]
                    for n in SMALL_SHARDED], 8 * LANES) for q in range(N_CHIPS)]
    return [grads[n] for n in EARLIER_GRADS] + [[jnp.stack(small)]]


def _last_items(big, grads, full_shapes, loss):
    repl = _pack([grads[n].reshape(full_shapes[n]) for n in REPLICATED] + [loss.reshape(1)], 8 * LANES)
    return [big[n] for n in LAST_GRADS] + [[jnp.stack([repl] * N_CHIPS)]]


def kernel(
        x, mem, positions, ev_norm, ev_w_in, ev_pool_w, ev_pool_scale, ev_q_norm, ev_w_q_up, ev_kv_norm,
        ev_w_kv_up, ev_w_out, od_norm, od_w_in, od_conv_w, od_conv_b, od_w_rgate, od_b_rgate, od_w_igate,
        od_b_igate, od_lambda, od_w_out, xa_norm_x, xa_norm_mem, xa_w_q, xa_w_kv, xa_w_o, ffn_norm,
        ffn_w_gate_up, ffn_w_down, final_norm, loss_target, m_ev_norm, m_ev_w_in, m_ev_pool_w, m_ev_pool_scale,
        m_ev_q_norm, m_ev_w_q_up, m_ev_kv_norm, m_ev_w_kv_up, m_ev_w_out, m_od_norm, m_od_w_in, m_od_conv_w,
        m_od_conv_b, m_od_w_rgate, m_od_b_rgate, m_od_w_igate, m_od_b_igate, m_od_lambda, m_od_w_out,
        m_xa_norm_x, m_xa_norm_mem, m_xa_w_q, m_xa_w_kv, m_xa_w_o, m_ffn_norm, m_ffn_w_gate_up, m_ffn_w_down,
        m_final_norm, v_ev_norm, v_ev_w_in, v_ev_pool_w, v_ev_pool_scale, v_ev_q_norm, v_ev_w_q_up,
        v_ev_kv_norm, v_ev_w_kv_up, v_ev_w_out, v_od_norm, v_od_w_in, v_od_conv_w, v_od_conv_b, v_od_w_rgate,
        v_od_b_rgate, v_od_w_igate, v_od_b_igate, v_od_lambda, v_od_w_out, v_xa_norm_x, v_xa_norm_mem, v_xa_w_q,
        v_xa_w_kv, v_xa_w_o, v_ffn_norm, v_ffn_w_gate_up, v_ffn_w_down, v_final_norm):
    given = dict(locals())
    w = {n: given[n] for n in WEIGHTS}
    full_shapes = {n: tuple(d * (N_CHIPS if a == SHARD_AXIS.get(n) else 1) for a, d in enumerate(w[n].shape))
                   for n in WEIGHTS}
    full = _gather_first(w)
    later, later_plan = _gather_later_start(w, full["ev_w_out"])
    full["ev_norm"] = full["ev_norm"] + later[4][0:1, 0:1]
    exchange = {}

    def later_weights(after):
        return dict(zip(LATER_WEIGHTS, _split_wait(later, after, later_plan, sibling=True, name="gather_later_wait")))

    def exchange_earlier(grads):
        exchange["handle"], exchange["plan"] = _exchange_start(_earlier_items(grads, full_shapes), cross=True,
                                                               name="exchange_earlier_start")
        return exchange["handle"][4]

    loss, grad_x, big, grads = _local_step(x[0], mem[0], positions[0], loss_target[0], full, later_weights,
                                           exchange_earlier)
    earlier = EARLIER_GRADS + ("small",)
    got = dict(zip(earlier, _split_wait(exchange["handle"], grad_x, exchange["plan"], sibling=True,
                                        name="exchange_earlier_wait")))
    last, last_plan = _exchange_start(_last_items(big, grads, full_shapes, loss), cross=False,
                                      name="exchange_last_start")
    sums = {n: _sum_slots(got[n], token=last[4] if i == 0 else None, name=f"sum_chips_{n}")
            for i, n in enumerate(earlier)}
    got = dict(zip(LAST_GRADS + ("replicated",),
                   _split_wait(last, sums[earlier[-1]], last_plan, name="exchange_last_wait")))
    sums.update({n: _sum_slots(got[n], name=f"sum_chips_{n}") for n in got})
    mine = [sums[n] for n in MATMUL_WEIGHTS + ("small", "replicated")]
    other = _exchange_sibling(mine, name="exchange_sibling")
    out = {}
    for i, n in enumerate(MATMUL_WEIGHTS):
        out[n] = _adamw(w[n], mine[i].reshape(w[n].shape), other[i].reshape(w[n].shape), given["m_" + n],
                        given["v_" + n], name=f"adamw_{n}")
    for i, group in ((len(MATMUL_WEIGHTS), SMALL_SHARDED), (len(MATMUL_WEIGHTS) + 1, REPLICATED)):
        spare = [jnp.zeros((1,), F32)] if group is REPLICATED else []
        packed = [_pack([given[pre + n] for n in group] + spare, 8 * LANES) for pre in ("", "m_", "v_")]
        res = _adamw(packed[0], mine[i].reshape(packed[0].shape), other[i].reshape(packed[0].shape), packed[1],
                     packed[2], name=f"adamw_group{i}")
        shapes = [w[n].shape for n in group] + [(1,)] * len(spare)
        for j, arrs in enumerate(zip(*[_unpack(r.reshape(-1), shapes) for r in res])):
            if j < len(group):
                out[group[j]] = list(arrs)
            else:
                loss = arrs[0][0]
    return (loss, grad_x[None], *[out[n][k] for k in range(4) for n in WEIGHTS])
```

```python
import functools
import math

import jax
import jax.numpy as jnp
from jax import lax
from jax.experimental import pallas as pl
from jax.experimental.pallas import tpu as pltpu

F32 = jnp.float32
BF16 = jnp.bfloat16

D_MODEL = 1024
POOL_DIM = 512
POOL_WINDOWS = (2, 4, 8, 16)
POOL_GROUP = 128
MLA_HEADS = 8
QK_NOPE = 64
QK_ROPE = 32
QK_DIM = QK_NOPE + QK_ROPE
V_HEAD = 64
HEAD_PAD = 128
Q_RANK = 256
KV_RANK = 128
ROPE_BASE = 10000.0
LRU_HEADS = 4
LRU_HEAD_DIM = 256
CONV_WIDTH = 4
LRU_C = 8.0
MEM_HEADS = 4
MEM_HEAD_DIM = 256
D_FF = 2816
RMS_EPS = 1e-6
NEG_INF = -1e30

ADAM_LR = 0.001
ADAM_B1 = 0.9
ADAM_B2 = 0.999
ADAM_EPS = 1e-08
ADAM_WD = 0.01
ADAM_STEP = 10

N_CHIPS = 4
LANES = 128
VMEM_LIMIT = 56 * 1024 * 1024
MESH = pl.DeviceIdType.MESH
ANY = pl.BlockSpec(memory_space=pl.ANY)
MIX_DIM = POOL_DIM + MLA_HEADS * HEAD_PAD

NN = (((1,), (0,)), ((), ()))
NT = (((1,), (1,)), ((), ()))
TN = (((0,), (0,)), ((), ()))


def _cp(n):
    return pltpu.CompilerParams(dimension_semantics=("arbitrary",) * n, vmem_limit_bytes=VMEM_LIMIT)


def _dot(a, b, dims=NN):
    return lax.dot_general(a, b, dims, preferred_element_type=F32)


def _row_tile(S):
    return 1024 if S % 1024 == 0 else min(S, 512)


def _rows(ts, w, cb=0):
    return pl.BlockSpec((ts, w), lambda i: (i, cb))


def _const(shape):
    return pl.BlockSpec(shape, lambda i: (0,) * len(shape))


MM_VMEM_BUDGET = 40 * 1024 * 1024


def _mm(a, bs, epi, outs, *, tn, nj, nt=False, also=None, extras=(), rows=(), sums=(), a_cb=0, k=None, tm=None,
        name):
    M = a.shape[0]
    k = k or a.shape[1]
    nb, ne, nr, no = len(bs), len(extras), len(rows), len(outs)
    lhs = [(a, k, a_cb, b) for b in bs[:1]] + ([(also[0], also[0].shape[1], 0, also[1])] if also else [])
    if tm is None:
        per_row = 2 * (sum(kk * x.dtype.itemsize for x, kk, _, _ in lhs)
                       + sum(e.dtype.itemsize for e, _ in extras) * tn
                       + sum(jnp.dtype(dt).itemsize for _, dt, _ in outs) * tn) + nb * tn * 4
        weights = (1 if nj == 1 else 2) * (sum(b.dtype.itemsize for b, _, _ in bs) * k
                                           + (also[1][0].dtype.itemsize * lhs[-1][1] if also else 0)) * tn
        tm = 1024 if M % 1024 == 0 and 1024 * per_row + weights <= MM_VMEM_BUDGET else min(M, 512)
    dims = NT if nt else NN
    assert not sums or nj == 1
    na = 2 if also else 0

    def body(*refs):
        av = refs[0][...].astype(BF16)
        accs = [_dot(av, r[...].astype(BF16), dims) for r in refs[1:1 + nb]]
        if also:
            accs[0] = accs[0] + _dot(refs[1 + nb][...].astype(BF16), refs[2 + nb][...].astype(BF16), dims)
        refs = refs[:1 + nb] + refs[1 + nb + na:]
        vals = epi(accs, [r[...] for r in refs[1 + nb:1 + nb + ne + nr]])
        outs_refs = refs[1 + nb + ne + nr:]
        for o, v in zip(outs_refs[:no], vals[:no]):
            o[...] = v.astype(o.dtype)
        if sums:
            @pl.when(pl.program_id(1) == 0)
            def _():
                for o in outs_refs[no:]:
                    o[...] = jnp.zeros_like(o)

            for o, v in zip(outs_refs[no:], vals[no:]):
                o[...] += v

    in_specs = [pl.BlockSpec((tm, k), lambda j, i: (i, a_cb))]
    weights = [(k, rb, cb) for (_, rb, cb) in bs]
    if also:
        in_specs_also = pl.BlockSpec((tm, lhs[-1][1]), lambda j, i: (i, 0))
        weights.append((lhs[-1][1], also[1][1], also[1][2]))
    for n, (kk, rb, cb) in enumerate(weights):
        if also and n == nb:
            in_specs.append(in_specs_also)
        mode = dict(pipeline_mode=pl.Buffered(1)) if nj == 1 else {}
        if nt:
            in_specs.append(pl.BlockSpec((tn, kk), lambda j, i, rb=rb, cb=cb: (rb + j, cb), **mode))
        else:
            in_specs.append(pl.BlockSpec((kk, tn), lambda j, i, rb=rb, cb=cb: (rb, cb + j), **mode))
    for (_, cb) in extras:
        in_specs.append(pl.BlockSpec((tm, tn), lambda j, i, cb=cb: (i, cb + j)))
    in_specs += [pl.BlockSpec((1, tn), lambda j, i: (0, 0))] * nr
    out_specs = [pl.BlockSpec((tm, tn), lambda j, i, cb=cb: (i, cb + j)) for (_, _, cb) in outs]
    out_specs += [pl.BlockSpec((1, w), lambda j, i: (0, 0)) for w in sums]
    res = pl.pallas_call(
        body, grid=(nj, M // tm), in_specs=in_specs, out_specs=out_specs,
        out_shape=[jax.ShapeDtypeStruct((M, n), dt) for (n, dt, _) in outs]
        + [jax.ShapeDtypeStruct((1, w), F32) for w in sums],
        compiler_params=_cp(2), name=name,
    )(a, *[b for (b, _, _) in bs], *([also[0], also[1][0]] if also else []), *[e for (e, _) in extras], *rows)
    return res


def _first(accs, extras):
    return [accs[0]]


def _add_res(accs, extras):
    return [accs[0] + extras[0].astype(F32)]


def _norm_bwd_epilogue(partials):
    def epi(accs, vals):
        dh = accs[0]
        for part in vals[:partials]:
            dh = dh + part.astype(F32)
        x, res, g = vals[partials:partials + 3]
        r = lax.rsqrt(jnp.mean(x * x, axis=-1, keepdims=True) + RMS_EPS)
        n = x * r
        dn = dh * g
        return [r * (dn - n * jnp.mean(dn * n, axis=-1, keepdims=True)) + res, jnp.sum(dh * n, axis=0, keepdims=True)]

    return epi


TN_VMEM_BUDGET = 36 * 1024 * 1024


def _contraction_rows(S, row_bytes, out_elems):
    ts = min(S, 2048)
    while ts > 512 and 2 * (ts * row_bytes + out_elems * 4) > TN_VMEM_BUDGET:
        ts //= 2
    return ts


def _mm_tn(a, b, *, ka=None, a_cb=0, nb=None, b_cb=0, tk=None, tn=None, ts=None, name):
    S = a.shape[0]
    ka = ka or a.shape[1]
    nb = nb or b.shape[1]
    tk = tk or ka
    tn = tn or nb
    ts = ts or _contraction_rows(S, tk * a.dtype.itemsize + tn * b.dtype.itemsize, tk * tn)
    a0, b0 = a_cb * (ka // tk), b_cb * (nb // tn)

    def body(a_ref, b_ref, o_ref):
        @pl.when(pl.program_id(2) == 0)
        def _():
            o_ref[...] = jnp.zeros_like(o_ref)

        o_ref[...] += _dot(a_ref[...].astype(BF16), b_ref[...].astype(BF16), TN)

    return pl.pallas_call(
        body, grid=(ka // tk, nb // tn, S // ts),
        in_specs=[pl.BlockSpec((ts, tk), lambda p, q, s: (s, a0 + p)),
                  pl.BlockSpec((ts, tn), lambda p, q, s: (s, b0 + q))],
        out_specs=pl.BlockSpec((tk, tn), lambda p, q, s: (p, q)),
        out_shape=jax.ShapeDtypeStruct((ka, nb), F32), compiler_params=_cp(3), name=name,
    )(a, b)


def _mm_tn_owners(a, bs, *, name):
    S, ka = a.shape
    nb = sum(b.shape[1] for b in bs)
    tn = nb // N_CHIPS
    ts = _contraction_rows(S, ka * a.dtype.itemsize + len(bs) * tn * bs[0].dtype.itemsize, ka * tn)
    per = N_CHIPS // len(bs)

    def body(a_ref, *refs):
        o_ref = refs[-1]
        q = pl.program_id(0)

        @pl.when(pl.program_id(1) == 0)
        def _():
            o_ref[...] = jnp.zeros_like(o_ref)

        av = a_ref[...].astype(BF16)
        for n, b_ref in enumerate(refs[:-1]):
            @pl.when(q // per == n)
            def _():
                o_ref[0] += _dot(av, b_ref[...].astype(BF16), TN)

    in_specs = [pl.BlockSpec((ts, ka), lambda q, s: (s, 0))]
    for n in range(len(bs)):
        in_specs.append(pl.BlockSpec((ts, tn), lambda q, s, n=n: (jnp.where(q // per == n, s, 0),
                                                                  jnp.clip(q - n * per, 0, per - 1))))
    return pl.pallas_call(
        body, grid=(N_CHIPS, S // ts), in_specs=in_specs,
        out_specs=pl.BlockSpec((1, ka, tn), lambda q, s: (q, 0, 0)),
        out_shape=jax.ShapeDtypeStruct((N_CHIPS, ka, tn), F32), compiler_params=_cp(2), name=name,
    )(a, *bs)


def _mm_tn_grouped(a, b, groups, w, *, name):
    S = a.shape[0]
    ts = _contraction_rows(S, w * (a.dtype.itemsize + b.dtype.itemsize), w * w)

    def body(a_ref, b_ref, o_ref):
        @pl.when(pl.program_id(1) == 0)
        def _():
            o_ref[...] = jnp.zeros_like(o_ref)

        o_ref[0] += _dot(a_ref[...].astype(BF16), b_ref[...].astype(BF16), TN)

    return pl.pallas_call(
        body, grid=(groups, S // ts),
        in_specs=[pl.BlockSpec((ts, w), lambda g, s: (s, g)), pl.BlockSpec((ts, w), lambda g, s: (s, g))],
        out_specs=pl.BlockSpec((1, w, w), lambda g, s: (g, 0, 0)),
        out_shape=jax.ShapeDtypeStruct((groups, w, w), F32), compiler_params=_cp(2), name=name,
    )(a, b)


def _rms(x, g, *, cb=0, w=None, ts=None, name):
    S = x.shape[0]
    w = w or x.shape[1]
    ts = ts or _row_tile(S)

    def body(x_ref, g_ref, o_ref):
        xv = x_ref[...].astype(F32)
        r = lax.rsqrt(jnp.mean(xv * xv, axis=-1, keepdims=True) + RMS_EPS)
        o_ref[...] = (xv * r * g_ref[...]).astype(o_ref.dtype)

    return pl.pallas_call(
        body, grid=(S // ts,), in_specs=[_rows(ts, w, cb), _const((1, w))], out_specs=_rows(ts, w),
        out_shape=jax.ShapeDtypeStruct((S, w), BF16), compiler_params=_cp(1), name=name,
    )(x, g.reshape(1, w))


def _rms_bwd(x, g, dy, *, cb=0, w=None, res=None, out_dtype=F32, ts=None, name):
    S = x.shape[0]
    w = w or x.shape[1]
    ts = ts or min(S, 512)
    has_res = res is not None

    def body(*refs):
        x_ref, g_ref, dy_ref = refs[:3]
        dx_ref, dg_ref = refs[-2:]
        xv = x_ref[...].astype(F32)
        r = lax.rsqrt(jnp.mean(xv * xv, axis=-1, keepdims=True) + RMS_EPS)
        n = xv * r
        dyv = dy_ref[...].astype(F32)
        dn = dyv * g_ref[...]
        dx = r * (dn - n * jnp.mean(dn * n, axis=-1, keepdims=True))
        if has_res:
            dx = dx + refs[3][...].astype(F32)
        dx_ref[...] = dx.astype(dx_ref.dtype)

        @pl.when(pl.program_id(0) == 0)
        def _():
            dg_ref[...] = jnp.zeros_like(dg_ref)

        dg_ref[...] += jnp.sum(dyv * n, axis=0, keepdims=True)

    ins = [x, g.reshape(1, w), dy] + ([res] if has_res else [])
    in_specs = [_rows(ts, w, cb), _const((1, w)), _rows(ts, w)] + ([_rows(ts, w)] if has_res else [])
    return pl.pallas_call(
        body, grid=(S // ts,), in_specs=in_specs, out_specs=[_rows(ts, w), _const((1, w))],
        out_shape=[jax.ShapeDtypeStruct((S, w), out_dtype), jax.ShapeDtypeStruct((1, w), F32)],
        compiler_params=_cp(1), name=name,
    )(*ins)


HALO = 16


def _pool_counts(i, ts, rows, first_row):
    t = i * ts + first_row + lax.broadcasted_iota(jnp.int32, (rows, 1), 0)
    return [jnp.minimum(t + 1, w).astype(F32) for w in POOL_WINDOWS]


def _even_front(x, g, w_in, pool_w, pool_scale, g_q, w_q, g_kv, w_kv, ctab, stab, *, name):
    S = x.shape[0]
    ts = min(S, 512)

    def body(x_ref, g_ref, win_ref, pw_ref, sc_ref, gq_ref, wq_ref, gkv_ref, wkv_ref, c_ref, s_ref,
             h_ref, z_ref, y_ref, p_ref, cqn_ref, ckvn_ref, q_ref, k_ref, v_ref, tail):
        i = pl.program_id(0)

        def normed(t, gain):
            r = lax.rsqrt(jnp.mean(t * t, axis=-1, keepdims=True) + RMS_EPS)
            return (t * r * gain).astype(BF16)

        h = normed(x_ref[...], g_ref[...])
        h_ref[...] = h
        z = _dot(h, win_ref[...])
        z_ref[...] = z
        u = z[:, :POOL_DIM]
        xe = jnp.concatenate([jnp.where(i > 0, tail[...], 0.0), u], axis=0)
        tail[...] = u[ts - HALO:]
        sums = []
        s = xe
        for sh in (1, 2, 4, 8):
            s = s + pltpu.roll(s, sh, 0)
            sums.append(s)
        cnts = _pool_counts(i, ts, ts, 0)
        for grp in range(4):
            lo, hi = grp * POOL_GROUP, (grp + 1) * POOL_GROUP
            pooled = (sums[grp][HALO:, lo:hi] / cnts[grp] - u[:, lo:hi]).astype(BF16)
            p_ref[:, lo:hi] = pooled
            y_ref[:, lo:hi] = (_dot(pooled, pw_ref[grp]) * sc_ref[:, lo:hi]).astype(y_ref.dtype)
        cqn = normed(z[:, POOL_DIM:POOL_DIM + Q_RANK], gq_ref[...])
        ckvn = normed(z[:, POOL_DIM + Q_RANK:POOL_DIM + Q_RANK + KV_RANK], gkv_ref[...])
        cqn_ref[...] = cqn
        ckvn_ref[...] = ckvn
        q = _dot(cqn, wq_ref[...])
        kv = _dot(ckvn, wkv_ref[...])
        c, sn = c_ref[...], s_ref[...]
        kr = z[:, D_MODEL - HEAD_PAD:]
        kr_rot = kr * c + _rope_partner(kr) * sn
        lane = lax.broadcasted_iota(jnp.int32, (ts, HEAD_PAD), 1)
        for hd in range(MLA_HEADS):
            lo, hi = hd * HEAD_PAD, (hd + 1) * HEAD_PAD
            qh = q[:, lo:hi]
            q_ref[:, lo:hi] = (qh * c + _rope_partner(qh) * sn).astype(q_ref.dtype)
            k_ref[:, lo:hi] = (kv[:, lo:hi] + kr_rot).astype(k_ref.dtype)
            v_ref[:, lo:hi] = jnp.where(lane == V_HEAD, 1.0, kv[:, D_MODEL + lo:D_MODEL + hi]).astype(v_ref.dtype)

    wide = jax.ShapeDtypeStruct((S, D_MODEL), BF16)
    return pl.pallas_call(
        body, grid=(S // ts,),
        in_specs=[_rows(ts, D_MODEL), _const((1, D_MODEL)), _const((D_MODEL, D_MODEL)),
                  _const((4, POOL_GROUP, POOL_GROUP)), _const((1, POOL_DIM)), _const((1, Q_RANK)),
                  _const((Q_RANK, D_MODEL)), _const((1, KV_RANK)), _const((KV_RANK, 2 * D_MODEL)),
                  _rows(ts, HEAD_PAD), _rows(ts, HEAD_PAD)],
        out_specs=[_rows(ts, D_MODEL), _rows(ts, D_MODEL), _rows(ts, POOL_DIM), _rows(ts, POOL_DIM),
                   _rows(ts, Q_RANK), _rows(ts, KV_RANK), _rows(ts, D_MODEL), _rows(ts, D_MODEL), _rows(ts, D_MODEL)],
        out_shape=[wide, jax.ShapeDtypeStruct((S, D_MODEL), F32), jax.ShapeDtypeStruct((S, MIX_DIM), BF16),
                   jax.ShapeDtypeStruct((S, POOL_DIM), BF16), jax.ShapeDtypeStruct((S, Q_RANK), BF16),
                   jax.ShapeDtypeStruct((S, KV_RANK), BF16), wide, wide, wide],
        scratch_shapes=[pltpu.VMEM((HALO, POOL_DIM), F32)], compiler_params=_cp(1), name=name,
    )(x, g.reshape(1, D_MODEL), w_in, pool_w, pool_scale, g_q.reshape(1, Q_RANK), w_q, g_kv.reshape(1, KV_RANK), w_kv,
      ctab, stab)


def _norm_bwd_values(xv, gain, dy):
    r = lax.rsqrt(jnp.mean(xv * xv, axis=-1, keepdims=True) + RMS_EPS)
    n = xv * r
    dn = dy * gain
    return r * (dn - n * jnp.mean(dn * n, axis=-1, keepdims=True)), jnp.sum(dy * n, axis=0, keepdims=True)


def _even_back(dq_rot, dk_cat, dv, dmix, pooled, z, x, dxo, ctab, stab, w_q, w_kv, w_in, pool_w, pool_scale, g_q, g_kv,
               g_x, *, name):
    S = x.shape[0]
    ts = min(S, 512)
    nh = ts // HALO
    last = S // HALO - 1
    n = ts + HALO

    def body(dq_ref, dk_ref, dv_ref, dy_ref, dyh_ref, p_ref, z_ref, x_ref, dxo_ref, c_ref, s_ref, wq_ref, wkv_ref,
             win_ref, pw_ref, sc_ref, gq_ref, gkv_ref, gx_ref,
             dx_ref, dqp_ref, dz_ref, dyp_ref, dgq_ref, dgkv_ref, dsc_ref, dgx_ref):
        i = pl.program_id(0)

        @pl.when(i == 0)
        def _():
            for ref in (dgq_ref, dgkv_ref, dsc_ref, dgx_ref):
                ref[...] = jnp.zeros_like(ref)

        c, sn = c_ref[...], s_ref[...]
        z = z_ref[...]
        dk = dk_ref[...]
        for hd in range(MLA_HEADS):
            lo, hi = hd * HEAD_PAD, (hd + 1) * HEAD_PAD
            g = dq_ref[:, lo:hi]
            dqp_ref[:, lo:hi] = (g * c + _rope_partner(g * sn)).astype(dqp_ref.dtype)
            heads_sum = dk[:, lo:hi] if hd == 0 else heads_sum + dk[:, lo:hi]
        lane = lax.broadcasted_iota(jnp.int32, heads_sum.shape, 1)
        dkr = jnp.where((lane >= QK_NOPE) & (lane < QK_DIM), heads_sum * c + _rope_partner(heads_sum * sn), 0.0)
        dcqn = _dot(dqp_ref[...], wq_ref[...], NT)
        dckvn = _dot(dk.astype(BF16), wkv_ref[:, :D_MODEL], NT) + _dot(dv_ref[...].astype(BF16),
                                                                       wkv_ref[:, D_MODEL:], NT)
        dcq, dgq = _norm_bwd_values(z[:, POOL_DIM:POOL_DIM + Q_RANK], gq_ref[...], dcqn)
        dckv, dgkv = _norm_bwd_values(z[:, POOL_DIM + Q_RANK:POOL_DIM + Q_RANK + KV_RANK], gkv_ref[...], dckvn)
        dgq_ref[...] += dgq
        dgkv_ref[...] += dgkv
        dyv = dy_ref[...].astype(F32)
        dyh = jnp.where(i < pl.num_programs(0) - 1, dyh_ref[...].astype(F32), 0.0)
        dypre = (jnp.concatenate([dyv, dyh], axis=0) * sc_ref[...]).astype(BF16)
        dyp_ref[...] = dypre[:ts]
        cnts = _pool_counts(i, ts, n, 0)
        dsc = []
        for grp in range(4):
            lo, hi = grp * POOL_GROUP, (grp + 1) * POOL_GROUP
            dsc.append(jnp.sum(dyv[:, lo:hi] * _dot(p_ref[:, lo:hi], pw_ref[grp]), axis=0, keepdims=True))
            dpool = _dot(dypre[:, lo:hi], pw_ref[grp], NT)
            s = dpool / cnts[grp]
            for sh in (1, 2, 4, 8)[:grp + 1]:
                s = s + pltpu.roll(s, n - sh, 0)
            dz_ref[:, lo:hi] = (s[:ts] - dpool[:ts]).astype(dz_ref.dtype)
        dsc_ref[...] += jnp.concatenate(dsc, axis=1)
        dz_ref[:, POOL_DIM:POOL_DIM + Q_RANK] = dcq.astype(dz_ref.dtype)
        dz_ref[:, POOL_DIM + Q_RANK:POOL_DIM + Q_RANK + KV_RANK] = dckv.astype(dz_ref.dtype)
        dz_ref[:, D_MODEL - HEAD_PAD:] = dkr.astype(dz_ref.dtype)
        dx, dgx = _norm_bwd_values(x_ref[...], gx_ref[...], _dot(dz_ref[...], win_ref[...], NT))
        dx_ref[...] = dx + dxo_ref[...]
        dgx_ref[...] += dgx

    wide, pool = _rows(ts, D_MODEL), _rows(ts, POOL_DIM)
    f32 = lambda w: jax.ShapeDtypeStruct((1, w), F32)
    return pl.pallas_call(
        body, grid=(S // ts,),
        in_specs=[wide, wide, wide, pool,
                  pl.BlockSpec((HALO, POOL_DIM), lambda i: (jnp.minimum((i + 1) * nh, last), 0)), pool, wide, wide, wide,
                  _rows(ts, HEAD_PAD), _rows(ts, HEAD_PAD), _const((Q_RANK, D_MODEL)), _const((KV_RANK, 2 * D_MODEL)),
                  _const((D_MODEL, D_MODEL)), _const((4, POOL_GROUP, POOL_GROUP)), _const((1, POOL_DIM)),
                  _const((1, Q_RANK)), _const((1, KV_RANK)), _const((1, D_MODEL))],
        out_specs=[wide, wide, wide, pool, _const((1, Q_RANK)), _const((1, KV_RANK)), _const((1, POOL_DIM)),
                   _const((1, D_MODEL))],
        out_shape=[jax.ShapeDtypeStruct((S, D_MODEL), F32), jax.ShapeDtypeStruct((S, D_MODEL), BF16),
                   jax.ShapeDtypeStruct((S, D_MODEL), BF16), jax.ShapeDtypeStruct((S, POOL_DIM), BF16),
                   f32(Q_RANK), f32(KV_RANK), f32(POOL_DIM), f32(D_MODEL)],
        compiler_params=_cp(1), name=name,
    )(dq_rot, dk_cat, dv, dmix, dmix, pooled, z, x, dxo, ctab, stab, w_q, w_kv, w_in, pool_w, pool_scale,
      g_q.reshape(1, Q_RANK), g_kv.reshape(1, KV_RANK), g_x.reshape(1, D_MODEL))


def _rope_partner(t):
    lane = lax.broadcasted_iota(jnp.int32, t.shape, 1)
    swapped = jnp.where(lane < QK_NOPE + QK_ROPE // 2, pltpu.roll(t, HEAD_PAD - QK_ROPE // 2, 1),
                        pltpu.roll(t, QK_ROPE // 2, 1))
    return jnp.where((lane >= QK_NOPE) & (lane < QK_DIM), swapped, 0.0)


ATT_SCALE = QK_DIM ** -0.5
LOG2E = math.log2(math.e)


HEADS_PER_STEP = 2
ATT_COL0 = POOL_DIM // HEAD_PAD


FWD_TILE = 1024


def _stat_rows(col):
    return jnp.broadcast_to(col, (col.shape[0], LANES)).T[0:8]


def _retile_rows(rows, tq):
    heads, n8, t = rows.shape
    if t == tq:
        return rows
    flat = rows.reshape(heads, n8 // 8, 8, t)[:, :, 0].reshape(heads, -1, 1, tq)
    return jnp.broadcast_to(flat, (heads, flat.shape[1], 8, tq)).reshape(heads, -1, tq)


def _flash_fwd(q, k, v, mix, *, name):
    S = q.shape[0]
    tq = FWD_TILE if S % FWD_TILE == 0 else min(S, 512)
    nq = S // tq
    hs = HEADS_PER_STEP
    wide = hs * HEAD_PAD

    def body(q_ref, k_ref, v_ref, mix_ref, o_ref, lse_ref):
        qi = pl.program_id(1)
        qv = [q_ref[:, a * HEAD_PAD:(a + 1) * HEAD_PAD] for a in range(hs)]

        def update(m, acc, s, v):
            m_new = jnp.maximum(m, jnp.max(s, axis=-1, keepdims=True))
            p = jnp.exp2((s - m_new) * (ATT_SCALE * LOG2E))
            alpha = jnp.exp2((m - m_new) * (ATT_SCALE * LOG2E))
            return m_new, alpha * acc + _dot(p.astype(BF16), v)

        def step(j, carry, masked):
            off = pl.multiple_of(j * tq, tq)
            out = []
            for a in range(hs):
                head = slice(a * HEAD_PAD, (a + 1) * HEAD_PAD)
                s = _dot(qv[a], k_ref[pl.ds(off, tq), head], NT)
                if masked:
                    row = lax.broadcasted_iota(jnp.int32, (tq, tq), 0)
                    col = lax.broadcasted_iota(jnp.int32, (tq, tq), 1)
                    s = jnp.where(col <= row, s, NEG_INF)
                out.append(update(*carry[a], s, v_ref[pl.ds(off, tq), head]))
            return tuple(out)

        one = (jnp.full((tq, 1), NEG_INF, F32), jnp.zeros((tq, HEAD_PAD), F32))
        carry = step(qi, lax.fori_loop(0, qi, lambda j, c: step(j, c, False), (one,) * hs), True)
        for a in range(hs):
            m, acc = carry[a]
            l = acc[:, V_HEAD:V_HEAD + 1]
            o_ref[:, a * HEAD_PAD:(a + 1) * HEAD_PAD] = (acc / l).astype(o_ref.dtype)
            lse_ref[a] = _stat_rows(m * ATT_SCALE + jnp.log(l))

    blk = pl.BlockSpec((tq, wide), lambda h, i: (i, h))
    full = pl.BlockSpec((S, wide), lambda h, i: (0, h))
    return pl.pallas_call(
        body, grid=(MLA_HEADS // hs, nq), in_specs=[blk, full, full, ANY],
        out_specs=[pl.BlockSpec((tq, wide), lambda h, i: (i, ATT_COL0 // hs + h)),
                   pl.BlockSpec((hs, 8, tq), lambda h, i: (h, i, 0))],
        out_shape=[jax.ShapeDtypeStruct(mix.shape, mix.dtype), jax.ShapeDtypeStruct((MLA_HEADS, nq * 8, tq), F32)],
        input_output_aliases={3: 0}, compiler_params=_cp(2), name=name,
    )(q, k, v, mix)


BWD_TILE = 1024
BWD_HEADS_PER_STEP = 1


def _bwd_tile(S):
    return BWD_TILE if S % BWD_TILE == 0 else min(S, 512)


def _attn_delta(dmix, mix, *, name):
    S = mix.shape[0]
    ts = _bwd_tile(S)
    half = MLA_HEADS // 2
    halves = [_rows(ts, half * HEAD_PAD, 1), _rows(ts, half * HEAD_PAD, 2)]

    def body(do0_ref, do1_ref, o0_ref, o1_ref, d_ref):
        for n, (do_ref, o_ref) in enumerate(((do0_ref, o0_ref), (do1_ref, o1_ref))):
            prod = do_ref[...].astype(F32) * o_ref[...].astype(F32)
            for a in range(half):
                d_ref[n * half + a] = _stat_rows(
                    jnp.sum(prod[:, a * HEAD_PAD:(a + 1) * HEAD_PAD], axis=-1, keepdims=True))

    return pl.pallas_call(
        body, grid=(S // ts,), in_specs=halves + halves,
        out_specs=pl.BlockSpec((MLA_HEADS, 8, ts), lambda i: (0, i, 0)),
        out_shape=jax.ShapeDtypeStruct((MLA_HEADS, (S // ts) * 8, ts), F32), compiler_params=_cp(1), name=name,
    )(dmix, dmix, mix, mix)


def _flash_bwd(q, k, v, dmix, lse_rows, delta_rows, *, name):
    S = q.shape[0]
    tq = _bwd_tile(S)
    nq = S // tq
    hs = BWD_HEADS_PER_STEP
    wide = hs * HEAD_PAD

    def body(q_hbm, do_hbm, lse_ref, dl_ref, k_ref, v_ref, dq_hbm, dk_ref, dv_ref, q_all, do_all, dq_all):
        g, j = pl.program_id(0), pl.program_id(1)
        cols = pl.multiple_of(g * wide, wide)

        @pl.when(j == 0)
        def _():
            pltpu.sync_copy(q_hbm.at[:, pl.ds(cols, wide)], q_all)
            pltpu.sync_copy(do_hbm.at[:, pl.ds(POOL_DIM + cols, wide)], do_all)
            dq_all[...] = jnp.zeros_like(dq_all)

        heads = [slice(a * HEAD_PAD, (a + 1) * HEAD_PAD) for a in range(hs)]
        kv = [k_ref[:, a] for a in heads]
        vv = [v_ref[:, a] for a in heads]

        def block(a, keys, rows, lse2, dl, first_query):
            qv, dov = q_all[rows, heads[a]], do_all[rows, heads[a]]
            st = _dot(kv[a][:keys], qv, NT)
            if first_query is not None:
                krow = lax.broadcasted_iota(jnp.int32, st.shape, 0)
                qcol = lax.broadcasted_iota(jnp.int32, st.shape, 1) + first_query
                st = jnp.where(krow <= qcol, st, NEG_INF)
            pt = jnp.exp2(st * (ATT_SCALE * LOG2E) - lse2)
            dst = (pt * (_dot(vv[a][:keys], dov, NT) - dl)).astype(BF16)
            dq_all[rows, heads[a]] += _dot(dst, kv[a][:keys], TN)
            return _dot(dst, qv), _dot(pt.astype(BF16), dov)

        def stats(a, i):
            off8 = pl.multiple_of(i * 8, 8)
            return lse_ref[a, pl.ds(off8, 8), :][0:1] * LOG2E, dl_ref[a, pl.ds(off8, 8), :][0:1]

        def step(i, carry):
            rows = pl.ds(pl.multiple_of(i * tq, tq), tq)
            out = []
            for a in range(hs):
                dk, dv = block(a, tq, rows, *stats(a, i), None)
                out.append((carry[a][0] + dk, carry[a][1] + dv))
            return tuple(out)

        def diagonal():
            half = tq // 2
            out = []
            for a in range(hs):
                lse2, dl = stats(a, j)
                off = pl.multiple_of(j * tq, tq)
                dk0, dv0 = block(a, half, pl.ds(off, half), lse2[:, :half], dl[:, :half], 0)
                dk1, dv1 = block(a, tq, pl.ds(pl.multiple_of(off + half, half), half), lse2[:, half:], dl[:, half:], half)
                zero = jnp.zeros((tq - half, HEAD_PAD), F32)
                out.append((dk1 + jnp.concatenate([dk0, zero], axis=0), dv1 + jnp.concatenate([dv0, zero], axis=0)))
            return tuple(out)

        carry = lax.fori_loop(j + 1, nq, step, diagonal())
        for a in range(hs):
            dk_ref[:, heads[a]] = carry[a][0] * ATT_SCALE
            dv_ref[:, heads[a]] = carry[a][1]

        @pl.when(j == nq - 1)
        def _():
            dq_all[...] = dq_all[...] * ATT_SCALE
            pltpu.sync_copy(dq_all, dq_hbm.at[:, pl.ds(cols, wide)])

    blk = pl.BlockSpec((tq, wide), lambda g, j: (j, g))
    stat = pl.BlockSpec((hs, nq * 8, tq), lambda g, j: (g, 0, 0))
    full = jax.ShapeDtypeStruct((S, MLA_HEADS * HEAD_PAD), F32)
    return pl.pallas_call(
        body, grid=(MLA_HEADS // hs, nq), in_specs=[ANY, ANY, stat, stat, blk, blk], out_specs=[ANY, blk, blk],
        out_shape=[full, full, full],
        scratch_shapes=[pltpu.VMEM((S, wide), BF16), pltpu.VMEM((S, wide), BF16), pltpu.VMEM((S, wide), F32)],
        compiler_params=_cp(2), name=name,
    )(q, dmix, lse_rows, delta_rows, k, v)


MEM_SCALE = MEM_HEAD_DIM ** -0.5


def _xattn_probs(qh, kh):
    s = _dot(qh, kh, NT) * MEM_SCALE
    e = jnp.exp(s - jnp.max(s, axis=-1, keepdims=True))
    return e / jnp.sum(e, axis=-1, keepdims=True)


def _xa_block_fwd(x, kvm, w_q, w_o, g, *, name):
    S = x.shape[0]
    ts = min(S, 512)
    nm = kvm.shape[0]

    def body(x_ref, kv_ref, wq_ref, wo_ref, g_ref, xo_ref, hx_ref, q_ref, o_ref):
        xv = x_ref[...]
        r = lax.rsqrt(jnp.mean(xv * xv, axis=-1, keepdims=True) + RMS_EPS)
        hx = (xv * r * g_ref[...]).astype(BF16)
        hx_ref[...] = hx
        q = _dot(hx, wq_ref[...]).astype(BF16)
        q_ref[...] = q
        for h in range(MEM_HEADS):
            lo, hi = h * MEM_HEAD_DIM, (h + 1) * MEM_HEAD_DIM
            p = _xattn_probs(q[:, lo:hi], kv_ref[:, lo:hi])
            o_ref[:, lo:hi] = _dot(p.astype(BF16), kv_ref[:, D_MODEL + lo:D_MODEL + hi]).astype(o_ref.dtype)
        xo_ref[...] = xv + _dot(o_ref[...], wo_ref[...])

    square = _const((D_MODEL, D_MODEL))
    act = jax.ShapeDtypeStruct((S, D_MODEL), BF16)
    return pl.pallas_call(
        body, grid=(S // ts,),
        in_specs=[_rows(ts, D_MODEL), _const((nm, 2 * D_MODEL)), square, square, _const((1, D_MODEL))],
        out_specs=[_rows(ts, D_MODEL)] * 4, out_shape=[jax.ShapeDtypeStruct((S, D_MODEL), F32), act, act, act],
        compiler_params=_cp(1), name=name,
    )(x, kvm, w_q, w_o, g.reshape(1, D_MODEL))


def _xa_block_bwd(dxo, x, q, kvm, w_q, w_o, g, *, name):
    S = q.shape[0]
    ts = min(S, 512)
    nm = kvm.shape[0]

    def body(dxo_ref, x_ref, q_ref, kv_ref, wq_ref, wo_ref, g_ref, dx_ref, dq_ref, dkv_ref, dg_ref):
        @pl.when(pl.program_id(0) == 0)
        def _():
            dkv_ref[...] = jnp.zeros_like(dkv_ref)
            dg_ref[...] = jnp.zeros_like(dg_ref)

        dxo = dxo_ref[...]
        do = _dot(dxo.astype(BF16), wo_ref[...], NT).astype(BF16)
        for h in range(MEM_HEADS):
            lo, hi = h * MEM_HEAD_DIM, (h + 1) * MEM_HEAD_DIM
            qh, kh, vh = q_ref[:, lo:hi], kv_ref[:, lo:hi], kv_ref[:, D_MODEL + lo:D_MODEL + hi]
            doh = do[:, lo:hi]
            p = _xattn_probs(qh, kh)
            dp = _dot(doh, vh, NT)
            ds = (p * (dp - jnp.sum(dp * p, axis=-1, keepdims=True)) * MEM_SCALE).astype(BF16)
            dq_ref[:, lo:hi] = _dot(ds, kh).astype(dq_ref.dtype)
            dkv_ref[:, lo:hi] += _dot(ds, qh, TN)
            dkv_ref[:, D_MODEL + lo:D_MODEL + hi] += _dot(p.astype(BF16), doh, TN)
        dx, dg = _norm_bwd_epilogue(0)([_dot(dq_ref[...], wq_ref[...], NT)], [x_ref[...], dxo, g_ref[...]])
        dx_ref[...] = dx
        dg_ref[...] += dg

    square = _const((D_MODEL, D_MODEL))
    return pl.pallas_call(
        body, grid=(S // ts,),
        in_specs=[_rows(ts, D_MODEL), _rows(ts, D_MODEL), _rows(ts, D_MODEL), _const((nm, 2 * D_MODEL)), square,
                  square, _const((1, D_MODEL))],
        out_specs=[_rows(ts, D_MODEL), _rows(ts, D_MODEL), _const((nm, 2 * D_MODEL)), _const((1, D_MODEL))],
        out_shape=[jax.ShapeDtypeStruct((S, D_MODEL), F32), jax.ShapeDtypeStruct((S, D_MODEL), BF16),
                   jax.ShapeDtypeStruct((nm, 2 * D_MODEL), F32), jax.ShapeDtypeStruct((1, D_MODEL), F32)],
        compiler_params=_cp(1), name=name,
    )(dxo, x, q, kvm, w_q, w_o, g.reshape(1, D_MODEL))


CONV_HALO = 8


def _sigmoid(x):
    return 0.5 * jnp.tanh(0.5 * x) + 0.5


def _softplus(x):
    return jnp.maximum(x, 0.0) + jnp.log(1.0 + jnp.exp(-jnp.abs(x)))


def _neg_expm1(x):
    series = -x * (1.0 + x * (1.0 / 2) * (1.0 + x * (1.0 / 3) * (1.0 + x * (1.0 / 4) * (1.0 + x * (1.0 / 5)))))
    return jnp.where(x > -0.05, series, 1.0 - jnp.exp(x))


GELU_C = math.sqrt(2.0 / math.pi)


def _gelu(x):
    return 0.5 * x * (1.0 + jnp.tanh(GELU_C * (x + 0.044715 * x * x * x)))


def _gelu_grad(x):
    t = jnp.tanh(GELU_C * (x + 0.044715 * x * x * x))
    return 0.5 * (1.0 + t) + 0.5 * x * (1.0 - t * t) * GELU_C * (1.0 + 3 * 0.044715 * x * x)


def _lru_gates(xc, wr_ref, br, wi_ref, bi, sp, reset):
    xcb = xc.astype(BF16)
    pr, pi = [], []
    for h in range(LRU_HEADS):
        lo, hi = h * LRU_HEAD_DIM, (h + 1) * LRU_HEAD_DIM
        pr.append(_dot(xcb[:, lo:hi], wr_ref[h]))
        pi.append(_dot(xcb[:, lo:hi], wi_ref[h]))
    r = _sigmoid(jnp.concatenate(pr, axis=1) + br)
    ig = _sigmoid(jnp.concatenate(pi, axis=1) + bi)
    log_a = -LRU_C * r * sp
    a = jnp.where(reset, 0.0, jnp.exp(log_a))
    mult = jnp.where(reset, 1.0, jnp.sqrt(jnp.maximum(_neg_expm1(2.0 * log_a), 0.0)))
    return r, ig, a, mult


SUBLANES = 8


def _compose_groups(a, b, reverse):
    n = a.shape[0]
    row = lax.broadcasted_iota(jnp.int32, a.shape, 0) % SUBLANES
    for s in (1, 2, 4):
        inside = (row < SUBLANES - s) if reverse else (row >= s)
        shift = n - s if reverse else s
        a_s = jnp.where(inside, pltpu.roll(a, shift, 0), 1.0)
        b_s = jnp.where(inside, pltpu.roll(b, shift, 0), 0.0)
        b = a * b_s + b
        a = a * a_s
    return a, b


def _chain_groups(a_buf, h_ref, state, reverse):
    groups = a_buf.shape[0] // SUBLANES

    def group(g, h_in):
        off = pl.multiple_of((groups - 1 - g if reverse else g) * SUBLANES, SUBLANES)
        h = a_buf[pl.ds(off, SUBLANES), :] * h_in + h_ref[pl.ds(off, SUBLANES), :]
        h_ref[pl.ds(off, SUBLANES), :] = h
        return jnp.broadcast_to(h[0:1] if reverse else h[SUBLANES - 1:SUBLANES], h.shape)

    return lax.fori_loop(0, groups, group, state, unroll=4)[0:1]


def _lru_fwd(x, g, w_in, reset, conv_w, conv_b, w_r, b_r, w_i, b_i, lam, *, name):
    S = x.shape[0]
    ts = min(S, 512)
    W = D_MODEL

    def body(x_ref, g_ref, win_ref, rs_ref, cw_ref, cb_ref, wr_ref, br_ref, wi_ref, bi_ref, lam_ref,
             hn_ref, z_ref, xc_ref, h_ref, y_ref, a_buf, carry, tail):
        i = pl.program_id(0)

        @pl.when(i == 0)
        def _():
            carry[...] = jnp.zeros_like(carry)
            tail[...] = jnp.zeros_like(tail)

        xv = x_ref[...]
        hn = (xv * lax.rsqrt(jnp.mean(xv * xv, axis=-1, keepdims=True) + RMS_EPS) * g_ref[...]).astype(BF16)
        hn_ref[...] = hn
        z_ref[...] = _dot(hn, win_ref[...])
        xb = z_ref[:, W:]
        xe = jnp.concatenate([tail[...], xb], axis=0)
        tail[...] = xb[ts - CONV_HALO:]
        xc = cb_ref[...] + cw_ref[3:4, :] * xe[CONV_HALO:]
        for kk in range(CONV_WIDTH - 1):
            xc = xc + cw_ref[kk:kk + 1, :] * pltpu.roll(xe, CONV_WIDTH - 1 - kk, 0)[CONV_HALO:]
        xc_ref[...] = xc
        reset = rs_ref[...] > 0.5
        _, ig, a, mult = _lru_gates(xc, wr_ref, br_ref[...], wi_ref, bi_ref[...], _softplus(-lam_ref[...]), reset)
        a_buf[...], h_ref[...] = _compose_groups(a, mult * (ig * xc), False)
        carry[...] = _chain_groups(a_buf, h_ref, jnp.broadcast_to(carry[...], (SUBLANES, W)), False)
        y_ref[...] = (_gelu(z_ref[:, :W]) * h_ref[...]).astype(y_ref.dtype)

    vec = _const((1, W))
    gw = _const((LRU_HEADS, LRU_HEAD_DIM, LRU_HEAD_DIM))
    return pl.pallas_call(
        body, grid=(S // ts,),
        in_specs=[_rows(ts, W), vec, _const((W, 2 * W)), _rows(ts, 1), _const((CONV_WIDTH, W)), vec, gw, vec, gw, vec,
                  vec],
        out_specs=[_rows(ts, W), _rows(ts, 2 * W), _rows(ts, W), _rows(ts, W), _rows(ts, W)],
        out_shape=[jax.ShapeDtypeStruct((S, W), BF16), jax.ShapeDtypeStruct((S, 2 * W), F32),
                   jax.ShapeDtypeStruct((S, W), F32), jax.ShapeDtypeStruct((S, W), F32),
                   jax.ShapeDtypeStruct((S, W), BF16)],
        scratch_shapes=[pltpu.VMEM((ts, W), F32), pltpu.VMEM((1, W), F32), pltpu.VMEM((CONV_HALO, W), F32)],
        compiler_params=_cp(1), name=name,
    )(x, g.reshape(1, W), w_in, reset, conv_w, conv_b, w_r, b_r, w_i, b_i, lam)


def _lru_bwd(dxo, w_out, z, xc, hseq, reset, w_r, b_r, w_i, b_i, lam, *, name):
    S = z.shape[0]
    ts = min(S, 512)
    nt = S // ts
    nh = ts // CONV_HALO
    W = D_MODEL

    def body(dxo_ref, wout_ref, gate_ref, xc_ref, h_ref, hh_ref, rs_ref, wr_ref, br_ref, wi_ref, bi_ref, lam_ref,
             dg_ref, dxc_ref, dpr_ref, dpi_ref, acc_ref, a_buf, dh_buf, carry):
        i = pl.program_id(0)
        tile = nt - 1 - i

        @pl.when(i == 0)
        def _():
            carry[...] = jnp.zeros_like(carry)
            acc_ref[...] = jnp.zeros_like(acc_ref)

        xc = xc_ref[...]
        lam_v = lam_ref[...]
        sp = _softplus(-lam_v)
        reset = rs_ref[...] > 0.5
        r, ig, a, mult = _lru_gates(xc, wr_ref, br_ref[...], wi_ref, bi_ref[...], sp, reset)
        gate = gate_ref[...]
        dyv = _dot(dxo_ref[...].astype(BF16), wout_ref[...], NT)
        h = h_ref[...]
        dg_ref[...] = (dyv * h * _gelu_grad(gate)).astype(dg_ref.dtype)
        last_row = lax.broadcasted_iota(jnp.int32, a.shape, 0) == ts - 1
        a_buf[...], dh_buf[...] = _compose_groups(jnp.where(last_row, 1.0, pltpu.roll(a, ts - 1, 0)),
                                                  dyv * _gelu(gate), True)
        _chain_groups(a_buf, dh_buf, jnp.broadcast_to(carry[...], (SUBLANES, W)), True)
        dh = dh_buf[...]
        carry[...] = a[0:1] * dh[0:1]
        hh = jnp.where(tile > 0, hh_ref[...], 0.0)
        h_prev = pltpu.roll(jnp.concatenate([hh, h], axis=0), 1, 0)[CONV_HALO:]
        da = dh * h_prev
        bx = ig * xc
        dmult = dh * bx
        dbx = dh * mult
        di = dbx * xc
        dlog_a = jnp.where(reset, 0.0, da * a - dmult * a * a / jnp.maximum(mult, 1e-30))
        dr = dlog_a * (-LRU_C) * sp
        dpre_r = dr * r * (1.0 - r)
        dpre_i = di * ig * (1.0 - ig)
        dprb, dpib = dpre_r.astype(BF16), dpre_i.astype(BF16)
        dpr_ref[...] = dprb
        dpi_ref[...] = dpib
        back = []
        for hd in range(LRU_HEADS):
            lo, hi = hd * LRU_HEAD_DIM, (hd + 1) * LRU_HEAD_DIM
            back.append(_dot(dprb[:, lo:hi], wr_ref[hd], NT) + _dot(dpib[:, lo:hi], wi_ref[hd], NT))
        dxc_ref[...] = dbx * ig + jnp.concatenate(back, axis=1)
        dlam = jnp.sum(dlog_a * (-LRU_C) * r, axis=0, keepdims=True) * (-_sigmoid(-lam_v))
        acc_ref[0:1, :] += jnp.sum(dpre_r, axis=0, keepdims=True)
        acc_ref[1:2, :] += jnp.sum(dpre_i, axis=0, keepdims=True)
        acc_ref[2:3, :] += dlam

    rev = lambda cb: pl.BlockSpec((ts, W), lambda i: (nt - 1 - i, cb))
    vec = _const((1, W))
    gw = _const((LRU_HEADS, LRU_HEAD_DIM, LRU_HEAD_DIM))
    return pl.pallas_call(
        body, grid=(nt,),
        in_specs=[rev(0), _const((W, W)), rev(0), rev(0), rev(0),
                  pl.BlockSpec((CONV_HALO, W), lambda i: (jnp.maximum((nt - 1 - i) * nh - 1, 0), 0)),
                  pl.BlockSpec((ts, 1), lambda i: (nt - 1 - i, 0)), gw, vec, gw, vec, vec],
        out_specs=[rev(0), rev(0), rev(0), rev(0), _const((8, W))],
        out_shape=[jax.ShapeDtypeStruct((S, W), BF16), jax.ShapeDtypeStruct((S, W), F32),
                   jax.ShapeDtypeStruct((S, W), BF16), jax.ShapeDtypeStruct((S, W), BF16),
                   jax.ShapeDtypeStruct((8, W), F32)],
        scratch_shapes=[pltpu.VMEM((ts, W), F32), pltpu.VMEM((ts, W), F32), pltpu.VMEM((1, W), F32)],
        compiler_params=_cp(1), name=name,
    )(dxo, w_out, z, xc, hseq, hseq, reset, w_r, b_r, w_i, b_i, lam)


def _conv_bwd(dxc, z, conv_w, *, name):
    S = dxc.shape[0]
    ts = min(S, 512)
    nh = ts // CONV_HALO
    last = S // CONV_HALO - 1
    W = D_MODEL
    n = ts + CONV_HALO

    def body(d_ref, dn_ref, xb_ref, xp_ref, cw_ref, dxb_ref, acc_ref):
        i = pl.program_id(0)

        @pl.when(i == 0)
        def _():
            acc_ref[...] = jnp.zeros_like(acc_ref)

        d = d_ref[...]
        de = jnp.concatenate([d, jnp.where(i < pl.num_programs(0) - 1, dn_ref[...], 0.0)], axis=0)
        xe = jnp.concatenate([jnp.where(i > 0, xp_ref[...], 0.0), xb_ref[...]], axis=0)
        dxb = cw_ref[3:4, :] * d
        acc_ref[3:4, :] += jnp.sum(d * xe[CONV_HALO:], axis=0, keepdims=True)
        for kk in range(CONV_WIDTH - 1):
            sh = CONV_WIDTH - 1 - kk
            dxb = dxb + cw_ref[kk:kk + 1, :] * pltpu.roll(de, n - sh, 0)[:ts]
            acc_ref[kk:kk + 1, :] += jnp.sum(d * pltpu.roll(xe, sh, 0)[CONV_HALO:], axis=0, keepdims=True)
        dxb_ref[...] = dxb.astype(dxb_ref.dtype)
        acc_ref[4:5, :] += jnp.sum(d, axis=0, keepdims=True)

    return pl.pallas_call(
        body, grid=(S // ts,),
        in_specs=[_rows(ts, W), pl.BlockSpec((CONV_HALO, W), lambda i: (jnp.minimum((i + 1) * nh, last), 0)),
                  _rows(ts, W, 1), pl.BlockSpec((CONV_HALO, W), lambda i: (jnp.maximum(i * nh - 1, 0), 1)),
                  _const((CONV_WIDTH, W))],
        out_specs=[_rows(ts, W), _const((8, W))],
        out_shape=[jax.ShapeDtypeStruct((S, W), BF16), jax.ShapeDtypeStruct((8, W), F32)],
        compiler_params=_cp(1), name=name,
    )(dxc, dxc, z, z, conv_w)


def _loss_head(x, g, target, *, name):
    S, D = x.shape
    ts = _row_tile(S)

    def body(x_ref, g_ref, t_ref, dx_ref, dg_ref, l_ref):
        @pl.when(pl.program_id(0) == 0)
        def _():
            dg_ref[...] = jnp.zeros_like(dg_ref)
            l_ref[...] = jnp.zeros_like(l_ref)

        xv = x_ref[...]
        r = lax.rsqrt(jnp.mean(xv * xv, axis=-1, keepdims=True) + RMS_EPS)
        n = xv * r
        err = n * g_ref[...] - t_ref[...]
        l_ref[...] += 0.5 * jnp.sum(jnp.sum(err * err, axis=-1, keepdims=True) * (1.0 / D), axis=0, keepdims=True)
        dy = err * (1.0 / D)
        dn = dy * g_ref[...]
        dx_ref[...] = r * (dn - n * jnp.mean(dn * n, axis=-1, keepdims=True))
        dg_ref[...] += jnp.sum(dy * n, axis=0, keepdims=True)

    return pl.pallas_call(
        body, grid=(S // ts,), in_specs=[_rows(ts, D), _const((1, D)), _rows(ts, D)],
        out_specs=[_rows(ts, D), _const((1, D)), _const((8, LANES))],
        out_shape=[jax.ShapeDtypeStruct((S, D), F32), jax.ShapeDtypeStruct((1, D), F32),
                   jax.ShapeDtypeStruct((8, LANES), F32)],
        compiler_params=_cp(1), name=name,
    )(x, g.reshape(1, D), target)


def _adamw(w, ga, gb, m, v, *, name):
    shape = w.shape
    cols = shape[-1]
    rows = w.size // cols
    br = rows
    if rows * cols * 4 > (1 << 20):
        br = max(d for d in range(8, rows + 1, 8) if rows % d == 0 and d * cols * 4 <= (1 << 20))

    def body(w_ref, ga_ref, gb_ref, m_ref, v_ref, g_ref, d_ref, mo_ref, vo_ref):
        gv = ga_ref[...] + gb_ref[...]
        g_ref[...] = gv
        mn = ADAM_B1 * m_ref[...] + (1.0 - ADAM_B1) * gv
        vn = ADAM_B2 * v_ref[...] + (1.0 - ADAM_B2) * (gv * gv)
        m_hat = mn / (1.0 - ADAM_B1 ** ADAM_STEP)
        v_hat = vn / (1.0 - ADAM_B2 ** ADAM_STEP)
        d_ref[...] = -ADAM_LR * (m_hat / (jnp.sqrt(v_hat) + ADAM_EPS) + ADAM_WD * w_ref[...])
        mo_ref[...] = mn
        vo_ref[...] = vn

    spec = _rows(br, cols)
    outs = pl.pallas_call(
        body, grid=(rows // br,), in_specs=[spec] * 5, out_specs=[spec] * 4,
        out_shape=[jax.ShapeDtypeStruct((rows, cols), F32)] * 4, compiler_params=_cp(1), name=name,
    )(*[t.reshape(rows, cols) for t in (w, ga, gb, m, v)])
    return [o.reshape(shape) for o in outs]


def _pad_heads(w, width):
    k = w.shape[0]
    return jnp.pad(w.reshape(k, MLA_HEADS, width), ((0, 0), (0, 0), (0, HEAD_PAD - width))).reshape(k, -1)


def _unpad_heads(w, width):
    k = w.shape[0]
    return w.reshape(k, MLA_HEADS, HEAD_PAD)[:, :, :width].reshape(k, MLA_HEADS * width)


def _rope_tables(positions):
    inv_freq = ROPE_BASE ** (-jnp.arange(0, QK_ROPE, 2, dtype=F32) / QK_ROPE)
    ang = positions.astype(F32)[:, None] * inv_freq
    cos, sin = jnp.cos(ang), jnp.sin(ang)
    S = positions.shape[0]
    ones, zeros = jnp.ones((S, QK_NOPE), F32), jnp.zeros((S, QK_NOPE), F32)
    ctab = jnp.concatenate([ones, cos, cos, ones[:, :HEAD_PAD - QK_DIM]], axis=1)
    stab = jnp.concatenate([zeros, -sin, sin, zeros[:, :HEAD_PAD - QK_DIM]], axis=1)
    return ctab, stab


def _memory_block(x, mem, W, layer, tag):
    mn = _rms(mem, W["xa_norm_mem"][layer], name=f"{tag}_xa_norm_mem")
    kvm = _mm(mn, [(W["xa_w_kv"][layer], 0, 0)], _first, [(2 * D_MODEL, BF16, 0)], tn=2 * D_MODEL, nj=1,
              name=f"{tag}_xa_kv")[0]
    xo, hx, qx, o = _xa_block_fwd(x, kvm, W["xa_w_q"][layer], W["xa_w_o"][layer], W["xa_norm_x"][layer],
                                  name=f"{tag}_xa_fwd")
    return xo, (x, hx, qx, mn, kvm, o)


def _memory_block_bwd(dxo, mem, W, layer, saved, tag, grads):
    x, hx, qx, mn, kvm, o = saved
    wq, wkv, wo = W["xa_w_q"][layer], W["xa_w_kv"][layer], W["xa_w_o"][layer]
    grads["xa_w_o"][layer] = _owner_major(_mm_tn(o, dxo, name=f"{tag}_xa_dwo"), 0)
    dx, dqx, dkvm, dg = _xa_block_bwd(dxo, x, qx, kvm, wq, wo, W["xa_norm_x"][layer], name=f"{tag}_xa_bwd")
    grads["xa_w_q"][layer] = _owner_major(_mm_tn(hx, dqx, name=f"{tag}_xa_dwq"), 0)
    grads["xa_norm_x"][layer] = dg[0]
    dmn = _mm(dkvm, [(wkv, 0, 0)], _first, [(D_MODEL, F32, 0)], nt=True, tn=D_MODEL, nj=1, name=f"{tag}_xa_dmn")[0]
    grads["xa_w_kv"][layer] = _mm_tn_owners(mn, [dkvm], name=f"{tag}_xa_dwkv")
    _, dgm = _rms_bwd(mem, W["xa_norm_mem"][layer], dmn, name=f"{tag}_xa_norm_mem_bwd")
    grads["xa_norm_mem"][layer] = dgm[0]
    return dx


FF_TN = D_FF // 2

def _silu_mul(accs, extras):
    g, u = accs
    return [g * _sigmoid(g) * u, g, u]


def _silu_mul_bwd(accs, extras):
    da = accs[0]
    g, u = extras[0].astype(F32), extras[1].astype(F32)
    sg = _sigmoid(g)
    return [da * u * sg * (1.0 + g * (1.0 - sg)), da * g * sg]


def _ffn_block(x, W, layer, tag):
    hf = _rms(x, W["ffn_norm"][layer], name=f"{tag}_ffn_norm")
    wgu, wd = W["ffn_w_gate_up"][layer], W["ffn_w_down"][layer]
    act, g, u = _mm(hf, [(wgu, 0, 0), (wgu, 0, 2)], _silu_mul, [(D_FF, BF16, 0)] * 3, tn=FF_TN, nj=2,
                    name=f"{tag}_ffn_up")
    xo = _mm(act, [(wd, 0, 0)], _add_res, [(D_MODEL, F32, 0)], extras=[(x, 0)], tn=D_MODEL, nj=1,
             name=f"{tag}_ffn_down")[0]
    return xo, (x, hf, act, g, u)


def _ffn_block_bwd(dxo, W, layer, saved, tag, grads):
    x, hf, act, g, u = saved
    wgu, wd = W["ffn_w_gate_up"][layer], W["ffn_w_down"][layer]
    dg, du = _mm(dxo, [(wd, 0, 0)], _silu_mul_bwd, [(D_FF, BF16, 0)] * 2, nt=True, extras=[(g, 0), (u, 0)], tn=FF_TN,
                 nj=2, name=f"{tag}_ffn_dact")
    grads["ffn_w_down"][layer] = _owner_major(_mm_tn(act, dxo, tk=FF_TN, name=f"{tag}_ffn_dwd"), 0)
    dx, dgn = _mm(dg, [(wgu, 0, 0)], _norm_bwd_epilogue(0), [(D_MODEL, F32, 0)], nt=True, also=(du, (wgu, 0, 1)),
                  extras=[(x, 0), (dxo, 0)], rows=[W["ffn_norm"][layer].reshape(1, D_MODEL)],
                  sums=[D_MODEL], tn=D_MODEL, nj=1, name=f"{tag}_ffn_dhf")
    grads["ffn_w_gate_up"][layer] = _mm_tn_owners(hf, [dg, du], name=f"{tag}_ffn_dwgu")
    grads["ffn_norm"][layer] = dgn[0]
    return dx


def _even_block(x, tabs, W, tag):
    ctab, stab = tabs
    w_in = W["ev_w_in"][0]
    zero = jnp.zeros((D_MODEL, QK_NOPE), BF16)
    w_in_pad = jnp.concatenate([w_in[:, :896], zero, w_in[:, 896:], zero[:, :HEAD_PAD - QK_DIM]], axis=1)
    w_q_pad = _pad_heads(W["ev_w_q_up"][0], QK_DIM)
    wkv = W["ev_w_kv_up"][0].reshape(KV_RANK, MLA_HEADS, QK_NOPE + V_HEAD)
    w_kv_pad = jnp.concatenate([_pad_heads(wkv[:, :, :QK_NOPE].reshape(KV_RANK, -1), QK_NOPE),
                                _pad_heads(wkv[:, :, QK_NOPE:].reshape(KV_RANK, -1), V_HEAD)], axis=1)
    w_out = W["ev_w_out"][0]
    w_att = jnp.pad(w_out[POOL_DIM:].reshape(MLA_HEADS, V_HEAD, D_MODEL), ((0, 0), (0, HEAD_PAD - V_HEAD), (0, 0)))
    w_out_pad = jnp.concatenate([w_out[:POOL_DIM], w_att.reshape(MLA_HEADS * HEAD_PAD, D_MODEL)], axis=0)
    pool_w = W["ev_pool_w"][0].astype(BF16)
    pool_scale = W["ev_pool_scale"]

    h, z, mix, pooled, cqn, ckvn, q_rot, k_cat, v_pad = _even_front(
        x, W["ev_norm"][0], w_in_pad, pool_w, pool_scale, W["ev_q_norm"][0], w_q_pad, W["ev_kv_norm"][0], w_kv_pad,
        ctab, stab, name=f"{tag}_front")
    mix, lse = _flash_fwd(q_rot, k_cat, v_pad, mix, name=f"{tag}_attn")
    xo = _mm(mix, [(w_out_pad, 0, 0)], _add_res, [(D_MODEL, F32, 0)], extras=[(x, 0)], tn=D_MODEL, nj=1,
             name=f"{tag}_out")[0]
    saved = (x, h, z, pooled, cqn, ckvn, q_rot, k_cat, v_pad, lse, mix,
             (w_in_pad, w_q_pad, w_kv_pad, w_out_pad, pool_w, pool_scale))
    return xo, saved


def _even_out_grad(dxo, saved, tag):
    mix = saved[10]
    dw_out_pad = _mm_tn(mix, dxo, tk=MIX_DIM // 3, name=f"{tag}_dw_out")
    datt = dw_out_pad[POOL_DIM:].reshape(MLA_HEADS, HEAD_PAD, D_MODEL)[:, :V_HEAD].reshape(-1, D_MODEL)
    return [_owner_major(jnp.concatenate([dw_out_pad[:POOL_DIM], datt], axis=0), 0)]


def _even_block_bwd(dxo, tabs, W, saved, tag, grads, token=None):
    ctab, stab = tabs
    x, h, z, pooled, cqn, ckvn, q_rot, k_cat, v_pad, lse, mix, wts = saved
    w_in_pad, w_q_pad, w_kv_pad, w_out_pad, pool_w, pool_scale = wts
    if token is not None:
        w_out_pad = w_out_pad + token[0:1, 0:1].astype(BF16)
    dmix = _mm(dxo, [(w_out_pad, 0, 0)], _first, [(MIX_DIM, BF16, 0)], nt=True, tn=MIX_DIM, nj=1,
               name=f"{tag}_dmix")[0]
    delta = _attn_delta(dmix, mix, name=f"{tag}_delta")
    dq_rot, dk_cat, dv_pad = _flash_bwd(q_rot, k_cat, v_pad, dmix, _retile_rows(lse, delta.shape[2]), delta,
                                        name=f"{tag}_attn_bwd")
    dx, dq_pad, dz, dypre, dgq, dgkv, dscale, dgn = _even_back(
        dq_rot, dk_cat, dv_pad, dmix, pooled, z, x, dxo, ctab, stab, w_q_pad, w_kv_pad, w_in_pad, pool_w, pool_scale,
        W["ev_q_norm"][0], W["ev_kv_norm"][0], W["ev_norm"][0], name=f"{tag}_back")
    grads["ev_q_norm"], grads["ev_kv_norm"], grads["ev_pool_scale"], grads["ev_norm"] = dgq, dgkv, dscale, dgn
    dw_q_pad = _mm_tn(cqn, dq_pad, name=f"{tag}_dw_q_up")
    grads["ev_w_q_up"] = [_owner_major(_unpad_heads(dw_q_pad, QK_DIM), 1)]
    dwk = _unpad_heads(_mm_tn(ckvn, dk_cat, name=f"{tag}_dw_k_up"), QK_NOPE).reshape(KV_RANK, MLA_HEADS, QK_NOPE)
    dwv = _unpad_heads(_mm_tn(ckvn, dv_pad, name=f"{tag}_dw_v_up"), V_HEAD).reshape(KV_RANK, MLA_HEADS, V_HEAD)
    grads["ev_w_kv_up"] = [_owner_major(jnp.concatenate([dwk, dwv], axis=2).reshape(KV_RANK, -1), 1)]
    grads["ev_pool_w"] = _mm_tn_grouped(pooled, dypre, 4, POOL_GROUP, name=f"{tag}_dpool_w")[None]
    dw_in_pad = _mm_tn(h, dz, name=f"{tag}_dw_in")
    grads["ev_w_in"] = [_owner_major(jnp.concatenate([dw_in_pad[:, :896], dw_in_pad[:, 960:992]], axis=1), 0)]
    return dx


def _odd_block(x, reset, W, tag):
    w_r, w_i = W["od_w_rgate"][0], W["od_w_igate"][0]
    vecs = [W[n].reshape(1, D_MODEL) for n in ("od_conv_b", "od_b_rgate", "od_b_igate", "od_lambda")]
    h, z, xc, hseq, y = _lru_fwd(x, W["od_norm"][0], W["od_w_in"][0], reset, W["od_conv_w"][0], vecs[0], w_r,
                                 vecs[1], w_i, vecs[2], vecs[3], name=f"{tag}_lru")
    xo = _mm(y, [(W["od_w_out"][0], 0, 0)], _add_res, [(D_MODEL, F32, 0)], extras=[(x, 0)], tn=D_MODEL, nj=1,
             name=f"{tag}_out")[0]
    return xo, (x, h, z, xc, hseq, y, vecs)


def _odd_block_bwd(dxo, reset, W, saved, tag, grads):
    x, h, z, xc, hseq, y, vecs = saved
    w_r, w_i = W["od_w_rgate"][0], W["od_w_igate"][0]
    grads["od_w_out"] = [_owner_major(_mm_tn(y, dxo, name=f"{tag}_dw_out"), 0)]
    dgate, dxc, dpr, dpi, acc = _lru_bwd(dxo, W["od_w_out"][0], z, xc, hseq, reset, w_r, vecs[1], w_i, vecs[2],
                                         vecs[3], name=f"{tag}_lru_bwd")
    grads["od_b_rgate"], grads["od_b_igate"], grads["od_lambda"] = acc[0:1], acc[1:2], acc[2:3]
    grads["od_w_rgate"] = [_owner_major(_mm_tn_grouped(xc, dpr, LRU_HEADS, LRU_HEAD_DIM, name=f"{tag}_dw_rgate"), 1)]
    grads["od_w_igate"] = [_owner_major(_mm_tn_grouped(xc, dpi, LRU_HEADS, LRU_HEAD_DIM, name=f"{tag}_dw_igate"), 1)]
    dxb, cacc = _conv_bwd(dxc, z, W["od_conv_w"][0], name=f"{tag}_conv_bwd")
    grads["od_conv_w"], grads["od_conv_b"] = cacc[None, 0:4], cacc[4:5]
    dz = jnp.concatenate([dgate, dxb], axis=1)
    grads["od_w_in"] = [_mm_tn_owners(h, [dz], name=f"{tag}_dw_in")]
    dx, dgn = _mm(dz, [(W["od_w_in"][0], 0, 0)], _norm_bwd_epilogue(0), [(D_MODEL, F32, 0)], nt=True,
                  extras=[(x, 0), (dxo, 0)], rows=[W["od_norm"][0].reshape(1, D_MODEL)], sums=[D_MODEL], tn=D_MODEL,
                  nj=1, name=f"{tag}_dh")
    grads["od_norm"] = dgn
    return dx


def _local_step(x, mem, positions, target, W, later_weights=None, exchange_earlier=None):
    tabs = _rope_tables(positions)
    reset = (positions == 0).astype(F32)[:, None]
    grads = {n: [None, None] for n in ("xa_norm_x", "xa_norm_mem", "xa_w_q", "xa_w_kv", "xa_w_o", "ffn_norm",
                                       "ffn_w_gate_up", "ffn_w_down")}
    x1, s_even = _even_block(x, tabs, W, "l0_even")
    if later_weights is not None:
        W = {**W, **later_weights(x1)}
    x2, s_xa0 = _memory_block(x1, mem, W, 0, "l0")
    x3, s_ff0 = _ffn_block(x2, W, 0, "l0")
    x4, s_odd = _odd_block(x3, reset, W, "l1_odd")
    x5, s_xa1 = _memory_block(x4, mem, W, 1, "l1")
    x6, s_ff1 = _ffn_block(x5, W, 1, "l1")
    d, dgf, loss = _loss_head(x6, W["final_norm"], target, name="loss_head")
    grads["final_norm"] = dgf[0]
    d = _ffn_block_bwd(d, W, 1, s_ff1, "l1", grads)
    d = _memory_block_bwd(d, mem, W, 1, s_xa1, "l1", grads)
    d = _odd_block_bwd(d, reset, W, s_odd, "l1_odd", grads)
    d = _ffn_block_bwd(d, W, 0, s_ff0, "l0", grads)
    d = _memory_block_bwd(d, mem, W, 0, s_xa0, "l0", grads)
    grads["ev_w_out"] = _even_out_grad(d, s_even, "l0_even")
    token = exchange_earlier(grads) if exchange_earlier is not None else None
    d = _even_block_bwd(d, tabs, W, s_even, "l0_even", grads, token)
    big = {n: grads.pop(n) for n in MATMUL_WEIGHTS}
    for n, v in grads.items():
        if isinstance(v, list):
            grads[n] = jnp.stack(v)
    return loss[0, 0], d, big, grads


WEIGHTS = ("ev_norm", "ev_w_in", "ev_pool_w", "ev_pool_scale", "ev_q_norm", "ev_w_q_up", "ev_kv_norm", "ev_w_kv_up",
           "ev_w_out", "od_norm", "od_w_in", "od_conv_w", "od_conv_b", "od_w_rgate", "od_b_rgate", "od_w_igate",
           "od_b_igate", "od_lambda", "od_w_out", "xa_norm_x", "xa_norm_mem", "xa_w_q", "xa_w_kv", "xa_w_o",
           "ffn_norm", "ffn_w_gate_up", "ffn_w_down", "final_norm")
SHARD_AXIS = {"ev_w_in": 1, "ev_w_q_up": 2, "ev_w_kv_up": 2, "ev_w_out": 1, "od_norm": 1, "od_w_in": 2,
              "od_conv_w": 2, "od_conv_b": 1, "od_w_rgate": 2, "od_b_rgate": 1, "od_w_igate": 2, "od_b_igate": 1,
              "od_lambda": 1, "od_w_out": 1, "xa_w_q": 1, "xa_w_kv": 2, "xa_w_o": 1, "ffn_w_gate_up": 2,
              "ffn_w_down": 1}
MATMUL_WEIGHTS = ("ev_w_in", "ev_w_q_up", "ev_w_kv_up", "ev_w_out", "od_w_in", "od_w_rgate", "od_w_igate",
                  "od_w_out", "xa_w_q", "xa_w_kv", "xa_w_o", "ffn_w_gate_up", "ffn_w_down")
SMALL_SHARDED = tuple(n for n in WEIGHTS if n in SHARD_AXIS and n not in MATMUL_WEIGHTS)
REPLICATED = tuple(n for n in WEIGHTS if n not in SHARD_AXIS)


def _pack(parts, quantum):
    flat = jnp.concatenate([p.reshape(-1) for p in parts])
    pad = (-flat.shape[0]) % quantum
    return jnp.pad(flat, (0, pad)).reshape(-1, LANES)


def _unpack(flat, shapes):
    out, off = [], 0
    for shape in shapes:
        size = math.prod(shape)
        out.append(flat[off:off + size].reshape(shape))
        off += size
    return out


def _run_copies(local, remote, send_sems, recv_sems, local_sems):
    locals_ = [pltpu.make_async_copy(src, dst, local_sems.at[n]) for n, (src, dst) in enumerate(local)]
    for cp in locals_:
        cp.start()
    sends = [pltpu.make_async_remote_copy(src_ref=src, dst_ref=dst, send_sem=send_sems.at[k, n],
                                          recv_sem=recv_sems.at[k, n], device_id=dev, device_id_type=MESH)
             for (k, n, src, dst, _, dev) in remote]
    for cp in sends:
        cp.start()
    for (k, n, src, _, arrival, dev) in remote:
        pltpu.make_async_remote_copy(src_ref=src, dst_ref=arrival, send_sem=send_sems.at[k, n],
                                     recv_sem=recv_sems.at[k, n], device_id=dev, device_id_type=MESH).wait_recv()
    for cp in sends:
        cp.wait_send()
    for cp in locals_:
        cp.wait()


def _chip_peers(x, y):
    return [(1 - x, y), (x, 1 - y), (1 - x, 1 - y)]


def _owner_block(ref, axis, q):
    size = ref.shape[axis] // N_CHIPS
    idx = [slice(None)] * len(ref.shape)
    idx[axis] = pl.ds(q * size, size)
    return ref.at[tuple(idx)]


def _comm_call(body, ins, out_shapes, n_items, n_peers, *, name):
    return pl.pallas_call(
        body, in_specs=[ANY] * len(ins), out_specs=[ANY] * len(out_shapes), out_shape=out_shapes,
        scratch_shapes=[pltpu.SemaphoreType.DMA((n_peers, n_items)), pltpu.SemaphoreType.DMA((n_peers, n_items)),
                        pltpu.SemaphoreType.DMA((n_items,))],
        name=name,
    )(*ins)


def _gather_chips(shards, axes, *, name):
    n = len(shards)
    full = [jax.ShapeDtypeStruct(tuple(d * (N_CHIPS if a == ax else 1) for a, d in enumerate(s.shape)), s.dtype)
            for s, ax in zip(shards, axes)]

    def body(*refs):
        srcs, dsts = refs[:n], refs[n:2 * n]
        x, y, c = lax.axis_index("x"), lax.axis_index("y"), lax.axis_index("c")
        me = 2 * x + y
        local = [(srcs[i], _owner_block(dsts[i], axes[i], me)) for i in range(n)]
        remote = [(k, i, srcs[i], _owner_block(dsts[i], axes[i], me), _owner_block(dsts[i], axes[i], 2 * px + py),
                   (px, py, c))
                  for k, (px, py) in enumerate(_chip_peers(x, y)) for i in range(n)]
        _run_copies(local, remote, *refs[2 * n:])

    return _comm_call(body, shards, full, n, 3, name=name)


HBM = pl.BlockSpec(memory_space=pltpu.HBM)
SEM = pl.BlockSpec(memory_space=pltpu.SEMAPHORE)
DATAFLOW = pltpu.SideEffectType.DATAFLOW_SIDE_EFFECTING


def _gather_plan(axes):
    return lambda srcs, lands, me, peer: [
        (srcs[i], _owner_block(lands[i], ax, me), _owner_block(lands[i], ax, peer)) for i, ax in enumerate(axes)]


def _exchange_plan(where):
    return lambda srcs, lands, me, peer: [
        (srcs[i].at[peer], lands[n].at[me, l], lands[n].at[peer, l]) for i, (n, l) in enumerate(where)]


def _split_peers(sibling):
    x, y, c = lax.axis_index("x"), lax.axis_index("y"), lax.axis_index("c")
    peers = [((px, py, c), 2 * px + py) for px, py in _chip_peers(x, y)]
    return 2 * x + y, peers + ([((x, y, 1 - c), 2 * x + y)] if sibling else [])


def _split_start(srcs, lands, plan, *, sibling=False, name):
    ns, nl = len(srcs), len(lands)
    nsem = (3 + sibling) * len(plan(list(srcs), list(lands), 0, 0))

    def body(*refs):
        src_refs, land_refs = refs[:ns], refs[ns:ns + nl]
        send_sems, recv_sems = refs[ns + nl:ns + nl + nsem], refs[ns + nl + nsem:ns + nl + 2 * nsem]
        me, peers = _split_peers(sibling)
        n = 0
        for device, chip in peers:
            for src, dst, _ in plan(src_refs, land_refs, me, chip):
                pltpu.make_async_remote_copy(src_ref=src, dst_ref=dst, send_sem=send_sems[n], recv_sem=recv_sems[n],
                                             device_id=device, device_id_type=MESH).start()
                n += 1
        refs[-1][...] = jnp.zeros_like(refs[-1])

    arrays = list(srcs) + list(lands)
    out = pl.pallas_call(
        body, name=name, in_specs=[HBM] * (ns + nl),
        out_specs=[SEM] * (2 * nsem) + [HBM] * (ns + nl) + [pl.BlockSpec(memory_space=pltpu.VMEM)],
        out_shape=[pltpu.SemaphoreType.DMA(())] * (2 * nsem) + [pltpu.HBM(a.shape, a.dtype) for a in arrays]
        + [jax.ShapeDtypeStruct((8, LANES), F32)],
        input_output_aliases={i: 2 * nsem + i for i in range(ns + nl)},
        compiler_params=pltpu.CompilerParams(has_side_effects=DATAFLOW),
    )(*[pltpu.with_memory_space_constraint(a, pltpu.HBM) for a in arrays])
    sems, rest = out[:2 * nsem], out[2 * nsem:]
    return sems[:nsem], sems[nsem:], rest[:ns], rest[ns:ns + nl], rest[-1]


def _split_wait(handle, after, plan, *, sibling=False, name):
    send_sems, recv_sems, srcs, lands, _ = handle
    ns, nl, nsem = len(srcs), len(lands), len(send_sems)

    def body(*refs):
        src_refs, land_refs = refs[:ns], refs[ns:ns + nl]
        send_refs, recv_refs = refs[ns + nl:ns + nl + nsem], refs[ns + nl + nsem:ns + nl + 2 * nsem]
        me, peers = _split_peers(sibling)
        n = 0
        for device, chip in peers:
            for src, _, arrival in plan(src_refs, land_refs, me, chip):
                cp = pltpu.make_async_remote_copy(src_ref=src, dst_ref=arrival, send_sem=send_refs[n],
                                                  recv_sem=recv_refs[n], device_id=device, device_id_type=MESH)
                cp.wait_send()
                cp.wait_recv()
                n += 1

    out = pl.pallas_call(
        body, name=name, in_specs=[HBM] * (ns + nl) + [SEM] * (2 * nsem) + [ANY], out_specs=[HBM] * (ns + nl),
        out_shape=[pltpu.HBM(a.shape, a.dtype) for a in list(srcs) + list(lands)],
        input_output_aliases={i: i for i in range(ns + nl)},
        compiler_params=pltpu.CompilerParams(has_side_effects=DATAFLOW),
    )(*srcs, *lands, *send_sems, *recv_sems, after)
    return out[ns:]


def _exchange_sibling(arrays, *, name):
    n = len(arrays)

    def body(*refs):
        x, y, c = lax.axis_index("x"), lax.axis_index("y"), lax.axis_index("c")
        remote = [(0, i, refs[i], refs[n + i], refs[n + i], (x, y, 1 - c)) for i in range(n)]
        _run_copies([], remote, *refs[2 * n:])

    return _comm_call(body, arrays, [jax.ShapeDtypeStruct(a.shape, a.dtype) for a in arrays], n, 1, name=name)


def _sum_slots(r, *, token=None, name):
    shape = r.shape[1:]
    cols = shape[-1]
    rows = math.prod(shape) // cols
    tr = max(d for d in range(8, rows + 1, 8) if rows % d == 0 and d * cols * 16 <= (4 << 20))

    def body(r_ref, *refs):
        total = ((r_ref[0] + r_ref[1]) + r_ref[2]) + r_ref[3]
        refs[-1][...] = total if token is None else total + refs[0][0:1, 0:1]

    in_specs = [pl.BlockSpec((N_CHIPS, tr, cols), lambda i: (0, i, 0))]
    in_specs += [] if token is None else [_const((8, LANES))]
    return pl.pallas_call(
        body, grid=(rows // tr,), in_specs=in_specs,
        out_specs=_rows(tr, cols), out_shape=jax.ShapeDtypeStruct((rows, cols), F32), compiler_params=_cp(1),
        name=name,
    )(r.reshape(N_CHIPS, rows, cols), *([] if token is None else [token])).reshape(shape)


FIRST_WEIGHTS = ("ev_w_in", "ev_w_q_up", "ev_w_kv_up", "ev_w_out")
LATER_WEIGHTS = tuple(n for n in MATMUL_WEIGHTS if n not in FIRST_WEIGHTS)
LAST_GRADS = ("ev_w_in", "ev_w_q_up", "ev_w_kv_up")
EARLIER_GRADS = tuple(n for n in MATMUL_WEIGHTS if n not in LAST_GRADS)


def _my_chip():
    return 2 * lax.axis_index("x") + lax.axis_index("y")


def _gather_first(w):
    small = _pack([w[n] for n in SMALL_SHARDED], 8 * LANES)
    stacked = [n for n in FIRST_WEIGHTS if SHARD_AXIS[n] == w[n].ndim - 1 and w[n].shape[-1] % LANES]
    shards = [w[n].astype(BF16)[None] if n in stacked else w[n].astype(BF16) for n in FIRST_WEIGHTS]
    got = _gather_chips(shards + [small], [0 if n in stacked else SHARD_AXIS[n] for n in FIRST_WEIGHTS] + [0],
                        name="gather_first")
    full = {n: w[n] for n in REPLICATED}
    for n, g in zip(FIRST_WEIGHTS, got[:-1]):
        full[n] = jnp.concatenate([g[q] for q in range(N_CHIPS)], axis=SHARD_AXIS[n]) if n in stacked else g
    per_chip = [_unpack(got[-1][q * small.shape[0]:(q + 1) * small.shape[0]].reshape(-1),
                        [w[n].shape for n in SMALL_SHARDED]) for q in range(N_CHIPS)]
    for i, n in enumerate(SMALL_SHARDED):
        full[n] = jnp.concatenate([per_chip[q][i] for q in range(N_CHIPS)], axis=SHARD_AXIS[n])
    return full


def _gather_later_start(w, after):
    behind = (after.reshape(-1)[0] * 0).astype(BF16)
    shards = [w[n].astype(BF16) + (behind if n == "od_w_rgate" else 0) for n in LATER_WEIGHTS]
    axes = [SHARD_AXIS[n] for n in LATER_WEIGHTS]
    lands = [lax.empty(tuple(d * (N_CHIPS if a == ax else 1) for a, d in enumerate(s.shape)), s.dtype)
             for s, ax in zip(shards, axes)]
    plan = _gather_plan(axes)
    return _split_start(shards, lands, plan, sibling=True, name="gather_later_start"), plan


def _owner_major(g, axis):
    shape = g.shape
    size = shape[axis] // N_CHIPS
    g = jnp.moveaxis(g.reshape(shape[:axis] + (N_CHIPS, size) + shape[axis + 1:]), axis, 0)
    return g.reshape(N_CHIPS, -1, shape[-1] if axis < len(shape) - 1 else size)


def _exchange_start(items, *, cross, name):
    me = _my_chip()
    srcs, lands, where = [], [], []
    for n, layers in enumerate(items):
        land = lax.empty((N_CHIPS, len(layers)) + layers[0].shape[1:], layers[0].dtype)
        for l, a in enumerate(layers):
            if not cross:
                own = lax.dynamic_index_in_dim(a, me, 0, keepdims=True)[:, None]
                land = lax.dynamic_update_slice(land, own, (me, l) + (0,) * (a.ndim - 1))
            srcs.append(a)
            where.append((n, l))
        lands.append(land)
    plan = _exchange_plan(where)
    return _split_start(srcs, lands, plan, sibling=cross, name=name), plan


def _earlier_items(grads, full_shapes):
    small = [_pack([jnp.split(grads[n].reshape(full_shapes[n]), N_CHIPS, axis=SHARD_AXIS[n])[q]
                    for n in SMALL_SHARDED], 8 * LANES) for q in range(N_CHIPS)]
    return [grads[n] for n in EARLIER_GRADS] + [[jnp.stack(small)]]


def _last_items(big, grads, full_shapes, loss):
    repl = _pack([grads[n].reshape(full_shapes[n]) for n in REPLICATED] + [loss.reshape(1)], 8 * LANES)
    return [big[n] for n in LAST_GRADS] + [[jnp.stack([repl] * N_CHIPS)]]


def kernel(
        x, mem, positions, ev_norm, ev_w_in, ev_pool_w, ev_pool_scale, ev_q_norm, ev_w_q_up, ev_kv_norm,
        ev_w_kv_up, ev_w_out, od_norm, od_w_in, od_conv_w, od_conv_b, od_w_rgate, od_b_rgate, od_w_igate,
        od_b_igate, od_lambda, od_w_out, xa_norm_x, xa_norm_mem, xa_w_q, xa_w_kv, xa_w_o, ffn_norm,
        ffn_w_gate_up, ffn_w_down, final_norm, loss_target, m_ev_norm, m_ev_w_in, m_ev_pool_w, m_ev_pool_scale,
        m_ev_q_norm, m_ev_w_q_up, m_ev_kv_norm, m_ev_w_kv_up, m_ev_w_out, m_od_norm, m_od_w_in, m_od_conv_w,
        m_od_conv_b, m_od_w_rgate, m_od_b_rgate, m_od_w_igate, m_od_b_igate, m_od_lambda, m_od_w_out,
        m_xa_norm_x, m_xa_norm_mem, m_xa_w_q, m_xa_w_kv, m_xa_w_o, m_ffn_norm, m_ffn_w_gate_up, m_ffn_w_down,
        m_final_norm, v_ev_norm, v_ev_w_in, v_ev_pool_w, v_ev_pool_scale, v_ev_q_norm, v_ev_w_q_up,
        v_ev_kv_norm, v_ev_w_kv_up, v_ev_w_out, v_od_norm, v_od_w_in, v_od_conv_w, v_od_conv_b, v_od_w_rgate,
        v_od_b_rgate, v_od_w_igate, v_od_b_igate, v_od_lambda, v_od_w_out, v_xa_norm_x, v_xa_norm_mem, v_xa_w_q,
        v_xa_w_kv, v_xa_w_o, v_ffn_norm, v_ffn_w_gate_up, v_ffn_w_down, v_final_norm):
    given = dict(locals())
    w = {n: given[n] for n in WEIGHTS}
    full_shapes = {n: tuple(d * (N_CHIPS if a == SHARD_AXIS.get(n) else 1) for a, d in enumerate(w[n].shape))
                   for n in WEIGHTS}
    full = _gather_first(w)
    later, later_plan = _gather_later_start(w, full["ev_w_out"])
    full["ev_norm"] = full["ev_norm"] + later[4][0:1, 0:1]
    exchange = {}

    def later_weights(after):
        return dict(zip(LATER_WEIGHTS, _split_wait(later, after, later_plan, sibling=True, name="gather_later_wait")))

    def exchange_earlier(grads):
        exchange["handle"], exchange["plan"] = _exchange_start(_earlier_items(grads, full_shapes), cross=True,
                                                               name="exchange_earlier_start")
        return exchange["handle"][4]

    loss, grad_x, big, grads = _local_step(x[0], mem[0], positions[0], loss_target[0], full, later_weights,
                                           exchange_earlier)
    earlier = EARLIER_GRADS + ("small",)
    got = dict(zip(earlier, _split_wait(exchange["handle"], grad_x, exchange["plan"], sibling=True,
                                        name="exchange_earlier_wait")))
    last, last_plan = _exchange_start(_last_items(big, grads, full_shapes, loss), cross=False,
                                      name="exchange_last_start")
    out = {}

    def finish(names, landed, token, tag):
        mine = [_sum_slots(landed[n], token=token if i == 0 else None, name=f"sum_chips_{n}")
                for i, n in enumerate(names)]
        other = _exchange_sibling(mine, name=f"exchange_sibling_{tag}")
        total = None
        for n, a, b in zip(names, mine, other):
            if n in MATMUL_WEIGHTS:
                out[n] = _adamw(w[n], a.reshape(w[n].shape), b.reshape(w[n].shape), given["m_" + n], given["v_" + n],
                                name=f"adamw_{n}")
                continue
            group = SMALL_SHARDED if n == "small" else REPLICATED
            spare = [jnp.zeros((1,), F32)] if group is REPLICATED else []
            packed = [_pack([given[pre + k] for k in group] + spare, 8 * LANES) for pre in ("", "m_", "v_")]
            res = _adamw(packed[0], a.reshape(packed[0].shape), b.reshape(packed[0].shape), packed[1], packed[2],
                         name=f"adamw_{n}")
            shapes = [w[k].shape for k in group] + [(1,)] * len(spare)
            for j, arrs in enumerate(zip(*[_unpack(r.reshape(-1), shapes) for r in res])):
                if j < len(group):
                    out[group[j]] = list(arrs)
                else:
                    total = arrs[0][0]
        return total

    finish(earlier, got, last[4], "earlier")
    names = LAST_GRADS + ("replicated",)
    got = dict(zip(names, _split_wait(last, out[EARLIER_GRADS[-1]][1], last_plan, name="exchange_last_wait")))
    loss = finish(names, got, None, "last")
    return (loss, grad_x[None], *[out[n][k] for k in range(4) for n in WEIGHTS])
```

```python
import math

import jax
import jax.numpy as jnp
from jax import lax
from jax.experimental import pallas as pl
from jax.experimental.pallas import tpu as pltpu

F32 = jnp.float32
BF16 = jnp.bfloat16

D_MODEL = 1024
POOL_DIM = 512
POOL_WINDOWS = (2, 4, 8, 16)
POOL_GROUP = 128
MLA_HEADS = 8
QK_NOPE = 64
QK_ROPE = 32
QK_DIM = QK_NOPE + QK_ROPE
V_HEAD = 64
HEAD_PAD = 128
Q_RANK = 256
KV_RANK = 128
ROPE_BASE = 10000.0
LRU_HEADS = 4
LRU_HEAD_DIM = 256
CONV_WIDTH = 4
LRU_C = 8.0
MEM_HEADS = 4
MEM_HEAD_DIM = 256
D_FF = 2816
RMS_EPS = 1e-6
NEG_INF = -1e30

ADAM_LR = 0.001
ADAM_B1 = 0.9
ADAM_B2 = 0.999
ADAM_EPS = 1e-08
ADAM_WD = 0.01
ADAM_STEP = 10

N_CHIPS = 4
LANES = 128
VMEM_LIMIT = 56 * 1024 * 1024
MESH = pl.DeviceIdType.MESH
ANY = pl.BlockSpec(memory_space=pl.ANY)
MIX_DIM = POOL_DIM + MLA_HEADS * HEAD_PAD

NN = (((1,), (0,)), ((), ()))
NT = (((1,), (1,)), ((), ()))
TN = (((0,), (0,)), ((), ()))


def _cp(n):
    return pltpu.CompilerParams(dimension_semantics=("arbitrary",) * n, vmem_limit_bytes=VMEM_LIMIT)


def _dot(a, b, dims=NN):
    return lax.dot_general(a, b, dims, preferred_element_type=F32)


def _row_tile(S):
    return 1024 if S % 1024 == 0 else min(S, 512)


def _rows(ts, w, cb=0):
    return pl.BlockSpec((ts, w), lambda i: (i, cb))


def _const(shape):
    return pl.BlockSpec(shape, lambda i: (0,) * len(shape))


MM_VMEM_BUDGET = 40 * 1024 * 1024


def _mm(a, bs, epi, outs, *, tn, nj, nt=False, also=None, extras=(), rows=(), sums=(), a_cb=0, k=None, tm=None,
        name):
    M = a.shape[0]
    k = k or a.shape[1]
    nb, ne, nr, no = len(bs), len(extras), len(rows), len(outs)
    lhs = [(a, k, a_cb, b) for b in bs[:1]] + ([(also[0], also[0].shape[1], 0, also[1])] if also else [])
    if tm is None:
        per_row = 2 * (sum(kk * x.dtype.itemsize for x, kk, _, _ in lhs)
                       + sum(e.dtype.itemsize for e, _ in extras) * tn
                       + sum(jnp.dtype(dt).itemsize for _, dt, _ in outs) * tn) + nb * tn * 4
        weights = (1 if nj == 1 else 2) * (sum(b.dtype.itemsize for b, _, _ in bs) * k
                                           + (also[1][0].dtype.itemsize * lhs[-1][1] if also else 0)) * tn
        tm = 1024 if M % 1024 == 0 and 1024 * per_row + weights <= MM_VMEM_BUDGET else min(M, 512)
    dims = NT if nt else NN
    assert not sums or nj == 1
    na = 2 if also else 0

    def body(*refs):
        av = refs[0][...].astype(BF16)
        accs = [_dot(av, r[...].astype(BF16), dims) for r in refs[1:1 + nb]]
        if also:
            accs[0] = accs[0] + _dot(refs[1 + nb][...].astype(BF16), refs[2 + nb][...].astype(BF16), dims)
        refs = refs[:1 + nb] + refs[1 + nb + na:]
        vals = epi(accs, [r[...] for r in refs[1 + nb:1 + nb + ne + nr]])
        outs_refs = refs[1 + nb + ne + nr:]
        for o, v in zip(outs_refs[:no], vals[:no]):
            o[...] = v.astype(o.dtype)
        if sums:
            @pl.when(pl.program_id(1) == 0)
            def _():
                for o in outs_refs[no:]:
                    o[...] = jnp.zeros_like(o)

            for o, v in zip(outs_refs[no:], vals[no:]):
                o[...] += v

    in_specs = [pl.BlockSpec((tm, k), lambda j, i: (i, a_cb))]
    weights = [(k, rb, cb) for (_, rb, cb) in bs]
    if also:
        in_specs_also = pl.BlockSpec((tm, lhs[-1][1]), lambda j, i: (i, 0))
        weights.append((lhs[-1][1], also[1][1], also[1][2]))
    for n, (kk, rb, cb) in enumerate(weights):
        if also and n == nb:
            in_specs.append(in_specs_also)
        mode = dict(pipeline_mode=pl.Buffered(1)) if nj == 1 else {}
        if nt:
            in_specs.append(pl.BlockSpec((tn, kk), lambda j, i, rb=rb, cb=cb: (rb + j, cb), **mode))
        else:
            in_specs.append(pl.BlockSpec((kk, tn), lambda j, i, rb=rb, cb=cb: (rb, cb + j), **mode))
    for (_, cb) in extras:
        in_specs.append(pl.BlockSpec((tm, tn), lambda j, i, cb=cb: (i, cb + j)))
    in_specs += [pl.BlockSpec((1, tn), lambda j, i: (0, 0))] * nr
    out_specs = [pl.BlockSpec((tm, tn), lambda j, i, cb=cb: (i, cb + j)) for (_, _, cb) in outs]
    out_specs += [pl.BlockSpec((1, w), lambda j, i: (0, 0)) for w in sums]
    res = pl.pallas_call(
        body, grid=(nj, M // tm), in_specs=in_specs, out_specs=out_specs,
        out_shape=[jax.ShapeDtypeStruct((M, n), dt) for (n, dt, _) in outs]
        + [jax.ShapeDtypeStruct((1, w), F32) for w in sums],
        compiler_params=_cp(2), name=name,
    )(a, *[b for (b, _, _) in bs], *([also[0], also[1][0]] if also else []), *[e for (e, _) in extras], *rows)
    return res


def _first(accs, extras):
    return [accs[0]]


def _add_res(accs, extras):
    return [accs[0] + extras[0].astype(F32)]


def _norm_bwd_epilogue(partials):
    def epi(accs, vals):
        dh = accs[0]
        for part in vals[:partials]:
            dh = dh + part.astype(F32)
        x, res, g = vals[partials:partials + 3]
        r = lax.rsqrt(jnp.mean(x * x, axis=-1, keepdims=True) + RMS_EPS)
        n = x * r
        dn = dh * g
        return [r * (dn - n * jnp.mean(dn * n, axis=-1, keepdims=True)) + res, jnp.sum(dh * n, axis=0, keepdims=True)]

    return epi


TN_VMEM_BUDGET = 36 * 1024 * 1024


def _contraction_rows(S, row_bytes, out_elems):
    ts = min(S, 2048)
    while ts > 512 and 2 * (ts * row_bytes + out_elems * 4) > TN_VMEM_BUDGET:
        ts //= 2
    return ts


def _mm_tn(a, b, *, ka=None, a_cb=0, nb=None, b_cb=0, tk=None, tn=None, ts=None, name):
    S = a.shape[0]
    ka = ka or a.shape[1]
    nb = nb or b.shape[1]
    tk = tk or ka
    tn = tn or nb
    ts = ts or _contraction_rows(S, tk * a.dtype.itemsize + tn * b.dtype.itemsize, tk * tn)
    a0, b0 = a_cb * (ka // tk), b_cb * (nb // tn)

    def body(a_ref, b_ref, o_ref):
        @pl.when(pl.program_id(2) == 0)
        def _():
            o_ref[...] = jnp.zeros_like(o_ref)

        o_ref[...] += _dot(a_ref[...].astype(BF16), b_ref[...].astype(BF16), TN)

    return pl.pallas_call(
        body, grid=(ka // tk, nb // tn, S // ts),
        in_specs=[pl.BlockSpec((ts, tk), lambda p, q, s: (s, a0 + p)),
                  pl.BlockSpec((ts, tn), lambda p, q, s: (s, b0 + q))],
        out_specs=pl.BlockSpec((tk, tn), lambda p, q, s: (p, q)),
        out_shape=jax.ShapeDtypeStruct((ka, nb), F32), compiler_params=_cp(3), name=name,
    )(a, b)


def _mm_tn_owners(a, bs, *, name):
    S, ka = a.shape
    nb = sum(b.shape[1] for b in bs)
    tn = nb // N_CHIPS
    ts = _contraction_rows(S, ka * a.dtype.itemsize + len(bs) * tn * bs[0].dtype.itemsize, ka * tn)
    per = N_CHIPS // len(bs)

    def body(a_ref, *refs):
        o_ref = refs[-1]
        q = pl.program_id(0)

        @pl.when(pl.program_id(1) == 0)
        def _():
            o_ref[...] = jnp.zeros_like(o_ref)

        av = a_ref[...].astype(BF16)
        for n, b_ref in enumerate(refs[:-1]):
            @pl.when(q // per == n)
            def _():
                o_ref[0] += _dot(av, b_ref[...].astype(BF16), TN)

    in_specs = [pl.BlockSpec((ts, ka), lambda q, s: (s, 0))]
    for n in range(len(bs)):
        in_specs.append(pl.BlockSpec((ts, tn), lambda q, s, n=n: (jnp.where(q // per == n, s, 0),
                                                                  jnp.clip(q - n * per, 0, per - 1))))
    return pl.pallas_call(
        body, grid=(N_CHIPS, S // ts), in_specs=in_specs,
        out_specs=pl.BlockSpec((1, ka, tn), lambda q, s: (q, 0, 0)),
        out_shape=jax.ShapeDtypeStruct((N_CHIPS, ka, tn), F32), compiler_params=_cp(2), name=name,
    )(a, *bs)


def _mm_tn_grouped(a, b, groups, w, *, name):
    S = a.shape[0]
    ts = _contraction_rows(S, w * (a.dtype.itemsize + b.dtype.itemsize), w * w)

    def body(a_ref, b_ref, o_ref):
        @pl.when(pl.program_id(1) == 0)
        def _():
            o_ref[...] = jnp.zeros_like(o_ref)

        o_ref[0] += _dot(a_ref[...].astype(BF16), b_ref[...].astype(BF16), TN)

    return pl.pallas_call(
        body, grid=(groups, S // ts),
        in_specs=[pl.BlockSpec((ts, w), lambda g, s: (s, g)), pl.BlockSpec((ts, w), lambda g, s: (s, g))],
        out_specs=pl.BlockSpec((1, w, w), lambda g, s: (g, 0, 0)),
        out_shape=jax.ShapeDtypeStruct((groups, w, w), F32), compiler_params=_cp(2), name=name,
    )(a, b)


def _rms(x, g, *, name):
    S, w = x.shape
    ts = _row_tile(S)

    def body(x_ref, g_ref, o_ref):
        xv = x_ref[...]
        r = lax.rsqrt(jnp.mean(xv * xv, axis=-1, keepdims=True) + RMS_EPS)
        o_ref[...] = (xv * r * g_ref[...]).astype(o_ref.dtype)

    return pl.pallas_call(
        body, grid=(S // ts,), in_specs=[_rows(ts, w), _const((1, w))], out_specs=_rows(ts, w),
        out_shape=jax.ShapeDtypeStruct((S, w), BF16), compiler_params=_cp(1), name=name,
    )(x, g.reshape(1, w))


def _norm_gain_grad(x, dy, *, name):
    S, w = x.shape
    ts = _row_tile(S)

    def body(x_ref, dy_ref, dg_ref):
        @pl.when(pl.program_id(0) == 0)
        def _():
            dg_ref[...] = jnp.zeros_like(dg_ref)

        xv = x_ref[...]
        r = lax.rsqrt(jnp.mean(xv * xv, axis=-1, keepdims=True) + RMS_EPS)
        dg_ref[...] += jnp.sum(dy_ref[...] * (xv * r), axis=0, keepdims=True)

    return pl.pallas_call(
        body, grid=(S // ts,), in_specs=[_rows(ts, w), _rows(ts, w)], out_specs=_const((1, w)),
        out_shape=jax.ShapeDtypeStruct((1, w), F32), compiler_params=_cp(1), name=name,
    )(x, dy)


HALO = 16


def _pool_counts(i, ts, rows, first_row):
    t = i * ts + first_row + lax.broadcasted_iota(jnp.int32, (rows, 1), 0)
    return [jnp.minimum(t + 1, w).astype(F32) for w in POOL_WINDOWS]


def _even_front(x, g, w_in, pool_w, pool_scale, g_q, w_q, g_kv, w_kv, ctab, stab, *, name):
    S = x.shape[0]
    ts = min(S, 512)

    def body(x_ref, g_ref, win_ref, pw_ref, sc_ref, gq_ref, wq_ref, gkv_ref, wkv_ref, c_ref, s_ref,
             h_ref, z_ref, y_ref, p_ref, cqn_ref, ckvn_ref, q_ref, k_ref, v_ref, tail):
        i = pl.program_id(0)

        def normed(t, gain):
            r = lax.rsqrt(jnp.mean(t * t, axis=-1, keepdims=True) + RMS_EPS)
            return (t * r * gain).astype(BF16)

        h = normed(x_ref[...], g_ref[...])
        h_ref[...] = h
        z = _dot(h, win_ref[...])
        z_ref[...] = z
        u = z[:, :POOL_DIM]
        xe = jnp.concatenate([jnp.where(i > 0, tail[...], 0.0), u], axis=0)
        tail[...] = u[ts - HALO:]
        sums = []
        s = xe
        for sh in (1, 2, 4, 8):
            s = s + pltpu.roll(s, sh, 0)
            sums.append(s)
        cnts = _pool_counts(i, ts, ts, 0)
        for grp in range(4):
            lo, hi = grp * POOL_GROUP, (grp + 1) * POOL_GROUP
            pooled = (sums[grp][HALO:, lo:hi] / cnts[grp] - u[:, lo:hi]).astype(BF16)
            p_ref[:, lo:hi] = pooled
            y_ref[:, lo:hi] = (_dot(pooled, pw_ref[grp]) * sc_ref[:, lo:hi]).astype(y_ref.dtype)
        cqn = normed(z[:, POOL_DIM:POOL_DIM + Q_RANK], gq_ref[...])
        ckvn = normed(z[:, POOL_DIM + Q_RANK:POOL_DIM + Q_RANK + KV_RANK], gkv_ref[...])
        cqn_ref[...] = cqn
        ckvn_ref[...] = ckvn
        q = _dot(cqn, wq_ref[...])
        kv = _dot(ckvn, wkv_ref[...])
        c, sn = c_ref[...], s_ref[...]
        kr = z[:, D_MODEL - HEAD_PAD:]
        kr_rot = kr * c + _rope_partner(kr) * sn
        lane = lax.broadcasted_iota(jnp.int32, (ts, HEAD_PAD), 1)
        for hd in range(MLA_HEADS):
            lo, hi = hd * HEAD_PAD, (hd + 1) * HEAD_PAD
            qh = q[:, lo:hi]
            q_ref[:, lo:hi] = (qh * c + _rope_partner(qh) * sn).astype(q_ref.dtype)
            k_ref[:, lo:hi] = (kv[:, lo:hi] + kr_rot).astype(k_ref.dtype)
            v_ref[:, lo:hi] = jnp.where(lane == V_HEAD, 1.0, kv[:, D_MODEL + lo:D_MODEL + hi]).astype(v_ref.dtype)

    wide = jax.ShapeDtypeStruct((S, D_MODEL), BF16)
    return pl.pallas_call(
        body, grid=(S // ts,),
        in_specs=[_rows(ts, D_MODEL), _const((1, D_MODEL)), _const((D_MODEL, D_MODEL)),
                  _const((4, POOL_GROUP, POOL_GROUP)), _const((1, POOL_DIM)), _const((1, Q_RANK)),
                  _const((Q_RANK, D_MODEL)), _const((1, KV_RANK)), _const((KV_RANK, 2 * D_MODEL)),
                  _rows(ts, HEAD_PAD), _rows(ts, HEAD_PAD)],
        out_specs=[_rows(ts, D_MODEL), _rows(ts, D_MODEL), _rows(ts, POOL_DIM), _rows(ts, POOL_DIM),
                   _rows(ts, Q_RANK), _rows(ts, KV_RANK), _rows(ts, D_MODEL), _rows(ts, D_MODEL), _rows(ts, D_MODEL)],
        out_shape=[wide, jax.ShapeDtypeStruct((S, D_MODEL), F32), jax.ShapeDtypeStruct((S, MIX_DIM), BF16),
                   jax.ShapeDtypeStruct((S, POOL_DIM), BF16), jax.ShapeDtypeStruct((S, Q_RANK), BF16),
                   jax.ShapeDtypeStruct((S, KV_RANK), BF16), wide, wide, wide],
        scratch_shapes=[pltpu.VMEM((HALO, POOL_DIM), F32)], compiler_params=_cp(1), name=name,
    )(x, g.reshape(1, D_MODEL), w_in, pool_w, pool_scale, g_q.reshape(1, Q_RANK), w_q, g_kv.reshape(1, KV_RANK), w_kv,
      ctab, stab)


def _norm_bwd_values(xv, gain, dy):
    r = lax.rsqrt(jnp.mean(xv * xv, axis=-1, keepdims=True) + RMS_EPS)
    n = xv * r
    dn = dy * gain
    return r * (dn - n * jnp.mean(dn * n, axis=-1, keepdims=True)), jnp.sum(dy * n, axis=0, keepdims=True)


def _even_back(dq_rot, dk_cat, dv, dmix, pooled, z, x, dxo, ctab, stab, w_q, w_kv, w_in, pool_w, pool_scale, g_q, g_kv,
               g_x, *, name):
    S = x.shape[0]
    ts = min(S, 512)
    nh = ts // HALO
    last = S // HALO - 1
    n = ts + HALO

    def body(dq_ref, dk_ref, dv_ref, dy_ref, dyh_ref, p_ref, z_ref, x_ref, dxo_ref, c_ref, s_ref, wq_ref, wkv_ref,
             win_ref, pw_ref, sc_ref, gq_ref, gkv_ref, gx_ref,
             dx_ref, dqp_ref, dz_ref, dyp_ref, dgq_ref, dgkv_ref, dsc_ref, dgx_ref):
        i = pl.program_id(0)

        @pl.when(i == 0)
        def _():
            for ref in (dgq_ref, dgkv_ref, dsc_ref, dgx_ref):
                ref[...] = jnp.zeros_like(ref)

        c, sn = c_ref[...], s_ref[...]
        z = z_ref[...]
        dk = dk_ref[...]
        for hd in range(MLA_HEADS):
            lo, hi = hd * HEAD_PAD, (hd + 1) * HEAD_PAD
            g = dq_ref[:, lo:hi]
            dqp_ref[:, lo:hi] = (g * c + _rope_partner(g * sn)).astype(dqp_ref.dtype)
            heads_sum = dk[:, lo:hi] if hd == 0 else heads_sum + dk[:, lo:hi]
        lane = lax.broadcasted_iota(jnp.int32, heads_sum.shape, 1)
        dkr = jnp.where((lane >= QK_NOPE) & (lane < QK_DIM), heads_sum * c + _rope_partner(heads_sum * sn), 0.0)
        dcqn = _dot(dqp_ref[...], wq_ref[...], NT)
        dckvn = _dot(dk.astype(BF16), wkv_ref[:, :D_MODEL], NT) + _dot(dv_ref[...].astype(BF16),
                                                                       wkv_ref[:, D_MODEL:], NT)
        dcq, dgq = _norm_bwd_values(z[:, POOL_DIM:POOL_DIM + Q_RANK], gq_ref[...], dcqn)
        dckv, dgkv = _norm_bwd_values(z[:, POOL_DIM + Q_RANK:POOL_DIM + Q_RANK + KV_RANK], gkv_ref[...], dckvn)
        dgq_ref[...] += dgq
        dgkv_ref[...] += dgkv
        dyv = dy_ref[...].astype(F32)
        dyh = jnp.where(i < pl.num_programs(0) - 1, dyh_ref[...].astype(F32), 0.0)
        dypre = (jnp.concatenate([dyv, dyh], axis=0) * sc_ref[...]).astype(BF16)
        dyp_ref[...] = dypre[:ts]
        cnts = _pool_counts(i, ts, n, 0)
        dsc = []
        for grp in range(4):
            lo, hi = grp * POOL_GROUP, (grp + 1) * POOL_GROUP
            dsc.append(jnp.sum(dyv[:, lo:hi] * _dot(p_ref[:, lo:hi], pw_ref[grp]), axis=0, keepdims=True))
            dpool = _dot(dypre[:, lo:hi], pw_ref[grp], NT)
            s = dpool / cnts[grp]
            for sh in (1, 2, 4, 8)[:grp + 1]:
                s = s + pltpu.roll(s, n - sh, 0)
            dz_ref[:, lo:hi] = (s[:ts] - dpool[:ts]).astype(dz_ref.dtype)
        dsc_ref[...] += jnp.concatenate(dsc, axis=1)
        dz_ref[:, POOL_DIM:POOL_DIM + Q_RANK] = dcq.astype(dz_ref.dtype)
        dz_ref[:, POOL_DIM + Q_RANK:POOL_DIM + Q_RANK + KV_RANK] = dckv.astype(dz_ref.dtype)
        dz_ref[:, D_MODEL - HEAD_PAD:] = dkr.astype(dz_ref.dtype)
        dx, dgx = _norm_bwd_values(x_ref[...], gx_ref[...], _dot(dz_ref[...], win_ref[...], NT))
        dx_ref[...] = dx + dxo_ref[...]
        dgx_ref[...] += dgx

    wide, pool = _rows(ts, D_MODEL), _rows(ts, POOL_DIM)
    f32 = lambda w: jax.ShapeDtypeStruct((1, w), F32)
    return pl.pallas_call(
        body, grid=(S // ts,),
        in_specs=[wide, wide, wide, pool,
                  pl.BlockSpec((HALO, POOL_DIM), lambda i: (jnp.minimum((i + 1) * nh, last), 0)), pool, wide, wide, wide,
                  _rows(ts, HEAD_PAD), _rows(ts, HEAD_PAD), _const((Q_RANK, D_MODEL)), _const((KV_RANK, 2 * D_MODEL)),
                  _const((D_MODEL, D_MODEL)), _const((4, POOL_GROUP, POOL_GROUP)), _const((1, POOL_DIM)),
                  _const((1, Q_RANK)), _const((1, KV_RANK)), _const((1, D_MODEL))],
        out_specs=[wide, wide, wide, pool, _const((1, Q_RANK)), _const((1, KV_RANK)), _const((1, POOL_DIM)),
                   _const((1, D_MODEL))],
        out_shape=[jax.ShapeDtypeStruct((S, D_MODEL), F32), jax.ShapeDtypeStruct((S, D_MODEL), BF16),
                   jax.ShapeDtypeStruct((S, D_MODEL), BF16), jax.ShapeDtypeStruct((S, POOL_DIM), BF16),
                   f32(Q_RANK), f32(KV_RANK), f32(POOL_DIM), f32(D_MODEL)],
        compiler_params=_cp(1), name=name,
    )(dq_rot, dk_cat, dv, dmix, dmix, pooled, z, x, dxo, ctab, stab, w_q, w_kv, w_in, pool_w, pool_scale,
      g_q.reshape(1, Q_RANK), g_kv.reshape(1, KV_RANK), g_x.reshape(1, D_MODEL))


def _rope_partner(t):
    lane = lax.broadcasted_iota(jnp.int32, t.shape, 1)
    swapped = jnp.where(lane < QK_NOPE + QK_ROPE // 2, pltpu.roll(t, HEAD_PAD - QK_ROPE // 2, 1),
                        pltpu.roll(t, QK_ROPE // 2, 1))
    return jnp.where((lane >= QK_NOPE) & (lane < QK_DIM), swapped, 0.0)


ATT_SCALE = QK_DIM ** -0.5
LOG2E = math.log2(math.e)


HEADS_PER_STEP = 2
ATT_COL0 = POOL_DIM // HEAD_PAD


FWD_TILE = 1024


def _stat_rows(col):
    return jnp.broadcast_to(col, (col.shape[0], LANES)).T[0:8]


def _retile_rows(rows, tq):
    heads, n8, t = rows.shape
    if t == tq:
        return rows
    flat = rows.reshape(heads, n8 // 8, 8, t)[:, :, 0].reshape(heads, -1, 1, tq)
    return jnp.broadcast_to(flat, (heads, flat.shape[1], 8, tq)).reshape(heads, -1, tq)


def _flash_fwd(q, k, v, mix, *, name):
    S = q.shape[0]
    tq = FWD_TILE if S % FWD_TILE == 0 else min(S, 512)
    nq = S // tq
    hs = HEADS_PER_STEP
    wide = hs * HEAD_PAD

    def body(q_ref, k_ref, v_ref, mix_ref, o_ref, lse_ref):
        qi = pl.program_id(1)
        qv = [q_ref[:, a * HEAD_PAD:(a + 1) * HEAD_PAD] for a in range(hs)]

        def update(m, acc, s, v):
            m_new = jnp.maximum(m, jnp.max(s, axis=-1, keepdims=True))
            p = jnp.exp2((s - m_new) * (ATT_SCALE * LOG2E))
            alpha = jnp.exp2((m - m_new) * (ATT_SCALE * LOG2E))
            return m_new, alpha * acc + _dot(p.astype(BF16), v)

        def step(j, carry, masked):
            off = pl.multiple_of(j * tq, tq)
            out = []
            for a in range(hs):
                head = slice(a * HEAD_PAD, (a + 1) * HEAD_PAD)
                s = _dot(qv[a], k_ref[pl.ds(off, tq), head], NT)
                if masked:
                    row = lax.broadcasted_iota(jnp.int32, (tq, tq), 0)
                    col = lax.broadcasted_iota(jnp.int32, (tq, tq), 1)
                    s = jnp.where(col <= row, s, NEG_INF)
                out.append(update(*carry[a], s, v_ref[pl.ds(off, tq), head]))
            return tuple(out)

        one = (jnp.full((tq, 1), NEG_INF, F32), jnp.zeros((tq, HEAD_PAD), F32))
        carry = step(qi, lax.fori_loop(0, qi, lambda j, c: step(j, c, False), (one,) * hs), True)
        for a in range(hs):
            m, acc = carry[a]
            l = acc[:, V_HEAD:V_HEAD + 1]
            o_ref[:, a * HEAD_PAD:(a + 1) * HEAD_PAD] = (acc / l).astype(o_ref.dtype)
            lse_ref[a] = _stat_rows(m * ATT_SCALE + jnp.log(l))

    blk = pl.BlockSpec((tq, wide), lambda h, i: (i, h))
    full = pl.BlockSpec((S, wide), lambda h, i: (0, h))
    return pl.pallas_call(
        body, grid=(MLA_HEADS // hs, nq), in_specs=[blk, full, full, ANY],
        out_specs=[pl.BlockSpec((tq, wide), lambda h, i: (i, ATT_COL0 // hs + h)),
                   pl.BlockSpec((hs, 8, tq), lambda h, i: (h, i, 0))],
        out_shape=[jax.ShapeDtypeStruct(mix.shape, mix.dtype), jax.ShapeDtypeStruct((MLA_HEADS, nq * 8, tq), F32)],
        input_output_aliases={3: 0}, compiler_params=_cp(2), name=name,
    )(q, k, v, mix)


BWD_TILE = 1024
BWD_HEADS_PER_STEP = 1


def _bwd_tile(S):
    return BWD_TILE if S % BWD_TILE == 0 else min(S, 512)


def _attn_delta(dmix, mix, *, name):
    S = mix.shape[0]
    ts = _bwd_tile(S)
    half = MLA_HEADS // 2
    halves = [_rows(ts, half * HEAD_PAD, 1), _rows(ts, half * HEAD_PAD, 2)]

    def body(do0_ref, do1_ref, o0_ref, o1_ref, d_ref):
        for n, (do_ref, o_ref) in enumerate(((do0_ref, o0_ref), (do1_ref, o1_ref))):
            prod = do_ref[...].astype(F32) * o_ref[...].astype(F32)
            for a in range(half):
                d_ref[n * half + a] = _stat_rows(
                    jnp.sum(prod[:, a * HEAD_PAD:(a + 1) * HEAD_PAD], axis=-1, keepdims=True))

    return pl.pallas_call(
        body, grid=(S // ts,), in_specs=halves + halves,
        out_specs=pl.BlockSpec((MLA_HEADS, 8, ts), lambda i: (0, i, 0)),
        out_shape=jax.ShapeDtypeStruct((MLA_HEADS, (S // ts) * 8, ts), F32), compiler_params=_cp(1), name=name,
    )(dmix, dmix, mix, mix)


def _flash_bwd(q, k, v, dmix, lse_rows, delta_rows, *, name):
    S = q.shape[0]
    tq = _bwd_tile(S)
    nq = S // tq
    hs = BWD_HEADS_PER_STEP
    wide = hs * HEAD_PAD

    def body(q_hbm, do_hbm, lse_ref, dl_ref, k_ref, v_ref, dq_hbm, dk_ref, dv_ref, q_all, do_all, dq_all):
        g, j = pl.program_id(0), pl.program_id(1)
        cols = pl.multiple_of(g * wide, wide)

        @pl.when(j == 0)
        def _():
            pltpu.sync_copy(q_hbm.at[:, pl.ds(cols, wide)], q_all)
            pltpu.sync_copy(do_hbm.at[:, pl.ds(POOL_DIM + cols, wide)], do_all)
            dq_all[...] = jnp.zeros_like(dq_all)

        heads = [slice(a * HEAD_PAD, (a + 1) * HEAD_PAD) for a in range(hs)]
        kv = [k_ref[:, a] for a in heads]
        vv = [v_ref[:, a] for a in heads]

        def block(a, keys, rows, lse2, dl, first_query):
            qv, dov = q_all[rows, heads[a]], do_all[rows, heads[a]]
            st = _dot(kv[a][:keys], qv, NT)
            if first_query is not None:
                krow = lax.broadcasted_iota(jnp.int32, st.shape, 0)
                qcol = lax.broadcasted_iota(jnp.int32, st.shape, 1) + first_query
                st = jnp.where(krow <= qcol, st, NEG_INF)
            pt = jnp.exp2(st * (ATT_SCALE * LOG2E) - lse2)
            dst = (pt * (_dot(vv[a][:keys], dov, NT) - dl)).astype(BF16)
            dq_all[rows, heads[a]] += _dot(dst, kv[a][:keys], TN)
            return _dot(dst, qv), _dot(pt.astype(BF16), dov)

        def stats(a, i):
            off8 = pl.multiple_of(i * 8, 8)
            return lse_ref[a, pl.ds(off8, 8), :][0:1] * LOG2E, dl_ref[a, pl.ds(off8, 8), :][0:1]

        def step(i, carry):
            rows = pl.ds(pl.multiple_of(i * tq, tq), tq)
            out = []
            for a in range(hs):
                dk, dv = block(a, tq, rows, *stats(a, i), None)
                out.append((carry[a][0] + dk, carry[a][1] + dv))
            return tuple(out)

        def diagonal():
            half = tq // 2
            out = []
            for a in range(hs):
                lse2, dl = stats(a, j)
                off = pl.multiple_of(j * tq, tq)
                dk0, dv0 = block(a, half, pl.ds(off, half), lse2[:, :half], dl[:, :half], 0)
                dk1, dv1 = block(a, tq, pl.ds(pl.multiple_of(off + half, half), half), lse2[:, half:], dl[:, half:], half)
                zero = jnp.zeros((tq - half, HEAD_PAD), F32)
                out.append((dk1 + jnp.concatenate([dk0, zero], axis=0), dv1 + jnp.concatenate([dv0, zero], axis=0)))
            return tuple(out)

        carry = lax.fori_loop(j + 1, nq, step, diagonal())
        for a in range(hs):
            dk_ref[:, heads[a]] = carry[a][0] * ATT_SCALE
            dv_ref[:, heads[a]] = carry[a][1]

        @pl.when(j == nq - 1)
        def _():
            dq_all[...] = dq_all[...] * ATT_SCALE
            pltpu.sync_copy(dq_all, dq_hbm.at[:, pl.ds(cols, wide)])

    blk = pl.BlockSpec((tq, wide), lambda g, j: (j, g))
    stat = pl.BlockSpec((hs, nq * 8, tq), lambda g, j: (g, 0, 0))
    full = jax.ShapeDtypeStruct((S, MLA_HEADS * HEAD_PAD), F32)
    return pl.pallas_call(
        body, grid=(MLA_HEADS // hs, nq), in_specs=[ANY, ANY, stat, stat, blk, blk], out_specs=[ANY, blk, blk],
        out_shape=[full, full, full],
        scratch_shapes=[pltpu.VMEM((S, wide), BF16), pltpu.VMEM((S, wide), BF16), pltpu.VMEM((S, wide), F32)],
        compiler_params=_cp(2), name=name,
    )(q, dmix, lse_rows, delta_rows, k, v)


MEM_SCALE = MEM_HEAD_DIM ** -0.5


def _xattn_probs(qh, kh):
    s = _dot(qh, kh, NT) * MEM_SCALE
    e = jnp.exp(s - jnp.max(s, axis=-1, keepdims=True))
    return e / jnp.sum(e, axis=-1, keepdims=True)


def _xa_block_fwd(x, kvm, w_q, w_o, g, *, name):
    S = x.shape[0]
    ts = min(S, 512)
    nm = kvm.shape[0]

    def body(x_ref, kv_ref, wq_ref, wo_ref, g_ref, xo_ref, hx_ref, q_ref, o_ref):
        xv = x_ref[...]
        r = lax.rsqrt(jnp.mean(xv * xv, axis=-1, keepdims=True) + RMS_EPS)
        hx = (xv * r * g_ref[...]).astype(BF16)
        hx_ref[...] = hx
        q = _dot(hx, wq_ref[...]).astype(BF16)
        q_ref[...] = q
        for h in range(MEM_HEADS):
            lo, hi = h * MEM_HEAD_DIM, (h + 1) * MEM_HEAD_DIM
            p = _xattn_probs(q[:, lo:hi], kv_ref[:, lo:hi])
            o_ref[:, lo:hi] = _dot(p.astype(BF16), kv_ref[:, D_MODEL + lo:D_MODEL + hi]).astype(o_ref.dtype)
        xo_ref[...] = xv + _dot(o_ref[...], wo_ref[...])

    square = _const((D_MODEL, D_MODEL))
    act = jax.ShapeDtypeStruct((S, D_MODEL), BF16)
    return pl.pallas_call(
        body, grid=(S // ts,),
        in_specs=[_rows(ts, D_MODEL), _const((nm, 2 * D_MODEL)), square, square, _const((1, D_MODEL))],
        out_specs=[_rows(ts, D_MODEL)] * 4, out_shape=[jax.ShapeDtypeStruct((S, D_MODEL), F32), act, act, act],
        compiler_params=_cp(1), name=name,
    )(x, kvm, w_q, w_o, g.reshape(1, D_MODEL))


def _xa_block_bwd(dxo, x, q, kvm, w_q, w_o, g, *, name):
    S = q.shape[0]
    ts = min(S, 512)
    nm = kvm.shape[0]

    def body(dxo_ref, x_ref, q_ref, kv_ref, wq_ref, wo_ref, g_ref, dx_ref, dq_ref, dkv_ref, dg_ref):
        @pl.when(pl.program_id(0) == 0)
        def _():
            dkv_ref[...] = jnp.zeros_like(dkv_ref)
            dg_ref[...] = jnp.zeros_like(dg_ref)

        dxo = dxo_ref[...]
        do = _dot(dxo.astype(BF16), wo_ref[...], NT).astype(BF16)
        for h in range(MEM_HEADS):
            lo, hi = h * MEM_HEAD_DIM, (h + 1) * MEM_HEAD_DIM
            qh, kh, vh = q_ref[:, lo:hi], kv_ref[:, lo:hi], kv_ref[:, D_MODEL + lo:D_MODEL + hi]
            doh = do[:, lo:hi]
            p = _xattn_probs(qh, kh)
            dp = _dot(doh, vh, NT)
            ds = (p * (dp - jnp.sum(dp * p, axis=-1, keepdims=True)) * MEM_SCALE).astype(BF16)
            dq_ref[:, lo:hi] = _dot(ds, kh).astype(dq_ref.dtype)
            dkv_ref[:, lo:hi] += _dot(ds, qh, TN)
            dkv_ref[:, D_MODEL + lo:D_MODEL + hi] += _dot(p.astype(BF16), doh, TN)
        dx, dg = _norm_bwd_epilogue(0)([_dot(dq_ref[...], wq_ref[...], NT)], [x_ref[...], dxo, g_ref[...]])
        dx_ref[...] = dx
        dg_ref[...] += dg

    square = _const((D_MODEL, D_MODEL))
    return pl.pallas_call(
        body, grid=(S // ts,),
        in_specs=[_rows(ts, D_MODEL), _rows(ts, D_MODEL), _rows(ts, D_MODEL), _const((nm, 2 * D_MODEL)), square,
                  square, _const((1, D_MODEL))],
        out_specs=[_rows(ts, D_MODEL), _rows(ts, D_MODEL), _const((nm, 2 * D_MODEL)), _const((1, D_MODEL))],
        out_shape=[jax.ShapeDtypeStruct((S, D_MODEL), F32), jax.ShapeDtypeStruct((S, D_MODEL), BF16),
                   jax.ShapeDtypeStruct((nm, 2 * D_MODEL), F32), jax.ShapeDtypeStruct((1, D_MODEL), F32)],
        compiler_params=_cp(1), name=name,
    )(dxo, x, q, kvm, w_q, w_o, g.reshape(1, D_MODEL))


CONV_HALO = 8


def _sigmoid(x):
    return 0.5 * jnp.tanh(0.5 * x) + 0.5


def _softplus(x):
    return jnp.maximum(x, 0.0) + jnp.log(1.0 + jnp.exp(-jnp.abs(x)))


def _neg_expm1(x):
    series = -x * (1.0 + x * (1.0 / 2) * (1.0 + x * (1.0 / 3) * (1.0 + x * (1.0 / 4) * (1.0 + x * (1.0 / 5)))))
    return jnp.where(x > -0.05, series, 1.0 - jnp.exp(x))


GELU_C = math.sqrt(2.0 / math.pi)


def _gelu(x):
    return 0.5 * x * (1.0 + jnp.tanh(GELU_C * (x + 0.044715 * x * x * x)))


def _gelu_grad(x):
    t = jnp.tanh(GELU_C * (x + 0.044715 * x * x * x))
    return 0.5 * (1.0 + t) + 0.5 * x * (1.0 - t * t) * GELU_C * (1.0 + 3 * 0.044715 * x * x)


def _lru_gates(xc, wr_ref, br, wi_ref, bi, sp, reset):
    xcb = xc.astype(BF16)
    pr, pi = [], []
    for h in range(LRU_HEADS):
        lo, hi = h * LRU_HEAD_DIM, (h + 1) * LRU_HEAD_DIM
        pr.append(_dot(xcb[:, lo:hi], wr_ref[h]))
        pi.append(_dot(xcb[:, lo:hi], wi_ref[h]))
    r = _sigmoid(jnp.concatenate(pr, axis=1) + br)
    ig = _sigmoid(jnp.concatenate(pi, axis=1) + bi)
    log_a = -LRU_C * r * sp
    a = jnp.where(reset, 0.0, jnp.exp(log_a))
    mult = jnp.where(reset, 1.0, jnp.sqrt(jnp.maximum(_neg_expm1(2.0 * log_a), 0.0)))
    return r, ig, a, mult


SUBLANES = 8


def _compose_groups(a, b, reverse):
    n = a.shape[0]
    row = lax.broadcasted_iota(jnp.int32, a.shape, 0) % SUBLANES
    for s in (1, 2, 4):
        inside = (row < SUBLANES - s) if reverse else (row >= s)
        shift = n - s if reverse else s
        a_s = jnp.where(inside, pltpu.roll(a, shift, 0), 1.0)
        b_s = jnp.where(inside, pltpu.roll(b, shift, 0), 0.0)
        b = a * b_s + b
        a = a * a_s
    return a, b


def _chain_groups(a_buf, h_ref, state, reverse):
    groups = a_buf.shape[0] // SUBLANES

    def group(g, h_in):
        off = pl.multiple_of((groups - 1 - g if reverse else g) * SUBLANES, SUBLANES)
        h = a_buf[pl.ds(off, SUBLANES), :] * h_in + h_ref[pl.ds(off, SUBLANES), :]
        h_ref[pl.ds(off, SUBLANES), :] = h
        return jnp.broadcast_to(h[0:1] if reverse else h[SUBLANES - 1:SUBLANES], h.shape)

    return lax.fori_loop(0, groups, group, state, unroll=4)[0:1]


def _lru_fwd(x, g, w_in, reset, conv_w, conv_b, w_r, b_r, w_i, b_i, lam, *, name):
    S = x.shape[0]
    ts = min(S, 512)
    W = D_MODEL

    def body(x_ref, g_ref, win_ref, rs_ref, cw_ref, cb_ref, wr_ref, br_ref, wi_ref, bi_ref, lam_ref,
             hn_ref, z_ref, xc_ref, h_ref, y_ref, a_buf, carry, tail):
        i = pl.program_id(0)

        @pl.when(i == 0)
        def _():
            carry[...] = jnp.zeros_like(carry)
            tail[...] = jnp.zeros_like(tail)

        xv = x_ref[...]
        hn = (xv * lax.rsqrt(jnp.mean(xv * xv, axis=-1, keepdims=True) + RMS_EPS) * g_ref[...]).astype(BF16)
        hn_ref[...] = hn
        z_ref[...] = _dot(hn, win_ref[...])
        xb = z_ref[:, W:]
        xe = jnp.concatenate([tail[...], xb], axis=0)
        tail[...] = xb[ts - CONV_HALO:]
        xc = cb_ref[...] + cw_ref[3:4, :] * xe[CONV_HALO:]
        for kk in range(CONV_WIDTH - 1):
            xc = xc + cw_ref[kk:kk + 1, :] * pltpu.roll(xe, CONV_WIDTH - 1 - kk, 0)[CONV_HALO:]
        xc_ref[...] = xc
        reset = rs_ref[...] > 0.5
        _, ig, a, mult = _lru_gates(xc, wr_ref, br_ref[...], wi_ref, bi_ref[...], _softplus(-lam_ref[...]), reset)
        a_buf[...], h_ref[...] = _compose_groups(a, mult * (ig * xc), False)
        carry[...] = _chain_groups(a_buf, h_ref, jnp.broadcast_to(carry[...], (SUBLANES, W)), False)
        y_ref[...] = (_gelu(z_ref[:, :W]) * h_ref[...]).astype(y_ref.dtype)

    vec = _const((1, W))
    gw = _const((LRU_HEADS, LRU_HEAD_DIM, LRU_HEAD_DIM))
    return pl.pallas_call(
        body, grid=(S // ts,),
        in_specs=[_rows(ts, W), vec, _const((W, 2 * W)), _rows(ts, 1), _const((CONV_WIDTH, W)), vec, gw, vec, gw, vec,
                  vec],
        out_specs=[_rows(ts, W), _rows(ts, 2 * W), _rows(ts, W), _rows(ts, W), _rows(ts, W)],
        out_shape=[jax.ShapeDtypeStruct((S, W), BF16), jax.ShapeDtypeStruct((S, 2 * W), F32),
                   jax.ShapeDtypeStruct((S, W), F32), jax.ShapeDtypeStruct((S, W), F32),
                   jax.ShapeDtypeStruct((S, W), BF16)],
        scratch_shapes=[pltpu.VMEM((ts, W), F32), pltpu.VMEM((1, W), F32), pltpu.VMEM((CONV_HALO, W), F32)],
        compiler_params=_cp(1), name=name,
    )(x, g.reshape(1, W), w_in, reset, conv_w, conv_b, w_r, b_r, w_i, b_i, lam)


def _lru_bwd(dxo, w_out, z, xc, hseq, reset, w_r, b_r, w_i, b_i, lam, *, name):
    S = z.shape[0]
    ts = min(S, 512)
    nt = S // ts
    nh = ts // CONV_HALO
    W = D_MODEL

    def body(dxo_ref, wout_ref, gate_ref, xc_ref, h_ref, hh_ref, rs_ref, wr_ref, br_ref, wi_ref, bi_ref, lam_ref,
             dg_ref, dxc_ref, dpr_ref, dpi_ref, acc_ref, a_buf, dh_buf, carry):
        i = pl.program_id(0)
        tile = nt - 1 - i

        @pl.when(i == 0)
        def _():
            carry[...] = jnp.zeros_like(carry)
            acc_ref[...] = jnp.zeros_like(acc_ref)

        xc = xc_ref[...]
        lam_v = lam_ref[...]
        sp = _softplus(-lam_v)
        reset = rs_ref[...] > 0.5
        r, ig, a, mult = _lru_gates(xc, wr_ref, br_ref[...], wi_ref, bi_ref[...], sp, reset)
        gate = gate_ref[...]
        dyv = _dot(dxo_ref[...].astype(BF16), wout_ref[...], NT)
        h = h_ref[...]
        dg_ref[...] = (dyv * h * _gelu_grad(gate)).astype(dg_ref.dtype)
        last_row = lax.broadcasted_iota(jnp.int32, a.shape, 0) == ts - 1
        a_buf[...], dh_buf[...] = _compose_groups(jnp.where(last_row, 1.0, pltpu.roll(a, ts - 1, 0)),
                                                  dyv * _gelu(gate), True)
        _chain_groups(a_buf, dh_buf, jnp.broadcast_to(carry[...], (SUBLANES, W)), True)
        dh = dh_buf[...]
        carry[...] = a[0:1] * dh[0:1]
        hh = jnp.where(tile > 0, hh_ref[...], 0.0)
        h_prev = pltpu.roll(jnp.concatenate([hh, h], axis=0), 1, 0)[CONV_HALO:]
        da = dh * h_prev
        bx = ig * xc
        dmult = dh * bx
        dbx = dh * mult
        di = dbx * xc
        dlog_a = jnp.where(reset, 0.0, da * a - dmult * a * a / jnp.maximum(mult, 1e-30))
        dr = dlog_a * (-LRU_C) * sp
        dpre_r = dr * r * (1.0 - r)
        dpre_i = di * ig * (1.0 - ig)
        dprb, dpib = dpre_r.astype(BF16), dpre_i.astype(BF16)
        dpr_ref[...] = dprb
        dpi_ref[...] = dpib
        back = []
        for hd in range(LRU_HEADS):
            lo, hi = hd * LRU_HEAD_DIM, (hd + 1) * LRU_HEAD_DIM
            back.append(_dot(dprb[:, lo:hi], wr_ref[hd], NT) + _dot(dpib[:, lo:hi], wi_ref[hd], NT))
        dxc_ref[...] = dbx * ig + jnp.concatenate(back, axis=1)
        dlam = jnp.sum(dlog_a * (-LRU_C) * r, axis=0, keepdims=True) * (-_sigmoid(-lam_v))
        acc_ref[0:1, :] += jnp.sum(dpre_r, axis=0, keepdims=True)
        acc_ref[1:2, :] += jnp.sum(dpre_i, axis=0, keepdims=True)
        acc_ref[2:3, :] += dlam

    rev = lambda cb: pl.BlockSpec((ts, W), lambda i: (nt - 1 - i, cb))
    vec = _const((1, W))
    gw = _const((LRU_HEADS, LRU_HEAD_DIM, LRU_HEAD_DIM))
    return pl.pallas_call(
        body, grid=(nt,),
        in_specs=[rev(0), _const((W, W)), rev(0), rev(0), rev(0),
                  pl.BlockSpec((CONV_HALO, W), lambda i: (jnp.maximum((nt - 1 - i) * nh - 1, 0), 0)),
                  pl.BlockSpec((ts, 1), lambda i: (nt - 1 - i, 0)), gw, vec, gw, vec, vec],
        out_specs=[rev(0), rev(0), rev(0), rev(0), _const((8, W))],
        out_shape=[jax.ShapeDtypeStruct((S, W), BF16), jax.ShapeDtypeStruct((S, W), F32),
                   jax.ShapeDtypeStruct((S, W), BF16), jax.ShapeDtypeStruct((S, W), BF16),
                   jax.ShapeDtypeStruct((8, W), F32)],
        scratch_shapes=[pltpu.VMEM((ts, W), F32), pltpu.VMEM((ts, W), F32), pltpu.VMEM((1, W), F32)],
        compiler_params=_cp(1), name=name,
    )(dxo, w_out, z, xc, hseq, hseq, reset, w_r, b_r, w_i, b_i, lam)


def _conv_bwd(dxc, z, conv_w, *, name):
    S = dxc.shape[0]
    ts = min(S, 512)
    nh = ts // CONV_HALO
    last = S // CONV_HALO - 1
    W = D_MODEL
    n = ts + CONV_HALO

    def body(d_ref, dn_ref, xb_ref, xp_ref, cw_ref, dxb_ref, acc_ref):
        i = pl.program_id(0)

        @pl.when(i == 0)
        def _():
            acc_ref[...] = jnp.zeros_like(acc_ref)

        d = d_ref[...]
        de = jnp.concatenate([d, jnp.where(i < pl.num_programs(0) - 1, dn_ref[...], 0.0)], axis=0)
        xe = jnp.concatenate([jnp.where(i > 0, xp_ref[...], 0.0), xb_ref[...]], axis=0)
        dxb = cw_ref[3:4, :] * d
        acc_ref[3:4, :] += jnp.sum(d * xe[CONV_HALO:], axis=0, keepdims=True)
        for kk in range(CONV_WIDTH - 1):
            sh = CONV_WIDTH - 1 - kk
            dxb = dxb + cw_ref[kk:kk + 1, :] * pltpu.roll(de, n - sh, 0)[:ts]
            acc_ref[kk:kk + 1, :] += jnp.sum(d * pltpu.roll(xe, sh, 0)[CONV_HALO:], axis=0, keepdims=True)
        dxb_ref[...] = dxb.astype(dxb_ref.dtype)
        acc_ref[4:5, :] += jnp.sum(d, axis=0, keepdims=True)

    return pl.pallas_call(
        body, grid=(S // ts,),
        in_specs=[_rows(ts, W), pl.BlockSpec((CONV_HALO, W), lambda i: (jnp.minimum((i + 1) * nh, last), 0)),
                  _rows(ts, W, 1), pl.BlockSpec((CONV_HALO, W), lambda i: (jnp.maximum(i * nh - 1, 0), 1)),
                  _const((CONV_WIDTH, W))],
        out_specs=[_rows(ts, W), _const((8, W))],
        out_shape=[jax.ShapeDtypeStruct((S, W), BF16), jax.ShapeDtypeStruct((8, W), F32)],
        compiler_params=_cp(1), name=name,
    )(dxc, dxc, z, z, conv_w)


def _loss_head(x, g, target, *, name):
    S, D = x.shape
    ts = _row_tile(S)

    def body(x_ref, g_ref, t_ref, dx_ref, dg_ref, l_ref):
        @pl.when(pl.program_id(0) == 0)
        def _():
            dg_ref[...] = jnp.zeros_like(dg_ref)
            l_ref[...] = jnp.zeros_like(l_ref)

        xv = x_ref[...]
        r = lax.rsqrt(jnp.mean(xv * xv, axis=-1, keepdims=True) + RMS_EPS)
        n = xv * r
        err = n * g_ref[...] - t_ref[...]
        l_ref[...] += 0.5 * jnp.sum(jnp.sum(err * err, axis=-1, keepdims=True) * (1.0 / D), axis=0, keepdims=True)
        dy = err * (1.0 / D)
        dn = dy * g_ref[...]
        dx_ref[...] = r * (dn - n * jnp.mean(dn * n, axis=-1, keepdims=True))
        dg_ref[...] += jnp.sum(dy * n, axis=0, keepdims=True)

    return pl.pallas_call(
        body, grid=(S // ts,), in_specs=[_rows(ts, D), _const((1, D)), _rows(ts, D)],
        out_specs=[_rows(ts, D), _const((1, D)), _const((8, LANES))],
        out_shape=[jax.ShapeDtypeStruct((S, D), F32), jax.ShapeDtypeStruct((1, D), F32),
                   jax.ShapeDtypeStruct((8, LANES), F32)],
        compiler_params=_cp(1), name=name,
    )(x, g.reshape(1, D), target)


def _adamw(w, ga, gb, m, v, *, name):
    shape = w.shape
    cols = shape[-1]
    rows = w.size // cols
    br = rows
    if rows * cols * 4 > (1 << 20):
        br = max(d for d in range(8, rows + 1, 8) if rows % d == 0 and d * cols * 4 <= (1 << 20))

    def body(w_ref, ga_ref, gb_ref, m_ref, v_ref, g_ref, d_ref, mo_ref, vo_ref):
        gv = ga_ref[...] + gb_ref[...]
        g_ref[...] = gv
        mn = ADAM_B1 * m_ref[...] + (1.0 - ADAM_B1) * gv
        vn = ADAM_B2 * v_ref[...] + (1.0 - ADAM_B2) * (gv * gv)
        m_hat = mn / (1.0 - ADAM_B1 ** ADAM_STEP)
        v_hat = vn / (1.0 - ADAM_B2 ** ADAM_STEP)
        d_ref[...] = -ADAM_LR * (m_hat / (jnp.sqrt(v_hat) + ADAM_EPS) + ADAM_WD * w_ref[...])
        mo_ref[...] = mn
        vo_ref[...] = vn

    spec = _rows(br, cols)
    outs = pl.pallas_call(
        body, grid=(rows // br,), in_specs=[spec] * 5, out_specs=[spec] * 4,
        out_shape=[jax.ShapeDtypeStruct((rows, cols), F32)] * 4, compiler_params=_cp(1), name=name,
    )(*[t.reshape(rows, cols) for t in (w, ga, gb, m, v)])
    return [o.reshape(shape) for o in outs]


def _pad_heads(w, width):
    k = w.shape[0]
    return jnp.pad(w.reshape(k, MLA_HEADS, width), ((0, 0), (0, 0), (0, HEAD_PAD - width))).reshape(k, -1)


def _unpad_heads(w, width):
    k = w.shape[0]
    return w.reshape(k, MLA_HEADS, HEAD_PAD)[:, :, :width].reshape(k, MLA_HEADS * width)


def _rope_tables(positions):
    inv_freq = ROPE_BASE ** (-jnp.arange(0, QK_ROPE, 2, dtype=F32) / QK_ROPE)
    ang = positions.astype(F32)[:, None] * inv_freq
    cos, sin = jnp.cos(ang), jnp.sin(ang)
    S = positions.shape[0]
    ones, zeros = jnp.ones((S, QK_NOPE), F32), jnp.zeros((S, QK_NOPE), F32)
    ctab = jnp.concatenate([ones, cos, cos, ones[:, :HEAD_PAD - QK_DIM]], axis=1)
    stab = jnp.concatenate([zeros, -sin, sin, zeros[:, :HEAD_PAD - QK_DIM]], axis=1)
    return ctab, stab


def _memory_block(x, mem, W, layer, tag):
    mn = _rms(mem, W["xa_norm_mem"][layer], name=f"{tag}_xa_norm_mem")
    kvm = _mm(mn, [(W["xa_w_kv"][layer], 0, 0)], _first, [(2 * D_MODEL, BF16, 0)], tn=2 * D_MODEL, nj=1,
              name=f"{tag}_xa_kv")[0]
    xo, hx, qx, o = _xa_block_fwd(x, kvm, W["xa_w_q"][layer], W["xa_w_o"][layer], W["xa_norm_x"][layer],
                                  name=f"{tag}_xa_fwd")
    return xo, (x, hx, qx, mn, kvm, o)


def _memory_block_bwd(dxo, mem, W, layer, saved, tag, grads):
    x, hx, qx, mn, kvm, o = saved
    wq, wkv, wo = W["xa_w_q"][layer], W["xa_w_kv"][layer], W["xa_w_o"][layer]
    grads["xa_w_o"][layer] = _owner_major(_mm_tn(o, dxo, name=f"{tag}_xa_dwo"), 0)
    dx, dqx, dkvm, dg = _xa_block_bwd(dxo, x, qx, kvm, wq, wo, W["xa_norm_x"][layer], name=f"{tag}_xa_bwd")
    grads["xa_w_q"][layer] = _owner_major(_mm_tn(hx, dqx, name=f"{tag}_xa_dwq"), 0)
    grads["xa_norm_x"][layer] = dg[0]
    dmn = _mm(dkvm, [(wkv, 0, 0)], _first, [(D_MODEL, F32, 0)], nt=True, tn=D_MODEL, nj=1, name=f"{tag}_xa_dmn")[0]
    grads["xa_w_kv"][layer] = _mm_tn_owners(mn, [dkvm], name=f"{tag}_xa_dwkv")
    grads["xa_norm_mem"][layer] = _norm_gain_grad(mem, dmn, name=f"{tag}_xa_norm_mem_bwd")[0]
    return dx


FF_TN = D_FF // 2

def _silu_mul(accs, extras):
    g, u = accs
    return [g * _sigmoid(g) * u, g, u]


def _silu_mul_bwd(accs, extras):
    da = accs[0]
    g, u = extras[0].astype(F32), extras[1].astype(F32)
    sg = _sigmoid(g)
    silu = g * sg
    return [da * u * (sg + silu * (1.0 - sg)), da * silu]


def _ffn_block(x, W, layer, tag):
    hf = _rms(x, W["ffn_norm"][layer], name=f"{tag}_ffn_norm")
    wgu, wd = W["ffn_w_gate_up"][layer], W["ffn_w_down"][layer]
    act, g, u = _mm(hf, [(wgu, 0, 0), (wgu, 0, 2)], _silu_mul, [(D_FF, BF16, 0)] * 3, tn=FF_TN, nj=2,
                    name=f"{tag}_ffn_up")
    xo = _mm(act, [(wd, 0, 0)], _add_res, [(D_MODEL, F32, 0)], extras=[(x, 0)], tn=D_MODEL, nj=1,
             name=f"{tag}_ffn_down")[0]
    return xo, (x, hf, act, g, u)


def _ffn_block_bwd(dxo, W, layer, saved, tag, grads):
    x, hf, act, g, u = saved
    wgu, wd = W["ffn_w_gate_up"][layer], W["ffn_w_down"][layer]
    dg, du = _mm(dxo, [(wd, 0, 0)], _silu_mul_bwd, [(D_FF, BF16, 0)] * 2, nt=True, extras=[(g, 0), (u, 0)], tn=FF_TN,
                 nj=2, name=f"{tag}_ffn_dact")
    grads["ffn_w_down"][layer] = _owner_major(_mm_tn(act, dxo, tk=FF_TN, name=f"{tag}_ffn_dwd"), 0)
    dx, dgn = _mm(dg, [(wgu, 0, 0)], _norm_bwd_epilogue(0), [(D_MODEL, F32, 0)], nt=True, also=(du, (wgu, 0, 1)),
                  extras=[(x, 0), (dxo, 0)], rows=[W["ffn_norm"][layer].reshape(1, D_MODEL)],
                  sums=[D_MODEL], tn=D_MODEL, nj=1, name=f"{tag}_ffn_dhf")
    grads["ffn_w_gate_up"][layer] = _mm_tn_owners(hf, [dg, du], name=f"{tag}_ffn_dwgu")
    grads["ffn_norm"][layer] = dgn[0]
    return dx


def _even_block(x, tabs, W, tag):
    ctab, stab = tabs
    w_in = W["ev_w_in"][0]
    zero = jnp.zeros((D_MODEL, QK_NOPE), BF16)
    w_in_pad = jnp.concatenate([w_in[:, :896], zero, w_in[:, 896:], zero[:, :HEAD_PAD - QK_DIM]], axis=1)
    w_q_pad = _pad_heads(W["ev_w_q_up"][0], QK_DIM)
    wkv = W["ev_w_kv_up"][0].reshape(KV_RANK, MLA_HEADS, QK_NOPE + V_HEAD)
    w_kv_pad = jnp.concatenate([_pad_heads(wkv[:, :, :QK_NOPE].reshape(KV_RANK, -1), QK_NOPE),
                                _pad_heads(wkv[:, :, QK_NOPE:].reshape(KV_RANK, -1), V_HEAD)], axis=1)
    w_out = W["ev_w_out"][0]
    w_att = jnp.pad(w_out[POOL_DIM:].reshape(MLA_HEADS, V_HEAD, D_MODEL), ((0, 0), (0, HEAD_PAD - V_HEAD), (0, 0)))
    w_out_pad = jnp.concatenate([w_out[:POOL_DIM], w_att.reshape(MLA_HEADS * HEAD_PAD, D_MODEL)], axis=0)
    pool_w = W["ev_pool_w"][0].astype(BF16)
    pool_scale = W["ev_pool_scale"]

    h, z, mix, pooled, cqn, ckvn, q_rot, k_cat, v_pad = _even_front(
        x, W["ev_norm"][0], w_in_pad, pool_w, pool_scale, W["ev_q_norm"][0], w_q_pad, W["ev_kv_norm"][0], w_kv_pad,
        ctab, stab, name=f"{tag}_front")
    mix, lse = _flash_fwd(q_rot, k_cat, v_pad, mix, name=f"{tag}_attn")
    xo = _mm(mix, [(w_out_pad, 0, 0)], _add_res, [(D_MODEL, F32, 0)], extras=[(x, 0)], tn=D_MODEL, nj=1,
             name=f"{tag}_out")[0]
    saved = (x, h, z, pooled, cqn, ckvn, q_rot, k_cat, v_pad, lse, mix,
             (w_in_pad, w_q_pad, w_kv_pad, w_out_pad, pool_w, pool_scale))
    return xo, saved


def _even_out_grad(dxo, saved, tag):
    mix = saved[10]
    dw_out_pad = _mm_tn(mix, dxo, tk=MIX_DIM // 3, name=f"{tag}_dw_out")
    datt = dw_out_pad[POOL_DIM:].reshape(MLA_HEADS, HEAD_PAD, D_MODEL)[:, :V_HEAD].reshape(-1, D_MODEL)
    return [_owner_major(jnp.concatenate([dw_out_pad[:POOL_DIM], datt], axis=0), 0)]


def _even_block_bwd(dxo, tabs, W, saved, tag, grads, token=None):
    ctab, stab = tabs
    x, h, z, pooled, cqn, ckvn, q_rot, k_cat, v_pad, lse, mix, wts = saved
    w_in_pad, w_q_pad, w_kv_pad, w_out_pad, pool_w, pool_scale = wts
    if token is not None:
        w_out_pad = w_out_pad + token[0:1, 0:1].astype(BF16)
    dmix = _mm(dxo, [(w_out_pad, 0, 0)], _first, [(MIX_DIM, BF16, 0)], nt=True, tn=MIX_DIM, nj=1,
               name=f"{tag}_dmix")[0]
    delta = _attn_delta(dmix, mix, name=f"{tag}_delta")
    dq_rot, dk_cat, dv_pad = _flash_bwd(q_rot, k_cat, v_pad, dmix, _retile_rows(lse, delta.shape[2]), delta,
                                        name=f"{tag}_attn_bwd")
    dx, dq_pad, dz, dypre, dgq, dgkv, dscale, dgn = _even_back(
        dq_rot, dk_cat, dv_pad, dmix, pooled, z, x, dxo, ctab, stab, w_q_pad, w_kv_pad, w_in_pad, pool_w, pool_scale,
        W["ev_q_norm"][0], W["ev_kv_norm"][0], W["ev_norm"][0], name=f"{tag}_back")
    grads["ev_q_norm"], grads["ev_kv_norm"], grads["ev_pool_scale"], grads["ev_norm"] = dgq, dgkv, dscale, dgn
    dw_q_pad = _mm_tn(cqn, dq_pad, name=f"{tag}_dw_q_up")
    grads["ev_w_q_up"] = [_owner_major(_unpad_heads(dw_q_pad, QK_DIM), 1)]
    dwk = _unpad_heads(_mm_tn(ckvn, dk_cat, name=f"{tag}_dw_k_up"), QK_NOPE).reshape(KV_RANK, MLA_HEADS, QK_NOPE)
    dwv = _unpad_heads(_mm_tn(ckvn, dv_pad, name=f"{tag}_dw_v_up"), V_HEAD).reshape(KV_RANK, MLA_HEADS, V_HEAD)
    grads["ev_w_kv_up"] = [_owner_major(jnp.concatenate([dwk, dwv], axis=2).reshape(KV_RANK, -1), 1)]
    grads["ev_pool_w"] = _mm_tn_grouped(pooled, dypre, 4, POOL_GROUP, name=f"{tag}_dpool_w")[None]
    dw_in_pad = _mm_tn(h, dz, name=f"{tag}_dw_in")
    grads["ev_w_in"] = [_owner_major(jnp.concatenate([dw_in_pad[:, :896], dw_in_pad[:, 960:992]], axis=1), 0)]
    return dx


def _odd_block(x, reset, W, tag):
    w_r, w_i = W["od_w_rgate"][0], W["od_w_igate"][0]
    vecs = [W[n].reshape(1, D_MODEL) for n in ("od_conv_b", "od_b_rgate", "od_b_igate", "od_lambda")]
    h, z, xc, hseq, y = _lru_fwd(x, W["od_norm"][0], W["od_w_in"][0], reset, W["od_conv_w"][0], vecs[0], w_r,
                                 vecs[1], w_i, vecs[2], vecs[3], name=f"{tag}_lru")
    xo = _mm(y, [(W["od_w_out"][0], 0, 0)], _add_res, [(D_MODEL, F32, 0)], extras=[(x, 0)], tn=D_MODEL, nj=1,
             name=f"{tag}_out")[0]
    return xo, (x, h, z, xc, hseq, y, vecs)


def _odd_block_bwd(dxo, reset, W, saved, tag, grads):
    x, h, z, xc, hseq, y, vecs = saved
    w_r, w_i = W["od_w_rgate"][0], W["od_w_igate"][0]
    grads["od_w_out"] = [_owner_major(_mm_tn(y, dxo, name=f"{tag}_dw_out"), 0)]
    dgate, dxc, dpr, dpi, acc = _lru_bwd(dxo, W["od_w_out"][0], z, xc, hseq, reset, w_r, vecs[1], w_i, vecs[2],
                                         vecs[3], name=f"{tag}_lru_bwd")
    grads["od_b_rgate"], grads["od_b_igate"], grads["od_lambda"] = acc[0:1], acc[1:2], acc[2:3]
    grads["od_w_rgate"] = [_owner_major(_mm_tn_grouped(xc, dpr, LRU_HEADS, LRU_HEAD_DIM, name=f"{tag}_dw_rgate"), 1)]
    grads["od_w_igate"] = [_owner_major(_mm_tn_grouped(xc, dpi, LRU_HEADS, LRU_HEAD_DIM, name=f"{tag}_dw_igate"), 1)]
    dxb, cacc = _conv_bwd(dxc, z, W["od_conv_w"][0], name=f"{tag}_conv_bwd")
    grads["od_conv_w"], grads["od_conv_b"] = cacc[None, 0:4], cacc[4:5]
    dz = jnp.concatenate([dgate, dxb], axis=1)
    grads["od_w_in"] = [_mm_tn_owners(h, [dz], name=f"{tag}_dw_in")]
    dx, dgn = _mm(dz, [(W["od_w_in"][0], 0, 0)], _norm_bwd_epilogue(0), [(D_MODEL, F32, 0)], nt=True,
                  extras=[(x, 0), (dxo, 0)], rows=[W["od_norm"][0].reshape(1, D_MODEL)], sums=[D_MODEL], tn=D_MODEL,
                  nj=1, name=f"{tag}_dh")
    grads["od_norm"] = dgn
    return dx


def _local_step(x, mem, positions, target, W, later_weights=None, exchange_earlier=None):
    tabs = _rope_tables(positions)
    reset = (positions == 0).astype(F32)[:, None]
    grads = {n: [None, None] for n in ("xa_norm_x", "xa_norm_mem", "xa_w_q", "xa_w_kv", "xa_w_o", "ffn_norm",
                                       "ffn_w_gate_up", "ffn_w_down")}
    x1, s_even = _even_block(x, tabs, W, "l0_even")
    if later_weights is not None:
        W = {**W, **later_weights(x1)}
    x2, s_xa0 = _memory_block(x1, mem, W, 0, "l0")
    x3, s_ff0 = _ffn_block(x2, W, 0, "l0")
    x4, s_odd = _odd_block(x3, reset, W, "l1_odd")
    x5, s_xa1 = _memory_block(x4, mem, W, 1, "l1")
    x6, s_ff1 = _ffn_block(x5, W, 1, "l1")
    d, dgf, loss = _loss_head(x6, W["final_norm"], target, name="loss_head")
    grads["final_norm"] = dgf[0]
    d = _ffn_block_bwd(d, W, 1, s_ff1, "l1", grads)
    d = _memory_block_bwd(d, mem, W, 1, s_xa1, "l1", grads)
    d = _odd_block_bwd(d, reset, W, s_odd, "l1_odd", grads)
    d = _ffn_block_bwd(d, W, 0, s_ff0, "l0", grads)
    d = _memory_block_bwd(d, mem, W, 0, s_xa0, "l0", grads)
    grads["ev_w_out"] = _even_out_grad(d, s_even, "l0_even")
    token = exchange_earlier(grads) if exchange_earlier is not None else None
    d = _even_block_bwd(d, tabs, W, s_even, "l0_even", grads, token)
    big = {n: grads.pop(n) for n in MATMUL_WEIGHTS}
    for n, v in grads.items():
        if isinstance(v, list):
            grads[n] = jnp.stack(v)
    return loss[0, 0], d, big, grads


WEIGHTS = ("ev_norm", "ev_w_in", "ev_pool_w", "ev_pool_scale", "ev_q_norm", "ev_w_q_up", "ev_kv_norm", "ev_w_kv_up",
           "ev_w_out", "od_norm", "od_w_in", "od_conv_w", "od_conv_b", "od_w_rgate", "od_b_rgate", "od_w_igate",
           "od_b_igate", "od_lambda", "od_w_out", "xa_norm_x", "xa_norm_mem", "xa_w_q", "xa_w_kv", "xa_w_o",
           "ffn_norm", "ffn_w_gate_up", "ffn_w_down", "final_norm")
SHARD_AXIS = {"ev_w_in": 1, "ev_w_q_up": 2, "ev_w_kv_up": 2, "ev_w_out": 1, "od_norm": 1, "od_w_in": 2,
              "od_conv_w": 2, "od_conv_b": 1, "od_w_rgate": 2, "od_b_rgate": 1, "od_w_igate": 2, "od_b_igate": 1,
              "od_lambda": 1, "od_w_out": 1, "xa_w_q": 1, "xa_w_kv": 2, "xa_w_o": 1, "ffn_w_gate_up": 2,
              "ffn_w_down": 1}
MATMUL_WEIGHTS = ("ev_w_in", "ev_w_q_up", "ev_w_kv_up", "ev_w_out", "od_w_in", "od_w_rgate", "od_w_igate",
                  "od_w_out", "xa_w_q", "xa_w_kv", "xa_w_o", "ffn_w_gate_up", "ffn_w_down")
SMALL_SHARDED = tuple(n for n in WEIGHTS if n in SHARD_AXIS and n not in MATMUL_WEIGHTS)
REPLICATED = tuple(n for n in WEIGHTS if n not in SHARD_AXIS)


def _pack(parts, quantum):
    flat = jnp.concatenate([p.reshape(-1) for p in parts])
    pad = (-flat.shape[0]) % quantum
    return jnp.pad(flat, (0, pad)).reshape(-1, LANES)


def _unpack(flat, shapes):
    out, off = [], 0
    for shape in shapes:
        size = math.prod(shape)
        out.append(flat[off:off + size].reshape(shape))
        off += size
    return out


def _run_copies(local, remote, send_sems, recv_sems, local_sems):
    locals_ = [pltpu.make_async_copy(src, dst, local_sems.at[n]) for n, (src, dst) in enumerate(local)]
    for cp in locals_:
        cp.start()
    sends = [pltpu.make_async_remote_copy(src_ref=src, dst_ref=dst, send_sem=send_sems.at[k, n],
                                          recv_sem=recv_sems.at[k, n], device_id=dev, device_id_type=MESH)
             for (k, n, src, dst, _, dev) in remote]
    for cp in sends:
        cp.start()
    for (k, n, src, _, arrival, dev) in remote:
        pltpu.make_async_remote_copy(src_ref=src, dst_ref=arrival, send_sem=send_sems.at[k, n],
                                     recv_sem=recv_sems.at[k, n], device_id=dev, device_id_type=MESH).wait_recv()
    for cp in sends:
        cp.wait_send()
    for cp in locals_:
        cp.wait()


def _chip_peers(x, y):
    return [(1 - x, y), (x, 1 - y), (1 - x, 1 - y)]


def _owner_block(ref, axis, q):
    size = ref.shape[axis] // N_CHIPS
    idx = [slice(None)] * len(ref.shape)
    idx[axis] = pl.ds(q * size, size)
    return ref.at[tuple(idx)]


def _comm_call(body, ins, out_shapes, n_items, n_peers, *, name):
    return pl.pallas_call(
        body, in_specs=[ANY] * len(ins), out_specs=[ANY] * len(out_shapes), out_shape=out_shapes,
        scratch_shapes=[pltpu.SemaphoreType.DMA((n_peers, n_items)), pltpu.SemaphoreType.DMA((n_peers, n_items)),
                        pltpu.SemaphoreType.DMA((n_items,))],
        name=name,
    )(*ins)


def _gather_chips(shards, axes, *, name):
    n = len(shards)
    full = [jax.ShapeDtypeStruct(tuple(d * (N_CHIPS if a == ax else 1) for a, d in enumerate(s.shape)), s.dtype)
            for s, ax in zip(shards, axes)]

    def body(*refs):
        srcs, dsts = refs[:n], refs[n:2 * n]
        x, y, c = lax.axis_index("x"), lax.axis_index("y"), lax.axis_index("c")
        me = 2 * x + y
        local = [(srcs[i], _owner_block(dsts[i], axes[i], me)) for i in range(n)]
        remote = [(k, i, srcs[i], _owner_block(dsts[i], axes[i], me), _owner_block(dsts[i], axes[i], 2 * px + py),
                   (px, py, c))
                  for k, (px, py) in enumerate(_chip_peers(x, y)) for i in range(n)]
        _run_copies(local, remote, *refs[2 * n:])

    return _comm_call(body, shards, full, n, 3, name=name)


HBM = pl.BlockSpec(memory_space=pltpu.HBM)
SEM = pl.BlockSpec(memory_space=pltpu.SEMAPHORE)
DATAFLOW = pltpu.SideEffectType.DATAFLOW_SIDE_EFFECTING


def _gather_plan(axes):
    return lambda srcs, lands, me, peer: [
        (srcs[i], _owner_block(lands[i], ax, me), _owner_block(lands[i], ax, peer)) for i, ax in enumerate(axes)]


def _exchange_plan(where):
    return lambda srcs, lands, me, peer: [
        (srcs[i].at[peer], lands[n].at[me, l], lands[n].at[peer, l]) for i, (n, l) in enumerate(where)]


def _split_peers(sibling):
    x, y, c = lax.axis_index("x"), lax.axis_index("y"), lax.axis_index("c")
    peers = [((px, py, c), 2 * px + py) for px, py in _chip_peers(x, y)]
    return 2 * x + y, peers + ([((x, y, 1 - c), 2 * x + y)] if sibling else [])


def _split_start(srcs, lands, plan, *, sibling=False, name):
    ns, nl = len(srcs), len(lands)
    nsem = (3 + sibling) * len(plan(list(srcs), list(lands), 0, 0))

    def body(*refs):
        src_refs, land_refs = refs[:ns], refs[ns:ns + nl]
        send_sems, recv_sems = refs[ns + nl:ns + nl + nsem], refs[ns + nl + nsem:ns + nl + 2 * nsem]
        me, peers = _split_peers(sibling)
        n = 0
        for device, chip in peers:
            for src, dst, _ in plan(src_refs, land_refs, me, chip):
                pltpu.make_async_remote_copy(src_ref=src, dst_ref=dst, send_sem=send_sems[n], recv_sem=recv_sems[n],
                                             device_id=device, device_id_type=MESH).start()
                n += 1
        refs[-1][...] = jnp.zeros_like(refs[-1])

    arrays = list(srcs) + list(lands)
    out = pl.pallas_call(
        body, name=name, in_specs=[HBM] * (ns + nl),
        out_specs=[SEM] * (2 * nsem) + [HBM] * (ns + nl) + [pl.BlockSpec(memory_space=pltpu.VMEM)],
        out_shape=[pltpu.SemaphoreType.DMA(())] * (2 * nsem) + [pltpu.HBM(a.shape, a.dtype) for a in arrays]
        + [jax.ShapeDtypeStruct((8, LANES), F32)],
        input_output_aliases={i: 2 * nsem + i for i in range(ns + nl)},
        compiler_params=pltpu.CompilerParams(has_side_effects=DATAFLOW),
    )(*[pltpu.with_memory_space_constraint(a, pltpu.HBM) for a in arrays])
    sems, rest = out[:2 * nsem], out[2 * nsem:]
    return sems[:nsem], sems[nsem:], rest[:ns], rest[ns:ns + nl], rest[-1]


def _split_wait(handle, after, plan, *, sibling=False, name):
    send_sems, recv_sems, srcs, lands, _ = handle
    ns, nl, nsem = len(srcs), len(lands), len(send_sems)

    def body(*refs):
        src_refs, land_refs = refs[:ns], refs[ns:ns + nl]
        send_refs, recv_refs = refs[ns + nl:ns + nl + nsem], refs[ns + nl + nsem:ns + nl + 2 * nsem]
        me, peers = _split_peers(sibling)
        n = 0
        for device, chip in peers:
            for src, _, arrival in plan(src_refs, land_refs, me, chip):
                cp = pltpu.make_async_remote_copy(src_ref=src, dst_ref=arrival, send_sem=send_refs[n],
                                                  recv_sem=recv_refs[n], device_id=device, device_id_type=MESH)
                cp.wait_send()
                cp.wait_recv()
                n += 1

    out = pl.pallas_call(
        body, name=name, in_specs=[HBM] * (ns + nl) + [SEM] * (2 * nsem) + [ANY], out_specs=[HBM] * (ns + nl),
        out_shape=[pltpu.HBM(a.shape, a.dtype) for a in list(srcs) + list(lands)],
        input_output_aliases={i: i for i in range(ns + nl)},
        compiler_params=pltpu.CompilerParams(has_side_effects=DATAFLOW),
    )(*srcs, *lands, *send_sems, *recv_sems, after)
    return out[ns:]


def _exchange_sibling(arrays, *, name):
    n = len(arrays)

    def body(*refs):
        x, y, c = lax.axis_index("x"), lax.axis_index("y"), lax.axis_index("c")
        remote = [(0, i, refs[i], refs[n + i], refs[n + i], (x, y, 1 - c)) for i in range(n)]
        _run_copies([], remote, *refs[2 * n:])

    return _comm_call(body, arrays, [jax.ShapeDtypeStruct(a.shape, a.dtype) for a in arrays], n, 1, name=name)


def _sum_slots(r, *, token=None, name):
    shape = r.shape[1:]
    cols = shape[-1]
    rows = math.prod(shape) // cols
    tr = max(d for d in range(8, rows + 1, 8) if rows % d == 0 and d * cols * 16 <= (4 << 20))

    def body(r_ref, *refs):
        total = ((r_ref[0] + r_ref[1]) + r_ref[2]) + r_ref[3]
        refs[-1][...] = total if token is None else total + refs[0][0:1, 0:1]

    in_specs = [pl.BlockSpec((N_CHIPS, tr, cols), lambda i: (0, i, 0))]
    in_specs += [] if token is None else [_const((8, LANES))]
    return pl.pallas_call(
        body, grid=(rows // tr,), in_specs=in_specs,
        out_specs=_rows(tr, cols), out_shape=jax.ShapeDtypeStruct((rows, cols), F32), compiler_params=_cp(1),
        name=name,
    )(r.reshape(N_CHIPS, rows, cols), *([] if token is None else [token])).reshape(shape)


FIRST_WEIGHTS = ("ev_w_in", "ev_w_q_up", "ev_w_kv_up", "ev_w_out")
LATER_WEIGHTS = tuple(n for n in MATMUL_WEIGHTS if n not in FIRST_WEIGHTS)
LAST_GRADS = ("ev_w_in", "ev_w_q_up", "ev_w_kv_up")
EARLIER_GRADS = tuple(n for n in MATMUL_WEIGHTS if n not in LAST_GRADS)


def _my_chip():
    return 2 * lax.axis_index("x") + lax.axis_index("y")


def _gather_first(w):
    small = _pack([w[n] for n in SMALL_SHARDED], 8 * LANES)
    stacked = [n for n in FIRST_WEIGHTS if SHARD_AXIS[n] == w[n].ndim - 1 and w[n].shape[-1] % LANES]
    shards = [w[n].astype(BF16)[None] if n in stacked else w[n].astype(BF16) for n in FIRST_WEIGHTS]
    got = _gather_chips(shards + [small], [0 if n in stacked else SHARD_AXIS[n] for n in FIRST_WEIGHTS] + [0],
                        name="gather_first")
    full = {n: w[n] for n in REPLICATED}
    for n, g in zip(FIRST_WEIGHTS, got[:-1]):
        full[n] = jnp.concatenate([g[q] for q in range(N_CHIPS)], axis=SHARD_AXIS[n]) if n in stacked else g
    per_chip = [_unpack(got[-1][q * small.shape[0]:(q + 1) * small.shape[0]].reshape(-1),
                        [w[n].shape for n in SMALL_SHARDED]) for q in range(N_CHIPS)]
    for i, n in enumerate(SMALL_SHARDED):
        full[n] = jnp.concatenate([per_chip[q][i] for q in range(N_CHIPS)], axis=SHARD_AXIS[n])
    return full


def _gather_later_start(w, after):
    behind = (after.reshape(-1)[0] * 0).astype(BF16)
    shards = [w[n].astype(BF16) + (behind if n == "od_w_rgate" else 0) for n in LATER_WEIGHTS]
    axes = [SHARD_AXIS[n] for n in LATER_WEIGHTS]
    lands = [lax.empty(tuple(d * (N_CHIPS if a == ax else 1) for a, d in enumerate(s.shape)), s.dtype)
             for s, ax in zip(shards, axes)]
    plan = _gather_plan(axes)
    return _split_start(shards, lands, plan, sibling=True, name="gather_later_start"), plan


def _owner_major(g, axis):
    shape = g.shape
    size = shape[axis] // N_CHIPS
    g = jnp.moveaxis(g.reshape(shape[:axis] + (N_CHIPS, size) + shape[axis + 1:]), axis, 0)
    return g.reshape(N_CHIPS, -1, shape[-1] if axis < len(shape) - 1 else size)


def _exchange_start(items, *, cross, name):
    me = _my_chip()
    srcs, lands, where = [], [], []
    for n, layers in enumerate(items):
        land = lax.empty((N_CHIPS, len(layers)) + layers[0].shape[1:], layers[0].dtype)
        for l, a in enumerate(layers):
            if not cross:
                own = lax.dynamic_index_in_dim(a, me, 0, keepdims=True)[:, None]
                land = lax.dynamic_update_slice(land, own, (me, l) + (0,) * (a.ndim - 1))
            srcs.append(a)
            where.append((n, l))
        lands.append(land)
    plan = _exchange_plan(where)
    return _split_start(srcs, lands, plan, sibling=cross, name=name), plan


def _earlier_items(grads, full_shapes):
    small = [_pack([jnp.split(grads[n].reshape(full_shapes[n]), N_CHIPS, axis=SHARD_AXIS[n])[q]
                    for n in SMALL_SHARDED], 8 * LANES) for q in range(N_CHIPS)]
    return [grads[n] for n in EARLIER_GRADS] + [[jnp.stack(small)]]


def _last_items(big, grads, full_shapes, loss):
    repl = _pack([grads[n].reshape(full_shapes[n]) for n in REPLICATED] + [loss.reshape(1)], 8 * LANES)
    return [big[n] for n in LAST_GRADS] + [[jnp.stack([repl] * N_CHIPS)]]


def kernel(
        x, mem, positions, ev_norm, ev_w_in, ev_pool_w, ev_pool_scale, ev_q_norm, ev_w_q_up, ev_kv_norm,
        ev_w_kv_up, ev_w_out, od_norm, od_w_in, od_conv_w, od_conv_b, od_w_rgate, od_b_rgate, od_w_igate,
        od_b_igate, od_lambda, od_w_out, xa_norm_x, xa_norm_mem, xa_w_q, xa_w_kv, xa_w_o, ffn_norm,
        ffn_w_gate_up, ffn_w_down, final_norm, loss_target, m_ev_norm, m_ev_w_in, m_ev_pool_w, m_ev_pool_scale,
        m_ev_q_norm, m_ev_w_q_up, m_ev_kv_norm, m_ev_w_kv_up, m_ev_w_out, m_od_norm, m_od_w_in, m_od_conv_w,
        m_od_conv_b, m_od_w_rgate, m_od_b_rgate, m_od_w_igate, m_od_b_igate, m_od_lambda, m_od_w_out,
        m_xa_norm_x, m_xa_norm_mem, m_xa_w_q, m_xa_w_kv, m_xa_w_o, m_ffn_norm, m_ffn_w_gate_up, m_ffn_w_down,
        m_final_norm, v_ev_norm, v_ev_w_in, v_ev_pool_w, v_ev_pool_scale, v_ev_q_norm, v_ev_w_q_up,
        v_ev_kv_norm, v_ev_w_kv_up, v_ev_w_out, v_od_norm, v_od_w_in, v_od_conv_w, v_od_conv_b, v_od_w_rgate,
        v_od_b_rgate, v_od_w_igate, v_od_b_igate, v_od_lambda, v_od_w_out, v_xa_norm_x, v_xa_norm_mem, v_xa_w_q,
        v_xa_w_kv, v_xa_w_o, v_ffn_norm, v_ffn_w_gate_up, v_ffn_w_down, v_final_norm):
    given = dict(locals())
    w = {n: given[n] for n in WEIGHTS}
    full_shapes = {n: tuple(d * (N_CHIPS if a == SHARD_AXIS.get(n) else 1) for a, d in enumerate(w[n].shape))
                   for n in WEIGHTS}
    full = _gather_first(w)
    later, later_plan = _gather_later_start(w, full["ev_w_out"])
    full["ev_norm"] = full["ev_norm"] + later[4][0:1, 0:1]
    exchange = {}

    def later_weights(after):
        return dict(zip(LATER_WEIGHTS, _split_wait(later, after, later_plan, sibling=True, name="gather_later_wait")))

    def exchange_earlier(grads):
        exchange["handle"], exchange["plan"] = _exchange_start(_earlier_items(grads, full_shapes), cross=True,
                                                               name="exchange_earlier_start")
        return exchange["handle"][4]

    loss, grad_x, big, grads = _local_step(x[0], mem[0], positions[0], loss_target[0], full, later_weights,
                                           exchange_earlier)
    earlier = EARLIER_GRADS + ("small",)
    got = dict(zip(earlier, _split_wait(exchange["handle"], grad_x, exchange["plan"], sibling=True,
                                        name="exchange_earlier_wait")))
    last, last_plan = _exchange_start(_last_items(big, grads, full_shapes, loss), cross=False,
                                      name="exchange_last_start")
    out = {}

    def finish(names, landed, token, tag):
        mine = [_sum_slots(landed[n], token=token if i == 0 else None, name=f"sum_chips_{n}")
                for i, n in enumerate(names)]
        other = _exchange_sibling(mine, name=f"exchange_sibling_{tag}")
        total = None
        for n, a, b in zip(names, mine, other):
            if n in MATMUL_WEIGHTS:
                out[n] = _adamw(w[n], a.reshape(w[n].shape), b.reshape(w[n].shape), given["m_" + n], given["v_" + n],
                                name=f"adamw_{n}")
                continue
            group = SMALL_SHARDED if n == "small" else REPLICATED
            spare = [jnp.zeros((1,), F32)] if group is REPLICATED else []
            packed = [_pack([given[pre + k] for k in group] + spare, 8 * LANES) for pre in ("", "m_", "v_")]
            res = _adamw(packed[0], a.reshape(packed[0].shape), b.reshape(packed[0].shape), packed[1], packed[2],
                         name=f"adamw_{n}")
            shapes = [w[k].shape for k in group] + [(1,)] * len(spare)
            for j, arrs in enumerate(zip(*[_unpack(r.reshape(-1), shapes) for r in res])):
                if j < len(group):
                    out[group[j]] = list(arrs)
                else:
                    total = arrs[0][0]
        return total

    finish(earlier, got, last[4], "earlier")
    names = LAST_GRADS + ("replicated",)
    got = dict(zip(names, _split_wait(last, out[EARLIER_GRADS[-1]][1], last_plan, name="exchange_last_wait")))
    loss = finish(names, got, None, "last")
    return (loss, grad_x[None], *[out[n][k] for k in range(4) for n in WEIGHTS])
```

```python
import math

import jax
import jax.numpy as jnp
from jax import lax
from jax.experimental import pallas as pl
from jax.experimental.pallas import tpu as pltpu

F32 = jnp.float32
BF16 = jnp.bfloat16

D_MODEL = 1024
POOL_DIM = 512
POOL_WINDOWS = (2, 4, 8, 16)
POOL_GROUP = 128
MLA_HEADS = 8
QK_NOPE = 64
QK_ROPE = 32
QK_DIM = QK_NOPE + QK_ROPE
V_HEAD = 64
HEAD_PAD = 128
Q_RANK = 256
KV_RANK = 128
ROPE_BASE = 10000.0
LRU_HEADS = 4
LRU_HEAD_DIM = 256
CONV_WIDTH = 4
LRU_C = 8.0
MEM_HEADS = 4
MEM_HEAD_DIM = 256
D_FF = 2816
RMS_EPS = 1e-6
NEG_INF = -1e30

ADAM_LR = 0.001
ADAM_B1 = 0.9
ADAM_B2 = 0.999
ADAM_EPS = 1e-08
ADAM_WD = 0.01
ADAM_STEP = 10

N_CHIPS = 4
LANES = 128
VMEM_LIMIT = 56 * 1024 * 1024
MESH = pl.DeviceIdType.MESH
ANY = pl.BlockSpec(memory_space=pl.ANY)
MIX_DIM = POOL_DIM + MLA_HEADS * HEAD_PAD

NN = (((1,), (0,)), ((), ()))
NT = (((1,), (1,)), ((), ()))
TN = (((0,), (0,)), ((), ()))


def _cp(n):
    return pltpu.CompilerParams(dimension_semantics=("arbitrary",) * n, vmem_limit_bytes=VMEM_LIMIT)


def _dot(a, b, dims=NN):
    return lax.dot_general(a, b, dims, preferred_element_type=F32)


def _row_tile(S):
    return 1024 if S % 1024 == 0 else min(S, 512)


def _rows(ts, w, cb=0):
    return pl.BlockSpec((ts, w), lambda i: (i, cb))


def _const(shape):
    return pl.BlockSpec(shape, lambda i: (0,) * len(shape))


MM_VMEM_BUDGET = 40 * 1024 * 1024


def _mm(a, bs, epi, outs, *, tn, nj, nt=False, also=None, extras=(), rows=(), sums=(), a_cb=0, k=None, tm=None,
        name):
    M = a.shape[0]
    k = k or a.shape[1]
    nb, ne, nr, no = len(bs), len(extras), len(rows), len(outs)
    lhs = [(a, k, a_cb, b) for b in bs[:1]] + ([(also[0], also[0].shape[1], 0, also[1])] if also else [])
    if tm is None:
        per_row = 2 * (sum(kk * x.dtype.itemsize for x, kk, _, _ in lhs)
                       + sum(e.dtype.itemsize for e, _ in extras) * tn
                       + sum(jnp.dtype(dt).itemsize for _, dt, _ in outs) * tn) + nb * tn * 4
        weights = (1 if nj == 1 else 2) * (sum(b.dtype.itemsize for b, _, _ in bs) * k
                                           + (also[1][0].dtype.itemsize * lhs[-1][1] if also else 0)) * tn
        tm = 1024 if M % 1024 == 0 and 1024 * per_row + weights <= MM_VMEM_BUDGET else min(M, 512)
    dims = NT if nt else NN
    assert not sums or nj == 1
    na = 2 if also else 0

    def body(*refs):
        av = refs[0][...].astype(BF16)
        accs = [_dot(av, r[...].astype(BF16), dims) for r in refs[1:1 + nb]]
        if also:
            accs[0] = accs[0] + _dot(refs[1 + nb][...].astype(BF16), refs[2 + nb][...].astype(BF16), dims)
        refs = refs[:1 + nb] + refs[1 + nb + na:]
        vals = epi(accs, [r[...] for r in refs[1 + nb:1 + nb + ne + nr]])
        outs_refs = refs[1 + nb + ne + nr:]
        for o, v in zip(outs_refs[:no], vals[:no]):
            o[...] = v.astype(o.dtype)
        if sums:
            @pl.when(pl.program_id(1) == 0)
            def _():
                for o in outs_refs[no:]:
                    o[...] = jnp.zeros_like(o)

            for o, v in zip(outs_refs[no:], vals[no:]):
                o[...] += v

    in_specs = [pl.BlockSpec((tm, k), lambda j, i: (i, a_cb))]
    weights = [(k, rb, cb) for (_, rb, cb) in bs]
    if also:
        in_specs_also = pl.BlockSpec((tm, lhs[-1][1]), lambda j, i: (i, 0))
        weights.append((lhs[-1][1], also[1][1], also[1][2]))
    for n, (kk, rb, cb) in enumerate(weights):
        if also and n == nb:
            in_specs.append(in_specs_also)
        mode = dict(pipeline_mode=pl.Buffered(1)) if nj == 1 else {}
        if nt:
            in_specs.append(pl.BlockSpec((tn, kk), lambda j, i, rb=rb, cb=cb: (rb + j, cb), **mode))
        else:
            in_specs.append(pl.BlockSpec((kk, tn), lambda j, i, rb=rb, cb=cb: (rb, cb + j), **mode))
    for (_, cb) in extras:
        in_specs.append(pl.BlockSpec((tm, tn), lambda j, i, cb=cb: (i, cb + j)))
    in_specs += [pl.BlockSpec((1, tn), lambda j, i: (0, 0))] * nr
    out_specs = [pl.BlockSpec((tm, tn), lambda j, i, cb=cb: (i, cb + j)) for (_, _, cb) in outs]
    out_specs += [pl.BlockSpec((1, w), lambda j, i: (0, 0)) for w in sums]
    res = pl.pallas_call(
        body, grid=(nj, M // tm), in_specs=in_specs, out_specs=out_specs,
        out_shape=[jax.ShapeDtypeStruct((M, n), dt) for (n, dt, _) in outs]
        + [jax.ShapeDtypeStruct((1, w), F32) for w in sums],
        compiler_params=_cp(2), name=name,
    )(a, *[b for (b, _, _) in bs], *([also[0], also[1][0]] if also else []), *[e for (e, _) in extras], *rows)
    return res


def _first(accs, extras):
    return [accs[0]]


def _add_res(accs, extras):
    return [accs[0] + extras[0].astype(F32)]


def _norm_bwd_epilogue(partials):
    def epi(accs, vals):
        dh = accs[0]
        for part in vals[:partials]:
            dh = dh + part.astype(F32)
        x, res, g = vals[partials:partials + 3]
        r = lax.rsqrt(jnp.mean(x * x, axis=-1, keepdims=True) + RMS_EPS)
        n = x * r
        dn = dh * g
        return [r * (dn - n * jnp.mean(dn * n, axis=-1, keepdims=True)) + res, jnp.sum(dh * n, axis=0, keepdims=True)]

    return epi


TN_VMEM_BUDGET = 44 * 1024 * 1024


def _contraction_rows(S, row_bytes, out_elems):
    ts = min(S, 2048)
    while ts > 512 and 2 * (ts * row_bytes + out_elems * 4) > TN_VMEM_BUDGET:
        ts //= 2
    return ts


def _mm_tn(a, b, *, ka=None, a_cb=0, nb=None, b_cb=0, tk=None, tn=None, ts=None, name):
    S = a.shape[0]
    ka = ka or a.shape[1]
    nb = nb or b.shape[1]
    tk = tk or ka
    tn = tn or nb
    ts = ts or _contraction_rows(S, tk * a.dtype.itemsize + tn * b.dtype.itemsize, tk * tn)
    a0, b0 = a_cb * (ka // tk), b_cb * (nb // tn)

    def body(a_ref, b_ref, o_ref):
        @pl.when(pl.program_id(2) == 0)
        def _():
            o_ref[...] = jnp.zeros_like(o_ref)

        o_ref[...] += _dot(a_ref[...].astype(BF16), b_ref[...].astype(BF16), TN)

    return pl.pallas_call(
        body, grid=(ka // tk, nb // tn, S // ts),
        in_specs=[pl.BlockSpec((ts, tk), lambda p, q, s: (s, a0 + p)),
                  pl.BlockSpec((ts, tn), lambda p, q, s: (s, b0 + q))],
        out_specs=pl.BlockSpec((tk, tn), lambda p, q, s: (p, q)),
        out_shape=jax.ShapeDtypeStruct((ka, nb), F32), compiler_params=_cp(3), name=name,
    )(a, b)


def _mm_tn_owners(a, bs, *, name):
    S, ka = a.shape
    nb = sum(b.shape[1] for b in bs)
    tn = nb // N_CHIPS
    ts = _contraction_rows(S, ka * a.dtype.itemsize + len(bs) * tn * bs[0].dtype.itemsize, ka * tn)
    per = N_CHIPS // len(bs)

    def body(a_ref, *refs):
        o_ref = refs[-1]
        q = pl.program_id(0)

        @pl.when(pl.program_id(1) == 0)
        def _():
            o_ref[...] = jnp.zeros_like(o_ref)

        av = a_ref[...].astype(BF16)
        for n, b_ref in enumerate(refs[:-1]):
            @pl.when(q // per == n)
            def _():
                o_ref[0] += _dot(av, b_ref[...].astype(BF16), TN)

    in_specs = [pl.BlockSpec((ts, ka), lambda q, s: (s, 0))]
    for n in range(len(bs)):
        in_specs.append(pl.BlockSpec((ts, tn), lambda q, s, n=n: (jnp.where(q // per == n, s, 0),
                                                                  jnp.clip(q - n * per, 0, per - 1))))
    return pl.pallas_call(
        body, grid=(N_CHIPS, S // ts), in_specs=in_specs,
        out_specs=pl.BlockSpec((1, ka, tn), lambda q, s: (q, 0, 0)),
        out_shape=jax.ShapeDtypeStruct((N_CHIPS, ka, tn), F32), compiler_params=_cp(2), name=name,
    )(a, *bs)


def _mm_tn_grouped(a, b, groups, w, *, name):
    S = a.shape[0]
    ts = _contraction_rows(S, w * (a.dtype.itemsize + b.dtype.itemsize), w * w)

    def body(a_ref, b_ref, o_ref):
        @pl.when(pl.program_id(1) == 0)
        def _():
            o_ref[...] = jnp.zeros_like(o_ref)

        o_ref[0] += _dot(a_ref[...].astype(BF16), b_ref[...].astype(BF16), TN)

    return pl.pallas_call(
        body, grid=(groups, S // ts),
        in_specs=[pl.BlockSpec((ts, w), lambda g, s: (s, g)), pl.BlockSpec((ts, w), lambda g, s: (s, g))],
        out_specs=pl.BlockSpec((1, w, w), lambda g, s: (g, 0, 0)),
        out_shape=jax.ShapeDtypeStruct((groups, w, w), F32), compiler_params=_cp(2), name=name,
    )(a, b)


def _rms(x, g, *, name):
    S, w = x.shape
    ts = _row_tile(S)

    def body(x_ref, g_ref, o_ref):
        xv = x_ref[...]
        r = lax.rsqrt(jnp.mean(xv * xv, axis=-1, keepdims=True) + RMS_EPS)
        o_ref[...] = (xv * r * g_ref[...]).astype(o_ref.dtype)

    return pl.pallas_call(
        body, grid=(S // ts,), in_specs=[_rows(ts, w), _const((1, w))], out_specs=_rows(ts, w),
        out_shape=jax.ShapeDtypeStruct((S, w), BF16), compiler_params=_cp(1), name=name,
    )(x, g.reshape(1, w))


def _norm_gain_grad(x, dy, *, name):
    S, w = x.shape
    ts = _row_tile(S)

    def body(x_ref, dy_ref, dg_ref):
        @pl.when(pl.program_id(0) == 0)
        def _():
            dg_ref[...] = jnp.zeros_like(dg_ref)

        xv = x_ref[...]
        r = lax.rsqrt(jnp.mean(xv * xv, axis=-1, keepdims=True) + RMS_EPS)
        dg_ref[...] += jnp.sum(dy_ref[...] * (xv * r), axis=0, keepdims=True)

    return pl.pallas_call(
        body, grid=(S // ts,), in_specs=[_rows(ts, w), _rows(ts, w)], out_specs=_const((1, w)),
        out_shape=jax.ShapeDtypeStruct((1, w), F32), compiler_params=_cp(1), name=name,
    )(x, dy)


HALO = 16


def _pool_counts(i, ts, rows, first_row):
    t = i * ts + first_row + lax.broadcasted_iota(jnp.int32, (rows, 1), 0)
    return [jnp.minimum(t + 1, w).astype(F32) for w in POOL_WINDOWS]


def _even_front(x, g, w_in, pool_w, pool_scale, g_q, w_q, g_kv, w_kv, ctab, stab, *, name):
    S = x.shape[0]
    ts = min(S, 512)

    def body(x_ref, g_ref, win_ref, pw_ref, sc_ref, gq_ref, wq_ref, gkv_ref, wkv_ref, c_ref, s_ref,
             h_ref, z_ref, y_ref, p_ref, cqn_ref, ckvn_ref, q_ref, k_ref, v_ref, tail):
        i = pl.program_id(0)

        def normed(t, gain):
            r = lax.rsqrt(jnp.mean(t * t, axis=-1, keepdims=True) + RMS_EPS)
            return (t * r * gain).astype(BF16)

        h = normed(x_ref[...], g_ref[...])
        h_ref[...] = h
        z = _dot(h, win_ref[...])
        z_ref[...] = z
        u = z[:, :POOL_DIM]
        xe = jnp.concatenate([jnp.where(i > 0, tail[...], 0.0), u], axis=0)
        tail[...] = u[ts - HALO:]
        sums = []
        s = xe
        for sh in (1, 2, 4, 8):
            s = s + pltpu.roll(s, sh, 0)
            sums.append(s)
        cnts = _pool_counts(i, ts, ts, 0)
        for grp in range(4):
            lo, hi = grp * POOL_GROUP, (grp + 1) * POOL_GROUP
            pooled = (sums[grp][HALO:, lo:hi] / cnts[grp] - u[:, lo:hi]).astype(BF16)
            p_ref[:, lo:hi] = pooled
            y_ref[:, lo:hi] = (_dot(pooled, pw_ref[grp]) * sc_ref[:, lo:hi]).astype(y_ref.dtype)
        cqn = normed(z[:, POOL_DIM:POOL_DIM + Q_RANK], gq_ref[...])
        ckvn = normed(z[:, POOL_DIM + Q_RANK:POOL_DIM + Q_RANK + KV_RANK], gkv_ref[...])
        cqn_ref[...] = cqn
        ckvn_ref[...] = ckvn
        q = _dot(cqn, wq_ref[...])
        kv = _dot(ckvn, wkv_ref[...])
        c, sn = c_ref[...], s_ref[...]
        kr = z[:, D_MODEL - HEAD_PAD:]
        kr_rot = kr * c + _rope_partner(kr) * sn
        lane = lax.broadcasted_iota(jnp.int32, (ts, HEAD_PAD), 1)
        for hd in range(MLA_HEADS):
            lo, hi = hd * HEAD_PAD, (hd + 1) * HEAD_PAD
            qh = q[:, lo:hi]
            q_ref[:, lo:hi] = (qh * c + _rope_partner(qh) * sn).astype(q_ref.dtype)
            k_ref[:, lo:hi] = (kv[:, lo:hi] + kr_rot).astype(k_ref.dtype)
            v_ref[:, lo:hi] = jnp.where(lane == V_HEAD, 1.0, kv[:, D_MODEL + lo:D_MODEL + hi]).astype(v_ref.dtype)

    wide = jax.ShapeDtypeStruct((S, D_MODEL), BF16)
    return pl.pallas_call(
        body, grid=(S // ts,),
        in_specs=[_rows(ts, D_MODEL), _const((1, D_MODEL)), _const((D_MODEL, D_MODEL)),
                  _const((4, POOL_GROUP, POOL_GROUP)), _const((1, POOL_DIM)), _const((1, Q_RANK)),
                  _const((Q_RANK, D_MODEL)), _const((1, KV_RANK)), _const((KV_RANK, 2 * D_MODEL)),
                  _rows(ts, HEAD_PAD), _rows(ts, HEAD_PAD)],
        out_specs=[_rows(ts, D_MODEL), _rows(ts, D_MODEL), _rows(ts, POOL_DIM), _rows(ts, POOL_DIM),
                   _rows(ts, Q_RANK), _rows(ts, KV_RANK), _rows(ts, D_MODEL), _rows(ts, D_MODEL), _rows(ts, D_MODEL)],
        out_shape=[wide, jax.ShapeDtypeStruct((S, D_MODEL), F32), jax.ShapeDtypeStruct((S, MIX_DIM), BF16),
                   jax.ShapeDtypeStruct((S, POOL_DIM), BF16), jax.ShapeDtypeStruct((S, Q_RANK), BF16),
                   jax.ShapeDtypeStruct((S, KV_RANK), BF16), wide, wide, wide],
        scratch_shapes=[pltpu.VMEM((HALO, POOL_DIM), F32)], compiler_params=_cp(1), name=name,
    )(x, g.reshape(1, D_MODEL), w_in, pool_w, pool_scale, g_q.reshape(1, Q_RANK), w_q, g_kv.reshape(1, KV_RANK), w_kv,
      ctab, stab)


def _norm_bwd_values(xv, gain, dy):
    r = lax.rsqrt(jnp.mean(xv * xv, axis=-1, keepdims=True) + RMS_EPS)
    n = xv * r
    dn = dy * gain
    return r * (dn - n * jnp.mean(dn * n, axis=-1, keepdims=True)), jnp.sum(dy * n, axis=0, keepdims=True)


def _even_back(dq_rot, dk_cat, dv, dmix, pooled, z, x, dxo, ctab, stab, w_q, w_kv, w_in, pool_w, pool_scale, g_q, g_kv,
               g_x, *, name):
    S = x.shape[0]
    ts = min(S, 512)
    nh = ts // HALO
    last = S // HALO - 1
    n = ts + HALO

    def body(dq_ref, dk_ref, dv_ref, dy_ref, dyh_ref, p_ref, z_ref, x_ref, dxo_ref, c_ref, s_ref, wq_ref, wkv_ref,
             win_ref, pw_ref, sc_ref, gq_ref, gkv_ref, gx_ref,
             dx_ref, dqp_ref, dz_ref, dyp_ref, dgq_ref, dgkv_ref, dsc_ref, dgx_ref):
        i = pl.program_id(0)

        @pl.when(i == 0)
        def _():
            for ref in (dgq_ref, dgkv_ref, dsc_ref, dgx_ref):
                ref[...] = jnp.zeros_like(ref)

        c, sn = c_ref[...], s_ref[...]
        z = z_ref[...]
        dk = dk_ref[...]
        for hd in range(MLA_HEADS):
            lo, hi = hd * HEAD_PAD, (hd + 1) * HEAD_PAD
            g = dq_ref[:, lo:hi]
            dqp_ref[:, lo:hi] = (g * c + _rope_partner(g * sn)).astype(dqp_ref.dtype)
            heads_sum = dk[:, lo:hi] if hd == 0 else heads_sum + dk[:, lo:hi]
        lane = lax.broadcasted_iota(jnp.int32, heads_sum.shape, 1)
        dkr = jnp.where((lane >= QK_NOPE) & (lane < QK_DIM), heads_sum * c + _rope_partner(heads_sum * sn), 0.0)
        dcqn = _dot(dqp_ref[...], wq_ref[...], NT)
        dckvn = _dot(dk.astype(BF16), wkv_ref[:, :D_MODEL], NT) + _dot(dv_ref[...].astype(BF16),
                                                                       wkv_ref[:, D_MODEL:], NT)
        dcq, dgq = _norm_bwd_values(z[:, POOL_DIM:POOL_DIM + Q_RANK], gq_ref[...], dcqn)
        dckv, dgkv = _norm_bwd_values(z[:, POOL_DIM + Q_RANK:POOL_DIM + Q_RANK + KV_RANK], gkv_ref[...], dckvn)
        dgq_ref[...] += dgq
        dgkv_ref[...] += dgkv
        dyv = dy_ref[...].astype(F32)
        dyh = jnp.where(i < pl.num_programs(0) - 1, dyh_ref[...].astype(F32), 0.0)
        dypre = (jnp.concatenate([dyv, dyh], axis=0) * sc_ref[...]).astype(BF16)
        dyp_ref[...] = dypre[:ts]
        cnts = _pool_counts(i, ts, n, 0)
        dsc = []
        for grp in range(4):
            lo, hi = grp * POOL_GROUP, (grp + 1) * POOL_GROUP
            dsc.append(jnp.sum(dyv[:, lo:hi] * _dot(p_ref[:, lo:hi], pw_ref[grp]), axis=0, keepdims=True))
            dpool = _dot(dypre[:, lo:hi], pw_ref[grp], NT)
            s = dpool / cnts[grp]
            for sh in (1, 2, 4, 8)[:grp + 1]:
                s = s + pltpu.roll(s, n - sh, 0)
            dz_ref[:, lo:hi] = (s[:ts] - dpool[:ts]).astype(dz_ref.dtype)
        dsc_ref[...] += jnp.concatenate(dsc, axis=1)
        dz_ref[:, POOL_DIM:POOL_DIM + Q_RANK] = dcq.astype(dz_ref.dtype)
        dz_ref[:, POOL_DIM + Q_RANK:POOL_DIM + Q_RANK + KV_RANK] = dckv.astype(dz_ref.dtype)
        dz_ref[:, D_MODEL - HEAD_PAD:] = dkr.astype(dz_ref.dtype)
        dx, dgx = _norm_bwd_values(x_ref[...], gx_ref[...], _dot(dz_ref[...], win_ref[...], NT))
        dx_ref[...] = dx + dxo_ref[...]
        dgx_ref[...] += dgx

    wide, pool = _rows(ts, D_MODEL), _rows(ts, POOL_DIM)
    f32 = lambda w: jax.ShapeDtypeStruct((1, w), F32)
    return pl.pallas_call(
        body, grid=(S // ts,),
        in_specs=[wide, wide, wide, pool,
                  pl.BlockSpec((HALO, POOL_DIM), lambda i: (jnp.minimum((i + 1) * nh, last), 0)), pool, wide, wide, wide,
                  _rows(ts, HEAD_PAD), _rows(ts, HEAD_PAD), _const((Q_RANK, D_MODEL)), _const((KV_RANK, 2 * D_MODEL)),
                  _const((D_MODEL, D_MODEL)), _const((4, POOL_GROUP, POOL_GROUP)), _const((1, POOL_DIM)),
                  _const((1, Q_RANK)), _const((1, KV_RANK)), _const((1, D_MODEL))],
        out_specs=[wide, wide, wide, pool, _const((1, Q_RANK)), _const((1, KV_RANK)), _const((1, POOL_DIM)),
                   _const((1, D_MODEL))],
        out_shape=[jax.ShapeDtypeStruct((S, D_MODEL), F32), jax.ShapeDtypeStruct((S, D_MODEL), BF16),
                   jax.ShapeDtypeStruct((S, D_MODEL), BF16), jax.ShapeDtypeStruct((S, POOL_DIM), BF16),
                   f32(Q_RANK), f32(KV_RANK), f32(POOL_DIM), f32(D_MODEL)],
        compiler_params=_cp(1), name=name,
    )(dq_rot, dk_cat, dv, dmix, dmix, pooled, z, x, dxo, ctab, stab, w_q, w_kv, w_in, pool_w, pool_scale,
      g_q.reshape(1, Q_RANK), g_kv.reshape(1, KV_RANK), g_x.reshape(1, D_MODEL))


def _rope_partner(t):
    lane = lax.broadcasted_iota(jnp.int32, t.shape, 1)
    swapped = jnp.where(lane < QK_NOPE + QK_ROPE // 2, pltpu.roll(t, HEAD_PAD - QK_ROPE // 2, 1),
                        pltpu.roll(t, QK_ROPE // 2, 1))
    return jnp.where((lane >= QK_NOPE) & (lane < QK_DIM), swapped, 0.0)


ATT_SCALE = QK_DIM ** -0.5
LOG2E = math.log2(math.e)


HEADS_PER_STEP = 2
ATT_COL0 = POOL_DIM // HEAD_PAD


FWD_TILE = 1024


def _stat_rows(col):
    return jnp.broadcast_to(col, (col.shape[0], LANES)).T[0:8]


def _retile_rows(rows, tq):
    heads, n8, t = rows.shape
    if t == tq:
        return rows
    flat = rows.reshape(heads, n8 // 8, 8, t)[:, :, 0].reshape(heads, -1, 1, tq)
    return jnp.broadcast_to(flat, (heads, flat.shape[1], 8, tq)).reshape(heads, -1, tq)


def _flash_fwd(q, k, v, mix, *, name):
    S = q.shape[0]
    tq = FWD_TILE if S % FWD_TILE == 0 else min(S, 512)
    nq = S // tq
    hs = HEADS_PER_STEP
    wide = hs * HEAD_PAD

    def body(q_ref, k_ref, v_ref, mix_ref, o_ref, lse_ref):
        qi = pl.program_id(1)
        qv = [q_ref[:, a * HEAD_PAD:(a + 1) * HEAD_PAD] for a in range(hs)]

        def update(m, acc, s, v):
            m_new = jnp.maximum(m, jnp.max(s, axis=-1, keepdims=True))
            p = jnp.exp2((s - m_new) * (ATT_SCALE * LOG2E))
            alpha = jnp.exp2((m - m_new) * (ATT_SCALE * LOG2E))
            return m_new, alpha * acc + _dot(p.astype(BF16), v)

        def step(j, carry, masked):
            off = pl.multiple_of(j * tq, tq)
            out = []
            for a in range(hs):
                head = slice(a * HEAD_PAD, (a + 1) * HEAD_PAD)
                s = _dot(qv[a], k_ref[pl.ds(off, tq), head], NT)
                if masked:
                    row = lax.broadcasted_iota(jnp.int32, (tq, tq), 0)
                    col = lax.broadcasted_iota(jnp.int32, (tq, tq), 1)
                    s = jnp.where(col <= row, s, NEG_INF)
                out.append(update(*carry[a], s, v_ref[pl.ds(off, tq), head]))
            return tuple(out)

        one = (jnp.full((tq, 1), NEG_INF, F32), jnp.zeros((tq, HEAD_PAD), F32))
        carry = step(qi, lax.fori_loop(0, qi, lambda j, c: step(j, c, False), (one,) * hs), True)
        for a in range(hs):
            m, acc = carry[a]
            l = acc[:, V_HEAD:V_HEAD + 1]
            o_ref[:, a * HEAD_PAD:(a + 1) * HEAD_PAD] = (acc / l).astype(o_ref.dtype)
            lse_ref[a] = _stat_rows(m * ATT_SCALE + jnp.log(l))

    blk = pl.BlockSpec((tq, wide), lambda h, i: (i, h))
    full = pl.BlockSpec((S, wide), lambda h, i: (0, h))
    return pl.pallas_call(
        body, grid=(MLA_HEADS // hs, nq), in_specs=[blk, full, full, ANY],
        out_specs=[pl.BlockSpec((tq, wide), lambda h, i: (i, ATT_COL0 // hs + h)),
                   pl.BlockSpec((hs, 8, tq), lambda h, i: (h, i, 0))],
        out_shape=[jax.ShapeDtypeStruct(mix.shape, mix.dtype), jax.ShapeDtypeStruct((MLA_HEADS, nq * 8, tq), F32)],
        input_output_aliases={3: 0}, compiler_params=_cp(2), name=name,
    )(q, k, v, mix)


BWD_TILE = 1024
BWD_HEADS_PER_STEP = 1


def _bwd_tile(S):
    return BWD_TILE if S % BWD_TILE == 0 else min(S, 512)


def _attn_delta(dmix, mix, *, name):
    S = mix.shape[0]
    ts = _bwd_tile(S)
    half = MLA_HEADS // 2
    halves = [_rows(ts, half * HEAD_PAD, 1), _rows(ts, half * HEAD_PAD, 2)]

    def body(do0_ref, do1_ref, o0_ref, o1_ref, d_ref):
        for n, (do_ref, o_ref) in enumerate(((do0_ref, o0_ref), (do1_ref, o1_ref))):
            prod = do_ref[...].astype(F32) * o_ref[...].astype(F32)
            for a in range(half):
                d_ref[n * half + a] = _stat_rows(
                    jnp.sum(prod[:, a * HEAD_PAD:(a + 1) * HEAD_PAD], axis=-1, keepdims=True))

    return pl.pallas_call(
        body, grid=(S // ts,), in_specs=halves + halves,
        out_specs=pl.BlockSpec((MLA_HEADS, 8, ts), lambda i: (0, i, 0)),
        out_shape=jax.ShapeDtypeStruct((MLA_HEADS, (S // ts) * 8, ts), F32), compiler_params=_cp(1), name=name,
    )(dmix, dmix, mix, mix)


def _flash_bwd(q, k, v, dmix, lse_rows, delta_rows, *, name):
    S = q.shape[0]
    tq = _bwd_tile(S)
    nq = S // tq
    hs = BWD_HEADS_PER_STEP
    wide = hs * HEAD_PAD

    def body(q_hbm, do_hbm, lse_ref, dl_ref, k_ref, v_ref, dq_hbm, dk_ref, dv_ref, q_all, do_all, dq_all):
        g, j = pl.program_id(0), pl.program_id(1)
        cols = pl.multiple_of(g * wide, wide)

        @pl.when(j == 0)
        def _():
            pltpu.sync_copy(q_hbm.at[:, pl.ds(cols, wide)], q_all)
            pltpu.sync_copy(do_hbm.at[:, pl.ds(POOL_DIM + cols, wide)], do_all)
            dq_all[...] = jnp.zeros_like(dq_all)

        heads = [slice(a * HEAD_PAD, (a + 1) * HEAD_PAD) for a in range(hs)]
        kv = [k_ref[:, a] for a in heads]
        vv = [v_ref[:, a] for a in heads]

        def block(a, keys, rows, lse2, dl, first_query):
            qv, dov = q_all[rows, heads[a]], do_all[rows, heads[a]]
            st = _dot(kv[a][:keys], qv, NT)
            if first_query is not None:
                krow = lax.broadcasted_iota(jnp.int32, st.shape, 0)
                qcol = lax.broadcasted_iota(jnp.int32, st.shape, 1) + first_query
                st = jnp.where(krow <= qcol, st, NEG_INF)
            pt = jnp.exp2(st * (ATT_SCALE * LOG2E) - lse2)
            dst = (pt * (_dot(vv[a][:keys], dov, NT) - dl)).astype(BF16)
            dq_all[rows, heads[a]] += _dot(dst, kv[a][:keys], TN)
            return _dot(dst, qv), _dot(pt.astype(BF16), dov)

        def stats(a, i):
            off8 = pl.multiple_of(i * 8, 8)
            return lse_ref[a, pl.ds(off8, 8), :][0:1] * LOG2E, dl_ref[a, pl.ds(off8, 8), :][0:1]

        def step(i, carry):
            rows = pl.ds(pl.multiple_of(i * tq, tq), tq)
            out = []
            for a in range(hs):
                dk, dv = block(a, tq, rows, *stats(a, i), None)
                out.append((carry[a][0] + dk, carry[a][1] + dv))
            return tuple(out)

        def diagonal():
            half = tq // 2
            out = []
            for a in range(hs):
                lse2, dl = stats(a, j)
                off = pl.multiple_of(j * tq, tq)
                dk0, dv0 = block(a, half, pl.ds(off, half), lse2[:, :half], dl[:, :half], 0)
                dk1, dv1 = block(a, tq, pl.ds(pl.multiple_of(off + half, half), half), lse2[:, half:], dl[:, half:], half)
                zero = jnp.zeros((tq - half, HEAD_PAD), F32)
                out.append((dk1 + jnp.concatenate([dk0, zero], axis=0), dv1 + jnp.concatenate([dv0, zero], axis=0)))
            return tuple(out)

        carry = lax.fori_loop(j + 1, nq, step, diagonal())
        for a in range(hs):
            dk_ref[:, heads[a]] = carry[a][0] * ATT_SCALE
            dv_ref[:, heads[a]] = carry[a][1]

        @pl.when(j == nq - 1)
        def _():
            dq_all[...] = dq_all[...] * ATT_SCALE
            pltpu.sync_copy(dq_all, dq_hbm.at[:, pl.ds(cols, wide)])

    blk = pl.BlockSpec((tq, wide), lambda g, j: (j, g))
    stat = pl.BlockSpec((hs, nq * 8, tq), lambda g, j: (g, 0, 0))
    full = jax.ShapeDtypeStruct((S, MLA_HEADS * HEAD_PAD), F32)
    return pl.pallas_call(
        body, grid=(MLA_HEADS // hs, nq), in_specs=[ANY, ANY, stat, stat, blk, blk], out_specs=[ANY, blk, blk],
        out_shape=[full, full, full],
        scratch_shapes=[pltpu.VMEM((S, wide), BF16), pltpu.VMEM((S, wide), BF16), pltpu.VMEM((S, wide), F32)],
        compiler_params=_cp(2), name=name,
    )(q, dmix, lse_rows, delta_rows, k, v)


MEM_SCALE = MEM_HEAD_DIM ** -0.5


def _xattn_probs(qh, kh):
    s = _dot(qh, kh, NT) * MEM_SCALE
    e = jnp.exp(s - jnp.max(s, axis=-1, keepdims=True))
    return e / jnp.sum(e, axis=-1, keepdims=True)


def _xa_block_fwd(x, kvm, w_q, w_o, g, *, name):
    S = x.shape[0]
    ts = min(S, 512)
    nm = kvm.shape[0]

    def body(x_ref, kv_ref, wq_ref, wo_ref, g_ref, xo_ref, hx_ref, q_ref, o_ref):
        xv = x_ref[...]
        r = lax.rsqrt(jnp.mean(xv * xv, axis=-1, keepdims=True) + RMS_EPS)
        hx = (xv * r * g_ref[...]).astype(BF16)
        hx_ref[...] = hx
        q = _dot(hx, wq_ref[...]).astype(BF16)
        q_ref[...] = q
        for h in range(MEM_HEADS):
            lo, hi = h * MEM_HEAD_DIM, (h + 1) * MEM_HEAD_DIM
            p = _xattn_probs(q[:, lo:hi], kv_ref[:, lo:hi])
            o_ref[:, lo:hi] = _dot(p.astype(BF16), kv_ref[:, D_MODEL + lo:D_MODEL + hi]).astype(o_ref.dtype)
        xo_ref[...] = xv + _dot(o_ref[...], wo_ref[...])

    square = _const((D_MODEL, D_MODEL))
    act = jax.ShapeDtypeStruct((S, D_MODEL), BF16)
    return pl.pallas_call(
        body, grid=(S // ts,),
        in_specs=[_rows(ts, D_MODEL), _const((nm, 2 * D_MODEL)), square, square, _const((1, D_MODEL))],
        out_specs=[_rows(ts, D_MODEL)] * 4, out_shape=[jax.ShapeDtypeStruct((S, D_MODEL), F32), act, act, act],
        compiler_params=_cp(1), name=name,
    )(x, kvm, w_q, w_o, g.reshape(1, D_MODEL))


def _xa_block_bwd(dxo, x, q, kvm, w_q, w_o, g, *, name):
    S = q.shape[0]
    ts = min(S, 512)
    nm = kvm.shape[0]

    def body(dxo_ref, x_ref, q_ref, kv_ref, wq_ref, wo_ref, g_ref, dx_ref, dq_ref, dkv_ref, dg_ref):
        @pl.when(pl.program_id(0) == 0)
        def _():
            dkv_ref[...] = jnp.zeros_like(dkv_ref)
            dg_ref[...] = jnp.zeros_like(dg_ref)

        dxo = dxo_ref[...]
        do = _dot(dxo.astype(BF16), wo_ref[...], NT).astype(BF16)
        for h in range(MEM_HEADS):
            lo, hi = h * MEM_HEAD_DIM, (h + 1) * MEM_HEAD_DIM
            qh, kh, vh = q_ref[:, lo:hi], kv_ref[:, lo:hi], kv_ref[:, D_MODEL + lo:D_MODEL + hi]
            doh = do[:, lo:hi]
            p = _xattn_probs(qh, kh)
            dp = _dot(doh, vh, NT)
            ds = (p * (dp - jnp.sum(dp * p, axis=-1, keepdims=True)) * MEM_SCALE).astype(BF16)
            dq_ref[:, lo:hi] = _dot(ds, kh).astype(dq_ref.dtype)
            dkv_ref[:, lo:hi] += _dot(ds, qh, TN)
            dkv_ref[:, D_MODEL + lo:D_MODEL + hi] += _dot(p.astype(BF16), doh, TN)
        dx, dg = _norm_bwd_epilogue(0)([_dot(dq_ref[...], wq_ref[...], NT)], [x_ref[...], dxo, g_ref[...]])
        dx_ref[...] = dx
        dg_ref[...] += dg

    square = _const((D_MODEL, D_MODEL))
    return pl.pallas_call(
        body, grid=(S // ts,),
        in_specs=[_rows(ts, D_MODEL), _rows(ts, D_MODEL), _rows(ts, D_MODEL), _const((nm, 2 * D_MODEL)), square,
                  square, _const((1, D_MODEL))],
        out_specs=[_rows(ts, D_MODEL), _rows(ts, D_MODEL), _const((nm, 2 * D_MODEL)), _const((1, D_MODEL))],
        out_shape=[jax.ShapeDtypeStruct((S, D_MODEL), F32), jax.ShapeDtypeStruct((S, D_MODEL), BF16),
                   jax.ShapeDtypeStruct((nm, 2 * D_MODEL), F32), jax.ShapeDtypeStruct((1, D_MODEL), F32)],
        compiler_params=_cp(1), name=name,
    )(dxo, x, q, kvm, w_q, w_o, g.reshape(1, D_MODEL))


CONV_HALO = 8


def _sigmoid(x):
    return 0.5 * jnp.tanh(0.5 * x) + 0.5


def _softplus(x):
    return jnp.maximum(x, 0.0) + jnp.log(1.0 + jnp.exp(-jnp.abs(x)))


def _neg_expm1(x):
    series = -x * (1.0 + x * (1.0 / 2) * (1.0 + x * (1.0 / 3) * (1.0 + x * (1.0 / 4) * (1.0 + x * (1.0 / 5)))))
    return jnp.where(x > -0.05, series, 1.0 - jnp.exp(x))


GELU_C = math.sqrt(2.0 / math.pi)


def _gelu(x):
    return 0.5 * x * (1.0 + jnp.tanh(GELU_C * (x + 0.044715 * x * x * x)))


def _gelu_grad(x):
    t = jnp.tanh(GELU_C * (x + 0.044715 * x * x * x))
    return 0.5 * (1.0 + t) + 0.5 * x * (1.0 - t * t) * GELU_C * (1.0 + 3 * 0.044715 * x * x)


def _lru_gates(xc, wr_ref, br, wi_ref, bi, sp, reset):
    xcb = xc.astype(BF16)
    pr, pi = [], []
    for h in range(LRU_HEADS):
        lo, hi = h * LRU_HEAD_DIM, (h + 1) * LRU_HEAD_DIM
        pr.append(_dot(xcb[:, lo:hi], wr_ref[h]))
        pi.append(_dot(xcb[:, lo:hi], wi_ref[h]))
    r = _sigmoid(jnp.concatenate(pr, axis=1) + br)
    ig = _sigmoid(jnp.concatenate(pi, axis=1) + bi)
    log_a = -LRU_C * r * sp
    a = jnp.where(reset, 0.0, jnp.exp(log_a))
    mult = jnp.where(reset, 1.0, jnp.sqrt(jnp.maximum(_neg_expm1(2.0 * log_a), 0.0)))
    return r, ig, a, mult


SUBLANES = 8


def _compose_groups(a, b, reverse):
    n = a.shape[0]
    row = lax.broadcasted_iota(jnp.int32, a.shape, 0) % SUBLANES
    for s in (1, 2, 4):
        inside = (row < SUBLANES - s) if reverse else (row >= s)
        shift = n - s if reverse else s
        a_s = jnp.where(inside, pltpu.roll(a, shift, 0), 1.0)
        b_s = jnp.where(inside, pltpu.roll(b, shift, 0), 0.0)
        b = a * b_s + b
        a = a * a_s
    return a, b


def _chain_groups(a_buf, h_ref, state, reverse):
    groups = a_buf.shape[0] // SUBLANES

    def group(g, h_in):
        off = pl.multiple_of((groups - 1 - g if reverse else g) * SUBLANES, SUBLANES)
        h = a_buf[pl.ds(off, SUBLANES), :] * h_in + h_ref[pl.ds(off, SUBLANES), :]
        h_ref[pl.ds(off, SUBLANES), :] = h
        return jnp.broadcast_to(h[0:1] if reverse else h[SUBLANES - 1:SUBLANES], h.shape)

    return lax.fori_loop(0, groups, group, state, unroll=4)[0:1]


def _lru_fwd(x, g, w_in, reset, conv_w, conv_b, w_r, b_r, w_i, b_i, lam, *, name):
    S = x.shape[0]
    ts = min(S, 512)
    W = D_MODEL

    def body(x_ref, g_ref, win_ref, rs_ref, cw_ref, cb_ref, wr_ref, br_ref, wi_ref, bi_ref, lam_ref,
             hn_ref, z_ref, xc_ref, h_ref, y_ref, a_buf, carry, tail):
        i = pl.program_id(0)

        @pl.when(i == 0)
        def _():
            carry[...] = jnp.zeros_like(carry)
            tail[...] = jnp.zeros_like(tail)

        xv = x_ref[...]
        hn = (xv * lax.rsqrt(jnp.mean(xv * xv, axis=-1, keepdims=True) + RMS_EPS) * g_ref[...]).astype(BF16)
        hn_ref[...] = hn
        z_ref[...] = _dot(hn, win_ref[...])
        xb = z_ref[:, W:]
        xe = jnp.concatenate([tail[...], xb], axis=0)
        tail[...] = xb[ts - CONV_HALO:]
        xc = cb_ref[...] + cw_ref[3:4, :] * xe[CONV_HALO:]
        for kk in range(CONV_WIDTH - 1):
            xc = xc + cw_ref[kk:kk + 1, :] * pltpu.roll(xe, CONV_WIDTH - 1 - kk, 0)[CONV_HALO:]
        xc_ref[...] = xc
        reset = rs_ref[...] > 0.5
        _, ig, a, mult = _lru_gates(xc, wr_ref, br_ref[...], wi_ref, bi_ref[...], _softplus(-lam_ref[...]), reset)
        a_buf[...], h_ref[...] = _compose_groups(a, mult * (ig * xc), False)
        carry[...] = _chain_groups(a_buf, h_ref, jnp.broadcast_to(carry[...], (SUBLANES, W)), False)
        y_ref[...] = (_gelu(z_ref[:, :W]) * h_ref[...]).astype(y_ref.dtype)

    vec = _const((1, W))
    gw = _const((LRU_HEADS, LRU_HEAD_DIM, LRU_HEAD_DIM))
    return pl.pallas_call(
        body, grid=(S // ts,),
        in_specs=[_rows(ts, W), vec, _const((W, 2 * W)), _rows(ts, 1), _const((CONV_WIDTH, W)), vec, gw, vec, gw, vec,
                  vec],
        out_specs=[_rows(ts, W), _rows(ts, 2 * W), _rows(ts, W), _rows(ts, W), _rows(ts, W)],
        out_shape=[jax.ShapeDtypeStruct((S, W), BF16), jax.ShapeDtypeStruct((S, 2 * W), F32),
                   jax.ShapeDtypeStruct((S, W), F32), jax.ShapeDtypeStruct((S, W), F32),
                   jax.ShapeDtypeStruct((S, W), BF16)],
        scratch_shapes=[pltpu.VMEM((ts, W), F32), pltpu.VMEM((1, W), F32), pltpu.VMEM((CONV_HALO, W), F32)],
        compiler_params=_cp(1), name=name,
    )(x, g.reshape(1, W), w_in, reset, conv_w, conv_b, w_r, b_r, w_i, b_i, lam)


def _lru_bwd(dxo, w_out, z, xc, hseq, reset, w_r, b_r, w_i, b_i, lam, *, name):
    S = z.shape[0]
    ts = min(S, 512)
    nt = S // ts
    nh = ts // CONV_HALO
    W = D_MODEL

    def body(dxo_ref, wout_ref, gate_ref, xc_ref, h_ref, hh_ref, rs_ref, wr_ref, br_ref, wi_ref, bi_ref, lam_ref,
             dg_ref, dxc_ref, dpr_ref, dpi_ref, acc_ref, a_buf, dh_buf, carry):
        i = pl.program_id(0)
        tile = nt - 1 - i

        @pl.when(i == 0)
        def _():
            carry[...] = jnp.zeros_like(carry)
            acc_ref[...] = jnp.zeros_like(acc_ref)

        xc = xc_ref[...]
        lam_v = lam_ref[...]
        sp = _softplus(-lam_v)
        reset = rs_ref[...] > 0.5
        r, ig, a, mult = _lru_gates(xc, wr_ref, br_ref[...], wi_ref, bi_ref[...], sp, reset)
        gate = gate_ref[...]
        dyv = _dot(dxo_ref[...].astype(BF16), wout_ref[...], NT)
        h = h_ref[...]
        dg_ref[...] = (dyv * h * _gelu_grad(gate)).astype(dg_ref.dtype)
        last_row = lax.broadcasted_iota(jnp.int32, a.shape, 0) == ts - 1
        a_buf[...], dh_buf[...] = _compose_groups(jnp.where(last_row, 1.0, pltpu.roll(a, ts - 1, 0)),
                                                  dyv * _gelu(gate), True)
        _chain_groups(a_buf, dh_buf, jnp.broadcast_to(carry[...], (SUBLANES, W)), True)
        dh = dh_buf[...]
        carry[...] = a[0:1] * dh[0:1]
        hh = jnp.where(tile > 0, hh_ref[...], 0.0)
        h_prev = pltpu.roll(jnp.concatenate([hh, h], axis=0), 1, 0)[CONV_HALO:]
        da = dh * h_prev
        bx = ig * xc
        dmult = dh * bx
        dbx = dh * mult
        di = dbx * xc
        dlog_a = jnp.where(reset, 0.0, da * a - dmult * a * a / jnp.maximum(mult, 1e-30))
        dr = dlog_a * (-LRU_C) * sp
        dpre_r = dr * r * (1.0 - r)
        dpre_i = di * ig * (1.0 - ig)
        dprb, dpib = dpre_r.astype(BF16), dpre_i.astype(BF16)
        dpr_ref[...] = dprb
        dpi_ref[...] = dpib
        back = []
        for hd in range(LRU_HEADS):
            lo, hi = hd * LRU_HEAD_DIM, (hd + 1) * LRU_HEAD_DIM
            back.append(_dot(dprb[:, lo:hi], wr_ref[hd], NT) + _dot(dpib[:, lo:hi], wi_ref[hd], NT))
        dxc_ref[...] = dbx * ig + jnp.concatenate(back, axis=1)
        dlam = jnp.sum(dlog_a * (-LRU_C) * r, axis=0, keepdims=True) * (-_sigmoid(-lam_v))
        acc_ref[0:1, :] += jnp.sum(dpre_r, axis=0, keepdims=True)
        acc_ref[1:2, :] += jnp.sum(dpre_i, axis=0, keepdims=True)
        acc_ref[2:3, :] += dlam

    rev = lambda cb: pl.BlockSpec((ts, W), lambda i: (nt - 1 - i, cb))
    vec = _const((1, W))
    gw = _const((LRU_HEADS, LRU_HEAD_DIM, LRU_HEAD_DIM))
    return pl.pallas_call(
        body, grid=(nt,),
        in_specs=[rev(0), _const((W, W)), rev(0), rev(0), rev(0),
                  pl.BlockSpec((CONV_HALO, W), lambda i: (jnp.maximum((nt - 1 - i) * nh - 1, 0), 0)),
                  pl.BlockSpec((ts, 1), lambda i: (nt - 1 - i, 0)), gw, vec, gw, vec, vec],
        out_specs=[rev(0), rev(0), rev(0), rev(0), _const((8, W))],
        out_shape=[jax.ShapeDtypeStruct((S, W), BF16), jax.ShapeDtypeStruct((S, W), F32),
                   jax.ShapeDtypeStruct((S, W), BF16), jax.ShapeDtypeStruct((S, W), BF16),
                   jax.ShapeDtypeStruct((8, W), F32)],
        scratch_shapes=[pltpu.VMEM((ts, W), F32), pltpu.VMEM((ts, W), F32), pltpu.VMEM((1, W), F32)],
        compiler_params=_cp(1), name=name,
    )(dxo, w_out, z, xc, hseq, hseq, reset, w_r, b_r, w_i, b_i, lam)


def _conv_bwd(dxc, z, conv_w, *, name):
    S = dxc.shape[0]
    ts = min(S, 512)
    nh = ts // CONV_HALO
    last = S // CONV_HALO - 1
    W = D_MODEL
    n = ts + CONV_HALO

    def body(d_ref, dn_ref, xb_ref, xp_ref, cw_ref, dxb_ref, acc_ref):
        i = pl.program_id(0)

        @pl.when(i == 0)
        def _():
            acc_ref[...] = jnp.zeros_like(acc_ref)

        d = d_ref[...]
        de = jnp.concatenate([d, jnp.where(i < pl.num_programs(0) - 1, dn_ref[...], 0.0)], axis=0)
        xe = jnp.concatenate([jnp.where(i > 0, xp_ref[...], 0.0), xb_ref[...]], axis=0)
        dxb = cw_ref[3:4, :] * d
        acc_ref[3:4, :] += jnp.sum(d * xe[CONV_HALO:], axis=0, keepdims=True)
        for kk in range(CONV_WIDTH - 1):
            sh = CONV_WIDTH - 1 - kk
            dxb = dxb + cw_ref[kk:kk + 1, :] * pltpu.roll(de, n - sh, 0)[:ts]
            acc_ref[kk:kk + 1, :] += jnp.sum(d * pltpu.roll(xe, sh, 0)[CONV_HALO:], axis=0, keepdims=True)
        dxb_ref[...] = dxb.astype(dxb_ref.dtype)
        acc_ref[4:5, :] += jnp.sum(d, axis=0, keepdims=True)

    return pl.pallas_call(
        body, grid=(S // ts,),
        in_specs=[_rows(ts, W), pl.BlockSpec((CONV_HALO, W), lambda i: (jnp.minimum((i + 1) * nh, last), 0)),
                  _rows(ts, W, 1), pl.BlockSpec((CONV_HALO, W), lambda i: (jnp.maximum(i * nh - 1, 0), 1)),
                  _const((CONV_WIDTH, W))],
        out_specs=[_rows(ts, W), _const((8, W))],
        out_shape=[jax.ShapeDtypeStruct((S, W), BF16), jax.ShapeDtypeStruct((8, W), F32)],
        compiler_params=_cp(1), name=name,
    )(dxc, dxc, z, z, conv_w)


def _loss_head(x, g, target, *, name):
    S, D = x.shape
    ts = _row_tile(S)

    def body(x_ref, g_ref, t_ref, dx_ref, dg_ref, l_ref):
        @pl.when(pl.program_id(0) == 0)
        def _():
            dg_ref[...] = jnp.zeros_like(dg_ref)
            l_ref[...] = jnp.zeros_like(l_ref)

        xv = x_ref[...]
        r = lax.rsqrt(jnp.mean(xv * xv, axis=-1, keepdims=True) + RMS_EPS)
        n = xv * r
        err = n * g_ref[...] - t_ref[...]
        l_ref[...] += 0.5 * jnp.sum(jnp.sum(err * err, axis=-1, keepdims=True) * (1.0 / D), axis=0, keepdims=True)
        dy = err * (1.0 / D)
        dn = dy * g_ref[...]
        dx_ref[...] = r * (dn - n * jnp.mean(dn * n, axis=-1, keepdims=True))
        dg_ref[...] += jnp.sum(dy * n, axis=0, keepdims=True)

    return pl.pallas_call(
        body, grid=(S // ts,), in_specs=[_rows(ts, D), _const((1, D)), _rows(ts, D)],
        out_specs=[_rows(ts, D), _const((1, D)), _const((8, LANES))],
        out_shape=[jax.ShapeDtypeStruct((S, D), F32), jax.ShapeDtypeStruct((1, D), F32),
                   jax.ShapeDtypeStruct((8, LANES), F32)],
        compiler_params=_cp(1), name=name,
    )(x, g.reshape(1, D), target)


def _adamw(w, ga, gb, m, v, *, name):
    shape = w.shape
    cols = shape[-1]
    rows = w.size // cols
    br = rows
    if rows * cols * 4 > (1 << 20):
        br = max(d for d in range(8, rows + 1, 8) if rows % d == 0 and d * cols * 4 <= (1 << 20))

    def body(w_ref, ga_ref, gb_ref, m_ref, v_ref, g_ref, d_ref, mo_ref, vo_ref):
        gv = ga_ref[...] + gb_ref[...]
        g_ref[...] = gv
        mn = ADAM_B1 * m_ref[...] + (1.0 - ADAM_B1) * gv
        vn = ADAM_B2 * v_ref[...] + (1.0 - ADAM_B2) * (gv * gv)
        m_hat = mn / (1.0 - ADAM_B1 ** ADAM_STEP)
        v_hat = vn / (1.0 - ADAM_B2 ** ADAM_STEP)
        d_ref[...] = -ADAM_LR * (m_hat / (jnp.sqrt(v_hat) + ADAM_EPS) + ADAM_WD * w_ref[...])
        mo_ref[...] = mn
        vo_ref[...] = vn

    spec = _rows(br, cols)
    outs = pl.pallas_call(
        body, grid=(rows // br,), in_specs=[spec] * 5, out_specs=[spec] * 4,
        out_shape=[jax.ShapeDtypeStruct((rows, cols), F32)] * 4, compiler_params=_cp(1), name=name,
    )(*[t.reshape(rows, cols) for t in (w, ga, gb, m, v)])
    return [o.reshape(shape) for o in outs]


def _pad_heads(w, width):
    k = w.shape[0]
    return jnp.pad(w.reshape(k, MLA_HEADS, width), ((0, 0), (0, 0), (0, HEAD_PAD - width))).reshape(k, -1)


def _unpad_heads(w, width):
    k = w.shape[0]
    return w.reshape(k, MLA_HEADS, HEAD_PAD)[:, :, :width].reshape(k, MLA_HEADS * width)


def _rope_tables(positions):
    inv_freq = ROPE_BASE ** (-jnp.arange(0, QK_ROPE, 2, dtype=F32) / QK_ROPE)
    none = jnp.zeros((QK_NOPE,), F32)
    freq = jnp.concatenate([none, inv_freq, inv_freq, none[:HEAD_PAD - QK_DIM]])
    sign = jnp.concatenate([none, -jnp.ones_like(inv_freq), jnp.ones_like(inv_freq), none[:HEAD_PAD - QK_DIM]])
    ang = positions.astype(F32)[:, None] * freq
    return jnp.cos(ang), jnp.sin(ang) * sign


def _memory_block(x, mem, W, layer, tag):
    mn = _rms(mem, W["xa_norm_mem"][layer], name=f"{tag}_xa_norm_mem")
    kvm = _mm(mn, [(W["xa_w_kv"][layer], 0, 0)], _first, [(2 * D_MODEL, BF16, 0)], tn=2 * D_MODEL, nj=1,
              name=f"{tag}_xa_kv")[0]
    xo, hx, qx, o = _xa_block_fwd(x, kvm, W["xa_w_q"][layer], W["xa_w_o"][layer], W["xa_norm_x"][layer],
                                  name=f"{tag}_xa_fwd")
    return xo, (x, hx, qx, mn, kvm, o)


def _memory_block_bwd(dxo, mem, W, layer, saved, tag, grads):
    x, hx, qx, mn, kvm, o = saved
    wq, wkv, wo = W["xa_w_q"][layer], W["xa_w_kv"][layer], W["xa_w_o"][layer]
    grads["xa_w_o"][layer] = _owner_major(_mm_tn(o, dxo, name=f"{tag}_xa_dwo"), 0)
    dx, dqx, dkvm, dg = _xa_block_bwd(dxo, x, qx, kvm, wq, wo, W["xa_norm_x"][layer], name=f"{tag}_xa_bwd")
    grads["xa_w_q"][layer] = _owner_major(_mm_tn(hx, dqx, name=f"{tag}_xa_dwq"), 0)
    grads["xa_norm_x"][layer] = dg[0]
    dmn = _mm(dkvm, [(wkv, 0, 0)], _first, [(D_MODEL, F32, 0)], nt=True, tn=D_MODEL, nj=1, name=f"{tag}_xa_dmn")[0]
    grads["xa_w_kv"][layer] = _mm_tn_owners(mn, [dkvm], name=f"{tag}_xa_dwkv")
    grads["xa_norm_mem"][layer] = _norm_gain_grad(mem, dmn, name=f"{tag}_xa_norm_mem_bwd")[0]
    return dx


FF_TN = D_FF // 2

def _silu_mul(accs, extras):
    g, u = accs
    return [g * _sigmoid(g) * u, g, u]


def _silu_mul_bwd(accs, extras):
    da = accs[0]
    g, u = extras[0].astype(F32), extras[1].astype(F32)
    sg = _sigmoid(g)
    silu = g * sg
    return [da * u * (sg + silu * (1.0 - sg)), da * silu]


def _ffn_block(x, W, layer, tag):
    hf = _rms(x, W["ffn_norm"][layer], name=f"{tag}_ffn_norm")
    wgu, wd = W["ffn_w_gate_up"][layer], W["ffn_w_down"][layer]
    act, g, u = _mm(hf, [(wgu, 0, 0), (wgu, 0, 2)], _silu_mul, [(D_FF, BF16, 0)] * 3, tn=FF_TN, nj=2,
                    name=f"{tag}_ffn_up")
    xo = _mm(act, [(wd, 0, 0)], _add_res, [(D_MODEL, F32, 0)], extras=[(x, 0)], tn=D_MODEL, nj=1,
             name=f"{tag}_ffn_down")[0]
    return xo, (x, hf, act, g, u)


def _ffn_block_bwd(dxo, W, layer, saved, tag, grads):
    x, hf, act, g, u = saved
    wgu, wd = W["ffn_w_gate_up"][layer], W["ffn_w_down"][layer]
    dg, du = _mm(dxo, [(wd, 0, 0)], _silu_mul_bwd, [(D_FF, BF16, 0)] * 2, nt=True, extras=[(g, 0), (u, 0)], tn=FF_TN,
                 nj=2, name=f"{tag}_ffn_dact")
    grads["ffn_w_down"][layer] = _owner_major(_mm_tn(act, dxo, tk=FF_TN, name=f"{tag}_ffn_dwd"), 0)
    dx, dgn = _mm(dg, [(wgu, 0, 0)], _norm_bwd_epilogue(0), [(D_MODEL, F32, 0)], nt=True, also=(du, (wgu, 0, 1)),
                  extras=[(x, 0), (dxo, 0)], rows=[W["ffn_norm"][layer].reshape(1, D_MODEL)],
                  sums=[D_MODEL], tn=D_MODEL, nj=1, name=f"{tag}_ffn_dhf")
    grads["ffn_w_gate_up"][layer] = _mm_tn_owners(hf, [dg, du], name=f"{tag}_ffn_dwgu")
    grads["ffn_norm"][layer] = dgn[0]
    return dx


def _even_block(x, tabs, W, tag):
    ctab, stab = tabs
    w_in = W["ev_w_in"][0]
    zero = jnp.zeros((D_MODEL, QK_NOPE), BF16)
    w_in_pad = jnp.concatenate([w_in[:, :896], zero, w_in[:, 896:], zero[:, :HEAD_PAD - QK_DIM]], axis=1)
    w_q_pad = _pad_heads(W["ev_w_q_up"][0], QK_DIM)
    wkv = W["ev_w_kv_up"][0].reshape(KV_RANK, MLA_HEADS, QK_NOPE + V_HEAD)
    w_kv_pad = jnp.concatenate([_pad_heads(wkv[:, :, :QK_NOPE].reshape(KV_RANK, -1), QK_NOPE),
                                _pad_heads(wkv[:, :, QK_NOPE:].reshape(KV_RANK, -1), V_HEAD)], axis=1)
    w_out = W["ev_w_out"][0]
    w_att = jnp.pad(w_out[POOL_DIM:].reshape(MLA_HEADS, V_HEAD, D_MODEL), ((0, 0), (0, HEAD_PAD - V_HEAD), (0, 0)))
    w_out_pad = jnp.concatenate([w_out[:POOL_DIM], w_att.reshape(MLA_HEADS * HEAD_PAD, D_MODEL)], axis=0)
    pool_w = W["ev_pool_w"][0].astype(BF16)
    pool_scale = W["ev_pool_scale"]

    h, z, mix, pooled, cqn, ckvn, q_rot, k_cat, v_pad = _even_front(
        x, W["ev_norm"][0], w_in_pad, pool_w, pool_scale, W["ev_q_norm"][0], w_q_pad, W["ev_kv_norm"][0], w_kv_pad,
        ctab, stab, name=f"{tag}_front")
    mix, lse = _flash_fwd(q_rot, k_cat, v_pad, mix, name=f"{tag}_attn")
    xo = _mm(mix, [(w_out_pad, 0, 0)], _add_res, [(D_MODEL, F32, 0)], extras=[(x, 0)], tn=D_MODEL, nj=1,
             name=f"{tag}_out")[0]
    saved = (x, h, z, pooled, cqn, ckvn, q_rot, k_cat, v_pad, lse, mix,
             (w_in_pad, w_q_pad, w_kv_pad, w_out_pad, pool_w, pool_scale))
    return xo, saved


def _even_out_grad(dxo, saved, tag):
    mix = saved[10]
    dw_out_pad = _mm_tn(mix, dxo, tk=MIX_DIM // 3, name=f"{tag}_dw_out")
    datt = dw_out_pad[POOL_DIM:].reshape(MLA_HEADS, HEAD_PAD, D_MODEL)[:, :V_HEAD].reshape(-1, D_MODEL)
    return [_owner_major(jnp.concatenate([dw_out_pad[:POOL_DIM], datt], axis=0), 0)]


def _even_block_bwd(dxo, tabs, W, saved, tag, grads, token=None):
    ctab, stab = tabs
    x, h, z, pooled, cqn, ckvn, q_rot, k_cat, v_pad, lse, mix, wts = saved
    w_in_pad, w_q_pad, w_kv_pad, w_out_pad, pool_w, pool_scale = wts
    if token is not None:
        w_out_pad = w_out_pad + token[0:1, 0:1].astype(BF16)
    dmix = _mm(dxo, [(w_out_pad, 0, 0)], _first, [(MIX_DIM, BF16, 0)], nt=True, tn=MIX_DIM, nj=1,
               name=f"{tag}_dmix")[0]
    delta = _attn_delta(dmix, mix, name=f"{tag}_delta")
    dq_rot, dk_cat, dv_pad = _flash_bwd(q_rot, k_cat, v_pad, dmix, _retile_rows(lse, delta.shape[2]), delta,
                                        name=f"{tag}_attn_bwd")
    dx, dq_pad, dz, dypre, dgq, dgkv, dscale, dgn = _even_back(
        dq_rot, dk_cat, dv_pad, dmix, pooled, z, x, dxo, ctab, stab, w_q_pad, w_kv_pad, w_in_pad, pool_w, pool_scale,
        W["ev_q_norm"][0], W["ev_kv_norm"][0], W["ev_norm"][0], name=f"{tag}_back")
    grads["ev_q_norm"], grads["ev_kv_norm"], grads["ev_pool_scale"], grads["ev_norm"] = dgq, dgkv, dscale, dgn
    dw_q_pad = _mm_tn(cqn, dq_pad, name=f"{tag}_dw_q_up")
    grads["ev_w_q_up"] = [_owner_major(_unpad_heads(dw_q_pad, QK_DIM), 1)]
    dwk = _unpad_heads(_mm_tn(ckvn, dk_cat, name=f"{tag}_dw_k_up"), QK_NOPE).reshape(KV_RANK, MLA_HEADS, QK_NOPE)
    dwv = _unpad_heads(_mm_tn(ckvn, dv_pad, name=f"{tag}_dw_v_up"), V_HEAD).reshape(KV_RANK, MLA_HEADS, V_HEAD)
    grads["ev_w_kv_up"] = [_owner_major(jnp.concatenate([dwk, dwv], axis=2).reshape(KV_RANK, -1), 1)]
    grads["ev_pool_w"] = _mm_tn_grouped(pooled, dypre, 4, POOL_GROUP, name=f"{tag}_dpool_w")[None]
    dw_in_pad = _mm_tn(h, dz, name=f"{tag}_dw_in")
    grads["ev_w_in"] = [_owner_major(jnp.concatenate([dw_in_pad[:, :896], dw_in_pad[:, 960:992]], axis=1), 0)]
    return dx


def _odd_block(x, reset, W, tag):
    w_r, w_i = W["od_w_rgate"][0], W["od_w_igate"][0]
    vecs = [W[n].reshape(1, D_MODEL) for n in ("od_conv_b", "od_b_rgate", "od_b_igate", "od_lambda")]
    h, z, xc, hseq, y = _lru_fwd(x, W["od_norm"][0], W["od_w_in"][0], reset, W["od_conv_w"][0], vecs[0], w_r,
                                 vecs[1], w_i, vecs[2], vecs[3], name=f"{tag}_lru")
    xo = _mm(y, [(W["od_w_out"][0], 0, 0)], _add_res, [(D_MODEL, F32, 0)], extras=[(x, 0)], tn=D_MODEL, nj=1,
             name=f"{tag}_out")[0]
    return xo, (x, h, z, xc, hseq, y, vecs)


def _odd_block_bwd(dxo, reset, W, saved, tag, grads):
    x, h, z, xc, hseq, y, vecs = saved
    w_r, w_i = W["od_w_rgate"][0], W["od_w_igate"][0]
    grads["od_w_out"] = [_owner_major(_mm_tn(y, dxo, name=f"{tag}_dw_out"), 0)]
    dgate, dxc, dpr, dpi, acc = _lru_bwd(dxo, W["od_w_out"][0], z, xc, hseq, reset, w_r, vecs[1], w_i, vecs[2],
                                         vecs[3], name=f"{tag}_lru_bwd")
    grads["od_b_rgate"], grads["od_b_igate"], grads["od_lambda"] = acc[0:1], acc[1:2], acc[2:3]
    grads["od_w_rgate"] = [_owner_major(_mm_tn_grouped(xc, dpr, LRU_HEADS, LRU_HEAD_DIM, name=f"{tag}_dw_rgate"), 1)]
    grads["od_w_igate"] = [_owner_major(_mm_tn_grouped(xc, dpi, LRU_HEADS, LRU_HEAD_DIM, name=f"{tag}_dw_igate"), 1)]
    dxb, cacc = _conv_bwd(dxc, z, W["od_conv_w"][0], name=f"{tag}_conv_bwd")
    grads["od_conv_w"], grads["od_conv_b"] = cacc[None, 0:4], cacc[4:5]
    dz = jnp.concatenate([dgate, dxb], axis=1)
    grads["od_w_in"] = [_mm_tn_owners(h, [dz], name=f"{tag}_dw_in")]
    dx, dgn = _mm(dz, [(W["od_w_in"][0], 0, 0)], _norm_bwd_epilogue(0), [(D_MODEL, F32, 0)], nt=True,
                  extras=[(x, 0), (dxo, 0)], rows=[W["od_norm"][0].reshape(1, D_MODEL)], sums=[D_MODEL], tn=D_MODEL,
                  nj=1, name=f"{tag}_dh")
    grads["od_norm"] = dgn
    return dx


def _local_step(x, mem, positions, target, W, later_weights=None, exchange_earlier=None):
    tabs = _rope_tables(positions)
    reset = (positions == 0).astype(F32)[:, None]
    grads = {n: [None, None] for n in ("xa_norm_x", "xa_norm_mem", "xa_w_q", "xa_w_kv", "xa_w_o", "ffn_norm",
                                       "ffn_w_gate_up", "ffn_w_down")}
    x1, s_even = _even_block(x, tabs, W, "l0_even")
    if later_weights is not None:
        W = {**W, **later_weights(x1)}
    x2, s_xa0 = _memory_block(x1, mem, W, 0, "l0")
    x3, s_ff0 = _ffn_block(x2, W, 0, "l0")
    x4, s_odd = _odd_block(x3, reset, W, "l1_odd")
    x5, s_xa1 = _memory_block(x4, mem, W, 1, "l1")
    x6, s_ff1 = _ffn_block(x5, W, 1, "l1")
    d, dgf, loss = _loss_head(x6, W["final_norm"], target, name="loss_head")
    grads["final_norm"] = dgf[0]
    d = _ffn_block_bwd(d, W, 1, s_ff1, "l1", grads)
    d = _memory_block_bwd(d, mem, W, 1, s_xa1, "l1", grads)
    d = _odd_block_bwd(d, reset, W, s_odd, "l1_odd", grads)
    d = _ffn_block_bwd(d, W, 0, s_ff0, "l0", grads)
    d = _memory_block_bwd(d, mem, W, 0, s_xa0, "l0", grads)
    grads["ev_w_out"] = _even_out_grad(d, s_even, "l0_even")
    token = exchange_earlier(grads) if exchange_earlier is not None else None
    d = _even_block_bwd(d, tabs, W, s_even, "l0_even", grads, token)
    big = {n: grads.pop(n) for n in MATMUL_WEIGHTS}
    for n, v in grads.items():
        if isinstance(v, list):
            grads[n] = jnp.stack(v)
    return loss[0, 0], d, big, grads


WEIGHTS = ("ev_norm", "ev_w_in", "ev_pool_w", "ev_pool_scale", "ev_q_norm", "ev_w_q_up", "ev_kv_norm", "ev_w_kv_up",
           "ev_w_out", "od_norm", "od_w_in", "od_conv_w", "od_conv_b", "od_w_rgate", "od_b_rgate", "od_w_igate",
           "od_b_igate", "od_lambda", "od_w_out", "xa_norm_x", "xa_norm_mem", "xa_w_q", "xa_w_kv", "xa_w_o",
           "ffn_norm", "ffn_w_gate_up", "ffn_w_down", "final_norm")
SHARD_AXIS = {"ev_w_in": 1, "ev_w_q_up": 2, "ev_w_kv_up": 2, "ev_w_out": 1, "od_norm": 1, "od_w_in": 2,
              "od_conv_w": 2, "od_conv_b": 1, "od_w_rgate": 2, "od_b_rgate": 1, "od_w_igate": 2, "od_b_igate": 1,
              "od_lambda": 1, "od_w_out": 1, "xa_w_q": 1, "xa_w_kv": 2, "xa_w_o": 1, "ffn_w_gate_up": 2,
              "ffn_w_down": 1}
MATMUL_WEIGHTS = ("ev_w_in", "ev_w_q_up", "ev_w_kv_up", "ev_w_out", "od_w_in", "od_w_rgate", "od_w_igate",
                  "od_w_out", "xa_w_q", "xa_w_kv", "xa_w_o", "ffn_w_gate_up", "ffn_w_down")
SMALL_SHARDED = tuple(n for n in WEIGHTS if n in SHARD_AXIS and n not in MATMUL_WEIGHTS)
REPLICATED = tuple(n for n in WEIGHTS if n not in SHARD_AXIS)


def _pack(parts, quantum):
    flat = jnp.concatenate([p.reshape(-1) for p in parts])
    pad = (-flat.shape[0]) % quantum
    return jnp.pad(flat, (0, pad)).reshape(-1, LANES)


def _unpack(flat, shapes):
    out, off = [], 0
    for shape in shapes:
        size = math.prod(shape)
        out.append(flat[off:off + size].reshape(shape))
        off += size
    return out


def _run_copies(local, remote, send_sems, recv_sems, local_sems):
    locals_ = [pltpu.make_async_copy(src, dst, local_sems.at[n]) for n, (src, dst) in enumerate(local)]
    for cp in locals_:
        cp.start()
    sends = [pltpu.make_async_remote_copy(src_ref=src, dst_ref=dst, send_sem=send_sems.at[k, n],
                                          recv_sem=recv_sems.at[k, n], device_id=dev, device_id_type=MESH)
             for (k, n, src, dst, _, dev) in remote]
    for cp in sends:
        cp.start()
    for (k, n, src, _, arrival, dev) in remote:
        pltpu.make_async_remote_copy(src_ref=src, dst_ref=arrival, send_sem=send_sems.at[k, n],
                                     recv_sem=recv_sems.at[k, n], device_id=dev, device_id_type=MESH).wait_recv()
    for cp in sends:
        cp.wait_send()
    for cp in locals_:
        cp.wait()


def _chip_peers(x, y):
    return [(1 - x, y), (x, 1 - y), (1 - x, 1 - y)]


def _owner_block(ref, axis, q):
    size = ref.shape[axis] // N_CHIPS
    idx = [slice(None)] * len(ref.shape)
    idx[axis] = pl.ds(q * size, size)
    return ref.at[tuple(idx)]


def _comm_call(body, ins, out_shapes, n_items, n_peers, *, name):
    return pl.pallas_call(
        body, in_specs=[ANY] * len(ins), out_specs=[ANY] * len(out_shapes), out_shape=out_shapes,
        scratch_shapes=[pltpu.SemaphoreType.DMA((n_peers, n_items)), pltpu.SemaphoreType.DMA((n_peers, n_items)),
                        pltpu.SemaphoreType.DMA((n_items,))],
        name=name,
    )(*ins)


def _gather_chips(shards, axes, *, name):
    n = len(shards)
    full = [jax.ShapeDtypeStruct(tuple(d * (N_CHIPS if a == ax else 1) for a, d in enumerate(s.shape)), s.dtype)
            for s, ax in zip(shards, axes)]

    def body(*refs):
        srcs, dsts = refs[:n], refs[n:2 * n]
        x, y, c = lax.axis_index("x"), lax.axis_index("y"), lax.axis_index("c")
        me = 2 * x + y
        local = [(srcs[i], _owner_block(dsts[i], axes[i], me)) for i in range(n)]
        remote = [(k, i, srcs[i], _owner_block(dsts[i], axes[i], me), _owner_block(dsts[i], axes[i], 2 * px + py),
                   (px, py, c))
                  for k, (px, py) in enumerate(_chip_peers(x, y)) for i in range(n)]
        _run_copies(local, remote, *refs[2 * n:])

    return _comm_call(body, shards, full, n, 3, name=name)


HBM = pl.BlockSpec(memory_space=pltpu.HBM)
SEM = pl.BlockSpec(memory_space=pltpu.SEMAPHORE)
DATAFLOW = pltpu.SideEffectType.DATAFLOW_SIDE_EFFECTING


def _gather_plan(axes):
    return lambda srcs, lands, me, peer: [
        (srcs[i], _owner_block(lands[i], ax, me), _owner_block(lands[i], ax, peer)) for i, ax in enumerate(axes)]


def _exchange_plan(where):
    return lambda srcs, lands, me, peer: [
        (srcs[i].at[peer], lands[n].at[me, l], lands[n].at[peer, l]) for i, (n, l) in enumerate(where)]


def _split_peers(sibling):
    x, y, c = lax.axis_index("x"), lax.axis_index("y"), lax.axis_index("c")
    peers = [((px, py, c), 2 * px + py) for px, py in _chip_peers(x, y)]
    return 2 * x + y, peers + ([((x, y, 1 - c), 2 * x + y)] if sibling else [])


def _split_start(srcs, lands, plan, *, sibling=False, name):
    ns, nl = len(srcs), len(lands)
    nsem = (3 + sibling) * len(plan(list(srcs), list(lands), 0, 0))

    def body(*refs):
        src_refs, land_refs = refs[:ns], refs[ns:ns + nl]
        send_sems, recv_sems = refs[ns + nl:ns + nl + nsem], refs[ns + nl + nsem:ns + nl + 2 * nsem]
        me, peers = _split_peers(sibling)
        n = 0
        for device, chip in peers:
            for src, dst, _ in plan(src_refs, land_refs, me, chip):
                pltpu.make_async_remote_copy(src_ref=src, dst_ref=dst, send_sem=send_sems[n], recv_sem=recv_sems[n],
                                             device_id=device, device_id_type=MESH).start()
                n += 1
        refs[-1][...] = jnp.zeros_like(refs[-1])

    arrays = list(srcs) + list(lands)
    out = pl.pallas_call(
        body, name=name, in_specs=[HBM] * (ns + nl),
        out_specs=[SEM] * (2 * nsem) + [HBM] * (ns + nl) + [pl.BlockSpec(memory_space=pltpu.VMEM)],
        out_shape=[pltpu.SemaphoreType.DMA(())] * (2 * nsem) + [pltpu.HBM(a.shape, a.dtype) for a in arrays]
        + [jax.ShapeDtypeStruct((8, LANES), F32)],
        input_output_aliases={i: 2 * nsem + i for i in range(ns + nl)},
        compiler_params=pltpu.CompilerParams(has_side_effects=DATAFLOW),
    )(*[pltpu.with_memory_space_constraint(a, pltpu.HBM) for a in arrays])
    sems, rest = out[:2 * nsem], out[2 * nsem:]
    return sems[:nsem], sems[nsem:], rest[:ns], rest[ns:ns + nl], rest[-1]


def _split_wait(handle, after, plan, *, sibling=False, name):
    send_sems, recv_sems, srcs, lands, _ = handle
    ns, nl, nsem = len(srcs), len(lands), len(send_sems)

    def body(*refs):
        src_refs, land_refs = refs[:ns], refs[ns:ns + nl]
        send_refs, recv_refs = refs[ns + nl:ns + nl + nsem], refs[ns + nl + nsem:ns + nl + 2 * nsem]
        me, peers = _split_peers(sibling)
        n = 0
        for device, chip in peers:
            for src, _, arrival in plan(src_refs, land_refs, me, chip):
                cp = pltpu.make_async_remote_copy(src_ref=src, dst_ref=arrival, send_sem=send_refs[n],
                                                  recv_sem=recv_refs[n], device_id=device, device_id_type=MESH)
                cp.wait_send()
                cp.wait_recv()
                n += 1

    out = pl.pallas_call(
        body, name=name, in_specs=[HBM] * (ns + nl) + [SEM] * (2 * nsem) + [ANY], out_specs=[HBM] * (ns + nl),
        out_shape=[pltpu.HBM(a.shape, a.dtype) for a in list(srcs) + list(lands)],
        input_output_aliases={i: i for i in range(ns + nl)},
        compiler_params=pltpu.CompilerParams(has_side_effects=DATAFLOW),
    )(*srcs, *lands, *send_sems, *recv_sems, after)
    return out[ns:]


def _exchange_sibling(arrays, *, name):
    n = len(arrays)

    def body(*refs):
        x, y, c = lax.axis_index("x"), lax.axis_index("y"), lax.axis_index("c")
        remote = [(0, i, refs[i], refs[n + i], refs[n + i], (x, y, 1 - c)) for i in range(n)]
        _run_copies([], remote, *refs[2 * n:])

    return _comm_call(body, arrays, [jax.ShapeDtypeStruct(a.shape, a.dtype) for a in arrays], n, 1, name=name)


def _sum_slots(r, *, token=None, name):
    shape = r.shape[1:]
    cols = shape[-1]
    rows = math.prod(shape) // cols
    tr = max(d for d in range(8, rows + 1, 8) if rows % d == 0 and d * cols * 16 <= (4 << 20))

    def body(r_ref, *refs):
        total = ((r_ref[0] + r_ref[1]) + r_ref[2]) + r_ref[3]
        refs[-1][...] = total if token is None else total + refs[0][0:1, 0:1]

    in_specs = [pl.BlockSpec((N_CHIPS, tr, cols), lambda i: (0, i, 0))]
    in_specs += [] if token is None else [_const((8, LANES))]
    return pl.pallas_call(
        body, grid=(rows // tr,), in_specs=in_specs,
        out_specs=_rows(tr, cols), out_shape=jax.ShapeDtypeStruct((rows, cols), F32), compiler_params=_cp(1),
        name=name,
    )(r.reshape(N_CHIPS, rows, cols), *([] if token is None else [token])).reshape(shape)


FIRST_WEIGHTS = ("ev_w_in", "ev_w_q_up", "ev_w_kv_up", "ev_w_out")
LATER_WEIGHTS = tuple(n for n in MATMUL_WEIGHTS if n not in FIRST_WEIGHTS)
LAST_GRADS = ("ev_w_in", "ev_w_q_up", "ev_w_kv_up")
EARLIER_GRADS = tuple(n for n in MATMUL_WEIGHTS if n not in LAST_GRADS)


def _my_chip():
    return 2 * lax.axis_index("x") + lax.axis_index("y")


def _gather_first(w):
    small = _pack([w[n] for n in SMALL_SHARDED], 8 * LANES)
    stacked = [n for n in FIRST_WEIGHTS if SHARD_AXIS[n] == w[n].ndim - 1 and w[n].shape[-1] % LANES]
    shards = [w[n].astype(BF16)[None] if n in stacked else w[n].astype(BF16) for n in FIRST_WEIGHTS]
    got = _gather_chips(shards + [small], [0 if n in stacked else SHARD_AXIS[n] for n in FIRST_WEIGHTS] + [0],
                        name="gather_first")
    full = {n: w[n] for n in REPLICATED}
    for n, g in zip(FIRST_WEIGHTS, got[:-1]):
        full[n] = jnp.concatenate([g[q] for q in range(N_CHIPS)], axis=SHARD_AXIS[n]) if n in stacked else g
    per_chip = [_unpack(got[-1][q * small.shape[0]:(q + 1) * small.shape[0]].reshape(-1),
                        [w[n].shape for n in SMALL_SHARDED]) for q in range(N_CHIPS)]
    for i, n in enumerate(SMALL_SHARDED):
        full[n] = jnp.concatenate([per_chip[q][i] for q in range(N_CHIPS)], axis=SHARD_AXIS[n])
    return full


def _gather_later_start(w, after):
    behind = (after.reshape(-1)[0] * 0).astype(BF16)
    shards = [w[n].astype(BF16) + (behind if n == "od_w_rgate" else 0) for n in LATER_WEIGHTS]
    axes = [SHARD_AXIS[n] for n in LATER_WEIGHTS]
    lands = [lax.empty(tuple(d * (N_CHIPS if a == ax else 1) for a, d in enumerate(s.shape)), s.dtype)
             for s, ax in zip(shards, axes)]
    plan = _gather_plan(axes)
    return _split_start(shards, lands, plan, sibling=True, name="gather_later_start"), plan


def _owner_major(g, axis):
    shape = g.shape
    size = shape[axis] // N_CHIPS
    g = jnp.moveaxis(g.reshape(shape[:axis] + (N_CHIPS, size) + shape[axis + 1:]), axis, 0)
    return g.reshape(N_CHIPS, -1, shape[-1] if axis < len(shape) - 1 else size)


def _exchange_start(items, *, cross, name):
    me = _my_chip()
    srcs, lands, where = [], [], []
    for n, layers in enumerate(items):
        land = lax.empty((N_CHIPS, len(layers)) + layers[0].shape[1:], layers[0].dtype)
        for l, a in enumerate(layers):
            if not cross:
                own = lax.dynamic_index_in_dim(a, me, 0, keepdims=True)[:, None]
                land = lax.dynamic_update_slice(land, own, (me, l) + (0,) * (a.ndim - 1))
            srcs.append(a)
            where.append((n, l))
        lands.append(land)
    plan = _exchange_plan(where)
    return _split_start(srcs, lands, plan, sibling=cross, name=name), plan


def _earlier_items(grads, full_shapes):
    small = [_pack([jnp.split(grads[n].reshape(full_shapes[n]), N_CHIPS, axis=SHARD_AXIS[n])[q]
                    for n in SMALL_SHARDED], 8 * LANES) for q in range(N_CHIPS)]
    return [grads[n] for n in EARLIER_GRADS] + [[jnp.stack(small)]]


def _last_items(big, grads, full_shapes, loss):
    repl = _pack([grads[n].reshape(full_shapes[n]) for n in REPLICATED] + [loss.reshape(1)], 8 * LANES)
    return [big[n] for n in LAST_GRADS] + [[jnp.stack([repl] * N_CHIPS)]]


def kernel(
        x, mem, positions, ev_norm, ev_w_in, ev_pool_w, ev_pool_scale, ev_q_norm, ev_w_q_up, ev_kv_norm,
        ev_w_kv_up, ev_w_out, od_norm, od_w_in, od_conv_w, od_conv_b, od_w_rgate, od_b_rgate, od_w_igate,
        od_b_igate, od_lambda, od_w_out, xa_norm_x, xa_norm_mem, xa_w_q, xa_w_kv, xa_w_o, ffn_norm,
        ffn_w_gate_up, ffn_w_down, final_norm, loss_target, m_ev_norm, m_ev_w_in, m_ev_pool_w, m_ev_pool_scale,
        m_ev_q_norm, m_ev_w_q_up, m_ev_kv_norm, m_ev_w_kv_up, m_ev_w_out, m_od_norm, m_od_w_in, m_od_conv_w,
        m_od_conv_b, m_od_w_rgate, m_od_b_rgate, m_od_w_igate, m_od_b_igate, m_od_lambda, m_od_w_out,
        m_xa_norm_x, m_xa_norm_mem, m_xa_w_q, m_xa_w_kv, m_xa_w_o, m_ffn_norm, m_ffn_w_gate_up, m_ffn_w_down,
        m_final_norm, v_ev_norm, v_ev_w_in, v_ev_pool_w, v_ev_pool_scale, v_ev_q_norm, v_ev_w_q_up,
        v_ev_kv_norm, v_ev_w_kv_up, v_ev_w_out, v_od_norm, v_od_w_in, v_od_conv_w, v_od_conv_b, v_od_w_rgate,
        v_od_b_rgate, v_od_w_igate, v_od_b_igate, v_od_lambda, v_od_w_out, v_xa_norm_x, v_xa_norm_mem, v_xa_w_q,
        v_xa_w_kv, v_xa_w_o, v_ffn_norm, v_ffn_w_gate_up, v_ffn_w_down, v_final_norm):
    given = dict(locals())
    w = {n: given[n] for n in WEIGHTS}
    full_shapes = {n: tuple(d * (N_CHIPS if a == SHARD_AXIS.get(n) else 1) for a, d in enumerate(w[n].shape))
                   for n in WEIGHTS}
    full = _gather_first(w)
    later, later_plan = _gather_later_start(w, full["ev_w_out"])
    full["ev_norm"] = full["ev_norm"] + later[4][0:1, 0:1]
    exchange = {}

    def later_weights(after):
        return dict(zip(LATER_WEIGHTS, _split_wait(later, after, later_plan, sibling=True, name="gather_later_wait")))

    def exchange_earlier(grads):
        exchange["handle"], exchange["plan"] = _exchange_start(_earlier_items(grads, full_shapes), cross=True,
                                                               name="exchange_earlier_start")
        return exchange["handle"][4]

    loss, grad_x, big, grads = _local_step(x[0], mem[0], positions[0], loss_target[0], full, later_weights,
                                           exchange_earlier)
    earlier = EARLIER_GRADS + ("small",)
    got = dict(zip(earlier, _split_wait(exchange["handle"], grad_x, exchange["plan"], sibling=True,
                                        name="exchange_earlier_wait")))
    last, last_plan = _exchange_start(_last_items(big, grads, full_shapes, loss), cross=False,
                                      name="exchange_last_start")
    out = {}

    def finish(names, landed, token, tag):
        mine = [_sum_slots(landed[n], token=token if i == 0 else None, name=f"sum_chips_{n}")
                for i, n in enumerate(names)]
        other = _exchange_sibling(mine, name=f"exchange_sibling_{tag}")
        total = None
        for n, a, b in zip(names, mine, other):
            if n in MATMUL_WEIGHTS:
                out[n] = _adamw(w[n], a.reshape(w[n].shape), b.reshape(w[n].shape), given["m_" + n], given["v_" + n],
                                name=f"adamw_{n}")
                continue
            group = SMALL_SHARDED if n == "small" else REPLICATED
            spare = [jnp.zeros((1,), F32)] if group is REPLICATED else []
            packed = [_pack([given[pre + k] for k in group] + spare, 8 * LANES) for pre in ("", "m_", "v_")]
            res = _adamw(packed[0], a.reshape(packed[0].shape), b.reshape(packed[0].shape), packed[1], packed[2],
                         name=f"adamw_{n}")
            shapes = [w[k].shape for k in group] + [(1,)] * len(spare)
            for j, arrs in enumerate(zip(*[_unpack(r.reshape(-1), shapes) for r in res])):
                if j < len(group):
                    out[group[j]] = list(arrs)
                else:
                    total = arrs[0][0]
        return total

    finish(earlier, got, last[4], "earlier")
    names = LAST_GRADS + ("replicated",)
    got = dict(zip(names, _split_wait(last, out[EARLIER_GRADS[-1]][1], last_plan, name="exchange_last_wait")))
    loss = finish(names, got, None, "last")
    return (loss, grad_x[None], *[out[n][k] for k in range(4) for n in WEIGHTS])
```

```python
import math

import jax
import jax.numpy as jnp
from jax import lax
from jax.experimental import pallas as pl
from jax.experimental.pallas import tpu as pltpu

F32 = jnp.float32
BF16 = jnp.bfloat16

D_MODEL = 1024
POOL_DIM = 512
POOL_WINDOWS = (2, 4, 8, 16)
POOL_GROUP = 128
MLA_HEADS = 8
QK_NOPE = 64
QK_ROPE = 32
QK_DIM = QK_NOPE + QK_ROPE
V_HEAD = 64
HEAD_PAD = 128
Q_RANK = 256
KV_RANK = 128
ROPE_BASE = 10000.0
LRU_HEADS = 4
LRU_HEAD_DIM = 256
CONV_WIDTH = 4
LRU_C = 8.0
MEM_HEADS = 4
MEM_HEAD_DIM = 256
D_FF = 2816
RMS_EPS = 1e-6
NEG_INF = -1e30

ADAM_LR = 0.001
ADAM_B1 = 0.9
ADAM_B2 = 0.999
ADAM_EPS = 1e-08
ADAM_WD = 0.01
ADAM_STEP = 10

N_CHIPS = 4
LANES = 128
VMEM_LIMIT = 56 * 1024 * 1024
MESH = pl.DeviceIdType.MESH
ANY = pl.BlockSpec(memory_space=pl.ANY)
MIX_DIM = POOL_DIM + MLA_HEADS * HEAD_PAD

NN = (((1,), (0,)), ((), ()))
NT = (((1,), (1,)), ((), ()))
TN = (((0,), (0,)), ((), ()))


def _cp(n):
    return pltpu.CompilerParams(dimension_semantics=("arbitrary",) * n, vmem_limit_bytes=VMEM_LIMIT)


def _dot(a, b, dims=NN):
    return lax.dot_general(a, b, dims, preferred_element_type=F32)


def _row_tile(S):
    return 1024 if S % 1024 == 0 else min(S, 512)


def _rows(ts, w, cb=0):
    return pl.BlockSpec((ts, w), lambda i: (i, cb))


def _const(shape):
    return pl.BlockSpec(shape, lambda i: (0,) * len(shape))


MM_VMEM_BUDGET = 46 * 1024 * 1024


def _mm(a, bs, epi, outs, *, tn, nj, nt=False, also=None, extras=(), rows=(), sums=(), a_cb=0, k=None, tm=None,
        name):
    M = a.shape[0]
    k = k or a.shape[1]
    nb, ne, nr, no = len(bs), len(extras), len(rows), len(outs)
    lhs = [(a, k, a_cb, b) for b in bs[:1]] + ([(also[0], also[0].shape[1], 0, also[1])] if also else [])
    if tm is None:
        per_row = 2 * (sum(kk * x.dtype.itemsize for x, kk, _, _ in lhs)
                       + sum(e.dtype.itemsize for e, _ in extras) * tn
                       + sum(jnp.dtype(dt).itemsize for _, dt, _ in outs) * tn) + nb * tn * 4
        weights = (1 if nj == 1 else 2) * (sum(b.dtype.itemsize for b, _, _ in bs) * k
                                           + (also[1][0].dtype.itemsize * lhs[-1][1] if also else 0)) * tn
        tm = 1024 if M % 1024 == 0 and 1024 * per_row + weights <= MM_VMEM_BUDGET else min(M, 512)
    dims = NT if nt else NN
    assert not sums or nj == 1
    na = 2 if also else 0

    def body(*refs):
        av = refs[0][...].astype(BF16)
        accs = [_dot(av, r[...].astype(BF16), dims) for r in refs[1:1 + nb]]
        if also:
            accs[0] = accs[0] + _dot(refs[1 + nb][...].astype(BF16), refs[2 + nb][...].astype(BF16), dims)
        refs = refs[:1 + nb] + refs[1 + nb + na:]
        vals = epi(accs, [r[...] for r in refs[1 + nb:1 + nb + ne + nr]])
        outs_refs = refs[1 + nb + ne + nr:]
        for o, v in zip(outs_refs[:no], vals[:no]):
            o[...] = v.astype(o.dtype)
        if sums:
            @pl.when(pl.program_id(1) == 0)
            def _():
                for o in outs_refs[no:]:
                    o[...] = jnp.zeros_like(o)

            for o, v in zip(outs_refs[no:], vals[no:]):
                o[...] += v

    in_specs = [pl.BlockSpec((tm, k), lambda j, i: (i, a_cb))]
    weights = [(k, rb, cb) for (_, rb, cb) in bs]
    if also:
        in_specs_also = pl.BlockSpec((tm, lhs[-1][1]), lambda j, i: (i, 0))
        weights.append((lhs[-1][1], also[1][1], also[1][2]))
    for n, (kk, rb, cb) in enumerate(weights):
        if also and n == nb:
            in_specs.append(in_specs_also)
        mode = dict(pipeline_mode=pl.Buffered(1)) if nj == 1 else {}
        if nt:
            in_specs.append(pl.BlockSpec((tn, kk), lambda j, i, rb=rb, cb=cb: (rb + j, cb), **mode))
        else:
            in_specs.append(pl.BlockSpec((kk, tn), lambda j, i, rb=rb, cb=cb: (rb, cb + j), **mode))
    for (_, cb) in extras:
        in_specs.append(pl.BlockSpec((tm, tn), lambda j, i, cb=cb: (i, cb + j)))
    in_specs += [pl.BlockSpec((1, tn), lambda j, i: (0, 0))] * nr
    out_specs = [pl.BlockSpec((tm, tn), lambda j, i, cb=cb: (i, cb + j)) for (_, _, cb) in outs]
    out_specs += [pl.BlockSpec((1, w), lambda j, i: (0, 0)) for w in sums]
    res = pl.pallas_call(
        body, grid=(nj, M // tm), in_specs=in_specs, out_specs=out_specs,
        out_shape=[jax.ShapeDtypeStruct((M, n), dt) for (n, dt, _) in outs]
        + [jax.ShapeDtypeStruct((1, w), F32) for w in sums],
        compiler_params=_cp(2), name=name,
    )(a, *[b for (b, _, _) in bs], *([also[0], also[1][0]] if also else []), *[e for (e, _) in extras], *rows)
    return res


def _first(accs, extras):
    return [accs[0]]


def _add_res(accs, extras):
    return [accs[0] + extras[0].astype(F32)]


def _norm_bwd_epilogue(partials):
    def epi(accs, vals):
        dh = accs[0]
        for part in vals[:partials]:
            dh = dh + part.astype(F32)
        x, res, g = vals[partials:partials + 3]
        r = lax.rsqrt(jnp.mean(x * x, axis=-1, keepdims=True) + RMS_EPS)
        n = x * r
        dn = dh * g
        return [r * (dn - n * jnp.mean(dn * n, axis=-1, keepdims=True)) + res, jnp.sum(dh * n, axis=0, keepdims=True)]

    return epi


TN_VMEM_BUDGET = 44 * 1024 * 1024


def _contraction_rows(S, row_bytes, out_elems):
    ts = min(S, 2048)
    while ts > 512 and 2 * (ts * row_bytes + out_elems * 4) > TN_VMEM_BUDGET:
        ts //= 2
    return ts


def _mm_tn(a, b, *, ka=None, a_cb=0, nb=None, b_cb=0, tk=None, tn=None, ts=None, name):
    S = a.shape[0]
    ka = ka or a.shape[1]
    nb = nb or b.shape[1]
    tk = tk or ka
    tn = tn or nb
    ts = ts or _contraction_rows(S, tk * a.dtype.itemsize + tn * b.dtype.itemsize, tk * tn)
    a0, b0 = a_cb * (ka // tk), b_cb * (nb // tn)

    def body(a_ref, b_ref, o_ref):
        @pl.when(pl.program_id(2) == 0)
        def _():
            o_ref[...] = jnp.zeros_like(o_ref)

        o_ref[...] += _dot(a_ref[...].astype(BF16), b_ref[...].astype(BF16), TN)

    return pl.pallas_call(
        body, grid=(ka // tk, nb // tn, S // ts),
        in_specs=[pl.BlockSpec((ts, tk), lambda p, q, s: (s, a0 + p)),
                  pl.BlockSpec((ts, tn), lambda p, q, s: (s, b0 + q))],
        out_specs=pl.BlockSpec((tk, tn), lambda p, q, s: (p, q)),
        out_shape=jax.ShapeDtypeStruct((ka, nb), F32), compiler_params=_cp(3), name=name,
    )(a, b)


def _mm_tn_owners(a, bs, *, name):
    S, ka = a.shape
    nb = sum(b.shape[1] for b in bs)
    tn = nb // N_CHIPS
    ts = _contraction_rows(S, ka * a.dtype.itemsize + len(bs) * tn * bs[0].dtype.itemsize, ka * tn)
    per = N_CHIPS // len(bs)

    def body(a_ref, *refs):
        o_ref = refs[-1]
        q = pl.program_id(0)

        @pl.when(pl.program_id(1) == 0)
        def _():
            o_ref[...] = jnp.zeros_like(o_ref)

        av = a_ref[...].astype(BF16)
        for n, b_ref in enumerate(refs[:-1]):
            @pl.when(q // per == n)
            def _():
                o_ref[0] += _dot(av, b_ref[...].astype(BF16), TN)

    in_specs = [pl.BlockSpec((ts, ka), lambda q, s: (s, 0))]
    for n in range(len(bs)):
        in_specs.append(pl.BlockSpec((ts, tn), lambda q, s, n=n: (jnp.where(q // per == n, s, 0),
                                                                  jnp.clip(q - n * per, 0, per - 1))))
    return pl.pallas_call(
        body, grid=(N_CHIPS, S // ts), in_specs=in_specs,
        out_specs=pl.BlockSpec((1, ka, tn), lambda q, s: (q, 0, 0)),
        out_shape=jax.ShapeDtypeStruct((N_CHIPS, ka, tn), F32), compiler_params=_cp(2), name=name,
    )(a, *bs)


def _mm_tn_grouped(a, b, groups, w, *, name):
    S = a.shape[0]
    ts = _contraction_rows(S, w * (a.dtype.itemsize + b.dtype.itemsize), w * w)

    def body(a_ref, b_ref, o_ref):
        @pl.when(pl.program_id(1) == 0)
        def _():
            o_ref[...] = jnp.zeros_like(o_ref)

        o_ref[0] += _dot(a_ref[...].astype(BF16), b_ref[...].astype(BF16), TN)

    return pl.pallas_call(
        body, grid=(groups, S // ts),
        in_specs=[pl.BlockSpec((ts, w), lambda g, s: (s, g)), pl.BlockSpec((ts, w), lambda g, s: (s, g))],
        out_specs=pl.BlockSpec((1, w, w), lambda g, s: (g, 0, 0)),
        out_shape=jax.ShapeDtypeStruct((groups, w, w), F32), compiler_params=_cp(2), name=name,
    )(a, b)


def _rms(x, g, *, name):
    S, w = x.shape
    ts = _row_tile(S)

    def body(x_ref, g_ref, o_ref):
        xv = x_ref[...]
        r = lax.rsqrt(jnp.mean(xv * xv, axis=-1, keepdims=True) + RMS_EPS)
        o_ref[...] = (xv * r * g_ref[...]).astype(o_ref.dtype)

    return pl.pallas_call(
        body, grid=(S // ts,), in_specs=[_rows(ts, w), _const((1, w))], out_specs=_rows(ts, w),
        out_shape=jax.ShapeDtypeStruct((S, w), BF16), compiler_params=_cp(1), name=name,
    )(x, g.reshape(1, w))


def _norm_gain_grad(x, dy, *, name):
    S, w = x.shape
    ts = _row_tile(S)

    def body(x_ref, dy_ref, dg_ref):
        @pl.when(pl.program_id(0) == 0)
        def _():
            dg_ref[...] = jnp.zeros_like(dg_ref)

        xv = x_ref[...]
        r = lax.rsqrt(jnp.mean(xv * xv, axis=-1, keepdims=True) + RMS_EPS)
        dg_ref[...] += jnp.sum(dy_ref[...] * (xv * r), axis=0, keepdims=True)

    return pl.pallas_call(
        body, grid=(S // ts,), in_specs=[_rows(ts, w), _rows(ts, w)], out_specs=_const((1, w)),
        out_shape=jax.ShapeDtypeStruct((1, w), F32), compiler_params=_cp(1), name=name,
    )(x, dy)


HALO = 16


def _pool_counts(i, ts, rows, first_row):
    t = i * ts + first_row + lax.broadcasted_iota(jnp.int32, (rows, 1), 0)
    return [jnp.minimum(t + 1, w).astype(F32) for w in POOL_WINDOWS]


def _even_front(x, g, w_in, pool_w, pool_scale, g_q, w_q, g_kv, w_kv, ctab, stab, *, name):
    S = x.shape[0]
    ts = min(S, 512)

    def body(x_ref, g_ref, win_ref, pw_ref, sc_ref, gq_ref, wq_ref, gkv_ref, wkv_ref, c_ref, s_ref,
             h_ref, z_ref, y_ref, p_ref, cqn_ref, ckvn_ref, q_ref, k_ref, v_ref, tail):
        i = pl.program_id(0)

        def normed(t, gain):
            r = lax.rsqrt(jnp.mean(t * t, axis=-1, keepdims=True) + RMS_EPS)
            return (t * r * gain).astype(BF16)

        h = normed(x_ref[...], g_ref[...])
        h_ref[...] = h
        z = _dot(h, win_ref[...])
        z_ref[...] = z
        u = z[:, :POOL_DIM]
        xe = jnp.concatenate([jnp.where(i > 0, tail[...], 0.0), u], axis=0)
        tail[...] = u[ts - HALO:]
        sums = []
        s = xe
        for sh in (1, 2, 4, 8):
            s = s + pltpu.roll(s, sh, 0)
            sums.append(s)
        cnts = _pool_counts(i, ts, ts, 0)
        for grp in range(4):
            lo, hi = grp * POOL_GROUP, (grp + 1) * POOL_GROUP
            pooled = (sums[grp][HALO:, lo:hi] / cnts[grp] - u[:, lo:hi]).astype(BF16)
            p_ref[:, lo:hi] = pooled
            y_ref[:, lo:hi] = (_dot(pooled, pw_ref[grp]) * sc_ref[:, lo:hi]).astype(y_ref.dtype)
        cqn = normed(z[:, POOL_DIM:POOL_DIM + Q_RANK], gq_ref[...])
        ckvn = normed(z[:, POOL_DIM + Q_RANK:POOL_DIM + Q_RANK + KV_RANK], gkv_ref[...])
        cqn_ref[...] = cqn
        ckvn_ref[...] = ckvn
        q = _dot(cqn, wq_ref[...])
        kv = _dot(ckvn, wkv_ref[...])
        c, sn = c_ref[...], s_ref[...]
        kr = z[:, D_MODEL - HEAD_PAD:]
        kr_rot = kr * c + _rope_partner(kr) * sn
        lane = lax.broadcasted_iota(jnp.int32, (ts, HEAD_PAD), 1)
        for hd in range(MLA_HEADS):
            lo, hi = hd * HEAD_PAD, (hd + 1) * HEAD_PAD
            qh = q[:, lo:hi]
            q_ref[:, lo:hi] = (qh * c + _rope_partner(qh) * sn).astype(q_ref.dtype)
            k_ref[:, lo:hi] = (kv[:, lo:hi] + kr_rot).astype(k_ref.dtype)
            v_ref[:, lo:hi] = jnp.where(lane == V_HEAD, 1.0, kv[:, D_MODEL + lo:D_MODEL + hi]).astype(v_ref.dtype)

    wide = jax.ShapeDtypeStruct((S, D_MODEL), BF16)
    return pl.pallas_call(
        body, grid=(S // ts,),
        in_specs=[_rows(ts, D_MODEL), _const((1, D_MODEL)), _const((D_MODEL, D_MODEL)),
                  _const((4, POOL_GROUP, POOL_GROUP)), _const((1, POOL_DIM)), _const((1, Q_RANK)),
                  _const((Q_RANK, D_MODEL)), _const((1, KV_RANK)), _const((KV_RANK, 2 * D_MODEL)),
                  _rows(ts, HEAD_PAD), _rows(ts, HEAD_PAD)],
        out_specs=[_rows(ts, D_MODEL), _rows(ts, D_MODEL), _rows(ts, POOL_DIM), _rows(ts, POOL_DIM),
                   _rows(ts, Q_RANK), _rows(ts, KV_RANK), _rows(ts, D_MODEL), _rows(ts, D_MODEL), _rows(ts, D_MODEL)],
        out_shape=[wide, jax.ShapeDtypeStruct((S, D_MODEL), F32), jax.ShapeDtypeStruct((S, MIX_DIM), BF16),
                   jax.ShapeDtypeStruct((S, POOL_DIM), BF16), jax.ShapeDtypeStruct((S, Q_RANK), BF16),
                   jax.ShapeDtypeStruct((S, KV_RANK), BF16), wide, wide, wide],
        scratch_shapes=[pltpu.VMEM((HALO, POOL_DIM), F32)], compiler_params=_cp(1), name=name,
    )(x, g.reshape(1, D_MODEL), w_in, pool_w, pool_scale, g_q.reshape(1, Q_RANK), w_q, g_kv.reshape(1, KV_RANK), w_kv,
      ctab, stab)


def _norm_bwd_values(xv, gain, dy):
    r = lax.rsqrt(jnp.mean(xv * xv, axis=-1, keepdims=True) + RMS_EPS)
    n = xv * r
    dn = dy * gain
    return r * (dn - n * jnp.mean(dn * n, axis=-1, keepdims=True)), jnp.sum(dy * n, axis=0, keepdims=True)


def _even_back(dq_rot, dk_cat, dv, dmix, pooled, z, x, dxo, ctab, stab, w_q, w_kv, w_in, pool_w, pool_scale, g_q, g_kv,
               g_x, *, name):
    S = x.shape[0]
    ts = min(S, 512)
    nh = ts // HALO
    last = S // HALO - 1
    n = ts + HALO

    def body(dq_ref, dk_ref, dv_ref, dy_ref, dyh_ref, p_ref, z_ref, x_ref, dxo_ref, c_ref, s_ref, wq_ref, wkv_ref,
             win_ref, pw_ref, sc_ref, gq_ref, gkv_ref, gx_ref,
             dx_ref, dqp_ref, dz_ref, dyp_ref, dgq_ref, dgkv_ref, dsc_ref, dgx_ref):
        i = pl.program_id(0)

        @pl.when(i == 0)
        def _():
            for ref in (dgq_ref, dgkv_ref, dsc_ref, dgx_ref):
                ref[...] = jnp.zeros_like(ref)

        c, sn = c_ref[...], s_ref[...]
        z = z_ref[...]
        dk = dk_ref[...]
        for hd in range(MLA_HEADS):
            lo, hi = hd * HEAD_PAD, (hd + 1) * HEAD_PAD
            g = dq_ref[:, lo:hi]
            dqp_ref[:, lo:hi] = (g * c + _rope_partner(g * sn)).astype(dqp_ref.dtype)
            heads_sum = dk[:, lo:hi] if hd == 0 else heads_sum + dk[:, lo:hi]
        lane = lax.broadcasted_iota(jnp.int32, heads_sum.shape, 1)
        dkr = jnp.where((lane >= QK_NOPE) & (lane < QK_DIM), heads_sum * c + _rope_partner(heads_sum * sn), 0.0)
        dcqn = _dot(dqp_ref[...], wq_ref[...], NT)
        dckvn = _dot(dk.astype(BF16), wkv_ref[:, :D_MODEL], NT) + _dot(dv_ref[...].astype(BF16),
                                                                       wkv_ref[:, D_MODEL:], NT)
        dcq, dgq = _norm_bwd_values(z[:, POOL_DIM:POOL_DIM + Q_RANK], gq_ref[...], dcqn)
        dckv, dgkv = _norm_bwd_values(z[:, POOL_DIM + Q_RANK:POOL_DIM + Q_RANK + KV_RANK], gkv_ref[...], dckvn)
        dgq_ref[...] += dgq
        dgkv_ref[...] += dgkv
        dyv = dy_ref[...].astype(F32)
        dyh = jnp.where(i < pl.num_programs(0) - 1, dyh_ref[...].astype(F32), 0.0)
        dypre = (jnp.concatenate([dyv, dyh], axis=0) * sc_ref[...]).astype(BF16)
        dyp_ref[...] = dypre[:ts]
        cnts = _pool_counts(i, ts, n, 0)
        dsc = []
        for grp in range(4):
            lo, hi = grp * POOL_GROUP, (grp + 1) * POOL_GROUP
            dsc.append(jnp.sum(dyv[:, lo:hi] * _dot(p_ref[:, lo:hi], pw_ref[grp]), axis=0, keepdims=True))
            dpool = _dot(dypre[:, lo:hi], pw_ref[grp], NT)
            s = dpool / cnts[grp]
            for sh in (1, 2, 4, 8)[:grp + 1]:
                s = s + pltpu.roll(s, n - sh, 0)
            dz_ref[:, lo:hi] = (s[:ts] - dpool[:ts]).astype(dz_ref.dtype)
        dsc_ref[...] += jnp.concatenate(dsc, axis=1)
        dz_ref[:, POOL_DIM:POOL_DIM + Q_RANK] = dcq.astype(dz_ref.dtype)
        dz_ref[:, POOL_DIM + Q_RANK:POOL_DIM + Q_RANK + KV_RANK] = dckv.astype(dz_ref.dtype)
        dz_ref[:, D_MODEL - HEAD_PAD:] = dkr.astype(dz_ref.dtype)
        dx, dgx = _norm_bwd_values(x_ref[...], gx_ref[...], _dot(dz_ref[...], win_ref[...], NT))
        dx_ref[...] = dx + dxo_ref[...]
        dgx_ref[...] += dgx

    wide, pool = _rows(ts, D_MODEL), _rows(ts, POOL_DIM)
    f32 = lambda w: jax.ShapeDtypeStruct((1, w), F32)
    return pl.pallas_call(
        body, grid=(S // ts,),
        in_specs=[wide, wide, wide, pool,
                  pl.BlockSpec((HALO, POOL_DIM), lambda i: (jnp.minimum((i + 1) * nh, last), 0)), pool, wide, wide, wide,
                  _rows(ts, HEAD_PAD), _rows(ts, HEAD_PAD), _const((Q_RANK, D_MODEL)), _const((KV_RANK, 2 * D_MODEL)),
                  _const((D_MODEL, D_MODEL)), _const((4, POOL_GROUP, POOL_GROUP)), _const((1, POOL_DIM)),
                  _const((1, Q_RANK)), _const((1, KV_RANK)), _const((1, D_MODEL))],
        out_specs=[wide, wide, wide, pool, _const((1, Q_RANK)), _const((1, KV_RANK)), _const((1, POOL_DIM)),
                   _const((1, D_MODEL))],
        out_shape=[jax.ShapeDtypeStruct((S, D_MODEL), F32), jax.ShapeDtypeStruct((S, D_MODEL), BF16),
                   jax.ShapeDtypeStruct((S, D_MODEL), BF16), jax.ShapeDtypeStruct((S, POOL_DIM), BF16),
                   f32(Q_RANK), f32(KV_RANK), f32(POOL_DIM), f32(D_MODEL)],
        compiler_params=_cp(1), name=name,
    )(dq_rot, dk_cat, dv, dmix, dmix, pooled, z, x, dxo, ctab, stab, w_q, w_kv, w_in, pool_w, pool_scale,
      g_q.reshape(1, Q_RANK), g_kv.reshape(1, KV_RANK), g_x.reshape(1, D_MODEL))


def _rope_partner(t):
    lane = lax.broadcasted_iota(jnp.int32, t.shape, 1)
    swapped = jnp.where(lane < QK_NOPE + QK_ROPE // 2, pltpu.roll(t, HEAD_PAD - QK_ROPE // 2, 1),
                        pltpu.roll(t, QK_ROPE // 2, 1))
    return jnp.where((lane >= QK_NOPE) & (lane < QK_DIM), swapped, 0.0)


ATT_SCALE = QK_DIM ** -0.5
LOG2E = math.log2(math.e)


HEADS_PER_STEP = 2
ATT_COL0 = POOL_DIM // HEAD_PAD


FWD_TILE = 1024


def _stat_rows(col):
    return jnp.broadcast_to(col, (col.shape[0], LANES)).T[0:8]


def _retile_rows(rows, tq):
    heads, n8, t = rows.shape
    if t == tq:
        return rows
    flat = rows.reshape(heads, n8 // 8, 8, t)[:, :, 0].reshape(heads, -1, 1, tq)
    return jnp.broadcast_to(flat, (heads, flat.shape[1], 8, tq)).reshape(heads, -1, tq)


def _flash_fwd(q, k, v, mix, *, name):
    S = q.shape[0]
    tq = FWD_TILE if S % FWD_TILE == 0 else min(S, 512)
    nq = S // tq
    hs = HEADS_PER_STEP
    wide = hs * HEAD_PAD

    def body(q_ref, k_ref, v_ref, mix_ref, o_ref, lse_ref):
        qi = pl.program_id(1)
        qv = [q_ref[:, a * HEAD_PAD:(a + 1) * HEAD_PAD] for a in range(hs)]

        def update(m, acc, s, v):
            m_new = jnp.maximum(m, jnp.max(s, axis=-1, keepdims=True))
            p = jnp.exp2((s - m_new) * (ATT_SCALE * LOG2E))
            alpha = jnp.exp2((m - m_new) * (ATT_SCALE * LOG2E))
            return m_new, alpha * acc + _dot(p.astype(BF16), v)

        def step(j, carry, masked):
            off = pl.multiple_of(j * tq, tq)
            out = []
            for a in range(hs):
                head = slice(a * HEAD_PAD, (a + 1) * HEAD_PAD)
                s = _dot(qv[a], k_ref[pl.ds(off, tq), head], NT)
                if masked:
                    row = lax.broadcasted_iota(jnp.int32, (tq, tq), 0)
                    col = lax.broadcasted_iota(jnp.int32, (tq, tq), 1)
                    s = jnp.where(col <= row, s, NEG_INF)
                out.append(update(*carry[a], s, v_ref[pl.ds(off, tq), head]))
            return tuple(out)

        one = (jnp.full((tq, 1), NEG_INF, F32), jnp.zeros((tq, HEAD_PAD), F32))
        carry = step(qi, lax.fori_loop(0, qi, lambda j, c: step(j, c, False), (one,) * hs), True)
        for a in range(hs):
            m, acc = carry[a]
            l = acc[:, V_HEAD:V_HEAD + 1]
            o_ref[:, a * HEAD_PAD:(a + 1) * HEAD_PAD] = (acc / l).astype(o_ref.dtype)
            lse_ref[a] = _stat_rows(m * ATT_SCALE + jnp.log(l))

    blk = pl.BlockSpec((tq, wide), lambda h, i: (i, h))
    full = pl.BlockSpec((S, wide), lambda h, i: (0, h))
    return pl.pallas_call(
        body, grid=(MLA_HEADS // hs, nq), in_specs=[blk, full, full, ANY],
        out_specs=[pl.BlockSpec((tq, wide), lambda h, i: (i, ATT_COL0 // hs + h)),
                   pl.BlockSpec((hs, 8, tq), lambda h, i: (h, i, 0))],
        out_shape=[jax.ShapeDtypeStruct(mix.shape, mix.dtype), jax.ShapeDtypeStruct((MLA_HEADS, nq * 8, tq), F32)],
        input_output_aliases={3: 0}, compiler_params=_cp(2), name=name,
    )(q, k, v, mix)


BWD_TILE = 1024
BWD_HEADS_PER_STEP = 1


def _bwd_tile(S):
    return BWD_TILE if S % BWD_TILE == 0 else min(S, 512)


def _attn_delta(dmix, mix, *, name):
    S = mix.shape[0]
    ts = _bwd_tile(S)
    half = MLA_HEADS // 2
    halves = [_rows(ts, half * HEAD_PAD, 1), _rows(ts, half * HEAD_PAD, 2)]

    def body(do0_ref, do1_ref, o0_ref, o1_ref, d_ref):
        for n, (do_ref, o_ref) in enumerate(((do0_ref, o0_ref), (do1_ref, o1_ref))):
            prod = do_ref[...].astype(F32) * o_ref[...].astype(F32)
            for a in range(half):
                d_ref[n * half + a] = _stat_rows(
                    jnp.sum(prod[:, a * HEAD_PAD:(a + 1) * HEAD_PAD], axis=-1, keepdims=True))

    return pl.pallas_call(
        body, grid=(S // ts,), in_specs=halves + halves,
        out_specs=pl.BlockSpec((MLA_HEADS, 8, ts), lambda i: (0, i, 0)),
        out_shape=jax.ShapeDtypeStruct((MLA_HEADS, (S // ts) * 8, ts), F32), compiler_params=_cp(1), name=name,
    )(dmix, dmix, mix, mix)


def _flash_bwd(q, k, v, dmix, lse_rows, delta_rows, *, name):
    S = q.shape[0]
    tq = _bwd_tile(S)
    nq = S // tq
    hs = BWD_HEADS_PER_STEP
    wide = hs * HEAD_PAD

    def body(q_hbm, do_hbm, lse_ref, dl_ref, k_ref, v_ref, dq_hbm, dk_ref, dv_ref, q_all, do_all, dq_all):
        g, j = pl.program_id(0), pl.program_id(1)
        cols = pl.multiple_of(g * wide, wide)

        @pl.when(j == 0)
        def _():
            pltpu.sync_copy(q_hbm.at[:, pl.ds(cols, wide)], q_all)
            pltpu.sync_copy(do_hbm.at[:, pl.ds(POOL_DIM + cols, wide)], do_all)
            dq_all[...] = jnp.zeros_like(dq_all)

        heads = [slice(a * HEAD_PAD, (a + 1) * HEAD_PAD) for a in range(hs)]
        kv = [k_ref[:, a] for a in heads]
        vv = [v_ref[:, a] for a in heads]

        def block(a, keys, rows, lse2, dl, first_query):
            qv, dov = q_all[rows, heads[a]], do_all[rows, heads[a]]
            st = _dot(kv[a][:keys], qv, NT)
            if first_query is not None:
                krow = lax.broadcasted_iota(jnp.int32, st.shape, 0)
                qcol = lax.broadcasted_iota(jnp.int32, st.shape, 1) + first_query
                st = jnp.where(krow <= qcol, st, NEG_INF)
            pt = jnp.exp2(st * (ATT_SCALE * LOG2E) - lse2)
            dst = (pt * (_dot(vv[a][:keys], dov, NT) - dl)).astype(BF16)
            dq_all[rows, heads[a]] += _dot(dst, kv[a][:keys], TN)
            return _dot(dst, qv), _dot(pt.astype(BF16), dov)

        def stats(a, i):
            off8 = pl.multiple_of(i * 8, 8)
            return lse_ref[a, pl.ds(off8, 8), :][0:1] * LOG2E, dl_ref[a, pl.ds(off8, 8), :][0:1]

        def step(i, carry):
            rows = pl.ds(pl.multiple_of(i * tq, tq), tq)
            out = []
            for a in range(hs):
                dk, dv = block(a, tq, rows, *stats(a, i), None)
                out.append((carry[a][0] + dk, carry[a][1] + dv))
            return tuple(out)

        def diagonal():
            half = tq // 2
            out = []
            for a in range(hs):
                lse2, dl = stats(a, j)
                off = pl.multiple_of(j * tq, tq)
                dk0, dv0 = block(a, half, pl.ds(off, half), lse2[:, :half], dl[:, :half], 0)
                dk1, dv1 = block(a, tq, pl.ds(pl.multiple_of(off + half, half), half), lse2[:, half:], dl[:, half:], half)
                zero = jnp.zeros((tq - half, HEAD_PAD), F32)
                out.append((dk1 + jnp.concatenate([dk0, zero], axis=0), dv1 + jnp.concatenate([dv0, zero], axis=0)))
            return tuple(out)

        carry = lax.fori_loop(j + 1, nq, step, diagonal())
        for a in range(hs):
            dk_ref[:, heads[a]] = carry[a][0] * ATT_SCALE
            dv_ref[:, heads[a]] = carry[a][1]

        @pl.when(j == nq - 1)
        def _():
            dq_all[...] = dq_all[...] * ATT_SCALE
            pltpu.sync_copy(dq_all, dq_hbm.at[:, pl.ds(cols, wide)])

    blk = pl.BlockSpec((tq, wide), lambda g, j: (j, g))
    stat = pl.BlockSpec((hs, nq * 8, tq), lambda g, j: (g, 0, 0))
    full = jax.ShapeDtypeStruct((S, MLA_HEADS * HEAD_PAD), F32)
    return pl.pallas_call(
        body, grid=(MLA_HEADS // hs, nq), in_specs=[ANY, ANY, stat, stat, blk, blk], out_specs=[ANY, blk, blk],
        out_shape=[full, full, full],
        scratch_shapes=[pltpu.VMEM((S, wide), BF16), pltpu.VMEM((S, wide), BF16), pltpu.VMEM((S, wide), F32)],
        compiler_params=_cp(2), name=name,
    )(q, dmix, lse_rows, delta_rows, k, v)


MEM_SCALE = MEM_HEAD_DIM ** -0.5


def _xattn_probs(qh, kh):
    s = _dot(qh, kh, NT) * MEM_SCALE
    e = jnp.exp(s - jnp.max(s, axis=-1, keepdims=True))
    return e / jnp.sum(e, axis=-1, keepdims=True)


def _xa_block_fwd(x, kvm, w_q, w_o, g, *, name):
    S = x.shape[0]
    ts = min(S, 512)
    nm = kvm.shape[0]

    def body(x_ref, kv_ref, wq_ref, wo_ref, g_ref, xo_ref, hx_ref, q_ref, o_ref):
        xv = x_ref[...]
        r = lax.rsqrt(jnp.mean(xv * xv, axis=-1, keepdims=True) + RMS_EPS)
        hx = (xv * r * g_ref[...]).astype(BF16)
        hx_ref[...] = hx
        q = _dot(hx, wq_ref[...]).astype(BF16)
        q_ref[...] = q
        for h in range(MEM_HEADS):
            lo, hi = h * MEM_HEAD_DIM, (h + 1) * MEM_HEAD_DIM
            p = _xattn_probs(q[:, lo:hi], kv_ref[:, lo:hi])
            o_ref[:, lo:hi] = _dot(p.astype(BF16), kv_ref[:, D_MODEL + lo:D_MODEL + hi]).astype(o_ref.dtype)
        xo_ref[...] = xv + _dot(o_ref[...], wo_ref[...])

    square = _const((D_MODEL, D_MODEL))
    act = jax.ShapeDtypeStruct((S, D_MODEL), BF16)
    return pl.pallas_call(
        body, grid=(S // ts,),
        in_specs=[_rows(ts, D_MODEL), _const((nm, 2 * D_MODEL)), square, square, _const((1, D_MODEL))],
        out_specs=[_rows(ts, D_MODEL)] * 4, out_shape=[jax.ShapeDtypeStruct((S, D_MODEL), F32), act, act, act],
        compiler_params=_cp(1), name=name,
    )(x, kvm, w_q, w_o, g.reshape(1, D_MODEL))


def _xa_block_bwd(dxo, x, q, kvm, w_q, w_o, g, *, name):
    S = q.shape[0]
    ts = min(S, 512)
    nm = kvm.shape[0]

    def body(dxo_ref, x_ref, q_ref, kv_ref, wq_ref, wo_ref, g_ref, dx_ref, dq_ref, dkv_ref, dg_ref):
        @pl.when(pl.program_id(0) == 0)
        def _():
            dkv_ref[...] = jnp.zeros_like(dkv_ref)
            dg_ref[...] = jnp.zeros_like(dg_ref)

        dxo = dxo_ref[...]
        do = _dot(dxo.astype(BF16), wo_ref[...], NT).astype(BF16)
        for h in range(MEM_HEADS):
            lo, hi = h * MEM_HEAD_DIM, (h + 1) * MEM_HEAD_DIM
            qh, kh, vh = q_ref[:, lo:hi], kv_ref[:, lo:hi], kv_ref[:, D_MODEL + lo:D_MODEL + hi]
            doh = do[:, lo:hi]
            p = _xattn_probs(qh, kh)
            dp = _dot(doh, vh, NT)
            ds = (p * (dp - jnp.sum(dp * p, axis=-1, keepdims=True)) * MEM_SCALE).astype(BF16)
            dq_ref[:, lo:hi] = _dot(ds, kh).astype(dq_ref.dtype)
            dkv_ref[:, lo:hi] += _dot(ds, qh, TN)
            dkv_ref[:, D_MODEL + lo:D_MODEL + hi] += _dot(p.astype(BF16), doh, TN)
        dx, dg = _norm_bwd_epilogue(0)([_dot(dq_ref[...], wq_ref[...], NT)], [x_ref[...], dxo, g_ref[...]])
        dx_ref[...] = dx
        dg_ref[...] += dg

    square = _const((D_MODEL, D_MODEL))
    return pl.pallas_call(
        body, grid=(S // ts,),
        in_specs=[_rows(ts, D_MODEL), _rows(ts, D_MODEL), _rows(ts, D_MODEL), _const((nm, 2 * D_MODEL)), square,
                  square, _const((1, D_MODEL))],
        out_specs=[_rows(ts, D_MODEL), _rows(ts, D_MODEL), _const((nm, 2 * D_MODEL)), _const((1, D_MODEL))],
        out_shape=[jax.ShapeDtypeStruct((S, D_MODEL), F32), jax.ShapeDtypeStruct((S, D_MODEL), BF16),
                   jax.ShapeDtypeStruct((nm, 2 * D_MODEL), F32), jax.ShapeDtypeStruct((1, D_MODEL), F32)],
        compiler_params=_cp(1), name=name,
    )(dxo, x, q, kvm, w_q, w_o, g.reshape(1, D_MODEL))


CONV_HALO = 8


def _sigmoid(x):
    return 0.5 * jnp.tanh(0.5 * x) + 0.5


def _softplus(x):
    return jnp.maximum(x, 0.0) + jnp.log(1.0 + jnp.exp(-jnp.abs(x)))


def _neg_expm1(x):
    series = -x * (1.0 + x * (1.0 / 2) * (1.0 + x * (1.0 / 3) * (1.0 + x * (1.0 / 4) * (1.0 + x * (1.0 / 5)))))
    return jnp.where(x > -0.05, series, 1.0 - jnp.exp(x))


GELU_C = math.sqrt(2.0 / math.pi)


def _gelu(x):
    return 0.5 * x * (1.0 + jnp.tanh(GELU_C * (x + 0.044715 * x * x * x)))


def _gelu_grad(x):
    t = jnp.tanh(GELU_C * (x + 0.044715 * x * x * x))
    return 0.5 * (1.0 + t) + 0.5 * x * (1.0 - t * t) * GELU_C * (1.0 + 3 * 0.044715 * x * x)


def _lru_gates(xc, wr_ref, br, wi_ref, bi, sp, reset):
    xcb = xc.astype(BF16)
    pr, pi = [], []
    for h in range(LRU_HEADS):
        lo, hi = h * LRU_HEAD_DIM, (h + 1) * LRU_HEAD_DIM
        pr.append(_dot(xcb[:, lo:hi], wr_ref[h]))
        pi.append(_dot(xcb[:, lo:hi], wi_ref[h]))
    r = _sigmoid(jnp.concatenate(pr, axis=1) + br)
    ig = _sigmoid(jnp.concatenate(pi, axis=1) + bi)
    log_a = -LRU_C * r * sp
    a = jnp.where(reset, 0.0, jnp.exp(log_a))
    mult = jnp.where(reset, 1.0, jnp.sqrt(jnp.maximum(_neg_expm1(2.0 * log_a), 0.0)))
    return r, ig, a, mult


SUBLANES = 8


def _compose_groups(a, b, reverse):
    n = a.shape[0]
    row = lax.broadcasted_iota(jnp.int32, a.shape, 0) % SUBLANES
    for s in (1, 2, 4):
        inside = (row < SUBLANES - s) if reverse else (row >= s)
        shift = n - s if reverse else s
        a_s = jnp.where(inside, pltpu.roll(a, shift, 0), 1.0)
        b_s = jnp.where(inside, pltpu.roll(b, shift, 0), 0.0)
        b = a * b_s + b
        a = a * a_s
    return a, b


def _chain_groups(a_buf, h_ref, state, reverse):
    groups = a_buf.shape[0] // SUBLANES

    def group(g, h_in):
        off = pl.multiple_of((groups - 1 - g if reverse else g) * SUBLANES, SUBLANES)
        h = a_buf[pl.ds(off, SUBLANES), :] * h_in + h_ref[pl.ds(off, SUBLANES), :]
        h_ref[pl.ds(off, SUBLANES), :] = h
        return jnp.broadcast_to(h[0:1] if reverse else h[SUBLANES - 1:SUBLANES], h.shape)

    return lax.fori_loop(0, groups, group, state, unroll=4)[0:1]


def _lru_fwd(x, g, w_in, reset, conv_w, conv_b, w_r, b_r, w_i, b_i, lam, *, name):
    S = x.shape[0]
    ts = min(S, 512)
    W = D_MODEL

    def body(x_ref, g_ref, win_ref, rs_ref, cw_ref, cb_ref, wr_ref, br_ref, wi_ref, bi_ref, lam_ref,
             hn_ref, z_ref, xc_ref, h_ref, y_ref, a_buf, carry, tail):
        i = pl.program_id(0)

        @pl.when(i == 0)
        def _():
            carry[...] = jnp.zeros_like(carry)
            tail[...] = jnp.zeros_like(tail)

        xv = x_ref[...]
        hn = (xv * lax.rsqrt(jnp.mean(xv * xv, axis=-1, keepdims=True) + RMS_EPS) * g_ref[...]).astype(BF16)
        hn_ref[...] = hn
        z_ref[...] = _dot(hn, win_ref[...])
        xb = z_ref[:, W:]
        xe = jnp.concatenate([tail[...], xb], axis=0)
        tail[...] = xb[ts - CONV_HALO:]
        xc = cb_ref[...] + cw_ref[3:4, :] * xe[CONV_HALO:]
        for kk in range(CONV_WIDTH - 1):
            xc = xc + cw_ref[kk:kk + 1, :] * pltpu.roll(xe, CONV_WIDTH - 1 - kk, 0)[CONV_HALO:]
        xc_ref[...] = xc
        reset = rs_ref[...] > 0.5
        _, ig, a, mult = _lru_gates(xc, wr_ref, br_ref[...], wi_ref, bi_ref[...], _softplus(-lam_ref[...]), reset)
        a_buf[...], h_ref[...] = _compose_groups(a, mult * (ig * xc), False)
        carry[...] = _chain_groups(a_buf, h_ref, jnp.broadcast_to(carry[...], (SUBLANES, W)), False)
        y_ref[...] = (_gelu(z_ref[:, :W]) * h_ref[...]).astype(y_ref.dtype)

    vec = _const((1, W))
    gw = _const((LRU_HEADS, LRU_HEAD_DIM, LRU_HEAD_DIM))
    return pl.pallas_call(
        body, grid=(S // ts,),
        in_specs=[_rows(ts, W), vec, _const((W, 2 * W)), _rows(ts, 1), _const((CONV_WIDTH, W)), vec, gw, vec, gw, vec,
                  vec],
        out_specs=[_rows(ts, W), _rows(ts, 2 * W), _rows(ts, W), _rows(ts, W), _rows(ts, W)],
        out_shape=[jax.ShapeDtypeStruct((S, W), BF16), jax.ShapeDtypeStruct((S, 2 * W), F32),
                   jax.ShapeDtypeStruct((S, W), F32), jax.ShapeDtypeStruct((S, W), F32),
                   jax.ShapeDtypeStruct((S, W), BF16)],
        scratch_shapes=[pltpu.VMEM((ts, W), F32), pltpu.VMEM((1, W), F32), pltpu.VMEM((CONV_HALO, W), F32)],
        compiler_params=_cp(1), name=name,
    )(x, g.reshape(1, W), w_in, reset, conv_w, conv_b, w_r, b_r, w_i, b_i, lam)


def _lru_bwd(dxo, w_out, z, xc, hseq, reset, w_r, b_r, w_i, b_i, lam, *, name):
    S = z.shape[0]
    ts = min(S, 512)
    nt = S // ts
    nh = ts // CONV_HALO
    W = D_MODEL

    def body(dxo_ref, wout_ref, gate_ref, xc_ref, h_ref, hh_ref, rs_ref, wr_ref, br_ref, wi_ref, bi_ref, lam_ref,
             dg_ref, dxc_ref, dpr_ref, dpi_ref, acc_ref, a_buf, dh_buf, carry):
        i = pl.program_id(0)
        tile = nt - 1 - i

        @pl.when(i == 0)
        def _():
            carry[...] = jnp.zeros_like(carry)
            acc_ref[...] = jnp.zeros_like(acc_ref)

        xc = xc_ref[...]
        lam_v = lam_ref[...]
        sp = _softplus(-lam_v)
        reset = rs_ref[...] > 0.5
        r, ig, a, mult = _lru_gates(xc, wr_ref, br_ref[...], wi_ref, bi_ref[...], sp, reset)
        gate = gate_ref[...]
        dyv = _dot(dxo_ref[...].astype(BF16), wout_ref[...], NT)
        h = h_ref[...]
        dg_ref[...] = (dyv * h * _gelu_grad(gate)).astype(dg_ref.dtype)
        last_row = lax.broadcasted_iota(jnp.int32, a.shape, 0) == ts - 1
        a_buf[...], dh_buf[...] = _compose_groups(jnp.where(last_row, 1.0, pltpu.roll(a, ts - 1, 0)),
                                                  dyv * _gelu(gate), True)
        _chain_groups(a_buf, dh_buf, jnp.broadcast_to(carry[...], (SUBLANES, W)), True)
        dh = dh_buf[...]
        carry[...] = a[0:1] * dh[0:1]
        hh = jnp.where(tile > 0, hh_ref[...], 0.0)
        h_prev = pltpu.roll(jnp.concatenate([hh, h], axis=0), 1, 0)[CONV_HALO:]
        da = dh * h_prev
        bx = ig * xc
        dmult = dh * bx
        dbx = dh * mult
        di = dbx * xc
        dlog_a = jnp.where(reset, 0.0, da * a - dmult * a * a / jnp.maximum(mult, 1e-30))
        dr = dlog_a * (-LRU_C) * sp
        dpre_r = dr * r * (1.0 - r)
        dpre_i = di * ig * (1.0 - ig)
        dprb, dpib = dpre_r.astype(BF16), dpre_i.astype(BF16)
        dpr_ref[...] = dprb
        dpi_ref[...] = dpib
        back = []
        for hd in range(LRU_HEADS):
            lo, hi = hd * LRU_HEAD_DIM, (hd + 1) * LRU_HEAD_DIM
            back.append(_dot(dprb[:, lo:hi], wr_ref[hd], NT) + _dot(dpib[:, lo:hi], wi_ref[hd], NT))
        dxc_ref[...] = dbx * ig + jnp.concatenate(back, axis=1)
        dlam = jnp.sum(dlog_a * (-LRU_C) * r, axis=0, keepdims=True) * (-_sigmoid(-lam_v))
        acc_ref[0:1, :] += jnp.sum(dpre_r, axis=0, keepdims=True)
        acc_ref[1:2, :] += jnp.sum(dpre_i, axis=0, keepdims=True)
        acc_ref[2:3, :] += dlam

    rev = lambda cb: pl.BlockSpec((ts, W), lambda i: (nt - 1 - i, cb))
    vec = _const((1, W))
    gw = _const((LRU_HEADS, LRU_HEAD_DIM, LRU_HEAD_DIM))
    return pl.pallas_call(
        body, grid=(nt,),
        in_specs=[rev(0), _const((W, W)), rev(0), rev(0), rev(0),
                  pl.BlockSpec((CONV_HALO, W), lambda i: (jnp.maximum((nt - 1 - i) * nh - 1, 0), 0)),
                  pl.BlockSpec((ts, 1), lambda i: (nt - 1 - i, 0)), gw, vec, gw, vec, vec],
        out_specs=[rev(0), rev(0), rev(0), rev(0), _const((8, W))],
        out_shape=[jax.ShapeDtypeStruct((S, W), BF16), jax.ShapeDtypeStruct((S, W), F32),
                   jax.ShapeDtypeStruct((S, W), BF16), jax.ShapeDtypeStruct((S, W), BF16),
                   jax.ShapeDtypeStruct((8, W), F32)],
        scratch_shapes=[pltpu.VMEM((ts, W), F32), pltpu.VMEM((ts, W), F32), pltpu.VMEM((1, W), F32)],
        compiler_params=_cp(1), name=name,
    )(dxo, w_out, z, xc, hseq, hseq, reset, w_r, b_r, w_i, b_i, lam)


def _conv_bwd(dxc, z, conv_w, *, name):
    S = dxc.shape[0]
    ts = min(S, 512)
    nh = ts // CONV_HALO
    last = S // CONV_HALO - 1
    W = D_MODEL
    n = ts + CONV_HALO

    def body(d_ref, dn_ref, xb_ref, xp_ref, cw_ref, dxb_ref, acc_ref):
        i = pl.program_id(0)

        @pl.when(i == 0)
        def _():
            acc_ref[...] = jnp.zeros_like(acc_ref)

        d = d_ref[...]
        de = jnp.concatenate([d, jnp.where(i < pl.num_programs(0) - 1, dn_ref[...], 0.0)], axis=0)
        xe = jnp.concatenate([jnp.where(i > 0, xp_ref[...], 0.0), xb_ref[...]], axis=0)
        dxb = cw_ref[3:4, :] * d
        acc_ref[3:4, :] += jnp.sum(d * xe[CONV_HALO:], axis=0, keepdims=True)
        for kk in range(CONV_WIDTH - 1):
            sh = CONV_WIDTH - 1 - kk
            dxb = dxb + cw_ref[kk:kk + 1, :] * pltpu.roll(de, n - sh, 0)[:ts]
            acc_ref[kk:kk + 1, :] += jnp.sum(d * pltpu.roll(xe, sh, 0)[CONV_HALO:], axis=0, keepdims=True)
        dxb_ref[...] = dxb.astype(dxb_ref.dtype)
        acc_ref[4:5, :] += jnp.sum(d, axis=0, keepdims=True)

    return pl.pallas_call(
        body, grid=(S // ts,),
        in_specs=[_rows(ts, W), pl.BlockSpec((CONV_HALO, W), lambda i: (jnp.minimum((i + 1) * nh, last), 0)),
                  _rows(ts, W, 1), pl.BlockSpec((CONV_HALO, W), lambda i: (jnp.maximum(i * nh - 1, 0), 1)),
                  _const((CONV_WIDTH, W))],
        out_specs=[_rows(ts, W), _const((8, W))],
        out_shape=[jax.ShapeDtypeStruct((S, W), BF16), jax.ShapeDtypeStruct((8, W), F32)],
        compiler_params=_cp(1), name=name,
    )(dxc, dxc, z, z, conv_w)


def _loss_head(x, g, target, *, name):
    S, D = x.shape
    ts = _row_tile(S)

    def body(x_ref, g_ref, t_ref, dx_ref, dg_ref, l_ref):
        @pl.when(pl.program_id(0) == 0)
        def _():
            dg_ref[...] = jnp.zeros_like(dg_ref)
            l_ref[...] = jnp.zeros_like(l_ref)

        xv = x_ref[...]
        r = lax.rsqrt(jnp.mean(xv * xv, axis=-1, keepdims=True) + RMS_EPS)
        n = xv * r
        err = n * g_ref[...] - t_ref[...]
        l_ref[...] += 0.5 * jnp.sum(jnp.sum(err * err, axis=-1, keepdims=True) * (1.0 / D), axis=0, keepdims=True)
        dy = err * (1.0 / D)
        dn = dy * g_ref[...]
        dx_ref[...] = r * (dn - n * jnp.mean(dn * n, axis=-1, keepdims=True))
        dg_ref[...] += jnp.sum(dy * n, axis=0, keepdims=True)

    return pl.pallas_call(
        body, grid=(S // ts,), in_specs=[_rows(ts, D), _const((1, D)), _rows(ts, D)],
        out_specs=[_rows(ts, D), _const((1, D)), _const((8, LANES))],
        out_shape=[jax.ShapeDtypeStruct((S, D), F32), jax.ShapeDtypeStruct((1, D), F32),
                   jax.ShapeDtypeStruct((8, LANES), F32)],
        compiler_params=_cp(1), name=name,
    )(x, g.reshape(1, D), target)


def _adamw(w, ga, gb, m, v, *, name):
    shape = w.shape
    cols = shape[-1]
    rows = w.size // cols
    br = rows
    if rows * cols * 4 > (1 << 20):
        br = max(d for d in range(8, rows + 1, 8) if rows % d == 0 and d * cols * 4 <= (1 << 20))

    def body(w_ref, ga_ref, gb_ref, m_ref, v_ref, g_ref, d_ref, mo_ref, vo_ref):
        gv = ga_ref[...] + gb_ref[...]
        g_ref[...] = gv
        mn = ADAM_B1 * m_ref[...] + (1.0 - ADAM_B1) * gv
        vn = ADAM_B2 * v_ref[...] + (1.0 - ADAM_B2) * (gv * gv)
        m_hat = mn / (1.0 - ADAM_B1 ** ADAM_STEP)
        v_hat = vn / (1.0 - ADAM_B2 ** ADAM_STEP)
        d_ref[...] = -ADAM_LR * (m_hat / (jnp.sqrt(v_hat) + ADAM_EPS) + ADAM_WD * w_ref[...])
        mo_ref[...] = mn
        vo_ref[...] = vn

    spec = _rows(br, cols)
    outs = pl.pallas_call(
        body, grid=(rows // br,), in_specs=[spec] * 5, out_specs=[spec] * 4,
        out_shape=[jax.ShapeDtypeStruct((rows, cols), F32)] * 4, compiler_params=_cp(1), name=name,
    )(*[t.reshape(rows, cols) for t in (w, ga, gb, m, v)])
    return [o.reshape(shape) for o in outs]


def _pad_heads(w, width):
    k = w.shape[0]
    return jnp.pad(w.reshape(k, MLA_HEADS, width), ((0, 0), (0, 0), (0, HEAD_PAD - width))).reshape(k, -1)


def _unpad_heads(w, width):
    k = w.shape[0]
    return w.reshape(k, MLA_HEADS, HEAD_PAD)[:, :, :width].reshape(k, MLA_HEADS * width)


def _rope_tables(positions):
    inv_freq = ROPE_BASE ** (-jnp.arange(0, QK_ROPE, 2, dtype=F32) / QK_ROPE)
    none = jnp.zeros((QK_NOPE,), F32)
    freq = jnp.concatenate([none, inv_freq, inv_freq, none[:HEAD_PAD - QK_DIM]])
    sign = jnp.concatenate([none, -jnp.ones_like(inv_freq), jnp.ones_like(inv_freq), none[:HEAD_PAD - QK_DIM]])
    ang = positions.astype(F32)[:, None] * freq
    return jnp.cos(ang), jnp.sin(ang) * sign


def _memory_block(x, mem, W, layer, tag):
    mn = _rms(mem, W["xa_norm_mem"][layer], name=f"{tag}_xa_norm_mem")
    kvm = _mm(mn, [(W["xa_w_kv"][layer], 0, 0)], _first, [(2 * D_MODEL, BF16, 0)], tn=2 * D_MODEL, nj=1,
              name=f"{tag}_xa_kv")[0]
    xo, hx, qx, o = _xa_block_fwd(x, kvm, W["xa_w_q"][layer], W["xa_w_o"][layer], W["xa_norm_x"][layer],
                                  name=f"{tag}_xa_fwd")
    return xo, (x, hx, qx, mn, kvm, o)


def _memory_block_bwd(dxo, mem, W, layer, saved, tag, grads):
    x, hx, qx, mn, kvm, o = saved
    wq, wkv, wo = W["xa_w_q"][layer], W["xa_w_kv"][layer], W["xa_w_o"][layer]
    grads["xa_w_o"][layer] = _owner_major(_mm_tn(o, dxo, name=f"{tag}_xa_dwo"), 0)
    dx, dqx, dkvm, dg = _xa_block_bwd(dxo, x, qx, kvm, wq, wo, W["xa_norm_x"][layer], name=f"{tag}_xa_bwd")
    grads["xa_w_q"][layer] = _owner_major(_mm_tn(hx, dqx, name=f"{tag}_xa_dwq"), 0)
    grads["xa_norm_x"][layer] = dg[0]
    dmn = _mm(dkvm, [(wkv, 0, 0)], _first, [(D_MODEL, F32, 0)], nt=True, tn=D_MODEL, nj=1, name=f"{tag}_xa_dmn")[0]
    grads["xa_w_kv"][layer] = _mm_tn_owners(mn, [dkvm], name=f"{tag}_xa_dwkv")
    grads["xa_norm_mem"][layer] = _norm_gain_grad(mem, dmn, name=f"{tag}_xa_norm_mem_bwd")[0]
    return dx


FF_TN = D_FF // 2

def _silu_mul(accs, extras):
    g, u = accs
    return [g * _sigmoid(g) * u, g, u]


def _silu_mul_bwd(accs, extras):
    da = accs[0]
    g, u = extras[0].astype(F32), extras[1].astype(F32)
    sg = _sigmoid(g)
    silu = g * sg
    return [da * u * (sg + silu * (1.0 - sg)), da * silu]


def _ffn_block(x, W, layer, tag):
    hf = _rms(x, W["ffn_norm"][layer], name=f"{tag}_ffn_norm")
    wgu, wd = W["ffn_w_gate_up"][layer], W["ffn_w_down"][layer]
    act, g, u = _mm(hf, [(wgu, 0, 0), (wgu, 0, 2)], _silu_mul, [(D_FF, BF16, 0)] * 3, tn=FF_TN, nj=2,
                    name=f"{tag}_ffn_up")
    xo = _mm(act, [(wd, 0, 0)], _add_res, [(D_MODEL, F32, 0)], extras=[(x, 0)], tn=D_MODEL, nj=1,
             name=f"{tag}_ffn_down")[0]
    return xo, (x, hf, act, g, u)


def _ffn_block_bwd(dxo, W, layer, saved, tag, grads):
    x, hf, act, g, u = saved
    wgu, wd = W["ffn_w_gate_up"][layer], W["ffn_w_down"][layer]
    dg, du = _mm(dxo, [(wd, 0, 0)], _silu_mul_bwd, [(D_FF, BF16, 0)] * 2, nt=True, extras=[(g, 0), (u, 0)], tn=FF_TN,
                 nj=2, name=f"{tag}_ffn_dact")
    grads["ffn_w_down"][layer] = _owner_major(_mm_tn(act, dxo, tk=FF_TN, name=f"{tag}_ffn_dwd"), 0)
    dx, dgn = _mm(dg, [(wgu, 0, 0)], _norm_bwd_epilogue(0), [(D_MODEL, F32, 0)], nt=True, also=(du, (wgu, 0, 1)),
                  extras=[(x, 0), (dxo, 0)], rows=[W["ffn_norm"][layer].reshape(1, D_MODEL)],
                  sums=[D_MODEL], tn=D_MODEL, nj=1, name=f"{tag}_ffn_dhf")
    grads["ffn_w_gate_up"][layer] = _mm_tn_owners(hf, [dg, du], name=f"{tag}_ffn_dwgu")
    grads["ffn_norm"][layer] = dgn[0]
    return dx


def _even_block(x, tabs, W, tag):
    ctab, stab = tabs
    w_in = W["ev_w_in"][0]
    zero = jnp.zeros((D_MODEL, QK_NOPE), BF16)
    w_in_pad = jnp.concatenate([w_in[:, :896], zero, w_in[:, 896:], zero[:, :HEAD_PAD - QK_DIM]], axis=1)
    w_q_pad = _pad_heads(W["ev_w_q_up"][0], QK_DIM)
    wkv = W["ev_w_kv_up"][0].reshape(KV_RANK, MLA_HEADS, QK_NOPE + V_HEAD)
    w_kv_pad = jnp.concatenate([_pad_heads(wkv[:, :, :QK_NOPE].reshape(KV_RANK, -1), QK_NOPE),
                                _pad_heads(wkv[:, :, QK_NOPE:].reshape(KV_RANK, -1), V_HEAD)], axis=1)
    w_out = W["ev_w_out"][0]
    w_att = jnp.pad(w_out[POOL_DIM:].reshape(MLA_HEADS, V_HEAD, D_MODEL), ((0, 0), (0, HEAD_PAD - V_HEAD), (0, 0)))
    w_out_pad = jnp.concatenate([w_out[:POOL_DIM], w_att.reshape(MLA_HEADS * HEAD_PAD, D_MODEL)], axis=0)
    pool_w = W["ev_pool_w"][0].astype(BF16)
    pool_scale = W["ev_pool_scale"]

    h, z, mix, pooled, cqn, ckvn, q_rot, k_cat, v_pad = _even_front(
        x, W["ev_norm"][0], w_in_pad, pool_w, pool_scale, W["ev_q_norm"][0], w_q_pad, W["ev_kv_norm"][0], w_kv_pad,
        ctab, stab, name=f"{tag}_front")
    mix, lse = _flash_fwd(q_rot, k_cat, v_pad, mix, name=f"{tag}_attn")
    xo = _mm(mix, [(w_out_pad, 0, 0)], _add_res, [(D_MODEL, F32, 0)], extras=[(x, 0)], tn=D_MODEL, nj=1,
             name=f"{tag}_out")[0]
    saved = (x, h, z, pooled, cqn, ckvn, q_rot, k_cat, v_pad, lse, mix,
             (w_in_pad, w_q_pad, w_kv_pad, w_out_pad, pool_w, pool_scale))
    return xo, saved


def _even_out_grad(dxo, saved, tag):
    mix = saved[10]
    dw_out_pad = _mm_tn(mix, dxo, tk=MIX_DIM // 3, name=f"{tag}_dw_out")
    datt = dw_out_pad[POOL_DIM:].reshape(MLA_HEADS, HEAD_PAD, D_MODEL)[:, :V_HEAD].reshape(-1, D_MODEL)
    return [_owner_major(jnp.concatenate([dw_out_pad[:POOL_DIM], datt], axis=0), 0)]


def _even_block_bwd(dxo, tabs, W, saved, tag, grads, token=None):
    ctab, stab = tabs
    x, h, z, pooled, cqn, ckvn, q_rot, k_cat, v_pad, lse, mix, wts = saved
    w_in_pad, w_q_pad, w_kv_pad, w_out_pad, pool_w, pool_scale = wts
    if token is not None:
        w_out_pad = w_out_pad + token[0:1, 0:1].astype(BF16)
    dmix = _mm(dxo, [(w_out_pad, 0, 0)], _first, [(MIX_DIM, BF16, 0)], nt=True, tn=MIX_DIM, nj=1,
               name=f"{tag}_dmix")[0]
    delta = _attn_delta(dmix, mix, name=f"{tag}_delta")
    dq_rot, dk_cat, dv_pad = _flash_bwd(q_rot, k_cat, v_pad, dmix, _retile_rows(lse, delta.shape[2]), delta,
                                        name=f"{tag}_attn_bwd")
    dx, dq_pad, dz, dypre, dgq, dgkv, dscale, dgn = _even_back(
        dq_rot, dk_cat, dv_pad, dmix, pooled, z, x, dxo, ctab, stab, w_q_pad, w_kv_pad, w_in_pad, pool_w, pool_scale,
        W["ev_q_norm"][0], W["ev_kv_norm"][0], W["ev_norm"][0], name=f"{tag}_back")
    grads["ev_q_norm"], grads["ev_kv_norm"], grads["ev_pool_scale"], grads["ev_norm"] = dgq, dgkv, dscale, dgn
    dw_q_pad = _mm_tn(cqn, dq_pad, name=f"{tag}_dw_q_up")
    grads["ev_w_q_up"] = [_owner_major(_unpad_heads(dw_q_pad, QK_DIM), 1)]
    dwk = _unpad_heads(_mm_tn(ckvn, dk_cat, name=f"{tag}_dw_k_up"), QK_NOPE).reshape(KV_RANK, MLA_HEADS, QK_NOPE)
    dwv = _unpad_heads(_mm_tn(ckvn, dv_pad, name=f"{tag}_dw_v_up"), V_HEAD).reshape(KV_RANK, MLA_HEADS, V_HEAD)
    grads["ev_w_kv_up"] = [_owner_major(jnp.concatenate([dwk, dwv], axis=2).reshape(KV_RANK, -1), 1)]
    grads["ev_pool_w"] = _mm_tn_grouped(pooled, dypre, 4, POOL_GROUP, name=f"{tag}_dpool_w")[None]
    dw_in_pad = _mm_tn(h, dz, name=f"{tag}_dw_in")
    grads["ev_w_in"] = [_owner_major(jnp.concatenate([dw_in_pad[:, :896], dw_in_pad[:, 960:992]], axis=1), 0)]
    return dx


def _odd_block(x, reset, W, tag):
    w_r, w_i = W["od_w_rgate"][0], W["od_w_igate"][0]
    vecs = [W[n].reshape(1, D_MODEL) for n in ("od_conv_b", "od_b_rgate", "od_b_igate", "od_lambda")]
    h, z, xc, hseq, y = _lru_fwd(x, W["od_norm"][0], W["od_w_in"][0], reset, W["od_conv_w"][0], vecs[0], w_r,
                                 vecs[1], w_i, vecs[2], vecs[3], name=f"{tag}_lru")
    xo = _mm(y, [(W["od_w_out"][0], 0, 0)], _add_res, [(D_MODEL, F32, 0)], extras=[(x, 0)], tn=D_MODEL, nj=1,
             name=f"{tag}_out")[0]
    return xo, (x, h, z, xc, hseq, y, vecs)


def _odd_block_bwd(dxo, reset, W, saved, tag, grads):
    x, h, z, xc, hseq, y, vecs = saved
    w_r, w_i = W["od_w_rgate"][0], W["od_w_igate"][0]
    grads["od_w_out"] = [_owner_major(_mm_tn(y, dxo, name=f"{tag}_dw_out"), 0)]
    dgate, dxc, dpr, dpi, acc = _lru_bwd(dxo, W["od_w_out"][0], z, xc, hseq, reset, w_r, vecs[1], w_i, vecs[2],
                                         vecs[3], name=f"{tag}_lru_bwd")
    grads["od_b_rgate"], grads["od_b_igate"], grads["od_lambda"] = acc[0:1], acc[1:2], acc[2:3]
    grads["od_w_rgate"] = [_owner_major(_mm_tn_grouped(xc, dpr, LRU_HEADS, LRU_HEAD_DIM, name=f"{tag}_dw_rgate"), 1)]
    grads["od_w_igate"] = [_owner_major(_mm_tn_grouped(xc, dpi, LRU_HEADS, LRU_HEAD_DIM, name=f"{tag}_dw_igate"), 1)]
    dxb, cacc = _conv_bwd(dxc, z, W["od_conv_w"][0], name=f"{tag}_conv_bwd")
    grads["od_conv_w"], grads["od_conv_b"] = cacc[None, 0:4], cacc[4:5]
    dz = jnp.concatenate([dgate, dxb], axis=1)
    grads["od_w_in"] = [_mm_tn_owners(h, [dz], name=f"{tag}_dw_in")]
    dx, dgn = _mm(dz, [(W["od_w_in"][0], 0, 0)], _norm_bwd_epilogue(0), [(D_MODEL, F32, 0)], nt=True,
                  extras=[(x, 0), (dxo, 0)], rows=[W["od_norm"][0].reshape(1, D_MODEL)], sums=[D_MODEL], tn=D_MODEL,
                  nj=1, name=f"{tag}_dh")
    grads["od_norm"] = dgn
    return dx


def _local_step(x, mem, positions, target, W, later_weights=None, exchange_earlier=None):
    tabs = _rope_tables(positions)
    reset = (positions == 0).astype(F32)[:, None]
    grads = {n: [None, None] for n in ("xa_norm_x", "xa_norm_mem", "xa_w_q", "xa_w_kv", "xa_w_o", "ffn_norm",
                                       "ffn_w_gate_up", "ffn_w_down")}
    x1, s_even = _even_block(x, tabs, W, "l0_even")
    if later_weights is not None:
        W = {**W, **later_weights(x1)}
    x2, s_xa0 = _memory_block(x1, mem, W, 0, "l0")
    x3, s_ff0 = _ffn_block(x2, W, 0, "l0")
    x4, s_odd = _odd_block(x3, reset, W, "l1_odd")
    x5, s_xa1 = _memory_block(x4, mem, W, 1, "l1")
    x6, s_ff1 = _ffn_block(x5, W, 1, "l1")
    d, dgf, loss = _loss_head(x6, W["final_norm"], target, name="loss_head")
    grads["final_norm"] = dgf[0]
    d = _ffn_block_bwd(d, W, 1, s_ff1, "l1", grads)
    d = _memory_block_bwd(d, mem, W, 1, s_xa1, "l1", grads)
    d = _odd_block_bwd(d, reset, W, s_odd, "l1_odd", grads)
    d = _ffn_block_bwd(d, W, 0, s_ff0, "l0", grads)
    d = _memory_block_bwd(d, mem, W, 0, s_xa0, "l0", grads)
    grads["ev_w_out"] = _even_out_grad(d, s_even, "l0_even")
    token = exchange_earlier(grads) if exchange_earlier is not None else None
    d = _even_block_bwd(d, tabs, W, s_even, "l0_even", grads, token)
    big = {n: grads.pop(n) for n in MATMUL_WEIGHTS}
    for n, v in grads.items():
        if isinstance(v, list):
            grads[n] = jnp.stack(v)
    return loss[0, 0], d, big, grads


WEIGHTS = ("ev_norm", "ev_w_in", "ev_pool_w", "ev_pool_scale", "ev_q_norm", "ev_w_q_up", "ev_kv_norm", "ev_w_kv_up",
           "ev_w_out", "od_norm", "od_w_in", "od_conv_w", "od_conv_b", "od_w_rgate", "od_b_rgate", "od_w_igate",
           "od_b_igate", "od_lambda", "od_w_out", "xa_norm_x", "xa_norm_mem", "xa_w_q", "xa_w_kv", "xa_w_o",
           "ffn_norm", "ffn_w_gate_up", "ffn_w_down", "final_norm")
SHARD_AXIS = {"ev_w_in": 1, "ev_w_q_up": 2, "ev_w_kv_up": 2, "ev_w_out": 1, "od_norm": 1, "od_w_in": 2,
              "od_conv_w": 2, "od_conv_b": 1, "od_w_rgate": 2, "od_b_rgate": 1, "od_w_igate": 2, "od_b_igate": 1,
              "od_lambda": 1, "od_w_out": 1, "xa_w_q": 1, "xa_w_kv": 2, "xa_w_o": 1, "ffn_w_gate_up": 2,
              "ffn_w_down": 1}
MATMUL_WEIGHTS = ("ev_w_in", "ev_w_q_up", "ev_w_kv_up", "ev_w_out", "od_w_in", "od_w_rgate", "od_w_igate",
                  "od_w_out", "xa_w_q", "xa_w_kv", "xa_w_o", "ffn_w_gate_up", "ffn_w_down")
SMALL_SHARDED = tuple(n for n in WEIGHTS if n in SHARD_AXIS and n not in MATMUL_WEIGHTS)
REPLICATED = tuple(n for n in WEIGHTS if n not in SHARD_AXIS)


def _pack(parts, quantum):
    flat = jnp.concatenate([p.reshape(-1) for p in parts])
    pad = (-flat.shape[0]) % quantum
    return jnp.pad(flat, (0, pad)).reshape(-1, LANES)


def _unpack(flat, shapes):
    out, off = [], 0
    for shape in shapes:
        size = math.prod(shape)
        out.append(flat[off:off + size].reshape(shape))
        off += size
    return out


def _run_copies(local, remote, send_sems, recv_sems, local_sems):
    locals_ = [pltpu.make_async_copy(src, dst, local_sems.at[n]) for n, (src, dst) in enumerate(local)]
    for cp in locals_:
        cp.start()
    sends = [pltpu.make_async_remote_copy(src_ref=src, dst_ref=dst, send_sem=send_sems.at[k, n],
                                          recv_sem=recv_sems.at[k, n], device_id=dev, device_id_type=MESH)
             for (k, n, src, dst, _, dev) in remote]
    for cp in sends:
        cp.start()
    for (k, n, src, _, arrival, dev) in remote:
        pltpu.make_async_remote_copy(src_ref=src, dst_ref=arrival, send_sem=send_sems.at[k, n],
                                     recv_sem=recv_sems.at[k, n], device_id=dev, device_id_type=MESH).wait_recv()
    for cp in sends:
        cp.wait_send()
    for cp in locals_:
        cp.wait()


def _chip_peers(x, y):
    return [(1 - x, y), (x, 1 - y), (1 - x, 1 - y)]


def _owner_block(ref, axis, q):
    size = ref.shape[axis] // N_CHIPS
    idx = [slice(None)] * len(ref.shape)
    idx[axis] = pl.ds(q * size, size)
    return ref.at[tuple(idx)]


def _comm_call(body, ins, out_shapes, n_items, n_peers, *, name):
    return pl.pallas_call(
        body, in_specs=[ANY] * len(ins), out_specs=[ANY] * len(out_shapes), out_shape=out_shapes,
        scratch_shapes=[pltpu.SemaphoreType.DMA((n_peers, n_items)), pltpu.SemaphoreType.DMA((n_peers, n_items)),
                        pltpu.SemaphoreType.DMA((n_items,))],
        name=name,
    )(*ins)


def _gather_chips(shards, axes, *, name):
    n = len(shards)
    full = [jax.ShapeDtypeStruct(tuple(d * (N_CHIPS if a == ax else 1) for a, d in enumerate(s.shape)), s.dtype)
            for s, ax in zip(shards, axes)]

    def body(*refs):
        srcs, dsts = refs[:n], refs[n:2 * n]
        x, y, c = lax.axis_index("x"), lax.axis_index("y"), lax.axis_index("c")
        me = 2 * x + y
        local = [(srcs[i], _owner_block(dsts[i], axes[i], me)) for i in range(n)]
        remote = [(k, i, srcs[i], _owner_block(dsts[i], axes[i], me), _owner_block(dsts[i], axes[i], 2 * px + py),
                   (px, py, c))
                  for k, (px, py) in enumerate(_chip_peers(x, y)) for i in range(n)]
        _run_copies(local, remote, *refs[2 * n:])

    return _comm_call(body, shards, full, n, 3, name=name)


HBM = pl.BlockSpec(memory_space=pltpu.HBM)
SEM = pl.BlockSpec(memory_space=pltpu.SEMAPHORE)
DATAFLOW = pltpu.SideEffectType.DATAFLOW_SIDE_EFFECTING


def _gather_plan(axes):
    return lambda srcs, lands, me, peer: [
        (srcs[i], _owner_block(lands[i], ax, me), _owner_block(lands[i], ax, peer)) for i, ax in enumerate(axes)]


def _exchange_plan(where):
    return lambda srcs, lands, me, peer: [
        (srcs[i].at[peer], lands[n].at[me, l], lands[n].at[peer, l]) for i, (n, l) in enumerate(where)]


def _split_peers(sibling):
    x, y, c = lax.axis_index("x"), lax.axis_index("y"), lax.axis_index("c")
    peers = [((px, py, c), 2 * px + py) for px, py in _chip_peers(x, y)]
    return 2 * x + y, peers + ([((x, y, 1 - c), 2 * x + y)] if sibling else [])


def _split_start(srcs, lands, plan, *, sibling=False, name):
    ns, nl = len(srcs), len(lands)
    nsem = (3 + sibling) * len(plan(list(srcs), list(lands), 0, 0))

    def body(*refs):
        src_refs, land_refs = refs[:ns], refs[ns:ns + nl]
        send_sems, recv_sems = refs[ns + nl:ns + nl + nsem], refs[ns + nl + nsem:ns + nl + 2 * nsem]
        me, peers = _split_peers(sibling)
        n = 0
        for device, chip in peers:
            for src, dst, _ in plan(src_refs, land_refs, me, chip):
                pltpu.make_async_remote_copy(src_ref=src, dst_ref=dst, send_sem=send_sems[n], recv_sem=recv_sems[n],
                                             device_id=device, device_id_type=MESH).start()
                n += 1
        refs[-1][...] = jnp.zeros_like(refs[-1])

    arrays = list(srcs) + list(lands)
    out = pl.pallas_call(
        body, name=name, in_specs=[HBM] * (ns + nl),
        out_specs=[SEM] * (2 * nsem) + [HBM] * (ns + nl) + [pl.BlockSpec(memory_space=pltpu.VMEM)],
        out_shape=[pltpu.SemaphoreType.DMA(())] * (2 * nsem) + [pltpu.HBM(a.shape, a.dtype) for a in arrays]
        + [jax.ShapeDtypeStruct((8, LANES), F32)],
        input_output_aliases={i: 2 * nsem + i for i in range(ns + nl)},
        compiler_params=pltpu.CompilerParams(has_side_effects=DATAFLOW),
    )(*[pltpu.with_memory_space_constraint(a, pltpu.HBM) for a in arrays])
    sems, rest = out[:2 * nsem], out[2 * nsem:]
    return sems[:nsem], sems[nsem:], rest[:ns], rest[ns:ns + nl], rest[-1]


def _split_wait(handle, after, plan, *, sibling=False, name):
    send_sems, recv_sems, srcs, lands, _ = handle
    ns, nl, nsem = len(srcs), len(lands), len(send_sems)

    def body(*refs):
        src_refs, land_refs = refs[:ns], refs[ns:ns + nl]
        send_refs, recv_refs = refs[ns + nl:ns + nl + nsem], refs[ns + nl + nsem:ns + nl + 2 * nsem]
        me, peers = _split_peers(sibling)
        n = 0
        for device, chip in peers:
            for src, _, arrival in plan(src_refs, land_refs, me, chip):
                cp = pltpu.make_async_remote_copy(src_ref=src, dst_ref=arrival, send_sem=send_refs[n],
                                                  recv_sem=recv_refs[n], device_id=device, device_id_type=MESH)
                cp.wait_send()
                cp.wait_recv()
                n += 1

    out = pl.pallas_call(
        body, name=name, in_specs=[HBM] * (ns + nl) + [SEM] * (2 * nsem) + [ANY], out_specs=[HBM] * (ns + nl),
        out_shape=[pltpu.HBM(a.shape, a.dtype) for a in list(srcs) + list(lands)],
        input_output_aliases={i: i for i in range(ns + nl)},
        compiler_params=pltpu.CompilerParams(has_side_effects=DATAFLOW),
    )(*srcs, *lands, *send_sems, *recv_sems, after)
    return out[ns:]


def _exchange_sibling(arrays, *, name):
    n = len(arrays)

    def body(*refs):
        x, y, c = lax.axis_index("x"), lax.axis_index("y"), lax.axis_index("c")
        remote = [(0, i, refs[i], refs[n + i], refs[n + i], (x, y, 1 - c)) for i in range(n)]
        _run_copies([], remote, *refs[2 * n:])

    return _comm_call(body, arrays, [jax.ShapeDtypeStruct(a.shape, a.dtype) for a in arrays], n, 1, name=name)


def _sum_slots(r, *, token=None, name):
    shape = r.shape[1:]
    cols = shape[-1]
    rows = math.prod(shape) // cols
    tr = max(d for d in range(8, rows + 1, 8) if rows % d == 0 and d * cols * 16 <= (4 << 20))

    def body(r_ref, *refs):
        total = ((r_ref[0] + r_ref[1]) + r_ref[2]) + r_ref[3]
        refs[-1][...] = total if token is None else total + refs[0][0:1, 0:1]

    in_specs = [pl.BlockSpec((N_CHIPS, tr, cols), lambda i: (0, i, 0))]
    in_specs += [] if token is None else [_const((8, LANES))]
    return pl.pallas_call(
        body, grid=(rows // tr,), in_specs=in_specs,
        out_specs=_rows(tr, cols), out_shape=jax.ShapeDtypeStruct((rows, cols), F32), compiler_params=_cp(1),
        name=name,
    )(r.reshape(N_CHIPS, rows, cols), *([] if token is None else [token])).reshape(shape)


FIRST_WEIGHTS = ("ev_w_in", "ev_w_q_up", "ev_w_kv_up", "ev_w_out")
LATER_WEIGHTS = tuple(n for n in MATMUL_WEIGHTS if n not in FIRST_WEIGHTS)
LAST_GRADS = ("ev_w_in", "ev_w_q_up", "ev_w_kv_up")
EARLIER_GRADS = tuple(n for n in MATMUL_WEIGHTS if n not in LAST_GRADS)


def _my_chip():
    return 2 * lax.axis_index("x") + lax.axis_index("y")


def _gather_first(w):
    small = _pack([w[n] for n in SMALL_SHARDED], 8 * LANES)
    stacked = [n for n in FIRST_WEIGHTS if SHARD_AXIS[n] == w[n].ndim - 1 and w[n].shape[-1] % LANES]
    shards = [w[n].astype(BF16)[None] if n in stacked else w[n].astype(BF16) for n in FIRST_WEIGHTS]
    got = _gather_chips(shards + [small], [0 if n in stacked else SHARD_AXIS[n] for n in FIRST_WEIGHTS] + [0],
                        name="gather_first")
    full = {n: w[n] for n in REPLICATED}
    for n, g in zip(FIRST_WEIGHTS, got[:-1]):
        full[n] = jnp.concatenate([g[q] for q in range(N_CHIPS)], axis=SHARD_AXIS[n]) if n in stacked else g
    per_chip = [_unpack(got[-1][q * small.shape[0]:(q + 1) * small.shape[0]].reshape(-1),
                        [w[n].shape for n in SMALL_SHARDED]) for q in range(N_CHIPS)]
    for i, n in enumerate(SMALL_SHARDED):
        full[n] = jnp.concatenate([per_chip[q][i] for q in range(N_CHIPS)], axis=SHARD_AXIS[n])
    return full


def _gather_later_start(w, after):
    behind = (after.reshape(-1)[0] * 0).astype(BF16)
    shards = [w[n].astype(BF16) + (behind if n == "od_w_rgate" else 0) for n in LATER_WEIGHTS]
    axes = [SHARD_AXIS[n] for n in LATER_WEIGHTS]
    lands = [lax.empty(tuple(d * (N_CHIPS if a == ax else 1) for a, d in enumerate(s.shape)), s.dtype)
             for s, ax in zip(shards, axes)]
    plan = _gather_plan(axes)
    return _split_start(shards, lands, plan, sibling=True, name="gather_later_start"), plan


def _owner_major(g, axis):
    shape = g.shape
    size = shape[axis] // N_CHIPS
    g = jnp.moveaxis(g.reshape(shape[:axis] + (N_CHIPS, size) + shape[axis + 1:]), axis, 0)
    return g.reshape(N_CHIPS, -1, shape[-1] if axis < len(shape) - 1 else size)


def _exchange_start(items, *, cross, name):
    me = _my_chip()
    srcs, lands, where = [], [], []
    for n, layers in enumerate(items):
        land = lax.empty((N_CHIPS, len(layers)) + layers[0].shape[1:], layers[0].dtype)
        for l, a in enumerate(layers):
            if not cross:
                own = lax.dynamic_index_in_dim(a, me, 0, keepdims=True)[:, None]
                land = lax.dynamic_update_slice(land, own, (me, l) + (0,) * (a.ndim - 1))
            srcs.append(a)
            where.append((n, l))
        lands.append(land)
    plan = _exchange_plan(where)
    return _split_start(srcs, lands, plan, sibling=cross, name=name), plan


def _earlier_items(grads, full_shapes):
    small = [_pack([jnp.split(grads[n].reshape(full_shapes[n]), N_CHIPS, axis=SHARD_AXIS[n])[q]
                    for n in SMALL_SHARDED], 8 * LANES) for q in range(N_CHIPS)]
    return [grads[n] for n in EARLIER_GRADS] + [[jnp.stack(small)]]


def _last_items(big, grads, full_shapes, loss):
    repl = _pack([grads[n].reshape(full_shapes[n]) for n in REPLICATED] + [loss.reshape(1)], 8 * LANES)
    return [big[n] for n in LAST_GRADS] + [[jnp.stack([repl] * N_CHIPS)]]


def kernel(
        x, mem, positions, ev_norm, ev_w_in, ev_pool_w, ev_pool_scale, ev_q_norm, ev_w_q_up, ev_kv_norm,
        ev_w_kv_up, ev_w_out, od_norm, od_w_in, od_conv_w, od_conv_b, od_w_rgate, od_b_rgate, od_w_igate,
        od_b_igate, od_lambda, od_w_out, xa_norm_x, xa_norm_mem, xa_w_q, xa_w_kv, xa_w_o, ffn_norm,
        ffn_w_gate_up, ffn_w_down, final_norm, loss_target, m_ev_norm, m_ev_w_in, m_ev_pool_w, m_ev_pool_scale,
        m_ev_q_norm, m_ev_w_q_up, m_ev_kv_norm, m_ev_w_kv_up, m_ev_w_out, m_od_norm, m_od_w_in, m_od_conv_w,
        m_od_conv_b, m_od_w_rgate, m_od_b_rgate, m_od_w_igate, m_od_b_igate, m_od_lambda, m_od_w_out,
        m_xa_norm_x, m_xa_norm_mem, m_xa_w_q, m_xa_w_kv, m_xa_w_o, m_ffn_norm, m_ffn_w_gate_up, m_ffn_w_down,
        m_final_norm, v_ev_norm, v_ev_w_in, v_ev_pool_w, v_ev_pool_scale, v_ev_q_norm, v_ev_w_q_up,
        v_ev_kv_norm, v_ev_w_kv_up, v_ev_w_out, v_od_norm, v_od_w_in, v_od_conv_w, v_od_conv_b, v_od_w_rgate,
        v_od_b_rgate, v_od_w_igate, v_od_b_igate, v_od_lambda, v_od_w_out, v_xa_norm_x, v_xa_norm_mem, v_xa_w_q,
        v_xa_w_kv, v_xa_w_o, v_ffn_norm, v_ffn_w_gate_up, v_ffn_w_down, v_final_norm):
    given = dict(locals())
    w = {n: given[n] for n in WEIGHTS}
    full_shapes = {n: tuple(d * (N_CHIPS if a == SHARD_AXIS.get(n) else 1) for a, d in enumerate(w[n].shape))
                   for n in WEIGHTS}
    full = _gather_first(w)
    later, later_plan = _gather_later_start(w, full["ev_w_out"])
    full["ev_norm"] = full["ev_norm"] + later[4][0:1, 0:1]
    exchange = {}

    def later_weights(after):
        return dict(zip(LATER_WEIGHTS, _split_wait(later, after, later_plan, sibling=True, name="gather_later_wait")))

    def exchange_earlier(grads):
        exchange["handle"], exchange["plan"] = _exchange_start(_earlier_items(grads, full_shapes), cross=True,
                                                               name="exchange_earlier_start")
        return exchange["handle"][4]

    loss, grad_x, big, grads = _local_step(x[0], mem[0], positions[0], loss_target[0], full, later_weights,
                                           exchange_earlier)
    earlier = EARLIER_GRADS + ("small",)
    got = dict(zip(earlier, _split_wait(exchange["handle"], grad_x, exchange["plan"], sibling=True,
                                        name="exchange_earlier_wait")))
    last, last_plan = _exchange_start(_last_items(big, grads, full_shapes, loss), cross=False,
                                      name="exchange_last_start")
    out = {}

    def finish(names, landed, token, tag):
        mine = [_sum_slots(landed[n], token=token if i == 0 else None, name=f"sum_chips_{n}")
                for i, n in enumerate(names)]
        other = _exchange_sibling(mine, name=f"exchange_sibling_{tag}")
        total = None
        for n, a, b in zip(names, mine, other):
            if n in MATMUL_WEIGHTS:
                out[n] = _adamw(w[n], a.reshape(w[n].shape), b.reshape(w[n].shape), given["m_" + n], given["v_" + n],
                                name=f"adamw_{n}")
                continue
            group = SMALL_SHARDED if n == "small" else REPLICATED
            spare = [jnp.zeros((1,), F32)] if group is REPLICATED else []
            packed = [_pack([given[pre + k] for k in group] + spare, 8 * LANES) for pre in ("", "m_", "v_")]
            res = _adamw(packed[0], a.reshape(packed[0].shape), b.reshape(packed[0].shape), packed[1], packed[2],
                         name=f"adamw_{n}")
            shapes = [w[k].shape for k in group] + [(1,)] * len(spare)
            for j, arrs in enumerate(zip(*[_unpack(r.reshape(-1), shapes) for r in res])):
                if j < len(group):
                    out[group[j]] = list(arrs)
                else:
                    total = arrs[0][0]
        return total

    finish(earlier, got, last[4], "earlier")
    names = LAST_GRADS + ("replicated",)
    got = dict(zip(names, _split_wait(last, out[EARLIER_GRADS[-1]][1], last_plan, name="exchange_last_wait")))
    loss = finish(names, got, None, "last")
    return (loss, grad_x[None], *[out[n][k] for k in range(4) for n in WEIGHTS])
```

```python
import math

import jax
import jax.numpy as jnp
from jax import lax
from jax.experimental import pallas as pl
from jax.experimental.pallas import tpu as pltpu

F32 = jnp.float32
BF16 = jnp.bfloat16

D_MODEL = 1024
POOL_DIM = 512
POOL_WINDOWS = (2, 4, 8, 16)
POOL_GROUP = 128
MLA_HEADS = 8
QK_NOPE = 64
QK_ROPE = 32
QK_DIM = QK_NOPE + QK_ROPE
V_HEAD = 64
HEAD_PAD = 128
Q_RANK = 256
KV_RANK = 128
ROPE_BASE = 10000.0
LRU_HEADS = 4
LRU_HEAD_DIM = 256
CONV_WIDTH = 4
LRU_C = 8.0
MEM_HEADS = 4
MEM_HEAD_DIM = 256
D_FF = 2816
RMS_EPS = 1e-6
NEG_INF = -1e30

ADAM_LR = 0.001
ADAM_B1 = 0.9
ADAM_B2 = 0.999
ADAM_EPS = 1e-08
ADAM_WD = 0.01
ADAM_STEP = 10

N_CHIPS = 4
LANES = 128
VMEM_LIMIT = 56 * 1024 * 1024
MESH = pl.DeviceIdType.MESH
ANY = pl.BlockSpec(memory_space=pl.ANY)
MIX_DIM = POOL_DIM + MLA_HEADS * HEAD_PAD

NN = (((1,), (0,)), ((), ()))
NT = (((1,), (1,)), ((), ()))
TN = (((0,), (0,)), ((), ()))


def _cp(n):
    return pltpu.CompilerParams(dimension_semantics=("arbitrary",) * n, vmem_limit_bytes=VMEM_LIMIT)


def _dot(a, b, dims=NN):
    return lax.dot_general(a, b, dims, preferred_element_type=F32)


def _row_tile(S):
    return 1024 if S % 1024 == 0 else min(S, 512)


def _rows(ts, w, cb=0):
    return pl.BlockSpec((ts, w), lambda i: (i, cb))


def _const(shape):
    return pl.BlockSpec(shape, lambda i: (0,) * len(shape))


MM_VMEM_BUDGET = 46 * 1024 * 1024


def _mm(a, bs, epi, outs, *, tn, nj, nt=False, also=None, extras=(), rows=(), sums=(), a_cb=0, k=None, tm=None,
        name):
    M = a.shape[0]
    k = k or a.shape[1]
    nb, ne, nr, no = len(bs), len(extras), len(rows), len(outs)
    lhs = [(a, k, a_cb, b) for b in bs[:1]] + ([(also[0], also[0].shape[1], 0, also[1])] if also else [])
    if tm is None:
        per_row = 2 * (sum(kk * x.dtype.itemsize for x, kk, _, _ in lhs)
                       + sum(e.dtype.itemsize for e, _ in extras) * tn
                       + sum(jnp.dtype(dt).itemsize for _, dt, _ in outs) * tn) + nb * tn * 4
        weights = (1 if nj == 1 else 2) * (sum(b.dtype.itemsize for b, _, _ in bs) * k
                                           + (also[1][0].dtype.itemsize * lhs[-1][1] if also else 0)) * tn
        tm = 1024 if M % 1024 == 0 and 1024 * per_row + weights <= MM_VMEM_BUDGET else min(M, 512)
    dims = NT if nt else NN
    assert not sums or nj == 1
    na = 2 if also else 0

    def body(*refs):
        av = refs[0][...].astype(BF16)
        accs = [_dot(av, r[...].astype(BF16), dims) for r in refs[1:1 + nb]]
        if also:
            accs[0] = accs[0] + _dot(refs[1 + nb][...].astype(BF16), refs[2 + nb][...].astype(BF16), dims)
        refs = refs[:1 + nb] + refs[1 + nb + na:]
        vals = epi(accs, [r[...] for r in refs[1 + nb:1 + nb + ne + nr]])
        outs_refs = refs[1 + nb + ne + nr:]
        for o, v in zip(outs_refs[:no], vals[:no]):
            o[...] = v.astype(o.dtype)
        if sums:
            @pl.when(pl.program_id(1) == 0)
            def _():
                for o in outs_refs[no:]:
                    o[...] = jnp.zeros_like(o)

            for o, v in zip(outs_refs[no:], vals[no:]):
                o[...] += v

    in_specs = [pl.BlockSpec((tm, k), lambda j, i: (i, a_cb))]
    weights = [(k, rb, cb) for (_, rb, cb) in bs]
    if also:
        in_specs_also = pl.BlockSpec((tm, lhs[-1][1]), lambda j, i: (i, 0))
        weights.append((lhs[-1][1], also[1][1], also[1][2]))
    for n, (kk, rb, cb) in enumerate(weights):
        if also and n == nb:
            in_specs.append(in_specs_also)
        mode = dict(pipeline_mode=pl.Buffered(1)) if nj == 1 else {}
        if nt:
            in_specs.append(pl.BlockSpec((tn, kk), lambda j, i, rb=rb, cb=cb: (rb + j, cb), **mode))
        else:
            in_specs.append(pl.BlockSpec((kk, tn), lambda j, i, rb=rb, cb=cb: (rb, cb + j), **mode))
    for (_, cb) in extras:
        in_specs.append(pl.BlockSpec((tm, tn), lambda j, i, cb=cb: (i, cb + j)))
    in_specs += [pl.BlockSpec((1, tn), lambda j, i: (0, 0))] * nr
    out_specs = [pl.BlockSpec((tm, tn), lambda j, i, cb=cb: (i, cb + j)) for (_, _, cb) in outs]
    out_specs += [pl.BlockSpec((1, w), lambda j, i: (0, 0)) for w in sums]
    res = pl.pallas_call(
        body, grid=(nj, M // tm), in_specs=in_specs, out_specs=out_specs,
        out_shape=[jax.ShapeDtypeStruct((M, n), dt) for (n, dt, _) in outs]
        + [jax.ShapeDtypeStruct((1, w), F32) for w in sums],
        compiler_params=_cp(2), name=name,
    )(a, *[b for (b, _, _) in bs], *([also[0], also[1][0]] if also else []), *[e for (e, _) in extras], *rows)
    return res


def _first(accs, extras):
    return [accs[0]]


def _add_res(accs, extras):
    return [accs[0] + extras[0].astype(F32)]


def _norm_bwd_epilogue(partials):
    def epi(accs, vals):
        dh = accs[0]
        for part in vals[:partials]:
            dh = dh + part.astype(F32)
        x, res, g = vals[partials:partials + 3]
        r = lax.rsqrt(jnp.mean(x * x, axis=-1, keepdims=True) + RMS_EPS)
        n = x * r
        dn = dh * g
        return [r * (dn - n * jnp.mean(dn * n, axis=-1, keepdims=True)) + res, jnp.sum(dh * n, axis=0, keepdims=True)]

    return epi


TN_VMEM_BUDGET = 44 * 1024 * 1024


def _contraction_rows(S, row_bytes, out_elems):
    ts = min(S, 2048)
    while ts > 512 and 2 * (ts * row_bytes + out_elems * 4) > TN_VMEM_BUDGET:
        ts //= 2
    return ts


def _mm_tn(a, b, *, ka=None, a_cb=0, nb=None, b_cb=0, tk=None, tn=None, ts=None, name):
    S = a.shape[0]
    ka = ka or a.shape[1]
    nb = nb or b.shape[1]
    tk = tk or ka
    tn = tn or nb
    ts = ts or _contraction_rows(S, tk * a.dtype.itemsize + tn * b.dtype.itemsize, tk * tn)
    a0, b0 = a_cb * (ka // tk), b_cb * (nb // tn)

    def body(a_ref, b_ref, o_ref):
        @pl.when(pl.program_id(2) == 0)
        def _():
            o_ref[...] = jnp.zeros_like(o_ref)

        o_ref[...] += _dot(a_ref[...].astype(BF16), b_ref[...].astype(BF16), TN)

    return pl.pallas_call(
        body, grid=(ka // tk, nb // tn, S // ts),
        in_specs=[pl.BlockSpec((ts, tk), lambda p, q, s: (s, a0 + p)),
                  pl.BlockSpec((ts, tn), lambda p, q, s: (s, b0 + q))],
        out_specs=pl.BlockSpec((tk, tn), lambda p, q, s: (p, q)),
        out_shape=jax.ShapeDtypeStruct((ka, nb), F32), compiler_params=_cp(3), name=name,
    )(a, b)


def _mm_tn_owners(a, bs, *, name):
    S, ka = a.shape
    nb = sum(b.shape[1] for b in bs)
    tn = nb // N_CHIPS
    ts = _contraction_rows(S, ka * a.dtype.itemsize + len(bs) * tn * bs[0].dtype.itemsize, ka * tn)
    per = N_CHIPS // len(bs)

    def body(a_ref, *refs):
        o_ref = refs[-1]
        q = pl.program_id(0)

        @pl.when(pl.program_id(1) == 0)
        def _():
            o_ref[...] = jnp.zeros_like(o_ref)

        av = a_ref[...].astype(BF16)
        for n, b_ref in enumerate(refs[:-1]):
            @pl.when(q // per == n)
            def _():
                o_ref[0] += _dot(av, b_ref[...].astype(BF16), TN)

    in_specs = [pl.BlockSpec((ts, ka), lambda q, s: (s, 0))]
    for n in range(len(bs)):
        in_specs.append(pl.BlockSpec((ts, tn), lambda q, s, n=n: (jnp.where(q // per == n, s, 0),
                                                                  jnp.clip(q - n * per, 0, per - 1))))
    return pl.pallas_call(
        body, grid=(N_CHIPS, S // ts), in_specs=in_specs,
        out_specs=pl.BlockSpec((1, ka, tn), lambda q, s: (q, 0, 0)),
        out_shape=jax.ShapeDtypeStruct((N_CHIPS, ka, tn), F32), compiler_params=_cp(2), name=name,
    )(a, *bs)


def _mm_tn_grouped(a, b, groups, w, *, name):
    S = a.shape[0]
    ts = _contraction_rows(S, w * (a.dtype.itemsize + b.dtype.itemsize), w * w)

    def body(a_ref, b_ref, o_ref):
        @pl.when(pl.program_id(1) == 0)
        def _():
            o_ref[...] = jnp.zeros_like(o_ref)

        o_ref[0] += _dot(a_ref[...].astype(BF16), b_ref[...].astype(BF16), TN)

    return pl.pallas_call(
        body, grid=(groups, S // ts),
        in_specs=[pl.BlockSpec((ts, w), lambda g, s: (s, g)), pl.BlockSpec((ts, w), lambda g, s: (s, g))],
        out_specs=pl.BlockSpec((1, w, w), lambda g, s: (g, 0, 0)),
        out_shape=jax.ShapeDtypeStruct((groups, w, w), F32), compiler_params=_cp(2), name=name,
    )(a, b)


def _rms(x, g, *, name):
    S, w = x.shape
    ts = _row_tile(S)

    def body(x_ref, g_ref, o_ref):
        xv = x_ref[...]
        r = lax.rsqrt(jnp.mean(xv * xv, axis=-1, keepdims=True) + RMS_EPS)
        o_ref[...] = (xv * r * g_ref[...]).astype(o_ref.dtype)

    return pl.pallas_call(
        body, grid=(S // ts,), in_specs=[_rows(ts, w), _const((1, w))], out_specs=_rows(ts, w),
        out_shape=jax.ShapeDtypeStruct((S, w), BF16), compiler_params=_cp(1), name=name,
    )(x, g.reshape(1, w))


def _norm_gain_grad(x, dy, *, name):
    S, w = x.shape
    ts = _row_tile(S)

    def body(x_ref, dy_ref, dg_ref):
        @pl.when(pl.program_id(0) == 0)
        def _():
            dg_ref[...] = jnp.zeros_like(dg_ref)

        xv = x_ref[...]
        r = lax.rsqrt(jnp.mean(xv * xv, axis=-1, keepdims=True) + RMS_EPS)
        dg_ref[...] += jnp.sum(dy_ref[...] * (xv * r), axis=0, keepdims=True)

    return pl.pallas_call(
        body, grid=(S // ts,), in_specs=[_rows(ts, w), _rows(ts, w)], out_specs=_const((1, w)),
        out_shape=jax.ShapeDtypeStruct((1, w), F32), compiler_params=_cp(1), name=name,
    )(x, dy)


HALO = 16


def _pool_counts(i, ts, rows, first_row):
    t = i * ts + first_row + lax.broadcasted_iota(jnp.int32, (rows, 1), 0)
    return [jnp.minimum(t + 1, w).astype(F32) for w in POOL_WINDOWS]


def _even_front(x, g, w_in, pool_w, pool_scale, g_q, w_q, g_kv, w_kv, ctab, stab, *, name):
    S = x.shape[0]
    ts = min(S, 512)

    def body(x_ref, g_ref, win_ref, pw_ref, sc_ref, gq_ref, wq_ref, gkv_ref, wkv_ref, c_ref, s_ref,
             h_ref, z_ref, y_ref, p_ref, cqn_ref, ckvn_ref, q_ref, k_ref, v_ref, tail):
        i = pl.program_id(0)

        def normed(t, gain):
            r = lax.rsqrt(jnp.mean(t * t, axis=-1, keepdims=True) + RMS_EPS)
            return (t * r * gain).astype(BF16)

        h = normed(x_ref[...], g_ref[...])
        h_ref[...] = h
        z = _dot(h, win_ref[...])
        z_ref[...] = z
        u = z[:, :POOL_DIM]
        xe = jnp.concatenate([jnp.where(i > 0, tail[...], 0.0), u], axis=0)
        tail[...] = u[ts - HALO:]
        sums = []
        s = xe
        for sh in (1, 2, 4, 8):
            s = s + pltpu.roll(s, sh, 0)
            sums.append(s)
        cnts = _pool_counts(i, ts, ts, 0)
        for grp in range(4):
            lo, hi = grp * POOL_GROUP, (grp + 1) * POOL_GROUP
            pooled = (sums[grp][HALO:, lo:hi] / cnts[grp] - u[:, lo:hi]).astype(BF16)
            p_ref[:, lo:hi] = pooled
            y_ref[:, lo:hi] = (_dot(pooled, pw_ref[grp]) * sc_ref[:, lo:hi]).astype(y_ref.dtype)
        cqn = normed(z[:, POOL_DIM:POOL_DIM + Q_RANK], gq_ref[...])
        ckvn = normed(z[:, POOL_DIM + Q_RANK:POOL_DIM + Q_RANK + KV_RANK], gkv_ref[...])
        cqn_ref[...] = cqn
        ckvn_ref[...] = ckvn
        q = _dot(cqn, wq_ref[...])
        kv = _dot(ckvn, wkv_ref[...])
        c, sn = c_ref[...], s_ref[...]
        kr = z[:, D_MODEL - HEAD_PAD:]
        kr_rot = kr * c + _rope_partner(kr) * sn
        lane = lax.broadcasted_iota(jnp.int32, (ts, HEAD_PAD), 1)
        for hd in range(MLA_HEADS):
            lo, hi = hd * HEAD_PAD, (hd + 1) * HEAD_PAD
            qh = q[:, lo:hi]
            q_ref[:, lo:hi] = (qh * c + _rope_partner(qh) * sn).astype(q_ref.dtype)
            k_ref[:, lo:hi] = (kv[:, lo:hi] + kr_rot).astype(k_ref.dtype)
            v_ref[:, lo:hi] = jnp.where(lane == V_HEAD, 1.0, kv[:, D_MODEL + lo:D_MODEL + hi]).astype(v_ref.dtype)

    wide = jax.ShapeDtypeStruct((S, D_MODEL), BF16)
    return pl.pallas_call(
        body, grid=(S // ts,),
        in_specs=[_rows(ts, D_MODEL), _const((1, D_MODEL)), _const((D_MODEL, D_MODEL)),
                  _const((4, POOL_GROUP, POOL_GROUP)), _const((1, POOL_DIM)), _const((1, Q_RANK)),
                  _const((Q_RANK, D_MODEL)), _const((1, KV_RANK)), _const((KV_RANK, 2 * D_MODEL)),
                  _rows(ts, HEAD_PAD), _rows(ts, HEAD_PAD)],
        out_specs=[_rows(ts, D_MODEL), _rows(ts, D_MODEL), _rows(ts, POOL_DIM), _rows(ts, POOL_DIM),
                   _rows(ts, Q_RANK), _rows(ts, KV_RANK), _rows(ts, D_MODEL), _rows(ts, D_MODEL), _rows(ts, D_MODEL)],
        out_shape=[wide, jax.ShapeDtypeStruct((S, D_MODEL), F32), jax.ShapeDtypeStruct((S, MIX_DIM), BF16),
                   jax.ShapeDtypeStruct((S, POOL_DIM), BF16), jax.ShapeDtypeStruct((S, Q_RANK), BF16),
                   jax.ShapeDtypeStruct((S, KV_RANK), BF16), wide, wide, wide],
        scratch_shapes=[pltpu.VMEM((HALO, POOL_DIM), F32)], compiler_params=_cp(1), name=name,
    )(x, g.reshape(1, D_MODEL), w_in, pool_w, pool_scale, g_q.reshape(1, Q_RANK), w_q, g_kv.reshape(1, KV_RANK), w_kv,
      ctab, stab)


def _norm_bwd_values(xv, gain, dy):
    r = lax.rsqrt(jnp.mean(xv * xv, axis=-1, keepdims=True) + RMS_EPS)
    n = xv * r
    dn = dy * gain
    return r * (dn - n * jnp.mean(dn * n, axis=-1, keepdims=True)), jnp.sum(dy * n, axis=0, keepdims=True)


def _even_back(dq_rot, dk_cat, dv, dmix, pooled, z, x, dxo, ctab, stab, w_q, w_kv, w_in, pool_w, pool_scale, g_q, g_kv,
               g_x, *, name):
    S = x.shape[0]
    ts = min(S, 512)
    nh = ts // HALO
    last = S // HALO - 1
    n = ts + HALO

    def body(dq_ref, dk_ref, dv_ref, dy_ref, dyh_ref, p_ref, z_ref, x_ref, dxo_ref, c_ref, s_ref, wq_ref, wkv_ref,
             win_ref, pw_ref, sc_ref, gq_ref, gkv_ref, gx_ref,
             dx_ref, dqp_ref, dz_ref, dyp_ref, dgq_ref, dgkv_ref, dsc_ref, dgx_ref):
        i = pl.program_id(0)

        @pl.when(i == 0)
        def _():
            for ref in (dgq_ref, dgkv_ref, dsc_ref, dgx_ref):
                ref[...] = jnp.zeros_like(ref)

        c, sn = c_ref[...], s_ref[...]
        z = z_ref[...]
        dk = dk_ref[...]
        for hd in range(MLA_HEADS):
            lo, hi = hd * HEAD_PAD, (hd + 1) * HEAD_PAD
            g = dq_ref[:, lo:hi]
            dqp_ref[:, lo:hi] = (g * c + _rope_partner(g * sn)).astype(dqp_ref.dtype)
            heads_sum = dk[:, lo:hi] if hd == 0 else heads_sum + dk[:, lo:hi]
        lane = lax.broadcasted_iota(jnp.int32, heads_sum.shape, 1)
        dkr = jnp.where((lane >= QK_NOPE) & (lane < QK_DIM), heads_sum * c + _rope_partner(heads_sum * sn), 0.0)
        dcqn = _dot(dqp_ref[...], wq_ref[...], NT)
        dckvn = _dot(dk.astype(BF16), wkv_ref[:, :D_MODEL], NT) + _dot(dv_ref[...].astype(BF16),
                                                                       wkv_ref[:, D_MODEL:], NT)
        dcq, dgq = _norm_bwd_values(z[:, POOL_DIM:POOL_DIM + Q_RANK], gq_ref[...], dcqn)
        dckv, dgkv = _norm_bwd_values(z[:, POOL_DIM + Q_RANK:POOL_DIM + Q_RANK + KV_RANK], gkv_ref[...], dckvn)
        dgq_ref[...] += dgq
        dgkv_ref[...] += dgkv
        dyv = dy_ref[...].astype(F32)
        dyh = jnp.where(i < pl.num_programs(0) - 1, dyh_ref[...].astype(F32), 0.0)
        dypre = (jnp.concatenate([dyv, dyh], axis=0) * sc_ref[...]).astype(BF16)
        dyp_ref[...] = dypre[:ts]
        cnts = _pool_counts(i, ts, n, 0)
        dsc = []
        for grp in range(4):
            lo, hi = grp * POOL_GROUP, (grp + 1) * POOL_GROUP
            dsc.append(jnp.sum(dyv[:, lo:hi] * _dot(p_ref[:, lo:hi], pw_ref[grp]), axis=0, keepdims=True))
            dpool = _dot(dypre[:, lo:hi], pw_ref[grp], NT)
            s = dpool / cnts[grp]
            for sh in (1, 2, 4, 8)[:grp + 1]:
                s = s + pltpu.roll(s, n - sh, 0)
            dz_ref[:, lo:hi] = (s[:ts] - dpool[:ts]).astype(dz_ref.dtype)
        dsc_ref[...] += jnp.concatenate(dsc, axis=1)
        dz_ref[:, POOL_DIM:POOL_DIM + Q_RANK] = dcq.astype(dz_ref.dtype)
        dz_ref[:, POOL_DIM + Q_RANK:POOL_DIM + Q_RANK + KV_RANK] = dckv.astype(dz_ref.dtype)
        dz_ref[:, D_MODEL - HEAD_PAD:] = dkr.astype(dz_ref.dtype)
        dx, dgx = _norm_bwd_values(x_ref[...], gx_ref[...], _dot(dz_ref[...], win_ref[...], NT))
        dx_ref[...] = dx + dxo_ref[...]
        dgx_ref[...] += dgx

    wide, pool = _rows(ts, D_MODEL), _rows(ts, POOL_DIM)
    f32 = lambda w: jax.ShapeDtypeStruct((1, w), F32)
    return pl.pallas_call(
        body, grid=(S // ts,),
        in_specs=[wide, wide, wide, pool,
                  pl.BlockSpec((HALO, POOL_DIM), lambda i: (jnp.minimum((i + 1) * nh, last), 0)), pool, wide, wide, wide,
                  _rows(ts, HEAD_PAD), _rows(ts, HEAD_PAD), _const((Q_RANK, D_MODEL)), _const((KV_RANK, 2 * D_MODEL)),
                  _const((D_MODEL, D_MODEL)), _const((4, POOL_GROUP, POOL_GROUP)), _const((1, POOL_DIM)),
                  _const((1, Q_RANK)), _const((1, KV_RANK)), _const((1, D_MODEL))],
        out_specs=[wide, wide, wide, pool, _const((1, Q_RANK)), _const((1, KV_RANK)), _const((1, POOL_DIM)),
                   _const((1, D_MODEL))],
        out_shape=[jax.ShapeDtypeStruct((S, D_MODEL), F32), jax.ShapeDtypeStruct((S, D_MODEL), BF16),
                   jax.ShapeDtypeStruct((S, D_MODEL), BF16), jax.ShapeDtypeStruct((S, POOL_DIM), BF16),
                   f32(Q_RANK), f32(KV_RANK), f32(POOL_DIM), f32(D_MODEL)],
        compiler_params=_cp(1), name=name,
    )(dq_rot, dk_cat, dv, dmix, dmix, pooled, z, x, dxo, ctab, stab, w_q, w_kv, w_in, pool_w, pool_scale,
      g_q.reshape(1, Q_RANK), g_kv.reshape(1, KV_RANK), g_x.reshape(1, D_MODEL))


def _rope_partner(t):
    lane = lax.broadcasted_iota(jnp.int32, t.shape, 1)
    swapped = jnp.where(lane < QK_NOPE + QK_ROPE // 2, pltpu.roll(t, HEAD_PAD - QK_ROPE // 2, 1),
                        pltpu.roll(t, QK_ROPE // 2, 1))
    return jnp.where((lane >= QK_NOPE) & (lane < QK_DIM), swapped, 0.0)


ATT_SCALE = QK_DIM ** -0.5
LOG2E = math.log2(math.e)


HEADS_PER_STEP = 2
ATT_COL0 = POOL_DIM // HEAD_PAD


FWD_TILE = 1024


def _stat_rows(col):
    return jnp.broadcast_to(col, (col.shape[0], LANES)).T[0:8]


def _retile_rows(rows, tq):
    heads, n8, t = rows.shape
    if t == tq:
        return rows
    flat = rows.reshape(heads, n8 // 8, 8, t)[:, :, 0].reshape(heads, -1, 1, tq)
    return jnp.broadcast_to(flat, (heads, flat.shape[1], 8, tq)).reshape(heads, -1, tq)


def _flash_fwd(q, k, v, mix, *, name):
    S = q.shape[0]
    tq = FWD_TILE if S % FWD_TILE == 0 else min(S, 512)
    nq = S // tq
    hs = HEADS_PER_STEP
    wide = hs * HEAD_PAD

    def body(q_ref, k_ref, v_ref, mix_ref, o_ref, lse_ref):
        qi = pl.program_id(1)
        qv = [q_ref[:, a * HEAD_PAD:(a + 1) * HEAD_PAD] for a in range(hs)]

        def update(m, acc, s, v):
            m_new = jnp.maximum(m, jnp.max(s, axis=-1, keepdims=True))
            p = jnp.exp2((s - m_new) * (ATT_SCALE * LOG2E))
            alpha = jnp.exp2((m - m_new) * (ATT_SCALE * LOG2E))
            return m_new, alpha * acc + _dot(p.astype(BF16), v)

        def step(j, carry, masked):
            off = pl.multiple_of(j * tq, tq)
            out = []
            for a in range(hs):
                head = slice(a * HEAD_PAD, (a + 1) * HEAD_PAD)
                s = _dot(qv[a], k_ref[pl.ds(off, tq), head], NT)
                if masked:
                    row = lax.broadcasted_iota(jnp.int32, (tq, tq), 0)
                    col = lax.broadcasted_iota(jnp.int32, (tq, tq), 1)
                    s = jnp.where(col <= row, s, NEG_INF)
                out.append(update(*carry[a], s, v_ref[pl.ds(off, tq), head]))
            return tuple(out)

        one = (jnp.full((tq, 1), NEG_INF, F32), jnp.zeros((tq, HEAD_PAD), F32))
        carry = step(qi, lax.fori_loop(0, qi, lambda j, c: step(j, c, False), (one,) * hs), True)
        for a in range(hs):
            m, acc = carry[a]
            l = acc[:, V_HEAD:V_HEAD + 1]
            o_ref[:, a * HEAD_PAD:(a + 1) * HEAD_PAD] = (acc / l).astype(o_ref.dtype)
            lse_ref[a] = _stat_rows(m * ATT_SCALE + jnp.log(l))

    blk = pl.BlockSpec((tq, wide), lambda h, i: (i, h))
    full = pl.BlockSpec((S, wide), lambda h, i: (0, h))
    return pl.pallas_call(
        body, grid=(MLA_HEADS // hs, nq), in_specs=[blk, full, full, ANY],
        out_specs=[pl.BlockSpec((tq, wide), lambda h, i: (i, ATT_COL0 // hs + h)),
                   pl.BlockSpec((hs, 8, tq), lambda h, i: (h, i, 0))],
        out_shape=[jax.ShapeDtypeStruct(mix.shape, mix.dtype), jax.ShapeDtypeStruct((MLA_HEADS, nq * 8, tq), F32)],
        input_output_aliases={3: 0}, compiler_params=_cp(2), name=name,
    )(q, k, v, mix)


BWD_TILE = 1024
BWD_HEADS_PER_STEP = 1


def _bwd_tile(S):
    return BWD_TILE if S % BWD_TILE == 0 else min(S, 512)


def _attn_delta(dmix, mix, *, name):
    S = mix.shape[0]
    ts = _bwd_tile(S)
    half = MLA_HEADS // 2
    halves = [_rows(ts, half * HEAD_PAD, 1), _rows(ts, half * HEAD_PAD, 2)]

    def body(do0_ref, do1_ref, o0_ref, o1_ref, d_ref):
        for n, (do_ref, o_ref) in enumerate(((do0_ref, o0_ref), (do1_ref, o1_ref))):
            prod = do_ref[...].astype(F32) * o_ref[...].astype(F32)
            for a in range(half):
                d_ref[n * half + a] = _stat_rows(
                    jnp.sum(prod[:, a * HEAD_PAD:(a + 1) * HEAD_PAD], axis=-1, keepdims=True))

    return pl.pallas_call(
        body, grid=(S // ts,), in_specs=halves + halves,
        out_specs=pl.BlockSpec((MLA_HEADS, 8, ts), lambda i: (0, i, 0)),
        out_shape=jax.ShapeDtypeStruct((MLA_HEADS, (S // ts) * 8, ts), F32), compiler_params=_cp(1), name=name,
    )(dmix, dmix, mix, mix)


def _flash_bwd(q, k, v, dmix, lse_rows, delta_rows, *, name):
    S = q.shape[0]
    tq = _bwd_tile(S)
    nq = S // tq
    hs = BWD_HEADS_PER_STEP
    wide = hs * HEAD_PAD

    def body(q_hbm, do_hbm, lse_ref, dl_ref, k_ref, v_ref, dq_hbm, dk_ref, dv_ref, q_all, do_all, dq_all):
        g, j = pl.program_id(0), pl.program_id(1)
        cols = pl.multiple_of(g * wide, wide)

        @pl.when(j == 0)
        def _():
            pltpu.sync_copy(q_hbm.at[:, pl.ds(cols, wide)], q_all)
            pltpu.sync_copy(do_hbm.at[:, pl.ds(POOL_DIM + cols, wide)], do_all)
            dq_all[...] = jnp.zeros_like(dq_all)

        heads = [slice(a * HEAD_PAD, (a + 1) * HEAD_PAD) for a in range(hs)]
        kv = [k_ref[:, a] for a in heads]
        vv = [v_ref[:, a] for a in heads]

        def block(a, keys, rows, lse2, dl, first_query):
            qv, dov = q_all[rows, heads[a]], do_all[rows, heads[a]]
            st = _dot(kv[a][:keys], qv, NT)
            if first_query is not None:
                krow = lax.broadcasted_iota(jnp.int32, st.shape, 0)
                qcol = lax.broadcasted_iota(jnp.int32, st.shape, 1) + first_query
                st = jnp.where(krow <= qcol, st, NEG_INF)
            pt = jnp.exp2(st * (ATT_SCALE * LOG2E) - lse2)
            dst = (pt * (_dot(vv[a][:keys], dov, NT) - dl)).astype(BF16)
            dq_all[rows, heads[a]] += _dot(dst, kv[a][:keys], TN)
            return _dot(dst, qv), _dot(pt.astype(BF16), dov)

        def stats(a, i):
            off8 = pl.multiple_of(i * 8, 8)
            return lse_ref[a, pl.ds(off8, 8), :][0:1] * LOG2E, dl_ref[a, pl.ds(off8, 8), :][0:1]

        def step(i, carry):
            rows = pl.ds(pl.multiple_of(i * tq, tq), tq)
            out = []
            for a in range(hs):
                dk, dv = block(a, tq, rows, *stats(a, i), None)
                out.append((carry[a][0] + dk, carry[a][1] + dv))
            return tuple(out)

        def diagonal():
            half = tq // 2
            out = []
            for a in range(hs):
                lse2, dl = stats(a, j)
                off = pl.multiple_of(j * tq, tq)
                dk0, dv0 = block(a, half, pl.ds(off, half), lse2[:, :half], dl[:, :half], 0)
                dk1, dv1 = block(a, tq, pl.ds(pl.multiple_of(off + half, half), half), lse2[:, half:], dl[:, half:], half)
                zero = jnp.zeros((tq - half, HEAD_PAD), F32)
                out.append((dk1 + jnp.concatenate([dk0, zero], axis=0), dv1 + jnp.concatenate([dv0, zero], axis=0)))
            return tuple(out)

        carry = lax.fori_loop(j + 1, nq, step, diagonal())
        for a in range(hs):
            dk_ref[:, heads[a]] = carry[a][0] * ATT_SCALE
            dv_ref[:, heads[a]] = carry[a][1]

        @pl.when(j == nq - 1)
        def _():
            dq_all[...] = dq_all[...] * ATT_SCALE
            pltpu.sync_copy(dq_all, dq_hbm.at[:, pl.ds(cols, wide)])

    blk = pl.BlockSpec((tq, wide), lambda g, j: (j, g))
    stat = pl.BlockSpec((hs, nq * 8, tq), lambda g, j: (g, 0, 0))
    full = jax.ShapeDtypeStruct((S, MLA_HEADS * HEAD_PAD), F32)
    return pl.pallas_call(
        body, grid=(MLA_HEADS // hs, nq), in_specs=[ANY, ANY, stat, stat, blk, blk], out_specs=[ANY, blk, blk],
        out_shape=[full, full, full],
        scratch_shapes=[pltpu.VMEM((S, wide), BF16), pltpu.VMEM((S, wide), BF16), pltpu.VMEM((S, wide), F32)],
        compiler_params=_cp(2), name=name,
    )(q, dmix, lse_rows, delta_rows, k, v)


MEM_SCALE = MEM_HEAD_DIM ** -0.5


def _xattn_probs(qh, kh):
    s = _dot(qh, kh, NT) * MEM_SCALE
    e = jnp.exp(s - jnp.max(s, axis=-1, keepdims=True))
    return e / jnp.sum(e, axis=-1, keepdims=True)


def _xa_block_fwd(x, kvm, w_q, w_o, g, *, name):
    S = x.shape[0]
    ts = min(S, 512)
    nm = kvm.shape[0]

    def body(x_ref, kv_ref, wq_ref, wo_ref, g_ref, xo_ref, hx_ref, q_ref, o_ref):
        xv = x_ref[...]
        r = lax.rsqrt(jnp.mean(xv * xv, axis=-1, keepdims=True) + RMS_EPS)
        hx = (xv * r * g_ref[...]).astype(BF16)
        hx_ref[...] = hx
        q = _dot(hx, wq_ref[...]).astype(BF16)
        q_ref[...] = q
        for h in range(MEM_HEADS):
            lo, hi = h * MEM_HEAD_DIM, (h + 1) * MEM_HEAD_DIM
            p = _xattn_probs(q[:, lo:hi], kv_ref[:, lo:hi])
            o_ref[:, lo:hi] = _dot(p.astype(BF16), kv_ref[:, D_MODEL + lo:D_MODEL + hi]).astype(o_ref.dtype)
        xo_ref[...] = xv + _dot(o_ref[...], wo_ref[...])

    square = _const((D_MODEL, D_MODEL))
    act = jax.ShapeDtypeStruct((S, D_MODEL), BF16)
    return pl.pallas_call(
        body, grid=(S // ts,),
        in_specs=[_rows(ts, D_MODEL), _const((nm, 2 * D_MODEL)), square, square, _const((1, D_MODEL))],
        out_specs=[_rows(ts, D_MODEL)] * 4, out_shape=[jax.ShapeDtypeStruct((S, D_MODEL), F32), act, act, act],
        compiler_params=_cp(1), name=name,
    )(x, kvm, w_q, w_o, g.reshape(1, D_MODEL))


def _xa_block_bwd(dxo, x, q, kvm, w_q, w_o, g, *, name):
    S = q.shape[0]
    ts = min(S, 512)
    nm = kvm.shape[0]

    def body(dxo_ref, x_ref, q_ref, kv_ref, wq_ref, wo_ref, g_ref, dx_ref, dq_ref, dkv_ref, dg_ref):
        @pl.when(pl.program_id(0) == 0)
        def _():
            dkv_ref[...] = jnp.zeros_like(dkv_ref)
            dg_ref[...] = jnp.zeros_like(dg_ref)

        dxo = dxo_ref[...]
        do = _dot(dxo.astype(BF16), wo_ref[...], NT).astype(BF16)
        for h in range(MEM_HEADS):
            lo, hi = h * MEM_HEAD_DIM, (h + 1) * MEM_HEAD_DIM
            qh, kh, vh = q_ref[:, lo:hi], kv_ref[:, lo:hi], kv_ref[:, D_MODEL + lo:D_MODEL + hi]
            doh = do[:, lo:hi]
            p = _xattn_probs(qh, kh)
            dp = _dot(doh, vh, NT)
            ds = (p * (dp - jnp.sum(dp * p, axis=-1, keepdims=True)) * MEM_SCALE).astype(BF16)
            dq_ref[:, lo:hi] = _dot(ds, kh).astype(dq_ref.dtype)
            dkv_ref[:, lo:hi] += _dot(ds, qh, TN)
            dkv_ref[:, D_MODEL + lo:D_MODEL + hi] += _dot(p.astype(BF16), doh, TN)
        dx, dg = _norm_bwd_epilogue(0)([_dot(dq_ref[...], wq_ref[...], NT)], [x_ref[...], dxo, g_ref[...]])
        dx_ref[...] = dx
        dg_ref[...] += dg

    square = _const((D_MODEL, D_MODEL))
    return pl.pallas_call(
        body, grid=(S // ts,),
        in_specs=[_rows(ts, D_MODEL), _rows(ts, D_MODEL), _rows(ts, D_MODEL), _const((nm, 2 * D_MODEL)), square,
                  square, _const((1, D_MODEL))],
        out_specs=[_rows(ts, D_MODEL), _rows(ts, D_MODEL), _const((nm, 2 * D_MODEL)), _const((1, D_MODEL))],
        out_shape=[jax.ShapeDtypeStruct((S, D_MODEL), F32), jax.ShapeDtypeStruct((S, D_MODEL), BF16),
                   jax.ShapeDtypeStruct((nm, 2 * D_MODEL), F32), jax.ShapeDtypeStruct((1, D_MODEL), F32)],
        compiler_params=_cp(1), name=name,
    )(dxo, x, q, kvm, w_q, w_o, g.reshape(1, D_MODEL))


CONV_HALO = 8


def _sigmoid(x):
    return 0.5 * jnp.tanh(0.5 * x) + 0.5


def _softplus(x):
    return jnp.maximum(x, 0.0) + jnp.log(1.0 + jnp.exp(-jnp.abs(x)))


def _neg_expm1(x):
    series = -x * (1.0 + x * (1.0 / 2) * (1.0 + x * (1.0 / 3) * (1.0 + x * (1.0 / 4) * (1.0 + x * (1.0 / 5)))))
    return jnp.where(x > -0.05, series, 1.0 - jnp.exp(x))


GELU_C = math.sqrt(2.0 / math.pi)


def _gelu(x):
    return 0.5 * x * (1.0 + jnp.tanh(GELU_C * (x + 0.044715 * x * x * x)))


def _gelu_grad(x):
    t = jnp.tanh(GELU_C * (x + 0.044715 * x * x * x))
    return 0.5 * (1.0 + t) + 0.5 * x * (1.0 - t * t) * GELU_C * (1.0 + 3 * 0.044715 * x * x)


def _lru_gates(xc, wr_ref, br, wi_ref, bi, sp, reset):
    xcb = xc.astype(BF16)
    pr, pi = [], []
    for h in range(LRU_HEADS):
        lo, hi = h * LRU_HEAD_DIM, (h + 1) * LRU_HEAD_DIM
        pr.append(_dot(xcb[:, lo:hi], wr_ref[h]))
        pi.append(_dot(xcb[:, lo:hi], wi_ref[h]))
    r = _sigmoid(jnp.concatenate(pr, axis=1) + br)
    ig = _sigmoid(jnp.concatenate(pi, axis=1) + bi)
    log_a = -LRU_C * r * sp
    a = jnp.where(reset, 0.0, jnp.exp(log_a))
    mult = jnp.where(reset, 1.0, jnp.sqrt(jnp.maximum(_neg_expm1(2.0 * log_a), 0.0)))
    return r, ig, a, mult


SUBLANES = 8


def _compose_groups(a, b, reverse):
    n = a.shape[0]
    row = lax.broadcasted_iota(jnp.int32, a.shape, 0) % SUBLANES
    for s in (1, 2, 4):
        inside = (row < SUBLANES - s) if reverse else (row >= s)
        shift = n - s if reverse else s
        a_s = jnp.where(inside, pltpu.roll(a, shift, 0), 1.0)
        b_s = jnp.where(inside, pltpu.roll(b, shift, 0), 0.0)
        b = a * b_s + b
        a = a * a_s
    return a, b


def _chain_groups(a_buf, h_ref, state, reverse):
    groups = a_buf.shape[0] // SUBLANES

    def group(g, h_in):
        off = pl.multiple_of((groups - 1 - g if reverse else g) * SUBLANES, SUBLANES)
        h = a_buf[pl.ds(off, SUBLANES), :] * h_in + h_ref[pl.ds(off, SUBLANES), :]
        h_ref[pl.ds(off, SUBLANES), :] = h
        return jnp.broadcast_to(h[0:1] if reverse else h[SUBLANES - 1:SUBLANES], h.shape)

    return lax.fori_loop(0, groups, group, state, unroll=4)[0:1]


def _lru_fwd(x, g, w_in, reset, conv_w, conv_b, w_r, b_r, w_i, b_i, lam, *, name):
    S = x.shape[0]
    ts = min(S, 512)
    W = D_MODEL

    def body(x_ref, g_ref, win_ref, rs_ref, cw_ref, cb_ref, wr_ref, br_ref, wi_ref, bi_ref, lam_ref,
             hn_ref, z_ref, xc_ref, h_ref, y_ref, a_buf, carry, tail):
        i = pl.program_id(0)

        @pl.when(i == 0)
        def _():
            carry[...] = jnp.zeros_like(carry)
            tail[...] = jnp.zeros_like(tail)

        xv = x_ref[...]
        hn = (xv * lax.rsqrt(jnp.mean(xv * xv, axis=-1, keepdims=True) + RMS_EPS) * g_ref[...]).astype(BF16)
        hn_ref[...] = hn
        z_ref[...] = _dot(hn, win_ref[...])
        xb = z_ref[:, W:]
        xe = jnp.concatenate([tail[...], xb], axis=0)
        tail[...] = xb[ts - CONV_HALO:]
        xc = cb_ref[...] + cw_ref[3:4, :] * xe[CONV_HALO:]
        for kk in range(CONV_WIDTH - 1):
            xc = xc + cw_ref[kk:kk + 1, :] * pltpu.roll(xe, CONV_WIDTH - 1 - kk, 0)[CONV_HALO:]
        xc_ref[...] = xc
        reset = rs_ref[...] > 0.5
        _, ig, a, mult = _lru_gates(xc, wr_ref, br_ref[...], wi_ref, bi_ref[...], _softplus(-lam_ref[...]), reset)
        a_buf[...], h_ref[...] = _compose_groups(a, mult * (ig * xc), False)
        carry[...] = _chain_groups(a_buf, h_ref, jnp.broadcast_to(carry[...], (SUBLANES, W)), False)
        y_ref[...] = (_gelu(z_ref[:, :W]) * h_ref[...]).astype(y_ref.dtype)

    vec = _const((1, W))
    gw = _const((LRU_HEADS, LRU_HEAD_DIM, LRU_HEAD_DIM))
    return pl.pallas_call(
        body, grid=(S // ts,),
        in_specs=[_rows(ts, W), vec, _const((W, 2 * W)), _rows(ts, 1), _const((CONV_WIDTH, W)), vec, gw, vec, gw, vec,
                  vec],
        out_specs=[_rows(ts, W), _rows(ts, 2 * W), _rows(ts, W), _rows(ts, W), _rows(ts, W)],
        out_shape=[jax.ShapeDtypeStruct((S, W), BF16), jax.ShapeDtypeStruct((S, 2 * W), F32),
                   jax.ShapeDtypeStruct((S, W), F32), jax.ShapeDtypeStruct((S, W), F32),
                   jax.ShapeDtypeStruct((S, W), BF16)],
        scratch_shapes=[pltpu.VMEM((ts, W), F32), pltpu.VMEM((1, W), F32), pltpu.VMEM((CONV_HALO, W), F32)],
        compiler_params=_cp(1), name=name,
    )(x, g.reshape(1, W), w_in, reset, conv_w, conv_b, w_r, b_r, w_i, b_i, lam)


def _lru_bwd(dxo, w_out, z, xc, hseq, reset, w_r, b_r, w_i, b_i, lam, *, name):
    S = z.shape[0]
    ts = min(S, 512)
    nt = S // ts
    nh = ts // CONV_HALO
    W = D_MODEL

    def body(dxo_ref, wout_ref, gate_ref, xc_ref, h_ref, hh_ref, rs_ref, wr_ref, br_ref, wi_ref, bi_ref, lam_ref,
             dg_ref, dxc_ref, dpr_ref, dpi_ref, acc_ref, a_buf, dh_buf, carry):
        i = pl.program_id(0)
        tile = nt - 1 - i

        @pl.when(i == 0)
        def _():
            carry[...] = jnp.zeros_like(carry)
            acc_ref[...] = jnp.zeros_like(acc_ref)

        xc = xc_ref[...]
        lam_v = lam_ref[...]
        sp = _softplus(-lam_v)
        reset = rs_ref[...] > 0.5
        r, ig, a, mult = _lru_gates(xc, wr_ref, br_ref[...], wi_ref, bi_ref[...], sp, reset)
        gate = gate_ref[...]
        dyv = _dot(dxo_ref[...].astype(BF16), wout_ref[...], NT)
        h = h_ref[...]
        dg_ref[...] = (dyv * h * _gelu_grad(gate)).astype(dg_ref.dtype)
        last_row = lax.broadcasted_iota(jnp.int32, a.shape, 0) == ts - 1
        a_buf[...], dh_buf[...] = _compose_groups(jnp.where(last_row, 1.0, pltpu.roll(a, ts - 1, 0)),
                                                  dyv * _gelu(gate), True)
        _chain_groups(a_buf, dh_buf, jnp.broadcast_to(carry[...], (SUBLANES, W)), True)
        dh = dh_buf[...]
        carry[...] = a[0:1] * dh[0:1]
        hh = jnp.where(tile > 0, hh_ref[...], 0.0)
        h_prev = pltpu.roll(jnp.concatenate([hh, h], axis=0), 1, 0)[CONV_HALO:]
        da = dh * h_prev
        bx = ig * xc
        dmult = dh * bx
        dbx = dh * mult
        di = dbx * xc
        dlog_a = jnp.where(reset, 0.0, da * a - dmult * a * a / jnp.maximum(mult, 1e-30))
        dr = dlog_a * (-LRU_C) * sp
        dpre_r = dr * r * (1.0 - r)
        dpre_i = di * ig * (1.0 - ig)
        dprb, dpib = dpre_r.astype(BF16), dpre_i.astype(BF16)
        dpr_ref[...] = dprb
        dpi_ref[...] = dpib
        back = []
        for hd in range(LRU_HEADS):
            lo, hi = hd * LRU_HEAD_DIM, (hd + 1) * LRU_HEAD_DIM
            back.append(_dot(dprb[:, lo:hi], wr_ref[hd], NT) + _dot(dpib[:, lo:hi], wi_ref[hd], NT))
        dxc_ref[...] = dbx * ig + jnp.concatenate(back, axis=1)
        dlam = jnp.sum(dlog_a * (-LRU_C) * r, axis=0, keepdims=True) * (-_sigmoid(-lam_v))
        acc_ref[0:1, :] += jnp.sum(dpre_r, axis=0, keepdims=True)
        acc_ref[1:2, :] += jnp.sum(dpre_i, axis=0, keepdims=True)
        acc_ref[2:3, :] += dlam

    rev = lambda cb: pl.BlockSpec((ts, W), lambda i: (nt - 1 - i, cb))
    vec = _const((1, W))
    gw = _const((LRU_HEADS, LRU_HEAD_DIM, LRU_HEAD_DIM))
    return pl.pallas_call(
        body, grid=(nt,),
        in_specs=[rev(0), _const((W, W)), rev(0), rev(0), rev(0),
                  pl.BlockSpec((CONV_HALO, W), lambda i: (jnp.maximum((nt - 1 - i) * nh - 1, 0), 0)),
                  pl.BlockSpec((ts, 1), lambda i: (nt - 1 - i, 0)), gw, vec, gw, vec, vec],
        out_specs=[rev(0), rev(0), rev(0), rev(0), _const((8, W))],
        out_shape=[jax.ShapeDtypeStruct((S, W), BF16), jax.ShapeDtypeStruct((S, W), F32),
                   jax.ShapeDtypeStruct((S, W), BF16), jax.ShapeDtypeStruct((S, W), BF16),
                   jax.ShapeDtypeStruct((8, W), F32)],
        scratch_shapes=[pltpu.VMEM((ts, W), F32), pltpu.VMEM((ts, W), F32), pltpu.VMEM((1, W), F32)],
        compiler_params=_cp(1), name=name,
    )(dxo, w_out, z, xc, hseq, hseq, reset, w_r, b_r, w_i, b_i, lam)


def _conv_bwd(dxc, z, conv_w, *, name):
    S = dxc.shape[0]
    ts = min(S, 512)
    nh = ts // CONV_HALO
    last = S // CONV_HALO - 1
    W = D_MODEL
    n = ts + CONV_HALO

    def body(d_ref, dn_ref, xb_ref, xp_ref, cw_ref, dxb_ref, acc_ref):
        i = pl.program_id(0)

        @pl.when(i == 0)
        def _():
            acc_ref[...] = jnp.zeros_like(acc_ref)

        d = d_ref[...]
        de = jnp.concatenate([d, jnp.where(i < pl.num_programs(0) - 1, dn_ref[...], 0.0)], axis=0)
        xe = jnp.concatenate([jnp.where(i > 0, xp_ref[...], 0.0), xb_ref[...]], axis=0)
        dxb = cw_ref[3:4, :] * d
        acc_ref[3:4, :] += jnp.sum(d * xe[CONV_HALO:], axis=0, keepdims=True)
        for kk in range(CONV_WIDTH - 1):
            sh = CONV_WIDTH - 1 - kk
            dxb = dxb + cw_ref[kk:kk + 1, :] * pltpu.roll(de, n - sh, 0)[:ts]
            acc_ref[kk:kk + 1, :] += jnp.sum(d * pltpu.roll(xe, sh, 0)[CONV_HALO:], axis=0, keepdims=True)
        dxb_ref[...] = dxb.astype(dxb_ref.dtype)
        acc_ref[4:5, :] += jnp.sum(d, axis=0, keepdims=True)

    return pl.pallas_call(
        body, grid=(S // ts,),
        in_specs=[_rows(ts, W), pl.BlockSpec((CONV_HALO, W), lambda i: (jnp.minimum((i + 1) * nh, last), 0)),
                  _rows(ts, W, 1), pl.BlockSpec((CONV_HALO, W), lambda i: (jnp.maximum(i * nh - 1, 0), 1)),
                  _const((CONV_WIDTH, W))],
        out_specs=[_rows(ts, W), _const((8, W))],
        out_shape=[jax.ShapeDtypeStruct((S, W), BF16), jax.ShapeDtypeStruct((8, W), F32)],
        compiler_params=_cp(1), name=name,
    )(dxc, dxc, z, z, conv_w)


def _loss_head(x, g, target, *, name):
    S, D = x.shape
    ts = _row_tile(S)

    def body(x_ref, g_ref, t_ref, dx_ref, dg_ref, l_ref):
        @pl.when(pl.program_id(0) == 0)
        def _():
            dg_ref[...] = jnp.zeros_like(dg_ref)
            l_ref[...] = jnp.zeros_like(l_ref)

        xv = x_ref[...]
        r = lax.rsqrt(jnp.mean(xv * xv, axis=-1, keepdims=True) + RMS_EPS)
        n = xv * r
        err = n * g_ref[...] - t_ref[...]
        l_ref[...] += 0.5 * jnp.sum(jnp.sum(err * err, axis=-1, keepdims=True) * (1.0 / D), axis=0, keepdims=True)
        dy = err * (1.0 / D)
        dn = dy * g_ref[...]
        dx_ref[...] = r * (dn - n * jnp.mean(dn * n, axis=-1, keepdims=True))
        dg_ref[...] += jnp.sum(dy * n, axis=0, keepdims=True)

    return pl.pallas_call(
        body, grid=(S // ts,), in_specs=[_rows(ts, D), _const((1, D)), _rows(ts, D)],
        out_specs=[_rows(ts, D), _const((1, D)), _const((8, LANES))],
        out_shape=[jax.ShapeDtypeStruct((S, D), F32), jax.ShapeDtypeStruct((1, D), F32),
                   jax.ShapeDtypeStruct((8, LANES), F32)],
        compiler_params=_cp(1), name=name,
    )(x, g.reshape(1, D), target)


def _adamw(w, ga, gb, m, v, *, name):
    shape = w.shape
    cols = shape[-1]
    rows = w.size // cols
    br = rows
    if rows * cols * 4 > (1 << 20):
        br = max(d for d in range(8, rows + 1, 8) if rows % d == 0 and d * cols * 4 <= (1 << 20))

    def body(w_ref, ga_ref, gb_ref, m_ref, v_ref, g_ref, d_ref, mo_ref, vo_ref):
        gv = ga_ref[...] + gb_ref[...]
        g_ref[...] = gv
        mn = ADAM_B1 * m_ref[...] + (1.0 - ADAM_B1) * gv
        vn = ADAM_B2 * v_ref[...] + (1.0 - ADAM_B2) * (gv * gv)
        m_hat = mn / (1.0 - ADAM_B1 ** ADAM_STEP)
        v_hat = vn / (1.0 - ADAM_B2 ** ADAM_STEP)
        d_ref[...] = -ADAM_LR * (m_hat / (jnp.sqrt(v_hat) + ADAM_EPS) + ADAM_WD * w_ref[...])
        mo_ref[...] = mn
        vo_ref[...] = vn

    spec = _rows(br, cols)
    outs = pl.pallas_call(
        body, grid=(rows // br,), in_specs=[spec] * 5, out_specs=[spec] * 4,
        out_shape=[jax.ShapeDtypeStruct((rows, cols), F32)] * 4, compiler_params=_cp(1), name=name,
    )(*[t.reshape(rows, cols) for t in (w, ga, gb, m, v)])
    return [o.reshape(shape) for o in outs]


def _pad_heads(w, width):
    k = w.shape[0]
    return jnp.pad(w.reshape(k, MLA_HEADS, width), ((0, 0), (0, 0), (0, HEAD_PAD - width))).reshape(k, -1)


def _unpad_heads(w, width):
    k = w.shape[0]
    return w.reshape(k, MLA_HEADS, HEAD_PAD)[:, :, :width].reshape(k, MLA_HEADS * width)


def _rope_tables(positions, token=None):
    inv_freq = ROPE_BASE ** (-jnp.arange(0, QK_ROPE, 2, dtype=F32) / QK_ROPE)
    none = jnp.zeros((QK_NOPE,), F32)
    freq = jnp.concatenate([none, inv_freq, inv_freq, none[:HEAD_PAD - QK_DIM]])
    sign = jnp.concatenate([none, -jnp.ones_like(inv_freq), jnp.ones_like(inv_freq), none[:HEAD_PAD - QK_DIM]])
    pos = positions.astype(F32) if token is None else positions.astype(F32) + token[0, 0]
    ang = pos[:, None] * freq
    return jnp.cos(ang), jnp.sin(ang) * sign


def _memory_block(x, mem, W, layer, tag):
    mn = _rms(mem, W["xa_norm_mem"][layer], name=f"{tag}_xa_norm_mem")
    kvm = _mm(mn, [(W["xa_w_kv"][layer], 0, 0)], _first, [(2 * D_MODEL, BF16, 0)], tn=2 * D_MODEL, nj=1,
              name=f"{tag}_xa_kv")[0]
    xo, hx, qx, o = _xa_block_fwd(x, kvm, W["xa_w_q"][layer], W["xa_w_o"][layer], W["xa_norm_x"][layer],
                                  name=f"{tag}_xa_fwd")
    return xo, (x, hx, qx, mn, kvm, o)


def _memory_block_bwd(dxo, mem, W, layer, saved, tag, grads):
    x, hx, qx, mn, kvm, o = saved
    wq, wkv, wo = W["xa_w_q"][layer], W["xa_w_kv"][layer], W["xa_w_o"][layer]
    grads["xa_w_o"][layer] = _owner_major(_mm_tn(o, dxo, name=f"{tag}_xa_dwo"), 0)
    dx, dqx, dkvm, dg = _xa_block_bwd(dxo, x, qx, kvm, wq, wo, W["xa_norm_x"][layer], name=f"{tag}_xa_bwd")
    grads["xa_w_q"][layer] = _owner_major(_mm_tn(hx, dqx, name=f"{tag}_xa_dwq"), 0)
    grads["xa_norm_x"][layer] = dg[0]
    dmn = _mm(dkvm, [(wkv, 0, 0)], _first, [(D_MODEL, F32, 0)], nt=True, tn=D_MODEL, nj=1, name=f"{tag}_xa_dmn")[0]
    grads["xa_w_kv"][layer] = _mm_tn_owners(mn, [dkvm], name=f"{tag}_xa_dwkv")
    grads["xa_norm_mem"][layer] = _norm_gain_grad(mem, dmn, name=f"{tag}_xa_norm_mem_bwd")[0]
    return dx


FF_TN = D_FF // 2

def _silu_mul(accs, extras):
    g, u = accs
    return [g * _sigmoid(g) * u, g, u]


def _silu_mul_bwd(accs, extras):
    da = accs[0]
    g, u = extras[0].astype(F32), extras[1].astype(F32)
    sg = _sigmoid(g)
    silu = g * sg
    return [da * u * (sg + silu * (1.0 - sg)), da * silu]


def _ffn_block(x, W, layer, tag):
    hf = _rms(x, W["ffn_norm"][layer], name=f"{tag}_ffn_norm")
    wgu, wd = W["ffn_w_gate_up"][layer], W["ffn_w_down"][layer]
    act, g, u = _mm(hf, [(wgu, 0, 0), (wgu, 0, 2)], _silu_mul, [(D_FF, BF16, 0)] * 3, tn=FF_TN, nj=2,
                    name=f"{tag}_ffn_up")
    xo = _mm(act, [(wd, 0, 0)], _add_res, [(D_MODEL, F32, 0)], extras=[(x, 0)], tn=D_MODEL, nj=1,
             name=f"{tag}_ffn_down")[0]
    return xo, (x, hf, act, g, u)


def _ffn_block_bwd(dxo, W, layer, saved, tag, grads):
    x, hf, act, g, u = saved
    wgu, wd = W["ffn_w_gate_up"][layer], W["ffn_w_down"][layer]
    dg, du = _mm(dxo, [(wd, 0, 0)], _silu_mul_bwd, [(D_FF, BF16, 0)] * 2, nt=True, extras=[(g, 0), (u, 0)], tn=FF_TN,
                 nj=2, name=f"{tag}_ffn_dact")
    grads["ffn_w_down"][layer] = _owner_major(_mm_tn(act, dxo, tk=FF_TN, name=f"{tag}_ffn_dwd"), 0)
    dx, dgn = _mm(dg, [(wgu, 0, 0)], _norm_bwd_epilogue(0), [(D_MODEL, F32, 0)], nt=True, also=(du, (wgu, 0, 1)),
                  extras=[(x, 0), (dxo, 0)], rows=[W["ffn_norm"][layer].reshape(1, D_MODEL)],
                  sums=[D_MODEL], tn=D_MODEL, nj=1, name=f"{tag}_ffn_dhf")
    grads["ffn_w_gate_up"][layer] = _mm_tn_owners(hf, [dg, du], name=f"{tag}_ffn_dwgu")
    grads["ffn_norm"][layer] = dgn[0]
    return dx


def _even_block(x, tabs, W, tag):
    ctab, stab = tabs
    w_in = W["ev_w_in"][0]
    zero = jnp.zeros((D_MODEL, QK_NOPE), BF16)
    w_in_pad = jnp.concatenate([w_in[:, :896], zero, w_in[:, 896:], zero[:, :HEAD_PAD - QK_DIM]], axis=1)
    w_q_pad = _pad_heads(W["ev_w_q_up"][0], QK_DIM)
    wkv = W["ev_w_kv_up"][0].reshape(KV_RANK, MLA_HEADS, QK_NOPE + V_HEAD)
    w_kv_pad = jnp.concatenate([_pad_heads(wkv[:, :, :QK_NOPE].reshape(KV_RANK, -1), QK_NOPE),
                                _pad_heads(wkv[:, :, QK_NOPE:].reshape(KV_RANK, -1), V_HEAD)], axis=1)
    w_out = W["ev_w_out"][0]
    w_att = jnp.pad(w_out[POOL_DIM:].reshape(MLA_HEADS, V_HEAD, D_MODEL), ((0, 0), (0, HEAD_PAD - V_HEAD), (0, 0)))
    w_out_pad = jnp.concatenate([w_out[:POOL_DIM], w_att.reshape(MLA_HEADS * HEAD_PAD, D_MODEL)], axis=0)
    pool_w = W["ev_pool_w"][0].astype(BF16)
    pool_scale = W["ev_pool_scale"]

    h, z, mix, pooled, cqn, ckvn, q_rot, k_cat, v_pad = _even_front(
        x, W["ev_norm"][0], w_in_pad, pool_w, pool_scale, W["ev_q_norm"][0], w_q_pad, W["ev_kv_norm"][0], w_kv_pad,
        ctab, stab, name=f"{tag}_front")
    mix, lse = _flash_fwd(q_rot, k_cat, v_pad, mix, name=f"{tag}_attn")
    xo = _mm(mix, [(w_out_pad, 0, 0)], _add_res, [(D_MODEL, F32, 0)], extras=[(x, 0)], tn=D_MODEL, nj=1,
             name=f"{tag}_out")[0]
    saved = (x, h, z, pooled, cqn, ckvn, q_rot, k_cat, v_pad, lse, mix,
             (w_in_pad, w_q_pad, w_kv_pad, w_out_pad, pool_w, pool_scale))
    return xo, saved


def _even_out_grad(dxo, saved, tag):
    mix = saved[10]
    dw_out_pad = _mm_tn(mix, dxo, tk=MIX_DIM // 3, name=f"{tag}_dw_out")
    datt = dw_out_pad[POOL_DIM:].reshape(MLA_HEADS, HEAD_PAD, D_MODEL)[:, :V_HEAD].reshape(-1, D_MODEL)
    return [_owner_major(jnp.concatenate([dw_out_pad[:POOL_DIM], datt], axis=0), 0)]


def _even_block_bwd(dxo, tabs, W, saved, tag, grads, token=None):
    ctab, stab = tabs
    x, h, z, pooled, cqn, ckvn, q_rot, k_cat, v_pad, lse, mix, wts = saved
    w_in_pad, w_q_pad, w_kv_pad, w_out_pad, pool_w, pool_scale = wts
    if token is not None:
        w_out_pad = w_out_pad + token[0:1, 0:1].astype(BF16)
    dmix = _mm(dxo, [(w_out_pad, 0, 0)], _first, [(MIX_DIM, BF16, 0)], nt=True, tn=MIX_DIM, nj=1,
               name=f"{tag}_dmix")[0]
    delta = _attn_delta(dmix, mix, name=f"{tag}_delta")
    dq_rot, dk_cat, dv_pad = _flash_bwd(q_rot, k_cat, v_pad, dmix, _retile_rows(lse, delta.shape[2]), delta,
                                        name=f"{tag}_attn_bwd")
    dx, dq_pad, dz, dypre, dgq, dgkv, dscale, dgn = _even_back(
        dq_rot, dk_cat, dv_pad, dmix, pooled, z, x, dxo, ctab, stab, w_q_pad, w_kv_pad, w_in_pad, pool_w, pool_scale,
        W["ev_q_norm"][0], W["ev_kv_norm"][0], W["ev_norm"][0], name=f"{tag}_back")
    grads["ev_q_norm"], grads["ev_kv_norm"], grads["ev_pool_scale"], grads["ev_norm"] = dgq, dgkv, dscale, dgn
    dw_q_pad = _mm_tn(cqn, dq_pad, name=f"{tag}_dw_q_up")
    grads["ev_w_q_up"] = [_owner_major(_unpad_heads(dw_q_pad, QK_DIM), 1)]
    dwk = _unpad_heads(_mm_tn(ckvn, dk_cat, name=f"{tag}_dw_k_up"), QK_NOPE).reshape(KV_RANK, MLA_HEADS, QK_NOPE)
    dwv = _unpad_heads(_mm_tn(ckvn, dv_pad, name=f"{tag}_dw_v_up"), V_HEAD).reshape(KV_RANK, MLA_HEADS, V_HEAD)
    grads["ev_w_kv_up"] = [_owner_major(jnp.concatenate([dwk, dwv], axis=2).reshape(KV_RANK, -1), 1)]
    grads["ev_pool_w"] = _mm_tn_grouped(pooled, dypre, 4, POOL_GROUP, name=f"{tag}_dpool_w")[None]
    dw_in_pad = _mm_tn(h, dz, name=f"{tag}_dw_in")
    grads["ev_w_in"] = [_owner_major(jnp.concatenate([dw_in_pad[:, :896], dw_in_pad[:, 960:992]], axis=1), 0)]
    return dx


def _odd_block(x, reset, W, tag):
    w_r, w_i = W["od_w_rgate"][0], W["od_w_igate"][0]
    vecs = [W[n].reshape(1, D_MODEL) for n in ("od_conv_b", "od_b_rgate", "od_b_igate", "od_lambda")]
    h, z, xc, hseq, y = _lru_fwd(x, W["od_norm"][0], W["od_w_in"][0], reset, W["od_conv_w"][0], vecs[0], w_r,
                                 vecs[1], w_i, vecs[2], vecs[3], name=f"{tag}_lru")
    xo = _mm(y, [(W["od_w_out"][0], 0, 0)], _add_res, [(D_MODEL, F32, 0)], extras=[(x, 0)], tn=D_MODEL, nj=1,
             name=f"{tag}_out")[0]
    return xo, (x, h, z, xc, hseq, y, vecs)


def _odd_block_bwd(dxo, reset, W, saved, tag, grads):
    x, h, z, xc, hseq, y, vecs = saved
    w_r, w_i = W["od_w_rgate"][0], W["od_w_igate"][0]
    grads["od_w_out"] = [_owner_major(_mm_tn(y, dxo, name=f"{tag}_dw_out"), 0)]
    dgate, dxc, dpr, dpi, acc = _lru_bwd(dxo, W["od_w_out"][0], z, xc, hseq, reset, w_r, vecs[1], w_i, vecs[2],
                                         vecs[3], name=f"{tag}_lru_bwd")
    grads["od_b_rgate"], grads["od_b_igate"], grads["od_lambda"] = acc[0:1], acc[1:2], acc[2:3]
    grads["od_w_rgate"] = [_owner_major(_mm_tn_grouped(xc, dpr, LRU_HEADS, LRU_HEAD_DIM, name=f"{tag}_dw_rgate"), 1)]
    grads["od_w_igate"] = [_owner_major(_mm_tn_grouped(xc, dpi, LRU_HEADS, LRU_HEAD_DIM, name=f"{tag}_dw_igate"), 1)]
    dxb, cacc = _conv_bwd(dxc, z, W["od_conv_w"][0], name=f"{tag}_conv_bwd")
    grads["od_conv_w"], grads["od_conv_b"] = cacc[None, 0:4], cacc[4:5]
    dz = jnp.concatenate([dgate, dxb], axis=1)
    grads["od_w_in"] = [_mm_tn_owners(h, [dz], name=f"{tag}_dw_in")]
    dx, dgn = _mm(dz, [(W["od_w_in"][0], 0, 0)], _norm_bwd_epilogue(0), [(D_MODEL, F32, 0)], nt=True,
                  extras=[(x, 0), (dxo, 0)], rows=[W["od_norm"][0].reshape(1, D_MODEL)], sums=[D_MODEL], tn=D_MODEL,
                  nj=1, name=f"{tag}_dh")
    grads["od_norm"] = dgn
    return dx


def _local_step(x, mem, positions, target, W, later_weights=None, exchange_earlier=None, tabs=None):
    tabs = _rope_tables(positions) if tabs is None else tabs
    reset = (positions == 0).astype(F32)[:, None]
    grads = {n: [None, None] for n in ("xa_norm_x", "xa_norm_mem", "xa_w_q", "xa_w_kv", "xa_w_o", "ffn_norm",
                                       "ffn_w_gate_up", "ffn_w_down")}
    x1, s_even = _even_block(x, tabs, W, "l0_even")
    if later_weights is not None:
        W = {**W, **later_weights(x1)}
    x2, s_xa0 = _memory_block(x1, mem, W, 0, "l0")
    x3, s_ff0 = _ffn_block(x2, W, 0, "l0")
    x4, s_odd = _odd_block(x3, reset, W, "l1_odd")
    x5, s_xa1 = _memory_block(x4, mem, W, 1, "l1")
    x6, s_ff1 = _ffn_block(x5, W, 1, "l1")
    d, dgf, loss = _loss_head(x6, W["final_norm"], target, name="loss_head")
    grads["final_norm"] = dgf[0]
    d = _ffn_block_bwd(d, W, 1, s_ff1, "l1", grads)
    d = _memory_block_bwd(d, mem, W, 1, s_xa1, "l1", grads)
    d = _odd_block_bwd(d, reset, W, s_odd, "l1_odd", grads)
    d = _ffn_block_bwd(d, W, 0, s_ff0, "l0", grads)
    d = _memory_block_bwd(d, mem, W, 0, s_xa0, "l0", grads)
    grads["ev_w_out"] = _even_out_grad(d, s_even, "l0_even")
    token = exchange_earlier(grads) if exchange_earlier is not None else None
    d = _even_block_bwd(d, tabs, W, s_even, "l0_even", grads, token)
    big = {n: grads.pop(n) for n in MATMUL_WEIGHTS}
    for n, v in grads.items():
        if isinstance(v, list):
            grads[n] = jnp.stack(v)
    return loss[0, 0], d, big, grads


WEIGHTS = ("ev_norm", "ev_w_in", "ev_pool_w", "ev_pool_scale", "ev_q_norm", "ev_w_q_up", "ev_kv_norm", "ev_w_kv_up",
           "ev_w_out", "od_norm", "od_w_in", "od_conv_w", "od_conv_b", "od_w_rgate", "od_b_rgate", "od_w_igate",
           "od_b_igate", "od_lambda", "od_w_out", "xa_norm_x", "xa_norm_mem", "xa_w_q", "xa_w_kv", "xa_w_o",
           "ffn_norm", "ffn_w_gate_up", "ffn_w_down", "final_norm")
SHARD_AXIS = {"ev_w_in": 1, "ev_w_q_up": 2, "ev_w_kv_up": 2, "ev_w_out": 1, "od_norm": 1, "od_w_in": 2,
              "od_conv_w": 2, "od_conv_b": 1, "od_w_rgate": 2, "od_b_rgate": 1, "od_w_igate": 2, "od_b_igate": 1,
              "od_lambda": 1, "od_w_out": 1, "xa_w_q": 1, "xa_w_kv": 2, "xa_w_o": 1, "ffn_w_gate_up": 2,
              "ffn_w_down": 1}
MATMUL_WEIGHTS = ("ev_w_in", "ev_w_q_up", "ev_w_kv_up", "ev_w_out", "od_w_in", "od_w_rgate", "od_w_igate",
                  "od_w_out", "xa_w_q", "xa_w_kv", "xa_w_o", "ffn_w_gate_up", "ffn_w_down")
SMALL_SHARDED = tuple(n for n in WEIGHTS if n in SHARD_AXIS and n not in MATMUL_WEIGHTS)
REPLICATED = tuple(n for n in WEIGHTS if n not in SHARD_AXIS)


def _pack(parts, quantum):
    flat = jnp.concatenate([p.reshape(-1) for p in parts])
    pad = (-flat.shape[0]) % quantum
    return jnp.pad(flat, (0, pad)).reshape(-1, LANES)


def _unpack(flat, shapes):
    out, off = [], 0
    for shape in shapes:
        size = math.prod(shape)
        out.append(flat[off:off + size].reshape(shape))
        off += size
    return out


def _run_copies(local, remote, send_sems, recv_sems, local_sems):
    locals_ = [pltpu.make_async_copy(src, dst, local_sems.at[n]) for n, (src, dst) in enumerate(local)]
    for cp in locals_:
        cp.start()
    sends = [pltpu.make_async_remote_copy(src_ref=src, dst_ref=dst, send_sem=send_sems.at[k, n],
                                          recv_sem=recv_sems.at[k, n], device_id=dev, device_id_type=MESH)
             for (k, n, src, dst, _, dev) in remote]
    for cp in sends:
        cp.start()
    for (k, n, src, _, arrival, dev) in remote:
        pltpu.make_async_remote_copy(src_ref=src, dst_ref=arrival, send_sem=send_sems.at[k, n],
                                     recv_sem=recv_sems.at[k, n], device_id=dev, device_id_type=MESH).wait_recv()
    for cp in sends:
        cp.wait_send()
    for cp in locals_:
        cp.wait()


def _chip_peers(x, y):
    return [(1 - x, y), (x, 1 - y), (1 - x, 1 - y)]


def _owner_block(ref, axis, q):
    size = ref.shape[axis] // N_CHIPS
    idx = [slice(None)] * len(ref.shape)
    idx[axis] = pl.ds(q * size, size)
    return ref.at[tuple(idx)]


def _comm_call(body, ins, out_shapes, n_items, n_peers, *, name):
    return pl.pallas_call(
        body, in_specs=[ANY] * len(ins), out_specs=[ANY] * len(out_shapes), out_shape=out_shapes,
        scratch_shapes=[pltpu.SemaphoreType.DMA((n_peers, n_items)), pltpu.SemaphoreType.DMA((n_peers, n_items)),
                        pltpu.SemaphoreType.DMA((n_items,))],
        name=name,
    )(*ins)


HBM = pl.BlockSpec(memory_space=pltpu.HBM)
SEM = pl.BlockSpec(memory_space=pltpu.SEMAPHORE)
DATAFLOW = pltpu.SideEffectType.DATAFLOW_SIDE_EFFECTING


def _gather_plan(axes):
    return lambda srcs, lands, me, peer: [
        (srcs[i], _owner_block(lands[i], ax, me), _owner_block(lands[i], ax, peer)) for i, ax in enumerate(axes)]


def _exchange_plan(where):
    return lambda srcs, lands, me, peer: [
        (srcs[i].at[peer], lands[n].at[me, l], lands[n].at[peer, l]) for i, (n, l) in enumerate(where)]


def _split_peers(sibling):
    x, y, c = lax.axis_index("x"), lax.axis_index("y"), lax.axis_index("c")
    peers = [((px, py, c), 2 * px + py) for px, py in _chip_peers(x, y)]
    return 2 * x + y, peers + ([((x, y, 1 - c), 2 * x + y)] if sibling else [])


def _split_start(srcs, lands, plan, *, sibling=False, name):
    ns, nl = len(srcs), len(lands)
    nsem = (3 + sibling) * len(plan(list(srcs), list(lands), 0, 0))

    def body(*refs):
        src_refs, land_refs = refs[:ns], refs[ns:ns + nl]
        send_sems, recv_sems = refs[ns + nl:ns + nl + nsem], refs[ns + nl + nsem:ns + nl + 2 * nsem]
        me, peers = _split_peers(sibling)
        n = 0
        for device, chip in peers:
            for src, dst, _ in plan(src_refs, land_refs, me, chip):
                pltpu.make_async_remote_copy(src_ref=src, dst_ref=dst, send_sem=send_sems[n], recv_sem=recv_sems[n],
                                             device_id=device, device_id_type=MESH).start()
                n += 1
        refs[-1][...] = jnp.zeros_like(refs[-1])

    arrays = list(srcs) + list(lands)
    out = pl.pallas_call(
        body, name=name, in_specs=[HBM] * (ns + nl),
        out_specs=[SEM] * (2 * nsem) + [HBM] * (ns + nl) + [pl.BlockSpec(memory_space=pltpu.VMEM)],
        out_shape=[pltpu.SemaphoreType.DMA(())] * (2 * nsem) + [pltpu.HBM(a.shape, a.dtype) for a in arrays]
        + [jax.ShapeDtypeStruct((8, LANES), F32)],
        input_output_aliases={i: 2 * nsem + i for i in range(ns + nl)},
        compiler_params=pltpu.CompilerParams(has_side_effects=DATAFLOW),
    )(*[pltpu.with_memory_space_constraint(a, pltpu.HBM) for a in arrays])
    sems, rest = out[:2 * nsem], out[2 * nsem:]
    return sems[:nsem], sems[nsem:], rest[:ns], rest[ns:ns + nl], rest[-1]


def _split_wait(handle, after, plan, *, sibling=False, name):
    send_sems, recv_sems, srcs, lands, _ = handle
    ns, nl, nsem = len(srcs), len(lands), len(send_sems)

    def body(*refs):
        src_refs, land_refs = refs[:ns], refs[ns:ns + nl]
        send_refs, recv_refs = refs[ns + nl:ns + nl + nsem], refs[ns + nl + nsem:ns + nl + 2 * nsem]
        me, peers = _split_peers(sibling)
        n = 0
        for device, chip in peers:
            for src, _, arrival in plan(src_refs, land_refs, me, chip):
                cp = pltpu.make_async_remote_copy(src_ref=src, dst_ref=arrival, send_sem=send_refs[n],
                                                  recv_sem=recv_refs[n], device_id=device, device_id_type=MESH)
                cp.wait_send()
                cp.wait_recv()
                n += 1

    out = pl.pallas_call(
        body, name=name, in_specs=[HBM] * (ns + nl) + [SEM] * (2 * nsem) + [ANY], out_specs=[HBM] * (ns + nl),
        out_shape=[pltpu.HBM(a.shape, a.dtype) for a in list(srcs) + list(lands)],
        input_output_aliases={i: i for i in range(ns + nl)},
        compiler_params=pltpu.CompilerParams(has_side_effects=DATAFLOW),
    )(*srcs, *lands, *send_sems, *recv_sems, after)
    return out[ns:]


def _exchange_sibling(arrays, *, name):
    n = len(arrays)

    def body(*refs):
        x, y, c = lax.axis_index("x"), lax.axis_index("y"), lax.axis_index("c")
        remote = [(0, i, refs[i], refs[n + i], refs[n + i], (x, y, 1 - c)) for i in range(n)]
        _run_copies([], remote, *refs[2 * n:])

    return _comm_call(body, arrays, [jax.ShapeDtypeStruct(a.shape, a.dtype) for a in arrays], n, 1, name=name)


def _sum_slots(r, *, token=None, name):
    shape = r.shape[1:]
    cols = shape[-1]
    rows = math.prod(shape) // cols
    tr = max(d for d in range(8, rows + 1, 8) if rows % d == 0 and d * cols * 16 <= (4 << 20))

    def body(r_ref, *refs):
        total = ((r_ref[0] + r_ref[1]) + r_ref[2]) + r_ref[3]
        refs[-1][...] = total if token is None else total + refs[0][0:1, 0:1]

    in_specs = [pl.BlockSpec((N_CHIPS, tr, cols), lambda i: (0, i, 0))]
    in_specs += [] if token is None else [_const((8, LANES))]
    return pl.pallas_call(
        body, grid=(rows // tr,), in_specs=in_specs,
        out_specs=_rows(tr, cols), out_shape=jax.ShapeDtypeStruct((rows, cols), F32), compiler_params=_cp(1),
        name=name,
    )(r.reshape(N_CHIPS, rows, cols), *([] if token is None else [token])).reshape(shape)


FIRST_WEIGHTS = ("ev_w_in", "ev_w_q_up", "ev_w_kv_up", "ev_w_out")
LATER_WEIGHTS = tuple(n for n in MATMUL_WEIGHTS if n not in FIRST_WEIGHTS)
LAST_GRADS = ("ev_w_in", "ev_w_q_up", "ev_w_kv_up")
EARLIER_GRADS = tuple(n for n in MATMUL_WEIGHTS if n not in LAST_GRADS)


def _my_chip():
    return 2 * lax.axis_index("x") + lax.axis_index("y")


def _gather_first(w, work):
    small = _pack([w[n] for n in SMALL_SHARDED], 8 * LANES)
    stacked = [n for n in FIRST_WEIGHTS if SHARD_AXIS[n] == w[n].ndim - 1 and w[n].shape[-1] % LANES]
    shards = [w[n].astype(BF16)[None] if n in stacked else w[n].astype(BF16) for n in FIRST_WEIGHTS] + [small]
    axes = [0 if n in stacked else SHARD_AXIS[n] for n in FIRST_WEIGHTS] + [0]
    plan = _gather_plan(axes)
    lands = [lax.empty(tuple(d * (N_CHIPS if a == ax else 1) for a, d in enumerate(s.shape)), s.dtype)
             for s, ax in zip(shards, axes)]
    handle = _split_start(shards, lands, plan, sibling=True, name="gather_first_start")
    done = work(handle[4])
    got = _split_wait(handle, done[0], plan, sibling=True, name="gather_first_wait")
    full = {n: w[n] for n in REPLICATED}
    for n, g in zip(FIRST_WEIGHTS, got[:-1]):
        full[n] = jnp.concatenate([g[q] for q in range(N_CHIPS)], axis=SHARD_AXIS[n]) if n in stacked else g
    per_chip = [_unpack(got[-1][q * small.shape[0]:(q + 1) * small.shape[0]].reshape(-1),
                        [w[n].shape for n in SMALL_SHARDED]) for q in range(N_CHIPS)]
    for i, n in enumerate(SMALL_SHARDED):
        full[n] = jnp.concatenate([per_chip[q][i] for q in range(N_CHIPS)], axis=SHARD_AXIS[n])
    return full, done


def _gather_later_start(w, after):
    behind = (after.reshape(-1)[0] * 0).astype(BF16)
    shards = [w[n].astype(BF16) + (behind if n == "od_w_rgate" else 0) for n in LATER_WEIGHTS]
    axes = [SHARD_AXIS[n] for n in LATER_WEIGHTS]
    lands = [lax.empty(tuple(d * (N_CHIPS if a == ax else 1) for a, d in enumerate(s.shape)), s.dtype)
             for s, ax in zip(shards, axes)]
    plan = _gather_plan(axes)
    return _split_start(shards, lands, plan, sibling=True, name="gather_later_start"), plan


def _owner_major(g, axis):
    shape = g.shape
    size = shape[axis] // N_CHIPS
    g = jnp.moveaxis(g.reshape(shape[:axis] + (N_CHIPS, size) + shape[axis + 1:]), axis, 0)
    return g.reshape(N_CHIPS, -1, shape[-1] if axis < len(shape) - 1 else size)


def _exchange_start(items, *, cross, name):
    me = _my_chip()
    srcs, lands, where = [], [], []
    for n, layers in enumerate(items):
        land = lax.empty((N_CHIPS, len(layers)) + layers[0].shape[1:], layers[0].dtype)
        for l, a in enumerate(layers):
            if not cross:
                own = lax.dynamic_index_in_dim(a, me, 0, keepdims=True)[:, None]
                land = lax.dynamic_update_slice(land, own, (me, l) + (0,) * (a.ndim - 1))
            srcs.append(a)
            where.append((n, l))
        lands.append(land)
    plan = _exchange_plan(where)
    return _split_start(srcs, lands, plan, sibling=cross, name=name), plan


def _earlier_items(grads, full_shapes):
    small = [_pack([jnp.split(grads[n].reshape(full_shapes[n]), N_CHIPS, axis=SHARD_AXIS[n])[q]
                    for n in SMALL_SHARDED], 8 * LANES) for q in range(N_CHIPS)]
    return [grads[n] for n in EARLIER_GRADS] + [[jnp.stack(small)]]


def _last_items(big, grads, full_shapes, loss):
    repl = _pack([grads[n].reshape(full_shapes[n]) for n in REPLICATED] + [loss.reshape(1)], 8 * LANES)
    return [big[n] for n in LAST_GRADS] + [[jnp.stack([repl] * N_CHIPS)]]


def kernel(
        x, mem, positions, ev_norm, ev_w_in, ev_pool_w, ev_pool_scale, ev_q_norm, ev_w_q_up, ev_kv_norm,
        ev_w_kv_up, ev_w_out, od_norm, od_w_in, od_conv_w, od_conv_b, od_w_rgate, od_b_rgate, od_w_igate,
        od_b_igate, od_lambda, od_w_out, xa_norm_x, xa_norm_mem, xa_w_q, xa_w_kv, xa_w_o, ffn_norm,
        ffn_w_gate_up, ffn_w_down, final_norm, loss_target, m_ev_norm, m_ev_w_in, m_ev_pool_w, m_ev_pool_scale,
        m_ev_q_norm, m_ev_w_q_up, m_ev_kv_norm, m_ev_w_kv_up, m_ev_w_out, m_od_norm, m_od_w_in, m_od_conv_w,
        m_od_conv_b, m_od_w_rgate, m_od_b_rgate, m_od_w_igate, m_od_b_igate, m_od_lambda, m_od_w_out,
        m_xa_norm_x, m_xa_norm_mem, m_xa_w_q, m_xa_w_kv, m_xa_w_o, m_ffn_norm, m_ffn_w_gate_up, m_ffn_w_down,
        m_final_norm, v_ev_norm, v_ev_w_in, v_ev_pool_w, v_ev_pool_scale, v_ev_q_norm, v_ev_w_q_up,
        v_ev_kv_norm, v_ev_w_kv_up, v_ev_w_out, v_od_norm, v_od_w_in, v_od_conv_w, v_od_conv_b, v_od_w_rgate,
        v_od_b_rgate, v_od_w_igate, v_od_b_igate, v_od_lambda, v_od_w_out, v_xa_norm_x, v_xa_norm_mem, v_xa_w_q,
        v_xa_w_kv, v_xa_w_o, v_ffn_norm, v_ffn_w_gate_up, v_ffn_w_down, v_final_norm):
    given = dict(locals())
    w = {n: given[n] for n in WEIGHTS}
    full_shapes = {n: tuple(d * (N_CHIPS if a == SHARD_AXIS.get(n) else 1) for a, d in enumerate(w[n].shape))
                   for n in WEIGHTS}
    full, tabs = _gather_first(w, lambda token: _rope_tables(positions[0], token))
    later, later_plan = _gather_later_start(w, full["ev_w_out"])
    full["ev_norm"] = full["ev_norm"] + later[4][0:1, 0:1]
    exchange = {}

    def later_weights(after):
        return dict(zip(LATER_WEIGHTS, _split_wait(later, after, later_plan, sibling=True, name="gather_later_wait")))

    def exchange_earlier(grads):
        exchange["handle"], exchange["plan"] = _exchange_start(_earlier_items(grads, full_shapes), cross=True,
                                                               name="exchange_earlier_start")
        return exchange["handle"][4]

    loss, grad_x, big, grads = _local_step(x[0], mem[0], positions[0], loss_target[0], full, later_weights,
                                           exchange_earlier, tabs)
    earlier = EARLIER_GRADS + ("small",)
    got = dict(zip(earlier, _split_wait(exchange["handle"], grad_x, exchange["plan"], sibling=True,
                                        name="exchange_earlier_wait")))
    last, last_plan = _exchange_start(_last_items(big, grads, full_shapes, loss), cross=False,
                                      name="exchange_last_start")
    out = {}

    def finish(names, landed, token, tag):
        mine = [_sum_slots(landed[n], token=token if i == 0 else None, name=f"sum_chips_{n}")
                for i, n in enumerate(names)]
        other = _exchange_sibling(mine, name=f"exchange_sibling_{tag}")
        total = None
        for n, a, b in zip(names, mine, other):
            if n in MATMUL_WEIGHTS:
                out[n] = _adamw(w[n], a.reshape(w[n].shape), b.reshape(w[n].shape), given["m_" + n], given["v_" + n],
                                name=f"adamw_{n}")
                continue
            group = SMALL_SHARDED if n == "small" else REPLICATED
            spare = [jnp.zeros((1,), F32)] if group is REPLICATED else []
            packed = [_pack([given[pre + k] for k in group] + spare, 8 * LANES) for pre in ("", "m_", "v_")]
            res = _adamw(packed[0], a.reshape(packed[0].shape), b.reshape(packed[0].shape), packed[1], packed[2],
                         name=f"adamw_{n}")
            shapes = [w[k].shape for k in group] + [(1,)] * len(spare)
            for j, arrs in enumerate(zip(*[_unpack(r.reshape(-1), shapes) for r in res])):
                if j < len(group):
                    out[group[j]] = list(arrs)
                else:
                    total = arrs[0][0]
        return total

    finish(earlier, got, last[4], "earlier")
    names = LAST_GRADS + ("replicated",)
    got = dict(zip(names, _split_wait(last, out[EARLIER_GRADS[-1]][1], last_plan, name="exchange_last_wait")))
    loss = finish(names, got, None, "last")
    return (loss, grad_x[None], *[out[n][k] for k in range(4) for n in WEIGHTS])
```

```python
import math

import jax
import jax.numpy as jnp
from jax import lax
from jax.experimental import pallas as pl
from jax.experimental.pallas import tpu as pltpu

F32 = jnp.float32
BF16 = jnp.bfloat16

D_MODEL = 1024
POOL_DIM = 512
POOL_WINDOWS = (2, 4, 8, 16)
POOL_GROUP = 128
MLA_HEADS = 8
QK_NOPE = 64
QK_ROPE = 32
QK_DIM = QK_NOPE + QK_ROPE
V_HEAD = 64
HEAD_PAD = 128
Q_RANK = 256
KV_RANK = 128
ROPE_BASE = 10000.0
LRU_HEADS = 4
LRU_HEAD_DIM = 256
CONV_WIDTH = 4
LRU_C = 8.0
MEM_HEADS = 4
MEM_HEAD_DIM = 256
D_FF = 2816
RMS_EPS = 1e-6
NEG_INF = -1e30

ADAM_LR = 0.001
ADAM_B1 = 0.9
ADAM_B2 = 0.999
ADAM_EPS = 1e-08
ADAM_WD = 0.01
ADAM_STEP = 10

N_CHIPS = 4
LANES = 128
VMEM_LIMIT = 56 * 1024 * 1024
MESH = pl.DeviceIdType.MESH
ANY = pl.BlockSpec(memory_space=pl.ANY)
MIX_DIM = POOL_DIM + MLA_HEADS * HEAD_PAD

NN = (((1,), (0,)), ((), ()))
NT = (((1,), (1,)), ((), ()))
TN = (((0,), (0,)), ((), ()))


def _cp(n):
    return pltpu.CompilerParams(dimension_semantics=("arbitrary",) * n, vmem_limit_bytes=VMEM_LIMIT)


def _dot(a, b, dims=NN):
    return lax.dot_general(a, b, dims, preferred_element_type=F32)


def _row_tile(S):
    return 1024 if S % 1024 == 0 else min(S, 512)


def _rows(ts, w, cb=0):
    return pl.BlockSpec((ts, w), lambda i: (i, cb))


def _const(shape):
    return pl.BlockSpec(shape, lambda i: (0,) * len(shape))


MM_VMEM_BUDGET = 46 * 1024 * 1024


def _mm(a, bs, epi, outs, *, tn, nj, nt=False, also=None, extras=(), rows=(), sums=(), a_cb=0, k=None, tm=None,
        name):
    M = a.shape[0]
    k = k or a.shape[1]
    nb, ne, nr, no = len(bs), len(extras), len(rows), len(outs)
    lhs = [(a, k, a_cb, b) for b in bs[:1]] + ([(also[0], also[0].shape[1], 0, also[1])] if also else [])
    if tm is None:
        per_row = 2 * (sum(kk * x.dtype.itemsize for x, kk, _, _ in lhs)
                       + sum(e.dtype.itemsize for e, _ in extras) * tn
                       + sum(jnp.dtype(dt).itemsize for _, dt, _ in outs) * tn) + nb * tn * 4
        weights = (1 if nj == 1 else 2) * (sum(b.dtype.itemsize for b, _, _ in bs) * k
                                           + (also[1][0].dtype.itemsize * lhs[-1][1] if also else 0)) * tn
        tm = 1024 if M % 1024 == 0 and 1024 * per_row + weights <= MM_VMEM_BUDGET else min(M, 512)
    dims = NT if nt else NN
    assert not sums or nj == 1
    na = 2 if also else 0

    def body(*refs):
        av = refs[0][...].astype(BF16)
        accs = [_dot(av, r[...].astype(BF16), dims) for r in refs[1:1 + nb]]
        if also:
            accs[0] = accs[0] + _dot(refs[1 + nb][...].astype(BF16), refs[2 + nb][...].astype(BF16), dims)
        refs = refs[:1 + nb] + refs[1 + nb + na:]
        vals = epi(accs, [r[...] for r in refs[1 + nb:1 + nb + ne + nr]])
        outs_refs = refs[1 + nb + ne + nr:]
        for o, v in zip(outs_refs[:no], vals[:no]):
            o[...] = v.astype(o.dtype)
        if sums:
            @pl.when(pl.program_id(1) == 0)
            def _():
                for o in outs_refs[no:]:
                    o[...] = jnp.zeros_like(o)

            for o, v in zip(outs_refs[no:], vals[no:]):
                o[...] += v

    in_specs = [pl.BlockSpec((tm, k), lambda j, i: (i, a_cb))]
    weights = [(k, rb, cb) for (_, rb, cb) in bs]
    if also:
        in_specs_also = pl.BlockSpec((tm, lhs[-1][1]), lambda j, i: (i, 0))
        weights.append((lhs[-1][1], also[1][1], also[1][2]))
    for n, (kk, rb, cb) in enumerate(weights):
        if also and n == nb:
            in_specs.append(in_specs_also)
        mode = dict(pipeline_mode=pl.Buffered(1)) if nj == 1 else {}
        if nt:
            in_specs.append(pl.BlockSpec((tn, kk), lambda j, i, rb=rb, cb=cb: (rb + j, cb), **mode))
        else:
            in_specs.append(pl.BlockSpec((kk, tn), lambda j, i, rb=rb, cb=cb: (rb, cb + j), **mode))
    for (_, cb) in extras:
        in_specs.append(pl.BlockSpec((tm, tn), lambda j, i, cb=cb: (i, cb + j)))
    in_specs += [pl.BlockSpec((1, tn), lambda j, i: (0, 0))] * nr
    out_specs = [pl.BlockSpec((tm, tn), lambda j, i, cb=cb: (i, cb + j)) for (_, _, cb) in outs]
    out_specs += [pl.BlockSpec((1, w), lambda j, i: (0, 0)) for w in sums]
    res = pl.pallas_call(
        body, grid=(nj, M // tm), in_specs=in_specs, out_specs=out_specs,
        out_shape=[jax.ShapeDtypeStruct((M, n), dt) for (n, dt, _) in outs]
        + [jax.ShapeDtypeStruct((1, w), F32) for w in sums],
        compiler_params=_cp(2), name=name,
    )(a, *[b for (b, _, _) in bs], *([also[0], also[1][0]] if also else []), *[e for (e, _) in extras], *rows)
    return res


def _first(accs, extras):
    return [accs[0]]


def _add_res(accs, extras):
    return [accs[0] + extras[0].astype(F32)]


def _norm_bwd_epilogue(partials):
    def epi(accs, vals):
        dh = accs[0]
        for part in vals[:partials]:
            dh = dh + part.astype(F32)
        x, res, g = vals[partials:partials + 3]
        r = lax.rsqrt(jnp.mean(x * x, axis=-1, keepdims=True) + RMS_EPS)
        n = x * r
        dn = dh * g
        return [r * (dn - n * jnp.mean(dn * n, axis=-1, keepdims=True)) + res, jnp.sum(dh * n, axis=0, keepdims=True)]

    return epi


TN_VMEM_BUDGET = 44 * 1024 * 1024


def _contraction_rows(S, row_bytes, out_elems):
    ts = min(S, 2048)
    while ts > 512 and 2 * (ts * row_bytes + out_elems * 4) > TN_VMEM_BUDGET:
        ts //= 2
    return ts


def _mm_tn(a, b, *, ka=None, a_cb=0, nb=None, b_cb=0, tk=None, tn=None, ts=None, name):
    S = a.shape[0]
    ka = ka or a.shape[1]
    nb = nb or b.shape[1]
    tk = tk or ka
    tn = tn or nb
    ts = ts or _contraction_rows(S, tk * a.dtype.itemsize + tn * b.dtype.itemsize, tk * tn)
    a0, b0 = a_cb * (ka // tk), b_cb * (nb // tn)

    def body(a_ref, b_ref, o_ref):
        @pl.when(pl.program_id(2) == 0)
        def _():
            o_ref[...] = jnp.zeros_like(o_ref)

        o_ref[...] += _dot(a_ref[...].astype(BF16), b_ref[...].astype(BF16), TN)

    return pl.pallas_call(
        body, grid=(ka // tk, nb // tn, S // ts),
        in_specs=[pl.BlockSpec((ts, tk), lambda p, q, s: (s, a0 + p)),
                  pl.BlockSpec((ts, tn), lambda p, q, s: (s, b0 + q))],
        out_specs=pl.BlockSpec((tk, tn), lambda p, q, s: (p, q)),
        out_shape=jax.ShapeDtypeStruct((ka, nb), F32), compiler_params=_cp(3), name=name,
    )(a, b)


def _mm_tn_owners(a, bs, *, name):
    S, ka = a.shape
    nb = sum(b.shape[1] for b in bs)
    tn = nb // N_CHIPS
    ts = _contraction_rows(S, ka * a.dtype.itemsize + len(bs) * tn * bs[0].dtype.itemsize, ka * tn)
    per = N_CHIPS // len(bs)

    def body(a_ref, *refs):
        o_ref = refs[-1]
        q = pl.program_id(0)

        @pl.when(pl.program_id(1) == 0)
        def _():
            o_ref[...] = jnp.zeros_like(o_ref)

        av = a_ref[...].astype(BF16)
        for n, b_ref in enumerate(refs[:-1]):
            @pl.when(q // per == n)
            def _():
                o_ref[0] += _dot(av, b_ref[...].astype(BF16), TN)

    in_specs = [pl.BlockSpec((ts, ka), lambda q, s: (s, 0))]
    for n in range(len(bs)):
        in_specs.append(pl.BlockSpec((ts, tn), lambda q, s, n=n: (jnp.where(q // per == n, s, 0),
                                                                  jnp.clip(q - n * per, 0, per - 1))))
    return pl.pallas_call(
        body, grid=(N_CHIPS, S // ts), in_specs=in_specs,
        out_specs=pl.BlockSpec((1, ka, tn), lambda q, s: (q, 0, 0)),
        out_shape=jax.ShapeDtypeStruct((N_CHIPS, ka, tn), F32), compiler_params=_cp(2), name=name,
    )(a, *bs)


def _mm_tn_grouped(a, b, groups, w, *, name):
    S = a.shape[0]
    ts = _contraction_rows(S, w * (a.dtype.itemsize + b.dtype.itemsize), w * w)

    def body(a_ref, b_ref, o_ref):
        @pl.when(pl.program_id(1) == 0)
        def _():
            o_ref[...] = jnp.zeros_like(o_ref)

        o_ref[0] += _dot(a_ref[...].astype(BF16), b_ref[...].astype(BF16), TN)

    return pl.pallas_call(
        body, grid=(groups, S // ts),
        in_specs=[pl.BlockSpec((ts, w), lambda g, s: (s, g)), pl.BlockSpec((ts, w), lambda g, s: (s, g))],
        out_specs=pl.BlockSpec((1, w, w), lambda g, s: (g, 0, 0)),
        out_shape=jax.ShapeDtypeStruct((groups, w, w), F32), compiler_params=_cp(2), name=name,
    )(a, b)


def _rms(x, g, *, name):
    S, w = x.shape
    ts = _row_tile(S)

    def body(x_ref, g_ref, o_ref):
        xv = x_ref[...]
        r = lax.rsqrt(jnp.mean(xv * xv, axis=-1, keepdims=True) + RMS_EPS)
        o_ref[...] = (xv * r * g_ref[...]).astype(o_ref.dtype)

    return pl.pallas_call(
        body, grid=(S // ts,), in_specs=[_rows(ts, w), _const((1, w))], out_specs=_rows(ts, w),
        out_shape=jax.ShapeDtypeStruct((S, w), BF16), compiler_params=_cp(1), name=name,
    )(x, g.reshape(1, w))


def _norm_gain_grad(x, dy, *, name):
    S, w = x.shape
    ts = _row_tile(S)

    def body(x_ref, dy_ref, dg_ref):
        @pl.when(pl.program_id(0) == 0)
        def _():
            dg_ref[...] = jnp.zeros_like(dg_ref)

        xv = x_ref[...]
        r = lax.rsqrt(jnp.mean(xv * xv, axis=-1, keepdims=True) + RMS_EPS)
        dg_ref[...] += jnp.sum(dy_ref[...] * (xv * r), axis=0, keepdims=True)

    return pl.pallas_call(
        body, grid=(S // ts,), in_specs=[_rows(ts, w), _rows(ts, w)], out_specs=_const((1, w)),
        out_shape=jax.ShapeDtypeStruct((1, w), F32), compiler_params=_cp(1), name=name,
    )(x, dy)


HALO = 16


def _pool_counts(i, ts, rows, first_row):
    t = i * ts + first_row + lax.broadcasted_iota(jnp.int32, (rows, 1), 0)
    return [jnp.minimum(t + 1, w).astype(F32) for w in POOL_WINDOWS]


def _even_front(x, g, w_in, pool_w, pool_scale, g_q, w_q, g_kv, w_kv, ctab, stab, *, name):
    S = x.shape[0]
    ts = min(S, 512)

    def body(x_ref, g_ref, win_ref, pw_ref, sc_ref, gq_ref, wq_ref, gkv_ref, wkv_ref, c_ref, s_ref,
             h_ref, z_ref, y_ref, p_ref, cqn_ref, ckvn_ref, q_ref, k_ref, v_ref, tail):
        i = pl.program_id(0)

        def normed(t, gain):
            r = lax.rsqrt(jnp.mean(t * t, axis=-1, keepdims=True) + RMS_EPS)
            return (t * r * gain).astype(BF16)

        h = normed(x_ref[...], g_ref[...])
        h_ref[...] = h
        z = _dot(h, win_ref[...])
        z_ref[...] = z
        u = z[:, :POOL_DIM]
        xe = jnp.concatenate([jnp.where(i > 0, tail[...], 0.0), u], axis=0)
        tail[...] = u[ts - HALO:]
        sums = []
        s = xe
        for sh in (1, 2, 4, 8):
            s = s + pltpu.roll(s, sh, 0)
            sums.append(s)
        cnts = _pool_counts(i, ts, ts, 0)
        for grp in range(4):
            lo, hi = grp * POOL_GROUP, (grp + 1) * POOL_GROUP
            pooled = (sums[grp][HALO:, lo:hi] / cnts[grp] - u[:, lo:hi]).astype(BF16)
            p_ref[:, lo:hi] = pooled
            y_ref[:, lo:hi] = (_dot(pooled, pw_ref[grp]) * sc_ref[:, lo:hi]).astype(y_ref.dtype)
        cqn = normed(z[:, POOL_DIM:POOL_DIM + Q_RANK], gq_ref[...])
        ckvn = normed(z[:, POOL_DIM + Q_RANK:POOL_DIM + Q_RANK + KV_RANK], gkv_ref[...])
        cqn_ref[...] = cqn
        ckvn_ref[...] = ckvn
        q = _dot(cqn, wq_ref[...])
        kv = _dot(ckvn, wkv_ref[...])
        c, sn = c_ref[...], s_ref[...]
        kr = z[:, D_MODEL - HEAD_PAD:]
        kr_rot = kr * c + _rope_partner(kr) * sn
        lane = lax.broadcasted_iota(jnp.int32, (ts, HEAD_PAD), 1)
        for hd in range(MLA_HEADS):
            lo, hi = hd * HEAD_PAD, (hd + 1) * HEAD_PAD
            qh = q[:, lo:hi]
            q_ref[:, lo:hi] = (qh * c + _rope_partner(qh) * sn).astype(q_ref.dtype)
            k_ref[:, lo:hi] = (kv[:, lo:hi] + kr_rot).astype(k_ref.dtype)
            v_ref[:, lo:hi] = jnp.where(lane == V_HEAD, 1.0, kv[:, D_MODEL + lo:D_MODEL + hi]).astype(v_ref.dtype)

    wide = jax.ShapeDtypeStruct((S, D_MODEL), BF16)
    return pl.pallas_call(
        body, grid=(S // ts,),
        in_specs=[_rows(ts, D_MODEL), _const((1, D_MODEL)), _const((D_MODEL, D_MODEL)),
                  _const((4, POOL_GROUP, POOL_GROUP)), _const((1, POOL_DIM)), _const((1, Q_RANK)),
                  _const((Q_RANK, D_MODEL)), _const((1, KV_RANK)), _const((KV_RANK, 2 * D_MODEL)),
                  _rows(ts, HEAD_PAD), _rows(ts, HEAD_PAD)],
        out_specs=[_rows(ts, D_MODEL), _rows(ts, D_MODEL), _rows(ts, POOL_DIM), _rows(ts, POOL_DIM),
                   _rows(ts, Q_RANK), _rows(ts, KV_RANK), _rows(ts, D_MODEL), _rows(ts, D_MODEL), _rows(ts, D_MODEL)],
        out_shape=[wide, jax.ShapeDtypeStruct((S, D_MODEL), F32), jax.ShapeDtypeStruct((S, MIX_DIM), BF16),
                   jax.ShapeDtypeStruct((S, POOL_DIM), BF16), jax.ShapeDtypeStruct((S, Q_RANK), BF16),
                   jax.ShapeDtypeStruct((S, KV_RANK), BF16), wide, wide, wide],
        scratch_shapes=[pltpu.VMEM((HALO, POOL_DIM), F32)], compiler_params=_cp(1), name=name,
    )(x, g.reshape(1, D_MODEL), w_in, pool_w, pool_scale, g_q.reshape(1, Q_RANK), w_q, g_kv.reshape(1, KV_RANK), w_kv,
      ctab, stab)


def _norm_bwd_values(xv, gain, dy):
    r = lax.rsqrt(jnp.mean(xv * xv, axis=-1, keepdims=True) + RMS_EPS)
    n = xv * r
    dn = dy * gain
    return r * (dn - n * jnp.mean(dn * n, axis=-1, keepdims=True)), jnp.sum(dy * n, axis=0, keepdims=True)


def _even_back(dq_rot, dk_cat, dv, dmix, pooled, z, x, dxo, ctab, stab, w_q, w_kv, w_in, pool_w, pool_scale, g_q, g_kv,
               g_x, *, name):
    S = x.shape[0]
    ts = min(S, 512)
    nh = ts // HALO
    last = S // HALO - 1
    n = ts + HALO

    def body(dq_ref, dk_ref, dv_ref, dy_ref, dyh_ref, p_ref, z_ref, x_ref, dxo_ref, c_ref, s_ref, wq_ref, wkv_ref,
             win_ref, pw_ref, sc_ref, gq_ref, gkv_ref, gx_ref,
             dx_ref, dqp_ref, dz_ref, dyp_ref, dgq_ref, dgkv_ref, dsc_ref, dgx_ref):
        i = pl.program_id(0)

        @pl.when(i == 0)
        def _():
            for ref in (dgq_ref, dgkv_ref, dsc_ref, dgx_ref):
                ref[...] = jnp.zeros_like(ref)

        c, sn = c_ref[...], s_ref[...]
        z = z_ref[...]
        dk = dk_ref[...]
        for hd in range(MLA_HEADS):
            lo, hi = hd * HEAD_PAD, (hd + 1) * HEAD_PAD
            g = dq_ref[:, lo:hi]
            dqp_ref[:, lo:hi] = (g * c + _rope_partner(g * sn)).astype(dqp_ref.dtype)
            heads_sum = dk[:, lo:hi] if hd == 0 else heads_sum + dk[:, lo:hi]
        lane = lax.broadcasted_iota(jnp.int32, heads_sum.shape, 1)
        dkr = jnp.where((lane >= QK_NOPE) & (lane < QK_DIM), heads_sum * c + _rope_partner(heads_sum * sn), 0.0)
        dcqn = _dot(dqp_ref[...], wq_ref[...], NT)
        dckvn = _dot(dk.astype(BF16), wkv_ref[:, :D_MODEL], NT) + _dot(dv_ref[...].astype(BF16),
                                                                       wkv_ref[:, D_MODEL:], NT)
        dcq, dgq = _norm_bwd_values(z[:, POOL_DIM:POOL_DIM + Q_RANK], gq_ref[...], dcqn)
        dckv, dgkv = _norm_bwd_values(z[:, POOL_DIM + Q_RANK:POOL_DIM + Q_RANK + KV_RANK], gkv_ref[...], dckvn)
        dgq_ref[...] += dgq
        dgkv_ref[...] += dgkv
        dyv = dy_ref[...].astype(F32)
        dyh = jnp.where(i < pl.num_programs(0) - 1, dyh_ref[...].astype(F32), 0.0)
        dypre = (jnp.concatenate([dyv, dyh], axis=0) * sc_ref[...]).astype(BF16)
        dyp_ref[...] = dypre[:ts]
        cnts = _pool_counts(i, ts, n, 0)
        dsc = []
        for grp in range(4):
            lo, hi = grp * POOL_GROUP, (grp + 1) * POOL_GROUP
            dsc.append(jnp.sum(dyv[:, lo:hi] * _dot(p_ref[:, lo:hi], pw_ref[grp]), axis=0, keepdims=True))
            dpool = _dot(dypre[:, lo:hi], pw_ref[grp], NT)
            s = dpool / cnts[grp]
            for sh in (1, 2, 4, 8)[:grp + 1]:
                s = s + pltpu.roll(s, n - sh, 0)
            dz_ref[:, lo:hi] = (s[:ts] - dpool[:ts]).astype(dz_ref.dtype)
        dsc_ref[...] += jnp.concatenate(dsc, axis=1)
        dz_ref[:, POOL_DIM:POOL_DIM + Q_RANK] = dcq.astype(dz_ref.dtype)
        dz_ref[:, POOL_DIM + Q_RANK:POOL_DIM + Q_RANK + KV_RANK] = dckv.astype(dz_ref.dtype)
        dz_ref[:, D_MODEL - HEAD_PAD:] = dkr.astype(dz_ref.dtype)
        dx, dgx = _norm_bwd_values(x_ref[...], gx_ref[...], _dot(dz_ref[...], win_ref[...], NT))
        dx_ref[...] = dx + dxo_ref[...]
        dgx_ref[...] += dgx

    wide, pool = _rows(ts, D_MODEL), _rows(ts, POOL_DIM)
    f32 = lambda w: jax.ShapeDtypeStruct((1, w), F32)
    return pl.pallas_call(
        body, grid=(S // ts,),
        in_specs=[wide, wide, wide, pool,
                  pl.BlockSpec((HALO, POOL_DIM), lambda i: (jnp.minimum((i + 1) * nh, last), 0)), pool, wide, wide, wide,
                  _rows(ts, HEAD_PAD), _rows(ts, HEAD_PAD), _const((Q_RANK, D_MODEL)), _const((KV_RANK, 2 * D_MODEL)),
                  _const((D_MODEL, D_MODEL)), _const((4, POOL_GROUP, POOL_GROUP)), _const((1, POOL_DIM)),
                  _const((1, Q_RANK)), _const((1, KV_RANK)), _const((1, D_MODEL))],
        out_specs=[wide, wide, wide, pool, _const((1, Q_RANK)), _const((1, KV_RANK)), _const((1, POOL_DIM)),
                   _const((1, D_MODEL))],
        out_shape=[jax.ShapeDtypeStruct((S, D_MODEL), F32), jax.ShapeDtypeStruct((S, D_MODEL), BF16),
                   jax.ShapeDtypeStruct((S, D_MODEL), BF16), jax.ShapeDtypeStruct((S, POOL_DIM), BF16),
                   f32(Q_RANK), f32(KV_RANK), f32(POOL_DIM), f32(D_MODEL)],
        compiler_params=_cp(1), name=name,
    )(dq_rot, dk_cat, dv, dmix, dmix, pooled, z, x, dxo, ctab, stab, w_q, w_kv, w_in, pool_w, pool_scale,
      g_q.reshape(1, Q_RANK), g_kv.reshape(1, KV_RANK), g_x.reshape(1, D_MODEL))


def _rope_partner(t):
    lane = lax.broadcasted_iota(jnp.int32, t.shape, 1)
    swapped = jnp.where(lane < QK_NOPE + QK_ROPE // 2, pltpu.roll(t, HEAD_PAD - QK_ROPE // 2, 1),
                        pltpu.roll(t, QK_ROPE // 2, 1))
    return jnp.where((lane >= QK_NOPE) & (lane < QK_DIM), swapped, 0.0)


ATT_SCALE = QK_DIM ** -0.5
LOG2E = math.log2(math.e)


HEADS_PER_STEP = 2
ATT_COL0 = POOL_DIM // HEAD_PAD


FWD_TILE = 1024


def _stat_rows(col):
    return jnp.broadcast_to(col, (col.shape[0], LANES)).T[0:8]


def _retile_rows(rows, tq):
    heads, n8, t = rows.shape
    if t == tq:
        return rows
    flat = rows.reshape(heads, n8 // 8, 8, t)[:, :, 0].reshape(heads, -1, 1, tq)
    return jnp.broadcast_to(flat, (heads, flat.shape[1], 8, tq)).reshape(heads, -1, tq)


def _flash_fwd(q, k, v, mix, *, name):
    S = q.shape[0]
    tq = FWD_TILE if S % FWD_TILE == 0 else min(S, 512)
    nq = S // tq
    hs = HEADS_PER_STEP
    wide = hs * HEAD_PAD

    def body(q_ref, k_ref, v_ref, mix_ref, o_ref, lse_ref):
        qi = pl.program_id(1)
        qv = [q_ref[:, a * HEAD_PAD:(a + 1) * HEAD_PAD] for a in range(hs)]

        def update(m, acc, s, v):
            m_new = jnp.maximum(m, jnp.max(s, axis=-1, keepdims=True))
            p = jnp.exp2((s - m_new) * (ATT_SCALE * LOG2E))
            alpha = jnp.exp2((m - m_new) * (ATT_SCALE * LOG2E))
            return m_new, alpha * acc + _dot(p.astype(BF16), v)

        def step(j, carry, masked):
            off = pl.multiple_of(j * tq, tq)
            out = []
            for a in range(hs):
                head = slice(a * HEAD_PAD, (a + 1) * HEAD_PAD)
                s = _dot(qv[a], k_ref[pl.ds(off, tq), head], NT)
                if masked:
                    row = lax.broadcasted_iota(jnp.int32, (tq, tq), 0)
                    col = lax.broadcasted_iota(jnp.int32, (tq, tq), 1)
                    s = jnp.where(col <= row, s, NEG_INF)
                out.append(update(*carry[a], s, v_ref[pl.ds(off, tq), head]))
            return tuple(out)

        one = (jnp.full((tq, 1), NEG_INF, F32), jnp.zeros((tq, HEAD_PAD), F32))
        carry = step(qi, lax.fori_loop(0, qi, lambda j, c: step(j, c, False), (one,) * hs), True)
        for a in range(hs):
            m, acc = carry[a]
            l = acc[:, V_HEAD:V_HEAD + 1]
            o_ref[:, a * HEAD_PAD:(a + 1) * HEAD_PAD] = (acc / l).astype(o_ref.dtype)
            lse_ref[a] = _stat_rows(m * ATT_SCALE + jnp.log(l))

    blk = pl.BlockSpec((tq, wide), lambda h, i: (i, h))
    full = pl.BlockSpec((S, wide), lambda h, i: (0, h))
    return pl.pallas_call(
        body, grid=(MLA_HEADS // hs, nq), in_specs=[blk, full, full, ANY],
        out_specs=[pl.BlockSpec((tq, wide), lambda h, i: (i, ATT_COL0 // hs + h)),
                   pl.BlockSpec((hs, 8, tq), lambda h, i: (h, i, 0))],
        out_shape=[jax.ShapeDtypeStruct(mix.shape, mix.dtype), jax.ShapeDtypeStruct((MLA_HEADS, nq * 8, tq), F32)],
        input_output_aliases={3: 0}, compiler_params=_cp(2), name=name,
    )(q, k, v, mix)


BWD_TILE = 1024
BWD_HEADS_PER_STEP = 1


def _bwd_tile(S):
    return BWD_TILE if S % BWD_TILE == 0 else min(S, 512)


def _attn_delta(dmix, mix, *, name):
    S = mix.shape[0]
    ts = _bwd_tile(S)
    half = MLA_HEADS // 2
    halves = [_rows(ts, half * HEAD_PAD, 1), _rows(ts, half * HEAD_PAD, 2)]

    def body(do0_ref, do1_ref, o0_ref, o1_ref, d_ref):
        for n, (do_ref, o_ref) in enumerate(((do0_ref, o0_ref), (do1_ref, o1_ref))):
            prod = do_ref[...].astype(F32) * o_ref[...].astype(F32)
            for a in range(half):
                d_ref[n * half + a] = _stat_rows(
                    jnp.sum(prod[:, a * HEAD_PAD:(a + 1) * HEAD_PAD], axis=-1, keepdims=True))

    return pl.pallas_call(
        body, grid=(S // ts,), in_specs=halves + halves,
        out_specs=pl.BlockSpec((MLA_HEADS, 8, ts), lambda i: (0, i, 0)),
        out_shape=jax.ShapeDtypeStruct((MLA_HEADS, (S // ts) * 8, ts), F32), compiler_params=_cp(1), name=name,
    )(dmix, dmix, mix, mix)


def _flash_bwd(q, k, v, dmix, lse_rows, delta_rows, *, name):
    S = q.shape[0]
    tq = _bwd_tile(S)
    nq = S // tq
    hs = BWD_HEADS_PER_STEP
    wide = hs * HEAD_PAD

    def body(q_hbm, do_hbm, lse_ref, dl_ref, k_ref, v_ref, dq_hbm, dk_ref, dv_ref, q_all, do_all, dq_all):
        g, j = pl.program_id(0), pl.program_id(1)
        cols = pl.multiple_of(g * wide, wide)

        @pl.when(j == 0)
        def _():
            pltpu.sync_copy(q_hbm.at[:, pl.ds(cols, wide)], q_all)
            pltpu.sync_copy(do_hbm.at[:, pl.ds(POOL_DIM + cols, wide)], do_all)
            dq_all[...] = jnp.zeros_like(dq_all)

        heads = [slice(a * HEAD_PAD, (a + 1) * HEAD_PAD) for a in range(hs)]
        kv = [k_ref[:, a] for a in heads]
        vv = [v_ref[:, a] for a in heads]

        def block(a, keys, rows, lse2, dl, first_query):
            qv, dov = q_all[rows, heads[a]], do_all[rows, heads[a]]
            st = _dot(kv[a][:keys], qv, NT)
            if first_query is not None:
                krow = lax.broadcasted_iota(jnp.int32, st.shape, 0)
                qcol = lax.broadcasted_iota(jnp.int32, st.shape, 1) + first_query
                st = jnp.where(krow <= qcol, st, NEG_INF)
            pt = jnp.exp2(st * (ATT_SCALE * LOG2E) - lse2)
            dst = (pt * (_dot(vv[a][:keys], dov, NT) - dl)).astype(BF16)
            dq_all[rows, heads[a]] += _dot(dst, kv[a][:keys], TN)
            return _dot(dst, qv), _dot(pt.astype(BF16), dov)

        def stats(a, i):
            off8 = pl.multiple_of(i * 8, 8)
            return lse_ref[a, pl.ds(off8, 8), :][0:1] * LOG2E, dl_ref[a, pl.ds(off8, 8), :][0:1]

        def step(i, carry):
            rows = pl.ds(pl.multiple_of(i * tq, tq), tq)
            out = []
            for a in range(hs):
                dk, dv = block(a, tq, rows, *stats(a, i), None)
                out.append((carry[a][0] + dk, carry[a][1] + dv))
            return tuple(out)

        def diagonal():
            half = tq // 2
            out = []
            for a in range(hs):
                lse2, dl = stats(a, j)
                off = pl.multiple_of(j * tq, tq)
                dk0, dv0 = block(a, half, pl.ds(off, half), lse2[:, :half], dl[:, :half], 0)
                dk1, dv1 = block(a, tq, pl.ds(pl.multiple_of(off + half, half), half), lse2[:, half:], dl[:, half:], half)
                zero = jnp.zeros((tq - half, HEAD_PAD), F32)
                out.append((dk1 + jnp.concatenate([dk0, zero], axis=0), dv1 + jnp.concatenate([dv0, zero], axis=0)))
            return tuple(out)

        carry = lax.fori_loop(j + 1, nq, step, diagonal())
        for a in range(hs):
            dk_ref[:, heads[a]] = carry[a][0] * ATT_SCALE
            dv_ref[:, heads[a]] = carry[a][1]

        @pl.when(j == nq - 1)
        def _():
            dq_all[...] = dq_all[...] * ATT_SCALE
            pltpu.sync_copy(dq_all, dq_hbm.at[:, pl.ds(cols, wide)])

    blk = pl.BlockSpec((tq, wide), lambda g, j: (j, g))
    stat = pl.BlockSpec((hs, nq * 8, tq), lambda g, j: (g, 0, 0))
    full = jax.ShapeDtypeStruct((S, MLA_HEADS * HEAD_PAD), F32)
    return pl.pallas_call(
        body, grid=(MLA_HEADS // hs, nq), in_specs=[ANY, ANY, stat, stat, blk, blk], out_specs=[ANY, blk, blk],
        out_shape=[full, full, full],
        scratch_shapes=[pltpu.VMEM((S, wide), BF16), pltpu.VMEM((S, wide), BF16), pltpu.VMEM((S, wide), F32)],
        compiler_params=_cp(2), name=name,
    )(q, dmix, lse_rows, delta_rows, k, v)


MEM_SCALE = MEM_HEAD_DIM ** -0.5


def _xattn_probs(qh, kh):
    s = _dot(qh, kh, NT) * MEM_SCALE
    e = jnp.exp(s - jnp.max(s, axis=-1, keepdims=True))
    return e / jnp.sum(e, axis=-1, keepdims=True)


def _xa_block_fwd(x, kvm, w_q, w_o, g, *, name):
    S = x.shape[0]
    ts = _row_tile(S)
    nm = kvm.shape[0]

    def body(x_ref, kv_ref, wq_ref, wo_ref, g_ref, xo_ref, hx_ref, q_ref, o_ref):
        xv = x_ref[...]
        r = lax.rsqrt(jnp.mean(xv * xv, axis=-1, keepdims=True) + RMS_EPS)
        hx = (xv * r * g_ref[...]).astype(BF16)
        hx_ref[...] = hx
        q = _dot(hx, wq_ref[...]).astype(BF16)
        q_ref[...] = q
        for h in range(MEM_HEADS):
            lo, hi = h * MEM_HEAD_DIM, (h + 1) * MEM_HEAD_DIM
            p = _xattn_probs(q[:, lo:hi], kv_ref[:, lo:hi])
            o_ref[:, lo:hi] = _dot(p.astype(BF16), kv_ref[:, D_MODEL + lo:D_MODEL + hi]).astype(o_ref.dtype)
        xo_ref[...] = xv + _dot(o_ref[...], wo_ref[...])

    square = _const((D_MODEL, D_MODEL))
    act = jax.ShapeDtypeStruct((S, D_MODEL), BF16)
    return pl.pallas_call(
        body, grid=(S // ts,),
        in_specs=[_rows(ts, D_MODEL), _const((nm, 2 * D_MODEL)), square, square, _const((1, D_MODEL))],
        out_specs=[_rows(ts, D_MODEL)] * 4, out_shape=[jax.ShapeDtypeStruct((S, D_MODEL), F32), act, act, act],
        compiler_params=_cp(1), name=name,
    )(x, kvm, w_q, w_o, g.reshape(1, D_MODEL))


def _xa_block_bwd(dxo, x, q, kvm, w_q, w_o, g, *, name):
    S = q.shape[0]
    ts = min(S, 512)
    nm = kvm.shape[0]

    def body(dxo_ref, x_ref, q_ref, kv_ref, wq_ref, wo_ref, g_ref, dx_ref, dq_ref, dkv_ref, dg_ref):
        @pl.when(pl.program_id(0) == 0)
        def _():
            dkv_ref[...] = jnp.zeros_like(dkv_ref)
            dg_ref[...] = jnp.zeros_like(dg_ref)

        dxo = dxo_ref[...]
        do = _dot(dxo.astype(BF16), wo_ref[...], NT).astype(BF16)
        for h in range(MEM_HEADS):
            lo, hi = h * MEM_HEAD_DIM, (h + 1) * MEM_HEAD_DIM
            qh, kh, vh = q_ref[:, lo:hi], kv_ref[:, lo:hi], kv_ref[:, D_MODEL + lo:D_MODEL + hi]
            doh = do[:, lo:hi]
            p = _xattn_probs(qh, kh)
            dp = _dot(doh, vh, NT)
            ds = (p * (dp - jnp.sum(dp * p, axis=-1, keepdims=True)) * MEM_SCALE).astype(BF16)
            dq_ref[:, lo:hi] = _dot(ds, kh).astype(dq_ref.dtype)
            dkv_ref[:, lo:hi] += _dot(ds, qh, TN)
            dkv_ref[:, D_MODEL + lo:D_MODEL + hi] += _dot(p.astype(BF16), doh, TN)
        dx, dg = _norm_bwd_epilogue(0)([_dot(dq_ref[...], wq_ref[...], NT)], [x_ref[...], dxo, g_ref[...]])
        dx_ref[...] = dx
        dg_ref[...] += dg

    square = _const((D_MODEL, D_MODEL))
    return pl.pallas_call(
        body, grid=(S // ts,),
        in_specs=[_rows(ts, D_MODEL), _rows(ts, D_MODEL), _rows(ts, D_MODEL), _const((nm, 2 * D_MODEL)), square,
                  square, _const((1, D_MODEL))],
        out_specs=[_rows(ts, D_MODEL), _rows(ts, D_MODEL), _const((nm, 2 * D_MODEL)), _const((1, D_MODEL))],
        out_shape=[jax.ShapeDtypeStruct((S, D_MODEL), F32), jax.ShapeDtypeStruct((S, D_MODEL), BF16),
                   jax.ShapeDtypeStruct((nm, 2 * D_MODEL), F32), jax.ShapeDtypeStruct((1, D_MODEL), F32)],
        compiler_params=_cp(1), name=name,
    )(dxo, x, q, kvm, w_q, w_o, g.reshape(1, D_MODEL))


CONV_HALO = 8


def _sigmoid(x):
    return 0.5 * jnp.tanh(0.5 * x) + 0.5


def _softplus(x):
    return jnp.maximum(x, 0.0) + jnp.log(1.0 + jnp.exp(-jnp.abs(x)))


def _neg_expm1(x):
    series = -x * (1.0 + x * (1.0 / 2) * (1.0 + x * (1.0 / 3) * (1.0 + x * (1.0 / 4) * (1.0 + x * (1.0 / 5)))))
    return jnp.where(x > -0.05, series, 1.0 - jnp.exp(x))


GELU_C = math.sqrt(2.0 / math.pi)


def _gelu(x):
    return 0.5 * x * (1.0 + jnp.tanh(GELU_C * (x + 0.044715 * x * x * x)))


def _gelu_grad(x):
    t = jnp.tanh(GELU_C * (x + 0.044715 * x * x * x))
    return 0.5 * (1.0 + t) + 0.5 * x * (1.0 - t * t) * GELU_C * (1.0 + 3 * 0.044715 * x * x)


def _lru_gates(xc, wr_ref, br, wi_ref, bi, sp, reset):
    xcb = xc.astype(BF16)
    pr, pi = [], []
    for h in range(LRU_HEADS):
        lo, hi = h * LRU_HEAD_DIM, (h + 1) * LRU_HEAD_DIM
        pr.append(_dot(xcb[:, lo:hi], wr_ref[h]))
        pi.append(_dot(xcb[:, lo:hi], wi_ref[h]))
    r = _sigmoid(jnp.concatenate(pr, axis=1) + br)
    ig = _sigmoid(jnp.concatenate(pi, axis=1) + bi)
    log_a = -LRU_C * r * sp
    a = jnp.where(reset, 0.0, jnp.exp(log_a))
    mult = jnp.where(reset, 1.0, jnp.sqrt(jnp.maximum(_neg_expm1(2.0 * log_a), 0.0)))
    return r, ig, a, mult


SUBLANES = 8


def _compose_groups(a, b, reverse):
    n = a.shape[0]
    row = lax.broadcasted_iota(jnp.int32, a.shape, 0) % SUBLANES
    for s in (1, 2, 4):
        inside = (row < SUBLANES - s) if reverse else (row >= s)
        shift = n - s if reverse else s
        a_s = jnp.where(inside, pltpu.roll(a, shift, 0), 1.0)
        b_s = jnp.where(inside, pltpu.roll(b, shift, 0), 0.0)
        b = a * b_s + b
        a = a * a_s
    return a, b


def _chain_groups(a_buf, h_ref, state, reverse):
    groups = a_buf.shape[0] // SUBLANES

    def group(g, h_in):
        off = pl.multiple_of((groups - 1 - g if reverse else g) * SUBLANES, SUBLANES)
        h = a_buf[pl.ds(off, SUBLANES), :] * h_in + h_ref[pl.ds(off, SUBLANES), :]
        h_ref[pl.ds(off, SUBLANES), :] = h
        return jnp.broadcast_to(h[0:1] if reverse else h[SUBLANES - 1:SUBLANES], h.shape)

    return lax.fori_loop(0, groups, group, state, unroll=4)[0:1]


def _lru_fwd(x, g, w_in, reset, conv_w, conv_b, w_r, b_r, w_i, b_i, lam, *, name):
    S = x.shape[0]
    ts = min(S, 512)
    W = D_MODEL

    def body(x_ref, g_ref, win_ref, rs_ref, cw_ref, cb_ref, wr_ref, br_ref, wi_ref, bi_ref, lam_ref,
             hn_ref, z_ref, xc_ref, h_ref, y_ref, a_buf, carry, tail):
        i = pl.program_id(0)

        @pl.when(i == 0)
        def _():
            carry[...] = jnp.zeros_like(carry)
            tail[...] = jnp.zeros_like(tail)

        xv = x_ref[...]
        hn = (xv * lax.rsqrt(jnp.mean(xv * xv, axis=-1, keepdims=True) + RMS_EPS) * g_ref[...]).astype(BF16)
        hn_ref[...] = hn
        z_ref[...] = _dot(hn, win_ref[...])
        xb = z_ref[:, W:]
        xe = jnp.concatenate([tail[...], xb], axis=0)
        tail[...] = xb[ts - CONV_HALO:]
        xc = cb_ref[...] + cw_ref[3:4, :] * xe[CONV_HALO:]
        for kk in range(CONV_WIDTH - 1):
            xc = xc + cw_ref[kk:kk + 1, :] * pltpu.roll(xe, CONV_WIDTH - 1 - kk, 0)[CONV_HALO:]
        xc_ref[...] = xc
        reset = rs_ref[...] > 0.5
        _, ig, a, mult = _lru_gates(xc, wr_ref, br_ref[...], wi_ref, bi_ref[...], _softplus(-lam_ref[...]), reset)
        a_buf[...], h_ref[...] = _compose_groups(a, mult * (ig * xc), False)
        carry[...] = _chain_groups(a_buf, h_ref, jnp.broadcast_to(carry[...], (SUBLANES, W)), False)
        y_ref[...] = (_gelu(z_ref[:, :W]) * h_ref[...]).astype(y_ref.dtype)

    vec = _const((1, W))
    gw = _const((LRU_HEADS, LRU_HEAD_DIM, LRU_HEAD_DIM))
    return pl.pallas_call(
        body, grid=(S // ts,),
        in_specs=[_rows(ts, W), vec, _const((W, 2 * W)), _rows(ts, 1), _const((CONV_WIDTH, W)), vec, gw, vec, gw, vec,
                  vec],
        out_specs=[_rows(ts, W), _rows(ts, 2 * W), _rows(ts, W), _rows(ts, W), _rows(ts, W)],
        out_shape=[jax.ShapeDtypeStruct((S, W), BF16), jax.ShapeDtypeStruct((S, 2 * W), F32),
                   jax.ShapeDtypeStruct((S, W), F32), jax.ShapeDtypeStruct((S, W), F32),
                   jax.ShapeDtypeStruct((S, W), BF16)],
        scratch_shapes=[pltpu.VMEM((ts, W), F32), pltpu.VMEM((1, W), F32), pltpu.VMEM((CONV_HALO, W), F32)],
        compiler_params=_cp(1), name=name,
    )(x, g.reshape(1, W), w_in, reset, conv_w, conv_b, w_r, b_r, w_i, b_i, lam)


def _lru_bwd(dxo, w_out, z, xc, hseq, reset, w_r, b_r, w_i, b_i, lam, *, name):
    S = z.shape[0]
    ts = min(S, 512)
    nt = S // ts
    nh = ts // CONV_HALO
    W = D_MODEL

    def body(dxo_ref, wout_ref, gate_ref, xc_ref, h_ref, hh_ref, rs_ref, wr_ref, br_ref, wi_ref, bi_ref, lam_ref,
             dg_ref, dxc_ref, dpr_ref, dpi_ref, acc_ref, a_buf, dh_buf, carry):
        i = pl.program_id(0)
        tile = nt - 1 - i

        @pl.when(i == 0)
        def _():
            carry[...] = jnp.zeros_like(carry)
            acc_ref[...] = jnp.zeros_like(acc_ref)

        xc = xc_ref[...]
        lam_v = lam_ref[...]
        sp = _softplus(-lam_v)
        reset = rs_ref[...] > 0.5
        r, ig, a, mult = _lru_gates(xc, wr_ref, br_ref[...], wi_ref, bi_ref[...], sp, reset)
        gate = gate_ref[...]
        dyv = _dot(dxo_ref[...].astype(BF16), wout_ref[...], NT)
        h = h_ref[...]
        dg_ref[...] = (dyv * h * _gelu_grad(gate)).astype(dg_ref.dtype)
        last_row = lax.broadcasted_iota(jnp.int32, a.shape, 0) == ts - 1
        a_buf[...], dh_buf[...] = _compose_groups(jnp.where(last_row, 1.0, pltpu.roll(a, ts - 1, 0)),
                                                  dyv * _gelu(gate), True)
        _chain_groups(a_buf, dh_buf, jnp.broadcast_to(carry[...], (SUBLANES, W)), True)
        dh = dh_buf[...]
        carry[...] = a[0:1] * dh[0:1]
        hh = jnp.where(tile > 0, hh_ref[...], 0.0)
        h_prev = pltpu.roll(jnp.concatenate([hh, h], axis=0), 1, 0)[CONV_HALO:]
        da = dh * h_prev
        bx = ig * xc
        dmult = dh * bx
        dbx = dh * mult
        di = dbx * xc
        dlog_a = jnp.where(reset, 0.0, da * a - dmult * a * a / jnp.maximum(mult, 1e-30))
        dr = dlog_a * (-LRU_C) * sp
        dpre_r = dr * r * (1.0 - r)
        dpre_i = di * ig * (1.0 - ig)
        dprb, dpib = dpre_r.astype(BF16), dpre_i.astype(BF16)
        dpr_ref[...] = dprb
        dpi_ref[...] = dpib
        back = []
        for hd in range(LRU_HEADS):
            lo, hi = hd * LRU_HEAD_DIM, (hd + 1) * LRU_HEAD_DIM
            back.append(_dot(dprb[:, lo:hi], wr_ref[hd], NT) + _dot(dpib[:, lo:hi], wi_ref[hd], NT))
        dxc_ref[...] = dbx * ig + jnp.concatenate(back, axis=1)
        dlam = jnp.sum(dlog_a * (-LRU_C) * r, axis=0, keepdims=True) * (-_sigmoid(-lam_v))
        acc_ref[0:1, :] += jnp.sum(dpre_r, axis=0, keepdims=True)
        acc_ref[1:2, :] += jnp.sum(dpre_i, axis=0, keepdims=True)
        acc_ref[2:3, :] += dlam

    rev = lambda cb: pl.BlockSpec((ts, W), lambda i: (nt - 1 - i, cb))
    vec = _const((1, W))
    gw = _const((LRU_HEADS, LRU_HEAD_DIM, LRU_HEAD_DIM))
    return pl.pallas_call(
        body, grid=(nt,),
        in_specs=[rev(0), _const((W, W)), rev(0), rev(0), rev(0),
                  pl.BlockSpec((CONV_HALO, W), lambda i: (jnp.maximum((nt - 1 - i) * nh - 1, 0), 0)),
                  pl.BlockSpec((ts, 1), lambda i: (nt - 1 - i, 0)), gw, vec, gw, vec, vec],
        out_specs=[rev(0), rev(0), rev(0), rev(0), _const((8, W))],
        out_shape=[jax.ShapeDtypeStruct((S, W), BF16), jax.ShapeDtypeStruct((S, W), F32),
                   jax.ShapeDtypeStruct((S, W), BF16), jax.ShapeDtypeStruct((S, W), BF16),
                   jax.ShapeDtypeStruct((8, W), F32)],
        scratch_shapes=[pltpu.VMEM((ts, W), F32), pltpu.VMEM((ts, W), F32), pltpu.VMEM((1, W), F32)],
        compiler_params=_cp(1), name=name,
    )(dxo, w_out, z, xc, hseq, hseq, reset, w_r, b_r, w_i, b_i, lam)


def _conv_bwd(dxc, z, conv_w, *, name):
    S = dxc.shape[0]
    ts = min(S, 512)
    nh = ts // CONV_HALO
    last = S // CONV_HALO - 1
    W = D_MODEL
    n = ts + CONV_HALO

    def body(d_ref, dn_ref, xb_ref, xp_ref, cw_ref, dxb_ref, acc_ref):
        i = pl.program_id(0)

        @pl.when(i == 0)
        def _():
            acc_ref[...] = jnp.zeros_like(acc_ref)

        d = d_ref[...]
        de = jnp.concatenate([d, jnp.where(i < pl.num_programs(0) - 1, dn_ref[...], 0.0)], axis=0)
        xe = jnp.concatenate([jnp.where(i > 0, xp_ref[...], 0.0), xb_ref[...]], axis=0)
        dxb = cw_ref[3:4, :] * d
        acc_ref[3:4, :] += jnp.sum(d * xe[CONV_HALO:], axis=0, keepdims=True)
        for kk in range(CONV_WIDTH - 1):
            sh = CONV_WIDTH - 1 - kk
            dxb = dxb + cw_ref[kk:kk + 1, :] * pltpu.roll(de, n - sh, 0)[:ts]
            acc_ref[kk:kk + 1, :] += jnp.sum(d * pltpu.roll(xe, sh, 0)[CONV_HALO:], axis=0, keepdims=True)
        dxb_ref[...] = dxb.astype(dxb_ref.dtype)
        acc_ref[4:5, :] += jnp.sum(d, axis=0, keepdims=True)

    return pl.pallas_call(
        body, grid=(S // ts,),
        in_specs=[_rows(ts, W), pl.BlockSpec((CONV_HALO, W), lambda i: (jnp.minimum((i + 1) * nh, last), 0)),
                  _rows(ts, W, 1), pl.BlockSpec((CONV_HALO, W), lambda i: (jnp.maximum(i * nh - 1, 0), 1)),
                  _const((CONV_WIDTH, W))],
        out_specs=[_rows(ts, W), _const((8, W))],
        out_shape=[jax.ShapeDtypeStruct((S, W), BF16), jax.ShapeDtypeStruct((8, W), F32)],
        compiler_params=_cp(1), name=name,
    )(dxc, dxc, z, z, conv_w)


def _loss_head(x, g, target, *, name):
    S, D = x.shape
    ts = _row_tile(S)

    def body(x_ref, g_ref, t_ref, dx_ref, dg_ref, l_ref):
        @pl.when(pl.program_id(0) == 0)
        def _():
            dg_ref[...] = jnp.zeros_like(dg_ref)
            l_ref[...] = jnp.zeros_like(l_ref)

        xv = x_ref[...]
        r = lax.rsqrt(jnp.mean(xv * xv, axis=-1, keepdims=True) + RMS_EPS)
        n = xv * r
        err = n * g_ref[...] - t_ref[...]
        l_ref[...] += 0.5 * jnp.sum(jnp.sum(err * err, axis=-1, keepdims=True) * (1.0 / D), axis=0, keepdims=True)
        dy = err * (1.0 / D)
        dn = dy * g_ref[...]
        dx_ref[...] = r * (dn - n * jnp.mean(dn * n, axis=-1, keepdims=True))
        dg_ref[...] += jnp.sum(dy * n, axis=0, keepdims=True)

    return pl.pallas_call(
        body, grid=(S // ts,), in_specs=[_rows(ts, D), _const((1, D)), _rows(ts, D)],
        out_specs=[_rows(ts, D), _const((1, D)), _const((8, LANES))],
        out_shape=[jax.ShapeDtypeStruct((S, D), F32), jax.ShapeDtypeStruct((1, D), F32),
                   jax.ShapeDtypeStruct((8, LANES), F32)],
        compiler_params=_cp(1), name=name,
    )(x, g.reshape(1, D), target)


def _adamw(w, ga, gb, m, v, *, name):
    shape = w.shape
    cols = shape[-1]
    rows = w.size // cols
    br = rows
    if rows * cols * 4 > (1 << 20):
        br = max(d for d in range(8, rows + 1, 8) if rows % d == 0 and d * cols * 4 <= (1 << 20))

    def body(w_ref, ga_ref, gb_ref, m_ref, v_ref, g_ref, d_ref, mo_ref, vo_ref):
        gv = ga_ref[...] + gb_ref[...]
        g_ref[...] = gv
        mn = ADAM_B1 * m_ref[...] + (1.0 - ADAM_B1) * gv
        vn = ADAM_B2 * v_ref[...] + (1.0 - ADAM_B2) * (gv * gv)
        m_hat = mn / (1.0 - ADAM_B1 ** ADAM_STEP)
        v_hat = vn / (1.0 - ADAM_B2 ** ADAM_STEP)
        d_ref[...] = -ADAM_LR * (m_hat / (jnp.sqrt(v_hat) + ADAM_EPS) + ADAM_WD * w_ref[...])
        mo_ref[...] = mn
        vo_ref[...] = vn

    spec = _rows(br, cols)
    outs = pl.pallas_call(
        body, grid=(rows // br,), in_specs=[spec] * 5, out_specs=[spec] * 4,
        out_shape=[jax.ShapeDtypeStruct((rows, cols), F32)] * 4, compiler_params=_cp(1), name=name,
    )(*[t.reshape(rows, cols) for t in (w, ga, gb, m, v)])
    return [o.reshape(shape) for o in outs]


def _pad_heads(w, width):
    k = w.shape[0]
    return jnp.pad(w.reshape(k, MLA_HEADS, width), ((0, 0), (0, 0), (0, HEAD_PAD - width))).reshape(k, -1)


def _unpad_heads(w, width):
    k = w.shape[0]
    return w.reshape(k, MLA_HEADS, HEAD_PAD)[:, :, :width].reshape(k, MLA_HEADS * width)


def _rope_tables(positions, token=None):
    inv_freq = ROPE_BASE ** (-jnp.arange(0, QK_ROPE, 2, dtype=F32) / QK_ROPE)
    none = jnp.zeros((QK_NOPE,), F32)
    freq = jnp.concatenate([none, inv_freq, inv_freq, none[:HEAD_PAD - QK_DIM]])
    sign = jnp.concatenate([none, -jnp.ones_like(inv_freq), jnp.ones_like(inv_freq), none[:HEAD_PAD - QK_DIM]])
    pos = positions.astype(F32) if token is None else positions.astype(F32) + token[0, 0]
    ang = pos[:, None] * freq
    return jnp.cos(ang), jnp.sin(ang) * sign


def _memory_block(x, mem, W, layer, tag):
    mn = _rms(mem, W["xa_norm_mem"][layer], name=f"{tag}_xa_norm_mem")
    kvm = _mm(mn, [(W["xa_w_kv"][layer], 0, 0)], _first, [(2 * D_MODEL, BF16, 0)], tn=2 * D_MODEL, nj=1,
              name=f"{tag}_xa_kv")[0]
    xo, hx, qx, o = _xa_block_fwd(x, kvm, W["xa_w_q"][layer], W["xa_w_o"][layer], W["xa_norm_x"][layer],
                                  name=f"{tag}_xa_fwd")
    return xo, (x, hx, qx, mn, kvm, o)


def _memory_block_bwd(dxo, mem, W, layer, saved, tag, grads):
    x, hx, qx, mn, kvm, o = saved
    wq, wkv, wo = W["xa_w_q"][layer], W["xa_w_kv"][layer], W["xa_w_o"][layer]
    grads["xa_w_o"][layer] = _owner_major(_mm_tn(o, dxo, name=f"{tag}_xa_dwo"), 0)
    dx, dqx, dkvm, dg = _xa_block_bwd(dxo, x, qx, kvm, wq, wo, W["xa_norm_x"][layer], name=f"{tag}_xa_bwd")
    grads["xa_w_q"][layer] = _owner_major(_mm_tn(hx, dqx, name=f"{tag}_xa_dwq"), 0)
    grads["xa_norm_x"][layer] = dg[0]
    dmn = _mm(dkvm, [(wkv, 0, 0)], _first, [(D_MODEL, F32, 0)], nt=True, tn=D_MODEL, nj=1, name=f"{tag}_xa_dmn")[0]
    grads["xa_w_kv"][layer] = _mm_tn_owners(mn, [dkvm], name=f"{tag}_xa_dwkv")
    grads["xa_norm_mem"][layer] = _norm_gain_grad(mem, dmn, name=f"{tag}_xa_norm_mem_bwd")[0]
    return dx


FF_TN = D_FF // 2

def _silu_mul(accs, extras):
    g, u = accs
    return [g * _sigmoid(g) * u, g, u]


def _silu_mul_bwd(accs, extras):
    da = accs[0]
    g, u = extras[0].astype(F32), extras[1].astype(F32)
    sg = _sigmoid(g)
    silu = g * sg
    return [da * u * (sg + silu * (1.0 - sg)), da * silu]


def _ffn_block(x, W, layer, tag):
    hf = _rms(x, W["ffn_norm"][layer], name=f"{tag}_ffn_norm")
    wgu, wd = W["ffn_w_gate_up"][layer], W["ffn_w_down"][layer]
    act, g, u = _mm(hf, [(wgu, 0, 0), (wgu, 0, 2)], _silu_mul, [(D_FF, BF16, 0)] * 3, tn=FF_TN, nj=2,
                    name=f"{tag}_ffn_up")
    xo = _mm(act, [(wd, 0, 0)], _add_res, [(D_MODEL, F32, 0)], extras=[(x, 0)], tn=D_MODEL, nj=1,
             name=f"{tag}_ffn_down")[0]
    return xo, (x, hf, act, g, u)


def _ffn_block_bwd(dxo, W, layer, saved, tag, grads):
    x, hf, act, g, u = saved
    wgu, wd = W["ffn_w_gate_up"][layer], W["ffn_w_down"][layer]
    dg, du = _mm(dxo, [(wd, 0, 0)], _silu_mul_bwd, [(D_FF, BF16, 0)] * 2, nt=True, extras=[(g, 0), (u, 0)], tn=FF_TN,
                 nj=2, name=f"{tag}_ffn_dact")
    grads["ffn_w_down"][layer] = _owner_major(_mm_tn(act, dxo, tk=FF_TN, name=f"{tag}_ffn_dwd"), 0)
    dx, dgn = _mm(dg, [(wgu, 0, 0)], _norm_bwd_epilogue(0), [(D_MODEL, F32, 0)], nt=True, also=(du, (wgu, 0, 1)),
                  extras=[(x, 0), (dxo, 0)], rows=[W["ffn_norm"][layer].reshape(1, D_MODEL)],
                  sums=[D_MODEL], tn=D_MODEL, nj=1, name=f"{tag}_ffn_dhf")
    grads["ffn_w_gate_up"][layer] = _mm_tn_owners(hf, [dg, du], name=f"{tag}_ffn_dwgu")
    grads["ffn_norm"][layer] = dgn[0]
    return dx


def _even_block(x, tabs, W, tag):
    ctab, stab = tabs
    w_in = W["ev_w_in"][0]
    zero = jnp.zeros((D_MODEL, QK_NOPE), BF16)
    w_in_pad = jnp.concatenate([w_in[:, :896], zero, w_in[:, 896:], zero[:, :HEAD_PAD - QK_DIM]], axis=1)
    w_q_pad = _pad_heads(W["ev_w_q_up"][0], QK_DIM)
    wkv = W["ev_w_kv_up"][0].reshape(KV_RANK, MLA_HEADS, QK_NOPE + V_HEAD)
    w_kv_pad = jnp.concatenate([_pad_heads(wkv[:, :, :QK_NOPE].reshape(KV_RANK, -1), QK_NOPE),
                                _pad_heads(wkv[:, :, QK_NOPE:].reshape(KV_RANK, -1), V_HEAD)], axis=1)
    w_out = W["ev_w_out"][0]
    w_att = jnp.pad(w_out[POOL_DIM:].reshape(MLA_HEADS, V_HEAD, D_MODEL), ((0, 0), (0, HEAD_PAD - V_HEAD), (0, 0)))
    w_out_pad = jnp.concatenate([w_out[:POOL_DIM], w_att.reshape(MLA_HEADS * HEAD_PAD, D_MODEL)], axis=0)
    pool_w = W["ev_pool_w"][0].astype(BF16)
    pool_scale = W["ev_pool_scale"]

    h, z, mix, pooled, cqn, ckvn, q_rot, k_cat, v_pad = _even_front(
        x, W["ev_norm"][0], w_in_pad, pool_w, pool_scale, W["ev_q_norm"][0], w_q_pad, W["ev_kv_norm"][0], w_kv_pad,
        ctab, stab, name=f"{tag}_front")
    mix, lse = _flash_fwd(q_rot, k_cat, v_pad, mix, name=f"{tag}_attn")
    xo = _mm(mix, [(w_out_pad, 0, 0)], _add_res, [(D_MODEL, F32, 0)], extras=[(x, 0)], tn=D_MODEL, nj=1,
             name=f"{tag}_out")[0]
    saved = (x, h, z, pooled, cqn, ckvn, q_rot, k_cat, v_pad, lse, mix,
             (w_in_pad, w_q_pad, w_kv_pad, w_out_pad, pool_w, pool_scale))
    return xo, saved


def _even_out_grad(dxo, saved, tag):
    mix = saved[10]
    dw_out_pad = _mm_tn(mix, dxo, tk=MIX_DIM // 3, name=f"{tag}_dw_out")
    datt = dw_out_pad[POOL_DIM:].reshape(MLA_HEADS, HEAD_PAD, D_MODEL)[:, :V_HEAD].reshape(-1, D_MODEL)
    return [_owner_major(jnp.concatenate([dw_out_pad[:POOL_DIM], datt], axis=0), 0)]


def _even_block_bwd(dxo, tabs, W, saved, tag, grads, token=None):
    ctab, stab = tabs
    x, h, z, pooled, cqn, ckvn, q_rot, k_cat, v_pad, lse, mix, wts = saved
    w_in_pad, w_q_pad, w_kv_pad, w_out_pad, pool_w, pool_scale = wts
    if token is not None:
        w_out_pad = w_out_pad + token[0:1, 0:1].astype(BF16)
    dmix = _mm(dxo, [(w_out_pad, 0, 0)], _first, [(MIX_DIM, BF16, 0)], nt=True, tn=MIX_DIM, nj=1,
               name=f"{tag}_dmix")[0]
    delta = _attn_delta(dmix, mix, name=f"{tag}_delta")
    dq_rot, dk_cat, dv_pad = _flash_bwd(q_rot, k_cat, v_pad, dmix, _retile_rows(lse, delta.shape[2]), delta,
                                        name=f"{tag}_attn_bwd")
    dx, dq_pad, dz, dypre, dgq, dgkv, dscale, dgn = _even_back(
        dq_rot, dk_cat, dv_pad, dmix, pooled, z, x, dxo, ctab, stab, w_q_pad, w_kv_pad, w_in_pad, pool_w, pool_scale,
        W["ev_q_norm"][0], W["ev_kv_norm"][0], W["ev_norm"][0], name=f"{tag}_back")
    grads["ev_q_norm"], grads["ev_kv_norm"], grads["ev_pool_scale"], grads["ev_norm"] = dgq, dgkv, dscale, dgn
    dw_q_pad = _mm_tn(cqn, dq_pad, name=f"{tag}_dw_q_up")
    grads["ev_w_q_up"] = [_owner_major(_unpad_heads(dw_q_pad, QK_DIM), 1)]
    dwk = _unpad_heads(_mm_tn(ckvn, dk_cat, name=f"{tag}_dw_k_up"), QK_NOPE).reshape(KV_RANK, MLA_HEADS, QK_NOPE)
    dwv = _unpad_heads(_mm_tn(ckvn, dv_pad, name=f"{tag}_dw_v_up"), V_HEAD).reshape(KV_RANK, MLA_HEADS, V_HEAD)
    grads["ev_w_kv_up"] = [_owner_major(jnp.concatenate([dwk, dwv], axis=2).reshape(KV_RANK, -1), 1)]
    grads["ev_pool_w"] = _mm_tn_grouped(pooled, dypre, 4, POOL_GROUP, name=f"{tag}_dpool_w")[None]
    dw_in_pad = _mm_tn(h, dz, name=f"{tag}_dw_in")
    grads["ev_w_in"] = [_owner_major(jnp.concatenate([dw_in_pad[:, :896], dw_in_pad[:, 960:992]], axis=1), 0)]
    return dx


def _odd_block(x, reset, W, tag):
    w_r, w_i = W["od_w_rgate"][0], W["od_w_igate"][0]
    vecs = [W[n].reshape(1, D_MODEL) for n in ("od_conv_b", "od_b_rgate", "od_b_igate", "od_lambda")]
    h, z, xc, hseq, y = _lru_fwd(x, W["od_norm"][0], W["od_w_in"][0], reset, W["od_conv_w"][0], vecs[0], w_r,
                                 vecs[1], w_i, vecs[2], vecs[3], name=f"{tag}_lru")
    xo = _mm(y, [(W["od_w_out"][0], 0, 0)], _add_res, [(D_MODEL, F32, 0)], extras=[(x, 0)], tn=D_MODEL, nj=1,
             name=f"{tag}_out")[0]
    return xo, (x, h, z, xc, hseq, y, vecs)


def _odd_block_bwd(dxo, reset, W, saved, tag, grads):
    x, h, z, xc, hseq, y, vecs = saved
    w_r, w_i = W["od_w_rgate"][0], W["od_w_igate"][0]
    grads["od_w_out"] = [_owner_major(_mm_tn(y, dxo, name=f"{tag}_dw_out"), 0)]
    dgate, dxc, dpr, dpi, acc = _lru_bwd(dxo, W["od_w_out"][0], z, xc, hseq, reset, w_r, vecs[1], w_i, vecs[2],
                                         vecs[3], name=f"{tag}_lru_bwd")
    grads["od_b_rgate"], grads["od_b_igate"], grads["od_lambda"] = acc[0:1], acc[1:2], acc[2:3]
    grads["od_w_rgate"] = [_owner_major(_mm_tn_grouped(xc, dpr, LRU_HEADS, LRU_HEAD_DIM, name=f"{tag}_dw_rgate"), 1)]
    grads["od_w_igate"] = [_owner_major(_mm_tn_grouped(xc, dpi, LRU_HEADS, LRU_HEAD_DIM, name=f"{tag}_dw_igate"), 1)]
    dxb, cacc = _conv_bwd(dxc, z, W["od_conv_w"][0], name=f"{tag}_conv_bwd")
    grads["od_conv_w"], grads["od_conv_b"] = cacc[None, 0:4], cacc[4:5]
    dz = jnp.concatenate([dgate, dxb], axis=1)
    grads["od_w_in"] = [_mm_tn_owners(h, [dz], name=f"{tag}_dw_in")]
    dx, dgn = _mm(dz, [(W["od_w_in"][0], 0, 0)], _norm_bwd_epilogue(0), [(D_MODEL, F32, 0)], nt=True,
                  extras=[(x, 0), (dxo, 0)], rows=[W["od_norm"][0].reshape(1, D_MODEL)], sums=[D_MODEL], tn=D_MODEL,
                  nj=1, name=f"{tag}_dh")
    grads["od_norm"] = dgn
    return dx


def _local_step(x, mem, positions, target, W, later_weights=None, exchange_earlier=None, tabs=None):
    tabs = _rope_tables(positions) if tabs is None else tabs
    reset = (positions == 0).astype(F32)[:, None]
    grads = {n: [None, None] for n in ("xa_norm_x", "xa_norm_mem", "xa_w_q", "xa_w_kv", "xa_w_o", "ffn_norm",
                                       "ffn_w_gate_up", "ffn_w_down")}
    x1, s_even = _even_block(x, tabs, W, "l0_even")
    if later_weights is not None:
        W = {**W, **later_weights(x1)}
    x2, s_xa0 = _memory_block(x1, mem, W, 0, "l0")
    x3, s_ff0 = _ffn_block(x2, W, 0, "l0")
    x4, s_odd = _odd_block(x3, reset, W, "l1_odd")
    x5, s_xa1 = _memory_block(x4, mem, W, 1, "l1")
    x6, s_ff1 = _ffn_block(x5, W, 1, "l1")
    d, dgf, loss = _loss_head(x6, W["final_norm"], target, name="loss_head")
    grads["final_norm"] = dgf[0]
    d = _ffn_block_bwd(d, W, 1, s_ff1, "l1", grads)
    d = _memory_block_bwd(d, mem, W, 1, s_xa1, "l1", grads)
    d = _odd_block_bwd(d, reset, W, s_odd, "l1_odd", grads)
    d = _ffn_block_bwd(d, W, 0, s_ff0, "l0", grads)
    d = _memory_block_bwd(d, mem, W, 0, s_xa0, "l0", grads)
    grads["ev_w_out"] = _even_out_grad(d, s_even, "l0_even")
    token = exchange_earlier(grads) if exchange_earlier is not None else None
    d = _even_block_bwd(d, tabs, W, s_even, "l0_even", grads, token)
    big = {n: grads.pop(n) for n in MATMUL_WEIGHTS}
    for n, v in grads.items():
        if isinstance(v, list):
            grads[n] = jnp.stack(v)
    return loss[0, 0], d, big, grads


WEIGHTS = ("ev_norm", "ev_w_in", "ev_pool_w", "ev_pool_scale", "ev_q_norm", "ev_w_q_up", "ev_kv_norm", "ev_w_kv_up",
           "ev_w_out", "od_norm", "od_w_in", "od_conv_w", "od_conv_b", "od_w_rgate", "od_b_rgate", "od_w_igate",
           "od_b_igate", "od_lambda", "od_w_out", "xa_norm_x", "xa_norm_mem", "xa_w_q", "xa_w_kv", "xa_w_o",
           "ffn_norm", "ffn_w_gate_up", "ffn_w_down", "final_norm")
SHARD_AXIS = {"ev_w_in": 1, "ev_w_q_up": 2, "ev_w_kv_up": 2, "ev_w_out": 1, "od_norm": 1, "od_w_in": 2,
              "od_conv_w": 2, "od_conv_b": 1, "od_w_rgate": 2, "od_b_rgate": 1, "od_w_igate": 2, "od_b_igate": 1,
              "od_lambda": 1, "od_w_out": 1, "xa_w_q": 1, "xa_w_kv": 2, "xa_w_o": 1, "ffn_w_gate_up": 2,
              "ffn_w_down": 1}
MATMUL_WEIGHTS = ("ev_w_in", "ev_w_q_up", "ev_w_kv_up", "ev_w_out", "od_w_in", "od_w_rgate", "od_w_igate",
                  "od_w_out", "xa_w_q", "xa_w_kv", "xa_w_o", "ffn_w_gate_up", "ffn_w_down")
SMALL_SHARDED = tuple(n for n in WEIGHTS if n in SHARD_AXIS and n not in MATMUL_WEIGHTS)
REPLICATED = tuple(n for n in WEIGHTS if n not in SHARD_AXIS)


def _pack(parts, quantum):
    flat = jnp.concatenate([p.reshape(-1) for p in parts])
    pad = (-flat.shape[0]) % quantum
    return jnp.pad(flat, (0, pad)).reshape(-1, LANES)


def _unpack(flat, shapes):
    out, off = [], 0
    for shape in shapes:
        size = math.prod(shape)
        out.append(flat[off:off + size].reshape(shape))
        off += size
    return out


def _run_copies(local, remote, send_sems, recv_sems, local_sems):
    locals_ = [pltpu.make_async_copy(src, dst, local_sems.at[n]) for n, (src, dst) in enumerate(local)]
    for cp in locals_:
        cp.start()
    sends = [pltpu.make_async_remote_copy(src_ref=src, dst_ref=dst, send_sem=send_sems.at[k, n],
                                          recv_sem=recv_sems.at[k, n], device_id=dev, device_id_type=MESH)
             for (k, n, src, dst, _, dev) in remote]
    for cp in sends:
        cp.start()
    for (k, n, src, _, arrival, dev) in remote:
        pltpu.make_async_remote_copy(src_ref=src, dst_ref=arrival, send_sem=send_sems.at[k, n],
                                     recv_sem=recv_sems.at[k, n], device_id=dev, device_id_type=MESH).wait_recv()
    for cp in sends:
        cp.wait_send()
    for cp in locals_:
        cp.wait()


def _chip_peers(x, y):
    return [(1 - x, y), (x, 1 - y), (1 - x, 1 - y)]


def _owner_block(ref, axis, q):
    size = ref.shape[axis] // N_CHIPS
    idx = [slice(None)] * len(ref.shape)
    idx[axis] = pl.ds(q * size, size)
    return ref.at[tuple(idx)]


def _comm_call(body, ins, out_shapes, n_items, n_peers, *, name):
    return pl.pallas_call(
        body, in_specs=[ANY] * len(ins), out_specs=[ANY] * len(out_shapes), out_shape=out_shapes,
        scratch_shapes=[pltpu.SemaphoreType.DMA((n_peers, n_items)), pltpu.SemaphoreType.DMA((n_peers, n_items)),
                        pltpu.SemaphoreType.DMA((n_items,))],
        name=name,
    )(*ins)


HBM = pl.BlockSpec(memory_space=pltpu.HBM)
SEM = pl.BlockSpec(memory_space=pltpu.SEMAPHORE)
DATAFLOW = pltpu.SideEffectType.DATAFLOW_SIDE_EFFECTING


def _gather_plan(axes):
    return lambda srcs, lands, me, peer: [
        (srcs[i], _owner_block(lands[i], ax, me), _owner_block(lands[i], ax, peer)) for i, ax in enumerate(axes)]


def _exchange_plan(where):
    return lambda srcs, lands, me, peer: [
        (srcs[i].at[peer], lands[n].at[me, l], lands[n].at[peer, l]) for i, (n, l) in enumerate(where)]


def _split_peers(sibling):
    x, y, c = lax.axis_index("x"), lax.axis_index("y"), lax.axis_index("c")
    peers = [((px, py, c), 2 * px + py) for px, py in _chip_peers(x, y)]
    return 2 * x + y, peers + ([((x, y, 1 - c), 2 * x + y)] if sibling else [])


def _split_start(srcs, lands, plan, *, sibling=False, name):
    ns, nl = len(srcs), len(lands)
    nsem = (3 + sibling) * len(plan(list(srcs), list(lands), 0, 0))

    def body(*refs):
        src_refs, land_refs = refs[:ns], refs[ns:ns + nl]
        send_sems, recv_sems = refs[ns + nl:ns + nl + nsem], refs[ns + nl + nsem:ns + nl + 2 * nsem]
        me, peers = _split_peers(sibling)
        n = 0
        for device, chip in peers:
            for src, dst, _ in plan(src_refs, land_refs, me, chip):
                pltpu.make_async_remote_copy(src_ref=src, dst_ref=dst, send_sem=send_sems[n], recv_sem=recv_sems[n],
                                             device_id=device, device_id_type=MESH).start()
                n += 1
        refs[-1][...] = jnp.zeros_like(refs[-1])

    arrays = list(srcs) + list(lands)
    out = pl.pallas_call(
        body, name=name, in_specs=[HBM] * (ns + nl),
        out_specs=[SEM] * (2 * nsem) + [HBM] * (ns + nl) + [pl.BlockSpec(memory_space=pltpu.VMEM)],
        out_shape=[pltpu.SemaphoreType.DMA(())] * (2 * nsem) + [pltpu.HBM(a.shape, a.dtype) for a in arrays]
        + [jax.ShapeDtypeStruct((8, LANES), F32)],
        input_output_aliases={i: 2 * nsem + i for i in range(ns + nl)},
        compiler_params=pltpu.CompilerParams(has_side_effects=DATAFLOW),
    )(*[pltpu.with_memory_space_constraint(a, pltpu.HBM) for a in arrays])
    sems, rest = out[:2 * nsem], out[2 * nsem:]
    return sems[:nsem], sems[nsem:], rest[:ns], rest[ns:ns + nl], rest[-1]


def _split_wait(handle, after, plan, *, sibling=False, name):
    send_sems, recv_sems, srcs, lands, _ = handle
    ns, nl, nsem = len(srcs), len(lands), len(send_sems)

    def body(*refs):
        src_refs, land_refs = refs[:ns], refs[ns:ns + nl]
        send_refs, recv_refs = refs[ns + nl:ns + nl + nsem], refs[ns + nl + nsem:ns + nl + 2 * nsem]
        me, peers = _split_peers(sibling)
        n = 0
        for device, chip in peers:
            for src, _, arrival in plan(src_refs, land_refs, me, chip):
                cp = pltpu.make_async_remote_copy(src_ref=src, dst_ref=arrival, send_sem=send_refs[n],
                                                  recv_sem=recv_refs[n], device_id=device, device_id_type=MESH)
                cp.wait_send()
                cp.wait_recv()
                n += 1

    out = pl.pallas_call(
        body, name=name, in_specs=[HBM] * (ns + nl) + [SEM] * (2 * nsem) + [ANY], out_specs=[HBM] * (ns + nl),
        out_shape=[pltpu.HBM(a.shape, a.dtype) for a in list(srcs) + list(lands)],
        input_output_aliases={i: i for i in range(ns + nl)},
        compiler_params=pltpu.CompilerParams(has_side_effects=DATAFLOW),
    )(*srcs, *lands, *send_sems, *recv_sems, after)
    return out[ns:]


def _exchange_sibling(arrays, *, name):
    n = len(arrays)

    def body(*refs):
        x, y, c = lax.axis_index("x"), lax.axis_index("y"), lax.axis_index("c")
        remote = [(0, i, refs[i], refs[n + i], refs[n + i], (x, y, 1 - c)) for i in range(n)]
        _run_copies([], remote, *refs[2 * n:])

    return _comm_call(body, arrays, [jax.ShapeDtypeStruct(a.shape, a.dtype) for a in arrays], n, 1, name=name)


def _sum_slots(r, *, token=None, name):
    shape = r.shape[1:]
    cols = shape[-1]
    rows = math.prod(shape) // cols
    tr = max(d for d in range(8, rows + 1, 8) if rows % d == 0 and d * cols * 16 <= (4 << 20))

    def body(r_ref, *refs):
        total = ((r_ref[0] + r_ref[1]) + r_ref[2]) + r_ref[3]
        refs[-1][...] = total if token is None else total + refs[0][0:1, 0:1]

    in_specs = [pl.BlockSpec((N_CHIPS, tr, cols), lambda i: (0, i, 0))]
    in_specs += [] if token is None else [_const((8, LANES))]
    return pl.pallas_call(
        body, grid=(rows // tr,), in_specs=in_specs,
        out_specs=_rows(tr, cols), out_shape=jax.ShapeDtypeStruct((rows, cols), F32), compiler_params=_cp(1),
        name=name,
    )(r.reshape(N_CHIPS, rows, cols), *([] if token is None else [token])).reshape(shape)


FIRST_WEIGHTS = ("ev_w_in", "ev_w_q_up", "ev_w_kv_up", "ev_w_out")
LATER_WEIGHTS = tuple(n for n in MATMUL_WEIGHTS if n not in FIRST_WEIGHTS)
LAST_GRADS = ("ev_w_in", "ev_w_q_up", "ev_w_kv_up")
EARLIER_GRADS = tuple(n for n in MATMUL_WEIGHTS if n not in LAST_GRADS)


def _my_chip():
    return 2 * lax.axis_index("x") + lax.axis_index("y")


def _gather_first(w, work):
    small = _pack([w[n] for n in SMALL_SHARDED], 8 * LANES)
    stacked = [n for n in FIRST_WEIGHTS if SHARD_AXIS[n] == w[n].ndim - 1 and w[n].shape[-1] % LANES]
    shards = [w[n].astype(BF16)[None] if n in stacked else w[n].astype(BF16) for n in FIRST_WEIGHTS] + [small]
    axes = [0 if n in stacked else SHARD_AXIS[n] for n in FIRST_WEIGHTS] + [0]
    plan = _gather_plan(axes)
    lands = [lax.empty(tuple(d * (N_CHIPS if a == ax else 1) for a, d in enumerate(s.shape)), s.dtype)
             for s, ax in zip(shards, axes)]
    handle = _split_start(shards, lands, plan, sibling=True, name="gather_first_start")
    done = work(handle[4])
    got = _split_wait(handle, done[0], plan, sibling=True, name="gather_first_wait")
    full = {n: w[n] for n in REPLICATED}
    for n, g in zip(FIRST_WEIGHTS, got[:-1]):
        full[n] = jnp.concatenate([g[q] for q in range(N_CHIPS)], axis=SHARD_AXIS[n]) if n in stacked else g
    per_chip = [_unpack(got[-1][q * small.shape[0]:(q + 1) * small.shape[0]].reshape(-1),
                        [w[n].shape for n in SMALL_SHARDED]) for q in range(N_CHIPS)]
    for i, n in enumerate(SMALL_SHARDED):
        full[n] = jnp.concatenate([per_chip[q][i] for q in range(N_CHIPS)], axis=SHARD_AXIS[n])
    return full, done


def _gather_later_start(w, after):
    behind = (after.reshape(-1)[0] * 0).astype(BF16)
    shards = [w[n].astype(BF16) + (behind if n == "od_w_rgate" else 0) for n in LATER_WEIGHTS]
    axes = [SHARD_AXIS[n] for n in LATER_WEIGHTS]
    lands = [lax.empty(tuple(d * (N_CHIPS if a == ax else 1) for a, d in enumerate(s.shape)), s.dtype)
             for s, ax in zip(shards, axes)]
    plan = _gather_plan(axes)
    return _split_start(shards, lands, plan, sibling=True, name="gather_later_start"), plan


def _owner_major(g, axis):
    shape = g.shape
    size = shape[axis] // N_CHIPS
    g = jnp.moveaxis(g.reshape(shape[:axis] + (N_CHIPS, size) + shape[axis + 1:]), axis, 0)
    return g.reshape(N_CHIPS, -1, shape[-1] if axis < len(shape) - 1 else size)


def _exchange_start(items, *, cross, name):
    me = _my_chip()
    srcs, lands, where = [], [], []
    for n, layers in enumerate(items):
        land = lax.empty((N_CHIPS, len(layers)) + layers[0].shape[1:], layers[0].dtype)
        for l, a in enumerate(layers):
            if not cross:
                own = lax.dynamic_index_in_dim(a, me, 0, keepdims=True)[:, None]
                land = lax.dynamic_update_slice(land, own, (me, l) + (0,) * (a.ndim - 1))
            srcs.append(a)
            where.append((n, l))
        lands.append(land)
    plan = _exchange_plan(where)
    return _split_start(srcs, lands, plan, sibling=cross, name=name), plan


def _earlier_items(grads, full_shapes):
    small = [_pack([jnp.split(grads[n].reshape(full_shapes[n]), N_CHIPS, axis=SHARD_AXIS[n])[q]
                    for n in SMALL_SHARDED], 8 * LANES) for q in range(N_CHIPS)]
    return [grads[n] for n in EARLIER_GRADS] + [[jnp.stack(small)]]


def _last_items(big, grads, full_shapes, loss):
    repl = _pack([grads[n].reshape(full_shapes[n]) for n in REPLICATED] + [loss.reshape(1)], 8 * LANES)
    return [big[n] for n in LAST_GRADS] + [[jnp.stack([repl] * N_CHIPS)]]


def kernel(
        x, mem, positions, ev_norm, ev_w_in, ev_pool_w, ev_pool_scale, ev_q_norm, ev_w_q_up, ev_kv_norm,
        ev_w_kv_up, ev_w_out, od_norm, od_w_in, od_conv_w, od_conv_b, od_w_rgate, od_b_rgate, od_w_igate,
        od_b_igate, od_lambda, od_w_out, xa_norm_x, xa_norm_mem, xa_w_q, xa_w_kv, xa_w_o, ffn_norm,
        ffn_w_gate_up, ffn_w_down, final_norm, loss_target, m_ev_norm, m_ev_w_in, m_ev_pool_w, m_ev_pool_scale,
        m_ev_q_norm, m_ev_w_q_up, m_ev_kv_norm, m_ev_w_kv_up, m_ev_w_out, m_od_norm, m_od_w_in, m_od_conv_w,
        m_od_conv_b, m_od_w_rgate, m_od_b_rgate, m_od_w_igate, m_od_b_igate, m_od_lambda, m_od_w_out,
        m_xa_norm_x, m_xa_norm_mem, m_xa_w_q, m_xa_w_kv, m_xa_w_o, m_ffn_norm, m_ffn_w_gate_up, m_ffn_w_down,
        m_final_norm, v_ev_norm, v_ev_w_in, v_ev_pool_w, v_ev_pool_scale, v_ev_q_norm, v_ev_w_q_up,
        v_ev_kv_norm, v_ev_w_kv_up, v_ev_w_out, v_od_norm, v_od_w_in, v_od_conv_w, v_od_conv_b, v_od_w_rgate,
        v_od_b_rgate, v_od_w_igate, v_od_b_igate, v_od_lambda, v_od_w_out, v_xa_norm_x, v_xa_norm_mem, v_xa_w_q,
        v_xa_w_kv, v_xa_w_o, v_ffn_norm, v_ffn_w_gate_up, v_ffn_w_down, v_final_norm):
    given = dict(locals())
    w = {n: given[n] for n in WEIGHTS}
    full_shapes = {n: tuple(d * (N_CHIPS if a == SHARD_AXIS.get(n) else 1) for a, d in enumerate(w[n].shape))
                   for n in WEIGHTS}
    full, tabs = _gather_first(w, lambda token: _rope_tables(positions[0], token))
    later, later_plan = _gather_later_start(w, full["ev_w_out"])
    full["ev_norm"] = full["ev_norm"] + later[4][0:1, 0:1]
    exchange = {}

    def later_weights(after):
        return dict(zip(LATER_WEIGHTS, _split_wait(later, after, later_plan, sibling=True, name="gather_later_wait")))

    def exchange_earlier(grads):
        exchange["handle"], exchange["plan"] = _exchange_start(_earlier_items(grads, full_shapes), cross=True,
                                                               name="exchange_earlier_start")
        return exchange["handle"][4]

    loss, grad_x, big, grads = _local_step(x[0], mem[0], positions[0], loss_target[0], full, later_weights,
                                           exchange_earlier, tabs)
    earlier = EARLIER_GRADS + ("small",)
    got = dict(zip(earlier, _split_wait(exchange["handle"], grad_x, exchange["plan"], sibling=True,
                                        name="exchange_earlier_wait")))
    last, last_plan = _exchange_start(_last_items(big, grads, full_shapes, loss), cross=False,
                                      name="exchange_last_start")
    out = {}

    def finish(names, landed, token, tag):
        mine = [_sum_slots(landed[n], token=token if i == 0 else None, name=f"sum_chips_{n}")
                for i, n in enumerate(names)]
        other = _exchange_sibling(mine, name=f"exchange_sibling_{tag}")
        total = None
        for n, a, b in zip(names, mine, other):
            if n in MATMUL_WEIGHTS:
                out[n] = _adamw(w[n], a.reshape(w[n].shape), b.reshape(w[n].shape), given["m_" + n], given["v_" + n],
                                name=f"adamw_{n}")
                continue
            group = SMALL_SHARDED if n == "small" else REPLICATED
            spare = [jnp.zeros((1,), F32)] if group is REPLICATED else []
            packed = [_pack([given[pre + k] for k in group] + spare, 8 * LANES) for pre in ("", "m_", "v_")]
            res = _adamw(packed[0], a.reshape(packed[0].shape), b.reshape(packed[0].shape), packed[1], packed[2],
                         name=f"adamw_{n}")
            shapes = [w[k].shape for k in group] + [(1,)] * len(spare)
            for j, arrs in enumerate(zip(*[_unpack(r.reshape(-1), shapes) for r in res])):
                if j < len(group):
                    out[group[j]] = list(arrs)
                else:
                    total = arrs[0][0]
        return total

    finish(earlier, got, last[4], "earlier")
    names = LAST_GRADS + ("replicated",)
    got = dict(zip(names, _split_wait(last, out[EARLIER_GRADS[-1]][1], last_plan, name="exchange_last_wait")))
    loss = finish(names, got, None, "last")
    return (loss, grad_x[None], *[out[n][k] for k in range(4) for n in WEIGHTS])
```

```python
import math

import jax
import jax.numpy as jnp
from jax import lax
from jax.experimental import pallas as pl
from jax.experimental.pallas import tpu as pltpu

F32 = jnp.float32
BF16 = jnp.bfloat16

D_MODEL = 1024
POOL_DIM = 512
POOL_WINDOWS = (2, 4, 8, 16)
POOL_GROUP = 128
MLA_HEADS = 8
QK_NOPE = 64
QK_ROPE = 32
QK_DIM = QK_NOPE + QK_ROPE
V_HEAD = 64
HEAD_PAD = 128
Q_RANK = 256
KV_RANK = 128
ROPE_BASE = 10000.0
LRU_HEADS = 4
LRU_HEAD_DIM = 256
CONV_WIDTH = 4
LRU_C = 8.0
MEM_HEADS = 4
MEM_HEAD_DIM = 256
D_FF = 2816
RMS_EPS = 1e-6
NEG_INF = -1e30

ADAM_LR = 0.001
ADAM_B1 = 0.9
ADAM_B2 = 0.999
ADAM_EPS = 1e-08
ADAM_WD = 0.01
ADAM_STEP = 10

N_CHIPS = 4
LANES = 128
VMEM_LIMIT = 56 * 1024 * 1024
MESH = pl.DeviceIdType.MESH
ANY = pl.BlockSpec(memory_space=pl.ANY)
MIX_DIM = POOL_DIM + MLA_HEADS * HEAD_PAD

NN = (((1,), (0,)), ((), ()))
NT = (((1,), (1,)), ((), ()))
TN = (((0,), (0,)), ((), ()))


def _cp(n):
    return pltpu.CompilerParams(dimension_semantics=("arbitrary",) * n, vmem_limit_bytes=VMEM_LIMIT)


def _dot(a, b, dims=NN):
    return lax.dot_general(a, b, dims, preferred_element_type=F32)


def _row_tile(S):
    return 1024 if S % 1024 == 0 else min(S, 512)


def _rows(ts, w, cb=0):
    return pl.BlockSpec((ts, w), lambda i: (i, cb))


def _const(shape):
    return pl.BlockSpec(shape, lambda i: (0,) * len(shape))


MM_VMEM_BUDGET = 46 * 1024 * 1024


def _mm(a, bs, epi, outs, *, tn, nj, nt=False, also=None, extras=(), rows=(), sums=(), a_cb=0, k=None, tm=None,
        name):
    M = a.shape[0]
    k = k or a.shape[1]
    nb, ne, nr, no = len(bs), len(extras), len(rows), len(outs)
    lhs = [(a, k, a_cb, b) for b in bs[:1]] + ([(also[0], also[0].shape[1], 0, also[1])] if also else [])
    if tm is None:
        per_row = 2 * (sum(kk * x.dtype.itemsize for x, kk, _, _ in lhs)
                       + sum(e.dtype.itemsize for e, _ in extras) * tn
                       + sum(jnp.dtype(dt).itemsize for _, dt, _ in outs) * tn) + nb * tn * 4
        weights = (1 if nj == 1 else 2) * (sum(b.dtype.itemsize for b, _, _ in bs) * k
                                           + (also[1][0].dtype.itemsize * lhs[-1][1] if also else 0)) * tn
        tm = 1024 if M % 1024 == 0 and 1024 * per_row + weights <= MM_VMEM_BUDGET else min(M, 512)
    dims = NT if nt else NN
    assert not sums or nj == 1
    na = 2 if also else 0

    def body(*refs):
        av = refs[0][...].astype(BF16)
        accs = [_dot(av, r[...].astype(BF16), dims) for r in refs[1:1 + nb]]
        if also:
            accs[0] = accs[0] + _dot(refs[1 + nb][...].astype(BF16), refs[2 + nb][...].astype(BF16), dims)
        refs = refs[:1 + nb] + refs[1 + nb + na:]
        vals = epi(accs, [r[...] for r in refs[1 + nb:1 + nb + ne + nr]])
        outs_refs = refs[1 + nb + ne + nr:]
        for o, v in zip(outs_refs[:no], vals[:no]):
            o[...] = v.astype(o.dtype)
        if sums:
            @pl.when(pl.program_id(1) == 0)
            def _():
                for o in outs_refs[no:]:
                    o[...] = jnp.zeros_like(o)

            for o, v in zip(outs_refs[no:], vals[no:]):
                o[...] += v

    in_specs = [pl.BlockSpec((tm, k), lambda j, i: (i, a_cb))]
    weights = [(k, rb, cb) for (_, rb, cb) in bs]
    if also:
        in_specs_also = pl.BlockSpec((tm, lhs[-1][1]), lambda j, i: (i, 0))
        weights.append((lhs[-1][1], also[1][1], also[1][2]))
    for n, (kk, rb, cb) in enumerate(weights):
        if also and n == nb:
            in_specs.append(in_specs_also)
        mode = dict(pipeline_mode=pl.Buffered(1)) if nj == 1 else {}
        if nt:
            in_specs.append(pl.BlockSpec((tn, kk), lambda j, i, rb=rb, cb=cb: (rb + j, cb), **mode))
        else:
            in_specs.append(pl.BlockSpec((kk, tn), lambda j, i, rb=rb, cb=cb: (rb, cb + j), **mode))
    for (_, cb) in extras:
        in_specs.append(pl.BlockSpec((tm, tn), lambda j, i, cb=cb: (i, cb + j)))
    in_specs += [pl.BlockSpec((1, tn), lambda j, i: (0, 0))] * nr
    out_specs = [pl.BlockSpec((tm, tn), lambda j, i, cb=cb: (i, cb + j)) for (_, _, cb) in outs]
    out_specs += [pl.BlockSpec((1, w), lambda j, i: (0, 0)) for w in sums]
    res = pl.pallas_call(
        body, grid=(nj, M // tm), in_specs=in_specs, out_specs=out_specs,
        out_shape=[jax.ShapeDtypeStruct((M, n), dt) for (n, dt, _) in outs]
        + [jax.ShapeDtypeStruct((1, w), F32) for w in sums],
        compiler_params=_cp(2), name=name,
    )(a, *[b for (b, _, _) in bs], *([also[0], also[1][0]] if also else []), *[e for (e, _) in extras], *rows)
    return res


def _first(accs, extras):
    return [accs[0]]


def _add_res(accs, extras):
    return [accs[0] + extras[0].astype(F32)]


def _norm_bwd_epilogue(partials):
    def epi(accs, vals):
        dh = accs[0]
        for part in vals[:partials]:
            dh = dh + part.astype(F32)
        x, res, g = vals[partials:partials + 3]
        r = lax.rsqrt(jnp.mean(x * x, axis=-1, keepdims=True) + RMS_EPS)
        n = x * r
        dn = dh * g
        return [r * (dn - n * jnp.mean(dn * n, axis=-1, keepdims=True)) + res, jnp.sum(dh * n, axis=0, keepdims=True)]

    return epi


TN_VMEM_BUDGET = 44 * 1024 * 1024


def _contraction_rows(S, row_bytes, out_elems):
    ts = min(S, 2048)
    while ts > 512 and 2 * (ts * row_bytes + out_elems * 4) > TN_VMEM_BUDGET:
        ts //= 2
    return ts


def _mm_tn(a, b, *, ka=None, a_cb=0, nb=None, b_cb=0, tk=None, tn=None, ts=None, name):
    S = a.shape[0]
    ka = ka or a.shape[1]
    nb = nb or b.shape[1]
    tk = tk or ka
    tn = tn or nb
    ts = ts or _contraction_rows(S, tk * a.dtype.itemsize + tn * b.dtype.itemsize, tk * tn)
    a0, b0 = a_cb * (ka // tk), b_cb * (nb // tn)

    def body(a_ref, b_ref, o_ref):
        @pl.when(pl.program_id(2) == 0)
        def _():
            o_ref[...] = jnp.zeros_like(o_ref)

        o_ref[...] += _dot(a_ref[...].astype(BF16), b_ref[...].astype(BF16), TN)

    return pl.pallas_call(
        body, grid=(ka // tk, nb // tn, S // ts),
        in_specs=[pl.BlockSpec((ts, tk), lambda p, q, s: (s, a0 + p)),
                  pl.BlockSpec((ts, tn), lambda p, q, s: (s, b0 + q))],
        out_specs=pl.BlockSpec((tk, tn), lambda p, q, s: (p, q)),
        out_shape=jax.ShapeDtypeStruct((ka, nb), F32), compiler_params=_cp(3), name=name,
    )(a, b)


def _mm_tn_owners(a, bs, *, name):
    S, ka = a.shape
    nb = sum(b.shape[1] for b in bs)
    tn = nb // N_CHIPS
    ts = _contraction_rows(S, ka * a.dtype.itemsize + len(bs) * tn * bs[0].dtype.itemsize, ka * tn)
    per = N_CHIPS // len(bs)

    def body(a_ref, *refs):
        o_ref = refs[-1]
        q = pl.program_id(0)

        @pl.when(pl.program_id(1) == 0)
        def _():
            o_ref[...] = jnp.zeros_like(o_ref)

        av = a_ref[...].astype(BF16)
        for n, b_ref in enumerate(refs[:-1]):
            @pl.when(q // per == n)
            def _():
                o_ref[0] += _dot(av, b_ref[...].astype(BF16), TN)

    in_specs = [pl.BlockSpec((ts, ka), lambda q, s: (s, 0))]
    for n in range(len(bs)):
        in_specs.append(pl.BlockSpec((ts, tn), lambda q, s, n=n: (jnp.where(q // per == n, s, 0),
                                                                  jnp.clip(q - n * per, 0, per - 1))))
    return pl.pallas_call(
        body, grid=(N_CHIPS, S // ts), in_specs=in_specs,
        out_specs=pl.BlockSpec((1, ka, tn), lambda q, s: (q, 0, 0)),
        out_shape=jax.ShapeDtypeStruct((N_CHIPS, ka, tn), F32), compiler_params=_cp(2), name=name,
    )(a, *bs)


def _mm_tn_grouped(a, b, groups, w, *, name):
    S = a.shape[0]
    ts = _contraction_rows(S, w * (a.dtype.itemsize + b.dtype.itemsize), w * w)

    def body(a_ref, b_ref, o_ref):
        @pl.when(pl.program_id(1) == 0)
        def _():
            o_ref[...] = jnp.zeros_like(o_ref)

        o_ref[0] += _dot(a_ref[...].astype(BF16), b_ref[...].astype(BF16), TN)

    return pl.pallas_call(
        body, grid=(groups, S // ts),
        in_specs=[pl.BlockSpec((ts, w), lambda g, s: (s, g)), pl.BlockSpec((ts, w), lambda g, s: (s, g))],
        out_specs=pl.BlockSpec((1, w, w), lambda g, s: (g, 0, 0)),
        out_shape=jax.ShapeDtypeStruct((groups, w, w), F32), compiler_params=_cp(2), name=name,
    )(a, b)


def _rms(x, g, *, name):
    S, w = x.shape
    ts = _row_tile(S)

    def body(x_ref, g_ref, o_ref):
        xv = x_ref[...]
        r = lax.rsqrt(jnp.mean(xv * xv, axis=-1, keepdims=True) + RMS_EPS)
        o_ref[...] = (xv * r * g_ref[...]).astype(o_ref.dtype)

    return pl.pallas_call(
        body, grid=(S // ts,), in_specs=[_rows(ts, w), _const((1, w))], out_specs=_rows(ts, w),
        out_shape=jax.ShapeDtypeStruct((S, w), BF16), compiler_params=_cp(1), name=name,
    )(x, g.reshape(1, w))


def _norm_gain_grad(x, dy, *, name):
    S, w = x.shape
    ts = _row_tile(S)

    def body(x_ref, dy_ref, dg_ref):
        @pl.when(pl.program_id(0) == 0)
        def _():
            dg_ref[...] = jnp.zeros_like(dg_ref)

        xv = x_ref[...]
        r = lax.rsqrt(jnp.mean(xv * xv, axis=-1, keepdims=True) + RMS_EPS)
        dg_ref[...] += jnp.sum(dy_ref[...] * (xv * r), axis=0, keepdims=True)

    return pl.pallas_call(
        body, grid=(S // ts,), in_specs=[_rows(ts, w), _rows(ts, w)], out_specs=_const((1, w)),
        out_shape=jax.ShapeDtypeStruct((1, w), F32), compiler_params=_cp(1), name=name,
    )(x, dy)


HALO = 16


def _pool_counts(i, ts, rows, first_row):
    t = i * ts + first_row + lax.broadcasted_iota(jnp.int32, (rows, 1), 0)
    return [jnp.minimum(t + 1, w).astype(F32) for w in POOL_WINDOWS]


def _even_front(x, g, w_in, pool_w, pool_scale, g_q, w_q, g_kv, w_kv, ctab, stab, *, name):
    S = x.shape[0]
    ts = min(S, 512)

    def body(x_ref, g_ref, win_ref, pw_ref, sc_ref, gq_ref, wq_ref, gkv_ref, wkv_ref, c_ref, s_ref,
             h_ref, z_ref, y_ref, p_ref, cqn_ref, ckvn_ref, q_ref, k_ref, v_ref, tail):
        i = pl.program_id(0)

        def normed(t, gain):
            r = lax.rsqrt(jnp.mean(t * t, axis=-1, keepdims=True) + RMS_EPS)
            return (t * r * gain).astype(BF16)

        h = normed(x_ref[...], g_ref[...])
        h_ref[...] = h
        z = _dot(h, win_ref[...])
        z_ref[...] = z
        u = z[:, :POOL_DIM]
        xe = jnp.concatenate([jnp.where(i > 0, tail[...], 0.0), u], axis=0)
        tail[...] = u[ts - HALO:]
        sums = []
        s = xe
        for sh in (1, 2, 4, 8):
            s = s + pltpu.roll(s, sh, 0)
            sums.append(s)
        cnts = _pool_counts(i, ts, ts, 0)
        for grp in range(4):
            lo, hi = grp * POOL_GROUP, (grp + 1) * POOL_GROUP
            pooled = (sums[grp][HALO:, lo:hi] / cnts[grp] - u[:, lo:hi]).astype(BF16)
            p_ref[:, lo:hi] = pooled
            y_ref[:, lo:hi] = (_dot(pooled, pw_ref[grp]) * sc_ref[:, lo:hi]).astype(y_ref.dtype)
        cqn = normed(z[:, POOL_DIM:POOL_DIM + Q_RANK], gq_ref[...])
        ckvn = normed(z[:, POOL_DIM + Q_RANK:POOL_DIM + Q_RANK + KV_RANK], gkv_ref[...])
        cqn_ref[...] = cqn
        ckvn_ref[...] = ckvn
        q = _dot(cqn, wq_ref[...])
        kv = _dot(ckvn, wkv_ref[...])
        c, sn = c_ref[...], s_ref[...]
        kr = z[:, D_MODEL - HEAD_PAD:]
        kr_rot = kr * c + _rope_partner(kr) * sn
        lane = lax.broadcasted_iota(jnp.int32, (ts, HEAD_PAD), 1)
        for hd in range(MLA_HEADS):
            lo, hi = hd * HEAD_PAD, (hd + 1) * HEAD_PAD
            qh = q[:, lo:hi]
            q_ref[:, lo:hi] = (qh * c + _rope_partner(qh) * sn).astype(q_ref.dtype)
            k_ref[:, lo:hi] = (kv[:, lo:hi] + kr_rot).astype(k_ref.dtype)
            v_ref[:, lo:hi] = jnp.where(lane == V_HEAD, 1.0, kv[:, D_MODEL + lo:D_MODEL + hi]).astype(v_ref.dtype)

    wide = jax.ShapeDtypeStruct((S, D_MODEL), BF16)
    return pl.pallas_call(
        body, grid=(S // ts,),
        in_specs=[_rows(ts, D_MODEL), _const((1, D_MODEL)), _const((D_MODEL, D_MODEL)),
                  _const((4, POOL_GROUP, POOL_GROUP)), _const((1, POOL_DIM)), _const((1, Q_RANK)),
                  _const((Q_RANK, D_MODEL)), _const((1, KV_RANK)), _const((KV_RANK, 2 * D_MODEL)),
                  _rows(ts, HEAD_PAD), _rows(ts, HEAD_PAD)],
        out_specs=[_rows(ts, D_MODEL), _rows(ts, D_MODEL), _rows(ts, POOL_DIM), _rows(ts, POOL_DIM),
                   _rows(ts, Q_RANK), _rows(ts, KV_RANK), _rows(ts, D_MODEL), _rows(ts, D_MODEL), _rows(ts, D_MODEL)],
        out_shape=[wide, jax.ShapeDtypeStruct((S, D_MODEL), F32), jax.ShapeDtypeStruct((S, MIX_DIM), BF16),
                   jax.ShapeDtypeStruct((S, POOL_DIM), BF16), jax.ShapeDtypeStruct((S, Q_RANK), BF16),
                   jax.ShapeDtypeStruct((S, KV_RANK), BF16), wide, wide, wide],
        scratch_shapes=[pltpu.VMEM((HALO, POOL_DIM), F32)], compiler_params=_cp(1), name=name,
    )(x, g.reshape(1, D_MODEL), w_in, pool_w, pool_scale, g_q.reshape(1, Q_RANK), w_q, g_kv.reshape(1, KV_RANK), w_kv,
      ctab, stab)


def _norm_bwd_values(xv, gain, dy):
    r = lax.rsqrt(jnp.mean(xv * xv, axis=-1, keepdims=True) + RMS_EPS)
    n = xv * r
    dn = dy * gain
    return r * (dn - n * jnp.mean(dn * n, axis=-1, keepdims=True)), jnp.sum(dy * n, axis=0, keepdims=True)


def _even_back(dq_rot, dk_cat, dv, dmix, pooled, z, x, dxo, ctab, stab, w_q, w_kv, w_in, pool_w, pool_scale, g_q, g_kv,
               g_x, *, name):
    S = x.shape[0]
    ts = min(S, 512)
    nh = ts // HALO
    last = S // HALO - 1
    n = ts + HALO

    def body(dq_ref, dk_ref, dv_ref, dy_ref, dyh_ref, p_ref, z_ref, x_ref, dxo_ref, c_ref, s_ref, wq_ref, wkv_ref,
             win_ref, pw_ref, sc_ref, gq_ref, gkv_ref, gx_ref,
             dx_ref, dqp_ref, dz_ref, dyp_ref, dgq_ref, dgkv_ref, dsc_ref, dgx_ref):
        i = pl.program_id(0)

        @pl.when(i == 0)
        def _():
            for ref in (dgq_ref, dgkv_ref, dsc_ref, dgx_ref):
                ref[...] = jnp.zeros_like(ref)

        c, sn = c_ref[...], s_ref[...]
        z = z_ref[...]
        dk = dk_ref[...]
        for hd in range(MLA_HEADS):
            lo, hi = hd * HEAD_PAD, (hd + 1) * HEAD_PAD
            g = dq_ref[:, lo:hi]
            dqp_ref[:, lo:hi] = (g * c + _rope_partner(g * sn)).astype(dqp_ref.dtype)
            heads_sum = dk[:, lo:hi] if hd == 0 else heads_sum + dk[:, lo:hi]
        lane = lax.broadcasted_iota(jnp.int32, heads_sum.shape, 1)
        dkr = jnp.where((lane >= QK_NOPE) & (lane < QK_DIM), heads_sum * c + _rope_partner(heads_sum * sn), 0.0)
        dcqn = _dot(dqp_ref[...], wq_ref[...], NT)
        dckvn = _dot(dk.astype(BF16), wkv_ref[:, :D_MODEL], NT) + _dot(dv_ref[...].astype(BF16),
                                                                       wkv_ref[:, D_MODEL:], NT)
        dcq, dgq = _norm_bwd_values(z[:, POOL_DIM:POOL_DIM + Q_RANK], gq_ref[...], dcqn)
        dckv, dgkv = _norm_bwd_values(z[:, POOL_DIM + Q_RANK:POOL_DIM + Q_RANK + KV_RANK], gkv_ref[...], dckvn)
        dgq_ref[...] += dgq
        dgkv_ref[...] += dgkv
        dyv = dy_ref[...].astype(F32)
        dyh = jnp.where(i < pl.num_programs(0) - 1, dyh_ref[...].astype(F32), 0.0)
        dypre = (jnp.concatenate([dyv, dyh], axis=0) * sc_ref[...]).astype(BF16)
        dyp_ref[...] = dypre[:ts]
        cnts = _pool_counts(i, ts, n, 0)
        dsc = []
        for grp in range(4):
            lo, hi = grp * POOL_GROUP, (grp + 1) * POOL_GROUP
            dsc.append(jnp.sum(dyv[:, lo:hi] * _dot(p_ref[:, lo:hi], pw_ref[grp]), axis=0, keepdims=True))
            dpool = _dot(dypre[:, lo:hi], pw_ref[grp], NT)
            s = dpool / cnts[grp]
            for sh in (1, 2, 4, 8)[:grp + 1]:
                s = s + pltpu.roll(s, n - sh, 0)
            dz_ref[:, lo:hi] = (s[:ts] - dpool[:ts]).astype(dz_ref.dtype)
        dsc_ref[...] += jnp.concatenate(dsc, axis=1)
        dz_ref[:, POOL_DIM:POOL_DIM + Q_RANK] = dcq.astype(dz_ref.dtype)
        dz_ref[:, POOL_DIM + Q_RANK:POOL_DIM + Q_RANK + KV_RANK] = dckv.astype(dz_ref.dtype)
        dz_ref[:, D_MODEL - HEAD_PAD:] = dkr.astype(dz_ref.dtype)
        dx, dgx = _norm_bwd_values(x_ref[...], gx_ref[...], _dot(dz_ref[...], win_ref[...], NT))
        dx_ref[...] = dx + dxo_ref[...]
        dgx_ref[...] += dgx

    wide, pool = _rows(ts, D_MODEL), _rows(ts, POOL_DIM)
    f32 = lambda w: jax.ShapeDtypeStruct((1, w), F32)
    return pl.pallas_call(
        body, grid=(S // ts,),
        in_specs=[wide, wide, wide, pool,
                  pl.BlockSpec((HALO, POOL_DIM), lambda i: (jnp.minimum((i + 1) * nh, last), 0)), pool, wide, wide, wide,
                  _rows(ts, HEAD_PAD), _rows(ts, HEAD_PAD), _const((Q_RANK, D_MODEL)), _const((KV_RANK, 2 * D_MODEL)),
                  _const((D_MODEL, D_MODEL)), _const((4, POOL_GROUP, POOL_GROUP)), _const((1, POOL_DIM)),
                  _const((1, Q_RANK)), _const((1, KV_RANK)), _const((1, D_MODEL))],
        out_specs=[wide, wide, wide, pool, _const((1, Q_RANK)), _const((1, KV_RANK)), _const((1, POOL_DIM)),
                   _const((1, D_MODEL))],
        out_shape=[jax.ShapeDtypeStruct((S, D_MODEL), F32), jax.ShapeDtypeStruct((S, D_MODEL), BF16),
                   jax.ShapeDtypeStruct((S, D_MODEL), BF16), jax.ShapeDtypeStruct((S, POOL_DIM), BF16),
                   f32(Q_RANK), f32(KV_RANK), f32(POOL_DIM), f32(D_MODEL)],
        compiler_params=_cp(1), name=name,
    )(dq_rot, dk_cat, dv, dmix, dmix, pooled, z, x, dxo, ctab, stab, w_q, w_kv, w_in, pool_w, pool_scale,
      g_q.reshape(1, Q_RANK), g_kv.reshape(1, KV_RANK), g_x.reshape(1, D_MODEL))


def _rope_partner(t):
    lane = lax.broadcasted_iota(jnp.int32, t.shape, 1)
    swapped = jnp.where(lane < QK_NOPE + QK_ROPE // 2, pltpu.roll(t, HEAD_PAD - QK_ROPE // 2, 1),
                        pltpu.roll(t, QK_ROPE // 2, 1))
    return jnp.where((lane >= QK_NOPE) & (lane < QK_DIM), swapped, 0.0)


ATT_SCALE = QK_DIM ** -0.5
LOG2E = math.log2(math.e)


HEADS_PER_STEP = 2
ATT_COL0 = POOL_DIM // HEAD_PAD


FWD_TILE = 1024


def _stat_rows(col):
    return jnp.broadcast_to(col, (col.shape[0], LANES)).T[0:8]


def _retile_rows(rows, tq):
    heads, n8, t = rows.shape
    if t == tq:
        return rows
    flat = rows.reshape(heads, n8 // 8, 8, t)[:, :, 0].reshape(heads, -1, 1, tq)
    return jnp.broadcast_to(flat, (heads, flat.shape[1], 8, tq)).reshape(heads, -1, tq)


def _flash_fwd(q, k, v, mix, *, name):
    S = q.shape[0]
    tq = FWD_TILE if S % FWD_TILE == 0 else min(S, 512)
    nq = S // tq
    hs = HEADS_PER_STEP
    wide = hs * HEAD_PAD

    def body(q_ref, k_ref, v_ref, mix_ref, o_ref, lse_ref):
        qi = pl.program_id(1)
        qv = [q_ref[:, a * HEAD_PAD:(a + 1) * HEAD_PAD] for a in range(hs)]

        def update(m, acc, s, v):
            m_new = jnp.maximum(m, jnp.max(s, axis=-1, keepdims=True))
            p = jnp.exp2((s - m_new) * (ATT_SCALE * LOG2E))
            alpha = jnp.exp2((m - m_new) * (ATT_SCALE * LOG2E))
            return m_new, alpha * acc + _dot(p.astype(BF16), v)

        def step(j, carry, masked):
            off = pl.multiple_of(j * tq, tq)
            out = []
            for a in range(hs):
                head = slice(a * HEAD_PAD, (a + 1) * HEAD_PAD)
                s = _dot(qv[a], k_ref[pl.ds(off, tq), head], NT)
                if masked:
                    row = lax.broadcasted_iota(jnp.int32, (tq, tq), 0)
                    col = lax.broadcasted_iota(jnp.int32, (tq, tq), 1)
                    s = jnp.where(col <= row, s, NEG_INF)
                out.append(update(*carry[a], s, v_ref[pl.ds(off, tq), head]))
            return tuple(out)

        one = (jnp.full((tq, 1), NEG_INF, F32), jnp.zeros((tq, HEAD_PAD), F32))
        carry = step(qi, lax.fori_loop(0, qi, lambda j, c: step(j, c, False), (one,) * hs), True)
        for a in range(hs):
            m, acc = carry[a]
            l = acc[:, V_HEAD:V_HEAD + 1]
            o_ref[:, a * HEAD_PAD:(a + 1) * HEAD_PAD] = (acc / l).astype(o_ref.dtype)
            lse_ref[a] = _stat_rows(m * ATT_SCALE + jnp.log(l))

    blk = pl.BlockSpec((tq, wide), lambda h, i: (i, h))
    full = pl.BlockSpec((S, wide), lambda h, i: (0, h))
    return pl.pallas_call(
        body, grid=(MLA_HEADS // hs, nq), in_specs=[blk, full, full, ANY],
        out_specs=[pl.BlockSpec((tq, wide), lambda h, i: (i, ATT_COL0 // hs + h)),
                   pl.BlockSpec((hs, 8, tq), lambda h, i: (h, i, 0))],
        out_shape=[jax.ShapeDtypeStruct(mix.shape, mix.dtype), jax.ShapeDtypeStruct((MLA_HEADS, nq * 8, tq), F32)],
        input_output_aliases={3: 0}, compiler_params=_cp(2), name=name,
    )(q, k, v, mix)


BWD_TILE = 1024
BWD_HEADS_PER_STEP = 1


def _bwd_tile(S):
    return BWD_TILE if S % BWD_TILE == 0 else min(S, 512)


def _attn_delta(dmix, mix, *, name):
    S = mix.shape[0]
    ts = _bwd_tile(S)
    half = MLA_HEADS // 2
    halves = [_rows(ts, half * HEAD_PAD, 1), _rows(ts, half * HEAD_PAD, 2)]

    def body(do0_ref, do1_ref, o0_ref, o1_ref, d_ref):
        for n, (do_ref, o_ref) in enumerate(((do0_ref, o0_ref), (do1_ref, o1_ref))):
            prod = do_ref[...].astype(F32) * o_ref[...].astype(F32)
            for a in range(half):
                d_ref[n * half + a] = _stat_rows(
                    jnp.sum(prod[:, a * HEAD_PAD:(a + 1) * HEAD_PAD], axis=-1, keepdims=True))

    return pl.pallas_call(
        body, grid=(S // ts,), in_specs=halves + halves,
        out_specs=pl.BlockSpec((MLA_HEADS, 8, ts), lambda i: (0, i, 0)),
        out_shape=jax.ShapeDtypeStruct((MLA_HEADS, (S // ts) * 8, ts), F32), compiler_params=_cp(1), name=name,
    )(dmix, dmix, mix, mix)


def _flash_bwd(q, k, v, dmix, lse_rows, delta_rows, *, name):
    S = q.shape[0]
    tq = _bwd_tile(S)
    nq = S // tq
    hs = BWD_HEADS_PER_STEP
    wide = hs * HEAD_PAD

    def body(q_hbm, do_hbm, lse_ref, dl_ref, k_ref, v_ref, dq_hbm, dk_ref, dv_ref, q_all, do_all, dq_all):
        g, j = pl.program_id(0), pl.program_id(1)
        cols = pl.multiple_of(g * wide, wide)

        @pl.when(j == 0)
        def _():
            pltpu.sync_copy(q_hbm.at[:, pl.ds(cols, wide)], q_all)
            pltpu.sync_copy(do_hbm.at[:, pl.ds(POOL_DIM + cols, wide)], do_all)
            dq_all[...] = jnp.zeros_like(dq_all)

        heads = [slice(a * HEAD_PAD, (a + 1) * HEAD_PAD) for a in range(hs)]
        kv = [k_ref[:, a] for a in heads]
        vv = [v_ref[:, a] for a in heads]

        def block(a, keys, rows, lse2, dl, first_query):
            qv, dov = q_all[rows, heads[a]], do_all[rows, heads[a]]
            st = _dot(kv[a][:keys], qv, NT)
            if first_query is not None:
                krow = lax.broadcasted_iota(jnp.int32, st.shape, 0)
                qcol = lax.broadcasted_iota(jnp.int32, st.shape, 1) + first_query
                st = jnp.where(krow <= qcol, st, NEG_INF)
            pt = jnp.exp2(st * (ATT_SCALE * LOG2E) - lse2)
            dst = (pt * (_dot(vv[a][:keys], dov, NT) - dl)).astype(BF16)
            dq_all[rows, heads[a]] += _dot(dst, kv[a][:keys], TN)
            return _dot(dst, qv), _dot(pt.astype(BF16), dov)

        def stats(a, i):
            off8 = pl.multiple_of(i * 8, 8)
            return lse_ref[a, pl.ds(off8, 8), :][0:1] * LOG2E, dl_ref[a, pl.ds(off8, 8), :][0:1]

        def step(i, carry):
            rows = pl.ds(pl.multiple_of(i * tq, tq), tq)
            out = []
            for a in range(hs):
                dk, dv = block(a, tq, rows, *stats(a, i), None)
                out.append((carry[a][0] + dk, carry[a][1] + dv))
            return tuple(out)

        def diagonal():
            half = tq // 2
            out = []
            for a in range(hs):
                lse2, dl = stats(a, j)
                off = pl.multiple_of(j * tq, tq)
                dk0, dv0 = block(a, half, pl.ds(off, half), lse2[:, :half], dl[:, :half], 0)
                dk1, dv1 = block(a, tq, pl.ds(pl.multiple_of(off + half, half), half), lse2[:, half:], dl[:, half:], half)
                zero = jnp.zeros((tq - half, HEAD_PAD), F32)
                out.append((dk1 + jnp.concatenate([dk0, zero], axis=0), dv1 + jnp.concatenate([dv0, zero], axis=0)))
            return tuple(out)

        carry = lax.fori_loop(j + 1, nq, step, diagonal())
        for a in range(hs):
            dk_ref[:, heads[a]] = carry[a][0] * ATT_SCALE
            dv_ref[:, heads[a]] = carry[a][1]

        @pl.when(j == nq - 1)
        def _():
            dq_all[...] = dq_all[...] * ATT_SCALE
            pltpu.sync_copy(dq_all, dq_hbm.at[:, pl.ds(cols, wide)])

    blk = pl.BlockSpec((tq, wide), lambda g, j: (j, g))
    stat = pl.BlockSpec((hs, nq * 8, tq), lambda g, j: (g, 0, 0))
    full = jax.ShapeDtypeStruct((S, MLA_HEADS * HEAD_PAD), F32)
    return pl.pallas_call(
        body, grid=(MLA_HEADS // hs, nq), in_specs=[ANY, ANY, stat, stat, blk, blk], out_specs=[ANY, blk, blk],
        out_shape=[full, full, full],
        scratch_shapes=[pltpu.VMEM((S, wide), BF16), pltpu.VMEM((S, wide), BF16), pltpu.VMEM((S, wide), F32)],
        compiler_params=_cp(2), name=name,
    )(q, dmix, lse_rows, delta_rows, k, v)


MEM_SCALE = MEM_HEAD_DIM ** -0.5


def _xattn_probs(qh, kh):
    s = _dot(qh, kh, NT) * MEM_SCALE
    e = jnp.exp(s - jnp.max(s, axis=-1, keepdims=True))
    return e / jnp.sum(e, axis=-1, keepdims=True)


def _xa_block_fwd(x, kvm, w_q, w_o, g, *, name):
    S = x.shape[0]
    ts = _row_tile(S)
    nm = kvm.shape[0]

    def body(x_ref, kv_ref, wq_ref, wo_ref, g_ref, xo_ref, hx_ref, q_ref, o_ref):
        xv = x_ref[...]
        r = lax.rsqrt(jnp.mean(xv * xv, axis=-1, keepdims=True) + RMS_EPS)
        hx = (xv * r * g_ref[...]).astype(BF16)
        hx_ref[...] = hx
        q = _dot(hx, wq_ref[...]).astype(BF16)
        q_ref[...] = q
        for h in range(MEM_HEADS):
            lo, hi = h * MEM_HEAD_DIM, (h + 1) * MEM_HEAD_DIM
            p = _xattn_probs(q[:, lo:hi], kv_ref[:, lo:hi])
            o_ref[:, lo:hi] = _dot(p.astype(BF16), kv_ref[:, D_MODEL + lo:D_MODEL + hi]).astype(o_ref.dtype)
        xo_ref[...] = xv + _dot(o_ref[...], wo_ref[...])

    square = _const((D_MODEL, D_MODEL))
    act = jax.ShapeDtypeStruct((S, D_MODEL), BF16)
    return pl.pallas_call(
        body, grid=(S // ts,),
        in_specs=[_rows(ts, D_MODEL), _const((nm, 2 * D_MODEL)), square, square, _const((1, D_MODEL))],
        out_specs=[_rows(ts, D_MODEL)] * 4, out_shape=[jax.ShapeDtypeStruct((S, D_MODEL), F32), act, act, act],
        compiler_params=_cp(1), name=name,
    )(x, kvm, w_q, w_o, g.reshape(1, D_MODEL))


def _xa_block_bwd(dxo, x, q, kvm, w_q, w_o, g, *, name):
    S = q.shape[0]
    ts = min(S, 512)
    nm = kvm.shape[0]

    def body(dxo_ref, x_ref, q_ref, kv_ref, wq_ref, wo_ref, g_ref, dx_ref, dq_ref, dkv_ref, dg_ref):
        @pl.when(pl.program_id(0) == 0)
        def _():
            dkv_ref[...] = jnp.zeros_like(dkv_ref)
            dg_ref[...] = jnp.zeros_like(dg_ref)

        dxo = dxo_ref[...]
        do = _dot(dxo.astype(BF16), wo_ref[...], NT).astype(BF16)
        for h in range(MEM_HEADS):
            lo, hi = h * MEM_HEAD_DIM, (h + 1) * MEM_HEAD_DIM
            qh, kh, vh = q_ref[:, lo:hi], kv_ref[:, lo:hi], kv_ref[:, D_MODEL + lo:D_MODEL + hi]
            doh = do[:, lo:hi]
            p = _xattn_probs(qh, kh)
            dp = _dot(doh, vh, NT)
            ds = (p * (dp - jnp.sum(dp * p, axis=-1, keepdims=True)) * MEM_SCALE).astype(BF16)
            dq_ref[:, lo:hi] = _dot(ds, kh).astype(dq_ref.dtype)
            dkv_ref[:, lo:hi] += _dot(ds, qh, TN)
            dkv_ref[:, D_MODEL + lo:D_MODEL + hi] += _dot(p.astype(BF16), doh, TN)
        dx, dg = _norm_bwd_epilogue(0)([_dot(dq_ref[...], wq_ref[...], NT)], [x_ref[...], dxo, g_ref[...]])
        dx_ref[...] = dx
        dg_ref[...] += dg

    square = _const((D_MODEL, D_MODEL))
    return pl.pallas_call(
        body, grid=(S // ts,),
        in_specs=[_rows(ts, D_MODEL), _rows(ts, D_MODEL), _rows(ts, D_MODEL), _const((nm, 2 * D_MODEL)), square,
                  square, _const((1, D_MODEL))],
        out_specs=[_rows(ts, D_MODEL), _rows(ts, D_MODEL), _const((nm, 2 * D_MODEL)), _const((1, D_MODEL))],
        out_shape=[jax.ShapeDtypeStruct((S, D_MODEL), F32), jax.ShapeDtypeStruct((S, D_MODEL), BF16),
                   jax.ShapeDtypeStruct((nm, 2 * D_MODEL), F32), jax.ShapeDtypeStruct((1, D_MODEL), F32)],
        compiler_params=_cp(1), name=name,
    )(dxo, x, q, kvm, w_q, w_o, g.reshape(1, D_MODEL))


CONV_HALO = 8


def _sigmoid(x):
    return 0.5 * jnp.tanh(0.5 * x) + 0.5


def _softplus(x):
    return jnp.maximum(x, 0.0) + jnp.log(1.0 + jnp.exp(-jnp.abs(x)))


def _neg_expm1(x):
    series = -x * (1.0 + x * (1.0 / 2) * (1.0 + x * (1.0 / 3) * (1.0 + x * (1.0 / 4) * (1.0 + x * (1.0 / 5)))))
    return jnp.where(x > -0.05, series, 1.0 - jnp.exp(x))


GELU_C = math.sqrt(2.0 / math.pi)


def _gelu(x):
    return 0.5 * x * (1.0 + jnp.tanh(GELU_C * (x + 0.044715 * x * x * x)))


def _gelu_grad(x):
    t = jnp.tanh(GELU_C * (x + 0.044715 * x * x * x))
    return 0.5 * (1.0 + t) + 0.5 * x * (1.0 - t * t) * GELU_C * (1.0 + 3 * 0.044715 * x * x)


def _lru_gates_head(xc, w_r, br, w_i, bi, sp, reset):
    xcb = xc.astype(BF16)
    r = _sigmoid(_dot(xcb, w_r) + br)
    ig = _sigmoid(_dot(xcb, w_i) + bi)
    log_a = -LRU_C * r * sp
    a = jnp.where(reset, 0.0, jnp.exp(log_a))
    mult = jnp.where(reset, 1.0, jnp.sqrt(jnp.maximum(_neg_expm1(2.0 * log_a), 0.0)))
    return r, ig, a, mult


def _lru_gates(xc, wr_ref, br, wi_ref, bi, sp, reset):
    heads = []
    for h in range(LRU_HEADS):
        cols = slice(h * LRU_HEAD_DIM, (h + 1) * LRU_HEAD_DIM)
        heads.append(_lru_gates_head(xc[:, cols], wr_ref[h], br[:, cols], wi_ref[h], bi[:, cols], sp[:, cols], reset))
    return tuple(jnp.concatenate(parts, axis=1) for parts in zip(*heads))


SUBLANES = 8


def _compose_groups(a, b, reverse):
    n = a.shape[0]
    row = lax.broadcasted_iota(jnp.int32, a.shape, 0) % SUBLANES
    for s in (1, 2, 4):
        inside = (row < SUBLANES - s) if reverse else (row >= s)
        shift = n - s if reverse else s
        a_s = jnp.where(inside, pltpu.roll(a, shift, 0), 1.0)
        b_s = jnp.where(inside, pltpu.roll(b, shift, 0), 0.0)
        b = a * b_s + b
        a = a * a_s
    return a, b


def _chain_groups(a_buf, h_ref, state, reverse):
    groups = a_buf.shape[0] // SUBLANES

    def group(g, h_in):
        off = pl.multiple_of((groups - 1 - g if reverse else g) * SUBLANES, SUBLANES)
        h = a_buf[pl.ds(off, SUBLANES), :] * h_in + h_ref[pl.ds(off, SUBLANES), :]
        h_ref[pl.ds(off, SUBLANES), :] = h
        return jnp.broadcast_to(h[0:1] if reverse else h[SUBLANES - 1:SUBLANES], h.shape)

    return lax.fori_loop(0, groups, group, state, unroll=4)[0:1]


def _lru_fwd(x, g, w_in, reset, conv_w, conv_b, w_r, b_r, w_i, b_i, lam, *, name):
    S = x.shape[0]
    ts = min(S, 512)
    W = D_MODEL

    def body(x_ref, g_ref, win_ref, rs_ref, cw_ref, cb_ref, wr_ref, br_ref, wi_ref, bi_ref, lam_ref,
             hn_ref, z_ref, xc_ref, h_ref, y_ref, a_buf, carry, tail):
        i = pl.program_id(0)

        @pl.when(i == 0)
        def _():
            carry[...] = jnp.zeros_like(carry)
            tail[...] = jnp.zeros_like(tail)

        xv = x_ref[...]
        hn = (xv * lax.rsqrt(jnp.mean(xv * xv, axis=-1, keepdims=True) + RMS_EPS) * g_ref[...]).astype(BF16)
        hn_ref[...] = hn
        z_ref[...] = _dot(hn, win_ref[...])
        xb = z_ref[:, W:]
        xe = jnp.concatenate([tail[...], xb], axis=0)
        tail[...] = xb[ts - CONV_HALO:]
        reset = rs_ref[...] > 0.5
        sp = _softplus(-lam_ref[...])
        for hd in range(LRU_HEADS):
            cols = slice(hd * LRU_HEAD_DIM, (hd + 1) * LRU_HEAD_DIM)
            xe_h = xe[:, cols]
            xc = cb_ref[:, cols] + cw_ref[3:4, cols] * xe_h[CONV_HALO:]
            for kk in range(CONV_WIDTH - 1):
                xc = xc + cw_ref[kk:kk + 1, cols] * pltpu.roll(xe_h, CONV_WIDTH - 1 - kk, 0)[CONV_HALO:]
            xc_ref[:, cols] = xc
            _, ig, a, mult = _lru_gates_head(xc, wr_ref[hd], br_ref[:, cols], wi_ref[hd], bi_ref[:, cols], sp[:, cols],
                                             reset)
            a_buf[:, cols], h_ref[:, cols] = _compose_groups(a, mult * (ig * xc), False)
        carry[...] = _chain_groups(a_buf, h_ref, jnp.broadcast_to(carry[...], (SUBLANES, W)), False)
        y_ref[...] = (_gelu(z_ref[:, :W]) * h_ref[...]).astype(y_ref.dtype)

    vec = _const((1, W))
    gw = _const((LRU_HEADS, LRU_HEAD_DIM, LRU_HEAD_DIM))
    return pl.pallas_call(
        body, grid=(S // ts,),
        in_specs=[_rows(ts, W), vec, _const((W, 2 * W)), _rows(ts, 1), _const((CONV_WIDTH, W)), vec, gw, vec, gw, vec,
                  vec],
        out_specs=[_rows(ts, W), _rows(ts, 2 * W), _rows(ts, W), _rows(ts, W), _rows(ts, W)],
        out_shape=[jax.ShapeDtypeStruct((S, W), BF16), jax.ShapeDtypeStruct((S, 2 * W), F32),
                   jax.ShapeDtypeStruct((S, W), F32), jax.ShapeDtypeStruct((S, W), F32),
                   jax.ShapeDtypeStruct((S, W), BF16)],
        scratch_shapes=[pltpu.VMEM((ts, W), F32), pltpu.VMEM((1, W), F32), pltpu.VMEM((CONV_HALO, W), F32)],
        compiler_params=_cp(1), name=name,
    )(x, g.reshape(1, W), w_in, reset, conv_w, conv_b, w_r, b_r, w_i, b_i, lam)


def _lru_bwd(dxo, w_out, z, xc, hseq, reset, w_r, b_r, w_i, b_i, lam, *, name):
    S = z.shape[0]
    ts = min(S, 512)
    nt = S // ts
    nh = ts // CONV_HALO
    W = D_MODEL

    def body(dxo_ref, wout_ref, gate_ref, xc_ref, h_ref, hh_ref, rs_ref, wr_ref, br_ref, wi_ref, bi_ref, lam_ref,
             dg_ref, dxc_ref, dpr_ref, dpi_ref, acc_ref, a_buf, dh_buf, carry):
        i = pl.program_id(0)
        tile = nt - 1 - i

        @pl.when(i == 0)
        def _():
            carry[...] = jnp.zeros_like(carry)
            acc_ref[...] = jnp.zeros_like(acc_ref)

        xc = xc_ref[...]
        lam_v = lam_ref[...]
        sp = _softplus(-lam_v)
        reset = rs_ref[...] > 0.5
        r, ig, a, mult = _lru_gates(xc, wr_ref, br_ref[...], wi_ref, bi_ref[...], sp, reset)
        gate = gate_ref[...]
        dyv = _dot(dxo_ref[...].astype(BF16), wout_ref[...], NT)
        h = h_ref[...]
        dg_ref[...] = (dyv * h * _gelu_grad(gate)).astype(dg_ref.dtype)
        last_row = lax.broadcasted_iota(jnp.int32, a.shape, 0) == ts - 1
        a_buf[...], dh_buf[...] = _compose_groups(jnp.where(last_row, 1.0, pltpu.roll(a, ts - 1, 0)),
                                                  dyv * _gelu(gate), True)
        _chain_groups(a_buf, dh_buf, jnp.broadcast_to(carry[...], (SUBLANES, W)), True)
        dh = dh_buf[...]
        carry[...] = a[0:1] * dh[0:1]
        hh = jnp.where(tile > 0, hh_ref[...], 0.0)
        h_prev = pltpu.roll(jnp.concatenate([hh, h], axis=0), 1, 0)[CONV_HALO:]
        da = dh * h_prev
        bx = ig * xc
        dmult = dh * bx
        dbx = dh * mult
        di = dbx * xc
        dlog_a = jnp.where(reset, 0.0, da * a - dmult * a * a / jnp.maximum(mult, 1e-30))
        dr = dlog_a * (-LRU_C) * sp
        dpre_r = dr * r * (1.0 - r)
        dpre_i = di * ig * (1.0 - ig)
        dprb, dpib = dpre_r.astype(BF16), dpre_i.astype(BF16)
        dpr_ref[...] = dprb
        dpi_ref[...] = dpib
        back = []
        for hd in range(LRU_HEADS):
            lo, hi = hd * LRU_HEAD_DIM, (hd + 1) * LRU_HEAD_DIM
            back.append(_dot(dprb[:, lo:hi], wr_ref[hd], NT) + _dot(dpib[:, lo:hi], wi_ref[hd], NT))
        dxc_ref[...] = dbx * ig + jnp.concatenate(back, axis=1)
        dlam = jnp.sum(dlog_a * (-LRU_C) * r, axis=0, keepdims=True) * (-_sigmoid(-lam_v))
        acc_ref[0:1, :] += jnp.sum(dpre_r, axis=0, keepdims=True)
        acc_ref[1:2, :] += jnp.sum(dpre_i, axis=0, keepdims=True)
        acc_ref[2:3, :] += dlam

    rev = lambda cb: pl.BlockSpec((ts, W), lambda i: (nt - 1 - i, cb))
    vec = _const((1, W))
    gw = _const((LRU_HEADS, LRU_HEAD_DIM, LRU_HEAD_DIM))
    return pl.pallas_call(
        body, grid=(nt,),
        in_specs=[rev(0), _const((W, W)), rev(0), rev(0), rev(0),
                  pl.BlockSpec((CONV_HALO, W), lambda i: (jnp.maximum((nt - 1 - i) * nh - 1, 0), 0)),
                  pl.BlockSpec((ts, 1), lambda i: (nt - 1 - i, 0)), gw, vec, gw, vec, vec],
        out_specs=[rev(0), rev(0), rev(0), rev(0), _const((8, W))],
        out_shape=[jax.ShapeDtypeStruct((S, W), BF16), jax.ShapeDtypeStruct((S, W), F32),
                   jax.ShapeDtypeStruct((S, W), BF16), jax.ShapeDtypeStruct((S, W), BF16),
                   jax.ShapeDtypeStruct((8, W), F32)],
        scratch_shapes=[pltpu.VMEM((ts, W), F32), pltpu.VMEM((ts, W), F32), pltpu.VMEM((1, W), F32)],
        compiler_params=_cp(1), name=name,
    )(dxo, w_out, z, xc, hseq, hseq, reset, w_r, b_r, w_i, b_i, lam)


def _conv_bwd(dxc, z, conv_w, *, name):
    S = dxc.shape[0]
    ts = min(S, 512)
    nh = ts // CONV_HALO
    last = S // CONV_HALO - 1
    W = D_MODEL
    n = ts + CONV_HALO

    def body(d_ref, dn_ref, xb_ref, xp_ref, cw_ref, dxb_ref, acc_ref):
        i = pl.program_id(0)

        @pl.when(i == 0)
        def _():
            acc_ref[...] = jnp.zeros_like(acc_ref)

        d = d_ref[...]
        de = jnp.concatenate([d, jnp.where(i < pl.num_programs(0) - 1, dn_ref[...], 0.0)], axis=0)
        xe = jnp.concatenate([jnp.where(i > 0, xp_ref[...], 0.0), xb_ref[...]], axis=0)
        dxb = cw_ref[3:4, :] * d
        acc_ref[3:4, :] += jnp.sum(d * xe[CONV_HALO:], axis=0, keepdims=True)
        for kk in range(CONV_WIDTH - 1):
            sh = CONV_WIDTH - 1 - kk
            dxb = dxb + cw_ref[kk:kk + 1, :] * pltpu.roll(de, n - sh, 0)[:ts]
            acc_ref[kk:kk + 1, :] += jnp.sum(d * pltpu.roll(xe, sh, 0)[CONV_HALO:], axis=0, keepdims=True)
        dxb_ref[...] = dxb.astype(dxb_ref.dtype)
        acc_ref[4:5, :] += jnp.sum(d, axis=0, keepdims=True)

    return pl.pallas_call(
        body, grid=(S // ts,),
        in_specs=[_rows(ts, W), pl.BlockSpec((CONV_HALO, W), lambda i: (jnp.minimum((i + 1) * nh, last), 0)),
                  _rows(ts, W, 1), pl.BlockSpec((CONV_HALO, W), lambda i: (jnp.maximum(i * nh - 1, 0), 1)),
                  _const((CONV_WIDTH, W))],
        out_specs=[_rows(ts, W), _const((8, W))],
        out_shape=[jax.ShapeDtypeStruct((S, W), BF16), jax.ShapeDtypeStruct((8, W), F32)],
        compiler_params=_cp(1), name=name,
    )(dxc, dxc, z, z, conv_w)


def _loss_head(x, g, target, *, name):
    S, D = x.shape
    ts = _row_tile(S)

    def body(x_ref, g_ref, t_ref, dx_ref, dg_ref, l_ref):
        @pl.when(pl.program_id(0) == 0)
        def _():
            dg_ref[...] = jnp.zeros_like(dg_ref)
            l_ref[...] = jnp.zeros_like(l_ref)

        xv = x_ref[...]
        r = lax.rsqrt(jnp.mean(xv * xv, axis=-1, keepdims=True) + RMS_EPS)
        n = xv * r
        err = n * g_ref[...] - t_ref[...]
        l_ref[...] += 0.5 * jnp.sum(jnp.sum(err * err, axis=-1, keepdims=True) * (1.0 / D), axis=0, keepdims=True)
        dy = err * (1.0 / D)
        dn = dy * g_ref[...]
        dx_ref[...] = r * (dn - n * jnp.mean(dn * n, axis=-1, keepdims=True))
        dg_ref[...] += jnp.sum(dy * n, axis=0, keepdims=True)

    return pl.pallas_call(
        body, grid=(S // ts,), in_specs=[_rows(ts, D), _const((1, D)), _rows(ts, D)],
        out_specs=[_rows(ts, D), _const((1, D)), _const((8, LANES))],
        out_shape=[jax.ShapeDtypeStruct((S, D), F32), jax.ShapeDtypeStruct((1, D), F32),
                   jax.ShapeDtypeStruct((8, LANES), F32)],
        compiler_params=_cp(1), name=name,
    )(x, g.reshape(1, D), target)


def _adamw(w, ga, gb, m, v, *, name):
    shape = w.shape
    cols = shape[-1]
    rows = w.size // cols
    br = rows
    if rows * cols * 4 > (1 << 20):
        br = max(d for d in range(8, rows + 1, 8) if rows % d == 0 and d * cols * 4 <= (1 << 20))

    def body(w_ref, ga_ref, gb_ref, m_ref, v_ref, g_ref, d_ref, mo_ref, vo_ref):
        gv = ga_ref[...] + gb_ref[...]
        g_ref[...] = gv
        mn = ADAM_B1 * m_ref[...] + (1.0 - ADAM_B1) * gv
        vn = ADAM_B2 * v_ref[...] + (1.0 - ADAM_B2) * (gv * gv)
        m_hat = mn / (1.0 - ADAM_B1 ** ADAM_STEP)
        v_hat = vn / (1.0 - ADAM_B2 ** ADAM_STEP)
        d_ref[...] = -ADAM_LR * (m_hat / (jnp.sqrt(v_hat) + ADAM_EPS) + ADAM_WD * w_ref[...])
        mo_ref[...] = mn
        vo_ref[...] = vn

    spec = _rows(br, cols)
    outs = pl.pallas_call(
        body, grid=(rows // br,), in_specs=[spec] * 5, out_specs=[spec] * 4,
        out_shape=[jax.ShapeDtypeStruct((rows, cols), F32)] * 4, compiler_params=_cp(1), name=name,
    )(*[t.reshape(rows, cols) for t in (w, ga, gb, m, v)])
    return [o.reshape(shape) for o in outs]


def _pad_heads(w, width):
    k = w.shape[0]
    return jnp.pad(w.reshape(k, MLA_HEADS, width), ((0, 0), (0, 0), (0, HEAD_PAD - width))).reshape(k, -1)


def _unpad_heads(w, width):
    k = w.shape[0]
    return w.reshape(k, MLA_HEADS, HEAD_PAD)[:, :, :width].reshape(k, MLA_HEADS * width)


def _rope_tables(positions, token=None):
    inv_freq = ROPE_BASE ** (-jnp.arange(0, QK_ROPE, 2, dtype=F32) / QK_ROPE)
    none = jnp.zeros((QK_NOPE,), F32)
    freq = jnp.concatenate([none, inv_freq, inv_freq, none[:HEAD_PAD - QK_DIM]])
    sign = jnp.concatenate([none, -jnp.ones_like(inv_freq), jnp.ones_like(inv_freq), none[:HEAD_PAD - QK_DIM]])
    pos = positions.astype(F32) if token is None else positions.astype(F32) + token[0, 0]
    ang = pos[:, None] * freq
    return jnp.cos(ang), jnp.sin(ang) * sign


def _memory_block(x, mem, W, layer, tag):
    mn = _rms(mem, W["xa_norm_mem"][layer], name=f"{tag}_xa_norm_mem")
    kvm = _mm(mn, [(W["xa_w_kv"][layer], 0, 0)], _first, [(2 * D_MODEL, BF16, 0)], tn=2 * D_MODEL, nj=1,
              name=f"{tag}_xa_kv")[0]
    xo, hx, qx, o = _xa_block_fwd(x, kvm, W["xa_w_q"][layer], W["xa_w_o"][layer], W["xa_norm_x"][layer],
                                  name=f"{tag}_xa_fwd")
    return xo, (x, hx, qx, mn, kvm, o)


def _memory_block_bwd(dxo, mem, W, layer, saved, tag, grads):
    x, hx, qx, mn, kvm, o = saved
    wq, wkv, wo = W["xa_w_q"][layer], W["xa_w_kv"][layer], W["xa_w_o"][layer]
    grads["xa_w_o"][layer] = _owner_major(_mm_tn(o, dxo, name=f"{tag}_xa_dwo"), 0)
    dx, dqx, dkvm, dg = _xa_block_bwd(dxo, x, qx, kvm, wq, wo, W["xa_norm_x"][layer], name=f"{tag}_xa_bwd")
    grads["xa_w_q"][layer] = _owner_major(_mm_tn(hx, dqx, name=f"{tag}_xa_dwq"), 0)
    grads["xa_norm_x"][layer] = dg[0]
    dmn = _mm(dkvm, [(wkv, 0, 0)], _first, [(D_MODEL, F32, 0)], nt=True, tn=D_MODEL, nj=1, name=f"{tag}_xa_dmn")[0]
    grads["xa_w_kv"][layer] = _mm_tn_owners(mn, [dkvm], name=f"{tag}_xa_dwkv")
    grads["xa_norm_mem"][layer] = _norm_gain_grad(mem, dmn, name=f"{tag}_xa_norm_mem_bwd")[0]
    return dx


FF_TN = D_FF // 2

def _silu_mul(accs, extras):
    g, u = accs
    return [g * _sigmoid(g) * u, g, u]


def _silu_mul_bwd(accs, extras):
    da = accs[0]
    g, u = extras[0].astype(F32), extras[1].astype(F32)
    sg = _sigmoid(g)
    silu = g * sg
    return [da * u * (sg + silu * (1.0 - sg)), da * silu]


def _ffn_block(x, W, layer, tag):
    hf = _rms(x, W["ffn_norm"][layer], name=f"{tag}_ffn_norm")
    wgu, wd = W["ffn_w_gate_up"][layer], W["ffn_w_down"][layer]
    act, g, u = _mm(hf, [(wgu, 0, 0), (wgu, 0, 2)], _silu_mul, [(D_FF, BF16, 0)] * 3, tn=FF_TN, nj=2,
                    name=f"{tag}_ffn_up")
    xo = _mm(act, [(wd, 0, 0)], _add_res, [(D_MODEL, F32, 0)], extras=[(x, 0)], tn=D_MODEL, nj=1,
             name=f"{tag}_ffn_down")[0]
    return xo, (x, hf, act, g, u)


def _ffn_block_bwd(dxo, W, layer, saved, tag, grads):
    x, hf, act, g, u = saved
    wgu, wd = W["ffn_w_gate_up"][layer], W["ffn_w_down"][layer]
    dg, du = _mm(dxo, [(wd, 0, 0)], _silu_mul_bwd, [(D_FF, BF16, 0)] * 2, nt=True, extras=[(g, 0), (u, 0)], tn=FF_TN,
                 nj=2, name=f"{tag}_ffn_dact")
    grads["ffn_w_down"][layer] = _owner_major(_mm_tn(act, dxo, tk=FF_TN, name=f"{tag}_ffn_dwd"), 0)
    dx, dgn = _mm(dg, [(wgu, 0, 0)], _norm_bwd_epilogue(0), [(D_MODEL, F32, 0)], nt=True, also=(du, (wgu, 0, 1)),
                  extras=[(x, 0), (dxo, 0)], rows=[W["ffn_norm"][layer].reshape(1, D_MODEL)],
                  sums=[D_MODEL], tn=D_MODEL, nj=1, name=f"{tag}_ffn_dhf")
    grads["ffn_w_gate_up"][layer] = _mm_tn_owners(hf, [dg, du], name=f"{tag}_ffn_dwgu")
    grads["ffn_norm"][layer] = dgn[0]
    return dx


def _even_block(x, tabs, W, tag):
    ctab, stab = tabs
    w_in = W["ev_w_in"][0]
    zero = jnp.zeros((D_MODEL, QK_NOPE), BF16)
    w_in_pad = jnp.concatenate([w_in[:, :896], zero, w_in[:, 896:], zero[:, :HEAD_PAD - QK_DIM]], axis=1)
    w_q_pad = _pad_heads(W["ev_w_q_up"][0], QK_DIM)
    wkv = W["ev_w_kv_up"][0].reshape(KV_RANK, MLA_HEADS, QK_NOPE + V_HEAD)
    w_kv_pad = jnp.concatenate([_pad_heads(wkv[:, :, :QK_NOPE].reshape(KV_RANK, -1), QK_NOPE),
                                _pad_heads(wkv[:, :, QK_NOPE:].reshape(KV_RANK, -1), V_HEAD)], axis=1)
    w_out = W["ev_w_out"][0]
    w_att = jnp.pad(w_out[POOL_DIM:].reshape(MLA_HEADS, V_HEAD, D_MODEL), ((0, 0), (0, HEAD_PAD - V_HEAD), (0, 0)))
    w_out_pad = jnp.concatenate([w_out[:POOL_DIM], w_att.reshape(MLA_HEADS * HEAD_PAD, D_MODEL)], axis=0)
    pool_w = W["ev_pool_w"][0].astype(BF16)
    pool_scale = W["ev_pool_scale"]

    h, z, mix, pooled, cqn, ckvn, q_rot, k_cat, v_pad = _even_front(
        x, W["ev_norm"][0], w_in_pad, pool_w, pool_scale, W["ev_q_norm"][0], w_q_pad, W["ev_kv_norm"][0], w_kv_pad,
        ctab, stab, name=f"{tag}_front")
    mix, lse = _flash_fwd(q_rot, k_cat, v_pad, mix, name=f"{tag}_attn")
    xo = _mm(mix, [(w_out_pad, 0, 0)], _add_res, [(D_MODEL, F32, 0)], extras=[(x, 0)], tn=D_MODEL, nj=1,
             name=f"{tag}_out")[0]
    saved = (x, h, z, pooled, cqn, ckvn, q_rot, k_cat, v_pad, lse, mix,
             (w_in_pad, w_q_pad, w_kv_pad, w_out_pad, pool_w, pool_scale))
    return xo, saved


def _even_out_grad(dxo, saved, tag):
    mix = saved[10]
    dw_out_pad = _mm_tn(mix, dxo, tk=MIX_DIM // 3, name=f"{tag}_dw_out")
    datt = dw_out_pad[POOL_DIM:].reshape(MLA_HEADS, HEAD_PAD, D_MODEL)[:, :V_HEAD].reshape(-1, D_MODEL)
    return [_owner_major(jnp.concatenate([dw_out_pad[:POOL_DIM], datt], axis=0), 0)]


def _even_block_bwd(dxo, tabs, W, saved, tag, grads, token=None):
    ctab, stab = tabs
    x, h, z, pooled, cqn, ckvn, q_rot, k_cat, v_pad, lse, mix, wts = saved
    w_in_pad, w_q_pad, w_kv_pad, w_out_pad, pool_w, pool_scale = wts
    if token is not None:
        w_out_pad = w_out_pad + token[0:1, 0:1].astype(BF16)
    dmix = _mm(dxo, [(w_out_pad, 0, 0)], _first, [(MIX_DIM, BF16, 0)], nt=True, tn=MIX_DIM, nj=1,
               name=f"{tag}_dmix")[0]
    delta = _attn_delta(dmix, mix, name=f"{tag}_delta")
    dq_rot, dk_cat, dv_pad = _flash_bwd(q_rot, k_cat, v_pad, dmix, _retile_rows(lse, delta.shape[2]), delta,
                                        name=f"{tag}_attn_bwd")
    dx, dq_pad, dz, dypre, dgq, dgkv, dscale, dgn = _even_back(
        dq_rot, dk_cat, dv_pad, dmix, pooled, z, x, dxo, ctab, stab, w_q_pad, w_kv_pad, w_in_pad, pool_w, pool_scale,
        W["ev_q_norm"][0], W["ev_kv_norm"][0], W["ev_norm"][0], name=f"{tag}_back")
    grads["ev_q_norm"], grads["ev_kv_norm"], grads["ev_pool_scale"], grads["ev_norm"] = dgq, dgkv, dscale, dgn
    dw_q_pad = _mm_tn(cqn, dq_pad, name=f"{tag}_dw_q_up")
    grads["ev_w_q_up"] = [_owner_major(_unpad_heads(dw_q_pad, QK_DIM), 1)]
    dwk = _unpad_heads(_mm_tn(ckvn, dk_cat, name=f"{tag}_dw_k_up"), QK_NOPE).reshape(KV_RANK, MLA_HEADS, QK_NOPE)
    dwv = _unpad_heads(_mm_tn(ckvn, dv_pad, name=f"{tag}_dw_v_up"), V_HEAD).reshape(KV_RANK, MLA_HEADS, V_HEAD)
    grads["ev_w_kv_up"] = [_owner_major(jnp.concatenate([dwk, dwv], axis=2).reshape(KV_RANK, -1), 1)]
    grads["ev_pool_w"] = _mm_tn_grouped(pooled, dypre, 4, POOL_GROUP, name=f"{tag}_dpool_w")[None]
    dw_in_pad = _mm_tn(h, dz, name=f"{tag}_dw_in")
    grads["ev_w_in"] = [_owner_major(jnp.concatenate([dw_in_pad[:, :896], dw_in_pad[:, 960:992]], axis=1), 0)]
    return dx


def _odd_block(x, reset, W, tag):
    w_r, w_i = W["od_w_rgate"][0], W["od_w_igate"][0]
    vecs = [W[n].reshape(1, D_MODEL) for n in ("od_conv_b", "od_b_rgate", "od_b_igate", "od_lambda")]
    h, z, xc, hseq, y = _lru_fwd(x, W["od_norm"][0], W["od_w_in"][0], reset, W["od_conv_w"][0], vecs[0], w_r,
                                 vecs[1], w_i, vecs[2], vecs[3], name=f"{tag}_lru")
    xo = _mm(y, [(W["od_w_out"][0], 0, 0)], _add_res, [(D_MODEL, F32, 0)], extras=[(x, 0)], tn=D_MODEL, nj=1,
             name=f"{tag}_out")[0]
    return xo, (x, h, z, xc, hseq, y, vecs)


def _odd_block_bwd(dxo, reset, W, saved, tag, grads):
    x, h, z, xc, hseq, y, vecs = saved
    w_r, w_i = W["od_w_rgate"][0], W["od_w_igate"][0]
    grads["od_w_out"] = [_owner_major(_mm_tn(y, dxo, name=f"{tag}_dw_out"), 0)]
    dgate, dxc, dpr, dpi, acc = _lru_bwd(dxo, W["od_w_out"][0], z, xc, hseq, reset, w_r, vecs[1], w_i, vecs[2],
                                         vecs[3], name=f"{tag}_lru_bwd")
    grads["od_b_rgate"], grads["od_b_igate"], grads["od_lambda"] = acc[0:1], acc[1:2], acc[2:3]
    grads["od_w_rgate"] = [_owner_major(_mm_tn_grouped(xc, dpr, LRU_HEADS, LRU_HEAD_DIM, name=f"{tag}_dw_rgate"), 1)]
    grads["od_w_igate"] = [_owner_major(_mm_tn_grouped(xc, dpi, LRU_HEADS, LRU_HEAD_DIM, name=f"{tag}_dw_igate"), 1)]
    dxb, cacc = _conv_bwd(dxc, z, W["od_conv_w"][0], name=f"{tag}_conv_bwd")
    grads["od_conv_w"], grads["od_conv_b"] = cacc[None, 0:4], cacc[4:5]
    dz = jnp.concatenate([dgate, dxb], axis=1)
    grads["od_w_in"] = [_mm_tn_owners(h, [dz], name=f"{tag}_dw_in")]
    dx, dgn = _mm(dz, [(W["od_w_in"][0], 0, 0)], _norm_bwd_epilogue(0), [(D_MODEL, F32, 0)], nt=True,
                  extras=[(x, 0), (dxo, 0)], rows=[W["od_norm"][0].reshape(1, D_MODEL)], sums=[D_MODEL], tn=D_MODEL,
                  nj=1, name=f"{tag}_dh")
    grads["od_norm"] = dgn
    return dx


def _local_step(x, mem, positions, target, W, later_weights=None, exchange_earlier=None, tabs=None):
    tabs = _rope_tables(positions) if tabs is None else tabs
    reset = (positions == 0).astype(F32)[:, None]
    grads = {n: [None, None] for n in ("xa_norm_x", "xa_norm_mem", "xa_w_q", "xa_w_kv", "xa_w_o", "ffn_norm",
                                       "ffn_w_gate_up", "ffn_w_down")}
    x1, s_even = _even_block(x, tabs, W, "l0_even")
    if later_weights is not None:
        W = {**W, **later_weights(x1)}
    x2, s_xa0 = _memory_block(x1, mem, W, 0, "l0")
    x3, s_ff0 = _ffn_block(x2, W, 0, "l0")
    x4, s_odd = _odd_block(x3, reset, W, "l1_odd")
    x5, s_xa1 = _memory_block(x4, mem, W, 1, "l1")
    x6, s_ff1 = _ffn_block(x5, W, 1, "l1")
    d, dgf, loss = _loss_head(x6, W["final_norm"], target, name="loss_head")
    grads["final_norm"] = dgf[0]
    d = _ffn_block_bwd(d, W, 1, s_ff1, "l1", grads)
    d = _memory_block_bwd(d, mem, W, 1, s_xa1, "l1", grads)
    d = _odd_block_bwd(d, reset, W, s_odd, "l1_odd", grads)
    d = _ffn_block_bwd(d, W, 0, s_ff0, "l0", grads)
    d = _memory_block_bwd(d, mem, W, 0, s_xa0, "l0", grads)
    grads["ev_w_out"] = _even_out_grad(d, s_even, "l0_even")
    token = exchange_earlier(grads) if exchange_earlier is not None else None
    d = _even_block_bwd(d, tabs, W, s_even, "l0_even", grads, token)
    big = {n: grads.pop(n) for n in MATMUL_WEIGHTS}
    for n, v in grads.items():
        if isinstance(v, list):
            grads[n] = jnp.stack(v)
    return loss[0, 0], d, big, grads


WEIGHTS = ("ev_norm", "ev_w_in", "ev_pool_w", "ev_pool_scale", "ev_q_norm", "ev_w_q_up", "ev_kv_norm", "ev_w_kv_up",
           "ev_w_out", "od_norm", "od_w_in", "od_conv_w", "od_conv_b", "od_w_rgate", "od_b_rgate", "od_w_igate",
           "od_b_igate", "od_lambda", "od_w_out", "xa_norm_x", "xa_norm_mem", "xa_w_q", "xa_w_kv", "xa_w_o",
           "ffn_norm", "ffn_w_gate_up", "ffn_w_down", "final_norm")
SHARD_AXIS = {"ev_w_in": 1, "ev_w_q_up": 2, "ev_w_kv_up": 2, "ev_w_out": 1, "od_norm": 1, "od_w_in": 2,
              "od_conv_w": 2, "od_conv_b": 1, "od_w_rgate": 2, "od_b_rgate": 1, "od_w_igate": 2, "od_b_igate": 1,
              "od_lambda": 1, "od_w_out": 1, "xa_w_q": 1, "xa_w_kv": 2, "xa_w_o": 1, "ffn_w_gate_up": 2,
              "ffn_w_down": 1}
MATMUL_WEIGHTS = ("ev_w_in", "ev_w_q_up", "ev_w_kv_up", "ev_w_out", "od_w_in", "od_w_rgate", "od_w_igate",
                  "od_w_out", "xa_w_q", "xa_w_kv", "xa_w_o", "ffn_w_gate_up", "ffn_w_down")
SMALL_SHARDED = tuple(n for n in WEIGHTS if n in SHARD_AXIS and n not in MATMUL_WEIGHTS)
REPLICATED = tuple(n for n in WEIGHTS if n not in SHARD_AXIS)


def _pack(parts, quantum):
    flat = jnp.concatenate([p.reshape(-1) for p in parts])
    pad = (-flat.shape[0]) % quantum
    return jnp.pad(flat, (0, pad)).reshape(-1, LANES)


def _unpack(flat, shapes):
    out, off = [], 0
    for shape in shapes:
        size = math.prod(shape)
        out.append(flat[off:off + size].reshape(shape))
        off += size
    return out


def _run_copies(local, remote, send_sems, recv_sems, local_sems):
    locals_ = [pltpu.make_async_copy(src, dst, local_sems.at[n]) for n, (src, dst) in enumerate(local)]
    for cp in locals_:
        cp.start()
    sends = [pltpu.make_async_remote_copy(src_ref=src, dst_ref=dst, send_sem=send_sems.at[k, n],
                                          recv_sem=recv_sems.at[k, n], device_id=dev, device_id_type=MESH)
             for (k, n, src, dst, _, dev) in remote]
    for cp in sends:
        cp.start()
    for (k, n, src, _, arrival, dev) in remote:
        pltpu.make_async_remote_copy(src_ref=src, dst_ref=arrival, send_sem=send_sems.at[k, n],
                                     recv_sem=recv_sems.at[k, n], device_id=dev, device_id_type=MESH).wait_recv()
    for cp in sends:
        cp.wait_send()
    for cp in locals_:
        cp.wait()


def _chip_peers(x, y):
    return [(1 - x, y), (x, 1 - y), (1 - x, 1 - y)]


def _owner_block(ref, axis, q):
    size = ref.shape[axis] // N_CHIPS
    idx = [slice(None)] * len(ref.shape)
    idx[axis] = pl.ds(q * size, size)
    return ref.at[tuple(idx)]


def _comm_call(body, ins, out_shapes, n_items, n_peers, *, name):
    return pl.pallas_call(
        body, in_specs=[ANY] * len(ins), out_specs=[ANY] * len(out_shapes), out_shape=out_shapes,
        scratch_shapes=[pltpu.SemaphoreType.DMA((n_peers, n_items)), pltpu.SemaphoreType.DMA((n_peers, n_items)),
                        pltpu.SemaphoreType.DMA((n_items,))],
        name=name,
    )(*ins)


HBM = pl.BlockSpec(memory_space=pltpu.HBM)
SEM = pl.BlockSpec(memory_space=pltpu.SEMAPHORE)
DATAFLOW = pltpu.SideEffectType.DATAFLOW_SIDE_EFFECTING


def _gather_plan(axes):
    return lambda srcs, lands, me, peer: [
        (srcs[i], _owner_block(lands[i], ax, me), _owner_block(lands[i], ax, peer)) for i, ax in enumerate(axes)]


def _exchange_plan(where):
    return lambda srcs, lands, me, peer: [
        (srcs[i].at[peer], lands[n].at[me, l], lands[n].at[peer, l]) for i, (n, l) in enumerate(where)]


def _split_peers(sibling):
    x, y, c = lax.axis_index("x"), lax.axis_index("y"), lax.axis_index("c")
    peers = [((px, py, c), 2 * px + py) for px, py in _chip_peers(x, y)]
    return 2 * x + y, peers + ([((x, y, 1 - c), 2 * x + y)] if sibling else [])


def _split_start(srcs, lands, plan, *, sibling=False, name):
    ns, nl = len(srcs), len(lands)
    nsem = (3 + sibling) * len(plan(list(srcs), list(lands), 0, 0))

    def body(*refs):
        src_refs, land_refs = refs[:ns], refs[ns:ns + nl]
        send_sems, recv_sems = refs[ns + nl:ns + nl + nsem], refs[ns + nl + nsem:ns + nl + 2 * nsem]
        me, peers = _split_peers(sibling)
        n = 0
        for device, chip in peers:
            for src, dst, _ in plan(src_refs, land_refs, me, chip):
                pltpu.make_async_remote_copy(src_ref=src, dst_ref=dst, send_sem=send_sems[n], recv_sem=recv_sems[n],
                                             device_id=device, device_id_type=MESH).start()
                n += 1
        refs[-1][...] = jnp.zeros_like(refs[-1])

    arrays = list(srcs) + list(lands)
    out = pl.pallas_call(
        body, name=name, in_specs=[HBM] * (ns + nl),
        out_specs=[SEM] * (2 * nsem) + [HBM] * (ns + nl) + [pl.BlockSpec(memory_space=pltpu.VMEM)],
        out_shape=[pltpu.SemaphoreType.DMA(())] * (2 * nsem) + [pltpu.HBM(a.shape, a.dtype) for a in arrays]
        + [jax.ShapeDtypeStruct((8, LANES), F32)],
        input_output_aliases={i: 2 * nsem + i for i in range(ns + nl)},
        compiler_params=pltpu.CompilerParams(has_side_effects=DATAFLOW),
    )(*[pltpu.with_memory_space_constraint(a, pltpu.HBM) for a in arrays])
    sems, rest = out[:2 * nsem], out[2 * nsem:]
    return sems[:nsem], sems[nsem:], rest[:ns], rest[ns:ns + nl], rest[-1]


def _split_wait(handle, after, plan, *, sibling=False, name):
    send_sems, recv_sems, srcs, lands, _ = handle
    ns, nl, nsem = len(srcs), len(lands), len(send_sems)

    def body(*refs):
        src_refs, land_refs = refs[:ns], refs[ns:ns + nl]
        send_refs, recv_refs = refs[ns + nl:ns + nl + nsem], refs[ns + nl + nsem:ns + nl + 2 * nsem]
        me, peers = _split_peers(sibling)
        n = 0
        for device, chip in peers:
            for src, _, arrival in plan(src_refs, land_refs, me, chip):
                cp = pltpu.make_async_remote_copy(src_ref=src, dst_ref=arrival, send_sem=send_refs[n],
                                                  recv_sem=recv_refs[n], device_id=device, device_id_type=MESH)
                cp.wait_send()
                cp.wait_recv()
                n += 1

    out = pl.pallas_call(
        body, name=name, in_specs=[HBM] * (ns + nl) + [SEM] * (2 * nsem) + [ANY], out_specs=[HBM] * (ns + nl),
        out_shape=[pltpu.HBM(a.shape, a.dtype) for a in list(srcs) + list(lands)],
        input_output_aliases={i: i for i in range(ns + nl)},
        compiler_params=pltpu.CompilerParams(has_side_effects=DATAFLOW),
    )(*srcs, *lands, *send_sems, *recv_sems, after)
    return out[ns:]


def _exchange_sibling(arrays, *, name):
    n = len(arrays)

    def body(*refs):
        x, y, c = lax.axis_index("x"), lax.axis_index("y"), lax.axis_index("c")
        remote = [(0, i, refs[i], refs[n + i], refs[n + i], (x, y, 1 - c)) for i in range(n)]
        _run_copies([], remote, *refs[2 * n:])

    return _comm_call(body, arrays, [jax.ShapeDtypeStruct(a.shape, a.dtype) for a in arrays], n, 1, name=name)


def _sum_slots(r, *, token=None, name):
    shape = r.shape[1:]
    cols = shape[-1]
    rows = math.prod(shape) // cols
    tr = max(d for d in range(8, rows + 1, 8) if rows % d == 0 and d * cols * 16 <= (4 << 20))

    def body(r_ref, *refs):
        total = ((r_ref[0] + r_ref[1]) + r_ref[2]) + r_ref[3]
        refs[-1][...] = total if token is None else total + refs[0][0:1, 0:1]

    in_specs = [pl.BlockSpec((N_CHIPS, tr, cols), lambda i: (0, i, 0))]
    in_specs += [] if token is None else [_const((8, LANES))]
    return pl.pallas_call(
        body, grid=(rows // tr,), in_specs=in_specs,
        out_specs=_rows(tr, cols), out_shape=jax.ShapeDtypeStruct((rows, cols), F32), compiler_params=_cp(1),
        name=name,
    )(r.reshape(N_CHIPS, rows, cols), *([] if token is None else [token])).reshape(shape)


FIRST_WEIGHTS = ("ev_w_in", "ev_w_q_up", "ev_w_kv_up", "ev_w_out")
LATER_WEIGHTS = tuple(n for n in MATMUL_WEIGHTS if n not in FIRST_WEIGHTS)
LAST_GRADS = ("ev_w_in", "ev_w_q_up", "ev_w_kv_up")
EARLIER_GRADS = tuple(n for n in MATMUL_WEIGHTS if n not in LAST_GRADS)


def _my_chip():
    return 2 * lax.axis_index("x") + lax.axis_index("y")


def _gather_first(w, work):
    small = _pack([w[n] for n in SMALL_SHARDED], 8 * LANES)
    stacked = [n for n in FIRST_WEIGHTS if SHARD_AXIS[n] == w[n].ndim - 1 and w[n].shape[-1] % LANES]
    shards = [w[n].astype(BF16)[None] if n in stacked else w[n].astype(BF16) for n in FIRST_WEIGHTS] + [small]
    axes = [0 if n in stacked else SHARD_AXIS[n] for n in FIRST_WEIGHTS] + [0]
    plan = _gather_plan(axes)
    lands = [lax.empty(tuple(d * (N_CHIPS if a == ax else 1) for a, d in enumerate(s.shape)), s.dtype)
             for s, ax in zip(shards, axes)]
    handle = _split_start(shards, lands, plan, sibling=True, name="gather_first_start")
    done = work(handle[4])
    got = _split_wait(handle, done[0], plan, sibling=True, name="gather_first_wait")
    full = {n: w[n] for n in REPLICATED}
    for n, g in zip(FIRST_WEIGHTS, got[:-1]):
        full[n] = jnp.concatenate([g[q] for q in range(N_CHIPS)], axis=SHARD_AXIS[n]) if n in stacked else g
    per_chip = [_unpack(got[-1][q * small.shape[0]:(q + 1) * small.shape[0]].reshape(-1),
                        [w[n].shape for n in SMALL_SHARDED]) for q in range(N_CHIPS)]
    for i, n in enumerate(SMALL_SHARDED):
        full[n] = jnp.concatenate([per_chip[q][i] for q in range(N_CHIPS)], axis=SHARD_AXIS[n])
    return full, done


def _gather_later_start(w, after):
    behind = (after.reshape(-1)[0] * 0).astype(BF16)
    shards = [w[n].astype(BF16) + (behind if n == "od_w_rgate" else 0) for n in LATER_WEIGHTS]
    axes = [SHARD_AXIS[n] for n in LATER_WEIGHTS]
    lands = [lax.empty(tuple(d * (N_CHIPS if a == ax else 1) for a, d in enumerate(s.shape)), s.dtype)
             for s, ax in zip(shards, axes)]
    plan = _gather_plan(axes)
    return _split_start(shards, lands, plan, sibling=True, name="gather_later_start"), plan


def _owner_major(g, axis):
    shape = g.shape
    size = shape[axis] // N_CHIPS
    g = jnp.moveaxis(g.reshape(shape[:axis] + (N_CHIPS, size) + shape[axis + 1:]), axis, 0)
    return g.reshape(N_CHIPS, -1, shape[-1] if axis < len(shape) - 1 else size)


def _exchange_start(items, *, cross, name):
    me = _my_chip()
    srcs, lands, where = [], [], []
    for n, layers in enumerate(items):
        land = lax.empty((N_CHIPS, len(layers)) + layers[0].shape[1:], layers[0].dtype)
        for l, a in enumerate(layers):
            if not cross:
                own = lax.dynamic_index_in_dim(a, me, 0, keepdims=True)[:, None]
                land = lax.dynamic_update_slice(land, own, (me, l) + (0,) * (a.ndim - 1))
            srcs.append(a)
            where.append((n, l))
        lands.append(land)
    plan = _exchange_plan(where)
    return _split_start(srcs, lands, plan, sibling=cross, name=name), plan


def _earlier_items(grads, full_shapes):
    small = [_pack([jnp.split(grads[n].reshape(full_shapes[n]), N_CHIPS, axis=SHARD_AXIS[n])[q]
                    for n in SMALL_SHARDED], 8 * LANES) for q in range(N_CHIPS)]
    return [grads[n] for n in EARLIER_GRADS] + [[jnp.stack(small)]]


def _last_items(big, grads, full_shapes, loss):
    repl = _pack([grads[n].reshape(full_shapes[n]) for n in REPLICATED] + [loss.reshape(1)], 8 * LANES)
    return [big[n] for n in LAST_GRADS] + [[jnp.stack([repl] * N_CHIPS)]]


def kernel(
        x, mem, positions, ev_norm, ev_w_in, ev_pool_w, ev_pool_scale, ev_q_norm, ev_w_q_up, ev_kv_norm,
        ev_w_kv_up, ev_w_out, od_norm, od_w_in, od_conv_w, od_conv_b, od_w_rgate, od_b_rgate, od_w_igate,
        od_b_igate, od_lambda, od_w_out, xa_norm_x, xa_norm_mem, xa_w_q, xa_w_kv, xa_w_o, ffn_norm,
        ffn_w_gate_up, ffn_w_down, final_norm, loss_target, m_ev_norm, m_ev_w_in, m_ev_pool_w, m_ev_pool_scale,
        m_ev_q_norm, m_ev_w_q_up, m_ev_kv_norm, m_ev_w_kv_up, m_ev_w_out, m_od_norm, m_od_w_in, m_od_conv_w,
        m_od_conv_b, m_od_w_rgate, m_od_b_rgate, m_od_w_igate, m_od_b_igate, m_od_lambda, m_od_w_out,
        m_xa_norm_x, m_xa_norm_mem, m_xa_w_q, m_xa_w_kv, m_xa_w_o, m_ffn_norm, m_ffn_w_gate_up, m_ffn_w_down,
        m_final_norm, v_ev_norm, v_ev_w_in, v_ev_pool_w, v_ev_pool_scale, v_ev_q_norm, v_ev_w_q_up,
        v_ev_kv_norm, v_ev_w_kv_up, v_ev_w_out, v_od_norm, v_od_w_in, v_od_conv_w, v_od_conv_b, v_od_w_rgate,
        v_od_b_rgate, v_od_w_igate, v_od_b_igate, v_od_lambda, v_od_w_out, v_xa_norm_x, v_xa_norm_mem, v_xa_w_q,
        v_xa_w_kv, v_xa_w_o, v_ffn_norm, v_ffn_w_gate_up, v_ffn_w_down, v_final_norm):
    given = dict(locals())
    w = {n: given[n] for n in WEIGHTS}
    full_shapes = {n: tuple(d * (N_CHIPS if a == SHARD_AXIS.get(n) else 1) for a, d in enumerate(w[n].shape))
                   for n in WEIGHTS}
    full, tabs = _gather_first(w, lambda token: _rope_tables(positions[0], token))
    later, later_plan = _gather_later_start(w, full["ev_w_out"])
    full["ev_norm"] = full["ev_norm"] + later[4][0:1, 0:1]
    exchange = {}

    def later_weights(after):
        return dict(zip(LATER_WEIGHTS, _split_wait(later, after, later_plan, sibling=True, name="gather_later_wait")))

    def exchange_earlier(grads):
        exchange["handle"], exchange["plan"] = _exchange_start(_earlier_items(grads, full_shapes), cross=True,
                                                               name="exchange_earlier_start")
        return exchange["handle"][4]

    loss, grad_x, big, grads = _local_step(x[0], mem[0], positions[0], loss_target[0], full, later_weights,
                                           exchange_earlier, tabs)
    earlier = EARLIER_GRADS + ("small",)
    got = dict(zip(earlier, _split_wait(exchange["handle"], grad_x, exchange["plan"], sibling=True,
                                        name="exchange_earlier_wait")))
    last, last_plan = _exchange_start(_last_items(big, grads, full_shapes, loss), cross=False,
                                      name="exchange_last_start")
    out = {}

    def finish(names, landed, token, tag):
        mine = [_sum_slots(landed[n], token=token if i == 0 else None, name=f"sum_chips_{n}")
                for i, n in enumerate(names)]
        other = _exchange_sibling(mine, name=f"exchange_sibling_{tag}")
        total = None
        for n, a, b in zip(names, mine, other):
            if n in MATMUL_WEIGHTS:
                out[n] = _adamw(w[n], a.reshape(w[n].shape), b.reshape(w[n].shape), given["m_" + n], given["v_" + n],
                                name=f"adamw_{n}")
                continue
            group = SMALL_SHARDED if n == "small" else REPLICATED
            spare = [jnp.zeros((1,), F32)] if group is REPLICATED else []
            packed = [_pack([given[pre + k] for k in group] + spare, 8 * LANES) for pre in ("", "m_", "v_")]
            res = _adamw(packed[0], a.reshape(packed[0].shape), b.reshape(packed[0].shape), packed[1], packed[2],
                         name=f"adamw_{n}")
            shapes = [w[k].shape for k in group] + [(1,)] * len(spare)
            for j, arrs in enumerate(zip(*[_unpack(r.reshape(-1), shapes) for r in res])):
                if j < len(group):
                    out[group[j]] = list(arrs)
                else:
                    total = arrs[0][0]
        return total

    finish(earlier, got, last[4], "earlier")
    names = LAST_GRADS + ("replicated",)
    got = dict(zip(names, _split_wait(last, out[EARLIER_GRADS[-1]][1], last_plan, name="exchange_last_wait")))
    loss = finish(names, got, None, "last")
    return (loss, grad_x[None], *[out[n][k] for k in range(4) for n in WEIGHTS])
```
